```python
import math
import jax, jax.numpy as jnp
from jax import lax
import numpy as np

D_MODEL = 1024
BATCH = 8
SEQ = 2048
DEPTH = 2

PLE_DIM = 256
HEAD_DIM = 64
N_HEADS_A = 8
N_HEADS_B = 8
N_KV_B = 2
DILATED_PATTERNS = ((128, 1), (512, 4), (2048, 16))
WINDOW_B = 128
NUM_BUCKETS = 32
MAX_DISTANCE = 1024
D_FF = ((8 * D_MODEL + 3 * 256 - 1) // (3 * 256)) * 256
RMS_EPS = 1e-6
NEG_INF = -1e30
WIDTH_A = N_HEADS_A * HEAD_DIM
WIDTH_BQ = N_HEADS_B * HEAD_DIM
WIDTH_BKV = N_KV_B * HEAD_DIM
IN_SPLITS = (WIDTH_A, WIDTH_A, WIDTH_A, WIDTH_BQ, WIDTH_BKV, WIDTH_BKV, D_MODEL, D_MODEL)
D_IN = sum(IN_SPLITS)

kernel_name = "hybrid_dilated_window_gqa_encoder"


def rmsnorm(x, g):
    xf = x.astype(jnp.float32)
    y = xf * lax.rsqrt(jnp.mean(xf * xf, axis=-1, keepdims=True) + RMS_EPS)
    return (y * g.astype(jnp.float32)).astype(x.dtype)


def t5_bucket(rel):
    half_b = NUM_BUCKETS // 2
    max_exact = half_b // 2
    sign = jnp.where(rel > 0, half_b, 0)
    n = jnp.abs(rel)
    nf = jnp.maximum(n, 1).astype(jnp.float32)
    large = max_exact + (jnp.log(nf / max_exact) / math.log(MAX_DISTANCE / max_exact)
                         * (half_b - max_exact)).astype(jnp.int32)
    large = jnp.minimum(large, half_b - 1)
    return sign + jnp.where(n < max_exact, n, large)


def band_rel(blk):
    i = jnp.arange(blk, dtype=jnp.int32)[:, None]
    j = jnp.arange(3 * blk, dtype=jnp.int32)[None, :]
    return j - blk - i


def rel_bias(table, blk, dilation):
    buckets = t5_bucket(band_rel(blk) * dilation)
    return jnp.transpose(table[buckets], (2, 0, 1))


def banded_attention(q, k, v, bias, half, blk, sink=None):
    N, L, Hq, hd = q.shape
    Hk = k.shape[2]
    G = Hq // Hk
    nb = -(-L // blk)
    Lp = nb * blk
    qp = jnp.pad(q, ((0, 0), (0, Lp - L), (0, 0), (0, 0)))
    kp = jnp.pad(k, ((0, 0), (blk, Lp - L + blk), (0, 0), (0, 0))).reshape(N, nb + 2, blk, Hk, hd)
    vp = jnp.pad(v, ((0, 0), (blk, Lp - L + blk), (0, 0), (0, 0))).reshape(N, nb + 2, blk, Hk, hd)
    kw = jnp.concatenate([kp[:, :-2], kp[:, 1:-1], kp[:, 2:]], axis=2)
    vw = jnp.concatenate([vp[:, :-2], vp[:, 1:-1], vp[:, 2:]], axis=2)
    qb = qp.reshape(N, nb, blk, Hk, G, hd)
    s = jnp.einsum('nbqkgd,nbjkd->nbkgqj', qb, kw).astype(jnp.float32) * (hd ** -0.5)
    s = s + bias.reshape(Hk, G, blk, 3 * blk).astype(jnp.float32)[None, None]
    rel = band_rel(blk)
    key_pos = (jnp.arange(nb, dtype=jnp.int32)[:, None, None] * blk
               + (jnp.arange(3 * blk, dtype=jnp.int32) - blk)[None, None, :])
    mask = (jnp.abs(rel) <= half)[None] & (key_pos >= 0) & (key_pos < L)
    s = jnp.where(mask[None, :, None, None], s, NEG_INF)
    m = jnp.max(s, axis=-1)
    if sink is not None:
        sk = sink.astype(jnp.float32).reshape(Hk, G)[None, None, :, :, None]
        m = jnp.maximum(m, sk)
    pr = jnp.exp(s - m[..., None])
    den = jnp.sum(pr, axis=-1)
    if sink is not None:
        den = den + jnp.exp(sk - m)
    pr = (pr / den[..., None]).astype(v.dtype)
    o = jnp.einsum('nbkgqj,nbjkd->nbqkgd', pr, vw).reshape(N, Lp, Hq, hd)[:, :L]
    lse = jnp.transpose(m + jnp.log(den), (0, 1, 4, 2, 3)).reshape(N, Lp, Hq)[:, :L]
    return o, lse


def dilated_mixer(q, k, v, biases):
    B_, S_, H, hd = q.shape
    outs, lses = [], []
    for (w, d), bias in zip(DILATED_PATTERNS, biases):
        half = w // (2 * d)

        def to_sub(t):
            return t.reshape(B_, S_ // d, d, H, hd).transpose(0, 2, 1, 3, 4).reshape(B_ * d, S_ // d, H, hd)

        o, lse = banded_attention(to_sub(q), to_sub(k), to_sub(v), bias, half, half)
        outs.append(o.reshape(B_, d, S_ // d, H, hd).transpose(0, 2, 1, 3, 4).reshape(B_, S_, H, hd))
        lses.append(lse.reshape(B_, d, S_ // d, H).transpose(0, 2, 1, 3).reshape(B_, S_, H))
    wts = jax.nn.softmax(jnp.stack(lses, axis=0), axis=0)
    return jnp.einsum('pbsh,pbshd->bshd', wts.astype(q.dtype), jnp.stack(outs, axis=0))


def _fwd_setup_inputs(seed: int = 0) -> dict:
    key = jax.random.key(seed)
    ks = jax.random.split(key, 24)
    f32 = jnp.float32

    def nrm(k_, shape, scale):
        return jax.random.normal(k_, shape, f32) * scale

    def gain(k_, shape):
        return 1.0 + 0.05 * jax.random.normal(k_, shape, f32)

    return {
        "x": nrm(ks[0], (BATCH, SEQ, D_MODEL), 1.0),
        "p": nrm(ks[1], (DEPTH, BATCH, SEQ, PLE_DIM), 1.0),
        "rel_table": nrm(ks[2], (NUM_BUCKETS, N_HEADS_A + N_HEADS_B), 0.1),
        "norm_mix_g": gain(ks[3], (DEPTH, D_MODEL)),
        "w_in": nrm(ks[4], (DEPTH, D_MODEL, D_IN), D_MODEL ** -0.5),
        "qnorm_a_g": gain(ks[5], (DEPTH, HEAD_DIM)),
        "knorm_a_g": gain(ks[6], (DEPTH, HEAD_DIM)),
        "qnorm_b_g": gain(ks[7], (DEPTH, HEAD_DIM)),
        "knorm_b_g": gain(ks[8], (DEPTH, HEAD_DIM)),
        "sink_b": nrm(ks[9], (DEPTH, N_HEADS_B), 0.5),
        "w_branch_a": nrm(ks[10], (DEPTH, WIDTH_A, D_MODEL), WIDTH_A ** -0.5),
        "w_branch_b": nrm(ks[11], (DEPTH, WIDTH_BQ, D_MODEL), WIDTH_BQ ** -0.5),
        "w_out": nrm(ks[12], (DEPTH, D_MODEL, D_MODEL), D_MODEL ** -0.5),
        "norm_ffn_g": gain(ks[13], (DEPTH, D_MODEL)),
        "w_ffn_gate": nrm(ks[14], (DEPTH, D_MODEL, D_FF), D_MODEL ** -0.5),
        "w_ffn_up": nrm(ks[15], (DEPTH, D_MODEL, D_FF), D_MODEL ** -0.5),
        "w_ffn_down": nrm(ks[16], (DEPTH, D_FF, D_MODEL), D_FF ** -0.5),
        "norm_ple_g": gain(ks[17], (DEPTH, D_MODEL)),
        "w_ple_gate": nrm(ks[18], (DEPTH, D_MODEL, D_MODEL), D_MODEL ** -0.5),
        "w_ple_proj": nrm(ks[19], (DEPTH, PLE_DIM, D_MODEL), PLE_DIM ** -0.5),
    }


def _fwd_reference(x, p, rel_table, norm_mix_g, w_in, qnorm_a_g, knorm_a_g, qnorm_b_g, knorm_b_g,
              sink_b, w_branch_a, w_branch_b, w_out, norm_ffn_g, w_ffn_gate, w_ffn_up, w_ffn_down,
              norm_ple_g, w_ple_gate, w_ple_proj):
    B_, S_, _ = x.shape
    table_a = rel_table[:, :N_HEADS_A]
    table_b = rel_table[:, N_HEADS_A:]
    biases_a = [rel_bias(table_a, w // (2 * d), d) for (w, d) in DILATED_PATTERNS]
    bias_b = rel_bias(table_b, WINDOW_B, 1)
    split_points = [int(v) for v in np.cumsum(IN_SPLITS)[:-1]]

    for l in range(DEPTH):
        h = rmsnorm(x, norm_mix_g[l])
        proj = jnp.einsum('bsd,de->bse', h, w_in[l])
        qa, ka, va, qb, kb, vb, ga, gb = jnp.split(proj, split_points, axis=-1)
        qa = rmsnorm(qa.reshape(B_, S_, N_HEADS_A, HEAD_DIM), qnorm_a_g[l])
        ka = rmsnorm(ka.reshape(B_, S_, N_HEADS_A, HEAD_DIM), knorm_a_g[l])
        va = va.reshape(B_, S_, N_HEADS_A, HEAD_DIM)
        qb = rmsnorm(qb.reshape(B_, S_, N_HEADS_B, HEAD_DIM), qnorm_b_g[l])
        kb = rmsnorm(kb.reshape(B_, S_, N_KV_B, HEAD_DIM), knorm_b_g[l])
        vb = vb.reshape(B_, S_, N_KV_B, HEAD_DIM)

        ya = dilated_mixer(qa, ka, va, biases_a).reshape(B_, S_, WIDTH_A)
        yb, _ = banded_attention(qb, kb, vb, bias_b, WINDOW_B, WINDOW_B, sink=sink_b[l])
        yb = yb.reshape(B_, S_, WIDTH_BQ)

        merged = (jax.nn.sigmoid(ga) * jnp.einsum('bsc,cd->bsd', ya, w_branch_a[l])
                  + jax.nn.sigmoid(gb) * jnp.einsum('bsc,cd->bsd', yb, w_branch_b[l]))
        x = x + jnp.einsum('bsd,de->bse', merged, w_out[l])

        h = rmsnorm(x, norm_ffn_g[l])
        hid = jax.nn.silu(jnp.einsum('bsd,df->bsf', h, w_ffn_gate[l])) * jnp.einsum('bsd,df->bsf', h, w_ffn_up[l])
        x = x + jnp.einsum('bsf,fd->bsd', hid, w_ffn_down[l])

        h = rmsnorm(x, norm_ple_g[l])
        x = x + jax.nn.sigmoid(jnp.einsum('bsd,de->bse', h, w_ple_gate[l])) * jnp.einsum('bsq,qd->bsd', p[l], w_ple_proj[l])
    return x


import jax as _jax
import jax.numpy as _jnp

TWIN_FORMAT = 'train_step'
FWD_PARAMS = ['x', 'p', 'rel_table', 'norm_mix_g', 'w_in', 'qnorm_a_g', 'knorm_a_g', 'qnorm_b_g', 'knorm_b_g', 'sink_b', 'w_branch_a', 'w_branch_b', 'w_out', 'norm_ffn_g', 'w_ffn_gate', 'w_ffn_up', 'w_ffn_down', 'norm_ple_g', 'w_ple_gate', 'w_ple_proj']
TWIN_WEIGHTS = ['rel_table', 'norm_mix_g', 'w_in', 'qnorm_a_g', 'knorm_a_g', 'qnorm_b_g', 'knorm_b_g', 'sink_b', 'w_branch_a', 'w_branch_b', 'w_out', 'norm_ffn_g', 'w_ffn_gate', 'w_ffn_up', 'w_ffn_down', 'norm_ple_g', 'w_ple_gate', 'w_ple_proj']
TWIN_DIFF_INPUT = 'x'
TWIN_INPUTS = ['x', 'p', 'rel_table', 'norm_mix_g', 'w_in', 'qnorm_a_g', 'knorm_a_g', 'qnorm_b_g', 'knorm_b_g', 'sink_b', 'w_branch_a', 'w_branch_b', 'w_out', 'norm_ffn_g', 'w_ffn_gate', 'w_ffn_up', 'w_ffn_down', 'norm_ple_g', 'w_ple_gate', 'w_ple_proj', 'loss_target', 'm_rel_table', 'm_norm_mix_g', 'm_w_in', 'm_qnorm_a_g', 'm_knorm_a_g', 'm_qnorm_b_g', 'm_knorm_b_g', 'm_sink_b', 'm_w_branch_a', 'm_w_branch_b', 'm_w_out', 'm_norm_ffn_g', 'm_w_ffn_gate', 'm_w_ffn_up', 'm_w_ffn_down', 'm_norm_ple_g', 'm_w_ple_gate', 'm_w_ple_proj', 'v_rel_table', 'v_norm_mix_g', 'v_w_in', 'v_qnorm_a_g', 'v_knorm_a_g', 'v_qnorm_b_g', 'v_knorm_b_g', 'v_sink_b', 'v_w_branch_a', 'v_w_branch_b', 'v_w_out', 'v_norm_ffn_g', 'v_w_ffn_gate', 'v_w_ffn_up', 'v_w_ffn_down', 'v_norm_ple_g', 'v_w_ple_gate', 'v_w_ple_proj']
TWIN_OUTPUTS = ['loss', 'grad_x', 'grad_rel_table', 'grad_norm_mix_g', 'grad_w_in', 'grad_qnorm_a_g', 'grad_knorm_a_g', 'grad_qnorm_b_g', 'grad_knorm_b_g', 'grad_sink_b', 'grad_w_branch_a', 'grad_w_branch_b', 'grad_w_out', 'grad_norm_ffn_g', 'grad_w_ffn_gate', 'grad_w_ffn_up', 'grad_w_ffn_down', 'grad_norm_ple_g', 'grad_w_ple_gate', 'grad_w_ple_proj', 'delta_rel_table', 'delta_norm_mix_g', 'delta_w_in', 'delta_qnorm_a_g', 'delta_knorm_a_g', 'delta_qnorm_b_g', 'delta_knorm_b_g', 'delta_sink_b', 'delta_w_branch_a', 'delta_w_branch_b', 'delta_w_out', 'delta_norm_ffn_g', 'delta_w_ffn_gate', 'delta_w_ffn_up', 'delta_w_ffn_down', 'delta_norm_ple_g', 'delta_w_ple_gate', 'delta_w_ple_proj', 'new_m_rel_table', 'new_m_norm_mix_g', 'new_m_w_in', 'new_m_qnorm_a_g', 'new_m_knorm_a_g', 'new_m_qnorm_b_g', 'new_m_knorm_b_g', 'new_m_sink_b', 'new_m_w_branch_a', 'new_m_w_branch_b', 'new_m_w_out', 'new_m_norm_ffn_g', 'new_m_w_ffn_gate', 'new_m_w_ffn_up', 'new_m_w_ffn_down', 'new_m_norm_ple_g', 'new_m_w_ple_gate', 'new_m_w_ple_proj', 'new_v_rel_table', 'new_v_norm_mix_g', 'new_v_w_in', 'new_v_qnorm_a_g', 'new_v_knorm_a_g', 'new_v_qnorm_b_g', 'new_v_knorm_b_g', 'new_v_sink_b', 'new_v_w_branch_a', 'new_v_w_branch_b', 'new_v_w_out', 'new_v_norm_ffn_g', 'new_v_w_ffn_gate', 'new_v_w_ffn_up', 'new_v_w_ffn_down', 'new_v_norm_ple_g', 'new_v_w_ple_gate', 'new_v_w_ple_proj']
TWIN_LEAF_KINDS = {'loss': 'loss', 'grad_x': 'grad_x', 'grad_rel_table': 'grad_w', 'grad_norm_mix_g': 'grad_w', 'grad_w_in': 'grad_w', 'grad_qnorm_a_g': 'grad_w', 'grad_knorm_a_g': 'grad_w', 'grad_qnorm_b_g': 'grad_w', 'grad_knorm_b_g': 'grad_w', 'grad_sink_b': 'grad_w', 'grad_w_branch_a': 'grad_w', 'grad_w_branch_b': 'grad_w', 'grad_w_out': 'grad_w', 'grad_norm_ffn_g': 'grad_w', 'grad_w_ffn_gate': 'grad_w', 'grad_w_ffn_up': 'grad_w', 'grad_w_ffn_down': 'grad_w', 'grad_norm_ple_g': 'grad_w', 'grad_w_ple_gate': 'grad_w', 'grad_w_ple_proj': 'grad_w', 'delta_rel_table': 'delta_w', 'delta_norm_mix_g': 'delta_w', 'delta_w_in': 'delta_w', 'delta_qnorm_a_g': 'delta_w', 'delta_knorm_a_g': 'delta_w', 'delta_qnorm_b_g': 'delta_w', 'delta_knorm_b_g': 'delta_w', 'delta_sink_b': 'delta_w', 'delta_w_branch_a': 'delta_w', 'delta_w_branch_b': 'delta_w', 'delta_w_out': 'delta_w', 'delta_norm_ffn_g': 'delta_w', 'delta_w_ffn_gate': 'delta_w', 'delta_w_ffn_up': 'delta_w', 'delta_w_ffn_down': 'delta_w', 'delta_norm_ple_g': 'delta_w', 'delta_w_ple_gate': 'delta_w', 'delta_w_ple_proj': 'delta_w', 'new_m_rel_table': 'new_m', 'new_m_norm_mix_g': 'new_m', 'new_m_w_in': 'new_m', 'new_m_qnorm_a_g': 'new_m', 'new_m_knorm_a_g': 'new_m', 'new_m_qnorm_b_g': 'new_m', 'new_m_knorm_b_g': 'new_m', 'new_m_sink_b': 'new_m', 'new_m_w_branch_a': 'new_m', 'new_m_w_branch_b': 'new_m', 'new_m_w_out': 'new_m', 'new_m_norm_ffn_g': 'new_m', 'new_m_w_ffn_gate': 'new_m', 'new_m_w_ffn_up': 'new_m', 'new_m_w_ffn_down': 'new_m', 'new_m_norm_ple_g': 'new_m', 'new_m_w_ple_gate': 'new_m', 'new_m_w_ple_proj': 'new_m', 'new_v_rel_table': 'new_v', 'new_v_norm_mix_g': 'new_v', 'new_v_w_in': 'new_v', 'new_v_qnorm_a_g': 'new_v', 'new_v_knorm_a_g': 'new_v', 'new_v_qnorm_b_g': 'new_v', 'new_v_knorm_b_g': 'new_v', 'new_v_sink_b': 'new_v', 'new_v_w_branch_a': 'new_v', 'new_v_w_branch_b': 'new_v', 'new_v_w_out': 'new_v', 'new_v_norm_ffn_g': 'new_v', 'new_v_w_ffn_gate': 'new_v', 'new_v_w_ffn_up': 'new_v', 'new_v_w_ffn_down': 'new_v', 'new_v_norm_ple_g': 'new_v', 'new_v_w_ple_gate': 'new_v', 'new_v_w_ple_proj': 'new_v'}


def _forward(args):
    return _fwd_reference(*[args[k] for k in FWD_PARAMS])


def _output_shape():
    out = _jax.eval_shape(lambda: _forward(_fwd_setup_inputs(0)))
    return out.shape, out.dtype

N_MICROBATCH = 1
ADAM_LR = 0.001
ADAM_B1 = 0.9
ADAM_B2 = 0.999
ADAM_EPS = 1e-08
ADAM_WD = 0.01
ADAM_STEP = 10
PER_EXAMPLE_BATCH_AXIS = {'x': 0, 'p': 1, 'loss_target': 0}
SHARED_INPUTS = []
_WEIGHT_DTYPES = {'rel_table': _jnp.float32, 'norm_mix_g': _jnp.float32, 'w_in': _jnp.float32, 'qnorm_a_g': _jnp.float32, 'knorm_a_g': _jnp.float32, 'qnorm_b_g': _jnp.float32, 'knorm_b_g': _jnp.float32, 'sink_b': _jnp.float32, 'w_branch_a': _jnp.float32, 'w_branch_b': _jnp.float32, 'w_out': _jnp.float32, 'norm_ffn_g': _jnp.float32, 'w_ffn_gate': _jnp.float32, 'w_ffn_up': _jnp.float32, 'w_ffn_down': _jnp.float32, 'norm_ple_g': _jnp.float32, 'w_ple_gate': _jnp.float32, 'w_ple_proj': _jnp.float32}
MOMENT_SCALE = {'rel_table': 7.292089e-02, 'norm_mix_g': 9.983156e-02, 'w_in': 3.370102e-02, 'qnorm_a_g': 6.153680e-01, 'knorm_a_g': 6.118971e-01, 'qnorm_b_g': 5.789413e-01, 'knorm_b_g': 5.752705e-01, 'sink_b': 1.635330e-02, 'w_branch_a': 3.431895e-02, 'w_branch_b': 2.983828e-02, 'w_out': 4.522082e-02, 'norm_ffn_g': 1.244410e+01, 'w_ffn_gate': 1.248270e-01, 'w_ffn_up': 1.287060e-01, 'w_ffn_down': 2.003603e-01, 'norm_ple_g': 4.856808e-01, 'w_ple_gate': 6.316698e-02, 'w_ple_proj': 2.646597e-01}


def _to_microbatches(a, axis):
    t = _jnp.moveaxis(a, axis, 0)
    t = t.reshape((N_MICROBATCH, t.shape[0] // N_MICROBATCH) + t.shape[1:])
    return _jnp.moveaxis(t, 1, axis + 1)


def setup_inputs(seed: int = 0) -> dict:
    inp = _fwd_setup_inputs(seed)
    key = _jax.random.fold_in(_jax.random.key(seed), 7919)
    shape, _ = _output_shape()
    out = dict(inp)
    out["loss_target"] = _jax.random.normal(_jax.random.fold_in(key, 0), shape, _jnp.float32)
    for i, name in enumerate(TWIN_WEIGHTS):
        w = inp[name].astype(_jnp.float32)
        if MOMENT_SCALE is None:
            s = _jnp.sqrt(_jnp.mean(_jnp.square(w)) + 1e-30)
        else:
            s = MOMENT_SCALE[name]
        km, kv = _jax.random.split(_jax.random.fold_in(key, i + 1))
        out[name] = w
        out["m_" + name] = s * _jax.random.normal(km, w.shape, _jnp.float32)
        out["v_" + name] = (s * s) * _jax.random.uniform(kv, w.shape, _jnp.float32, 0.5, 1.5)
    if N_MICROBATCH > 1:
        for name, axis in PER_EXAMPLE_BATCH_AXIS.items():
            out[name] = _to_microbatches(out[name], axis)
    return {'x': out['x'], 'p': out['p'], 'rel_table': out['rel_table'], 'norm_mix_g': out['norm_mix_g'], 'w_in': out['w_in'], 'qnorm_a_g': out['qnorm_a_g'], 'knorm_a_g': out['knorm_a_g'], 'qnorm_b_g': out['qnorm_b_g'], 'knorm_b_g': out['knorm_b_g'], 'sink_b': out['sink_b'], 'w_branch_a': out['w_branch_a'], 'w_branch_b': out['w_branch_b'], 'w_out': out['w_out'], 'norm_ffn_g': out['norm_ffn_g'], 'w_ffn_gate': out['w_ffn_gate'], 'w_ffn_up': out['w_ffn_up'], 'w_ffn_down': out['w_ffn_down'], 'norm_ple_g': out['norm_ple_g'], 'w_ple_gate': out['w_ple_gate'], 'w_ple_proj': out['w_ple_proj'], 'loss_target': out['loss_target'], 'm_rel_table': out['m_rel_table'], 'm_norm_mix_g': out['m_norm_mix_g'], 'm_w_in': out['m_w_in'], 'm_qnorm_a_g': out['m_qnorm_a_g'], 'm_knorm_a_g': out['m_knorm_a_g'], 'm_qnorm_b_g': out['m_qnorm_b_g'], 'm_knorm_b_g': out['m_knorm_b_g'], 'm_sink_b': out['m_sink_b'], 'm_w_branch_a': out['m_w_branch_a'], 'm_w_branch_b': out['m_w_branch_b'], 'm_w_out': out['m_w_out'], 'm_norm_ffn_g': out['m_norm_ffn_g'], 'm_w_ffn_gate': out['m_w_ffn_gate'], 'm_w_ffn_up': out['m_w_ffn_up'], 'm_w_ffn_down': out['m_w_ffn_down'], 'm_norm_ple_g': out['m_norm_ple_g'], 'm_w_ple_gate': out['m_w_ple_gate'], 'm_w_ple_proj': out['m_w_ple_proj'], 'v_rel_table': out['v_rel_table'], 'v_norm_mix_g': out['v_norm_mix_g'], 'v_w_in': out['v_w_in'], 'v_qnorm_a_g': out['v_qnorm_a_g'], 'v_knorm_a_g': out['v_knorm_a_g'], 'v_qnorm_b_g': out['v_qnorm_b_g'], 'v_knorm_b_g': out['v_knorm_b_g'], 'v_sink_b': out['v_sink_b'], 'v_w_branch_a': out['v_w_branch_a'], 'v_w_branch_b': out['v_w_branch_b'], 'v_w_out': out['v_w_out'], 'v_norm_ffn_g': out['v_norm_ffn_g'], 'v_w_ffn_gate': out['v_w_ffn_gate'], 'v_w_ffn_up': out['v_w_ffn_up'], 'v_w_ffn_down': out['v_w_ffn_down'], 'v_norm_ple_g': out['v_norm_ple_g'], 'v_w_ple_gate': out['v_w_ple_gate'], 'v_w_ple_proj': out['v_w_ple_proj']}


def _loss(weights, diff, rest, loss_target):
    with _jax.named_scope("forward"):
        args = {**rest, TWIN_DIFF_INPUT: diff, **{k: w.astype(_WEIGHT_DTYPES[k]) for k, w in weights.items()}}
        y = _forward(args)
    with _jax.named_scope("loss_head"):
        err = _jnp.square(y.astype(_jnp.float32) - loss_target)
        return 0.5 * _jnp.sum(_jnp.mean(err, axis=-1)) if err.ndim else 0.5 * err


def _adamw(w, g, m, v):
    m = ADAM_B1 * m + (1.0 - ADAM_B1) * g
    v = ADAM_B2 * v + (1.0 - ADAM_B2) * _jnp.square(g)
    m_hat = m / (1.0 - ADAM_B1 ** ADAM_STEP)
    v_hat = v / (1.0 - ADAM_B2 ** ADAM_STEP)
    delta = -ADAM_LR * (m_hat / (_jnp.sqrt(v_hat) + ADAM_EPS) + ADAM_WD * w)
    return delta, m, v


def reference(x, p, rel_table, norm_mix_g, w_in, qnorm_a_g, knorm_a_g, qnorm_b_g, knorm_b_g, sink_b, w_branch_a, w_branch_b, w_out, norm_ffn_g, w_ffn_gate, w_ffn_up, w_ffn_down, norm_ple_g, w_ple_gate, w_ple_proj, loss_target, m_rel_table, m_norm_mix_g, m_w_in, m_qnorm_a_g, m_knorm_a_g, m_qnorm_b_g, m_knorm_b_g, m_sink_b, m_w_branch_a, m_w_branch_b, m_w_out, m_norm_ffn_g, m_w_ffn_gate, m_w_ffn_up, m_w_ffn_down, m_norm_ple_g, m_w_ple_gate, m_w_ple_proj, v_rel_table, v_norm_mix_g, v_w_in, v_qnorm_a_g, v_knorm_a_g, v_qnorm_b_g, v_knorm_b_g, v_sink_b, v_w_branch_a, v_w_branch_b, v_w_out, v_norm_ffn_g, v_w_ffn_gate, v_w_ffn_up, v_w_ffn_down, v_norm_ple_g, v_w_ple_gate, v_w_ple_proj):
    given = dict(x=x, p=p, rel_table=rel_table, norm_mix_g=norm_mix_g, w_in=w_in, qnorm_a_g=qnorm_a_g, knorm_a_g=knorm_a_g, qnorm_b_g=qnorm_b_g, knorm_b_g=knorm_b_g, sink_b=sink_b, w_branch_a=w_branch_a, w_branch_b=w_branch_b, w_out=w_out, norm_ffn_g=norm_ffn_g, w_ffn_gate=w_ffn_gate, w_ffn_up=w_ffn_up, w_ffn_down=w_ffn_down, norm_ple_g=norm_ple_g, w_ple_gate=w_ple_gate, w_ple_proj=w_ple_proj, loss_target=loss_target, m_rel_table=m_rel_table, m_norm_mix_g=m_norm_mix_g, m_w_in=m_w_in, m_qnorm_a_g=m_qnorm_a_g, m_knorm_a_g=m_knorm_a_g, m_qnorm_b_g=m_qnorm_b_g, m_knorm_b_g=m_knorm_b_g, m_sink_b=m_sink_b, m_w_branch_a=m_w_branch_a, m_w_branch_b=m_w_branch_b, m_w_out=m_w_out, m_norm_ffn_g=m_norm_ffn_g, m_w_ffn_gate=m_w_ffn_gate, m_w_ffn_up=m_w_ffn_up, m_w_ffn_down=m_w_ffn_down, m_norm_ple_g=m_norm_ple_g, m_w_ple_gate=m_w_ple_gate, m_w_ple_proj=m_w_ple_proj, v_rel_table=v_rel_table, v_norm_mix_g=v_norm_mix_g, v_w_in=v_w_in, v_qnorm_a_g=v_qnorm_a_g, v_knorm_a_g=v_knorm_a_g, v_qnorm_b_g=v_qnorm_b_g, v_knorm_b_g=v_knorm_b_g, v_sink_b=v_sink_b, v_w_branch_a=v_w_branch_a, v_w_branch_b=v_w_branch_b, v_w_out=v_w_out, v_norm_ffn_g=v_norm_ffn_g, v_w_ffn_gate=v_w_ffn_gate, v_w_ffn_up=v_w_ffn_up, v_w_ffn_down=v_w_ffn_down, v_norm_ple_g=v_norm_ple_g, v_w_ple_gate=v_w_ple_gate, v_w_ple_proj=v_w_ple_proj)
    weights = {n: given[n] for n in TWIN_WEIGHTS}
    shared = {n: given[n] for n in SHARED_INPUTS}
    per_example = {n: given[n] for n in ['x', 'p']}
    grad_fn = _jax.value_and_grad(_loss, argnums=(0, 1))

    def one_microbatch(ex, loss_target):
        ex = dict(ex)
        diff = ex.pop(TWIN_DIFF_INPUT)
        return grad_fn(weights, diff, {**shared, **ex}, loss_target)

    if N_MICROBATCH == 1:
        loss, (grad_w, grad_x) = one_microbatch(per_example, given["loss_target"])
    else:
        def body(carry, xs):
            loss_sum, grad_sum = carry
            l_k, (gw_k, gx_k) = one_microbatch(xs[0], xs[1])
            with _jax.named_scope("update"):
                return (loss_sum + l_k, _jax.tree.map(_jnp.add, grad_sum, gw_k)), gx_k

        init = (_jnp.zeros((), _jnp.float32), _jax.tree.map(_jnp.zeros_like, weights))
        (loss, grad_w), grad_x = _jax.lax.scan(body, init, (per_example, given["loss_target"]))
    with _jax.named_scope("update"):
        delta_w, new_m, new_v = {}, {}, {}
        for n in TWIN_WEIGHTS:
            delta_w[n], new_m[n], new_v[n] = _adamw(weights[n], grad_w[n], given["m_" + n], given["v_" + n])
    return (loss, grad_x, *[grad_w[n] for n in TWIN_WEIGHTS], *[delta_w[n] for n in TWIN_WEIGHTS],
            *[new_m[n] for n in TWIN_WEIGHTS], *[new_v[n] for n in TWIN_WEIGHTS])
```

```python
import functools
import math

import jax
import jax.numpy as jnp
from jax import lax
from jax.experimental import pallas as pl
from jax.experimental.pallas import tpu as pltpu

F32 = jnp.float32
BF16 = jnp.bfloat16
MESH_ID = pl.DeviceIdType.MESH

SEQ = 2048
D_MODEL = 1024
DEPTH = 2
HEAD_DIM = 64
N_HEADS = 8
WIDTH = N_HEADS * HEAD_DIM
N_KV_B = 2
PLE_DIM = 256
D_FF = 2816
D_IN = 4352
OFF_QA, OFF_KA, OFF_VA, OFF_QB, OFF_KB, OFF_VB, OFF_GA, OFF_GB = 0, 512, 1024, 1536, 2048, 2176, 2304, 3328
DILATED = ((64, 1), (64, 4), (64, 16))
BLK_B = 128
NUM_BUCKETS = 32
MAX_DISTANCE = 1024
RMS_EPS = 1e-6
NEG_INF = -1e30
LANES = 128
VMEM_LIMIT = 48 * 1024 * 1024

ADAM_LR, ADAM_B1, ADAM_B2, ADAM_EPS, ADAM_WD, ADAM_STEP = 0.001, 0.9, 0.999, 1e-08, 0.01, 10

BIG = (
    ("w_in", (D_MODEL, D_IN), 1),
    ("w_branch_a", (WIDTH, D_MODEL), 1),
    ("w_branch_b", (WIDTH, D_MODEL), 1),
    ("w_out", (D_MODEL, D_MODEL), 0),
    ("w_ffn_gate", (D_MODEL, D_FF), 1),
    ("w_ffn_up", (D_MODEL, D_FF), 1),
    ("w_ffn_down", (D_FF, D_MODEL), 0),
    ("w_ple_gate", (D_MODEL, D_MODEL), 0),
    ("w_ple_proj", (PLE_DIM, D_MODEL), 1),
)
SMALL = ("rel_table", "norm_mix_g", "qnorm_a_g", "knorm_a_g", "qnorm_b_g", "knorm_b_g", "sink_b",
         "norm_ffn_g", "norm_ple_g")
WEIGHTS = ("rel_table", "norm_mix_g", "w_in", "qnorm_a_g", "knorm_a_g", "qnorm_b_g", "knorm_b_g", "sink_b",
           "w_branch_a", "w_branch_b", "w_out", "norm_ffn_g", "w_ffn_gate", "w_ffn_up", "w_ffn_down",
           "norm_ple_g", "w_ple_gate", "w_ple_proj")
N_CHIPS = 4
FLAT_COLS = 1024
FLAT_ROWS = 8192
HALF_ROWS = FLAT_ROWS // 2
SMALL_ROWS = 64


def _params(*sem):
    return pltpu.CompilerParams(dimension_semantics=sem, vmem_limit_bytes=VMEM_LIMIT)


def _pick(dim, target):
    for t in (target, 512, 256, 128, 64, 32, 16, 8):
        if t <= target and dim % t == 0:
            return t
    return dim


def _mm(a, b, mode, out_dtype, name, res=None, tm=512, tn=512):
    if mode == "nn":
        (m, k), (_, n) = a.shape, b.shape
    elif mode == "nt":
        (m, k), (n, _) = a.shape, b.shape
    else:
        (k, m), (_, n) = a.shape, b.shape
    tm, tn = _pick(m, tm), _pick(n, tn)
    a_spec = pl.BlockSpec((k, tm), lambda i, j: (0, i)) if mode == "tn" else pl.BlockSpec((tm, k), lambda i, j: (i, 0))
    b_spec = pl.BlockSpec((tn, k), lambda i, j: (j, 0)) if mode == "nt" else pl.BlockSpec((k, tn), lambda i, j: (0, j))
    dims = {"nn": (((1,), (0,)), ((), ())), "nt": (((1,), (1,)), ((), ())), "tn": (((0,), (0,)), ((), ()))}[mode]
    has_res = res is not None

    def body(*refs):
        a_ref, b_ref = refs[0], refs[1]
        o_ref = refs[-1]
        acc = lax.dot_general(a_ref[...].astype(BF16), b_ref[...].astype(BF16), dims, preferred_element_type=F32)
        if has_res:
            acc = acc + refs[2][...]
        o_ref[...] = acc.astype(out_dtype)

    in_specs = [a_spec, b_spec]
    args = [a, b]
    if has_res:
        in_specs.append(pl.BlockSpec((tm, tn), lambda i, j: (i, j)))
        args.append(res)
    return pl.pallas_call(
        body, out_shape=jax.ShapeDtypeStruct((m, n), out_dtype), grid=(m // tm, n // tn),
        in_specs=in_specs, out_specs=pl.BlockSpec((tm, tn), lambda i, j: (i, j)),
        name=name, compiler_params=_params("parallel", "parallel"))(*args)


def _ew(fn, ins, out_dtypes, *, width, bw, name, vecs=(), tm=256):
    rows = ins[0][0].shape[0]
    tm = _pick(rows, tm)
    n_in = len(ins) + len(vecs)

    def col_map(off_blocks):
        return lambda i, j: (i, off_blocks + j)

    in_specs = [pl.BlockSpec((tm, bw), col_map(off // bw)) for _, off in ins]
    in_specs += [pl.BlockSpec((1, bw), lambda i, j: (0, j)) for _ in vecs]

    def body(*refs):
        outs = fn(*[r[...] for r in refs[:n_in]])
        for r, o in zip(refs[n_in:], outs):
            r[...] = o.astype(r.dtype)

    return pl.pallas_call(
        body, out_shape=[jax.ShapeDtypeStruct((rows, width), dt) for dt in out_dtypes],
        grid=(rows // tm, width // bw), in_specs=in_specs,
        out_specs=[pl.BlockSpec((tm, bw), lambda i, j: (i, j)) for _ in out_dtypes],
        name=name, compiler_params=_params("parallel", "parallel"))(*[a for a, _ in ins], *vecs)


def _sigmoid(x):
    return 1.0 / (1.0 + jnp.exp(-x))


def _seg_sum(v):
    outs = []
    for k in range(v.shape[1] // LANES):
        vp = v[:, k * LANES:(k + 1) * LANES]
        left = lax.broadcasted_iota(jnp.int32, vp.shape, 1) < HEAD_DIM
        sl = jnp.sum(jnp.where(left, vp, 0.0), axis=-1, keepdims=True)
        sr = jnp.sum(jnp.where(left, 0.0, vp), axis=-1, keepdims=True)
        outs.append(jnp.where(left, sl, sr))
    return outs[0] if len(outs) == 1 else jnp.concatenate(outs, axis=1)


def _seg_rstd(x):
    return lax.rsqrt(_seg_sum(x * x) * (1.0 / HEAD_DIM) + RMS_EPS)


def _rms_fwd(x, g, name):
    rows, d = x.shape
    tm = 256

    def body(x_ref, g_ref, h_ref):
        xv = x_ref[...]
        r = lax.rsqrt(jnp.mean(xv * xv, axis=-1, keepdims=True) + RMS_EPS)
        h_ref[...] = ((xv * r) * g_ref[...]).astype(BF16)

    return pl.pallas_call(
        body, out_shape=jax.ShapeDtypeStruct((rows, d), BF16), grid=(rows // tm,),
        in_specs=[pl.BlockSpec((tm, d), lambda i: (i, 0)), pl.BlockSpec((1, d), lambda i: (0, 0))],
        out_specs=pl.BlockSpec((tm, d), lambda i: (i, 0)), name=name, compiler_params=_params("parallel"))(x, g)


def _rms_bwd(x, g, dh, dres, name):
    rows, d = x.shape
    tm = 256

    def body(x_ref, g_ref, dh_ref, dres_ref, dx_ref, dg_ref):
        xv = x_ref[...]
        r = lax.rsqrt(jnp.mean(xv * xv, axis=-1, keepdims=True) + RMS_EPS)
        xh = xv * r
        dhv = dh_ref[...]
        dxh = dhv * g_ref[...]
        dx_ref[...] = dres_ref[...] + r * (dxh - xh * jnp.mean(dxh * xh, axis=-1, keepdims=True))
        part = jnp.sum(dhv * xh, axis=0, keepdims=True)

        @pl.when(pl.program_id(0) == 0)
        def _():
            dg_ref[...] = part

        @pl.when(pl.program_id(0) > 0)
        def _():
            dg_ref[...] += part

    row = pl.BlockSpec((tm, d), lambda i: (i, 0))
    vec = pl.BlockSpec((1, d), lambda i: (0, 0))
    return pl.pallas_call(
        body, out_shape=[jax.ShapeDtypeStruct((rows, d), F32), jax.ShapeDtypeStruct((1, d), F32)],
        grid=(rows // tm,), in_specs=[row, vec, row, row], out_specs=[row, vec],
        name=name, compiler_params=_params("arbitrary"))(x, g, dh, dres)


def _loss_grad(y, t):
    rows, d = y.shape
    tm = 256

    def body(y_ref, t_ref, dy_ref, l_ref):
        e = y_ref[...] - t_ref[...]
        dy_ref[...] = e * (1.0 / d)
        part = jnp.zeros((1, LANES), F32) + jnp.sum(e * e) * (0.5 / d)

        @pl.when(pl.program_id(0) == 0)
        def _():
            l_ref[...] = part

        @pl.when(pl.program_id(0) > 0)
        def _():
            l_ref[...] += part

    row = pl.BlockSpec((tm, d), lambda i: (i, 0))
    return pl.pallas_call(
        body, out_shape=[jax.ShapeDtypeStruct((rows, d), F32), jax.ShapeDtypeStruct((1, LANES), F32)],
        grid=(rows // tm,), in_specs=[row, row], out_specs=[row, pl.BlockSpec((1, LANES), lambda i: (0, 0))],
        name="loss_grad", compiler_params=_params("arbitrary"))(y, t)


def _swap_halves(v):
    return pltpu.roll(v, HEAD_DIM, axis=1)


def _expand_kv(kv):
    left = lax.broadcasted_iota(jnp.int32, kv.shape, 1) < HEAD_DIM
    sw = _swap_halves(kv)
    h0 = jnp.where(left, kv, sw)
    h1 = jnp.where(left, sw, kv)
    return jnp.concatenate([h0, h0, h1, h1], axis=1)


def _reduce_kv(dkv):
    left = lax.broadcasted_iota(jnp.int32, (dkv.shape[0], LANES), 1) < HEAD_DIM
    t = dkv[:, 0:LANES] + dkv[:, LANES:2 * LANES]
    u = dkv[:, 2 * LANES:3 * LANES] + dkv[:, 3 * LANES:4 * LANES]
    t = t + _swap_halves(t)
    u = u + _swap_halves(u)
    return jnp.where(left, t, u)


def _qknorm_fwd(proj, gqa, gka, gqb, gkb):
    rows = proj.shape[0]
    tm = 256

    def body(qa_ref, ka_ref, va_ref, qb_ref, kb_ref, vb_ref, gqa_ref, gka_ref, gqb_ref, gkb_ref,
             oqa, oka, ova, oqb, okb, ovb):
        for src, g_ref, dst in ((qa_ref, gqa_ref, oqa), (ka_ref, gka_ref, oka), (qb_ref, gqb_ref, oqb)):
            xv = src[...]
            dst[...] = ((xv * _seg_rstd(xv)) * g_ref[...]).astype(BF16)
        ova[...] = va_ref[...].astype(BF16)
        kv = kb_ref[...]
        okb[...] = _expand_kv((kv * _seg_rstd(kv)) * gkb_ref[...]).astype(BF16)
        ovb[...] = _expand_kv(vb_ref[...]).astype(BF16)

    def win(width, off):
        return pl.BlockSpec((tm, width), lambda i: (i, off // width))

    vec = lambda w: pl.BlockSpec((1, w), lambda i: (0, 0))
    out = pl.BlockSpec((tm, WIDTH), lambda i: (i, 0))
    return pl.pallas_call(
        body, out_shape=[jax.ShapeDtypeStruct((rows, WIDTH), BF16)] * 6, grid=(rows // tm,),
        in_specs=[win(WIDTH, OFF_QA), win(WIDTH, OFF_KA), win(WIDTH, OFF_VA), win(WIDTH, OFF_QB),
                  win(LANES, OFF_KB), win(LANES, OFF_VB), vec(WIDTH), vec(WIDTH), vec(WIDTH), vec(LANES)],
        out_specs=[out] * 6, name="qknorm_fwd", compiler_params=_params("parallel"))(
            proj, proj, proj, proj, proj, proj, gqa, gka, gqb, gkb)


def _norm_bwd(xv, g, dy):
    r = _seg_rstd(xv)
    xh = xv * r
    dxh = dy * g
    dx = r * (dxh - xh * (_seg_sum(dxh * xh) * (1.0 / HEAD_DIM)))
    return dx, jnp.sum(dy * xh, axis=0, keepdims=True)


def _qknorm_bwd(proj, gqa, gka, gqb, gkb, dqa, dka, dva, dqb, dkb, dvb, dgab):
    rows = proj.shape[0]
    tm = 256
    n_a = len(dqa)

    def body(*refs):
        qa_ref, ka_ref, qb_ref, kb_ref, gqa_ref, gka_ref, gqb_ref, gkb_ref = refs[:8]
        pos = 8
        dqa_refs, dka_refs, dva_refs = refs[pos:pos + n_a], refs[pos + n_a:pos + 2 * n_a], refs[pos + 2 * n_a:pos + 3 * n_a]
        pos += 3 * n_a
        dqb_ref, dkb_ref, dvb_ref, dgab_ref = refs[pos:pos + 4]
        dproj_ref, ogqa, ogka, ogqb, ogkb = refs[pos + 4:]

        def total(rs):
            acc = rs[0][...]
            for r in rs[1:]:
                acc = acc + r[...]
            return acc

        dx_qa, p_qa = _norm_bwd(qa_ref[...], gqa_ref[...], total(dqa_refs))
        dx_ka, p_ka = _norm_bwd(ka_ref[...], gka_ref[...], total(dka_refs))
        dx_qb, p_qb = _norm_bwd(qb_ref[...], gqb_ref[...], dqb_ref[...])
        dx_kb, p_kb = _norm_bwd(kb_ref[...], gkb_ref[...], _reduce_kv(dkb_ref[...]))
        dproj_ref[:, OFF_QA:OFF_QA + WIDTH] = dx_qa.astype(BF16)
        dproj_ref[:, OFF_KA:OFF_KA + WIDTH] = dx_ka.astype(BF16)
        dproj_ref[:, OFF_VA:OFF_VA + WIDTH] = total(dva_refs).astype(BF16)
        dproj_ref[:, OFF_QB:OFF_QB + WIDTH] = dx_qb.astype(BF16)
        dproj_ref[:, OFF_KB:OFF_KB + LANES] = dx_kb.astype(BF16)
        dproj_ref[:, OFF_VB:OFF_VB + LANES] = _reduce_kv(dvb_ref[...]).astype(BF16)
        dproj_ref[:, OFF_GA:D_IN] = dgab_ref[...]
        first = pl.program_id(0) == 0
        for o_ref, part in ((ogqa, p_qa), (ogka, p_ka), (ogqb, p_qb), (ogkb, p_kb)):
            @pl.when(first)
            def _(o_ref=o_ref, part=part):
                o_ref[...] = part

            @pl.when(jnp.logical_not(first))
            def _(o_ref=o_ref, part=part):
                o_ref[...] += part

    def win(width, off):
        return pl.BlockSpec((tm, width), lambda i: (i, off // width))

    vec = lambda w: pl.BlockSpec((1, w), lambda i: (0, 0))
    row = lambda w: pl.BlockSpec((tm, w), lambda i: (i, 0))
    in_specs = [win(WIDTH, OFF_QA), win(WIDTH, OFF_KA), win(WIDTH, OFF_QB), win(LANES, OFF_KB),
                vec(WIDTH), vec(WIDTH), vec(WIDTH), vec(LANES)]
    in_specs += [row(WIDTH)] * (3 * n_a + 3) + [row(2 * D_MODEL)]
    return pl.pallas_call(
        body,
        out_shape=[jax.ShapeDtypeStruct((rows, D_IN), BF16), jax.ShapeDtypeStruct((1, WIDTH), F32),
                   jax.ShapeDtypeStruct((1, WIDTH), F32), jax.ShapeDtypeStruct((1, WIDTH), F32),
                   jax.ShapeDtypeStruct((1, LANES), F32)],
        grid=(rows // tm,), in_specs=in_specs,
        out_specs=[row(D_IN), vec(WIDTH), vec(WIDTH), vec(WIDTH), vec(LANES)],
        name="qknorm_bwd", compiler_params=_params("arbitrary"))(
            proj, proj, proj, proj, gqa, gka, gqb, gkb, *dqa, *dka, *dva, dqb, dkb, dvb, dgab)


def _t5_bucket(rel):
    half_b = NUM_BUCKETS // 2
    max_exact = half_b // 2
    sign = jnp.where(rel > 0, half_b, 0)
    n = jnp.abs(rel)
    nf = jnp.maximum(n, 1).astype(F32)
    large = max_exact + (jnp.log(nf / max_exact) / math.log(MAX_DISTANCE / max_exact)
                         * (half_b - max_exact)).astype(jnp.int32)
    large = jnp.minimum(large, half_b - 1)
    return sign + jnp.where(n < max_exact, n, large)


def _band_buckets(blk, dilation):
    i = jnp.arange(blk, dtype=jnp.int32)[:, None]
    j = jnp.arange(3 * blk, dtype=jnp.int32)[None, :]
    rel = j - blk - i
    return jnp.where(jnp.abs(rel) <= blk, _t5_bucket(rel * dilation), -1)


def _bias_tiles(table, buckets, head_off, name):
    blk = buckets.shape[0]

    def body(tab_ref, bk_ref, o_ref):
        h = pl.program_id(0) + head_off
        bk = bk_ref[...]
        acc = jnp.full(bk.shape, NEG_INF, F32)
        for b in range(NUM_BUCKETS):
            acc = jnp.where(bk == b, tab_ref[b, h], acc)
        o_ref[0] = acc

    return pl.pallas_call(
        body, out_shape=jax.ShapeDtypeStruct((N_HEADS, blk, 3 * blk), F32), grid=(N_HEADS,),
        in_specs=[pl.BlockSpec(memory_space=pltpu.SMEM), pl.BlockSpec((blk, 3 * blk), lambda h: (0, 0))],
        out_specs=pl.BlockSpec((1, blk, 3 * blk), lambda h: (h, 0, 0)),
        name=name, compiler_params=_params("parallel"))(table, buckets)


def _table_grad(dbias, buckets, name):
    blk = buckets.shape[0]

    def body(db_ref, bk_ref, o_ref):
        bk = bk_ref[...]
        dbv = db_ref[0]
        lane = lax.broadcasted_iota(jnp.int32, (1, LANES), 1)
        acc = jnp.zeros((1, LANES), F32)
        for b in range(NUM_BUCKETS):
            acc = jnp.where(lane == b, jnp.sum(jnp.where(bk == b, dbv, 0.0)), acc)
        o_ref[0] = acc

    out = pl.pallas_call(
        body, out_shape=jax.ShapeDtypeStruct((N_HEADS, 1, LANES), F32), grid=(N_HEADS,),
        in_specs=[pl.BlockSpec((1, blk, 3 * blk), lambda h: (h, 0, 0)), pl.BlockSpec((blk, 3 * blk), lambda h: (0, 0))],
        out_specs=pl.BlockSpec((1, 1, LANES), lambda h: (h, 0, 0)),
        name=name, compiler_params=_params("parallel"))(dbias, buckets)
    return out[:, 0, :NUM_BUCKETS]


def _dot_nt(a, b):
    return lax.dot_general(a, b, (((1,), (1,)), ((), ())), preferred_element_type=F32)


def _edge_penalty(i, nb, blk):
    col = lax.broadcasted_iota(jnp.int32, (1, 3 * blk), 1)
    ok = jnp.logical_and(jnp.logical_or(col >= blk, i > 0), jnp.logical_or(col < 2 * blk, i < nb - 1))
    return jnp.where(ok, 0.0, NEG_INF).astype(F32)


def _attn_fwd(q, k, v, bias, sink, blk, name):
    lv, rc = q.shape
    nres, nb = rc // WIDTH, lv // blk
    has_sink = sink is not None

    def body(*refs):
        q_ref, kp, kc, kn, vp, vc, vn, b_ref = refs[:8]
        s_ref = refs[8] if has_sink else None
        o_ref, l_ref = refs[-2], refs[-1]
        i = pl.program_id(1)
        pen = _edge_penalty(i, nb, blk)
        lane = lax.broadcasted_iota(jnp.int32, (1, LANES), 1)
        for hp in range(WIDTH // LANES):
            sl = slice(hp * LANES, (hp + 1) * LANES)
            q2 = q_ref[:, sl].astype(F32) * (HEAD_DIM ** -0.5)
            kcat = jnp.concatenate([kp[:, sl], kc[:, sl], kn[:, sl]], axis=0)
            vcat = jnp.concatenate([vp[:, sl], vc[:, sl], vn[:, sl]], axis=0)
            o_acc = jnp.zeros((blk, LANES), F32)
            l_acc = jnp.zeros((blk, LANES), F32)
            for side in range(2):
                h = 2 * hp + side
                mine = (lane < HEAD_DIM) if side == 0 else (lane >= HEAD_DIM)
                qm = jnp.where(mine, q2, 0.0).astype(BF16)
                s = _dot_nt(qm, kcat) + b_ref[h] + pen
                m = jnp.max(s, axis=-1, keepdims=True)
                if has_sink:
                    m = jnp.maximum(m, s_ref[h])
                p = jnp.exp(s - m)
                den = jnp.sum(p, axis=-1, keepdims=True)
                if has_sink:
                    den = den + jnp.exp(s_ref[h] - m)
                pn = (p / den).astype(BF16)
                o_acc = jnp.where(mine, jnp.dot(pn, vcat, preferred_element_type=F32), o_acc)
                l_acc = jnp.where(mine, m + jnp.log(den), l_acc)
            o_ref[:, sl] = o_acc
            l_ref[:, sl] = l_acc

    cur = pl.BlockSpec((blk, WIDTH), lambda r, i: (i, r))
    prev = pl.BlockSpec((blk, WIDTH), lambda r, i: (jnp.maximum(i - 1, 0), r))
    nxt = pl.BlockSpec((blk, WIDTH), lambda r, i: (jnp.minimum(i + 1, nb - 1), r))
    in_specs = [cur, prev, cur, nxt, prev, cur, nxt, pl.BlockSpec((N_HEADS, blk, 3 * blk), lambda r, i: (0, 0, 0))]
    args = [q, k, k, k, v, v, v, bias]
    if has_sink:
        in_specs.append(pl.BlockSpec(memory_space=pltpu.SMEM))
        args.append(sink)
    return pl.pallas_call(
        body, out_shape=[jax.ShapeDtypeStruct((lv, rc), F32)] * 2, grid=(nres, nb),
        in_specs=in_specs, out_specs=[cur, cur], name=name, compiler_params=_params("parallel", "parallel"))(*args)


def _attn_bwd(q, k, v, do, lse, delta, bias, sink, blk, name):
    lv, rc = q.shape
    nres, nb = rc // WIDTH, lv // blk
    has_sink = sink is not None
    n_in = 12 if has_sink else 11

    def body(*refs):
        q_ref, kp, kc, kn, vp, vc, vn, do_ref, l_ref, d_ref, b_ref = refs[:11]
        s_ref = refs[11] if has_sink else None
        dq_ref, dk_ref, dv_ref, db_ref = refs[n_in:n_in + 4]
        ds_ref = refs[n_in + 4] if has_sink else None
        acck, accv = refs[-2], refs[-1]
        r, i = pl.program_id(0), pl.program_id(1)

        @pl.when(jnp.logical_and(r == 0, i == 0))
        def _():
            db_ref[...] = jnp.zeros_like(db_ref)
            if has_sink:
                ds_ref[...] = jnp.zeros_like(ds_ref)

        @pl.when(i == 0)
        def _():
            acck[...] = jnp.zeros_like(acck)
            accv[...] = jnp.zeros_like(accv)

        @pl.when(i < nb)
        def _():
            pen = _edge_penalty(i, nb, blk)
            lane = lax.broadcasted_iota(jnp.int32, (1, LANES), 1)
            for hp in range(WIDTH // LANES):
                sl = slice(hp * LANES, (hp + 1) * LANES)
                q2 = q_ref[:, sl].astype(F32) * (HEAD_DIM ** -0.5)
                do2, l2, d2 = do_ref[:, sl], l_ref[:, sl], d_ref[:, sl]
                kcat = jnp.concatenate([kp[:, sl], kc[:, sl], kn[:, sl]], axis=0)
                vcat = jnp.concatenate([vp[:, sl], vc[:, sl], vn[:, sl]], axis=0)
                dq_acc = jnp.zeros((blk, LANES), F32)
                dk_acc = jnp.zeros((3 * blk, LANES), F32)
                dv_acc = jnp.zeros((3 * blk, LANES), F32)
                for side in range(2):
                    h = 2 * hp + side
                    mine = (lane < HEAD_DIM) if side == 0 else (lane >= HEAD_DIM)
                    qm = jnp.where(mine, q2, 0.0).astype(BF16)
                    dom = jnp.where(mine, do2, 0.0).astype(BF16)
                    lcol = jnp.max(jnp.where(mine, l2, NEG_INF), axis=-1, keepdims=True)
                    dcol = jnp.sum(jnp.where(lane == side * HEAD_DIM, d2, 0.0), axis=-1, keepdims=True)
                    p = jnp.exp(_dot_nt(qm, kcat) + b_ref[h] + pen - lcol)
                    ds = p * (_dot_nt(dom, vcat) - dcol)
                    db_ref[h] += ds
                    dsb = ds.astype(BF16)
                    dq_acc = jnp.where(mine, jnp.dot(dsb, kcat, preferred_element_type=F32) * (HEAD_DIM ** -0.5), dq_acc)
                    dk_acc = dk_acc + jnp.dot(jnp.transpose(ds).astype(BF16), qm, preferred_element_type=F32)
                    dv_acc = dv_acc + jnp.dot(jnp.transpose(p).astype(BF16), dom, preferred_element_type=F32)
                    if has_sink:
                        ds_ref[h:h + 1, :] += jnp.sum(-dcol * jnp.exp(s_ref[h] - lcol))
                dq_ref[:, sl] = dq_acc
                for out_ref, acc, new in ((dk_ref, acck, dk_acc), (dv_ref, accv, dv_acc)):
                    out_ref[:, sl] = acc[0, :, sl] + new[0:blk]
                    acc[0, :, sl] = acc[1, :, sl] + new[blk:2 * blk]
                    acc[1, :, sl] = new[2 * blk:3 * blk]

        @pl.when(i == nb)
        def _():
            dk_ref[...] = acck[0]
            dv_ref[...] = accv[0]

    last = nb - 1
    cur = pl.BlockSpec((blk, WIDTH), lambda r, i: (jnp.minimum(i, last), r))
    prev = pl.BlockSpec((blk, WIDTH), lambda r, i: (jnp.clip(i - 1, 0, last), r))
    nxt = pl.BlockSpec((blk, WIDTH), lambda r, i: (jnp.minimum(i + 1, last), r))
    lag = pl.BlockSpec((blk, WIDTH), lambda r, i: (jnp.maximum(i - 1, 0), r))
    band = pl.BlockSpec((N_HEADS, blk, 3 * blk), lambda r, i: (0, 0, 0))
    in_specs = [cur, prev, cur, nxt, prev, cur, nxt, cur, cur, cur, band]
    args = [q, k, k, k, v, v, v, do, lse, delta, bias]
    out_shape = [jax.ShapeDtypeStruct((lv, rc), F32)] * 3 + [jax.ShapeDtypeStruct((N_HEADS, blk, 3 * blk), F32)]
    out_specs = [cur, lag, lag, band]
    if has_sink:
        in_specs.append(pl.BlockSpec(memory_space=pltpu.SMEM))
        args.append(sink)
        out_shape.append(jax.ShapeDtypeStruct((N_HEADS, LANES), F32))
        out_specs.append(pl.BlockSpec((N_HEADS, LANES), lambda r, i: (0, 0)))
    return pl.pallas_call(
        body, out_shape=out_shape, grid=(nres, nb + 1), in_specs=in_specs, out_specs=out_specs,
        scratch_shapes=[pltpu.VMEM((2, blk, WIDTH), F32), pltpu.VMEM((2, blk, WIDTH), F32)],
        name=name, compiler_params=_params("arbitrary", "arbitrary"))(*args)


def _to_view(a, d):
    return a if d == 1 else a.reshape(a.shape[0] // d, d * a.shape[1])


def _from_view(a, d):
    return a if d == 1 else a.reshape(a.shape[0] * d, a.shape[1] // d)


def _tile_gain(g, reps):
    return jnp.tile(g[None, :], (1, reps))


def _local_step(x, p, target, big, small):
    rel_table = small["rel_table"]
    buckets_a = [_band_buckets(blk, d) for blk, d in DILATED]
    buckets_b = _band_buckets(BLK_B, 1)
    bias_a = [_bias_tiles(rel_table, bk, 0, "bias_a") for bk in buckets_a]
    bias_b = _bias_tiles(rel_table, buckets_b, N_HEADS, "bias_b")

    saved = []
    for l in range(DEPTH):
        w = {n: big[n][l] for n, _, _ in BIG}
        g_mix, g_ffn, g_ple = (small[n][l][None, :] for n in ("norm_mix_g", "norm_ffn_g", "norm_ple_g"))
        gqa, gka, gqb = (_tile_gain(small[n][l], N_HEADS) for n in ("qnorm_a_g", "knorm_a_g", "qnorm_b_g"))
        gkb = _tile_gain(small["knorm_b_g"][l], N_KV_B)
        sink = small["sink_b"][l]

        h = _rms_fwd(x, g_mix, "rms_mix")
        proj = _mm(h, w["w_in"], "nn", F32, "mm_in")
        qa, ka, va, qb, kb, vb = _qknorm_fwd(proj, gqa, gka, gqb, gkb)
        outs, lses = [], []
        for (blk, d), bias in zip(DILATED, bias_a):
            o, ls = _attn_fwd(_to_view(qa, d), _to_view(ka, d), _to_view(va, d), bias, None, blk, f"attn_a{d}_fwd")
            outs.append(_from_view(o, d))
            lses.append(_from_view(ls, d))

        def combine(o1, o2, o3, l1, l2, l3):
            m = jnp.maximum(jnp.maximum(l1, l2), l3)
            e1, e2, e3 = jnp.exp(l1 - m), jnp.exp(l2 - m), jnp.exp(l3 - m)
            den = e1 + e2 + e3
            return (e1 * o1 + e2 * o2 + e3 * o3) / den, m + jnp.log(den)

        ya, lse_a = _ew(combine, [(t, 0) for t in outs + lses], [F32, F32], width=WIDTH, bw=WIDTH, name="combine_a")
        yb, lse_b = _attn_fwd(qb, kb, vb, bias_b, sink, BLK_B, "attn_b_fwd")
        ca = _mm(ya, w["w_branch_a"], "nn", F32, "mm_branch_a")
        cb = _mm(yb, w["w_branch_b"], "nn", F32, "mm_branch_b")

        def gate(ca_, cb_, ga_, gb_):
            return (_sigmoid(ga_) * ca_ + _sigmoid(gb_) * cb_,)

        (merged,) = _ew(gate, [(ca, 0), (cb, 0), (proj, OFF_GA), (proj, OFF_GB)], [BF16],
                        width=D_MODEL, bw=256, name="gate")
        x1 = _mm(merged, w["w_out"], "nn", F32, "mm_out", res=x)

        h2 = _rms_fwd(x1, g_ffn, "rms_ffn")
        a = _mm(h2, w["w_ffn_gate"], "nn", F32, "mm_ffn_gate")
        u = _mm(h2, w["w_ffn_up"], "nn", F32, "mm_ffn_up")

        def swiglu(a_, u_):
            return ((a_ * _sigmoid(a_)) * u_,)

        (hid,) = _ew(swiglu, [(a, 0), (u, 0)], [BF16], width=D_FF, bw=D_FF, name="swiglu")
        x2 = _mm(hid, w["w_ffn_down"], "nn", F32, "mm_ffn_down", res=x1)

        h3 = _rms_fwd(x2, g_ple, "rms_ple")
        z = _mm(h3, w["w_ple_gate"], "nn", F32, "mm_ple_gate")
        e = _mm(p[l], w["w_ple_proj"], "nn", F32, "mm_ple_proj")

        def ple(x2_, z_, e_):
            return (x2_ + _sigmoid(z_) * e_,)

        (x3,) = _ew(ple, [(x2, 0), (z, 0), (e, 0)], [F32], width=D_MODEL, bw=D_MODEL, name="ple")
        saved.append(dict(x0=x, h=h, proj=proj, qa=qa, ka=ka, va=va, qb=qb, kb=kb, vb=vb, ya=ya, lse_a=lse_a,
                          yb=yb, lse_b=lse_b, ca=ca, cb=cb, merged=merged, x1=x1, h2=h2, a=a, u=u, hid=hid,
                          x2=x2, h3=h3, z=z, e=e))
        x = x3

    dx, loss_acc = _loss_grad(x, target)
    loss = loss_acc[0, 0]

    gbig = {n: [None] * DEPTH for n, _, _ in BIG}
    gsmall = {n: [None] * DEPTH for n in SMALL if n != "rel_table"}
    dtable_a = jnp.zeros((N_HEADS, NUM_BUCKETS), F32)
    dtable_b = jnp.zeros((N_HEADS, NUM_BUCKETS), F32)

    for l in reversed(range(DEPTH)):
        sv = saved[l]
        w = {n: big[n][l] for n, _, _ in BIG}
        g_mix, g_ffn, g_ple = (small[n][l][None, :] for n in ("norm_mix_g", "norm_ffn_g", "norm_ple_g"))
        gqa, gka, gqb = (_tile_gain(small[n][l], N_HEADS) for n in ("qnorm_a_g", "knorm_a_g", "qnorm_b_g"))
        gkb = _tile_gain(small["knorm_b_g"][l], N_KV_B)
        sink = small["sink_b"][l]

        def ple_bwd(dx_, z_, e_):
            s = _sigmoid(z_)
            return dx_ * s, dx_ * e_ * (s * (1.0 - s))

        de, dz = _ew(ple_bwd, [(dx, 0), (sv["z"], 0), (sv["e"], 0)], [BF16, BF16], width=D_MODEL, bw=D_MODEL,
                     name="ple_bwd")
        gbig["w_ple_proj"][l] = _mm(p[l], de, "tn", F32, "mm_d_ple_proj")
        gbig["w_ple_gate"][l] = _mm(sv["h3"], dz, "tn", F32, "mm_d_ple_gate")
        dh3 = _mm(dz, w["w_ple_gate"], "nt", F32, "mm_dh3")
        dx, gsmall["norm_ple_g"][l] = _rms_bwd(sv["x2"], g_ple, dh3, dx, "rms_ple_bwd")

        dhid = _mm(dx, w["w_ffn_down"], "nt", F32, "mm_dhid")
        gbig["w_ffn_down"][l] = _mm(sv["hid"], dx, "tn", F32, "mm_d_ffn_down")

        def swiglu_bwd(a_, u_, dh_):
            s = _sigmoid(a_)
            return dh_ * u_ * (s * (1.0 + a_ * (1.0 - s))), dh_ * (a_ * s)

        da, du = _ew(swiglu_bwd, [(sv["a"], 0), (sv["u"], 0), (dhid, 0)], [BF16, BF16], width=D_FF, bw=D_FF,
                     name="swiglu_bwd")
        gbig["w_ffn_gate"][l] = _mm(sv["h2"], da, "tn", F32, "mm_d_ffn_gate")
        gbig["w_ffn_up"][l] = _mm(sv["h2"], du, "tn", F32, "mm_d_ffn_up")
        dh2 = _mm(da, w["w_ffn_gate"], "nt", F32, "mm_dh2_gate")
        dh2 = _mm(du, w["w_ffn_up"], "nt", F32, "mm_dh2_up", res=dh2)
        dx, gsmall["norm_ffn_g"][l] = _rms_bwd(sv["x1"], g_ffn, dh2, dx, "rms_ffn_bwd")

        dmerged = _mm(dx, w["w_out"], "nt", F32, "mm_dmerged")
        gbig["w_out"][l] = _mm(sv["merged"], dx, "tn", F32, "mm_d_out")

        def gate_bwd(dm_, ca_, cb_, ga_, gb_):
            sa, sb = _sigmoid(ga_), _sigmoid(gb_)
            return dm_ * sa, dm_ * sb, dm_ * ca_ * (sa * (1.0 - sa)), dm_ * cb_ * (sb * (1.0 - sb))

        dca, dcb, dga, dgb = _ew(gate_bwd, [(dmerged, 0), (sv["ca"], 0), (sv["cb"], 0), (sv["proj"], OFF_GA),
                                            (sv["proj"], OFF_GB)], [BF16, BF16, BF16, BF16],
                                 width=D_MODEL, bw=256, name="gate_bwd")
        dgab = jnp.concatenate([dga, dgb], axis=1)
        gbig["w_branch_a"][l] = _mm(sv["ya"], dca, "tn", F32, "mm_d_branch_a")
        gbig["w_branch_b"][l] = _mm(sv["yb"], dcb, "tn", F32, "mm_d_branch_b")
        dya = _mm(dca, w["w_branch_a"], "nt", F32, "mm_dya")
        dyb = _mm(dcb, w["w_branch_b"], "nt", F32, "mm_dyb")

        def rowdot(dy_, y_):
            return (_seg_sum(dy_ * y_),)

        (delta_a,) = _ew(rowdot, [(dya, 0), (sv["ya"], 0)], [F32], width=WIDTH, bw=WIDTH, name="delta_a")
        (delta_b,) = _ew(rowdot, [(dyb, 0), (sv["yb"], 0)], [F32], width=WIDTH, bw=WIDTH, name="delta_b")

        dqa, dka, dva = [], [], []
        for (blk, d), bias, bk in zip(DILATED, bias_a, buckets_a):
            dq_, dk_, dv_, db_ = _attn_bwd(
                _to_view(sv["qa"], d), _to_view(sv["ka"], d), _to_view(sv["va"], d), _to_view(dya, d),
                _to_view(sv["lse_a"], d), _to_view(delta_a, d), bias, None, blk, f"attn_a{d}_bwd")
            dqa.append(_from_view(dq_, d))
            dka.append(_from_view(dk_, d))
            dva.append(_from_view(dv_, d))
            dtable_a = dtable_a + _table_grad(db_, bk, "table_grad_a")
        dqb, dkb, dvb, db_, dsink = _attn_bwd(sv["qb"], sv["kb"], sv["vb"], dyb, sv["lse_b"], delta_b, bias_b, sink,
                                              BLK_B, "attn_b_bwd")
        dtable_b = dtable_b + _table_grad(db_, buckets_b, "table_grad_b")
        gsmall["sink_b"][l] = dsink[:, 0]

        dproj, pqa, pka, pqb, pkb = _qknorm_bwd(sv["proj"], gqa, gka, gqb, gkb, dqa, dka, dva, dqb, dkb, dvb, dgab)
        gsmall["qnorm_a_g"][l] = pqa.reshape(N_HEADS, HEAD_DIM).sum(0)
        gsmall["knorm_a_g"][l] = pka.reshape(N_HEADS, HEAD_DIM).sum(0)
        gsmall["qnorm_b_g"][l] = pqb.reshape(N_HEADS, HEAD_DIM).sum(0)
        gsmall["knorm_b_g"][l] = pkb.reshape(N_KV_B, HEAD_DIM).sum(0)
        gbig["w_in"][l] = _mm(sv["h"], dproj, "tn", F32, "mm_d_in")
        dh = _mm(dproj, w["w_in"], "nt", F32, "mm_dh")
        dx, gsmall["norm_mix_g"][l] = _rms_bwd(sv["x0"], g_mix, dh, dx, "rms_mix_bwd")
        gsmall["norm_mix_g"][l] = gsmall["norm_mix_g"][l][0]
        gsmall["norm_ffn_g"][l] = gsmall["norm_ffn_g"][l][0]
        gsmall["norm_ple_g"][l] = gsmall["norm_ple_g"][l][0]

    gbig = {n: jnp.stack(v) for n, v in gbig.items()}
    gsmall = {n: jnp.stack(v) for n, v in gsmall.items()}
    gsmall["rel_table"] = jnp.concatenate([dtable_a, dtable_b], axis=0).T
    return loss, dx, gbig, gsmall


def _place():
    return lax.axis_index("x"), lax.axis_index("y"), lax.axis_index("c")


def _flip(v, bit):
    return 1 - v if bit else v


CHIP_RELATIONS = ((0, 1), (1, 0), (1, 1))
ANY = pl.BlockSpec(memory_space=pl.ANY)


def _allgather_flat(wflat):
    rows, cols = wflat.shape
    half = rows // 2

    def body(w_ref, out_ref, send_sems, recv_sems, local_sem):
        x, y, c = _place()

        def part(px, py, pc):
            return out_ref.at[2 * px + py, pl.ds(pc * half, half), :]

        def copy(k, block, to, src=None):
            return pltpu.make_async_remote_copy(
                src_ref=part(*block) if src is None else src, dst_ref=part(*block),
                send_sem=send_sems.at[k], recv_sem=recv_sems.at[k], device_id=to, device_id_type=MESH_ID)

        mine = pltpu.make_async_copy(w_ref, out_ref.at[2 * x + y], local_sem)
        mine.start()
        chips = [(_flip(x, a), _flip(y, b)) for a, b in CHIP_RELATIONS]
        first = [copy(k, (x, y, c), (*chip, c), src=w_ref.at[pl.ds(c * half, half), :]) for k, chip in enumerate(chips)]
        for cp in first:
            cp.start()
        passed = [copy(3 + k, (*chip, c), (x, y, 1 - c)) for k, chip in enumerate(chips)]
        for k, chip in enumerate(chips):
            copy(k, (*chip, c), (x, y, c)).wait_recv()
            passed[k].start()
        for k, chip in enumerate(chips):
            copy(3 + k, (*chip, 1 - c), (x, y, c)).wait_recv()
        for cp in first + passed:
            cp.wait_send()
        mine.wait()

    return pl.pallas_call(
        body, out_shape=jax.ShapeDtypeStruct((N_CHIPS, rows, cols), wflat.dtype), in_specs=[ANY], out_specs=ANY,
        scratch_shapes=[pltpu.SemaphoreType.DMA((6,)), pltpu.SemaphoreType.DMA((6,)), pltpu.SemaphoreType.DMA],
        name="allgather_weights")(wflat)


def _sibling_halves(gsend):
    n, rows, cols = gsend.shape
    half = rows // 2

    def body(g_ref, out_ref, send_sem, recv_sem):
        x, y, c = _place()
        cp = pltpu.make_async_remote_copy(
            src_ref=g_ref.at[:, pl.ds((1 - c) * half, half), :], dst_ref=out_ref, send_sem=send_sem, recv_sem=recv_sem,
            device_id=(x, y, 1 - c), device_id_type=MESH_ID)
        cp.start()
        cp.wait_recv()
        cp.wait_send()

    return pl.pallas_call(
        body, out_shape=jax.ShapeDtypeStruct((n, half, cols), gsend.dtype), in_specs=[ANY], out_specs=ANY,
        scratch_shapes=[pltpu.SemaphoreType.DMA, pltpu.SemaphoreType.DMA], name="rs_sibling_halves")(gsend)


def _chip_sums(gsend, sib, place):
    n, rows, cols = gsend.shape
    half = rows // 2
    tm = 512
    nblk = half // tm

    def body(s_ref, g_ref, sib_ref, o_ref):
        o_ref[0] = (g_ref[0].astype(F32) + sib_ref[0].astype(F32)).astype(o_ref.dtype)

    grid_spec = pltpu.PrefetchScalarGridSpec(
        num_scalar_prefetch=1, grid=(n, nblk),
        in_specs=[pl.BlockSpec((1, tm, cols), lambda k, i, s: (jnp.bitwise_xor(s[0], k), s[1] * nblk + i, 0)),
                  pl.BlockSpec((1, tm, cols), lambda k, i, s: (jnp.bitwise_xor(s[0], k), i, 0))],
        out_specs=pl.BlockSpec((1, tm, cols), lambda k, i, s: (k, i, 0)))
    return pl.pallas_call(
        body, out_shape=jax.ShapeDtypeStruct((n, half, cols), BF16), grid_spec=grid_spec,
        name="rs_chip_sums", compiler_params=_params("parallel", "parallel"))(place, gsend, sib)


def _exchange_chip_sums(tsend):
    n, half, cols = tsend.shape

    def body(t_ref, out_ref, send_sems, recv_sems):
        x, y, c = _place()
        cps = []
        for k, (a, b) in enumerate(CHIP_RELATIONS):
            cps.append(pltpu.make_async_remote_copy(
                src_ref=t_ref.at[k + 1], dst_ref=out_ref.at[k], send_sem=send_sems.at[k], recv_sem=recv_sems.at[k],
                device_id=(_flip(x, a), _flip(y, b), c), device_id_type=MESH_ID))
        for cp in cps:
            cp.start()
        for cp in cps:
            cp.wait_recv()
        for cp in cps:
            cp.wait_send()

    return pl.pallas_call(
        body, out_shape=jax.ShapeDtypeStruct((n - 1, half, cols), tsend.dtype), in_specs=[ANY], out_specs=ANY,
        scratch_shapes=[pltpu.SemaphoreType.DMA((3,)), pltpu.SemaphoreType.DMA((3,))], name="rs_exchange")(tsend)


def _final_sum(tsend, recv):
    n, half, cols = tsend.shape
    tm = 512

    def body(t_ref, r_ref, o_ref):
        o_ref[...] = ((t_ref[0].astype(F32) + r_ref[0].astype(F32)) + r_ref[1].astype(F32)) + r_ref[2].astype(F32)

    return pl.pallas_call(
        body, out_shape=jax.ShapeDtypeStruct((half, cols), F32), grid=(half // tm,),
        in_specs=[pl.BlockSpec((1, tm, cols), lambda i: (0, i, 0)), pl.BlockSpec((n - 1, tm, cols), lambda i: (0, i, 0))],
        out_specs=pl.BlockSpec((tm, cols), lambda i: (i, 0)), name="rs_final_sum",
        compiler_params=_params("parallel"))(tsend, recv)


def _join_halves(ghalf):
    half, cols = ghalf.shape

    def body(g_ref, out_ref, send_sem, recv_sem, local_sem):
        x, y, c = _place()
        mine = pltpu.make_async_copy(g_ref, out_ref.at[pl.ds(c * half, half), :], local_sem)
        mine.start()
        cp = pltpu.make_async_remote_copy(
            src_ref=g_ref, dst_ref=out_ref.at[pl.ds(c * half, half), :], send_sem=send_sem, recv_sem=recv_sem,
            device_id=(x, y, 1 - c), device_id_type=MESH_ID)
        cp.start()
        pltpu.make_async_remote_copy(
            src_ref=g_ref, dst_ref=out_ref.at[pl.ds((1 - c) * half, half), :], send_sem=send_sem, recv_sem=recv_sem,
            device_id=(x, y, 1 - c), device_id_type=MESH_ID).wait_recv()
        cp.wait_send()
        mine.wait()

    return pl.pallas_call(
        body, out_shape=jax.ShapeDtypeStruct((2 * half, cols), ghalf.dtype), in_specs=[ANY], out_specs=ANY,
        scratch_shapes=[pltpu.SemaphoreType.DMA, pltpu.SemaphoreType.DMA, pltpu.SemaphoreType.DMA],
        name="rs_join_halves")(ghalf)


def _allreduce_small(v):
    rows, cols = v.shape

    def body(v_ref, out_ref, buf, send_sems, recv_sems):
        x, y, c = _place()
        cps = []
        for k in range(1, 8):
            peer = (_flip(x, (k >> 2) & 1), _flip(y, (k >> 1) & 1), _flip(c, k & 1))
            cps.append(pltpu.make_async_remote_copy(
                src_ref=v_ref, dst_ref=buf.at[k - 1], send_sem=send_sems.at[k - 1], recv_sem=recv_sems.at[k - 1],
                device_id=peer, device_id_type=MESH_ID))
        for cp in cps:
            cp.start()
        for cp in cps:
            cp.wait_recv()
        for cp in cps:
            cp.wait_send()
        t0 = v_ref[...] + buf[0]
        t1 = buf[1] + buf[2]
        t2 = buf[3] + buf[4]
        t3 = buf[5] + buf[6]
        out_ref[...] = (t0 + t1) + (t2 + t3)

    vm = pl.BlockSpec(memory_space=pltpu.VMEM)
    return pl.pallas_call(
        body, out_shape=jax.ShapeDtypeStruct((rows, cols), F32), in_specs=[vm], out_specs=vm,
        scratch_shapes=[pltpu.VMEM((7, rows, cols), F32), pltpu.SemaphoreType.DMA((7,)), pltpu.SemaphoreType.DMA((7,))],
        name="allreduce_small")(v)


def _pad_rows(a, rows):
    return jnp.concatenate([a, jnp.zeros((rows - a.shape[0], a.shape[1]), a.dtype)], axis=0)


def _pack_shards(shards, dtype):
    flat = jnp.concatenate([shards[n].astype(dtype).reshape(-1, FLAT_COLS) for n, _, _ in BIG], axis=0)
    return _pad_rows(flat, FLAT_ROWS)


def _unpack_shards(flat, dtype):
    out, off = {}, 0
    for n, (k, m), ax in BIG:
        shape = (DEPTH, k // N_CHIPS, m) if ax == 0 else (DEPTH, k, m // N_CHIPS)
        rows = math.prod(shape) // FLAT_COLS
        out[n] = flat[off:off + rows].reshape(shape).astype(dtype)
        off += rows
    return out


def _unpack_full(wall):
    per_chip = [_unpack_shards(wall[j], wall.dtype) for j in range(N_CHIPS)]
    return {n: jnp.concatenate([pc[n] for pc in per_chip], axis=1 + ax) for n, _, ax in BIG}


def _pack_grads(gbig):
    chunks = []
    for j in range(N_CHIPS):
        shards = {}
        for n, (k, m), ax in BIG:
            size = (k if ax == 0 else m) // N_CHIPS
            shards[n] = lax.slice_in_dim(gbig[n], j * size, (j + 1) * size, axis=1 + ax)
        chunks.append(_pack_shards(shards, BF16))
    return jnp.stack(chunks)


SMALL_SHAPES = {"rel_table": (NUM_BUCKETS, 2 * N_HEADS), "norm_mix_g": (DEPTH, D_MODEL), "qnorm_a_g": (DEPTH, HEAD_DIM),
                "knorm_a_g": (DEPTH, HEAD_DIM), "qnorm_b_g": (DEPTH, HEAD_DIM), "knorm_b_g": (DEPTH, HEAD_DIM),
                "sink_b": (DEPTH, N_HEADS), "norm_ffn_g": (DEPTH, D_MODEL), "norm_ple_g": (DEPTH, D_MODEL)}


def _pack_small(vals):
    flat = jnp.concatenate([vals[n].astype(F32).reshape(-1) for n in SMALL])
    flat = jnp.concatenate([flat, jnp.zeros((SMALL_ROWS * LANES - flat.shape[0],), F32)])
    return flat.reshape(SMALL_ROWS, LANES)


def _unpack_small(packed):
    flat, out, off = packed.reshape(-1), {}, 0
    for n in SMALL:
        size = math.prod(SMALL_SHAPES[n])
        out[n] = flat[off:off + size].reshape(SMALL_SHAPES[n])
        off += size
    return out


def _adamw(w, g, m, v, name):
    c1 = 1.0 - ADAM_B1 ** ADAM_STEP
    c2 = 1.0 - ADAM_B2 ** ADAM_STEP

    def fn(w_, g_, m_, v_):
        m_new = ADAM_B1 * m_ + (1.0 - ADAM_B1) * g_
        v_new = ADAM_B2 * v_ + (1.0 - ADAM_B2) * (g_ * g_)
        delta = -ADAM_LR * ((m_new / c1) / (jnp.sqrt(v_new / c2) + ADAM_EPS) + ADAM_WD * w_)
        return delta, m_new, v_new

    width = w.shape[1]
    return _ew(fn, [(w, 0), (g, 0), (m, 0), (v, 0)], [F32, F32, F32], width=width, bw=width, name=name)


def kernel(x, p, rel_table, norm_mix_g, w_in, qnorm_a_g, knorm_a_g, qnorm_b_g, knorm_b_g, sink_b, w_branch_a, w_branch_b, w_out, norm_ffn_g, w_ffn_gate, w_ffn_up, w_ffn_down, norm_ple_g, w_ple_gate, w_ple_proj, loss_target, m_rel_table, m_norm_mix_g, m_w_in, m_qnorm_a_g, m_knorm_a_g, m_qnorm_b_g, m_knorm_b_g, m_sink_b, m_w_branch_a, m_w_branch_b, m_w_out, m_norm_ffn_g, m_w_ffn_gate, m_w_ffn_up, m_w_ffn_down, m_norm_ple_g, m_w_ple_gate, m_w_ple_proj, v_rel_table, v_norm_mix_g, v_w_in, v_qnorm_a_g, v_knorm_a_g, v_qnorm_b_g, v_knorm_b_g, v_sink_b, v_w_branch_a, v_w_branch_b, v_w_out, v_norm_ffn_g, v_w_ffn_gate, v_w_ffn_up, v_w_ffn_down, v_norm_ple_g, v_w_ple_gate, v_w_ple_proj):
    given = dict(locals())
    weights = {n: given[n] for n in WEIGHTS}
    moments_m = {n: given["m_" + n] for n in WEIGHTS}
    moments_v = {n: given["v_" + n] for n in WEIGHTS}
    xi, yi, ci = _place()
    place = jnp.stack([2 * xi + yi, ci]).astype(jnp.int32)

    wall = _allgather_flat(_pack_shards(weights, BF16))
    big = _unpack_full(wall)
    small = {n: weights[n] for n in SMALL}

    loss, dx, gbig, gsmall = _local_step(x[0], p[:, 0], loss_target[0], big, small)

    gsend = _pack_grads(gbig)
    tsend = _chip_sums(gsend, _sibling_halves(gsend), place)
    gflat = _join_halves(_final_sum(tsend, _exchange_chip_sums(tsend)))
    grads = _unpack_shards(gflat, F32)
    grads.update(_unpack_small(_allreduce_small(_pack_small(gsmall))))

    delta, new_m, new_v = {}, {}, {}
    for n, _, _ in BIG:
        shape = weights[n].shape
        two_d = lambda a: a.reshape(shape[0] * shape[1], shape[2])
        d_, m_, v_ = _adamw(two_d(weights[n]), two_d(grads[n]), two_d(moments_m[n]), two_d(moments_v[n]), "adamw_" + n)
        delta[n], new_m[n], new_v[n] = d_.reshape(shape), m_.reshape(shape), v_.reshape(shape)
    d_, m_, v_ = _adamw(_pack_small(weights), _pack_small(grads), _pack_small(moments_m), _pack_small(moments_v),
                        "adamw_small")
    delta.update(_unpack_small(d_))
    new_m.update(_unpack_small(m_))
    new_v.update(_unpack_small(v_))

    loss = lax.psum(loss, ("x", "y", "c"))
    return (loss, dx[None], *[grads[n] for n in WEIGHTS], *[delta[n] for n in WEIGHTS],
            *[new_m[n] for n in WEIGHTS], *[new_v[n] for n in WEIGHTS])
```

```python
import functools
import math

import jax
import jax.numpy as jnp
from jax import lax
from jax.experimental import pallas as pl
from jax.experimental.pallas import tpu as pltpu

F32 = jnp.float32
BF16 = jnp.bfloat16
MESH_ID = pl.DeviceIdType.MESH

SEQ = 2048
D_MODEL = 1024
DEPTH = 2
HEAD_DIM = 64
N_HEADS = 8
WIDTH = N_HEADS * HEAD_DIM
N_KV_B = 2
PLE_DIM = 256
D_FF = 2816
D_IN = 4352
OFF_QA, OFF_KA, OFF_VA, OFF_QB, OFF_KB, OFF_VB, OFF_GA, OFF_GB = 0, 512, 1024, 1536, 2048, 2176, 2304, 3328
DILATED = ((64, 1), (64, 4), (64, 16))
BLK_B = 128
NUM_BUCKETS = 32
MAX_DISTANCE = 1024
RMS_EPS = 1e-6
NEG_INF = -1e30
LANES = 128
VMEM_LIMIT = 48 * 1024 * 1024

ADAM_LR, ADAM_B1, ADAM_B2, ADAM_EPS, ADAM_WD, ADAM_STEP = 0.001, 0.9, 0.999, 1e-08, 0.01, 10

BIG = (
    ("w_in", (D_MODEL, D_IN), 1),
    ("w_branch_a", (WIDTH, D_MODEL), 1),
    ("w_branch_b", (WIDTH, D_MODEL), 1),
    ("w_out", (D_MODEL, D_MODEL), 0),
    ("w_ffn_gate", (D_MODEL, D_FF), 1),
    ("w_ffn_up", (D_MODEL, D_FF), 1),
    ("w_ffn_down", (D_FF, D_MODEL), 0),
    ("w_ple_gate", (D_MODEL, D_MODEL), 0),
    ("w_ple_proj", (PLE_DIM, D_MODEL), 1),
)
SMALL = ("rel_table", "norm_mix_g", "qnorm_a_g", "knorm_a_g", "qnorm_b_g", "knorm_b_g", "sink_b",
         "norm_ffn_g", "norm_ple_g")
WEIGHTS = ("rel_table", "norm_mix_g", "w_in", "qnorm_a_g", "knorm_a_g", "qnorm_b_g", "knorm_b_g", "sink_b",
           "w_branch_a", "w_branch_b", "w_out", "norm_ffn_g", "w_ffn_gate", "w_ffn_up", "w_ffn_down",
           "norm_ple_g", "w_ple_gate", "w_ple_proj")
N_CHIPS = 4
FLAT_COLS = 1024
FLAT_ROWS = 8192
HALF_ROWS = FLAT_ROWS // 2
SMALL_ROWS = 64


def _params(*sem):
    return pltpu.CompilerParams(dimension_semantics=sem, vmem_limit_bytes=VMEM_LIMIT)


def _pick(dim, target):
    for t in (target, 512, 256, 128, 64, 32, 16, 8):
        if t <= target and dim % t == 0:
            return t
    return dim


def _mm(a, b, mode, out_dtype, name, res=None, tm=512, tn=512):
    if mode == "nn":
        (m, k), (_, n) = a.shape, b.shape
    elif mode == "nt":
        (m, k), (n, _) = a.shape, b.shape
    else:
        (k, m), (_, n) = a.shape, b.shape
    tm, tn = _pick(m, tm), _pick(n, tn)
    a_spec = pl.BlockSpec((k, tm), lambda i, j: (0, i)) if mode == "tn" else pl.BlockSpec((tm, k), lambda i, j: (i, 0))
    b_spec = pl.BlockSpec((tn, k), lambda i, j: (j, 0)) if mode == "nt" else pl.BlockSpec((k, tn), lambda i, j: (0, j))
    dims = {"nn": (((1,), (0,)), ((), ())), "nt": (((1,), (1,)), ((), ())), "tn": (((0,), (0,)), ((), ()))}[mode]
    has_res = res is not None

    def body(*refs):
        a_ref, b_ref = refs[0], refs[1]
        o_ref = refs[-1]
        acc = lax.dot_general(a_ref[...].astype(BF16), b_ref[...].astype(BF16), dims, preferred_element_type=F32)
        if has_res:
            acc = acc + refs[2][...]
        o_ref[...] = acc.astype(out_dtype)

    in_specs = [a_spec, b_spec]
    args = [a, b]
    if has_res:
        in_specs.append(pl.BlockSpec((tm, tn), lambda i, j: (i, j)))
        args.append(res)
    return pl.pallas_call(
        body, out_shape=jax.ShapeDtypeStruct((m, n), out_dtype), grid=(m // tm, n // tn),
        in_specs=in_specs, out_specs=pl.BlockSpec((tm, tn), lambda i, j: (i, j)),
        name=name, compiler_params=_params("parallel", "parallel"))(*args)


def _ew(fn, ins, out_dtypes, *, width, bw, name, vecs=(), tm=256):
    rows = ins[0][0].shape[0]
    tm = _pick(rows, tm)
    n_in = len(ins) + len(vecs)

    def col_map(off_blocks):
        return lambda i, j: (i, off_blocks + j)

    in_specs = [pl.BlockSpec((tm, bw), col_map(off // bw)) for _, off in ins]
    in_specs += [pl.BlockSpec((1, bw), lambda i, j: (0, j)) for _ in vecs]

    def body(*refs):
        outs = fn(*[r[...] for r in refs[:n_in]])
        for r, o in zip(refs[n_in:], outs):
            r[...] = o.astype(r.dtype)

    return pl.pallas_call(
        body, out_shape=[jax.ShapeDtypeStruct((rows, width), dt) for dt in out_dtypes],
        grid=(rows // tm, width // bw), in_specs=in_specs,
        out_specs=[pl.BlockSpec((tm, bw), lambda i, j: (i, j)) for _ in out_dtypes],
        name=name, compiler_params=_params("parallel", "parallel"))(*[a for a, _ in ins], *vecs)


def _sigmoid(x):
    return 1.0 / (1.0 + jnp.exp(-x))


def _seg_sum(v):
    outs = []
    for k in range(v.shape[1] // LANES):
        vp = v[:, k * LANES:(k + 1) * LANES]
        left = lax.broadcasted_iota(jnp.int32, vp.shape, 1) < HEAD_DIM
        sl = jnp.sum(jnp.where(left, vp, 0.0), axis=-1, keepdims=True)
        sr = jnp.sum(jnp.where(left, 0.0, vp), axis=-1, keepdims=True)
        outs.append(jnp.where(left, sl, sr))
    return outs[0] if len(outs) == 1 else jnp.concatenate(outs, axis=1)


def _seg_rstd(x):
    return lax.rsqrt(_seg_sum(x * x) * (1.0 / HEAD_DIM) + RMS_EPS)


def _rms_fwd(x, g, name):
    rows, d = x.shape
    tm = 256

    def body(x_ref, g_ref, h_ref):
        xv = x_ref[...]
        r = lax.rsqrt(jnp.mean(xv * xv, axis=-1, keepdims=True) + RMS_EPS)
        h_ref[...] = ((xv * r) * g_ref[...]).astype(BF16)

    return pl.pallas_call(
        body, out_shape=jax.ShapeDtypeStruct((rows, d), BF16), grid=(rows // tm,),
        in_specs=[pl.BlockSpec((tm, d), lambda i: (i, 0)), pl.BlockSpec((1, d), lambda i: (0, 0))],
        out_specs=pl.BlockSpec((tm, d), lambda i: (i, 0)), name=name, compiler_params=_params("parallel"))(x, g)


def _rms_bwd(x, g, dh, dres, name):
    rows, d = x.shape
    tm = 256

    def body(x_ref, g_ref, dh_ref, dres_ref, dx_ref, dg_ref):
        xv = x_ref[...]
        r = lax.rsqrt(jnp.mean(xv * xv, axis=-1, keepdims=True) + RMS_EPS)
        xh = xv * r
        dhv = dh_ref[...]
        dxh = dhv * g_ref[...]
        dx_ref[...] = dres_ref[...] + r * (dxh - xh * jnp.mean(dxh * xh, axis=-1, keepdims=True))
        part = jnp.sum(dhv * xh, axis=0, keepdims=True)

        @pl.when(pl.program_id(0) == 0)
        def _():
            dg_ref[...] = part

        @pl.when(pl.program_id(0) > 0)
        def _():
            dg_ref[...] += part

    row = pl.BlockSpec((tm, d), lambda i: (i, 0))
    vec = pl.BlockSpec((1, d), lambda i: (0, 0))
    return pl.pallas_call(
        body, out_shape=[jax.ShapeDtypeStruct((rows, d), F32), jax.ShapeDtypeStruct((1, d), F32)],
        grid=(rows // tm,), in_specs=[row, vec, row, row], out_specs=[row, vec],
        name=name, compiler_params=_params("arbitrary"))(x, g, dh, dres)


def _loss_grad(y, t):
    rows, d = y.shape
    tm = 256

    def body(y_ref, t_ref, dy_ref, l_ref):
        e = y_ref[...] - t_ref[...]
        dy_ref[...] = e * (1.0 / d)
        part = jnp.zeros((1, LANES), F32) + jnp.sum(e * e) * (0.5 / d)

        @pl.when(pl.program_id(0) == 0)
        def _():
            l_ref[...] = part

        @pl.when(pl.program_id(0) > 0)
        def _():
            l_ref[...] += part

    row = pl.BlockSpec((tm, d), lambda i: (i, 0))
    return pl.pallas_call(
        body, out_shape=[jax.ShapeDtypeStruct((rows, d), F32), jax.ShapeDtypeStruct((1, LANES), F32)],
        grid=(rows // tm,), in_specs=[row, row], out_specs=[row, pl.BlockSpec((1, LANES), lambda i: (0, 0))],
        name="loss_grad", compiler_params=_params("arbitrary"))(y, t)


def _swap_halves(v):
    return pltpu.roll(v, HEAD_DIM, axis=1)


def _expand_kv(kv):
    left = lax.broadcasted_iota(jnp.int32, kv.shape, 1) < HEAD_DIM
    sw = _swap_halves(kv)
    h0 = jnp.where(left, kv, sw)
    h1 = jnp.where(left, sw, kv)
    return jnp.concatenate([h0, h0, h1, h1], axis=1)


def _reduce_kv(dkv):
    left = lax.broadcasted_iota(jnp.int32, (dkv.shape[0], LANES), 1) < HEAD_DIM
    t = dkv[:, 0:LANES] + dkv[:, LANES:2 * LANES]
    u = dkv[:, 2 * LANES:3 * LANES] + dkv[:, 3 * LANES:4 * LANES]
    t = t + _swap_halves(t)
    u = u + _swap_halves(u)
    return jnp.where(left, t, u)


def _qknorm_fwd(proj, gqa, gka, gqb, gkb):
    rows = proj.shape[0]
    tm = 256

    def body(qa_ref, ka_ref, va_ref, qb_ref, kb_ref, vb_ref, gqa_ref, gka_ref, gqb_ref, gkb_ref,
             oqa, oka, ova, oqb, okb, ovb):
        for src, g_ref, dst in ((qa_ref, gqa_ref, oqa), (ka_ref, gka_ref, oka), (qb_ref, gqb_ref, oqb)):
            xv = src[...]
            dst[...] = ((xv * _seg_rstd(xv)) * g_ref[...]).astype(BF16)
        ova[...] = va_ref[...].astype(BF16)
        kv = kb_ref[...]
        okb[...] = _expand_kv((kv * _seg_rstd(kv)) * gkb_ref[...]).astype(BF16)
        ovb[...] = _expand_kv(vb_ref[...]).astype(BF16)

    def win(width, off):
        return pl.BlockSpec((tm, width), lambda i: (i, off // width))

    vec = lambda w: pl.BlockSpec((1, w), lambda i: (0, 0))
    out = pl.BlockSpec((tm, WIDTH), lambda i: (i, 0))
    return pl.pallas_call(
        body, out_shape=[jax.ShapeDtypeStruct((rows, WIDTH), BF16)] * 6, grid=(rows // tm,),
        in_specs=[win(WIDTH, OFF_QA), win(WIDTH, OFF_KA), win(WIDTH, OFF_VA), win(WIDTH, OFF_QB),
                  win(LANES, OFF_KB), win(LANES, OFF_VB), vec(WIDTH), vec(WIDTH), vec(WIDTH), vec(LANES)],
        out_specs=[out] * 6, name="qknorm_fwd", compiler_params=_params("parallel"))(
            proj, proj, proj, proj, proj, proj, gqa, gka, gqb, gkb)


def _norm_bwd(xv, g, dy):
    r = _seg_rstd(xv)
    xh = xv * r
    dxh = dy * g
    dx = r * (dxh - xh * (_seg_sum(dxh * xh) * (1.0 / HEAD_DIM)))
    return dx, jnp.sum(dy * xh, axis=0, keepdims=True)


def _qknorm_bwd(proj, gqa, gka, gqb, gkb, dqa, dka, dva, dqb, dkb, dvb, dgab):
    rows = proj.shape[0]
    tm = 256
    n_a = len(dqa)

    def body(*refs):
        qa_ref, ka_ref, qb_ref, kb_ref, gqa_ref, gka_ref, gqb_ref, gkb_ref = refs[:8]
        pos = 8
        dqa_refs, dka_refs, dva_refs = refs[pos:pos + n_a], refs[pos + n_a:pos + 2 * n_a], refs[pos + 2 * n_a:pos + 3 * n_a]
        pos += 3 * n_a
        dqb_ref, dkb_ref, dvb_ref, dgab_ref = refs[pos:pos + 4]
        dproj_ref, ogqa, ogka, ogqb, ogkb = refs[pos + 4:]

        def total(rs):
            acc = rs[0][...]
            for r in rs[1:]:
                acc = acc + r[...]
            return acc

        dx_qa, p_qa = _norm_bwd(qa_ref[...], gqa_ref[...], total(dqa_refs))
        dx_ka, p_ka = _norm_bwd(ka_ref[...], gka_ref[...], total(dka_refs))
        dx_qb, p_qb = _norm_bwd(qb_ref[...], gqb_ref[...], dqb_ref[...])
        dx_kb, p_kb = _norm_bwd(kb_ref[...], gkb_ref[...], _reduce_kv(dkb_ref[...]))
        dproj_ref[:, OFF_QA:OFF_QA + WIDTH] = dx_qa.astype(BF16)
        dproj_ref[:, OFF_KA:OFF_KA + WIDTH] = dx_ka.astype(BF16)
        dproj_ref[:, OFF_VA:OFF_VA + WIDTH] = total(dva_refs).astype(BF16)
        dproj_ref[:, OFF_QB:OFF_QB + WIDTH] = dx_qb.astype(BF16)
        dproj_ref[:, OFF_KB:OFF_KB + LANES] = dx_kb.astype(BF16)
        dproj_ref[:, OFF_VB:OFF_VB + LANES] = _reduce_kv(dvb_ref[...]).astype(BF16)
        dproj_ref[:, OFF_GA:D_IN] = dgab_ref[...]
        first = pl.program_id(0) == 0
        for o_ref, part in ((ogqa, p_qa), (ogka, p_ka), (ogqb, p_qb), (ogkb, p_kb)):
            @pl.when(first)
            def _(o_ref=o_ref, part=part):
                o_ref[...] = part

            @pl.when(jnp.logical_not(first))
            def _(o_ref=o_ref, part=part):
                o_ref[...] += part

    def win(width, off):
        return pl.BlockSpec((tm, width), lambda i: (i, off // width))

    vec = lambda w: pl.BlockSpec((1, w), lambda i: (0, 0))
    row = lambda w: pl.BlockSpec((tm, w), lambda i: (i, 0))
    in_specs = [win(WIDTH, OFF_QA), win(WIDTH, OFF_KA), win(WIDTH, OFF_QB), win(LANES, OFF_KB),
                vec(WIDTH), vec(WIDTH), vec(WIDTH), vec(LANES)]
    in_specs += [row(WIDTH)] * (3 * n_a + 3) + [row(2 * D_MODEL)]
    return pl.pallas_call(
        body,
        out_shape=[jax.ShapeDtypeStruct((rows, D_IN), BF16), jax.ShapeDtypeStruct((1, WIDTH), F32),
                   jax.ShapeDtypeStruct((1, WIDTH), F32), jax.ShapeDtypeStruct((1, WIDTH), F32),
                   jax.ShapeDtypeStruct((1, LANES), F32)],
        grid=(rows // tm,), in_specs=in_specs,
        out_specs=[row(D_IN), vec(WIDTH), vec(WIDTH), vec(WIDTH), vec(LANES)],
        name="qknorm_bwd", compiler_params=_params("arbitrary"))(
            proj, proj, proj, proj, gqa, gka, gqb, gkb, *dqa, *dka, *dva, dqb, dkb, dvb, dgab)


def _t5_bucket(rel):
    half_b = NUM_BUCKETS // 2
    max_exact = half_b // 2
    sign = jnp.where(rel > 0, half_b, 0)
    n = jnp.abs(rel)
    nf = jnp.maximum(n, 1).astype(F32)
    large = max_exact + (jnp.log(nf / max_exact) / math.log(MAX_DISTANCE / max_exact)
                         * (half_b - max_exact)).astype(jnp.int32)
    large = jnp.minimum(large, half_b - 1)
    return sign + jnp.where(n < max_exact, n, large)


def _band_buckets(blk, dilation):
    i = jnp.arange(blk, dtype=jnp.int32)[:, None]
    j = jnp.arange(3 * blk, dtype=jnp.int32)[None, :]
    rel = j - blk - i
    return jnp.where(jnp.abs(rel) <= blk, _t5_bucket(rel * dilation), -1)


def _bias_tiles(table, buckets, head_off, name):
    blk = buckets.shape[0]

    def body(tab_ref, bk_ref, o_ref):
        h = pl.program_id(0) + head_off
        bk = bk_ref[...]
        acc = jnp.full(bk.shape, NEG_INF, F32)
        for b in range(NUM_BUCKETS):
            acc = jnp.where(bk == b, tab_ref[b, h], acc)
        o_ref[0] = acc

    return pl.pallas_call(
        body, out_shape=jax.ShapeDtypeStruct((N_HEADS, blk, 3 * blk), F32), grid=(N_HEADS,),
        in_specs=[pl.BlockSpec(memory_space=pltpu.SMEM), pl.BlockSpec((blk, 3 * blk), lambda h: (0, 0))],
        out_specs=pl.BlockSpec((1, blk, 3 * blk), lambda h: (h, 0, 0)),
        name=name, compiler_params=_params("parallel"))(table, buckets)


def _table_grad(dbias, buckets, name):
    blk = buckets.shape[0]

    def body(db_ref, bk_ref, o_ref):
        bk = bk_ref[...]
        dbv = db_ref[0]
        lane = lax.broadcasted_iota(jnp.int32, (1, LANES), 1)
        acc = jnp.zeros((1, LANES), F32)
        for b in range(NUM_BUCKETS):
            acc = jnp.where(lane == b, jnp.sum(jnp.where(bk == b, dbv, 0.0)), acc)
        o_ref[0] = acc

    out = pl.pallas_call(
        body, out_shape=jax.ShapeDtypeStruct((N_HEADS, 1, LANES), F32), grid=(N_HEADS,),
        in_specs=[pl.BlockSpec((1, blk, 3 * blk), lambda h: (h, 0, 0)), pl.BlockSpec((blk, 3 * blk), lambda h: (0, 0))],
        out_specs=pl.BlockSpec((1, 1, LANES), lambda h: (h, 0, 0)),
        name=name, compiler_params=_params("parallel"))(dbias, buckets)
    return out[:, 0, :NUM_BUCKETS]


def _dot_nt(a, b):
    return lax.dot_general(a, b, (((1,), (1,)), ((), ())), preferred_element_type=F32)


def _edge_penalty(i, nb, blk):
    col = lax.broadcasted_iota(jnp.int32, (1, 3 * blk), 1)
    ok = jnp.logical_and(jnp.logical_or(col >= blk, i > 0), jnp.logical_or(col < 2 * blk, i < nb - 1))
    return jnp.where(ok, 0.0, NEG_INF).astype(F32)


def _stack_pair(x2, left):
    return jnp.concatenate([jnp.where(left, x2, 0.0), jnp.where(left, 0.0, x2)], axis=0).astype(BF16)


def _attn_fwd(q, k, v, bias, sink, blk, name):
    lv, rc = q.shape
    nres, nb = rc // WIDTH, lv // blk
    has_sink = sink is not None
    n_pairs = WIDTH // LANES

    def body(*refs):
        q_ref, kp, kc, kn, vp, vc, vn, b_ref = refs[:8]
        s_ref = refs[8] if has_sink else None
        o_ref, l_ref = refs[-2], refs[-1]
        i = pl.program_id(1)
        left = lax.broadcasted_iota(jnp.int32, (1, LANES), 1) < HEAD_DIM
        scores, vcats = [], []
        for hp in range(n_pairs):
            sl = slice(hp * LANES, (hp + 1) * LANES)
            qs = _stack_pair(q_ref[:, sl].astype(F32) * (HEAD_DIM ** -0.5), left)
            kcat = jnp.concatenate([kp[:, sl], kc[:, sl], kn[:, sl]], axis=0)
            vcats.append(jnp.concatenate([vp[:, sl], vc[:, sl], vn[:, sl]], axis=0))
            scores.append(_dot_nt(qs, kcat))
        s = jnp.concatenate(scores, axis=0) + b_ref[...] + _edge_penalty(i, nb, blk)
        m = jnp.max(s, axis=-1, keepdims=True)
        if has_sink:
            m = jnp.maximum(m, s_ref[...])
        p = jnp.exp(s - m)
        den = jnp.sum(p, axis=-1, keepdims=True)
        if has_sink:
            den = den + jnp.exp(s_ref[...] - m)
        pn = (p * (1.0 / den)).astype(BF16)
        lse = m + jnp.log(den)
        for hp in range(n_pairs):
            sl = slice(hp * LANES, (hp + 1) * LANES)
            top, mid, bot = 2 * hp * blk, (2 * hp + 1) * blk, (2 * hp + 2) * blk
            o2 = jnp.dot(pn[top:bot], vcats[hp], preferred_element_type=F32)
            o_ref[:, sl] = jnp.where(left, o2[:blk], o2[blk:])
            l_ref[:, sl] = jnp.where(left, lse[top:mid], lse[mid:bot])

    cur = pl.BlockSpec((blk, WIDTH), lambda r, i: (i, r))
    prev = pl.BlockSpec((blk, WIDTH), lambda r, i: (jnp.maximum(i - 1, 0), r))
    nxt = pl.BlockSpec((blk, WIDTH), lambda r, i: (jnp.minimum(i + 1, nb - 1), r))
    in_specs = [cur, prev, cur, nxt, prev, cur, nxt, pl.BlockSpec((N_HEADS * blk, 3 * blk), lambda r, i: (0, 0))]
    args = [q, k, k, k, v, v, v, bias]
    if has_sink:
        in_specs.append(pl.BlockSpec((N_HEADS * blk, 1), lambda r, i: (0, 0)))
        args.append(sink)
    return pl.pallas_call(
        body, out_shape=[jax.ShapeDtypeStruct((lv, rc), F32)] * 2, grid=(nres, nb),
        in_specs=in_specs, out_specs=[cur, cur], name=name, compiler_params=_params("parallel", "parallel"))(*args)


def _attn_bwd(q, k, v, do, lse, delta, bias, sink, blk, name):
    lv, rc = q.shape
    nres, nb = rc // WIDTH, lv // blk
    has_sink = sink is not None
    n_in = 12 if has_sink else 11
    n_pairs = WIDTH // LANES

    def body(*refs):
        q_ref, kp, kc, kn, vp, vc, vn, do_ref, l_ref, d_ref, b_ref = refs[:11]
        s_ref = refs[11] if has_sink else None
        dq_ref, dk_ref, dv_ref, db_ref = refs[n_in:n_in + 4]
        ds_ref = refs[n_in + 4] if has_sink else None
        acck, accv = refs[-2], refs[-1]
        r, i = pl.program_id(0), pl.program_id(1)

        @pl.when(jnp.logical_and(r == 0, i == 0))
        def _():
            db_ref[...] = jnp.zeros_like(db_ref)
            if has_sink:
                ds_ref[...] = jnp.zeros_like(ds_ref)

        @pl.when(i == 0)
        def _():
            acck[...] = jnp.zeros_like(acck)
            accv[...] = jnp.zeros_like(accv)

        @pl.when(i < nb)
        def _():
            lane = lax.broadcasted_iota(jnp.int32, (1, LANES), 1)
            left = lane < HEAD_DIM
            qss, doss, kcats, scores, dps, lcols, dcols = [], [], [], [], [], [], []
            for hp in range(n_pairs):
                sl = slice(hp * LANES, (hp + 1) * LANES)
                qs = _stack_pair(q_ref[:, sl].astype(F32) * (HEAD_DIM ** -0.5), left)
                dos = _stack_pair(do_ref[:, sl], left)
                kcat = jnp.concatenate([kp[:, sl], kc[:, sl], kn[:, sl]], axis=0)
                vcat = jnp.concatenate([vp[:, sl], vc[:, sl], vn[:, sl]], axis=0)
                l2, d2 = l_ref[:, sl], d_ref[:, sl]
                lcols.append(jnp.max(jnp.where(left, l2, NEG_INF), axis=-1, keepdims=True))
                lcols.append(jnp.max(jnp.where(left, NEG_INF, l2), axis=-1, keepdims=True))
                dcols.append(jnp.sum(jnp.where(lane == 0, d2, 0.0), axis=-1, keepdims=True))
                dcols.append(jnp.sum(jnp.where(lane == HEAD_DIM, d2, 0.0), axis=-1, keepdims=True))
                scores.append(_dot_nt(qs, kcat))
                dps.append(_dot_nt(dos, vcat))
                qss.append(qs)
                doss.append(dos)
                kcats.append(kcat)
            lcol = jnp.concatenate(lcols, axis=0)
            dcol = jnp.concatenate(dcols, axis=0)
            p = jnp.exp(jnp.concatenate(scores, axis=0) + b_ref[...] + _edge_penalty(i, nb, blk) - lcol)
            ds = p * (jnp.concatenate(dps, axis=0) - dcol)
            db_ref[...] += ds
            if has_sink:
                ds_ref[...] -= dcol * jnp.exp(s_ref[...] - lcol)
            for hp in range(n_pairs):
                sl = slice(hp * LANES, (hp + 1) * LANES)
                top, bot = 2 * hp * blk, (2 * hp + 2) * blk
                dsp, pp = ds[top:bot], p[top:bot]
                dq2 = jnp.dot(dsp.astype(BF16), kcats[hp], preferred_element_type=F32) * (HEAD_DIM ** -0.5)
                dq_ref[:, sl] = jnp.where(left, dq2[:blk], dq2[blk:])
                dk_new = jnp.dot(jnp.transpose(dsp).astype(BF16), qss[hp], preferred_element_type=F32)
                dv_new = jnp.dot(jnp.transpose(pp).astype(BF16), doss[hp], preferred_element_type=F32)
                for out_ref, acc, new in ((dk_ref, acck, dk_new), (dv_ref, accv, dv_new)):
                    out_ref[:, sl] = acc[0, :, sl] + new[0:blk]
                    acc[0, :, sl] = acc[1, :, sl] + new[blk:2 * blk]
                    acc[1, :, sl] = new[2 * blk:3 * blk]

        @pl.when(i == nb)
        def _():
            dk_ref[...] = acck[0]
            dv_ref[...] = accv[0]

    last = nb - 1
    cur = pl.BlockSpec((blk, WIDTH), lambda r, i: (jnp.minimum(i, last), r))
    prev = pl.BlockSpec((blk, WIDTH), lambda r, i: (jnp.clip(i - 1, 0, last), r))
    nxt = pl.BlockSpec((blk, WIDTH), lambda r, i: (jnp.minimum(i + 1, last), r))
    lag = pl.BlockSpec((blk, WIDTH), lambda r, i: (jnp.maximum(i - 1, 0), r))
    band = pl.BlockSpec((N_HEADS * blk, 3 * blk), lambda r, i: (0, 0))
    col = pl.BlockSpec((N_HEADS * blk, 1), lambda r, i: (0, 0))
    in_specs = [cur, prev, cur, nxt, prev, cur, nxt, cur, cur, cur, band]
    args = [q, k, k, k, v, v, v, do, lse, delta, bias]
    out_shape = [jax.ShapeDtypeStruct((lv, rc), F32)] * 3 + [jax.ShapeDtypeStruct((N_HEADS * blk, 3 * blk), F32)]
    out_specs = [cur, lag, lag, band]
    if has_sink:
        in_specs.append(col)
        args.append(sink)
        out_shape.append(jax.ShapeDtypeStruct((N_HEADS * blk, 1), F32))
        out_specs.append(col)
    return pl.pallas_call(
        body, out_shape=out_shape, grid=(nres, nb + 1), in_specs=in_specs, out_specs=out_specs,
        scratch_shapes=[pltpu.VMEM((2, blk, WIDTH), F32), pltpu.VMEM((2, blk, WIDTH), F32)],
        name=name, compiler_params=_params("arbitrary", "arbitrary"))(*args)


def _to_view(a, d):
    return a if d == 1 else a.reshape(a.shape[0] // d, d * a.shape[1])


def _from_view(a, d):
    return a if d == 1 else a.reshape(a.shape[0] * d, a.shape[1] // d)


def _tile_gain(g, reps):
    return jnp.tile(g[None, :], (1, reps))


def _local_step(x, p, target, big, small):
    rel_table = small["rel_table"]
    buckets_a = [_band_buckets(blk, d) for blk, d in DILATED]
    buckets_b = _band_buckets(BLK_B, 1)
    bias_a = [_bias_tiles(rel_table, bk, 0, "bias_a").reshape(N_HEADS * bk.shape[0], -1) for bk in buckets_a]
    bias_b = _bias_tiles(rel_table, buckets_b, N_HEADS, "bias_b").reshape(N_HEADS * BLK_B, -1)

    saved = []
    for l in range(DEPTH):
        w = {n: big[n][l] for n, _, _ in BIG}
        g_mix, g_ffn, g_ple = (small[n][l][None, :] for n in ("norm_mix_g", "norm_ffn_g", "norm_ple_g"))
        gqa, gka, gqb = (_tile_gain(small[n][l], N_HEADS) for n in ("qnorm_a_g", "knorm_a_g", "qnorm_b_g"))
        gkb = _tile_gain(small["knorm_b_g"][l], N_KV_B)
        sink = jnp.repeat(small["sink_b"][l], BLK_B)[:, None]

        h = _rms_fwd(x, g_mix, "rms_mix")
        proj = _mm(h, w["w_in"], "nn", F32, "mm_in")
        qa, ka, va, qb, kb, vb = _qknorm_fwd(proj, gqa, gka, gqb, gkb)
        outs, lses = [], []
        for (blk, d), bias in zip(DILATED, bias_a):
            o, ls = _attn_fwd(_to_view(qa, d), _to_view(ka, d), _to_view(va, d), bias, None, blk, f"attn_a{d}_fwd")
            outs.append(_from_view(o, d))
            lses.append(_from_view(ls, d))

        def combine(o1, o2, o3, l1, l2, l3):
            m = jnp.maximum(jnp.maximum(l1, l2), l3)
            e1, e2, e3 = jnp.exp(l1 - m), jnp.exp(l2 - m), jnp.exp(l3 - m)
            den = e1 + e2 + e3
            return (e1 * o1 + e2 * o2 + e3 * o3) / den, m + jnp.log(den)

        ya, lse_a = _ew(combine, [(t, 0) for t in outs + lses], [F32, F32], width=WIDTH, bw=WIDTH, name="combine_a")
        yb, lse_b = _attn_fwd(qb, kb, vb, bias_b, sink, BLK_B, "attn_b_fwd")
        ca = _mm(ya, w["w_branch_a"], "nn", F32, "mm_branch_a")
        cb = _mm(yb, w["w_branch_b"], "nn", F32, "mm_branch_b")

        def gate(ca_, cb_, ga_, gb_):
            return (_sigmoid(ga_) * ca_ + _sigmoid(gb_) * cb_,)

        (merged,) = _ew(gate, [(ca, 0), (cb, 0), (proj, OFF_GA), (proj, OFF_GB)], [BF16],
                        width=D_MODEL, bw=256, name="gate")
        x1 = _mm(merged, w["w_out"], "nn", F32, "mm_out", res=x)

        h2 = _rms_fwd(x1, g_ffn, "rms_ffn")
        a = _mm(h2, w["w_ffn_gate"], "nn", F32, "mm_ffn_gate")
        u = _mm(h2, w["w_ffn_up"], "nn", F32, "mm_ffn_up")

        def swiglu(a_, u_):
            return ((a_ * _sigmoid(a_)) * u_,)

        (hid,) = _ew(swiglu, [(a, 0), (u, 0)], [BF16], width=D_FF, bw=D_FF, name="swiglu")
        x2 = _mm(hid, w["w_ffn_down"], "nn", F32, "mm_ffn_down", res=x1)

        h3 = _rms_fwd(x2, g_ple, "rms_ple")
        z = _mm(h3, w["w_ple_gate"], "nn", F32, "mm_ple_gate")
        e = _mm(p[l], w["w_ple_proj"], "nn", F32, "mm_ple_proj")

        def ple(x2_, z_, e_):
            return (x2_ + _sigmoid(z_) * e_,)

        (x3,) = _ew(ple, [(x2, 0), (z, 0), (e, 0)], [F32], width=D_MODEL, bw=D_MODEL, name="ple")
        saved.append(dict(x0=x, h=h, proj=proj, qa=qa, ka=ka, va=va, qb=qb, kb=kb, vb=vb, ya=ya, lse_a=lse_a,
                          yb=yb, lse_b=lse_b, ca=ca, cb=cb, merged=merged, x1=x1, h2=h2, a=a, u=u, hid=hid,
                          x2=x2, h3=h3, z=z, e=e))
        x = x3

    dx, loss_acc = _loss_grad(x, target)
    loss = loss_acc[0, 0]

    gbig = {n: [None] * DEPTH for n, _, _ in BIG}
    gsmall = {n: [None] * DEPTH for n in SMALL if n != "rel_table"}
    dtable_a = jnp.zeros((N_HEADS, NUM_BUCKETS), F32)
    dtable_b = jnp.zeros((N_HEADS, NUM_BUCKETS), F32)

    for l in reversed(range(DEPTH)):
        sv = saved[l]
        w = {n: big[n][l] for n, _, _ in BIG}
        g_mix, g_ffn, g_ple = (small[n][l][None, :] for n in ("norm_mix_g", "norm_ffn_g", "norm_ple_g"))
        gqa, gka, gqb = (_tile_gain(small[n][l], N_HEADS) for n in ("qnorm_a_g", "knorm_a_g", "qnorm_b_g"))
        gkb = _tile_gain(small["knorm_b_g"][l], N_KV_B)
        sink = jnp.repeat(small["sink_b"][l], BLK_B)[:, None]

        def ple_bwd(dx_, z_, e_):
            s = _sigmoid(z_)
            return dx_ * s, dx_ * e_ * (s * (1.0 - s))

        de, dz = _ew(ple_bwd, [(dx, 0), (sv["z"], 0), (sv["e"], 0)], [BF16, BF16], width=D_MODEL, bw=D_MODEL,
                     name="ple_bwd")
        gbig["w_ple_proj"][l] = _mm(p[l], de, "tn", F32, "mm_d_ple_proj")
        gbig["w_ple_gate"][l] = _mm(sv["h3"], dz, "tn", F32, "mm_d_ple_gate")
        dh3 = _mm(dz, w["w_ple_gate"], "nt", F32, "mm_dh3")
        dx, gsmall["norm_ple_g"][l] = _rms_bwd(sv["x2"], g_ple, dh3, dx, "rms_ple_bwd")

        dhid = _mm(dx, w["w_ffn_down"], "nt", F32, "mm_dhid")
        gbig["w_ffn_down"][l] = _mm(sv["hid"], dx, "tn", F32, "mm_d_ffn_down")

        def swiglu_bwd(a_, u_, dh_):
            s = _sigmoid(a_)
            return dh_ * u_ * (s * (1.0 + a_ * (1.0 - s))), dh_ * (a_ * s)

        da, du = _ew(swiglu_bwd, [(sv["a"], 0), (sv["u"], 0), (dhid, 0)], [BF16, BF16], width=D_FF, bw=D_FF,
                     name="swiglu_bwd")
        gbig["w_ffn_gate"][l] = _mm(sv["h2"], da, "tn", F32, "mm_d_ffn_gate")
        gbig["w_ffn_up"][l] = _mm(sv["h2"], du, "tn", F32, "mm_d_ffn_up")
        dh2 = _mm(da, w["w_ffn_gate"], "nt", F32, "mm_dh2_gate")
        dh2 = _mm(du, w["w_ffn_up"], "nt", F32, "mm_dh2_up", res=dh2)
        dx, gsmall["norm_ffn_g"][l] = _rms_bwd(sv["x1"], g_ffn, dh2, dx, "rms_ffn_bwd")

        dmerged = _mm(dx, w["w_out"], "nt", F32, "mm_dmerged")
        gbig["w_out"][l] = _mm(sv["merged"], dx, "tn", F32, "mm_d_out")

        def gate_bwd(dm_, ca_, cb_, ga_, gb_):
            sa, sb = _sigmoid(ga_), _sigmoid(gb_)
            return dm_ * sa, dm_ * sb, dm_ * ca_ * (sa * (1.0 - sa)), dm_ * cb_ * (sb * (1.0 - sb))

        dca, dcb, dga, dgb = _ew(gate_bwd, [(dmerged, 0), (sv["ca"], 0), (sv["cb"], 0), (sv["proj"], OFF_GA),
                                            (sv["proj"], OFF_GB)], [BF16, BF16, BF16, BF16],
                                 width=D_MODEL, bw=256, name="gate_bwd")
        dgab = jnp.concatenate([dga, dgb], axis=1)
        gbig["w_branch_a"][l] = _mm(sv["ya"], dca, "tn", F32, "mm_d_branch_a")
        gbig["w_branch_b"][l] = _mm(sv["yb"], dcb, "tn", F32, "mm_d_branch_b")
        dya = _mm(dca, w["w_branch_a"], "nt", F32, "mm_dya")
        dyb = _mm(dcb, w["w_branch_b"], "nt", F32, "mm_dyb")

        def rowdot(dy_, y_):
            return (_seg_sum(dy_ * y_),)

        (delta_a,) = _ew(rowdot, [(dya, 0), (sv["ya"], 0)], [F32], width=WIDTH, bw=WIDTH, name="delta_a")
        (delta_b,) = _ew(rowdot, [(dyb, 0), (sv["yb"], 0)], [F32], width=WIDTH, bw=WIDTH, name="delta_b")

        dqa, dka, dva = [], [], []
        for (blk, d), bias, bk in zip(DILATED, bias_a, buckets_a):
            dq_, dk_, dv_, db_ = _attn_bwd(
                _to_view(sv["qa"], d), _to_view(sv["ka"], d), _to_view(sv["va"], d), _to_view(dya, d),
                _to_view(sv["lse_a"], d), _to_view(delta_a, d), bias, None, blk, f"attn_a{d}_bwd")
            dqa.append(_from_view(dq_, d))
            dka.append(_from_view(dk_, d))
            dva.append(_from_view(dv_, d))
            dtable_a = dtable_a + _table_grad(db_.reshape(N_HEADS, blk, 3 * blk), bk, "table_grad_a")
        dqb, dkb, dvb, db_, dsink = _attn_bwd(sv["qb"], sv["kb"], sv["vb"], dyb, sv["lse_b"], delta_b, bias_b, sink,
                                              BLK_B, "attn_b_bwd")
        dtable_b = dtable_b + _table_grad(db_.reshape(N_HEADS, BLK_B, 3 * BLK_B), buckets_b, "table_grad_b")
        gsmall["sink_b"][l] = dsink.reshape(N_HEADS, BLK_B).sum(axis=1)

        dproj, pqa, pka, pqb, pkb = _qknorm_bwd(sv["proj"], gqa, gka, gqb, gkb, dqa, dka, dva, dqb, dkb, dvb, dgab)
        gsmall["qnorm_a_g"][l] = pqa.reshape(N_HEADS, HEAD_DIM).sum(0)
        gsmall["knorm_a_g"][l] = pka.reshape(N_HEADS, HEAD_DIM).sum(0)
        gsmall["qnorm_b_g"][l] = pqb.reshape(N_HEADS, HEAD_DIM).sum(0)
        gsmall["knorm_b_g"][l] = pkb.reshape(N_KV_B, HEAD_DIM).sum(0)
        gbig["w_in"][l] = _mm(sv["h"], dproj, "tn", F32, "mm_d_in")
        dh = _mm(dproj, w["w_in"], "nt", F32, "mm_dh")
        dx, gsmall["norm_mix_g"][l] = _rms_bwd(sv["x0"], g_mix, dh, dx, "rms_mix_bwd")
        gsmall["norm_mix_g"][l] = gsmall["norm_mix_g"][l][0]
        gsmall["norm_ffn_g"][l] = gsmall["norm_ffn_g"][l][0]
        gsmall["norm_ple_g"][l] = gsmall["norm_ple_g"][l][0]

    gbig = {n: jnp.stack(v) for n, v in gbig.items()}
    gsmall = {n: jnp.stack(v) for n, v in gsmall.items()}
    gsmall["rel_table"] = jnp.concatenate([dtable_a, dtable_b], axis=0).T
    return loss, dx, gbig, gsmall


def _place():
    return lax.axis_index("x"), lax.axis_index("y"), lax.axis_index("c")


def _flip(v, bit):
    return 1 - v if bit else v


CHIP_RELATIONS = ((0, 1), (1, 0), (1, 1))
ANY = pl.BlockSpec(memory_space=pl.ANY)


def _allgather_flat(wflat):
    rows, cols = wflat.shape
    half = rows // 2

    def body(w_ref, out_ref, send_sems, recv_sems):
        x, y, c = _place()

        def part(px, py, pc):
            return out_ref.at[2 * px + py, pl.ds(pc * half, half), :]

        def copy(k, block, to, src=None):
            return pltpu.make_async_remote_copy(
                src_ref=part(*block) if src is None else src, dst_ref=part(*block),
                send_sem=send_sems.at[k], recv_sem=recv_sems.at[k], device_id=to, device_id_type=MESH_ID)

        own = pltpu.make_async_remote_copy(
            src_ref=w_ref, dst_ref=out_ref.at[2 * x + y], send_sem=send_sems.at[6], recv_sem=recv_sems.at[6],
            device_id=(x, y, 1 - c), device_id_type=MESH_ID)
        own.start()
        chips = [(_flip(x, a), _flip(y, b)) for a, b in CHIP_RELATIONS]
        first = [copy(k, (x, y, c), (*chip, c), src=w_ref.at[pl.ds(c * half, half), :]) for k, chip in enumerate(chips)]
        for cp in first:
            cp.start()
        passed = [copy(3 + k, (*chip, c), (x, y, 1 - c)) for k, chip in enumerate(chips)]
        for k, chip in enumerate(chips):
            copy(k, (*chip, c), (x, y, c)).wait_recv()
            passed[k].start()
        for k, chip in enumerate(chips):
            copy(3 + k, (*chip, 1 - c), (x, y, c)).wait_recv()
        own.wait_recv()
        for cp in first + passed + [own]:
            cp.wait_send()

    return pl.pallas_call(
        body, out_shape=jax.ShapeDtypeStruct((N_CHIPS, rows, cols), wflat.dtype), in_specs=[ANY], out_specs=ANY,
        scratch_shapes=[pltpu.SemaphoreType.DMA((7,)), pltpu.SemaphoreType.DMA((7,))],
        name="allgather_weights")(wflat)


def _sibling_halves(gsend):
    n, rows, cols = gsend.shape
    half = rows // 2

    def body(g_ref, out_ref, send_sem, recv_sem):
        x, y, c = _place()
        cp = pltpu.make_async_remote_copy(
            src_ref=g_ref.at[:, pl.ds((1 - c) * half, half), :], dst_ref=out_ref, send_sem=send_sem, recv_sem=recv_sem,
            device_id=(x, y, 1 - c), device_id_type=MESH_ID)
        cp.start()
        cp.wait_recv()
        cp.wait_send()

    return pl.pallas_call(
        body, out_shape=jax.ShapeDtypeStruct((n, half, cols), gsend.dtype), in_specs=[ANY], out_specs=ANY,
        scratch_shapes=[pltpu.SemaphoreType.DMA, pltpu.SemaphoreType.DMA], name="rs_sibling_halves")(gsend)


def _chip_sums(gsend, sib, place):
    n, rows, cols = gsend.shape
    half = rows // 2
    tm = 512
    nblk = half // tm

    def body(s_ref, g_ref, sib_ref, o_ref):
        o_ref[0] = (g_ref[0].astype(F32) + sib_ref[0].astype(F32)).astype(o_ref.dtype)

    grid_spec = pltpu.PrefetchScalarGridSpec(
        num_scalar_prefetch=1, grid=(n, nblk),
        in_specs=[pl.BlockSpec((1, tm, cols), lambda k, i, s: (jnp.bitwise_xor(s[0], k), s[1] * nblk + i, 0)),
                  pl.BlockSpec((1, tm, cols), lambda k, i, s: (jnp.bitwise_xor(s[0], k), i, 0))],
        out_specs=pl.BlockSpec((1, tm, cols), lambda k, i, s: (k, i, 0)))
    return pl.pallas_call(
        body, out_shape=jax.ShapeDtypeStruct((n, half, cols), BF16), grid_spec=grid_spec,
        name="rs_chip_sums", compiler_params=_params("parallel", "parallel"))(place, gsend, sib)


def _exchange_chip_sums(tsend):
    n, half, cols = tsend.shape

    def body(t_ref, out_ref, send_sems, recv_sems):
        x, y, c = _place()
        cps = []
        for k, (a, b) in enumerate(CHIP_RELATIONS):
            cps.append(pltpu.make_async_remote_copy(
                src_ref=t_ref.at[k + 1], dst_ref=out_ref.at[k], send_sem=send_sems.at[k], recv_sem=recv_sems.at[k],
                device_id=(_flip(x, a), _flip(y, b), c), device_id_type=MESH_ID))
        for cp in cps:
            cp.start()
        for cp in cps:
            cp.wait_recv()
        for cp in cps:
            cp.wait_send()

    return pl.pallas_call(
        body, out_shape=jax.ShapeDtypeStruct((n - 1, half, cols), tsend.dtype), in_specs=[ANY], out_specs=ANY,
        scratch_shapes=[pltpu.SemaphoreType.DMA((3,)), pltpu.SemaphoreType.DMA((3,))], name="rs_exchange")(tsend)


def _final_sum(tsend, recv, place):
    n, half, cols = tsend.shape
    tm = 512
    nblk = half // tm

    def body(s_ref, t_ref, r_ref, o_ref):
        o_ref[...] = ((t_ref[0].astype(F32) + r_ref[0].astype(F32)) + r_ref[1].astype(F32)) + r_ref[2].astype(F32)

    grid_spec = pltpu.PrefetchScalarGridSpec(
        num_scalar_prefetch=1, grid=(nblk,),
        in_specs=[pl.BlockSpec((1, tm, cols), lambda i, s: (0, i, 0)), pl.BlockSpec((n - 1, tm, cols), lambda i, s: (0, i, 0))],
        out_specs=pl.BlockSpec((tm, cols), lambda i, s: (s[1] * nblk + i, 0)))
    return pl.pallas_call(
        body, out_shape=jax.ShapeDtypeStruct((2 * half, cols), F32), grid_spec=grid_spec, name="rs_final_sum",
        compiler_params=_params("parallel"))(place, tsend, recv)


def _join_halves(gfull):
    rows, cols = gfull.shape
    half = rows // 2

    def body(g_ref, out_ref, send_sem, recv_sem):
        x, y, c = _place()

        def copy(pc):
            return pltpu.make_async_remote_copy(
                src_ref=g_ref.at[pl.ds(pc * half, half), :], dst_ref=out_ref.at[pl.ds(pc * half, half), :],
                send_sem=send_sem, recv_sem=recv_sem, device_id=(x, y, 1 - c), device_id_type=MESH_ID)

        mine = copy(c)
        mine.start()
        copy(1 - c).wait_recv()
        mine.wait_send()

    return pl.pallas_call(
        body, out_shape=jax.ShapeDtypeStruct((rows, cols), gfull.dtype), in_specs=[ANY], out_specs=ANY,
        input_output_aliases={0: 0}, scratch_shapes=[pltpu.SemaphoreType.DMA, pltpu.SemaphoreType.DMA],
        name="rs_join_halves")(gfull)


def _allreduce_small(v):
    rows, cols = v.shape

    def body(v_ref, out_ref, buf, send_sems, recv_sems):
        x, y, c = _place()
        cps = []
        for k in range(1, 8):
            peer = (_flip(x, (k >> 2) & 1), _flip(y, (k >> 1) & 1), _flip(c, k & 1))
            cps.append(pltpu.make_async_remote_copy(
                src_ref=v_ref, dst_ref=buf.at[k - 1], send_sem=send_sems.at[k - 1], recv_sem=recv_sems.at[k - 1],
                device_id=peer, device_id_type=MESH_ID))
        for cp in cps:
            cp.start()
        for cp in cps:
            cp.wait_recv()
        for cp in cps:
            cp.wait_send()
        t0 = v_ref[...] + buf[0]
        t1 = buf[1] + buf[2]
        t2 = buf[3] + buf[4]
        t3 = buf[5] + buf[6]
        out_ref[...] = (t0 + t1) + (t2 + t3)

    vm = pl.BlockSpec(memory_space=pltpu.VMEM)
    return pl.pallas_call(
        body, out_shape=jax.ShapeDtypeStruct((rows, cols), F32), in_specs=[vm], out_specs=vm,
        scratch_shapes=[pltpu.VMEM((7, rows, cols), F32), pltpu.SemaphoreType.DMA((7,)), pltpu.SemaphoreType.DMA((7,))],
        name="allreduce_small")(v)


def _pad_rows(a, rows):
    return jnp.concatenate([a, jnp.zeros((rows - a.shape[0], a.shape[1]), a.dtype)], axis=0)


def _pack_shards(shards, dtype):
    flat = jnp.concatenate([shards[n].astype(dtype).reshape(-1, FLAT_COLS) for n, _, _ in BIG], axis=0)
    return _pad_rows(flat, FLAT_ROWS)


def _unpack_shards(flat, dtype):
    out, off = {}, 0
    for n, (k, m), ax in BIG:
        shape = (DEPTH, k // N_CHIPS, m) if ax == 0 else (DEPTH, k, m // N_CHIPS)
        rows = math.prod(shape) // FLAT_COLS
        out[n] = flat[off:off + rows].reshape(shape).astype(dtype)
        off += rows
    return out


def _unpack_full(wall):
    per_chip = [_unpack_shards(wall[j], wall.dtype) for j in range(N_CHIPS)]
    return {n: jnp.concatenate([pc[n] for pc in per_chip], axis=1 + ax) for n, _, ax in BIG}


def _pack_grads(gbig):
    chunks = []
    for j in range(N_CHIPS):
        shards = {}
        for n, (k, m), ax in BIG:
            size = (k if ax == 0 else m) // N_CHIPS
            shards[n] = lax.slice_in_dim(gbig[n], j * size, (j + 1) * size, axis=1 + ax)
        chunks.append(_pack_shards(shards, BF16))
    return jnp.stack(chunks)


SMALL_SHAPES = {"rel_table": (NUM_BUCKETS, 2 * N_HEADS), "norm_mix_g": (DEPTH, D_MODEL), "qnorm_a_g": (DEPTH, HEAD_DIM),
                "knorm_a_g": (DEPTH, HEAD_DIM), "qnorm_b_g": (DEPTH, HEAD_DIM), "knorm_b_g": (DEPTH, HEAD_DIM),
                "sink_b": (DEPTH, N_HEADS), "norm_ffn_g": (DEPTH, D_MODEL), "norm_ple_g": (DEPTH, D_MODEL)}


def _pack_small(vals):
    flat = jnp.concatenate([vals[n].astype(F32).reshape(-1) for n in SMALL])
    flat = jnp.concatenate([flat, jnp.zeros((SMALL_ROWS * LANES - flat.shape[0],), F32)])
    return flat.reshape(SMALL_ROWS, LANES)


def _unpack_small(packed):
    flat, out, off = packed.reshape(-1), {}, 0
    for n in SMALL:
        size = math.prod(SMALL_SHAPES[n])
        out[n] = flat[off:off + size].reshape(SMALL_SHAPES[n])
        off += size
    return out


def _adamw(w, g, m, v, name):
    c1 = 1.0 - ADAM_B1 ** ADAM_STEP
    c2 = 1.0 - ADAM_B2 ** ADAM_STEP

    def fn(w_, g_, m_, v_):
        m_new = ADAM_B1 * m_ + (1.0 - ADAM_B1) * g_
        v_new = ADAM_B2 * v_ + (1.0 - ADAM_B2) * (g_ * g_)
        delta = -ADAM_LR * ((m_new / c1) / (jnp.sqrt(v_new / c2) + ADAM_EPS) + ADAM_WD * w_)
        return delta, m_new, v_new

    width = w.shape[1]
    return _ew(fn, [(w, 0), (g, 0), (m, 0), (v, 0)], [F32, F32, F32], width=width, bw=width, name=name)


def kernel(x, p, rel_table, norm_mix_g, w_in, qnorm_a_g, knorm_a_g, qnorm_b_g, knorm_b_g, sink_b, w_branch_a, w_branch_b, w_out, norm_ffn_g, w_ffn_gate, w_ffn_up, w_ffn_down, norm_ple_g, w_ple_gate, w_ple_proj, loss_target, m_rel_table, m_norm_mix_g, m_w_in, m_qnorm_a_g, m_knorm_a_g, m_qnorm_b_g, m_knorm_b_g, m_sink_b, m_w_branch_a, m_w_branch_b, m_w_out, m_norm_ffn_g, m_w_ffn_gate, m_w_ffn_up, m_w_ffn_down, m_norm_ple_g, m_w_ple_gate, m_w_ple_proj, v_rel_table, v_norm_mix_g, v_w_in, v_qnorm_a_g, v_knorm_a_g, v_qnorm_b_g, v_knorm_b_g, v_sink_b, v_w_branch_a, v_w_branch_b, v_w_out, v_norm_ffn_g, v_w_ffn_gate, v_w_ffn_up, v_w_ffn_down, v_norm_ple_g, v_w_ple_gate, v_w_ple_proj):
    given = dict(locals())
    weights = {n: given[n] for n in WEIGHTS}
    moments_m = {n: given["m_" + n] for n in WEIGHTS}
    moments_v = {n: given["v_" + n] for n in WEIGHTS}
    xi, yi, ci = _place()
    place = jnp.stack([2 * xi + yi, ci]).astype(jnp.int32)

    wall = _allgather_flat(_pack_shards(weights, BF16))
    big = _unpack_full(wall)
    small = {n: weights[n] for n in SMALL}

    loss, dx, gbig, gsmall = _local_step(x[0], p[:, 0], loss_target[0], big, small)

    gsend = _pack_grads(gbig)
    tsend = _chip_sums(gsend, _sibling_halves(gsend), place)
    gflat = _join_halves(_final_sum(tsend, _exchange_chip_sums(tsend), place))
    grads = _unpack_shards(gflat, F32)
    grads.update(_unpack_small(_allreduce_small(_pack_small(gsmall))))

    delta, new_m, new_v = {}, {}, {}
    for n, _, _ in BIG:
        shape = weights[n].shape
        two_d = lambda a: a.reshape(shape[0] * shape[1], shape[2])
        d_, m_, v_ = _adamw(two_d(weights[n]), two_d(grads[n]), two_d(moments_m[n]), two_d(moments_v[n]), "adamw_" + n)
        delta[n], new_m[n], new_v[n] = d_.reshape(shape), m_.reshape(shape), v_.reshape(shape)
    d_, m_, v_ = _adamw(_pack_small(weights), _pack_small(grads), _pack_small(moments_m), _pack_small(moments_v),
                        "adamw_small")
    delta.update(_unpack_small(d_))
    new_m.update(_unpack_small(m_))
    new_v.update(_unpack_small(v_))

    loss = lax.psum(loss, ("x", "y", "c"))
    return (loss, dx[None], *[grads[n] for n in WEIGHTS], *[delta[n] for n in WEIGHTS],
            *[new_m[n] for n in WEIGHTS], *[new_v[n] for n in WEIGHTS])
```

```python
import functools
import math

import jax
import jax.numpy as jnp
from jax import lax
from jax.experimental import pallas as pl
from jax.experimental.pallas import tpu as pltpu

F32 = jnp.float32
BF16 = jnp.bfloat16
MESH_ID = pl.DeviceIdType.MESH

SEQ = 2048
D_MODEL = 1024
DEPTH = 2
HEAD_DIM = 64
N_HEADS = 8
WIDTH = N_HEADS * HEAD_DIM
N_PAIRS = 4
ITEMS = 4
N_KV_B = 2
PLE_DIM = 256
D_FF = 2816
D_IN = 4352
OFF_QA, OFF_KA, OFF_VA, OFF_QB, OFF_KB, OFF_VB, OFF_GA, OFF_GB = 0, 512, 1024, 1536, 2048, 2176, 2304, 3328
DILATED = ((64, 1), (64, 4), (64, 16))
BLK_B = 128
NUM_BUCKETS = 32
MAX_DISTANCE = 1024
RMS_EPS = 1e-6
NEG_INF = -1e30
LANES = 128
VMEM_LIMIT = 48 * 1024 * 1024

ADAM_LR, ADAM_B1, ADAM_B2, ADAM_EPS, ADAM_WD, ADAM_STEP = 0.001, 0.9, 0.999, 1e-08, 0.01, 10

BIG = (
    ("w_in", (D_MODEL, D_IN), 1),
    ("w_branch_a", (WIDTH, D_MODEL), 1),
    ("w_branch_b", (WIDTH, D_MODEL), 1),
    ("w_out", (D_MODEL, D_MODEL), 0),
    ("w_ffn_gate", (D_MODEL, D_FF), 1),
    ("w_ffn_up", (D_MODEL, D_FF), 1),
    ("w_ffn_down", (D_FF, D_MODEL), 0),
    ("w_ple_gate", (D_MODEL, D_MODEL), 0),
    ("w_ple_proj", (PLE_DIM, D_MODEL), 1),
)
SMALL = ("rel_table", "norm_mix_g", "qnorm_a_g", "knorm_a_g", "qnorm_b_g", "knorm_b_g", "sink_b",
         "norm_ffn_g", "norm_ple_g")
WEIGHTS = ("rel_table", "norm_mix_g", "w_in", "qnorm_a_g", "knorm_a_g", "qnorm_b_g", "knorm_b_g", "sink_b",
           "w_branch_a", "w_branch_b", "w_out", "norm_ffn_g", "w_ffn_gate", "w_ffn_up", "w_ffn_down",
           "norm_ple_g", "w_ple_gate", "w_ple_proj")
N_CHIPS = 4
FLAT_COLS = 1024
FLAT_ROWS = 8192
HALF_ROWS = FLAT_ROWS // 2
SMALL_ROWS = 64


def _params(*sem):
    return pltpu.CompilerParams(dimension_semantics=sem, vmem_limit_bytes=VMEM_LIMIT)


def _pick(dim, target):
    for t in (target, 512, 256, 128, 64, 32, 16, 8):
        if t <= target and dim % t == 0:
            return t
    return dim


def _mm(a, b, mode, out_dtype, name, res=None, tm=512, tn=512):
    if mode == "nn":
        (m, k), (_, n) = a.shape, b.shape
    elif mode == "nt":
        (m, k), (n, _) = a.shape, b.shape
    else:
        (k, m), (_, n) = a.shape, b.shape
    tm, tn = _pick(m, tm), _pick(n, tn)
    a_spec = pl.BlockSpec((k, tm), lambda i, j: (0, i)) if mode == "tn" else pl.BlockSpec((tm, k), lambda i, j: (i, 0))
    b_spec = pl.BlockSpec((tn, k), lambda i, j: (j, 0)) if mode == "nt" else pl.BlockSpec((k, tn), lambda i, j: (0, j))
    dims = {"nn": (((1,), (0,)), ((), ())), "nt": (((1,), (1,)), ((), ())), "tn": (((0,), (0,)), ((), ()))}[mode]
    has_res = res is not None

    def body(*refs):
        a_ref, b_ref = refs[0], refs[1]
        o_ref = refs[-1]
        acc = lax.dot_general(a_ref[...].astype(BF16), b_ref[...].astype(BF16), dims, preferred_element_type=F32)
        if has_res:
            acc = acc + refs[2][...]
        o_ref[...] = acc.astype(out_dtype)

    in_specs = [a_spec, b_spec]
    args = [a, b]
    if has_res:
        in_specs.append(pl.BlockSpec((tm, tn), lambda i, j: (i, j)))
        args.append(res)
    return pl.pallas_call(
        body, out_shape=jax.ShapeDtypeStruct((m, n), out_dtype), grid=(m // tm, n // tn),
        in_specs=in_specs, out_specs=pl.BlockSpec((tm, tn), lambda i, j: (i, j)),
        name=name, compiler_params=_params("parallel", "parallel"))(*args)


def _ew(fn, ins, out_dtypes, *, width, bw, name, vecs=(), tm=256):
    rows = ins[0][0].shape[0]
    tm = _pick(rows, tm)
    n_in = len(ins) + len(vecs)

    def col_map(off_blocks):
        return lambda i, j: (i, off_blocks + j)

    in_specs = [pl.BlockSpec((tm, bw), col_map(off // bw)) for _, off in ins]
    in_specs += [pl.BlockSpec((1, bw), lambda i, j: (0, j)) for _ in vecs]

    def body(*refs):
        outs = fn(*[r[...] for r in refs[:n_in]])
        for r, o in zip(refs[n_in:], outs):
            r[...] = o.astype(r.dtype)

    return pl.pallas_call(
        body, out_shape=[jax.ShapeDtypeStruct((rows, width), dt) for dt in out_dtypes],
        grid=(rows // tm, width // bw), in_specs=in_specs,
        out_specs=[pl.BlockSpec((tm, bw), lambda i, j: (i, j)) for _ in out_dtypes],
        name=name, compiler_params=_params("parallel", "parallel"))(*[a for a, _ in ins], *vecs)


def _sigmoid(x):
    return 1.0 / (1.0 + jnp.exp(-x))


def _seg_sum(v):
    outs = []
    for k in range(v.shape[1] // LANES):
        vp = v[:, k * LANES:(k + 1) * LANES]
        left = lax.broadcasted_iota(jnp.int32, vp.shape, 1) < HEAD_DIM
        sl = jnp.sum(jnp.where(left, vp, 0.0), axis=-1, keepdims=True)
        sr = jnp.sum(jnp.where(left, 0.0, vp), axis=-1, keepdims=True)
        outs.append(jnp.where(left, sl, sr))
    return outs[0] if len(outs) == 1 else jnp.concatenate(outs, axis=1)


def _seg_rstd(x):
    return lax.rsqrt(_seg_sum(x * x) * (1.0 / HEAD_DIM) + RMS_EPS)


def _rms_fwd(x, g, name):
    rows, d = x.shape
    tm = 256

    def body(x_ref, g_ref, h_ref):
        xv = x_ref[...]
        r = lax.rsqrt(jnp.mean(xv * xv, axis=-1, keepdims=True) + RMS_EPS)
        h_ref[...] = ((xv * r) * g_ref[...]).astype(BF16)

    return pl.pallas_call(
        body, out_shape=jax.ShapeDtypeStruct((rows, d), BF16), grid=(rows // tm,),
        in_specs=[pl.BlockSpec((tm, d), lambda i: (i, 0)), pl.BlockSpec((1, d), lambda i: (0, 0))],
        out_specs=pl.BlockSpec((tm, d), lambda i: (i, 0)), name=name, compiler_params=_params("parallel"))(x, g)


def _rms_bwd(x, g, dh, dres, name):
    rows, d = x.shape
    tm = 256

    def body(x_ref, g_ref, dh_ref, dres_ref, dx_ref, dg_ref):
        xv = x_ref[...]
        r = lax.rsqrt(jnp.mean(xv * xv, axis=-1, keepdims=True) + RMS_EPS)
        xh = xv * r
        dhv = dh_ref[...]
        dxh = dhv * g_ref[...]
        dx_ref[...] = dres_ref[...] + r * (dxh - xh * jnp.mean(dxh * xh, axis=-1, keepdims=True))
        part = jnp.sum(dhv * xh, axis=0, keepdims=True)

        @pl.when(pl.program_id(0) == 0)
        def _():
            dg_ref[...] = part

        @pl.when(pl.program_id(0) > 0)
        def _():
            dg_ref[...] += part

    row = pl.BlockSpec((tm, d), lambda i: (i, 0))
    vec = pl.BlockSpec((1, d), lambda i: (0, 0))
    return pl.pallas_call(
        body, out_shape=[jax.ShapeDtypeStruct((rows, d), F32), jax.ShapeDtypeStruct((1, d), F32)],
        grid=(rows // tm,), in_specs=[row, vec, row, row], out_specs=[row, vec],
        name=name, compiler_params=_params("arbitrary"))(x, g, dh, dres)


def _loss_grad(y, t):
    rows, d = y.shape
    tm = 256

    def body(y_ref, t_ref, dy_ref, l_ref):
        e = y_ref[...] - t_ref[...]
        dy_ref[...] = e * (1.0 / d)
        part = jnp.zeros((1, LANES), F32) + jnp.sum(e * e) * (0.5 / d)

        @pl.when(pl.program_id(0) == 0)
        def _():
            l_ref[...] = part

        @pl.when(pl.program_id(0) > 0)
        def _():
            l_ref[...] += part

    row = pl.BlockSpec((tm, d), lambda i: (i, 0))
    return pl.pallas_call(
        body, out_shape=[jax.ShapeDtypeStruct((rows, d), F32), jax.ShapeDtypeStruct((1, LANES), F32)],
        grid=(rows // tm,), in_specs=[row, row], out_specs=[row, pl.BlockSpec((1, LANES), lambda i: (0, 0))],
        name="loss_grad", compiler_params=_params("arbitrary"))(y, t)


def _put_pairs(ref, val):
    for hp in range(N_PAIRS):
        ref[hp] = val[:, hp * LANES:(hp + 1) * LANES].astype(ref.dtype)


def _get_pairs(ref):
    return jnp.concatenate([ref[hp] for hp in range(N_PAIRS)], axis=1)


def _swap_halves(v):
    return pltpu.roll(v, HEAD_DIM, axis=1)


def _expand_kv(kv):
    left = lax.broadcasted_iota(jnp.int32, kv.shape, 1) < HEAD_DIM
    sw = _swap_halves(kv)
    h0 = jnp.where(left, kv, sw)
    h1 = jnp.where(left, sw, kv)
    return jnp.concatenate([h0, h0, h1, h1], axis=1)


def _reduce_kv(dkv):
    left = lax.broadcasted_iota(jnp.int32, (dkv.shape[0], LANES), 1) < HEAD_DIM
    t = dkv[:, 0:LANES] + dkv[:, LANES:2 * LANES]
    u = dkv[:, 2 * LANES:3 * LANES] + dkv[:, 3 * LANES:4 * LANES]
    t = t + _swap_halves(t)
    u = u + _swap_halves(u)
    return jnp.where(left, t, u)


def _qknorm_fwd(proj, gqa, gka, gqb, gkb):
    rows = proj.shape[0]
    tm = 256

    def body(qa_ref, ka_ref, va_ref, qb_ref, kb_ref, vb_ref, gqa_ref, gka_ref, gqb_ref, gkb_ref,
             oqa, oka, ova, oqb, okb, ovb):
        for src, g_ref, dst in ((qa_ref, gqa_ref, oqa), (ka_ref, gka_ref, oka), (qb_ref, gqb_ref, oqb)):
            xv = src[...]
            _put_pairs(dst, (xv * _seg_rstd(xv)) * g_ref[...])
        _put_pairs(ova, va_ref[...])
        kv = kb_ref[...]
        _put_pairs(okb, _expand_kv((kv * _seg_rstd(kv)) * gkb_ref[...]))
        _put_pairs(ovb, _expand_kv(vb_ref[...]))

    def win(width, off):
        return pl.BlockSpec((tm, width), lambda i: (i, off // width))

    vec = lambda w: pl.BlockSpec((1, w), lambda i: (0, 0))
    out = pl.BlockSpec((N_PAIRS, tm, LANES), lambda i: (0, i, 0))
    return pl.pallas_call(
        body, out_shape=[jax.ShapeDtypeStruct((N_PAIRS, rows, LANES), F32)] * 6, grid=(rows // tm,),
        in_specs=[win(WIDTH, OFF_QA), win(WIDTH, OFF_KA), win(WIDTH, OFF_VA), win(WIDTH, OFF_QB),
                  win(LANES, OFF_KB), win(LANES, OFF_VB), vec(WIDTH), vec(WIDTH), vec(WIDTH), vec(LANES)],
        out_specs=[out] * 6, name="qknorm_fwd", compiler_params=_params("parallel"))(
            proj, proj, proj, proj, proj, proj, gqa, gka, gqb, gkb)


def _norm_bwd(xv, g, dy):
    r = _seg_rstd(xv)
    xh = xv * r
    dxh = dy * g
    dx = r * (dxh - xh * (_seg_sum(dxh * xh) * (1.0 / HEAD_DIM)))
    return dx, jnp.sum(dy * xh, axis=0, keepdims=True)


def _qknorm_bwd(proj, gqa, gka, gqb, gkb, dqa, dka, dva, dqb, dkb, dvb, dgab):
    rows = proj.shape[0]
    tm = 256
    n_a = len(dqa)

    def body(*refs):
        qa_ref, ka_ref, qb_ref, kb_ref, gqa_ref, gka_ref, gqb_ref, gkb_ref = refs[:8]
        pos = 8
        dqa_refs, dka_refs, dva_refs = refs[pos:pos + n_a], refs[pos + n_a:pos + 2 * n_a], refs[pos + 2 * n_a:pos + 3 * n_a]
        pos += 3 * n_a
        dqb_ref, dkb_ref, dvb_ref, dgab_ref = refs[pos:pos + 4]
        dproj_ref, ogqa, ogka, ogqb, ogkb = refs[pos + 4:]

        def total(rs):
            acc = _get_pairs(rs[0])
            for r in rs[1:]:
                acc = acc + _get_pairs(r)
            return acc

        dx_qa, p_qa = _norm_bwd(qa_ref[...], gqa_ref[...], total(dqa_refs))
        dx_ka, p_ka = _norm_bwd(ka_ref[...], gka_ref[...], total(dka_refs))
        dx_qb, p_qb = _norm_bwd(qb_ref[...], gqb_ref[...], _get_pairs(dqb_ref))
        dx_kb, p_kb = _norm_bwd(kb_ref[...], gkb_ref[...], _reduce_kv(_get_pairs(dkb_ref)))
        dproj_ref[:, OFF_QA:OFF_QA + WIDTH] = dx_qa.astype(BF16)
        dproj_ref[:, OFF_KA:OFF_KA + WIDTH] = dx_ka.astype(BF16)
        dproj_ref[:, OFF_VA:OFF_VA + WIDTH] = total(dva_refs).astype(BF16)
        dproj_ref[:, OFF_QB:OFF_QB + WIDTH] = dx_qb.astype(BF16)
        dproj_ref[:, OFF_KB:OFF_KB + LANES] = dx_kb.astype(BF16)
        dproj_ref[:, OFF_VB:OFF_VB + LANES] = _reduce_kv(_get_pairs(dvb_ref)).astype(BF16)
        dproj_ref[:, OFF_GA:D_IN] = dgab_ref[...]
        first = pl.program_id(0) == 0
        for o_ref, part in ((ogqa, p_qa), (ogka, p_ka), (ogqb, p_qb), (ogkb, p_kb)):
            @pl.when(first)
            def _(o_ref=o_ref, part=part):
                o_ref[...] = part

            @pl.when(jnp.logical_not(first))
            def _(o_ref=o_ref, part=part):
                o_ref[...] += part

    def win(width, off):
        return pl.BlockSpec((tm, width), lambda i: (i, off // width))

    vec = lambda w: pl.BlockSpec((1, w), lambda i: (0, 0))
    row = lambda w: pl.BlockSpec((tm, w), lambda i: (i, 0))
    in_specs = [win(WIDTH, OFF_QA), win(WIDTH, OFF_KA), win(WIDTH, OFF_QB), win(LANES, OFF_KB),
                vec(WIDTH), vec(WIDTH), vec(WIDTH), vec(LANES)]
    in_specs += [pl.BlockSpec((N_PAIRS, tm, LANES), lambda i: (0, i, 0))] * (3 * n_a + 3) + [row(2 * D_MODEL)]
    return pl.pallas_call(
        body,
        out_shape=[jax.ShapeDtypeStruct((rows, D_IN), BF16), jax.ShapeDtypeStruct((1, WIDTH), F32),
                   jax.ShapeDtypeStruct((1, WIDTH), F32), jax.ShapeDtypeStruct((1, WIDTH), F32),
                   jax.ShapeDtypeStruct((1, LANES), F32)],
        grid=(rows // tm,), in_specs=in_specs,
        out_specs=[row(D_IN), vec(WIDTH), vec(WIDTH), vec(WIDTH), vec(LANES)],
        name="qknorm_bwd", compiler_params=_params("arbitrary"))(
            proj, proj, proj, proj, gqa, gka, gqb, gkb, *dqa, *dka, *dva, dqb, dkb, dvb, dgab)


def _t5_bucket(rel):
    half_b = NUM_BUCKETS // 2
    max_exact = half_b // 2
    sign = jnp.where(rel > 0, half_b, 0)
    n = jnp.abs(rel)
    nf = jnp.maximum(n, 1).astype(F32)
    large = max_exact + (jnp.log(nf / max_exact) / math.log(MAX_DISTANCE / max_exact)
                         * (half_b - max_exact)).astype(jnp.int32)
    large = jnp.minimum(large, half_b - 1)
    return sign + jnp.where(n < max_exact, n, large)


def _band_buckets(blk, dilation):
    i = jnp.arange(blk, dtype=jnp.int32)[:, None]
    j = jnp.arange(3 * blk, dtype=jnp.int32)[None, :]
    rel = j - blk - i
    return jnp.where(jnp.abs(rel) <= blk, _t5_bucket(rel * dilation), -1)


def _bias_tiles(table, buckets, head_off, name):
    blk = buckets.shape[0]

    def body(tab_ref, bk_ref, o_ref):
        h = pl.program_id(0) + head_off
        bk = bk_ref[...]
        acc = jnp.full(bk.shape, NEG_INF, F32)
        for b in range(NUM_BUCKETS):
            acc = jnp.where(bk == b, tab_ref[b, h], acc)
        o_ref[0] = acc

    return pl.pallas_call(
        body, out_shape=jax.ShapeDtypeStruct((N_HEADS, blk, 3 * blk), F32), grid=(N_HEADS,),
        in_specs=[pl.BlockSpec(memory_space=pltpu.SMEM), pl.BlockSpec((blk, 3 * blk), lambda h: (0, 0))],
        out_specs=pl.BlockSpec((1, blk, 3 * blk), lambda h: (h, 0, 0)),
        name=name, compiler_params=_params("parallel"))(table, buckets)


def _table_grad(dbias, buckets, name):
    blk = buckets.shape[0]

    def body(db_ref, bk_ref, o_ref):
        bk = bk_ref[...]
        dbv = db_ref[0]
        lane = lax.broadcasted_iota(jnp.int32, (1, LANES), 1)
        acc = jnp.zeros((1, LANES), F32)
        for b in range(NUM_BUCKETS):
            acc = jnp.where(lane == b, jnp.sum(jnp.where(bk == b, dbv, 0.0)), acc)
        o_ref[0] = acc

    out = pl.pallas_call(
        body, out_shape=jax.ShapeDtypeStruct((N_HEADS, 1, LANES), F32), grid=(N_HEADS,),
        in_specs=[pl.BlockSpec((1, blk, 3 * blk), lambda h: (h, 0, 0)), pl.BlockSpec((blk, 3 * blk), lambda h: (0, 0))],
        out_specs=pl.BlockSpec((1, 1, LANES), lambda h: (h, 0, 0)),
        name=name, compiler_params=_params("parallel"))(dbias, buckets)
    return out[:, 0, :NUM_BUCKETS]


def _dot_nt(a, b):
    return lax.dot_general(a, b, (((1,), (1,)), ((), ())), preferred_element_type=F32)


def _stack_pair(x2, left):
    return jnp.concatenate([jnp.where(left, x2, 0.0), jnp.where(left, 0.0, x2)], axis=0).astype(BF16)


def _attn_geometry(blk, d):
    chunk = blk * ITEMS if d == 1 else blk * d
    groups = 1 if d == 1 else d // ITEMS
    halo = blk if d == 1 else chunk
    return chunk, groups, halo


def _item_rows(ref, j, r0, blk, d):
    if d == 1:
        return ref[j * blk:(j + 1) * blk, :]
    return ref[pl.ds(r0 + j, blk, stride=d), :]


def _item_penalty(t, nct, j, blk, d):
    first_ok, last_ok = t > 0, t < nct - 1
    if d == 1:
        first_ok = True if j > 0 else first_ok
        last_ok = True if j < ITEMS - 1 else last_ok
    col = lax.broadcasted_iota(jnp.int32, (1, 3 * blk), 1)
    ok = jnp.logical_and(jnp.logical_or(col >= blk, first_ok), jnp.logical_or(col < 2 * blk, last_ok))
    return jnp.where(ok, 0.0, NEG_INF).astype(F32)


def _attn_specs(seq, blk, d, step_of):
    chunk, _, halo = _attn_geometry(blk, d)
    per, last = chunk // halo, seq // halo - 1
    cur = pl.BlockSpec((None, chunk, LANES), lambda hp, t: (hp, step_of(t), 0))
    prev = pl.BlockSpec((None, halo, LANES), lambda hp, t: (hp, jnp.clip(step_of(t) * per - 1, 0, last), 0))
    nxt = pl.BlockSpec((None, halo, LANES), lambda hp, t: (hp, jnp.minimum((step_of(t) + 1) * per, last), 0))
    return cur, prev, nxt


def _attn_fwd(q, k, v, bias, sink, blk, d, name):
    _, seq, _ = q.shape
    chunk, groups, _ = _attn_geometry(blk, d)
    nct = seq // chunk
    has_sink = sink is not None
    scale = HEAD_DIM ** -0.5

    def body(*refs):
        q_ref, kp, kc, kn, vp, vc, vn, b_ref = refs[:8]
        s_ref = refs[8] if has_sink else None
        o_ref, l_ref = refs[-2], refs[-1]
        t = pl.program_id(1)
        left = lax.broadcasted_iota(jnp.int32, (1, LANES), 1) < HEAD_DIM
        bias2 = b_ref[...]
        if d == 1:
            kwin = jnp.concatenate([kp[...], kc[...], kn[...]], axis=0).astype(BF16)
            vwin = jnp.concatenate([vp[...], vc[...], vn[...]], axis=0).astype(BF16)

        def group(r0):
            scores, vcats = [], []
            for j in range(ITEMS):
                qs = _stack_pair(_item_rows(q_ref, j, r0, blk, d) * scale, left)
                if d == 1:
                    kcat, vcat = kwin[j * blk:(j + 3) * blk], vwin[j * blk:(j + 3) * blk]
                else:
                    kcat = jnp.concatenate([_item_rows(r, j, r0, blk, d) for r in (kp, kc, kn)], axis=0).astype(BF16)
                    vcat = jnp.concatenate([_item_rows(r, j, r0, blk, d) for r in (vp, vc, vn)], axis=0).astype(BF16)
                scores.append(_dot_nt(qs, kcat) + bias2 + _item_penalty(t, nct, j, blk, d))
                vcats.append(vcat)
            s = jnp.concatenate(scores, axis=0)
            m = jnp.max(s, axis=-1, keepdims=True)
            if has_sink:
                sk = jnp.concatenate([s_ref[...]] * ITEMS, axis=0)
                m = jnp.maximum(m, sk)
            p = jnp.exp(s - m)
            den = jnp.sum(p, axis=-1, keepdims=True)
            if has_sink:
                den = den + jnp.exp(sk - m)
            pn = (p * (1.0 / den)).astype(BF16)
            lse = m + jnp.log(den)
            for j in range(ITEMS):
                top, mid, bot = 2 * j * blk, (2 * j + 1) * blk, (2 * j + 2) * blk
                o2 = jnp.dot(pn[top:bot], vcats[j], preferred_element_type=F32)
                o_val = jnp.where(left, o2[:blk], o2[blk:])
                l_val = jnp.where(left, lse[top:mid], lse[mid:bot])
                if d == 1:
                    o_ref[j * blk:(j + 1) * blk, :] = o_val
                    l_ref[j * blk:(j + 1) * blk, :] = l_val
                else:
                    o_ref[pl.ds(r0 + j, blk, stride=d), :] = o_val
                    l_ref[pl.ds(r0 + j, blk, stride=d), :] = l_val

        if groups == 1:
            group(0)
        else:
            def step(g, carry):
                group(g * ITEMS)
                return carry

            lax.fori_loop(0, groups, step, 0)

    cur, prev, nxt = _attn_specs(seq, blk, d, lambda t: t)
    in_specs = [cur, prev, cur, nxt, prev, cur, nxt, pl.BlockSpec((2 * blk, 3 * blk), lambda hp, t: (hp, 0))]
    args = [q, k, k, k, v, v, v, bias]
    if has_sink:
        in_specs.append(pl.BlockSpec((2 * blk, 1), lambda hp, t: (hp, 0)))
        args.append(sink)
    return pl.pallas_call(
        body, out_shape=[jax.ShapeDtypeStruct(q.shape, F32)] * 2, grid=(N_PAIRS, nct),
        in_specs=in_specs, out_specs=[cur, cur], name=name, compiler_params=_params("parallel", "parallel"))(*args)


def _attn_bwd(q, k, v, do, lse, delta, bias, sink, blk, d, name):
    _, seq, _ = q.shape
    chunk, groups, halo = _attn_geometry(blk, d)
    nct = seq // chunk
    has_sink = sink is not None
    n_in = 12 if has_sink else 11
    scale = HEAD_DIM ** -0.5

    def body(*refs):
        q_ref, kp, kc, kn, vp, vc, vn, do_ref, l_ref, d_ref, b_ref = refs[:11]
        s_ref = refs[11] if has_sink else None
        dq_ref, dk_ref, dv_ref, db_ref = refs[n_in:n_in + 4]
        ds_ref = refs[n_in + 4] if has_sink else None
        wk, wv = refs[-2], refs[-1]
        t = pl.program_id(1)

        @pl.when(t == 0)
        def _():
            wk[...] = jnp.zeros_like(wk)
            wv[...] = jnp.zeros_like(wv)
            db_ref[...] = jnp.zeros_like(db_ref)
            if has_sink:
                ds_ref[...] = jnp.zeros_like(ds_ref)

        @pl.when(t > 0)
        def _():
            for w in (wk, wv):
                keep = w[chunk:2 * chunk + halo]
                w[0:chunk + halo] = keep
                w[chunk + halo:2 * chunk + halo] = jnp.zeros((chunk, LANES), F32)

        @pl.when(t < nct)
        def _():
            lane = lax.broadcasted_iota(jnp.int32, (1, LANES), 1)
            left = lane < HEAD_DIM
            bias2 = b_ref[...]
            if d == 1:
                kwin = jnp.concatenate([kp[...], kc[...], kn[...]], axis=0).astype(BF16)
                vwin = jnp.concatenate([vp[...], vc[...], vn[...]], axis=0).astype(BF16)

            def group(r0):
                qss, doss, kcats, scores, dps, lcols, dcols = [], [], [], [], [], [], []
                for j in range(ITEMS):
                    qs = _stack_pair(_item_rows(q_ref, j, r0, blk, d) * scale, left)
                    dos = _stack_pair(_item_rows(do_ref, j, r0, blk, d), left)
                    if d == 1:
                        kcat, vcat = kwin[j * blk:(j + 3) * blk], vwin[j * blk:(j + 3) * blk]
                    else:
                        kcat = jnp.concatenate([_item_rows(r, j, r0, blk, d) for r in (kp, kc, kn)], axis=0).astype(BF16)
                        vcat = jnp.concatenate([_item_rows(r, j, r0, blk, d) for r in (vp, vc, vn)], axis=0).astype(BF16)
                    l2, d2 = _item_rows(l_ref, j, r0, blk, d), _item_rows(d_ref, j, r0, blk, d)
                    lcols.append(jnp.max(jnp.where(left, l2, NEG_INF), axis=-1, keepdims=True))
                    lcols.append(jnp.max(jnp.where(left, NEG_INF, l2), axis=-1, keepdims=True))
                    dcols.append(jnp.sum(jnp.where(lane == 0, d2, 0.0), axis=-1, keepdims=True))
                    dcols.append(jnp.sum(jnp.where(lane == HEAD_DIM, d2, 0.0), axis=-1, keepdims=True))
                    scores.append(_dot_nt(qs, kcat) + bias2 + _item_penalty(t, nct, j, blk, d))
                    dps.append(_dot_nt(dos, vcat))
                    qss.append(qs)
                    doss.append(dos)
                    kcats.append(kcat)
                lcol = jnp.concatenate(lcols, axis=0)
                dcol = jnp.concatenate(dcols, axis=0)
                p = jnp.exp(jnp.concatenate(scores, axis=0) - lcol)
                ds = p * (jnp.concatenate(dps, axis=0) - dcol)
                if has_sink:
                    sgrad = dcol * jnp.exp(jnp.concatenate([s_ref[...]] * ITEMS, axis=0) - lcol)
                for j in range(ITEMS):
                    top, bot = 2 * j * blk, (2 * j + 2) * blk
                    dsj, pj = ds[top:bot], p[top:bot]
                    db_ref[...] += dsj
                    if has_sink:
                        ds_ref[...] -= sgrad[top:bot]
                    dq2 = jnp.dot(dsj.astype(BF16), kcats[j], preferred_element_type=F32) * scale
                    dq_val = jnp.where(left, dq2[:blk], dq2[blk:])
                    if d == 1:
                        dq_ref[j * blk:(j + 1) * blk, :] = dq_val
                    else:
                        dq_ref[pl.ds(r0 + j, blk, stride=d), :] = dq_val
                    dk_new = jnp.dot(jnp.transpose(dsj).astype(BF16), qss[j], preferred_element_type=F32)
                    dv_new = jnp.dot(jnp.transpose(pj).astype(BF16), doss[j], preferred_element_type=F32)
                    for w, new in ((wk, dk_new), (wv, dv_new)):
                        if d == 1:
                            w[chunk + (j - 1) * blk:chunk + (j + 2) * blk, :] += new
                        else:
                            for c in range(3):
                                w[pl.ds(c * chunk + r0 + j, blk, stride=d), :] += new[c * blk:(c + 1) * blk]

            if groups == 1:
                group(0)
            else:
                def step(g, carry):
                    group(g * ITEMS)
                    return carry

                lax.fori_loop(0, groups, step, 0)

        dk_ref[...] = wk[0:chunk]
        dv_ref[...] = wv[0:chunk]

    cur, prev, nxt = _attn_specs(seq, blk, d, lambda t: jnp.minimum(t, nct - 1))
    lag = pl.BlockSpec((None, chunk, LANES), lambda hp, t: (hp, jnp.maximum(t - 1, 0), 0))
    band = pl.BlockSpec((2 * blk, 3 * blk), lambda hp, t: (hp, 0))
    col = pl.BlockSpec((2 * blk, 1), lambda hp, t: (hp, 0))
    in_specs = [cur, prev, cur, nxt, prev, cur, nxt, cur, cur, cur, band]
    args = [q, k, k, k, v, v, v, do, lse, delta, bias]
    out_shape = [jax.ShapeDtypeStruct(q.shape, F32)] * 3 + [jax.ShapeDtypeStruct((N_HEADS * blk, 3 * blk), F32)]
    out_specs = [cur, lag, lag, band]
    if has_sink:
        in_specs.append(col)
        args.append(sink)
        out_shape.append(jax.ShapeDtypeStruct((N_HEADS * blk, 1), F32))
        out_specs.append(col)
    window = pltpu.VMEM((2 * chunk + halo, LANES), F32)
    return pl.pallas_call(
        body, out_shape=out_shape, grid=(N_PAIRS, nct + 1), in_specs=in_specs, out_specs=out_specs,
        scratch_shapes=[window, window], name=name, compiler_params=_params("arbitrary", "arbitrary"))(*args)


def _combine_patterns(outs, lses):
    _, rows, _ = outs[0].shape
    tm = 256
    n = len(outs)

    def body(*refs):
        o_refs, l_refs = refs[:n], refs[n:2 * n]
        y_ref, lse_ref = refs[2 * n], refs[2 * n + 1]
        for hp in range(N_PAIRS):
            ls = [r[hp] for r in l_refs]
            m = functools.reduce(jnp.maximum, ls)
            es = [jnp.exp(l - m) for l in ls]
            den = functools.reduce(lambda a, b: a + b, es)
            num = functools.reduce(lambda a, b: a + b, [e * r[hp] for e, r in zip(es, o_refs)])
            y_ref[:, hp * LANES:(hp + 1) * LANES] = num / den
            lse_ref[hp] = m + jnp.log(den)

    pm = pl.BlockSpec((N_PAIRS, tm, LANES), lambda i: (0, i, 0))
    return pl.pallas_call(
        body, out_shape=[jax.ShapeDtypeStruct((rows, WIDTH), F32), jax.ShapeDtypeStruct((N_PAIRS, rows, LANES), F32)],
        grid=(rows // tm,), in_specs=[pm] * (2 * n), out_specs=[pl.BlockSpec((tm, WIDTH), lambda i: (i, 0)), pm],
        name="combine_a", compiler_params=_params("parallel"))(*outs, *lses)


def _pairs_to_tokens(a):
    _, rows, _ = a.shape
    tm = 256

    def body(a_ref, o_ref):
        o_ref[...] = _get_pairs(a_ref)

    return pl.pallas_call(
        body, out_shape=jax.ShapeDtypeStruct((rows, WIDTH), a.dtype), grid=(rows // tm,),
        in_specs=[pl.BlockSpec((N_PAIRS, tm, LANES), lambda i: (0, i, 0))],
        out_specs=pl.BlockSpec((tm, WIDTH), lambda i: (i, 0)), name="pairs_to_tokens",
        compiler_params=_params("parallel"))(a)


def _attn_bwd_prep(dy, y, name):
    rows = dy.shape[0]
    tm = 256

    def body(dy_ref, y_ref, do_ref, dl_ref):
        dyv = dy_ref[...]
        _put_pairs(do_ref, dyv)
        _put_pairs(dl_ref, _seg_sum(dyv * y_ref[...]))

    tok = pl.BlockSpec((tm, WIDTH), lambda i: (i, 0))
    pm = pl.BlockSpec((N_PAIRS, tm, LANES), lambda i: (0, i, 0))
    return pl.pallas_call(
        body, out_shape=[jax.ShapeDtypeStruct((N_PAIRS, rows, LANES), F32)] * 2, grid=(rows // tm,),
        in_specs=[tok, tok], out_specs=[pm, pm], name=name, compiler_params=_params("parallel"))(dy, y)


def _tile_gain(g, reps):
    return jnp.tile(g[None, :], (1, reps))


def _local_step(x, p, target, big, small):
    rel_table = small["rel_table"]
    buckets_a = [_band_buckets(blk, d) for blk, d in DILATED]
    buckets_b = _band_buckets(BLK_B, 1)
    bias_a = [_bias_tiles(rel_table, bk, 0, "bias_a").reshape(N_HEADS * bk.shape[0], -1) for bk in buckets_a]
    bias_b = _bias_tiles(rel_table, buckets_b, N_HEADS, "bias_b").reshape(N_HEADS * BLK_B, -1)

    saved = []
    for l in range(DEPTH):
        w = {n: big[n][l] for n, _, _ in BIG}
        g_mix, g_ffn, g_ple = (small[n][l][None, :] for n in ("norm_mix_g", "norm_ffn_g", "norm_ple_g"))
        gqa, gka, gqb = (_tile_gain(small[n][l], N_HEADS) for n in ("qnorm_a_g", "knorm_a_g", "qnorm_b_g"))
        gkb = _tile_gain(small["knorm_b_g"][l], N_KV_B)
        sink = jnp.repeat(small["sink_b"][l], BLK_B)[:, None]

        h = _rms_fwd(x, g_mix, "rms_mix")
        proj = _mm(h, w["w_in"], "nn", F32, "mm_in")
        qa, ka, va, qb, kb, vb = _qknorm_fwd(proj, gqa, gka, gqb, gkb)
        outs, lses = [], []
        for (blk, d), bias in zip(DILATED, bias_a):
            o, ls = _attn_fwd(qa, ka, va, bias, None, blk, d, f"attn_a{d}_fwd")
            outs.append(o)
            lses.append(ls)
        ya, lse_a = _combine_patterns(outs, lses)
        yb, lse_b = _attn_fwd(qb, kb, vb, bias_b, sink, BLK_B, 1, "attn_b_fwd")
        yb = _pairs_to_tokens(yb)
        ca = _mm(ya, w["w_branch_a"], "nn", F32, "mm_branch_a")
        cb = _mm(yb, w["w_branch_b"], "nn", F32, "mm_branch_b")

        def gate(ca_, cb_, ga_, gb_):
            return (_sigmoid(ga_) * ca_ + _sigmoid(gb_) * cb_,)

        (merged,) = _ew(gate, [(ca, 0), (cb, 0), (proj, OFF_GA), (proj, OFF_GB)], [BF16],
                        width=D_MODEL, bw=256, name="gate")
        x1 = _mm(merged, w["w_out"], "nn", F32, "mm_out", res=x)

        h2 = _rms_fwd(x1, g_ffn, "rms_ffn")
        a = _mm(h2, w["w_ffn_gate"], "nn", F32, "mm_ffn_gate")
        u = _mm(h2, w["w_ffn_up"], "nn", F32, "mm_ffn_up")

        def swiglu(a_, u_):
            return ((a_ * _sigmoid(a_)) * u_,)

        (hid,) = _ew(swiglu, [(a, 0), (u, 0)], [BF16], width=D_FF, bw=D_FF, name="swiglu")
        x2 = _mm(hid, w["w_ffn_down"], "nn", F32, "mm_ffn_down", res=x1)

        h3 = _rms_fwd(x2, g_ple, "rms_ple")
        z = _mm(h3, w["w_ple_gate"], "nn", F32, "mm_ple_gate")
        e = _mm(p[l], w["w_ple_proj"], "nn", F32, "mm_ple_proj")

        def ple(x2_, z_, e_):
            return (x2_ + _sigmoid(z_) * e_,)

        (x3,) = _ew(ple, [(x2, 0), (z, 0), (e, 0)], [F32], width=D_MODEL, bw=D_MODEL, name="ple")
        saved.append(dict(x0=x, h=h, proj=proj, qa=qa, ka=ka, va=va, qb=qb, kb=kb, vb=vb, ya=ya, lse_a=lse_a,
                          yb=yb, lse_b=lse_b, ca=ca, cb=cb, merged=merged, x1=x1, h2=h2, a=a, u=u, hid=hid,
                          x2=x2, h3=h3, z=z, e=e))
        x = x3

    dx, loss_acc = _loss_grad(x, target)
    loss = loss_acc[0, 0]

    gbig = {n: [None] * DEPTH for n, _, _ in BIG}
    gsmall = {n: [None] * DEPTH for n in SMALL if n != "rel_table"}
    dtable_a = jnp.zeros((N_HEADS, NUM_BUCKETS), F32)
    dtable_b = jnp.zeros((N_HEADS, NUM_BUCKETS), F32)

    for l in reversed(range(DEPTH)):
        sv = saved[l]
        w = {n: big[n][l] for n, _, _ in BIG}
        g_mix, g_ffn, g_ple = (small[n][l][None, :] for n in ("norm_mix_g", "norm_ffn_g", "norm_ple_g"))
        gqa, gka, gqb = (_tile_gain(small[n][l], N_HEADS) for n in ("qnorm_a_g", "knorm_a_g", "qnorm_b_g"))
        gkb = _tile_gain(small["knorm_b_g"][l], N_KV_B)
        sink = jnp.repeat(small["sink_b"][l], BLK_B)[:, None]

        def ple_bwd(dx_, z_, e_):
            s = _sigmoid(z_)
            return dx_ * s, dx_ * e_ * (s * (1.0 - s))

        de, dz = _ew(ple_bwd, [(dx, 0), (sv["z"], 0), (sv["e"], 0)], [BF16, BF16], width=D_MODEL, bw=D_MODEL,
                     name="ple_bwd")
        gbig["w_ple_proj"][l] = _mm(p[l], de, "tn", F32, "mm_d_ple_proj")
        gbig["w_ple_gate"][l] = _mm(sv["h3"], dz, "tn", F32, "mm_d_ple_gate")
        dh3 = _mm(dz, w["w_ple_gate"], "nt", F32, "mm_dh3")
        dx, gsmall["norm_ple_g"][l] = _rms_bwd(sv["x2"], g_ple, dh3, dx, "rms_ple_bwd")

        dhid = _mm(dx, w["w_ffn_down"], "nt", F32, "mm_dhid")
        gbig["w_ffn_down"][l] = _mm(sv["hid"], dx, "tn", F32, "mm_d_ffn_down")

        def swiglu_bwd(a_, u_, dh_):
            s = _sigmoid(a_)
            return dh_ * u_ * (s * (1.0 + a_ * (1.0 - s))), dh_ * (a_ * s)

        da, du = _ew(swiglu_bwd, [(sv["a"], 0), (sv["u"], 0), (dhid, 0)], [BF16, BF16], width=D_FF, bw=D_FF,
                     name="swiglu_bwd")
        gbig["w_ffn_gate"][l] = _mm(sv["h2"], da, "tn", F32, "mm_d_ffn_gate")
        gbig["w_ffn_up"][l] = _mm(sv["h2"], du, "tn", F32, "mm_d_ffn_up")
        dh2 = _mm(da, w["w_ffn_gate"], "nt", F32, "mm_dh2_gate")
        dh2 = _mm(du, w["w_ffn_up"], "nt", F32, "mm_dh2_up", res=dh2)
        dx, gsmall["norm_ffn_g"][l] = _rms_bwd(sv["x1"], g_ffn, dh2, dx, "rms_ffn_bwd")

        dmerged = _mm(dx, w["w_out"], "nt", F32, "mm_dmerged")
        gbig["w_out"][l] = _mm(sv["merged"], dx, "tn", F32, "mm_d_out")

        def gate_bwd(dm_, ca_, cb_, ga_, gb_):
            sa, sb = _sigmoid(ga_), _sigmoid(gb_)
            return dm_ * sa, dm_ * sb, dm_ * ca_ * (sa * (1.0 - sa)), dm_ * cb_ * (sb * (1.0 - sb))

        dca, dcb, dga, dgb = _ew(gate_bwd, [(dmerged, 0), (sv["ca"], 0), (sv["cb"], 0), (sv["proj"], OFF_GA),
                                            (sv["proj"], OFF_GB)], [BF16, BF16, BF16, BF16],
                                 width=D_MODEL, bw=256, name="gate_bwd")
        dgab = jnp.concatenate([dga, dgb], axis=1)
        gbig["w_branch_a"][l] = _mm(sv["ya"], dca, "tn", F32, "mm_d_branch_a")
        gbig["w_branch_b"][l] = _mm(sv["yb"], dcb, "tn", F32, "mm_d_branch_b")
        dya = _mm(dca, w["w_branch_a"], "nt", F32, "mm_dya")
        dyb = _mm(dcb, w["w_branch_b"], "nt", F32, "mm_dyb")

        dya, delta_a = _attn_bwd_prep(dya, sv["ya"], "attn_a_bwd_prep")
        dyb, delta_b = _attn_bwd_prep(dyb, sv["yb"], "attn_b_bwd_prep")

        dqa, dka, dva = [], [], []
        for (blk, d), bias, bk in zip(DILATED, bias_a, buckets_a):
            dq_, dk_, dv_, db_ = _attn_bwd(sv["qa"], sv["ka"], sv["va"], dya, sv["lse_a"], delta_a, bias, None, blk, d,
                                           f"attn_a{d}_bwd")
            dqa.append(dq_)
            dka.append(dk_)
            dva.append(dv_)
            dtable_a = dtable_a + _table_grad(db_.reshape(N_HEADS, blk, 3 * blk), bk, "table_grad_a")
        dqb, dkb, dvb, db_, dsink = _attn_bwd(sv["qb"], sv["kb"], sv["vb"], dyb, sv["lse_b"], delta_b, bias_b, sink,
                                              BLK_B, 1, "attn_b_bwd")
        dtable_b = dtable_b + _table_grad(db_.reshape(N_HEADS, BLK_B, 3 * BLK_B), buckets_b, "table_grad_b")
        gsmall["sink_b"][l] = dsink.reshape(N_HEADS, BLK_B).sum(axis=1)

        dproj, pqa, pka, pqb, pkb = _qknorm_bwd(sv["proj"], gqa, gka, gqb, gkb, dqa, dka, dva, dqb, dkb, dvb, dgab)
        gsmall["qnorm_a_g"][l] = pqa.reshape(N_HEADS, HEAD_DIM).sum(0)
        gsmall["knorm_a_g"][l] = pka.reshape(N_HEADS, HEAD_DIM).sum(0)
        gsmall["qnorm_b_g"][l] = pqb.reshape(N_HEADS, HEAD_DIM).sum(0)
        gsmall["knorm_b_g"][l] = pkb.reshape(N_KV_B, HEAD_DIM).sum(0)
        gbig["w_in"][l] = _mm(sv["h"], dproj, "tn", F32, "mm_d_in")
        dh = _mm(dproj, w["w_in"], "nt", F32, "mm_dh")
        dx, gsmall["norm_mix_g"][l] = _rms_bwd(sv["x0"], g_mix, dh, dx, "rms_mix_bwd")
        gsmall["norm_mix_g"][l] = gsmall["norm_mix_g"][l][0]
        gsmall["norm_ffn_g"][l] = gsmall["norm_ffn_g"][l][0]
        gsmall["norm_ple_g"][l] = gsmall["norm_ple_g"][l][0]

    gbig = {n: jnp.stack(v) for n, v in gbig.items()}
    gsmall = {n: jnp.stack(v) for n, v in gsmall.items()}
    gsmall["rel_table"] = jnp.concatenate([dtable_a, dtable_b], axis=0).T
    return loss, dx, gbig, gsmall


def _place():
    return lax.axis_index("x"), lax.axis_index("y"), lax.axis_index("c")


def _flip(v, bit):
    return 1 - v if bit else v


CHIP_RELATIONS = ((0, 1), (1, 0), (1, 1))
ANY = pl.BlockSpec(memory_space=pl.ANY)


def _allgather_flat(wflat):
    rows, cols = wflat.shape
    half = rows // 2

    def body(w_ref, out_ref, send_sems, recv_sems):
        x, y, c = _place()

        def part(px, py, pc):
            return out_ref.at[2 * px + py, pl.ds(pc * half, half), :]

        def copy(k, block, to, src=None):
            return pltpu.make_async_remote_copy(
                src_ref=part(*block) if src is None else src, dst_ref=part(*block),
                send_sem=send_sems.at[k], recv_sem=recv_sems.at[k], device_id=to, device_id_type=MESH_ID)

        own = pltpu.make_async_remote_copy(
            src_ref=w_ref, dst_ref=out_ref.at[2 * x + y], send_sem=send_sems.at[6], recv_sem=recv_sems.at[6],
            device_id=(x, y, 1 - c), device_id_type=MESH_ID)
        own.start()
        chips = [(_flip(x, a), _flip(y, b)) for a, b in CHIP_RELATIONS]
        first = [copy(k, (x, y, c), (*chip, c), src=w_ref.at[pl.ds(c * half, half), :]) for k, chip in enumerate(chips)]
        for cp in first:
            cp.start()
        passed = [copy(3 + k, (*chip, c), (x, y, 1 - c)) for k, chip in enumerate(chips)]
        for k, chip in enumerate(chips):
            copy(k, (*chip, c), (x, y, c)).wait_recv()
            passed[k].start()
        for k, chip in enumerate(chips):
            copy(3 + k, (*chip, 1 - c), (x, y, c)).wait_recv()
        own.wait_recv()
        for cp in first + passed + [own]:
            cp.wait_send()

    return pl.pallas_call(
        body, out_shape=jax.ShapeDtypeStruct((N_CHIPS, rows, cols), wflat.dtype), in_specs=[ANY], out_specs=ANY,
        scratch_shapes=[pltpu.SemaphoreType.DMA((7,)), pltpu.SemaphoreType.DMA((7,))],
        name="allgather_weights")(wflat)


def _sibling_halves(gsend):
    n, rows, cols = gsend.shape
    half = rows // 2

    def body(g_ref, out_ref, send_sem, recv_sem):
        x, y, c = _place()
        cp = pltpu.make_async_remote_copy(
            src_ref=g_ref.at[:, pl.ds((1 - c) * half, half), :], dst_ref=out_ref, send_sem=send_sem, recv_sem=recv_sem,
            device_id=(x, y, 1 - c), device_id_type=MESH_ID)
        cp.start()
        cp.wait_recv()
        cp.wait_send()

    return pl.pallas_call(
        body, out_shape=jax.ShapeDtypeStruct((n, half, cols), gsend.dtype), in_specs=[ANY], out_specs=ANY,
        scratch_shapes=[pltpu.SemaphoreType.DMA, pltpu.SemaphoreType.DMA], name="rs_sibling_halves")(gsend)


def _chip_sums(gsend, sib, place):
    n, rows, cols = gsend.shape
    half = rows // 2
    tm = 512
    nblk = half // tm

    def body(s_ref, g_ref, sib_ref, o_ref):
        o_ref[0] = (g_ref[0].astype(F32) + sib_ref[0].astype(F32)).astype(o_ref.dtype)

    grid_spec = pltpu.PrefetchScalarGridSpec(
        num_scalar_prefetch=1, grid=(n, nblk),
        in_specs=[pl.BlockSpec((1, tm, cols), lambda k, i, s: (jnp.bitwise_xor(s[0], k), s[1] * nblk + i, 0)),
                  pl.BlockSpec((1, tm, cols), lambda k, i, s: (jnp.bitwise_xor(s[0], k), i, 0))],
        out_specs=pl.BlockSpec((1, tm, cols), lambda k, i, s: (k, i, 0)))
    return pl.pallas_call(
        body, out_shape=jax.ShapeDtypeStruct((n, half, cols), BF16), grid_spec=grid_spec,
        name="rs_chip_sums", compiler_params=_params("parallel", "parallel"))(place, gsend, sib)


def _exchange_chip_sums(tsend):
    n, half, cols = tsend.shape

    def body(t_ref, out_ref, send_sems, recv_sems):
        x, y, c = _place()
        cps = []
        for k, (a, b) in enumerate(CHIP_RELATIONS):
            cps.append(pltpu.make_async_remote_copy(
                src_ref=t_ref.at[k + 1], dst_ref=out_ref.at[k], send_sem=send_sems.at[k], recv_sem=recv_sems.at[k],
                device_id=(_flip(x, a), _flip(y, b), c), device_id_type=MESH_ID))
        for cp in cps:
            cp.start()
        for cp in cps:
            cp.wait_recv()
        for cp in cps:
            cp.wait_send()

    return pl.pallas_call(
        body, out_shape=jax.ShapeDtypeStruct((n - 1, half, cols), tsend.dtype), in_specs=[ANY], out_specs=ANY,
        scratch_shapes=[pltpu.SemaphoreType.DMA((3,)), pltpu.SemaphoreType.DMA((3,))], name="rs_exchange")(tsend)


def _final_sum(tsend, recv, place):
    n, half, cols = tsend.shape
    tm = 512
    nblk = half // tm

    def body(s_ref, t_ref, r_ref, o_ref):
        o_ref[...] = ((t_ref[0].astype(F32) + r_ref[0].astype(F32)) + r_ref[1].astype(F32)) + r_ref[2].astype(F32)

    grid_spec = pltpu.PrefetchScalarGridSpec(
        num_scalar_prefetch=1, grid=(nblk,),
        in_specs=[pl.BlockSpec((1, tm, cols), lambda i, s: (0, i, 0)), pl.BlockSpec((n - 1, tm, cols), lambda i, s: (0, i, 0))],
        out_specs=pl.BlockSpec((tm, cols), lambda i, s: (s[1] * nblk + i, 0)))
    return pl.pallas_call(
        body, out_shape=jax.ShapeDtypeStruct((2 * half, cols), F32), grid_spec=grid_spec, name="rs_final_sum",
        compiler_params=_params("parallel"))(place, tsend, recv)


def _join_halves(gfull):
    rows, cols = gfull.shape
    half = rows // 2

    def body(g_ref, out_ref, send_sem, recv_sem):
        x, y, c = _place()

        def copy(pc):
            return pltpu.make_async_remote_copy(
                src_ref=g_ref.at[pl.ds(pc * half, half), :], dst_ref=out_ref.at[pl.ds(pc * half, half), :],
                send_sem=send_sem, recv_sem=recv_sem, device_id=(x, y, 1 - c), device_id_type=MESH_ID)

        mine = copy(c)
        mine.start()
        copy(1 - c).wait_recv()
        mine.wait_send()

    return pl.pallas_call(
        body, out_shape=jax.ShapeDtypeStruct((rows, cols), gfull.dtype), in_specs=[ANY], out_specs=ANY,
        input_output_aliases={0: 0}, scratch_shapes=[pltpu.SemaphoreType.DMA, pltpu.SemaphoreType.DMA],
        name="rs_join_halves")(gfull)


def _allreduce_small(v):
    rows, cols = v.shape

    def body(v_ref, out_ref, buf, send_sems, recv_sems):
        x, y, c = _place()
        cps = []
        for k in range(1, 8):
            peer = (_flip(x, (k >> 2) & 1), _flip(y, (k >> 1) & 1), _flip(c, k & 1))
            cps.append(pltpu.make_async_remote_copy(
                src_ref=v_ref, dst_ref=buf.at[k - 1], send_sem=send_sems.at[k - 1], recv_sem=recv_sems.at[k - 1],
                device_id=peer, device_id_type=MESH_ID))
        for cp in cps:
            cp.start()
        for cp in cps:
            cp.wait_recv()
        for cp in cps:
            cp.wait_send()
        t0 = v_ref[...] + buf[0]
        t1 = buf[1] + buf[2]
        t2 = buf[3] + buf[4]
        t3 = buf[5] + buf[6]
        out_ref[...] = (t0 + t1) + (t2 + t3)

    vm = pl.BlockSpec(memory_space=pltpu.VMEM)
    return pl.pallas_call(
        body, out_shape=jax.ShapeDtypeStruct((rows, cols), F32), in_specs=[vm], out_specs=vm,
        scratch_shapes=[pltpu.VMEM((7, rows, cols), F32), pltpu.SemaphoreType.DMA((7,)), pltpu.SemaphoreType.DMA((7,))],
        name="allreduce_small")(v)


def _pad_rows(a, rows):
    return jnp.concatenate([a, jnp.zeros((rows - a.shape[0], a.shape[1]), a.dtype)], axis=0)


def _pack_shards(shards, dtype):
    flat = jnp.concatenate([shards[n].astype(dtype).reshape(-1, FLAT_COLS) for n, _, _ in BIG], axis=0)
    return _pad_rows(flat, FLAT_ROWS)


def _unpack_shards(flat, dtype):
    out, off = {}, 0
    for n, (k, m), ax in BIG:
        shape = (DEPTH, k // N_CHIPS, m) if ax == 0 else (DEPTH, k, m // N_CHIPS)
        rows = math.prod(shape) // FLAT_COLS
        out[n] = flat[off:off + rows].reshape(shape).astype(dtype)
        off += rows
    return out


def _unpack_full(wall):
    per_chip = [_unpack_shards(wall[j], wall.dtype) for j in range(N_CHIPS)]
    return {n: jnp.concatenate([pc[n] for pc in per_chip], axis=1 + ax) for n, _, ax in BIG}


def _pack_grads(gbig):
    chunks = []
    for j in range(N_CHIPS):
        shards = {}
        for n, (k, m), ax in BIG:
            size = (k if ax == 0 else m) // N_CHIPS
            shards[n] = lax.slice_in_dim(gbig[n], j * size, (j + 1) * size, axis=1 + ax)
        chunks.append(_pack_shards(shards, BF16))
    return jnp.stack(chunks)


SMALL_SHAPES = {"rel_table": (NUM_BUCKETS, 2 * N_HEADS), "norm_mix_g": (DEPTH, D_MODEL), "qnorm_a_g": (DEPTH, HEAD_DIM),
                "knorm_a_g": (DEPTH, HEAD_DIM), "qnorm_b_g": (DEPTH, HEAD_DIM), "knorm_b_g": (DEPTH, HEAD_DIM),
                "sink_b": (DEPTH, N_HEADS), "norm_ffn_g": (DEPTH, D_MODEL), "norm_ple_g": (DEPTH, D_MODEL)}


def _pack_small(vals):
    flat = jnp.concatenate([vals[n].astype(F32).reshape(-1) for n in SMALL])
    flat = jnp.concatenate([flat, jnp.zeros((SMALL_ROWS * LANES - flat.shape[0],), F32)])
    return flat.reshape(SMALL_ROWS, LANES)


def _unpack_small(packed):
    flat, out, off = packed.reshape(-1), {}, 0
    for n in SMALL:
        size = math.prod(SMALL_SHAPES[n])
        out[n] = flat[off:off + size].reshape(SMALL_SHAPES[n])
        off += size
    return out


def _adamw(w, g, m, v, name):
    c1 = 1.0 - ADAM_B1 ** ADAM_STEP
    c2 = 1.0 - ADAM_B2 ** ADAM_STEP

    def fn(w_, g_, m_, v_):
        m_new = ADAM_B1 * m_ + (1.0 - ADAM_B1) * g_
        v_new = ADAM_B2 * v_ + (1.0 - ADAM_B2) * (g_ * g_)
        delta = -ADAM_LR * ((m_new / c1) / (jnp.sqrt(v_new / c2) + ADAM_EPS) + ADAM_WD * w_)
        return delta, m_new, v_new

    width = w.shape[1]
    return _ew(fn, [(w, 0), (g, 0), (m, 0), (v, 0)], [F32, F32, F32], width=width, bw=width, name=name)


def kernel(x, p, rel_table, norm_mix_g, w_in, qnorm_a_g, knorm_a_g, qnorm_b_g, knorm_b_g, sink_b, w_branch_a, w_branch_b, w_out, norm_ffn_g, w_ffn_gate, w_ffn_up, w_ffn_down, norm_ple_g, w_ple_gate, w_ple_proj, loss_target, m_rel_table, m_norm_mix_g, m_w_in, m_qnorm_a_g, m_knorm_a_g, m_qnorm_b_g, m_knorm_b_g, m_sink_b, m_w_branch_a, m_w_branch_b, m_w_out, m_norm_ffn_g, m_w_ffn_gate, m_w_ffn_up, m_w_ffn_down, m_norm_ple_g, m_w_ple_gate, m_w_ple_proj, v_rel_table, v_norm_mix_g, v_w_in, v_qnorm_a_g, v_knorm_a_g, v_qnorm_b_g, v_knorm_b_g, v_sink_b, v_w_branch_a, v_w_branch_b, v_w_out, v_norm_ffn_g, v_w_ffn_gate, v_w_ffn_up, v_w_ffn_down, v_norm_ple_g, v_w_ple_gate, v_w_ple_proj):
    given = dict(locals())
    weights = {n: given[n] for n in WEIGHTS}
    moments_m = {n: given["m_" + n] for n in WEIGHTS}
    moments_v = {n: given["v_" + n] for n in WEIGHTS}
    xi, yi, ci = _place()
    place = jnp.stack([2 * xi + yi, ci]).astype(jnp.int32)

    wall = _allgather_flat(_pack_shards(weights, BF16))
    big = _unpack_full(wall)
    small = {n: weights[n] for n in SMALL}

    loss, dx, gbig, gsmall = _local_step(x[0], p[:, 0], loss_target[0], big, small)

    gsend = _pack_grads(gbig)
    tsend = _chip_sums(gsend, _sibling_halves(gsend), place)
    gflat = _join_halves(_final_sum(tsend, _exchange_chip_sums(tsend), place))
    grads = _unpack_shards(gflat, F32)
    grads.update(_unpack_small(_allreduce_small(_pack_small(gsmall))))

    delta, new_m, new_v = {}, {}, {}
    for n, _, _ in BIG:
        shape = weights[n].shape
        two_d = lambda a: a.reshape(shape[0] * shape[1], shape[2])
        d_, m_, v_ = _adamw(two_d(weights[n]), two_d(grads[n]), two_d(moments_m[n]), two_d(moments_v[n]), "adamw_" + n)
        delta[n], new_m[n], new_v[n] = d_.reshape(shape), m_.reshape(shape), v_.reshape(shape)
    d_, m_, v_ = _adamw(_pack_small(weights), _pack_small(grads), _pack_small(moments_m), _pack_small(moments_v),
                        "adamw_small")
    delta.update(_unpack_small(d_))
    new_m.update(_unpack_small(m_))
    new_v.update(_unpack_small(v_))

    loss = lax.psum(loss, ("x", "y", "c"))
    return (loss, dx[None], *[grads[n] for n in WEIGHTS], *[delta[n] for n in WEIGHTS],
            *[new_m[n] for n in WEIGHTS], *[new_v[n] for n in WEIGHTS])
```

```python
import functools
import math

import jax
import jax.numpy as jnp
from jax import lax
from jax.experimental import pallas as pl
from jax.experimental.pallas import tpu as pltpu

F32 = jnp.float32
BF16 = jnp.bfloat16
MESH_ID = pl.DeviceIdType.MESH

SEQ = 2048
D_MODEL = 1024
DEPTH = 2
HEAD_DIM = 64
N_HEADS = 8
WIDTH = N_HEADS * HEAD_DIM
N_PAIRS = 4
ITEMS = 4
N_KV_B = 2
PLE_DIM = 256
D_FF = 2816
D_IN = 4352
OFF_QA, OFF_KA, OFF_VA, OFF_QB, OFF_KB, OFF_VB, OFF_GA, OFF_GB = 0, 512, 1024, 1536, 2048, 2176, 2304, 3328
DILATED = ((64, 1), (64, 4), (64, 16))
BLK_B = 128
NUM_BUCKETS = 32
MAX_DISTANCE = 1024
RMS_EPS = 1e-6
NEG_INF = -1e30
LANES = 128
VMEM_LIMIT = 48 * 1024 * 1024

ADAM_LR, ADAM_B1, ADAM_B2, ADAM_EPS, ADAM_WD, ADAM_STEP = 0.001, 0.9, 0.999, 1e-08, 0.01, 10

BIG = (
    ("w_in", (D_MODEL, D_IN), 1),
    ("w_branch_a", (WIDTH, D_MODEL), 1),
    ("w_branch_b", (WIDTH, D_MODEL), 1),
    ("w_out", (D_MODEL, D_MODEL), 0),
    ("w_ffn_gate", (D_MODEL, D_FF), 1),
    ("w_ffn_up", (D_MODEL, D_FF), 1),
    ("w_ffn_down", (D_FF, D_MODEL), 0),
    ("w_ple_gate", (D_MODEL, D_MODEL), 0),
    ("w_ple_proj", (PLE_DIM, D_MODEL), 1),
)
SMALL = ("rel_table", "norm_mix_g", "qnorm_a_g", "knorm_a_g", "qnorm_b_g", "knorm_b_g", "sink_b",
         "norm_ffn_g", "norm_ple_g")
WEIGHTS = ("rel_table", "norm_mix_g", "w_in", "qnorm_a_g", "knorm_a_g", "qnorm_b_g", "knorm_b_g", "sink_b",
           "w_branch_a", "w_branch_b", "w_out", "norm_ffn_g", "w_ffn_gate", "w_ffn_up", "w_ffn_down",
           "norm_ple_g", "w_ple_gate", "w_ple_proj")
N_CHIPS = 4
SMALL_ROWS = 64


def _params(*sem):
    return pltpu.CompilerParams(dimension_semantics=sem, vmem_limit_bytes=VMEM_LIMIT)


def _pick(dim, target):
    for t in (target, 512, 256, 128, 64, 32, 16, 8):
        if t <= target and dim % t == 0:
            return t
    return dim


def _mm(a, b, mode, out_dtype, name, res=None, tm=512, tn=512, out_layer=None):
    b, b_layer = b if isinstance(b, tuple) else (b, None)
    b_shape = b.shape if b_layer is None else b.shape[1:]
    if mode == "nn":
        (m, k), (_, n) = a.shape, b_shape
    elif mode == "nt":
        (m, k), (n, _) = a.shape, b_shape
    else:
        (k, m), (_, n) = a.shape, b_shape
    tm, tn = _pick(m, tm), _pick(n, tn)
    a_spec = pl.BlockSpec((k, tm), lambda i, j: (0, i)) if mode == "tn" else pl.BlockSpec((tm, k), lambda i, j: (i, 0))
    b_blk, b_idx = ((tn, k), lambda i, j: (j, 0)) if mode == "nt" else ((k, tn), lambda i, j: (0, j))
    if b_layer is None:
        b_spec = pl.BlockSpec(b_blk, b_idx)
    else:
        b_spec = pl.BlockSpec((None,) + b_blk, lambda i, j: (b_layer,) + b_idx(i, j))
    dims = {"nn": (((1,), (0,)), ((), ())), "nt": (((1,), (1,)), ((), ())), "tn": (((0,), (0,)), ((), ()))}[mode]
    has_res = res is not None
    n_in = 3 if has_res else 2

    def body(*refs):
        a_ref, b_ref = refs[0], refs[1]
        o_ref = refs[-1]
        acc = lax.dot_general(a_ref[...].astype(BF16), b_ref[...].astype(BF16), dims, preferred_element_type=F32)
        if has_res:
            acc = acc + refs[2][...]
        o_ref[...] = acc.astype(out_dtype)

    in_specs = [a_spec, b_spec]
    args = [a, b]
    if has_res:
        in_specs.append(pl.BlockSpec((tm, tn), lambda i, j: (i, j)))
        args.append(res)
    aliases = {}
    if out_layer is None:
        out_shape = jax.ShapeDtypeStruct((m, n), out_dtype)
        out_spec = pl.BlockSpec((tm, tn), lambda i, j: (i, j))
    else:
        layer, buf = out_layer
        out_shape = jax.ShapeDtypeStruct((DEPTH, m, n), out_dtype)
        out_spec = pl.BlockSpec((None, tm, tn), lambda i, j: (layer, i, j))
        if buf is not None:
            in_specs.append(ANY)
            args.append(buf)
            aliases = {n_in: 0}
    return pl.pallas_call(
        body, out_shape=out_shape, grid=(m // tm, n // tn), in_specs=in_specs, out_specs=out_spec,
        input_output_aliases=aliases, name=name, compiler_params=_params("parallel", "parallel"))(*args)


def _ew(fn, ins, out_dtypes, *, width, bw, name, vecs=(), tm=256, rows=None, row_offs=None):
    rows = ins[0][0].shape[0] if rows is None else rows
    tm = _pick(rows, tm)
    row_offs = [0] * len(ins) if row_offs is None else row_offs
    assert all(r % tm == 0 for r in row_offs)
    n_in = len(ins) + len(vecs)

    def col_map(row_blocks, off_blocks):
        return lambda i, j: (row_blocks + i, off_blocks + j)

    in_specs = [pl.BlockSpec((tm, bw), col_map(r // tm, off // bw)) for (_, off), r in zip(ins, row_offs)]
    in_specs += [pl.BlockSpec((1, bw), lambda i, j: (0, j)) for _ in vecs]

    def body(*refs):
        outs = fn(*[r[...] for r in refs[:n_in]])
        for r, o in zip(refs[n_in:], outs):
            r[...] = o.astype(r.dtype)

    return pl.pallas_call(
        body, out_shape=[jax.ShapeDtypeStruct((rows, width), dt) for dt in out_dtypes],
        grid=(rows // tm, width // bw), in_specs=in_specs,
        out_specs=[pl.BlockSpec((tm, bw), lambda i, j: (i, j)) for _ in out_dtypes],
        name=name, compiler_params=_params("parallel", "parallel"))(*[a for a, _ in ins], *vecs)


def _sigmoid(x):
    return 1.0 / (1.0 + jnp.exp(-x))


def _seg_sum(v):
    outs = []
    for k in range(v.shape[1] // LANES):
        vp = v[:, k * LANES:(k + 1) * LANES]
        left = lax.broadcasted_iota(jnp.int32, vp.shape, 1) < HEAD_DIM
        sl = jnp.sum(jnp.where(left, vp, 0.0), axis=-1, keepdims=True)
        sr = jnp.sum(jnp.where(left, 0.0, vp), axis=-1, keepdims=True)
        outs.append(jnp.where(left, sl, sr))
    return outs[0] if len(outs) == 1 else jnp.concatenate(outs, axis=1)


def _seg_rstd(x):
    return lax.rsqrt(_seg_sum(x * x) * (1.0 / HEAD_DIM) + RMS_EPS)


def _rms_fwd(x, g, name):
    rows, d = x.shape
    tm = 256

    def body(x_ref, g_ref, h_ref):
        xv = x_ref[...]
        r = lax.rsqrt(jnp.mean(xv * xv, axis=-1, keepdims=True) + RMS_EPS)
        h_ref[...] = ((xv * r) * g_ref[...]).astype(BF16)

    return pl.pallas_call(
        body, out_shape=jax.ShapeDtypeStruct((rows, d), BF16), grid=(rows // tm,),
        in_specs=[pl.BlockSpec((tm, d), lambda i: (i, 0)), pl.BlockSpec((1, d), lambda i: (0, 0))],
        out_specs=pl.BlockSpec((tm, d), lambda i: (i, 0)), name=name, compiler_params=_params("parallel"))(x, g)


def _rms_bwd(x, g, dh, dres, name):
    rows, d = x.shape
    tm = 256

    def body(x_ref, g_ref, dh_ref, dres_ref, dx_ref, dg_ref):
        xv = x_ref[...]
        r = lax.rsqrt(jnp.mean(xv * xv, axis=-1, keepdims=True) + RMS_EPS)
        xh = xv * r
        dhv = dh_ref[...]
        dxh = dhv * g_ref[...]
        dx_ref[...] = dres_ref[...] + r * (dxh - xh * jnp.mean(dxh * xh, axis=-1, keepdims=True))
        part = jnp.sum(dhv * xh, axis=0, keepdims=True)

        @pl.when(pl.program_id(0) == 0)
        def _():
            dg_ref[...] = part

        @pl.when(pl.program_id(0) > 0)
        def _():
            dg_ref[...] += part

    row = pl.BlockSpec((tm, d), lambda i: (i, 0))
    vec = pl.BlockSpec((1, d), lambda i: (0, 0))
    return pl.pallas_call(
        body, out_shape=[jax.ShapeDtypeStruct((rows, d), F32), jax.ShapeDtypeStruct((1, d), F32)],
        grid=(rows // tm,), in_specs=[row, vec, row, row], out_specs=[row, vec],
        name=name, compiler_params=_params("arbitrary"))(x, g, dh, dres)


def _loss_grad(y, t):
    rows, d = y.shape
    tm = 256

    def body(y_ref, t_ref, dy_ref, l_ref):
        e = y_ref[...] - t_ref[...]
        dy_ref[...] = e * (1.0 / d)
        part = jnp.zeros((1, LANES), F32) + jnp.sum(e * e) * (0.5 / d)

        @pl.when(pl.program_id(0) == 0)
        def _():
            l_ref[...] = part

        @pl.when(pl.program_id(0) > 0)
        def _():
            l_ref[...] += part

    row = pl.BlockSpec((tm, d), lambda i: (i, 0))
    return pl.pallas_call(
        body, out_shape=[jax.ShapeDtypeStruct((rows, d), F32), jax.ShapeDtypeStruct((1, LANES), F32)],
        grid=(rows // tm,), in_specs=[row, row], out_specs=[row, pl.BlockSpec((1, LANES), lambda i: (0, 0))],
        name="loss_grad", compiler_params=_params("arbitrary"))(y, t)


def _put_pairs(ref, val):
    for hp in range(N_PAIRS):
        ref[hp] = val[:, hp * LANES:(hp + 1) * LANES].astype(ref.dtype)


def _get_pairs(ref):
    return jnp.concatenate([ref[hp] for hp in range(N_PAIRS)], axis=1)


def _swap_halves(v):
    return pltpu.roll(v, HEAD_DIM, axis=1)


def _expand_kv(kv):
    left = lax.broadcasted_iota(jnp.int32, kv.shape, 1) < HEAD_DIM
    sw = _swap_halves(kv)
    h0 = jnp.where(left, kv, sw)
    h1 = jnp.where(left, sw, kv)
    return jnp.concatenate([h0, h0, h1, h1], axis=1)


def _reduce_kv(dkv):
    left = lax.broadcasted_iota(jnp.int32, (dkv.shape[0], LANES), 1) < HEAD_DIM
    t = dkv[:, 0:LANES] + dkv[:, LANES:2 * LANES]
    u = dkv[:, 2 * LANES:3 * LANES] + dkv[:, 3 * LANES:4 * LANES]
    t = t + _swap_halves(t)
    u = u + _swap_halves(u)
    return jnp.where(left, t, u)


def _qknorm_fwd(proj, gqa, gka, gqb, gkb):
    rows = proj.shape[0]
    tm = 256

    def body(qa_ref, ka_ref, va_ref, qb_ref, kb_ref, vb_ref, gqa_ref, gka_ref, gqb_ref, gkb_ref,
             oqa, oka, ova, oqb, okb, ovb):
        for src, g_ref, dst in ((qa_ref, gqa_ref, oqa), (ka_ref, gka_ref, oka), (qb_ref, gqb_ref, oqb)):
            xv = src[...]
            _put_pairs(dst, (xv * _seg_rstd(xv)) * g_ref[...])
        _put_pairs(ova, va_ref[...])
        kv = kb_ref[...]
        _put_pairs(okb, _expand_kv((kv * _seg_rstd(kv)) * gkb_ref[...]))
        _put_pairs(ovb, _expand_kv(vb_ref[...]))

    def win(width, off):
        return pl.BlockSpec((tm, width), lambda i: (i, off // width))

    vec = lambda w: pl.BlockSpec((1, w), lambda i: (0, 0))
    out = pl.BlockSpec((N_PAIRS, tm, LANES), lambda i: (0, i, 0))
    return pl.pallas_call(
        body, out_shape=[jax.ShapeDtypeStruct((N_PAIRS, rows, LANES), F32)] * 6, grid=(rows // tm,),
        in_specs=[win(WIDTH, OFF_QA), win(WIDTH, OFF_KA), win(WIDTH, OFF_VA), win(WIDTH, OFF_QB),
                  win(LANES, OFF_KB), win(LANES, OFF_VB), vec(WIDTH), vec(WIDTH), vec(WIDTH), vec(LANES)],
        out_specs=[out] * 6, name="qknorm_fwd", compiler_params=_params("parallel"))(
            proj, proj, proj, proj, proj, proj, gqa, gka, gqb, gkb)


def _norm_bwd(xv, g, dy):
    r = _seg_rstd(xv)
    xh = xv * r
    dxh = dy * g
    dx = r * (dxh - xh * (_seg_sum(dxh * xh) * (1.0 / HEAD_DIM)))
    return dx, jnp.sum(dy * xh, axis=0, keepdims=True)


def _qknorm_bwd(proj, gqa, gka, gqb, gkb, dqa, dka, dva, dqb, dkb, dvb, dgab):
    rows = proj.shape[0]
    tm = 256
    n_a = len(dqa)

    def body(*refs):
        qa_ref, ka_ref, qb_ref, kb_ref, gqa_ref, gka_ref, gqb_ref, gkb_ref = refs[:8]
        pos = 8
        dqa_refs, dka_refs, dva_refs = refs[pos:pos + n_a], refs[pos + n_a:pos + 2 * n_a], refs[pos + 2 * n_a:pos + 3 * n_a]
        pos += 3 * n_a
        dqb_ref, dkb_ref, dvb_ref, dgab_ref = refs[pos:pos + 4]
        dproj_ref, ogqa, ogka, ogqb, ogkb = refs[pos + 4:]

        def total(rs):
            acc = _get_pairs(rs[0])
            for r in rs[1:]:
                acc = acc + _get_pairs(r)
            return acc

        dx_qa, p_qa = _norm_bwd(qa_ref[...], gqa_ref[...], total(dqa_refs))
        dx_ka, p_ka = _norm_bwd(ka_ref[...], gka_ref[...], total(dka_refs))
        dx_qb, p_qb = _norm_bwd(qb_ref[...], gqb_ref[...], _get_pairs(dqb_ref))
        dx_kb, p_kb = _norm_bwd(kb_ref[...], gkb_ref[...], _reduce_kv(_get_pairs(dkb_ref)))
        dproj_ref[:, OFF_QA:OFF_QA + WIDTH] = dx_qa.astype(BF16)
        dproj_ref[:, OFF_KA:OFF_KA + WIDTH] = dx_ka.astype(BF16)
        dproj_ref[:, OFF_VA:OFF_VA + WIDTH] = total(dva_refs).astype(BF16)
        dproj_ref[:, OFF_QB:OFF_QB + WIDTH] = dx_qb.astype(BF16)
        dproj_ref[:, OFF_KB:OFF_KB + LANES] = dx_kb.astype(BF16)
        dproj_ref[:, OFF_VB:OFF_VB + LANES] = _reduce_kv(_get_pairs(dvb_ref)).astype(BF16)
        dproj_ref[:, OFF_GA:D_IN] = dgab_ref[...]
        first = pl.program_id(0) == 0
        for o_ref, part in ((ogqa, p_qa), (ogka, p_ka), (ogqb, p_qb), (ogkb, p_kb)):
            @pl.when(first)
            def _(o_ref=o_ref, part=part):
                o_ref[...] = part

            @pl.when(jnp.logical_not(first))
            def _(o_ref=o_ref, part=part):
                o_ref[...] += part

    def win(width, off):
        return pl.BlockSpec((tm, width), lambda i: (i, off // width))

    vec = lambda w: pl.BlockSpec((1, w), lambda i: (0, 0))
    row = lambda w: pl.BlockSpec((tm, w), lambda i: (i, 0))
    in_specs = [win(WIDTH, OFF_QA), win(WIDTH, OFF_KA), win(WIDTH, OFF_QB), win(LANES, OFF_KB),
                vec(WIDTH), vec(WIDTH), vec(WIDTH), vec(LANES)]
    in_specs += [pl.BlockSpec((N_PAIRS, tm, LANES), lambda i: (0, i, 0))] * (3 * n_a + 3) + [row(2 * D_MODEL)]
    return pl.pallas_call(
        body,
        out_shape=[jax.ShapeDtypeStruct((rows, D_IN), BF16), jax.ShapeDtypeStruct((1, WIDTH), F32),
                   jax.ShapeDtypeStruct((1, WIDTH), F32), jax.ShapeDtypeStruct((1, WIDTH), F32),
                   jax.ShapeDtypeStruct((1, LANES), F32)],
        grid=(rows // tm,), in_specs=in_specs,
        out_specs=[row(D_IN), vec(WIDTH), vec(WIDTH), vec(WIDTH), vec(LANES)],
        name="qknorm_bwd", compiler_params=_params("arbitrary"))(
            proj, proj, proj, proj, gqa, gka, gqb, gkb, *dqa, *dka, *dva, dqb, dkb, dvb, dgab)


def _t5_bucket(rel):
    half_b = NUM_BUCKETS // 2
    max_exact = half_b // 2
    sign = jnp.where(rel > 0, half_b, 0)
    n = jnp.abs(rel)
    nf = jnp.maximum(n, 1).astype(F32)
    large = max_exact + (jnp.log(nf / max_exact) / math.log(MAX_DISTANCE / max_exact)
                         * (half_b - max_exact)).astype(jnp.int32)
    large = jnp.minimum(large, half_b - 1)
    return sign + jnp.where(n < max_exact, n, large)


def _band_buckets(blk, dilation):
    i = jnp.arange(blk, dtype=jnp.int32)[:, None]
    j = jnp.arange(3 * blk, dtype=jnp.int32)[None, :]
    rel = j - blk - i
    return jnp.where(jnp.abs(rel) <= blk, _t5_bucket(rel * dilation), -1)


def _bias_tiles(table, buckets, head_off, name):
    blk = buckets.shape[0]

    def body(tab_ref, bk_ref, o_ref):
        h = pl.program_id(0) + head_off
        bk = bk_ref[...]
        acc = jnp.full(bk.shape, NEG_INF, F32)
        for b in range(NUM_BUCKETS):
            acc = jnp.where(bk == b, tab_ref[b, h], acc)
        o_ref[0] = acc

    return pl.pallas_call(
        body, out_shape=jax.ShapeDtypeStruct((N_HEADS, blk, 3 * blk), F32), grid=(N_HEADS,),
        in_specs=[pl.BlockSpec(memory_space=pltpu.SMEM), pl.BlockSpec((blk, 3 * blk), lambda h: (0, 0))],
        out_specs=pl.BlockSpec((1, blk, 3 * blk), lambda h: (h, 0, 0)),
        name=name, compiler_params=_params("parallel"))(table, buckets)


def _table_grad(dbias, buckets, name):
    blk = buckets.shape[0]

    def body(db_ref, bk_ref, o_ref):
        bk = bk_ref[...]
        dbv = db_ref[0]
        lane = lax.broadcasted_iota(jnp.int32, (1, LANES), 1)
        acc = jnp.zeros((1, LANES), F32)
        for b in range(NUM_BUCKETS):
            acc = jnp.where(lane == b, jnp.sum(jnp.where(bk == b, dbv, 0.0)), acc)
        o_ref[0] = acc

    out = pl.pallas_call(
        body, out_shape=jax.ShapeDtypeStruct((N_HEADS, 1, LANES), F32), grid=(N_HEADS,),
        in_specs=[pl.BlockSpec((1, blk, 3 * blk), lambda h: (h, 0, 0)), pl.BlockSpec((blk, 3 * blk), lambda h: (0, 0))],
        out_specs=pl.BlockSpec((1, 1, LANES), lambda h: (h, 0, 0)),
        name=name, compiler_params=_params("parallel"))(dbias, buckets)
    return out[:, 0, :NUM_BUCKETS]


def _dot_nt(a, b):
    return lax.dot_general(a, b, (((1,), (1,)), ((), ())), preferred_element_type=F32)


def _stack_pair(x2, left):
    return jnp.concatenate([jnp.where(left, x2, 0.0), jnp.where(left, 0.0, x2)], axis=0).astype(BF16)


def _attn_geometry(blk, d):
    chunk = blk * ITEMS if d == 1 else blk * d
    groups = 1 if d == 1 else d // ITEMS
    halo = blk if d == 1 else chunk
    return chunk, groups, halo


def _item_rows(ref, j, r0, blk, d):
    if d == 1:
        return ref[j * blk:(j + 1) * blk, :]
    return ref[pl.ds(r0 + j, blk, stride=d), :]


def _item_penalty(t, nct, j, blk, d):
    first_ok, last_ok = t > 0, t < nct - 1
    if d == 1:
        first_ok = True if j > 0 else first_ok
        last_ok = True if j < ITEMS - 1 else last_ok
    col = lax.broadcasted_iota(jnp.int32, (1, 3 * blk), 1)
    ok = jnp.logical_and(jnp.logical_or(col >= blk, first_ok), jnp.logical_or(col < 2 * blk, last_ok))
    return jnp.where(ok, 0.0, NEG_INF).astype(F32)


def _attn_specs(seq, blk, d, step_of):
    chunk, _, halo = _attn_geometry(blk, d)
    per, last = chunk // halo, seq // halo - 1
    cur = pl.BlockSpec((None, chunk, LANES), lambda hp, t: (hp, step_of(t), 0))
    prev = pl.BlockSpec((None, halo, LANES), lambda hp, t: (hp, jnp.clip(step_of(t) * per - 1, 0, last), 0))
    nxt = pl.BlockSpec((None, halo, LANES), lambda hp, t: (hp, jnp.minimum((step_of(t) + 1) * per, last), 0))
    return cur, prev, nxt


def _attn_fwd(q, k, v, bias, sink, blk, d, name):
    _, seq, _ = q.shape
    chunk, groups, _ = _attn_geometry(blk, d)
    nct = seq // chunk
    has_sink = sink is not None
    scale = HEAD_DIM ** -0.5

    def body(*refs):
        q_ref, kp, kc, kn, vp, vc, vn, b_ref = refs[:8]
        s_ref = refs[8] if has_sink else None
        o_ref, l_ref = refs[-2], refs[-1]
        t = pl.program_id(1)
        left = lax.broadcasted_iota(jnp.int32, (1, LANES), 1) < HEAD_DIM
        bias2 = b_ref[...]
        if d == 1:
            kwin = jnp.concatenate([kp[...], kc[...], kn[...]], axis=0).astype(BF16)
            vwin = jnp.concatenate([vp[...], vc[...], vn[...]], axis=0).astype(BF16)

        def group(r0):
            scores, vcats = [], []
            for j in range(ITEMS):
                qs = _stack_pair(_item_rows(q_ref, j, r0, blk, d) * scale, left)
                if d == 1:
                    kcat, vcat = kwin[j * blk:(j + 3) * blk], vwin[j * blk:(j + 3) * blk]
                else:
                    kcat = jnp.concatenate([_item_rows(r, j, r0, blk, d) for r in (kp, kc, kn)], axis=0).astype(BF16)
                    vcat = jnp.concatenate([_item_rows(r, j, r0, blk, d) for r in (vp, vc, vn)], axis=0).astype(BF16)
                scores.append(_dot_nt(qs, kcat) + bias2 + _item_penalty(t, nct, j, blk, d))
                vcats.append(vcat)
            s = jnp.concatenate(scores, axis=0)
            m = jnp.max(s, axis=-1, keepdims=True)
            if has_sink:
                sk = jnp.concatenate([s_ref[...]] * ITEMS, axis=0)
                m = jnp.maximum(m, sk)
            p = jnp.exp(s - m)
            den = jnp.sum(p, axis=-1, keepdims=True)
            if has_sink:
                den = den + jnp.exp(sk - m)
            pn = (p * (1.0 / den)).astype(BF16)
            lse = m + jnp.log(den)
            for j in range(ITEMS):
                top, mid, bot = 2 * j * blk, (2 * j + 1) * blk, (2 * j + 2) * blk
                o2 = jnp.dot(pn[top:bot], vcats[j], preferred_element_type=F32)
                o_val = jnp.where(left, o2[:blk], o2[blk:])
                l_val = jnp.where(left, lse[top:mid], lse[mid:bot])
                if d == 1:
                    o_ref[j * blk:(j + 1) * blk, :] = o_val
                    l_ref[j * blk:(j + 1) * blk, :] = l_val
                else:
                    o_ref[pl.ds(r0 + j, blk, stride=d), :] = o_val
                    l_ref[pl.ds(r0 + j, blk, stride=d), :] = l_val

        if groups == 1:
            group(0)
        else:
            def step(g, carry):
                group(g * ITEMS)
                return carry

            lax.fori_loop(0, groups, step, 0)

    cur, prev, nxt = _attn_specs(seq, blk, d, lambda t: t)
    in_specs = [cur, prev, cur, nxt, prev, cur, nxt, pl.BlockSpec((2 * blk, 3 * blk), lambda hp, t: (hp, 0))]
    args = [q, k, k, k, v, v, v, bias]
    if has_sink:
        in_specs.append(pl.BlockSpec((2 * blk, 1), lambda hp, t: (hp, 0)))
        args.append(sink)
    return pl.pallas_call(
        body, out_shape=[jax.ShapeDtypeStruct(q.shape, F32)] * 2, grid=(N_PAIRS, nct),
        in_specs=in_specs, out_specs=[cur, cur], name=name, compiler_params=_params("parallel", "parallel"))(*args)


def _attn_bwd(q, k, v, do, lse, delta, bias, sink, blk, d, name):
    _, seq, _ = q.shape
    chunk, groups, halo = _attn_geometry(blk, d)
    nct = seq // chunk
    has_sink = sink is not None
    n_in = 12 if has_sink else 11
    scale = HEAD_DIM ** -0.5

    def body(*refs):
        q_ref, kp, kc, kn, vp, vc, vn, do_ref, l_ref, d_ref, b_ref = refs[:11]
        s_ref = refs[11] if has_sink else None
        dq_ref, dk_ref, dv_ref, db_ref = refs[n_in:n_in + 4]
        ds_ref = refs[n_in + 4] if has_sink else None
        wk, wv = refs[-2], refs[-1]
        t = pl.program_id(1)

        @pl.when(t == 0)
        def _():
            wk[...] = jnp.zeros_like(wk)
            wv[...] = jnp.zeros_like(wv)
            db_ref[...] = jnp.zeros_like(db_ref)
            if has_sink:
                ds_ref[...] = jnp.zeros_like(ds_ref)

        @pl.when(t > 0)
        def _():
            for w in (wk, wv):
                keep = w[chunk:2 * chunk + halo]
                w[0:chunk + halo] = keep
                w[chunk + halo:2 * chunk + halo] = jnp.zeros((chunk, LANES), F32)

        @pl.when(t < nct)
        def _():
            lane = lax.broadcasted_iota(jnp.int32, (1, LANES), 1)
            left = lane < HEAD_DIM
            bias2 = b_ref[...]
            if d == 1:
                kwin = jnp.concatenate([kp[...], kc[...], kn[...]], axis=0).astype(BF16)
                vwin = jnp.concatenate([vp[...], vc[...], vn[...]], axis=0).astype(BF16)

            def group(r0):
                qss, doss, kcats, scores, dps, lcols, dcols = [], [], [], [], [], [], []
                for j in range(ITEMS):
                    qs = _stack_pair(_item_rows(q_ref, j, r0, blk, d) * scale, left)
                    dos = _stack_pair(_item_rows(do_ref, j, r0, blk, d), left)
                    if d == 1:
                        kcat, vcat = kwin[j * blk:(j + 3) * blk], vwin[j * blk:(j + 3) * blk]
                    else:
                        kcat = jnp.concatenate([_item_rows(r, j, r0, blk, d) for r in (kp, kc, kn)], axis=0).astype(BF16)
                        vcat = jnp.concatenate([_item_rows(r, j, r0, blk, d) for r in (vp, vc, vn)], axis=0).astype(BF16)
                    l2, d2 = _item_rows(l_ref, j, r0, blk, d), _item_rows(d_ref, j, r0, blk, d)
                    lcols.append(jnp.max(jnp.where(left, l2, NEG_INF), axis=-1, keepdims=True))
                    lcols.append(jnp.max(jnp.where(left, NEG_INF, l2), axis=-1, keepdims=True))
                    dcols.append(jnp.sum(jnp.where(lane == 0, d2, 0.0), axis=-1, keepdims=True))
                    dcols.append(jnp.sum(jnp.where(lane == HEAD_DIM, d2, 0.0), axis=-1, keepdims=True))
                    scores.append(_dot_nt(qs, kcat) + bias2 + _item_penalty(t, nct, j, blk, d))
                    dps.append(_dot_nt(dos, vcat))
                    qss.append(qs)
                    doss.append(dos)
                    kcats.append(kcat)
                lcol = jnp.concatenate(lcols, axis=0)
                dcol = jnp.concatenate(dcols, axis=0)
                p = jnp.exp(jnp.concatenate(scores, axis=0) - lcol)
                ds = p * (jnp.concatenate(dps, axis=0) - dcol)
                if has_sink:
                    sgrad = dcol * jnp.exp(jnp.concatenate([s_ref[...]] * ITEMS, axis=0) - lcol)
                for j in range(ITEMS):
                    top, bot = 2 * j * blk, (2 * j + 2) * blk
                    dsj, pj = ds[top:bot], p[top:bot]
                    db_ref[...] += dsj
                    if has_sink:
                        ds_ref[...] -= sgrad[top:bot]
                    dq2 = jnp.dot(dsj.astype(BF16), kcats[j], preferred_element_type=F32) * scale
                    dq_val = jnp.where(left, dq2[:blk], dq2[blk:])
                    if d == 1:
                        dq_ref[j * blk:(j + 1) * blk, :] = dq_val
                    else:
                        dq_ref[pl.ds(r0 + j, blk, stride=d), :] = dq_val
                    dk_new = jnp.dot(jnp.transpose(dsj).astype(BF16), qss[j], preferred_element_type=F32)
                    dv_new = jnp.dot(jnp.transpose(pj).astype(BF16), doss[j], preferred_element_type=F32)
                    for w, new in ((wk, dk_new), (wv, dv_new)):
                        if d == 1:
                            w[chunk + (j - 1) * blk:chunk + (j + 2) * blk, :] += new
                        else:
                            for c in range(3):
                                w[pl.ds(c * chunk + r0 + j, blk, stride=d), :] += new[c * blk:(c + 1) * blk]

            if groups == 1:
                group(0)
            else:
                def step(g, carry):
                    group(g * ITEMS)
                    return carry

                lax.fori_loop(0, groups, step, 0)

        dk_ref[...] = wk[0:chunk]
        dv_ref[...] = wv[0:chunk]

    cur, prev, nxt = _attn_specs(seq, blk, d, lambda t: jnp.minimum(t, nct - 1))
    lag = pl.BlockSpec((None, chunk, LANES), lambda hp, t: (hp, jnp.maximum(t - 1, 0), 0))
    band = pl.BlockSpec((2 * blk, 3 * blk), lambda hp, t: (hp, 0))
    col = pl.BlockSpec((2 * blk, 1), lambda hp, t: (hp, 0))
    in_specs = [cur, prev, cur, nxt, prev, cur, nxt, cur, cur, cur, band]
    args = [q, k, k, k, v, v, v, do, lse, delta, bias]
    out_shape = [jax.ShapeDtypeStruct(q.shape, F32)] * 3 + [jax.ShapeDtypeStruct((N_HEADS * blk, 3 * blk), F32)]
    out_specs = [cur, lag, lag, band]
    if has_sink:
        in_specs.append(col)
        args.append(sink)
        out_shape.append(jax.ShapeDtypeStruct((N_HEADS * blk, 1), F32))
        out_specs.append(col)
    window = pltpu.VMEM((2 * chunk + halo, LANES), F32)
    return pl.pallas_call(
        body, out_shape=out_shape, grid=(N_PAIRS, nct + 1), in_specs=in_specs, out_specs=out_specs,
        scratch_shapes=[window, window], name=name, compiler_params=_params("arbitrary", "arbitrary"))(*args)


def _combine_patterns(outs, lses):
    _, rows, _ = outs[0].shape
    tm = 256
    n = len(outs)

    def body(*refs):
        o_refs, l_refs = refs[:n], refs[n:2 * n]
        y_ref, lse_ref = refs[2 * n], refs[2 * n + 1]
        for hp in range(N_PAIRS):
            ls = [r[hp] for r in l_refs]
            m = functools.reduce(jnp.maximum, ls)
            es = [jnp.exp(l - m) for l in ls]
            den = functools.reduce(lambda a, b: a + b, es)
            num = functools.reduce(lambda a, b: a + b, [e * r[hp] for e, r in zip(es, o_refs)])
            y_ref[:, hp * LANES:(hp + 1) * LANES] = num / den
            lse_ref[hp] = m + jnp.log(den)

    pm = pl.BlockSpec((N_PAIRS, tm, LANES), lambda i: (0, i, 0))
    return pl.pallas_call(
        body, out_shape=[jax.ShapeDtypeStruct((rows, WIDTH), F32), jax.ShapeDtypeStruct((N_PAIRS, rows, LANES), F32)],
        grid=(rows // tm,), in_specs=[pm] * (2 * n), out_specs=[pl.BlockSpec((tm, WIDTH), lambda i: (i, 0)), pm],
        name="combine_a", compiler_params=_params("parallel"))(*outs, *lses)


def _pairs_to_tokens(a):
    _, rows, _ = a.shape
    tm = 256

    def body(a_ref, o_ref):
        o_ref[...] = _get_pairs(a_ref)

    return pl.pallas_call(
        body, out_shape=jax.ShapeDtypeStruct((rows, WIDTH), a.dtype), grid=(rows // tm,),
        in_specs=[pl.BlockSpec((N_PAIRS, tm, LANES), lambda i: (0, i, 0))],
        out_specs=pl.BlockSpec((tm, WIDTH), lambda i: (i, 0)), name="pairs_to_tokens",
        compiler_params=_params("parallel"))(a)


def _attn_bwd_prep(dy, y, name):
    rows = dy.shape[0]
    tm = 256

    def body(dy_ref, y_ref, do_ref, dl_ref):
        dyv = dy_ref[...]
        _put_pairs(do_ref, dyv)
        _put_pairs(dl_ref, _seg_sum(dyv * y_ref[...]))

    tok = pl.BlockSpec((tm, WIDTH), lambda i: (i, 0))
    pm = pl.BlockSpec((N_PAIRS, tm, LANES), lambda i: (0, i, 0))
    return pl.pallas_call(
        body, out_shape=[jax.ShapeDtypeStruct((N_PAIRS, rows, LANES), F32)] * 2, grid=(rows // tm,),
        in_specs=[tok, tok], out_specs=[pm, pm], name=name, compiler_params=_params("parallel"))(dy, y)


def _tile_gain(g, reps):
    return jnp.tile(g[None, :], (1, reps))


def _local_step(x, p, target, big, small):
    rel_table = small["rel_table"]
    buckets_a = [_band_buckets(blk, d) for blk, d in DILATED]
    buckets_b = _band_buckets(BLK_B, 1)
    bias_a = [_bias_tiles(rel_table, bk, 0, "bias_a").reshape(N_HEADS * bk.shape[0], -1) for bk in buckets_a]
    bias_b = _bias_tiles(rel_table, buckets_b, N_HEADS, "bias_b").reshape(N_HEADS * BLK_B, -1)

    saved = []
    for l in range(DEPTH):
        w = {n: (big[n], l) for n, _, _ in BIG}
        g_mix, g_ffn, g_ple = (small[n][l][None, :] for n in ("norm_mix_g", "norm_ffn_g", "norm_ple_g"))
        gqa, gka, gqb = (_tile_gain(small[n][l], N_HEADS) for n in ("qnorm_a_g", "knorm_a_g", "qnorm_b_g"))
        gkb = _tile_gain(small["knorm_b_g"][l], N_KV_B)
        sink = jnp.repeat(small["sink_b"][l], BLK_B)[:, None]

        h = _rms_fwd(x, g_mix, "rms_mix")
        proj = _mm(h, w["w_in"], "nn", F32, "mm_in")
        qa, ka, va, qb, kb, vb = _qknorm_fwd(proj, gqa, gka, gqb, gkb)
        outs, lses = [], []
        for (blk, d), bias in zip(DILATED, bias_a):
            o, ls = _attn_fwd(qa, ka, va, bias, None, blk, d, f"attn_a{d}_fwd")
            outs.append(o)
            lses.append(ls)
        ya, lse_a = _combine_patterns(outs, lses)
        yb, lse_b = _attn_fwd(qb, kb, vb, bias_b, sink, BLK_B, 1, "attn_b_fwd")
        yb = _pairs_to_tokens(yb)
        ca = _mm(ya, w["w_branch_a"], "nn", F32, "mm_branch_a")
        cb = _mm(yb, w["w_branch_b"], "nn", F32, "mm_branch_b")

        def gate(ca_, cb_, ga_, gb_):
            return (_sigmoid(ga_) * ca_ + _sigmoid(gb_) * cb_,)

        (merged,) = _ew(gate, [(ca, 0), (cb, 0), (proj, OFF_GA), (proj, OFF_GB)], [BF16],
                        width=D_MODEL, bw=256, name="gate")
        x1 = _mm(merged, w["w_out"], "nn", F32, "mm_out", res=x)

        h2 = _rms_fwd(x1, g_ffn, "rms_ffn")
        a = _mm(h2, w["w_ffn_gate"], "nn", F32, "mm_ffn_gate")
        u = _mm(h2, w["w_ffn_up"], "nn", F32, "mm_ffn_up")

        def swiglu(a_, u_):
            return ((a_ * _sigmoid(a_)) * u_,)

        (hid,) = _ew(swiglu, [(a, 0), (u, 0)], [BF16], width=D_FF, bw=D_FF, name="swiglu")
        x2 = _mm(hid, w["w_ffn_down"], "nn", F32, "mm_ffn_down", res=x1)

        h3 = _rms_fwd(x2, g_ple, "rms_ple")
        z = _mm(h3, w["w_ple_gate"], "nn", F32, "mm_ple_gate")
        e = _mm(p[l], w["w_ple_proj"], "nn", F32, "mm_ple_proj")

        def ple(x2_, z_, e_):
            return (x2_ + _sigmoid(z_) * e_,)

        (x3,) = _ew(ple, [(x2, 0), (z, 0), (e, 0)], [F32], width=D_MODEL, bw=D_MODEL, name="ple")
        saved.append(dict(x0=x, h=h, proj=proj, qa=qa, ka=ka, va=va, qb=qb, kb=kb, vb=vb, ya=ya, lse_a=lse_a,
                          yb=yb, lse_b=lse_b, ca=ca, cb=cb, merged=merged, x1=x1, h2=h2, a=a, u=u, hid=hid,
                          x2=x2, h3=h3, z=z, e=e))
        x = x3

    dx, loss_acc = _loss_grad(x, target)
    loss = loss_acc[0, 0]

    gbig = {n: None for n, _, _ in BIG}
    gsmall = {n: [None] * DEPTH for n in SMALL if n != "rel_table"}
    dtable_a = jnp.zeros((N_HEADS, NUM_BUCKETS), F32)
    dtable_b = jnp.zeros((N_HEADS, NUM_BUCKETS), F32)

    for l in reversed(range(DEPTH)):
        sv = saved[l]
        w = {n: (big[n], l) for n, _, _ in BIG}
        g_mix, g_ffn, g_ple = (small[n][l][None, :] for n in ("norm_mix_g", "norm_ffn_g", "norm_ple_g"))
        gqa, gka, gqb = (_tile_gain(small[n][l], N_HEADS) for n in ("qnorm_a_g", "knorm_a_g", "qnorm_b_g"))
        gkb = _tile_gain(small["knorm_b_g"][l], N_KV_B)
        sink = jnp.repeat(small["sink_b"][l], BLK_B)[:, None]

        def ple_bwd(dx_, z_, e_):
            s = _sigmoid(z_)
            return dx_ * s, dx_ * e_ * (s * (1.0 - s))

        de, dz = _ew(ple_bwd, [(dx, 0), (sv["z"], 0), (sv["e"], 0)], [BF16, BF16], width=D_MODEL, bw=D_MODEL,
                     name="ple_bwd")
        gbig["w_ple_proj"] = _mm(p[l], de, "tn", F32, "mm_d_ple_proj", out_layer=(l, gbig["w_ple_proj"]))
        gbig["w_ple_gate"] = _mm(sv["h3"], dz, "tn", F32, "mm_d_ple_gate", out_layer=(l, gbig["w_ple_gate"]))
        dh3 = _mm(dz, w["w_ple_gate"], "nt", F32, "mm_dh3")
        dx, gsmall["norm_ple_g"][l] = _rms_bwd(sv["x2"], g_ple, dh3, dx, "rms_ple_bwd")

        dhid = _mm(dx, w["w_ffn_down"], "nt", F32, "mm_dhid")
        gbig["w_ffn_down"] = _mm(sv["hid"], dx, "tn", F32, "mm_d_ffn_down", out_layer=(l, gbig["w_ffn_down"]))

        def swiglu_bwd(a_, u_, dh_):
            s = _sigmoid(a_)
            return dh_ * u_ * (s * (1.0 + a_ * (1.0 - s))), dh_ * (a_ * s)

        da, du = _ew(swiglu_bwd, [(sv["a"], 0), (sv["u"], 0), (dhid, 0)], [BF16, BF16], width=D_FF, bw=D_FF,
                     name="swiglu_bwd")
        gbig["w_ffn_gate"] = _mm(sv["h2"], da, "tn", F32, "mm_d_ffn_gate", out_layer=(l, gbig["w_ffn_gate"]))
        gbig["w_ffn_up"] = _mm(sv["h2"], du, "tn", F32, "mm_d_ffn_up", out_layer=(l, gbig["w_ffn_up"]))
        dh2 = _mm(da, w["w_ffn_gate"], "nt", F32, "mm_dh2_gate")
        dh2 = _mm(du, w["w_ffn_up"], "nt", F32, "mm_dh2_up", res=dh2)
        dx, gsmall["norm_ffn_g"][l] = _rms_bwd(sv["x1"], g_ffn, dh2, dx, "rms_ffn_bwd")

        dmerged = _mm(dx, w["w_out"], "nt", F32, "mm_dmerged")
        gbig["w_out"] = _mm(sv["merged"], dx, "tn", F32, "mm_d_out", out_layer=(l, gbig["w_out"]))

        def gate_bwd(dm_, ca_, cb_, ga_, gb_):
            sa, sb = _sigmoid(ga_), _sigmoid(gb_)
            return dm_ * sa, dm_ * sb, dm_ * ca_ * (sa * (1.0 - sa)), dm_ * cb_ * (sb * (1.0 - sb))

        dca, dcb, dga, dgb = _ew(gate_bwd, [(dmerged, 0), (sv["ca"], 0), (sv["cb"], 0), (sv["proj"], OFF_GA),
                                            (sv["proj"], OFF_GB)], [BF16, BF16, BF16, BF16],
                                 width=D_MODEL, bw=256, name="gate_bwd")
        dgab = jnp.concatenate([dga, dgb], axis=1)
        gbig["w_branch_a"] = _mm(sv["ya"], dca, "tn", F32, "mm_d_branch_a", out_layer=(l, gbig["w_branch_a"]))
        gbig["w_branch_b"] = _mm(sv["yb"], dcb, "tn", F32, "mm_d_branch_b", out_layer=(l, gbig["w_branch_b"]))
        dya = _mm(dca, w["w_branch_a"], "nt", F32, "mm_dya")
        dyb = _mm(dcb, w["w_branch_b"], "nt", F32, "mm_dyb")

        dya, delta_a = _attn_bwd_prep(dya, sv["ya"], "attn_a_bwd_prep")
        dyb, delta_b = _attn_bwd_prep(dyb, sv["yb"], "attn_b_bwd_prep")

        dqa, dka, dva = [], [], []
        for (blk, d), bias, bk in zip(DILATED, bias_a, buckets_a):
            dq_, dk_, dv_, db_ = _attn_bwd(sv["qa"], sv["ka"], sv["va"], dya, sv["lse_a"], delta_a, bias, None, blk, d,
                                           f"attn_a{d}_bwd")
            dqa.append(dq_)
            dka.append(dk_)
            dva.append(dv_)
            dtable_a = dtable_a + _table_grad(db_.reshape(N_HEADS, blk, 3 * blk), bk, "table_grad_a")
        dqb, dkb, dvb, db_, dsink = _attn_bwd(sv["qb"], sv["kb"], sv["vb"], dyb, sv["lse_b"], delta_b, bias_b, sink,
                                              BLK_B, 1, "attn_b_bwd")
        dtable_b = dtable_b + _table_grad(db_.reshape(N_HEADS, BLK_B, 3 * BLK_B), buckets_b, "table_grad_b")
        gsmall["sink_b"][l] = dsink.reshape(N_HEADS, BLK_B).sum(axis=1)

        dproj, pqa, pka, pqb, pkb = _qknorm_bwd(sv["proj"], gqa, gka, gqb, gkb, dqa, dka, dva, dqb, dkb, dvb, dgab)
        gsmall["qnorm_a_g"][l] = pqa.reshape(N_HEADS, HEAD_DIM).sum(0)
        gsmall["knorm_a_g"][l] = pka.reshape(N_HEADS, HEAD_DIM).sum(0)
        gsmall["qnorm_b_g"][l] = pqb.reshape(N_HEADS, HEAD_DIM).sum(0)
        gsmall["knorm_b_g"][l] = pkb.reshape(N_KV_B, HEAD_DIM).sum(0)
        gbig["w_in"] = _mm(sv["h"], dproj, "tn", F32, "mm_d_in", out_layer=(l, gbig["w_in"]))
        dh = _mm(dproj, w["w_in"], "nt", F32, "mm_dh")
        dx, gsmall["norm_mix_g"][l] = _rms_bwd(sv["x0"], g_mix, dh, dx, "rms_mix_bwd")
        gsmall["norm_mix_g"][l] = gsmall["norm_mix_g"][l][0]
        gsmall["norm_ffn_g"][l] = gsmall["norm_ffn_g"][l][0]
        gsmall["norm_ple_g"][l] = gsmall["norm_ple_g"][l][0]

    gsmall = {n: jnp.stack(v) for n, v in gsmall.items()}
    gsmall["rel_table"] = jnp.concatenate([dtable_a, dtable_b], axis=0).T
    return loss, dx, gbig, gsmall


def _place():
    return lax.axis_index("x"), lax.axis_index("y"), lax.axis_index("c")


def _flip(v, bit):
    return 1 - v if bit else v


CHIP_RELATIONS = ((0, 1), (1, 0), (1, 1))
ANY = pl.BlockSpec(memory_space=pl.ANY)


def _allgather(shards):
    n = len(shards)

    def body(*refs):
        w_refs, out_refs = refs[:n], refs[n:2 * n]
        send_sems, recv_sems = refs[2 * n], refs[2 * n + 1]
        x, y, c = _place()
        chips = [(_flip(x, a), _flip(y, b)) for a, b in CHIP_RELATIONS]

        def make(g):
            w_ref, out_ref = w_refs[g], out_refs[g]
            half = w_ref.shape[0] // 2

            def part(px, py, pc):
                return out_ref.at[2 * px + py, pl.ds(pc * half, half), :]

            def copy(k, block, to, src=None):
                return pltpu.make_async_remote_copy(
                    src_ref=part(*block) if src is None else src, dst_ref=part(*block),
                    send_sem=send_sems.at[7 * g + k], recv_sem=recv_sems.at[7 * g + k], device_id=to,
                    device_id_type=MESH_ID)

            own = pltpu.make_async_remote_copy(
                src_ref=w_ref, dst_ref=out_ref.at[2 * x + y], send_sem=send_sems.at[7 * g + 6],
                recv_sem=recv_sems.at[7 * g + 6], device_id=(x, y, 1 - c), device_id_type=MESH_ID)
            first = [copy(k, (x, y, c), (*chip, c), src=w_ref.at[pl.ds(c * half, half), :])
                     for k, chip in enumerate(chips)]
            passed = [copy(3 + k, (*chip, c), (x, y, 1 - c)) for k, chip in enumerate(chips)]
            arrive = [copy(k, (*chip, c), (x, y, c)) for k, chip in enumerate(chips)]
            arrive2 = [copy(3 + k, (*chip, 1 - c), (x, y, c)) for k, chip in enumerate(chips)]
            return own, first, passed, arrive, arrive2

        made = [make(g) for g in range(n)]
        for own, first, _, _, _ in made:
            own.start()
            for cp in first:
                cp.start()
        for _, _, passed, arrive, _ in made:
            for k in range(3):
                arrive[k].wait_recv()
                passed[k].start()
        for own, first, passed, _, arrive2 in made:
            for k in range(3):
                arrive2[k].wait_recv()
            own.wait_recv()
            for cp in first + passed + [own]:
                cp.wait_send()

    return pl.pallas_call(
        body, out_shape=[jax.ShapeDtypeStruct((N_CHIPS,) + s.shape, s.dtype) for s in shards],
        in_specs=[ANY] * n, out_specs=[ANY] * n,
        scratch_shapes=[pltpu.SemaphoreType.DMA((7 * n,)), pltpu.SemaphoreType.DMA((7 * n,))],
        name="allgather_weights")(*shards)


def _half_tile(half):
    return max(t for t in range(16, 1025, 16) if half % t == 0)


def _sibling_halves(gsends):
    n = len(gsends)

    def body(*refs):
        g_refs, out_refs, send_sems, recv_sems = refs[:n], refs[n:2 * n], refs[2 * n], refs[2 * n + 1]
        x, y, c = _place()
        cps = []
        for g in range(n):
            half = g_refs[g].shape[1] // 2
            cps.append(pltpu.make_async_remote_copy(
                src_ref=g_refs[g].at[:, pl.ds((1 - c) * half, half), :], dst_ref=out_refs[g],
                send_sem=send_sems.at[g], recv_sem=recv_sems.at[g], device_id=(x, y, 1 - c), device_id_type=MESH_ID))
        for cp in cps:
            cp.start()
        for cp in cps:
            cp.wait_recv()
        for cp in cps:
            cp.wait_send()

    return pl.pallas_call(
        body, out_shape=[jax.ShapeDtypeStruct((s.shape[0], s.shape[1] // 2, s.shape[2]), s.dtype) for s in gsends],
        in_specs=[ANY] * n, out_specs=[ANY] * n,
        scratch_shapes=[pltpu.SemaphoreType.DMA((n,)), pltpu.SemaphoreType.DMA((n,))], name="rs_sibling_halves")(*gsends)


def _chip_sums(gsend, sib, place):
    n, rows, cols = gsend.shape
    half = rows // 2
    tm = _half_tile(half)
    nblk = half // tm

    def body(s_ref, g_ref, sib_ref, o_ref):
        o_ref[0] = (g_ref[0].astype(F32) + sib_ref[0].astype(F32)).astype(o_ref.dtype)

    grid_spec = pltpu.PrefetchScalarGridSpec(
        num_scalar_prefetch=1, grid=(n, nblk),
        in_specs=[pl.BlockSpec((1, tm, cols), lambda k, i, s: (jnp.bitwise_xor(s[0], k), s[1] * nblk + i, 0)),
                  pl.BlockSpec((1, tm, cols), lambda k, i, s: (jnp.bitwise_xor(s[0], k), i, 0))],
        out_specs=pl.BlockSpec((1, tm, cols), lambda k, i, s: (k, i, 0)))
    return pl.pallas_call(
        body, out_shape=jax.ShapeDtypeStruct((n, half, cols), BF16), grid_spec=grid_spec,
        name="rs_chip_sums", compiler_params=_params("parallel", "parallel"))(place, gsend, sib)


def _exchange_chip_sums(tsends):
    n = len(tsends)

    def body(*refs):
        t_refs, out_refs, send_sems, recv_sems = refs[:n], refs[n:2 * n], refs[2 * n], refs[2 * n + 1]
        x, y, c = _place()
        cps = []
        for g in range(n):
            for k, (a, b) in enumerate(CHIP_RELATIONS):
                cps.append(pltpu.make_async_remote_copy(
                    src_ref=t_refs[g].at[k + 1], dst_ref=out_refs[g].at[k], send_sem=send_sems.at[3 * g + k],
                    recv_sem=recv_sems.at[3 * g + k], device_id=(_flip(x, a), _flip(y, b), c), device_id_type=MESH_ID))
        for cp in cps:
            cp.start()
        for cp in cps:
            cp.wait_recv()
        for cp in cps:
            cp.wait_send()

    return pl.pallas_call(
        body, out_shape=[jax.ShapeDtypeStruct((3,) + s.shape[1:], s.dtype) for s in tsends],
        in_specs=[ANY] * n, out_specs=[ANY] * n,
        scratch_shapes=[pltpu.SemaphoreType.DMA((3 * n,)), pltpu.SemaphoreType.DMA((3 * n,))], name="rs_exchange")(*tsends)


def _final_sum(tsend, recv, place):
    n, half, cols = tsend.shape
    tm = _half_tile(half)
    nblk = half // tm

    def body(s_ref, t_ref, r_ref, o_ref):
        o_ref[...] = ((t_ref[0].astype(F32) + r_ref[0].astype(F32)) + r_ref[1].astype(F32)) + r_ref[2].astype(F32)

    grid_spec = pltpu.PrefetchScalarGridSpec(
        num_scalar_prefetch=1, grid=(nblk,),
        in_specs=[pl.BlockSpec((1, tm, cols), lambda i, s: (0, i, 0)), pl.BlockSpec((n - 1, tm, cols), lambda i, s: (0, i, 0))],
        out_specs=pl.BlockSpec((tm, cols), lambda i, s: (s[1] * nblk + i, 0)))
    return pl.pallas_call(
        body, out_shape=jax.ShapeDtypeStruct((2 * half, cols), F32), grid_spec=grid_spec, name="rs_final_sum",
        compiler_params=_params("parallel"))(place, tsend, recv)


def _join_halves(gfulls):
    n = len(gfulls)

    def body(*refs):
        g_refs, out_refs, send_sems, recv_sems = refs[:n], refs[n:2 * n], refs[2 * n], refs[2 * n + 1]
        x, y, c = _place()

        def copy(g, pc):
            half = g_refs[g].shape[0] // 2
            return pltpu.make_async_remote_copy(
                src_ref=g_refs[g].at[pl.ds(pc * half, half), :], dst_ref=out_refs[g].at[pl.ds(pc * half, half), :],
                send_sem=send_sems.at[g], recv_sem=recv_sems.at[g], device_id=(x, y, 1 - c), device_id_type=MESH_ID)

        mine = [copy(g, c) for g in range(n)]
        for cp in mine:
            cp.start()
        for g in range(n):
            copy(g, 1 - c).wait_recv()
        for cp in mine:
            cp.wait_send()

    return pl.pallas_call(
        body, out_shape=[jax.ShapeDtypeStruct(s.shape, s.dtype) for s in gfulls], in_specs=[ANY] * n, out_specs=[ANY] * n,
        input_output_aliases={g: g for g in range(n)},
        scratch_shapes=[pltpu.SemaphoreType.DMA((n,)), pltpu.SemaphoreType.DMA((n,))], name="rs_join_halves")(*gfulls)


def _allreduce_small(v):
    rows, cols = v.shape

    def body(v_ref, out_ref, buf, send_sems, recv_sems):
        x, y, c = _place()
        cps = []
        for k in range(1, 8):
            peer = (_flip(x, (k >> 2) & 1), _flip(y, (k >> 1) & 1), _flip(c, k & 1))
            cps.append(pltpu.make_async_remote_copy(
                src_ref=v_ref, dst_ref=buf.at[k - 1], send_sem=send_sems.at[k - 1], recv_sem=recv_sems.at[k - 1],
                device_id=peer, device_id_type=MESH_ID))
        for cp in cps:
            cp.start()
        for cp in cps:
            cp.wait_recv()
        for cp in cps:
            cp.wait_send()
        t0 = v_ref[...] + buf[0]
        t1 = buf[1] + buf[2]
        t2 = buf[3] + buf[4]
        t3 = buf[5] + buf[6]
        out_ref[...] = (t0 + t1) + (t2 + t3)

    vm = pl.BlockSpec(memory_space=pltpu.VMEM)
    return pl.pallas_call(
        body, out_shape=jax.ShapeDtypeStruct((rows, cols), F32), in_specs=[vm], out_specs=vm,
        scratch_shapes=[pltpu.VMEM((7, rows, cols), F32), pltpu.SemaphoreType.DMA((7,)), pltpu.SemaphoreType.DMA((7,))],
        name="allreduce_small")(v)


BIG_INFO = {n: (shape, ax) for n, shape, ax in BIG}
GROUPS = (("w_in",), ("w_ffn_gate", "w_ffn_up"), ("w_out", "w_ffn_down", "w_ple_gate"),
          ("w_branch_a", "w_branch_b", "w_ple_proj"))


def _shard_shape(name):
    (k, m), ax = BIG_INFO[name]
    return (k // N_CHIPS, m) if ax == 0 else (k, m // N_CHIPS)


def _group_rows(group):
    offs, off = {}, 0
    for n in group:
        offs[n] = off
        off += DEPTH * _shard_shape(n)[0]
    return offs, off


def _pack_groups(shards, dtype):
    return [jnp.concatenate([shards[n].reshape(-1, shards[n].shape[-1]).astype(dtype) for n in group], axis=0)
            for group in GROUPS]


def _unpack_full(gathered):
    out = {}
    for group, arr in zip(GROUPS, gathered):
        offs, _ = _group_rows(group)
        for n in group:
            rows, cols = _shard_shape(n)
            (k, m), ax = BIG_INFO[n]
            slab = arr[:, offs[n]:offs[n] + DEPTH * rows].reshape(N_CHIPS, DEPTH, rows, cols)
            order = (1, 0, 2, 3) if ax == 0 else (1, 2, 0, 3)
            out[n] = jnp.transpose(slab, order).reshape(DEPTH, k, m)
    return out


def _pack_grads(gbig):
    out = []
    for group in GROUPS:
        parts = []
        for n in group:
            rows, cols = _shard_shape(n)
            (k, m), ax = BIG_INFO[n]
            if ax == 0:
                slab = jnp.transpose(gbig[n].reshape(DEPTH, N_CHIPS, rows, cols), (1, 0, 2, 3))
            else:
                slab = jnp.transpose(gbig[n].reshape(DEPTH, rows, N_CHIPS, cols), (2, 0, 1, 3))
            parts.append(slab.reshape(N_CHIPS, DEPTH * rows, cols).astype(BF16))
        out.append(jnp.concatenate(parts, axis=1))
    return out


SMALL_SHAPES = {"rel_table": (NUM_BUCKETS, 2 * N_HEADS), "norm_mix_g": (DEPTH, D_MODEL), "qnorm_a_g": (DEPTH, HEAD_DIM),
                "knorm_a_g": (DEPTH, HEAD_DIM), "qnorm_b_g": (DEPTH, HEAD_DIM), "knorm_b_g": (DEPTH, HEAD_DIM),
                "sink_b": (DEPTH, N_HEADS), "norm_ffn_g": (DEPTH, D_MODEL), "norm_ple_g": (DEPTH, D_MODEL)}


def _pack_small(vals):
    flat = jnp.concatenate([vals[n].astype(F32).reshape(-1) for n in SMALL])
    flat = jnp.concatenate([flat, jnp.zeros((SMALL_ROWS * LANES - flat.shape[0],), F32)])
    return flat.reshape(SMALL_ROWS, LANES)


def _unpack_small(packed):
    flat, out, off = packed.reshape(-1), {}, 0
    for n in SMALL:
        size = math.prod(SMALL_SHAPES[n])
        out[n] = flat[off:off + size].reshape(SMALL_SHAPES[n])
        off += size
    return out


def _adamw(w, g, m, v, name, g_row=0):
    c1 = 1.0 - ADAM_B1 ** ADAM_STEP
    c2 = 1.0 - ADAM_B2 ** ADAM_STEP

    def fn(w_, g_, m_, v_):
        m_new = ADAM_B1 * m_ + (1.0 - ADAM_B1) * g_
        v_new = ADAM_B2 * v_ + (1.0 - ADAM_B2) * (g_ * g_)
        delta = -ADAM_LR * ((m_new / c1) / (jnp.sqrt(v_new / c2) + ADAM_EPS) + ADAM_WD * w_)
        return g_, delta, m_new, v_new

    rows, width = w.shape
    tm = max(t for t in (512, 256, 128, 64) if rows % t == 0 and g_row % t == 0)
    return _ew(fn, [(w, 0), (g, 0), (m, 0), (v, 0)], [F32, F32, F32, F32], width=width, bw=width, name=name,
               rows=rows, row_offs=[0, g_row, 0, 0], tm=tm)


def kernel(x, p, rel_table, norm_mix_g, w_in, qnorm_a_g, knorm_a_g, qnorm_b_g, knorm_b_g, sink_b, w_branch_a, w_branch_b, w_out, norm_ffn_g, w_ffn_gate, w_ffn_up, w_ffn_down, norm_ple_g, w_ple_gate, w_ple_proj, loss_target, m_rel_table, m_norm_mix_g, m_w_in, m_qnorm_a_g, m_knorm_a_g, m_qnorm_b_g, m_knorm_b_g, m_sink_b, m_w_branch_a, m_w_branch_b, m_w_out, m_norm_ffn_g, m_w_ffn_gate, m_w_ffn_up, m_w_ffn_down, m_norm_ple_g, m_w_ple_gate, m_w_ple_proj, v_rel_table, v_norm_mix_g, v_w_in, v_qnorm_a_g, v_knorm_a_g, v_qnorm_b_g, v_knorm_b_g, v_sink_b, v_w_branch_a, v_w_branch_b, v_w_out, v_norm_ffn_g, v_w_ffn_gate, v_w_ffn_up, v_w_ffn_down, v_norm_ple_g, v_w_ple_gate, v_w_ple_proj):
    given = dict(locals())
    weights = {n: given[n] for n in WEIGHTS}
    moments_m = {n: given["m_" + n] for n in WEIGHTS}
    moments_v = {n: given["v_" + n] for n in WEIGHTS}
    xi, yi, ci = _place()
    place = jnp.stack([2 * xi + yi, ci]).astype(jnp.int32)

    big = _unpack_full(_allgather(_pack_groups(weights, BF16)))
    small = {n: weights[n] for n in SMALL}

    loss, dx, gbig, gsmall = _local_step(x[0], p[:, 0], loss_target[0], big, small)

    gsends = _pack_grads(gbig)
    tsends = [_chip_sums(g, s, place) for g, s in zip(gsends, _sibling_halves(gsends))]
    greds = _join_halves([_final_sum(t, r, place) for t, r in zip(tsends, _exchange_chip_sums(tsends))])

    grads, delta, new_m, new_v = {}, {}, {}, {}
    for group, gred in zip(GROUPS, greds):
        offs, _ = _group_rows(group)
        for n in group:
            shape = weights[n].shape
            two_d = lambda a: a.reshape(shape[0] * shape[1], shape[2])
            outs = _adamw(two_d(weights[n]), gred, two_d(moments_m[n]), two_d(moments_v[n]), "adamw_" + n, g_row=offs[n])
            grads[n], delta[n], new_m[n], new_v[n] = (o.reshape(shape) for o in outs)
    small_grads = _allreduce_small(_pack_small(gsmall))
    g_, d_, m_, v_ = _adamw(_pack_small(weights), small_grads, _pack_small(moments_m), _pack_small(moments_v),
                            "adamw_small")
    grads.update(_unpack_small(g_))
    delta.update(_unpack_small(d_))
    new_m.update(_unpack_small(m_))
    new_v.update(_unpack_small(v_))

    loss = lax.psum(loss, ("x", "y", "c"))
    return (loss, dx[None], *[grads[n] for n in WEIGHTS], *[delta[n] for n in WEIGHTS],
            *[new_m[n] for n in WEIGHTS], *[new_v[n] for n in WEIGHTS])
```

```python
import functools
import math

import jax
import jax.numpy as jnp
from jax import lax
from jax.experimental import pallas as pl
from jax.experimental.pallas import tpu as pltpu
from jax.experimental.pallas import tpu_sc as plsc

F32 = jnp.float32
BF16 = jnp.bfloat16
MESH_ID = pl.DeviceIdType.MESH

SEQ = 2048
D_MODEL = 1024
DEPTH = 2
HEAD_DIM = 64
N_HEADS = 8
WIDTH = N_HEADS * HEAD_DIM
N_PAIRS = 4
ITEMS = 4
N_KV_B = 2
PLE_DIM = 256
D_FF = 2816
D_IN = 4352
OFF_QA, OFF_KA, OFF_VA, OFF_QB, OFF_KB, OFF_VB, OFF_GA, OFF_GB = 0, 512, 1024, 1536, 2048, 2176, 2304, 3328
DILATED = ((64, 1), (64, 4), (64, 16))
BLK_B = 128
NUM_BUCKETS = 32
MAX_DISTANCE = 1024
RMS_EPS = 1e-6
NEG_INF = -1e30
LANES = 128
VMEM_LIMIT = 48 * 1024 * 1024

ADAM_LR, ADAM_B1, ADAM_B2, ADAM_EPS, ADAM_WD, ADAM_STEP = 0.001, 0.9, 0.999, 1e-08, 0.01, 10

BIG = (
    ("w_in", (D_MODEL, D_IN), 1),
    ("w_branch_a", (WIDTH, D_MODEL), 1),
    ("w_branch_b", (WIDTH, D_MODEL), 1),
    ("w_out", (D_MODEL, D_MODEL), 0),
    ("w_ffn_gate", (D_MODEL, D_FF), 1),
    ("w_ffn_up", (D_MODEL, D_FF), 1),
    ("w_ffn_down", (D_FF, D_MODEL), 0),
    ("w_ple_gate", (D_MODEL, D_MODEL), 0),
    ("w_ple_proj", (PLE_DIM, D_MODEL), 1),
)
SMALL = ("rel_table", "norm_mix_g", "qnorm_a_g", "knorm_a_g", "qnorm_b_g", "knorm_b_g", "sink_b",
         "norm_ffn_g", "norm_ple_g")
WEIGHTS = ("rel_table", "norm_mix_g", "w_in", "qnorm_a_g", "knorm_a_g", "qnorm_b_g", "knorm_b_g", "sink_b",
           "w_branch_a", "w_branch_b", "w_out", "norm_ffn_g", "w_ffn_gate", "w_ffn_up", "w_ffn_down",
           "norm_ple_g", "w_ple_gate", "w_ple_proj")
N_CHIPS = 4
SMALL_ROWS = 64


def _params(*sem):
    return pltpu.CompilerParams(dimension_semantics=sem, vmem_limit_bytes=VMEM_LIMIT)


def _pick(dim, target):
    for t in (target, 512, 256, 128, 64, 32, 16, 8):
        if t <= target and dim % t == 0:
            return t
    return dim


def _mm(a, b, mode, out_dtype, name, res=None, tm=512, tn=512):
    if mode == "nn":
        (m, k), (_, n) = a.shape, b.shape
    elif mode == "nt":
        (m, k), (n, _) = a.shape, b.shape
    else:
        (k, m), (_, n) = a.shape, b.shape
    tm, tn = _pick(m, tm), _pick(n, tn)
    a_spec = pl.BlockSpec((k, tm), lambda i, j: (0, i)) if mode == "tn" else pl.BlockSpec((tm, k), lambda i, j: (i, 0))
    b_spec = pl.BlockSpec((tn, k), lambda i, j: (j, 0)) if mode == "nt" else pl.BlockSpec((k, tn), lambda i, j: (0, j))
    dims = {"nn": (((1,), (0,)), ((), ())), "nt": (((1,), (1,)), ((), ())), "tn": (((0,), (0,)), ((), ()))}[mode]
    has_res = res is not None

    def body(*refs):
        a_ref, b_ref = refs[0], refs[1]
        o_ref = refs[-1]
        acc = lax.dot_general(a_ref[...].astype(BF16), b_ref[...].astype(BF16), dims, preferred_element_type=F32)
        if has_res:
            acc = acc + refs[2][...]
        o_ref[...] = acc.astype(out_dtype)

    in_specs = [a_spec, b_spec]
    args = [a, b]
    if has_res:
        in_specs.append(pl.BlockSpec((tm, tn), lambda i, j: (i, j)))
        args.append(res)
    return pl.pallas_call(
        body, out_shape=jax.ShapeDtypeStruct((m, n), out_dtype), grid=(m // tm, n // tn), in_specs=in_specs,
        out_specs=pl.BlockSpec((tm, tn), lambda i, j: (i, j)), name=name,
        compiler_params=_params("parallel", "parallel"))(*args)


def _ew(fn, ins, out_dtypes, *, width, bw, name, vecs=(), tm=256, rows=None, row_offs=None):
    rows = ins[0][0].shape[0] if rows is None else rows
    tm = _pick(rows, tm)
    row_offs = [0] * len(ins) if row_offs is None else row_offs
    assert all(r % tm == 0 for r in row_offs)
    n_in = len(ins) + len(vecs)

    def col_map(row_blocks, off_blocks):
        return lambda i, j: (row_blocks + i, off_blocks + j)

    in_specs = [pl.BlockSpec((tm, bw), col_map(r // tm, off // bw)) for (_, off), r in zip(ins, row_offs)]
    in_specs += [pl.BlockSpec((1, bw), lambda i, j: (0, j)) for _ in vecs]

    def body(*refs):
        outs = fn(*[r[...] for r in refs[:n_in]])
        for r, o in zip(refs[n_in:], outs):
            r[...] = o.astype(r.dtype)

    return pl.pallas_call(
        body, out_shape=[jax.ShapeDtypeStruct((rows, width), dt) for dt in out_dtypes],
        grid=(rows // tm, width // bw), in_specs=in_specs,
        out_specs=[pl.BlockSpec((tm, bw), lambda i, j: (i, j)) for _ in out_dtypes],
        name=name, compiler_params=_params("parallel", "parallel"))(*[a for a, _ in ins], *vecs)


def _sigmoid(x):
    return 1.0 / (1.0 + jnp.exp(-x))


def _seg_sum(v):
    outs = []
    for k in range(v.shape[1] // LANES):
        vp = v[:, k * LANES:(k + 1) * LANES]
        left = lax.broadcasted_iota(jnp.int32, vp.shape, 1) < HEAD_DIM
        sl = jnp.sum(jnp.where(left, vp, 0.0), axis=-1, keepdims=True)
        sr = jnp.sum(jnp.where(left, 0.0, vp), axis=-1, keepdims=True)
        outs.append(jnp.where(left, sl, sr))
    return outs[0] if len(outs) == 1 else jnp.concatenate(outs, axis=1)


def _seg_rstd(x):
    return lax.rsqrt(_seg_sum(x * x) * (1.0 / HEAD_DIM) + RMS_EPS)


def _rms_fwd(x, g, name):
    rows, d = x.shape
    tm = 256

    def body(x_ref, g_ref, h_ref):
        xv = x_ref[...]
        r = lax.rsqrt(jnp.mean(xv * xv, axis=-1, keepdims=True) + RMS_EPS)
        h_ref[...] = ((xv * r) * g_ref[...]).astype(BF16)

    return pl.pallas_call(
        body, out_shape=jax.ShapeDtypeStruct((rows, d), BF16), grid=(rows // tm,),
        in_specs=[pl.BlockSpec((tm, d), lambda i: (i, 0)), pl.BlockSpec((1, d), lambda i: (0, 0))],
        out_specs=pl.BlockSpec((tm, d), lambda i: (i, 0)), name=name, compiler_params=_params("parallel"))(x, g)


def _rms_bwd(x, g, dh, dres, name):
    rows, d = x.shape
    tm = 256

    def body(x_ref, g_ref, dh_ref, dres_ref, dx_ref, dg_ref):
        xv = x_ref[...]
        r = lax.rsqrt(jnp.mean(xv * xv, axis=-1, keepdims=True) + RMS_EPS)
        xh = xv * r
        dhv = dh_ref[...]
        dxh = dhv * g_ref[...]
        dx_ref[...] = dres_ref[...] + r * (dxh - xh * jnp.mean(dxh * xh, axis=-1, keepdims=True))
        part = jnp.sum(dhv * xh, axis=0, keepdims=True)

        @pl.when(pl.program_id(0) == 0)
        def _():
            dg_ref[...] = part

        @pl.when(pl.program_id(0) > 0)
        def _():
            dg_ref[...] += part

    row = pl.BlockSpec((tm, d), lambda i: (i, 0))
    vec = pl.BlockSpec((1, d), lambda i: (0, 0))
    return pl.pallas_call(
        body, out_shape=[jax.ShapeDtypeStruct((rows, d), F32), jax.ShapeDtypeStruct((1, d), F32)],
        grid=(rows // tm,), in_specs=[row, vec, row, row], out_specs=[row, vec],
        name=name, compiler_params=_params("arbitrary"))(x, g, dh, dres)


def _loss_grad(y, t):
    rows, d = y.shape
    tm = 256

    def body(y_ref, t_ref, dy_ref, l_ref):
        e = y_ref[...] - t_ref[...]
        dy_ref[...] = e * (1.0 / d)
        part = jnp.zeros((1, LANES), F32) + jnp.sum(e * e) * (0.5 / d)

        @pl.when(pl.program_id(0) == 0)
        def _():
            l_ref[...] = part

        @pl.when(pl.program_id(0) > 0)
        def _():
            l_ref[...] += part

    row = pl.BlockSpec((tm, d), lambda i: (i, 0))
    return pl.pallas_call(
        body, out_shape=[jax.ShapeDtypeStruct((rows, d), F32), jax.ShapeDtypeStruct((1, LANES), F32)],
        grid=(rows // tm,), in_specs=[row, row], out_specs=[row, pl.BlockSpec((1, LANES), lambda i: (0, 0))],
        name="loss_grad", compiler_params=_params("arbitrary"))(y, t)


def _put_pairs(ref, val):
    for hp in range(N_PAIRS):
        ref[hp] = val[:, hp * LANES:(hp + 1) * LANES].astype(ref.dtype)


def _get_pairs(ref):
    return jnp.concatenate([ref[hp] for hp in range(N_PAIRS)], axis=1)


def _swap_halves(v):
    return pltpu.roll(v, HEAD_DIM, axis=1)


def _expand_kv(kv):
    left = lax.broadcasted_iota(jnp.int32, kv.shape, 1) < HEAD_DIM
    sw = _swap_halves(kv)
    h0 = jnp.where(left, kv, sw)
    h1 = jnp.where(left, sw, kv)
    return jnp.concatenate([h0, h0, h1, h1], axis=1)


def _reduce_kv(dkv):
    left = lax.broadcasted_iota(jnp.int32, (dkv.shape[0], LANES), 1) < HEAD_DIM
    t = dkv[:, 0:LANES] + dkv[:, LANES:2 * LANES]
    u = dkv[:, 2 * LANES:3 * LANES] + dkv[:, 3 * LANES:4 * LANES]
    t = t + _swap_halves(t)
    u = u + _swap_halves(u)
    return jnp.where(left, t, u)


def _qknorm_fwd(proj, gqa, gka, gqb, gkb):
    rows = proj.shape[0]
    tm = 256

    def body(qa_ref, ka_ref, va_ref, qb_ref, kb_ref, vb_ref, gqa_ref, gka_ref, gqb_ref, gkb_ref,
             oqa, oka, ova, oqb, okb, ovb):
        for src, g_ref, dst in ((qa_ref, gqa_ref, oqa), (ka_ref, gka_ref, oka), (qb_ref, gqb_ref, oqb)):
            xv = src[...]
            _put_pairs(dst, (xv * _seg_rstd(xv)) * g_ref[...])
        _put_pairs(ova, va_ref[...])
        kv = kb_ref[...]
        _put_pairs(okb, _expand_kv((kv * _seg_rstd(kv)) * gkb_ref[...]))
        _put_pairs(ovb, _expand_kv(vb_ref[...]))

    def win(width, off):
        return pl.BlockSpec((tm, width), lambda i: (i, off // width))

    vec = lambda w: pl.BlockSpec((1, w), lambda i: (0, 0))
    out = pl.BlockSpec((N_PAIRS, tm, LANES), lambda i: (0, i, 0))
    return pl.pallas_call(
        body, out_shape=[jax.ShapeDtypeStruct((N_PAIRS, rows, LANES), F32)] * 6, grid=(rows // tm,),
        in_specs=[win(WIDTH, OFF_QA), win(WIDTH, OFF_KA), win(WIDTH, OFF_VA), win(WIDTH, OFF_QB),
                  win(LANES, OFF_KB), win(LANES, OFF_VB), vec(WIDTH), vec(WIDTH), vec(WIDTH), vec(LANES)],
        out_specs=[out] * 6, name="qknorm_fwd", compiler_params=_params("parallel"))(
            proj, proj, proj, proj, proj, proj, gqa, gka, gqb, gkb)


def _norm_bwd(xv, g, dy):
    r = _seg_rstd(xv)
    xh = xv * r
    dxh = dy * g
    dx = r * (dxh - xh * (_seg_sum(dxh * xh) * (1.0 / HEAD_DIM)))
    return dx, jnp.sum(dy * xh, axis=0, keepdims=True)


def _qknorm_bwd(proj, gqa, gka, gqb, gkb, dqa, dka, dva, dqb, dkb, dvb, dgab):
    rows = proj.shape[0]
    tm = 256
    n_a = len(dqa)

    def body(*refs):
        qa_ref, ka_ref, qb_ref, kb_ref, gqa_ref, gka_ref, gqb_ref, gkb_ref = refs[:8]
        pos = 8
        dqa_refs, dka_refs, dva_refs = refs[pos:pos + n_a], refs[pos + n_a:pos + 2 * n_a], refs[pos + 2 * n_a:pos + 3 * n_a]
        pos += 3 * n_a
        dqb_ref, dkb_ref, dvb_ref, dgab_ref = refs[pos:pos + 4]
        dproj_ref, ogqa, ogka, ogqb, ogkb = refs[pos + 4:]

        def total(rs):
            acc = _get_pairs(rs[0])
            for r in rs[1:]:
                acc = acc + _get_pairs(r)
            return acc

        dx_qa, p_qa = _norm_bwd(qa_ref[...], gqa_ref[...], total(dqa_refs))
        dx_ka, p_ka = _norm_bwd(ka_ref[...], gka_ref[...], total(dka_refs))
        dx_qb, p_qb = _norm_bwd(qb_ref[...], gqb_ref[...], _get_pairs(dqb_ref))
        dx_kb, p_kb = _norm_bwd(kb_ref[...], gkb_ref[...], _reduce_kv(_get_pairs(dkb_ref)))
        dproj_ref[:, OFF_QA:OFF_QA + WIDTH] = dx_qa.astype(BF16)
        dproj_ref[:, OFF_KA:OFF_KA + WIDTH] = dx_ka.astype(BF16)
        dproj_ref[:, OFF_VA:OFF_VA + WIDTH] = total(dva_refs).astype(BF16)
        dproj_ref[:, OFF_QB:OFF_QB + WIDTH] = dx_qb.astype(BF16)
        dproj_ref[:, OFF_KB:OFF_KB + LANES] = dx_kb.astype(BF16)
        dproj_ref[:, OFF_VB:OFF_VB + LANES] = _reduce_kv(_get_pairs(dvb_ref)).astype(BF16)
        dproj_ref[:, OFF_GA:D_IN] = dgab_ref[...]
        first = pl.program_id(0) == 0
        for o_ref, part in ((ogqa, p_qa), (ogka, p_ka), (ogqb, p_qb), (ogkb, p_kb)):
            @pl.when(first)
            def _(o_ref=o_ref, part=part):
                o_ref[...] = part

            @pl.when(jnp.logical_not(first))
            def _(o_ref=o_ref, part=part):
                o_ref[...] += part

    def win(width, off):
        return pl.BlockSpec((tm, width), lambda i: (i, off // width))

    vec = lambda w: pl.BlockSpec((1, w), lambda i: (0, 0))
    row = lambda w: pl.BlockSpec((tm, w), lambda i: (i, 0))
    in_specs = [win(WIDTH, OFF_QA), win(WIDTH, OFF_KA), win(WIDTH, OFF_QB), win(LANES, OFF_KB),
                vec(WIDTH), vec(WIDTH), vec(WIDTH), vec(LANES)]
    in_specs += [pl.BlockSpec((N_PAIRS, tm, LANES), lambda i: (0, i, 0))] * (3 * n_a + 3) + [row(2 * D_MODEL)]
    return pl.pallas_call(
        body,
        out_shape=[jax.ShapeDtypeStruct((rows, D_IN), BF16), jax.ShapeDtypeStruct((1, WIDTH), F32),
                   jax.ShapeDtypeStruct((1, WIDTH), F32), jax.ShapeDtypeStruct((1, WIDTH), F32),
                   jax.ShapeDtypeStruct((1, LANES), F32)],
        grid=(rows // tm,), in_specs=in_specs,
        out_specs=[row(D_IN), vec(WIDTH), vec(WIDTH), vec(WIDTH), vec(LANES)],
        name="qknorm_bwd", compiler_params=_params("arbitrary"))(
            proj, proj, proj, proj, gqa, gka, gqb, gkb, *dqa, *dka, *dva, dqb, dkb, dvb, dgab)


def _t5_bucket(rel):
    half_b = NUM_BUCKETS // 2
    max_exact = half_b // 2
    sign = jnp.where(rel > 0, half_b, 0)
    n = jnp.abs(rel)
    nf = jnp.maximum(n, 1).astype(F32)
    large = max_exact + (jnp.log(nf / max_exact) / math.log(MAX_DISTANCE / max_exact)
                         * (half_b - max_exact)).astype(jnp.int32)
    large = jnp.minimum(large, half_b - 1)
    return sign + jnp.where(n < max_exact, n, large)


def _band_buckets(blk, dilation):
    i = jnp.arange(blk, dtype=jnp.int32)[:, None]
    j = jnp.arange(3 * blk, dtype=jnp.int32)[None, :]
    rel = j - blk - i
    return jnp.where(jnp.abs(rel) <= blk, _t5_bucket(rel * dilation), -1)


def _bias_tiles(table, buckets, head_off, name):
    blk = buckets.shape[0]

    def body(tab_ref, bk_ref, o_ref):
        h = pl.program_id(0) + head_off
        bk = bk_ref[...]
        acc = jnp.full(bk.shape, NEG_INF, F32)
        for b in range(NUM_BUCKETS):
            acc = jnp.where(bk == b, tab_ref[b, h], acc)
        o_ref[0] = acc

    return pl.pallas_call(
        body, out_shape=jax.ShapeDtypeStruct((N_HEADS, blk, 3 * blk), F32), grid=(N_HEADS,),
        in_specs=[pl.BlockSpec(memory_space=pltpu.SMEM), pl.BlockSpec((blk, 3 * blk), lambda h: (0, 0))],
        out_specs=pl.BlockSpec((1, blk, 3 * blk), lambda h: (h, 0, 0)),
        name=name, compiler_params=_params("parallel"))(table, buckets)


def _table_grad(dbias, buckets, name):
    blk = buckets.shape[0]

    def body(db_ref, bk_ref, o_ref):
        bk = bk_ref[...]
        dbv = db_ref[0]
        lane = lax.broadcasted_iota(jnp.int32, (1, LANES), 1)
        acc = jnp.zeros((1, LANES), F32)
        for b in range(NUM_BUCKETS):
            acc = jnp.where(lane == b, jnp.sum(jnp.where(bk == b, dbv, 0.0)), acc)
        o_ref[0] = acc

    out = pl.pallas_call(
        body, out_shape=jax.ShapeDtypeStruct((N_HEADS, 1, LANES), F32), grid=(N_HEADS,),
        in_specs=[pl.BlockSpec((1, blk, 3 * blk), lambda h: (h, 0, 0)), pl.BlockSpec((blk, 3 * blk), lambda h: (0, 0))],
        out_specs=pl.BlockSpec((1, 1, LANES), lambda h: (h, 0, 0)),
        name=name, compiler_params=_params("parallel"))(dbias, buckets)
    return out[:, 0, :NUM_BUCKETS]


def _dot_nt(a, b):
    return lax.dot_general(a, b, (((1,), (1,)), ((), ())), preferred_element_type=F32)


def _stack_pair(x2, left):
    return jnp.concatenate([jnp.where(left, x2, 0.0), jnp.where(left, 0.0, x2)], axis=0).astype(BF16)


def _attn_geometry(blk, d):
    chunk = blk * ITEMS if d == 1 else blk * d
    groups = 1 if d == 1 else d // ITEMS
    halo = blk if d == 1 else chunk
    return chunk, groups, halo


def _item_rows(ref, j, r0, blk, d):
    if d == 1:
        return ref[j * blk:(j + 1) * blk, :]
    return ref[pl.ds(r0 + j, blk, stride=d), :]


def _item_penalty(t, nct, j, blk, d):
    first_ok, last_ok = t > 0, t < nct - 1
    if d == 1:
        first_ok = True if j > 0 else first_ok
        last_ok = True if j < ITEMS - 1 else last_ok
    col = lax.broadcasted_iota(jnp.int32, (1, 3 * blk), 1)
    ok = jnp.logical_and(jnp.logical_or(col >= blk, first_ok), jnp.logical_or(col < 2 * blk, last_ok))
    return jnp.where(ok, 0.0, NEG_INF).astype(F32)


def _attn_specs(seq, blk, d, step_of):
    chunk, _, halo = _attn_geometry(blk, d)
    per, last = chunk // halo, seq // halo - 1
    cur = pl.BlockSpec((None, chunk, LANES), lambda hp, t: (hp, step_of(t), 0))
    prev = pl.BlockSpec((None, halo, LANES), lambda hp, t: (hp, jnp.clip(step_of(t) * per - 1, 0, last), 0))
    nxt = pl.BlockSpec((None, halo, LANES), lambda hp, t: (hp, jnp.minimum((step_of(t) + 1) * per, last), 0))
    return cur, prev, nxt


def _attn_fwd(q, k, v, bias, sink, blk, d, name):
    _, seq, _ = q.shape
    chunk, groups, _ = _attn_geometry(blk, d)
    nct = seq // chunk
    has_sink = sink is not None
    scale = HEAD_DIM ** -0.5

    def body(*refs):
        q_ref, kp, kc, kn, vp, vc, vn, b_ref = refs[:8]
        s_ref = refs[8] if has_sink else None
        o_ref, l_ref = refs[-2], refs[-1]
        t = pl.program_id(1)
        left = lax.broadcasted_iota(jnp.int32, (1, LANES), 1) < HEAD_DIM
        bias2 = b_ref[...]
        if d == 1:
            kwin = jnp.concatenate([kp[...], kc[...], kn[...]], axis=0).astype(BF16)
            vwin = jnp.concatenate([vp[...], vc[...], vn[...]], axis=0).astype(BF16)

        def group(r0):
            scores, vcats = [], []
            for j in range(ITEMS):
                qs = _stack_pair(_item_rows(q_ref, j, r0, blk, d) * scale, left)
                if d == 1:
                    kcat, vcat = kwin[j * blk:(j + 3) * blk], vwin[j * blk:(j + 3) * blk]
                else:
                    kcat = jnp.concatenate([_item_rows(r, j, r0, blk, d) for r in (kp, kc, kn)], axis=0).astype(BF16)
                    vcat = jnp.concatenate([_item_rows(r, j, r0, blk, d) for r in (vp, vc, vn)], axis=0).astype(BF16)
                scores.append(_dot_nt(qs, kcat) + bias2 + _item_penalty(t, nct, j, blk, d))
                vcats.append(vcat)
            s = jnp.concatenate(scores, axis=0)
            m = jnp.max(s, axis=-1, keepdims=True)
            if has_sink:
                sk = jnp.concatenate([s_ref[...]] * ITEMS, axis=0)
                m = jnp.maximum(m, sk)
            p = jnp.exp(s - m)
            den = jnp.sum(p, axis=-1, keepdims=True)
            if has_sink:
                den = den + jnp.exp(sk - m)
            pn = (p * (1.0 / den)).astype(BF16)
            lse = m + jnp.log(den)
            for j in range(ITEMS):
                top, mid, bot = 2 * j * blk, (2 * j + 1) * blk, (2 * j + 2) * blk
                o2 = jnp.dot(pn[top:bot], vcats[j], preferred_element_type=F32)
                o_val = jnp.where(left, o2[:blk], o2[blk:])
                l_val = jnp.where(left, lse[top:mid], lse[mid:bot])
                if d == 1:
                    o_ref[j * blk:(j + 1) * blk, :] = o_val
                    l_ref[j * blk:(j + 1) * blk, :] = l_val
                else:
                    o_ref[pl.ds(r0 + j, blk, stride=d), :] = o_val
                    l_ref[pl.ds(r0 + j, blk, stride=d), :] = l_val

        if groups == 1:
            group(0)
        else:
            def step(g, carry):
                group(g * ITEMS)
                return carry

            lax.fori_loop(0, groups, step, 0)

    cur, prev, nxt = _attn_specs(seq, blk, d, lambda t: t)
    in_specs = [cur, prev, cur, nxt, prev, cur, nxt, pl.BlockSpec((2 * blk, 3 * blk), lambda hp, t: (hp, 0))]
    args = [q, k, k, k, v, v, v, bias]
    if has_sink:
        in_specs.append(pl.BlockSpec((2 * blk, 1), lambda hp, t: (hp, 0)))
        args.append(sink)
    return pl.pallas_call(
        body, out_shape=[jax.ShapeDtypeStruct(q.shape, F32)] * 2, grid=(N_PAIRS, nct),
        in_specs=in_specs, out_specs=[cur, cur], name=name, compiler_params=_params("parallel", "parallel"))(*args)


def _attn_bwd(q, k, v, do, lse, delta, bias, sink, blk, d, name):
    _, seq, _ = q.shape
    chunk, groups, halo = _attn_geometry(blk, d)
    nct = seq // chunk
    has_sink = sink is not None
    n_in = 12 if has_sink else 11
    scale = HEAD_DIM ** -0.5

    def body(*refs):
        q_ref, kp, kc, kn, vp, vc, vn, do_ref, l_ref, d_ref, b_ref = refs[:11]
        s_ref = refs[11] if has_sink else None
        dq_ref, dk_ref, dv_ref, db_ref = refs[n_in:n_in + 4]
        ds_ref = refs[n_in + 4] if has_sink else None
        wk, wv = refs[-2], refs[-1]
        t = pl.program_id(1)

        @pl.when(t == 0)
        def _():
            wk[...] = jnp.zeros_like(wk)
            wv[...] = jnp.zeros_like(wv)
            db_ref[...] = jnp.zeros_like(db_ref)
            if has_sink:
                ds_ref[...] = jnp.zeros_like(ds_ref)

        @pl.when(t > 0)
        def _():
            for w in (wk, wv):
                keep = w[chunk:2 * chunk + halo]
                w[0:chunk + halo] = keep
                w[chunk + halo:2 * chunk + halo] = jnp.zeros((chunk, LANES), F32)

        @pl.when(t < nct)
        def _():
            lane = lax.broadcasted_iota(jnp.int32, (1, LANES), 1)
            left = lane < HEAD_DIM
            bias2 = b_ref[...]
            if d == 1:
                kwin = jnp.concatenate([kp[...], kc[...], kn[...]], axis=0).astype(BF16)
                vwin = jnp.concatenate([vp[...], vc[...], vn[...]], axis=0).astype(BF16)

            def group(r0):
                qss, doss, kcats, scores, dps, lcols, dcols = [], [], [], [], [], [], []
                for j in range(ITEMS):
                    qs = _stack_pair(_item_rows(q_ref, j, r0, blk, d) * scale, left)
                    dos = _stack_pair(_item_rows(do_ref, j, r0, blk, d), left)
                    if d == 1:
                        kcat, vcat = kwin[j * blk:(j + 3) * blk], vwin[j * blk:(j + 3) * blk]
                    else:
                        kcat = jnp.concatenate([_item_rows(r, j, r0, blk, d) for r in (kp, kc, kn)], axis=0).astype(BF16)
                        vcat = jnp.concatenate([_item_rows(r, j, r0, blk, d) for r in (vp, vc, vn)], axis=0).astype(BF16)
                    l2, d2 = _item_rows(l_ref, j, r0, blk, d), _item_rows(d_ref, j, r0, blk, d)
                    lcols.append(jnp.max(jnp.where(left, l2, NEG_INF), axis=-1, keepdims=True))
                    lcols.append(jnp.max(jnp.where(left, NEG_INF, l2), axis=-1, keepdims=True))
                    dcols.append(jnp.sum(jnp.where(lane == 0, d2, 0.0), axis=-1, keepdims=True))
                    dcols.append(jnp.sum(jnp.where(lane == HEAD_DIM, d2, 0.0), axis=-1, keepdims=True))
                    scores.append(_dot_nt(qs, kcat) + bias2 + _item_penalty(t, nct, j, blk, d))
                    dps.append(_dot_nt(dos, vcat))
                    qss.append(qs)
                    doss.append(dos)
                    kcats.append(kcat)
                lcol = jnp.concatenate(lcols, axis=0)
                dcol = jnp.concatenate(dcols, axis=0)
                p = jnp.exp(jnp.concatenate(scores, axis=0) - lcol)
                ds = p * (jnp.concatenate(dps, axis=0) - dcol)
                if has_sink:
                    sgrad = dcol * jnp.exp(jnp.concatenate([s_ref[...]] * ITEMS, axis=0) - lcol)
                for j in range(ITEMS):
                    top, bot = 2 * j * blk, (2 * j + 2) * blk
                    dsj, pj = ds[top:bot], p[top:bot]
                    db_ref[...] += dsj
                    if has_sink:
                        ds_ref[...] -= sgrad[top:bot]
                    dq2 = jnp.dot(dsj.astype(BF16), kcats[j], preferred_element_type=F32) * scale
                    dq_val = jnp.where(left, dq2[:blk], dq2[blk:])
                    if d == 1:
                        dq_ref[j * blk:(j + 1) * blk, :] = dq_val
                    else:
                        dq_ref[pl.ds(r0 + j, blk, stride=d), :] = dq_val
                    dk_new = jnp.dot(jnp.transpose(dsj).astype(BF16), qss[j], preferred_element_type=F32)
                    dv_new = jnp.dot(jnp.transpose(pj).astype(BF16), doss[j], preferred_element_type=F32)
                    for w, new in ((wk, dk_new), (wv, dv_new)):
                        if d == 1:
                            w[chunk + (j - 1) * blk:chunk + (j + 2) * blk, :] += new
                        else:
                            for c in range(3):
                                w[pl.ds(c * chunk + r0 + j, blk, stride=d), :] += new[c * blk:(c + 1) * blk]

            if groups == 1:
                group(0)
            else:
                def step(g, carry):
                    group(g * ITEMS)
                    return carry

                lax.fori_loop(0, groups, step, 0)

        dk_ref[...] = wk[0:chunk]
        dv_ref[...] = wv[0:chunk]

    cur, prev, nxt = _attn_specs(seq, blk, d, lambda t: jnp.minimum(t, nct - 1))
    lag = pl.BlockSpec((None, chunk, LANES), lambda hp, t: (hp, jnp.maximum(t - 1, 0), 0))
    band = pl.BlockSpec((2 * blk, 3 * blk), lambda hp, t: (hp, 0))
    col = pl.BlockSpec((2 * blk, 1), lambda hp, t: (hp, 0))
    in_specs = [cur, prev, cur, nxt, prev, cur, nxt, cur, cur, cur, band]
    args = [q, k, k, k, v, v, v, do, lse, delta, bias]
    out_shape = [jax.ShapeDtypeStruct(q.shape, F32)] * 3 + [jax.ShapeDtypeStruct((N_HEADS * blk, 3 * blk), F32)]
    out_specs = [cur, lag, lag, band]
    if has_sink:
        in_specs.append(col)
        args.append(sink)
        out_shape.append(jax.ShapeDtypeStruct((N_HEADS * blk, 1), F32))
        out_specs.append(col)
    window = pltpu.VMEM((2 * chunk + halo, LANES), F32)
    return pl.pallas_call(
        body, out_shape=out_shape, grid=(N_PAIRS, nct + 1), in_specs=in_specs, out_specs=out_specs,
        scratch_shapes=[window, window], name=name, compiler_params=_params("arbitrary", "arbitrary"))(*args)


def _combine_patterns(outs, lses):
    _, rows, _ = outs[0].shape
    tm = 256
    n = len(outs)

    def body(*refs):
        o_refs, l_refs = refs[:n], refs[n:2 * n]
        y_ref, lse_ref = refs[2 * n], refs[2 * n + 1]
        for hp in range(N_PAIRS):
            ls = [r[hp] for r in l_refs]
            m = functools.reduce(jnp.maximum, ls)
            es = [jnp.exp(l - m) for l in ls]
            den = functools.reduce(lambda a, b: a + b, es)
            num = functools.reduce(lambda a, b: a + b, [e * r[hp] for e, r in zip(es, o_refs)])
            y_ref[:, hp * LANES:(hp + 1) * LANES] = num / den
            lse_ref[hp] = m + jnp.log(den)

    pm = pl.BlockSpec((N_PAIRS, tm, LANES), lambda i: (0, i, 0))
    return pl.pallas_call(
        body, out_shape=[jax.ShapeDtypeStruct((rows, WIDTH), F32), jax.ShapeDtypeStruct((N_PAIRS, rows, LANES), F32)],
        grid=(rows // tm,), in_specs=[pm] * (2 * n), out_specs=[pl.BlockSpec((tm, WIDTH), lambda i: (i, 0)), pm],
        name="combine_a", compiler_params=_params("parallel"))(*outs, *lses)


def _pairs_to_tokens(a):
    _, rows, _ = a.shape
    tm = 256

    def body(a_ref, o_ref):
        o_ref[...] = _get_pairs(a_ref)

    return pl.pallas_call(
        body, out_shape=jax.ShapeDtypeStruct((rows, WIDTH), a.dtype), grid=(rows // tm,),
        in_specs=[pl.BlockSpec((N_PAIRS, tm, LANES), lambda i: (0, i, 0))],
        out_specs=pl.BlockSpec((tm, WIDTH), lambda i: (i, 0)), name="pairs_to_tokens",
        compiler_params=_params("parallel"))(a)


def _attn_bwd_prep(dy, y, name):
    rows = dy.shape[0]
    tm = 256

    def body(dy_ref, y_ref, do_ref, dl_ref):
        dyv = dy_ref[...]
        _put_pairs(do_ref, dyv)
        _put_pairs(dl_ref, _seg_sum(dyv * y_ref[...]))

    tok = pl.BlockSpec((tm, WIDTH), lambda i: (i, 0))
    pm = pl.BlockSpec((N_PAIRS, tm, LANES), lambda i: (0, i, 0))
    return pl.pallas_call(
        body, out_shape=[jax.ShapeDtypeStruct((N_PAIRS, rows, LANES), F32)] * 2, grid=(rows // tm,),
        in_specs=[tok, tok], out_specs=[pm, pm], name=name, compiler_params=_params("parallel"))(dy, y)


def _tile_gain(g, reps):
    return jnp.tile(g[None, :], (1, reps))


def _local_step(x, p, target, big, small):
    rel_table = small["rel_table"]
    buckets_a = [_band_buckets(blk, d) for blk, d in DILATED]
    buckets_b = _band_buckets(BLK_B, 1)
    bias_a = [_bias_tiles(rel_table, bk, 0, "bias_a").reshape(N_HEADS * bk.shape[0], -1) for bk in buckets_a]
    bias_b = _bias_tiles(rel_table, buckets_b, N_HEADS, "bias_b").reshape(N_HEADS * BLK_B, -1)

    saved = []
    for l in range(DEPTH):
        w = big[l]
        g_mix, g_ffn, g_ple = (small[n][l][None, :] for n in ("norm_mix_g", "norm_ffn_g", "norm_ple_g"))
        gqa, gka, gqb = (_tile_gain(small[n][l], N_HEADS) for n in ("qnorm_a_g", "knorm_a_g", "qnorm_b_g"))
        gkb = _tile_gain(small["knorm_b_g"][l], N_KV_B)
        sink = jnp.repeat(small["sink_b"][l], BLK_B)[:, None]

        h = _rms_fwd(x, g_mix, "rms_mix")
        proj = _mm(h, w["w_in"], "nn", F32, "mm_in")
        qa, ka, va, qb, kb, vb = _qknorm_fwd(proj, gqa, gka, gqb, gkb)
        outs, lses = [], []
        for (blk, d), bias in zip(DILATED, bias_a):
            o, ls = _attn_fwd(qa, ka, va, bias, None, blk, d, f"attn_a{d}_fwd")
            outs.append(o)
            lses.append(ls)
        ya, lse_a = _combine_patterns(outs, lses)
        yb, lse_b = _attn_fwd(qb, kb, vb, bias_b, sink, BLK_B, 1, "attn_b_fwd")
        yb = _pairs_to_tokens(yb)
        ca = _mm(ya, w["w_branch_a"], "nn", F32, "mm_branch_a")
        cb = _mm(yb, w["w_branch_b"], "nn", F32, "mm_branch_b")

        def gate(ca_, cb_, ga_, gb_):
            return (_sigmoid(ga_) * ca_ + _sigmoid(gb_) * cb_,)

        (merged,) = _ew(gate, [(ca, 0), (cb, 0), (proj, OFF_GA), (proj, OFF_GB)], [BF16],
                        width=D_MODEL, bw=256, name="gate")
        x1 = _mm(merged, w["w_out"], "nn", F32, "mm_out", res=x)

        h2 = _rms_fwd(x1, g_ffn, "rms_ffn")
        a = _mm(h2, w["w_ffn_gate"], "nn", F32, "mm_ffn_gate")
        u = _mm(h2, w["w_ffn_up"], "nn", F32, "mm_ffn_up")

        def swiglu(a_, u_):
            return ((a_ * _sigmoid(a_)) * u_,)

        (hid,) = _ew(swiglu, [(a, 0), (u, 0)], [BF16], width=D_FF, bw=D_FF, name="swiglu")
        x2 = _mm(hid, w["w_ffn_down"], "nn", F32, "mm_ffn_down", res=x1)

        h3 = _rms_fwd(x2, g_ple, "rms_ple")
        z = _mm(h3, w["w_ple_gate"], "nn", F32, "mm_ple_gate")
        e = _mm(p[l], w["w_ple_proj"], "nn", F32, "mm_ple_proj")

        def ple(x2_, z_, e_):
            return (x2_ + _sigmoid(z_) * e_,)

        (x3,) = _ew(ple, [(x2, 0), (z, 0), (e, 0)], [F32], width=D_MODEL, bw=D_MODEL, name="ple")
        saved.append(dict(x0=x, h=h, proj=proj, qa=qa, ka=ka, va=va, qb=qb, kb=kb, vb=vb, ya=ya, lse_a=lse_a,
                          yb=yb, lse_b=lse_b, ca=ca, cb=cb, merged=merged, x1=x1, h2=h2, a=a, u=u, hid=hid,
                          x2=x2, h3=h3, z=z, e=e))
        x = x3

    dx, loss_acc = _loss_grad(x, target)
    loss = loss_acc[0, 0]

    gbig = [{} for _ in range(DEPTH)]
    gsmall = {n: [None] * DEPTH for n in SMALL if n != "rel_table"}
    dtable_a = jnp.zeros((N_HEADS, NUM_BUCKETS), F32)
    dtable_b = jnp.zeros((N_HEADS, NUM_BUCKETS), F32)

    for l in reversed(range(DEPTH)):
        sv = saved[l]
        w = big[l]
        g_mix, g_ffn, g_ple = (small[n][l][None, :] for n in ("norm_mix_g", "norm_ffn_g", "norm_ple_g"))
        gqa, gka, gqb = (_tile_gain(small[n][l], N_HEADS) for n in ("qnorm_a_g", "knorm_a_g", "qnorm_b_g"))
        gkb = _tile_gain(small["knorm_b_g"][l], N_KV_B)
        sink = jnp.repeat(small["sink_b"][l], BLK_B)[:, None]

        def ple_bwd(dx_, z_, e_):
            s = _sigmoid(z_)
            return dx_ * s, dx_ * e_ * (s * (1.0 - s))

        de, dz = _ew(ple_bwd, [(dx, 0), (sv["z"], 0), (sv["e"], 0)], [BF16, BF16], width=D_MODEL, bw=D_MODEL,
                     name="ple_bwd")
        gbig[l]["w_ple_proj"] = _mm(p[l], de, "tn", F32, "mm_d_ple_proj")
        gbig[l]["w_ple_gate"] = _mm(sv["h3"], dz, "tn", F32, "mm_d_ple_gate")
        dh3 = _mm(dz, w["w_ple_gate"], "nt", F32, "mm_dh3")
        dx, gsmall["norm_ple_g"][l] = _rms_bwd(sv["x2"], g_ple, dh3, dx, "rms_ple_bwd")

        dhid = _mm(dx, w["w_ffn_down"], "nt", F32, "mm_dhid")
        gbig[l]["w_ffn_down"] = _mm(sv["hid"], dx, "tn", F32, "mm_d_ffn_down")

        def swiglu_bwd(a_, u_, dh_):
            s = _sigmoid(a_)
            return dh_ * u_ * (s * (1.0 + a_ * (1.0 - s))), dh_ * (a_ * s)

        da, du = _ew(swiglu_bwd, [(sv["a"], 0), (sv["u"], 0), (dhid, 0)], [BF16, BF16], width=D_FF, bw=D_FF,
                     name="swiglu_bwd")
        gbig[l]["w_ffn_gate"] = _mm(sv["h2"], da, "tn", F32, "mm_d_ffn_gate")
        gbig[l]["w_ffn_up"] = _mm(sv["h2"], du, "tn", F32, "mm_d_ffn_up")
        dh2 = _mm(da, w["w_ffn_gate"], "nt", F32, "mm_dh2_gate")
        dh2 = _mm(du, w["w_ffn_up"], "nt", F32, "mm_dh2_up", res=dh2)
        dx, gsmall["norm_ffn_g"][l] = _rms_bwd(sv["x1"], g_ffn, dh2, dx, "rms_ffn_bwd")

        dmerged = _mm(dx, w["w_out"], "nt", F32, "mm_dmerged")
        gbig[l]["w_out"] = _mm(sv["merged"], dx, "tn", F32, "mm_d_out")

        def gate_bwd(dm_, ca_, cb_, ga_, gb_):
            sa, sb = _sigmoid(ga_), _sigmoid(gb_)
            return dm_ * sa, dm_ * sb, dm_ * ca_ * (sa * (1.0 - sa)), dm_ * cb_ * (sb * (1.0 - sb))

        dca, dcb, dga, dgb = _ew(gate_bwd, [(dmerged, 0), (sv["ca"], 0), (sv["cb"], 0), (sv["proj"], OFF_GA),
                                            (sv["proj"], OFF_GB)], [BF16, BF16, BF16, BF16],
                                 width=D_MODEL, bw=256, name="gate_bwd")
        dgab = jnp.concatenate([dga, dgb], axis=1)
        gbig[l]["w_branch_a"] = _mm(sv["ya"], dca, "tn", F32, "mm_d_branch_a")
        gbig[l]["w_branch_b"] = _mm(sv["yb"], dcb, "tn", F32, "mm_d_branch_b")
        dya = _mm(dca, w["w_branch_a"], "nt", F32, "mm_dya")
        dyb = _mm(dcb, w["w_branch_b"], "nt", F32, "mm_dyb")

        dya, delta_a = _attn_bwd_prep(dya, sv["ya"], "attn_a_bwd_prep")
        dyb, delta_b = _attn_bwd_prep(dyb, sv["yb"], "attn_b_bwd_prep")

        dqa, dka, dva = [], [], []
        for (blk, d), bias, bk in zip(DILATED, bias_a, buckets_a):
            dq_, dk_, dv_, db_ = _attn_bwd(sv["qa"], sv["ka"], sv["va"], dya, sv["lse_a"], delta_a, bias, None, blk, d,
                                           f"attn_a{d}_bwd")
            dqa.append(dq_)
            dka.append(dk_)
            dva.append(dv_)
            dtable_a = dtable_a + _table_grad(db_.reshape(N_HEADS, blk, 3 * blk), bk, "table_grad_a")
        dqb, dkb, dvb, db_, dsink = _attn_bwd(sv["qb"], sv["kb"], sv["vb"], dyb, sv["lse_b"], delta_b, bias_b, sink,
                                              BLK_B, 1, "attn_b_bwd")
        dtable_b = dtable_b + _table_grad(db_.reshape(N_HEADS, BLK_B, 3 * BLK_B), buckets_b, "table_grad_b")
        gsmall["sink_b"][l] = dsink.reshape(N_HEADS, BLK_B).sum(axis=1)

        dproj, pqa, pka, pqb, pkb = _qknorm_bwd(sv["proj"], gqa, gka, gqb, gkb, dqa, dka, dva, dqb, dkb, dvb, dgab)
        gsmall["qnorm_a_g"][l] = pqa.reshape(N_HEADS, HEAD_DIM).sum(0)
        gsmall["knorm_a_g"][l] = pka.reshape(N_HEADS, HEAD_DIM).sum(0)
        gsmall["qnorm_b_g"][l] = pqb.reshape(N_HEADS, HEAD_DIM).sum(0)
        gsmall["knorm_b_g"][l] = pkb.reshape(N_KV_B, HEAD_DIM).sum(0)
        gbig[l]["w_in"] = _mm(sv["h"], dproj, "tn", F32, "mm_d_in")
        dh = _mm(dproj, w["w_in"], "nt", F32, "mm_dh")
        dx, gsmall["norm_mix_g"][l] = _rms_bwd(sv["x0"], g_mix, dh, dx, "rms_mix_bwd")
        gsmall["norm_mix_g"][l] = gsmall["norm_mix_g"][l][0]
        gsmall["norm_ffn_g"][l] = gsmall["norm_ffn_g"][l][0]
        gsmall["norm_ple_g"][l] = gsmall["norm_ple_g"][l][0]

    gsmall = {n: jnp.stack(v) for n, v in gsmall.items()}
    gsmall["rel_table"] = jnp.concatenate([dtable_a, dtable_b], axis=0).T
    return loss, dx, gbig, gsmall


def _place():
    return lax.axis_index("x"), lax.axis_index("y"), lax.axis_index("c")


def _flip(v, bit):
    return 1 - v if bit else v


CHIP_RELATIONS = ((0, 1), (1, 0), (1, 1))
ANY = pl.BlockSpec(memory_space=pl.ANY)


def _allgather_body(w_refs, out_refs, send_sems, recv_sems):
    x, y, c = _place()
    chips = [(_flip(x, a), _flip(y, b)) for a, b in CHIP_RELATIONS]

    def make(g):
        w_ref, out_ref = w_refs[g], out_refs[g]
        half = w_ref.shape[0] // 2

        def part(px, py, pc):
            return out_ref.at[2 * px + py, pl.ds(pc * half, half), :]

        def copy(k, block, to, src=None):
            return pltpu.make_async_remote_copy(
                src_ref=part(*block) if src is None else src, dst_ref=part(*block),
                send_sem=send_sems.at[7 * g + k], recv_sem=recv_sems.at[7 * g + k], device_id=to,
                device_id_type=MESH_ID)

        own = pltpu.make_async_remote_copy(
            src_ref=w_ref, dst_ref=out_ref.at[2 * x + y], send_sem=send_sems.at[7 * g + 6],
            recv_sem=recv_sems.at[7 * g + 6], device_id=(x, y, 1 - c), device_id_type=MESH_ID)
        first = [copy(k, (x, y, c), (*chip, c), src=w_ref.at[pl.ds(c * half, half), :]) for k, chip in enumerate(chips)]
        passed = [copy(3 + k, (*chip, c), (x, y, 1 - c)) for k, chip in enumerate(chips)]
        arrive = [copy(k, (*chip, c), (x, y, c)) for k, chip in enumerate(chips)]
        arrive2 = [copy(3 + k, (*chip, 1 - c), (x, y, c)) for k, chip in enumerate(chips)]
        return own, first, passed, arrive, arrive2

    made = [make(g) for g in range(len(w_refs))]
    for own, first, _, _, _ in made:
        own.start()
        for cp in first:
            cp.start()
    for _, _, passed, arrive, _ in made:
        for k in range(3):
            arrive[k].wait_recv()
            passed[k].start()
    for own, first, passed, _, arrive2 in made:
        for k in range(3):
            arrive2[k].wait_recv()
        own.wait_recv()
        for cp in first + passed + [own]:
            cp.wait_send()


def _allgather(shards):
    n = len(shards)

    def body(*refs):
        _allgather_body(refs[:n], refs[n:2 * n], refs[2 * n], refs[2 * n + 1])

    return pl.pallas_call(
        body, out_shape=[jax.ShapeDtypeStruct((N_CHIPS,) + s.shape, s.dtype) for s in shards],
        in_specs=[ANY] * n, out_specs=[ANY] * n,
        scratch_shapes=[pltpu.SemaphoreType.DMA((7 * n,)), pltpu.SemaphoreType.DMA((7 * n,))],
        name="allgather_weights")(*shards)


def _handshake(peers):
    barrier = pltpu.get_barrier_semaphore()
    for peer in peers:
        pl.semaphore_signal(barrier, inc=1, device_id=peer, device_id_type=MESH_ID)
    pl.semaphore_wait(barrier, len(peers))


def _allgather_async(shards, collective_id, name):
    n = len(shards)
    hbm = pltpu.MemorySpace.HBM
    w_refs = [jax.new_ref(s, memory_space=hbm) for s in shards]
    out_refs = [jax.empty_ref(jax.ShapeDtypeStruct((N_CHIPS,) + s.shape, s.dtype), memory_space=hbm) for s in shards]

    @pl.kernel(mesh=plsc.ScalarSubcoreMesh(axis_name="sequencer", num_cores=1), name=name,
               scratch_types=(pltpu.SemaphoreType.DMA((7 * n,)), pltpu.SemaphoreType.DMA((7 * n,))),
               compiler_params=pltpu.CompilerParams(collective_id=collective_id))
    def launch(send_sems, recv_sems):
        x, y, c = _place()
        _handshake([(x, y, 1 - c)] + [(_flip(x, a), _flip(y, b), c) for a, b in CHIP_RELATIONS])
        _allgather_body(w_refs, out_refs, send_sems, recv_sems)

    launch()
    return [r[...] for r in out_refs]


def _half_tile(half):
    return max(t for t in range(16, 1025, 16) if half % t == 0)


def _sibling_halves(gsends):
    n = len(gsends)

    def body(*refs):
        g_refs, out_refs, send_sems, recv_sems = refs[:n], refs[n:2 * n], refs[2 * n], refs[2 * n + 1]
        x, y, c = _place()
        cps = []
        for g in range(n):
            half = g_refs[g].shape[1] // 2
            cps.append(pltpu.make_async_remote_copy(
                src_ref=g_refs[g].at[:, pl.ds((1 - c) * half, half), :], dst_ref=out_refs[g],
                send_sem=send_sems.at[g], recv_sem=recv_sems.at[g], device_id=(x, y, 1 - c), device_id_type=MESH_ID))
        for cp in cps:
            cp.start()
        for cp in cps:
            cp.wait_recv()
        for cp in cps:
            cp.wait_send()

    return pl.pallas_call(
        body, out_shape=[jax.ShapeDtypeStruct((s.shape[0], s.shape[1] // 2, s.shape[2]), s.dtype) for s in gsends],
        in_specs=[ANY] * n, out_specs=[ANY] * n,
        scratch_shapes=[pltpu.SemaphoreType.DMA((n,)), pltpu.SemaphoreType.DMA((n,))], name="rs_sibling_halves")(*gsends)


def _chip_sums(gsend, sib, place):
    n, rows, cols = gsend.shape
    half = rows // 2
    tm = _half_tile(half)
    nblk = half // tm

    def body(s_ref, g_ref, sib_ref, o_ref):
        o_ref[0] = (g_ref[0].astype(F32) + sib_ref[0].astype(F32)).astype(o_ref.dtype)

    grid_spec = pltpu.PrefetchScalarGridSpec(
        num_scalar_prefetch=1, grid=(n, nblk),
        in_specs=[pl.BlockSpec((1, tm, cols), lambda k, i, s: (jnp.bitwise_xor(s[0], k), s[1] * nblk + i, 0)),
                  pl.BlockSpec((1, tm, cols), lambda k, i, s: (jnp.bitwise_xor(s[0], k), i, 0))],
        out_specs=pl.BlockSpec((1, tm, cols), lambda k, i, s: (k, i, 0)))
    return pl.pallas_call(
        body, out_shape=jax.ShapeDtypeStruct((n, half, cols), BF16), grid_spec=grid_spec,
        name="rs_chip_sums", compiler_params=_params("parallel", "parallel"))(place, gsend, sib)


def _exchange_chip_sums(tsends):
    n = len(tsends)

    def body(*refs):
        t_refs, out_refs, send_sems, recv_sems = refs[:n], refs[n:2 * n], refs[2 * n], refs[2 * n + 1]
        x, y, c = _place()
        cps = []
        for g in range(n):
            for k, (a, b) in enumerate(CHIP_RELATIONS):
                cps.append(pltpu.make_async_remote_copy(
                    src_ref=t_refs[g].at[k + 1], dst_ref=out_refs[g].at[k], send_sem=send_sems.at[3 * g + k],
                    recv_sem=recv_sems.at[3 * g + k], device_id=(_flip(x, a), _flip(y, b), c), device_id_type=MESH_ID))
        for cp in cps:
            cp.start()
        for cp in cps:
            cp.wait_recv()
        for cp in cps:
            cp.wait_send()

    return pl.pallas_call(
        body, out_shape=[jax.ShapeDtypeStruct((3,) + s.shape[1:], s.dtype) for s in tsends],
        in_specs=[ANY] * n, out_specs=[ANY] * n,
        scratch_shapes=[pltpu.SemaphoreType.DMA((3 * n,)), pltpu.SemaphoreType.DMA((3 * n,))], name="rs_exchange")(*tsends)


def _final_sum(tsend, recv, place):
    n, half, cols = tsend.shape
    tm = _half_tile(half)
    nblk = half // tm

    def body(s_ref, t_ref, r_ref, o_ref):
        o_ref[...] = ((t_ref[0].astype(F32) + r_ref[0].astype(F32)) + r_ref[1].astype(F32)) + r_ref[2].astype(F32)

    grid_spec = pltpu.PrefetchScalarGridSpec(
        num_scalar_prefetch=1, grid=(nblk,),
        in_specs=[pl.BlockSpec((1, tm, cols), lambda i, s: (0, i, 0)), pl.BlockSpec((n - 1, tm, cols), lambda i, s: (0, i, 0))],
        out_specs=pl.BlockSpec((tm, cols), lambda i, s: (s[1] * nblk + i, 0)))
    return pl.pallas_call(
        body, out_shape=jax.ShapeDtypeStruct((2 * half, cols), F32), grid_spec=grid_spec, name="rs_final_sum",
        compiler_params=_params("parallel"))(place, tsend, recv)


def _join_halves(gfulls):
    n = len(gfulls)

    def body(*refs):
        g_refs, out_refs, send_sems, recv_sems = refs[:n], refs[n:2 * n], refs[2 * n], refs[2 * n + 1]
        x, y, c = _place()

        def copy(g, pc):
            half = g_refs[g].shape[0] // 2
            return pltpu.make_async_remote_copy(
                src_ref=g_refs[g].at[pl.ds(pc * half, half), :], dst_ref=out_refs[g].at[pl.ds(pc * half, half), :],
                send_sem=send_sems.at[g], recv_sem=recv_sems.at[g], device_id=(x, y, 1 - c), device_id_type=MESH_ID)

        mine = [copy(g, c) for g in range(n)]
        for cp in mine:
            cp.start()
        for g in range(n):
            copy(g, 1 - c).wait_recv()
        for cp in mine:
            cp.wait_send()

    return pl.pallas_call(
        body, out_shape=[jax.ShapeDtypeStruct(s.shape, s.dtype) for s in gfulls], in_specs=[ANY] * n, out_specs=[ANY] * n,
        input_output_aliases={g: g for g in range(n)},
        scratch_shapes=[pltpu.SemaphoreType.DMA((n,)), pltpu.SemaphoreType.DMA((n,))], name="rs_join_halves")(*gfulls)


def _allreduce_small(v):
    rows, cols = v.shape

    def body(v_ref, out_ref, buf, send_sems, recv_sems):
        x, y, c = _place()
        cps = []
        for k in range(1, 8):
            peer = (_flip(x, (k >> 2) & 1), _flip(y, (k >> 1) & 1), _flip(c, k & 1))
            cps.append(pltpu.make_async_remote_copy(
                src_ref=v_ref, dst_ref=buf.at[k - 1], send_sem=send_sems.at[k - 1], recv_sem=recv_sems.at[k - 1],
                device_id=peer, device_id_type=MESH_ID))
        for cp in cps:
            cp.start()
        for cp in cps:
            cp.wait_recv()
        for cp in cps:
            cp.wait_send()
        t0 = v_ref[...] + buf[0]
        t1 = buf[1] + buf[2]
        t2 = buf[3] + buf[4]
        t3 = buf[5] + buf[6]
        out_ref[...] = (t0 + t1) + (t2 + t3)

    vm = pl.BlockSpec(memory_space=pltpu.VMEM)
    return pl.pallas_call(
        body, out_shape=jax.ShapeDtypeStruct((rows, cols), F32), in_specs=[vm], out_specs=vm,
        scratch_shapes=[pltpu.VMEM((7, rows, cols), F32), pltpu.SemaphoreType.DMA((7,)), pltpu.SemaphoreType.DMA((7,))],
        name="allreduce_small")(v)


BIG_INFO = {n: (shape, ax) for n, shape, ax in BIG}
GROUPS = (("w_in",), ("w_ffn_gate", "w_ffn_up"), ("w_out", "w_ffn_down", "w_ple_gate"),
          ("w_branch_a", "w_branch_b", "w_ple_proj"))


def _shard_shape(name):
    (k, m), ax = BIG_INFO[name]
    return (k // N_CHIPS, m) if ax == 0 else (k, m // N_CHIPS)


def _group_rows(group):
    offs, off = {}, 0
    for n in group:
        offs[n] = off
        off += _shard_shape(n)[0]
    return offs, off


def _pack_groups(shards, layer, dtype):
    return [jnp.concatenate([shards[n][layer].astype(dtype) for n in group], axis=0) for group in GROUPS]


def _unpack_full(gathered):
    out = {}
    for group, arr in zip(GROUPS, gathered):
        offs, _ = _group_rows(group)
        for n in group:
            rows, cols = _shard_shape(n)
            (k, m), ax = BIG_INFO[n]
            slab = arr[:, offs[n]:offs[n] + rows]
            out[n] = slab.reshape(k, m) if ax == 0 else jnp.transpose(slab, (1, 0, 2)).reshape(k, m)
    return out


def _pack_grads(gfull):
    out = []
    for group in GROUPS:
        parts = []
        for n in group:
            rows, cols = _shard_shape(n)
            ax = BIG_INFO[n][1]
            slab = (gfull[n].reshape(N_CHIPS, rows, cols) if ax == 0
                    else jnp.transpose(gfull[n].reshape(rows, N_CHIPS, cols), (1, 0, 2)))
            parts.append(slab.astype(BF16))
        out.append(jnp.concatenate(parts, axis=1))
    return out


def _reduce_scatter(gsends, place):
    tsends = [_chip_sums(g, s, place) for g, s in zip(gsends, _sibling_halves(gsends))]
    return _join_halves([_final_sum(t, r, place) for t, r in zip(tsends, _exchange_chip_sums(tsends))])


SMALL_SHAPES = {"rel_table": (NUM_BUCKETS, 2 * N_HEADS), "norm_mix_g": (DEPTH, D_MODEL), "qnorm_a_g": (DEPTH, HEAD_DIM),
                "knorm_a_g": (DEPTH, HEAD_DIM), "qnorm_b_g": (DEPTH, HEAD_DIM), "knorm_b_g": (DEPTH, HEAD_DIM),
                "sink_b": (DEPTH, N_HEADS), "norm_ffn_g": (DEPTH, D_MODEL), "norm_ple_g": (DEPTH, D_MODEL)}


def _pack_small(vals):
    flat = jnp.concatenate([vals[n].astype(F32).reshape(-1) for n in SMALL])
    flat = jnp.concatenate([flat, jnp.zeros((SMALL_ROWS * LANES - flat.shape[0],), F32)])
    return flat.reshape(SMALL_ROWS, LANES)


def _unpack_small(packed):
    flat, out, off = packed.reshape(-1), {}, 0
    for n in SMALL:
        size = math.prod(SMALL_SHAPES[n])
        out[n] = flat[off:off + size].reshape(SMALL_SHAPES[n])
        off += size
    return out


def _adamw(w, gs, g_row, m, v, name):
    c1 = 1.0 - ADAM_B1 ** ADAM_STEP
    c2 = 1.0 - ADAM_B2 ** ADAM_STEP
    total, width = w.shape
    n_layers = len(gs)
    per = total // n_layers
    tm = max(t for t in (512, 256, 128, 64, 32, 16, 8) if per % t == 0 and g_row % t == 0)
    nblk = per // tm

    def body(*refs):
        w_ref, g_refs = refs[0], refs[1:1 + n_layers]
        m_ref, v_ref, og, od, om, ov = refs[1 + n_layers:]
        layer = pl.program_id(0) // nblk
        g = g_refs[0][...]
        for l in range(1, n_layers):
            g = jnp.where(layer == l, g_refs[l][...], g)
        m_new = ADAM_B1 * m_ref[...] + (1.0 - ADAM_B1) * g
        v_new = ADAM_B2 * v_ref[...] + (1.0 - ADAM_B2) * (g * g)
        og[...] = g
        od[...] = -ADAM_LR * ((m_new / c1) / (jnp.sqrt(v_new / c2) + ADAM_EPS) + ADAM_WD * w_ref[...])
        om[...] = m_new
        ov[...] = v_new

    row = pl.BlockSpec((tm, width), lambda i: (i, 0))
    g_specs = [pl.BlockSpec((tm, width), lambda i, l=l: (g_row // tm + jnp.clip(i - l * nblk, 0, nblk - 1), 0))
               for l in range(n_layers)]
    return pl.pallas_call(
        body, out_shape=[jax.ShapeDtypeStruct((total, width), F32)] * 4, grid=(total // tm,),
        in_specs=[row] + g_specs + [row, row], out_specs=[row] * 4, name=name,
        compiler_params=_params("parallel"))(w, *gs, m, v)


def kernel(x, p, rel_table, norm_mix_g, w_in, qnorm_a_g, knorm_a_g, qnorm_b_g, knorm_b_g, sink_b, w_branch_a, w_branch_b, w_out, norm_ffn_g, w_ffn_gate, w_ffn_up, w_ffn_down, norm_ple_g, w_ple_gate, w_ple_proj, loss_target, m_rel_table, m_norm_mix_g, m_w_in, m_qnorm_a_g, m_knorm_a_g, m_qnorm_b_g, m_knorm_b_g, m_sink_b, m_w_branch_a, m_w_branch_b, m_w_out, m_norm_ffn_g, m_w_ffn_gate, m_w_ffn_up, m_w_ffn_down, m_norm_ple_g, m_w_ple_gate, m_w_ple_proj, v_rel_table, v_norm_mix_g, v_w_in, v_qnorm_a_g, v_knorm_a_g, v_qnorm_b_g, v_knorm_b_g, v_sink_b, v_w_branch_a, v_w_branch_b, v_w_out, v_norm_ffn_g, v_w_ffn_gate, v_w_ffn_up, v_w_ffn_down, v_norm_ple_g, v_w_ple_gate, v_w_ple_proj):
    given = dict(locals())
    weights = {n: given[n] for n in WEIGHTS}
    moments_m = {n: given["m_" + n] for n in WEIGHTS}
    moments_v = {n: given["v_" + n] for n in WEIGHTS}
    xi, yi, ci = _place()
    place = jnp.stack([2 * xi + yi, ci]).astype(jnp.int32)

    gathered0 = _allgather(_pack_groups(weights, 0, BF16))
    gathered0, shards1 = lax.optimization_barrier((gathered0, _pack_groups(weights, 1, BF16)))
    gathered1 = _allgather_async(shards1, 1, "allgather_weights_layer1")
    big = [_unpack_full(gathered0), _unpack_full(gathered1)]
    small = {n: weights[n] for n in SMALL}

    loss, dx, gbig, gsmall = _local_step(x[0], p[:, 0], loss_target[0], big, small)

    greds = [_reduce_scatter(_pack_grads(gbig[l]), place) for l in range(DEPTH)]

    grads, delta, new_m, new_v = {}, {}, {}, {}
    for gi, group in enumerate(GROUPS):
        offs, _ = _group_rows(group)
        for n in group:
            shape = weights[n].shape
            two_d = lambda a: a.reshape(shape[0] * shape[1], shape[2])
            outs = _adamw(two_d(weights[n]), [greds[l][gi] for l in range(DEPTH)], offs[n], two_d(moments_m[n]),
                          two_d(moments_v[n]), "adamw_" + n)
            grads[n], delta[n], new_m[n], new_v[n] = (o.reshape(shape) for o in outs)
    small_grads = _allreduce_small(_pack_small(gsmall))
    g_, d_, m_, v_ = _adamw(_pack_small(weights), [small_grads], 0, _pack_small(moments_m), _pack_small(moments_v),
                            "adamw_small")
    grads.update(_unpack_small(g_))
    delta.update(_unpack_small(d_))
    new_m.update(_unpack_small(m_))
    new_v.update(_unpack_small(v_))

    loss = lax.psum(loss, ("x", "y", "c"))
    return (loss, dx[None], *[grads[n] for n in WEIGHTS], *[delta[n] for n in WEIGHTS],
            *[new_m[n] for n in WEIGHTS], *[new_v[n] for n in WEIGHTS])
```

```python
import functools
import math

import jax
import jax.numpy as jnp
from jax import lax
from jax.experimental import pallas as pl
from jax.experimental.pallas import tpu as pltpu
from jax.experimental.pallas import tpu_sc as plsc

F32 = jnp.float32
BF16 = jnp.bfloat16
MESH_ID = pl.DeviceIdType.MESH

SEQ = 2048
D_MODEL = 1024
DEPTH = 2
HEAD_DIM = 64
N_HEADS = 8
WIDTH = N_HEADS * HEAD_DIM
N_PAIRS = 4
ITEMS = 4
N_KV_B = 2
PLE_DIM = 256
D_FF = 2816
D_IN = 4352
OFF_QA, OFF_KA, OFF_VA, OFF_QB, OFF_KB, OFF_VB, OFF_GA, OFF_GB = 0, 512, 1024, 1536, 2048, 2176, 2304, 3328
DILATED = ((64, 1), (64, 4), (64, 16))
BLK_B = 128
NUM_BUCKETS = 32
MAX_DISTANCE = 1024
RMS_EPS = 1e-6
NEG_INF = -1e30
LANES = 128
VMEM_LIMIT = 48 * 1024 * 1024

ADAM_LR, ADAM_B1, ADAM_B2, ADAM_EPS, ADAM_WD, ADAM_STEP = 0.001, 0.9, 0.999, 1e-08, 0.01, 10

BIG = (
    ("w_in", (D_MODEL, D_IN), 1),
    ("w_branch_a", (WIDTH, D_MODEL), 1),
    ("w_branch_b", (WIDTH, D_MODEL), 1),
    ("w_out", (D_MODEL, D_MODEL), 0),
    ("w_ffn_gate", (D_MODEL, D_FF), 1),
    ("w_ffn_up", (D_MODEL, D_FF), 1),
    ("w_ffn_down", (D_FF, D_MODEL), 0),
    ("w_ple_gate", (D_MODEL, D_MODEL), 0),
    ("w_ple_proj", (PLE_DIM, D_MODEL), 1),
)
SMALL = ("rel_table", "norm_mix_g", "qnorm_a_g", "knorm_a_g", "qnorm_b_g", "knorm_b_g", "sink_b",
         "norm_ffn_g", "norm_ple_g")
WEIGHTS = ("rel_table", "norm_mix_g", "w_in", "qnorm_a_g", "knorm_a_g", "qnorm_b_g", "knorm_b_g", "sink_b",
           "w_branch_a", "w_branch_b", "w_out", "norm_ffn_g", "w_ffn_gate", "w_ffn_up", "w_ffn_down",
           "norm_ple_g", "w_ple_gate", "w_ple_proj")
N_CHIPS = 4
SMALL_ROWS = 64


def _params(*sem):
    return pltpu.CompilerParams(dimension_semantics=sem, vmem_limit_bytes=VMEM_LIMIT)


def _pick(dim, target):
    for t in (target, 512, 256, 128, 64, 32, 16, 8):
        if t <= target and dim % t == 0:
            return t
    return dim


def _mm(a, b, mode, out_dtype, name, res=None, tm=512, tn=512):
    if mode == "nn":
        (m, k), (_, n) = a.shape, b.shape
    elif mode == "nt":
        (m, k), (n, _) = a.shape, b.shape
    else:
        (k, m), (_, n) = a.shape, b.shape
    tm, tn = _pick(m, tm), _pick(n, tn)
    a_spec = pl.BlockSpec((k, tm), lambda i, j: (0, i)) if mode == "tn" else pl.BlockSpec((tm, k), lambda i, j: (i, 0))
    b_spec = pl.BlockSpec((tn, k), lambda i, j: (j, 0)) if mode == "nt" else pl.BlockSpec((k, tn), lambda i, j: (0, j))
    dims = {"nn": (((1,), (0,)), ((), ())), "nt": (((1,), (1,)), ((), ())), "tn": (((0,), (0,)), ((), ()))}[mode]
    has_res = res is not None

    def body(*refs):
        a_ref, b_ref = refs[0], refs[1]
        o_ref = refs[-1]
        acc = lax.dot_general(a_ref[...].astype(BF16), b_ref[...].astype(BF16), dims, preferred_element_type=F32)
        if has_res:
            acc = acc + refs[2][...]
        o_ref[...] = acc.astype(out_dtype)

    in_specs = [a_spec, b_spec]
    args = [a, b]
    if has_res:
        in_specs.append(pl.BlockSpec((tm, tn), lambda i, j: (i, j)))
        args.append(res)
    return pl.pallas_call(
        body, out_shape=jax.ShapeDtypeStruct((m, n), out_dtype), grid=(m // tm, n // tn), in_specs=in_specs,
        out_specs=pl.BlockSpec((tm, tn), lambda i, j: (i, j)), name=name,
        compiler_params=_params("parallel", "parallel"))(*args)


def _ew(fn, ins, out_dtypes, *, width, bw, name, vecs=(), tm=256):
    rows = ins[0][0].shape[0]
    tm = _pick(rows, tm)
    n_in = len(ins) + len(vecs)

    def col_map(off_blocks):
        return lambda i, j: (i, off_blocks + j)

    in_specs = [pl.BlockSpec((tm, bw), col_map(off // bw)) for _, off in ins]
    in_specs += [pl.BlockSpec((1, bw), lambda i, j: (0, j)) for _ in vecs]

    def body(*refs):
        outs = fn(*[r[...] for r in refs[:n_in]])
        for r, o in zip(refs[n_in:], outs):
            r[...] = o.astype(r.dtype)

    return pl.pallas_call(
        body, out_shape=[jax.ShapeDtypeStruct((rows, width), dt) for dt in out_dtypes],
        grid=(rows // tm, width // bw), in_specs=in_specs,
        out_specs=[pl.BlockSpec((tm, bw), lambda i, j: (i, j)) for _ in out_dtypes],
        name=name, compiler_params=_params("parallel", "parallel"))(*[a for a, _ in ins], *vecs)


def _sigmoid(x):
    return 1.0 / (1.0 + jnp.exp(-x))


def _seg_sum(v):
    outs = []
    for k in range(v.shape[1] // LANES):
        vp = v[:, k * LANES:(k + 1) * LANES]
        left = lax.broadcasted_iota(jnp.int32, vp.shape, 1) < HEAD_DIM
        sl = jnp.sum(jnp.where(left, vp, 0.0), axis=-1, keepdims=True)
        sr = jnp.sum(jnp.where(left, 0.0, vp), axis=-1, keepdims=True)
        outs.append(jnp.where(left, sl, sr))
    return outs[0] if len(outs) == 1 else jnp.concatenate(outs, axis=1)


def _seg_rstd(x):
    return lax.rsqrt(_seg_sum(x * x) * (1.0 / HEAD_DIM) + RMS_EPS)


def _rms_fwd(x, g, name):
    rows, d = x.shape
    tm = 256

    def body(x_ref, g_ref, h_ref):
        xv = x_ref[...]
        r = lax.rsqrt(jnp.mean(xv * xv, axis=-1, keepdims=True) + RMS_EPS)
        h_ref[...] = ((xv * r) * g_ref[...]).astype(BF16)

    return pl.pallas_call(
        body, out_shape=jax.ShapeDtypeStruct((rows, d), BF16), grid=(rows // tm,),
        in_specs=[pl.BlockSpec((tm, d), lambda i: (i, 0)), pl.BlockSpec((1, d), lambda i: (0, 0))],
        out_specs=pl.BlockSpec((tm, d), lambda i: (i, 0)), name=name, compiler_params=_params("parallel"))(x, g)


def _rms_bwd(x, g, dh, dres, name):
    rows, d = x.shape
    tm = 256

    def body(x_ref, g_ref, dh_ref, dres_ref, dx_ref, dg_ref):
        xv = x_ref[...]
        r = lax.rsqrt(jnp.mean(xv * xv, axis=-1, keepdims=True) + RMS_EPS)
        xh = xv * r
        dhv = dh_ref[...]
        dxh = dhv * g_ref[...]
        dx_ref[...] = dres_ref[...] + r * (dxh - xh * jnp.mean(dxh * xh, axis=-1, keepdims=True))
        part = jnp.sum(dhv * xh, axis=0, keepdims=True)

        @pl.when(pl.program_id(0) == 0)
        def _():
            dg_ref[...] = part

        @pl.when(pl.program_id(0) > 0)
        def _():
            dg_ref[...] += part

    row = pl.BlockSpec((tm, d), lambda i: (i, 0))
    vec = pl.BlockSpec((1, d), lambda i: (0, 0))
    return pl.pallas_call(
        body, out_shape=[jax.ShapeDtypeStruct((rows, d), F32), jax.ShapeDtypeStruct((1, d), F32)],
        grid=(rows // tm,), in_specs=[row, vec, row, row], out_specs=[row, vec],
        name=name, compiler_params=_params("arbitrary"))(x, g, dh, dres)


def _loss_grad(y, t):
    rows, d = y.shape
    tm = 256

    def body(y_ref, t_ref, dy_ref, l_ref):
        e = y_ref[...] - t_ref[...]
        dy_ref[...] = e * (1.0 / d)
        part = jnp.zeros((1, LANES), F32) + jnp.sum(e * e) * (0.5 / d)

        @pl.when(pl.program_id(0) == 0)
        def _():
            l_ref[...] = part

        @pl.when(pl.program_id(0) > 0)
        def _():
            l_ref[...] += part

    row = pl.BlockSpec((tm, d), lambda i: (i, 0))
    return pl.pallas_call(
        body, out_shape=[jax.ShapeDtypeStruct((rows, d), F32), jax.ShapeDtypeStruct((1, LANES), F32)],
        grid=(rows // tm,), in_specs=[row, row], out_specs=[row, pl.BlockSpec((1, LANES), lambda i: (0, 0))],
        name="loss_grad", compiler_params=_params("arbitrary"))(y, t)


def _put_pairs(ref, val):
    for hp in range(N_PAIRS):
        ref[hp] = val[:, hp * LANES:(hp + 1) * LANES].astype(ref.dtype)


def _get_pairs(ref):
    return jnp.concatenate([ref[hp] for hp in range(N_PAIRS)], axis=1)


def _swap_halves(v):
    return pltpu.roll(v, HEAD_DIM, axis=1)


def _expand_kv(kv):
    left = lax.broadcasted_iota(jnp.int32, kv.shape, 1) < HEAD_DIM
    sw = _swap_halves(kv)
    h0 = jnp.where(left, kv, sw)
    h1 = jnp.where(left, sw, kv)
    return jnp.concatenate([h0, h0, h1, h1], axis=1)


def _reduce_kv(dkv):
    left = lax.broadcasted_iota(jnp.int32, (dkv.shape[0], LANES), 1) < HEAD_DIM
    t = dkv[:, 0:LANES] + dkv[:, LANES:2 * LANES]
    u = dkv[:, 2 * LANES:3 * LANES] + dkv[:, 3 * LANES:4 * LANES]
    t = t + _swap_halves(t)
    u = u + _swap_halves(u)
    return jnp.where(left, t, u)


def _qknorm_fwd(proj, gqa, gka, gqb, gkb):
    rows = proj.shape[0]
    tm = 256

    def body(qa_ref, ka_ref, va_ref, qb_ref, kb_ref, vb_ref, gqa_ref, gka_ref, gqb_ref, gkb_ref,
             oqa, oka, ova, oqb, okb, ovb):
        for src, g_ref, dst in ((qa_ref, gqa_ref, oqa), (ka_ref, gka_ref, oka), (qb_ref, gqb_ref, oqb)):
            xv = src[...]
            _put_pairs(dst, (xv * _seg_rstd(xv)) * g_ref[...])
        _put_pairs(ova, va_ref[...])
        kv = kb_ref[...]
        _put_pairs(okb, _expand_kv((kv * _seg_rstd(kv)) * gkb_ref[...]))
        _put_pairs(ovb, _expand_kv(vb_ref[...]))

    def win(width, off):
        return pl.BlockSpec((tm, width), lambda i: (i, off // width))

    vec = lambda w: pl.BlockSpec((1, w), lambda i: (0, 0))
    out = pl.BlockSpec((N_PAIRS, tm, LANES), lambda i: (0, i, 0))
    return pl.pallas_call(
        body, out_shape=[jax.ShapeDtypeStruct((N_PAIRS, rows, LANES), F32)] * 6, grid=(rows // tm,),
        in_specs=[win(WIDTH, OFF_QA), win(WIDTH, OFF_KA), win(WIDTH, OFF_VA), win(WIDTH, OFF_QB),
                  win(LANES, OFF_KB), win(LANES, OFF_VB), vec(WIDTH), vec(WIDTH), vec(WIDTH), vec(LANES)],
        out_specs=[out] * 6, name="qknorm_fwd", compiler_params=_params("parallel"))(
            proj, proj, proj, proj, proj, proj, gqa, gka, gqb, gkb)


def _norm_bwd(xv, g, dy):
    r = _seg_rstd(xv)
    xh = xv * r
    dxh = dy * g
    dx = r * (dxh - xh * (_seg_sum(dxh * xh) * (1.0 / HEAD_DIM)))
    return dx, jnp.sum(dy * xh, axis=0, keepdims=True)


def _qknorm_bwd(proj, gqa, gka, gqb, gkb, dqa, dka, dva, dqb, dkb, dvb, dgab):
    rows = proj.shape[0]
    tm = 256
    n_a = len(dqa)

    def body(*refs):
        qa_ref, ka_ref, qb_ref, kb_ref, gqa_ref, gka_ref, gqb_ref, gkb_ref = refs[:8]
        pos = 8
        dqa_refs, dka_refs, dva_refs = refs[pos:pos + n_a], refs[pos + n_a:pos + 2 * n_a], refs[pos + 2 * n_a:pos + 3 * n_a]
        pos += 3 * n_a
        dqb_ref, dkb_ref, dvb_ref, dgab_ref = refs[pos:pos + 4]
        dproj_ref, ogqa, ogka, ogqb, ogkb = refs[pos + 4:]

        def total(rs):
            acc = _get_pairs(rs[0])
            for r in rs[1:]:
                acc = acc + _get_pairs(r)
            return acc

        dx_qa, p_qa = _norm_bwd(qa_ref[...], gqa_ref[...], total(dqa_refs))
        dx_ka, p_ka = _norm_bwd(ka_ref[...], gka_ref[...], total(dka_refs))
        dx_qb, p_qb = _norm_bwd(qb_ref[...], gqb_ref[...], _get_pairs(dqb_ref))
        dx_kb, p_kb = _norm_bwd(kb_ref[...], gkb_ref[...], _reduce_kv(_get_pairs(dkb_ref)))
        dproj_ref[:, OFF_QA:OFF_QA + WIDTH] = dx_qa.astype(BF16)
        dproj_ref[:, OFF_KA:OFF_KA + WIDTH] = dx_ka.astype(BF16)
        dproj_ref[:, OFF_VA:OFF_VA + WIDTH] = total(dva_refs).astype(BF16)
        dproj_ref[:, OFF_QB:OFF_QB + WIDTH] = dx_qb.astype(BF16)
        dproj_ref[:, OFF_KB:OFF_KB + LANES] = dx_kb.astype(BF16)
        dproj_ref[:, OFF_VB:OFF_VB + LANES] = _reduce_kv(_get_pairs(dvb_ref)).astype(BF16)
        dproj_ref[:, OFF_GA:D_IN] = dgab_ref[...]
        first = pl.program_id(0) == 0
        for o_ref, part in ((ogqa, p_qa), (ogka, p_ka), (ogqb, p_qb), (ogkb, p_kb)):
            @pl.when(first)
            def _(o_ref=o_ref, part=part):
                o_ref[...] = part

            @pl.when(jnp.logical_not(first))
            def _(o_ref=o_ref, part=part):
                o_ref[...] += part

    def win(width, off):
        return pl.BlockSpec((tm, width), lambda i: (i, off // width))

    vec = lambda w: pl.BlockSpec((1, w), lambda i: (0, 0))
    row = lambda w: pl.BlockSpec((tm, w), lambda i: (i, 0))
    in_specs = [win(WIDTH, OFF_QA), win(WIDTH, OFF_KA), win(WIDTH, OFF_QB), win(LANES, OFF_KB),
                vec(WIDTH), vec(WIDTH), vec(WIDTH), vec(LANES)]
    in_specs += [pl.BlockSpec((N_PAIRS, tm, LANES), lambda i: (0, i, 0))] * (3 * n_a + 3) + [row(2 * D_MODEL)]
    return pl.pallas_call(
        body,
        out_shape=[jax.ShapeDtypeStruct((rows, D_IN), BF16), jax.ShapeDtypeStruct((1, WIDTH), F32),
                   jax.ShapeDtypeStruct((1, WIDTH), F32), jax.ShapeDtypeStruct((1, WIDTH), F32),
                   jax.ShapeDtypeStruct((1, LANES), F32)],
        grid=(rows // tm,), in_specs=in_specs,
        out_specs=[row(D_IN), vec(WIDTH), vec(WIDTH), vec(WIDTH), vec(LANES)],
        name="qknorm_bwd", compiler_params=_params("arbitrary"))(
            proj, proj, proj, proj, gqa, gka, gqb, gkb, *dqa, *dka, *dva, dqb, dkb, dvb, dgab)


def _t5_bucket(rel):
    half_b = NUM_BUCKETS // 2
    max_exact = half_b // 2
    sign = jnp.where(rel > 0, half_b, 0)
    n = jnp.abs(rel)
    nf = jnp.maximum(n, 1).astype(F32)
    large = max_exact + (jnp.log(nf / max_exact) / math.log(MAX_DISTANCE / max_exact)
                         * (half_b - max_exact)).astype(jnp.int32)
    large = jnp.minimum(large, half_b - 1)
    return sign + jnp.where(n < max_exact, n, large)


def _band_buckets(blk, dilation):
    i = jnp.arange(blk, dtype=jnp.int32)[:, None]
    j = jnp.arange(3 * blk, dtype=jnp.int32)[None, :]
    rel = j - blk - i
    return jnp.where(jnp.abs(rel) <= blk, _t5_bucket(rel * dilation), -1)


def _bias_tiles(table, buckets, head_off, name):
    blk = buckets.shape[0]

    def body(tab_ref, bk_ref, o_ref):
        h = pl.program_id(0) + head_off
        bk = bk_ref[...]
        acc = jnp.full(bk.shape, NEG_INF, F32)
        for b in range(NUM_BUCKETS):
            acc = jnp.where(bk == b, tab_ref[b, h], acc)
        o_ref[0] = acc

    return pl.pallas_call(
        body, out_shape=jax.ShapeDtypeStruct((N_HEADS, blk, 3 * blk), F32), grid=(N_HEADS,),
        in_specs=[pl.BlockSpec(memory_space=pltpu.SMEM), pl.BlockSpec((blk, 3 * blk), lambda h: (0, 0))],
        out_specs=pl.BlockSpec((1, blk, 3 * blk), lambda h: (h, 0, 0)),
        name=name, compiler_params=_params("parallel"))(table, buckets)


def _table_grad(dbias, buckets, name):
    blk = buckets.shape[0]

    def body(db_ref, bk_ref, o_ref):
        bk = bk_ref[...]
        dbv = db_ref[0]
        lane = lax.broadcasted_iota(jnp.int32, (1, LANES), 1)
        acc = jnp.zeros((1, LANES), F32)
        for b in range(NUM_BUCKETS):
            acc = jnp.where(lane == b, jnp.sum(jnp.where(bk == b, dbv, 0.0)), acc)
        o_ref[0] = acc

    out = pl.pallas_call(
        body, out_shape=jax.ShapeDtypeStruct((N_HEADS, 1, LANES), F32), grid=(N_HEADS,),
        in_specs=[pl.BlockSpec((1, blk, 3 * blk), lambda h: (h, 0, 0)), pl.BlockSpec((blk, 3 * blk), lambda h: (0, 0))],
        out_specs=pl.BlockSpec((1, 1, LANES), lambda h: (h, 0, 0)),
        name=name, compiler_params=_params("parallel"))(dbias, buckets)
    return out[:, 0, :NUM_BUCKETS]


def _dot_nt(a, b):
    return lax.dot_general(a, b, (((1,), (1,)), ((), ())), preferred_element_type=F32)


def _stack_pair(x2, left):
    return jnp.concatenate([jnp.where(left, x2, 0.0), jnp.where(left, 0.0, x2)], axis=0).astype(BF16)


def _attn_geometry(blk, d):
    chunk = blk * ITEMS if d == 1 else blk * d
    groups = 1 if d == 1 else d // ITEMS
    halo = blk if d == 1 else chunk
    return chunk, groups, halo


def _item_rows(ref, j, r0, blk, d):
    if d == 1:
        return ref[j * blk:(j + 1) * blk, :]
    return ref[pl.ds(r0 + j, blk, stride=d), :]


def _item_penalty(t, nct, j, blk, d):
    first_ok, last_ok = t > 0, t < nct - 1
    if d == 1:
        first_ok = True if j > 0 else first_ok
        last_ok = True if j < ITEMS - 1 else last_ok
    col = lax.broadcasted_iota(jnp.int32, (1, 3 * blk), 1)
    ok = jnp.logical_and(jnp.logical_or(col >= blk, first_ok), jnp.logical_or(col < 2 * blk, last_ok))
    return jnp.where(ok, 0.0, NEG_INF).astype(F32)


def _attn_specs(seq, blk, d, step_of):
    chunk, _, halo = _attn_geometry(blk, d)
    per, last = chunk // halo, seq // halo - 1
    cur = pl.BlockSpec((None, chunk, LANES), lambda hp, t: (hp, step_of(t), 0))
    prev = pl.BlockSpec((None, halo, LANES), lambda hp, t: (hp, jnp.clip(step_of(t) * per - 1, 0, last), 0))
    nxt = pl.BlockSpec((None, halo, LANES), lambda hp, t: (hp, jnp.minimum((step_of(t) + 1) * per, last), 0))
    return cur, prev, nxt


def _attn_fwd(q, k, v, bias, sink, blk, d, name):
    _, seq, _ = q.shape
    chunk, groups, _ = _attn_geometry(blk, d)
    nct = seq // chunk
    has_sink = sink is not None
    scale = HEAD_DIM ** -0.5

    def body(*refs):
        q_ref, kp, kc, kn, vp, vc, vn, b_ref = refs[:8]
        s_ref = refs[8] if has_sink else None
        o_ref, l_ref = refs[-2], refs[-1]
        t = pl.program_id(1)
        left = lax.broadcasted_iota(jnp.int32, (1, LANES), 1) < HEAD_DIM
        bias2 = b_ref[...]
        if d == 1:
            kwin = jnp.concatenate([kp[...], kc[...], kn[...]], axis=0).astype(BF16)
            vwin = jnp.concatenate([vp[...], vc[...], vn[...]], axis=0).astype(BF16)

        def group(r0):
            scores, vcats = [], []
            for j in range(ITEMS):
                qs = _stack_pair(_item_rows(q_ref, j, r0, blk, d) * scale, left)
                if d == 1:
                    kcat, vcat = kwin[j * blk:(j + 3) * blk], vwin[j * blk:(j + 3) * blk]
                else:
                    kcat = jnp.concatenate([_item_rows(r, j, r0, blk, d) for r in (kp, kc, kn)], axis=0).astype(BF16)
                    vcat = jnp.concatenate([_item_rows(r, j, r0, blk, d) for r in (vp, vc, vn)], axis=0).astype(BF16)
                scores.append(_dot_nt(qs, kcat) + bias2 + _item_penalty(t, nct, j, blk, d))
                vcats.append(vcat)
            s = jnp.concatenate(scores, axis=0)
            m = jnp.max(s, axis=-1, keepdims=True)
            if has_sink:
                sk = jnp.concatenate([s_ref[...]] * ITEMS, axis=0)
                m = jnp.maximum(m, sk)
            p = jnp.exp(s - m)
            den = jnp.sum(p, axis=-1, keepdims=True)
            if has_sink:
                den = den + jnp.exp(sk - m)
            pn = (p * (1.0 / den)).astype(BF16)
            lse = m + jnp.log(den)
            for j in range(ITEMS):
                top, mid, bot = 2 * j * blk, (2 * j + 1) * blk, (2 * j + 2) * blk
                o2 = jnp.dot(pn[top:bot], vcats[j], preferred_element_type=F32)
                o_val = jnp.where(left, o2[:blk], o2[blk:])
                l_val = jnp.where(left, lse[top:mid], lse[mid:bot])
                if d == 1:
                    o_ref[j * blk:(j + 1) * blk, :] = o_val
                    l_ref[j * blk:(j + 1) * blk, :] = l_val
                else:
                    o_ref[pl.ds(r0 + j, blk, stride=d), :] = o_val
                    l_ref[pl.ds(r0 + j, blk, stride=d), :] = l_val

        if groups == 1:
            group(0)
        else:
            def step(g, carry):
                group(g * ITEMS)
                return carry

            lax.fori_loop(0, groups, step, 0)

    cur, prev, nxt = _attn_specs(seq, blk, d, lambda t: t)
    in_specs = [cur, prev, cur, nxt, prev, cur, nxt, pl.BlockSpec((2 * blk, 3 * blk), lambda hp, t: (hp, 0))]
    args = [q, k, k, k, v, v, v, bias]
    if has_sink:
        in_specs.append(pl.BlockSpec((2 * blk, 1), lambda hp, t: (hp, 0)))
        args.append(sink)
    return pl.pallas_call(
        body, out_shape=[jax.ShapeDtypeStruct(q.shape, F32)] * 2, grid=(N_PAIRS, nct),
        in_specs=in_specs, out_specs=[cur, cur], name=name, compiler_params=_params("parallel", "parallel"))(*args)


def _attn_bwd(q, k, v, do, lse, delta, bias, sink, blk, d, name):
    _, seq, _ = q.shape
    chunk, groups, halo = _attn_geometry(blk, d)
    nct = seq // chunk
    has_sink = sink is not None
    n_in = 12 if has_sink else 11
    scale = HEAD_DIM ** -0.5

    def body(*refs):
        q_ref, kp, kc, kn, vp, vc, vn, do_ref, l_ref, d_ref, b_ref = refs[:11]
        s_ref = refs[11] if has_sink else None
        dq_ref, dk_ref, dv_ref, db_ref = refs[n_in:n_in + 4]
        ds_ref = refs[n_in + 4] if has_sink else None
        wk, wv = refs[-2], refs[-1]
        t = pl.program_id(1)

        @pl.when(t == 0)
        def _():
            wk[...] = jnp.zeros_like(wk)
            wv[...] = jnp.zeros_like(wv)
            db_ref[...] = jnp.zeros_like(db_ref)
            if has_sink:
                ds_ref[...] = jnp.zeros_like(ds_ref)

        @pl.when(t > 0)
        def _():
            for w in (wk, wv):
                keep = w[chunk:2 * chunk + halo]
                w[0:chunk + halo] = keep
                w[chunk + halo:2 * chunk + halo] = jnp.zeros((chunk, LANES), F32)

        @pl.when(t < nct)
        def _():
            lane = lax.broadcasted_iota(jnp.int32, (1, LANES), 1)
            left = lane < HEAD_DIM
            bias2 = b_ref[...]
            if d == 1:
                kwin = jnp.concatenate([kp[...], kc[...], kn[...]], axis=0).astype(BF16)
                vwin = jnp.concatenate([vp[...], vc[...], vn[...]], axis=0).astype(BF16)

            def group(r0):
                qss, doss, kcats, scores, dps, lcols, dcols = [], [], [], [], [], [], []
                for j in range(ITEMS):
                    qs = _stack_pair(_item_rows(q_ref, j, r0, blk, d) * scale, left)
                    dos = _stack_pair(_item_rows(do_ref, j, r0, blk, d), left)
                    if d == 1:
                        kcat, vcat = kwin[j * blk:(j + 3) * blk], vwin[j * blk:(j + 3) * blk]
                    else:
                        kcat = jnp.concatenate([_item_rows(r, j, r0, blk, d) for r in (kp, kc, kn)], axis=0).astype(BF16)
                        vcat = jnp.concatenate([_item_rows(r, j, r0, blk, d) for r in (vp, vc, vn)], axis=0).astype(BF16)
                    l2, d2 = _item_rows(l_ref, j, r0, blk, d), _item_rows(d_ref, j, r0, blk, d)
                    lcols.append(jnp.max(jnp.where(left, l2, NEG_INF), axis=-1, keepdims=True))
                    lcols.append(jnp.max(jnp.where(left, NEG_INF, l2), axis=-1, keepdims=True))
                    dcols.append(jnp.sum(jnp.where(lane == 0, d2, 0.0), axis=-1, keepdims=True))
                    dcols.append(jnp.sum(jnp.where(lane == HEAD_DIM, d2, 0.0), axis=-1, keepdims=True))
                    scores.append(_dot_nt(qs, kcat) + bias2 + _item_penalty(t, nct, j, blk, d))
                    dps.append(_dot_nt(dos, vcat))
                    qss.append(qs)
                    doss.append(dos)
                    kcats.append(kcat)
                lcol = jnp.concatenate(lcols, axis=0)
                dcol = jnp.concatenate(dcols, axis=0)
                p = jnp.exp(jnp.concatenate(scores, axis=0) - lcol)
                ds = p * (jnp.concatenate(dps, axis=0) - dcol)
                if has_sink:
                    sgrad = dcol * jnp.exp(jnp.concatenate([s_ref[...]] * ITEMS, axis=0) - lcol)
                for j in range(ITEMS):
                    top, bot = 2 * j * blk, (2 * j + 2) * blk
                    dsj, pj = ds[top:bot], p[top:bot]
                    db_ref[...] += dsj
                    if has_sink:
                        ds_ref[...] -= sgrad[top:bot]
                    dq2 = jnp.dot(dsj.astype(BF16), kcats[j], preferred_element_type=F32) * scale
                    dq_val = jnp.where(left, dq2[:blk], dq2[blk:])
                    if d == 1:
                        dq_ref[j * blk:(j + 1) * blk, :] = dq_val
                    else:
                        dq_ref[pl.ds(r0 + j, blk, stride=d), :] = dq_val
                    dk_new = jnp.dot(jnp.transpose(dsj).astype(BF16), qss[j], preferred_element_type=F32)
                    dv_new = jnp.dot(jnp.transpose(pj).astype(BF16), doss[j], preferred_element_type=F32)
                    for w, new in ((wk, dk_new), (wv, dv_new)):
                        if d == 1:
                            w[chunk + (j - 1) * blk:chunk + (j + 2) * blk, :] += new
                        else:
                            for c in range(3):
                                w[pl.ds(c * chunk + r0 + j, blk, stride=d), :] += new[c * blk:(c + 1) * blk]

            if groups == 1:
                group(0)
            else:
                def step(g, carry):
                    group(g * ITEMS)
                    return carry

                lax.fori_loop(0, groups, step, 0)

        dk_ref[...] = wk[0:chunk]
        dv_ref[...] = wv[0:chunk]

    cur, prev, nxt = _attn_specs(seq, blk, d, lambda t: jnp.minimum(t, nct - 1))
    lag = pl.BlockSpec((None, chunk, LANES), lambda hp, t: (hp, jnp.maximum(t - 1, 0), 0))
    band = pl.BlockSpec((2 * blk, 3 * blk), lambda hp, t: (hp, 0))
    col = pl.BlockSpec((2 * blk, 1), lambda hp, t: (hp, 0))
    in_specs = [cur, prev, cur, nxt, prev, cur, nxt, cur, cur, cur, band]
    args = [q, k, k, k, v, v, v, do, lse, delta, bias]
    out_shape = [jax.ShapeDtypeStruct(q.shape, F32)] * 3 + [jax.ShapeDtypeStruct((N_HEADS * blk, 3 * blk), F32)]
    out_specs = [cur, lag, lag, band]
    if has_sink:
        in_specs.append(col)
        args.append(sink)
        out_shape.append(jax.ShapeDtypeStruct((N_HEADS * blk, 1), F32))
        out_specs.append(col)
    window = pltpu.VMEM((2 * chunk + halo, LANES), F32)
    return pl.pallas_call(
        body, out_shape=out_shape, grid=(N_PAIRS, nct + 1), in_specs=in_specs, out_specs=out_specs,
        scratch_shapes=[window, window], name=name, compiler_params=_params("arbitrary", "arbitrary"))(*args)


def _combine_patterns(outs, lses):
    _, rows, _ = outs[0].shape
    tm = 256
    n = len(outs)

    def body(*refs):
        o_refs, l_refs = refs[:n], refs[n:2 * n]
        y_ref, lse_ref = refs[2 * n], refs[2 * n + 1]
        for hp in range(N_PAIRS):
            ls = [r[hp] for r in l_refs]
            m = functools.reduce(jnp.maximum, ls)
            es = [jnp.exp(l - m) for l in ls]
            den = functools.reduce(lambda a, b: a + b, es)
            num = functools.reduce(lambda a, b: a + b, [e * r[hp] for e, r in zip(es, o_refs)])
            y_ref[:, hp * LANES:(hp + 1) * LANES] = num / den
            lse_ref[hp] = m + jnp.log(den)

    pm = pl.BlockSpec((N_PAIRS, tm, LANES), lambda i: (0, i, 0))
    return pl.pallas_call(
        body, out_shape=[jax.ShapeDtypeStruct((rows, WIDTH), F32), jax.ShapeDtypeStruct((N_PAIRS, rows, LANES), F32)],
        grid=(rows // tm,), in_specs=[pm] * (2 * n), out_specs=[pl.BlockSpec((tm, WIDTH), lambda i: (i, 0)), pm],
        name="combine_a", compiler_params=_params("parallel"))(*outs, *lses)


def _pairs_to_tokens(a):
    _, rows, _ = a.shape
    tm = 256

    def body(a_ref, o_ref):
        o_ref[...] = _get_pairs(a_ref)

    return pl.pallas_call(
        body, out_shape=jax.ShapeDtypeStruct((rows, WIDTH), a.dtype), grid=(rows // tm,),
        in_specs=[pl.BlockSpec((N_PAIRS, tm, LANES), lambda i: (0, i, 0))],
        out_specs=pl.BlockSpec((tm, WIDTH), lambda i: (i, 0)), name="pairs_to_tokens",
        compiler_params=_params("parallel"))(a)


def _attn_bwd_prep(dy, y, name):
    rows = dy.shape[0]
    tm = 256

    def body(dy_ref, y_ref, do_ref, dl_ref):
        dyv = dy_ref[...]
        _put_pairs(do_ref, dyv)
        _put_pairs(dl_ref, _seg_sum(dyv * y_ref[...]))

    tok = pl.BlockSpec((tm, WIDTH), lambda i: (i, 0))
    pm = pl.BlockSpec((N_PAIRS, tm, LANES), lambda i: (0, i, 0))
    return pl.pallas_call(
        body, out_shape=[jax.ShapeDtypeStruct((N_PAIRS, rows, LANES), F32)] * 2, grid=(rows // tm,),
        in_specs=[tok, tok], out_specs=[pm, pm], name=name, compiler_params=_params("parallel"))(dy, y)


def _tile_gain(g, reps):
    return jnp.tile(g[None, :], (1, reps))


def _local_step(x, p, target, big, small):
    rel_table = small["rel_table"]
    buckets_a = [_band_buckets(blk, d) for blk, d in DILATED]
    buckets_b = _band_buckets(BLK_B, 1)
    bias_a = [_bias_tiles(rel_table, bk, 0, "bias_a").reshape(N_HEADS * bk.shape[0], -1) for bk in buckets_a]
    bias_b = _bias_tiles(rel_table, buckets_b, N_HEADS, "bias_b").reshape(N_HEADS * BLK_B, -1)

    saved = []
    for l in range(DEPTH):
        w = big[l]
        g_mix, g_ffn, g_ple = (small[n][l][None, :] for n in ("norm_mix_g", "norm_ffn_g", "norm_ple_g"))
        gqa, gka, gqb = (_tile_gain(small[n][l], N_HEADS) for n in ("qnorm_a_g", "knorm_a_g", "qnorm_b_g"))
        gkb = _tile_gain(small["knorm_b_g"][l], N_KV_B)
        sink = jnp.repeat(small["sink_b"][l], BLK_B)[:, None]

        h = _rms_fwd(x, g_mix, "rms_mix")
        proj = _mm(h, w["w_in"], "nn", F32, "mm_in")
        qa, ka, va, qb, kb, vb = _qknorm_fwd(proj, gqa, gka, gqb, gkb)
        outs, lses = [], []
        for (blk, d), bias in zip(DILATED, bias_a):
            o, ls = _attn_fwd(qa, ka, va, bias, None, blk, d, f"attn_a{d}_fwd")
            outs.append(o)
            lses.append(ls)
        ya, lse_a = _combine_patterns(outs, lses)
        yb, lse_b = _attn_fwd(qb, kb, vb, bias_b, sink, BLK_B, 1, "attn_b_fwd")
        yb = _pairs_to_tokens(yb)
        ca = _mm(ya, w["w_branch_a"], "nn", F32, "mm_branch_a")
        cb = _mm(yb, w["w_branch_b"], "nn", F32, "mm_branch_b")

        def gate(ca_, cb_, ga_, gb_):
            return (_sigmoid(ga_) * ca_ + _sigmoid(gb_) * cb_,)

        (merged,) = _ew(gate, [(ca, 0), (cb, 0), (proj, OFF_GA), (proj, OFF_GB)], [BF16],
                        width=D_MODEL, bw=256, name="gate")
        x1 = _mm(merged, w["w_out"], "nn", F32, "mm_out", res=x)

        h2 = _rms_fwd(x1, g_ffn, "rms_ffn")
        a = _mm(h2, w["w_ffn_gate"], "nn", F32, "mm_ffn_gate")
        u = _mm(h2, w["w_ffn_up"], "nn", F32, "mm_ffn_up")

        def swiglu(a_, u_):
            return ((a_ * _sigmoid(a_)) * u_,)

        (hid,) = _ew(swiglu, [(a, 0), (u, 0)], [BF16], width=D_FF, bw=D_FF, name="swiglu")
        x2 = _mm(hid, w["w_ffn_down"], "nn", F32, "mm_ffn_down", res=x1)

        h3 = _rms_fwd(x2, g_ple, "rms_ple")
        z = _mm(h3, w["w_ple_gate"], "nn", F32, "mm_ple_gate")
        e = _mm(p[l], w["w_ple_proj"], "nn", F32, "mm_ple_proj")

        def ple(x2_, z_, e_):
            return (x2_ + _sigmoid(z_) * e_,)

        (x3,) = _ew(ple, [(x2, 0), (z, 0), (e, 0)], [F32], width=D_MODEL, bw=D_MODEL, name="ple")
        saved.append(dict(x0=x, h=h, proj=proj, qa=qa, ka=ka, va=va, qb=qb, kb=kb, vb=vb, ya=ya, lse_a=lse_a,
                          yb=yb, lse_b=lse_b, ca=ca, cb=cb, merged=merged, x1=x1, h2=h2, a=a, u=u, hid=hid,
                          x2=x2, h3=h3, z=z, e=e))
        x = x3

    dx, loss_acc = _loss_grad(x, target)
    loss = loss_acc[0, 0]

    gbig = [{} for _ in range(DEPTH)]
    gsmall = {n: [None] * DEPTH for n in SMALL if n != "rel_table"}
    dtable_a = jnp.zeros((N_HEADS, NUM_BUCKETS), F32)
    dtable_b = jnp.zeros((N_HEADS, NUM_BUCKETS), F32)

    for l in reversed(range(DEPTH)):
        sv = saved[l]
        w = big[l]
        g_mix, g_ffn, g_ple = (small[n][l][None, :] for n in ("norm_mix_g", "norm_ffn_g", "norm_ple_g"))
        gqa, gka, gqb = (_tile_gain(small[n][l], N_HEADS) for n in ("qnorm_a_g", "knorm_a_g", "qnorm_b_g"))
        gkb = _tile_gain(small["knorm_b_g"][l], N_KV_B)
        sink = jnp.repeat(small["sink_b"][l], BLK_B)[:, None]

        def ple_bwd(dx_, z_, e_):
            s = _sigmoid(z_)
            return dx_ * s, dx_ * e_ * (s * (1.0 - s))

        de, dz = _ew(ple_bwd, [(dx, 0), (sv["z"], 0), (sv["e"], 0)], [BF16, BF16], width=D_MODEL, bw=D_MODEL,
                     name="ple_bwd")
        gbig[l]["w_ple_proj"] = _mm(p[l], de, "tn", F32, "mm_d_ple_proj")
        gbig[l]["w_ple_gate"] = _mm(sv["h3"], dz, "tn", F32, "mm_d_ple_gate")
        dh3 = _mm(dz, w["w_ple_gate"], "nt", F32, "mm_dh3")
        dx, gsmall["norm_ple_g"][l] = _rms_bwd(sv["x2"], g_ple, dh3, dx, "rms_ple_bwd")

        dhid = _mm(dx, w["w_ffn_down"], "nt", F32, "mm_dhid")
        gbig[l]["w_ffn_down"] = _mm(sv["hid"], dx, "tn", F32, "mm_d_ffn_down")

        def swiglu_bwd(a_, u_, dh_):
            s = _sigmoid(a_)
            return dh_ * u_ * (s * (1.0 + a_ * (1.0 - s))), dh_ * (a_ * s)

        da, du = _ew(swiglu_bwd, [(sv["a"], 0), (sv["u"], 0), (dhid, 0)], [BF16, BF16], width=D_FF, bw=D_FF,
                     name="swiglu_bwd")
        gbig[l]["w_ffn_gate"] = _mm(sv["h2"], da, "tn", F32, "mm_d_ffn_gate")
        gbig[l]["w_ffn_up"] = _mm(sv["h2"], du, "tn", F32, "mm_d_ffn_up")
        dh2 = _mm(da, w["w_ffn_gate"], "nt", F32, "mm_dh2_gate")
        dh2 = _mm(du, w["w_ffn_up"], "nt", F32, "mm_dh2_up", res=dh2)
        dx, gsmall["norm_ffn_g"][l] = _rms_bwd(sv["x1"], g_ffn, dh2, dx, "rms_ffn_bwd")

        dmerged = _mm(dx, w["w_out"], "nt", F32, "mm_dmerged")
        gbig[l]["w_out"] = _mm(sv["merged"], dx, "tn", F32, "mm_d_out")

        def gate_bwd(dm_, ca_, cb_, ga_, gb_):
            sa, sb = _sigmoid(ga_), _sigmoid(gb_)
            return dm_ * sa, dm_ * sb, dm_ * ca_ * (sa * (1.0 - sa)), dm_ * cb_ * (sb * (1.0 - sb))

        dca, dcb, dga, dgb = _ew(gate_bwd, [(dmerged, 0), (sv["ca"], 0), (sv["cb"], 0), (sv["proj"], OFF_GA),
                                            (sv["proj"], OFF_GB)], [BF16, BF16, BF16, BF16],
                                 width=D_MODEL, bw=256, name="gate_bwd")
        dgab = jnp.concatenate([dga, dgb], axis=1)
        gbig[l]["w_branch_a"] = _mm(sv["ya"], dca, "tn", F32, "mm_d_branch_a")
        gbig[l]["w_branch_b"] = _mm(sv["yb"], dcb, "tn", F32, "mm_d_branch_b")
        dya = _mm(dca, w["w_branch_a"], "nt", F32, "mm_dya")
        dyb = _mm(dcb, w["w_branch_b"], "nt", F32, "mm_dyb")

        dya, delta_a = _attn_bwd_prep(dya, sv["ya"], "attn_a_bwd_prep")
        dyb, delta_b = _attn_bwd_prep(dyb, sv["yb"], "attn_b_bwd_prep")

        dqa, dka, dva = [], [], []
        for (blk, d), bias, bk in zip(DILATED, bias_a, buckets_a):
            dq_, dk_, dv_, db_ = _attn_bwd(sv["qa"], sv["ka"], sv["va"], dya, sv["lse_a"], delta_a, bias, None, blk, d,
                                           f"attn_a{d}_bwd")
            dqa.append(dq_)
            dka.append(dk_)
            dva.append(dv_)
            dtable_a = dtable_a + _table_grad(db_.reshape(N_HEADS, blk, 3 * blk), bk, "table_grad_a")
        dqb, dkb, dvb, db_, dsink = _attn_bwd(sv["qb"], sv["kb"], sv["vb"], dyb, sv["lse_b"], delta_b, bias_b, sink,
                                              BLK_B, 1, "attn_b_bwd")
        dtable_b = dtable_b + _table_grad(db_.reshape(N_HEADS, BLK_B, 3 * BLK_B), buckets_b, "table_grad_b")
        gsmall["sink_b"][l] = dsink.reshape(N_HEADS, BLK_B).sum(axis=1)

        dproj, pqa, pka, pqb, pkb = _qknorm_bwd(sv["proj"], gqa, gka, gqb, gkb, dqa, dka, dva, dqb, dkb, dvb, dgab)
        gsmall["qnorm_a_g"][l] = pqa.reshape(N_HEADS, HEAD_DIM).sum(0)
        gsmall["knorm_a_g"][l] = pka.reshape(N_HEADS, HEAD_DIM).sum(0)
        gsmall["qnorm_b_g"][l] = pqb.reshape(N_HEADS, HEAD_DIM).sum(0)
        gsmall["knorm_b_g"][l] = pkb.reshape(N_KV_B, HEAD_DIM).sum(0)
        gbig[l]["w_in"] = _mm(sv["h"], dproj, "tn", F32, "mm_d_in")
        dh = _mm(dproj, w["w_in"], "nt", F32, "mm_dh")
        dx, gsmall["norm_mix_g"][l] = _rms_bwd(sv["x0"], g_mix, dh, dx, "rms_mix_bwd")
        gsmall["norm_mix_g"][l] = gsmall["norm_mix_g"][l][0]
        gsmall["norm_ffn_g"][l] = gsmall["norm_ffn_g"][l][0]
        gsmall["norm_ple_g"][l] = gsmall["norm_ple_g"][l][0]

    gsmall = {n: jnp.stack(v) for n, v in gsmall.items()}
    gsmall["rel_table"] = jnp.concatenate([dtable_a, dtable_b], axis=0).T
    return loss, dx, gbig, gsmall


def _place():
    return lax.axis_index("x"), lax.axis_index("y"), lax.axis_index("c")


def _flip(v, bit):
    return 1 - v if bit else v


CHIP_RELATIONS = ((0, 1), (1, 0), (1, 1))
ANY = pl.BlockSpec(memory_space=pl.ANY)


def _allgather_body(w_refs, out_refs, send_sems, recv_sems):
    x, y, c = _place()
    chips = [(_flip(x, a), _flip(y, b)) for a, b in CHIP_RELATIONS]

    def make(g):
        w_ref, out_ref = w_refs[g], out_refs[g]
        half = w_ref.shape[0] // 2

        def part(px, py, pc):
            return out_ref.at[2 * px + py, pl.ds(pc * half, half), :]

        def copy(k, block, to, src=None):
            return pltpu.make_async_remote_copy(
                src_ref=part(*block) if src is None else src, dst_ref=part(*block),
                send_sem=send_sems.at[7 * g + k], recv_sem=recv_sems.at[7 * g + k], device_id=to,
                device_id_type=MESH_ID)

        own = pltpu.make_async_remote_copy(
            src_ref=w_ref, dst_ref=out_ref.at[2 * x + y], send_sem=send_sems.at[7 * g + 6],
            recv_sem=recv_sems.at[7 * g + 6], device_id=(x, y, 1 - c), device_id_type=MESH_ID)
        first = [copy(k, (x, y, c), (*chip, c), src=w_ref.at[pl.ds(c * half, half), :]) for k, chip in enumerate(chips)]
        passed = [copy(3 + k, (*chip, c), (x, y, 1 - c)) for k, chip in enumerate(chips)]
        arrive = [copy(k, (*chip, c), (x, y, c)) for k, chip in enumerate(chips)]
        arrive2 = [copy(3 + k, (*chip, 1 - c), (x, y, c)) for k, chip in enumerate(chips)]
        return own, first, passed, arrive, arrive2

    made = [make(g) for g in range(len(w_refs))]
    for own, first, _, _, _ in made:
        own.start()
        for cp in first:
            cp.start()
    for _, _, passed, arrive, _ in made:
        for k in range(3):
            arrive[k].wait_recv()
            passed[k].start()
    for own, first, passed, _, arrive2 in made:
        for k in range(3):
            arrive2[k].wait_recv()
        own.wait_recv()
        for cp in first + passed + [own]:
            cp.wait_send()


def _sibling(x, y, c):
    return [(x, y, 1 - c)]


def _same_core_of_other_chips(x, y, c):
    return [(_flip(x, a), _flip(y, b), c) for a, b in CHIP_RELATIONS]


def _exchange(body, ins, out_types, n_sems, name, sequencer=None):
    n = len(ins)
    sems = (pltpu.SemaphoreType.DMA((n_sems,)), pltpu.SemaphoreType.DMA((n_sems,)))
    if sequencer is None:
        in_place = out_types is None
        out_shape = [jax.ShapeDtypeStruct(a.shape, a.dtype) for a in ins] if in_place else out_types

        def tc_body(*refs):
            body(refs[:n], refs[n:n + len(out_shape)], refs[-2], refs[-1])

        return pl.pallas_call(
            tc_body, out_shape=out_shape, in_specs=[ANY] * n, out_specs=[ANY] * len(out_shape),
            input_output_aliases={g: g for g in range(n)} if in_place else {}, scratch_shapes=list(sems), name=name)(*ins)

    collective_id, peers = sequencer
    hbm = pltpu.MemorySpace.HBM
    in_refs = [jax.new_ref(a, memory_space=hbm) for a in ins]
    out_refs = in_refs if out_types is None else [jax.empty_ref(t, memory_space=hbm) for t in out_types]

    @pl.kernel(mesh=plsc.ScalarSubcoreMesh(axis_name="sequencer", num_cores=1), name=name, scratch_types=sems,
               compiler_params=pltpu.CompilerParams(collective_id=collective_id))
    def launch(send_sems, recv_sems):
        barrier = pltpu.get_barrier_semaphore()
        devices = peers(*_place())
        for device in devices:
            pl.semaphore_signal(barrier, inc=1, device_id=device, device_id_type=MESH_ID)
        pl.semaphore_wait(barrier, len(devices))
        body(in_refs, out_refs, send_sems, recv_sems)

    launch()
    return [r[...] for r in out_refs]


def _allgather(shards, name, sequencer=None):
    out_types = [jax.ShapeDtypeStruct((N_CHIPS,) + s.shape, s.dtype) for s in shards]
    if sequencer is not None:
        sequencer = (sequencer, lambda x, y, c: _sibling(x, y, c) + _same_core_of_other_chips(x, y, c))
    return _exchange(_allgather_body, shards, out_types, 7 * len(shards), name, sequencer)


def _half_tile(half):
    return max(t for t in range(16, 1025, 16) if half % t == 0)


def _run_copies(cps):
    for cp in cps:
        cp.start()
    for cp in cps:
        cp.wait_recv()
    for cp in cps:
        cp.wait_send()


def _sibling_halves(gsends, name, sequencer=None):
    def body(g_refs, out_refs, send_sems, recv_sems):
        x, y, c = _place()
        cps = []
        for g, (g_ref, out_ref) in enumerate(zip(g_refs, out_refs)):
            half = g_ref.shape[1] // 2
            cps.append(pltpu.make_async_remote_copy(
                src_ref=g_ref.at[:, pl.ds((1 - c) * half, half), :], dst_ref=out_ref,
                send_sem=send_sems.at[g], recv_sem=recv_sems.at[g], device_id=(x, y, 1 - c), device_id_type=MESH_ID))
        _run_copies(cps)

    out_types = [jax.ShapeDtypeStruct((s.shape[0], s.shape[1] // 2, s.shape[2]), s.dtype) for s in gsends]
    return _exchange(body, gsends, out_types, len(gsends), name, sequencer and (sequencer, _sibling))


def _chip_sums(gsend, sib, place):
    n, rows, cols = gsend.shape
    half = rows // 2
    tm = _half_tile(half)
    nblk = half // tm

    def body(s_ref, g_ref, sib_ref, o_ref):
        o_ref[0] = (g_ref[0].astype(F32) + sib_ref[0].astype(F32)).astype(o_ref.dtype)

    grid_spec = pltpu.PrefetchScalarGridSpec(
        num_scalar_prefetch=1, grid=(n, nblk),
        in_specs=[pl.BlockSpec((1, tm, cols), lambda k, i, s: (jnp.bitwise_xor(s[0], k), s[1] * nblk + i, 0)),
                  pl.BlockSpec((1, tm, cols), lambda k, i, s: (jnp.bitwise_xor(s[0], k), i, 0))],
        out_specs=pl.BlockSpec((1, tm, cols), lambda k, i, s: (k, i, 0)))
    return pl.pallas_call(
        body, out_shape=jax.ShapeDtypeStruct((n, half, cols), BF16), grid_spec=grid_spec,
        name="rs_chip_sums", compiler_params=_params("parallel", "parallel"))(place, gsend, sib)


def _exchange_chip_sums(tsends, name, sequencer=None):
    def body(t_refs, out_refs, send_sems, recv_sems):
        x, y, c = _place()
        cps = []
        for g, (t_ref, out_ref) in enumerate(zip(t_refs, out_refs)):
            for k, device in enumerate(_same_core_of_other_chips(x, y, c)):
                cps.append(pltpu.make_async_remote_copy(
                    src_ref=t_ref.at[k + 1], dst_ref=out_ref.at[k], send_sem=send_sems.at[3 * g + k],
                    recv_sem=recv_sems.at[3 * g + k], device_id=device, device_id_type=MESH_ID))
        _run_copies(cps)

    out_types = [jax.ShapeDtypeStruct((3,) + s.shape[1:], s.dtype) for s in tsends]
    return _exchange(body, tsends, out_types, 3 * len(tsends), name,
                     sequencer and (sequencer, _same_core_of_other_chips))


def _final_sum(tsend, recv, place):
    n, half, cols = tsend.shape
    tm = _half_tile(half)
    nblk = half // tm

    def body(s_ref, t_ref, r_ref, o_ref):
        o_ref[...] = ((t_ref[0].astype(F32) + r_ref[0].astype(F32)) + r_ref[1].astype(F32)) + r_ref[2].astype(F32)

    grid_spec = pltpu.PrefetchScalarGridSpec(
        num_scalar_prefetch=1, grid=(nblk,),
        in_specs=[pl.BlockSpec((1, tm, cols), lambda i, s: (0, i, 0)), pl.BlockSpec((n - 1, tm, cols), lambda i, s: (0, i, 0))],
        out_specs=pl.BlockSpec((tm, cols), lambda i, s: (s[1] * nblk + i, 0)))
    return pl.pallas_call(
        body, out_shape=jax.ShapeDtypeStruct((2 * half, cols), F32), grid_spec=grid_spec, name="rs_final_sum",
        compiler_params=_params("parallel"))(place, tsend, recv)


def _join_halves(gfulls, name, sequencer=None):
    def body(g_refs, out_refs, send_sems, recv_sems):
        x, y, c = _place()
        n = len(g_refs)

        def copy(g, pc):
            half = g_refs[g].shape[0] // 2
            return pltpu.make_async_remote_copy(
                src_ref=g_refs[g].at[pl.ds(pc * half, half), :], dst_ref=out_refs[g].at[pl.ds(pc * half, half), :],
                send_sem=send_sems.at[g], recv_sem=recv_sems.at[g], device_id=(x, y, 1 - c), device_id_type=MESH_ID)

        mine = [copy(g, c) for g in range(n)]
        for cp in mine:
            cp.start()
        for g in range(n):
            copy(g, 1 - c).wait_recv()
        for cp in mine:
            cp.wait_send()

    return _exchange(body, gfulls, None, len(gfulls), name, sequencer and (sequencer, _sibling))


def _allreduce_small(v):
    rows, cols = v.shape

    def body(v_ref, out_ref, buf, send_sems, recv_sems):
        x, y, c = _place()
        cps = []
        for k in range(1, 8):
            peer = (_flip(x, (k >> 2) & 1), _flip(y, (k >> 1) & 1), _flip(c, k & 1))
            cps.append(pltpu.make_async_remote_copy(
                src_ref=v_ref, dst_ref=buf.at[k - 1], send_sem=send_sems.at[k - 1], recv_sem=recv_sems.at[k - 1],
                device_id=peer, device_id_type=MESH_ID))
        for cp in cps:
            cp.start()
        for cp in cps:
            cp.wait_recv()
        for cp in cps:
            cp.wait_send()
        t0 = v_ref[...] + buf[0]
        t1 = buf[1] + buf[2]
        t2 = buf[3] + buf[4]
        t3 = buf[5] + buf[6]
        out_ref[...] = (t0 + t1) + (t2 + t3)

    vm = pl.BlockSpec(memory_space=pltpu.VMEM)
    return pl.pallas_call(
        body, out_shape=jax.ShapeDtypeStruct((rows, cols), F32), in_specs=[vm], out_specs=vm,
        scratch_shapes=[pltpu.VMEM((7, rows, cols), F32), pltpu.SemaphoreType.DMA((7,)), pltpu.SemaphoreType.DMA((7,))],
        name="allreduce_small")(v)


BIG_INFO = {n: (shape, ax) for n, shape, ax in BIG}
GROUPS = (("w_in",), ("w_ffn_gate", "w_ffn_up"), ("w_out", "w_ffn_down", "w_ple_gate"),
          ("w_branch_a", "w_branch_b", "w_ple_proj"))


def _shard_shape(name):
    (k, m), ax = BIG_INFO[name]
    return (k // N_CHIPS, m) if ax == 0 else (k, m // N_CHIPS)


def _group_rows(group):
    offs, off = {}, 0
    for n in group:
        offs[n] = off
        off += _shard_shape(n)[0]
    return offs, off


def _pack_groups(shards, layer, dtype):
    return [jnp.concatenate([shards[n][layer].astype(dtype) for n in group], axis=0) for group in GROUPS]


def _unpack_full(gathered):
    out = {}
    for group, arr in zip(GROUPS, gathered):
        offs, _ = _group_rows(group)
        for n in group:
            rows, cols = _shard_shape(n)
            (k, m), ax = BIG_INFO[n]
            slab = arr[:, offs[n]:offs[n] + rows]
            out[n] = slab.reshape(k, m) if ax == 0 else jnp.transpose(slab, (1, 0, 2)).reshape(k, m)
    return out


def _pack_grads(gfull):
    out = []
    for group in GROUPS:
        parts = []
        for n in group:
            rows, cols = _shard_shape(n)
            ax = BIG_INFO[n][1]
            slab = (gfull[n].reshape(N_CHIPS, rows, cols) if ax == 0
                    else jnp.transpose(gfull[n].reshape(rows, N_CHIPS, cols), (1, 0, 2)))
            parts.append(slab.astype(BF16))
        out.append(jnp.concatenate(parts, axis=1))
    return out


def _reduce_scatter(gsends, place, tag, sequencer_ids=None):
    ids = sequencer_ids or (None, None, None)
    sibs = _sibling_halves(gsends, "rs_sibling_halves_" + tag, ids[0])
    tsends = [_chip_sums(g, s, place) for g, s in zip(gsends, sibs)]
    recvs = _exchange_chip_sums(tsends, "rs_exchange_" + tag, ids[1])
    return _join_halves([_final_sum(t, r, place) for t, r in zip(tsends, recvs)], "rs_join_halves_" + tag, ids[2])


SMALL_SHAPES = {"rel_table": (NUM_BUCKETS, 2 * N_HEADS), "norm_mix_g": (DEPTH, D_MODEL), "qnorm_a_g": (DEPTH, HEAD_DIM),
                "knorm_a_g": (DEPTH, HEAD_DIM), "qnorm_b_g": (DEPTH, HEAD_DIM), "knorm_b_g": (DEPTH, HEAD_DIM),
                "sink_b": (DEPTH, N_HEADS), "norm_ffn_g": (DEPTH, D_MODEL), "norm_ple_g": (DEPTH, D_MODEL)}


def _pack_small(vals):
    flat = jnp.concatenate([vals[n].astype(F32).reshape(-1) for n in SMALL])
    flat = jnp.concatenate([flat, jnp.zeros((SMALL_ROWS * LANES - flat.shape[0],), F32)])
    return flat.reshape(SMALL_ROWS, LANES)


def _unpack_small(packed):
    flat, out, off = packed.reshape(-1), {}, 0
    for n in SMALL:
        size = math.prod(SMALL_SHAPES[n])
        out[n] = flat[off:off + size].reshape(SMALL_SHAPES[n])
        off += size
    return out


def _adamw(w, gs, g_row, m, v, name):
    c1 = 1.0 - ADAM_B1 ** ADAM_STEP
    c2 = 1.0 - ADAM_B2 ** ADAM_STEP
    total, width = w.shape
    n_layers = len(gs)
    per = total // n_layers
    tm = max(t for t in (512, 256, 128, 64, 32, 16, 8) if per % t == 0 and g_row % t == 0)
    nblk = per // tm

    def body(*refs):
        w_ref, g_refs = refs[0], refs[1:1 + n_layers]
        m_ref, v_ref, og, od, om, ov = refs[1 + n_layers:]
        layer = pl.program_id(0) // nblk
        g = g_refs[0][...]
        for l in range(1, n_layers):
            g = jnp.where(layer == l, g_refs[l][...], g)
        m_new = ADAM_B1 * m_ref[...] + (1.0 - ADAM_B1) * g
        v_new = ADAM_B2 * v_ref[...] + (1.0 - ADAM_B2) * (g * g)
        og[...] = g
        od[...] = -ADAM_LR * ((m_new / c1) / (jnp.sqrt(v_new / c2) + ADAM_EPS) + ADAM_WD * w_ref[...])
        om[...] = m_new
        ov[...] = v_new

    row = pl.BlockSpec((tm, width), lambda i: (i, 0))
    g_specs = [pl.BlockSpec((tm, width), lambda i, l=l: (g_row // tm + jnp.clip(i - l * nblk, 0, nblk - 1), 0))
               for l in range(n_layers)]
    return pl.pallas_call(
        body, out_shape=[jax.ShapeDtypeStruct((total, width), F32)] * 4, grid=(total // tm,),
        in_specs=[row] + g_specs + [row, row], out_specs=[row] * 4, name=name,
        compiler_params=_params("parallel"))(w, *gs, m, v)


def kernel(x, p, rel_table, norm_mix_g, w_in, qnorm_a_g, knorm_a_g, qnorm_b_g, knorm_b_g, sink_b, w_branch_a, w_branch_b, w_out, norm_ffn_g, w_ffn_gate, w_ffn_up, w_ffn_down, norm_ple_g, w_ple_gate, w_ple_proj, loss_target, m_rel_table, m_norm_mix_g, m_w_in, m_qnorm_a_g, m_knorm_a_g, m_qnorm_b_g, m_knorm_b_g, m_sink_b, m_w_branch_a, m_w_branch_b, m_w_out, m_norm_ffn_g, m_w_ffn_gate, m_w_ffn_up, m_w_ffn_down, m_norm_ple_g, m_w_ple_gate, m_w_ple_proj, v_rel_table, v_norm_mix_g, v_w_in, v_qnorm_a_g, v_knorm_a_g, v_qnorm_b_g, v_knorm_b_g, v_sink_b, v_w_branch_a, v_w_branch_b, v_w_out, v_norm_ffn_g, v_w_ffn_gate, v_w_ffn_up, v_w_ffn_down, v_norm_ple_g, v_w_ple_gate, v_w_ple_proj):
    given = dict(locals())
    weights = {n: given[n] for n in WEIGHTS}
    moments_m = {n: given["m_" + n] for n in WEIGHTS}
    moments_v = {n: given["v_" + n] for n in WEIGHTS}
    xi, yi, ci = _place()
    place = jnp.stack([2 * xi + yi, ci]).astype(jnp.int32)

    gathered0 = _allgather(_pack_groups(weights, 0, BF16), "allgather_weights_layer0")
    gathered0, shards1 = lax.optimization_barrier((gathered0, _pack_groups(weights, 1, BF16)))
    gathered1 = _allgather(shards1, "allgather_weights_layer1", sequencer=1)
    big = [_unpack_full(gathered0), _unpack_full(gathered1)]
    small = {n: weights[n] for n in SMALL}

    loss, dx, gbig, gsmall = _local_step(x[0], p[:, 0], loss_target[0], big, small)

    greds = [_reduce_scatter(_pack_grads(gbig[0]), place, "layer0"),
             _reduce_scatter(_pack_grads(gbig[1]), place, "layer1", sequencer_ids=(2, 3, 4))]

    grads, delta, new_m, new_v = {}, {}, {}, {}
    for gi, group in enumerate(GROUPS):
        offs, _ = _group_rows(group)
        for n in group:
            shape = weights[n].shape
            two_d = lambda a: a.reshape(shape[0] * shape[1], shape[2])
            outs = _adamw(two_d(weights[n]), [greds[l][gi] for l in range(DEPTH)], offs[n], two_d(moments_m[n]),
                          two_d(moments_v[n]), "adamw_" + n)
            grads[n], delta[n], new_m[n], new_v[n] = (o.reshape(shape) for o in outs)
    small_grads = _allreduce_small(_pack_small(gsmall))
    g_, d_, m_, v_ = _adamw(_pack_small(weights), [small_grads], 0, _pack_small(moments_m), _pack_small(moments_v),
                            "adamw_small")
    grads.update(_unpack_small(g_))
    delta.update(_unpack_small(d_))
    new_m.update(_unpack_small(m_))
    new_v.update(_unpack_small(v_))

    loss = lax.psum(loss, ("x", "y", "c"))
    return (loss, dx[None], *[grads[n] for n in WEIGHTS], *[delta[n] for n in WEIGHTS],
            *[new_m[n] for n in WEIGHTS], *[new_v[n] for n in WEIGHTS])
```

```python
import functools
import math

import jax
import jax.numpy as jnp
from jax import lax
from jax.experimental import pallas as pl
from jax.experimental.pallas import tpu as pltpu
from jax.experimental.pallas import tpu_sc as plsc

F32 = jnp.float32
BF16 = jnp.bfloat16
MESH_ID = pl.DeviceIdType.MESH

SEQ = 2048
D_MODEL = 1024
DEPTH = 2
HEAD_DIM = 64
N_HEADS = 8
WIDTH = N_HEADS * HEAD_DIM
N_PAIRS = 4
ITEMS = 4
N_KV_B = 2
PLE_DIM = 256
D_FF = 2816
D_IN = 4352
OFF_QA, OFF_KA, OFF_VA, OFF_QB, OFF_KB, OFF_VB, OFF_GA, OFF_GB = 0, 512, 1024, 1536, 2048, 2176, 2304, 3328
DILATED = ((64, 1), (64, 4), (64, 16))
BLK_B = 128
NUM_BUCKETS = 32
MAX_DISTANCE = 1024
RMS_EPS = 1e-6
NEG_INF = -1e30
LANES = 128
VMEM_LIMIT = 48 * 1024 * 1024

ADAM_LR, ADAM_B1, ADAM_B2, ADAM_EPS, ADAM_WD, ADAM_STEP = 0.001, 0.9, 0.999, 1e-08, 0.01, 10

BIG = (
    ("w_in", (D_MODEL, D_IN), 1),
    ("w_branch_a", (WIDTH, D_MODEL), 1),
    ("w_branch_b", (WIDTH, D_MODEL), 1),
    ("w_out", (D_MODEL, D_MODEL), 0),
    ("w_ffn_gate", (D_MODEL, D_FF), 1),
    ("w_ffn_up", (D_MODEL, D_FF), 1),
    ("w_ffn_down", (D_FF, D_MODEL), 0),
    ("w_ple_gate", (D_MODEL, D_MODEL), 0),
    ("w_ple_proj", (PLE_DIM, D_MODEL), 1),
)
SMALL = ("rel_table", "norm_mix_g", "qnorm_a_g", "knorm_a_g", "qnorm_b_g", "knorm_b_g", "sink_b",
         "norm_ffn_g", "norm_ple_g")
WEIGHTS = ("rel_table", "norm_mix_g", "w_in", "qnorm_a_g", "knorm_a_g", "qnorm_b_g", "knorm_b_g", "sink_b",
           "w_branch_a", "w_branch_b", "w_out", "norm_ffn_g", "w_ffn_gate", "w_ffn_up", "w_ffn_down",
           "norm_ple_g", "w_ple_gate", "w_ple_proj")
N_CHIPS = 4
SMALL_ROWS = 64


def _params(*sem):
    return pltpu.CompilerParams(dimension_semantics=sem, vmem_limit_bytes=VMEM_LIMIT)


def _pick(dim, target):
    for t in (target, 512, 256, 128, 64, 32, 16, 8):
        if t <= target and dim % t == 0:
            return t
    return dim


def _mm(a, b, mode, out_dtype, name, res=None, tm=512, tn=512):
    if mode == "nn":
        (m, k), (_, n) = a.shape, b.shape
    elif mode == "nt":
        (m, k), (n, _) = a.shape, b.shape
    else:
        (k, m), (_, n) = a.shape, b.shape
    tm, tn = _pick(m, tm), _pick(n, tn)
    a_spec = pl.BlockSpec((k, tm), lambda i, j: (0, i)) if mode == "tn" else pl.BlockSpec((tm, k), lambda i, j: (i, 0))
    b_spec = pl.BlockSpec((tn, k), lambda i, j: (j, 0)) if mode == "nt" else pl.BlockSpec((k, tn), lambda i, j: (0, j))
    dims = {"nn": (((1,), (0,)), ((), ())), "nt": (((1,), (1,)), ((), ())), "tn": (((0,), (0,)), ((), ()))}[mode]
    has_res = res is not None

    def body(*refs):
        a_ref, b_ref = refs[0], refs[1]
        o_ref = refs[-1]
        acc = lax.dot_general(a_ref[...].astype(BF16), b_ref[...].astype(BF16), dims, preferred_element_type=F32)
        if has_res:
            acc = acc + refs[2][...]
        o_ref[...] = acc.astype(out_dtype)

    in_specs = [a_spec, b_spec]
    args = [a, b]
    if has_res:
        in_specs.append(pl.BlockSpec((tm, tn), lambda i, j: (i, j)))
        args.append(res)
    return pl.pallas_call(
        body, out_shape=jax.ShapeDtypeStruct((m, n), out_dtype), grid=(m // tm, n // tn), in_specs=in_specs,
        out_specs=pl.BlockSpec((tm, tn), lambda i, j: (i, j)), name=name,
        compiler_params=_params("parallel", "parallel"))(*args)


def _ew(fn, ins, out_dtypes, *, width, bw, name, vecs=(), tm=256):
    rows = ins[0][0].shape[0]
    tm = _pick(rows, tm)
    n_in = len(ins) + len(vecs)

    def col_map(off_blocks):
        return lambda i, j: (i, off_blocks + j)

    in_specs = [pl.BlockSpec((tm, bw), col_map(off // bw)) for _, off in ins]
    in_specs += [pl.BlockSpec((1, bw), lambda i, j: (0, j)) for _ in vecs]

    def body(*refs):
        outs = fn(*[r[...] for r in refs[:n_in]])
        for r, o in zip(refs[n_in:], outs):
            r[...] = o.astype(r.dtype)

    return pl.pallas_call(
        body, out_shape=[jax.ShapeDtypeStruct((rows, width), dt) for dt in out_dtypes],
        grid=(rows // tm, width // bw), in_specs=in_specs,
        out_specs=[pl.BlockSpec((tm, bw), lambda i, j: (i, j)) for _ in out_dtypes],
        name=name, compiler_params=_params("parallel", "parallel"))(*[a for a, _ in ins], *vecs)


def _sigmoid(x):
    return 1.0 / (1.0 + jnp.exp(-x))


def _seg_sum(v):
    outs = []
    for k in range(v.shape[1] // LANES):
        vp = v[:, k * LANES:(k + 1) * LANES]
        left = lax.broadcasted_iota(jnp.int32, vp.shape, 1) < HEAD_DIM
        sl = jnp.sum(jnp.where(left, vp, 0.0), axis=-1, keepdims=True)
        sr = jnp.sum(jnp.where(left, 0.0, vp), axis=-1, keepdims=True)
        outs.append(jnp.where(left, sl, sr))
    return outs[0] if len(outs) == 1 else jnp.concatenate(outs, axis=1)


def _seg_rstd(x):
    return lax.rsqrt(_seg_sum(x * x) * (1.0 / HEAD_DIM) + RMS_EPS)


def _rms_fwd(x, g, name):
    rows, d = x.shape
    tm = 256

    def body(x_ref, g_ref, h_ref):
        xv = x_ref[...]
        r = lax.rsqrt(jnp.mean(xv * xv, axis=-1, keepdims=True) + RMS_EPS)
        h_ref[...] = ((xv * r) * g_ref[...]).astype(BF16)

    return pl.pallas_call(
        body, out_shape=jax.ShapeDtypeStruct((rows, d), BF16), grid=(rows // tm,),
        in_specs=[pl.BlockSpec((tm, d), lambda i: (i, 0)), pl.BlockSpec((1, d), lambda i: (0, 0))],
        out_specs=pl.BlockSpec((tm, d), lambda i: (i, 0)), name=name, compiler_params=_params("parallel"))(x, g)


def _rms_bwd(x, g, dh, dres, name):
    rows, d = x.shape
    tm = 256

    def body(x_ref, g_ref, dh_ref, dres_ref, dx_ref, dg_ref):
        xv = x_ref[...]
        r = lax.rsqrt(jnp.mean(xv * xv, axis=-1, keepdims=True) + RMS_EPS)
        xh = xv * r
        dhv = dh_ref[...]
        dxh = dhv * g_ref[...]
        dx_ref[...] = dres_ref[...] + r * (dxh - xh * jnp.mean(dxh * xh, axis=-1, keepdims=True))
        part = jnp.sum(dhv * xh, axis=0, keepdims=True)

        @pl.when(pl.program_id(0) == 0)
        def _():
            dg_ref[...] = part

        @pl.when(pl.program_id(0) > 0)
        def _():
            dg_ref[...] += part

    row = pl.BlockSpec((tm, d), lambda i: (i, 0))
    vec = pl.BlockSpec((1, d), lambda i: (0, 0))
    return pl.pallas_call(
        body, out_shape=[jax.ShapeDtypeStruct((rows, d), F32), jax.ShapeDtypeStruct((1, d), F32)],
        grid=(rows // tm,), in_specs=[row, vec, row, row], out_specs=[row, vec],
        name=name, compiler_params=_params("arbitrary"))(x, g, dh, dres)


def _loss_grad(y, t):
    rows, d = y.shape
    tm = 256

    def body(y_ref, t_ref, dy_ref, l_ref):
        e = y_ref[...] - t_ref[...]
        dy_ref[...] = e * (1.0 / d)
        part = jnp.zeros((1, LANES), F32) + jnp.sum(e * e) * (0.5 / d)

        @pl.when(pl.program_id(0) == 0)
        def _():
            l_ref[...] = part

        @pl.when(pl.program_id(0) > 0)
        def _():
            l_ref[...] += part

    row = pl.BlockSpec((tm, d), lambda i: (i, 0))
    return pl.pallas_call(
        body, out_shape=[jax.ShapeDtypeStruct((rows, d), F32), jax.ShapeDtypeStruct((1, LANES), F32)],
        grid=(rows // tm,), in_specs=[row, row], out_specs=[row, pl.BlockSpec((1, LANES), lambda i: (0, 0))],
        name="loss_grad", compiler_params=_params("arbitrary"))(y, t)


def _put_pairs(ref, val):
    for hp in range(N_PAIRS):
        ref[hp] = val[:, hp * LANES:(hp + 1) * LANES].astype(ref.dtype)


def _get_pairs(ref):
    return jnp.concatenate([ref[hp] for hp in range(N_PAIRS)], axis=1)


def _swap_halves(v):
    return pltpu.roll(v, HEAD_DIM, axis=1)


def _expand_kv(kv):
    left = lax.broadcasted_iota(jnp.int32, kv.shape, 1) < HEAD_DIM
    sw = _swap_halves(kv)
    h0 = jnp.where(left, kv, sw)
    h1 = jnp.where(left, sw, kv)
    return jnp.concatenate([h0, h0, h1, h1], axis=1)


def _reduce_kv(dkv):
    left = lax.broadcasted_iota(jnp.int32, (dkv.shape[0], LANES), 1) < HEAD_DIM
    t = dkv[:, 0:LANES] + dkv[:, LANES:2 * LANES]
    u = dkv[:, 2 * LANES:3 * LANES] + dkv[:, 3 * LANES:4 * LANES]
    t = t + _swap_halves(t)
    u = u + _swap_halves(u)
    return jnp.where(left, t, u)


def _qknorm_fwd(proj, gqa, gka, gqb, gkb):
    rows = proj.shape[0]
    tm = 256

    def body(qa_ref, ka_ref, va_ref, qb_ref, kb_ref, vb_ref, gqa_ref, gka_ref, gqb_ref, gkb_ref,
             oqa, oka, ova, oqb, okb, ovb):
        for src, g_ref, dst in ((qa_ref, gqa_ref, oqa), (ka_ref, gka_ref, oka), (qb_ref, gqb_ref, oqb)):
            xv = src[...]
            _put_pairs(dst, (xv * _seg_rstd(xv)) * g_ref[...])
        _put_pairs(ova, va_ref[...])
        kv = kb_ref[...]
        _put_pairs(okb, _expand_kv((kv * _seg_rstd(kv)) * gkb_ref[...]))
        _put_pairs(ovb, _expand_kv(vb_ref[...]))

    def win(width, off):
        return pl.BlockSpec((tm, width), lambda i: (i, off // width))

    vec = lambda w: pl.BlockSpec((1, w), lambda i: (0, 0))
    out = pl.BlockSpec((N_PAIRS, tm, LANES), lambda i: (0, i, 0))
    return pl.pallas_call(
        body, out_shape=[jax.ShapeDtypeStruct((N_PAIRS, rows, LANES), F32)] * 6, grid=(rows // tm,),
        in_specs=[win(WIDTH, OFF_QA), win(WIDTH, OFF_KA), win(WIDTH, OFF_VA), win(WIDTH, OFF_QB),
                  win(LANES, OFF_KB), win(LANES, OFF_VB), vec(WIDTH), vec(WIDTH), vec(WIDTH), vec(LANES)],
        out_specs=[out] * 6, name="qknorm_fwd", compiler_params=_params("parallel"))(
            proj, proj, proj, proj, proj, proj, gqa, gka, gqb, gkb)


def _norm_bwd(xv, g, dy):
    r = _seg_rstd(xv)
    xh = xv * r
    dxh = dy * g
    dx = r * (dxh - xh * (_seg_sum(dxh * xh) * (1.0 / HEAD_DIM)))
    return dx, jnp.sum(dy * xh, axis=0, keepdims=True)


def _qknorm_bwd(proj, gqa, gka, gqb, gkb, dqa, dka, dva, dqb, dkb, dvb, dgab):
    rows = proj.shape[0]
    tm = 256
    n_a = len(dqa)

    def body(*refs):
        qa_ref, ka_ref, qb_ref, kb_ref, gqa_ref, gka_ref, gqb_ref, gkb_ref = refs[:8]
        pos = 8
        dqa_refs, dka_refs, dva_refs = refs[pos:pos + n_a], refs[pos + n_a:pos + 2 * n_a], refs[pos + 2 * n_a:pos + 3 * n_a]
        pos += 3 * n_a
        dqb_ref, dkb_ref, dvb_ref, dgab_ref = refs[pos:pos + 4]
        dproj_ref, ogqa, ogka, ogqb, ogkb = refs[pos + 4:]

        def total(rs):
            acc = _get_pairs(rs[0])
            for r in rs[1:]:
                acc = acc + _get_pairs(r)
            return acc

        dx_qa, p_qa = _norm_bwd(qa_ref[...], gqa_ref[...], total(dqa_refs))
        dx_ka, p_ka = _norm_bwd(ka_ref[...], gka_ref[...], total(dka_refs))
        dx_qb, p_qb = _norm_bwd(qb_ref[...], gqb_ref[...], _get_pairs(dqb_ref))
        dx_kb, p_kb = _norm_bwd(kb_ref[...], gkb_ref[...], _reduce_kv(_get_pairs(dkb_ref)))
        dproj_ref[:, OFF_QA:OFF_QA + WIDTH] = dx_qa.astype(BF16)
        dproj_ref[:, OFF_KA:OFF_KA + WIDTH] = dx_ka.astype(BF16)
        dproj_ref[:, OFF_VA:OFF_VA + WIDTH] = total(dva_refs).astype(BF16)
        dproj_ref[:, OFF_QB:OFF_QB + WIDTH] = dx_qb.astype(BF16)
        dproj_ref[:, OFF_KB:OFF_KB + LANES] = dx_kb.astype(BF16)
        dproj_ref[:, OFF_VB:OFF_VB + LANES] = _reduce_kv(_get_pairs(dvb_ref)).astype(BF16)
        dproj_ref[:, OFF_GA:D_IN] = dgab_ref[...]
        first = pl.program_id(0) == 0
        for o_ref, part in ((ogqa, p_qa), (ogka, p_ka), (ogqb, p_qb), (ogkb, p_kb)):
            @pl.when(first)
            def _(o_ref=o_ref, part=part):
                o_ref[...] = part

            @pl.when(jnp.logical_not(first))
            def _(o_ref=o_ref, part=part):
                o_ref[...] += part

    def win(width, off):
        return pl.BlockSpec((tm, width), lambda i: (i, off // width))

    vec = lambda w: pl.BlockSpec((1, w), lambda i: (0, 0))
    row = lambda w: pl.BlockSpec((tm, w), lambda i: (i, 0))
    in_specs = [win(WIDTH, OFF_QA), win(WIDTH, OFF_KA), win(WIDTH, OFF_QB), win(LANES, OFF_KB),
                vec(WIDTH), vec(WIDTH), vec(WIDTH), vec(LANES)]
    in_specs += [pl.BlockSpec((N_PAIRS, tm, LANES), lambda i: (0, i, 0))] * (3 * n_a + 3) + [row(2 * D_MODEL)]
    return pl.pallas_call(
        body,
        out_shape=[jax.ShapeDtypeStruct((rows, D_IN), BF16), jax.ShapeDtypeStruct((1, WIDTH), F32),
                   jax.ShapeDtypeStruct((1, WIDTH), F32), jax.ShapeDtypeStruct((1, WIDTH), F32),
                   jax.ShapeDtypeStruct((1, LANES), F32)],
        grid=(rows // tm,), in_specs=in_specs,
        out_specs=[row(D_IN), vec(WIDTH), vec(WIDTH), vec(WIDTH), vec(LANES)],
        name="qknorm_bwd", compiler_params=_params("arbitrary"))(
            proj, proj, proj, proj, gqa, gka, gqb, gkb, *dqa, *dka, *dva, dqb, dkb, dvb, dgab)


def _t5_bucket(rel):
    half_b = NUM_BUCKETS // 2
    max_exact = half_b // 2
    sign = jnp.where(rel > 0, half_b, 0)
    n = jnp.abs(rel)
    nf = jnp.maximum(n, 1).astype(F32)
    large = max_exact + (jnp.log(nf / max_exact) / math.log(MAX_DISTANCE / max_exact)
                         * (half_b - max_exact)).astype(jnp.int32)
    large = jnp.minimum(large, half_b - 1)
    return sign + jnp.where(n < max_exact, n, large)


def _band_buckets(blk, dilation):
    i = jnp.arange(blk, dtype=jnp.int32)[:, None]
    j = jnp.arange(3 * blk, dtype=jnp.int32)[None, :]
    rel = j - blk - i
    return jnp.where(jnp.abs(rel) <= blk, _t5_bucket(rel * dilation), -1)


def _bias_tiles(table, buckets, head_off, name):
    blk = buckets.shape[0]

    def body(tab_ref, bk_ref, o_ref):
        h = pl.program_id(0) + head_off
        bk = bk_ref[...]
        acc = jnp.full(bk.shape, NEG_INF, F32)
        for b in range(NUM_BUCKETS):
            acc = jnp.where(bk == b, tab_ref[b, h], acc)
        o_ref[0] = acc

    return pl.pallas_call(
        body, out_shape=jax.ShapeDtypeStruct((N_HEADS, blk, 3 * blk), F32), grid=(N_HEADS,),
        in_specs=[pl.BlockSpec(memory_space=pltpu.SMEM), pl.BlockSpec((blk, 3 * blk), lambda h: (0, 0))],
        out_specs=pl.BlockSpec((1, blk, 3 * blk), lambda h: (h, 0, 0)),
        name=name, compiler_params=_params("parallel"))(table, buckets)


def _table_grad(dbias, buckets, name):
    blk = buckets.shape[0]

    def body(db_ref, bk_ref, o_ref):
        bk = bk_ref[...]
        dbv = db_ref[0]
        lane = lax.broadcasted_iota(jnp.int32, (1, LANES), 1)
        acc = jnp.zeros((1, LANES), F32)
        for b in range(NUM_BUCKETS):
            acc = jnp.where(lane == b, jnp.sum(jnp.where(bk == b, dbv, 0.0)), acc)
        o_ref[0] = acc

    out = pl.pallas_call(
        body, out_shape=jax.ShapeDtypeStruct((N_HEADS, 1, LANES), F32), grid=(N_HEADS,),
        in_specs=[pl.BlockSpec((1, blk, 3 * blk), lambda h: (h, 0, 0)), pl.BlockSpec((blk, 3 * blk), lambda h: (0, 0))],
        out_specs=pl.BlockSpec((1, 1, LANES), lambda h: (h, 0, 0)),
        name=name, compiler_params=_params("parallel"))(dbias, buckets)
    return out[:, 0, :NUM_BUCKETS]


def _dot_nt(a, b):
    return lax.dot_general(a, b, (((1,), (1,)), ((), ())), preferred_element_type=F32)


def _stack_pair(x2, left):
    return jnp.concatenate([jnp.where(left, x2, 0.0), jnp.where(left, 0.0, x2)], axis=0).astype(BF16)


def _attn_geometry(blk, d):
    chunk = blk * ITEMS if d == 1 else blk * d
    groups = 1 if d == 1 else d // ITEMS
    halo = blk if d == 1 else chunk
    return chunk, groups, halo


def _item_rows(ref, j, r0, blk, d):
    if d == 1:
        return ref[j * blk:(j + 1) * blk, :]
    return ref[pl.ds(r0 + j, blk, stride=d), :]


def _item_penalty(t, nct, j, blk, d):
    first_ok, last_ok = t > 0, t < nct - 1
    if d == 1:
        first_ok = True if j > 0 else first_ok
        last_ok = True if j < ITEMS - 1 else last_ok
    col = lax.broadcasted_iota(jnp.int32, (1, 3 * blk), 1)
    ok = jnp.logical_and(jnp.logical_or(col >= blk, first_ok), jnp.logical_or(col < 2 * blk, last_ok))
    return jnp.where(ok, 0.0, NEG_INF).astype(F32)


def _attn_specs(seq, blk, d, step_of):
    chunk, _, halo = _attn_geometry(blk, d)
    per, last = chunk // halo, seq // halo - 1
    cur = pl.BlockSpec((None, chunk, LANES), lambda hp, t: (hp, step_of(t), 0))
    prev = pl.BlockSpec((None, halo, LANES), lambda hp, t: (hp, jnp.clip(step_of(t) * per - 1, 0, last), 0))
    nxt = pl.BlockSpec((None, halo, LANES), lambda hp, t: (hp, jnp.minimum((step_of(t) + 1) * per, last), 0))
    return cur, prev, nxt


def _attn_fwd(q, k, v, bias, sink, blk, d, name):
    _, seq, _ = q.shape
    chunk, groups, _ = _attn_geometry(blk, d)
    nct = seq // chunk
    has_sink = sink is not None
    scale = HEAD_DIM ** -0.5

    def body(*refs):
        q_ref, kp, kc, kn, vp, vc, vn, b_ref = refs[:8]
        s_ref = refs[8] if has_sink else None
        o_ref, l_ref = refs[-2], refs[-1]
        t = pl.program_id(1)
        left = lax.broadcasted_iota(jnp.int32, (1, LANES), 1) < HEAD_DIM
        bias2 = b_ref[...]
        if d == 1:
            kwin = jnp.concatenate([kp[...], kc[...], kn[...]], axis=0).astype(BF16)
            vwin = jnp.concatenate([vp[...], vc[...], vn[...]], axis=0).astype(BF16)

        def group(r0):
            scores, vcats = [], []
            for j in range(ITEMS):
                qs = _stack_pair(_item_rows(q_ref, j, r0, blk, d) * scale, left)
                if d == 1:
                    kcat, vcat = kwin[j * blk:(j + 3) * blk], vwin[j * blk:(j + 3) * blk]
                else:
                    kcat = jnp.concatenate([_item_rows(r, j, r0, blk, d) for r in (kp, kc, kn)], axis=0).astype(BF16)
                    vcat = jnp.concatenate([_item_rows(r, j, r0, blk, d) for r in (vp, vc, vn)], axis=0).astype(BF16)
                scores.append(_dot_nt(qs, kcat) + bias2 + _item_penalty(t, nct, j, blk, d))
                vcats.append(vcat)
            s = jnp.concatenate(scores, axis=0)
            m = jnp.max(s, axis=-1, keepdims=True)
            if has_sink:
                sk = jnp.concatenate([s_ref[...]] * ITEMS, axis=0)
                m = jnp.maximum(m, sk)
            p = jnp.exp(s - m)
            den = jnp.sum(p, axis=-1, keepdims=True)
            if has_sink:
                den = den + jnp.exp(sk - m)
            pn = (p * (1.0 / den)).astype(BF16)
            lse = m + jnp.log(den)
            for j in range(ITEMS):
                top, mid, bot = 2 * j * blk, (2 * j + 1) * blk, (2 * j + 2) * blk
                o2 = jnp.dot(pn[top:bot], vcats[j], preferred_element_type=F32)
                o_val = jnp.where(left, o2[:blk], o2[blk:])
                l_val = jnp.where(left, lse[top:mid], lse[mid:bot])
                if d == 1:
                    o_ref[j * blk:(j + 1) * blk, :] = o_val
                    l_ref[j * blk:(j + 1) * blk, :] = l_val
                else:
                    o_ref[pl.ds(r0 + j, blk, stride=d), :] = o_val
                    l_ref[pl.ds(r0 + j, blk, stride=d), :] = l_val

        if groups == 1:
            group(0)
        else:
            def step(g, carry):
                group(g * ITEMS)
                return carry

            lax.fori_loop(0, groups, step, 0)

    cur, prev, nxt = _attn_specs(seq, blk, d, lambda t: t)
    in_specs = [cur, prev, cur, nxt, prev, cur, nxt, pl.BlockSpec((2 * blk, 3 * blk), lambda hp, t: (hp, 0))]
    args = [q, k, k, k, v, v, v, bias]
    if has_sink:
        in_specs.append(pl.BlockSpec((2 * blk, 1), lambda hp, t: (hp, 0)))
        args.append(sink)
    return pl.pallas_call(
        body, out_shape=[jax.ShapeDtypeStruct(q.shape, F32)] * 2, grid=(N_PAIRS, nct),
        in_specs=in_specs, out_specs=[cur, cur], name=name, compiler_params=_params("parallel", "parallel"))(*args)


def _attn_bwd(q, k, v, do, lse, delta, bias, sink, blk, d, name):
    _, seq, _ = q.shape
    chunk, groups, halo = _attn_geometry(blk, d)
    nct = seq // chunk
    has_sink = sink is not None
    n_in = 12 if has_sink else 11
    scale = HEAD_DIM ** -0.5

    def body(*refs):
        q_ref, kp, kc, kn, vp, vc, vn, do_ref, l_ref, d_ref, b_ref = refs[:11]
        s_ref = refs[11] if has_sink else None
        dq_ref, dk_ref, dv_ref, db_ref = refs[n_in:n_in + 4]
        ds_ref = refs[n_in + 4] if has_sink else None
        wk, wv = refs[-2], refs[-1]
        t = pl.program_id(1)

        @pl.when(t == 0)
        def _():
            wk[...] = jnp.zeros_like(wk)
            wv[...] = jnp.zeros_like(wv)
            db_ref[...] = jnp.zeros_like(db_ref)
            if has_sink:
                ds_ref[...] = jnp.zeros_like(ds_ref)

        @pl.when(t > 0)
        def _():
            for w in (wk, wv):
                keep = w[chunk:2 * chunk + halo]
                w[0:chunk + halo] = keep
                w[chunk + halo:2 * chunk + halo] = jnp.zeros((chunk, LANES), F32)

        @pl.when(t < nct)
        def _():
            lane = lax.broadcasted_iota(jnp.int32, (1, LANES), 1)
            left = lane < HEAD_DIM
            bias2 = b_ref[...]
            if d == 1:
                kwin = jnp.concatenate([kp[...], kc[...], kn[...]], axis=0).astype(BF16)
                vwin = jnp.concatenate([vp[...], vc[...], vn[...]], axis=0).astype(BF16)

            def group(r0):
                qss, doss, kcats, scores, dps, lcols, dcols = [], [], [], [], [], [], []
                for j in range(ITEMS):
                    qs = _stack_pair(_item_rows(q_ref, j, r0, blk, d) * scale, left)
                    dos = _stack_pair(_item_rows(do_ref, j, r0, blk, d), left)
                    if d == 1:
                        kcat, vcat = kwin[j * blk:(j + 3) * blk], vwin[j * blk:(j + 3) * blk]
                    else:
                        kcat = jnp.concatenate([_item_rows(r, j, r0, blk, d) for r in (kp, kc, kn)], axis=0).astype(BF16)
                        vcat = jnp.concatenate([_item_rows(r, j, r0, blk, d) for r in (vp, vc, vn)], axis=0).astype(BF16)
                    l2, d2 = _item_rows(l_ref, j, r0, blk, d), _item_rows(d_ref, j, r0, blk, d)
                    lcols.append(jnp.max(jnp.where(left, l2, NEG_INF), axis=-1, keepdims=True))
                    lcols.append(jnp.max(jnp.where(left, NEG_INF, l2), axis=-1, keepdims=True))
                    dcols.append(jnp.sum(jnp.where(lane == 0, d2, 0.0), axis=-1, keepdims=True))
                    dcols.append(jnp.sum(jnp.where(lane == HEAD_DIM, d2, 0.0), axis=-1, keepdims=True))
                    scores.append(_dot_nt(qs, kcat) + bias2 + _item_penalty(t, nct, j, blk, d))
                    dps.append(_dot_nt(dos, vcat))
                    qss.append(qs)
                    doss.append(dos)
                    kcats.append(kcat)
                lcol = jnp.concatenate(lcols, axis=0)
                dcol = jnp.concatenate(dcols, axis=0)
                p = jnp.exp(jnp.concatenate(scores, axis=0) - lcol)
                ds = p * (jnp.concatenate(dps, axis=0) - dcol)
                if has_sink:
                    sgrad = dcol * jnp.exp(jnp.concatenate([s_ref[...]] * ITEMS, axis=0) - lcol)
                for j in range(ITEMS):
                    top, bot = 2 * j * blk, (2 * j + 2) * blk
                    dsj, pj = ds[top:bot], p[top:bot]
                    db_ref[...] += dsj
                    if has_sink:
                        ds_ref[...] -= sgrad[top:bot]
                    dq2 = jnp.dot(dsj.astype(BF16), kcats[j], preferred_element_type=F32) * scale
                    dq_val = jnp.where(left, dq2[:blk], dq2[blk:])
                    if d == 1:
                        dq_ref[j * blk:(j + 1) * blk, :] = dq_val
                    else:
                        dq_ref[pl.ds(r0 + j, blk, stride=d), :] = dq_val
                    dk_new = jnp.dot(jnp.transpose(dsj).astype(BF16), qss[j], preferred_element_type=F32)
                    dv_new = jnp.dot(jnp.transpose(pj).astype(BF16), doss[j], preferred_element_type=F32)
                    for w, new in ((wk, dk_new), (wv, dv_new)):
                        if d == 1:
                            w[chunk + (j - 1) * blk:chunk + (j + 2) * blk, :] += new
                        else:
                            for c in range(3):
                                w[pl.ds(c * chunk + r0 + j, blk, stride=d), :] += new[c * blk:(c + 1) * blk]

            if groups == 1:
                group(0)
            else:
                def step(g, carry):
                    group(g * ITEMS)
                    return carry

                lax.fori_loop(0, groups, step, 0)

        dk_ref[...] = wk[0:chunk]
        dv_ref[...] = wv[0:chunk]

    cur, prev, nxt = _attn_specs(seq, blk, d, lambda t: jnp.minimum(t, nct - 1))
    lag = pl.BlockSpec((None, chunk, LANES), lambda hp, t: (hp, jnp.maximum(t - 1, 0), 0))
    band = pl.BlockSpec((2 * blk, 3 * blk), lambda hp, t: (hp, 0))
    col = pl.BlockSpec((2 * blk, 1), lambda hp, t: (hp, 0))
    in_specs = [cur, prev, cur, nxt, prev, cur, nxt, cur, cur, cur, band]
    args = [q, k, k, k, v, v, v, do, lse, delta, bias]
    out_shape = [jax.ShapeDtypeStruct(q.shape, F32)] * 3 + [jax.ShapeDtypeStruct((N_HEADS * blk, 3 * blk), F32)]
    out_specs = [cur, lag, lag, band]
    if has_sink:
        in_specs.append(col)
        args.append(sink)
        out_shape.append(jax.ShapeDtypeStruct((N_HEADS * blk, 1), F32))
        out_specs.append(col)
    window = pltpu.VMEM((2 * chunk + halo, LANES), F32)
    return pl.pallas_call(
        body, out_shape=out_shape, grid=(N_PAIRS, nct + 1), in_specs=in_specs, out_specs=out_specs,
        scratch_shapes=[window, window], name=name, compiler_params=_params("arbitrary", "arbitrary"))(*args)


def _combine_patterns(outs, lses):
    _, rows, _ = outs[0].shape
    tm = 256
    n = len(outs)

    def body(*refs):
        o_refs, l_refs = refs[:n], refs[n:2 * n]
        y_ref, lse_ref = refs[2 * n], refs[2 * n + 1]
        for hp in range(N_PAIRS):
            ls = [r[hp] for r in l_refs]
            m = functools.reduce(jnp.maximum, ls)
            es = [jnp.exp(l - m) for l in ls]
            den = functools.reduce(lambda a, b: a + b, es)
            num = functools.reduce(lambda a, b: a + b, [e * r[hp] for e, r in zip(es, o_refs)])
            y_ref[:, hp * LANES:(hp + 1) * LANES] = num / den
            lse_ref[hp] = m + jnp.log(den)

    pm = pl.BlockSpec((N_PAIRS, tm, LANES), lambda i: (0, i, 0))
    return pl.pallas_call(
        body, out_shape=[jax.ShapeDtypeStruct((rows, WIDTH), F32), jax.ShapeDtypeStruct((N_PAIRS, rows, LANES), F32)],
        grid=(rows // tm,), in_specs=[pm] * (2 * n), out_specs=[pl.BlockSpec((tm, WIDTH), lambda i: (i, 0)), pm],
        name="combine_a", compiler_params=_params("parallel"))(*outs, *lses)


def _pairs_to_tokens(a):
    _, rows, _ = a.shape
    tm = 256

    def body(a_ref, o_ref):
        o_ref[...] = _get_pairs(a_ref)

    return pl.pallas_call(
        body, out_shape=jax.ShapeDtypeStruct((rows, WIDTH), a.dtype), grid=(rows // tm,),
        in_specs=[pl.BlockSpec((N_PAIRS, tm, LANES), lambda i: (0, i, 0))],
        out_specs=pl.BlockSpec((tm, WIDTH), lambda i: (i, 0)), name="pairs_to_tokens",
        compiler_params=_params("parallel"))(a)


def _attn_bwd_prep(dy, y, name):
    rows = dy.shape[0]
    tm = 256

    def body(dy_ref, y_ref, do_ref, dl_ref):
        dyv = dy_ref[...]
        _put_pairs(do_ref, dyv)
        _put_pairs(dl_ref, _seg_sum(dyv * y_ref[...]))

    tok = pl.BlockSpec((tm, WIDTH), lambda i: (i, 0))
    pm = pl.BlockSpec((N_PAIRS, tm, LANES), lambda i: (0, i, 0))
    return pl.pallas_call(
        body, out_shape=[jax.ShapeDtypeStruct((N_PAIRS, rows, LANES), F32)] * 2, grid=(rows // tm,),
        in_specs=[tok, tok], out_specs=[pm, pm], name=name, compiler_params=_params("parallel"))(dy, y)


def _tile_gain(g, reps):
    return jnp.tile(g[None, :], (1, reps))


def _local_step(x, p, target, big, small):
    rel_table = small["rel_table"]
    buckets_a = [_band_buckets(blk, d) for blk, d in DILATED]
    buckets_b = _band_buckets(BLK_B, 1)
    bias_a = [_bias_tiles(rel_table, bk, 0, "bias_a").reshape(N_HEADS * bk.shape[0], -1) for bk in buckets_a]
    bias_b = _bias_tiles(rel_table, buckets_b, N_HEADS, "bias_b").reshape(N_HEADS * BLK_B, -1)

    saved = []
    for l in range(DEPTH):
        w = big[l]
        g_mix, g_ffn, g_ple = (small[n][l][None, :] for n in ("norm_mix_g", "norm_ffn_g", "norm_ple_g"))
        gqa, gka, gqb = (_tile_gain(small[n][l], N_HEADS) for n in ("qnorm_a_g", "knorm_a_g", "qnorm_b_g"))
        gkb = _tile_gain(small["knorm_b_g"][l], N_KV_B)
        sink = jnp.repeat(small["sink_b"][l], BLK_B)[:, None]

        h = _rms_fwd(x, g_mix, "rms_mix")
        proj = _mm(h, w["w_in"], "nn", F32, "mm_in")
        qa, ka, va, qb, kb, vb = _qknorm_fwd(proj, gqa, gka, gqb, gkb)
        outs, lses = [], []
        for (blk, d), bias in zip(DILATED, bias_a):
            o, ls = _attn_fwd(qa, ka, va, bias, None, blk, d, f"attn_a{d}_fwd")
            outs.append(o)
            lses.append(ls)
        ya, lse_a = _combine_patterns(outs, lses)
        yb, lse_b = _attn_fwd(qb, kb, vb, bias_b, sink, BLK_B, 1, "attn_b_fwd")
        yb = _pairs_to_tokens(yb)
        ca = _mm(ya, w["w_branch_a"], "nn", F32, "mm_branch_a")
        cb = _mm(yb, w["w_branch_b"], "nn", F32, "mm_branch_b")

        def gate(ca_, cb_, ga_, gb_):
            return (_sigmoid(ga_) * ca_ + _sigmoid(gb_) * cb_,)

        (merged,) = _ew(gate, [(ca, 0), (cb, 0), (proj, OFF_GA), (proj, OFF_GB)], [BF16],
                        width=D_MODEL, bw=256, name="gate")
        x1 = _mm(merged, w["w_out"], "nn", F32, "mm_out", res=x)

        h2 = _rms_fwd(x1, g_ffn, "rms_ffn")
        a = _mm(h2, w["w_ffn_gate"], "nn", F32, "mm_ffn_gate")
        u = _mm(h2, w["w_ffn_up"], "nn", F32, "mm_ffn_up")

        def swiglu(a_, u_):
            return ((a_ * _sigmoid(a_)) * u_,)

        (hid,) = _ew(swiglu, [(a, 0), (u, 0)], [BF16], width=D_FF, bw=D_FF, name="swiglu")
        x2 = _mm(hid, w["w_ffn_down"], "nn", F32, "mm_ffn_down", res=x1)

        h3 = _rms_fwd(x2, g_ple, "rms_ple")
        z = _mm(h3, w["w_ple_gate"], "nn", F32, "mm_ple_gate")
        e = _mm(p[l], w["w_ple_proj"], "nn", F32, "mm_ple_proj")

        def ple(x2_, z_, e_):
            return (x2_ + _sigmoid(z_) * e_,)

        (x3,) = _ew(ple, [(x2, 0), (z, 0), (e, 0)], [F32], width=D_MODEL, bw=D_MODEL, name="ple")
        saved.append(dict(x0=x, h=h, proj=proj, qa=qa, ka=ka, va=va, qb=qb, kb=kb, vb=vb, ya=ya, lse_a=lse_a,
                          yb=yb, lse_b=lse_b, ca=ca, cb=cb, merged=merged, x1=x1, h2=h2, a=a, u=u, hid=hid,
                          x2=x2, h3=h3, z=z, e=e))
        x = x3

    dx, loss_acc = _loss_grad(x, target)
    loss = loss_acc[0, 0]

    gbig = [{} for _ in range(DEPTH)]
    marks = [{} for _ in range(DEPTH)]
    gsmall = {n: [None] * DEPTH for n in SMALL if n != "rel_table"}
    dtable_a = jnp.zeros((N_HEADS, NUM_BUCKETS), F32)
    dtable_b = jnp.zeros((N_HEADS, NUM_BUCKETS), F32)

    for l in reversed(range(DEPTH)):
        sv = saved[l]
        w = big[l]
        g_mix, g_ffn, g_ple = (small[n][l][None, :] for n in ("norm_mix_g", "norm_ffn_g", "norm_ple_g"))
        gqa, gka, gqb = (_tile_gain(small[n][l], N_HEADS) for n in ("qnorm_a_g", "knorm_a_g", "qnorm_b_g"))
        gkb = _tile_gain(small["knorm_b_g"][l], N_KV_B)
        sink = jnp.repeat(small["sink_b"][l], BLK_B)[:, None]

        def ple_bwd(dx_, z_, e_):
            s = _sigmoid(z_)
            return dx_ * s, dx_ * e_ * (s * (1.0 - s))

        de, dz = _ew(ple_bwd, [(dx, 0), (sv["z"], 0), (sv["e"], 0)], [BF16, BF16], width=D_MODEL, bw=D_MODEL,
                     name="ple_bwd")
        gbig[l]["w_ple_proj"] = _mm(p[l], de, "tn", F32, "mm_d_ple_proj")
        gbig[l]["w_ple_gate"] = _mm(sv["h3"], dz, "tn", F32, "mm_d_ple_gate")
        dh3 = _mm(dz, w["w_ple_gate"], "nt", F32, "mm_dh3")
        dx, gsmall["norm_ple_g"][l] = _rms_bwd(sv["x2"], g_ple, dh3, dx, "rms_ple_bwd")

        dhid = _mm(dx, w["w_ffn_down"], "nt", F32, "mm_dhid")
        gbig[l]["w_ffn_down"] = _mm(sv["hid"], dx, "tn", F32, "mm_d_ffn_down")

        def swiglu_bwd(a_, u_, dh_):
            s = _sigmoid(a_)
            return dh_ * u_ * (s * (1.0 + a_ * (1.0 - s))), dh_ * (a_ * s)

        da, du = _ew(swiglu_bwd, [(sv["a"], 0), (sv["u"], 0), (dhid, 0)], [BF16, BF16], width=D_FF, bw=D_FF,
                     name="swiglu_bwd")
        gbig[l]["w_ffn_gate"] = _mm(sv["h2"], da, "tn", F32, "mm_d_ffn_gate")
        gbig[l]["w_ffn_up"] = _mm(sv["h2"], du, "tn", F32, "mm_d_ffn_up")
        dh2 = _mm(da, w["w_ffn_gate"], "nt", F32, "mm_dh2_gate")
        dh2 = _mm(du, w["w_ffn_up"], "nt", F32, "mm_dh2_up", res=dh2)
        dx, gsmall["norm_ffn_g"][l] = _rms_bwd(sv["x1"], g_ffn, dh2, dx, "rms_ffn_bwd")

        dmerged = _mm(dx, w["w_out"], "nt", F32, "mm_dmerged")
        marks[l]["ffn_bwd_done"] = dmerged
        gbig[l]["w_out"] = _mm(sv["merged"], dx, "tn", F32, "mm_d_out")

        def gate_bwd(dm_, ca_, cb_, ga_, gb_):
            sa, sb = _sigmoid(ga_), _sigmoid(gb_)
            return dm_ * sa, dm_ * sb, dm_ * ca_ * (sa * (1.0 - sa)), dm_ * cb_ * (sb * (1.0 - sb))

        dca, dcb, dga, dgb = _ew(gate_bwd, [(dmerged, 0), (sv["ca"], 0), (sv["cb"], 0), (sv["proj"], OFF_GA),
                                            (sv["proj"], OFF_GB)], [BF16, BF16, BF16, BF16],
                                 width=D_MODEL, bw=256, name="gate_bwd")
        dgab = jnp.concatenate([dga, dgb], axis=1)
        gbig[l]["w_branch_a"] = _mm(sv["ya"], dca, "tn", F32, "mm_d_branch_a")
        gbig[l]["w_branch_b"] = _mm(sv["yb"], dcb, "tn", F32, "mm_d_branch_b")
        dya = _mm(dca, w["w_branch_a"], "nt", F32, "mm_dya")
        dyb = _mm(dcb, w["w_branch_b"], "nt", F32, "mm_dyb")

        dya, delta_a = _attn_bwd_prep(dya, sv["ya"], "attn_a_bwd_prep")
        dyb, delta_b = _attn_bwd_prep(dyb, sv["yb"], "attn_b_bwd_prep")

        dqa, dka, dva = [], [], []
        for (blk, d), bias, bk in zip(DILATED, bias_a, buckets_a):
            dq_, dk_, dv_, db_ = _attn_bwd(sv["qa"], sv["ka"], sv["va"], dya, sv["lse_a"], delta_a, bias, None, blk, d,
                                           f"attn_a{d}_bwd")
            dqa.append(dq_)
            dka.append(dk_)
            dva.append(dv_)
            dtable_a = dtable_a + _table_grad(db_.reshape(N_HEADS, blk, 3 * blk), bk, "table_grad_a")
        dqb, dkb, dvb, db_, dsink = _attn_bwd(sv["qb"], sv["kb"], sv["vb"], dyb, sv["lse_b"], delta_b, bias_b, sink,
                                              BLK_B, 1, "attn_b_bwd")
        dtable_b = dtable_b + _table_grad(db_.reshape(N_HEADS, BLK_B, 3 * BLK_B), buckets_b, "table_grad_b")
        gsmall["sink_b"][l] = dsink.reshape(N_HEADS, BLK_B).sum(axis=1)

        dproj, pqa, pka, pqb, pkb = _qknorm_bwd(sv["proj"], gqa, gka, gqb, gkb, dqa, dka, dva, dqb, dkb, dvb, dgab)
        marks[l]["attn_bwd_done"] = dproj
        gsmall["qnorm_a_g"][l] = pqa.reshape(N_HEADS, HEAD_DIM).sum(0)
        gsmall["knorm_a_g"][l] = pka.reshape(N_HEADS, HEAD_DIM).sum(0)
        gsmall["qnorm_b_g"][l] = pqb.reshape(N_HEADS, HEAD_DIM).sum(0)
        gsmall["knorm_b_g"][l] = pkb.reshape(N_KV_B, HEAD_DIM).sum(0)
        gbig[l]["w_in"] = _mm(sv["h"], dproj, "tn", F32, "mm_d_in")
        dh = _mm(dproj, w["w_in"], "nt", F32, "mm_dh")
        dx, gsmall["norm_mix_g"][l] = _rms_bwd(sv["x0"], g_mix, dh, dx, "rms_mix_bwd")
        gsmall["norm_mix_g"][l] = gsmall["norm_mix_g"][l][0]
        gsmall["norm_ffn_g"][l] = gsmall["norm_ffn_g"][l][0]
        gsmall["norm_ple_g"][l] = gsmall["norm_ple_g"][l][0]

    gsmall = {n: jnp.stack(v) for n, v in gsmall.items()}
    gsmall["rel_table"] = jnp.concatenate([dtable_a, dtable_b], axis=0).T
    return loss, dx, gbig, gsmall, marks


def _place():
    return lax.axis_index("x"), lax.axis_index("y"), lax.axis_index("c")


def _flip(v, bit):
    return 1 - v if bit else v


CHIP_RELATIONS = ((0, 1), (1, 0), (1, 1))
ANY = pl.BlockSpec(memory_space=pl.ANY)


def _allgather_body(w_refs, out_refs, send_sems, recv_sems):
    x, y, c = _place()
    chips = [(_flip(x, a), _flip(y, b)) for a, b in CHIP_RELATIONS]

    def make(g):
        w_ref, out_ref = w_refs[g], out_refs[g]
        half = w_ref.shape[0] // 2

        def part(px, py, pc):
            return out_ref.at[2 * px + py, pl.ds(pc * half, half), :]

        def copy(k, block, to, src=None):
            return pltpu.make_async_remote_copy(
                src_ref=part(*block) if src is None else src, dst_ref=part(*block),
                send_sem=send_sems.at[7 * g + k], recv_sem=recv_sems.at[7 * g + k], device_id=to,
                device_id_type=MESH_ID)

        own = pltpu.make_async_remote_copy(
            src_ref=w_ref, dst_ref=out_ref.at[2 * x + y], send_sem=send_sems.at[7 * g + 6],
            recv_sem=recv_sems.at[7 * g + 6], device_id=(x, y, 1 - c), device_id_type=MESH_ID)
        first = [copy(k, (x, y, c), (*chip, c), src=w_ref.at[pl.ds(c * half, half), :]) for k, chip in enumerate(chips)]
        passed = [copy(3 + k, (*chip, c), (x, y, 1 - c)) for k, chip in enumerate(chips)]
        arrive = [copy(k, (*chip, c), (x, y, c)) for k, chip in enumerate(chips)]
        arrive2 = [copy(3 + k, (*chip, 1 - c), (x, y, c)) for k, chip in enumerate(chips)]
        return own, first, passed, arrive, arrive2

    made = [make(g) for g in range(len(w_refs))]
    for own, first, _, _, _ in made:
        own.start()
        for cp in first:
            cp.start()
    for _, _, passed, arrive, _ in made:
        for k in range(3):
            arrive[k].wait_recv()
            passed[k].start()
    for own, first, passed, _, arrive2 in made:
        for k in range(3):
            arrive2[k].wait_recv()
        own.wait_recv()
        for cp in first + passed + [own]:
            cp.wait_send()


def _sibling(x, y, c):
    return [(x, y, 1 - c)]


def _same_core_of_other_chips(x, y, c):
    return [(_flip(x, a), _flip(y, b), c) for a, b in CHIP_RELATIONS]


def _exchange(body, ins, out_types, n_sems, name, sequencer=None):
    n = len(ins)
    sems = (pltpu.SemaphoreType.DMA((n_sems,)), pltpu.SemaphoreType.DMA((n_sems,)))
    if sequencer is None:
        in_place = out_types is None
        out_shape = [jax.ShapeDtypeStruct(a.shape, a.dtype) for a in ins] if in_place else out_types

        def tc_body(*refs):
            body(refs[:n], refs[n:n + len(out_shape)], refs[-2], refs[-1])

        return list(pl.pallas_call(
            tc_body, out_shape=out_shape, in_specs=[ANY] * n, out_specs=[ANY] * len(out_shape),
            input_output_aliases={g: g for g in range(n)} if in_place else {}, scratch_shapes=list(sems), name=name)(*ins))

    collective_id, peers = sequencer
    hbm = pltpu.MemorySpace.HBM
    in_refs = [jax.new_ref(a, memory_space=hbm) for a in ins]
    out_refs = in_refs if out_types is None else [jax.empty_ref(t, memory_space=hbm) for t in out_types]

    @pl.kernel(mesh=plsc.ScalarSubcoreMesh(axis_name="sequencer", num_cores=1), name=name, scratch_types=sems,
               compiler_params=pltpu.CompilerParams(collective_id=collective_id))
    def launch(send_sems, recv_sems):
        barrier = pltpu.get_barrier_semaphore()
        devices = peers(*_place())
        for device in devices:
            pl.semaphore_signal(barrier, inc=1, device_id=device, device_id_type=MESH_ID)
        pl.semaphore_wait(barrier, len(devices))
        body(in_refs, out_refs, send_sems, recv_sems)

    launch()
    return [r[...] for r in out_refs]


def _allgather(shards, name, sequencer=None):
    out_types = [jax.ShapeDtypeStruct((N_CHIPS,) + s.shape, s.dtype) for s in shards]
    if sequencer is not None:
        sequencer = (sequencer, lambda x, y, c: _sibling(x, y, c) + _same_core_of_other_chips(x, y, c))
    return _exchange(_allgather_body, shards, out_types, 7 * len(shards), name, sequencer)


def _half_tile(half):
    return max(t for t in range(16, 1025, 16) if half % t == 0)


def _run_copies(cps):
    for cp in cps:
        cp.start()
    for cp in cps:
        cp.wait_recv()
    for cp in cps:
        cp.wait_send()


def _sibling_halves(gsends, name, sequencer=None):
    def body(g_refs, out_refs, send_sems, recv_sems):
        x, y, c = _place()
        cps = []
        for g, (g_ref, out_ref) in enumerate(zip(g_refs, out_refs)):
            half = g_ref.shape[1] // 2
            cps.append(pltpu.make_async_remote_copy(
                src_ref=g_ref.at[:, pl.ds((1 - c) * half, half), :], dst_ref=out_ref,
                send_sem=send_sems.at[g], recv_sem=recv_sems.at[g], device_id=(x, y, 1 - c), device_id_type=MESH_ID))
        _run_copies(cps)

    out_types = [jax.ShapeDtypeStruct((s.shape[0], s.shape[1] // 2, s.shape[2]), s.dtype) for s in gsends]
    return _exchange(body, gsends, out_types, len(gsends), name, sequencer and (sequencer, _sibling))


def _chip_sums(gsend, sib, place):
    n, rows, cols = gsend.shape
    half = rows // 2
    tm = _half_tile(half)
    nblk = half // tm

    def body(s_ref, g_ref, sib_ref, o_ref):
        o_ref[0] = (g_ref[0].astype(F32) + sib_ref[0].astype(F32)).astype(o_ref.dtype)

    grid_spec = pltpu.PrefetchScalarGridSpec(
        num_scalar_prefetch=1, grid=(n, nblk),
        in_specs=[pl.BlockSpec((1, tm, cols), lambda k, i, s: (jnp.bitwise_xor(s[0], k), s[1] * nblk + i, 0)),
                  pl.BlockSpec((1, tm, cols), lambda k, i, s: (jnp.bitwise_xor(s[0], k), i, 0))],
        out_specs=pl.BlockSpec((1, tm, cols), lambda k, i, s: (k, i, 0)))
    return pl.pallas_call(
        body, out_shape=jax.ShapeDtypeStruct((n, half, cols), BF16), grid_spec=grid_spec,
        name="rs_chip_sums", compiler_params=_params("parallel", "parallel"))(place, gsend, sib)


def _exchange_chip_sums(tsends, name, sequencer=None):
    def body(t_refs, out_refs, send_sems, recv_sems):
        x, y, c = _place()
        cps = []
        for g, (t_ref, out_ref) in enumerate(zip(t_refs, out_refs)):
            for k, device in enumerate(_same_core_of_other_chips(x, y, c)):
                cps.append(pltpu.make_async_remote_copy(
                    src_ref=t_ref.at[k + 1], dst_ref=out_ref.at[k], send_sem=send_sems.at[3 * g + k],
                    recv_sem=recv_sems.at[3 * g + k], device_id=device, device_id_type=MESH_ID))
        _run_copies(cps)

    out_types = [jax.ShapeDtypeStruct((3,) + s.shape[1:], s.dtype) for s in tsends]
    return _exchange(body, tsends, out_types, 3 * len(tsends), name,
                     sequencer and (sequencer, _same_core_of_other_chips))


def _final_sum(tsend, recv, place):
    n, half, cols = tsend.shape
    tm = _half_tile(half)
    nblk = half // tm

    def body(s_ref, t_ref, r_ref, o_ref):
        o_ref[...] = ((t_ref[0].astype(F32) + r_ref[0].astype(F32)) + r_ref[1].astype(F32)) + r_ref[2].astype(F32)

    grid_spec = pltpu.PrefetchScalarGridSpec(
        num_scalar_prefetch=1, grid=(nblk,),
        in_specs=[pl.BlockSpec((1, tm, cols), lambda i, s: (0, i, 0)), pl.BlockSpec((n - 1, tm, cols), lambda i, s: (0, i, 0))],
        out_specs=pl.BlockSpec((tm, cols), lambda i, s: (s[1] * nblk + i, 0)))
    return pl.pallas_call(
        body, out_shape=jax.ShapeDtypeStruct((2 * half, cols), F32), grid_spec=grid_spec, name="rs_final_sum",
        compiler_params=_params("parallel"))(place, tsend, recv)


def _join_halves(gfulls, name, sequencer=None):
    def body(g_refs, out_refs, send_sems, recv_sems):
        x, y, c = _place()
        n = len(g_refs)

        def copy(g, pc):
            half = g_refs[g].shape[0] // 2
            return pltpu.make_async_remote_copy(
                src_ref=g_refs[g].at[pl.ds(pc * half, half), :], dst_ref=out_refs[g].at[pl.ds(pc * half, half), :],
                send_sem=send_sems.at[g], recv_sem=recv_sems.at[g], device_id=(x, y, 1 - c), device_id_type=MESH_ID)

        mine = [copy(g, c) for g in range(n)]
        for cp in mine:
            cp.start()
        for g in range(n):
            copy(g, 1 - c).wait_recv()
        for cp in mine:
            cp.wait_send()

    return _exchange(body, gfulls, None, len(gfulls), name, sequencer and (sequencer, _sibling))


def _allreduce_small(v):
    rows, cols = v.shape

    def body(v_ref, out_ref, buf, send_sems, recv_sems):
        x, y, c = _place()
        cps = []
        for k in range(1, 8):
            peer = (_flip(x, (k >> 2) & 1), _flip(y, (k >> 1) & 1), _flip(c, k & 1))
            cps.append(pltpu.make_async_remote_copy(
                src_ref=v_ref, dst_ref=buf.at[k - 1], send_sem=send_sems.at[k - 1], recv_sem=recv_sems.at[k - 1],
                device_id=peer, device_id_type=MESH_ID))
        for cp in cps:
            cp.start()
        for cp in cps:
            cp.wait_recv()
        for cp in cps:
            cp.wait_send()
        t0 = v_ref[...] + buf[0]
        t1 = buf[1] + buf[2]
        t2 = buf[3] + buf[4]
        t3 = buf[5] + buf[6]
        out_ref[...] = (t0 + t1) + (t2 + t3)

    vm = pl.BlockSpec(memory_space=pltpu.VMEM)
    return pl.pallas_call(
        body, out_shape=jax.ShapeDtypeStruct((rows, cols), F32), in_specs=[vm], out_specs=vm,
        scratch_shapes=[pltpu.VMEM((7, rows, cols), F32), pltpu.SemaphoreType.DMA((7,)), pltpu.SemaphoreType.DMA((7,))],
        name="allreduce_small")(v)


BIG_INFO = {n: (shape, ax) for n, shape, ax in BIG}
GROUPS = (("w_in",), ("w_ffn_gate", "w_ffn_up"), ("w_out", "w_ffn_down", "w_ple_gate"),
          ("w_branch_a", "w_branch_b", "w_ple_proj"))


def _shard_shape(name):
    (k, m), ax = BIG_INFO[name]
    return (k // N_CHIPS, m) if ax == 0 else (k, m // N_CHIPS)


def _group_rows(group):
    offs, off = {}, 0
    for n in group:
        offs[n] = off
        off += _shard_shape(n)[0]
    return offs, off


def _pack_groups(shards, layer, dtype):
    return [jnp.concatenate([shards[n][layer].astype(dtype) for n in group], axis=0) for group in GROUPS]


def _unpack_full(gathered):
    out = {}
    for group, arr in zip(GROUPS, gathered):
        offs, _ = _group_rows(group)
        for n in group:
            rows, cols = _shard_shape(n)
            (k, m), ax = BIG_INFO[n]
            slab = arr[:, offs[n]:offs[n] + rows]
            out[n] = slab.reshape(k, m) if ax == 0 else jnp.transpose(slab, (1, 0, 2)).reshape(k, m)
    return out


def _pack_grads(gfull):
    out = []
    for group in GROUPS:
        parts = []
        for n in group:
            rows, cols = _shard_shape(n)
            ax = BIG_INFO[n][1]
            slab = (gfull[n].reshape(N_CHIPS, rows, cols) if ax == 0
                    else jnp.transpose(gfull[n].reshape(rows, N_CHIPS, cols), (1, 0, 2)))
            parts.append(slab.astype(BF16))
        out.append(jnp.concatenate(parts, axis=1))
    return out


def _after(values, mark):
    values, _ = lax.optimization_barrier((values, mark))
    return values


def _reduce_scatter(gsends, place, tag, sequencer_ids=None, hold=None):
    ids = sequencer_ids or (None, None, None)
    sibs = _sibling_halves(gsends, "rs_sibling_halves_" + tag, ids[0])
    tsends = [_chip_sums(g, s, place) for g, s in zip(gsends, sibs)]
    recvs = _exchange_chip_sums(tsends, "rs_exchange_" + tag, ids[1])
    if hold is not None:
        recvs = _after(recvs, hold)
    return _join_halves([_final_sum(t, r, place) for t, r in zip(tsends, recvs)], "rs_join_halves_" + tag, ids[2])


SMALL_SHAPES = {"rel_table": (NUM_BUCKETS, 2 * N_HEADS), "norm_mix_g": (DEPTH, D_MODEL), "qnorm_a_g": (DEPTH, HEAD_DIM),
                "knorm_a_g": (DEPTH, HEAD_DIM), "qnorm_b_g": (DEPTH, HEAD_DIM), "knorm_b_g": (DEPTH, HEAD_DIM),
                "sink_b": (DEPTH, N_HEADS), "norm_ffn_g": (DEPTH, D_MODEL), "norm_ple_g": (DEPTH, D_MODEL)}


def _pack_small(vals):
    flat = jnp.concatenate([vals[n].astype(F32).reshape(-1) for n in SMALL])
    flat = jnp.concatenate([flat, jnp.zeros((SMALL_ROWS * LANES - flat.shape[0],), F32)])
    return flat.reshape(SMALL_ROWS, LANES)


def _unpack_small(packed):
    flat, out, off = packed.reshape(-1), {}, 0
    for n in SMALL:
        size = math.prod(SMALL_SHAPES[n])
        out[n] = flat[off:off + size].reshape(SMALL_SHAPES[n])
        off += size
    return out


def _adamw(w, gs, g_row, m, v, name):
    c1 = 1.0 - ADAM_B1 ** ADAM_STEP
    c2 = 1.0 - ADAM_B2 ** ADAM_STEP
    total, width = w.shape
    n_layers = len(gs)
    per = total // n_layers
    tm = max(t for t in (512, 256, 128, 64, 32, 16, 8) if per % t == 0 and g_row % t == 0)
    nblk = per // tm

    def body(*refs):
        w_ref, g_refs = refs[0], refs[1:1 + n_layers]
        m_ref, v_ref, og, od, om, ov = refs[1 + n_layers:]
        layer = pl.program_id(0) // nblk
        g = g_refs[0][...]
        for l in range(1, n_layers):
            g = jnp.where(layer == l, g_refs[l][...], g)
        m_new = ADAM_B1 * m_ref[...] + (1.0 - ADAM_B1) * g
        v_new = ADAM_B2 * v_ref[...] + (1.0 - ADAM_B2) * (g * g)
        og[...] = g
        od[...] = -ADAM_LR * ((m_new / c1) / (jnp.sqrt(v_new / c2) + ADAM_EPS) + ADAM_WD * w_ref[...])
        om[...] = m_new
        ov[...] = v_new

    row = pl.BlockSpec((tm, width), lambda i: (i, 0))
    g_specs = [pl.BlockSpec((tm, width), lambda i, l=l: (g_row // tm + jnp.clip(i - l * nblk, 0, nblk - 1), 0))
               for l in range(n_layers)]
    return pl.pallas_call(
        body, out_shape=[jax.ShapeDtypeStruct((total, width), F32)] * 4, grid=(total // tm,),
        in_specs=[row] + g_specs + [row, row], out_specs=[row] * 4, name=name,
        compiler_params=_params("parallel"))(w, *gs, m, v)


def kernel(x, p, rel_table, norm_mix_g, w_in, qnorm_a_g, knorm_a_g, qnorm_b_g, knorm_b_g, sink_b, w_branch_a, w_branch_b, w_out, norm_ffn_g, w_ffn_gate, w_ffn_up, w_ffn_down, norm_ple_g, w_ple_gate, w_ple_proj, loss_target, m_rel_table, m_norm_mix_g, m_w_in, m_qnorm_a_g, m_knorm_a_g, m_qnorm_b_g, m_knorm_b_g, m_sink_b, m_w_branch_a, m_w_branch_b, m_w_out, m_norm_ffn_g, m_w_ffn_gate, m_w_ffn_up, m_w_ffn_down, m_norm_ple_g, m_w_ple_gate, m_w_ple_proj, v_rel_table, v_norm_mix_g, v_w_in, v_qnorm_a_g, v_knorm_a_g, v_qnorm_b_g, v_knorm_b_g, v_sink_b, v_w_branch_a, v_w_branch_b, v_w_out, v_norm_ffn_g, v_w_ffn_gate, v_w_ffn_up, v_w_ffn_down, v_norm_ple_g, v_w_ple_gate, v_w_ple_proj):
    given = dict(locals())
    weights = {n: given[n] for n in WEIGHTS}
    moments_m = {n: given["m_" + n] for n in WEIGHTS}
    moments_v = {n: given["v_" + n] for n in WEIGHTS}
    xi, yi, ci = _place()
    place = jnp.stack([2 * xi + yi, ci]).astype(jnp.int32)

    shards = [_pack_groups(weights, l, BF16) for l in range(DEPTH)]
    w_in0 = _allgather(shards[0][:1], "allgather_w_in_layer0")
    rest0 = _allgather(_after(shards[0][1:], w_in0), "allgather_rest_layer0", sequencer=1)
    gathered1 = _allgather(_after(shards[1], rest0), "allgather_layer1", sequencer=2)
    big = [_unpack_full(w_in0 + rest0), _unpack_full(gathered1)]
    small = {n: weights[n] for n in SMALL}

    loss, dx, gbig, gsmall, marks = _local_step(x[0], p[:, 0], loss_target[0], big, small)

    gsends = [_pack_grads(gbig[l]) for l in range(DEPTH)]
    red1 = _reduce_scatter(gsends[1], place, "layer1", (3, 4, 5), hold=marks[0]["ffn_bwd_done"])
    rest0 = _reduce_scatter(gsends[0][1:], place, "rest_layer0", (6, 7, 8), hold=marks[0]["attn_bwd_done"])
    greds = [_reduce_scatter(gsends[0][:1], place, "w_in_layer0") + rest0, red1]

    grads, delta, new_m, new_v = {}, {}, {}, {}
    for gi, group in enumerate(GROUPS):
        offs, _ = _group_rows(group)
        for n in group:
            shape = weights[n].shape
            two_d = lambda a: a.reshape(shape[0] * shape[1], shape[2])
            outs = _adamw(two_d(weights[n]), [greds[l][gi] for l in range(DEPTH)], offs[n], two_d(moments_m[n]),
                          two_d(moments_v[n]), "adamw_" + n)
            grads[n], delta[n], new_m[n], new_v[n] = (o.reshape(shape) for o in outs)
    small_grads = _allreduce_small(_pack_small(gsmall))
    g_, d_, m_, v_ = _adamw(_pack_small(weights), [small_grads], 0, _pack_small(moments_m), _pack_small(moments_v),
                            "adamw_small")
    grads.update(_unpack_small(g_))
    delta.update(_unpack_small(d_))
    new_m.update(_unpack_small(m_))
    new_v.update(_unpack_small(v_))

    loss = lax.psum(loss, ("x", "y", "c"))
    return (loss, dx[None], *[grads[n] for n in WEIGHTS], *[delta[n] for n in WEIGHTS],
            *[new_m[n] for n in WEIGHTS], *[new_v[n] for n in WEIGHTS])
```

```python
import functools
import math

import jax
import jax.numpy as jnp
from jax import lax
from jax.experimental import pallas as pl
from jax.experimental.pallas import tpu as pltpu
from jax.experimental.pallas import tpu_sc as plsc

F32 = jnp.float32
BF16 = jnp.bfloat16
MESH_ID = pl.DeviceIdType.MESH

SEQ = 2048
D_MODEL = 1024
DEPTH = 2
HEAD_DIM = 64
N_HEADS = 8
WIDTH = N_HEADS * HEAD_DIM
N_PAIRS = 4
ITEMS = 4
N_KV_B = 2
PLE_DIM = 256
D_FF = 2816
D_IN = 4352
OFF_QA, OFF_KA, OFF_VA, OFF_QB, OFF_KB, OFF_VB, OFF_GA, OFF_GB = 0, 512, 1024, 1536, 2048, 2176, 2304, 3328
DILATED = ((64, 1), (64, 4), (64, 16))
BLK_B = 128
NUM_BUCKETS = 32
MAX_DISTANCE = 1024
RMS_EPS = 1e-6
NEG_INF = -1e30
LANES = 128
VMEM_LIMIT = 48 * 1024 * 1024

ADAM_LR, ADAM_B1, ADAM_B2, ADAM_EPS, ADAM_WD, ADAM_STEP = 0.001, 0.9, 0.999, 1e-08, 0.01, 10

BIG = (
    ("w_in", (D_MODEL, D_IN), 1),
    ("w_branch_a", (WIDTH, D_MODEL), 1),
    ("w_branch_b", (WIDTH, D_MODEL), 1),
    ("w_out", (D_MODEL, D_MODEL), 0),
    ("w_ffn_gate", (D_MODEL, D_FF), 1),
    ("w_ffn_up", (D_MODEL, D_FF), 1),
    ("w_ffn_down", (D_FF, D_MODEL), 0),
    ("w_ple_gate", (D_MODEL, D_MODEL), 0),
    ("w_ple_proj", (PLE_DIM, D_MODEL), 1),
)
SMALL = ("rel_table", "norm_mix_g", "qnorm_a_g", "knorm_a_g", "qnorm_b_g", "knorm_b_g", "sink_b",
         "norm_ffn_g", "norm_ple_g")
WEIGHTS = ("rel_table", "norm_mix_g", "w_in", "qnorm_a_g", "knorm_a_g", "qnorm_b_g", "knorm_b_g", "sink_b",
           "w_branch_a", "w_branch_b", "w_out", "norm_ffn_g", "w_ffn_gate", "w_ffn_up", "w_ffn_down",
           "norm_ple_g", "w_ple_gate", "w_ple_proj")
N_CHIPS = 4
SMALL_ROWS = 64


def _params(*sem):
    return pltpu.CompilerParams(dimension_semantics=sem, vmem_limit_bytes=VMEM_LIMIT)


def _pick(dim, target):
    for t in (target, 512, 256, 128, 64, 32, 16, 8):
        if t <= target and dim % t == 0:
            return t
    return dim


def _mm(a, b, mode, out_dtype, name, res=None, tm=512, tn=512):
    if mode == "nn":
        (m, k), (_, n) = a.shape, b.shape
    elif mode == "nt":
        (m, k), (n, _) = a.shape, b.shape
    else:
        (k, m), (_, n) = a.shape, b.shape
    tm, tn = _pick(m, tm), _pick(n, tn)
    a_spec = pl.BlockSpec((k, tm), lambda i, j: (0, i)) if mode == "tn" else pl.BlockSpec((tm, k), lambda i, j: (i, 0))
    b_spec = pl.BlockSpec((tn, k), lambda i, j: (j, 0)) if mode == "nt" else pl.BlockSpec((k, tn), lambda i, j: (0, j))
    dims = {"nn": (((1,), (0,)), ((), ())), "nt": (((1,), (1,)), ((), ())), "tn": (((0,), (0,)), ((), ()))}[mode]
    has_res = res is not None

    def body(*refs):
        a_ref, b_ref = refs[0], refs[1]
        o_ref = refs[-1]
        acc = lax.dot_general(a_ref[...].astype(BF16), b_ref[...].astype(BF16), dims, preferred_element_type=F32)
        if has_res:
            acc = acc + refs[2][...]
        o_ref[...] = acc.astype(out_dtype)

    in_specs = [a_spec, b_spec]
    args = [a, b]
    if has_res:
        in_specs.append(pl.BlockSpec((tm, tn), lambda i, j: (i, j)))
        args.append(res)
    return pl.pallas_call(
        body, out_shape=jax.ShapeDtypeStruct((m, n), out_dtype), grid=(m // tm, n // tn), in_specs=in_specs,
        out_specs=pl.BlockSpec((tm, tn), lambda i, j: (i, j)), name=name,
        compiler_params=_params("parallel", "parallel"))(*args)


def _ew(fn, ins, out_dtypes, *, width, bw, name, vecs=(), tm=256):
    rows = ins[0][0].shape[0]
    tm = _pick(rows, tm)
    n_in = len(ins) + len(vecs)

    def col_map(off_blocks):
        return lambda i, j: (i, off_blocks + j)

    in_specs = [pl.BlockSpec((tm, bw), col_map(off // bw)) for _, off in ins]
    in_specs += [pl.BlockSpec((1, bw), lambda i, j: (0, j)) for _ in vecs]

    def body(*refs):
        outs = fn(*[r[...] for r in refs[:n_in]])
        for r, o in zip(refs[n_in:], outs):
            r[...] = o.astype(r.dtype)

    return pl.pallas_call(
        body, out_shape=[jax.ShapeDtypeStruct((rows, width), dt) for dt in out_dtypes],
        grid=(rows // tm, width // bw), in_specs=in_specs,
        out_specs=[pl.BlockSpec((tm, bw), lambda i, j: (i, j)) for _ in out_dtypes],
        name=name, compiler_params=_params("parallel", "parallel"))(*[a for a, _ in ins], *vecs)


def _sigmoid(x):
    return 1.0 / (1.0 + jnp.exp(-x))


def _seg_sum(v):
    outs = []
    for k in range(v.shape[1] // LANES):
        vp = v[:, k * LANES:(k + 1) * LANES]
        left = lax.broadcasted_iota(jnp.int32, vp.shape, 1) < HEAD_DIM
        sl = jnp.sum(jnp.where(left, vp, 0.0), axis=-1, keepdims=True)
        sr = jnp.sum(jnp.where(left, 0.0, vp), axis=-1, keepdims=True)
        outs.append(jnp.where(left, sl, sr))
    return outs[0] if len(outs) == 1 else jnp.concatenate(outs, axis=1)


def _seg_rstd(x):
    return lax.rsqrt(_seg_sum(x * x) * (1.0 / HEAD_DIM) + RMS_EPS)


def _rms_fwd(x, g, name):
    rows, d = x.shape
    tm = 256

    def body(x_ref, g_ref, h_ref):
        xv = x_ref[...]
        r = lax.rsqrt(jnp.mean(xv * xv, axis=-1, keepdims=True) + RMS_EPS)
        h_ref[...] = ((xv * r) * g_ref[...]).astype(BF16)

    return pl.pallas_call(
        body, out_shape=jax.ShapeDtypeStruct((rows, d), BF16), grid=(rows // tm,),
        in_specs=[pl.BlockSpec((tm, d), lambda i: (i, 0)), pl.BlockSpec((1, d), lambda i: (0, 0))],
        out_specs=pl.BlockSpec((tm, d), lambda i: (i, 0)), name=name, compiler_params=_params("parallel"))(x, g)


def _rms_bwd(x, g, dh, dres, name):
    rows, d = x.shape
    tm = 256

    def body(x_ref, g_ref, dh_ref, dres_ref, dx_ref, dg_ref):
        xv = x_ref[...]
        r = lax.rsqrt(jnp.mean(xv * xv, axis=-1, keepdims=True) + RMS_EPS)
        xh = xv * r
        dhv = dh_ref[...]
        dxh = dhv * g_ref[...]
        dx_ref[...] = dres_ref[...] + r * (dxh - xh * jnp.mean(dxh * xh, axis=-1, keepdims=True))
        part = jnp.sum(dhv * xh, axis=0, keepdims=True)

        @pl.when(pl.program_id(0) == 0)
        def _():
            dg_ref[...] = part

        @pl.when(pl.program_id(0) > 0)
        def _():
            dg_ref[...] += part

    row = pl.BlockSpec((tm, d), lambda i: (i, 0))
    vec = pl.BlockSpec((1, d), lambda i: (0, 0))
    return pl.pallas_call(
        body, out_shape=[jax.ShapeDtypeStruct((rows, d), F32), jax.ShapeDtypeStruct((1, d), F32)],
        grid=(rows // tm,), in_specs=[row, vec, row, row], out_specs=[row, vec],
        name=name, compiler_params=_params("arbitrary"))(x, g, dh, dres)


def _loss_grad(y, t):
    rows, d = y.shape
    tm = 256

    def body(y_ref, t_ref, dy_ref, l_ref):
        e = y_ref[...] - t_ref[...]
        dy_ref[...] = e * (1.0 / d)
        part = jnp.zeros((1, LANES), F32) + jnp.sum(e * e) * (0.5 / d)

        @pl.when(pl.program_id(0) == 0)
        def _():
            l_ref[...] = part

        @pl.when(pl.program_id(0) > 0)
        def _():
            l_ref[...] += part

    row = pl.BlockSpec((tm, d), lambda i: (i, 0))
    return pl.pallas_call(
        body, out_shape=[jax.ShapeDtypeStruct((rows, d), F32), jax.ShapeDtypeStruct((1, LANES), F32)],
        grid=(rows // tm,), in_specs=[row, row], out_specs=[row, pl.BlockSpec((1, LANES), lambda i: (0, 0))],
        name="loss_grad", compiler_params=_params("arbitrary"))(y, t)


def _put_pairs(ref, val):
    for hp in range(N_PAIRS):
        ref[hp] = val[:, hp * LANES:(hp + 1) * LANES].astype(ref.dtype)


def _get_pairs(ref):
    return jnp.concatenate([ref[hp] for hp in range(N_PAIRS)], axis=1)


def _swap_halves(v):
    return pltpu.roll(v, HEAD_DIM, axis=1)


def _expand_kv(kv):
    left = lax.broadcasted_iota(jnp.int32, kv.shape, 1) < HEAD_DIM
    sw = _swap_halves(kv)
    h0 = jnp.where(left, kv, sw)
    h1 = jnp.where(left, sw, kv)
    return jnp.concatenate([h0, h0, h1, h1], axis=1)


def _reduce_kv(dkv):
    left = lax.broadcasted_iota(jnp.int32, (dkv.shape[0], LANES), 1) < HEAD_DIM
    t = dkv[:, 0:LANES] + dkv[:, LANES:2 * LANES]
    u = dkv[:, 2 * LANES:3 * LANES] + dkv[:, 3 * LANES:4 * LANES]
    t = t + _swap_halves(t)
    u = u + _swap_halves(u)
    return jnp.where(left, t, u)


def _qknorm_fwd(proj, gqa, gka, gqb, gkb):
    rows = proj.shape[0]
    tm = 256

    def body(qa_ref, ka_ref, va_ref, qb_ref, kb_ref, vb_ref, gqa_ref, gka_ref, gqb_ref, gkb_ref,
             oqa, oka, ova, oqb, okb, ovb):
        for src, g_ref, dst in ((qa_ref, gqa_ref, oqa), (ka_ref, gka_ref, oka), (qb_ref, gqb_ref, oqb)):
            xv = src[...]
            _put_pairs(dst, (xv * _seg_rstd(xv)) * g_ref[...])
        _put_pairs(ova, va_ref[...])
        kv = kb_ref[...]
        _put_pairs(okb, _expand_kv((kv * _seg_rstd(kv)) * gkb_ref[...]))
        _put_pairs(ovb, _expand_kv(vb_ref[...]))

    def win(width, off):
        return pl.BlockSpec((tm, width), lambda i: (i, off // width))

    vec = lambda w: pl.BlockSpec((1, w), lambda i: (0, 0))
    out = pl.BlockSpec((N_PAIRS, tm, LANES), lambda i: (0, i, 0))
    return pl.pallas_call(
        body, out_shape=[jax.ShapeDtypeStruct((N_PAIRS, rows, LANES), F32)] * 6, grid=(rows // tm,),
        in_specs=[win(WIDTH, OFF_QA), win(WIDTH, OFF_KA), win(WIDTH, OFF_VA), win(WIDTH, OFF_QB),
                  win(LANES, OFF_KB), win(LANES, OFF_VB), vec(WIDTH), vec(WIDTH), vec(WIDTH), vec(LANES)],
        out_specs=[out] * 6, name="qknorm_fwd", compiler_params=_params("parallel"))(
            proj, proj, proj, proj, proj, proj, gqa, gka, gqb, gkb)


def _norm_bwd(xv, g, dy):
    r = _seg_rstd(xv)
    xh = xv * r
    dxh = dy * g
    dx = r * (dxh - xh * (_seg_sum(dxh * xh) * (1.0 / HEAD_DIM)))
    return dx, jnp.sum(dy * xh, axis=0, keepdims=True)


def _qknorm_bwd(proj, gqa, gka, gqb, gkb, dqa, dka, dva, dqb, dkb, dvb, dgab):
    rows = proj.shape[0]
    tm = 256
    n_a = len(dqa)

    def body(*refs):
        qa_ref, ka_ref, qb_ref, kb_ref, gqa_ref, gka_ref, gqb_ref, gkb_ref = refs[:8]
        pos = 8
        dqa_refs, dka_refs, dva_refs = refs[pos:pos + n_a], refs[pos + n_a:pos + 2 * n_a], refs[pos + 2 * n_a:pos + 3 * n_a]
        pos += 3 * n_a
        dqb_ref, dkb_ref, dvb_ref, dgab_ref = refs[pos:pos + 4]
        dproj_ref, ogqa, ogka, ogqb, ogkb = refs[pos + 4:]

        def total(rs):
            acc = _get_pairs(rs[0])
            for r in rs[1:]:
                acc = acc + _get_pairs(r)
            return acc

        dx_qa, p_qa = _norm_bwd(qa_ref[...], gqa_ref[...], total(dqa_refs))
        dx_ka, p_ka = _norm_bwd(ka_ref[...], gka_ref[...], total(dka_refs))
        dx_qb, p_qb = _norm_bwd(qb_ref[...], gqb_ref[...], _get_pairs(dqb_ref))
        dx_kb, p_kb = _norm_bwd(kb_ref[...], gkb_ref[...], _reduce_kv(_get_pairs(dkb_ref)))
        dproj_ref[:, OFF_QA:OFF_QA + WIDTH] = dx_qa.astype(BF16)
        dproj_ref[:, OFF_KA:OFF_KA + WIDTH] = dx_ka.astype(BF16)
        dproj_ref[:, OFF_VA:OFF_VA + WIDTH] = total(dva_refs).astype(BF16)
        dproj_ref[:, OFF_QB:OFF_QB + WIDTH] = dx_qb.astype(BF16)
        dproj_ref[:, OFF_KB:OFF_KB + LANES] = dx_kb.astype(BF16)
        dproj_ref[:, OFF_VB:OFF_VB + LANES] = _reduce_kv(_get_pairs(dvb_ref)).astype(BF16)
        dproj_ref[:, OFF_GA:D_IN] = dgab_ref[...]
        first = pl.program_id(0) == 0
        for o_ref, part in ((ogqa, p_qa), (ogka, p_ka), (ogqb, p_qb), (ogkb, p_kb)):
            @pl.when(first)
            def _(o_ref=o_ref, part=part):
                o_ref[...] = part

            @pl.when(jnp.logical_not(first))
            def _(o_ref=o_ref, part=part):
                o_ref[...] += part

    def win(width, off):
        return pl.BlockSpec((tm, width), lambda i: (i, off // width))

    vec = lambda w: pl.BlockSpec((1, w), lambda i: (0, 0))
    row = lambda w: pl.BlockSpec((tm, w), lambda i: (i, 0))
    in_specs = [win(WIDTH, OFF_QA), win(WIDTH, OFF_KA), win(WIDTH, OFF_QB), win(LANES, OFF_KB),
                vec(WIDTH), vec(WIDTH), vec(WIDTH), vec(LANES)]
    in_specs += [pl.BlockSpec((N_PAIRS, tm, LANES), lambda i: (0, i, 0))] * (3 * n_a + 3) + [row(2 * D_MODEL)]
    return pl.pallas_call(
        body,
        out_shape=[jax.ShapeDtypeStruct((rows, D_IN), BF16), jax.ShapeDtypeStruct((1, WIDTH), F32),
                   jax.ShapeDtypeStruct((1, WIDTH), F32), jax.ShapeDtypeStruct((1, WIDTH), F32),
                   jax.ShapeDtypeStruct((1, LANES), F32)],
        grid=(rows // tm,), in_specs=in_specs,
        out_specs=[row(D_IN), vec(WIDTH), vec(WIDTH), vec(WIDTH), vec(LANES)],
        name="qknorm_bwd", compiler_params=_params("arbitrary"))(
            proj, proj, proj, proj, gqa, gka, gqb, gkb, *dqa, *dka, *dva, dqb, dkb, dvb, dgab)


def _t5_bucket(rel):
    half_b = NUM_BUCKETS // 2
    max_exact = half_b // 2
    sign = jnp.where(rel > 0, half_b, 0)
    n = jnp.abs(rel)
    nf = jnp.maximum(n, 1).astype(F32)
    large = max_exact + (jnp.log(nf / max_exact) / math.log(MAX_DISTANCE / max_exact)
                         * (half_b - max_exact)).astype(jnp.int32)
    large = jnp.minimum(large, half_b - 1)
    return sign + jnp.where(n < max_exact, n, large)


def _band_buckets(blk, dilation):
    i = jnp.arange(blk, dtype=jnp.int32)[:, None]
    j = jnp.arange(3 * blk, dtype=jnp.int32)[None, :]
    rel = j - blk - i
    return jnp.where(jnp.abs(rel) <= blk, _t5_bucket(rel * dilation), -1)


def _bias_tiles(table, buckets, head_off, name):
    blk = buckets.shape[0]

    def body(tab_ref, bk_ref, o_ref):
        h = pl.program_id(0) + head_off
        bk = bk_ref[...]
        acc = jnp.full(bk.shape, NEG_INF, F32)
        for b in range(NUM_BUCKETS):
            acc = jnp.where(bk == b, tab_ref[b, h], acc)
        o_ref[0] = acc

    return pl.pallas_call(
        body, out_shape=jax.ShapeDtypeStruct((N_HEADS, blk, 3 * blk), F32), grid=(N_HEADS,),
        in_specs=[pl.BlockSpec(memory_space=pltpu.SMEM), pl.BlockSpec((blk, 3 * blk), lambda h: (0, 0))],
        out_specs=pl.BlockSpec((1, blk, 3 * blk), lambda h: (h, 0, 0)),
        name=name, compiler_params=_params("parallel"))(table, buckets)


def _table_grad(dbias, buckets, name):
    blk = buckets.shape[0]

    def body(db_ref, bk_ref, o_ref):
        bk = bk_ref[...]
        dbv = db_ref[0]
        lane = lax.broadcasted_iota(jnp.int32, (1, LANES), 1)
        acc = jnp.zeros((1, LANES), F32)
        for b in range(NUM_BUCKETS):
            acc = jnp.where(lane == b, jnp.sum(jnp.where(bk == b, dbv, 0.0)), acc)
        o_ref[0] = acc

    out = pl.pallas_call(
        body, out_shape=jax.ShapeDtypeStruct((N_HEADS, 1, LANES), F32), grid=(N_HEADS,),
        in_specs=[pl.BlockSpec((1, blk, 3 * blk), lambda h: (h, 0, 0)), pl.BlockSpec((blk, 3 * blk), lambda h: (0, 0))],
        out_specs=pl.BlockSpec((1, 1, LANES), lambda h: (h, 0, 0)),
        name=name, compiler_params=_params("parallel"))(dbias, buckets)
    return out[:, 0, :NUM_BUCKETS]


def _dot_nt(a, b):
    return lax.dot_general(a, b, (((1,), (1,)), ((), ())), preferred_element_type=F32)


def _stack_pair(x2, left):
    return jnp.concatenate([jnp.where(left, x2, 0.0), jnp.where(left, 0.0, x2)], axis=0).astype(BF16)


def _attn_geometry(blk, d):
    chunk = blk * ITEMS if d == 1 else blk * d
    groups = 1 if d == 1 else d // ITEMS
    halo = blk if d == 1 else chunk
    return chunk, groups, halo


def _item_rows(ref, j, r0, blk, d):
    if d == 1:
        return ref[j * blk:(j + 1) * blk, :]
    return ref[pl.ds(r0 + j, blk, stride=d), :]


def _item_penalty(t, nct, j, blk, d):
    first_ok, last_ok = t > 0, t < nct - 1
    if d == 1:
        first_ok = True if j > 0 else first_ok
        last_ok = True if j < ITEMS - 1 else last_ok
    col = lax.broadcasted_iota(jnp.int32, (1, 3 * blk), 1)
    ok = jnp.logical_and(jnp.logical_or(col >= blk, first_ok), jnp.logical_or(col < 2 * blk, last_ok))
    return jnp.where(ok, 0.0, NEG_INF).astype(F32)


def _attn_specs(seq, blk, d, step_of):
    chunk, _, halo = _attn_geometry(blk, d)
    per, last = chunk // halo, seq // halo - 1
    cur = pl.BlockSpec((None, chunk, LANES), lambda hp, t: (hp, step_of(t), 0))
    prev = pl.BlockSpec((None, halo, LANES), lambda hp, t: (hp, jnp.clip(step_of(t) * per - 1, 0, last), 0))
    nxt = pl.BlockSpec((None, halo, LANES), lambda hp, t: (hp, jnp.minimum((step_of(t) + 1) * per, last), 0))
    return cur, prev, nxt


def _attn_fwd(q, k, v, bias, sink, blk, d, name):
    _, seq, _ = q.shape
    chunk, groups, _ = _attn_geometry(blk, d)
    nct = seq // chunk
    has_sink = sink is not None
    scale = HEAD_DIM ** -0.5

    def body(*refs):
        q_ref, kp, kc, kn, vp, vc, vn, b_ref = refs[:8]
        s_ref = refs[8] if has_sink else None
        o_ref, l_ref = refs[-2], refs[-1]
        t = pl.program_id(1)
        left = lax.broadcasted_iota(jnp.int32, (1, LANES), 1) < HEAD_DIM
        bias2 = b_ref[...]
        if d == 1:
            kwin = jnp.concatenate([kp[...], kc[...], kn[...]], axis=0).astype(BF16)
            vwin = jnp.concatenate([vp[...], vc[...], vn[...]], axis=0).astype(BF16)

        def group(r0):
            scores, vcats = [], []
            for j in range(ITEMS):
                qs = _stack_pair(_item_rows(q_ref, j, r0, blk, d) * scale, left)
                if d == 1:
                    kcat, vcat = kwin[j * blk:(j + 3) * blk], vwin[j * blk:(j + 3) * blk]
                else:
                    kcat = jnp.concatenate([_item_rows(r, j, r0, blk, d) for r in (kp, kc, kn)], axis=0).astype(BF16)
                    vcat = jnp.concatenate([_item_rows(r, j, r0, blk, d) for r in (vp, vc, vn)], axis=0).astype(BF16)
                scores.append(_dot_nt(qs, kcat) + bias2 + _item_penalty(t, nct, j, blk, d))
                vcats.append(vcat)
            s = jnp.concatenate(scores, axis=0)
            m = jnp.max(s, axis=-1, keepdims=True)
            if has_sink:
                sk = jnp.concatenate([s_ref[...]] * ITEMS, axis=0)
                m = jnp.maximum(m, sk)
            p = jnp.exp(s - m)
            den = jnp.sum(p, axis=-1, keepdims=True)
            if has_sink:
                den = den + jnp.exp(sk - m)
            pn = (p * (1.0 / den)).astype(BF16)
            lse = m + jnp.log(den)
            for j in range(ITEMS):
                top, mid, bot = 2 * j * blk, (2 * j + 1) * blk, (2 * j + 2) * blk
                o2 = jnp.dot(pn[top:bot], vcats[j], preferred_element_type=F32)
                o_val = jnp.where(left, o2[:blk], o2[blk:])
                l_val = jnp.where(left, lse[top:mid], lse[mid:bot])
                if d == 1:
                    o_ref[j * blk:(j + 1) * blk, :] = o_val
                    l_ref[j * blk:(j + 1) * blk, :] = l_val
                else:
                    o_ref[pl.ds(r0 + j, blk, stride=d), :] = o_val
                    l_ref[pl.ds(r0 + j, blk, stride=d), :] = l_val

        if groups == 1:
            group(0)
        else:
            def step(g, carry):
                group(g * ITEMS)
                return carry

            lax.fori_loop(0, groups, step, 0)

    cur, prev, nxt = _attn_specs(seq, blk, d, lambda t: t)
    in_specs = [cur, prev, cur, nxt, prev, cur, nxt, pl.BlockSpec((2 * blk, 3 * blk), lambda hp, t: (hp, 0))]
    args = [q, k, k, k, v, v, v, bias]
    if has_sink:
        in_specs.append(pl.BlockSpec((2 * blk, 1), lambda hp, t: (hp, 0)))
        args.append(sink)
    return pl.pallas_call(
        body, out_shape=[jax.ShapeDtypeStruct(q.shape, F32)] * 2, grid=(N_PAIRS, nct),
        in_specs=in_specs, out_specs=[cur, cur], name=name, compiler_params=_params("parallel", "parallel"))(*args)


def _attn_bwd(q, k, v, do, lse, delta, bias, sink, blk, d, name):
    _, seq, _ = q.shape
    chunk, groups, halo = _attn_geometry(blk, d)
    nct = seq // chunk
    has_sink = sink is not None
    n_in = 12 if has_sink else 11
    scale = HEAD_DIM ** -0.5

    def body(*refs):
        q_ref, kp, kc, kn, vp, vc, vn, do_ref, l_ref, d_ref, b_ref = refs[:11]
        s_ref = refs[11] if has_sink else None
        dq_ref, dk_ref, dv_ref, db_ref = refs[n_in:n_in + 4]
        ds_ref = refs[n_in + 4] if has_sink else None
        wk, wv = refs[-2], refs[-1]
        t = pl.program_id(1)

        @pl.when(t == 0)
        def _():
            wk[...] = jnp.zeros_like(wk)
            wv[...] = jnp.zeros_like(wv)
            db_ref[...] = jnp.zeros_like(db_ref)
            if has_sink:
                ds_ref[...] = jnp.zeros_like(ds_ref)

        @pl.when(t > 0)
        def _():
            for w in (wk, wv):
                keep = w[chunk:2 * chunk + halo]
                w[0:chunk + halo] = keep
                w[chunk + halo:2 * chunk + halo] = jnp.zeros((chunk, LANES), F32)

        @pl.when(t < nct)
        def _():
            lane = lax.broadcasted_iota(jnp.int32, (1, LANES), 1)
            left = lane < HEAD_DIM
            bias2 = b_ref[...]
            if d == 1:
                kwin = jnp.concatenate([kp[...], kc[...], kn[...]], axis=0).astype(BF16)
                vwin = jnp.concatenate([vp[...], vc[...], vn[...]], axis=0).astype(BF16)

            def group(r0):
                qss, doss, kcats, scores, dps, lcols, dcols = [], [], [], [], [], [], []
                for j in range(ITEMS):
                    qs = _stack_pair(_item_rows(q_ref, j, r0, blk, d) * scale, left)
                    dos = _stack_pair(_item_rows(do_ref, j, r0, blk, d), left)
                    if d == 1:
                        kcat, vcat = kwin[j * blk:(j + 3) * blk], vwin[j * blk:(j + 3) * blk]
                    else:
                        kcat = jnp.concatenate([_item_rows(r, j, r0, blk, d) for r in (kp, kc, kn)], axis=0).astype(BF16)
                        vcat = jnp.concatenate([_item_rows(r, j, r0, blk, d) for r in (vp, vc, vn)], axis=0).astype(BF16)
                    l2, d2 = _item_rows(l_ref, j, r0, blk, d), _item_rows(d_ref, j, r0, blk, d)
                    lcols.append(jnp.max(jnp.where(left, l2, NEG_INF), axis=-1, keepdims=True))
                    lcols.append(jnp.max(jnp.where(left, NEG_INF, l2), axis=-1, keepdims=True))
                    dcols.append(jnp.sum(jnp.where(lane == 0, d2, 0.0), axis=-1, keepdims=True))
                    dcols.append(jnp.sum(jnp.where(lane == HEAD_DIM, d2, 0.0), axis=-1, keepdims=True))
                    scores.append(_dot_nt(qs, kcat) + bias2 + _item_penalty(t, nct, j, blk, d))
                    dps.append(_dot_nt(dos, vcat))
                    qss.append(qs)
                    doss.append(dos)
                    kcats.append(kcat)
                lcol = jnp.concatenate(lcols, axis=0)
                dcol = jnp.concatenate(dcols, axis=0)
                p = jnp.exp(jnp.concatenate(scores, axis=0) - lcol)
                ds = p * (jnp.concatenate(dps, axis=0) - dcol)
                if has_sink:
                    sgrad = dcol * jnp.exp(jnp.concatenate([s_ref[...]] * ITEMS, axis=0) - lcol)
                for j in range(ITEMS):
                    top, bot = 2 * j * blk, (2 * j + 2) * blk
                    dsj, pj = ds[top:bot], p[top:bot]
                    db_ref[...] += dsj
                    if has_sink:
                        ds_ref[...] -= sgrad[top:bot]
                    dq2 = jnp.dot(dsj.astype(BF16), kcats[j], preferred_element_type=F32) * scale
                    dq_val = jnp.where(left, dq2[:blk], dq2[blk:])
                    if d == 1:
                        dq_ref[j * blk:(j + 1) * blk, :] = dq_val
                    else:
                        dq_ref[pl.ds(r0 + j, blk, stride=d), :] = dq_val
                    dk_new = jnp.dot(jnp.transpose(dsj).astype(BF16), qss[j], preferred_element_type=F32)
                    dv_new = jnp.dot(jnp.transpose(pj).astype(BF16), doss[j], preferred_element_type=F32)
                    for w, new in ((wk, dk_new), (wv, dv_new)):
                        if d == 1:
                            w[chunk + (j - 1) * blk:chunk + (j + 2) * blk, :] += new
                        else:
                            for c in range(3):
                                w[pl.ds(c * chunk + r0 + j, blk, stride=d), :] += new[c * blk:(c + 1) * blk]

            if groups == 1:
                group(0)
            else:
                def step(g, carry):
                    group(g * ITEMS)
                    return carry

                lax.fori_loop(0, groups, step, 0)

        dk_ref[...] = wk[0:chunk]
        dv_ref[...] = wv[0:chunk]

    cur, prev, nxt = _attn_specs(seq, blk, d, lambda t: jnp.minimum(t, nct - 1))
    lag = pl.BlockSpec((None, chunk, LANES), lambda hp, t: (hp, jnp.maximum(t - 1, 0), 0))
    band = pl.BlockSpec((2 * blk, 3 * blk), lambda hp, t: (hp, 0))
    col = pl.BlockSpec((2 * blk, 1), lambda hp, t: (hp, 0))
    in_specs = [cur, prev, cur, nxt, prev, cur, nxt, cur, cur, cur, band]
    args = [q, k, k, k, v, v, v, do, lse, delta, bias]
    out_shape = [jax.ShapeDtypeStruct(q.shape, F32)] * 3 + [jax.ShapeDtypeStruct((N_HEADS * blk, 3 * blk), F32)]
    out_specs = [cur, lag, lag, band]
    if has_sink:
        in_specs.append(col)
        args.append(sink)
        out_shape.append(jax.ShapeDtypeStruct((N_HEADS * blk, 1), F32))
        out_specs.append(col)
    window = pltpu.VMEM((2 * chunk + halo, LANES), F32)
    return pl.pallas_call(
        body, out_shape=out_shape, grid=(N_PAIRS, nct + 1), in_specs=in_specs, out_specs=out_specs,
        scratch_shapes=[window, window], name=name, compiler_params=_params("arbitrary", "arbitrary"))(*args)


def _combine_patterns(outs, lses):
    _, rows, _ = outs[0].shape
    tm = 256
    n = len(outs)

    def body(*refs):
        o_refs, l_refs = refs[:n], refs[n:2 * n]
        y_ref, lse_ref = refs[2 * n], refs[2 * n + 1]
        for hp in range(N_PAIRS):
            ls = [r[hp] for r in l_refs]
            m = functools.reduce(jnp.maximum, ls)
            es = [jnp.exp(l - m) for l in ls]
            den = functools.reduce(lambda a, b: a + b, es)
            num = functools.reduce(lambda a, b: a + b, [e * r[hp] for e, r in zip(es, o_refs)])
            y_ref[:, hp * LANES:(hp + 1) * LANES] = num / den
            lse_ref[hp] = m + jnp.log(den)

    pm = pl.BlockSpec((N_PAIRS, tm, LANES), lambda i: (0, i, 0))
    return pl.pallas_call(
        body, out_shape=[jax.ShapeDtypeStruct((rows, WIDTH), F32), jax.ShapeDtypeStruct((N_PAIRS, rows, LANES), F32)],
        grid=(rows // tm,), in_specs=[pm] * (2 * n), out_specs=[pl.BlockSpec((tm, WIDTH), lambda i: (i, 0)), pm],
        name="combine_a", compiler_params=_params("parallel"))(*outs, *lses)


def _pairs_to_tokens(a):
    _, rows, _ = a.shape
    tm = 256

    def body(a_ref, o_ref):
        o_ref[...] = _get_pairs(a_ref)

    return pl.pallas_call(
        body, out_shape=jax.ShapeDtypeStruct((rows, WIDTH), a.dtype), grid=(rows // tm,),
        in_specs=[pl.BlockSpec((N_PAIRS, tm, LANES), lambda i: (0, i, 0))],
        out_specs=pl.BlockSpec((tm, WIDTH), lambda i: (i, 0)), name="pairs_to_tokens",
        compiler_params=_params("parallel"))(a)


def _attn_bwd_prep(dy, y, name):
    rows = dy.shape[0]
    tm = 256

    def body(dy_ref, y_ref, do_ref, dl_ref):
        dyv = dy_ref[...]
        _put_pairs(do_ref, dyv)
        _put_pairs(dl_ref, _seg_sum(dyv * y_ref[...]))

    tok = pl.BlockSpec((tm, WIDTH), lambda i: (i, 0))
    pm = pl.BlockSpec((N_PAIRS, tm, LANES), lambda i: (0, i, 0))
    return pl.pallas_call(
        body, out_shape=[jax.ShapeDtypeStruct((N_PAIRS, rows, LANES), F32)] * 2, grid=(rows // tm,),
        in_specs=[tok, tok], out_specs=[pm, pm], name=name, compiler_params=_params("parallel"))(dy, y)


def _tile_gain(g, reps):
    return jnp.tile(g[None, :], (1, reps))


def _local_step(x, p, target, w_in_of, rest_of, small):
    rel_table = small["rel_table"]
    buckets_a = [_band_buckets(blk, d) for blk, d in DILATED]
    buckets_b = _band_buckets(BLK_B, 1)
    bias_a = [_bias_tiles(rel_table, bk, 0, "bias_a").reshape(N_HEADS * bk.shape[0], -1) for bk in buckets_a]
    bias_b = _bias_tiles(rel_table, buckets_b, N_HEADS, "bias_b").reshape(N_HEADS * BLK_B, -1)

    saved = []
    for l in range(DEPTH):
        g_mix, g_ffn, g_ple = (small[n][l][None, :] for n in ("norm_mix_g", "norm_ffn_g", "norm_ple_g"))
        gqa, gka, gqb = (_tile_gain(small[n][l], N_HEADS) for n in ("qnorm_a_g", "knorm_a_g", "qnorm_b_g"))
        gkb = _tile_gain(small["knorm_b_g"][l], N_KV_B)
        sink = jnp.repeat(small["sink_b"][l], BLK_B)[:, None]

        h = _rms_fwd(x, g_mix, "rms_mix")
        w_in = w_in_of(l, h)
        proj = _mm(h, w_in, "nn", F32, "mm_in")
        qa, ka, va, qb, kb, vb = _qknorm_fwd(proj, gqa, gka, gqb, gkb)
        outs, lses = [], []
        for (blk, d), bias in zip(DILATED, bias_a):
            o, ls = _attn_fwd(qa, ka, va, bias, None, blk, d, f"attn_a{d}_fwd")
            outs.append(o)
            lses.append(ls)
        ya, lse_a = _combine_patterns(outs, lses)
        yb, lse_b = _attn_fwd(qb, kb, vb, bias_b, sink, BLK_B, 1, "attn_b_fwd")
        yb = _pairs_to_tokens(yb)
        w = dict(rest_of(l, yb), w_in=w_in)
        ca = _mm(ya, w["w_branch_a"], "nn", F32, "mm_branch_a")
        cb = _mm(yb, w["w_branch_b"], "nn", F32, "mm_branch_b")

        def gate(ca_, cb_, ga_, gb_):
            return (_sigmoid(ga_) * ca_ + _sigmoid(gb_) * cb_,)

        (merged,) = _ew(gate, [(ca, 0), (cb, 0), (proj, OFF_GA), (proj, OFF_GB)], [BF16],
                        width=D_MODEL, bw=256, name="gate")
        x1 = _mm(merged, w["w_out"], "nn", F32, "mm_out", res=x)

        h2 = _rms_fwd(x1, g_ffn, "rms_ffn")
        a = _mm(h2, w["w_ffn_gate"], "nn", F32, "mm_ffn_gate")
        u = _mm(h2, w["w_ffn_up"], "nn", F32, "mm_ffn_up")

        def swiglu(a_, u_):
            return ((a_ * _sigmoid(a_)) * u_,)

        (hid,) = _ew(swiglu, [(a, 0), (u, 0)], [BF16], width=D_FF, bw=D_FF, name="swiglu")
        x2 = _mm(hid, w["w_ffn_down"], "nn", F32, "mm_ffn_down", res=x1)

        h3 = _rms_fwd(x2, g_ple, "rms_ple")
        z = _mm(h3, w["w_ple_gate"], "nn", F32, "mm_ple_gate")
        e = _mm(p[l], w["w_ple_proj"], "nn", F32, "mm_ple_proj")

        def ple(x2_, z_, e_):
            return (x2_ + _sigmoid(z_) * e_,)

        (x3,) = _ew(ple, [(x2, 0), (z, 0), (e, 0)], [F32], width=D_MODEL, bw=D_MODEL, name="ple")
        saved.append(dict(w=w, x0=x, h=h, proj=proj, qa=qa, ka=ka, va=va, qb=qb, kb=kb, vb=vb, ya=ya, lse_a=lse_a,
                          yb=yb, lse_b=lse_b, ca=ca, cb=cb, merged=merged, x1=x1, h2=h2, a=a, u=u, hid=hid,
                          x2=x2, h3=h3, z=z, e=e))
        x = x3

    dx, loss_acc = _loss_grad(x, target)
    loss = loss_acc[0, 0]

    gbig = [{} for _ in range(DEPTH)]
    marks = [{} for _ in range(DEPTH)]
    gsmall = {n: [None] * DEPTH for n in SMALL if n != "rel_table"}
    dtable_a = jnp.zeros((N_HEADS, NUM_BUCKETS), F32)
    dtable_b = jnp.zeros((N_HEADS, NUM_BUCKETS), F32)

    for l in reversed(range(DEPTH)):
        sv = saved[l]
        w = sv["w"]
        g_mix, g_ffn, g_ple = (small[n][l][None, :] for n in ("norm_mix_g", "norm_ffn_g", "norm_ple_g"))
        gqa, gka, gqb = (_tile_gain(small[n][l], N_HEADS) for n in ("qnorm_a_g", "knorm_a_g", "qnorm_b_g"))
        gkb = _tile_gain(small["knorm_b_g"][l], N_KV_B)
        sink = jnp.repeat(small["sink_b"][l], BLK_B)[:, None]

        def ple_bwd(dx_, z_, e_):
            s = _sigmoid(z_)
            return dx_ * s, dx_ * e_ * (s * (1.0 - s))

        de, dz = _ew(ple_bwd, [(dx, 0), (sv["z"], 0), (sv["e"], 0)], [BF16, BF16], width=D_MODEL, bw=D_MODEL,
                     name="ple_bwd")
        gbig[l]["w_ple_proj"] = _mm(p[l], de, "tn", F32, "mm_d_ple_proj")
        gbig[l]["w_ple_gate"] = _mm(sv["h3"], dz, "tn", F32, "mm_d_ple_gate")
        dh3 = _mm(dz, w["w_ple_gate"], "nt", F32, "mm_dh3")
        dx, gsmall["norm_ple_g"][l] = _rms_bwd(sv["x2"], g_ple, dh3, dx, "rms_ple_bwd")

        dhid = _mm(dx, w["w_ffn_down"], "nt", F32, "mm_dhid")
        gbig[l]["w_ffn_down"] = _mm(sv["hid"], dx, "tn", F32, "mm_d_ffn_down")

        def swiglu_bwd(a_, u_, dh_):
            s = _sigmoid(a_)
            return dh_ * u_ * (s * (1.0 + a_ * (1.0 - s))), dh_ * (a_ * s)

        da, du = _ew(swiglu_bwd, [(sv["a"], 0), (sv["u"], 0), (dhid, 0)], [BF16, BF16], width=D_FF, bw=D_FF,
                     name="swiglu_bwd")
        gbig[l]["w_ffn_gate"] = _mm(sv["h2"], da, "tn", F32, "mm_d_ffn_gate")
        gbig[l]["w_ffn_up"] = _mm(sv["h2"], du, "tn", F32, "mm_d_ffn_up")
        dh2 = _mm(da, w["w_ffn_gate"], "nt", F32, "mm_dh2_gate")
        dh2 = _mm(du, w["w_ffn_up"], "nt", F32, "mm_dh2_up", res=dh2)
        dx, gsmall["norm_ffn_g"][l] = _rms_bwd(sv["x1"], g_ffn, dh2, dx, "rms_ffn_bwd")

        dmerged = _mm(dx, w["w_out"], "nt", F32, "mm_dmerged")
        marks[l]["ffn_bwd_done"] = dmerged
        gbig[l]["w_out"] = _mm(sv["merged"], dx, "tn", F32, "mm_d_out")

        def gate_bwd(dm_, ca_, cb_, ga_, gb_):
            sa, sb = _sigmoid(ga_), _sigmoid(gb_)
            return dm_ * sa, dm_ * sb, dm_ * ca_ * (sa * (1.0 - sa)), dm_ * cb_ * (sb * (1.0 - sb))

        dca, dcb, dga, dgb = _ew(gate_bwd, [(dmerged, 0), (sv["ca"], 0), (sv["cb"], 0), (sv["proj"], OFF_GA),
                                            (sv["proj"], OFF_GB)], [BF16, BF16, BF16, BF16],
                                 width=D_MODEL, bw=256, name="gate_bwd")
        dgab = jnp.concatenate([dga, dgb], axis=1)
        gbig[l]["w_branch_a"] = _mm(sv["ya"], dca, "tn", F32, "mm_d_branch_a")
        gbig[l]["w_branch_b"] = _mm(sv["yb"], dcb, "tn", F32, "mm_d_branch_b")
        dya = _mm(dca, w["w_branch_a"], "nt", F32, "mm_dya")
        dyb = _mm(dcb, w["w_branch_b"], "nt", F32, "mm_dyb")

        dya, delta_a = _attn_bwd_prep(dya, sv["ya"], "attn_a_bwd_prep")
        dyb, delta_b = _attn_bwd_prep(dyb, sv["yb"], "attn_b_bwd_prep")

        dqa, dka, dva = [], [], []
        for (blk, d), bias, bk in zip(DILATED, bias_a, buckets_a):
            dq_, dk_, dv_, db_ = _attn_bwd(sv["qa"], sv["ka"], sv["va"], dya, sv["lse_a"], delta_a, bias, None, blk, d,
                                           f"attn_a{d}_bwd")
            dqa.append(dq_)
            dka.append(dk_)
            dva.append(dv_)
            dtable_a = dtable_a + _table_grad(db_.reshape(N_HEADS, blk, 3 * blk), bk, "table_grad_a")
        dqb, dkb, dvb, db_, dsink = _attn_bwd(sv["qb"], sv["kb"], sv["vb"], dyb, sv["lse_b"], delta_b, bias_b, sink,
                                              BLK_B, 1, "attn_b_bwd")
        dtable_b = dtable_b + _table_grad(db_.reshape(N_HEADS, BLK_B, 3 * BLK_B), buckets_b, "table_grad_b")
        gsmall["sink_b"][l] = dsink.reshape(N_HEADS, BLK_B).sum(axis=1)

        dproj, pqa, pka, pqb, pkb = _qknorm_bwd(sv["proj"], gqa, gka, gqb, gkb, dqa, dka, dva, dqb, dkb, dvb, dgab)
        marks[l]["attn_bwd_done"] = dproj
        gsmall["qnorm_a_g"][l] = pqa.reshape(N_HEADS, HEAD_DIM).sum(0)
        gsmall["knorm_a_g"][l] = pka.reshape(N_HEADS, HEAD_DIM).sum(0)
        gsmall["qnorm_b_g"][l] = pqb.reshape(N_HEADS, HEAD_DIM).sum(0)
        gsmall["knorm_b_g"][l] = pkb.reshape(N_KV_B, HEAD_DIM).sum(0)
        gbig[l]["w_in"] = _mm(sv["h"], dproj, "tn", F32, "mm_d_in")
        dh = _mm(dproj, w["w_in"], "nt", F32, "mm_dh")
        dx, gsmall["norm_mix_g"][l] = _rms_bwd(sv["x0"], g_mix, dh, dx, "rms_mix_bwd")
        gsmall["norm_mix_g"][l] = gsmall["norm_mix_g"][l][0]
        gsmall["norm_ffn_g"][l] = gsmall["norm_ffn_g"][l][0]
        gsmall["norm_ple_g"][l] = gsmall["norm_ple_g"][l][0]

    gsmall = {n: jnp.stack(v) for n, v in gsmall.items()}
    gsmall["rel_table"] = jnp.concatenate([dtable_a, dtable_b], axis=0).T
    return loss, dx, gbig, gsmall, marks


def _place():
    return lax.axis_index("x"), lax.axis_index("y"), lax.axis_index("c")


def _flip(v, bit):
    return 1 - v if bit else v


CHIP_RELATIONS = ((0, 1), (1, 0), (1, 1))
ANY = pl.BlockSpec(memory_space=pl.ANY)


def _allgather_body(w_refs, out_refs, send_sems, recv_sems):
    x, y, c = _place()
    chips = [(_flip(x, a), _flip(y, b)) for a, b in CHIP_RELATIONS]

    def make(g):
        w_ref, out_ref = w_refs[g], out_refs[g]
        half = w_ref.shape[0] // 2

        def part(px, py, pc):
            return out_ref.at[2 * px + py, pl.ds(pc * half, half), :]

        def copy(k, block, to, src=None):
            return pltpu.make_async_remote_copy(
                src_ref=part(*block) if src is None else src, dst_ref=part(*block),
                send_sem=send_sems.at[7 * g + k], recv_sem=recv_sems.at[7 * g + k], device_id=to,
                device_id_type=MESH_ID)

        own = pltpu.make_async_remote_copy(
            src_ref=w_ref, dst_ref=out_ref.at[2 * x + y], send_sem=send_sems.at[7 * g + 6],
            recv_sem=recv_sems.at[7 * g + 6], device_id=(x, y, 1 - c), device_id_type=MESH_ID)
        first = [copy(k, (x, y, c), (*chip, c), src=w_ref.at[pl.ds(c * half, half), :]) for k, chip in enumerate(chips)]
        passed = [copy(3 + k, (*chip, c), (x, y, 1 - c)) for k, chip in enumerate(chips)]
        arrive = [copy(k, (*chip, c), (x, y, c)) for k, chip in enumerate(chips)]
        arrive2 = [copy(3 + k, (*chip, 1 - c), (x, y, c)) for k, chip in enumerate(chips)]
        return own, first, passed, arrive, arrive2

    made = [make(g) for g in range(len(w_refs))]
    for own, first, _, _, _ in made:
        own.start()
        for cp in first:
            cp.start()
    for _, _, passed, arrive, _ in made:
        for k in range(3):
            arrive[k].wait_recv()
            passed[k].start()
    for own, first, passed, _, arrive2 in made:
        for k in range(3):
            arrive2[k].wait_recv()
        own.wait_recv()
        for cp in first + passed + [own]:
            cp.wait_send()


def _sibling(x, y, c):
    return [(x, y, 1 - c)]


def _same_core_of_other_chips(x, y, c):
    return [(_flip(x, a), _flip(y, b), c) for a, b in CHIP_RELATIONS]


def _exchange(body, ins, out_types, n_sems, name, sequencer=None):
    n = len(ins)
    sems = (pltpu.SemaphoreType.DMA((n_sems,)), pltpu.SemaphoreType.DMA((n_sems,)))
    if sequencer is None:
        in_place = out_types is None
        out_shape = [jax.ShapeDtypeStruct(a.shape, a.dtype) for a in ins] if in_place else out_types

        def tc_body(*refs):
            body(refs[:n], refs[n:n + len(out_shape)], refs[-2], refs[-1])

        return list(pl.pallas_call(
            tc_body, out_shape=out_shape, in_specs=[ANY] * n, out_specs=[ANY] * len(out_shape),
            input_output_aliases={g: g for g in range(n)} if in_place else {}, scratch_shapes=list(sems), name=name)(*ins))

    collective_id, peers = sequencer
    hbm = pltpu.MemorySpace.HBM
    in_refs = [jax.new_ref(a, memory_space=hbm) for a in ins]
    out_refs = in_refs if out_types is None else [jax.empty_ref(t, memory_space=hbm) for t in out_types]

    @pl.kernel(mesh=plsc.ScalarSubcoreMesh(axis_name="sequencer", num_cores=1), name=name, scratch_types=sems,
               compiler_params=pltpu.CompilerParams(collective_id=collective_id))
    def launch(send_sems, recv_sems):
        barrier = pltpu.get_barrier_semaphore()
        devices = peers(*_place())
        for device in devices:
            pl.semaphore_signal(barrier, inc=1, device_id=device, device_id_type=MESH_ID)
        pl.semaphore_wait(barrier, len(devices))
        body(in_refs, out_refs, send_sems, recv_sems)

    launch()
    return [r[...] for r in out_refs]


def _allgather(shards, name, sequencer=None):
    out_types = [jax.ShapeDtypeStruct((N_CHIPS,) + s.shape, s.dtype) for s in shards]
    if sequencer is not None:
        sequencer = (sequencer, lambda x, y, c: _sibling(x, y, c) + _same_core_of_other_chips(x, y, c))
    return _exchange(_allgather_body, shards, out_types, 7 * len(shards), name, sequencer)


def _half_tile(half):
    return max(t for t in range(16, 1025, 16) if half % t == 0)


def _run_copies(cps):
    for cp in cps:
        cp.start()
    for cp in cps:
        cp.wait_recv()
    for cp in cps:
        cp.wait_send()


def _sibling_halves(gsends, name, sequencer=None):
    def body(g_refs, out_refs, send_sems, recv_sems):
        x, y, c = _place()
        cps = []
        for g, (g_ref, out_ref) in enumerate(zip(g_refs, out_refs)):
            half = g_ref.shape[1] // 2
            cps.append(pltpu.make_async_remote_copy(
                src_ref=g_ref.at[:, pl.ds((1 - c) * half, half), :], dst_ref=out_ref,
                send_sem=send_sems.at[g], recv_sem=recv_sems.at[g], device_id=(x, y, 1 - c), device_id_type=MESH_ID))
        _run_copies(cps)

    out_types = [jax.ShapeDtypeStruct((s.shape[0], s.shape[1] // 2, s.shape[2]), s.dtype) for s in gsends]
    return _exchange(body, gsends, out_types, len(gsends), name, sequencer and (sequencer, _sibling))


def _chip_sums(gsend, sib, place):
    n, rows, cols = gsend.shape
    half = rows // 2
    tm = _half_tile(half)
    nblk = half // tm

    def body(s_ref, g_ref, sib_ref, o_ref):
        o_ref[0] = (g_ref[0].astype(F32) + sib_ref[0].astype(F32)).astype(o_ref.dtype)

    grid_spec = pltpu.PrefetchScalarGridSpec(
        num_scalar_prefetch=1, grid=(n, nblk),
        in_specs=[pl.BlockSpec((1, tm, cols), lambda k, i, s: (jnp.bitwise_xor(s[0], k), s[1] * nblk + i, 0)),
                  pl.BlockSpec((1, tm, cols), lambda k, i, s: (jnp.bitwise_xor(s[0], k), i, 0))],
        out_specs=pl.BlockSpec((1, tm, cols), lambda k, i, s: (k, i, 0)))
    return pl.pallas_call(
        body, out_shape=jax.ShapeDtypeStruct((n, half, cols), BF16), grid_spec=grid_spec,
        name="rs_chip_sums", compiler_params=_params("parallel", "parallel"))(place, gsend, sib)


def _exchange_chip_sums(tsends, name, sequencer=None):
    def body(t_refs, out_refs, send_sems, recv_sems):
        x, y, c = _place()
        cps = []
        for g, (t_ref, out_ref) in enumerate(zip(t_refs, out_refs)):
            for k, device in enumerate(_same_core_of_other_chips(x, y, c)):
                cps.append(pltpu.make_async_remote_copy(
                    src_ref=t_ref.at[k + 1], dst_ref=out_ref.at[k], send_sem=send_sems.at[3 * g + k],
                    recv_sem=recv_sems.at[3 * g + k], device_id=device, device_id_type=MESH_ID))
        _run_copies(cps)

    out_types = [jax.ShapeDtypeStruct((3,) + s.shape[1:], s.dtype) for s in tsends]
    return _exchange(body, tsends, out_types, 3 * len(tsends), name,
                     sequencer and (sequencer, _same_core_of_other_chips))


def _final_sum(tsend, recv, place):
    n, half, cols = tsend.shape
    tm = _half_tile(half)
    nblk = half // tm

    def body(s_ref, t_ref, r_ref, o_ref):
        o_ref[...] = ((t_ref[0].astype(F32) + r_ref[0].astype(F32)) + r_ref[1].astype(F32)) + r_ref[2].astype(F32)

    grid_spec = pltpu.PrefetchScalarGridSpec(
        num_scalar_prefetch=1, grid=(nblk,),
        in_specs=[pl.BlockSpec((1, tm, cols), lambda i, s: (0, i, 0)), pl.BlockSpec((n - 1, tm, cols), lambda i, s: (0, i, 0))],
        out_specs=pl.BlockSpec((tm, cols), lambda i, s: (s[1] * nblk + i, 0)))
    return pl.pallas_call(
        body, out_shape=jax.ShapeDtypeStruct((2 * half, cols), F32), grid_spec=grid_spec, name="rs_final_sum",
        compiler_params=_params("parallel"))(place, tsend, recv)


def _join_halves(gfulls, name, sequencer=None):
    def body(g_refs, out_refs, send_sems, recv_sems):
        x, y, c = _place()
        n = len(g_refs)

        def copy(g, pc):
            half = g_refs[g].shape[0] // 2
            return pltpu.make_async_remote_copy(
                src_ref=g_refs[g].at[pl.ds(pc * half, half), :], dst_ref=out_refs[g].at[pl.ds(pc * half, half), :],
                send_sem=send_sems.at[g], recv_sem=recv_sems.at[g], device_id=(x, y, 1 - c), device_id_type=MESH_ID)

        mine = [copy(g, c) for g in range(n)]
        for cp in mine:
            cp.start()
        for g in range(n):
            copy(g, 1 - c).wait_recv()
        for cp in mine:
            cp.wait_send()

    return _exchange(body, gfulls, None, len(gfulls), name, sequencer and (sequencer, _sibling))


def _allreduce_small(v):
    rows, cols = v.shape

    def body(v_ref, out_ref, buf, send_sems, recv_sems):
        x, y, c = _place()
        cps = []
        for k in range(1, 8):
            peer = (_flip(x, (k >> 2) & 1), _flip(y, (k >> 1) & 1), _flip(c, k & 1))
            cps.append(pltpu.make_async_remote_copy(
                src_ref=v_ref, dst_ref=buf.at[k - 1], send_sem=send_sems.at[k - 1], recv_sem=recv_sems.at[k - 1],
                device_id=peer, device_id_type=MESH_ID))
        for cp in cps:
            cp.start()
        for cp in cps:
            cp.wait_recv()
        for cp in cps:
            cp.wait_send()
        t0 = v_ref[...] + buf[0]
        t1 = buf[1] + buf[2]
        t2 = buf[3] + buf[4]
        t3 = buf[5] + buf[6]
        out_ref[...] = (t0 + t1) + (t2 + t3)

    vm = pl.BlockSpec(memory_space=pltpu.VMEM)
    return pl.pallas_call(
        body, out_shape=jax.ShapeDtypeStruct((rows, cols), F32), in_specs=[vm], out_specs=vm,
        scratch_shapes=[pltpu.VMEM((7, rows, cols), F32), pltpu.SemaphoreType.DMA((7,)), pltpu.SemaphoreType.DMA((7,))],
        name="allreduce_small")(v)


BIG_INFO = {n: (shape, ax) for n, shape, ax in BIG}
GROUPS = (("w_in",), ("w_ffn_gate", "w_ffn_up"), ("w_out", "w_ffn_down", "w_ple_gate"),
          ("w_branch_a", "w_branch_b", "w_ple_proj"))


def _shard_shape(name):
    (k, m), ax = BIG_INFO[name]
    return (k // N_CHIPS, m) if ax == 0 else (k, m // N_CHIPS)


def _group_rows(group):
    offs, off = {}, 0
    for n in group:
        offs[n] = off
        off += _shard_shape(n)[0]
    return offs, off


def _pack_groups(shards, layer, dtype):
    return [jnp.concatenate([shards[n][layer].astype(dtype) for n in group], axis=0) for group in GROUPS]


def _unpack_full(gathered, groups):
    out = {}
    for group, arr in zip(groups, gathered):
        offs, _ = _group_rows(group)
        for n in group:
            rows, cols = _shard_shape(n)
            (k, m), ax = BIG_INFO[n]
            slab = arr[:, offs[n]:offs[n] + rows]
            out[n] = slab.reshape(k, m) if ax == 0 else jnp.transpose(slab, (1, 0, 2)).reshape(k, m)
    return out


def _pack_grads(gfull):
    out = []
    for group in GROUPS:
        parts = []
        for n in group:
            rows, cols = _shard_shape(n)
            ax = BIG_INFO[n][1]
            slab = (gfull[n].reshape(N_CHIPS, rows, cols) if ax == 0
                    else jnp.transpose(gfull[n].reshape(rows, N_CHIPS, cols), (1, 0, 2)))
            parts.append(slab.astype(BF16))
        out.append(jnp.concatenate(parts, axis=1))
    return out


def _after(values, mark):
    values, _ = lax.optimization_barrier((values, mark))
    return values


def _reduce_scatter(gsends, place, tag, sequencer_ids=None, hold=None):
    ids = sequencer_ids or (None, None, None)
    sibs = _sibling_halves(gsends, "rs_sibling_halves_" + tag, ids[0])
    tsends = [_chip_sums(g, s, place) for g, s in zip(gsends, sibs)]
    recvs = _exchange_chip_sums(tsends, "rs_exchange_" + tag, ids[1])
    if hold is not None:
        recvs = _after(recvs, hold)
    return _join_halves([_final_sum(t, r, place) for t, r in zip(tsends, recvs)], "rs_join_halves_" + tag, ids[2])


SMALL_SHAPES = {"rel_table": (NUM_BUCKETS, 2 * N_HEADS), "norm_mix_g": (DEPTH, D_MODEL), "qnorm_a_g": (DEPTH, HEAD_DIM),
                "knorm_a_g": (DEPTH, HEAD_DIM), "qnorm_b_g": (DEPTH, HEAD_DIM), "knorm_b_g": (DEPTH, HEAD_DIM),
                "sink_b": (DEPTH, N_HEADS), "norm_ffn_g": (DEPTH, D_MODEL), "norm_ple_g": (DEPTH, D_MODEL)}


def _pack_small(vals):
    flat = jnp.concatenate([vals[n].astype(F32).reshape(-1) for n in SMALL])
    flat = jnp.concatenate([flat, jnp.zeros((SMALL_ROWS * LANES - flat.shape[0],), F32)])
    return flat.reshape(SMALL_ROWS, LANES)


def _unpack_small(packed):
    flat, out, off = packed.reshape(-1), {}, 0
    for n in SMALL:
        size = math.prod(SMALL_SHAPES[n])
        out[n] = flat[off:off + size].reshape(SMALL_SHAPES[n])
        off += size
    return out


def _adamw(w, gs, g_row, m, v, name):
    c1 = 1.0 - ADAM_B1 ** ADAM_STEP
    c2 = 1.0 - ADAM_B2 ** ADAM_STEP
    total, width = w.shape
    n_layers = len(gs)
    per = total // n_layers
    tm = max(t for t in (512, 256, 128, 64, 32, 16, 8) if per % t == 0 and g_row % t == 0)
    nblk = per // tm

    def body(*refs):
        w_ref, g_refs = refs[0], refs[1:1 + n_layers]
        m_ref, v_ref, og, od, om, ov = refs[1 + n_layers:]
        layer = pl.program_id(0) // nblk
        g = g_refs[0][...]
        for l in range(1, n_layers):
            g = jnp.where(layer == l, g_refs[l][...], g)
        m_new = ADAM_B1 * m_ref[...] + (1.0 - ADAM_B1) * g
        v_new = ADAM_B2 * v_ref[...] + (1.0 - ADAM_B2) * (g * g)
        og[...] = g
        od[...] = -ADAM_LR * ((m_new / c1) / (jnp.sqrt(v_new / c2) + ADAM_EPS) + ADAM_WD * w_ref[...])
        om[...] = m_new
        ov[...] = v_new

    row = pl.BlockSpec((tm, width), lambda i: (i, 0))
    g_specs = [pl.BlockSpec((tm, width), lambda i, l=l: (g_row // tm + jnp.clip(i - l * nblk, 0, nblk - 1), 0))
               for l in range(n_layers)]
    return pl.pallas_call(
        body, out_shape=[jax.ShapeDtypeStruct((total, width), F32)] * 4, grid=(total // tm,),
        in_specs=[row] + g_specs + [row, row], out_specs=[row] * 4, name=name,
        compiler_params=_params("parallel"))(w, *gs, m, v)


def kernel(x, p, rel_table, norm_mix_g, w_in, qnorm_a_g, knorm_a_g, qnorm_b_g, knorm_b_g, sink_b, w_branch_a, w_branch_b, w_out, norm_ffn_g, w_ffn_gate, w_ffn_up, w_ffn_down, norm_ple_g, w_ple_gate, w_ple_proj, loss_target, m_rel_table, m_norm_mix_g, m_w_in, m_qnorm_a_g, m_knorm_a_g, m_qnorm_b_g, m_knorm_b_g, m_sink_b, m_w_branch_a, m_w_branch_b, m_w_out, m_norm_ffn_g, m_w_ffn_gate, m_w_ffn_up, m_w_ffn_down, m_norm_ple_g, m_w_ple_gate, m_w_ple_proj, v_rel_table, v_norm_mix_g, v_w_in, v_qnorm_a_g, v_knorm_a_g, v_qnorm_b_g, v_knorm_b_g, v_sink_b, v_w_branch_a, v_w_branch_b, v_w_out, v_norm_ffn_g, v_w_ffn_gate, v_w_ffn_up, v_w_ffn_down, v_norm_ple_g, v_w_ple_gate, v_w_ple_proj):
    given = dict(locals())
    weights = {n: given[n] for n in WEIGHTS}
    moments_m = {n: given["m_" + n] for n in WEIGHTS}
    moments_v = {n: given["v_" + n] for n in WEIGHTS}
    xi, yi, ci = _place()
    place = jnp.stack([2 * xi + yi, ci]).astype(jnp.int32)

    shards = [_pack_groups(weights, l, BF16) for l in range(DEPTH)]
    w_in0 = _allgather(shards[0][:1], "allgather_w_in_layer0")
    rest0 = _allgather(_after(shards[0][1:], w_in0), "allgather_rest_layer0", sequencer=1)
    gathered1 = _allgather(_after(shards[1], rest0), "allgather_layer1", sequencer=2)
    gathered = [(w_in0, rest0), (gathered1[:1], gathered1[1:])]
    small = {n: weights[n] for n in SMALL}

    def w_in_of(l, mark):
        return _unpack_full(gathered[l][0] if l == 0 else _after(gathered[l][0], mark), GROUPS[:1])["w_in"]

    def rest_of(l, mark):
        return _unpack_full(_after(gathered[l][1], mark), GROUPS[1:])

    loss, dx, gbig, gsmall, marks = _local_step(x[0], p[:, 0], loss_target[0], w_in_of, rest_of, small)

    gsends = [_pack_grads(gbig[l]) for l in range(DEPTH)]
    red1 = _reduce_scatter(gsends[1], place, "layer1", (3, 4, 5), hold=marks[0]["ffn_bwd_done"])
    rest0 = _reduce_scatter(gsends[0][1:], place, "rest_layer0", (6, 7, 8), hold=marks[0]["attn_bwd_done"])
    greds = [_reduce_scatter(gsends[0][:1], place, "w_in_layer0") + rest0, red1]

    grads, delta, new_m, new_v = {}, {}, {}, {}
    for gi, group in enumerate(GROUPS):
        offs, _ = _group_rows(group)
        for n in group:
            shape = weights[n].shape
            two_d = lambda a: a.reshape(shape[0] * shape[1], shape[2])
            outs = _adamw(two_d(weights[n]), [greds[l][gi] for l in range(DEPTH)], offs[n], two_d(moments_m[n]),
                          two_d(moments_v[n]), "adamw_" + n)
            grads[n], delta[n], new_m[n], new_v[n] = (o.reshape(shape) for o in outs)
    small_grads = _allreduce_small(_pack_small(gsmall))
    g_, d_, m_, v_ = _adamw(_pack_small(weights), [small_grads], 0, _pack_small(moments_m), _pack_small(moments_v),
                            "adamw_small")
    grads.update(_unpack_small(g_))
    delta.update(_unpack_small(d_))
    new_m.update(_unpack_small(m_))
    new_v.update(_unpack_small(v_))

    loss = lax.psum(loss, ("x", "y", "c"))
    return (loss, dx[None], *[grads[n] for n in WEIGHTS], *[delta[n] for n in WEIGHTS],
            *[new_m[n] for n in WEIGHTS], *[new_v[n] for n in WEIGHTS])
```

```python
import functools
import math

import jax
import jax.numpy as jnp
from jax import lax
from jax.experimental import pallas as pl
from jax.experimental.pallas import tpu as pltpu
from jax.experimental.pallas import tpu_sc as plsc

F32 = jnp.float32
BF16 = jnp.bfloat16
MESH_ID = pl.DeviceIdType.MESH

SEQ = 2048
D_MODEL = 1024
DEPTH = 2
HEAD_DIM = 64
N_HEADS = 8
WIDTH = N_HEADS * HEAD_DIM
N_PAIRS = 4
ITEMS = 4
N_KV_B = 2
PLE_DIM = 256
D_FF = 2816
D_IN = 4352
OFF_QA, OFF_KA, OFF_VA, OFF_QB, OFF_KB, OFF_VB, OFF_GA, OFF_GB = 0, 512, 1024, 1536, 2048, 2176, 2304, 3328
DILATED = ((64, 1), (64, 4), (64, 16))
BLK_B = 128
NUM_BUCKETS = 32
MAX_DISTANCE = 1024
RMS_EPS = 1e-6
NEG_INF = -1e30
LANES = 128
VMEM_LIMIT = 48 * 1024 * 1024

ADAM_LR, ADAM_B1, ADAM_B2, ADAM_EPS, ADAM_WD, ADAM_STEP = 0.001, 0.9, 0.999, 1e-08, 0.01, 10

BIG = (
    ("w_in", (D_MODEL, D_IN), 1),
    ("w_branch_a", (WIDTH, D_MODEL), 1),
    ("w_branch_b", (WIDTH, D_MODEL), 1),
    ("w_out", (D_MODEL, D_MODEL), 0),
    ("w_ffn_gate", (D_MODEL, D_FF), 1),
    ("w_ffn_up", (D_MODEL, D_FF), 1),
    ("w_ffn_down", (D_FF, D_MODEL), 0),
    ("w_ple_gate", (D_MODEL, D_MODEL), 0),
    ("w_ple_proj", (PLE_DIM, D_MODEL), 1),
)
SMALL = ("rel_table", "norm_mix_g", "qnorm_a_g", "knorm_a_g", "qnorm_b_g", "knorm_b_g", "sink_b",
         "norm_ffn_g", "norm_ple_g")
WEIGHTS = ("rel_table", "norm_mix_g", "w_in", "qnorm_a_g", "knorm_a_g", "qnorm_b_g", "knorm_b_g", "sink_b",
           "w_branch_a", "w_branch_b", "w_out", "norm_ffn_g", "w_ffn_gate", "w_ffn_up", "w_ffn_down",
           "norm_ple_g", "w_ple_gate", "w_ple_proj")
N_CHIPS = 4
SMALL_ROWS = 64


def _params(*sem):
    return pltpu.CompilerParams(dimension_semantics=sem, vmem_limit_bytes=VMEM_LIMIT)


def _pick(dim, target):
    for t in (target, 512, 256, 128, 64, 32, 16, 8):
        if t <= target and dim % t == 0:
            return t
    return dim


def _mm(a, b, mode, out_dtype, name, res=None, tm=512, tn=512):
    if mode == "nn":
        (m, k), (_, n) = a.shape, b.shape
    elif mode == "nt":
        (m, k), (n, _) = a.shape, b.shape
    else:
        (k, m), (_, n) = a.shape, b.shape
    tm, tn = _pick(m, tm), _pick(n, tn)
    a_spec = pl.BlockSpec((k, tm), lambda i, j: (0, i)) if mode == "tn" else pl.BlockSpec((tm, k), lambda i, j: (i, 0))
    b_spec = pl.BlockSpec((tn, k), lambda i, j: (j, 0)) if mode == "nt" else pl.BlockSpec((k, tn), lambda i, j: (0, j))
    dims = {"nn": (((1,), (0,)), ((), ())), "nt": (((1,), (1,)), ((), ())), "tn": (((0,), (0,)), ((), ()))}[mode]
    has_res = res is not None

    def body(*refs):
        a_ref, b_ref = refs[0], refs[1]
        o_ref = refs[-1]
        acc = lax.dot_general(a_ref[...].astype(BF16), b_ref[...].astype(BF16), dims, preferred_element_type=F32)
        if has_res:
            acc = acc + refs[2][...]
        o_ref[...] = acc.astype(out_dtype)

    in_specs = [a_spec, b_spec]
    args = [a, b]
    if has_res:
        in_specs.append(pl.BlockSpec((tm, tn), lambda i, j: (i, j)))
        args.append(res)
    return pl.pallas_call(
        body, out_shape=jax.ShapeDtypeStruct((m, n), out_dtype), grid=(m // tm, n // tn), in_specs=in_specs,
        out_specs=pl.BlockSpec((tm, tn), lambda i, j: (i, j)), name=name,
        compiler_params=_params("parallel", "parallel"))(*args)


def _ew(fn, ins, out_dtypes, *, width, bw, name, vecs=(), tm=256):
    rows = ins[0][0].shape[0]
    tm = _pick(rows, tm)
    n_in = len(ins) + len(vecs)

    def col_map(off_blocks):
        return lambda i, j: (i, off_blocks + j)

    in_specs = [pl.BlockSpec((tm, bw), col_map(off // bw)) for _, off in ins]
    in_specs += [pl.BlockSpec((1, bw), lambda i, j: (0, j)) for _ in vecs]

    def body(*refs):
        outs = fn(*[r[...] for r in refs[:n_in]])
        for r, o in zip(refs[n_in:], outs):
            r[...] = o.astype(r.dtype)

    return pl.pallas_call(
        body, out_shape=[jax.ShapeDtypeStruct((rows, width), dt) for dt in out_dtypes],
        grid=(rows // tm, width // bw), in_specs=in_specs,
        out_specs=[pl.BlockSpec((tm, bw), lambda i, j: (i, j)) for _ in out_dtypes],
        name=name, compiler_params=_params("parallel", "parallel"))(*[a for a, _ in ins], *vecs)


def _sigmoid(x):
    return 1.0 / (1.0 + jnp.exp(-x))


def _seg_sum(v):
    outs = []
    for k in range(v.shape[1] // LANES):
        vp = v[:, k * LANES:(k + 1) * LANES]
        left = lax.broadcasted_iota(jnp.int32, vp.shape, 1) < HEAD_DIM
        sl = jnp.sum(jnp.where(left, vp, 0.0), axis=-1, keepdims=True)
        sr = jnp.sum(jnp.where(left, 0.0, vp), axis=-1, keepdims=True)
        outs.append(jnp.where(left, sl, sr))
    return outs[0] if len(outs) == 1 else jnp.concatenate(outs, axis=1)


def _seg_rstd(x):
    return lax.rsqrt(_seg_sum(x * x) * (1.0 / HEAD_DIM) + RMS_EPS)


def _rms_fwd(x, g, name):
    rows, d = x.shape
    tm = 256

    def body(x_ref, g_ref, h_ref):
        xv = x_ref[...]
        r = lax.rsqrt(jnp.mean(xv * xv, axis=-1, keepdims=True) + RMS_EPS)
        h_ref[...] = ((xv * r) * g_ref[...]).astype(BF16)

    return pl.pallas_call(
        body, out_shape=jax.ShapeDtypeStruct((rows, d), BF16), grid=(rows // tm,),
        in_specs=[pl.BlockSpec((tm, d), lambda i: (i, 0)), pl.BlockSpec((1, d), lambda i: (0, 0))],
        out_specs=pl.BlockSpec((tm, d), lambda i: (i, 0)), name=name, compiler_params=_params("parallel"))(x, g)


def _rms_bwd(x, g, dh, dres, name):
    rows, d = x.shape
    tm = 256

    def body(x_ref, g_ref, dh_ref, dres_ref, dx_ref, dg_ref):
        xv = x_ref[...]
        r = lax.rsqrt(jnp.mean(xv * xv, axis=-1, keepdims=True) + RMS_EPS)
        xh = xv * r
        dhv = dh_ref[...]
        dxh = dhv * g_ref[...]
        dx_ref[...] = dres_ref[...] + r * (dxh - xh * jnp.mean(dxh * xh, axis=-1, keepdims=True))
        part = jnp.sum(dhv * xh, axis=0, keepdims=True)

        @pl.when(pl.program_id(0) == 0)
        def _():
            dg_ref[...] = part

        @pl.when(pl.program_id(0) > 0)
        def _():
            dg_ref[...] += part

    row = pl.BlockSpec((tm, d), lambda i: (i, 0))
    vec = pl.BlockSpec((1, d), lambda i: (0, 0))
    return pl.pallas_call(
        body, out_shape=[jax.ShapeDtypeStruct((rows, d), F32), jax.ShapeDtypeStruct((1, d), F32)],
        grid=(rows // tm,), in_specs=[row, vec, row, row], out_specs=[row, vec],
        name=name, compiler_params=_params("arbitrary"))(x, g, dh, dres)


def _loss_grad(y, t):
    rows, d = y.shape
    tm = 256

    def body(y_ref, t_ref, dy_ref, l_ref):
        e = y_ref[...] - t_ref[...]
        dy_ref[...] = e * (1.0 / d)
        part = jnp.zeros((1, LANES), F32) + jnp.sum(e * e) * (0.5 / d)

        @pl.when(pl.program_id(0) == 0)
        def _():
            l_ref[...] = part

        @pl.when(pl.program_id(0) > 0)
        def _():
            l_ref[...] += part

    row = pl.BlockSpec((tm, d), lambda i: (i, 0))
    return pl.pallas_call(
        body, out_shape=[jax.ShapeDtypeStruct((rows, d), F32), jax.ShapeDtypeStruct((1, LANES), F32)],
        grid=(rows // tm,), in_specs=[row, row], out_specs=[row, pl.BlockSpec((1, LANES), lambda i: (0, 0))],
        name="loss_grad", compiler_params=_params("arbitrary"))(y, t)


def _put_pairs(ref, val):
    for hp in range(N_PAIRS):
        ref[hp] = val[:, hp * LANES:(hp + 1) * LANES].astype(ref.dtype)


def _get_pairs(ref):
    return jnp.concatenate([ref[hp] for hp in range(N_PAIRS)], axis=1)


def _swap_halves(v):
    return pltpu.roll(v, HEAD_DIM, axis=1)


def _expand_kv(kv):
    left = lax.broadcasted_iota(jnp.int32, kv.shape, 1) < HEAD_DIM
    sw = _swap_halves(kv)
    h0 = jnp.where(left, kv, sw)
    h1 = jnp.where(left, sw, kv)
    return jnp.concatenate([h0, h0, h1, h1], axis=1)


def _reduce_kv(dkv):
    left = lax.broadcasted_iota(jnp.int32, (dkv.shape[0], LANES), 1) < HEAD_DIM
    t = dkv[:, 0:LANES] + dkv[:, LANES:2 * LANES]
    u = dkv[:, 2 * LANES:3 * LANES] + dkv[:, 3 * LANES:4 * LANES]
    t = t + _swap_halves(t)
    u = u + _swap_halves(u)
    return jnp.where(left, t, u)


def _qknorm_fwd(proj, gqa, gka, gqb, gkb):
    rows = proj.shape[0]
    tm = 256

    def body(qa_ref, ka_ref, va_ref, qb_ref, kb_ref, vb_ref, gqa_ref, gka_ref, gqb_ref, gkb_ref,
             oqa, oka, ova, oqb, okb, ovb):
        for src, g_ref, dst in ((qa_ref, gqa_ref, oqa), (ka_ref, gka_ref, oka), (qb_ref, gqb_ref, oqb)):
            xv = src[...]
            _put_pairs(dst, (xv * _seg_rstd(xv)) * g_ref[...])
        _put_pairs(ova, va_ref[...])
        kv = kb_ref[...]
        _put_pairs(okb, _expand_kv((kv * _seg_rstd(kv)) * gkb_ref[...]))
        _put_pairs(ovb, _expand_kv(vb_ref[...]))

    def win(width, off):
        return pl.BlockSpec((tm, width), lambda i: (i, off // width))

    vec = lambda w: pl.BlockSpec((1, w), lambda i: (0, 0))
    out = pl.BlockSpec((N_PAIRS, tm, LANES), lambda i: (0, i, 0))
    return pl.pallas_call(
        body, out_shape=[jax.ShapeDtypeStruct((N_PAIRS, rows, LANES), F32)] * 6, grid=(rows // tm,),
        in_specs=[win(WIDTH, OFF_QA), win(WIDTH, OFF_KA), win(WIDTH, OFF_VA), win(WIDTH, OFF_QB),
                  win(LANES, OFF_KB), win(LANES, OFF_VB), vec(WIDTH), vec(WIDTH), vec(WIDTH), vec(LANES)],
        out_specs=[out] * 6, name="qknorm_fwd", compiler_params=_params("parallel"))(
            proj, proj, proj, proj, proj, proj, gqa, gka, gqb, gkb)


def _norm_bwd(xv, g, dy):
    r = _seg_rstd(xv)
    xh = xv * r
    dxh = dy * g
    dx = r * (dxh - xh * (_seg_sum(dxh * xh) * (1.0 / HEAD_DIM)))
    return dx, jnp.sum(dy * xh, axis=0, keepdims=True)


def _qknorm_bwd(proj, gqa, gka, gqb, gkb, dqa, dka, dva, dqb, dkb, dvb, dgab):
    rows = proj.shape[0]
    tm = 256
    n_a = len(dqa)

    def body(*refs):
        qa_ref, ka_ref, qb_ref, kb_ref, gqa_ref, gka_ref, gqb_ref, gkb_ref = refs[:8]
        pos = 8
        dqa_refs, dka_refs, dva_refs = refs[pos:pos + n_a], refs[pos + n_a:pos + 2 * n_a], refs[pos + 2 * n_a:pos + 3 * n_a]
        pos += 3 * n_a
        dqb_ref, dkb_ref, dvb_ref, dgab_ref = refs[pos:pos + 4]
        dproj_ref, ogqa, ogka, ogqb, ogkb = refs[pos + 4:]

        def total(rs):
            acc = _get_pairs(rs[0])
            for r in rs[1:]:
                acc = acc + _get_pairs(r)
            return acc

        dx_qa, p_qa = _norm_bwd(qa_ref[...], gqa_ref[...], total(dqa_refs))
        dx_ka, p_ka = _norm_bwd(ka_ref[...], gka_ref[...], total(dka_refs))
        dx_qb, p_qb = _norm_bwd(qb_ref[...], gqb_ref[...], _get_pairs(dqb_ref))
        dx_kb, p_kb = _norm_bwd(kb_ref[...], gkb_ref[...], _reduce_kv(_get_pairs(dkb_ref)))
        dproj_ref[:, OFF_QA:OFF_QA + WIDTH] = dx_qa.astype(BF16)
        dproj_ref[:, OFF_KA:OFF_KA + WIDTH] = dx_ka.astype(BF16)
        dproj_ref[:, OFF_VA:OFF_VA + WIDTH] = total(dva_refs).astype(BF16)
        dproj_ref[:, OFF_QB:OFF_QB + WIDTH] = dx_qb.astype(BF16)
        dproj_ref[:, OFF_KB:OFF_KB + LANES] = dx_kb.astype(BF16)
        dproj_ref[:, OFF_VB:OFF_VB + LANES] = _reduce_kv(_get_pairs(dvb_ref)).astype(BF16)
        dproj_ref[:, OFF_GA:D_IN] = dgab_ref[...]
        first = pl.program_id(0) == 0
        for o_ref, part in ((ogqa, p_qa), (ogka, p_ka), (ogqb, p_qb), (ogkb, p_kb)):
            @pl.when(first)
            def _(o_ref=o_ref, part=part):
                o_ref[...] = part

            @pl.when(jnp.logical_not(first))
            def _(o_ref=o_ref, part=part):
                o_ref[...] += part

    def win(width, off):
        return pl.BlockSpec((tm, width), lambda i: (i, off // width))

    vec = lambda w: pl.BlockSpec((1, w), lambda i: (0, 0))
    row = lambda w: pl.BlockSpec((tm, w), lambda i: (i, 0))
    in_specs = [win(WIDTH, OFF_QA), win(WIDTH, OFF_KA), win(WIDTH, OFF_QB), win(LANES, OFF_KB),
                vec(WIDTH), vec(WIDTH), vec(WIDTH), vec(LANES)]
    in_specs += [pl.BlockSpec((N_PAIRS, tm, LANES), lambda i: (0, i, 0))] * (3 * n_a + 3) + [row(2 * D_MODEL)]
    return pl.pallas_call(
        body,
        out_shape=[jax.ShapeDtypeStruct((rows, D_IN), BF16), jax.ShapeDtypeStruct((1, WIDTH), F32),
                   jax.ShapeDtypeStruct((1, WIDTH), F32), jax.ShapeDtypeStruct((1, WIDTH), F32),
                   jax.ShapeDtypeStruct((1, LANES), F32)],
        grid=(rows // tm,), in_specs=in_specs,
        out_specs=[row(D_IN), vec(WIDTH), vec(WIDTH), vec(WIDTH), vec(LANES)],
        name="qknorm_bwd", compiler_params=_params("arbitrary"))(
            proj, proj, proj, proj, gqa, gka, gqb, gkb, *dqa, *dka, *dva, dqb, dkb, dvb, dgab)


def _t5_bucket(rel):
    half_b = NUM_BUCKETS // 2
    max_exact = half_b // 2
    sign = jnp.where(rel > 0, half_b, 0)
    n = jnp.abs(rel)
    nf = jnp.maximum(n, 1).astype(F32)
    large = max_exact + (jnp.log(nf / max_exact) / math.log(MAX_DISTANCE / max_exact)
                         * (half_b - max_exact)).astype(jnp.int32)
    large = jnp.minimum(large, half_b - 1)
    return sign + jnp.where(n < max_exact, n, large)


def _band_buckets(blk, dilation):
    i = jnp.arange(blk, dtype=jnp.int32)[:, None]
    j = jnp.arange(3 * blk, dtype=jnp.int32)[None, :]
    rel = j - blk - i
    return jnp.where(jnp.abs(rel) <= blk, _t5_bucket(rel * dilation), -1)


def _bias_tiles(table, buckets, head_off, name):
    blk = buckets.shape[0]

    def body(tab_ref, bk_ref, o_ref):
        h = pl.program_id(0) + head_off
        bk = bk_ref[...]
        acc = jnp.full(bk.shape, NEG_INF, F32)
        for b in range(NUM_BUCKETS):
            acc = jnp.where(bk == b, tab_ref[b, h], acc)
        o_ref[0] = acc

    return pl.pallas_call(
        body, out_shape=jax.ShapeDtypeStruct((N_HEADS, blk, 3 * blk), F32), grid=(N_HEADS,),
        in_specs=[pl.BlockSpec(memory_space=pltpu.SMEM), pl.BlockSpec((blk, 3 * blk), lambda h: (0, 0))],
        out_specs=pl.BlockSpec((1, blk, 3 * blk), lambda h: (h, 0, 0)),
        name=name, compiler_params=_params("parallel"))(table, buckets)


def _table_grad(dbias, buckets, name):
    blk = buckets.shape[0]

    def body(db_ref, bk_ref, o_ref):
        bk = bk_ref[...]
        dbv = db_ref[0]
        lane = lax.broadcasted_iota(jnp.int32, (1, LANES), 1)
        acc = jnp.zeros((1, LANES), F32)
        for b in range(NUM_BUCKETS):
            acc = jnp.where(lane == b, jnp.sum(jnp.where(bk == b, dbv, 0.0)), acc)
        o_ref[0] = acc

    out = pl.pallas_call(
        body, out_shape=jax.ShapeDtypeStruct((N_HEADS, 1, LANES), F32), grid=(N_HEADS,),
        in_specs=[pl.BlockSpec((1, blk, 3 * blk), lambda h: (h, 0, 0)), pl.BlockSpec((blk, 3 * blk), lambda h: (0, 0))],
        out_specs=pl.BlockSpec((1, 1, LANES), lambda h: (h, 0, 0)),
        name=name, compiler_params=_params("parallel"))(dbias, buckets)
    return out[:, 0, :NUM_BUCKETS]


def _dot_nt(a, b):
    return lax.dot_general(a, b, (((1,), (1,)), ((), ())), preferred_element_type=F32)


def _stack_pair(x2, left):
    return jnp.concatenate([jnp.where(left, x2, 0.0), jnp.where(left, 0.0, x2)], axis=0).astype(BF16)


def _attn_geometry(blk, d):
    chunk = blk * ITEMS if d == 1 else blk * d
    groups = 1 if d == 1 else d // ITEMS
    halo = blk if d == 1 else chunk
    return chunk, groups, halo


def _item_rows(ref, j, r0, blk, d):
    if d == 1:
        return ref[j * blk:(j + 1) * blk, :]
    return ref[pl.ds(r0 + j, blk, stride=d), :]


def _item_penalty(t, nct, j, blk, d):
    first_ok, last_ok = t > 0, t < nct - 1
    if d == 1:
        first_ok = True if j > 0 else first_ok
        last_ok = True if j < ITEMS - 1 else last_ok
    col = lax.broadcasted_iota(jnp.int32, (1, 3 * blk), 1)
    ok = jnp.logical_and(jnp.logical_or(col >= blk, first_ok), jnp.logical_or(col < 2 * blk, last_ok))
    return jnp.where(ok, 0.0, NEG_INF).astype(F32)


def _attn_specs(seq, blk, d, step_of):
    chunk, _, halo = _attn_geometry(blk, d)
    per, last = chunk // halo, seq // halo - 1
    cur = pl.BlockSpec((None, chunk, LANES), lambda hp, t: (hp, step_of(t), 0))
    prev = pl.BlockSpec((None, halo, LANES), lambda hp, t: (hp, jnp.clip(step_of(t) * per - 1, 0, last), 0))
    nxt = pl.BlockSpec((None, halo, LANES), lambda hp, t: (hp, jnp.minimum((step_of(t) + 1) * per, last), 0))
    return cur, prev, nxt


def _attn_fwd(q, k, v, bias, sink, blk, d, name):
    _, seq, _ = q.shape
    chunk, groups, _ = _attn_geometry(blk, d)
    nct = seq // chunk
    has_sink = sink is not None
    scale = HEAD_DIM ** -0.5

    def body(*refs):
        q_ref, kp, kc, kn, vp, vc, vn, b_ref = refs[:8]
        s_ref = refs[8] if has_sink else None
        o_ref, l_ref = refs[-2], refs[-1]
        t = pl.program_id(1)
        left = lax.broadcasted_iota(jnp.int32, (1, LANES), 1) < HEAD_DIM
        bias2 = b_ref[...]
        if d == 1:
            kwin = jnp.concatenate([kp[...], kc[...], kn[...]], axis=0).astype(BF16)
            vwin = jnp.concatenate([vp[...], vc[...], vn[...]], axis=0).astype(BF16)

        def group(r0):
            scores, vcats = [], []
            for j in range(ITEMS):
                qs = _stack_pair(_item_rows(q_ref, j, r0, blk, d) * scale, left)
                if d == 1:
                    kcat, vcat = kwin[j * blk:(j + 3) * blk], vwin[j * blk:(j + 3) * blk]
                else:
                    kcat = jnp.concatenate([_item_rows(r, j, r0, blk, d) for r in (kp, kc, kn)], axis=0).astype(BF16)
                    vcat = jnp.concatenate([_item_rows(r, j, r0, blk, d) for r in (vp, vc, vn)], axis=0).astype(BF16)
                scores.append(_dot_nt(qs, kcat) + bias2 + _item_penalty(t, nct, j, blk, d))
                vcats.append(vcat)
            s = jnp.concatenate(scores, axis=0)
            m = jnp.max(s, axis=-1, keepdims=True)
            if has_sink:
                sk = jnp.concatenate([s_ref[...]] * ITEMS, axis=0)
                m = jnp.maximum(m, sk)
            p = jnp.exp(s - m)
            den = jnp.sum(p, axis=-1, keepdims=True)
            if has_sink:
                den = den + jnp.exp(sk - m)
            pn = (p * (1.0 / den)).astype(BF16)
            lse = m + jnp.log(den)
            for j in range(ITEMS):
                top, mid, bot = 2 * j * blk, (2 * j + 1) * blk, (2 * j + 2) * blk
                o2 = jnp.dot(pn[top:bot], vcats[j], preferred_element_type=F32)
                o_val = jnp.where(left, o2[:blk], o2[blk:])
                l_val = jnp.where(left, lse[top:mid], lse[mid:bot])
                if d == 1:
                    o_ref[j * blk:(j + 1) * blk, :] = o_val
                    l_ref[j * blk:(j + 1) * blk, :] = l_val
                else:
                    o_ref[pl.ds(r0 + j, blk, stride=d), :] = o_val
                    l_ref[pl.ds(r0 + j, blk, stride=d), :] = l_val

        if groups == 1:
            group(0)
        else:
            def step(g, carry):
                group(g * ITEMS)
                return carry

            lax.fori_loop(0, groups, step, 0)

    cur, prev, nxt = _attn_specs(seq, blk, d, lambda t: t)
    in_specs = [cur, prev, cur, nxt, prev, cur, nxt, pl.BlockSpec((2 * blk, 3 * blk), lambda hp, t: (hp, 0))]
    args = [q, k, k, k, v, v, v, bias]
    if has_sink:
        in_specs.append(pl.BlockSpec((2 * blk, 1), lambda hp, t: (hp, 0)))
        args.append(sink)
    return pl.pallas_call(
        body, out_shape=[jax.ShapeDtypeStruct(q.shape, F32)] * 2, grid=(N_PAIRS, nct),
        in_specs=in_specs, out_specs=[cur, cur], name=name, compiler_params=_params("parallel", "parallel"))(*args)


def _attn_bwd(q, k, v, do, lse, delta, bias, sink, blk, d, name):
    _, seq, _ = q.shape
    chunk, groups, halo = _attn_geometry(blk, d)
    nct = seq // chunk
    has_sink = sink is not None
    n_in = 12 if has_sink else 11
    scale = HEAD_DIM ** -0.5

    def body(*refs):
        q_ref, kp, kc, kn, vp, vc, vn, do_ref, l_ref, d_ref, b_ref = refs[:11]
        s_ref = refs[11] if has_sink else None
        dq_ref, dk_ref, dv_ref, db_ref = refs[n_in:n_in + 4]
        ds_ref = refs[n_in + 4] if has_sink else None
        wk, wv = refs[-2], refs[-1]
        t = pl.program_id(1)

        @pl.when(t == 0)
        def _():
            wk[...] = jnp.zeros_like(wk)
            wv[...] = jnp.zeros_like(wv)
            db_ref[...] = jnp.zeros_like(db_ref)
            if has_sink:
                ds_ref[...] = jnp.zeros_like(ds_ref)

        @pl.when(t > 0)
        def _():
            for w in (wk, wv):
                keep = w[chunk:2 * chunk + halo]
                w[0:chunk + halo] = keep
                w[chunk + halo:2 * chunk + halo] = jnp.zeros((chunk, LANES), F32)

        @pl.when(t < nct)
        def _():
            lane = lax.broadcasted_iota(jnp.int32, (1, LANES), 1)
            left = lane < HEAD_DIM
            bias2 = b_ref[...]
            if d == 1:
                kwin = jnp.concatenate([kp[...], kc[...], kn[...]], axis=0).astype(BF16)
                vwin = jnp.concatenate([vp[...], vc[...], vn[...]], axis=0).astype(BF16)

            def group(r0):
                qss, doss, kcats, scores, dps, lcols, dcols = [], [], [], [], [], [], []
                for j in range(ITEMS):
                    qs = _stack_pair(_item_rows(q_ref, j, r0, blk, d) * scale, left)
                    dos = _stack_pair(_item_rows(do_ref, j, r0, blk, d), left)
                    if d == 1:
                        kcat, vcat = kwin[j * blk:(j + 3) * blk], vwin[j * blk:(j + 3) * blk]
                    else:
                        kcat = jnp.concatenate([_item_rows(r, j, r0, blk, d) for r in (kp, kc, kn)], axis=0).astype(BF16)
                        vcat = jnp.concatenate([_item_rows(r, j, r0, blk, d) for r in (vp, vc, vn)], axis=0).astype(BF16)
                    l2, d2 = _item_rows(l_ref, j, r0, blk, d), _item_rows(d_ref, j, r0, blk, d)
                    lcols.append(jnp.max(jnp.where(left, l2, NEG_INF), axis=-1, keepdims=True))
                    lcols.append(jnp.max(jnp.where(left, NEG_INF, l2), axis=-1, keepdims=True))
                    dcols.append(jnp.sum(jnp.where(lane == 0, d2, 0.0), axis=-1, keepdims=True))
                    dcols.append(jnp.sum(jnp.where(lane == HEAD_DIM, d2, 0.0), axis=-1, keepdims=True))
                    scores.append(_dot_nt(qs, kcat) + bias2 + _item_penalty(t, nct, j, blk, d))
                    dps.append(_dot_nt(dos, vcat))
                    qss.append(qs)
                    doss.append(dos)
                    kcats.append(kcat)
                lcol = jnp.concatenate(lcols, axis=0)
                dcol = jnp.concatenate(dcols, axis=0)
                p = jnp.exp(jnp.concatenate(scores, axis=0) - lcol)
                ds = p * (jnp.concatenate(dps, axis=0) - dcol)
                if has_sink:
                    sgrad = dcol * jnp.exp(jnp.concatenate([s_ref[...]] * ITEMS, axis=0) - lcol)
                for j in range(ITEMS):
                    top, bot = 2 * j * blk, (2 * j + 2) * blk
                    dsj, pj = ds[top:bot], p[top:bot]
                    db_ref[...] += dsj
                    if has_sink:
                        ds_ref[...] -= sgrad[top:bot]
                    dq2 = jnp.dot(dsj.astype(BF16), kcats[j], preferred_element_type=F32) * scale
                    dq_val = jnp.where(left, dq2[:blk], dq2[blk:])
                    if d == 1:
                        dq_ref[j * blk:(j + 1) * blk, :] = dq_val
                    else:
                        dq_ref[pl.ds(r0 + j, blk, stride=d), :] = dq_val
                    dk_new = jnp.dot(jnp.transpose(dsj).astype(BF16), qss[j], preferred_element_type=F32)
                    dv_new = jnp.dot(jnp.transpose(pj).astype(BF16), doss[j], preferred_element_type=F32)
                    for w, new in ((wk, dk_new), (wv, dv_new)):
                        if d == 1:
                            w[chunk + (j - 1) * blk:chunk + (j + 2) * blk, :] += new
                        else:
                            for c in range(3):
                                w[pl.ds(c * chunk + r0 + j, blk, stride=d), :] += new[c * blk:(c + 1) * blk]

            if groups == 1:
                group(0)
            else:
                def step(g, carry):
                    group(g * ITEMS)
                    return carry

                lax.fori_loop(0, groups, step, 0)

        dk_ref[...] = wk[0:chunk]
        dv_ref[...] = wv[0:chunk]

    cur, prev, nxt = _attn_specs(seq, blk, d, lambda t: jnp.minimum(t, nct - 1))
    lag = pl.BlockSpec((None, chunk, LANES), lambda hp, t: (hp, jnp.maximum(t - 1, 0), 0))
    band = pl.BlockSpec((2 * blk, 3 * blk), lambda hp, t: (hp, 0))
    col = pl.BlockSpec((2 * blk, 1), lambda hp, t: (hp, 0))
    in_specs = [cur, prev, cur, nxt, prev, cur, nxt, cur, cur, cur, band]
    args = [q, k, k, k, v, v, v, do, lse, delta, bias]
    out_shape = [jax.ShapeDtypeStruct(q.shape, F32)] * 3 + [jax.ShapeDtypeStruct((N_HEADS * blk, 3 * blk), F32)]
    out_specs = [cur, lag, lag, band]
    if has_sink:
        in_specs.append(col)
        args.append(sink)
        out_shape.append(jax.ShapeDtypeStruct((N_HEADS * blk, 1), F32))
        out_specs.append(col)
    window = pltpu.VMEM((2 * chunk + halo, LANES), F32)
    return pl.pallas_call(
        body, out_shape=out_shape, grid=(N_PAIRS, nct + 1), in_specs=in_specs, out_specs=out_specs,
        scratch_shapes=[window, window], name=name, compiler_params=_params("arbitrary", "arbitrary"))(*args)


def _combine_patterns(outs, lses):
    _, rows, _ = outs[0].shape
    tm = 256
    n = len(outs)

    def body(*refs):
        o_refs, l_refs = refs[:n], refs[n:2 * n]
        y_ref, lse_ref = refs[2 * n], refs[2 * n + 1]
        for hp in range(N_PAIRS):
            ls = [r[hp] for r in l_refs]
            m = functools.reduce(jnp.maximum, ls)
            es = [jnp.exp(l - m) for l in ls]
            den = functools.reduce(lambda a, b: a + b, es)
            num = functools.reduce(lambda a, b: a + b, [e * r[hp] for e, r in zip(es, o_refs)])
            y_ref[:, hp * LANES:(hp + 1) * LANES] = num / den
            lse_ref[hp] = m + jnp.log(den)

    pm = pl.BlockSpec((N_PAIRS, tm, LANES), lambda i: (0, i, 0))
    return pl.pallas_call(
        body, out_shape=[jax.ShapeDtypeStruct((rows, WIDTH), F32), jax.ShapeDtypeStruct((N_PAIRS, rows, LANES), F32)],
        grid=(rows // tm,), in_specs=[pm] * (2 * n), out_specs=[pl.BlockSpec((tm, WIDTH), lambda i: (i, 0)), pm],
        name="combine_a", compiler_params=_params("parallel"))(*outs, *lses)


def _pairs_to_tokens(a):
    _, rows, _ = a.shape
    tm = 256

    def body(a_ref, o_ref):
        o_ref[...] = _get_pairs(a_ref)

    return pl.pallas_call(
        body, out_shape=jax.ShapeDtypeStruct((rows, WIDTH), a.dtype), grid=(rows // tm,),
        in_specs=[pl.BlockSpec((N_PAIRS, tm, LANES), lambda i: (0, i, 0))],
        out_specs=pl.BlockSpec((tm, WIDTH), lambda i: (i, 0)), name="pairs_to_tokens",
        compiler_params=_params("parallel"))(a)


def _attn_bwd_prep(dy, y, name):
    rows = dy.shape[0]
    tm = 256

    def body(dy_ref, y_ref, do_ref, dl_ref):
        dyv = dy_ref[...]
        _put_pairs(do_ref, dyv)
        _put_pairs(dl_ref, _seg_sum(dyv * y_ref[...]))

    tok = pl.BlockSpec((tm, WIDTH), lambda i: (i, 0))
    pm = pl.BlockSpec((N_PAIRS, tm, LANES), lambda i: (0, i, 0))
    return pl.pallas_call(
        body, out_shape=[jax.ShapeDtypeStruct((N_PAIRS, rows, LANES), F32)] * 2, grid=(rows // tm,),
        in_specs=[tok, tok], out_specs=[pm, pm], name=name, compiler_params=_params("parallel"))(dy, y)


def _tile_gain(g, reps):
    return jnp.tile(g[None, :], (1, reps))


def _local_step(x, p, target, w_in_of, rest_of, small):
    rel_table = small["rel_table"]
    buckets_a = [_band_buckets(blk, d) for blk, d in DILATED]
    buckets_b = _band_buckets(BLK_B, 1)
    bias_a = [_bias_tiles(rel_table, bk, 0, "bias_a").reshape(N_HEADS * bk.shape[0], -1) for bk in buckets_a]
    bias_b = _bias_tiles(rel_table, buckets_b, N_HEADS, "bias_b").reshape(N_HEADS * BLK_B, -1)

    saved = []
    for l in range(DEPTH):
        g_mix, g_ffn, g_ple = (small[n][l][None, :] for n in ("norm_mix_g", "norm_ffn_g", "norm_ple_g"))
        gqa, gka, gqb = (_tile_gain(small[n][l], N_HEADS) for n in ("qnorm_a_g", "knorm_a_g", "qnorm_b_g"))
        gkb = _tile_gain(small["knorm_b_g"][l], N_KV_B)
        sink = jnp.repeat(small["sink_b"][l], BLK_B)[:, None]

        h = _rms_fwd(x, g_mix, "rms_mix")
        w_in = w_in_of(l, h)
        proj = _mm(h, w_in, "nn", F32, "mm_in")
        qa, ka, va, qb, kb, vb = _qknorm_fwd(proj, gqa, gka, gqb, gkb)
        outs, lses = [], []
        for (blk, d), bias in zip(DILATED, bias_a):
            o, ls = _attn_fwd(qa, ka, va, bias, None, blk, d, f"attn_a{d}_fwd")
            outs.append(o)
            lses.append(ls)
        ya, lse_a = _combine_patterns(outs, lses)
        yb, lse_b = _attn_fwd(qb, kb, vb, bias_b, sink, BLK_B, 1, "attn_b_fwd")
        yb = _pairs_to_tokens(yb)
        w = dict(rest_of(l, yb), w_in=w_in)
        ca = _mm(ya, w["w_branch_a"], "nn", F32, "mm_branch_a")
        cb = _mm(yb, w["w_branch_b"], "nn", F32, "mm_branch_b")

        def gate(ca_, cb_, ga_, gb_):
            return (_sigmoid(ga_) * ca_ + _sigmoid(gb_) * cb_,)

        (merged,) = _ew(gate, [(ca, 0), (cb, 0), (proj, OFF_GA), (proj, OFF_GB)], [BF16],
                        width=D_MODEL, bw=256, name="gate")
        x1 = _mm(merged, w["w_out"], "nn", F32, "mm_out", res=x)

        h2 = _rms_fwd(x1, g_ffn, "rms_ffn")
        a = _mm(h2, w["w_ffn_gate"], "nn", F32, "mm_ffn_gate")
        u = _mm(h2, w["w_ffn_up"], "nn", F32, "mm_ffn_up")

        def swiglu(a_, u_):
            return ((a_ * _sigmoid(a_)) * u_,)

        (hid,) = _ew(swiglu, [(a, 0), (u, 0)], [BF16], width=D_FF, bw=D_FF, name="swiglu")
        x2 = _mm(hid, w["w_ffn_down"], "nn", F32, "mm_ffn_down", res=x1)

        h3 = _rms_fwd(x2, g_ple, "rms_ple")
        z = _mm(h3, w["w_ple_gate"], "nn", F32, "mm_ple_gate")
        e = _mm(p[l], w["w_ple_proj"], "nn", F32, "mm_ple_proj")

        def ple(x2_, z_, e_):
            return (x2_ + _sigmoid(z_) * e_,)

        (x3,) = _ew(ple, [(x2, 0), (z, 0), (e, 0)], [F32], width=D_MODEL, bw=D_MODEL, name="ple")
        saved.append(dict(w=w, x0=x, h=h, proj=proj, qa=qa, ka=ka, va=va, qb=qb, kb=kb, vb=vb, ya=ya, lse_a=lse_a,
                          yb=yb, lse_b=lse_b, ca=ca, cb=cb, merged=merged, x1=x1, h2=h2, a=a, u=u, hid=hid,
                          x2=x2, h3=h3, z=z, e=e))
        x = x3

    dx, loss_acc = _loss_grad(x, target)
    loss = loss_acc[0, 0]

    gbig = [{} for _ in range(DEPTH)]
    marks = [{} for _ in range(DEPTH)]
    gsmall = {n: [None] * DEPTH for n in SMALL if n != "rel_table"}
    dtable_a = jnp.zeros((N_HEADS, NUM_BUCKETS), F32)
    dtable_b = jnp.zeros((N_HEADS, NUM_BUCKETS), F32)

    for l in reversed(range(DEPTH)):
        sv = saved[l]
        w = sv["w"]
        g_mix, g_ffn, g_ple = (small[n][l][None, :] for n in ("norm_mix_g", "norm_ffn_g", "norm_ple_g"))
        gqa, gka, gqb = (_tile_gain(small[n][l], N_HEADS) for n in ("qnorm_a_g", "knorm_a_g", "qnorm_b_g"))
        gkb = _tile_gain(small["knorm_b_g"][l], N_KV_B)
        sink = jnp.repeat(small["sink_b"][l], BLK_B)[:, None]

        def ple_bwd(dx_, z_, e_):
            s = _sigmoid(z_)
            return dx_ * s, dx_ * e_ * (s * (1.0 - s))

        de, dz = _ew(ple_bwd, [(dx, 0), (sv["z"], 0), (sv["e"], 0)], [BF16, BF16], width=D_MODEL, bw=D_MODEL,
                     name="ple_bwd")
        gbig[l]["w_ple_proj"] = _mm(p[l], de, "tn", F32, "mm_d_ple_proj")
        gbig[l]["w_ple_gate"] = _mm(sv["h3"], dz, "tn", F32, "mm_d_ple_gate")
        dh3 = _mm(dz, w["w_ple_gate"], "nt", F32, "mm_dh3")
        dx, gsmall["norm_ple_g"][l] = _rms_bwd(sv["x2"], g_ple, dh3, dx, "rms_ple_bwd")

        dhid = _mm(dx, w["w_ffn_down"], "nt", F32, "mm_dhid")
        gbig[l]["w_ffn_down"] = _mm(sv["hid"], dx, "tn", F32, "mm_d_ffn_down")

        def swiglu_bwd(a_, u_, dh_):
            s = _sigmoid(a_)
            return dh_ * u_ * (s * (1.0 + a_ * (1.0 - s))), dh_ * (a_ * s)

        da, du = _ew(swiglu_bwd, [(sv["a"], 0), (sv["u"], 0), (dhid, 0)], [BF16, BF16], width=D_FF, bw=D_FF,
                     name="swiglu_bwd")
        gbig[l]["w_ffn_gate"] = _mm(sv["h2"], da, "tn", F32, "mm_d_ffn_gate")
        gbig[l]["w_ffn_up"] = _mm(sv["h2"], du, "tn", F32, "mm_d_ffn_up")
        dh2 = _mm(da, w["w_ffn_gate"], "nt", F32, "mm_dh2_gate")
        dh2 = _mm(du, w["w_ffn_up"], "nt", F32, "mm_dh2_up", res=dh2)
        dx, gsmall["norm_ffn_g"][l] = _rms_bwd(sv["x1"], g_ffn, dh2, dx, "rms_ffn_bwd")

        dmerged = _mm(dx, w["w_out"], "nt", F32, "mm_dmerged")
        marks[l]["ffn_bwd_done"] = dmerged
        gbig[l]["w_out"] = _mm(sv["merged"], dx, "tn", F32, "mm_d_out")

        def gate_bwd(dm_, ca_, cb_, ga_, gb_):
            sa, sb = _sigmoid(ga_), _sigmoid(gb_)
            return dm_ * sa, dm_ * sb, dm_ * ca_ * (sa * (1.0 - sa)), dm_ * cb_ * (sb * (1.0 - sb))

        dca, dcb, dga, dgb = _ew(gate_bwd, [(dmerged, 0), (sv["ca"], 0), (sv["cb"], 0), (sv["proj"], OFF_GA),
                                            (sv["proj"], OFF_GB)], [BF16, BF16, BF16, BF16],
                                 width=D_MODEL, bw=256, name="gate_bwd")
        dgab = jnp.concatenate([dga, dgb], axis=1)
        gbig[l]["w_branch_a"] = _mm(sv["ya"], dca, "tn", F32, "mm_d_branch_a")
        gbig[l]["w_branch_b"] = _mm(sv["yb"], dcb, "tn", F32, "mm_d_branch_b")
        dya = _mm(dca, w["w_branch_a"], "nt", F32, "mm_dya")
        dyb = _mm(dcb, w["w_branch_b"], "nt", F32, "mm_dyb")

        dya, delta_a = _attn_bwd_prep(dya, sv["ya"], "attn_a_bwd_prep")
        dyb, delta_b = _attn_bwd_prep(dyb, sv["yb"], "attn_b_bwd_prep")

        dqa, dka, dva = [], [], []
        for (blk, d), bias, bk in zip(DILATED, bias_a, buckets_a):
            dq_, dk_, dv_, db_ = _attn_bwd(sv["qa"], sv["ka"], sv["va"], dya, sv["lse_a"], delta_a, bias, None, blk, d,
                                           f"attn_a{d}_bwd")
            dqa.append(dq_)
            dka.append(dk_)
            dva.append(dv_)
            dtable_a = dtable_a + _table_grad(db_.reshape(N_HEADS, blk, 3 * blk), bk, "table_grad_a")
        dqb, dkb, dvb, db_, dsink = _attn_bwd(sv["qb"], sv["kb"], sv["vb"], dyb, sv["lse_b"], delta_b, bias_b, sink,
                                              BLK_B, 1, "attn_b_bwd")
        dtable_b = dtable_b + _table_grad(db_.reshape(N_HEADS, BLK_B, 3 * BLK_B), buckets_b, "table_grad_b")
        gsmall["sink_b"][l] = dsink.reshape(N_HEADS, BLK_B).sum(axis=1)

        dproj, pqa, pka, pqb, pkb = _qknorm_bwd(sv["proj"], gqa, gka, gqb, gkb, dqa, dka, dva, dqb, dkb, dvb, dgab)
        marks[l]["attn_bwd_done"] = dproj
        gsmall["qnorm_a_g"][l] = pqa.reshape(N_HEADS, HEAD_DIM).sum(0)
        gsmall["knorm_a_g"][l] = pka.reshape(N_HEADS, HEAD_DIM).sum(0)
        gsmall["qnorm_b_g"][l] = pqb.reshape(N_HEADS, HEAD_DIM).sum(0)
        gsmall["knorm_b_g"][l] = pkb.reshape(N_KV_B, HEAD_DIM).sum(0)
        gbig[l]["w_in"] = _mm(sv["h"], dproj, "tn", F32, "mm_d_in")
        dh = _mm(dproj, w["w_in"], "nt", F32, "mm_dh")
        dx, gsmall["norm_mix_g"][l] = _rms_bwd(sv["x0"], g_mix, dh, dx, "rms_mix_bwd")
        gsmall["norm_mix_g"][l] = gsmall["norm_mix_g"][l][0]
        gsmall["norm_ffn_g"][l] = gsmall["norm_ffn_g"][l][0]
        gsmall["norm_ple_g"][l] = gsmall["norm_ple_g"][l][0]

    gsmall = {n: jnp.stack(v) for n, v in gsmall.items()}
    gsmall["rel_table"] = jnp.concatenate([dtable_a, dtable_b], axis=0).T
    return loss, dx, gbig, gsmall, marks


def _place():
    return lax.axis_index("x"), lax.axis_index("y"), lax.axis_index("c")


def _flip(v, bit):
    return 1 - v if bit else v


CHIP_RELATIONS = ((0, 1), (1, 0), (1, 1))
ANY = pl.BlockSpec(memory_space=pl.ANY)


def _allgather_body(w_refs, out_refs, send_sems, recv_sems):
    x, y, c = _place()
    chips = [(_flip(x, a), _flip(y, b)) for a, b in CHIP_RELATIONS]

    def make(g):
        w_ref, out_ref = w_refs[g], out_refs[g]
        half = w_ref.shape[0] // 2

        def part(px, py, pc):
            return out_ref.at[2 * px + py, pl.ds(pc * half, half), :]

        def copy(k, block, to, src=None):
            return pltpu.make_async_remote_copy(
                src_ref=part(*block) if src is None else src, dst_ref=part(*block),
                send_sem=send_sems.at[7 * g + k], recv_sem=recv_sems.at[7 * g + k], device_id=to,
                device_id_type=MESH_ID)

        own = pltpu.make_async_remote_copy(
            src_ref=w_ref, dst_ref=out_ref.at[2 * x + y], send_sem=send_sems.at[7 * g + 6],
            recv_sem=recv_sems.at[7 * g + 6], device_id=(x, y, 1 - c), device_id_type=MESH_ID)
        first = [copy(k, (x, y, c), (*chip, c), src=w_ref.at[pl.ds(c * half, half), :]) for k, chip in enumerate(chips)]
        passed = [copy(3 + k, (*chip, c), (x, y, 1 - c)) for k, chip in enumerate(chips)]
        arrive = [copy(k, (*chip, c), (x, y, c)) for k, chip in enumerate(chips)]
        arrive2 = [copy(3 + k, (*chip, 1 - c), (x, y, c)) for k, chip in enumerate(chips)]
        return own, first, passed, arrive, arrive2

    made = [make(g) for g in range(len(w_refs))]
    for own, first, _, _, _ in made:
        own.start()
        for cp in first:
            cp.start()
    for _, _, passed, arrive, _ in made:
        for k in range(3):
            arrive[k].wait_recv()
            passed[k].start()
    for own, first, passed, _, arrive2 in made:
        for k in range(3):
            arrive2[k].wait_recv()
        own.wait_recv()
        for cp in first + passed + [own]:
            cp.wait_send()


def _sibling(x, y, c):
    return [(x, y, 1 - c)]


def _same_core_of_other_chips(x, y, c):
    return [(_flip(x, a), _flip(y, b), c) for a, b in CHIP_RELATIONS]


def _exchange(body, ins, out_types, n_sems, name, sequencer=None):
    n = len(ins)
    sems = (pltpu.SemaphoreType.DMA((n_sems,)), pltpu.SemaphoreType.DMA((n_sems,)))
    if sequencer is None:
        in_place = out_types is None
        out_shape = [jax.ShapeDtypeStruct(a.shape, a.dtype) for a in ins] if in_place else out_types

        def tc_body(*refs):
            body(refs[:n], refs[n:n + len(out_shape)], refs[-2], refs[-1])

        return list(pl.pallas_call(
            tc_body, out_shape=out_shape, in_specs=[ANY] * n, out_specs=[ANY] * len(out_shape),
            input_output_aliases={g: g for g in range(n)} if in_place else {}, scratch_shapes=list(sems), name=name)(*ins))

    collective_id, peers = sequencer
    hbm = pltpu.MemorySpace.HBM
    in_refs = [jax.new_ref(a, memory_space=hbm) for a in ins]
    out_refs = in_refs if out_types is None else [jax.empty_ref(t, memory_space=hbm) for t in out_types]

    @pl.kernel(mesh=plsc.ScalarSubcoreMesh(axis_name="sequencer", num_cores=1), name=name, scratch_types=sems,
               compiler_params=pltpu.CompilerParams(collective_id=collective_id))
    def launch(send_sems, recv_sems):
        barrier = pltpu.get_barrier_semaphore()
        devices = peers(*_place())
        for device in devices:
            pl.semaphore_signal(barrier, inc=1, device_id=device, device_id_type=MESH_ID)
        pl.semaphore_wait(barrier, len(devices))
        body(in_refs, out_refs, send_sems, recv_sems)

    launch()
    return [r[...] for r in out_refs]


def _allgather(shards, name, sequencer=None):
    out_types = [jax.ShapeDtypeStruct((N_CHIPS,) + s.shape, s.dtype) for s in shards]
    if sequencer is not None:
        sequencer = (sequencer, lambda x, y, c: _sibling(x, y, c) + _same_core_of_other_chips(x, y, c))
    return _exchange(_allgather_body, shards, out_types, 7 * len(shards), name, sequencer)


def _half_tile(half):
    return max(t for t in range(16, 1025, 16) if half % t == 0)


def _run_copies(cps):
    for cp in cps:
        cp.start()
    for cp in cps:
        cp.wait_recv()
    for cp in cps:
        cp.wait_send()


def _sibling_halves(gsends, name, sequencer=None):
    def body(g_refs, out_refs, send_sems, recv_sems):
        x, y, c = _place()
        cps = []
        for g, (g_ref, out_ref) in enumerate(zip(g_refs, out_refs)):
            half = g_ref.shape[1] // 2
            cps.append(pltpu.make_async_remote_copy(
                src_ref=g_ref.at[:, pl.ds((1 - c) * half, half), :], dst_ref=out_ref,
                send_sem=send_sems.at[g], recv_sem=recv_sems.at[g], device_id=(x, y, 1 - c), device_id_type=MESH_ID))
        _run_copies(cps)

    out_types = [jax.ShapeDtypeStruct((s.shape[0], s.shape[1] // 2, s.shape[2]), s.dtype) for s in gsends]
    return _exchange(body, gsends, out_types, len(gsends), name, sequencer and (sequencer, _sibling))


def _chip_sums(gsend, sib, place):
    n, rows, cols = gsend.shape
    half = rows // 2
    tm = _half_tile(half)
    nblk = half // tm

    def body(s_ref, g_ref, sib_ref, o_ref):
        o_ref[0] = (g_ref[0].astype(F32) + sib_ref[0].astype(F32)).astype(o_ref.dtype)

    grid_spec = pltpu.PrefetchScalarGridSpec(
        num_scalar_prefetch=1, grid=(n, nblk),
        in_specs=[pl.BlockSpec((1, tm, cols), lambda k, i, s: (jnp.bitwise_xor(s[0], k), s[1] * nblk + i, 0)),
                  pl.BlockSpec((1, tm, cols), lambda k, i, s: (jnp.bitwise_xor(s[0], k), i, 0))],
        out_specs=pl.BlockSpec((1, tm, cols), lambda k, i, s: (k, i, 0)))
    return pl.pallas_call(
        body, out_shape=jax.ShapeDtypeStruct((n, half, cols), BF16), grid_spec=grid_spec,
        name="rs_chip_sums", compiler_params=_params("parallel", "parallel"))(place, gsend, sib)


def _exchange_chip_sums(tsends, name, sequencer=None):
    def body(t_refs, out_refs, send_sems, recv_sems):
        x, y, c = _place()
        cps = []
        for g, (t_ref, out_ref) in enumerate(zip(t_refs, out_refs)):
            for k, device in enumerate(_same_core_of_other_chips(x, y, c)):
                cps.append(pltpu.make_async_remote_copy(
                    src_ref=t_ref.at[k + 1], dst_ref=out_ref.at[k], send_sem=send_sems.at[3 * g + k],
                    recv_sem=recv_sems.at[3 * g + k], device_id=device, device_id_type=MESH_ID))
        _run_copies(cps)

    out_types = [jax.ShapeDtypeStruct((3,) + s.shape[1:], s.dtype) for s in tsends]
    return _exchange(body, tsends, out_types, 3 * len(tsends), name,
                     sequencer and (sequencer, _same_core_of_other_chips))


def _final_sum(tsend, recv, place):
    n, half, cols = tsend.shape
    tm = _half_tile(half)
    nblk = half // tm

    def body(s_ref, t_ref, r_ref, o_ref):
        o_ref[...] = ((t_ref[0].astype(F32) + r_ref[0].astype(F32)) + r_ref[1].astype(F32)) + r_ref[2].astype(F32)

    grid_spec = pltpu.PrefetchScalarGridSpec(
        num_scalar_prefetch=1, grid=(nblk,),
        in_specs=[pl.BlockSpec((1, tm, cols), lambda i, s: (0, i, 0)), pl.BlockSpec((n - 1, tm, cols), lambda i, s: (0, i, 0))],
        out_specs=pl.BlockSpec((tm, cols), lambda i, s: (s[1] * nblk + i, 0)))
    return pl.pallas_call(
        body, out_shape=jax.ShapeDtypeStruct((2 * half, cols), F32), grid_spec=grid_spec, name="rs_final_sum",
        compiler_params=_params("parallel"))(place, tsend, recv)


def _join_halves(gfulls, name, sequencer=None):
    def body(g_refs, out_refs, send_sems, recv_sems):
        x, y, c = _place()
        n = len(g_refs)

        def copy(g, pc):
            half = g_refs[g].shape[0] // 2
            return pltpu.make_async_remote_copy(
                src_ref=g_refs[g].at[pl.ds(pc * half, half), :], dst_ref=out_refs[g].at[pl.ds(pc * half, half), :],
                send_sem=send_sems.at[g], recv_sem=recv_sems.at[g], device_id=(x, y, 1 - c), device_id_type=MESH_ID)

        mine = [copy(g, c) for g in range(n)]
        for cp in mine:
            cp.start()
        for g in range(n):
            copy(g, 1 - c).wait_recv()
        for cp in mine:
            cp.wait_send()

    return _exchange(body, gfulls, None, len(gfulls), name, sequencer and (sequencer, _sibling))


def _allreduce_small(v):
    rows, cols = v.shape

    def body(v_ref, out_ref, buf, send_sems, recv_sems):
        x, y, c = _place()
        cps = []
        for k in range(1, 8):
            peer = (_flip(x, (k >> 2) & 1), _flip(y, (k >> 1) & 1), _flip(c, k & 1))
            cps.append(pltpu.make_async_remote_copy(
                src_ref=v_ref, dst_ref=buf.at[k - 1], send_sem=send_sems.at[k - 1], recv_sem=recv_sems.at[k - 1],
                device_id=peer, device_id_type=MESH_ID))
        for cp in cps:
            cp.start()
        for cp in cps:
            cp.wait_recv()
        for cp in cps:
            cp.wait_send()
        t0 = v_ref[...] + buf[0]
        t1 = buf[1] + buf[2]
        t2 = buf[3] + buf[4]
        t3 = buf[5] + buf[6]
        out_ref[...] = (t0 + t1) + (t2 + t3)

    vm = pl.BlockSpec(memory_space=pltpu.VMEM)
    return pl.pallas_call(
        body, out_shape=jax.ShapeDtypeStruct((rows, cols), F32), in_specs=[vm], out_specs=vm,
        scratch_shapes=[pltpu.VMEM((7, rows, cols), F32), pltpu.SemaphoreType.DMA((7,)), pltpu.SemaphoreType.DMA((7,))],
        name="allreduce_small")(v)


BIG_INFO = {n: (shape, ax) for n, shape, ax in BIG}
GROUPS = (("w_in",), ("w_ffn_gate", "w_ffn_up"), ("w_out", "w_ffn_down", "w_ple_gate"),
          ("w_branch_a", "w_branch_b", "w_ple_proj"))


def _shard_shape(name):
    (k, m), ax = BIG_INFO[name]
    return (k // N_CHIPS, m) if ax == 0 else (k, m // N_CHIPS)


def _group_rows(group):
    offs, off = {}, 0
    for n in group:
        offs[n] = off
        off += _shard_shape(n)[0]
    return offs, off


def _pack_groups(shards, layer, dtype):
    return [jnp.concatenate([shards[n][layer].astype(dtype) for n in group], axis=0) for group in GROUPS]


def _unpack_full(gathered, groups):
    out = {}
    for group, arr in zip(groups, gathered):
        offs, _ = _group_rows(group)
        for n in group:
            rows, cols = _shard_shape(n)
            (k, m), ax = BIG_INFO[n]
            slab = arr[:, offs[n]:offs[n] + rows]
            out[n] = slab.reshape(k, m) if ax == 0 else jnp.transpose(slab, (1, 0, 2)).reshape(k, m)
    return out


def _pack_grads(gfull):
    out = []
    for group in GROUPS:
        parts = []
        for n in group:
            rows, cols = _shard_shape(n)
            ax = BIG_INFO[n][1]
            slab = (gfull[n].reshape(N_CHIPS, rows, cols) if ax == 0
                    else jnp.transpose(gfull[n].reshape(rows, N_CHIPS, cols), (1, 0, 2)))
            parts.append(slab.astype(BF16))
        out.append(jnp.concatenate(parts, axis=1))
    return out


def _after(values, mark):
    values, _ = lax.optimization_barrier((values, mark))
    return values


def _reduce_scatter(gsends, place, tag, sequencer_ids=None, hold=None):
    ids = sequencer_ids or (None, None, None)
    sibs = _sibling_halves(gsends, "rs_sibling_halves_" + tag, ids[0])
    tsends = [_chip_sums(g, s, place) for g, s in zip(gsends, sibs)]
    recvs = _exchange_chip_sums(tsends, "rs_exchange_" + tag, ids[1])
    if hold is not None:
        recvs = _after(recvs, hold)
    return _join_halves([_final_sum(t, r, place) for t, r in zip(tsends, recvs)], "rs_join_halves_" + tag, ids[2])


SMALL_SHAPES = {"rel_table": (NUM_BUCKETS, 2 * N_HEADS), "norm_mix_g": (DEPTH, D_MODEL), "qnorm_a_g": (DEPTH, HEAD_DIM),
                "knorm_a_g": (DEPTH, HEAD_DIM), "qnorm_b_g": (DEPTH, HEAD_DIM), "knorm_b_g": (DEPTH, HEAD_DIM),
                "sink_b": (DEPTH, N_HEADS), "norm_ffn_g": (DEPTH, D_MODEL), "norm_ple_g": (DEPTH, D_MODEL)}


def _pack_small(vals):
    flat = jnp.concatenate([vals[n].astype(F32).reshape(-1) for n in SMALL])
    flat = jnp.concatenate([flat, jnp.zeros((SMALL_ROWS * LANES - flat.shape[0],), F32)])
    return flat.reshape(SMALL_ROWS, LANES)


def _unpack_small(packed):
    flat, out, off = packed.reshape(-1), {}, 0
    for n in SMALL:
        size = math.prod(SMALL_SHAPES[n])
        out[n] = flat[off:off + size].reshape(SMALL_SHAPES[n])
        off += size
    return out


def _adamw(w, gs, g_row, m, v, name):
    c1 = 1.0 - ADAM_B1 ** ADAM_STEP
    c2 = 1.0 - ADAM_B2 ** ADAM_STEP
    total, width = w.shape
    n_layers = len(gs)
    per = total // n_layers
    tm = max(t for t in (512, 256, 128, 64, 32, 16, 8) if per % t == 0 and g_row % t == 0)
    nblk = per // tm

    def body(*refs):
        w_ref, g_refs = refs[0], refs[1:1 + n_layers]
        m_ref, v_ref, og, od, om, ov = refs[1 + n_layers:]
        layer = pl.program_id(0) // nblk
        g = g_refs[0][...]
        for l in range(1, n_layers):
            g = jnp.where(layer == l, g_refs[l][...], g)
        m_new = ADAM_B1 * m_ref[...] + (1.0 - ADAM_B1) * g
        v_new = ADAM_B2 * v_ref[...] + (1.0 - ADAM_B2) * (g * g)
        og[...] = g
        od[...] = -ADAM_LR * ((m_new / c1) / (jnp.sqrt(v_new / c2) + ADAM_EPS) + ADAM_WD * w_ref[...])
        om[...] = m_new
        ov[...] = v_new

    row = pl.BlockSpec((tm, width), lambda i: (i, 0))
    g_specs = [pl.BlockSpec((tm, width), lambda i, l=l: (g_row // tm + jnp.clip(i - l * nblk, 0, nblk - 1), 0))
               for l in range(n_layers)]
    return pl.pallas_call(
        body, out_shape=[jax.ShapeDtypeStruct((total, width), F32)] * 4, grid=(total // tm,),
        in_specs=[row] + g_specs + [row, row], out_specs=[row] * 4, name=name,
        compiler_params=_params("parallel"))(w, *gs, m, v)


def kernel(x, p, rel_table, norm_mix_g, w_in, qnorm_a_g, knorm_a_g, qnorm_b_g, knorm_b_g, sink_b, w_branch_a, w_branch_b, w_out, norm_ffn_g, w_ffn_gate, w_ffn_up, w_ffn_down, norm_ple_g, w_ple_gate, w_ple_proj, loss_target, m_rel_table, m_norm_mix_g, m_w_in, m_qnorm_a_g, m_knorm_a_g, m_qnorm_b_g, m_knorm_b_g, m_sink_b, m_w_branch_a, m_w_branch_b, m_w_out, m_norm_ffn_g, m_w_ffn_gate, m_w_ffn_up, m_w_ffn_down, m_norm_ple_g, m_w_ple_gate, m_w_ple_proj, v_rel_table, v_norm_mix_g, v_w_in, v_qnorm_a_g, v_knorm_a_g, v_qnorm_b_g, v_knorm_b_g, v_sink_b, v_w_branch_a, v_w_branch_b, v_w_out, v_norm_ffn_g, v_w_ffn_gate, v_w_ffn_up, v_w_ffn_down, v_norm_ple_g, v_w_ple_gate, v_w_ple_proj):
    given = dict(locals())
    weights = {n: given[n] for n in WEIGHTS}
    moments_m = {n: given["m_" + n] for n in WEIGHTS}
    moments_v = {n: given["v_" + n] for n in WEIGHTS}
    xi, yi, ci = _place()
    place = jnp.stack([2 * xi + yi, ci]).astype(jnp.int32)

    shards = [_pack_groups(weights, l, BF16) for l in range(DEPTH)]
    w_in0 = _allgather(shards[0][:1], "allgather_w_in_layer0")
    rest0 = _allgather(_after(shards[0][1:], w_in0), "allgather_rest_layer0", sequencer=1)
    gathered = [w_in0 + rest0, None]
    small = {n: weights[n] for n in SMALL}

    def w_in_of(l, mark):
        return _unpack_full(gathered[l][:1] if l == 0 else _after(gathered[l][:1], mark), GROUPS[:1])["w_in"]

    def rest_of(l, mark):
        if l == 0:
            gathered[1] = _allgather(_after(shards[1], mark), "allgather_layer1", sequencer=2)
        return _unpack_full(_after(gathered[l][1:], mark), GROUPS[1:])

    loss, dx, gbig, gsmall, marks = _local_step(x[0], p[:, 0], loss_target[0], w_in_of, rest_of, small)

    gsends = [_pack_grads(gbig[l]) for l in range(DEPTH)]
    red1 = _reduce_scatter(gsends[1], place, "layer1", (3, 4, 5), hold=marks[0]["ffn_bwd_done"])
    rest0 = _reduce_scatter(gsends[0][1:], place, "rest_layer0", (6, 7, 8), hold=marks[0]["attn_bwd_done"])
    greds = [_reduce_scatter(gsends[0][:1], place, "w_in_layer0") + rest0, red1]

    grads, delta, new_m, new_v = {}, {}, {}, {}
    for gi, group in enumerate(GROUPS):
        offs, _ = _group_rows(group)
        for n in group:
            shape = weights[n].shape
            two_d = lambda a: a.reshape(shape[0] * shape[1], shape[2])
            outs = _adamw(two_d(weights[n]), [greds[l][gi] for l in range(DEPTH)], offs[n], two_d(moments_m[n]),
                          two_d(moments_v[n]), "adamw_" + n)
            grads[n], delta[n], new_m[n], new_v[n] = (o.reshape(shape) for o in outs)
    small_grads = _allreduce_small(_pack_small(gsmall))
    g_, d_, m_, v_ = _adamw(_pack_small(weights), [small_grads], 0, _pack_small(moments_m), _pack_small(moments_v),
                            "adamw_small")
    grads.update(_unpack_small(g_))
    delta.update(_unpack_small(d_))
    new_m.update(_unpack_small(m_))
    new_v.update(_unpack_small(v_))

    loss = lax.psum(loss, ("x", "y", "c"))
    return (loss, dx[None], *[grads[n] for n in WEIGHTS], *[delta[n] for n in WEIGHTS],
            *[new_m[n] for n in WEIGHTS], *[new_v[n] for n in WEIGHTS])
```

```python
import functools
import math

import jax
import jax.numpy as jnp
from jax import lax
from jax.experimental import pallas as pl
from jax.experimental.pallas import tpu as pltpu
from jax.experimental.pallas import tpu_sc as plsc

F32 = jnp.float32
BF16 = jnp.bfloat16
MESH_ID = pl.DeviceIdType.MESH

SEQ = 2048
D_MODEL = 1024
DEPTH = 2
HEAD_DIM = 64
N_HEADS = 8
WIDTH = N_HEADS * HEAD_DIM
N_PAIRS = 4
ITEMS = 4
N_KV_B = 2
PLE_DIM = 256
D_FF = 2816
D_IN = 4352
OFF_QA, OFF_KA, OFF_VA, OFF_QB, OFF_KB, OFF_VB, OFF_GA, OFF_GB = 0, 512, 1024, 1536, 2048, 2176, 2304, 3328
DILATED = ((64, 1), (64, 4), (64, 16))
BLK_B = 128
NUM_BUCKETS = 32
MAX_DISTANCE = 1024
RMS_EPS = 1e-6
NEG_INF = -1e30
LANES = 128
VMEM_LIMIT = 48 * 1024 * 1024

ADAM_LR, ADAM_B1, ADAM_B2, ADAM_EPS, ADAM_WD, ADAM_STEP = 0.001, 0.9, 0.999, 1e-08, 0.01, 10

BIG = (
    ("w_in", (D_MODEL, D_IN), 1),
    ("w_branch_a", (WIDTH, D_MODEL), 1),
    ("w_branch_b", (WIDTH, D_MODEL), 1),
    ("w_out", (D_MODEL, D_MODEL), 0),
    ("w_ffn_gate", (D_MODEL, D_FF), 1),
    ("w_ffn_up", (D_MODEL, D_FF), 1),
    ("w_ffn_down", (D_FF, D_MODEL), 0),
    ("w_ple_gate", (D_MODEL, D_MODEL), 0),
    ("w_ple_proj", (PLE_DIM, D_MODEL), 1),
)
SMALL = ("rel_table", "norm_mix_g", "qnorm_a_g", "knorm_a_g", "qnorm_b_g", "knorm_b_g", "sink_b",
         "norm_ffn_g", "norm_ple_g")
WEIGHTS = ("rel_table", "norm_mix_g", "w_in", "qnorm_a_g", "knorm_a_g", "qnorm_b_g", "knorm_b_g", "sink_b",
           "w_branch_a", "w_branch_b", "w_out", "norm_ffn_g", "w_ffn_gate", "w_ffn_up", "w_ffn_down",
           "norm_ple_g", "w_ple_gate", "w_ple_proj")
N_CHIPS = 4
SMALL_ROWS = 64


def _params(*sem):
    return pltpu.CompilerParams(dimension_semantics=sem, vmem_limit_bytes=VMEM_LIMIT)


def _pick(dim, target):
    for t in (target, 512, 256, 128, 64, 32, 16, 8):
        if t <= target and dim % t == 0:
            return t
    return dim


MM_VMEM_BUDGET = 40 * 1024 * 1024
STEP_OVERHEAD_S = 0.4e-6
TILE_DMA_BYTES_PER_S = 1.5e12
MXU_FLOPS_PER_S = 7e14


def _mm_tiles(m, n, k, a_bytes, b_bytes, out_bytes, has_res):
    best = None
    for tm in (t for t in range(LANES, m + 1, LANES) if m % t == 0):
        for tn in (t for t in range(LANES, n + 1, LANES) if n % t == 0):
            io = tm * k * a_bytes + tn * k * b_bytes + tm * tn * (out_bytes + (4 if has_res else 0))
            casts = (tm * k * 2 if a_bytes == 4 else 0) + (tn * k * 2 if b_bytes == 4 else 0)
            if 2 * io + tm * tn * 4 + casts > MM_VMEM_BUDGET:
                continue
            steps = (m // tm) * (n // tn)
            cost = steps * STEP_OVERHEAD_S + io / TILE_DMA_BYTES_PER_S + 2.0 * m * n * k / MXU_FLOPS_PER_S
            if best is None or (cost, -tm) < best[0]:
                best = ((cost, -tm), tm, tn)
    return best[1], best[2]


def _mm(a, b, mode, out_dtype, name, res=None):
    if mode == "nn":
        (m, k), (_, n) = a.shape, b.shape
    elif mode == "nt":
        (m, k), (n, _) = a.shape, b.shape
    else:
        (k, m), (_, n) = a.shape, b.shape
    tm, tn = _mm_tiles(m, n, k, a.dtype.itemsize, b.dtype.itemsize, jnp.dtype(out_dtype).itemsize, res is not None)
    a_spec = pl.BlockSpec((k, tm), lambda i, j: (0, i)) if mode == "tn" else pl.BlockSpec((tm, k), lambda i, j: (i, 0))
    b_spec = pl.BlockSpec((tn, k), lambda i, j: (j, 0)) if mode == "nt" else pl.BlockSpec((k, tn), lambda i, j: (0, j))
    dims = {"nn": (((1,), (0,)), ((), ())), "nt": (((1,), (1,)), ((), ())), "tn": (((0,), (0,)), ((), ()))}[mode]
    has_res = res is not None

    def body(*refs):
        a_ref, b_ref = refs[0], refs[1]
        o_ref = refs[-1]
        acc = lax.dot_general(a_ref[...].astype(BF16), b_ref[...].astype(BF16), dims, preferred_element_type=F32)
        if has_res:
            acc = acc + refs[2][...]
        o_ref[...] = acc.astype(out_dtype)

    in_specs = [a_spec, b_spec]
    args = [a, b]
    if has_res:
        in_specs.append(pl.BlockSpec((tm, tn), lambda i, j: (i, j)))
        args.append(res)
    return pl.pallas_call(
        body, out_shape=jax.ShapeDtypeStruct((m, n), out_dtype), grid=(m // tm, n // tn), in_specs=in_specs,
        out_specs=pl.BlockSpec((tm, tn), lambda i, j: (i, j)), name=name,
        compiler_params=_params("parallel", "parallel"))(*args)


def _ew(fn, ins, out_dtypes, *, width, bw, name, vecs=(), tm=256):
    rows = ins[0][0].shape[0]
    tm = _pick(rows, tm)
    n_in = len(ins) + len(vecs)

    def col_map(off_blocks):
        return lambda i, j: (i, off_blocks + j)

    in_specs = [pl.BlockSpec((tm, bw), col_map(off // bw)) for _, off in ins]
    in_specs += [pl.BlockSpec((1, bw), lambda i, j: (0, j)) for _ in vecs]

    def body(*refs):
        outs = fn(*[r[...] for r in refs[:n_in]])
        for r, o in zip(refs[n_in:], outs):
            r[...] = o.astype(r.dtype)

    return pl.pallas_call(
        body, out_shape=[jax.ShapeDtypeStruct((rows, width), dt) for dt in out_dtypes],
        grid=(rows // tm, width // bw), in_specs=in_specs,
        out_specs=[pl.BlockSpec((tm, bw), lambda i, j: (i, j)) for _ in out_dtypes],
        name=name, compiler_params=_params("parallel", "parallel"))(*[a for a, _ in ins], *vecs)


def _sigmoid(x):
    return 1.0 / (1.0 + jnp.exp(-x))


def _seg_sum(v):
    outs = []
    for k in range(v.shape[1] // LANES):
        vp = v[:, k * LANES:(k + 1) * LANES]
        left = lax.broadcasted_iota(jnp.int32, vp.shape, 1) < HEAD_DIM
        sl = jnp.sum(jnp.where(left, vp, 0.0), axis=-1, keepdims=True)
        sr = jnp.sum(jnp.where(left, 0.0, vp), axis=-1, keepdims=True)
        outs.append(jnp.where(left, sl, sr))
    return outs[0] if len(outs) == 1 else jnp.concatenate(outs, axis=1)


def _seg_rstd(x):
    return lax.rsqrt(_seg_sum(x * x) * (1.0 / HEAD_DIM) + RMS_EPS)


def _rms_fwd(x, g, name):
    rows, d = x.shape
    tm = 256

    def body(x_ref, g_ref, h_ref):
        xv = x_ref[...]
        r = lax.rsqrt(jnp.mean(xv * xv, axis=-1, keepdims=True) + RMS_EPS)
        h_ref[...] = ((xv * r) * g_ref[...]).astype(BF16)

    return pl.pallas_call(
        body, out_shape=jax.ShapeDtypeStruct((rows, d), BF16), grid=(rows // tm,),
        in_specs=[pl.BlockSpec((tm, d), lambda i: (i, 0)), pl.BlockSpec((1, d), lambda i: (0, 0))],
        out_specs=pl.BlockSpec((tm, d), lambda i: (i, 0)), name=name, compiler_params=_params("parallel"))(x, g)


def _rms_bwd(x, g, dh, dres, name):
    rows, d = x.shape
    tm = 256

    def body(x_ref, g_ref, dh_ref, dres_ref, dx_ref, dxb_ref, dg_ref):
        xv = x_ref[...]
        r = lax.rsqrt(jnp.mean(xv * xv, axis=-1, keepdims=True) + RMS_EPS)
        xh = xv * r
        dhv = dh_ref[...]
        dxh = dhv * g_ref[...]
        dxv = dres_ref[...] + r * (dxh - xh * jnp.mean(dxh * xh, axis=-1, keepdims=True))
        dx_ref[...] = dxv
        dxb_ref[...] = dxv.astype(BF16)
        part = jnp.sum(dhv * xh, axis=0, keepdims=True)

        @pl.when(pl.program_id(0) == 0)
        def _():
            dg_ref[...] = part

        @pl.when(pl.program_id(0) > 0)
        def _():
            dg_ref[...] += part

    row = pl.BlockSpec((tm, d), lambda i: (i, 0))
    vec = pl.BlockSpec((1, d), lambda i: (0, 0))
    return pl.pallas_call(
        body, out_shape=[jax.ShapeDtypeStruct((rows, d), F32), jax.ShapeDtypeStruct((rows, d), BF16),
                         jax.ShapeDtypeStruct((1, d), F32)],
        grid=(rows // tm,), in_specs=[row, vec, row, row], out_specs=[row, row, vec],
        name=name, compiler_params=_params("arbitrary"))(x, g, dh, dres)


def _loss_grad(y, t):
    rows, d = y.shape
    tm = 256

    def body(y_ref, t_ref, dy_ref, l_ref):
        e = y_ref[...] - t_ref[...]
        dy_ref[...] = e * (1.0 / d)
        part = jnp.zeros((1, LANES), F32) + jnp.sum(e * e) * (0.5 / d)

        @pl.when(pl.program_id(0) == 0)
        def _():
            l_ref[...] = part

        @pl.when(pl.program_id(0) > 0)
        def _():
            l_ref[...] += part

    row = pl.BlockSpec((tm, d), lambda i: (i, 0))
    return pl.pallas_call(
        body, out_shape=[jax.ShapeDtypeStruct((rows, d), F32), jax.ShapeDtypeStruct((1, LANES), F32)],
        grid=(rows // tm,), in_specs=[row, row], out_specs=[row, pl.BlockSpec((1, LANES), lambda i: (0, 0))],
        name="loss_grad", compiler_params=_params("arbitrary"))(y, t)


def _put_pairs(ref, val):
    for hp in range(N_PAIRS):
        ref[hp] = val[:, hp * LANES:(hp + 1) * LANES].astype(ref.dtype)


def _get_pairs(ref):
    return jnp.concatenate([ref[hp] for hp in range(N_PAIRS)], axis=1)


def _swap_halves(v):
    return pltpu.roll(v, HEAD_DIM, axis=1)


def _expand_kv(kv):
    left = lax.broadcasted_iota(jnp.int32, kv.shape, 1) < HEAD_DIM
    sw = _swap_halves(kv)
    h0 = jnp.where(left, kv, sw)
    h1 = jnp.where(left, sw, kv)
    return jnp.concatenate([h0, h0, h1, h1], axis=1)


def _reduce_kv(dkv):
    left = lax.broadcasted_iota(jnp.int32, (dkv.shape[0], LANES), 1) < HEAD_DIM
    t = dkv[:, 0:LANES] + dkv[:, LANES:2 * LANES]
    u = dkv[:, 2 * LANES:3 * LANES] + dkv[:, 3 * LANES:4 * LANES]
    t = t + _swap_halves(t)
    u = u + _swap_halves(u)
    return jnp.where(left, t, u)


def _qknorm_fwd(proj, gqa, gka, gqb, gkb):
    rows = proj.shape[0]
    tm = 256

    def body(qa_ref, ka_ref, va_ref, qb_ref, kb_ref, vb_ref, gqa_ref, gka_ref, gqb_ref, gkb_ref,
             oqa, oka, ova, oqb, okb, ovb):
        for src, g_ref, dst in ((qa_ref, gqa_ref, oqa), (ka_ref, gka_ref, oka), (qb_ref, gqb_ref, oqb)):
            xv = src[...]
            _put_pairs(dst, (xv * _seg_rstd(xv)) * g_ref[...])
        _put_pairs(ova, va_ref[...])
        kv = kb_ref[...]
        _put_pairs(okb, _expand_kv((kv * _seg_rstd(kv)) * gkb_ref[...]))
        _put_pairs(ovb, _expand_kv(vb_ref[...]))

    def win(width, off):
        return pl.BlockSpec((tm, width), lambda i: (i, off // width))

    vec = lambda w: pl.BlockSpec((1, w), lambda i: (0, 0))
    out = pl.BlockSpec((N_PAIRS, tm, LANES), lambda i: (0, i, 0))
    return pl.pallas_call(
        body, out_shape=[jax.ShapeDtypeStruct((N_PAIRS, rows, LANES), F32)] * 6, grid=(rows // tm,),
        in_specs=[win(WIDTH, OFF_QA), win(WIDTH, OFF_KA), win(WIDTH, OFF_VA), win(WIDTH, OFF_QB),
                  win(LANES, OFF_KB), win(LANES, OFF_VB), vec(WIDTH), vec(WIDTH), vec(WIDTH), vec(LANES)],
        out_specs=[out] * 6, name="qknorm_fwd", compiler_params=_params("parallel"))(
            proj, proj, proj, proj, proj, proj, gqa, gka, gqb, gkb)


def _norm_bwd(xv, g, dy):
    r = _seg_rstd(xv)
    xh = xv * r
    dxh = dy * g
    dx = r * (dxh - xh * (_seg_sum(dxh * xh) * (1.0 / HEAD_DIM)))
    return dx, jnp.sum(dy * xh, axis=0, keepdims=True)


def _qknorm_bwd(proj, gqa, gka, gqb, gkb, dqa, dka, dva, dqb, dkb, dvb, dgab):
    rows = proj.shape[0]
    tm = 256
    n_a = len(dqa)

    def body(*refs):
        qa_ref, ka_ref, qb_ref, kb_ref, gqa_ref, gka_ref, gqb_ref, gkb_ref = refs[:8]
        pos = 8
        dqa_refs, dka_refs, dva_refs = refs[pos:pos + n_a], refs[pos + n_a:pos + 2 * n_a], refs[pos + 2 * n_a:pos + 3 * n_a]
        pos += 3 * n_a
        dqb_ref, dkb_ref, dvb_ref, dgab_ref = refs[pos:pos + 4]
        dproj_ref, ogqa, ogka, ogqb, ogkb = refs[pos + 4:]

        def total(rs):
            acc = _get_pairs(rs[0])
            for r in rs[1:]:
                acc = acc + _get_pairs(r)
            return acc

        dx_qa, p_qa = _norm_bwd(qa_ref[...], gqa_ref[...], total(dqa_refs))
        dx_ka, p_ka = _norm_bwd(ka_ref[...], gka_ref[...], total(dka_refs))
        dx_qb, p_qb = _norm_bwd(qb_ref[...], gqb_ref[...], _get_pairs(dqb_ref))
        dx_kb, p_kb = _norm_bwd(kb_ref[...], gkb_ref[...], _reduce_kv(_get_pairs(dkb_ref)))
        dproj_ref[:, OFF_QA:OFF_QA + WIDTH] = dx_qa.astype(BF16)
        dproj_ref[:, OFF_KA:OFF_KA + WIDTH] = dx_ka.astype(BF16)
        dproj_ref[:, OFF_VA:OFF_VA + WIDTH] = total(dva_refs).astype(BF16)
        dproj_ref[:, OFF_QB:OFF_QB + WIDTH] = dx_qb.astype(BF16)
        dproj_ref[:, OFF_KB:OFF_KB + LANES] = dx_kb.astype(BF16)
        dproj_ref[:, OFF_VB:OFF_VB + LANES] = _reduce_kv(_get_pairs(dvb_ref)).astype(BF16)
        dproj_ref[:, OFF_GA:D_IN] = dgab_ref[...]
        first = pl.program_id(0) == 0
        for o_ref, part in ((ogqa, p_qa), (ogka, p_ka), (ogqb, p_qb), (ogkb, p_kb)):
            @pl.when(first)
            def _(o_ref=o_ref, part=part):
                o_ref[...] = part

            @pl.when(jnp.logical_not(first))
            def _(o_ref=o_ref, part=part):
                o_ref[...] += part

    def win(width, off):
        return pl.BlockSpec((tm, width), lambda i: (i, off // width))

    vec = lambda w: pl.BlockSpec((1, w), lambda i: (0, 0))
    row = lambda w: pl.BlockSpec((tm, w), lambda i: (i, 0))
    in_specs = [win(WIDTH, OFF_QA), win(WIDTH, OFF_KA), win(WIDTH, OFF_QB), win(LANES, OFF_KB),
                vec(WIDTH), vec(WIDTH), vec(WIDTH), vec(LANES)]
    in_specs += [pl.BlockSpec((N_PAIRS, tm, LANES), lambda i: (0, i, 0))] * (3 * n_a + 3) + [row(2 * D_MODEL)]
    return pl.pallas_call(
        body,
        out_shape=[jax.ShapeDtypeStruct((rows, D_IN), BF16), jax.ShapeDtypeStruct((1, WIDTH), F32),
                   jax.ShapeDtypeStruct((1, WIDTH), F32), jax.ShapeDtypeStruct((1, WIDTH), F32),
                   jax.ShapeDtypeStruct((1, LANES), F32)],
        grid=(rows // tm,), in_specs=in_specs,
        out_specs=[row(D_IN), vec(WIDTH), vec(WIDTH), vec(WIDTH), vec(LANES)],
        name="qknorm_bwd", compiler_params=_params("arbitrary"))(
            proj, proj, proj, proj, gqa, gka, gqb, gkb, *dqa, *dka, *dva, dqb, dkb, dvb, dgab)


def _t5_bucket(rel):
    half_b = NUM_BUCKETS // 2
    max_exact = half_b // 2
    sign = jnp.where(rel > 0, half_b, 0)
    n = jnp.abs(rel)
    nf = jnp.maximum(n, 1).astype(F32)
    large = max_exact + (jnp.log(nf / max_exact) / math.log(MAX_DISTANCE / max_exact)
                         * (half_b - max_exact)).astype(jnp.int32)
    large = jnp.minimum(large, half_b - 1)
    return sign + jnp.where(n < max_exact, n, large)


def _band_buckets(blk, dilation):
    i = jnp.arange(blk, dtype=jnp.int32)[:, None]
    j = jnp.arange(3 * blk, dtype=jnp.int32)[None, :]
    rel = j - blk - i
    return jnp.where(jnp.abs(rel) <= blk, _t5_bucket(rel * dilation), -1)


def _bias_tiles(table, buckets, head_off, name):
    blk = buckets.shape[0]

    def body(tab_ref, bk_ref, o_ref):
        h = pl.program_id(0) + head_off
        bk = bk_ref[...]
        acc = jnp.full(bk.shape, NEG_INF, F32)
        for b in range(NUM_BUCKETS):
            acc = jnp.where(bk == b, tab_ref[b, h], acc)
        o_ref[0] = acc

    return pl.pallas_call(
        body, out_shape=jax.ShapeDtypeStruct((N_HEADS, blk, 3 * blk), F32), grid=(N_HEADS,),
        in_specs=[pl.BlockSpec(memory_space=pltpu.SMEM), pl.BlockSpec((blk, 3 * blk), lambda h: (0, 0))],
        out_specs=pl.BlockSpec((1, blk, 3 * blk), lambda h: (h, 0, 0)),
        name=name, compiler_params=_params("parallel"))(table, buckets)


def _table_grad(dbias, buckets, name):
    blk = buckets.shape[0]

    def body(db_ref, bk_ref, o_ref):
        bk = bk_ref[...]
        dbv = db_ref[0]
        lane = lax.broadcasted_iota(jnp.int32, (1, LANES), 1)
        acc = jnp.zeros((1, LANES), F32)
        for b in range(NUM_BUCKETS):
            acc = jnp.where(lane == b, jnp.sum(jnp.where(bk == b, dbv, 0.0)), acc)
        o_ref[0] = acc

    out = pl.pallas_call(
        body, out_shape=jax.ShapeDtypeStruct((N_HEADS, 1, LANES), F32), grid=(N_HEADS,),
        in_specs=[pl.BlockSpec((1, blk, 3 * blk), lambda h: (h, 0, 0)), pl.BlockSpec((blk, 3 * blk), lambda h: (0, 0))],
        out_specs=pl.BlockSpec((1, 1, LANES), lambda h: (h, 0, 0)),
        name=name, compiler_params=_params("parallel"))(dbias, buckets)
    return out[:, 0, :NUM_BUCKETS]


def _dot_nt(a, b):
    return lax.dot_general(a, b, (((1,), (1,)), ((), ())), preferred_element_type=F32)


def _stack_pair(x2, left):
    return jnp.concatenate([jnp.where(left, x2, 0.0), jnp.where(left, 0.0, x2)], axis=0).astype(BF16)


def _attn_geometry(blk, d):
    chunk = blk * ITEMS if d == 1 else blk * d
    groups = 1 if d == 1 else d // ITEMS
    halo = blk if d == 1 else chunk
    return chunk, groups, halo


def _item_rows(ref, j, r0, blk, d):
    if d == 1:
        return ref[j * blk:(j + 1) * blk, :]
    return ref[pl.ds(r0 + j, blk, stride=d), :]


def _item_penalty(t, nct, j, blk, d):
    first_ok, last_ok = t > 0, t < nct - 1
    if d == 1:
        first_ok = True if j > 0 else first_ok
        last_ok = True if j < ITEMS - 1 else last_ok
    col = lax.broadcasted_iota(jnp.int32, (1, 3 * blk), 1)
    ok = jnp.logical_and(jnp.logical_or(col >= blk, first_ok), jnp.logical_or(col < 2 * blk, last_ok))
    return jnp.where(ok, 0.0, NEG_INF).astype(F32)


def _attn_specs(seq, blk, d, step_of):
    chunk, _, halo = _attn_geometry(blk, d)
    per, last = chunk // halo, seq // halo - 1
    cur = pl.BlockSpec((None, chunk, LANES), lambda hp, t: (hp, step_of(t), 0))
    prev = pl.BlockSpec((None, halo, LANES), lambda hp, t: (hp, jnp.clip(step_of(t) * per - 1, 0, last), 0))
    nxt = pl.BlockSpec((None, halo, LANES), lambda hp, t: (hp, jnp.minimum((step_of(t) + 1) * per, last), 0))
    return cur, prev, nxt


def _attn_fwd(q, k, v, bias, sink, blk, d, name):
    _, seq, _ = q.shape
    chunk, groups, _ = _attn_geometry(blk, d)
    nct = seq // chunk
    has_sink = sink is not None
    scale = HEAD_DIM ** -0.5

    def body(*refs):
        q_ref, kp, kc, kn, vp, vc, vn, b_ref = refs[:8]
        s_ref = refs[8] if has_sink else None
        o_ref, l_ref = refs[-2], refs[-1]
        t = pl.program_id(1)
        left = lax.broadcasted_iota(jnp.int32, (1, LANES), 1) < HEAD_DIM
        bias2 = b_ref[...]
        if d == 1:
            kwin = jnp.concatenate([kp[...], kc[...], kn[...]], axis=0).astype(BF16)
            vwin = jnp.concatenate([vp[...], vc[...], vn[...]], axis=0).astype(BF16)

        def group(r0):
            scores, vcats = [], []
            for j in range(ITEMS):
                qs = _stack_pair(_item_rows(q_ref, j, r0, blk, d) * scale, left)
                if d == 1:
                    kcat, vcat = kwin[j * blk:(j + 3) * blk], vwin[j * blk:(j + 3) * blk]
                else:
                    kcat = jnp.concatenate([_item_rows(r, j, r0, blk, d) for r in (kp, kc, kn)], axis=0).astype(BF16)
                    vcat = jnp.concatenate([_item_rows(r, j, r0, blk, d) for r in (vp, vc, vn)], axis=0).astype(BF16)
                scores.append(_dot_nt(qs, kcat) + bias2 + _item_penalty(t, nct, j, blk, d))
                vcats.append(vcat)
            s = jnp.concatenate(scores, axis=0)
            m = jnp.max(s, axis=-1, keepdims=True)
            if has_sink:
                sk = jnp.concatenate([s_ref[...]] * ITEMS, axis=0)
                m = jnp.maximum(m, sk)
            p = jnp.exp(s - m)
            den = jnp.sum(p, axis=-1, keepdims=True)
            if has_sink:
                den = den + jnp.exp(sk - m)
            pn = (p * (1.0 / den)).astype(BF16)
            lse = m + jnp.log(den)
            for j in range(ITEMS):
                top, mid, bot = 2 * j * blk, (2 * j + 1) * blk, (2 * j + 2) * blk
                o2 = jnp.dot(pn[top:bot], vcats[j], preferred_element_type=F32)
                o_val = jnp.where(left, o2[:blk], o2[blk:])
                l_val = jnp.where(left, lse[top:mid], lse[mid:bot])
                if d == 1:
                    o_ref[j * blk:(j + 1) * blk, :] = o_val
                    l_ref[j * blk:(j + 1) * blk, :] = l_val
                else:
                    o_ref[pl.ds(r0 + j, blk, stride=d), :] = o_val
                    l_ref[pl.ds(r0 + j, blk, stride=d), :] = l_val

        if groups == 1:
            group(0)
        else:
            def step(g, carry):
                group(g * ITEMS)
                return carry

            lax.fori_loop(0, groups, step, 0)

    cur, prev, nxt = _attn_specs(seq, blk, d, lambda t: t)
    in_specs = [cur, prev, cur, nxt, prev, cur, nxt, pl.BlockSpec((2 * blk, 3 * blk), lambda hp, t: (hp, 0))]
    args = [q, k, k, k, v, v, v, bias]
    if has_sink:
        in_specs.append(pl.BlockSpec((2 * blk, 1), lambda hp, t: (hp, 0)))
        args.append(sink)
    return pl.pallas_call(
        body, out_shape=[jax.ShapeDtypeStruct(q.shape, F32)] * 2, grid=(N_PAIRS, nct),
        in_specs=in_specs, out_specs=[cur, cur], name=name, compiler_params=_params("parallel", "parallel"))(*args)


def _attn_bwd(q, k, v, do, lse, delta, bias, sink, blk, d, name):
    _, seq, _ = q.shape
    chunk, groups, halo = _attn_geometry(blk, d)
    nct = seq // chunk
    has_sink = sink is not None
    n_in = 12 if has_sink else 11
    scale = HEAD_DIM ** -0.5

    def body(*refs):
        q_ref, kp, kc, kn, vp, vc, vn, do_ref, l_ref, d_ref, b_ref = refs[:11]
        s_ref = refs[11] if has_sink else None
        dq_ref, dk_ref, dv_ref, db_ref = refs[n_in:n_in + 4]
        ds_ref = refs[n_in + 4] if has_sink else None
        wk, wv = refs[-2], refs[-1]
        t = pl.program_id(1)

        @pl.when(t == 0)
        def _():
            wk[...] = jnp.zeros_like(wk)
            wv[...] = jnp.zeros_like(wv)
            db_ref[...] = jnp.zeros_like(db_ref)
            if has_sink:
                ds_ref[...] = jnp.zeros_like(ds_ref)

        @pl.when(t > 0)
        def _():
            for w in (wk, wv):
                keep = w[chunk:2 * chunk + halo]
                w[0:chunk + halo] = keep
                w[chunk + halo:2 * chunk + halo] = jnp.zeros((chunk, LANES), F32)

        @pl.when(t < nct)
        def _():
            lane = lax.broadcasted_iota(jnp.int32, (1, LANES), 1)
            left = lane < HEAD_DIM
            bias2 = b_ref[...]
            if d == 1:
                kwin = jnp.concatenate([kp[...], kc[...], kn[...]], axis=0).astype(BF16)
                vwin = jnp.concatenate([vp[...], vc[...], vn[...]], axis=0).astype(BF16)

            def group(r0):
                qss, doss, kcats, scores, dps, lcols, dcols = [], [], [], [], [], [], []
                for j in range(ITEMS):
                    qs = _stack_pair(_item_rows(q_ref, j, r0, blk, d) * scale, left)
                    dos = _stack_pair(_item_rows(do_ref, j, r0, blk, d), left)
                    if d == 1:
                        kcat, vcat = kwin[j * blk:(j + 3) * blk], vwin[j * blk:(j + 3) * blk]
                    else:
                        kcat = jnp.concatenate([_item_rows(r, j, r0, blk, d) for r in (kp, kc, kn)], axis=0).astype(BF16)
                        vcat = jnp.concatenate([_item_rows(r, j, r0, blk, d) for r in (vp, vc, vn)], axis=0).astype(BF16)
                    l2, d2 = _item_rows(l_ref, j, r0, blk, d), _item_rows(d_ref, j, r0, blk, d)
                    lcols.append(jnp.max(jnp.where(left, l2, NEG_INF), axis=-1, keepdims=True))
                    lcols.append(jnp.max(jnp.where(left, NEG_INF, l2), axis=-1, keepdims=True))
                    dcols.append(jnp.sum(jnp.where(lane == 0, d2, 0.0), axis=-1, keepdims=True))
                    dcols.append(jnp.sum(jnp.where(lane == HEAD_DIM, d2, 0.0), axis=-1, keepdims=True))
                    scores.append(_dot_nt(qs, kcat) + bias2 + _item_penalty(t, nct, j, blk, d))
                    dps.append(_dot_nt(dos, vcat))
                    qss.append(qs)
                    doss.append(dos)
                    kcats.append(kcat)
                lcol = jnp.concatenate(lcols, axis=0)
                dcol = jnp.concatenate(dcols, axis=0)
                p = jnp.exp(jnp.concatenate(scores, axis=0) - lcol)
                ds = p * (jnp.concatenate(dps, axis=0) - dcol)
                if has_sink:
                    sgrad = dcol * jnp.exp(jnp.concatenate([s_ref[...]] * ITEMS, axis=0) - lcol)
                for j in range(ITEMS):
                    top, bot = 2 * j * blk, (2 * j + 2) * blk
                    dsj, pj = ds[top:bot], p[top:bot]
                    db_ref[...] += dsj
                    if has_sink:
                        ds_ref[...] -= sgrad[top:bot]
                    dq2 = jnp.dot(dsj.astype(BF16), kcats[j], preferred_element_type=F32) * scale
                    dq_val = jnp.where(left, dq2[:blk], dq2[blk:])
                    if d == 1:
                        dq_ref[j * blk:(j + 1) * blk, :] = dq_val
                    else:
                        dq_ref[pl.ds(r0 + j, blk, stride=d), :] = dq_val
                    dk_new = jnp.dot(jnp.transpose(dsj).astype(BF16), qss[j], preferred_element_type=F32)
                    dv_new = jnp.dot(jnp.transpose(pj).astype(BF16), doss[j], preferred_element_type=F32)
                    for w, new in ((wk, dk_new), (wv, dv_new)):
                        if d == 1:
                            w[chunk + (j - 1) * blk:chunk + (j + 2) * blk, :] += new
                        else:
                            for c in range(3):
                                w[pl.ds(c * chunk + r0 + j, blk, stride=d), :] += new[c * blk:(c + 1) * blk]

            if groups == 1:
                group(0)
            else:
                def step(g, carry):
                    group(g * ITEMS)
                    return carry

                lax.fori_loop(0, groups, step, 0)

        dk_ref[...] = wk[0:chunk]
        dv_ref[...] = wv[0:chunk]

    cur, prev, nxt = _attn_specs(seq, blk, d, lambda t: jnp.minimum(t, nct - 1))
    lag = pl.BlockSpec((None, chunk, LANES), lambda hp, t: (hp, jnp.maximum(t - 1, 0), 0))
    band = pl.BlockSpec((2 * blk, 3 * blk), lambda hp, t: (hp, 0))
    col = pl.BlockSpec((2 * blk, 1), lambda hp, t: (hp, 0))
    in_specs = [cur, prev, cur, nxt, prev, cur, nxt, cur, cur, cur, band]
    args = [q, k, k, k, v, v, v, do, lse, delta, bias]
    out_shape = [jax.ShapeDtypeStruct(q.shape, F32)] * 3 + [jax.ShapeDtypeStruct((N_HEADS * blk, 3 * blk), F32)]
    out_specs = [cur, lag, lag, band]
    if has_sink:
        in_specs.append(col)
        args.append(sink)
        out_shape.append(jax.ShapeDtypeStruct((N_HEADS * blk, 1), F32))
        out_specs.append(col)
    window = pltpu.VMEM((2 * chunk + halo, LANES), F32)
    return pl.pallas_call(
        body, out_shape=out_shape, grid=(N_PAIRS, nct + 1), in_specs=in_specs, out_specs=out_specs,
        scratch_shapes=[window, window], name=name, compiler_params=_params("arbitrary", "arbitrary"))(*args)


def _combine_patterns(outs, lses):
    _, rows, _ = outs[0].shape
    tm = 256
    n = len(outs)

    def body(*refs):
        o_refs, l_refs = refs[:n], refs[n:2 * n]
        y_ref, lse_ref = refs[2 * n], refs[2 * n + 1]
        for hp in range(N_PAIRS):
            ls = [r[hp] for r in l_refs]
            m = functools.reduce(jnp.maximum, ls)
            es = [jnp.exp(l - m) for l in ls]
            den = functools.reduce(lambda a, b: a + b, es)
            num = functools.reduce(lambda a, b: a + b, [e * r[hp] for e, r in zip(es, o_refs)])
            y_ref[:, hp * LANES:(hp + 1) * LANES] = num / den
            lse_ref[hp] = m + jnp.log(den)

    pm = pl.BlockSpec((N_PAIRS, tm, LANES), lambda i: (0, i, 0))
    return pl.pallas_call(
        body, out_shape=[jax.ShapeDtypeStruct((rows, WIDTH), F32), jax.ShapeDtypeStruct((N_PAIRS, rows, LANES), F32)],
        grid=(rows // tm,), in_specs=[pm] * (2 * n), out_specs=[pl.BlockSpec((tm, WIDTH), lambda i: (i, 0)), pm],
        name="combine_a", compiler_params=_params("parallel"))(*outs, *lses)


def _pairs_to_tokens(a):
    _, rows, _ = a.shape
    tm = 256

    def body(a_ref, o_ref):
        o_ref[...] = _get_pairs(a_ref)

    return pl.pallas_call(
        body, out_shape=jax.ShapeDtypeStruct((rows, WIDTH), a.dtype), grid=(rows // tm,),
        in_specs=[pl.BlockSpec((N_PAIRS, tm, LANES), lambda i: (0, i, 0))],
        out_specs=pl.BlockSpec((tm, WIDTH), lambda i: (i, 0)), name="pairs_to_tokens",
        compiler_params=_params("parallel"))(a)


def _attn_bwd_prep(dy, y, name):
    rows = dy.shape[0]
    tm = 256

    def body(dy_ref, y_ref, do_ref, dl_ref):
        dyv = dy_ref[...]
        _put_pairs(do_ref, dyv)
        _put_pairs(dl_ref, _seg_sum(dyv * y_ref[...]))

    tok = pl.BlockSpec((tm, WIDTH), lambda i: (i, 0))
    pm = pl.BlockSpec((N_PAIRS, tm, LANES), lambda i: (0, i, 0))
    return pl.pallas_call(
        body, out_shape=[jax.ShapeDtypeStruct((N_PAIRS, rows, LANES), F32)] * 2, grid=(rows // tm,),
        in_specs=[tok, tok], out_specs=[pm, pm], name=name, compiler_params=_params("parallel"))(dy, y)


def _tile_gain(g, reps):
    return jnp.tile(g[None, :], (1, reps))


def _local_step(x, p, target, w_in_of, rest_of, small):
    rel_table = small["rel_table"]
    buckets_a = [_band_buckets(blk, d) for blk, d in DILATED]
    buckets_b = _band_buckets(BLK_B, 1)
    bias_a = [_bias_tiles(rel_table, bk, 0, "bias_a").reshape(N_HEADS * bk.shape[0], -1) for bk in buckets_a]
    bias_b = _bias_tiles(rel_table, buckets_b, N_HEADS, "bias_b").reshape(N_HEADS * BLK_B, -1)

    saved = []
    for l in range(DEPTH):
        g_mix, g_ffn, g_ple = (small[n][l][None, :] for n in ("norm_mix_g", "norm_ffn_g", "norm_ple_g"))
        gqa, gka, gqb = (_tile_gain(small[n][l], N_HEADS) for n in ("qnorm_a_g", "knorm_a_g", "qnorm_b_g"))
        gkb = _tile_gain(small["knorm_b_g"][l], N_KV_B)
        sink = jnp.repeat(small["sink_b"][l], BLK_B)[:, None]

        h = _rms_fwd(x, g_mix, "rms_mix")
        w_in = w_in_of(l, h)
        proj = _mm(h, w_in, "nn", F32, "mm_in")
        qa, ka, va, qb, kb, vb = _qknorm_fwd(proj, gqa, gka, gqb, gkb)
        outs, lses = [], []
        for (blk, d), bias in zip(DILATED, bias_a):
            o, ls = _attn_fwd(qa, ka, va, bias, None, blk, d, f"attn_a{d}_fwd")
            outs.append(o)
            lses.append(ls)
        ya, lse_a = _combine_patterns(outs, lses)
        yb, lse_b = _attn_fwd(qb, kb, vb, bias_b, sink, BLK_B, 1, "attn_b_fwd")
        yb = _pairs_to_tokens(yb)
        w = dict(rest_of(l, yb), w_in=w_in)
        ca = _mm(ya, w["w_branch_a"], "nn", F32, "mm_branch_a")
        cb = _mm(yb, w["w_branch_b"], "nn", F32, "mm_branch_b")

        def gate(ca_, cb_, ga_, gb_):
            return (_sigmoid(ga_) * ca_ + _sigmoid(gb_) * cb_,)

        (merged,) = _ew(gate, [(ca, 0), (cb, 0), (proj, OFF_GA), (proj, OFF_GB)], [BF16],
                        width=D_MODEL, bw=256, name="gate")
        x1 = _mm(merged, w["w_out"], "nn", F32, "mm_out", res=x)

        h2 = _rms_fwd(x1, g_ffn, "rms_ffn")
        a = _mm(h2, w["w_ffn_gate"], "nn", F32, "mm_ffn_gate")
        u = _mm(h2, w["w_ffn_up"], "nn", F32, "mm_ffn_up")

        def swiglu(a_, u_):
            return ((a_ * _sigmoid(a_)) * u_,)

        (hid,) = _ew(swiglu, [(a, 0), (u, 0)], [BF16], width=D_FF, bw=D_FF, name="swiglu")
        x2 = _mm(hid, w["w_ffn_down"], "nn", F32, "mm_ffn_down", res=x1)

        h3 = _rms_fwd(x2, g_ple, "rms_ple")
        z = _mm(h3, w["w_ple_gate"], "nn", F32, "mm_ple_gate")
        e = _mm(p[l], w["w_ple_proj"], "nn", F32, "mm_ple_proj")

        def ple(x2_, z_, e_):
            return (x2_ + _sigmoid(z_) * e_,)

        (x3,) = _ew(ple, [(x2, 0), (z, 0), (e, 0)], [F32], width=D_MODEL, bw=D_MODEL, name="ple")
        saved.append(dict(w=w, x0=x, h=h, proj=proj, qa=qa, ka=ka, va=va, qb=qb, kb=kb, vb=vb, ya=ya, lse_a=lse_a,
                          yb=yb, lse_b=lse_b, ca=ca, cb=cb, merged=merged, x1=x1, h2=h2, a=a, u=u, hid=hid,
                          x2=x2, h3=h3, z=z, e=e))
        x = x3

    dx, loss_acc = _loss_grad(x, target)
    loss = loss_acc[0, 0]

    gbig = [{} for _ in range(DEPTH)]
    marks = [{} for _ in range(DEPTH)]
    gsmall = {n: [None] * DEPTH for n in SMALL if n != "rel_table"}
    dtable_a = jnp.zeros((N_HEADS, NUM_BUCKETS), F32)
    dtable_b = jnp.zeros((N_HEADS, NUM_BUCKETS), F32)

    for l in reversed(range(DEPTH)):
        sv = saved[l]
        w = sv["w"]
        g_mix, g_ffn, g_ple = (small[n][l][None, :] for n in ("norm_mix_g", "norm_ffn_g", "norm_ple_g"))
        gqa, gka, gqb = (_tile_gain(small[n][l], N_HEADS) for n in ("qnorm_a_g", "knorm_a_g", "qnorm_b_g"))
        gkb = _tile_gain(small["knorm_b_g"][l], N_KV_B)
        sink = jnp.repeat(small["sink_b"][l], BLK_B)[:, None]

        def ple_bwd(dx_, z_, e_):
            s = _sigmoid(z_)
            return dx_ * s, dx_ * e_ * (s * (1.0 - s))

        de, dz = _ew(ple_bwd, [(dx, 0), (sv["z"], 0), (sv["e"], 0)], [BF16, BF16], width=D_MODEL, bw=D_MODEL,
                     name="ple_bwd")
        gbig[l]["w_ple_proj"] = _mm(p[l], de, "tn", F32, "mm_d_ple_proj")
        gbig[l]["w_ple_gate"] = _mm(sv["h3"], dz, "tn", F32, "mm_d_ple_gate")
        dh3 = _mm(dz, w["w_ple_gate"], "nt", F32, "mm_dh3")
        dx, dxb, gsmall["norm_ple_g"][l] = _rms_bwd(sv["x2"], g_ple, dh3, dx, "rms_ple_bwd")

        dhid = _mm(dxb, w["w_ffn_down"], "nt", F32, "mm_dhid")
        gbig[l]["w_ffn_down"] = _mm(sv["hid"], dxb, "tn", F32, "mm_d_ffn_down")

        def swiglu_bwd(a_, u_, dh_):
            s = _sigmoid(a_)
            return dh_ * u_ * (s * (1.0 + a_ * (1.0 - s))), dh_ * (a_ * s)

        da, du = _ew(swiglu_bwd, [(sv["a"], 0), (sv["u"], 0), (dhid, 0)], [BF16, BF16], width=D_FF, bw=D_FF,
                     name="swiglu_bwd")
        gbig[l]["w_ffn_gate"] = _mm(sv["h2"], da, "tn", F32, "mm_d_ffn_gate")
        gbig[l]["w_ffn_up"] = _mm(sv["h2"], du, "tn", F32, "mm_d_ffn_up")
        dh2 = _mm(da, w["w_ffn_gate"], "nt", F32, "mm_dh2_gate")
        dh2 = _mm(du, w["w_ffn_up"], "nt", F32, "mm_dh2_up", res=dh2)
        dx, dxb, gsmall["norm_ffn_g"][l] = _rms_bwd(sv["x1"], g_ffn, dh2, dx, "rms_ffn_bwd")

        dmerged = _mm(dxb, w["w_out"], "nt", F32, "mm_dmerged")
        marks[l]["ffn_bwd_done"] = dmerged
        gbig[l]["w_out"] = _mm(sv["merged"], dxb, "tn", F32, "mm_d_out")

        def gate_bwd(dm_, ca_, cb_, ga_, gb_):
            sa, sb = _sigmoid(ga_), _sigmoid(gb_)
            return dm_ * sa, dm_ * sb, dm_ * ca_ * (sa * (1.0 - sa)), dm_ * cb_ * (sb * (1.0 - sb))

        dca, dcb, dga, dgb = _ew(gate_bwd, [(dmerged, 0), (sv["ca"], 0), (sv["cb"], 0), (sv["proj"], OFF_GA),
                                            (sv["proj"], OFF_GB)], [BF16, BF16, BF16, BF16],
                                 width=D_MODEL, bw=256, name="gate_bwd")
        dgab = jnp.concatenate([dga, dgb], axis=1)
        gbig[l]["w_branch_a"] = _mm(sv["ya"], dca, "tn", F32, "mm_d_branch_a")
        gbig[l]["w_branch_b"] = _mm(sv["yb"], dcb, "tn", F32, "mm_d_branch_b")
        dya = _mm(dca, w["w_branch_a"], "nt", F32, "mm_dya")
        dyb = _mm(dcb, w["w_branch_b"], "nt", F32, "mm_dyb")

        dya, delta_a = _attn_bwd_prep(dya, sv["ya"], "attn_a_bwd_prep")
        dyb, delta_b = _attn_bwd_prep(dyb, sv["yb"], "attn_b_bwd_prep")

        dqa, dka, dva = [], [], []
        for (blk, d), bias, bk in zip(DILATED, bias_a, buckets_a):
            dq_, dk_, dv_, db_ = _attn_bwd(sv["qa"], sv["ka"], sv["va"], dya, sv["lse_a"], delta_a, bias, None, blk, d,
                                           f"attn_a{d}_bwd")
            dqa.append(dq_)
            dka.append(dk_)
            dva.append(dv_)
            dtable_a = dtable_a + _table_grad(db_.reshape(N_HEADS, blk, 3 * blk), bk, "table_grad_a")
        dqb, dkb, dvb, db_, dsink = _attn_bwd(sv["qb"], sv["kb"], sv["vb"], dyb, sv["lse_b"], delta_b, bias_b, sink,
                                              BLK_B, 1, "attn_b_bwd")
        dtable_b = dtable_b + _table_grad(db_.reshape(N_HEADS, BLK_B, 3 * BLK_B), buckets_b, "table_grad_b")
        gsmall["sink_b"][l] = dsink.reshape(N_HEADS, BLK_B).sum(axis=1)

        dproj, pqa, pka, pqb, pkb = _qknorm_bwd(sv["proj"], gqa, gka, gqb, gkb, dqa, dka, dva, dqb, dkb, dvb, dgab)
        marks[l]["attn_bwd_done"] = dproj
        gsmall["qnorm_a_g"][l] = pqa.reshape(N_HEADS, HEAD_DIM).sum(0)
        gsmall["knorm_a_g"][l] = pka.reshape(N_HEADS, HEAD_DIM).sum(0)
        gsmall["qnorm_b_g"][l] = pqb.reshape(N_HEADS, HEAD_DIM).sum(0)
        gsmall["knorm_b_g"][l] = pkb.reshape(N_KV_B, HEAD_DIM).sum(0)
        gbig[l]["w_in"] = _mm(sv["h"], dproj, "tn", F32, "mm_d_in")
        dh = _mm(dproj, w["w_in"], "nt", F32, "mm_dh")
        dx, _, gsmall["norm_mix_g"][l] = _rms_bwd(sv["x0"], g_mix, dh, dx, "rms_mix_bwd")
        gsmall["norm_mix_g"][l] = gsmall["norm_mix_g"][l][0]
        gsmall["norm_ffn_g"][l] = gsmall["norm_ffn_g"][l][0]
        gsmall["norm_ple_g"][l] = gsmall["norm_ple_g"][l][0]

    gsmall = {n: jnp.stack(v) for n, v in gsmall.items()}
    gsmall["rel_table"] = jnp.concatenate([dtable_a, dtable_b], axis=0).T
    return loss, dx, gbig, gsmall, marks


def _place():
    return lax.axis_index("x"), lax.axis_index("y"), lax.axis_index("c")


def _flip(v, bit):
    return 1 - v if bit else v


CHIP_RELATIONS = ((0, 1), (1, 0), (1, 1))
ANY = pl.BlockSpec(memory_space=pl.ANY)


def _allgather_body(w_refs, out_refs, send_sems, recv_sems):
    x, y, c = _place()
    chips = [(_flip(x, a), _flip(y, b)) for a, b in CHIP_RELATIONS]

    def make(g):
        w_ref, out_ref = w_refs[g], out_refs[g]
        half = w_ref.shape[0] // 2

        def part(px, py, pc):
            return out_ref.at[2 * px + py, pl.ds(pc * half, half), :]

        def copy(k, block, to, src=None):
            return pltpu.make_async_remote_copy(
                src_ref=part(*block) if src is None else src, dst_ref=part(*block),
                send_sem=send_sems.at[7 * g + k], recv_sem=recv_sems.at[7 * g + k], device_id=to,
                device_id_type=MESH_ID)

        own = pltpu.make_async_remote_copy(
            src_ref=w_ref, dst_ref=out_ref.at[2 * x + y], send_sem=send_sems.at[7 * g + 6],
            recv_sem=recv_sems.at[7 * g + 6], device_id=(x, y, 1 - c), device_id_type=MESH_ID)
        first = [copy(k, (x, y, c), (*chip, c), src=w_ref.at[pl.ds(c * half, half), :]) for k, chip in enumerate(chips)]
        passed = [copy(3 + k, (*chip, c), (x, y, 1 - c)) for k, chip in enumerate(chips)]
        arrive = [copy(k, (*chip, c), (x, y, c)) for k, chip in enumerate(chips)]
        arrive2 = [copy(3 + k, (*chip, 1 - c), (x, y, c)) for k, chip in enumerate(chips)]
        return own, first, passed, arrive, arrive2

    made = [make(g) for g in range(len(w_refs))]
    for own, first, _, _, _ in made:
        own.start()
        for cp in first:
            cp.start()
    for _, _, passed, arrive, _ in made:
        for k in range(3):
            arrive[k].wait_recv()
            passed[k].start()
    for own, first, passed, _, arrive2 in made:
        for k in range(3):
            arrive2[k].wait_recv()
        own.wait_recv()
        for cp in first + passed + [own]:
            cp.wait_send()


def _sibling(x, y, c):
    return [(x, y, 1 - c)]


def _same_core_of_other_chips(x, y, c):
    return [(_flip(x, a), _flip(y, b), c) for a, b in CHIP_RELATIONS]


def _exchange(body, ins, out_types, n_sems, name, sequencer=None):
    n = len(ins)
    sems = (pltpu.SemaphoreType.DMA((n_sems,)), pltpu.SemaphoreType.DMA((n_sems,)))
    if sequencer is None:
        in_place = out_types is None
        out_shape = [jax.ShapeDtypeStruct(a.shape, a.dtype) for a in ins] if in_place else out_types

        def tc_body(*refs):
            body(refs[:n], refs[n:n + len(out_shape)], refs[-2], refs[-1])

        return list(pl.pallas_call(
            tc_body, out_shape=out_shape, in_specs=[ANY] * n, out_specs=[ANY] * len(out_shape),
            input_output_aliases={g: g for g in range(n)} if in_place else {}, scratch_shapes=list(sems), name=name)(*ins))

    collective_id, peers = sequencer
    hbm = pltpu.MemorySpace.HBM
    in_refs = [jax.new_ref(a, memory_space=hbm) for a in ins]
    out_refs = in_refs if out_types is None else [jax.empty_ref(t, memory_space=hbm) for t in out_types]

    @pl.kernel(mesh=plsc.ScalarSubcoreMesh(axis_name="sequencer", num_cores=1), name=name, scratch_types=sems,
               compiler_params=pltpu.CompilerParams(collective_id=collective_id))
    def launch(send_sems, recv_sems):
        barrier = pltpu.get_barrier_semaphore()
        devices = peers(*_place())
        for device in devices:
            pl.semaphore_signal(barrier, inc=1, device_id=device, device_id_type=MESH_ID)
        pl.semaphore_wait(barrier, len(devices))
        body(in_refs, out_refs, send_sems, recv_sems)

    launch()
    return [r[...] for r in out_refs]


def _allgather(shards, name, sequencer=None):
    out_types = [jax.ShapeDtypeStruct((N_CHIPS,) + s.shape, s.dtype) for s in shards]
    if sequencer is not None:
        sequencer = (sequencer, lambda x, y, c: _sibling(x, y, c) + _same_core_of_other_chips(x, y, c))
    return _exchange(_allgather_body, shards, out_types, 7 * len(shards), name, sequencer)


def _half_tile(half):
    return max(t for t in range(16, 1025, 16) if half % t == 0)


def _run_copies(cps):
    for cp in cps:
        cp.start()
    for cp in cps:
        cp.wait_recv()
    for cp in cps:
        cp.wait_send()


def _sibling_halves(gsends, name, sequencer=None):
    def body(g_refs, out_refs, send_sems, recv_sems):
        x, y, c = _place()
        cps = []
        for g, (g_ref, out_ref) in enumerate(zip(g_refs, out_refs)):
            half = g_ref.shape[1] // 2
            cps.append(pltpu.make_async_remote_copy(
                src_ref=g_ref.at[:, pl.ds((1 - c) * half, half), :], dst_ref=out_ref,
                send_sem=send_sems.at[g], recv_sem=recv_sems.at[g], device_id=(x, y, 1 - c), device_id_type=MESH_ID))
        _run_copies(cps)

    out_types = [jax.ShapeDtypeStruct((s.shape[0], s.shape[1] // 2, s.shape[2]), s.dtype) for s in gsends]
    return _exchange(body, gsends, out_types, len(gsends), name, sequencer and (sequencer, _sibling))


def _chip_sums(gsend, sib, place):
    n, rows, cols = gsend.shape
    half = rows // 2
    tm = _half_tile(half)
    nblk = half // tm

    def body(s_ref, g_ref, sib_ref, o_ref):
        o_ref[0] = (g_ref[0].astype(F32) + sib_ref[0].astype(F32)).astype(o_ref.dtype)

    grid_spec = pltpu.PrefetchScalarGridSpec(
        num_scalar_prefetch=1, grid=(n, nblk),
        in_specs=[pl.BlockSpec((1, tm, cols), lambda k, i, s: (jnp.bitwise_xor(s[0], k), s[1] * nblk + i, 0)),
                  pl.BlockSpec((1, tm, cols), lambda k, i, s: (jnp.bitwise_xor(s[0], k), i, 0))],
        out_specs=pl.BlockSpec((1, tm, cols), lambda k, i, s: (k, i, 0)))
    return pl.pallas_call(
        body, out_shape=jax.ShapeDtypeStruct((n, half, cols), BF16), grid_spec=grid_spec,
        name="rs_chip_sums", compiler_params=_params("parallel", "parallel"))(place, gsend, sib)


def _exchange_chip_sums(tsends, name, sequencer=None):
    def body(t_refs, out_refs, send_sems, recv_sems):
        x, y, c = _place()
        cps = []
        for g, (t_ref, out_ref) in enumerate(zip(t_refs, out_refs)):
            for k, device in enumerate(_same_core_of_other_chips(x, y, c)):
                cps.append(pltpu.make_async_remote_copy(
                    src_ref=t_ref.at[k + 1], dst_ref=out_ref.at[k], send_sem=send_sems.at[3 * g + k],
                    recv_sem=recv_sems.at[3 * g + k], device_id=device, device_id_type=MESH_ID))
        _run_copies(cps)

    out_types = [jax.ShapeDtypeStruct((3,) + s.shape[1:], s.dtype) for s in tsends]
    return _exchange(body, tsends, out_types, 3 * len(tsends), name,
                     sequencer and (sequencer, _same_core_of_other_chips))


def _final_sum(tsend, recv, place):
    n, half, cols = tsend.shape
    tm = _half_tile(half)
    nblk = half // tm

    def body(s_ref, t_ref, r_ref, o_ref):
        o_ref[...] = ((t_ref[0].astype(F32) + r_ref[0].astype(F32)) + r_ref[1].astype(F32)) + r_ref[2].astype(F32)

    grid_spec = pltpu.PrefetchScalarGridSpec(
        num_scalar_prefetch=1, grid=(nblk,),
        in_specs=[pl.BlockSpec((1, tm, cols), lambda i, s: (0, i, 0)), pl.BlockSpec((n - 1, tm, cols), lambda i, s: (0, i, 0))],
        out_specs=pl.BlockSpec((tm, cols), lambda i, s: (s[1] * nblk + i, 0)))
    return pl.pallas_call(
        body, out_shape=jax.ShapeDtypeStruct((2 * half, cols), F32), grid_spec=grid_spec, name="rs_final_sum",
        compiler_params=_params("parallel"))(place, tsend, recv)


def _join_halves(gfulls, name, sequencer=None):
    def body(g_refs, out_refs, send_sems, recv_sems):
        x, y, c = _place()
        n = len(g_refs)

        def copy(g, pc):
            half = g_refs[g].shape[0] // 2
            return pltpu.make_async_remote_copy(
                src_ref=g_refs[g].at[pl.ds(pc * half, half), :], dst_ref=out_refs[g].at[pl.ds(pc * half, half), :],
                send_sem=send_sems.at[g], recv_sem=recv_sems.at[g], device_id=(x, y, 1 - c), device_id_type=MESH_ID)

        mine = [copy(g, c) for g in range(n)]
        for cp in mine:
            cp.start()
        for g in range(n):
            copy(g, 1 - c).wait_recv()
        for cp in mine:
            cp.wait_send()

    return _exchange(body, gfulls, None, len(gfulls), name, sequencer and (sequencer, _sibling))


def _allreduce_small(v):
    rows, cols = v.shape

    def body(v_ref, out_ref, buf, send_sems, recv_sems):
        x, y, c = _place()
        cps = []
        for k in range(1, 8):
            peer = (_flip(x, (k >> 2) & 1), _flip(y, (k >> 1) & 1), _flip(c, k & 1))
            cps.append(pltpu.make_async_remote_copy(
                src_ref=v_ref, dst_ref=buf.at[k - 1], send_sem=send_sems.at[k - 1], recv_sem=recv_sems.at[k - 1],
                device_id=peer, device_id_type=MESH_ID))
        for cp in cps:
            cp.start()
        for cp in cps:
            cp.wait_recv()
        for cp in cps:
            cp.wait_send()
        t0 = v_ref[...] + buf[0]
        t1 = buf[1] + buf[2]
        t2 = buf[3] + buf[4]
        t3 = buf[5] + buf[6]
        out_ref[...] = (t0 + t1) + (t2 + t3)

    vm = pl.BlockSpec(memory_space=pltpu.VMEM)
    return pl.pallas_call(
        body, out_shape=jax.ShapeDtypeStruct((rows, cols), F32), in_specs=[vm], out_specs=vm,
        scratch_shapes=[pltpu.VMEM((7, rows, cols), F32), pltpu.SemaphoreType.DMA((7,)), pltpu.SemaphoreType.DMA((7,))],
        name="allreduce_small")(v)


BIG_INFO = {n: (shape, ax) for n, shape, ax in BIG}
GROUPS = (("w_in",), ("w_ffn_gate", "w_ffn_up"), ("w_out", "w_ffn_down", "w_ple_gate"),
          ("w_branch_a", "w_branch_b", "w_ple_proj"))


def _shard_shape(name):
    (k, m), ax = BIG_INFO[name]
    return (k // N_CHIPS, m) if ax == 0 else (k, m // N_CHIPS)


def _group_rows(group):
    offs, off = {}, 0
    for n in group:
        offs[n] = off
        off += _shard_shape(n)[0]
    return offs, off


def _pack_groups(shards, layer, dtype):
    return [jnp.concatenate([shards[n][layer].astype(dtype) for n in group], axis=0) for group in GROUPS]


def _unpack_full(gathered, groups):
    out = {}
    for group, arr in zip(groups, gathered):
        offs, _ = _group_rows(group)
        for n in group:
            rows, cols = _shard_shape(n)
            (k, m), ax = BIG_INFO[n]
            slab = arr[:, offs[n]:offs[n] + rows]
            out[n] = slab.reshape(k, m) if ax == 0 else jnp.transpose(slab, (1, 0, 2)).reshape(k, m)
    return out


def _pack_grads(gfull):
    out = []
    for group in GROUPS:
        parts = []
        for n in group:
            rows, cols = _shard_shape(n)
            ax = BIG_INFO[n][1]
            slab = (gfull[n].reshape(N_CHIPS, rows, cols) if ax == 0
                    else jnp.transpose(gfull[n].reshape(rows, N_CHIPS, cols), (1, 0, 2)))
            parts.append(slab.astype(BF16))
        out.append(jnp.concatenate(parts, axis=1))
    return out


def _after(values, mark):
    values, _ = lax.optimization_barrier((values, mark))
    return values


def _reduce_scatter(gsends, place, tag, sequencer_ids=None, hold=None):
    ids = sequencer_ids or (None, None, None)
    sibs = _sibling_halves(gsends, "rs_sibling_halves_" + tag, ids[0])
    tsends = [_chip_sums(g, s, place) for g, s in zip(gsends, sibs)]
    recvs = _exchange_chip_sums(tsends, "rs_exchange_" + tag, ids[1])
    if hold is not None:
        recvs = _after(recvs, hold)
    return _join_halves([_final_sum(t, r, place) for t, r in zip(tsends, recvs)], "rs_join_halves_" + tag, ids[2])


SMALL_SHAPES = {"rel_table": (NUM_BUCKETS, 2 * N_HEADS), "norm_mix_g": (DEPTH, D_MODEL), "qnorm_a_g": (DEPTH, HEAD_DIM),
                "knorm_a_g": (DEPTH, HEAD_DIM), "qnorm_b_g": (DEPTH, HEAD_DIM), "knorm_b_g": (DEPTH, HEAD_DIM),
                "sink_b": (DEPTH, N_HEADS), "norm_ffn_g": (DEPTH, D_MODEL), "norm_ple_g": (DEPTH, D_MODEL)}


def _pack_small(vals):
    flat = jnp.concatenate([vals[n].astype(F32).reshape(-1) for n in SMALL])
    flat = jnp.concatenate([flat, jnp.zeros((SMALL_ROWS * LANES - flat.shape[0],), F32)])
    return flat.reshape(SMALL_ROWS, LANES)


def _unpack_small(packed):
    flat, out, off = packed.reshape(-1), {}, 0
    for n in SMALL:
        size = math.prod(SMALL_SHAPES[n])
        out[n] = flat[off:off + size].reshape(SMALL_SHAPES[n])
        off += size
    return out


def _adamw(w, gs, g_row, m, v, name):
    c1 = 1.0 - ADAM_B1 ** ADAM_STEP
    c2 = 1.0 - ADAM_B2 ** ADAM_STEP
    total, width = w.shape
    n_layers = len(gs)
    per = total // n_layers
    tm = max(t for t in (512, 256, 128, 64, 32, 16, 8) if per % t == 0 and g_row % t == 0)
    nblk = per // tm

    def body(*refs):
        w_ref, g_refs = refs[0], refs[1:1 + n_layers]
        m_ref, v_ref, og, od, om, ov = refs[1 + n_layers:]
        layer = pl.program_id(0) // nblk
        g = g_refs[0][...]
        for l in range(1, n_layers):
            g = jnp.where(layer == l, g_refs[l][...], g)
        m_new = ADAM_B1 * m_ref[...] + (1.0 - ADAM_B1) * g
        v_new = ADAM_B2 * v_ref[...] + (1.0 - ADAM_B2) * (g * g)
        og[...] = g
        od[...] = -ADAM_LR * ((m_new / c1) / (jnp.sqrt(v_new / c2) + ADAM_EPS) + ADAM_WD * w_ref[...])
        om[...] = m_new
        ov[...] = v_new

    row = pl.BlockSpec((tm, width), lambda i: (i, 0))
    g_specs = [pl.BlockSpec((tm, width), lambda i, l=l: (g_row // tm + jnp.clip(i - l * nblk, 0, nblk - 1), 0))
               for l in range(n_layers)]
    return pl.pallas_call(
        body, out_shape=[jax.ShapeDtypeStruct((total, width), F32)] * 4, grid=(total // tm,),
        in_specs=[row] + g_specs + [row, row], out_specs=[row] * 4, name=name,
        compiler_params=_params("parallel"))(w, *gs, m, v)


def kernel(x, p, rel_table, norm_mix_g, w_in, qnorm_a_g, knorm_a_g, qnorm_b_g, knorm_b_g, sink_b, w_branch_a, w_branch_b, w_out, norm_ffn_g, w_ffn_gate, w_ffn_up, w_ffn_down, norm_ple_g, w_ple_gate, w_ple_proj, loss_target, m_rel_table, m_norm_mix_g, m_w_in, m_qnorm_a_g, m_knorm_a_g, m_qnorm_b_g, m_knorm_b_g, m_sink_b, m_w_branch_a, m_w_branch_b, m_w_out, m_norm_ffn_g, m_w_ffn_gate, m_w_ffn_up, m_w_ffn_down, m_norm_ple_g, m_w_ple_gate, m_w_ple_proj, v_rel_table, v_norm_mix_g, v_w_in, v_qnorm_a_g, v_knorm_a_g, v_qnorm_b_g, v_knorm_b_g, v_sink_b, v_w_branch_a, v_w_branch_b, v_w_out, v_norm_ffn_g, v_w_ffn_gate, v_w_ffn_up, v_w_ffn_down, v_norm_ple_g, v_w_ple_gate, v_w_ple_proj):
    given = dict(locals())
    weights = {n: given[n] for n in WEIGHTS}
    moments_m = {n: given["m_" + n] for n in WEIGHTS}
    moments_v = {n: given["v_" + n] for n in WEIGHTS}
    xi, yi, ci = _place()
    place = jnp.stack([2 * xi + yi, ci]).astype(jnp.int32)

    shards = [_pack_groups(weights, l, BF16) for l in range(DEPTH)]
    w_in0 = _allgather(shards[0][:1], "allgather_w_in_layer0")
    rest0 = _allgather(_after(shards[0][1:], w_in0), "allgather_rest_layer0", sequencer=1)
    gathered = [w_in0 + rest0, None]
    small = {n: weights[n] for n in SMALL}

    def w_in_of(l, mark):
        return _unpack_full(gathered[l][:1] if l == 0 else _after(gathered[l][:1], mark), GROUPS[:1])["w_in"]

    def rest_of(l, mark):
        if l == 0:
            gathered[1] = _allgather(_after(shards[1], mark), "allgather_layer1", sequencer=2)
        return _unpack_full(_after(gathered[l][1:], mark), GROUPS[1:])

    loss, dx, gbig, gsmall, marks = _local_step(x[0], p[:, 0], loss_target[0], w_in_of, rest_of, small)

    gsends = [_pack_grads(gbig[l]) for l in range(DEPTH)]
    red1 = _reduce_scatter(gsends[1], place, "layer1", (3, 4, 5), hold=marks[0]["ffn_bwd_done"])
    rest0 = _reduce_scatter(gsends[0][1:], place, "rest_layer0", (6, 7, 8), hold=marks[0]["attn_bwd_done"])
    greds = [_reduce_scatter(gsends[0][:1], place, "w_in_layer0") + rest0, red1]

    grads, delta, new_m, new_v = {}, {}, {}, {}
    for gi, group in enumerate(GROUPS):
        offs, _ = _group_rows(group)
        for n in group:
            shape = weights[n].shape
            two_d = lambda a: a.reshape(shape[0] * shape[1], shape[2])
            outs = _adamw(two_d(weights[n]), [greds[l][gi] for l in range(DEPTH)], offs[n], two_d(moments_m[n]),
                          two_d(moments_v[n]), "adamw_" + n)
            grads[n], delta[n], new_m[n], new_v[n] = (o.reshape(shape) for o in outs)
    small_grads = _allreduce_small(_pack_small(gsmall))
    g_, d_, m_, v_ = _adamw(_pack_small(weights), [small_grads], 0, _pack_small(moments_m), _pack_small(moments_v),
                            "adamw_small")
    grads.update(_unpack_small(g_))
    delta.update(_unpack_small(d_))
    new_m.update(_unpack_small(m_))
    new_v.update(_unpack_small(v_))

    loss = lax.psum(loss, ("x", "y", "c"))
    return (loss, dx[None], *[grads[n] for n in WEIGHTS], *[delta[n] for n in WEIGHTS],
            *[new_m[n] for n in WEIGHTS], *[new_v[n] for n in WEIGHTS])
```

```python
import functools
import math

import jax
import jax.numpy as jnp
from jax import lax
from jax.experimental import pallas as pl
from jax.experimental.pallas import tpu as pltpu
from jax.experimental.pallas import tpu_sc as plsc

F32 = jnp.float32
BF16 = jnp.bfloat16
MESH_ID = pl.DeviceIdType.MESH

SEQ = 2048
D_MODEL = 1024
DEPTH = 2
HEAD_DIM = 64
N_HEADS = 8
WIDTH = N_HEADS * HEAD_DIM
N_PAIRS = 4
ITEMS = 4
N_KV_B = 2
PLE_DIM = 256
D_FF = 2816
D_IN = 4352
OFF_QA, OFF_KA, OFF_VA, OFF_QB, OFF_KB, OFF_VB, OFF_GA, OFF_GB = 0, 512, 1024, 1536, 2048, 2176, 2304, 3328
DILATED = ((64, 1), (64, 4), (64, 16))
BLK_B = 128
NUM_BUCKETS = 32
MAX_DISTANCE = 1024
RMS_EPS = 1e-6
NEG_INF = -1e30
LANES = 128
VMEM_LIMIT = 48 * 1024 * 1024

ADAM_LR, ADAM_B1, ADAM_B2, ADAM_EPS, ADAM_WD, ADAM_STEP = 0.001, 0.9, 0.999, 1e-08, 0.01, 10

TRANSPOSED = ("w_in", "w_ffn_gate", "w_ffn_up")
BIG = (
    ("w_in", (D_IN, D_MODEL), 0),
    ("w_branch_a", (WIDTH, D_MODEL), 1),
    ("w_branch_b", (WIDTH, D_MODEL), 1),
    ("w_out", (D_MODEL, D_MODEL), 0),
    ("w_ffn_gate", (D_FF, D_MODEL), 0),
    ("w_ffn_up", (D_FF, D_MODEL), 0),
    ("w_ffn_down", (D_FF, D_MODEL), 0),
    ("w_ple_gate", (D_MODEL, D_MODEL), 0),
    ("w_ple_proj", (PLE_DIM, D_MODEL), 1),
)
SMALL = ("rel_table", "norm_mix_g", "qnorm_a_g", "knorm_a_g", "qnorm_b_g", "knorm_b_g", "sink_b",
         "norm_ffn_g", "norm_ple_g")
WEIGHTS = ("rel_table", "norm_mix_g", "w_in", "qnorm_a_g", "knorm_a_g", "qnorm_b_g", "knorm_b_g", "sink_b",
           "w_branch_a", "w_branch_b", "w_out", "norm_ffn_g", "w_ffn_gate", "w_ffn_up", "w_ffn_down",
           "norm_ple_g", "w_ple_gate", "w_ple_proj")
N_CHIPS = 4
SMALL_ROWS = 64


def _params(*sem):
    return pltpu.CompilerParams(dimension_semantics=sem, vmem_limit_bytes=VMEM_LIMIT)


def _pick(dim, target):
    for t in (target, 512, 256, 128, 64, 32, 16, 8):
        if t <= target and dim % t == 0:
            return t
    return dim


MM_VMEM_BUDGET = 40 * 1024 * 1024
STEP_OVERHEAD_S = 0.4e-6
TILE_DMA_BYTES_PER_S = 1.5e12
MXU_FLOPS_PER_S = 7e14


def _mm_tiles(m, n, k, a_bytes, b_bytes, out_bytes, has_res):
    best = None
    for tm in (t for t in range(LANES, m + 1, LANES) if m % t == 0):
        for tn in (t for t in range(LANES, n + 1, LANES) if n % t == 0):
            io = tm * k * a_bytes + tn * k * b_bytes + tm * tn * (out_bytes + (4 if has_res else 0))
            casts = (tm * k * 2 if a_bytes == 4 else 0) + (tn * k * 2 if b_bytes == 4 else 0)
            if 2 * io + tm * tn * 4 + casts > MM_VMEM_BUDGET:
                continue
            steps = (m // tm) * (n // tn)
            cost = steps * STEP_OVERHEAD_S + io / TILE_DMA_BYTES_PER_S + 2.0 * m * n * k / MXU_FLOPS_PER_S
            if best is None or (cost, -tm) < best[0]:
                best = ((cost, -tm), tm, tn)
    return best[1], best[2]


def _mm(a, b, mode, out_dtype, name, res=None):
    if mode == "nn":
        (m, k), (_, n) = a.shape, b.shape
    elif mode == "nt":
        (m, k), (n, _) = a.shape, b.shape
    else:
        (k, m), (_, n) = a.shape, b.shape
    tm, tn = _mm_tiles(m, n, k, a.dtype.itemsize, b.dtype.itemsize, jnp.dtype(out_dtype).itemsize, res is not None)
    a_spec = pl.BlockSpec((k, tm), lambda i, j: (0, i)) if mode == "tn" else pl.BlockSpec((tm, k), lambda i, j: (i, 0))
    b_spec = pl.BlockSpec((tn, k), lambda i, j: (j, 0)) if mode == "nt" else pl.BlockSpec((k, tn), lambda i, j: (0, j))
    dims = {"nn": (((1,), (0,)), ((), ())), "nt": (((1,), (1,)), ((), ())), "tn": (((0,), (0,)), ((), ()))}[mode]
    has_res = res is not None

    def body(*refs):
        a_ref, b_ref = refs[0], refs[1]
        o_ref = refs[-1]
        acc = lax.dot_general(a_ref[...].astype(BF16), b_ref[...].astype(BF16), dims, preferred_element_type=F32)
        if has_res:
            acc = acc + refs[2][...]
        o_ref[...] = acc.astype(out_dtype)

    in_specs = [a_spec, b_spec]
    args = [a, b]
    if has_res:
        in_specs.append(pl.BlockSpec((tm, tn), lambda i, j: (i, j)))
        args.append(res)
    return pl.pallas_call(
        body, out_shape=jax.ShapeDtypeStruct((m, n), out_dtype), grid=(m // tm, n // tn), in_specs=in_specs,
        out_specs=pl.BlockSpec((tm, tn), lambda i, j: (i, j)), name=name,
        compiler_params=_params("parallel", "parallel"))(*args)


def _ew(fn, ins, out_dtypes, *, width, bw, name, vecs=(), tm=256):
    rows = ins[0][0].shape[0]
    tm = _pick(rows, tm)
    n_in = len(ins) + len(vecs)

    def col_map(off_blocks):
        return lambda i, j: (i, off_blocks + j)

    in_specs = [pl.BlockSpec((tm, bw), col_map(off // bw)) for _, off in ins]
    in_specs += [pl.BlockSpec((1, bw), lambda i, j: (0, j)) for _ in vecs]

    def body(*refs):
        outs = fn(*[r[...] for r in refs[:n_in]])
        for r, o in zip(refs[n_in:], outs):
            r[...] = o.astype(r.dtype)

    return pl.pallas_call(
        body, out_shape=[jax.ShapeDtypeStruct((rows, width), dt) for dt in out_dtypes],
        grid=(rows // tm, width // bw), in_specs=in_specs,
        out_specs=[pl.BlockSpec((tm, bw), lambda i, j: (i, j)) for _ in out_dtypes],
        name=name, compiler_params=_params("parallel", "parallel"))(*[a for a, _ in ins], *vecs)


def _sigmoid(x):
    return 1.0 / (1.0 + jnp.exp(-x))


def _seg_sum(v):
    outs = []
    for k in range(v.shape[1] // LANES):
        vp = v[:, k * LANES:(k + 1) * LANES]
        left = lax.broadcasted_iota(jnp.int32, vp.shape, 1) < HEAD_DIM
        sl = jnp.sum(jnp.where(left, vp, 0.0), axis=-1, keepdims=True)
        sr = jnp.sum(jnp.where(left, 0.0, vp), axis=-1, keepdims=True)
        outs.append(jnp.where(left, sl, sr))
    return outs[0] if len(outs) == 1 else jnp.concatenate(outs, axis=1)


def _seg_rstd(x):
    return lax.rsqrt(_seg_sum(x * x) * (1.0 / HEAD_DIM) + RMS_EPS)


def _rms_fwd(x, g, name):
    rows, d = x.shape
    tm = 256

    def body(x_ref, g_ref, h_ref):
        xv = x_ref[...]
        r = lax.rsqrt(jnp.mean(xv * xv, axis=-1, keepdims=True) + RMS_EPS)
        h_ref[...] = ((xv * r) * g_ref[...]).astype(BF16)

    return pl.pallas_call(
        body, out_shape=jax.ShapeDtypeStruct((rows, d), BF16), grid=(rows // tm,),
        in_specs=[pl.BlockSpec((tm, d), lambda i: (i, 0)), pl.BlockSpec((1, d), lambda i: (0, 0))],
        out_specs=pl.BlockSpec((tm, d), lambda i: (i, 0)), name=name, compiler_params=_params("parallel"))(x, g)


def _rms_bwd(x, g, dh, dres, name):
    rows, d = x.shape
    tm = 256

    def body(x_ref, g_ref, dh_ref, dres_ref, dx_ref, dxb_ref, dg_ref):
        xv = x_ref[...]
        r = lax.rsqrt(jnp.mean(xv * xv, axis=-1, keepdims=True) + RMS_EPS)
        xh = xv * r
        dhv = dh_ref[...]
        dxh = dhv * g_ref[...]
        dxv = dres_ref[...] + r * (dxh - xh * jnp.mean(dxh * xh, axis=-1, keepdims=True))
        dx_ref[...] = dxv
        dxb_ref[...] = dxv.astype(BF16)
        part = jnp.sum(dhv * xh, axis=0, keepdims=True)

        @pl.when(pl.program_id(0) == 0)
        def _():
            dg_ref[...] = part

        @pl.when(pl.program_id(0) > 0)
        def _():
            dg_ref[...] += part

    row = pl.BlockSpec((tm, d), lambda i: (i, 0))
    vec = pl.BlockSpec((1, d), lambda i: (0, 0))
    return pl.pallas_call(
        body, out_shape=[jax.ShapeDtypeStruct((rows, d), F32), jax.ShapeDtypeStruct((rows, d), BF16),
                         jax.ShapeDtypeStruct((1, d), F32)],
        grid=(rows // tm,), in_specs=[row, vec, row, row], out_specs=[row, row, vec],
        name=name, compiler_params=_params("arbitrary"))(x, g, dh, dres)


def _loss_grad(y, t):
    rows, d = y.shape
    tm = 256

    def body(y_ref, t_ref, dy_ref, l_ref):
        e = y_ref[...] - t_ref[...]
        dy_ref[...] = e * (1.0 / d)
        part = jnp.zeros((1, LANES), F32) + jnp.sum(e * e) * (0.5 / d)

        @pl.when(pl.program_id(0) == 0)
        def _():
            l_ref[...] = part

        @pl.when(pl.program_id(0) > 0)
        def _():
            l_ref[...] += part

    row = pl.BlockSpec((tm, d), lambda i: (i, 0))
    return pl.pallas_call(
        body, out_shape=[jax.ShapeDtypeStruct((rows, d), F32), jax.ShapeDtypeStruct((1, LANES), F32)],
        grid=(rows // tm,), in_specs=[row, row], out_specs=[row, pl.BlockSpec((1, LANES), lambda i: (0, 0))],
        name="loss_grad", compiler_params=_params("arbitrary"))(y, t)


def _put_pairs(ref, val):
    for hp in range(N_PAIRS):
        ref[hp] = val[:, hp * LANES:(hp + 1) * LANES].astype(ref.dtype)


def _get_pairs(ref):
    return jnp.concatenate([ref[hp] for hp in range(N_PAIRS)], axis=1)


def _swap_halves(v):
    return pltpu.roll(v, HEAD_DIM, axis=1)


def _expand_kv(kv):
    left = lax.broadcasted_iota(jnp.int32, kv.shape, 1) < HEAD_DIM
    sw = _swap_halves(kv)
    h0 = jnp.where(left, kv, sw)
    h1 = jnp.where(left, sw, kv)
    return jnp.concatenate([h0, h0, h1, h1], axis=1)


def _reduce_kv(dkv):
    left = lax.broadcasted_iota(jnp.int32, (dkv.shape[0], LANES), 1) < HEAD_DIM
    t = dkv[:, 0:LANES] + dkv[:, LANES:2 * LANES]
    u = dkv[:, 2 * LANES:3 * LANES] + dkv[:, 3 * LANES:4 * LANES]
    t = t + _swap_halves(t)
    u = u + _swap_halves(u)
    return jnp.where(left, t, u)


def _qknorm_fwd(proj, gqa, gka, gqb, gkb):
    rows = proj.shape[0]
    tm = 256

    def body(qa_ref, ka_ref, va_ref, qb_ref, kb_ref, vb_ref, gqa_ref, gka_ref, gqb_ref, gkb_ref,
             oqa, oka, ova, oqb, okb, ovb):
        for src, g_ref, dst in ((qa_ref, gqa_ref, oqa), (ka_ref, gka_ref, oka), (qb_ref, gqb_ref, oqb)):
            xv = src[...]
            _put_pairs(dst, (xv * _seg_rstd(xv)) * g_ref[...])
        _put_pairs(ova, va_ref[...])
        kv = kb_ref[...]
        _put_pairs(okb, _expand_kv((kv * _seg_rstd(kv)) * gkb_ref[...]))
        _put_pairs(ovb, _expand_kv(vb_ref[...]))

    def win(width, off):
        return pl.BlockSpec((tm, width), lambda i: (i, off // width))

    vec = lambda w: pl.BlockSpec((1, w), lambda i: (0, 0))
    out = pl.BlockSpec((N_PAIRS, tm, LANES), lambda i: (0, i, 0))
    return pl.pallas_call(
        body, out_shape=[jax.ShapeDtypeStruct((N_PAIRS, rows, LANES), F32)] * 6, grid=(rows // tm,),
        in_specs=[win(WIDTH, OFF_QA), win(WIDTH, OFF_KA), win(WIDTH, OFF_VA), win(WIDTH, OFF_QB),
                  win(LANES, OFF_KB), win(LANES, OFF_VB), vec(WIDTH), vec(WIDTH), vec(WIDTH), vec(LANES)],
        out_specs=[out] * 6, name="qknorm_fwd", compiler_params=_params("parallel"))(
            proj, proj, proj, proj, proj, proj, gqa, gka, gqb, gkb)


def _norm_bwd(xv, g, dy):
    r = _seg_rstd(xv)
    xh = xv * r
    dxh = dy * g
    dx = r * (dxh - xh * (_seg_sum(dxh * xh) * (1.0 / HEAD_DIM)))
    return dx, jnp.sum(dy * xh, axis=0, keepdims=True)


def _qknorm_bwd(proj, gqa, gka, gqb, gkb, dqa, dka, dva, dqb, dkb, dvb, dgab):
    rows = proj.shape[0]
    tm = 256
    n_a = len(dqa)

    def body(*refs):
        qa_ref, ka_ref, qb_ref, kb_ref, gqa_ref, gka_ref, gqb_ref, gkb_ref = refs[:8]
        pos = 8
        dqa_refs, dka_refs, dva_refs = refs[pos:pos + n_a], refs[pos + n_a:pos + 2 * n_a], refs[pos + 2 * n_a:pos + 3 * n_a]
        pos += 3 * n_a
        dqb_ref, dkb_ref, dvb_ref, dgab_ref = refs[pos:pos + 4]
        dproj_ref, ogqa, ogka, ogqb, ogkb = refs[pos + 4:]

        def total(rs):
            acc = _get_pairs(rs[0])
            for r in rs[1:]:
                acc = acc + _get_pairs(r)
            return acc

        dx_qa, p_qa = _norm_bwd(qa_ref[...], gqa_ref[...], total(dqa_refs))
        dx_ka, p_ka = _norm_bwd(ka_ref[...], gka_ref[...], total(dka_refs))
        dx_qb, p_qb = _norm_bwd(qb_ref[...], gqb_ref[...], _get_pairs(dqb_ref))
        dx_kb, p_kb = _norm_bwd(kb_ref[...], gkb_ref[...], _reduce_kv(_get_pairs(dkb_ref)))
        dproj_ref[:, OFF_QA:OFF_QA + WIDTH] = dx_qa.astype(BF16)
        dproj_ref[:, OFF_KA:OFF_KA + WIDTH] = dx_ka.astype(BF16)
        dproj_ref[:, OFF_VA:OFF_VA + WIDTH] = total(dva_refs).astype(BF16)
        dproj_ref[:, OFF_QB:OFF_QB + WIDTH] = dx_qb.astype(BF16)
        dproj_ref[:, OFF_KB:OFF_KB + LANES] = dx_kb.astype(BF16)
        dproj_ref[:, OFF_VB:OFF_VB + LANES] = _reduce_kv(_get_pairs(dvb_ref)).astype(BF16)
        dproj_ref[:, OFF_GA:D_IN] = dgab_ref[...]
        first = pl.program_id(0) == 0
        for o_ref, part in ((ogqa, p_qa), (ogka, p_ka), (ogqb, p_qb), (ogkb, p_kb)):
            @pl.when(first)
            def _(o_ref=o_ref, part=part):
                o_ref[...] = part

            @pl.when(jnp.logical_not(first))
            def _(o_ref=o_ref, part=part):
                o_ref[...] += part

    def win(width, off):
        return pl.BlockSpec((tm, width), lambda i: (i, off // width))

    vec = lambda w: pl.BlockSpec((1, w), lambda i: (0, 0))
    row = lambda w: pl.BlockSpec((tm, w), lambda i: (i, 0))
    in_specs = [win(WIDTH, OFF_QA), win(WIDTH, OFF_KA), win(WIDTH, OFF_QB), win(LANES, OFF_KB),
                vec(WIDTH), vec(WIDTH), vec(WIDTH), vec(LANES)]
    in_specs += [pl.BlockSpec((N_PAIRS, tm, LANES), lambda i: (0, i, 0))] * (3 * n_a + 3) + [row(2 * D_MODEL)]
    return pl.pallas_call(
        body,
        out_shape=[jax.ShapeDtypeStruct((rows, D_IN), BF16), jax.ShapeDtypeStruct((1, WIDTH), F32),
                   jax.ShapeDtypeStruct((1, WIDTH), F32), jax.ShapeDtypeStruct((1, WIDTH), F32),
                   jax.ShapeDtypeStruct((1, LANES), F32)],
        grid=(rows // tm,), in_specs=in_specs,
        out_specs=[row(D_IN), vec(WIDTH), vec(WIDTH), vec(WIDTH), vec(LANES)],
        name="qknorm_bwd", compiler_params=_params("arbitrary"))(
            proj, proj, proj, proj, gqa, gka, gqb, gkb, *dqa, *dka, *dva, dqb, dkb, dvb, dgab)


def _t5_bucket(rel):
    half_b = NUM_BUCKETS // 2
    max_exact = half_b // 2
    sign = jnp.where(rel > 0, half_b, 0)
    n = jnp.abs(rel)
    nf = jnp.maximum(n, 1).astype(F32)
    large = max_exact + (jnp.log(nf / max_exact) / math.log(MAX_DISTANCE / max_exact)
                         * (half_b - max_exact)).astype(jnp.int32)
    large = jnp.minimum(large, half_b - 1)
    return sign + jnp.where(n < max_exact, n, large)


def _band_buckets(blk, dilation):
    i = jnp.arange(blk, dtype=jnp.int32)[:, None]
    j = jnp.arange(3 * blk, dtype=jnp.int32)[None, :]
    rel = j - blk - i
    return jnp.where(jnp.abs(rel) <= blk, _t5_bucket(rel * dilation), -1)


def _bias_tiles(table, buckets, head_off, name):
    blk = buckets.shape[0]

    def body(tab_ref, bk_ref, o_ref):
        h = pl.program_id(0) + head_off
        bk = bk_ref[...]
        acc = jnp.full(bk.shape, NEG_INF, F32)
        for b in range(NUM_BUCKETS):
            acc = jnp.where(bk == b, tab_ref[b, h], acc)
        o_ref[0] = acc

    return pl.pallas_call(
        body, out_shape=jax.ShapeDtypeStruct((N_HEADS, blk, 3 * blk), F32), grid=(N_HEADS,),
        in_specs=[pl.BlockSpec(memory_space=pltpu.SMEM), pl.BlockSpec((blk, 3 * blk), lambda h: (0, 0))],
        out_specs=pl.BlockSpec((1, blk, 3 * blk), lambda h: (h, 0, 0)),
        name=name, compiler_params=_params("parallel"))(table, buckets)


def _table_grad(dbias, buckets, name):
    blk = buckets.shape[0]

    def body(db_ref, bk_ref, o_ref):
        bk = bk_ref[...]
        dbv = db_ref[0]
        lane = lax.broadcasted_iota(jnp.int32, (1, LANES), 1)
        acc = jnp.zeros((1, LANES), F32)
        for b in range(NUM_BUCKETS):
            acc = jnp.where(lane == b, jnp.sum(jnp.where(bk == b, dbv, 0.0)), acc)
        o_ref[0] = acc

    out = pl.pallas_call(
        body, out_shape=jax.ShapeDtypeStruct((N_HEADS, 1, LANES), F32), grid=(N_HEADS,),
        in_specs=[pl.BlockSpec((1, blk, 3 * blk), lambda h: (h, 0, 0)), pl.BlockSpec((blk, 3 * blk), lambda h: (0, 0))],
        out_specs=pl.BlockSpec((1, 1, LANES), lambda h: (h, 0, 0)),
        name=name, compiler_params=_params("parallel"))(dbias, buckets)
    return out[:, 0, :NUM_BUCKETS]


def _dot_nt(a, b):
    return lax.dot_general(a, b, (((1,), (1,)), ((), ())), preferred_element_type=F32)


def _stack_pair(x2, left):
    return jnp.concatenate([jnp.where(left, x2, 0.0), jnp.where(left, 0.0, x2)], axis=0).astype(BF16)


def _attn_geometry(blk, d):
    chunk = blk * ITEMS if d == 1 else blk * d
    groups = 1 if d == 1 else d // ITEMS
    halo = blk if d == 1 else chunk
    return chunk, groups, halo


def _item_rows(ref, j, r0, blk, d):
    if d == 1:
        return ref[j * blk:(j + 1) * blk, :]
    return ref[pl.ds(r0 + j, blk, stride=d), :]


def _item_penalty(t, nct, j, blk, d):
    first_ok, last_ok = t > 0, t < nct - 1
    if d == 1:
        first_ok = True if j > 0 else first_ok
        last_ok = True if j < ITEMS - 1 else last_ok
    col = lax.broadcasted_iota(jnp.int32, (1, 3 * blk), 1)
    ok = jnp.logical_and(jnp.logical_or(col >= blk, first_ok), jnp.logical_or(col < 2 * blk, last_ok))
    return jnp.where(ok, 0.0, NEG_INF).astype(F32)


def _attn_specs(seq, blk, d, step_of):
    chunk, _, halo = _attn_geometry(blk, d)
    per, last = chunk // halo, seq // halo - 1
    cur = pl.BlockSpec((None, chunk, LANES), lambda hp, t: (hp, step_of(t), 0))
    prev = pl.BlockSpec((None, halo, LANES), lambda hp, t: (hp, jnp.clip(step_of(t) * per - 1, 0, last), 0))
    nxt = pl.BlockSpec((None, halo, LANES), lambda hp, t: (hp, jnp.minimum((step_of(t) + 1) * per, last), 0))
    return cur, prev, nxt


def _attn_fwd(q, k, v, bias, sink, blk, d, name):
    _, seq, _ = q.shape
    chunk, groups, _ = _attn_geometry(blk, d)
    nct = seq // chunk
    has_sink = sink is not None
    scale = HEAD_DIM ** -0.5

    def body(*refs):
        q_ref, kp, kc, kn, vp, vc, vn, b_ref = refs[:8]
        s_ref = refs[8] if has_sink else None
        o_ref, l_ref = refs[-2], refs[-1]
        t = pl.program_id(1)
        left = lax.broadcasted_iota(jnp.int32, (1, LANES), 1) < HEAD_DIM
        bias2 = b_ref[...]
        if d == 1:
            kwin = jnp.concatenate([kp[...], kc[...], kn[...]], axis=0).astype(BF16)
            vwin = jnp.concatenate([vp[...], vc[...], vn[...]], axis=0).astype(BF16)

        def group(r0):
            scores, vcats = [], []
            for j in range(ITEMS):
                qs = _stack_pair(_item_rows(q_ref, j, r0, blk, d) * scale, left)
                if d == 1:
                    kcat, vcat = kwin[j * blk:(j + 3) * blk], vwin[j * blk:(j + 3) * blk]
                else:
                    kcat = jnp.concatenate([_item_rows(r, j, r0, blk, d) for r in (kp, kc, kn)], axis=0).astype(BF16)
                    vcat = jnp.concatenate([_item_rows(r, j, r0, blk, d) for r in (vp, vc, vn)], axis=0).astype(BF16)
                scores.append(_dot_nt(qs, kcat) + bias2 + _item_penalty(t, nct, j, blk, d))
                vcats.append(vcat)
            s = jnp.concatenate(scores, axis=0)
            m = jnp.max(s, axis=-1, keepdims=True)
            if has_sink:
                sk = jnp.concatenate([s_ref[...]] * ITEMS, axis=0)
                m = jnp.maximum(m, sk)
            p = jnp.exp(s - m)
            den = jnp.sum(p, axis=-1, keepdims=True)
            if has_sink:
                den = den + jnp.exp(sk - m)
            pn = (p * (1.0 / den)).astype(BF16)
            lse = m + jnp.log(den)
            for j in range(ITEMS):
                top, mid, bot = 2 * j * blk, (2 * j + 1) * blk, (2 * j + 2) * blk
                o2 = jnp.dot(pn[top:bot], vcats[j], preferred_element_type=F32)
                o_val = jnp.where(left, o2[:blk], o2[blk:])
                l_val = jnp.where(left, lse[top:mid], lse[mid:bot])
                if d == 1:
                    o_ref[j * blk:(j + 1) * blk, :] = o_val
                    l_ref[j * blk:(j + 1) * blk, :] = l_val
                else:
                    o_ref[pl.ds(r0 + j, blk, stride=d), :] = o_val
                    l_ref[pl.ds(r0 + j, blk, stride=d), :] = l_val

        if groups == 1:
            group(0)
        else:
            def step(g, carry):
                group(g * ITEMS)
                return carry

            lax.fori_loop(0, groups, step, 0)

    cur, prev, nxt = _attn_specs(seq, blk, d, lambda t: t)
    in_specs = [cur, prev, cur, nxt, prev, cur, nxt, pl.BlockSpec((2 * blk, 3 * blk), lambda hp, t: (hp, 0))]
    args = [q, k, k, k, v, v, v, bias]
    if has_sink:
        in_specs.append(pl.BlockSpec((2 * blk, 1), lambda hp, t: (hp, 0)))
        args.append(sink)
    return pl.pallas_call(
        body, out_shape=[jax.ShapeDtypeStruct(q.shape, F32)] * 2, grid=(N_PAIRS, nct),
        in_specs=in_specs, out_specs=[cur, cur], name=name, compiler_params=_params("parallel", "parallel"))(*args)


def _attn_bwd(q, k, v, do, lse, delta, bias, sink, blk, d, name):
    _, seq, _ = q.shape
    chunk, groups, halo = _attn_geometry(blk, d)
    nct = seq // chunk
    has_sink = sink is not None
    n_in = 12 if has_sink else 11
    scale = HEAD_DIM ** -0.5

    def body(*refs):
        q_ref, kp, kc, kn, vp, vc, vn, do_ref, l_ref, d_ref, b_ref = refs[:11]
        s_ref = refs[11] if has_sink else None
        dq_ref, dk_ref, dv_ref, db_ref = refs[n_in:n_in + 4]
        ds_ref = refs[n_in + 4] if has_sink else None
        wk, wv = refs[-2], refs[-1]
        t = pl.program_id(1)

        @pl.when(t == 0)
        def _():
            wk[...] = jnp.zeros_like(wk)
            wv[...] = jnp.zeros_like(wv)
            db_ref[...] = jnp.zeros_like(db_ref)
            if has_sink:
                ds_ref[...] = jnp.zeros_like(ds_ref)

        @pl.when(t > 0)
        def _():
            for w in (wk, wv):
                keep = w[chunk:2 * chunk + halo]
                w[0:chunk + halo] = keep
                w[chunk + halo:2 * chunk + halo] = jnp.zeros((chunk, LANES), F32)

        @pl.when(t < nct)
        def _():
            lane = lax.broadcasted_iota(jnp.int32, (1, LANES), 1)
            left = lane < HEAD_DIM
            bias2 = b_ref[...]
            if d == 1:
                kwin = jnp.concatenate([kp[...], kc[...], kn[...]], axis=0).astype(BF16)
                vwin = jnp.concatenate([vp[...], vc[...], vn[...]], axis=0).astype(BF16)

            def group(r0):
                qss, doss, kcats, scores, dps, lcols, dcols = [], [], [], [], [], [], []
                for j in range(ITEMS):
                    qs = _stack_pair(_item_rows(q_ref, j, r0, blk, d) * scale, left)
                    dos = _stack_pair(_item_rows(do_ref, j, r0, blk, d), left)
                    if d == 1:
                        kcat, vcat = kwin[j * blk:(j + 3) * blk], vwin[j * blk:(j + 3) * blk]
                    else:
                        kcat = jnp.concatenate([_item_rows(r, j, r0, blk, d) for r in (kp, kc, kn)], axis=0).astype(BF16)
                        vcat = jnp.concatenate([_item_rows(r, j, r0, blk, d) for r in (vp, vc, vn)], axis=0).astype(BF16)
                    l2, d2 = _item_rows(l_ref, j, r0, blk, d), _item_rows(d_ref, j, r0, blk, d)
                    lcols.append(jnp.max(jnp.where(left, l2, NEG_INF), axis=-1, keepdims=True))
                    lcols.append(jnp.max(jnp.where(left, NEG_INF, l2), axis=-1, keepdims=True))
                    dcols.append(jnp.sum(jnp.where(lane == 0, d2, 0.0), axis=-1, keepdims=True))
                    dcols.append(jnp.sum(jnp.where(lane == HEAD_DIM, d2, 0.0), axis=-1, keepdims=True))
                    scores.append(_dot_nt(qs, kcat) + bias2 + _item_penalty(t, nct, j, blk, d))
                    dps.append(_dot_nt(dos, vcat))
                    qss.append(qs)
                    doss.append(dos)
                    kcats.append(kcat)
                lcol = jnp.concatenate(lcols, axis=0)
                dcol = jnp.concatenate(dcols, axis=0)
                p = jnp.exp(jnp.concatenate(scores, axis=0) - lcol)
                ds = p * (jnp.concatenate(dps, axis=0) - dcol)
                if has_sink:
                    sgrad = dcol * jnp.exp(jnp.concatenate([s_ref[...]] * ITEMS, axis=0) - lcol)
                for j in range(ITEMS):
                    top, bot = 2 * j * blk, (2 * j + 2) * blk
                    dsj, pj = ds[top:bot], p[top:bot]
                    db_ref[...] += dsj
                    if has_sink:
                        ds_ref[...] -= sgrad[top:bot]
                    dq2 = jnp.dot(dsj.astype(BF16), kcats[j], preferred_element_type=F32) * scale
                    dq_val = jnp.where(left, dq2[:blk], dq2[blk:])
                    if d == 1:
                        dq_ref[j * blk:(j + 1) * blk, :] = dq_val
                    else:
                        dq_ref[pl.ds(r0 + j, blk, stride=d), :] = dq_val
                    dk_new = jnp.dot(jnp.transpose(dsj).astype(BF16), qss[j], preferred_element_type=F32)
                    dv_new = jnp.dot(jnp.transpose(pj).astype(BF16), doss[j], preferred_element_type=F32)
                    for w, new in ((wk, dk_new), (wv, dv_new)):
                        if d == 1:
                            w[chunk + (j - 1) * blk:chunk + (j + 2) * blk, :] += new
                        else:
                            for c in range(3):
                                w[pl.ds(c * chunk + r0 + j, blk, stride=d), :] += new[c * blk:(c + 1) * blk]

            if groups == 1:
                group(0)
            else:
                def step(g, carry):
                    group(g * ITEMS)
                    return carry

                lax.fori_loop(0, groups, step, 0)

        dk_ref[...] = wk[0:chunk]
        dv_ref[...] = wv[0:chunk]

    cur, prev, nxt = _attn_specs(seq, blk, d, lambda t: jnp.minimum(t, nct - 1))
    lag = pl.BlockSpec((None, chunk, LANES), lambda hp, t: (hp, jnp.maximum(t - 1, 0), 0))
    band = pl.BlockSpec((2 * blk, 3 * blk), lambda hp, t: (hp, 0))
    col = pl.BlockSpec((2 * blk, 1), lambda hp, t: (hp, 0))
    in_specs = [cur, prev, cur, nxt, prev, cur, nxt, cur, cur, cur, band]
    args = [q, k, k, k, v, v, v, do, lse, delta, bias]
    out_shape = [jax.ShapeDtypeStruct(q.shape, F32)] * 3 + [jax.ShapeDtypeStruct((N_HEADS * blk, 3 * blk), F32)]
    out_specs = [cur, lag, lag, band]
    if has_sink:
        in_specs.append(col)
        args.append(sink)
        out_shape.append(jax.ShapeDtypeStruct((N_HEADS * blk, 1), F32))
        out_specs.append(col)
    window = pltpu.VMEM((2 * chunk + halo, LANES), F32)
    return pl.pallas_call(
        body, out_shape=out_shape, grid=(N_PAIRS, nct + 1), in_specs=in_specs, out_specs=out_specs,
        scratch_shapes=[window, window], name=name, compiler_params=_params("arbitrary", "arbitrary"))(*args)


def _combine_patterns(outs, lses):
    _, rows, _ = outs[0].shape
    tm = 256
    n = len(outs)

    def body(*refs):
        o_refs, l_refs = refs[:n], refs[n:2 * n]
        y_ref, lse_ref = refs[2 * n], refs[2 * n + 1]
        for hp in range(N_PAIRS):
            ls = [r[hp] for r in l_refs]
            m = functools.reduce(jnp.maximum, ls)
            es = [jnp.exp(l - m) for l in ls]
            den = functools.reduce(lambda a, b: a + b, es)
            num = functools.reduce(lambda a, b: a + b, [e * r[hp] for e, r in zip(es, o_refs)])
            y_ref[:, hp * LANES:(hp + 1) * LANES] = num / den
            lse_ref[hp] = m + jnp.log(den)

    pm = pl.BlockSpec((N_PAIRS, tm, LANES), lambda i: (0, i, 0))
    return pl.pallas_call(
        body, out_shape=[jax.ShapeDtypeStruct((rows, WIDTH), F32), jax.ShapeDtypeStruct((N_PAIRS, rows, LANES), F32)],
        grid=(rows // tm,), in_specs=[pm] * (2 * n), out_specs=[pl.BlockSpec((tm, WIDTH), lambda i: (i, 0)), pm],
        name="combine_a", compiler_params=_params("parallel"))(*outs, *lses)


def _pairs_to_tokens(a):
    _, rows, _ = a.shape
    tm = 256

    def body(a_ref, o_ref):
        o_ref[...] = _get_pairs(a_ref)

    return pl.pallas_call(
        body, out_shape=jax.ShapeDtypeStruct((rows, WIDTH), a.dtype), grid=(rows // tm,),
        in_specs=[pl.BlockSpec((N_PAIRS, tm, LANES), lambda i: (0, i, 0))],
        out_specs=pl.BlockSpec((tm, WIDTH), lambda i: (i, 0)), name="pairs_to_tokens",
        compiler_params=_params("parallel"))(a)


def _attn_bwd_prep(dy, y, name):
    rows = dy.shape[0]
    tm = 256

    def body(dy_ref, y_ref, do_ref, dl_ref):
        dyv = dy_ref[...]
        _put_pairs(do_ref, dyv)
        _put_pairs(dl_ref, _seg_sum(dyv * y_ref[...]))

    tok = pl.BlockSpec((tm, WIDTH), lambda i: (i, 0))
    pm = pl.BlockSpec((N_PAIRS, tm, LANES), lambda i: (0, i, 0))
    return pl.pallas_call(
        body, out_shape=[jax.ShapeDtypeStruct((N_PAIRS, rows, LANES), F32)] * 2, grid=(rows // tm,),
        in_specs=[tok, tok], out_specs=[pm, pm], name=name, compiler_params=_params("parallel"))(dy, y)


def _tile_gain(g, reps):
    return jnp.tile(g[None, :], (1, reps))


def _local_step(x, p, target, w_in_of, rest_of, small):
    rel_table = small["rel_table"]
    buckets_a = [_band_buckets(blk, d) for blk, d in DILATED]
    buckets_b = _band_buckets(BLK_B, 1)
    bias_a = [_bias_tiles(rel_table, bk, 0, "bias_a").reshape(N_HEADS * bk.shape[0], -1) for bk in buckets_a]
    bias_b = _bias_tiles(rel_table, buckets_b, N_HEADS, "bias_b").reshape(N_HEADS * BLK_B, -1)

    saved = []
    for l in range(DEPTH):
        g_mix, g_ffn, g_ple = (small[n][l][None, :] for n in ("norm_mix_g", "norm_ffn_g", "norm_ple_g"))
        gqa, gka, gqb = (_tile_gain(small[n][l], N_HEADS) for n in ("qnorm_a_g", "knorm_a_g", "qnorm_b_g"))
        gkb = _tile_gain(small["knorm_b_g"][l], N_KV_B)
        sink = jnp.repeat(small["sink_b"][l], BLK_B)[:, None]

        h = _rms_fwd(x, g_mix, "rms_mix")
        w_in = w_in_of(l, h)
        proj = _mm(h, w_in, "nt", F32, "mm_in")
        qa, ka, va, qb, kb, vb = _qknorm_fwd(proj, gqa, gka, gqb, gkb)
        outs, lses = [], []
        for (blk, d), bias in zip(DILATED, bias_a):
            o, ls = _attn_fwd(qa, ka, va, bias, None, blk, d, f"attn_a{d}_fwd")
            outs.append(o)
            lses.append(ls)
        ya, lse_a = _combine_patterns(outs, lses)
        yb, lse_b = _attn_fwd(qb, kb, vb, bias_b, sink, BLK_B, 1, "attn_b_fwd")
        yb = _pairs_to_tokens(yb)
        w = dict(rest_of(l, yb), w_in=w_in)
        ca = _mm(ya, w["w_branch_a"], "nn", F32, "mm_branch_a")
        cb = _mm(yb, w["w_branch_b"], "nn", F32, "mm_branch_b")

        def gate(ca_, cb_, ga_, gb_):
            return (_sigmoid(ga_) * ca_ + _sigmoid(gb_) * cb_,)

        (merged,) = _ew(gate, [(ca, 0), (cb, 0), (proj, OFF_GA), (proj, OFF_GB)], [BF16],
                        width=D_MODEL, bw=256, name="gate")
        x1 = _mm(merged, w["w_out"], "nn", F32, "mm_out", res=x)

        h2 = _rms_fwd(x1, g_ffn, "rms_ffn")
        a = _mm(h2, w["w_ffn_gate"], "nt", F32, "mm_ffn_gate")
        u = _mm(h2, w["w_ffn_up"], "nt", F32, "mm_ffn_up")

        def swiglu(a_, u_):
            return ((a_ * _sigmoid(a_)) * u_,)

        (hid,) = _ew(swiglu, [(a, 0), (u, 0)], [BF16], width=D_FF, bw=D_FF, name="swiglu")
        x2 = _mm(hid, w["w_ffn_down"], "nn", F32, "mm_ffn_down", res=x1)

        h3 = _rms_fwd(x2, g_ple, "rms_ple")
        z = _mm(h3, w["w_ple_gate"], "nn", F32, "mm_ple_gate")
        e = _mm(p[l], w["w_ple_proj"], "nn", F32, "mm_ple_proj")

        def ple(x2_, z_, e_):
            return (x2_ + _sigmoid(z_) * e_,)

        (x3,) = _ew(ple, [(x2, 0), (z, 0), (e, 0)], [F32], width=D_MODEL, bw=D_MODEL, name="ple")
        saved.append(dict(w=w, x0=x, h=h, proj=proj, qa=qa, ka=ka, va=va, qb=qb, kb=kb, vb=vb, ya=ya, lse_a=lse_a,
                          yb=yb, lse_b=lse_b, ca=ca, cb=cb, merged=merged, x1=x1, h2=h2, a=a, u=u, hid=hid,
                          x2=x2, h3=h3, z=z, e=e))
        x = x3

    dx, loss_acc = _loss_grad(x, target)
    loss = loss_acc[0, 0]

    gbig = [{} for _ in range(DEPTH)]
    marks = [{} for _ in range(DEPTH)]
    gsmall = {n: [None] * DEPTH for n in SMALL if n != "rel_table"}
    dtable_a = jnp.zeros((N_HEADS, NUM_BUCKETS), F32)
    dtable_b = jnp.zeros((N_HEADS, NUM_BUCKETS), F32)

    for l in reversed(range(DEPTH)):
        sv = saved[l]
        w = sv["w"]
        g_mix, g_ffn, g_ple = (small[n][l][None, :] for n in ("norm_mix_g", "norm_ffn_g", "norm_ple_g"))
        gqa, gka, gqb = (_tile_gain(small[n][l], N_HEADS) for n in ("qnorm_a_g", "knorm_a_g", "qnorm_b_g"))
        gkb = _tile_gain(small["knorm_b_g"][l], N_KV_B)
        sink = jnp.repeat(small["sink_b"][l], BLK_B)[:, None]

        def ple_bwd(dx_, z_, e_):
            s = _sigmoid(z_)
            return dx_ * s, dx_ * e_ * (s * (1.0 - s))

        de, dz = _ew(ple_bwd, [(dx, 0), (sv["z"], 0), (sv["e"], 0)], [BF16, BF16], width=D_MODEL, bw=D_MODEL,
                     name="ple_bwd")
        gbig[l]["w_ple_proj"] = _mm(p[l], de, "tn", F32, "mm_d_ple_proj")
        gbig[l]["w_ple_gate"] = _mm(sv["h3"], dz, "tn", F32, "mm_d_ple_gate")
        dh3 = _mm(dz, w["w_ple_gate"], "nt", F32, "mm_dh3")
        dx, dxb, gsmall["norm_ple_g"][l] = _rms_bwd(sv["x2"], g_ple, dh3, dx, "rms_ple_bwd")

        dhid = _mm(dxb, w["w_ffn_down"], "nt", F32, "mm_dhid")
        gbig[l]["w_ffn_down"] = _mm(sv["hid"], dxb, "tn", F32, "mm_d_ffn_down")

        def swiglu_bwd(a_, u_, dh_):
            s = _sigmoid(a_)
            return dh_ * u_ * (s * (1.0 + a_ * (1.0 - s))), dh_ * (a_ * s)

        da, du = _ew(swiglu_bwd, [(sv["a"], 0), (sv["u"], 0), (dhid, 0)], [BF16, BF16], width=D_FF, bw=D_FF,
                     name="swiglu_bwd")
        gbig[l]["w_ffn_gate"] = _mm(da, sv["h2"], "tn", F32, "mm_d_ffn_gate")
        gbig[l]["w_ffn_up"] = _mm(du, sv["h2"], "tn", F32, "mm_d_ffn_up")
        dh2 = _mm(da, w["w_ffn_gate"], "nn", F32, "mm_dh2_gate")
        dh2 = _mm(du, w["w_ffn_up"], "nn", F32, "mm_dh2_up", res=dh2)
        dx, dxb, gsmall["norm_ffn_g"][l] = _rms_bwd(sv["x1"], g_ffn, dh2, dx, "rms_ffn_bwd")

        dmerged = _mm(dxb, w["w_out"], "nt", F32, "mm_dmerged")
        marks[l]["ffn_bwd_done"] = dmerged
        gbig[l]["w_out"] = _mm(sv["merged"], dxb, "tn", F32, "mm_d_out")

        def gate_bwd(dm_, ca_, cb_, ga_, gb_):
            sa, sb = _sigmoid(ga_), _sigmoid(gb_)
            return dm_ * sa, dm_ * sb, dm_ * ca_ * (sa * (1.0 - sa)), dm_ * cb_ * (sb * (1.0 - sb))

        dca, dcb, dga, dgb = _ew(gate_bwd, [(dmerged, 0), (sv["ca"], 0), (sv["cb"], 0), (sv["proj"], OFF_GA),
                                            (sv["proj"], OFF_GB)], [BF16, BF16, BF16, BF16],
                                 width=D_MODEL, bw=256, name="gate_bwd")
        dgab = jnp.concatenate([dga, dgb], axis=1)
        gbig[l]["w_branch_a"] = _mm(sv["ya"], dca, "tn", F32, "mm_d_branch_a")
        gbig[l]["w_branch_b"] = _mm(sv["yb"], dcb, "tn", F32, "mm_d_branch_b")
        dya = _mm(dca, w["w_branch_a"], "nt", F32, "mm_dya")
        dyb = _mm(dcb, w["w_branch_b"], "nt", F32, "mm_dyb")

        dya, delta_a = _attn_bwd_prep(dya, sv["ya"], "attn_a_bwd_prep")
        dyb, delta_b = _attn_bwd_prep(dyb, sv["yb"], "attn_b_bwd_prep")

        dqa, dka, dva = [], [], []
        for (blk, d), bias, bk in zip(DILATED, bias_a, buckets_a):
            dq_, dk_, dv_, db_ = _attn_bwd(sv["qa"], sv["ka"], sv["va"], dya, sv["lse_a"], delta_a, bias, None, blk, d,
                                           f"attn_a{d}_bwd")
            dqa.append(dq_)
            dka.append(dk_)
            dva.append(dv_)
            dtable_a = dtable_a + _table_grad(db_.reshape(N_HEADS, blk, 3 * blk), bk, "table_grad_a")
        dqb, dkb, dvb, db_, dsink = _attn_bwd(sv["qb"], sv["kb"], sv["vb"], dyb, sv["lse_b"], delta_b, bias_b, sink,
                                              BLK_B, 1, "attn_b_bwd")
        dtable_b = dtable_b + _table_grad(db_.reshape(N_HEADS, BLK_B, 3 * BLK_B), buckets_b, "table_grad_b")
        gsmall["sink_b"][l] = dsink.reshape(N_HEADS, BLK_B).sum(axis=1)

        dproj, pqa, pka, pqb, pkb = _qknorm_bwd(sv["proj"], gqa, gka, gqb, gkb, dqa, dka, dva, dqb, dkb, dvb, dgab)
        marks[l]["attn_bwd_done"] = dproj
        gsmall["qnorm_a_g"][l] = pqa.reshape(N_HEADS, HEAD_DIM).sum(0)
        gsmall["knorm_a_g"][l] = pka.reshape(N_HEADS, HEAD_DIM).sum(0)
        gsmall["qnorm_b_g"][l] = pqb.reshape(N_HEADS, HEAD_DIM).sum(0)
        gsmall["knorm_b_g"][l] = pkb.reshape(N_KV_B, HEAD_DIM).sum(0)
        gbig[l]["w_in"] = _mm(dproj, sv["h"], "tn", F32, "mm_d_in")
        dh = _mm(dproj, w["w_in"], "nn", F32, "mm_dh")
        dx, _, gsmall["norm_mix_g"][l] = _rms_bwd(sv["x0"], g_mix, dh, dx, "rms_mix_bwd")
        gsmall["norm_mix_g"][l] = gsmall["norm_mix_g"][l][0]
        gsmall["norm_ffn_g"][l] = gsmall["norm_ffn_g"][l][0]
        gsmall["norm_ple_g"][l] = gsmall["norm_ple_g"][l][0]

    gsmall = {n: jnp.stack(v) for n, v in gsmall.items()}
    gsmall["rel_table"] = jnp.concatenate([dtable_a, dtable_b], axis=0).T
    return loss, dx, gbig, gsmall, marks


def _place():
    return lax.axis_index("x"), lax.axis_index("y"), lax.axis_index("c")


def _flip(v, bit):
    return 1 - v if bit else v


CHIP_RELATIONS = ((0, 1), (1, 0), (1, 1))
ANY = pl.BlockSpec(memory_space=pl.ANY)


def _allgather_body(w_refs, out_refs, send_sems, recv_sems):
    x, y, c = _place()
    chips = [(_flip(x, a), _flip(y, b)) for a, b in CHIP_RELATIONS]

    def make(g):
        w_ref, out_ref = w_refs[g], out_refs[g]
        half = w_ref.shape[0] // 2

        def part(px, py, pc):
            return out_ref.at[2 * px + py, pl.ds(pc * half, half), :]

        def copy(k, block, to, src=None):
            return pltpu.make_async_remote_copy(
                src_ref=part(*block) if src is None else src, dst_ref=part(*block),
                send_sem=send_sems.at[7 * g + k], recv_sem=recv_sems.at[7 * g + k], device_id=to,
                device_id_type=MESH_ID)

        own = pltpu.make_async_remote_copy(
            src_ref=w_ref, dst_ref=out_ref.at[2 * x + y], send_sem=send_sems.at[7 * g + 6],
            recv_sem=recv_sems.at[7 * g + 6], device_id=(x, y, 1 - c), device_id_type=MESH_ID)
        first = [copy(k, (x, y, c), (*chip, c), src=w_ref.at[pl.ds(c * half, half), :]) for k, chip in enumerate(chips)]
        passed = [copy(3 + k, (*chip, c), (x, y, 1 - c)) for k, chip in enumerate(chips)]
        arrive = [copy(k, (*chip, c), (x, y, c)) for k, chip in enumerate(chips)]
        arrive2 = [copy(3 + k, (*chip, 1 - c), (x, y, c)) for k, chip in enumerate(chips)]
        return own, first, passed, arrive, arrive2

    made = [make(g) for g in range(len(w_refs))]
    for own, first, _, _, _ in made:
        own.start()
        for cp in first:
            cp.start()
    for _, _, passed, arrive, _ in made:
        for k in range(3):
            arrive[k].wait_recv()
            passed[k].start()
    for own, first, passed, _, arrive2 in made:
        for k in range(3):
            arrive2[k].wait_recv()
        own.wait_recv()
        for cp in first + passed + [own]:
            cp.wait_send()


def _sibling(x, y, c):
    return [(x, y, 1 - c)]


def _same_core_of_other_chips(x, y, c):
    return [(_flip(x, a), _flip(y, b), c) for a, b in CHIP_RELATIONS]


def _exchange(body, ins, out_types, n_sems, name, sequencer=None):
    n = len(ins)
    sems = (pltpu.SemaphoreType.DMA((n_sems,)), pltpu.SemaphoreType.DMA((n_sems,)))
    if sequencer is None:
        in_place = out_types is None
        out_shape = [jax.ShapeDtypeStruct(a.shape, a.dtype) for a in ins] if in_place else out_types

        def tc_body(*refs):
            body(refs[:n], refs[n:n + len(out_shape)], refs[-2], refs[-1])

        return list(pl.pallas_call(
            tc_body, out_shape=out_shape, in_specs=[ANY] * n, out_specs=[ANY] * len(out_shape),
            input_output_aliases={g: g for g in range(n)} if in_place else {}, scratch_shapes=list(sems), name=name)(*ins))

    collective_id, peers = sequencer
    hbm = pltpu.MemorySpace.HBM
    in_refs = [jax.new_ref(a, memory_space=hbm) for a in ins]
    out_refs = in_refs if out_types is None else [jax.empty_ref(t, memory_space=hbm) for t in out_types]

    @pl.kernel(mesh=plsc.ScalarSubcoreMesh(axis_name="sequencer", num_cores=1), name=name, scratch_types=sems,
               compiler_params=pltpu.CompilerParams(collective_id=collective_id))
    def launch(send_sems, recv_sems):
        barrier = pltpu.get_barrier_semaphore()
        devices = peers(*_place())
        for device in devices:
            pl.semaphore_signal(barrier, inc=1, device_id=device, device_id_type=MESH_ID)
        pl.semaphore_wait(barrier, len(devices))
        body(in_refs, out_refs, send_sems, recv_sems)

    launch()
    return [r[...] for r in out_refs]


def _allgather(shards, name, sequencer=None):
    out_types = [jax.ShapeDtypeStruct((N_CHIPS,) + s.shape, s.dtype) for s in shards]
    if sequencer is not None:
        sequencer = (sequencer, lambda x, y, c: _sibling(x, y, c) + _same_core_of_other_chips(x, y, c))
    return _exchange(_allgather_body, shards, out_types, 7 * len(shards), name, sequencer)


def _half_tile(half):
    return max(t for t in range(16, 1025, 16) if half % t == 0)


def _run_copies(cps):
    for cp in cps:
        cp.start()
    for cp in cps:
        cp.wait_recv()
    for cp in cps:
        cp.wait_send()


def _sibling_halves(gsends, name, sequencer=None):
    def body(g_refs, out_refs, send_sems, recv_sems):
        x, y, c = _place()
        cps = []
        for g, (g_ref, out_ref) in enumerate(zip(g_refs, out_refs)):
            half = g_ref.shape[1] // 2
            cps.append(pltpu.make_async_remote_copy(
                src_ref=g_ref.at[:, pl.ds((1 - c) * half, half), :], dst_ref=out_ref,
                send_sem=send_sems.at[g], recv_sem=recv_sems.at[g], device_id=(x, y, 1 - c), device_id_type=MESH_ID))
        _run_copies(cps)

    out_types = [jax.ShapeDtypeStruct((s.shape[0], s.shape[1] // 2, s.shape[2]), s.dtype) for s in gsends]
    return _exchange(body, gsends, out_types, len(gsends), name, sequencer and (sequencer, _sibling))


def _chip_sums(gsend, sib, place):
    n, rows, cols = gsend.shape
    half = rows // 2
    tm = _half_tile(half)
    nblk = half // tm

    def body(s_ref, g_ref, sib_ref, o_ref):
        o_ref[0] = (g_ref[0].astype(F32) + sib_ref[0].astype(F32)).astype(o_ref.dtype)

    grid_spec = pltpu.PrefetchScalarGridSpec(
        num_scalar_prefetch=1, grid=(n, nblk),
        in_specs=[pl.BlockSpec((1, tm, cols), lambda k, i, s: (jnp.bitwise_xor(s[0], k), s[1] * nblk + i, 0)),
                  pl.BlockSpec((1, tm, cols), lambda k, i, s: (jnp.bitwise_xor(s[0], k), i, 0))],
        out_specs=pl.BlockSpec((1, tm, cols), lambda k, i, s: (k, i, 0)))
    return pl.pallas_call(
        body, out_shape=jax.ShapeDtypeStruct((n, half, cols), BF16), grid_spec=grid_spec,
        name="rs_chip_sums", compiler_params=_params("parallel", "parallel"))(place, gsend, sib)


def _exchange_chip_sums(tsends, name, sequencer=None):
    def body(t_refs, out_refs, send_sems, recv_sems):
        x, y, c = _place()
        cps = []
        for g, (t_ref, out_ref) in enumerate(zip(t_refs, out_refs)):
            for k, device in enumerate(_same_core_of_other_chips(x, y, c)):
                cps.append(pltpu.make_async_remote_copy(
                    src_ref=t_ref.at[k + 1], dst_ref=out_ref.at[k], send_sem=send_sems.at[3 * g + k],
                    recv_sem=recv_sems.at[3 * g + k], device_id=device, device_id_type=MESH_ID))
        _run_copies(cps)

    out_types = [jax.ShapeDtypeStruct((3,) + s.shape[1:], s.dtype) for s in tsends]
    return _exchange(body, tsends, out_types, 3 * len(tsends), name,
                     sequencer and (sequencer, _same_core_of_other_chips))


def _final_sum(tsend, recv, place):
    n, half, cols = tsend.shape
    tm = _half_tile(half)
    nblk = half // tm

    def body(s_ref, t_ref, r_ref, o_ref):
        o_ref[...] = ((t_ref[0].astype(F32) + r_ref[0].astype(F32)) + r_ref[1].astype(F32)) + r_ref[2].astype(F32)

    grid_spec = pltpu.PrefetchScalarGridSpec(
        num_scalar_prefetch=1, grid=(nblk,),
        in_specs=[pl.BlockSpec((1, tm, cols), lambda i, s: (0, i, 0)), pl.BlockSpec((n - 1, tm, cols), lambda i, s: (0, i, 0))],
        out_specs=pl.BlockSpec((tm, cols), lambda i, s: (s[1] * nblk + i, 0)))
    return pl.pallas_call(
        body, out_shape=jax.ShapeDtypeStruct((2 * half, cols), F32), grid_spec=grid_spec, name="rs_final_sum",
        compiler_params=_params("parallel"))(place, tsend, recv)


def _join_halves(gfulls, name, sequencer=None):
    def body(g_refs, out_refs, send_sems, recv_sems):
        x, y, c = _place()
        n = len(g_refs)

        def copy(g, pc):
            half = g_refs[g].shape[0] // 2
            return pltpu.make_async_remote_copy(
                src_ref=g_refs[g].at[pl.ds(pc * half, half), :], dst_ref=out_refs[g].at[pl.ds(pc * half, half), :],
                send_sem=send_sems.at[g], recv_sem=recv_sems.at[g], device_id=(x, y, 1 - c), device_id_type=MESH_ID)

        mine = [copy(g, c) for g in range(n)]
        for cp in mine:
            cp.start()
        for g in range(n):
            copy(g, 1 - c).wait_recv()
        for cp in mine:
            cp.wait_send()

    return _exchange(body, gfulls, None, len(gfulls), name, sequencer and (sequencer, _sibling))


def _allreduce_small(v):
    rows, cols = v.shape

    def body(v_ref, out_ref, buf, send_sems, recv_sems):
        x, y, c = _place()
        cps = []
        for k in range(1, 8):
            peer = (_flip(x, (k >> 2) & 1), _flip(y, (k >> 1) & 1), _flip(c, k & 1))
            cps.append(pltpu.make_async_remote_copy(
                src_ref=v_ref, dst_ref=buf.at[k - 1], send_sem=send_sems.at[k - 1], recv_sem=recv_sems.at[k - 1],
                device_id=peer, device_id_type=MESH_ID))
        for cp in cps:
            cp.start()
        for cp in cps:
            cp.wait_recv()
        for cp in cps:
            cp.wait_send()
        t0 = v_ref[...] + buf[0]
        t1 = buf[1] + buf[2]
        t2 = buf[3] + buf[4]
        t3 = buf[5] + buf[6]
        out_ref[...] = (t0 + t1) + (t2 + t3)

    vm = pl.BlockSpec(memory_space=pltpu.VMEM)
    return pl.pallas_call(
        body, out_shape=jax.ShapeDtypeStruct((rows, cols), F32), in_specs=[vm], out_specs=vm,
        scratch_shapes=[pltpu.VMEM((7, rows, cols), F32), pltpu.SemaphoreType.DMA((7,)), pltpu.SemaphoreType.DMA((7,))],
        name="allreduce_small")(v)


BIG_INFO = {n: (shape, ax) for n, shape, ax in BIG}
GROUPS = (("w_in",), ("w_ffn_gate", "w_ffn_up", "w_ffn_down", "w_out", "w_ple_gate"),
          ("w_branch_a", "w_branch_b", "w_ple_proj"))


def _shard_shape(name):
    (k, m), ax = BIG_INFO[name]
    return (k // N_CHIPS, m) if ax == 0 else (k, m // N_CHIPS)


def _group_rows(group):
    offs, off = {}, 0
    for n in group:
        offs[n] = off
        off += _shard_shape(n)[0]
    return offs, off


def _pack_groups(shards, layer, dtype):
    return [jnp.concatenate([shards[n][layer].astype(dtype) for n in group], axis=0) for group in GROUPS]


def _unpack_full(gathered, groups):
    out = {}
    for group, arr in zip(groups, gathered):
        offs, _ = _group_rows(group)
        for n in group:
            rows, cols = _shard_shape(n)
            (k, m), ax = BIG_INFO[n]
            slab = arr[:, offs[n]:offs[n] + rows]
            out[n] = slab.reshape(k, m) if ax == 0 else jnp.transpose(slab, (1, 0, 2)).reshape(k, m)
    return out


def _pack_grads(gfull):
    out = []
    for group in GROUPS:
        parts = []
        for n in group:
            rows, cols = _shard_shape(n)
            ax = BIG_INFO[n][1]
            slab = (gfull[n].reshape(N_CHIPS, rows, cols) if ax == 0
                    else jnp.transpose(gfull[n].reshape(rows, N_CHIPS, cols), (1, 0, 2)))
            parts.append(slab.astype(BF16))
        out.append(jnp.concatenate(parts, axis=1))
    return out


def _after(values, mark):
    values, _ = lax.optimization_barrier((values, mark))
    return values


def _reduce_scatter(gsends, place, tag, sequencer_ids=None, hold=None):
    ids = sequencer_ids or (None, None, None)
    sibs = _sibling_halves(gsends, "rs_sibling_halves_" + tag, ids[0])
    tsends = [_chip_sums(g, s, place) for g, s in zip(gsends, sibs)]
    recvs = _exchange_chip_sums(tsends, "rs_exchange_" + tag, ids[1])
    if hold is not None:
        recvs = _after(recvs, hold)
    return _join_halves([_final_sum(t, r, place) for t, r in zip(tsends, recvs)], "rs_join_halves_" + tag, ids[2])


SMALL_SHAPES = {"rel_table": (NUM_BUCKETS, 2 * N_HEADS), "norm_mix_g": (DEPTH, D_MODEL), "qnorm_a_g": (DEPTH, HEAD_DIM),
                "knorm_a_g": (DEPTH, HEAD_DIM), "qnorm_b_g": (DEPTH, HEAD_DIM), "knorm_b_g": (DEPTH, HEAD_DIM),
                "sink_b": (DEPTH, N_HEADS), "norm_ffn_g": (DEPTH, D_MODEL), "norm_ple_g": (DEPTH, D_MODEL)}


def _pack_small(vals):
    flat = jnp.concatenate([vals[n].astype(F32).reshape(-1) for n in SMALL])
    flat = jnp.concatenate([flat, jnp.zeros((SMALL_ROWS * LANES - flat.shape[0],), F32)])
    return flat.reshape(SMALL_ROWS, LANES)


def _unpack_small(packed):
    flat, out, off = packed.reshape(-1), {}, 0
    for n in SMALL:
        size = math.prod(SMALL_SHAPES[n])
        out[n] = flat[off:off + size].reshape(SMALL_SHAPES[n])
        off += size
    return out


def _adamw(w, gs, g_row, m, v, name):
    c1 = 1.0 - ADAM_B1 ** ADAM_STEP
    c2 = 1.0 - ADAM_B2 ** ADAM_STEP
    total, width = w.shape
    n_layers = len(gs)
    per = total // n_layers
    tm = max(t for t in range(8, 513, 8) if per % t == 0 and g_row % t == 0)
    nblk = per // tm

    def body(*refs):
        w_ref, g_refs = refs[0], refs[1:1 + n_layers]
        m_ref, v_ref, og, od, om, ov = refs[1 + n_layers:]
        layer = pl.program_id(0) // nblk
        g = g_refs[0][...]
        for l in range(1, n_layers):
            g = jnp.where(layer == l, g_refs[l][...], g)
        m_new = ADAM_B1 * m_ref[...] + (1.0 - ADAM_B1) * g
        v_new = ADAM_B2 * v_ref[...] + (1.0 - ADAM_B2) * (g * g)
        og[...] = g
        od[...] = -ADAM_LR * ((m_new / c1) / (jnp.sqrt(v_new / c2) + ADAM_EPS) + ADAM_WD * w_ref[...])
        om[...] = m_new
        ov[...] = v_new

    row = pl.BlockSpec((tm, width), lambda i: (i, 0))
    g_specs = [pl.BlockSpec((tm, width), lambda i, l=l: (g_row // tm + jnp.clip(i - l * nblk, 0, nblk - 1), 0))
               for l in range(n_layers)]
    return pl.pallas_call(
        body, out_shape=[jax.ShapeDtypeStruct((total, width), F32)] * 4, grid=(total // tm,),
        in_specs=[row] + g_specs + [row, row], out_specs=[row] * 4, name=name,
        compiler_params=_params("parallel"))(w, *gs, m, v)


def kernel(x, p, rel_table, norm_mix_g, w_in, qnorm_a_g, knorm_a_g, qnorm_b_g, knorm_b_g, sink_b, w_branch_a, w_branch_b, w_out, norm_ffn_g, w_ffn_gate, w_ffn_up, w_ffn_down, norm_ple_g, w_ple_gate, w_ple_proj, loss_target, m_rel_table, m_norm_mix_g, m_w_in, m_qnorm_a_g, m_knorm_a_g, m_qnorm_b_g, m_knorm_b_g, m_sink_b, m_w_branch_a, m_w_branch_b, m_w_out, m_norm_ffn_g, m_w_ffn_gate, m_w_ffn_up, m_w_ffn_down, m_norm_ple_g, m_w_ple_gate, m_w_ple_proj, v_rel_table, v_norm_mix_g, v_w_in, v_qnorm_a_g, v_knorm_a_g, v_qnorm_b_g, v_knorm_b_g, v_sink_b, v_w_branch_a, v_w_branch_b, v_w_out, v_norm_ffn_g, v_w_ffn_gate, v_w_ffn_up, v_w_ffn_down, v_norm_ple_g, v_w_ple_gate, v_w_ple_proj):
    given = dict(locals())

    def held(name, a):
        return jnp.swapaxes(a, 1, 2) if name in TRANSPOSED else a

    weights = {n: held(n, given[n]) for n in WEIGHTS}
    moments_m = {n: held(n, given["m_" + n]) for n in WEIGHTS}
    moments_v = {n: held(n, given["v_" + n]) for n in WEIGHTS}
    xi, yi, ci = _place()
    place = jnp.stack([2 * xi + yi, ci]).astype(jnp.int32)

    shards = [_pack_groups(weights, l, BF16) for l in range(DEPTH)]
    w_in0 = _allgather(shards[0][:1], "allgather_w_in_layer0")
    rest0 = _allgather(_after(shards[0][1:], w_in0), "allgather_rest_layer0", sequencer=1)
    gathered = [w_in0 + rest0, None]
    small = {n: weights[n] for n in SMALL}

    def w_in_of(l, mark):
        return _unpack_full(gathered[l][:1] if l == 0 else _after(gathered[l][:1], mark), GROUPS[:1])["w_in"]

    def rest_of(l, mark):
        if l == 0:
            gathered[1] = _allgather(_after(shards[1], mark), "allgather_layer1", sequencer=2)
        return _unpack_full(_after(gathered[l][1:], mark), GROUPS[1:])

    loss, dx, gbig, gsmall, marks = _local_step(x[0], p[:, 0], loss_target[0], w_in_of, rest_of, small)

    gsends = [_pack_grads(gbig[l]) for l in range(DEPTH)]
    red1 = _reduce_scatter(gsends[1], place, "layer1", (3, 4, 5), hold=marks[0]["ffn_bwd_done"])
    rest0 = _reduce_scatter(gsends[0][1:], place, "rest_layer0", (6, 7, 8), hold=marks[0]["attn_bwd_done"])
    greds = [_reduce_scatter(gsends[0][:1], place, "w_in_layer0") + rest0, red1]

    grads, delta, new_m, new_v = {}, {}, {}, {}
    for gi, group in enumerate(GROUPS):
        offs, _ = _group_rows(group)
        for n in group:
            shape = weights[n].shape
            two_d = lambda a: a.reshape(shape[0] * shape[1], shape[2])
            outs = _adamw(two_d(weights[n]), [greds[l][gi] for l in range(DEPTH)], offs[n], two_d(moments_m[n]),
                          two_d(moments_v[n]), "adamw_" + n)
            grads[n], delta[n], new_m[n], new_v[n] = (held(n, o.reshape(shape)) for o in outs)
    small_grads = _allreduce_small(_pack_small(gsmall))
    g_, d_, m_, v_ = _adamw(_pack_small(weights), [small_grads], 0, _pack_small(moments_m), _pack_small(moments_v),
                            "adamw_small")
    grads.update(_unpack_small(g_))
    delta.update(_unpack_small(d_))
    new_m.update(_unpack_small(m_))
    new_v.update(_unpack_small(v_))

    loss = lax.psum(loss, ("x", "y", "c"))
    return (loss, dx[None], *[grads[n] for n in WEIGHTS], *[delta[n] for n in WEIGHTS],
            *[new_m[n] for n in WEIGHTS], *[new_v[n] for n in WEIGHTS])
```

```python
import functools
import math

import jax
import jax.numpy as jnp
from jax import lax
from jax.experimental import pallas as pl
from jax.experimental.pallas import tpu as pltpu
from jax.experimental.pallas import tpu_sc as plsc

F32 = jnp.float32
BF16 = jnp.bfloat16
MESH_ID = pl.DeviceIdType.MESH

SEQ = 2048
D_MODEL = 1024
DEPTH = 2
HEAD_DIM = 64
N_HEADS = 8
WIDTH = N_HEADS * HEAD_DIM
N_PAIRS = 4
ITEMS = 4
N_KV_B = 2
PLE_DIM = 256
D_FF = 2816
D_IN = 4352
OFF_QA, OFF_KA, OFF_VA, OFF_QB, OFF_KB, OFF_VB, OFF_GA, OFF_GB = 0, 512, 1024, 1536, 2048, 2176, 2304, 3328
DILATED = ((64, 1), (64, 4), (64, 16))
BLK_B = 128
NUM_BUCKETS = 32
MAX_DISTANCE = 1024
RMS_EPS = 1e-6
NEG_INF = -1e30
LANES = 128
VMEM_LIMIT = 48 * 1024 * 1024

ADAM_LR, ADAM_B1, ADAM_B2, ADAM_EPS, ADAM_WD, ADAM_STEP = 0.001, 0.9, 0.999, 1e-08, 0.01, 10

TRANSPOSED = ("w_in", "w_ffn_gate", "w_ffn_up")
BIG = (
    ("w_in", (D_IN, D_MODEL), 0),
    ("w_branch_a", (WIDTH, D_MODEL), 1),
    ("w_branch_b", (WIDTH, D_MODEL), 1),
    ("w_out", (D_MODEL, D_MODEL), 0),
    ("w_ffn_gate", (D_FF, D_MODEL), 0),
    ("w_ffn_up", (D_FF, D_MODEL), 0),
    ("w_ffn_down", (D_FF, D_MODEL), 0),
    ("w_ple_gate", (D_MODEL, D_MODEL), 0),
    ("w_ple_proj", (PLE_DIM, D_MODEL), 1),
)
SMALL = ("rel_table", "norm_mix_g", "qnorm_a_g", "knorm_a_g", "qnorm_b_g", "knorm_b_g", "sink_b",
         "norm_ffn_g", "norm_ple_g")
WEIGHTS = ("rel_table", "norm_mix_g", "w_in", "qnorm_a_g", "knorm_a_g", "qnorm_b_g", "knorm_b_g", "sink_b",
           "w_branch_a", "w_branch_b", "w_out", "norm_ffn_g", "w_ffn_gate", "w_ffn_up", "w_ffn_down",
           "norm_ple_g", "w_ple_gate", "w_ple_proj")
N_CHIPS = 4
SMALL_ROWS = 64


def _params(*sem):
    return pltpu.CompilerParams(dimension_semantics=sem, vmem_limit_bytes=VMEM_LIMIT)


def _pick(dim, target):
    for t in (target, 512, 256, 128, 64, 32, 16, 8):
        if t <= target and dim % t == 0:
            return t
    return dim


MM_VMEM_BUDGET = 40 * 1024 * 1024
STEP_OVERHEAD_S = 0.4e-6
TILE_DMA_BYTES_PER_S = 1.5e12
MXU_FLOPS_PER_S = 7e14


def _mm_tiles(m, n, k, a_bytes, b_bytes, out_bytes, has_res):
    best = None
    for tm in (t for t in range(LANES, m + 1, LANES) if m % t == 0):
        for tn in (t for t in range(LANES, n + 1, LANES) if n % t == 0):
            io = tm * k * a_bytes + tn * k * b_bytes + tm * tn * (out_bytes + (4 if has_res else 0))
            casts = (tm * k * 2 if a_bytes == 4 else 0) + (tn * k * 2 if b_bytes == 4 else 0)
            if 2 * io + tm * tn * 4 + casts > MM_VMEM_BUDGET:
                continue
            steps = (m // tm) * (n // tn)
            cost = steps * STEP_OVERHEAD_S + io / TILE_DMA_BYTES_PER_S + 2.0 * m * n * k / MXU_FLOPS_PER_S
            if best is None or (cost, -tm) < best[0]:
                best = ((cost, -tm), tm, tn)
    return best[1], best[2]


def _mm(a, b, mode, out_dtype, name, res=None):
    if mode == "nn":
        (m, k), (_, n) = a.shape, b.shape
    elif mode == "nt":
        (m, k), (n, _) = a.shape, b.shape
    else:
        (k, m), (_, n) = a.shape, b.shape
    tm, tn = _mm_tiles(m, n, k, a.dtype.itemsize, b.dtype.itemsize, jnp.dtype(out_dtype).itemsize, res is not None)
    a_spec = pl.BlockSpec((k, tm), lambda i, j: (0, i)) if mode == "tn" else pl.BlockSpec((tm, k), lambda i, j: (i, 0))
    b_spec = pl.BlockSpec((tn, k), lambda i, j: (j, 0)) if mode == "nt" else pl.BlockSpec((k, tn), lambda i, j: (0, j))
    dims = {"nn": (((1,), (0,)), ((), ())), "nt": (((1,), (1,)), ((), ())), "tn": (((0,), (0,)), ((), ()))}[mode]
    has_res = res is not None

    def body(*refs):
        a_ref, b_ref = refs[0], refs[1]
        o_ref = refs[-1]
        acc = lax.dot_general(a_ref[...].astype(BF16), b_ref[...].astype(BF16), dims, preferred_element_type=F32)
        if has_res:
            acc = acc + refs[2][...]
        o_ref[...] = acc.astype(out_dtype)

    in_specs = [a_spec, b_spec]
    args = [a, b]
    if has_res:
        in_specs.append(pl.BlockSpec((tm, tn), lambda i, j: (i, j)))
        args.append(res)
    return pl.pallas_call(
        body, out_shape=jax.ShapeDtypeStruct((m, n), out_dtype), grid=(m // tm, n // tn), in_specs=in_specs,
        out_specs=pl.BlockSpec((tm, tn), lambda i, j: (i, j)), name=name,
        compiler_params=_params("parallel", "parallel"))(*args)


def _ew(fn, ins, out_dtypes, *, width, bw, name, vecs=(), tm=256):
    rows = ins[0][0].shape[0]
    tm = _pick(rows, tm)
    n_in = len(ins) + len(vecs)

    def col_map(off_blocks):
        return lambda i, j: (i, off_blocks + j)

    in_specs = [pl.BlockSpec((tm, bw), col_map(off // bw)) for _, off in ins]
    in_specs += [pl.BlockSpec((1, bw), lambda i, j: (0, j)) for _ in vecs]

    def body(*refs):
        outs = fn(*[r[...] for r in refs[:n_in]])
        for r, o in zip(refs[n_in:], outs):
            r[...] = o.astype(r.dtype)

    return pl.pallas_call(
        body, out_shape=[jax.ShapeDtypeStruct((rows, width), dt) for dt in out_dtypes],
        grid=(rows // tm, width // bw), in_specs=in_specs,
        out_specs=[pl.BlockSpec((tm, bw), lambda i, j: (i, j)) for _ in out_dtypes],
        name=name, compiler_params=_params("parallel", "parallel"))(*[a for a, _ in ins], *vecs)


def _sigmoid(x):
    return 1.0 / (1.0 + jnp.exp(-x))


def _seg_sum(v):
    outs = []
    for k in range(v.shape[1] // LANES):
        vp = v[:, k * LANES:(k + 1) * LANES]
        left = lax.broadcasted_iota(jnp.int32, vp.shape, 1) < HEAD_DIM
        sl = jnp.sum(jnp.where(left, vp, 0.0), axis=-1, keepdims=True)
        sr = jnp.sum(jnp.where(left, 0.0, vp), axis=-1, keepdims=True)
        outs.append(jnp.where(left, sl, sr))
    return outs[0] if len(outs) == 1 else jnp.concatenate(outs, axis=1)


def _seg_rstd(x):
    return lax.rsqrt(_seg_sum(x * x) * (1.0 / HEAD_DIM) + RMS_EPS)


def _rms_fwd(x, g, name):
    rows, d = x.shape
    tm = 256

    def body(x_ref, g_ref, h_ref):
        xv = x_ref[...]
        r = lax.rsqrt(jnp.mean(xv * xv, axis=-1, keepdims=True) + RMS_EPS)
        h_ref[...] = ((xv * r) * g_ref[...]).astype(BF16)

    return pl.pallas_call(
        body, out_shape=jax.ShapeDtypeStruct((rows, d), BF16), grid=(rows // tm,),
        in_specs=[pl.BlockSpec((tm, d), lambda i: (i, 0)), pl.BlockSpec((1, d), lambda i: (0, 0))],
        out_specs=pl.BlockSpec((tm, d), lambda i: (i, 0)), name=name, compiler_params=_params("parallel"))(x, g)


def _rms_bwd(x, g, dh, dres, name):
    rows, d = x.shape
    tm = 256

    def body(x_ref, g_ref, dh_ref, dres_ref, dx_ref, dxb_ref, dg_ref):
        xv = x_ref[...]
        r = lax.rsqrt(jnp.mean(xv * xv, axis=-1, keepdims=True) + RMS_EPS)
        xh = xv * r
        dhv = dh_ref[...]
        dxh = dhv * g_ref[...]
        dxv = dres_ref[...] + r * (dxh - xh * jnp.mean(dxh * xh, axis=-1, keepdims=True))
        dx_ref[...] = dxv
        dxb_ref[...] = dxv.astype(BF16)
        part = jnp.sum(dhv * xh, axis=0, keepdims=True)

        @pl.when(pl.program_id(0) == 0)
        def _():
            dg_ref[...] = part

        @pl.when(pl.program_id(0) > 0)
        def _():
            dg_ref[...] += part

    row = pl.BlockSpec((tm, d), lambda i: (i, 0))
    vec = pl.BlockSpec((1, d), lambda i: (0, 0))
    return pl.pallas_call(
        body, out_shape=[jax.ShapeDtypeStruct((rows, d), F32), jax.ShapeDtypeStruct((rows, d), BF16),
                         jax.ShapeDtypeStruct((1, d), F32)],
        grid=(rows // tm,), in_specs=[row, vec, row, row], out_specs=[row, row, vec],
        name=name, compiler_params=_params("arbitrary"))(x, g, dh, dres)


def _loss_grad(y, t):
    rows, d = y.shape
    tm = 256

    def body(y_ref, t_ref, dy_ref, l_ref):
        e = y_ref[...] - t_ref[...]
        dy_ref[...] = e * (1.0 / d)
        part = jnp.zeros((1, LANES), F32) + jnp.sum(e * e) * (0.5 / d)

        @pl.when(pl.program_id(0) == 0)
        def _():
            l_ref[...] = part

        @pl.when(pl.program_id(0) > 0)
        def _():
            l_ref[...] += part

    row = pl.BlockSpec((tm, d), lambda i: (i, 0))
    return pl.pallas_call(
        body, out_shape=[jax.ShapeDtypeStruct((rows, d), F32), jax.ShapeDtypeStruct((1, LANES), F32)],
        grid=(rows // tm,), in_specs=[row, row], out_specs=[row, pl.BlockSpec((1, LANES), lambda i: (0, 0))],
        name="loss_grad", compiler_params=_params("arbitrary"))(y, t)


def _put_pairs(ref, val):
    for hp in range(N_PAIRS):
        ref[hp] = val[:, hp * LANES:(hp + 1) * LANES].astype(ref.dtype)


def _get_pairs(ref):
    return jnp.concatenate([ref[hp] for hp in range(N_PAIRS)], axis=1)


def _swap_halves(v):
    return pltpu.roll(v, HEAD_DIM, axis=1)


def _expand_kv(kv):
    left = lax.broadcasted_iota(jnp.int32, kv.shape, 1) < HEAD_DIM
    sw = _swap_halves(kv)
    h0 = jnp.where(left, kv, sw)
    h1 = jnp.where(left, sw, kv)
    return jnp.concatenate([h0, h0, h1, h1], axis=1)


def _reduce_kv(dkv):
    left = lax.broadcasted_iota(jnp.int32, (dkv.shape[0], LANES), 1) < HEAD_DIM
    t = dkv[:, 0:LANES] + dkv[:, LANES:2 * LANES]
    u = dkv[:, 2 * LANES:3 * LANES] + dkv[:, 3 * LANES:4 * LANES]
    t = t + _swap_halves(t)
    u = u + _swap_halves(u)
    return jnp.where(left, t, u)


def _qknorm_fwd(proj, gqa, gka, gqb, gkb):
    rows = proj.shape[0]
    tm = 256

    def body(qa_ref, ka_ref, va_ref, qb_ref, kb_ref, vb_ref, gqa_ref, gka_ref, gqb_ref, gkb_ref,
             oqa, oka, ova, oqb, okb, ovb):
        for src, g_ref, dst in ((qa_ref, gqa_ref, oqa), (ka_ref, gka_ref, oka), (qb_ref, gqb_ref, oqb)):
            xv = src[...]
            _put_pairs(dst, (xv * _seg_rstd(xv)) * g_ref[...])
        _put_pairs(ova, va_ref[...])
        kv = kb_ref[...]
        _put_pairs(okb, _expand_kv((kv * _seg_rstd(kv)) * gkb_ref[...]))
        _put_pairs(ovb, _expand_kv(vb_ref[...]))

    def win(width, off):
        return pl.BlockSpec((tm, width), lambda i: (i, off // width))

    vec = lambda w: pl.BlockSpec((1, w), lambda i: (0, 0))
    out = pl.BlockSpec((N_PAIRS, tm, LANES), lambda i: (0, i, 0))
    return pl.pallas_call(
        body, out_shape=[jax.ShapeDtypeStruct((N_PAIRS, rows, LANES), F32)] * 6, grid=(rows // tm,),
        in_specs=[win(WIDTH, OFF_QA), win(WIDTH, OFF_KA), win(WIDTH, OFF_VA), win(WIDTH, OFF_QB),
                  win(LANES, OFF_KB), win(LANES, OFF_VB), vec(WIDTH), vec(WIDTH), vec(WIDTH), vec(LANES)],
        out_specs=[out] * 6, name="qknorm_fwd", compiler_params=_params("parallel"))(
            proj, proj, proj, proj, proj, proj, gqa, gka, gqb, gkb)


def _norm_bwd(xv, g, dy):
    r = _seg_rstd(xv)
    xh = xv * r
    dxh = dy * g
    dx = r * (dxh - xh * (_seg_sum(dxh * xh) * (1.0 / HEAD_DIM)))
    return dx, jnp.sum(dy * xh, axis=0, keepdims=True)


def _qknorm_bwd(proj, gqa, gka, gqb, gkb, dqa, dka, dva, dqb, dkb, dvb, dgab):
    rows = proj.shape[0]
    tm = 256
    n_a = len(dqa)

    def body(*refs):
        qa_ref, ka_ref, qb_ref, kb_ref, gqa_ref, gka_ref, gqb_ref, gkb_ref = refs[:8]
        pos = 8
        dqa_refs, dka_refs, dva_refs = refs[pos:pos + n_a], refs[pos + n_a:pos + 2 * n_a], refs[pos + 2 * n_a:pos + 3 * n_a]
        pos += 3 * n_a
        dqb_ref, dkb_ref, dvb_ref, dgab_ref = refs[pos:pos + 4]
        dproj_ref, ogqa, ogka, ogqb, ogkb = refs[pos + 4:]

        def total(rs):
            acc = _get_pairs(rs[0])
            for r in rs[1:]:
                acc = acc + _get_pairs(r)
            return acc

        dx_qa, p_qa = _norm_bwd(qa_ref[...], gqa_ref[...], total(dqa_refs))
        dx_ka, p_ka = _norm_bwd(ka_ref[...], gka_ref[...], total(dka_refs))
        dx_qb, p_qb = _norm_bwd(qb_ref[...], gqb_ref[...], _get_pairs(dqb_ref))
        dx_kb, p_kb = _norm_bwd(kb_ref[...], gkb_ref[...], _reduce_kv(_get_pairs(dkb_ref)))
        dproj_ref[:, OFF_QA:OFF_QA + WIDTH] = dx_qa.astype(BF16)
        dproj_ref[:, OFF_KA:OFF_KA + WIDTH] = dx_ka.astype(BF16)
        dproj_ref[:, OFF_VA:OFF_VA + WIDTH] = total(dva_refs).astype(BF16)
        dproj_ref[:, OFF_QB:OFF_QB + WIDTH] = dx_qb.astype(BF16)
        dproj_ref[:, OFF_KB:OFF_KB + LANES] = dx_kb.astype(BF16)
        dproj_ref[:, OFF_VB:OFF_VB + LANES] = _reduce_kv(_get_pairs(dvb_ref)).astype(BF16)
        dproj_ref[:, OFF_GA:D_IN] = dgab_ref[...]
        first = pl.program_id(0) == 0
        for o_ref, part in ((ogqa, p_qa), (ogka, p_ka), (ogqb, p_qb), (ogkb, p_kb)):
            @pl.when(first)
            def _(o_ref=o_ref, part=part):
                o_ref[...] = part

            @pl.when(jnp.logical_not(first))
            def _(o_ref=o_ref, part=part):
                o_ref[...] += part

    def win(width, off):
        return pl.BlockSpec((tm, width), lambda i: (i, off // width))

    vec = lambda w: pl.BlockSpec((1, w), lambda i: (0, 0))
    row = lambda w: pl.BlockSpec((tm, w), lambda i: (i, 0))
    in_specs = [win(WIDTH, OFF_QA), win(WIDTH, OFF_KA), win(WIDTH, OFF_QB), win(LANES, OFF_KB),
                vec(WIDTH), vec(WIDTH), vec(WIDTH), vec(LANES)]
    in_specs += [pl.BlockSpec((N_PAIRS, tm, LANES), lambda i: (0, i, 0))] * (3 * n_a + 3) + [row(2 * D_MODEL)]
    return pl.pallas_call(
        body,
        out_shape=[jax.ShapeDtypeStruct((rows, D_IN), BF16), jax.ShapeDtypeStruct((1, WIDTH), F32),
                   jax.ShapeDtypeStruct((1, WIDTH), F32), jax.ShapeDtypeStruct((1, WIDTH), F32),
                   jax.ShapeDtypeStruct((1, LANES), F32)],
        grid=(rows // tm,), in_specs=in_specs,
        out_specs=[row(D_IN), vec(WIDTH), vec(WIDTH), vec(WIDTH), vec(LANES)],
        name="qknorm_bwd", compiler_params=_params("arbitrary"))(
            proj, proj, proj, proj, gqa, gka, gqb, gkb, *dqa, *dka, *dva, dqb, dkb, dvb, dgab)


def _t5_bucket(rel):
    half_b = NUM_BUCKETS // 2
    max_exact = half_b // 2
    sign = jnp.where(rel > 0, half_b, 0)
    n = jnp.abs(rel)
    nf = jnp.maximum(n, 1).astype(F32)
    large = max_exact + (jnp.log(nf / max_exact) / math.log(MAX_DISTANCE / max_exact)
                         * (half_b - max_exact)).astype(jnp.int32)
    large = jnp.minimum(large, half_b - 1)
    return sign + jnp.where(n < max_exact, n, large)


def _band_buckets(blk, dilation):
    i = jnp.arange(blk, dtype=jnp.int32)[:, None]
    j = jnp.arange(3 * blk, dtype=jnp.int32)[None, :]
    rel = j - blk - i
    return jnp.where(jnp.abs(rel) <= blk, _t5_bucket(rel * dilation), -1)


def _bias_tiles(table, buckets, head_off, name):
    blk = buckets.shape[0]

    def body(tab_ref, bk_ref, o_ref):
        h = pl.program_id(0) + head_off
        bk = bk_ref[...]
        acc = jnp.full(bk.shape, NEG_INF, F32)
        for b in range(NUM_BUCKETS):
            acc = jnp.where(bk == b, tab_ref[b, h], acc)
        o_ref[0] = acc

    return pl.pallas_call(
        body, out_shape=jax.ShapeDtypeStruct((N_HEADS, blk, 3 * blk), F32), grid=(N_HEADS,),
        in_specs=[pl.BlockSpec(memory_space=pltpu.SMEM), pl.BlockSpec((blk, 3 * blk), lambda h: (0, 0))],
        out_specs=pl.BlockSpec((1, blk, 3 * blk), lambda h: (h, 0, 0)),
        name=name, compiler_params=_params("parallel"))(table, buckets)


def _table_grad(dbias, buckets, name):
    blk = buckets.shape[0]

    def body(db_ref, bk_ref, o_ref):
        bk = bk_ref[...]
        dbv = db_ref[0]
        lane = lax.broadcasted_iota(jnp.int32, (1, LANES), 1)
        acc = jnp.zeros((1, LANES), F32)
        for b in range(NUM_BUCKETS):
            acc = jnp.where(lane == b, jnp.sum(jnp.where(bk == b, dbv, 0.0)), acc)
        o_ref[0] = acc

    out = pl.pallas_call(
        body, out_shape=jax.ShapeDtypeStruct((N_HEADS, 1, LANES), F32), grid=(N_HEADS,),
        in_specs=[pl.BlockSpec((1, blk, 3 * blk), lambda h: (h, 0, 0)), pl.BlockSpec((blk, 3 * blk), lambda h: (0, 0))],
        out_specs=pl.BlockSpec((1, 1, LANES), lambda h: (h, 0, 0)),
        name=name, compiler_params=_params("parallel"))(dbias, buckets)
    return out[:, 0, :NUM_BUCKETS]


def _dot_nt(a, b):
    return lax.dot_general(a, b, (((1,), (1,)), ((), ())), preferred_element_type=F32)


def _stack_pair(x2, left):
    return jnp.concatenate([jnp.where(left, x2, 0.0), jnp.where(left, 0.0, x2)], axis=0).astype(BF16)


def _attn_geometry(blk, d):
    chunk = blk * ITEMS if d == 1 else blk * d
    groups = 1 if d == 1 else d // ITEMS
    halo = blk if d == 1 else chunk
    return chunk, groups, halo


def _item_rows(ref, j, r0, blk, d):
    if d == 1:
        return ref[j * blk:(j + 1) * blk, :]
    return ref[pl.ds(r0 + j, blk, stride=d), :]


def _item_penalty(t, nct, j, blk, d):
    first_ok, last_ok = t > 0, t < nct - 1
    if d == 1:
        first_ok = True if j > 0 else first_ok
        last_ok = True if j < ITEMS - 1 else last_ok
    col = lax.broadcasted_iota(jnp.int32, (1, 3 * blk), 1)
    ok = jnp.logical_and(jnp.logical_or(col >= blk, first_ok), jnp.logical_or(col < 2 * blk, last_ok))
    return jnp.where(ok, 0.0, NEG_INF).astype(F32)


def _attn_specs(seq, blk, d, step_of):
    chunk, _, halo = _attn_geometry(blk, d)
    per, last = chunk // halo, seq // halo - 1
    cur = pl.BlockSpec((None, chunk, LANES), lambda hp, t: (hp, step_of(t), 0))
    prev = pl.BlockSpec((None, halo, LANES), lambda hp, t: (hp, jnp.clip(step_of(t) * per - 1, 0, last), 0))
    nxt = pl.BlockSpec((None, halo, LANES), lambda hp, t: (hp, jnp.minimum((step_of(t) + 1) * per, last), 0))
    return cur, prev, nxt


def _attn_fwd(q, k, v, bias, sink, blk, d, name):
    _, seq, _ = q.shape
    chunk, groups, _ = _attn_geometry(blk, d)
    nct = seq // chunk
    has_sink = sink is not None
    scale = HEAD_DIM ** -0.5

    def body(*refs):
        q_ref, kp, kc, kn, vp, vc, vn, b_ref = refs[:8]
        s_ref = refs[8] if has_sink else None
        o_ref, l_ref = refs[-2], refs[-1]
        t = pl.program_id(1)
        left = lax.broadcasted_iota(jnp.int32, (1, LANES), 1) < HEAD_DIM
        bias2 = b_ref[...]
        if d == 1:
            kwin = jnp.concatenate([kp[...], kc[...], kn[...]], axis=0).astype(BF16)
            vwin = jnp.concatenate([vp[...], vc[...], vn[...]], axis=0).astype(BF16)

        def group(r0):
            scores, vcats = [], []
            for j in range(ITEMS):
                qs = _stack_pair(_item_rows(q_ref, j, r0, blk, d) * scale, left)
                if d == 1:
                    kcat, vcat = kwin[j * blk:(j + 3) * blk], vwin[j * blk:(j + 3) * blk]
                else:
                    kcat = jnp.concatenate([_item_rows(r, j, r0, blk, d) for r in (kp, kc, kn)], axis=0).astype(BF16)
                    vcat = jnp.concatenate([_item_rows(r, j, r0, blk, d) for r in (vp, vc, vn)], axis=0).astype(BF16)
                scores.append(_dot_nt(qs, kcat) + bias2 + _item_penalty(t, nct, j, blk, d))
                vcats.append(vcat)
            s = jnp.concatenate(scores, axis=0)
            m = jnp.max(s, axis=-1, keepdims=True)
            if has_sink:
                sk = jnp.concatenate([s_ref[...]] * ITEMS, axis=0)
                m = jnp.maximum(m, sk)
            p = jnp.exp(s - m)
            den = jnp.sum(p, axis=-1, keepdims=True)
            if has_sink:
                den = den + jnp.exp(sk - m)
            pn = (p * (1.0 / den)).astype(BF16)
            lse = m + jnp.log(den)
            for j in range(ITEMS):
                top, mid, bot = 2 * j * blk, (2 * j + 1) * blk, (2 * j + 2) * blk
                o2 = jnp.dot(pn[top:bot], vcats[j], preferred_element_type=F32)
                o_val = jnp.where(left, o2[:blk], o2[blk:])
                l_val = jnp.where(left, lse[top:mid], lse[mid:bot])
                if d == 1:
                    o_ref[j * blk:(j + 1) * blk, :] = o_val
                    l_ref[j * blk:(j + 1) * blk, :] = l_val
                else:
                    o_ref[pl.ds(r0 + j, blk, stride=d), :] = o_val
                    l_ref[pl.ds(r0 + j, blk, stride=d), :] = l_val

        if groups == 1:
            group(0)
        else:
            def step(g, carry):
                group(g * ITEMS)
                return carry

            lax.fori_loop(0, groups, step, 0)

    cur, prev, nxt = _attn_specs(seq, blk, d, lambda t: t)
    in_specs = [cur, prev, cur, nxt, prev, cur, nxt, pl.BlockSpec((2 * blk, 3 * blk), lambda hp, t: (hp, 0))]
    args = [q, k, k, k, v, v, v, bias]
    if has_sink:
        in_specs.append(pl.BlockSpec((2 * blk, 1), lambda hp, t: (hp, 0)))
        args.append(sink)
    return pl.pallas_call(
        body, out_shape=[jax.ShapeDtypeStruct(q.shape, F32)] * 2, grid=(N_PAIRS, nct),
        in_specs=in_specs, out_specs=[cur, cur], name=name, compiler_params=_params("parallel", "parallel"))(*args)


def _attn_bwd(q, k, v, do, lse, delta, bias, sink, blk, d, name):
    _, seq, _ = q.shape
    chunk, groups, halo = _attn_geometry(blk, d)
    nct = seq // chunk
    has_sink = sink is not None
    n_in = 12 if has_sink else 11
    scale = HEAD_DIM ** -0.5

    def body(*refs):
        q_ref, kp, kc, kn, vp, vc, vn, do_ref, l_ref, d_ref, b_ref = refs[:11]
        s_ref = refs[11] if has_sink else None
        dq_ref, dk_ref, dv_ref, db_ref = refs[n_in:n_in + 4]
        ds_ref = refs[n_in + 4] if has_sink else None
        wk, wv = refs[-2], refs[-1]
        t = pl.program_id(1)

        @pl.when(t == 0)
        def _():
            wk[...] = jnp.zeros_like(wk)
            wv[...] = jnp.zeros_like(wv)
            db_ref[...] = jnp.zeros_like(db_ref)
            if has_sink:
                ds_ref[...] = jnp.zeros_like(ds_ref)

        @pl.when(t > 0)
        def _():
            for w in (wk, wv):
                keep = w[chunk:2 * chunk + halo]
                w[0:chunk + halo] = keep
                w[chunk + halo:2 * chunk + halo] = jnp.zeros((chunk, LANES), F32)

        @pl.when(t < nct)
        def _():
            lane = lax.broadcasted_iota(jnp.int32, (1, LANES), 1)
            left = lane < HEAD_DIM
            bias2 = b_ref[...]
            if d == 1:
                kwin = jnp.concatenate([kp[...], kc[...], kn[...]], axis=0).astype(BF16)
                vwin = jnp.concatenate([vp[...], vc[...], vn[...]], axis=0).astype(BF16)

            def group(r0):
                qss, doss, kcats, scores, dps, lcols, dcols = [], [], [], [], [], [], []
                for j in range(ITEMS):
                    qs = _stack_pair(_item_rows(q_ref, j, r0, blk, d) * scale, left)
                    dos = _stack_pair(_item_rows(do_ref, j, r0, blk, d), left)
                    if d == 1:
                        kcat, vcat = kwin[j * blk:(j + 3) * blk], vwin[j * blk:(j + 3) * blk]
                    else:
                        kcat = jnp.concatenate([_item_rows(r, j, r0, blk, d) for r in (kp, kc, kn)], axis=0).astype(BF16)
                        vcat = jnp.concatenate([_item_rows(r, j, r0, blk, d) for r in (vp, vc, vn)], axis=0).astype(BF16)
                    l2, d2 = _item_rows(l_ref, j, r0, blk, d), _item_rows(d_ref, j, r0, blk, d)
                    lcols.append(jnp.max(jnp.where(left, l2, NEG_INF), axis=-1, keepdims=True))
                    lcols.append(jnp.max(jnp.where(left, NEG_INF, l2), axis=-1, keepdims=True))
                    dcols.append(jnp.sum(jnp.where(lane == 0, d2, 0.0), axis=-1, keepdims=True))
                    dcols.append(jnp.sum(jnp.where(lane == HEAD_DIM, d2, 0.0), axis=-1, keepdims=True))
                    scores.append(_dot_nt(qs, kcat) + bias2 + _item_penalty(t, nct, j, blk, d))
                    dps.append(_dot_nt(dos, vcat))
                    qss.append(qs)
                    doss.append(dos)
                    kcats.append(kcat)
                lcol = jnp.concatenate(lcols, axis=0)
                dcol = jnp.concatenate(dcols, axis=0)
                p = jnp.exp(jnp.concatenate(scores, axis=0) - lcol)
                ds = p * (jnp.concatenate(dps, axis=0) - dcol)
                if has_sink:
                    sgrad = dcol * jnp.exp(jnp.concatenate([s_ref[...]] * ITEMS, axis=0) - lcol)
                for j in range(ITEMS):
                    top, bot = 2 * j * blk, (2 * j + 2) * blk
                    dsj, pj = ds[top:bot], p[top:bot]
                    db_ref[...] += dsj
                    if has_sink:
                        ds_ref[...] -= sgrad[top:bot]
                    dq2 = jnp.dot(dsj.astype(BF16), kcats[j], preferred_element_type=F32) * scale
                    dq_val = jnp.where(left, dq2[:blk], dq2[blk:])
                    if d == 1:
                        dq_ref[j * blk:(j + 1) * blk, :] = dq_val
                    else:
                        dq_ref[pl.ds(r0 + j, blk, stride=d), :] = dq_val
                    dk_new = jnp.dot(jnp.transpose(dsj).astype(BF16), qss[j], preferred_element_type=F32)
                    dv_new = jnp.dot(jnp.transpose(pj).astype(BF16), doss[j], preferred_element_type=F32)
                    for w, new in ((wk, dk_new), (wv, dv_new)):
                        if d == 1:
                            w[chunk + (j - 1) * blk:chunk + (j + 2) * blk, :] += new
                        else:
                            for c in range(3):
                                w[pl.ds(c * chunk + r0 + j, blk, stride=d), :] += new[c * blk:(c + 1) * blk]

            if groups == 1:
                group(0)
            else:
                def step(g, carry):
                    group(g * ITEMS)
                    return carry

                lax.fori_loop(0, groups, step, 0)

        dk_ref[...] = wk[0:chunk]
        dv_ref[...] = wv[0:chunk]

    cur, prev, nxt = _attn_specs(seq, blk, d, lambda t: jnp.minimum(t, nct - 1))
    lag = pl.BlockSpec((None, chunk, LANES), lambda hp, t: (hp, jnp.maximum(t - 1, 0), 0))
    band = pl.BlockSpec((2 * blk, 3 * blk), lambda hp, t: (hp, 0))
    col = pl.BlockSpec((2 * blk, 1), lambda hp, t: (hp, 0))
    in_specs = [cur, prev, cur, nxt, prev, cur, nxt, cur, cur, cur, band]
    args = [q, k, k, k, v, v, v, do, lse, delta, bias]
    out_shape = [jax.ShapeDtypeStruct(q.shape, F32)] * 3 + [jax.ShapeDtypeStruct((N_HEADS * blk, 3 * blk), F32)]
    out_specs = [cur, lag, lag, band]
    if has_sink:
        in_specs.append(col)
        args.append(sink)
        out_shape.append(jax.ShapeDtypeStruct((N_HEADS * blk, 1), F32))
        out_specs.append(col)
    window = pltpu.VMEM((2 * chunk + halo, LANES), F32)
    return pl.pallas_call(
        body, out_shape=out_shape, grid=(N_PAIRS, nct + 1), in_specs=in_specs, out_specs=out_specs,
        scratch_shapes=[window, window], name=name, compiler_params=_params("arbitrary", "arbitrary"))(*args)


def _combine_patterns(outs, lses):
    _, rows, _ = outs[0].shape
    tm = 256
    n = len(outs)

    def body(*refs):
        o_refs, l_refs = refs[:n], refs[n:2 * n]
        y_ref, lse_ref = refs[2 * n], refs[2 * n + 1]
        for hp in range(N_PAIRS):
            ls = [r[hp] for r in l_refs]
            m = functools.reduce(jnp.maximum, ls)
            es = [jnp.exp(l - m) for l in ls]
            den = functools.reduce(lambda a, b: a + b, es)
            num = functools.reduce(lambda a, b: a + b, [e * r[hp] for e, r in zip(es, o_refs)])
            y_ref[:, hp * LANES:(hp + 1) * LANES] = num / den
            lse_ref[hp] = m + jnp.log(den)

    pm = pl.BlockSpec((N_PAIRS, tm, LANES), lambda i: (0, i, 0))
    return pl.pallas_call(
        body, out_shape=[jax.ShapeDtypeStruct((rows, WIDTH), F32), jax.ShapeDtypeStruct((N_PAIRS, rows, LANES), F32)],
        grid=(rows // tm,), in_specs=[pm] * (2 * n), out_specs=[pl.BlockSpec((tm, WIDTH), lambda i: (i, 0)), pm],
        name="combine_a", compiler_params=_params("parallel"))(*outs, *lses)


def _pairs_to_tokens(a):
    _, rows, _ = a.shape
    tm = 256

    def body(a_ref, o_ref):
        o_ref[...] = _get_pairs(a_ref)

    return pl.pallas_call(
        body, out_shape=jax.ShapeDtypeStruct((rows, WIDTH), a.dtype), grid=(rows // tm,),
        in_specs=[pl.BlockSpec((N_PAIRS, tm, LANES), lambda i: (0, i, 0))],
        out_specs=pl.BlockSpec((tm, WIDTH), lambda i: (i, 0)), name="pairs_to_tokens",
        compiler_params=_params("parallel"))(a)


def _attn_bwd_prep(dy, y, name):
    rows = dy.shape[0]
    tm = 256

    def body(dy_ref, y_ref, do_ref, dl_ref):
        dyv = dy_ref[...]
        _put_pairs(do_ref, dyv)
        _put_pairs(dl_ref, _seg_sum(dyv * y_ref[...]))

    tok = pl.BlockSpec((tm, WIDTH), lambda i: (i, 0))
    pm = pl.BlockSpec((N_PAIRS, tm, LANES), lambda i: (0, i, 0))
    return pl.pallas_call(
        body, out_shape=[jax.ShapeDtypeStruct((N_PAIRS, rows, LANES), F32)] * 2, grid=(rows // tm,),
        in_specs=[tok, tok], out_specs=[pm, pm], name=name, compiler_params=_params("parallel"))(dy, y)


def _tile_gain(g, reps):
    return jnp.tile(g[None, :], (1, reps))


def _local_step(x, p, target, w_in_of, rest_of, small):
    rel_table = small["rel_table"]
    buckets_a = [_band_buckets(blk, d) for blk, d in DILATED]
    buckets_b = _band_buckets(BLK_B, 1)
    bias_a = [_bias_tiles(rel_table, bk, 0, "bias_a").reshape(N_HEADS * bk.shape[0], -1) for bk in buckets_a]
    bias_b = _bias_tiles(rel_table, buckets_b, N_HEADS, "bias_b").reshape(N_HEADS * BLK_B, -1)

    saved = []
    for l in range(DEPTH):
        g_mix, g_ffn, g_ple = (small[n][l][None, :] for n in ("norm_mix_g", "norm_ffn_g", "norm_ple_g"))
        gqa, gka, gqb = (_tile_gain(small[n][l], N_HEADS) for n in ("qnorm_a_g", "knorm_a_g", "qnorm_b_g"))
        gkb = _tile_gain(small["knorm_b_g"][l], N_KV_B)
        sink = jnp.repeat(small["sink_b"][l], BLK_B)[:, None]

        h = _rms_fwd(x, g_mix, "rms_mix")
        w_in = w_in_of(l, h)
        proj = _mm(h, w_in, "nt", F32, "mm_in")
        qa, ka, va, qb, kb, vb = _qknorm_fwd(proj, gqa, gka, gqb, gkb)
        outs, lses = [], []
        for (blk, d), bias in zip(DILATED, bias_a):
            o, ls = _attn_fwd(qa, ka, va, bias, None, blk, d, f"attn_a{d}_fwd")
            outs.append(o)
            lses.append(ls)
        ya, lse_a = _combine_patterns(outs, lses)
        yb, lse_b = _attn_fwd(qb, kb, vb, bias_b, sink, BLK_B, 1, "attn_b_fwd")
        yb = _pairs_to_tokens(yb)
        w = dict(rest_of(l, yb), w_in=w_in)
        ca = _mm(ya, w["w_branch_a"], "nn", F32, "mm_branch_a")
        cb = _mm(yb, w["w_branch_b"], "nn", F32, "mm_branch_b")

        def gate(ca_, cb_, ga_, gb_):
            return (_sigmoid(ga_) * ca_ + _sigmoid(gb_) * cb_,)

        (merged,) = _ew(gate, [(ca, 0), (cb, 0), (proj, OFF_GA), (proj, OFF_GB)], [BF16],
                        width=D_MODEL, bw=256, name="gate")
        x1 = _mm(merged, w["w_out"], "nn", F32, "mm_out", res=x)

        h2 = _rms_fwd(x1, g_ffn, "rms_ffn")
        a = _mm(h2, w["w_ffn_gate"], "nt", F32, "mm_ffn_gate")
        u = _mm(h2, w["w_ffn_up"], "nt", F32, "mm_ffn_up")

        def swiglu(a_, u_):
            return ((a_ * _sigmoid(a_)) * u_,)

        (hid,) = _ew(swiglu, [(a, 0), (u, 0)], [BF16], width=D_FF, bw=D_FF, name="swiglu")
        x2 = _mm(hid, w["w_ffn_down"], "nn", F32, "mm_ffn_down", res=x1)

        h3 = _rms_fwd(x2, g_ple, "rms_ple")
        z = _mm(h3, w["w_ple_gate"], "nn", F32, "mm_ple_gate")
        e = _mm(p[l], w["w_ple_proj"], "nn", F32, "mm_ple_proj")

        def ple(x2_, z_, e_):
            return (x2_ + _sigmoid(z_) * e_,)

        (x3,) = _ew(ple, [(x2, 0), (z, 0), (e, 0)], [F32], width=D_MODEL, bw=D_MODEL, name="ple")
        saved.append(dict(w=w, x0=x, h=h, proj=proj, qa=qa, ka=ka, va=va, qb=qb, kb=kb, vb=vb, ya=ya, lse_a=lse_a,
                          yb=yb, lse_b=lse_b, ca=ca, cb=cb, merged=merged, x1=x1, h2=h2, a=a, u=u, hid=hid,
                          x2=x2, h3=h3, z=z, e=e))
        x = x3

    dx, loss_acc = _loss_grad(x, target)
    loss = loss_acc[0, 0]

    gbig = [{} for _ in range(DEPTH)]
    marks = [{} for _ in range(DEPTH)]
    gsmall = {n: [None] * DEPTH for n in SMALL if n != "rel_table"}
    dtable_a = jnp.zeros((N_HEADS, NUM_BUCKETS), F32)
    dtable_b = jnp.zeros((N_HEADS, NUM_BUCKETS), F32)

    for l in reversed(range(DEPTH)):
        sv = saved[l]
        w = sv["w"]
        g_mix, g_ffn, g_ple = (small[n][l][None, :] for n in ("norm_mix_g", "norm_ffn_g", "norm_ple_g"))
        gqa, gka, gqb = (_tile_gain(small[n][l], N_HEADS) for n in ("qnorm_a_g", "knorm_a_g", "qnorm_b_g"))
        gkb = _tile_gain(small["knorm_b_g"][l], N_KV_B)
        sink = jnp.repeat(small["sink_b"][l], BLK_B)[:, None]

        def ple_bwd(dx_, z_, e_):
            s = _sigmoid(z_)
            return dx_ * s, dx_ * e_ * (s * (1.0 - s))

        de, dz = _ew(ple_bwd, [(dx, 0), (sv["z"], 0), (sv["e"], 0)], [BF16, BF16], width=D_MODEL, bw=D_MODEL,
                     name="ple_bwd")
        gbig[l]["w_ple_proj"] = _mm(p[l], de, "tn", F32, "mm_d_ple_proj")
        gbig[l]["w_ple_gate"] = _mm(sv["h3"], dz, "tn", F32, "mm_d_ple_gate")
        dh3 = _mm(dz, w["w_ple_gate"], "nt", F32, "mm_dh3")
        dx, dxb, gsmall["norm_ple_g"][l] = _rms_bwd(sv["x2"], g_ple, dh3, dx, "rms_ple_bwd")

        dhid = _mm(dxb, w["w_ffn_down"], "nt", F32, "mm_dhid")
        gbig[l]["w_ffn_down"] = _mm(sv["hid"], dxb, "tn", F32, "mm_d_ffn_down")

        def swiglu_bwd(a_, u_, dh_):
            s = _sigmoid(a_)
            return dh_ * u_ * (s * (1.0 + a_ * (1.0 - s))), dh_ * (a_ * s)

        da, du = _ew(swiglu_bwd, [(sv["a"], 0), (sv["u"], 0), (dhid, 0)], [BF16, BF16], width=D_FF, bw=D_FF,
                     name="swiglu_bwd")
        gbig[l]["w_ffn_gate"] = _mm(da, sv["h2"], "tn", F32, "mm_d_ffn_gate")
        gbig[l]["w_ffn_up"] = _mm(du, sv["h2"], "tn", F32, "mm_d_ffn_up")
        dh2 = _mm(da, w["w_ffn_gate"], "nn", F32, "mm_dh2_gate")
        dh2 = _mm(du, w["w_ffn_up"], "nn", F32, "mm_dh2_up", res=dh2)
        dx, dxb, gsmall["norm_ffn_g"][l] = _rms_bwd(sv["x1"], g_ffn, dh2, dx, "rms_ffn_bwd")

        dmerged = _mm(dxb, w["w_out"], "nt", F32, "mm_dmerged")
        marks[l]["ffn_bwd_done"] = dmerged
        gbig[l]["w_out"] = _mm(sv["merged"], dxb, "tn", F32, "mm_d_out")

        def gate_bwd(dm_, ca_, cb_, ga_, gb_):
            sa, sb = _sigmoid(ga_), _sigmoid(gb_)
            return dm_ * sa, dm_ * sb, dm_ * ca_ * (sa * (1.0 - sa)), dm_ * cb_ * (sb * (1.0 - sb))

        dca, dcb, dga, dgb = _ew(gate_bwd, [(dmerged, 0), (sv["ca"], 0), (sv["cb"], 0), (sv["proj"], OFF_GA),
                                            (sv["proj"], OFF_GB)], [BF16, BF16, BF16, BF16],
                                 width=D_MODEL, bw=256, name="gate_bwd")
        dgab = jnp.concatenate([dga, dgb], axis=1)
        gbig[l]["w_branch_a"] = _mm(sv["ya"], dca, "tn", F32, "mm_d_branch_a")
        gbig[l]["w_branch_b"] = _mm(sv["yb"], dcb, "tn", F32, "mm_d_branch_b")
        dya = _mm(dca, w["w_branch_a"], "nt", F32, "mm_dya")
        dyb = _mm(dcb, w["w_branch_b"], "nt", F32, "mm_dyb")

        dya, delta_a = _attn_bwd_prep(dya, sv["ya"], "attn_a_bwd_prep")
        dyb, delta_b = _attn_bwd_prep(dyb, sv["yb"], "attn_b_bwd_prep")

        dqa, dka, dva = [], [], []
        for (blk, d), bias, bk in zip(DILATED, bias_a, buckets_a):
            dq_, dk_, dv_, db_ = _attn_bwd(sv["qa"], sv["ka"], sv["va"], dya, sv["lse_a"], delta_a, bias, None, blk, d,
                                           f"attn_a{d}_bwd")
            dqa.append(dq_)
            dka.append(dk_)
            dva.append(dv_)
            dtable_a = dtable_a + _table_grad(db_.reshape(N_HEADS, blk, 3 * blk), bk, "table_grad_a")
        dqb, dkb, dvb, db_, dsink = _attn_bwd(sv["qb"], sv["kb"], sv["vb"], dyb, sv["lse_b"], delta_b, bias_b, sink,
                                              BLK_B, 1, "attn_b_bwd")
        dtable_b = dtable_b + _table_grad(db_.reshape(N_HEADS, BLK_B, 3 * BLK_B), buckets_b, "table_grad_b")
        gsmall["sink_b"][l] = dsink.reshape(N_HEADS, BLK_B).sum(axis=1)

        dproj, pqa, pka, pqb, pkb = _qknorm_bwd(sv["proj"], gqa, gka, gqb, gkb, dqa, dka, dva, dqb, dkb, dvb, dgab)
        marks[l]["attn_bwd_done"] = dproj
        gsmall["qnorm_a_g"][l] = pqa.reshape(N_HEADS, HEAD_DIM).sum(0)
        gsmall["knorm_a_g"][l] = pka.reshape(N_HEADS, HEAD_DIM).sum(0)
        gsmall["qnorm_b_g"][l] = pqb.reshape(N_HEADS, HEAD_DIM).sum(0)
        gsmall["knorm_b_g"][l] = pkb.reshape(N_KV_B, HEAD_DIM).sum(0)
        gbig[l]["w_in"] = _mm(dproj, sv["h"], "tn", F32, "mm_d_in")
        dh = _mm(dproj, w["w_in"], "nn", F32, "mm_dh")
        dx, _, gsmall["norm_mix_g"][l] = _rms_bwd(sv["x0"], g_mix, dh, dx, "rms_mix_bwd")
        gsmall["norm_mix_g"][l] = gsmall["norm_mix_g"][l][0]
        gsmall["norm_ffn_g"][l] = gsmall["norm_ffn_g"][l][0]
        gsmall["norm_ple_g"][l] = gsmall["norm_ple_g"][l][0]

    gsmall = {n: jnp.stack(v) for n, v in gsmall.items()}
    gsmall["rel_table"] = jnp.concatenate([dtable_a, dtable_b], axis=0).T
    return loss, dx, gbig, gsmall, marks


def _place():
    return lax.axis_index("x"), lax.axis_index("y"), lax.axis_index("c")


def _flip(v, bit):
    return 1 - v if bit else v


CHIP_RELATIONS = ((0, 1), (1, 0), (1, 1))
ANY = pl.BlockSpec(memory_space=pl.ANY)


def _allgather_body(w_refs, out_refs, send_sems, recv_sems):
    x, y, c = _place()
    chips = [(_flip(x, a), _flip(y, b)) for a, b in CHIP_RELATIONS]

    def make(g):
        w_ref, out_ref = w_refs[g], out_refs[g]
        half = w_ref.shape[0] // 2

        def part(px, py, pc):
            return out_ref.at[2 * px + py, pl.ds(pc * half, half), :]

        def copy(k, block, to, src=None):
            return pltpu.make_async_remote_copy(
                src_ref=part(*block) if src is None else src, dst_ref=part(*block),
                send_sem=send_sems.at[7 * g + k], recv_sem=recv_sems.at[7 * g + k], device_id=to,
                device_id_type=MESH_ID)

        own = pltpu.make_async_remote_copy(
            src_ref=w_ref, dst_ref=out_ref.at[2 * x + y], send_sem=send_sems.at[7 * g + 6],
            recv_sem=recv_sems.at[7 * g + 6], device_id=(x, y, 1 - c), device_id_type=MESH_ID)
        first = [copy(k, (x, y, c), (*chip, c), src=w_ref.at[pl.ds(c * half, half), :]) for k, chip in enumerate(chips)]
        passed = [copy(3 + k, (*chip, c), (x, y, 1 - c)) for k, chip in enumerate(chips)]
        arrive = [copy(k, (*chip, c), (x, y, c)) for k, chip in enumerate(chips)]
        arrive2 = [copy(3 + k, (*chip, 1 - c), (x, y, c)) for k, chip in enumerate(chips)]
        return own, first, passed, arrive, arrive2

    made = [make(g) for g in range(len(w_refs))]
    for own, first, _, _, _ in made:
        own.start()
        for cp in first:
            cp.start()
    for _, _, passed, arrive, _ in made:
        for k in range(3):
            arrive[k].wait_recv()
            passed[k].start()
    for own, first, passed, _, arrive2 in made:
        for k in range(3):
            arrive2[k].wait_recv()
        own.wait_recv()
        for cp in first + passed + [own]:
            cp.wait_send()


def _sibling(x, y, c):
    return [(x, y, 1 - c)]


def _same_core_of_other_chips(x, y, c):
    return [(_flip(x, a), _flip(y, b), c) for a, b in CHIP_RELATIONS]


def _exchange(body, ins, out_types, n_sems, name, sequencer=None):
    n = len(ins)
    sems = (pltpu.SemaphoreType.DMA((n_sems,)), pltpu.SemaphoreType.DMA((n_sems,)))
    if sequencer is None:
        in_place = out_types is None
        out_shape = [jax.ShapeDtypeStruct(a.shape, a.dtype) for a in ins] if in_place else out_types

        def tc_body(*refs):
            body(refs[:n], refs[n:n + len(out_shape)], refs[-2], refs[-1])

        return list(pl.pallas_call(
            tc_body, out_shape=out_shape, in_specs=[ANY] * n, out_specs=[ANY] * len(out_shape),
            input_output_aliases={g: g for g in range(n)} if in_place else {}, scratch_shapes=list(sems), name=name)(*ins))

    collective_id, peers = sequencer
    hbm = pltpu.MemorySpace.HBM
    in_refs = [jax.new_ref(a, memory_space=hbm) for a in ins]
    out_refs = in_refs if out_types is None else [jax.empty_ref(t, memory_space=hbm) for t in out_types]

    @pl.kernel(mesh=plsc.ScalarSubcoreMesh(axis_name="sequencer", num_cores=1), name=name, scratch_types=sems,
               compiler_params=pltpu.CompilerParams(collective_id=collective_id))
    def launch(send_sems, recv_sems):
        barrier = pltpu.get_barrier_semaphore()
        devices = peers(*_place())
        for device in devices:
            pl.semaphore_signal(barrier, inc=1, device_id=device, device_id_type=MESH_ID)
        pl.semaphore_wait(barrier, len(devices))
        body(in_refs, out_refs, send_sems, recv_sems)

    launch()
    return [r[...] for r in out_refs]


def _allgather(shards, name, sequencer=None):
    out_types = [jax.ShapeDtypeStruct((N_CHIPS,) + s.shape, s.dtype) for s in shards]
    if sequencer is not None:
        sequencer = (sequencer, lambda x, y, c: _sibling(x, y, c) + _same_core_of_other_chips(x, y, c))
    return _exchange(_allgather_body, shards, out_types, 7 * len(shards), name, sequencer)


def _half_tile(half):
    return max(t for t in range(16, 1025, 16) if half % t == 0)


def _run_copies(cps):
    for cp in cps:
        cp.start()
    for cp in cps:
        cp.wait_recv()
    for cp in cps:
        cp.wait_send()


def _sibling_halves(gsends, name, sequencer=None):
    def body(g_refs, out_refs, send_sems, recv_sems):
        x, y, c = _place()
        cps = []
        for g, (g_ref, out_ref) in enumerate(zip(g_refs, out_refs)):
            half = g_ref.shape[1] // 2
            cps.append(pltpu.make_async_remote_copy(
                src_ref=g_ref.at[:, pl.ds((1 - c) * half, half), :], dst_ref=out_ref,
                send_sem=send_sems.at[g], recv_sem=recv_sems.at[g], device_id=(x, y, 1 - c), device_id_type=MESH_ID))
        _run_copies(cps)

    out_types = [jax.ShapeDtypeStruct((s.shape[0], s.shape[1] // 2, s.shape[2]), s.dtype) for s in gsends]
    return _exchange(body, gsends, out_types, len(gsends), name, sequencer and (sequencer, _sibling))


def _chip_sums(gsend, sib, place):
    n, rows, cols = gsend.shape
    half = rows // 2
    tm = _half_tile(half)
    nblk = half // tm

    def body(s_ref, g_ref, sib_ref, o_ref):
        o_ref[0] = (g_ref[0].astype(F32) + sib_ref[0].astype(F32)).astype(o_ref.dtype)

    grid_spec = pltpu.PrefetchScalarGridSpec(
        num_scalar_prefetch=1, grid=(n, nblk),
        in_specs=[pl.BlockSpec((1, tm, cols), lambda k, i, s: (jnp.bitwise_xor(s[0], k), s[1] * nblk + i, 0)),
                  pl.BlockSpec((1, tm, cols), lambda k, i, s: (jnp.bitwise_xor(s[0], k), i, 0))],
        out_specs=pl.BlockSpec((1, tm, cols), lambda k, i, s: (k, i, 0)))
    return pl.pallas_call(
        body, out_shape=jax.ShapeDtypeStruct((n, half, cols), BF16), grid_spec=grid_spec,
        name="rs_chip_sums", compiler_params=_params("parallel", "parallel"))(place, gsend, sib)


def _exchange_chip_sums(tsends, name, sequencer=None):
    def body(t_refs, out_refs, send_sems, recv_sems):
        x, y, c = _place()
        cps = []
        for g, (t_ref, out_ref) in enumerate(zip(t_refs, out_refs)):
            for k, device in enumerate(_same_core_of_other_chips(x, y, c)):
                cps.append(pltpu.make_async_remote_copy(
                    src_ref=t_ref.at[k + 1], dst_ref=out_ref.at[k], send_sem=send_sems.at[3 * g + k],
                    recv_sem=recv_sems.at[3 * g + k], device_id=device, device_id_type=MESH_ID))
        _run_copies(cps)

    out_types = [jax.ShapeDtypeStruct((3,) + s.shape[1:], s.dtype) for s in tsends]
    return _exchange(body, tsends, out_types, 3 * len(tsends), name,
                     sequencer and (sequencer, _same_core_of_other_chips))


def _final_sum(tsend, recv, place):
    n, half, cols = tsend.shape
    tm = _half_tile(half)
    nblk = half // tm

    def body(s_ref, t_ref, r_ref, o_ref):
        o_ref[...] = ((t_ref[0].astype(F32) + r_ref[0].astype(F32)) + r_ref[1].astype(F32)) + r_ref[2].astype(F32)

    grid_spec = pltpu.PrefetchScalarGridSpec(
        num_scalar_prefetch=1, grid=(nblk,),
        in_specs=[pl.BlockSpec((1, tm, cols), lambda i, s: (0, i, 0)), pl.BlockSpec((n - 1, tm, cols), lambda i, s: (0, i, 0))],
        out_specs=pl.BlockSpec((tm, cols), lambda i, s: (s[1] * nblk + i, 0)))
    return pl.pallas_call(
        body, out_shape=jax.ShapeDtypeStruct((2 * half, cols), F32), grid_spec=grid_spec, name="rs_final_sum",
        compiler_params=_params("parallel"))(place, tsend, recv)


def _join_halves(gfulls, name, sequencer=None):
    def body(g_refs, out_refs, send_sems, recv_sems):
        x, y, c = _place()
        n = len(g_refs)

        def copy(g, pc):
            half = g_refs[g].shape[0] // 2
            return pltpu.make_async_remote_copy(
                src_ref=g_refs[g].at[pl.ds(pc * half, half), :], dst_ref=out_refs[g].at[pl.ds(pc * half, half), :],
                send_sem=send_sems.at[g], recv_sem=recv_sems.at[g], device_id=(x, y, 1 - c), device_id_type=MESH_ID)

        mine = [copy(g, c) for g in range(n)]
        for cp in mine:
            cp.start()
        for g in range(n):
            copy(g, 1 - c).wait_recv()
        for cp in mine:
            cp.wait_send()

    return _exchange(body, gfulls, None, len(gfulls), name, sequencer and (sequencer, _sibling))


def _allreduce_small(v):
    rows, cols = v.shape

    def body(v_ref, out_ref, buf, send_sems, recv_sems):
        x, y, c = _place()
        cps = []
        for k in range(1, 8):
            peer = (_flip(x, (k >> 2) & 1), _flip(y, (k >> 1) & 1), _flip(c, k & 1))
            cps.append(pltpu.make_async_remote_copy(
                src_ref=v_ref, dst_ref=buf.at[k - 1], send_sem=send_sems.at[k - 1], recv_sem=recv_sems.at[k - 1],
                device_id=peer, device_id_type=MESH_ID))
        for cp in cps:
            cp.start()
        for cp in cps:
            cp.wait_recv()
        for cp in cps:
            cp.wait_send()
        t0 = v_ref[...] + buf[0]
        t1 = buf[1] + buf[2]
        t2 = buf[3] + buf[4]
        t3 = buf[5] + buf[6]
        out_ref[...] = (t0 + t1) + (t2 + t3)

    vm = pl.BlockSpec(memory_space=pltpu.VMEM)
    return pl.pallas_call(
        body, out_shape=jax.ShapeDtypeStruct((rows, cols), F32), in_specs=[vm], out_specs=vm,
        scratch_shapes=[pltpu.VMEM((7, rows, cols), F32), pltpu.SemaphoreType.DMA((7,)), pltpu.SemaphoreType.DMA((7,))],
        name="allreduce_small")(v)


BIG_INFO = {n: (shape, ax) for n, shape, ax in BIG}
GROUPS = (("w_in",), ("w_ffn_gate", "w_ffn_up", "w_ffn_down", "w_out", "w_ple_gate"),
          ("w_branch_a", "w_branch_b", "w_ple_proj"))


def _shard_shape(name):
    (k, m), ax = BIG_INFO[name]
    return (k // N_CHIPS, m) if ax == 0 else (k, m // N_CHIPS)


def _group_rows(group):
    offs, off = {}, 0
    for n in group:
        offs[n] = off
        off += _shard_shape(n)[0]
    return offs, off


def _pack_groups(shards, layer, dtype):
    return [jnp.concatenate([shards[n][layer].astype(dtype) for n in group], axis=0) for group in GROUPS]


def _unpack_full(gathered, groups):
    out = {}
    for group, arr in zip(groups, gathered):
        offs, _ = _group_rows(group)
        for n in group:
            rows, cols = _shard_shape(n)
            (k, m), ax = BIG_INFO[n]
            slab = arr[:, offs[n]:offs[n] + rows]
            out[n] = slab.reshape(k, m) if ax == 0 else jnp.transpose(slab, (1, 0, 2)).reshape(k, m)
    return out


def _pack_grads(gfull):
    out = []
    for group in GROUPS:
        parts = []
        for n in group:
            rows, cols = _shard_shape(n)
            ax = BIG_INFO[n][1]
            slab = (gfull[n].reshape(N_CHIPS, rows, cols) if ax == 0
                    else jnp.transpose(gfull[n].reshape(rows, N_CHIPS, cols), (1, 0, 2)))
            parts.append(slab.astype(BF16))
        out.append(jnp.concatenate(parts, axis=1))
    return out


def _after(values, mark):
    values, _ = lax.optimization_barrier((values, mark))
    return values


def _reduce_scatter(gsends, place, tag, sequencer_ids=None, hold=None):
    ids = sequencer_ids or (None, None, None)
    sibs = _sibling_halves(gsends, "rs_sibling_halves_" + tag, ids[0])
    tsends = [_chip_sums(g, s, place) for g, s in zip(gsends, sibs)]
    recvs = _exchange_chip_sums(tsends, "rs_exchange_" + tag, ids[1])
    if hold is not None:
        recvs = _after(recvs, hold)
    return _join_halves([_final_sum(t, r, place) for t, r in zip(tsends, recvs)], "rs_join_halves_" + tag, ids[2])


SMALL_SHAPES = {"rel_table": (NUM_BUCKETS, 2 * N_HEADS), "norm_mix_g": (DEPTH, D_MODEL), "qnorm_a_g": (DEPTH, HEAD_DIM),
                "knorm_a_g": (DEPTH, HEAD_DIM), "qnorm_b_g": (DEPTH, HEAD_DIM), "knorm_b_g": (DEPTH, HEAD_DIM),
                "sink_b": (DEPTH, N_HEADS), "norm_ffn_g": (DEPTH, D_MODEL), "norm_ple_g": (DEPTH, D_MODEL)}


def _pack_small(vals):
    flat = jnp.concatenate([vals[n].astype(F32).reshape(-1) for n in SMALL])
    flat = jnp.concatenate([flat, jnp.zeros((SMALL_ROWS * LANES - flat.shape[0],), F32)])
    return flat.reshape(SMALL_ROWS, LANES)


def _unpack_small(packed):
    flat, out, off = packed.reshape(-1), {}, 0
    for n in SMALL:
        size = math.prod(SMALL_SHAPES[n])
        out[n] = flat[off:off + size].reshape(SMALL_SHAPES[n])
        off += size
    return out


def _adamw(w, gs, g_row, m, v, name):
    c1 = 1.0 - ADAM_B1 ** ADAM_STEP
    c2 = 1.0 - ADAM_B2 ** ADAM_STEP
    total, width = w.shape
    n_layers = len(gs)
    per = total // n_layers
    tm = max(t for t in range(8, 513, 8) if per % t == 0 and g_row % t == 0)
    nblk = per // tm

    def body(*refs):
        w_ref, g_refs = refs[0], refs[1:1 + n_layers]
        m_ref, v_ref, og, od, om, ov = refs[1 + n_layers:]
        layer = pl.program_id(0) // nblk
        g = g_refs[0][...]
        for l in range(1, n_layers):
            g = jnp.where(layer == l, g_refs[l][...], g)
        m_new = ADAM_B1 * m_ref[...] + (1.0 - ADAM_B1) * g
        v_new = ADAM_B2 * v_ref[...] + (1.0 - ADAM_B2) * (g * g)
        og[...] = g
        od[...] = -ADAM_LR * ((m_new / c1) / (jnp.sqrt(v_new / c2) + ADAM_EPS) + ADAM_WD * w_ref[...])
        om[...] = m_new
        ov[...] = v_new

    row = pl.BlockSpec((tm, width), lambda i: (i, 0))
    g_specs = [pl.BlockSpec((tm, width), lambda i, l=l: (g_row // tm + jnp.clip(i - l * nblk, 0, nblk - 1), 0))
               for l in range(n_layers)]
    return pl.pallas_call(
        body, out_shape=[jax.ShapeDtypeStruct((total, width), F32)] * 4, grid=(total // tm,),
        in_specs=[row] + g_specs + [row, row], out_specs=[row] * 4, name=name,
        compiler_params=_params("parallel"))(w, *gs, m, v)


def kernel(x, p, rel_table, norm_mix_g, w_in, qnorm_a_g, knorm_a_g, qnorm_b_g, knorm_b_g, sink_b, w_branch_a, w_branch_b, w_out, norm_ffn_g, w_ffn_gate, w_ffn_up, w_ffn_down, norm_ple_g, w_ple_gate, w_ple_proj, loss_target, m_rel_table, m_norm_mix_g, m_w_in, m_qnorm_a_g, m_knorm_a_g, m_qnorm_b_g, m_knorm_b_g, m_sink_b, m_w_branch_a, m_w_branch_b, m_w_out, m_norm_ffn_g, m_w_ffn_gate, m_w_ffn_up, m_w_ffn_down, m_norm_ple_g, m_w_ple_gate, m_w_ple_proj, v_rel_table, v_norm_mix_g, v_w_in, v_qnorm_a_g, v_knorm_a_g, v_qnorm_b_g, v_knorm_b_g, v_sink_b, v_w_branch_a, v_w_branch_b, v_w_out, v_norm_ffn_g, v_w_ffn_gate, v_w_ffn_up, v_w_ffn_down, v_norm_ple_g, v_w_ple_gate, v_w_ple_proj):
    given = dict(locals())

    def held(name, a):
        return jnp.swapaxes(a, 1, 2) if name in TRANSPOSED else a

    weights = {n: held(n, given[n]) for n in WEIGHTS}
    moments_m = {n: held(n, given["m_" + n]) for n in WEIGHTS}
    moments_v = {n: held(n, given["v_" + n]) for n in WEIGHTS}
    xi, yi, ci = _place()
    place = jnp.stack([2 * xi + yi, ci]).astype(jnp.int32)

    shards = [_pack_groups(weights, l, BF16) for l in range(DEPTH)]
    w_in0 = _allgather(shards[0][:1], "allgather_w_in_layer0", sequencer=9)
    rest0 = _allgather(_after(shards[0][1:], w_in0), "allgather_rest_layer0", sequencer=1)
    gathered = [w_in0 + rest0, None]
    small = {n: weights[n] for n in SMALL}

    def w_in_of(l, mark):
        return _unpack_full(gathered[l][:1] if l == 0 else _after(gathered[l][:1], mark), GROUPS[:1])["w_in"]

    def rest_of(l, mark):
        if l == 0:
            gathered[1] = _allgather(_after(shards[1], mark), "allgather_layer1", sequencer=2)
        return _unpack_full(_after(gathered[l][1:], mark), GROUPS[1:])

    loss, dx, gbig, gsmall, marks = _local_step(x[0], p[:, 0], loss_target[0], w_in_of, rest_of, small)

    gsends = [_pack_grads(gbig[l]) for l in range(DEPTH)]
    red1 = _reduce_scatter(gsends[1], place, "layer1", (3, 4, 5), hold=marks[0]["attn_bwd_done"])
    rest0 = _reduce_scatter(gsends[0][1:], place, "rest_layer0", (6, 7, 8), hold=marks[0]["attn_bwd_done"])

    grads, delta, new_m, new_v = {}, {}, {}, {}

    def update(group, reduced):
        offs, _ = _group_rows(group)
        for n in group:
            shape = weights[n].shape
            two_d = lambda a: a.reshape(shape[0] * shape[1], shape[2])
            outs = _adamw(two_d(weights[n]), reduced, offs[n], two_d(moments_m[n]), two_d(moments_v[n]), "adamw_" + n)
            grads[n], delta[n], new_m[n], new_v[n] = (held(n, o.reshape(shape)) for o in outs)

    for gi in (1, 2):
        update(GROUPS[gi], [rest0[gi - 1], red1[gi]])
    w_in_red = _reduce_scatter(gsends[0][:1], place, "w_in_layer0", (10, 11, 12), hold=delta[GROUPS[2][-1]])
    update(GROUPS[0], [w_in_red[0], red1[0]])
    small_grads = _allreduce_small(_pack_small(gsmall))
    g_, d_, m_, v_ = _adamw(_pack_small(weights), [small_grads], 0, _pack_small(moments_m), _pack_small(moments_v),
                            "adamw_small")
    grads.update(_unpack_small(g_))
    delta.update(_unpack_small(d_))
    new_m.update(_unpack_small(m_))
    new_v.update(_unpack_small(v_))

    loss = lax.psum(loss, ("x", "y", "c"))
    return (loss, dx[None], *[grads[n] for n in WEIGHTS], *[delta[n] for n in WEIGHTS],
            *[new_m[n] for n in WEIGHTS], *[new_v[n] for n in WEIGHTS])
```

```python
import functools
import math

import jax
import jax.numpy as jnp
from jax import lax
from jax.experimental import pallas as pl
from jax.experimental.pallas import tpu as pltpu
from jax.experimental.pallas import tpu_sc as plsc

F32 = jnp.float32
BF16 = jnp.bfloat16
MESH_ID = pl.DeviceIdType.MESH

SEQ = 2048
D_MODEL = 1024
DEPTH = 2
HEAD_DIM = 64
N_HEADS = 8
WIDTH = N_HEADS * HEAD_DIM
N_PAIRS = 4
ITEMS = 4
N_KV_B = 2
PLE_DIM = 256
D_FF = 2816
D_IN = 4352
OFF_QA, OFF_KA, OFF_VA, OFF_QB, OFF_KB, OFF_VB, OFF_GA, OFF_GB = 0, 512, 1024, 1536, 2048, 2176, 2304, 3328
DILATED = ((64, 1), (64, 4), (64, 16))
BLK_B = 128
NUM_BUCKETS = 32
MAX_DISTANCE = 1024
RMS_EPS = 1e-6
NEG_INF = -1e30
LANES = 128
VMEM_LIMIT = 48 * 1024 * 1024

ADAM_LR, ADAM_B1, ADAM_B2, ADAM_EPS, ADAM_WD, ADAM_STEP = 0.001, 0.9, 0.999, 1e-08, 0.01, 10

TRANSPOSED = ("w_in", "w_ffn_gate", "w_ffn_up")
BIG = (
    ("w_in", (D_IN, D_MODEL), 0),
    ("w_branch_a", (WIDTH, D_MODEL), 1),
    ("w_branch_b", (WIDTH, D_MODEL), 1),
    ("w_out", (D_MODEL, D_MODEL), 0),
    ("w_ffn_gate", (D_FF, D_MODEL), 0),
    ("w_ffn_up", (D_FF, D_MODEL), 0),
    ("w_ffn_down", (D_FF, D_MODEL), 0),
    ("w_ple_gate", (D_MODEL, D_MODEL), 0),
    ("w_ple_proj", (PLE_DIM, D_MODEL), 1),
)
SMALL = ("rel_table", "norm_mix_g", "qnorm_a_g", "knorm_a_g", "qnorm_b_g", "knorm_b_g", "sink_b",
         "norm_ffn_g", "norm_ple_g")
WEIGHTS = ("rel_table", "norm_mix_g", "w_in", "qnorm_a_g", "knorm_a_g", "qnorm_b_g", "knorm_b_g", "sink_b",
           "w_branch_a", "w_branch_b", "w_out", "norm_ffn_g", "w_ffn_gate", "w_ffn_up", "w_ffn_down",
           "norm_ple_g", "w_ple_gate", "w_ple_proj")
N_CHIPS = 4
SMALL_ROWS = 64


def _params(*sem):
    return pltpu.CompilerParams(dimension_semantics=sem, vmem_limit_bytes=VMEM_LIMIT)


def _pick(dim, target):
    for t in (target, 512, 256, 128, 64, 32, 16, 8):
        if t <= target and dim % t == 0:
            return t
    return dim


MM_VMEM_BUDGET = 40 * 1024 * 1024
STEP_OVERHEAD_S = 0.4e-6
TILE_DMA_BYTES_PER_S = 1.5e12
MXU_FLOPS_PER_S = 7e14


def _mm_tiles(m, n, k, a_bytes, b_bytes, out_bytes, has_res):
    best = None
    for tm in (t for t in range(LANES, m + 1, LANES) if m % t == 0):
        for tn in (t for t in range(LANES, n + 1, LANES) if n % t == 0):
            io = tm * k * a_bytes + tn * k * b_bytes + tm * tn * (out_bytes + (4 if has_res else 0))
            casts = (tm * k * 2 if a_bytes == 4 else 0) + (tn * k * 2 if b_bytes == 4 else 0)
            if 2 * io + tm * tn * 4 + casts > MM_VMEM_BUDGET:
                continue
            steps = (m // tm) * (n // tn)
            cost = steps * STEP_OVERHEAD_S + io / TILE_DMA_BYTES_PER_S + 2.0 * m * n * k / MXU_FLOPS_PER_S
            if best is None or (cost, -tm) < best[0]:
                best = ((cost, -tm), tm, tn)
    return best[1], best[2]


def _mm(a, b, mode, out_dtype, name, res=None):
    if mode == "nn":
        (m, k), (_, n) = a.shape, b.shape
    elif mode == "nt":
        (m, k), (n, _) = a.shape, b.shape
    else:
        (k, m), (_, n) = a.shape, b.shape
    tm, tn = _mm_tiles(m, n, k, a.dtype.itemsize, b.dtype.itemsize, jnp.dtype(out_dtype).itemsize, res is not None)
    a_spec = pl.BlockSpec((k, tm), lambda i, j: (0, i)) if mode == "tn" else pl.BlockSpec((tm, k), lambda i, j: (i, 0))
    b_spec = pl.BlockSpec((tn, k), lambda i, j: (j, 0)) if mode == "nt" else pl.BlockSpec((k, tn), lambda i, j: (0, j))
    dims = {"nn": (((1,), (0,)), ((), ())), "nt": (((1,), (1,)), ((), ())), "tn": (((0,), (0,)), ((), ()))}[mode]
    has_res = res is not None

    def body(*refs):
        a_ref, b_ref = refs[0], refs[1]
        o_ref = refs[-1]
        acc = lax.dot_general(a_ref[...].astype(BF16), b_ref[...].astype(BF16), dims, preferred_element_type=F32)
        if has_res:
            acc = acc + refs[2][...]
        o_ref[...] = acc.astype(out_dtype)

    in_specs = [a_spec, b_spec]
    args = [a, b]
    if has_res:
        in_specs.append(pl.BlockSpec((tm, tn), lambda i, j: (i, j)))
        args.append(res)
    return pl.pallas_call(
        body, out_shape=jax.ShapeDtypeStruct((m, n), out_dtype), grid=(m // tm, n // tn), in_specs=in_specs,
        out_specs=pl.BlockSpec((tm, tn), lambda i, j: (i, j)), name=name,
        compiler_params=_params("parallel", "parallel"))(*args)


def _ew(fn, ins, out_dtypes, *, width, bw, name, vecs=(), tm=256):
    rows = ins[0][0].shape[0]
    tm = _pick(rows, tm)
    n_in = len(ins) + len(vecs)

    def col_map(off_blocks):
        return lambda i, j: (i, off_blocks + j)

    in_specs = [pl.BlockSpec((tm, bw), col_map(off // bw)) for _, off in ins]
    in_specs += [pl.BlockSpec((1, bw), lambda i, j: (0, j)) for _ in vecs]

    def body(*refs):
        outs = fn(*[r[...] for r in refs[:n_in]])
        for r, o in zip(refs[n_in:], outs):
            r[...] = o.astype(r.dtype)

    return pl.pallas_call(
        body, out_shape=[jax.ShapeDtypeStruct((rows, width), dt) for dt in out_dtypes],
        grid=(rows // tm, width // bw), in_specs=in_specs,
        out_specs=[pl.BlockSpec((tm, bw), lambda i, j: (i, j)) for _ in out_dtypes],
        name=name, compiler_params=_params("parallel", "parallel"))(*[a for a, _ in ins], *vecs)


def _sigmoid(x):
    return 1.0 / (1.0 + jnp.exp(-x))


def _seg_sum(v):
    outs = []
    for k in range(v.shape[1] // LANES):
        vp = v[:, k * LANES:(k + 1) * LANES]
        left = lax.broadcasted_iota(jnp.int32, vp.shape, 1) < HEAD_DIM
        sl = jnp.sum(jnp.where(left, vp, 0.0), axis=-1, keepdims=True)
        sr = jnp.sum(jnp.where(left, 0.0, vp), axis=-1, keepdims=True)
        outs.append(jnp.where(left, sl, sr))
    return outs[0] if len(outs) == 1 else jnp.concatenate(outs, axis=1)


def _seg_rstd(x):
    return lax.rsqrt(_seg_sum(x * x) * (1.0 / HEAD_DIM) + RMS_EPS)


def _rms_fwd(x, g, name):
    rows, d = x.shape
    tm = 256

    def body(x_ref, g_ref, h_ref):
        xv = x_ref[...]
        r = lax.rsqrt(jnp.mean(xv * xv, axis=-1, keepdims=True) + RMS_EPS)
        h_ref[...] = ((xv * r) * g_ref[...]).astype(BF16)

    return pl.pallas_call(
        body, out_shape=jax.ShapeDtypeStruct((rows, d), BF16), grid=(rows // tm,),
        in_specs=[pl.BlockSpec((tm, d), lambda i: (i, 0)), pl.BlockSpec((1, d), lambda i: (0, 0))],
        out_specs=pl.BlockSpec((tm, d), lambda i: (i, 0)), name=name, compiler_params=_params("parallel"))(x, g)


def _rms_bwd(x, g, dh, dres, name):
    rows, d = x.shape
    tm = 256

    def body(x_ref, g_ref, dh_ref, dres_ref, dx_ref, dxb_ref, dg_ref):
        xv = x_ref[...]
        r = lax.rsqrt(jnp.mean(xv * xv, axis=-1, keepdims=True) + RMS_EPS)
        xh = xv * r
        dhv = dh_ref[...]
        dxh = dhv * g_ref[...]
        dxv = dres_ref[...] + r * (dxh - xh * jnp.mean(dxh * xh, axis=-1, keepdims=True))
        dx_ref[...] = dxv
        dxb_ref[...] = dxv.astype(BF16)
        part = jnp.sum(dhv * xh, axis=0, keepdims=True)

        @pl.when(pl.program_id(0) == 0)
        def _():
            dg_ref[...] = part

        @pl.when(pl.program_id(0) > 0)
        def _():
            dg_ref[...] += part

    row = pl.BlockSpec((tm, d), lambda i: (i, 0))
    vec = pl.BlockSpec((1, d), lambda i: (0, 0))
    return pl.pallas_call(
        body, out_shape=[jax.ShapeDtypeStruct((rows, d), F32), jax.ShapeDtypeStruct((rows, d), BF16),
                         jax.ShapeDtypeStruct((1, d), F32)],
        grid=(rows // tm,), in_specs=[row, vec, row, row], out_specs=[row, row, vec],
        name=name, compiler_params=_params("arbitrary"))(x, g, dh, dres)


def _loss_grad(y, t):
    rows, d = y.shape
    tm = 256

    def body(y_ref, t_ref, dy_ref, l_ref):
        e = y_ref[...] - t_ref[...]
        dy_ref[...] = e * (1.0 / d)
        part = jnp.zeros((1, LANES), F32) + jnp.sum(e * e) * (0.5 / d)

        @pl.when(pl.program_id(0) == 0)
        def _():
            l_ref[...] = part

        @pl.when(pl.program_id(0) > 0)
        def _():
            l_ref[...] += part

    row = pl.BlockSpec((tm, d), lambda i: (i, 0))
    return pl.pallas_call(
        body, out_shape=[jax.ShapeDtypeStruct((rows, d), F32), jax.ShapeDtypeStruct((1, LANES), F32)],
        grid=(rows // tm,), in_specs=[row, row], out_specs=[row, pl.BlockSpec((1, LANES), lambda i: (0, 0))],
        name="loss_grad", compiler_params=_params("arbitrary"))(y, t)


def _put_pairs(ref, val):
    for hp in range(N_PAIRS):
        ref[hp] = val[:, hp * LANES:(hp + 1) * LANES].astype(ref.dtype)


def _get_pairs(ref):
    return jnp.concatenate([ref[hp] for hp in range(N_PAIRS)], axis=1)


def _swap_halves(v):
    return pltpu.roll(v, HEAD_DIM, axis=1)


def _expand_kv(kv):
    left = lax.broadcasted_iota(jnp.int32, kv.shape, 1) < HEAD_DIM
    sw = _swap_halves(kv)
    h0 = jnp.where(left, kv, sw)
    h1 = jnp.where(left, sw, kv)
    return jnp.concatenate([h0, h0, h1, h1], axis=1)


def _reduce_kv(dkv):
    left = lax.broadcasted_iota(jnp.int32, (dkv.shape[0], LANES), 1) < HEAD_DIM
    t = dkv[:, 0:LANES] + dkv[:, LANES:2 * LANES]
    u = dkv[:, 2 * LANES:3 * LANES] + dkv[:, 3 * LANES:4 * LANES]
    t = t + _swap_halves(t)
    u = u + _swap_halves(u)
    return jnp.where(left, t, u)


def _qknorm_fwd(proj, gqa, gka, gqb, gkb):
    rows = proj.shape[0]
    tm = 256

    def body(qa_ref, ka_ref, va_ref, qb_ref, kb_ref, vb_ref, gqa_ref, gka_ref, gqb_ref, gkb_ref,
             oqa, oka, ova, oqb, okb, ovb):
        for src, g_ref, dst in ((qa_ref, gqa_ref, oqa), (ka_ref, gka_ref, oka), (qb_ref, gqb_ref, oqb)):
            xv = src[...]
            _put_pairs(dst, (xv * _seg_rstd(xv)) * g_ref[...])
        _put_pairs(ova, va_ref[...])
        kv = kb_ref[...]
        _put_pairs(okb, _expand_kv((kv * _seg_rstd(kv)) * gkb_ref[...]))
        _put_pairs(ovb, _expand_kv(vb_ref[...]))

    def win(width, off):
        return pl.BlockSpec((tm, width), lambda i: (i, off // width))

    vec = lambda w: pl.BlockSpec((1, w), lambda i: (0, 0))
    out = pl.BlockSpec((N_PAIRS, tm, LANES), lambda i: (0, i, 0))
    return pl.pallas_call(
        body, out_shape=[jax.ShapeDtypeStruct((N_PAIRS, rows, LANES), F32)] * 6, grid=(rows // tm,),
        in_specs=[win(WIDTH, OFF_QA), win(WIDTH, OFF_KA), win(WIDTH, OFF_VA), win(WIDTH, OFF_QB),
                  win(LANES, OFF_KB), win(LANES, OFF_VB), vec(WIDTH), vec(WIDTH), vec(WIDTH), vec(LANES)],
        out_specs=[out] * 6, name="qknorm_fwd", compiler_params=_params("parallel"))(
            proj, proj, proj, proj, proj, proj, gqa, gka, gqb, gkb)


def _norm_bwd(xv, g, dy):
    r = _seg_rstd(xv)
    xh = xv * r
    dxh = dy * g
    dx = r * (dxh - xh * (_seg_sum(dxh * xh) * (1.0 / HEAD_DIM)))
    return dx, jnp.sum(dy * xh, axis=0, keepdims=True)


def _qknorm_bwd(proj, gqa, gka, gqb, gkb, dqa, dka, dva, dqb, dkb, dvb, dgab):
    rows = proj.shape[0]
    tm = 256
    n_a = len(dqa)

    def body(*refs):
        qa_ref, ka_ref, qb_ref, kb_ref, gqa_ref, gka_ref, gqb_ref, gkb_ref = refs[:8]
        pos = 8
        dqa_refs, dka_refs, dva_refs = refs[pos:pos + n_a], refs[pos + n_a:pos + 2 * n_a], refs[pos + 2 * n_a:pos + 3 * n_a]
        pos += 3 * n_a
        dqb_ref, dkb_ref, dvb_ref, dgab_ref = refs[pos:pos + 4]
        dproj_ref, ogqa, ogka, ogqb, ogkb = refs[pos + 4:]

        def total(rs):
            acc = _get_pairs(rs[0])
            for r in rs[1:]:
                acc = acc + _get_pairs(r)
            return acc

        dx_qa, p_qa = _norm_bwd(qa_ref[...], gqa_ref[...], total(dqa_refs))
        dx_ka, p_ka = _norm_bwd(ka_ref[...], gka_ref[...], total(dka_refs))
        dx_qb, p_qb = _norm_bwd(qb_ref[...], gqb_ref[...], _get_pairs(dqb_ref))
        dx_kb, p_kb = _norm_bwd(kb_ref[...], gkb_ref[...], _reduce_kv(_get_pairs(dkb_ref)))
        dproj_ref[:, OFF_QA:OFF_QA + WIDTH] = dx_qa.astype(BF16)
        dproj_ref[:, OFF_KA:OFF_KA + WIDTH] = dx_ka.astype(BF16)
        dproj_ref[:, OFF_VA:OFF_VA + WIDTH] = total(dva_refs).astype(BF16)
        dproj_ref[:, OFF_QB:OFF_QB + WIDTH] = dx_qb.astype(BF16)
        dproj_ref[:, OFF_KB:OFF_KB + LANES] = dx_kb.astype(BF16)
        dproj_ref[:, OFF_VB:OFF_VB + LANES] = _reduce_kv(_get_pairs(dvb_ref)).astype(BF16)
        dproj_ref[:, OFF_GA:D_IN] = dgab_ref[...]
        first = pl.program_id(0) == 0
        for o_ref, part in ((ogqa, p_qa), (ogka, p_ka), (ogqb, p_qb), (ogkb, p_kb)):
            @pl.when(first)
            def _(o_ref=o_ref, part=part):
                o_ref[...] = part

            @pl.when(jnp.logical_not(first))
            def _(o_ref=o_ref, part=part):
                o_ref[...] += part

    def win(width, off):
        return pl.BlockSpec((tm, width), lambda i: (i, off // width))

    vec = lambda w: pl.BlockSpec((1, w), lambda i: (0, 0))
    row = lambda w: pl.BlockSpec((tm, w), lambda i: (i, 0))
    in_specs = [win(WIDTH, OFF_QA), win(WIDTH, OFF_KA), win(WIDTH, OFF_QB), win(LANES, OFF_KB),
                vec(WIDTH), vec(WIDTH), vec(WIDTH), vec(LANES)]
    in_specs += [pl.BlockSpec((N_PAIRS, tm, LANES), lambda i: (0, i, 0))] * (3 * n_a + 3) + [row(2 * D_MODEL)]
    return pl.pallas_call(
        body,
        out_shape=[jax.ShapeDtypeStruct((rows, D_IN), BF16), jax.ShapeDtypeStruct((1, WIDTH), F32),
                   jax.ShapeDtypeStruct((1, WIDTH), F32), jax.ShapeDtypeStruct((1, WIDTH), F32),
                   jax.ShapeDtypeStruct((1, LANES), F32)],
        grid=(rows // tm,), in_specs=in_specs,
        out_specs=[row(D_IN), vec(WIDTH), vec(WIDTH), vec(WIDTH), vec(LANES)],
        name="qknorm_bwd", compiler_params=_params("arbitrary"))(
            proj, proj, proj, proj, gqa, gka, gqb, gkb, *dqa, *dka, *dva, dqb, dkb, dvb, dgab)


def _t5_bucket(rel):
    half_b = NUM_BUCKETS // 2
    max_exact = half_b // 2
    sign = jnp.where(rel > 0, half_b, 0)
    n = jnp.abs(rel)
    nf = jnp.maximum(n, 1).astype(F32)
    large = max_exact + (jnp.log(nf / max_exact) / math.log(MAX_DISTANCE / max_exact)
                         * (half_b - max_exact)).astype(jnp.int32)
    large = jnp.minimum(large, half_b - 1)
    return sign + jnp.where(n < max_exact, n, large)


def _band_buckets(blk, dilation):
    i = jnp.arange(blk, dtype=jnp.int32)[:, None]
    j = jnp.arange(3 * blk, dtype=jnp.int32)[None, :]
    rel = j - blk - i
    return jnp.where(jnp.abs(rel) <= blk, _t5_bucket(rel * dilation), -1)


def _bias_tiles(table, buckets, head_off, name):
    blk = buckets.shape[0]

    def body(tab_ref, bk_ref, o_ref):
        h = pl.program_id(0) + head_off
        bk = bk_ref[...]
        acc = jnp.full(bk.shape, NEG_INF, F32)
        for b in range(NUM_BUCKETS):
            acc = jnp.where(bk == b, tab_ref[b, h], acc)
        o_ref[0] = acc

    return pl.pallas_call(
        body, out_shape=jax.ShapeDtypeStruct((N_HEADS, blk, 3 * blk), F32), grid=(N_HEADS,),
        in_specs=[pl.BlockSpec(memory_space=pltpu.SMEM), pl.BlockSpec((blk, 3 * blk), lambda h: (0, 0))],
        out_specs=pl.BlockSpec((1, blk, 3 * blk), lambda h: (h, 0, 0)),
        name=name, compiler_params=_params("parallel"))(table, buckets)


def _table_grad(dbias, buckets, name):
    blk = buckets.shape[0]

    def body(db_ref, bk_ref, o_ref):
        bk = bk_ref[...]
        dbv = db_ref[0]
        lane = lax.broadcasted_iota(jnp.int32, (1, LANES), 1)
        acc = jnp.zeros((1, LANES), F32)
        for b in range(NUM_BUCKETS):
            acc = jnp.where(lane == b, jnp.sum(jnp.where(bk == b, dbv, 0.0)), acc)
        o_ref[0] = acc

    out = pl.pallas_call(
        body, out_shape=jax.ShapeDtypeStruct((N_HEADS, 1, LANES), F32), grid=(N_HEADS,),
        in_specs=[pl.BlockSpec((1, blk, 3 * blk), lambda h: (h, 0, 0)), pl.BlockSpec((blk, 3 * blk), lambda h: (0, 0))],
        out_specs=pl.BlockSpec((1, 1, LANES), lambda h: (h, 0, 0)),
        name=name, compiler_params=_params("parallel"))(dbias, buckets)
    return out[:, 0, :NUM_BUCKETS]


def _dot_nt(a, b):
    return lax.dot_general(a, b, (((1,), (1,)), ((), ())), preferred_element_type=F32)


def _stack_pair(x2, left):
    return jnp.concatenate([jnp.where(left, x2, 0.0), jnp.where(left, 0.0, x2)], axis=0).astype(BF16)


def _attn_geometry(blk, d):
    chunk = blk * ITEMS if d == 1 else blk * d
    groups = 1 if d == 1 else d // ITEMS
    halo = blk if d == 1 else chunk
    return chunk, groups, halo


def _item_rows(ref, j, r0, blk, d):
    if d == 1:
        return ref[j * blk:(j + 1) * blk, :]
    return ref[pl.ds(r0 + j, blk, stride=d), :]


def _item_penalty(t, nct, j, blk, d):
    first_ok, last_ok = t > 0, t < nct - 1
    if d == 1:
        first_ok = True if j > 0 else first_ok
        last_ok = True if j < ITEMS - 1 else last_ok
    col = lax.broadcasted_iota(jnp.int32, (1, 3 * blk), 1)
    ok = jnp.logical_and(jnp.logical_or(col >= blk, first_ok), jnp.logical_or(col < 2 * blk, last_ok))
    return jnp.where(ok, 0.0, NEG_INF).astype(F32)


def _attn_specs(seq, blk, d, step_of):
    chunk, _, halo = _attn_geometry(blk, d)
    per, last = chunk // halo, seq // halo - 1
    cur = pl.BlockSpec((None, chunk, LANES), lambda hp, t: (hp, step_of(t), 0))
    prev = pl.BlockSpec((None, halo, LANES), lambda hp, t: (hp, jnp.clip(step_of(t) * per - 1, 0, last), 0))
    nxt = pl.BlockSpec((None, halo, LANES), lambda hp, t: (hp, jnp.minimum((step_of(t) + 1) * per, last), 0))
    return cur, prev, nxt


def _attn_fwd(q, k, v, bias, sink, blk, d, name):
    _, seq, _ = q.shape
    chunk, groups, _ = _attn_geometry(blk, d)
    nct = seq // chunk
    has_sink = sink is not None
    scale = HEAD_DIM ** -0.5

    def body(*refs):
        q_ref, kp, kc, kn, vp, vc, vn, b_ref = refs[:8]
        s_ref = refs[8] if has_sink else None
        o_ref, l_ref = refs[-2], refs[-1]
        t = pl.program_id(1)
        left = lax.broadcasted_iota(jnp.int32, (1, LANES), 1) < HEAD_DIM
        bias2 = b_ref[...]
        if d == 1:
            kwin = jnp.concatenate([kp[...], kc[...], kn[...]], axis=0).astype(BF16)
            vwin = jnp.concatenate([vp[...], vc[...], vn[...]], axis=0).astype(BF16)

        def group(r0):
            scores, vcats = [], []
            for j in range(ITEMS):
                qs = _stack_pair(_item_rows(q_ref, j, r0, blk, d) * scale, left)
                if d == 1:
                    kcat, vcat = kwin[j * blk:(j + 3) * blk], vwin[j * blk:(j + 3) * blk]
                else:
                    kcat = jnp.concatenate([_item_rows(r, j, r0, blk, d) for r in (kp, kc, kn)], axis=0).astype(BF16)
                    vcat = jnp.concatenate([_item_rows(r, j, r0, blk, d) for r in (vp, vc, vn)], axis=0).astype(BF16)
                scores.append(_dot_nt(qs, kcat) + bias2 + _item_penalty(t, nct, j, blk, d))
                vcats.append(vcat)
            s = jnp.concatenate(scores, axis=0)
            m = jnp.max(s, axis=-1, keepdims=True)
            if has_sink:
                sk = jnp.concatenate([s_ref[...]] * ITEMS, axis=0)
                m = jnp.maximum(m, sk)
            p = jnp.exp(s - m)
            den = jnp.sum(p, axis=-1, keepdims=True)
            if has_sink:
                den = den + jnp.exp(sk - m)
            pn = (p * (1.0 / den)).astype(BF16)
            lse = m + jnp.log(den)
            for j in range(ITEMS):
                top, mid, bot = 2 * j * blk, (2 * j + 1) * blk, (2 * j + 2) * blk
                o2 = jnp.dot(pn[top:bot], vcats[j], preferred_element_type=F32)
                o_val = jnp.where(left, o2[:blk], o2[blk:])
                l_val = jnp.where(left, lse[top:mid], lse[mid:bot])
                if d == 1:
                    o_ref[j * blk:(j + 1) * blk, :] = o_val
                    l_ref[j * blk:(j + 1) * blk, :] = l_val
                else:
                    o_ref[pl.ds(r0 + j, blk, stride=d), :] = o_val
                    l_ref[pl.ds(r0 + j, blk, stride=d), :] = l_val

        if groups == 1:
            group(0)
        else:
            def step(g, carry):
                group(g * ITEMS)
                return carry

            lax.fori_loop(0, groups, step, 0)

    cur, prev, nxt = _attn_specs(seq, blk, d, lambda t: t)
    in_specs = [cur, prev, cur, nxt, prev, cur, nxt, pl.BlockSpec((2 * blk, 3 * blk), lambda hp, t: (hp, 0))]
    args = [q, k, k, k, v, v, v, bias]
    if has_sink:
        in_specs.append(pl.BlockSpec((2 * blk, 1), lambda hp, t: (hp, 0)))
        args.append(sink)
    return pl.pallas_call(
        body, out_shape=[jax.ShapeDtypeStruct(q.shape, F32)] * 2, grid=(N_PAIRS, nct),
        in_specs=in_specs, out_specs=[cur, cur], name=name, compiler_params=_params("parallel", "parallel"))(*args)


def _attn_bwd(q, k, v, do, lse, delta, bias, sink, blk, d, name):
    _, seq, _ = q.shape
    chunk, groups, halo = _attn_geometry(blk, d)
    nct = seq // chunk
    has_sink = sink is not None
    n_in = 12 if has_sink else 11
    scale = HEAD_DIM ** -0.5

    def body(*refs):
        q_ref, kp, kc, kn, vp, vc, vn, do_ref, l_ref, d_ref, b_ref = refs[:11]
        s_ref = refs[11] if has_sink else None
        dq_ref, dk_ref, dv_ref, db_ref = refs[n_in:n_in + 4]
        ds_ref = refs[n_in + 4] if has_sink else None
        wk, wv = refs[-2], refs[-1]
        t = pl.program_id(1)

        @pl.when(t == 0)
        def _():
            wk[...] = jnp.zeros_like(wk)
            wv[...] = jnp.zeros_like(wv)
            db_ref[...] = jnp.zeros_like(db_ref)
            if has_sink:
                ds_ref[...] = jnp.zeros_like(ds_ref)

        @pl.when(t > 0)
        def _():
            for w in (wk, wv):
                keep = w[chunk:2 * chunk + halo]
                w[0:chunk + halo] = keep
                w[chunk + halo:2 * chunk + halo] = jnp.zeros((chunk, LANES), F32)

        @pl.when(t < nct)
        def _():
            lane = lax.broadcasted_iota(jnp.int32, (1, LANES), 1)
            left = lane < HEAD_DIM
            bias2 = b_ref[...]
            if d == 1:
                kwin = jnp.concatenate([kp[...], kc[...], kn[...]], axis=0).astype(BF16)
                vwin = jnp.concatenate([vp[...], vc[...], vn[...]], axis=0).astype(BF16)

            def group(r0):
                qss, doss, kcats, scores, dps, lcols, dcols = [], [], [], [], [], [], []
                for j in range(ITEMS):
                    qs = _stack_pair(_item_rows(q_ref, j, r0, blk, d) * scale, left)
                    dos = _stack_pair(_item_rows(do_ref, j, r0, blk, d), left)
                    if d == 1:
                        kcat, vcat = kwin[j * blk:(j + 3) * blk], vwin[j * blk:(j + 3) * blk]
                    else:
                        kcat = jnp.concatenate([_item_rows(r, j, r0, blk, d) for r in (kp, kc, kn)], axis=0).astype(BF16)
                        vcat = jnp.concatenate([_item_rows(r, j, r0, blk, d) for r in (vp, vc, vn)], axis=0).astype(BF16)
                    l2, d2 = _item_rows(l_ref, j, r0, blk, d), _item_rows(d_ref, j, r0, blk, d)
                    lcols.append(jnp.max(jnp.where(left, l2, NEG_INF), axis=-1, keepdims=True))
                    lcols.append(jnp.max(jnp.where(left, NEG_INF, l2), axis=-1, keepdims=True))
                    dcols.append(jnp.sum(jnp.where(lane == 0, d2, 0.0), axis=-1, keepdims=True))
                    dcols.append(jnp.sum(jnp.where(lane == HEAD_DIM, d2, 0.0), axis=-1, keepdims=True))
                    scores.append(_dot_nt(qs, kcat) + bias2 + _item_penalty(t, nct, j, blk, d))
                    dps.append(_dot_nt(dos, vcat))
                    qss.append(qs)
                    doss.append(dos)
                    kcats.append(kcat)
                lcol = jnp.concatenate(lcols, axis=0)
                dcol = jnp.concatenate(dcols, axis=0)
                p = jnp.exp(jnp.concatenate(scores, axis=0) - lcol)
                ds = p * (jnp.concatenate(dps, axis=0) - dcol)
                if has_sink:
                    sgrad = dcol * jnp.exp(jnp.concatenate([s_ref[...]] * ITEMS, axis=0) - lcol)
                for j in range(ITEMS):
                    top, bot = 2 * j * blk, (2 * j + 2) * blk
                    dsj, pj = ds[top:bot], p[top:bot]
                    db_ref[...] += dsj
                    if has_sink:
                        ds_ref[...] -= sgrad[top:bot]
                    dq2 = jnp.dot(dsj.astype(BF16), kcats[j], preferred_element_type=F32) * scale
                    dq_val = jnp.where(left, dq2[:blk], dq2[blk:])
                    if d == 1:
                        dq_ref[j * blk:(j + 1) * blk, :] = dq_val
                    else:
                        dq_ref[pl.ds(r0 + j, blk, stride=d), :] = dq_val
                    dk_new = jnp.dot(jnp.transpose(dsj).astype(BF16), qss[j], preferred_element_type=F32)
                    dv_new = jnp.dot(jnp.transpose(pj).astype(BF16), doss[j], preferred_element_type=F32)
                    for w, new in ((wk, dk_new), (wv, dv_new)):
                        if d == 1:
                            w[chunk + (j - 1) * blk:chunk + (j + 2) * blk, :] += new
                        else:
                            for c in range(3):
                                w[pl.ds(c * chunk + r0 + j, blk, stride=d), :] += new[c * blk:(c + 1) * blk]

            if groups == 1:
                group(0)
            else:
                def step(g, carry):
                    group(g * ITEMS)
                    return carry

                lax.fori_loop(0, groups, step, 0)

        dk_ref[...] = wk[0:chunk]
        dv_ref[...] = wv[0:chunk]

    cur, prev, nxt = _attn_specs(seq, blk, d, lambda t: jnp.minimum(t, nct - 1))
    lag = pl.BlockSpec((None, chunk, LANES), lambda hp, t: (hp, jnp.maximum(t - 1, 0), 0))
    band = pl.BlockSpec((2 * blk, 3 * blk), lambda hp, t: (hp, 0))
    col = pl.BlockSpec((2 * blk, 1), lambda hp, t: (hp, 0))
    in_specs = [cur, prev, cur, nxt, prev, cur, nxt, cur, cur, cur, band]
    args = [q, k, k, k, v, v, v, do, lse, delta, bias]
    out_shape = [jax.ShapeDtypeStruct(q.shape, F32)] * 3 + [jax.ShapeDtypeStruct((N_HEADS * blk, 3 * blk), F32)]
    out_specs = [cur, lag, lag, band]
    if has_sink:
        in_specs.append(col)
        args.append(sink)
        out_shape.append(jax.ShapeDtypeStruct((N_HEADS * blk, 1), F32))
        out_specs.append(col)
    window = pltpu.VMEM((2 * chunk + halo, LANES), F32)
    return pl.pallas_call(
        body, out_shape=out_shape, grid=(N_PAIRS, nct + 1), in_specs=in_specs, out_specs=out_specs,
        scratch_shapes=[window, window], name=name, compiler_params=_params("arbitrary", "arbitrary"))(*args)


def _combine_patterns(outs, lses):
    _, rows, _ = outs[0].shape
    tm = 256
    n = len(outs)

    def body(*refs):
        o_refs, l_refs = refs[:n], refs[n:2 * n]
        y_ref, lse_ref = refs[2 * n], refs[2 * n + 1]
        for hp in range(N_PAIRS):
            ls = [r[hp] for r in l_refs]
            m = functools.reduce(jnp.maximum, ls)
            es = [jnp.exp(l - m) for l in ls]
            den = functools.reduce(lambda a, b: a + b, es)
            num = functools.reduce(lambda a, b: a + b, [e * r[hp] for e, r in zip(es, o_refs)])
            y_ref[:, hp * LANES:(hp + 1) * LANES] = num / den
            lse_ref[hp] = m + jnp.log(den)

    pm = pl.BlockSpec((N_PAIRS, tm, LANES), lambda i: (0, i, 0))
    return pl.pallas_call(
        body, out_shape=[jax.ShapeDtypeStruct((rows, WIDTH), F32), jax.ShapeDtypeStruct((N_PAIRS, rows, LANES), F32)],
        grid=(rows // tm,), in_specs=[pm] * (2 * n), out_specs=[pl.BlockSpec((tm, WIDTH), lambda i: (i, 0)), pm],
        name="combine_a", compiler_params=_params("parallel"))(*outs, *lses)


def _pairs_to_tokens(a):
    _, rows, _ = a.shape
    tm = 256

    def body(a_ref, o_ref):
        o_ref[...] = _get_pairs(a_ref)

    return pl.pallas_call(
        body, out_shape=jax.ShapeDtypeStruct((rows, WIDTH), a.dtype), grid=(rows // tm,),
        in_specs=[pl.BlockSpec((N_PAIRS, tm, LANES), lambda i: (0, i, 0))],
        out_specs=pl.BlockSpec((tm, WIDTH), lambda i: (i, 0)), name="pairs_to_tokens",
        compiler_params=_params("parallel"))(a)


def _attn_bwd_prep(dy, y, name):
    rows = dy.shape[0]
    tm = 256

    def body(dy_ref, y_ref, do_ref, dl_ref):
        dyv = dy_ref[...]
        _put_pairs(do_ref, dyv)
        _put_pairs(dl_ref, _seg_sum(dyv * y_ref[...]))

    tok = pl.BlockSpec((tm, WIDTH), lambda i: (i, 0))
    pm = pl.BlockSpec((N_PAIRS, tm, LANES), lambda i: (0, i, 0))
    return pl.pallas_call(
        body, out_shape=[jax.ShapeDtypeStruct((N_PAIRS, rows, LANES), F32)] * 2, grid=(rows // tm,),
        in_specs=[tok, tok], out_specs=[pm, pm], name=name, compiler_params=_params("parallel"))(dy, y)


def _tile_gain(g, reps):
    return jnp.tile(g[None, :], (1, reps))


def _local_step(x, p, target, w_in_of, rest_of, small):
    rel_table = small["rel_table"]
    buckets_a = [_band_buckets(blk, d) for blk, d in DILATED]
    buckets_b = _band_buckets(BLK_B, 1)
    bias_a = [_bias_tiles(rel_table, bk, 0, "bias_a").reshape(N_HEADS * bk.shape[0], -1) for bk in buckets_a]
    bias_b = _bias_tiles(rel_table, buckets_b, N_HEADS, "bias_b").reshape(N_HEADS * BLK_B, -1)

    saved = []
    for l in range(DEPTH):
        g_mix, g_ffn, g_ple = (small[n][l][None, :] for n in ("norm_mix_g", "norm_ffn_g", "norm_ple_g"))
        gqa, gka, gqb = (_tile_gain(small[n][l], N_HEADS) for n in ("qnorm_a_g", "knorm_a_g", "qnorm_b_g"))
        gkb = _tile_gain(small["knorm_b_g"][l], N_KV_B)
        sink = jnp.repeat(small["sink_b"][l], BLK_B)[:, None]

        h = _rms_fwd(x, g_mix, "rms_mix")
        w_in = w_in_of(l, h)
        proj = _mm(h, w_in, "nt", F32, "mm_in")
        qa, ka, va, qb, kb, vb = _qknorm_fwd(proj, gqa, gka, gqb, gkb)
        outs, lses = [], []
        for (blk, d), bias in zip(DILATED, bias_a):
            o, ls = _attn_fwd(qa, ka, va, bias, None, blk, d, f"attn_a{d}_fwd")
            outs.append(o)
            lses.append(ls)
        ya, lse_a = _combine_patterns(outs, lses)
        yb, lse_b = _attn_fwd(qb, kb, vb, bias_b, sink, BLK_B, 1, "attn_b_fwd")
        yb = _pairs_to_tokens(yb)
        w = dict(rest_of(l, yb), w_in=w_in)
        ca = _mm(ya, w["w_branch_a"], "nn", F32, "mm_branch_a")
        cb = _mm(yb, w["w_branch_b"], "nn", F32, "mm_branch_b")

        def gate(ca_, cb_, ga_, gb_):
            return (_sigmoid(ga_) * ca_ + _sigmoid(gb_) * cb_,)

        (merged,) = _ew(gate, [(ca, 0), (cb, 0), (proj, OFF_GA), (proj, OFF_GB)], [BF16],
                        width=D_MODEL, bw=256, name="gate")
        x1 = _mm(merged, w["w_out"], "nn", F32, "mm_out", res=x)

        h2 = _rms_fwd(x1, g_ffn, "rms_ffn")
        a = _mm(h2, w["w_ffn_gate"], "nt", F32, "mm_ffn_gate")
        u = _mm(h2, w["w_ffn_up"], "nt", F32, "mm_ffn_up")

        def swiglu(a_, u_):
            return ((a_ * _sigmoid(a_)) * u_,)

        (hid,) = _ew(swiglu, [(a, 0), (u, 0)], [BF16], width=D_FF, bw=D_FF, name="swiglu")
        x2 = _mm(hid, w["w_ffn_down"], "nn", F32, "mm_ffn_down", res=x1)

        h3 = _rms_fwd(x2, g_ple, "rms_ple")
        z = _mm(h3, w["w_ple_gate"], "nn", F32, "mm_ple_gate")
        e = _mm(p[l], w["w_ple_proj"], "nn", F32, "mm_ple_proj")

        def ple(x2_, z_, e_):
            return (x2_ + _sigmoid(z_) * e_,)

        (x3,) = _ew(ple, [(x2, 0), (z, 0), (e, 0)], [F32], width=D_MODEL, bw=D_MODEL, name="ple")
        saved.append(dict(w=w, x0=x, h=h, proj=proj, qa=qa, ka=ka, va=va, qb=qb, kb=kb, vb=vb, ya=ya, lse_a=lse_a,
                          yb=yb, lse_b=lse_b, ca=ca, cb=cb, merged=merged, x1=x1, h2=h2, a=a, u=u, hid=hid,
                          x2=x2, h3=h3, z=z, e=e))
        x = x3

    dx, loss_acc = _loss_grad(x, target)
    loss = loss_acc[0, 0]

    gbig = [{} for _ in range(DEPTH)]
    marks = [{} for _ in range(DEPTH)]
    gsmall = {n: [None] * DEPTH for n in SMALL if n != "rel_table"}
    dtable_a = jnp.zeros((N_HEADS, NUM_BUCKETS), F32)
    dtable_b = jnp.zeros((N_HEADS, NUM_BUCKETS), F32)

    for l in reversed(range(DEPTH)):
        sv = saved[l]
        w = sv["w"]
        g_mix, g_ffn, g_ple = (small[n][l][None, :] for n in ("norm_mix_g", "norm_ffn_g", "norm_ple_g"))
        gqa, gka, gqb = (_tile_gain(small[n][l], N_HEADS) for n in ("qnorm_a_g", "knorm_a_g", "qnorm_b_g"))
        gkb = _tile_gain(small["knorm_b_g"][l], N_KV_B)
        sink = jnp.repeat(small["sink_b"][l], BLK_B)[:, None]

        def ple_bwd(dx_, z_, e_):
            s = _sigmoid(z_)
            return dx_ * s, dx_ * e_ * (s * (1.0 - s))

        de, dz = _ew(ple_bwd, [(dx, 0), (sv["z"], 0), (sv["e"], 0)], [BF16, BF16], width=D_MODEL, bw=D_MODEL,
                     name="ple_bwd")
        gbig[l]["w_ple_proj"] = _mm(p[l], de, "tn", F32, "mm_d_ple_proj")
        gbig[l]["w_ple_gate"] = _mm(sv["h3"], dz, "tn", F32, "mm_d_ple_gate")
        dh3 = _mm(dz, w["w_ple_gate"], "nt", F32, "mm_dh3")
        dx, dxb, gsmall["norm_ple_g"][l] = _rms_bwd(sv["x2"], g_ple, dh3, dx, "rms_ple_bwd")

        dhid = _mm(dxb, w["w_ffn_down"], "nt", F32, "mm_dhid")
        gbig[l]["w_ffn_down"] = _mm(sv["hid"], dxb, "tn", F32, "mm_d_ffn_down")

        def swiglu_bwd(a_, u_, dh_):
            s = _sigmoid(a_)
            return dh_ * u_ * (s * (1.0 + a_ * (1.0 - s))), dh_ * (a_ * s)

        da, du = _ew(swiglu_bwd, [(sv["a"], 0), (sv["u"], 0), (dhid, 0)], [BF16, BF16], width=D_FF, bw=D_FF,
                     name="swiglu_bwd")
        gbig[l]["w_ffn_gate"] = _mm(da, sv["h2"], "tn", F32, "mm_d_ffn_gate")
        gbig[l]["w_ffn_up"] = _mm(du, sv["h2"], "tn", F32, "mm_d_ffn_up")
        dh2 = _mm(da, w["w_ffn_gate"], "nn", F32, "mm_dh2_gate")
        dh2 = _mm(du, w["w_ffn_up"], "nn", F32, "mm_dh2_up", res=dh2)
        dx, dxb, gsmall["norm_ffn_g"][l] = _rms_bwd(sv["x1"], g_ffn, dh2, dx, "rms_ffn_bwd")

        dmerged = _mm(dxb, w["w_out"], "nt", F32, "mm_dmerged")
        marks[l]["ffn_bwd_done"] = dmerged
        gbig[l]["w_out"] = _mm(sv["merged"], dxb, "tn", F32, "mm_d_out")

        def gate_bwd(dm_, ca_, cb_, ga_, gb_):
            sa, sb = _sigmoid(ga_), _sigmoid(gb_)
            return dm_ * sa, dm_ * sb, dm_ * ca_ * (sa * (1.0 - sa)), dm_ * cb_ * (sb * (1.0 - sb))

        dca, dcb, dga, dgb = _ew(gate_bwd, [(dmerged, 0), (sv["ca"], 0), (sv["cb"], 0), (sv["proj"], OFF_GA),
                                            (sv["proj"], OFF_GB)], [BF16, BF16, BF16, BF16],
                                 width=D_MODEL, bw=256, name="gate_bwd")
        dgab = jnp.concatenate([dga, dgb], axis=1)
        gbig[l]["w_branch_a"] = _mm(sv["ya"], dca, "tn", F32, "mm_d_branch_a")
        gbig[l]["w_branch_b"] = _mm(sv["yb"], dcb, "tn", F32, "mm_d_branch_b")
        dya = _mm(dca, w["w_branch_a"], "nt", F32, "mm_dya")
        dyb = _mm(dcb, w["w_branch_b"], "nt", F32, "mm_dyb")

        dya, delta_a = _attn_bwd_prep(dya, sv["ya"], "attn_a_bwd_prep")
        dyb, delta_b = _attn_bwd_prep(dyb, sv["yb"], "attn_b_bwd_prep")

        dqa, dka, dva = [], [], []
        for (blk, d), bias, bk in zip(DILATED, bias_a, buckets_a):
            dq_, dk_, dv_, db_ = _attn_bwd(sv["qa"], sv["ka"], sv["va"], dya, sv["lse_a"], delta_a, bias, None, blk, d,
                                           f"attn_a{d}_bwd")
            dqa.append(dq_)
            dka.append(dk_)
            dva.append(dv_)
            dtable_a = dtable_a + _table_grad(db_.reshape(N_HEADS, blk, 3 * blk), bk, "table_grad_a")
        dqb, dkb, dvb, db_, dsink = _attn_bwd(sv["qb"], sv["kb"], sv["vb"], dyb, sv["lse_b"], delta_b, bias_b, sink,
                                              BLK_B, 1, "attn_b_bwd")
        dtable_b = dtable_b + _table_grad(db_.reshape(N_HEADS, BLK_B, 3 * BLK_B), buckets_b, "table_grad_b")
        gsmall["sink_b"][l] = dsink.reshape(N_HEADS, BLK_B).sum(axis=1)

        dproj, pqa, pka, pqb, pkb = _qknorm_bwd(sv["proj"], gqa, gka, gqb, gkb, dqa, dka, dva, dqb, dkb, dvb, dgab)
        marks[l]["attn_bwd_done"] = dproj
        gsmall["qnorm_a_g"][l] = pqa.reshape(N_HEADS, HEAD_DIM).sum(0)
        gsmall["knorm_a_g"][l] = pka.reshape(N_HEADS, HEAD_DIM).sum(0)
        gsmall["qnorm_b_g"][l] = pqb.reshape(N_HEADS, HEAD_DIM).sum(0)
        gsmall["knorm_b_g"][l] = pkb.reshape(N_KV_B, HEAD_DIM).sum(0)
        gbig[l]["w_in"] = _mm(dproj, sv["h"], "tn", F32, "mm_d_in")
        dh = _mm(dproj, w["w_in"], "nn", F32, "mm_dh")
        dx, _, gsmall["norm_mix_g"][l] = _rms_bwd(sv["x0"], g_mix, dh, dx, "rms_mix_bwd")
        gsmall["norm_mix_g"][l] = gsmall["norm_mix_g"][l][0]
        gsmall["norm_ffn_g"][l] = gsmall["norm_ffn_g"][l][0]
        gsmall["norm_ple_g"][l] = gsmall["norm_ple_g"][l][0]

    gsmall = {n: jnp.stack(v) for n, v in gsmall.items()}
    gsmall["rel_table"] = jnp.concatenate([dtable_a, dtable_b], axis=0).T
    return loss, dx, gbig, gsmall, marks


def _place():
    return lax.axis_index("x"), lax.axis_index("y"), lax.axis_index("c")


def _flip(v, bit):
    return 1 - v if bit else v


CHIP_RELATIONS = ((0, 1), (1, 0), (1, 1))
ANY = pl.BlockSpec(memory_space=pl.ANY)


def _allgather_body(w_refs, out_refs, send_sems, recv_sems):
    x, y, c = _place()
    chips = [(_flip(x, a), _flip(y, b)) for a, b in CHIP_RELATIONS]

    def make(g):
        w_ref, out_ref = w_refs[g], out_refs[g]
        half = w_ref.shape[0] // 2

        def part(px, py, pc):
            return out_ref.at[2 * px + py, pl.ds(pc * half, half), :]

        def copy(k, block, to, src=None):
            return pltpu.make_async_remote_copy(
                src_ref=part(*block) if src is None else src, dst_ref=part(*block),
                send_sem=send_sems.at[7 * g + k], recv_sem=recv_sems.at[7 * g + k], device_id=to,
                device_id_type=MESH_ID)

        own = pltpu.make_async_remote_copy(
            src_ref=w_ref, dst_ref=out_ref.at[2 * x + y], send_sem=send_sems.at[7 * g + 6],
            recv_sem=recv_sems.at[7 * g + 6], device_id=(x, y, 1 - c), device_id_type=MESH_ID)
        first = [copy(k, (x, y, c), (*chip, c), src=w_ref.at[pl.ds(c * half, half), :]) for k, chip in enumerate(chips)]
        passed = [copy(3 + k, (*chip, c), (x, y, 1 - c)) for k, chip in enumerate(chips)]
        arrive = [copy(k, (*chip, c), (x, y, c)) for k, chip in enumerate(chips)]
        arrive2 = [copy(3 + k, (*chip, 1 - c), (x, y, c)) for k, chip in enumerate(chips)]
        return own, first, passed, arrive, arrive2

    made = [make(g) for g in range(len(w_refs))]
    for own, first, _, _, _ in made:
        own.start()
        for cp in first:
            cp.start()
    for _, _, passed, arrive, _ in made:
        for k in range(3):
            arrive[k].wait_recv()
            passed[k].start()
    for own, first, passed, _, arrive2 in made:
        for k in range(3):
            arrive2[k].wait_recv()
        own.wait_recv()
        for cp in first + passed + [own]:
            cp.wait_send()


def _sibling(x, y, c):
    return [(x, y, 1 - c)]


def _same_core_of_other_chips(x, y, c):
    return [(_flip(x, a), _flip(y, b), c) for a, b in CHIP_RELATIONS]


def _exchange(body, ins, out_types, n_sems, name, sequencer=None):
    n = len(ins)
    sems = (pltpu.SemaphoreType.DMA((n_sems,)), pltpu.SemaphoreType.DMA((n_sems,)))
    if sequencer is None:
        in_place = out_types is None
        out_shape = [jax.ShapeDtypeStruct(a.shape, a.dtype) for a in ins] if in_place else out_types

        def tc_body(*refs):
            body(refs[:n], refs[n:n + len(out_shape)], refs[-2], refs[-1])

        return list(pl.pallas_call(
            tc_body, out_shape=out_shape, in_specs=[ANY] * n, out_specs=[ANY] * len(out_shape),
            input_output_aliases={g: g for g in range(n)} if in_place else {}, scratch_shapes=list(sems), name=name)(*ins))

    collective_id, peers = sequencer
    hbm = pltpu.MemorySpace.HBM
    in_refs = [jax.new_ref(a, memory_space=hbm) for a in ins]
    out_refs = in_refs if out_types is None else [jax.empty_ref(t, memory_space=hbm) for t in out_types]

    @pl.kernel(mesh=plsc.ScalarSubcoreMesh(axis_name="sequencer", num_cores=1), name=name, scratch_types=sems,
               compiler_params=pltpu.CompilerParams(collective_id=collective_id))
    def launch(send_sems, recv_sems):
        barrier = pltpu.get_barrier_semaphore()
        devices = peers(*_place())
        for device in devices:
            pl.semaphore_signal(barrier, inc=1, device_id=device, device_id_type=MESH_ID)
        pl.semaphore_wait(barrier, len(devices))
        body(in_refs, out_refs, send_sems, recv_sems)

    launch()
    return [r[...] for r in out_refs]


def _allgather(shards, name, sequencer=None):
    out_types = [jax.ShapeDtypeStruct((N_CHIPS,) + s.shape, s.dtype) for s in shards]
    if sequencer is not None:
        sequencer = (sequencer, lambda x, y, c: _sibling(x, y, c) + _same_core_of_other_chips(x, y, c))
    return _exchange(_allgather_body, shards, out_types, 7 * len(shards), name, sequencer)


def _half_tile(half):
    return max(t for t in range(16, 1025, 16) if half % t == 0)


def _run_copies(cps):
    for cp in cps:
        cp.start()
    for cp in cps:
        cp.wait_recv()
    for cp in cps:
        cp.wait_send()


def _sibling_halves(gsends, name, sequencer=None):
    def body(g_refs, out_refs, send_sems, recv_sems):
        x, y, c = _place()
        cps = []
        for g, (g_ref, out_ref) in enumerate(zip(g_refs, out_refs)):
            half = g_ref.shape[1] // 2
            cps.append(pltpu.make_async_remote_copy(
                src_ref=g_ref.at[:, pl.ds((1 - c) * half, half), :], dst_ref=out_ref,
                send_sem=send_sems.at[g], recv_sem=recv_sems.at[g], device_id=(x, y, 1 - c), device_id_type=MESH_ID))
        _run_copies(cps)

    out_types = [jax.ShapeDtypeStruct((s.shape[0], s.shape[1] // 2, s.shape[2]), s.dtype) for s in gsends]
    return _exchange(body, gsends, out_types, len(gsends), name, sequencer and (sequencer, _sibling))


def _chip_sums(gsend, sib, place):
    n, rows, cols = gsend.shape
    half = rows // 2
    tm = _half_tile(half)
    nblk = half // tm

    def body(s_ref, g_ref, sib_ref, o_ref):
        o_ref[0] = (g_ref[0].astype(F32) + sib_ref[0].astype(F32)).astype(o_ref.dtype)

    grid_spec = pltpu.PrefetchScalarGridSpec(
        num_scalar_prefetch=1, grid=(n, nblk),
        in_specs=[pl.BlockSpec((1, tm, cols), lambda k, i, s: (jnp.bitwise_xor(s[0], k), s[1] * nblk + i, 0)),
                  pl.BlockSpec((1, tm, cols), lambda k, i, s: (jnp.bitwise_xor(s[0], k), i, 0))],
        out_specs=pl.BlockSpec((1, tm, cols), lambda k, i, s: (k, i, 0)))
    return pl.pallas_call(
        body, out_shape=jax.ShapeDtypeStruct((n, half, cols), BF16), grid_spec=grid_spec,
        name="rs_chip_sums", compiler_params=_params("parallel", "parallel"))(place, gsend, sib)


def _exchange_chip_sums(tsends, name, sequencer=None):
    def body(t_refs, out_refs, send_sems, recv_sems):
        x, y, c = _place()
        cps = []
        for g, (t_ref, out_ref) in enumerate(zip(t_refs, out_refs)):
            for k, device in enumerate(_same_core_of_other_chips(x, y, c)):
                cps.append(pltpu.make_async_remote_copy(
                    src_ref=t_ref.at[k + 1], dst_ref=out_ref.at[k], send_sem=send_sems.at[3 * g + k],
                    recv_sem=recv_sems.at[3 * g + k], device_id=device, device_id_type=MESH_ID))
        _run_copies(cps)

    out_types = [jax.ShapeDtypeStruct((3,) + s.shape[1:], s.dtype) for s in tsends]
    return _exchange(body, tsends, out_types, 3 * len(tsends), name,
                     sequencer and (sequencer, _same_core_of_other_chips))


def _final_sum(tsend, recv, place):
    n, half, cols = tsend.shape
    tm = _half_tile(half)
    nblk = half // tm

    def body(s_ref, t_ref, r_ref, o_ref):
        o_ref[...] = ((t_ref[0].astype(F32) + r_ref[0].astype(F32)) + r_ref[1].astype(F32)) + r_ref[2].astype(F32)

    grid_spec = pltpu.PrefetchScalarGridSpec(
        num_scalar_prefetch=1, grid=(nblk,),
        in_specs=[pl.BlockSpec((1, tm, cols), lambda i, s: (0, i, 0)), pl.BlockSpec((n - 1, tm, cols), lambda i, s: (0, i, 0))],
        out_specs=pl.BlockSpec((tm, cols), lambda i, s: (s[1] * nblk + i, 0)))
    return pl.pallas_call(
        body, out_shape=jax.ShapeDtypeStruct((2 * half, cols), F32), grid_spec=grid_spec, name="rs_final_sum",
        compiler_params=_params("parallel"))(place, tsend, recv)


def _join_halves(gfulls, name, sequencer=None):
    def body(g_refs, out_refs, send_sems, recv_sems):
        x, y, c = _place()
        n = len(g_refs)

        def copy(g, pc):
            half = g_refs[g].shape[0] // 2
            return pltpu.make_async_remote_copy(
                src_ref=g_refs[g].at[pl.ds(pc * half, half), :], dst_ref=out_refs[g].at[pl.ds(pc * half, half), :],
                send_sem=send_sems.at[g], recv_sem=recv_sems.at[g], device_id=(x, y, 1 - c), device_id_type=MESH_ID)

        mine = [copy(g, c) for g in range(n)]
        for cp in mine:
            cp.start()
        for g in range(n):
            copy(g, 1 - c).wait_recv()
        for cp in mine:
            cp.wait_send()

    return _exchange(body, gfulls, None, len(gfulls), name, sequencer and (sequencer, _sibling))


def _allreduce_small(v):
    rows, cols = v.shape

    def body(v_ref, out_ref, buf, send_sems, recv_sems):
        x, y, c = _place()
        cps = []
        for k in range(1, 8):
            peer = (_flip(x, (k >> 2) & 1), _flip(y, (k >> 1) & 1), _flip(c, k & 1))
            cps.append(pltpu.make_async_remote_copy(
                src_ref=v_ref, dst_ref=buf.at[k - 1], send_sem=send_sems.at[k - 1], recv_sem=recv_sems.at[k - 1],
                device_id=peer, device_id_type=MESH_ID))
        for cp in cps:
            cp.start()
        for cp in cps:
            cp.wait_recv()
        for cp in cps:
            cp.wait_send()
        t0 = v_ref[...] + buf[0]
        t1 = buf[1] + buf[2]
        t2 = buf[3] + buf[4]
        t3 = buf[5] + buf[6]
        out_ref[...] = (t0 + t1) + (t2 + t3)

    vm = pl.BlockSpec(memory_space=pltpu.VMEM)
    return pl.pallas_call(
        body, out_shape=jax.ShapeDtypeStruct((rows, cols), F32), in_specs=[vm], out_specs=vm,
        scratch_shapes=[pltpu.VMEM((7, rows, cols), F32), pltpu.SemaphoreType.DMA((7,)), pltpu.SemaphoreType.DMA((7,))],
        name="allreduce_small")(v)


BIG_INFO = {n: (shape, ax) for n, shape, ax in BIG}
GROUPS = (("w_in",), ("w_ffn_gate", "w_ffn_up", "w_ffn_down", "w_out", "w_ple_gate"),
          ("w_branch_a", "w_branch_b", "w_ple_proj"))


def _shard_shape(name):
    (k, m), ax = BIG_INFO[name]
    return (k // N_CHIPS, m) if ax == 0 else (k, m // N_CHIPS)


def _group_rows(group):
    offs, off = {}, 0
    for n in group:
        offs[n] = off
        off += _shard_shape(n)[0]
    return offs, off


def _pack_groups(shards, layer, dtype):
    return [jnp.concatenate([shards[n][layer].astype(dtype) for n in group], axis=0) for group in GROUPS]


def _unpack_full(gathered, groups):
    out = {}
    for group, arr in zip(groups, gathered):
        offs, _ = _group_rows(group)
        for n in group:
            rows, cols = _shard_shape(n)
            (k, m), ax = BIG_INFO[n]
            slab = arr[:, offs[n]:offs[n] + rows]
            out[n] = slab.reshape(k, m) if ax == 0 else jnp.transpose(slab, (1, 0, 2)).reshape(k, m)
    return out


def _pack_grads(gfull):
    out = []
    for group in GROUPS:
        parts = []
        for n in group:
            rows, cols = _shard_shape(n)
            ax = BIG_INFO[n][1]
            slab = (gfull[n].reshape(N_CHIPS, rows, cols) if ax == 0
                    else jnp.transpose(gfull[n].reshape(rows, N_CHIPS, cols), (1, 0, 2)))
            parts.append(slab.astype(BF16))
        out.append(jnp.concatenate(parts, axis=1))
    return out


def _after(values, mark):
    values, _ = lax.optimization_barrier((values, mark))
    return values


def _reduce_scatter_begin(gsends, place, tag, ids):
    sibs = _sibling_halves(gsends, "rs_sibling_halves_" + tag, ids[0])
    tsends = [_chip_sums(g, s, place) for g, s in zip(gsends, sibs)]
    return tsends, _exchange_chip_sums(tsends, "rs_exchange_" + tag, ids[1])


def _reduce_scatter_finish(begun, place, tag, ids, hold):
    tsends, recvs = begun
    recvs = _after(recvs, hold)
    return _join_halves([_final_sum(t, r, place) for t, r in zip(tsends, recvs)], "rs_join_halves_" + tag, ids[2])


SMALL_SHAPES = {"rel_table": (NUM_BUCKETS, 2 * N_HEADS), "norm_mix_g": (DEPTH, D_MODEL), "qnorm_a_g": (DEPTH, HEAD_DIM),
                "knorm_a_g": (DEPTH, HEAD_DIM), "qnorm_b_g": (DEPTH, HEAD_DIM), "knorm_b_g": (DEPTH, HEAD_DIM),
                "sink_b": (DEPTH, N_HEADS), "norm_ffn_g": (DEPTH, D_MODEL), "norm_ple_g": (DEPTH, D_MODEL)}


def _pack_small(vals):
    flat = jnp.concatenate([vals[n].astype(F32).reshape(-1) for n in SMALL])
    flat = jnp.concatenate([flat, jnp.zeros((SMALL_ROWS * LANES - flat.shape[0],), F32)])
    return flat.reshape(SMALL_ROWS, LANES)


def _unpack_small(packed):
    flat, out, off = packed.reshape(-1), {}, 0
    for n in SMALL:
        size = math.prod(SMALL_SHAPES[n])
        out[n] = flat[off:off + size].reshape(SMALL_SHAPES[n])
        off += size
    return out


def _adamw(w, gs, g_row, m, v, name):
    c1 = 1.0 - ADAM_B1 ** ADAM_STEP
    c2 = 1.0 - ADAM_B2 ** ADAM_STEP
    total, width = w.shape
    n_layers = len(gs)
    per = total // n_layers
    tm = max(t for t in range(8, 513, 8) if per % t == 0 and g_row % t == 0)
    nblk = per // tm

    def body(*refs):
        w_ref, g_refs = refs[0], refs[1:1 + n_layers]
        m_ref, v_ref, og, od, om, ov = refs[1 + n_layers:]
        layer = pl.program_id(0) // nblk
        g = g_refs[0][...]
        for l in range(1, n_layers):
            g = jnp.where(layer == l, g_refs[l][...], g)
        m_new = ADAM_B1 * m_ref[...] + (1.0 - ADAM_B1) * g
        v_new = ADAM_B2 * v_ref[...] + (1.0 - ADAM_B2) * (g * g)
        og[...] = g
        od[...] = -ADAM_LR * ((m_new / c1) / (jnp.sqrt(v_new / c2) + ADAM_EPS) + ADAM_WD * w_ref[...])
        om[...] = m_new
        ov[...] = v_new

    row = pl.BlockSpec((tm, width), lambda i: (i, 0))
    g_specs = [pl.BlockSpec((tm, width), lambda i, l=l: (g_row // tm + jnp.clip(i - l * nblk, 0, nblk - 1), 0))
               for l in range(n_layers)]
    return pl.pallas_call(
        body, out_shape=[jax.ShapeDtypeStruct((total, width), F32)] * 4, grid=(total // tm,),
        in_specs=[row] + g_specs + [row, row], out_specs=[row] * 4, name=name,
        compiler_params=_params("parallel"))(w, *gs, m, v)


def kernel(x, p, rel_table, norm_mix_g, w_in, qnorm_a_g, knorm_a_g, qnorm_b_g, knorm_b_g, sink_b, w_branch_a, w_branch_b, w_out, norm_ffn_g, w_ffn_gate, w_ffn_up, w_ffn_down, norm_ple_g, w_ple_gate, w_ple_proj, loss_target, m_rel_table, m_norm_mix_g, m_w_in, m_qnorm_a_g, m_knorm_a_g, m_qnorm_b_g, m_knorm_b_g, m_sink_b, m_w_branch_a, m_w_branch_b, m_w_out, m_norm_ffn_g, m_w_ffn_gate, m_w_ffn_up, m_w_ffn_down, m_norm_ple_g, m_w_ple_gate, m_w_ple_proj, v_rel_table, v_norm_mix_g, v_w_in, v_qnorm_a_g, v_knorm_a_g, v_qnorm_b_g, v_knorm_b_g, v_sink_b, v_w_branch_a, v_w_branch_b, v_w_out, v_norm_ffn_g, v_w_ffn_gate, v_w_ffn_up, v_w_ffn_down, v_norm_ple_g, v_w_ple_gate, v_w_ple_proj):
    given = dict(locals())

    def held(name, a):
        return jnp.swapaxes(a, 1, 2) if name in TRANSPOSED else a

    weights = {n: held(n, given[n]) for n in WEIGHTS}
    moments_m = {n: held(n, given["m_" + n]) for n in WEIGHTS}
    moments_v = {n: held(n, given["v_" + n]) for n in WEIGHTS}
    xi, yi, ci = _place()
    place = jnp.stack([2 * xi + yi, ci]).astype(jnp.int32)

    shards = [_pack_groups(weights, l, BF16) for l in range(DEPTH)]
    w_in0 = _allgather(shards[0][:1], "allgather_w_in_layer0", sequencer=9)
    rest0 = _allgather(_after(shards[0][1:], w_in0), "allgather_rest_layer0", sequencer=1)
    gathered = [w_in0 + rest0, None]
    small = {n: weights[n] for n in SMALL}

    def w_in_of(l, mark):
        return _unpack_full(gathered[l][:1] if l == 0 else _after(gathered[l][:1], mark), GROUPS[:1])["w_in"]

    def rest_of(l, mark):
        if l == 0:
            gathered[1] = _allgather(_after(shards[1], mark), "allgather_layer1", sequencer=2)
        return _unpack_full(_after(gathered[l][1:], mark), GROUPS[1:])

    loss, dx, gbig, gsmall, marks = _local_step(x[0], p[:, 0], loss_target[0], w_in_of, rest_of, small)

    gsends = [_pack_grads(gbig[l]) for l in range(DEPTH)]
    stages = {"layer1": (gsends[1], (3, 4, 5)), "rest_layer0": (gsends[0][1:], (6, 7, 8)),
              "w_in_layer0": (gsends[0][:1], (10, 11, 12))}
    begun = {tag: _reduce_scatter_begin(g, place, tag, ids) for tag, (g, ids) in stages.items()}

    def finish(tag, hold):
        return _reduce_scatter_finish(begun[tag], place, tag, stages[tag][1], hold)

    red1 = finish("layer1", marks[0]["attn_bwd_done"])
    rest0 = finish("rest_layer0", marks[0]["attn_bwd_done"])

    grads, delta, new_m, new_v = {}, {}, {}, {}

    def update(group, reduced):
        offs, _ = _group_rows(group)
        for n in group:
            shape = weights[n].shape
            two_d = lambda a: a.reshape(shape[0] * shape[1], shape[2])
            outs = _adamw(two_d(weights[n]), reduced, offs[n], two_d(moments_m[n]), two_d(moments_v[n]), "adamw_" + n)
            grads[n], delta[n], new_m[n], new_v[n] = (held(n, o.reshape(shape)) for o in outs)

    for gi in (1, 2):
        update(GROUPS[gi], _after([rest0[gi - 1], red1[gi]], begun["w_in_layer0"][0]))
    update(GROUPS[0], [finish("w_in_layer0", delta[GROUPS[2][-1]])[0], red1[0]])
    small_grads = _allreduce_small(_pack_small(gsmall))
    g_, d_, m_, v_ = _adamw(_pack_small(weights), [small_grads], 0, _pack_small(moments_m), _pack_small(moments_v),
                            "adamw_small")
    grads.update(_unpack_small(g_))
    delta.update(_unpack_small(d_))
    new_m.update(_unpack_small(m_))
    new_v.update(_unpack_small(v_))

    loss = lax.psum(loss, ("x", "y", "c"))
    return (loss, dx[None], *[grads[n] for n in WEIGHTS], *[delta[n] for n in WEIGHTS],
            *[new_m[n] for n in WEIGHTS], *[new_v[n] for n in WEIGHTS])
```

```python
import functools
import math

import jax
import jax.numpy as jnp
from jax import lax
from jax.experimental import pallas as pl
from jax.experimental.pallas import tpu as pltpu
from jax.experimental.pallas import tpu_sc as plsc

F32 = jnp.float32
BF16 = jnp.bfloat16
MESH_ID = pl.DeviceIdType.MESH

SEQ = 2048
D_MODEL = 1024
DEPTH = 2
HEAD_DIM = 64
N_HEADS = 8
WIDTH = N_HEADS * HEAD_DIM
N_PAIRS = 4
ITEMS = 4
N_KV_B = 2
PLE_DIM = 256
D_FF = 2816
D_IN = 4352
OFF_QA, OFF_KA, OFF_VA, OFF_QB, OFF_KB, OFF_VB, OFF_GA, OFF_GB = 0, 512, 1024, 1536, 2048, 2176, 2304, 3328
DILATED = ((64, 1), (64, 4), (64, 16))
BLK_B = 128
NUM_BUCKETS = 32
MAX_DISTANCE = 1024
RMS_EPS = 1e-6
NEG_INF = -1e30
LANES = 128
VMEM_LIMIT = 48 * 1024 * 1024

ADAM_LR, ADAM_B1, ADAM_B2, ADAM_EPS, ADAM_WD, ADAM_STEP = 0.001, 0.9, 0.999, 1e-08, 0.01, 10

TRANSPOSED = ("w_in", "w_ffn_gate", "w_ffn_up")
BIG = (
    ("w_in", (D_IN, D_MODEL), 0),
    ("w_branch_a", (WIDTH, D_MODEL), 1),
    ("w_branch_b", (WIDTH, D_MODEL), 1),
    ("w_out", (D_MODEL, D_MODEL), 0),
    ("w_ffn_gate", (D_FF, D_MODEL), 0),
    ("w_ffn_up", (D_FF, D_MODEL), 0),
    ("w_ffn_down", (D_FF, D_MODEL), 0),
    ("w_ple_gate", (D_MODEL, D_MODEL), 0),
    ("w_ple_proj", (PLE_DIM, D_MODEL), 1),
)
SMALL = ("rel_table", "norm_mix_g", "qnorm_a_g", "knorm_a_g", "qnorm_b_g", "knorm_b_g", "sink_b",
         "norm_ffn_g", "norm_ple_g")
WEIGHTS = ("rel_table", "norm_mix_g", "w_in", "qnorm_a_g", "knorm_a_g", "qnorm_b_g", "knorm_b_g", "sink_b",
           "w_branch_a", "w_branch_b", "w_out", "norm_ffn_g", "w_ffn_gate", "w_ffn_up", "w_ffn_down",
           "norm_ple_g", "w_ple_gate", "w_ple_proj")
N_CHIPS = 4
SMALL_ROWS = 64


def _params(*sem):
    return pltpu.CompilerParams(dimension_semantics=sem, vmem_limit_bytes=VMEM_LIMIT)


def _pick(dim, target):
    for t in (target, 512, 256, 128, 64, 32, 16, 8):
        if t <= target and dim % t == 0:
            return t
    return dim


MM_VMEM_BUDGET = 40 * 1024 * 1024
STEP_OVERHEAD_S = 0.4e-6
TILE_DMA_BYTES_PER_S = 1.5e12
MXU_FLOPS_PER_S = 7e14


def _mm_tiles(m, n, k, a_bytes, b_bytes, out_bytes, has_res):
    best = None
    for tm in (t for t in range(LANES, m + 1, LANES) if m % t == 0):
        for tn in (t for t in range(LANES, n + 1, LANES) if n % t == 0):
            io = tm * k * a_bytes + tn * k * b_bytes + tm * tn * (out_bytes + (4 if has_res else 0))
            casts = (tm * k * 2 if a_bytes == 4 else 0) + (tn * k * 2 if b_bytes == 4 else 0)
            if 2 * io + tm * tn * 4 + casts > MM_VMEM_BUDGET:
                continue
            steps = (m // tm) * (n // tn)
            cost = steps * STEP_OVERHEAD_S + io / TILE_DMA_BYTES_PER_S + 2.0 * m * n * k / MXU_FLOPS_PER_S
            if best is None or (cost, -tm) < best[0]:
                best = ((cost, -tm), tm, tn)
    return best[1], best[2]


def _mm(a, b, mode, out_dtype, name, res=None):
    if mode == "nn":
        (m, k), (_, n) = a.shape, b.shape
    elif mode == "nt":
        (m, k), (n, _) = a.shape, b.shape
    else:
        (k, m), (_, n) = a.shape, b.shape
    tm, tn = _mm_tiles(m, n, k, a.dtype.itemsize, b.dtype.itemsize, jnp.dtype(out_dtype).itemsize, res is not None)
    a_spec = pl.BlockSpec((k, tm), lambda i, j: (0, i)) if mode == "tn" else pl.BlockSpec((tm, k), lambda i, j: (i, 0))
    b_spec = pl.BlockSpec((tn, k), lambda i, j: (j, 0)) if mode == "nt" else pl.BlockSpec((k, tn), lambda i, j: (0, j))
    dims = {"nn": (((1,), (0,)), ((), ())), "nt": (((1,), (1,)), ((), ())), "tn": (((0,), (0,)), ((), ()))}[mode]
    has_res = res is not None

    def body(*refs):
        a_ref, b_ref = refs[0], refs[1]
        o_ref = refs[-1]
        acc = lax.dot_general(a_ref[...].astype(BF16), b_ref[...].astype(BF16), dims, preferred_element_type=F32)
        if has_res:
            acc = acc + refs[2][...]
        o_ref[...] = acc.astype(out_dtype)

    in_specs = [a_spec, b_spec]
    args = [a, b]
    if has_res:
        in_specs.append(pl.BlockSpec((tm, tn), lambda i, j: (i, j)))
        args.append(res)
    return pl.pallas_call(
        body, out_shape=jax.ShapeDtypeStruct((m, n), out_dtype), grid=(m // tm, n // tn), in_specs=in_specs,
        out_specs=pl.BlockSpec((tm, tn), lambda i, j: (i, j)), name=name,
        compiler_params=_params("parallel", "parallel"))(*args)


def _ew(fn, ins, out_dtypes, *, width, bw, name, vecs=(), tm=256):
    rows = ins[0][0].shape[0]
    tm = _pick(rows, tm)
    n_in = len(ins) + len(vecs)

    def col_map(off_blocks):
        return lambda i, j: (i, off_blocks + j)

    in_specs = [pl.BlockSpec((tm, bw), col_map(off // bw)) for _, off in ins]
    in_specs += [pl.BlockSpec((1, bw), lambda i, j: (0, j)) for _ in vecs]

    def body(*refs):
        outs = fn(*[r[...] for r in refs[:n_in]])
        for r, o in zip(refs[n_in:], outs):
            r[...] = o.astype(r.dtype)

    return pl.pallas_call(
        body, out_shape=[jax.ShapeDtypeStruct((rows, width), dt) for dt in out_dtypes],
        grid=(rows // tm, width // bw), in_specs=in_specs,
        out_specs=[pl.BlockSpec((tm, bw), lambda i, j: (i, j)) for _ in out_dtypes],
        name=name, compiler_params=_params("parallel", "parallel"))(*[a for a, _ in ins], *vecs)


def _sigmoid(x):
    return 1.0 / (1.0 + jnp.exp(-x))


def _seg_sum(v):
    outs = []
    for k in range(v.shape[1] // LANES):
        vp = v[:, k * LANES:(k + 1) * LANES]
        left = lax.broadcasted_iota(jnp.int32, vp.shape, 1) < HEAD_DIM
        sl = jnp.sum(jnp.where(left, vp, 0.0), axis=-1, keepdims=True)
        sr = jnp.sum(jnp.where(left, 0.0, vp), axis=-1, keepdims=True)
        outs.append(jnp.where(left, sl, sr))
    return outs[0] if len(outs) == 1 else jnp.concatenate(outs, axis=1)


def _seg_rstd(x):
    return lax.rsqrt(_seg_sum(x * x) * (1.0 / HEAD_DIM) + RMS_EPS)


def _rms_fwd(x, g, name):
    rows, d = x.shape
    tm = 256

    def body(x_ref, g_ref, h_ref):
        xv = x_ref[...]
        r = lax.rsqrt(jnp.mean(xv * xv, axis=-1, keepdims=True) + RMS_EPS)
        h_ref[...] = ((xv * r) * g_ref[...]).astype(BF16)

    return pl.pallas_call(
        body, out_shape=jax.ShapeDtypeStruct((rows, d), BF16), grid=(rows // tm,),
        in_specs=[pl.BlockSpec((tm, d), lambda i: (i, 0)), pl.BlockSpec((1, d), lambda i: (0, 0))],
        out_specs=pl.BlockSpec((tm, d), lambda i: (i, 0)), name=name, compiler_params=_params("parallel"))(x, g)


def _rms_bwd(x, g, dh, dres, name):
    rows, d = x.shape
    tm = 256

    def body(x_ref, g_ref, dh_ref, dres_ref, dx_ref, dxb_ref, dg_ref):
        xv = x_ref[...]
        r = lax.rsqrt(jnp.mean(xv * xv, axis=-1, keepdims=True) + RMS_EPS)
        xh = xv * r
        dhv = dh_ref[...]
        dxh = dhv * g_ref[...]
        dxv = dres_ref[...] + r * (dxh - xh * jnp.mean(dxh * xh, axis=-1, keepdims=True))
        dx_ref[...] = dxv
        dxb_ref[...] = dxv.astype(BF16)
        part = jnp.sum(dhv * xh, axis=0, keepdims=True)

        @pl.when(pl.program_id(0) == 0)
        def _():
            dg_ref[...] = part

        @pl.when(pl.program_id(0) > 0)
        def _():
            dg_ref[...] += part

    row = pl.BlockSpec((tm, d), lambda i: (i, 0))
    vec = pl.BlockSpec((1, d), lambda i: (0, 0))
    return pl.pallas_call(
        body, out_shape=[jax.ShapeDtypeStruct((rows, d), F32), jax.ShapeDtypeStruct((rows, d), BF16),
                         jax.ShapeDtypeStruct((1, d), F32)],
        grid=(rows // tm,), in_specs=[row, vec, row, row], out_specs=[row, row, vec],
        name=name, compiler_params=_params("arbitrary"))(x, g, dh, dres)


def _loss_grad(y, t):
    rows, d = y.shape
    tm = 256

    def body(y_ref, t_ref, dy_ref, l_ref):
        e = y_ref[...] - t_ref[...]
        dy_ref[...] = e * (1.0 / d)
        part = jnp.zeros((1, LANES), F32) + jnp.sum(e * e) * (0.5 / d)

        @pl.when(pl.program_id(0) == 0)
        def _():
            l_ref[...] = part

        @pl.when(pl.program_id(0) > 0)
        def _():
            l_ref[...] += part

    row = pl.BlockSpec((tm, d), lambda i: (i, 0))
    return pl.pallas_call(
        body, out_shape=[jax.ShapeDtypeStruct((rows, d), F32), jax.ShapeDtypeStruct((1, LANES), F32)],
        grid=(rows // tm,), in_specs=[row, row], out_specs=[row, pl.BlockSpec((1, LANES), lambda i: (0, 0))],
        name="loss_grad", compiler_params=_params("arbitrary"))(y, t)


def _put_pairs(ref, val):
    for hp in range(N_PAIRS):
        ref[hp] = val[:, hp * LANES:(hp + 1) * LANES].astype(ref.dtype)


def _get_pairs(ref):
    return jnp.concatenate([ref[hp] for hp in range(N_PAIRS)], axis=1)


def _swap_halves(v):
    return pltpu.roll(v, HEAD_DIM, axis=1)


def _expand_kv(kv):
    left = lax.broadcasted_iota(jnp.int32, kv.shape, 1) < HEAD_DIM
    sw = _swap_halves(kv)
    h0 = jnp.where(left, kv, sw)
    h1 = jnp.where(left, sw, kv)
    return jnp.concatenate([h0, h0, h1, h1], axis=1)


def _reduce_kv(dkv):
    left = lax.broadcasted_iota(jnp.int32, (dkv.shape[0], LANES), 1) < HEAD_DIM
    t = dkv[:, 0:LANES] + dkv[:, LANES:2 * LANES]
    u = dkv[:, 2 * LANES:3 * LANES] + dkv[:, 3 * LANES:4 * LANES]
    t = t + _swap_halves(t)
    u = u + _swap_halves(u)
    return jnp.where(left, t, u)


def _qknorm_fwd(proj, gqa, gka, gqb, gkb):
    rows = proj.shape[0]
    tm = 256

    def body(qa_ref, ka_ref, va_ref, qb_ref, kb_ref, vb_ref, gqa_ref, gka_ref, gqb_ref, gkb_ref,
             oqa, oka, ova, oqb, okb, ovb):
        for src, g_ref, dst in ((qa_ref, gqa_ref, oqa), (ka_ref, gka_ref, oka), (qb_ref, gqb_ref, oqb)):
            xv = src[...]
            _put_pairs(dst, (xv * _seg_rstd(xv)) * g_ref[...])
        _put_pairs(ova, va_ref[...])
        kv = kb_ref[...]
        _put_pairs(okb, _expand_kv((kv * _seg_rstd(kv)) * gkb_ref[...]))
        _put_pairs(ovb, _expand_kv(vb_ref[...]))

    def win(width, off):
        return pl.BlockSpec((tm, width), lambda i: (i, off // width))

    vec = lambda w: pl.BlockSpec((1, w), lambda i: (0, 0))
    out = pl.BlockSpec((N_PAIRS, tm, LANES), lambda i: (0, i, 0))
    return pl.pallas_call(
        body, out_shape=[jax.ShapeDtypeStruct((N_PAIRS, rows, LANES), F32)] * 6, grid=(rows // tm,),
        in_specs=[win(WIDTH, OFF_QA), win(WIDTH, OFF_KA), win(WIDTH, OFF_VA), win(WIDTH, OFF_QB),
                  win(LANES, OFF_KB), win(LANES, OFF_VB), vec(WIDTH), vec(WIDTH), vec(WIDTH), vec(LANES)],
        out_specs=[out] * 6, name="qknorm_fwd", compiler_params=_params("parallel"))(
            proj, proj, proj, proj, proj, proj, gqa, gka, gqb, gkb)


def _norm_bwd(xv, g, dy):
    r = _seg_rstd(xv)
    xh = xv * r
    dxh = dy * g
    dx = r * (dxh - xh * (_seg_sum(dxh * xh) * (1.0 / HEAD_DIM)))
    return dx, jnp.sum(dy * xh, axis=0, keepdims=True)


def _qknorm_bwd(proj, gqa, gka, gqb, gkb, dqa, dka, dva, dqb, dkb, dvb, dgab):
    rows = proj.shape[0]
    tm = 256
    n_a = len(dqa)

    def body(*refs):
        qa_ref, ka_ref, qb_ref, kb_ref, gqa_ref, gka_ref, gqb_ref, gkb_ref = refs[:8]
        pos = 8
        dqa_refs, dka_refs, dva_refs = refs[pos:pos + n_a], refs[pos + n_a:pos + 2 * n_a], refs[pos + 2 * n_a:pos + 3 * n_a]
        pos += 3 * n_a
        dqb_ref, dkb_ref, dvb_ref, dgab_ref = refs[pos:pos + 4]
        dproj_ref, ogqa, ogka, ogqb, ogkb = refs[pos + 4:]

        def total(rs):
            acc = _get_pairs(rs[0])
            for r in rs[1:]:
                acc = acc + _get_pairs(r)
            return acc

        dx_qa, p_qa = _norm_bwd(qa_ref[...], gqa_ref[...], total(dqa_refs))
        dx_ka, p_ka = _norm_bwd(ka_ref[...], gka_ref[...], total(dka_refs))
        dx_qb, p_qb = _norm_bwd(qb_ref[...], gqb_ref[...], _get_pairs(dqb_ref))
        dx_kb, p_kb = _norm_bwd(kb_ref[...], gkb_ref[...], _reduce_kv(_get_pairs(dkb_ref)))
        dproj_ref[:, OFF_QA:OFF_QA + WIDTH] = dx_qa.astype(BF16)
        dproj_ref[:, OFF_KA:OFF_KA + WIDTH] = dx_ka.astype(BF16)
        dproj_ref[:, OFF_VA:OFF_VA + WIDTH] = total(dva_refs).astype(BF16)
        dproj_ref[:, OFF_QB:OFF_QB + WIDTH] = dx_qb.astype(BF16)
        dproj_ref[:, OFF_KB:OFF_KB + LANES] = dx_kb.astype(BF16)
        dproj_ref[:, OFF_VB:OFF_VB + LANES] = _reduce_kv(_get_pairs(dvb_ref)).astype(BF16)
        dproj_ref[:, OFF_GA:D_IN] = dgab_ref[...]
        first = pl.program_id(0) == 0
        for o_ref, part in ((ogqa, p_qa), (ogka, p_ka), (ogqb, p_qb), (ogkb, p_kb)):
            @pl.when(first)
            def _(o_ref=o_ref, part=part):
                o_ref[...] = part

            @pl.when(jnp.logical_not(first))
            def _(o_ref=o_ref, part=part):
                o_ref[...] += part

    def win(width, off):
        return pl.BlockSpec((tm, width), lambda i: (i, off // width))

    vec = lambda w: pl.BlockSpec((1, w), lambda i: (0, 0))
    row = lambda w: pl.BlockSpec((tm, w), lambda i: (i, 0))
    in_specs = [win(WIDTH, OFF_QA), win(WIDTH, OFF_KA), win(WIDTH, OFF_QB), win(LANES, OFF_KB),
                vec(WIDTH), vec(WIDTH), vec(WIDTH), vec(LANES)]
    in_specs += [pl.BlockSpec((N_PAIRS, tm, LANES), lambda i: (0, i, 0))] * (3 * n_a + 3) + [row(2 * D_MODEL)]
    return pl.pallas_call(
        body,
        out_shape=[jax.ShapeDtypeStruct((rows, D_IN), BF16), jax.ShapeDtypeStruct((1, WIDTH), F32),
                   jax.ShapeDtypeStruct((1, WIDTH), F32), jax.ShapeDtypeStruct((1, WIDTH), F32),
                   jax.ShapeDtypeStruct((1, LANES), F32)],
        grid=(rows // tm,), in_specs=in_specs,
        out_specs=[row(D_IN), vec(WIDTH), vec(WIDTH), vec(WIDTH), vec(LANES)],
        name="qknorm_bwd", compiler_params=_params("arbitrary"))(
            proj, proj, proj, proj, gqa, gka, gqb, gkb, *dqa, *dka, *dva, dqb, dkb, dvb, dgab)


def _t5_bucket(rel):
    half_b = NUM_BUCKETS // 2
    max_exact = half_b // 2
    sign = jnp.where(rel > 0, half_b, 0)
    n = jnp.abs(rel)
    nf = jnp.maximum(n, 1).astype(F32)
    large = max_exact + (jnp.log(nf / max_exact) / math.log(MAX_DISTANCE / max_exact)
                         * (half_b - max_exact)).astype(jnp.int32)
    large = jnp.minimum(large, half_b - 1)
    return sign + jnp.where(n < max_exact, n, large)


def _band_buckets(blk, dilation):
    i = jnp.arange(blk, dtype=jnp.int32)[:, None]
    j = jnp.arange(3 * blk, dtype=jnp.int32)[None, :]
    rel = j - blk - i
    return jnp.where(jnp.abs(rel) <= blk, _t5_bucket(rel * dilation), -1)


def _bias_tiles(table, buckets, head_off, name):
    blk = buckets.shape[0]

    def body(tab_ref, bk_ref, o_ref):
        h = pl.program_id(0) + head_off
        bk = bk_ref[...]
        acc = jnp.full(bk.shape, NEG_INF, F32)
        for b in range(NUM_BUCKETS):
            acc = jnp.where(bk == b, tab_ref[b, h], acc)
        o_ref[0] = acc

    return pl.pallas_call(
        body, out_shape=jax.ShapeDtypeStruct((N_HEADS, blk, 3 * blk), F32), grid=(N_HEADS,),
        in_specs=[pl.BlockSpec(memory_space=pltpu.SMEM), pl.BlockSpec((blk, 3 * blk), lambda h: (0, 0))],
        out_specs=pl.BlockSpec((1, blk, 3 * blk), lambda h: (h, 0, 0)),
        name=name, compiler_params=_params("parallel"))(table, buckets)


def _table_grad(dbias, buckets, name):
    blk = buckets.shape[0]

    def body(db_ref, bk_ref, o_ref):
        bk = bk_ref[...]
        dbv = db_ref[0]
        lane = lax.broadcasted_iota(jnp.int32, (1, LANES), 1)
        acc = jnp.zeros((1, LANES), F32)
        for b in range(NUM_BUCKETS):
            acc = jnp.where(lane == b, jnp.sum(jnp.where(bk == b, dbv, 0.0)), acc)
        o_ref[0] = acc

    out = pl.pallas_call(
        body, out_shape=jax.ShapeDtypeStruct((N_HEADS, 1, LANES), F32), grid=(N_HEADS,),
        in_specs=[pl.BlockSpec((1, blk, 3 * blk), lambda h: (h, 0, 0)), pl.BlockSpec((blk, 3 * blk), lambda h: (0, 0))],
        out_specs=pl.BlockSpec((1, 1, LANES), lambda h: (h, 0, 0)),
        name=name, compiler_params=_params("parallel"))(dbias, buckets)
    return out[:, 0, :NUM_BUCKETS]


def _dot_nt(a, b):
    return lax.dot_general(a, b, (((1,), (1,)), ((), ())), preferred_element_type=F32)


def _stack_pair(x2, left):
    return jnp.concatenate([jnp.where(left, x2, 0.0), jnp.where(left, 0.0, x2)], axis=0).astype(BF16)


def _attn_geometry(blk, d):
    chunk = blk * ITEMS if d == 1 else blk * d
    groups = 1 if d == 1 else d // ITEMS
    halo = blk if d == 1 else chunk
    return chunk, groups, halo


def _item_rows(ref, j, r0, blk, d):
    if d == 1:
        return ref[j * blk:(j + 1) * blk, :]
    return ref[pl.ds(r0 + j, blk, stride=d), :]


def _item_penalty(t, nct, j, blk, d):
    first_ok, last_ok = t > 0, t < nct - 1
    if d == 1:
        first_ok = True if j > 0 else first_ok
        last_ok = True if j < ITEMS - 1 else last_ok
    col = lax.broadcasted_iota(jnp.int32, (1, 3 * blk), 1)
    ok = jnp.logical_and(jnp.logical_or(col >= blk, first_ok), jnp.logical_or(col < 2 * blk, last_ok))
    return jnp.where(ok, 0.0, NEG_INF).astype(F32)


def _attn_specs(seq, blk, d, step_of):
    chunk, _, halo = _attn_geometry(blk, d)
    per, last = chunk // halo, seq // halo - 1
    cur = pl.BlockSpec((None, chunk, LANES), lambda hp, t: (hp, step_of(t), 0))
    prev = pl.BlockSpec((None, halo, LANES), lambda hp, t: (hp, jnp.clip(step_of(t) * per - 1, 0, last), 0))
    nxt = pl.BlockSpec((None, halo, LANES), lambda hp, t: (hp, jnp.minimum((step_of(t) + 1) * per, last), 0))
    return cur, prev, nxt


def _attn_fwd(q, k, v, bias, sink, blk, d, name):
    _, seq, _ = q.shape
    chunk, groups, _ = _attn_geometry(blk, d)
    nct = seq // chunk
    has_sink = sink is not None
    scale = HEAD_DIM ** -0.5

    def body(*refs):
        q_ref, kp, kc, kn, vp, vc, vn, b_ref = refs[:8]
        s_ref = refs[8] if has_sink else None
        o_ref, l_ref = refs[-2], refs[-1]
        t = pl.program_id(1)
        left = lax.broadcasted_iota(jnp.int32, (1, LANES), 1) < HEAD_DIM
        bias2 = b_ref[...]
        if d == 1:
            kwin = jnp.concatenate([kp[...], kc[...], kn[...]], axis=0).astype(BF16)
            vwin = jnp.concatenate([vp[...], vc[...], vn[...]], axis=0).astype(BF16)

        def group(r0):
            scores, vcats = [], []
            for j in range(ITEMS):
                qs = _stack_pair(_item_rows(q_ref, j, r0, blk, d) * scale, left)
                if d == 1:
                    kcat, vcat = kwin[j * blk:(j + 3) * blk], vwin[j * blk:(j + 3) * blk]
                else:
                    kcat = jnp.concatenate([_item_rows(r, j, r0, blk, d) for r in (kp, kc, kn)], axis=0).astype(BF16)
                    vcat = jnp.concatenate([_item_rows(r, j, r0, blk, d) for r in (vp, vc, vn)], axis=0).astype(BF16)
                scores.append(_dot_nt(qs, kcat) + bias2 + _item_penalty(t, nct, j, blk, d))
                vcats.append(vcat)
            s = jnp.concatenate(scores, axis=0)
            m = jnp.max(s, axis=-1, keepdims=True)
            if has_sink:
                sk = jnp.concatenate([s_ref[...]] * ITEMS, axis=0)
                m = jnp.maximum(m, sk)
            p = jnp.exp(s - m)
            den = jnp.sum(p, axis=-1, keepdims=True)
            if has_sink:
                den = den + jnp.exp(sk - m)
            pn = (p * (1.0 / den)).astype(BF16)
            lse = m + jnp.log(den)
            for j in range(ITEMS):
                top, mid, bot = 2 * j * blk, (2 * j + 1) * blk, (2 * j + 2) * blk
                o2 = jnp.dot(pn[top:bot], vcats[j], preferred_element_type=F32)
                o_val = jnp.where(left, o2[:blk], o2[blk:])
                l_val = jnp.where(left, lse[top:mid], lse[mid:bot])
                if d == 1:
                    o_ref[j * blk:(j + 1) * blk, :] = o_val
                    l_ref[j * blk:(j + 1) * blk, :] = l_val
                else:
                    o_ref[pl.ds(r0 + j, blk, stride=d), :] = o_val
                    l_ref[pl.ds(r0 + j, blk, stride=d), :] = l_val

        if groups == 1:
            group(0)
        else:
            def step(g, carry):
                group(g * ITEMS)
                return carry

            lax.fori_loop(0, groups, step, 0)

    cur, prev, nxt = _attn_specs(seq, blk, d, lambda t: t)
    in_specs = [cur, prev, cur, nxt, prev, cur, nxt, pl.BlockSpec((2 * blk, 3 * blk), lambda hp, t: (hp, 0))]
    args = [q, k, k, k, v, v, v, bias]
    if has_sink:
        in_specs.append(pl.BlockSpec((2 * blk, 1), lambda hp, t: (hp, 0)))
        args.append(sink)
    return pl.pallas_call(
        body, out_shape=[jax.ShapeDtypeStruct(q.shape, F32)] * 2, grid=(N_PAIRS, nct),
        in_specs=in_specs, out_specs=[cur, cur], name=name, compiler_params=_params("parallel", "parallel"))(*args)


def _attn_bwd(q, k, v, do, lse, delta, bias, sink, blk, d, name):
    _, seq, _ = q.shape
    chunk, groups, halo = _attn_geometry(blk, d)
    nct = seq // chunk
    has_sink = sink is not None
    n_in = 12 if has_sink else 11
    scale = HEAD_DIM ** -0.5

    def body(*refs):
        q_ref, kp, kc, kn, vp, vc, vn, do_ref, l_ref, d_ref, b_ref = refs[:11]
        s_ref = refs[11] if has_sink else None
        dq_ref, dk_ref, dv_ref, db_ref = refs[n_in:n_in + 4]
        ds_ref = refs[n_in + 4] if has_sink else None
        wk, wv = refs[-2], refs[-1]
        t = pl.program_id(1)

        @pl.when(t == 0)
        def _():
            wk[...] = jnp.zeros_like(wk)
            wv[...] = jnp.zeros_like(wv)
            db_ref[...] = jnp.zeros_like(db_ref)
            if has_sink:
                ds_ref[...] = jnp.zeros_like(ds_ref)

        @pl.when(t > 0)
        def _():
            for w in (wk, wv):
                keep = w[chunk:2 * chunk + halo]
                w[0:chunk + halo] = keep
                w[chunk + halo:2 * chunk + halo] = jnp.zeros((chunk, LANES), F32)

        @pl.when(t < nct)
        def _():
            lane = lax.broadcasted_iota(jnp.int32, (1, LANES), 1)
            left = lane < HEAD_DIM
            bias2 = b_ref[...]
            if d == 1:
                kwin = jnp.concatenate([kp[...], kc[...], kn[...]], axis=0).astype(BF16)
                vwin = jnp.concatenate([vp[...], vc[...], vn[...]], axis=0).astype(BF16)

            def group(r0):
                qss, doss, kcats, scores, dps, lcols, dcols = [], [], [], [], [], [], []
                for j in range(ITEMS):
                    qs = _stack_pair(_item_rows(q_ref, j, r0, blk, d) * scale, left)
                    dos = _stack_pair(_item_rows(do_ref, j, r0, blk, d), left)
                    if d == 1:
                        kcat, vcat = kwin[j * blk:(j + 3) * blk], vwin[j * blk:(j + 3) * blk]
                    else:
                        kcat = jnp.concatenate([_item_rows(r, j, r0, blk, d) for r in (kp, kc, kn)], axis=0).astype(BF16)
                        vcat = jnp.concatenate([_item_rows(r, j, r0, blk, d) for r in (vp, vc, vn)], axis=0).astype(BF16)
                    l2, d2 = _item_rows(l_ref, j, r0, blk, d), _item_rows(d_ref, j, r0, blk, d)
                    lcols.append(jnp.max(jnp.where(left, l2, NEG_INF), axis=-1, keepdims=True))
                    lcols.append(jnp.max(jnp.where(left, NEG_INF, l2), axis=-1, keepdims=True))
                    dcols.append(jnp.sum(jnp.where(lane == 0, d2, 0.0), axis=-1, keepdims=True))
                    dcols.append(jnp.sum(jnp.where(lane == HEAD_DIM, d2, 0.0), axis=-1, keepdims=True))
                    scores.append(_dot_nt(qs, kcat) + bias2 + _item_penalty(t, nct, j, blk, d))
                    dps.append(_dot_nt(dos, vcat))
                    qss.append(qs)
                    doss.append(dos)
                    kcats.append(kcat)
                lcol = jnp.concatenate(lcols, axis=0)
                dcol = jnp.concatenate(dcols, axis=0)
                p = jnp.exp(jnp.concatenate(scores, axis=0) - lcol)
                ds = p * (jnp.concatenate(dps, axis=0) - dcol)
                if has_sink:
                    sgrad = dcol * jnp.exp(jnp.concatenate([s_ref[...]] * ITEMS, axis=0) - lcol)
                for j in range(ITEMS):
                    top, bot = 2 * j * blk, (2 * j + 2) * blk
                    dsj, pj = ds[top:bot], p[top:bot]
                    db_ref[...] += dsj
                    if has_sink:
                        ds_ref[...] -= sgrad[top:bot]
                    dq2 = jnp.dot(dsj.astype(BF16), kcats[j], preferred_element_type=F32) * scale
                    dq_val = jnp.where(left, dq2[:blk], dq2[blk:])
                    if d == 1:
                        dq_ref[j * blk:(j + 1) * blk, :] = dq_val
                    else:
                        dq_ref[pl.ds(r0 + j, blk, stride=d), :] = dq_val
                    dk_new = jnp.dot(jnp.transpose(dsj).astype(BF16), qss[j], preferred_element_type=F32)
                    dv_new = jnp.dot(jnp.transpose(pj).astype(BF16), doss[j], preferred_element_type=F32)
                    for w, new in ((wk, dk_new), (wv, dv_new)):
                        if d == 1:
                            w[chunk + (j - 1) * blk:chunk + (j + 2) * blk, :] += new
                        else:
                            for c in range(3):
                                w[pl.ds(c * chunk + r0 + j, blk, stride=d), :] += new[c * blk:(c + 1) * blk]

            if groups == 1:
                group(0)
            else:
                def step(g, carry):
                    group(g * ITEMS)
                    return carry

                lax.fori_loop(0, groups, step, 0)

        dk_ref[...] = wk[0:chunk]
        dv_ref[...] = wv[0:chunk]

    cur, prev, nxt = _attn_specs(seq, blk, d, lambda t: jnp.minimum(t, nct - 1))
    lag = pl.BlockSpec((None, chunk, LANES), lambda hp, t: (hp, jnp.maximum(t - 1, 0), 0))
    band = pl.BlockSpec((2 * blk, 3 * blk), lambda hp, t: (hp, 0))
    col = pl.BlockSpec((2 * blk, 1), lambda hp, t: (hp, 0))
    in_specs = [cur, prev, cur, nxt, prev, cur, nxt, cur, cur, cur, band]
    args = [q, k, k, k, v, v, v, do, lse, delta, bias]
    out_shape = [jax.ShapeDtypeStruct(q.shape, F32)] * 3 + [jax.ShapeDtypeStruct((N_HEADS * blk, 3 * blk), F32)]
    out_specs = [cur, lag, lag, band]
    if has_sink:
        in_specs.append(col)
        args.append(sink)
        out_shape.append(jax.ShapeDtypeStruct((N_HEADS * blk, 1), F32))
        out_specs.append(col)
    window = pltpu.VMEM((2 * chunk + halo, LANES), F32)
    return pl.pallas_call(
        body, out_shape=out_shape, grid=(N_PAIRS, nct + 1), in_specs=in_specs, out_specs=out_specs,
        scratch_shapes=[window, window], name=name, compiler_params=_params("arbitrary", "arbitrary"))(*args)


def _combine_patterns(outs, lses):
    _, rows, _ = outs[0].shape
    tm = 256
    n = len(outs)

    def body(*refs):
        o_refs, l_refs = refs[:n], refs[n:2 * n]
        y_ref, lse_ref = refs[2 * n], refs[2 * n + 1]
        for hp in range(N_PAIRS):
            ls = [r[hp] for r in l_refs]
            m = functools.reduce(jnp.maximum, ls)
            es = [jnp.exp(l - m) for l in ls]
            den = functools.reduce(lambda a, b: a + b, es)
            num = functools.reduce(lambda a, b: a + b, [e * r[hp] for e, r in zip(es, o_refs)])
            y_ref[:, hp * LANES:(hp + 1) * LANES] = num / den
            lse_ref[hp] = m + jnp.log(den)

    pm = pl.BlockSpec((N_PAIRS, tm, LANES), lambda i: (0, i, 0))
    return pl.pallas_call(
        body, out_shape=[jax.ShapeDtypeStruct((rows, WIDTH), F32), jax.ShapeDtypeStruct((N_PAIRS, rows, LANES), F32)],
        grid=(rows // tm,), in_specs=[pm] * (2 * n), out_specs=[pl.BlockSpec((tm, WIDTH), lambda i: (i, 0)), pm],
        name="combine_a", compiler_params=_params("parallel"))(*outs, *lses)


def _pairs_to_tokens(a):
    _, rows, _ = a.shape
    tm = 256

    def body(a_ref, o_ref):
        o_ref[...] = _get_pairs(a_ref)

    return pl.pallas_call(
        body, out_shape=jax.ShapeDtypeStruct((rows, WIDTH), a.dtype), grid=(rows // tm,),
        in_specs=[pl.BlockSpec((N_PAIRS, tm, LANES), lambda i: (0, i, 0))],
        out_specs=pl.BlockSpec((tm, WIDTH), lambda i: (i, 0)), name="pairs_to_tokens",
        compiler_params=_params("parallel"))(a)


def _attn_bwd_prep(dy, y, name):
    rows = dy.shape[0]
    tm = 256

    def body(dy_ref, y_ref, do_ref, dl_ref):
        dyv = dy_ref[...]
        _put_pairs(do_ref, dyv)
        _put_pairs(dl_ref, _seg_sum(dyv * y_ref[...]))

    tok = pl.BlockSpec((tm, WIDTH), lambda i: (i, 0))
    pm = pl.BlockSpec((N_PAIRS, tm, LANES), lambda i: (0, i, 0))
    return pl.pallas_call(
        body, out_shape=[jax.ShapeDtypeStruct((N_PAIRS, rows, LANES), F32)] * 2, grid=(rows // tm,),
        in_specs=[tok, tok], out_specs=[pm, pm], name=name, compiler_params=_params("parallel"))(dy, y)


def _tile_gain(g, reps):
    return jnp.tile(g[None, :], (1, reps))


def _local_step(x, p, target, w_in_of, rest_of, small):
    rel_table = small["rel_table"]
    buckets_a = [_band_buckets(blk, d) for blk, d in DILATED]
    buckets_b = _band_buckets(BLK_B, 1)
    bias_a = [_bias_tiles(rel_table, bk, 0, "bias_a").reshape(N_HEADS * bk.shape[0], -1) for bk in buckets_a]
    bias_b = _bias_tiles(rel_table, buckets_b, N_HEADS, "bias_b").reshape(N_HEADS * BLK_B, -1)

    saved = []
    for l in range(DEPTH):
        g_mix, g_ffn, g_ple = (small[n][l][None, :] for n in ("norm_mix_g", "norm_ffn_g", "norm_ple_g"))
        gqa, gka, gqb = (_tile_gain(small[n][l], N_HEADS) for n in ("qnorm_a_g", "knorm_a_g", "qnorm_b_g"))
        gkb = _tile_gain(small["knorm_b_g"][l], N_KV_B)
        sink = jnp.repeat(small["sink_b"][l], BLK_B)[:, None]

        h = _rms_fwd(x, g_mix, "rms_mix")
        w_in = w_in_of(l, h)
        proj = _mm(h, w_in, "nt", F32, "mm_in")
        qa, ka, va, qb, kb, vb = _qknorm_fwd(proj, gqa, gka, gqb, gkb)
        outs, lses = [], []
        for (blk, d), bias in zip(DILATED, bias_a):
            o, ls = _attn_fwd(qa, ka, va, bias, None, blk, d, f"attn_a{d}_fwd")
            outs.append(o)
            lses.append(ls)
        ya, lse_a = _combine_patterns(outs, lses)
        yb, lse_b = _attn_fwd(qb, kb, vb, bias_b, sink, BLK_B, 1, "attn_b_fwd")
        yb = _pairs_to_tokens(yb)
        w = dict(rest_of(l, yb), w_in=w_in)
        ca = _mm(ya, w["w_branch_a"], "nn", F32, "mm_branch_a")
        cb = _mm(yb, w["w_branch_b"], "nn", F32, "mm_branch_b")

        def gate(ca_, cb_, ga_, gb_):
            return (_sigmoid(ga_) * ca_ + _sigmoid(gb_) * cb_,)

        (merged,) = _ew(gate, [(ca, 0), (cb, 0), (proj, OFF_GA), (proj, OFF_GB)], [BF16],
                        width=D_MODEL, bw=256, name="gate")
        x1 = _mm(merged, w["w_out"], "nn", F32, "mm_out", res=x)

        h2 = _rms_fwd(x1, g_ffn, "rms_ffn")
        a = _mm(h2, w["w_ffn_gate"], "nt", F32, "mm_ffn_gate")
        u = _mm(h2, w["w_ffn_up"], "nt", F32, "mm_ffn_up")

        def swiglu(a_, u_):
            return ((a_ * _sigmoid(a_)) * u_,)

        (hid,) = _ew(swiglu, [(a, 0), (u, 0)], [BF16], width=D_FF, bw=D_FF, name="swiglu")
        x2 = _mm(hid, w["w_ffn_down"], "nn", F32, "mm_ffn_down", res=x1)

        h3 = _rms_fwd(x2, g_ple, "rms_ple")
        z = _mm(h3, w["w_ple_gate"], "nn", F32, "mm_ple_gate")
        e = _mm(p[l], w["w_ple_proj"], "nn", F32, "mm_ple_proj")

        def ple(x2_, z_, e_):
            return (x2_ + _sigmoid(z_) * e_,)

        (x3,) = _ew(ple, [(x2, 0), (z, 0), (e, 0)], [F32], width=D_MODEL, bw=D_MODEL, name="ple")
        saved.append(dict(w=w, x0=x, h=h, proj=proj, qa=qa, ka=ka, va=va, qb=qb, kb=kb, vb=vb, ya=ya, lse_a=lse_a,
                          yb=yb, lse_b=lse_b, ca=ca, cb=cb, merged=merged, x1=x1, h2=h2, a=a, u=u, hid=hid,
                          x2=x2, h3=h3, z=z, e=e))
        x = x3

    dx, loss_acc = _loss_grad(x, target)
    loss = loss_acc[0, 0]

    gbig = [{} for _ in range(DEPTH)]
    marks = [{} for _ in range(DEPTH)]
    gsmall = {n: [None] * DEPTH for n in SMALL if n != "rel_table"}
    dtable_a = jnp.zeros((N_HEADS, NUM_BUCKETS), F32)
    dtable_b = jnp.zeros((N_HEADS, NUM_BUCKETS), F32)

    for l in reversed(range(DEPTH)):
        sv = saved[l]
        w = sv["w"]
        g_mix, g_ffn, g_ple = (small[n][l][None, :] for n in ("norm_mix_g", "norm_ffn_g", "norm_ple_g"))
        gqa, gka, gqb = (_tile_gain(small[n][l], N_HEADS) for n in ("qnorm_a_g", "knorm_a_g", "qnorm_b_g"))
        gkb = _tile_gain(small["knorm_b_g"][l], N_KV_B)
        sink = jnp.repeat(small["sink_b"][l], BLK_B)[:, None]

        def ple_bwd(dx_, z_, e_):
            s = _sigmoid(z_)
            return dx_ * s, dx_ * e_ * (s * (1.0 - s))

        de, dz = _ew(ple_bwd, [(dx, 0), (sv["z"], 0), (sv["e"], 0)], [BF16, BF16], width=D_MODEL, bw=D_MODEL,
                     name="ple_bwd")
        gbig[l]["w_ple_proj"] = _mm(p[l], de, "tn", F32, "mm_d_ple_proj")
        gbig[l]["w_ple_gate"] = _mm(sv["h3"], dz, "tn", F32, "mm_d_ple_gate")
        dh3 = _mm(dz, w["w_ple_gate"], "nt", F32, "mm_dh3")
        dx, dxb, gsmall["norm_ple_g"][l] = _rms_bwd(sv["x2"], g_ple, dh3, dx, "rms_ple_bwd")

        dhid = _mm(dxb, w["w_ffn_down"], "nt", F32, "mm_dhid")
        gbig[l]["w_ffn_down"] = _mm(sv["hid"], dxb, "tn", F32, "mm_d_ffn_down")

        def swiglu_bwd(a_, u_, dh_):
            s = _sigmoid(a_)
            return dh_ * u_ * (s * (1.0 + a_ * (1.0 - s))), dh_ * (a_ * s)

        da, du = _ew(swiglu_bwd, [(sv["a"], 0), (sv["u"], 0), (dhid, 0)], [BF16, BF16], width=D_FF, bw=D_FF,
                     name="swiglu_bwd")
        gbig[l]["w_ffn_gate"] = _mm(da, sv["h2"], "tn", F32, "mm_d_ffn_gate")
        gbig[l]["w_ffn_up"] = _mm(du, sv["h2"], "tn", F32, "mm_d_ffn_up")
        dh2 = _mm(da, w["w_ffn_gate"], "nn", F32, "mm_dh2_gate")
        dh2 = _mm(du, w["w_ffn_up"], "nn", F32, "mm_dh2_up", res=dh2)
        dx, dxb, gsmall["norm_ffn_g"][l] = _rms_bwd(sv["x1"], g_ffn, dh2, dx, "rms_ffn_bwd")

        dmerged = _mm(dxb, w["w_out"], "nt", F32, "mm_dmerged")
        marks[l]["ffn_bwd_done"] = dmerged
        gbig[l]["w_out"] = _mm(sv["merged"], dxb, "tn", F32, "mm_d_out")

        def gate_bwd(dm_, ca_, cb_, ga_, gb_):
            sa, sb = _sigmoid(ga_), _sigmoid(gb_)
            return dm_ * sa, dm_ * sb, dm_ * ca_ * (sa * (1.0 - sa)), dm_ * cb_ * (sb * (1.0 - sb))

        dca, dcb, dga, dgb = _ew(gate_bwd, [(dmerged, 0), (sv["ca"], 0), (sv["cb"], 0), (sv["proj"], OFF_GA),
                                            (sv["proj"], OFF_GB)], [BF16, BF16, BF16, BF16],
                                 width=D_MODEL, bw=256, name="gate_bwd")
        dgab = jnp.concatenate([dga, dgb], axis=1)
        gbig[l]["w_branch_a"] = _mm(sv["ya"], dca, "tn", F32, "mm_d_branch_a")
        gbig[l]["w_branch_b"] = _mm(sv["yb"], dcb, "tn", F32, "mm_d_branch_b")
        dya = _mm(dca, w["w_branch_a"], "nt", F32, "mm_dya")
        dyb = _mm(dcb, w["w_branch_b"], "nt", F32, "mm_dyb")

        dya, delta_a = _attn_bwd_prep(dya, sv["ya"], "attn_a_bwd_prep")
        dyb, delta_b = _attn_bwd_prep(dyb, sv["yb"], "attn_b_bwd_prep")

        dqa, dka, dva = [], [], []
        for (blk, d), bias, bk in zip(DILATED, bias_a, buckets_a):
            dq_, dk_, dv_, db_ = _attn_bwd(sv["qa"], sv["ka"], sv["va"], dya, sv["lse_a"], delta_a, bias, None, blk, d,
                                           f"attn_a{d}_bwd")
            dqa.append(dq_)
            dka.append(dk_)
            dva.append(dv_)
            dtable_a = dtable_a + _table_grad(db_.reshape(N_HEADS, blk, 3 * blk), bk, "table_grad_a")
        dqb, dkb, dvb, db_, dsink = _attn_bwd(sv["qb"], sv["kb"], sv["vb"], dyb, sv["lse_b"], delta_b, bias_b, sink,
                                              BLK_B, 1, "attn_b_bwd")
        dtable_b = dtable_b + _table_grad(db_.reshape(N_HEADS, BLK_B, 3 * BLK_B), buckets_b, "table_grad_b")
        gsmall["sink_b"][l] = dsink.reshape(N_HEADS, BLK_B).sum(axis=1)

        dproj, pqa, pka, pqb, pkb = _qknorm_bwd(sv["proj"], gqa, gka, gqb, gkb, dqa, dka, dva, dqb, dkb, dvb, dgab)
        marks[l]["attn_bwd_done"] = dproj
        gsmall["qnorm_a_g"][l] = pqa.reshape(N_HEADS, HEAD_DIM).sum(0)
        gsmall["knorm_a_g"][l] = pka.reshape(N_HEADS, HEAD_DIM).sum(0)
        gsmall["qnorm_b_g"][l] = pqb.reshape(N_HEADS, HEAD_DIM).sum(0)
        gsmall["knorm_b_g"][l] = pkb.reshape(N_KV_B, HEAD_DIM).sum(0)
        gbig[l]["w_in"] = _mm(dproj, sv["h"], "tn", F32, "mm_d_in")
        dh = _mm(dproj, w["w_in"], "nn", F32, "mm_dh")
        dx, _, gsmall["norm_mix_g"][l] = _rms_bwd(sv["x0"], g_mix, dh, dx, "rms_mix_bwd")
        gsmall["norm_mix_g"][l] = gsmall["norm_mix_g"][l][0]
        gsmall["norm_ffn_g"][l] = gsmall["norm_ffn_g"][l][0]
        gsmall["norm_ple_g"][l] = gsmall["norm_ple_g"][l][0]

    gsmall = {n: jnp.stack(v) for n, v in gsmall.items()}
    gsmall["rel_table"] = jnp.concatenate([dtable_a, dtable_b], axis=0).T
    return loss, dx, gbig, gsmall, marks


def _place():
    return lax.axis_index("x"), lax.axis_index("y"), lax.axis_index("c")


def _flip(v, bit):
    return 1 - v if bit else v


CHIP_RELATIONS = ((0, 1), (1, 0), (1, 1))
ANY = pl.BlockSpec(memory_space=pl.ANY)


def _allgather_body(w_refs, out_refs, send_sems, recv_sems):
    x, y, c = _place()
    chips = [(_flip(x, a), _flip(y, b)) for a, b in CHIP_RELATIONS]

    def make(g):
        w_ref, out_ref = w_refs[g], out_refs[g]
        half = w_ref.shape[0] // 2

        def part(px, py, pc):
            return out_ref.at[2 * px + py, pl.ds(pc * half, half), :]

        def copy(k, block, to, src=None):
            return pltpu.make_async_remote_copy(
                src_ref=part(*block) if src is None else src, dst_ref=part(*block),
                send_sem=send_sems.at[7 * g + k], recv_sem=recv_sems.at[7 * g + k], device_id=to,
                device_id_type=MESH_ID)

        own = pltpu.make_async_remote_copy(
            src_ref=w_ref, dst_ref=out_ref.at[2 * x + y], send_sem=send_sems.at[7 * g + 6],
            recv_sem=recv_sems.at[7 * g + 6], device_id=(x, y, 1 - c), device_id_type=MESH_ID)
        first = [copy(k, (x, y, c), (*chip, c), src=w_ref.at[pl.ds(c * half, half), :]) for k, chip in enumerate(chips)]
        passed = [copy(3 + k, (*chip, c), (x, y, 1 - c)) for k, chip in enumerate(chips)]
        arrive = [copy(k, (*chip, c), (x, y, c)) for k, chip in enumerate(chips)]
        arrive2 = [copy(3 + k, (*chip, 1 - c), (x, y, c)) for k, chip in enumerate(chips)]
        return own, first, passed, arrive, arrive2

    made = [make(g) for g in range(len(w_refs))]
    for own, first, _, _, _ in made:
        own.start()
        for cp in first:
            cp.start()
    for _, _, passed, arrive, _ in made:
        for k in range(3):
            arrive[k].wait_recv()
            passed[k].start()
    for own, first, passed, _, arrive2 in made:
        for k in range(3):
            arrive2[k].wait_recv()
        own.wait_recv()
        for cp in first + passed + [own]:
            cp.wait_send()


def _sibling(x, y, c):
    return [(x, y, 1 - c)]


def _same_core_of_other_chips(x, y, c):
    return [(_flip(x, a), _flip(y, b), c) for a, b in CHIP_RELATIONS]


def _exchange(body, ins, out_types, n_sems, name, sequencer=None):
    n = len(ins)
    sems = (pltpu.SemaphoreType.DMA((n_sems,)), pltpu.SemaphoreType.DMA((n_sems,)))
    if sequencer is None:
        in_place = out_types is None
        out_shape = [jax.ShapeDtypeStruct(a.shape, a.dtype) for a in ins] if in_place else out_types

        def tc_body(*refs):
            body(refs[:n], refs[n:n + len(out_shape)], refs[-2], refs[-1])

        return list(pl.pallas_call(
            tc_body, out_shape=out_shape, in_specs=[ANY] * n, out_specs=[ANY] * len(out_shape),
            input_output_aliases={g: g for g in range(n)} if in_place else {}, scratch_shapes=list(sems), name=name)(*ins))

    collective_id, peers = sequencer
    hbm = pltpu.MemorySpace.HBM
    in_refs = [jax.new_ref(a, memory_space=hbm) for a in ins]
    out_refs = in_refs if out_types is None else [jax.empty_ref(t, memory_space=hbm) for t in out_types]

    @pl.kernel(mesh=plsc.ScalarSubcoreMesh(axis_name="sequencer", num_cores=1), name=name, scratch_types=sems,
               compiler_params=pltpu.CompilerParams(collective_id=collective_id))
    def launch(send_sems, recv_sems):
        barrier = pltpu.get_barrier_semaphore()
        devices = peers(*_place())
        for device in devices:
            pl.semaphore_signal(barrier, inc=1, device_id=device, device_id_type=MESH_ID)
        pl.semaphore_wait(barrier, len(devices))
        body(in_refs, out_refs, send_sems, recv_sems)

    launch()
    return [r[...] for r in out_refs]


def _allgather(shards, name, sequencer=None):
    out_types = [jax.ShapeDtypeStruct((N_CHIPS,) + s.shape, s.dtype) for s in shards]
    if sequencer is not None:
        sequencer = (sequencer, lambda x, y, c: _sibling(x, y, c) + _same_core_of_other_chips(x, y, c))
    return _exchange(_allgather_body, shards, out_types, 7 * len(shards), name, sequencer)


def _half_tile(half):
    return max(t for t in range(16, 1025, 16) if half % t == 0)


def _run_copies(cps):
    for cp in cps:
        cp.start()
    for cp in cps:
        cp.wait_recv()
    for cp in cps:
        cp.wait_send()


def _sibling_halves(gsends, name, sequencer=None):
    def body(g_refs, out_refs, send_sems, recv_sems):
        x, y, c = _place()
        cps = []
        for g, (g_ref, out_ref) in enumerate(zip(g_refs, out_refs)):
            half = g_ref.shape[1] // 2
            cps.append(pltpu.make_async_remote_copy(
                src_ref=g_ref.at[:, pl.ds((1 - c) * half, half), :], dst_ref=out_ref,
                send_sem=send_sems.at[g], recv_sem=recv_sems.at[g], device_id=(x, y, 1 - c), device_id_type=MESH_ID))
        _run_copies(cps)

    out_types = [jax.ShapeDtypeStruct((s.shape[0], s.shape[1] // 2, s.shape[2]), s.dtype) for s in gsends]
    return _exchange(body, gsends, out_types, len(gsends), name, sequencer and (sequencer, _sibling))


def _chip_sums(gsend, sib, place):
    n, rows, cols = gsend.shape
    half = rows // 2
    tm = _half_tile(half)
    nblk = half // tm

    def body(s_ref, g_ref, sib_ref, o_ref):
        o_ref[0] = (g_ref[0].astype(F32) + sib_ref[0].astype(F32)).astype(o_ref.dtype)

    grid_spec = pltpu.PrefetchScalarGridSpec(
        num_scalar_prefetch=1, grid=(n, nblk),
        in_specs=[pl.BlockSpec((1, tm, cols), lambda k, i, s: (jnp.bitwise_xor(s[0], k), s[1] * nblk + i, 0)),
                  pl.BlockSpec((1, tm, cols), lambda k, i, s: (jnp.bitwise_xor(s[0], k), i, 0))],
        out_specs=pl.BlockSpec((1, tm, cols), lambda k, i, s: (k, i, 0)))
    return pl.pallas_call(
        body, out_shape=jax.ShapeDtypeStruct((n, half, cols), BF16), grid_spec=grid_spec,
        name="rs_chip_sums", compiler_params=_params("parallel", "parallel"))(place, gsend, sib)


def _exchange_chip_sums(tsends, name, sequencer=None):
    def body(t_refs, out_refs, send_sems, recv_sems):
        x, y, c = _place()
        cps = []
        for g, (t_ref, out_ref) in enumerate(zip(t_refs, out_refs)):
            for k, device in enumerate(_same_core_of_other_chips(x, y, c)):
                cps.append(pltpu.make_async_remote_copy(
                    src_ref=t_ref.at[k + 1], dst_ref=out_ref.at[k], send_sem=send_sems.at[3 * g + k],
                    recv_sem=recv_sems.at[3 * g + k], device_id=device, device_id_type=MESH_ID))
        _run_copies(cps)

    out_types = [jax.ShapeDtypeStruct((3,) + s.shape[1:], s.dtype) for s in tsends]
    return _exchange(body, tsends, out_types, 3 * len(tsends), name,
                     sequencer and (sequencer, _same_core_of_other_chips))


def _final_sum(tsend, recv, place):
    n, half, cols = tsend.shape
    tm = _half_tile(half)
    nblk = half // tm

    def body(s_ref, t_ref, r_ref, o_ref):
        o_ref[...] = ((t_ref[0].astype(F32) + r_ref[0].astype(F32)) + r_ref[1].astype(F32)) + r_ref[2].astype(F32)

    grid_spec = pltpu.PrefetchScalarGridSpec(
        num_scalar_prefetch=1, grid=(nblk,),
        in_specs=[pl.BlockSpec((1, tm, cols), lambda i, s: (0, i, 0)), pl.BlockSpec((n - 1, tm, cols), lambda i, s: (0, i, 0))],
        out_specs=pl.BlockSpec((tm, cols), lambda i, s: (s[1] * nblk + i, 0)))
    return pl.pallas_call(
        body, out_shape=jax.ShapeDtypeStruct((2 * half, cols), F32), grid_spec=grid_spec, name="rs_final_sum",
        compiler_params=_params("parallel"))(place, tsend, recv)


def _join_halves(gfulls, name, sequencer=None):
    def body(g_refs, out_refs, send_sems, recv_sems):
        x, y, c = _place()
        n = len(g_refs)

        def copy(g, pc):
            half = g_refs[g].shape[0] // 2
            return pltpu.make_async_remote_copy(
                src_ref=g_refs[g].at[pl.ds(pc * half, half), :], dst_ref=out_refs[g].at[pl.ds(pc * half, half), :],
                send_sem=send_sems.at[g], recv_sem=recv_sems.at[g], device_id=(x, y, 1 - c), device_id_type=MESH_ID)

        mine = [copy(g, c) for g in range(n)]
        for cp in mine:
            cp.start()
        for g in range(n):
            copy(g, 1 - c).wait_recv()
        for cp in mine:
            cp.wait_send()

    return _exchange(body, gfulls, None, len(gfulls), name, sequencer and (sequencer, _sibling))


def _allreduce_small(v):
    rows, cols = v.shape

    def body(v_ref, out_ref, buf, send_sems, recv_sems):
        x, y, c = _place()
        cps = []
        for k in range(1, 8):
            peer = (_flip(x, (k >> 2) & 1), _flip(y, (k >> 1) & 1), _flip(c, k & 1))
            cps.append(pltpu.make_async_remote_copy(
                src_ref=v_ref, dst_ref=buf.at[k - 1], send_sem=send_sems.at[k - 1], recv_sem=recv_sems.at[k - 1],
                device_id=peer, device_id_type=MESH_ID))
        for cp in cps:
            cp.start()
        for cp in cps:
            cp.wait_recv()
        for cp in cps:
            cp.wait_send()
        t0 = v_ref[...] + buf[0]
        t1 = buf[1] + buf[2]
        t2 = buf[3] + buf[4]
        t3 = buf[5] + buf[6]
        out_ref[...] = (t0 + t1) + (t2 + t3)

    vm = pl.BlockSpec(memory_space=pltpu.VMEM)
    return pl.pallas_call(
        body, out_shape=jax.ShapeDtypeStruct((rows, cols), F32), in_specs=[vm], out_specs=vm,
        scratch_shapes=[pltpu.VMEM((7, rows, cols), F32), pltpu.SemaphoreType.DMA((7,)), pltpu.SemaphoreType.DMA((7,))],
        name="allreduce_small")(v)


BIG_INFO = {n: (shape, ax) for n, shape, ax in BIG}
GROUPS = (("w_in",), ("w_ffn_gate", "w_ffn_up", "w_ffn_down", "w_out", "w_ple_gate"),
          ("w_branch_a", "w_branch_b", "w_ple_proj"))


def _shard_shape(name):
    (k, m), ax = BIG_INFO[name]
    return (k // N_CHIPS, m) if ax == 0 else (k, m // N_CHIPS)


def _group_rows(group):
    offs, off = {}, 0
    for n in group:
        offs[n] = off
        off += _shard_shape(n)[0]
    return offs, off


def _pack_groups(shards, layer, dtype):
    return [jnp.concatenate([shards[n][layer].astype(dtype) for n in group], axis=0) for group in GROUPS]


def _unpack_full(gathered, groups):
    out = {}
    for group, arr in zip(groups, gathered):
        offs, _ = _group_rows(group)
        for n in group:
            rows, cols = _shard_shape(n)
            (k, m), ax = BIG_INFO[n]
            slab = arr[:, offs[n]:offs[n] + rows]
            out[n] = slab.reshape(k, m) if ax == 0 else jnp.transpose(slab, (1, 0, 2)).reshape(k, m)
    return out


def _pack_grads(gfull):
    out = []
    for group in GROUPS:
        parts = []
        for n in group:
            rows, cols = _shard_shape(n)
            ax = BIG_INFO[n][1]
            slab = (gfull[n].reshape(N_CHIPS, rows, cols) if ax == 0
                    else jnp.transpose(gfull[n].reshape(rows, N_CHIPS, cols), (1, 0, 2)))
            parts.append(slab.astype(BF16))
        out.append(jnp.concatenate(parts, axis=1))
    return out


def _after(values, mark):
    values, _ = lax.optimization_barrier((values, mark))
    return values


def _reduce_scatter_begin(gsends, place, tag, ids):
    sibs = _sibling_halves(gsends, "rs_sibling_halves_" + tag, ids[0])
    tsends = [_chip_sums(g, s, place) for g, s in zip(gsends, sibs)]
    return tsends, _exchange_chip_sums(tsends, "rs_exchange_" + tag, ids[1])


def _reduce_scatter_finish(begun, place, tag, ids, hold):
    tsends, recvs = begun
    recvs = _after(recvs, hold)
    return _join_halves([_final_sum(t, r, place) for t, r in zip(tsends, recvs)], "rs_join_halves_" + tag, ids[2])


SMALL_SHAPES = {"rel_table": (NUM_BUCKETS, 2 * N_HEADS), "norm_mix_g": (DEPTH, D_MODEL), "qnorm_a_g": (DEPTH, HEAD_DIM),
                "knorm_a_g": (DEPTH, HEAD_DIM), "qnorm_b_g": (DEPTH, HEAD_DIM), "knorm_b_g": (DEPTH, HEAD_DIM),
                "sink_b": (DEPTH, N_HEADS), "norm_ffn_g": (DEPTH, D_MODEL), "norm_ple_g": (DEPTH, D_MODEL)}


def _pack_small(vals):
    flat = jnp.concatenate([vals[n].astype(F32).reshape(-1) for n in SMALL])
    flat = jnp.concatenate([flat, jnp.zeros((SMALL_ROWS * LANES - flat.shape[0],), F32)])
    return flat.reshape(SMALL_ROWS, LANES)


def _unpack_small(packed):
    flat, out, off = packed.reshape(-1), {}, 0
    for n in SMALL:
        size = math.prod(SMALL_SHAPES[n])
        out[n] = flat[off:off + size].reshape(SMALL_SHAPES[n])
        off += size
    return out


def _adamw(w, gs, g_row, m, v, name):
    c1 = 1.0 - ADAM_B1 ** ADAM_STEP
    c2 = 1.0 - ADAM_B2 ** ADAM_STEP
    total, width = w.shape
    n_layers = len(gs)
    per = total // n_layers
    tm = max(t for t in range(8, 513, 8) if per % t == 0 and g_row % t == 0)
    nblk = per // tm

    def body(*refs):
        w_ref, g_refs = refs[0], refs[1:1 + n_layers]
        m_ref, v_ref, og, od, om, ov = refs[1 + n_layers:]
        layer = pl.program_id(0) // nblk
        g = g_refs[0][...]
        for l in range(1, n_layers):
            g = jnp.where(layer == l, g_refs[l][...], g)
        m_new = ADAM_B1 * m_ref[...] + (1.0 - ADAM_B1) * g
        v_new = ADAM_B2 * v_ref[...] + (1.0 - ADAM_B2) * (g * g)
        og[...] = g
        od[...] = -ADAM_LR * ((m_new / c1) / (jnp.sqrt(v_new / c2) + ADAM_EPS) + ADAM_WD * w_ref[...])
        om[...] = m_new
        ov[...] = v_new

    row = pl.BlockSpec((tm, width), lambda i: (i, 0))
    g_specs = [pl.BlockSpec((tm, width), lambda i, l=l: (g_row // tm + jnp.clip(i - l * nblk, 0, nblk - 1), 0))
               for l in range(n_layers)]
    return pl.pallas_call(
        body, out_shape=[jax.ShapeDtypeStruct((total, width), F32)] * 4, grid=(total // tm,),
        in_specs=[row] + g_specs + [row, row], out_specs=[row] * 4, name=name,
        compiler_params=_params("parallel"))(w, *gs, m, v)


def kernel(x, p, rel_table, norm_mix_g, w_in, qnorm_a_g, knorm_a_g, qnorm_b_g, knorm_b_g, sink_b, w_branch_a, w_branch_b, w_out, norm_ffn_g, w_ffn_gate, w_ffn_up, w_ffn_down, norm_ple_g, w_ple_gate, w_ple_proj, loss_target, m_rel_table, m_norm_mix_g, m_w_in, m_qnorm_a_g, m_knorm_a_g, m_qnorm_b_g, m_knorm_b_g, m_sink_b, m_w_branch_a, m_w_branch_b, m_w_out, m_norm_ffn_g, m_w_ffn_gate, m_w_ffn_up, m_w_ffn_down, m_norm_ple_g, m_w_ple_gate, m_w_ple_proj, v_rel_table, v_norm_mix_g, v_w_in, v_qnorm_a_g, v_knorm_a_g, v_qnorm_b_g, v_knorm_b_g, v_sink_b, v_w_branch_a, v_w_branch_b, v_w_out, v_norm_ffn_g, v_w_ffn_gate, v_w_ffn_up, v_w_ffn_down, v_norm_ple_g, v_w_ple_gate, v_w_ple_proj):
    given = dict(locals())

    def held(name, a):
        return jnp.swapaxes(a, 1, 2) if name in TRANSPOSED else a

    weights = {n: held(n, given[n]) for n in WEIGHTS}
    moments_m = {n: held(n, given["m_" + n]) for n in WEIGHTS}
    moments_v = {n: held(n, given["v_" + n]) for n in WEIGHTS}
    xi, yi, ci = _place()
    place = jnp.stack([2 * xi + yi, ci]).astype(jnp.int32)

    shards = [_pack_groups(weights, l, BF16) for l in range(DEPTH)]
    w_in0 = _allgather(shards[0][:1], "allgather_w_in_layer0", sequencer=9)
    rest0 = _allgather(_after(shards[0][1:], w_in0), "allgather_rest_layer0", sequencer=1)
    gathered = [w_in0 + rest0, None]
    small = {n: weights[n] for n in SMALL}

    def w_in_of(l, mark):
        return _unpack_full(gathered[l][:1] if l == 0 else _after(gathered[l][:1], mark), GROUPS[:1])["w_in"]

    def rest_of(l, mark):
        if l == 0:
            gathered[1] = _allgather(_after(shards[1], mark), "allgather_layer1", sequencer=2)
        return _unpack_full(_after(gathered[l][1:], mark), GROUPS[1:])

    loss, dx, gbig, gsmall, marks = _local_step(x[0], p[:, 0], loss_target[0], w_in_of, rest_of, small)

    gsends = [_pack_grads(gbig[l]) for l in range(DEPTH)]
    stages = {"layer1": (gsends[1], (3, 4, 5)), "rest_layer0": (gsends[0][1:], (6, 7, 8)),
              "w_in_layer0": (gsends[0][:1], (10, 11, 12))}
    begun = {tag: _reduce_scatter_begin(g, place, tag, ids) for tag, (g, ids) in stages.items()}

    def finish(tag, hold):
        return _reduce_scatter_finish(begun[tag], place, tag, stages[tag][1], hold)

    red1 = finish("layer1", marks[0]["attn_bwd_done"])
    rest0 = finish("rest_layer0", marks[0]["attn_bwd_done"])

    grads, delta, new_m, new_v = {}, {}, {}, {}

    def update(group, reduced):
        offs, _ = _group_rows(group)
        for n in group:
            shape = weights[n].shape
            two_d = lambda a: a.reshape(shape[0] * shape[1], shape[2])
            outs = _adamw(two_d(weights[n]), reduced, offs[n], two_d(moments_m[n]), two_d(moments_v[n]), "adamw_" + n)
            grads[n], delta[n], new_m[n], new_v[n] = (held(n, o.reshape(shape)) for o in outs)

    for gi in (1, 2):
        update(GROUPS[gi], _after([rest0[gi - 1], red1[gi]], begun["w_in_layer0"][0]))
    others_done = [dx] + [delta[n] for gi in (1, 2) for n in GROUPS[gi]]
    update(GROUPS[0], [finish("w_in_layer0", others_done)[0], red1[0]])
    small_grads = _allreduce_small(_pack_small(gsmall))
    g_, d_, m_, v_ = _adamw(_pack_small(weights), [small_grads], 0, _pack_small(moments_m), _pack_small(moments_v),
                            "adamw_small")
    grads.update(_unpack_small(g_))
    delta.update(_unpack_small(d_))
    new_m.update(_unpack_small(m_))
    new_v.update(_unpack_small(v_))

    loss = lax.psum(loss, ("x", "y", "c"))
    return (loss, dx[None], *[grads[n] for n in WEIGHTS], *[delta[n] for n in WEIGHTS],
            *[new_m[n] for n in WEIGHTS], *[new_v[n] for n in WEIGHTS])
```

```python
import functools
import math

import jax
import jax.numpy as jnp
from jax import lax
from jax.experimental import pallas as pl
from jax.experimental.pallas import tpu as pltpu
from jax.experimental.pallas import tpu_sc as plsc

F32 = jnp.float32
BF16 = jnp.bfloat16
MESH_ID = pl.DeviceIdType.MESH

SEQ = 2048
D_MODEL = 1024
DEPTH = 2
HEAD_DIM = 64
N_HEADS = 8
WIDTH = N_HEADS * HEAD_DIM
N_PAIRS = 4
ITEMS = 4
N_KV_B = 2
PLE_DIM = 256
D_FF = 2816
D_IN = 4352
OFF_QA, OFF_KA, OFF_VA, OFF_QB, OFF_KB, OFF_VB, OFF_GA, OFF_GB = 0, 512, 1024, 1536, 2048, 2176, 2304, 3328
DILATED = ((64, 1), (64, 4), (64, 16))
BLK_B = 128
NUM_BUCKETS = 32
MAX_DISTANCE = 1024
RMS_EPS = 1e-6
NEG_INF = -1e30
LANES = 128
VMEM_LIMIT = 48 * 1024 * 1024

ADAM_LR, ADAM_B1, ADAM_B2, ADAM_EPS, ADAM_WD, ADAM_STEP = 0.001, 0.9, 0.999, 1e-08, 0.01, 10

TRANSPOSED = ("w_in", "w_ffn_gate", "w_ffn_up")
BIG = (
    ("w_in", (D_IN, D_MODEL), 0),
    ("w_branch_a", (WIDTH, D_MODEL), 1),
    ("w_branch_b", (WIDTH, D_MODEL), 1),
    ("w_out", (D_MODEL, D_MODEL), 0),
    ("w_ffn_gate", (D_FF, D_MODEL), 0),
    ("w_ffn_up", (D_FF, D_MODEL), 0),
    ("w_ffn_down", (D_FF, D_MODEL), 0),
    ("w_ple_gate", (D_MODEL, D_MODEL), 0),
    ("w_ple_proj", (PLE_DIM, D_MODEL), 1),
)
SMALL = ("rel_table", "norm_mix_g", "qnorm_a_g", "knorm_a_g", "qnorm_b_g", "knorm_b_g", "sink_b",
         "norm_ffn_g", "norm_ple_g")
WEIGHTS = ("rel_table", "norm_mix_g", "w_in", "qnorm_a_g", "knorm_a_g", "qnorm_b_g", "knorm_b_g", "sink_b",
           "w_branch_a", "w_branch_b", "w_out", "norm_ffn_g", "w_ffn_gate", "w_ffn_up", "w_ffn_down",
           "norm_ple_g", "w_ple_gate", "w_ple_proj")
N_CHIPS = 4
SMALL_ROWS = 64


def _params(*sem):
    return pltpu.CompilerParams(dimension_semantics=sem, vmem_limit_bytes=VMEM_LIMIT)


def _pick(dim, target):
    for t in (target, 512, 256, 128, 64, 32, 16, 8):
        if t <= target and dim % t == 0:
            return t
    return dim


MM_VMEM_BUDGET = 40 * 1024 * 1024
STEP_OVERHEAD_S = 0.4e-6
TILE_DMA_BYTES_PER_S = 1.5e12


def _mm_dims(a, b, mode):
    if mode == "nn":
        return a.shape[0], b.shape[1], a.shape[1]
    if mode == "nt":
        return a.shape[0], b.shape[0], a.shape[1]
    return a.shape[1], b.shape[1], a.shape[0]


def _mm_tiles(m, n, pairs, tile_bytes, col_offsets):
    best = None
    for tm in (t for t in range(LANES, m + 1, LANES) if m % t == 0):
        for tn in (t for t in range(LANES, n + 1, LANES) if n % t == 0 and all(o % t == 0 for o in col_offsets)):
            io = sum(tm * k * ab + tn * k * bb for k, ab, bb in pairs) + tm * tn * sum(tile_bytes)
            casts = sum((tm * k * 2 if ab == 4 else 0) + (tn * k * 2 if bb == 4 else 0) for k, ab, bb in pairs)
            if 2 * io + len(pairs) * tm * tn * 4 + casts > MM_VMEM_BUDGET:
                continue
            cost = (m // tm) * (n // tn) * STEP_OVERHEAD_S + io / TILE_DMA_BYTES_PER_S
            if best is None or (cost, -tm) < best[0]:
                best = ((cost, -tm), tm, tn)
    return best[1], best[2]


def _mm_fused(pairs, extras, epilogue, out_dtypes, name):
    m, n, _ = _mm_dims(*pairs[0])
    assert all(_mm_dims(*p)[:2] == (m, n) for p in pairs)
    tm, tn = _mm_tiles(
        m, n, [(_mm_dims(a, b, mode)[2], a.dtype.itemsize, b.dtype.itemsize) for a, b, mode in pairs],
        [e.dtype.itemsize for e, _ in extras] + [jnp.dtype(d).itemsize for d in out_dtypes], [off for _, off in extras])
    dims = {"nn": (((1,), (0,)), ((), ())), "nt": (((1,), (1,)), ((), ())), "tn": (((0,), (0,)), ((), ()))}
    in_specs, args = [], []
    for a, b, mode in pairs:
        k = _mm_dims(a, b, mode)[2]
        in_specs.append(pl.BlockSpec((k, tm), lambda i, j: (0, i)) if mode == "tn" else pl.BlockSpec((tm, k), lambda i, j: (i, 0)))
        in_specs.append(pl.BlockSpec((tn, k), lambda i, j: (j, 0)) if mode == "nt" else pl.BlockSpec((k, tn), lambda i, j: (0, j)))
        args += [a, b]
    for e, off in extras:
        in_specs.append(pl.BlockSpec((tm, tn), lambda i, j, o=off // tn: (i, o + j)))
        args.append(e)
    n_pairs, n_in = len(pairs), 2 * len(pairs) + len(extras)

    def body(*refs):
        products = [lax.dot_general(refs[2 * p][...].astype(BF16), refs[2 * p + 1][...].astype(BF16), dims[pairs[p][2]],
                                    preferred_element_type=F32) for p in range(n_pairs)]
        outs = epilogue(products, [r[...] for r in refs[2 * n_pairs:n_in]])
        for r, o in zip(refs[n_in:], outs):
            r[...] = o.astype(r.dtype)

    tile = pl.BlockSpec((tm, tn), lambda i, j: (i, j))
    return pl.pallas_call(
        body, out_shape=[jax.ShapeDtypeStruct((m, n), d) for d in out_dtypes], grid=(m // tm, n // tn),
        in_specs=in_specs, out_specs=[tile] * len(out_dtypes), name=name,
        compiler_params=_params("parallel", "parallel"))(*args)


def _mm(a, b, mode, out_dtype, name, res=None):
    if res is None:
        return _mm_fused([(a, b, mode)], [], lambda products, extra: products, [out_dtype], name)[0]
    return _mm_fused([(a, b, mode)], [(res, 0)], lambda products, extra: [products[0] + extra[0]], [out_dtype], name)[0]


def _ew(fn, ins, out_dtypes, *, width, bw, name, vecs=(), tm=256):
    rows = ins[0][0].shape[0]
    tm = _pick(rows, tm)
    n_in = len(ins) + len(vecs)

    def col_map(off_blocks):
        return lambda i, j: (i, off_blocks + j)

    in_specs = [pl.BlockSpec((tm, bw), col_map(off // bw)) for _, off in ins]
    in_specs += [pl.BlockSpec((1, bw), lambda i, j: (0, j)) for _ in vecs]

    def body(*refs):
        outs = fn(*[r[...] for r in refs[:n_in]])
        for r, o in zip(refs[n_in:], outs):
            r[...] = o.astype(r.dtype)

    return pl.pallas_call(
        body, out_shape=[jax.ShapeDtypeStruct((rows, width), dt) for dt in out_dtypes],
        grid=(rows // tm, width // bw), in_specs=in_specs,
        out_specs=[pl.BlockSpec((tm, bw), lambda i, j: (i, j)) for _ in out_dtypes],
        name=name, compiler_params=_params("parallel", "parallel"))(*[a for a, _ in ins], *vecs)


def _sigmoid(x):
    return 1.0 / (1.0 + jnp.exp(-x))


def _seg_sum(v):
    outs = []
    for k in range(v.shape[1] // LANES):
        vp = v[:, k * LANES:(k + 1) * LANES]
        left = lax.broadcasted_iota(jnp.int32, vp.shape, 1) < HEAD_DIM
        sl = jnp.sum(jnp.where(left, vp, 0.0), axis=-1, keepdims=True)
        sr = jnp.sum(jnp.where(left, 0.0, vp), axis=-1, keepdims=True)
        outs.append(jnp.where(left, sl, sr))
    return outs[0] if len(outs) == 1 else jnp.concatenate(outs, axis=1)


def _seg_rstd(x):
    return lax.rsqrt(_seg_sum(x * x) * (1.0 / HEAD_DIM) + RMS_EPS)


def _rms_fwd(x, g, name):
    rows, d = x.shape
    tm = 256

    def body(x_ref, g_ref, h_ref):
        xv = x_ref[...]
        r = lax.rsqrt(jnp.mean(xv * xv, axis=-1, keepdims=True) + RMS_EPS)
        h_ref[...] = ((xv * r) * g_ref[...]).astype(BF16)

    return pl.pallas_call(
        body, out_shape=jax.ShapeDtypeStruct((rows, d), BF16), grid=(rows // tm,),
        in_specs=[pl.BlockSpec((tm, d), lambda i: (i, 0)), pl.BlockSpec((1, d), lambda i: (0, 0))],
        out_specs=pl.BlockSpec((tm, d), lambda i: (i, 0)), name=name, compiler_params=_params("parallel"))(x, g)


def _rms_bwd(x, g, dh, dres, name):
    rows, d = x.shape
    tm = 256

    def body(x_ref, g_ref, dh_ref, dres_ref, dx_ref, dxb_ref, dg_ref):
        xv = x_ref[...]
        r = lax.rsqrt(jnp.mean(xv * xv, axis=-1, keepdims=True) + RMS_EPS)
        xh = xv * r
        dhv = dh_ref[...]
        dxh = dhv * g_ref[...]
        dxv = dres_ref[...] + r * (dxh - xh * jnp.mean(dxh * xh, axis=-1, keepdims=True))
        dx_ref[...] = dxv
        dxb_ref[...] = dxv.astype(BF16)
        part = jnp.sum(dhv * xh, axis=0, keepdims=True)

        @pl.when(pl.program_id(0) == 0)
        def _():
            dg_ref[...] = part

        @pl.when(pl.program_id(0) > 0)
        def _():
            dg_ref[...] += part

    row = pl.BlockSpec((tm, d), lambda i: (i, 0))
    vec = pl.BlockSpec((1, d), lambda i: (0, 0))
    return pl.pallas_call(
        body, out_shape=[jax.ShapeDtypeStruct((rows, d), F32), jax.ShapeDtypeStruct((rows, d), BF16),
                         jax.ShapeDtypeStruct((1, d), F32)],
        grid=(rows // tm,), in_specs=[row, vec, row, row], out_specs=[row, row, vec],
        name=name, compiler_params=_params("arbitrary"))(x, g, dh, dres)


def _loss_grad(y, t):
    rows, d = y.shape
    tm = 256

    def body(y_ref, t_ref, dy_ref, l_ref):
        e = y_ref[...] - t_ref[...]
        dy_ref[...] = e * (1.0 / d)
        part = jnp.zeros((1, LANES), F32) + jnp.sum(e * e) * (0.5 / d)

        @pl.when(pl.program_id(0) == 0)
        def _():
            l_ref[...] = part

        @pl.when(pl.program_id(0) > 0)
        def _():
            l_ref[...] += part

    row = pl.BlockSpec((tm, d), lambda i: (i, 0))
    return pl.pallas_call(
        body, out_shape=[jax.ShapeDtypeStruct((rows, d), F32), jax.ShapeDtypeStruct((1, LANES), F32)],
        grid=(rows // tm,), in_specs=[row, row], out_specs=[row, pl.BlockSpec((1, LANES), lambda i: (0, 0))],
        name="loss_grad", compiler_params=_params("arbitrary"))(y, t)


def _put_pairs(ref, val):
    for hp in range(N_PAIRS):
        ref[hp] = val[:, hp * LANES:(hp + 1) * LANES].astype(ref.dtype)


def _get_pairs(ref):
    return jnp.concatenate([ref[hp] for hp in range(N_PAIRS)], axis=1)


def _swap_halves(v):
    return pltpu.roll(v, HEAD_DIM, axis=1)


def _expand_kv(kv):
    left = lax.broadcasted_iota(jnp.int32, kv.shape, 1) < HEAD_DIM
    sw = _swap_halves(kv)
    h0 = jnp.where(left, kv, sw)
    h1 = jnp.where(left, sw, kv)
    return jnp.concatenate([h0, h0, h1, h1], axis=1)


def _reduce_kv(dkv):
    left = lax.broadcasted_iota(jnp.int32, (dkv.shape[0], LANES), 1) < HEAD_DIM
    t = dkv[:, 0:LANES] + dkv[:, LANES:2 * LANES]
    u = dkv[:, 2 * LANES:3 * LANES] + dkv[:, 3 * LANES:4 * LANES]
    t = t + _swap_halves(t)
    u = u + _swap_halves(u)
    return jnp.where(left, t, u)


def _qknorm_fwd(proj, gqa, gka, gqb, gkb):
    rows = proj.shape[0]
    tm = 256

    def body(qa_ref, ka_ref, va_ref, qb_ref, kb_ref, vb_ref, gqa_ref, gka_ref, gqb_ref, gkb_ref,
             oqa, oka, ova, oqb, okb, ovb):
        for src, g_ref, dst in ((qa_ref, gqa_ref, oqa), (ka_ref, gka_ref, oka), (qb_ref, gqb_ref, oqb)):
            xv = src[...]
            _put_pairs(dst, (xv * _seg_rstd(xv)) * g_ref[...])
        _put_pairs(ova, va_ref[...])
        kv = kb_ref[...]
        _put_pairs(okb, _expand_kv((kv * _seg_rstd(kv)) * gkb_ref[...]))
        _put_pairs(ovb, _expand_kv(vb_ref[...]))

    def win(width, off):
        return pl.BlockSpec((tm, width), lambda i: (i, off // width))

    vec = lambda w: pl.BlockSpec((1, w), lambda i: (0, 0))
    out = pl.BlockSpec((N_PAIRS, tm, LANES), lambda i: (0, i, 0))
    return pl.pallas_call(
        body, out_shape=[jax.ShapeDtypeStruct((N_PAIRS, rows, LANES), F32)] * 6, grid=(rows // tm,),
        in_specs=[win(WIDTH, OFF_QA), win(WIDTH, OFF_KA), win(WIDTH, OFF_VA), win(WIDTH, OFF_QB),
                  win(LANES, OFF_KB), win(LANES, OFF_VB), vec(WIDTH), vec(WIDTH), vec(WIDTH), vec(LANES)],
        out_specs=[out] * 6, name="qknorm_fwd", compiler_params=_params("parallel"))(
            proj, proj, proj, proj, proj, proj, gqa, gka, gqb, gkb)


def _norm_bwd(xv, g, dy):
    r = _seg_rstd(xv)
    xh = xv * r
    dxh = dy * g
    dx = r * (dxh - xh * (_seg_sum(dxh * xh) * (1.0 / HEAD_DIM)))
    return dx, jnp.sum(dy * xh, axis=0, keepdims=True)


def _qknorm_bwd(proj, gqa, gka, gqb, gkb, dqa, dka, dva, dqb, dkb, dvb, dga, dgb):
    rows = proj.shape[0]
    tm = 256
    n_a = len(dqa)

    def body(*refs):
        qa_ref, ka_ref, qb_ref, kb_ref, gqa_ref, gka_ref, gqb_ref, gkb_ref = refs[:8]
        pos = 8
        dqa_refs, dka_refs, dva_refs = refs[pos:pos + n_a], refs[pos + n_a:pos + 2 * n_a], refs[pos + 2 * n_a:pos + 3 * n_a]
        pos += 3 * n_a
        dqb_ref, dkb_ref, dvb_ref, dga_ref, dgb_ref = refs[pos:pos + 5]
        dproj_ref, ogqa, ogka, ogqb, ogkb = refs[pos + 5:]

        def total(rs):
            acc = _get_pairs(rs[0])
            for r in rs[1:]:
                acc = acc + _get_pairs(r)
            return acc

        dx_qa, p_qa = _norm_bwd(qa_ref[...], gqa_ref[...], total(dqa_refs))
        dx_ka, p_ka = _norm_bwd(ka_ref[...], gka_ref[...], total(dka_refs))
        dx_qb, p_qb = _norm_bwd(qb_ref[...], gqb_ref[...], _get_pairs(dqb_ref))
        dx_kb, p_kb = _norm_bwd(kb_ref[...], gkb_ref[...], _reduce_kv(_get_pairs(dkb_ref)))
        dproj_ref[:, OFF_QA:OFF_QA + WIDTH] = dx_qa.astype(BF16)
        dproj_ref[:, OFF_KA:OFF_KA + WIDTH] = dx_ka.astype(BF16)
        dproj_ref[:, OFF_VA:OFF_VA + WIDTH] = total(dva_refs).astype(BF16)
        dproj_ref[:, OFF_QB:OFF_QB + WIDTH] = dx_qb.astype(BF16)
        dproj_ref[:, OFF_KB:OFF_KB + LANES] = dx_kb.astype(BF16)
        dproj_ref[:, OFF_VB:OFF_VB + LANES] = _reduce_kv(_get_pairs(dvb_ref)).astype(BF16)
        dproj_ref[:, OFF_GA:OFF_GB] = dga_ref[...]
        dproj_ref[:, OFF_GB:D_IN] = dgb_ref[...]
        first = pl.program_id(0) == 0
        for o_ref, part in ((ogqa, p_qa), (ogka, p_ka), (ogqb, p_qb), (ogkb, p_kb)):
            @pl.when(first)
            def _(o_ref=o_ref, part=part):
                o_ref[...] = part

            @pl.when(jnp.logical_not(first))
            def _(o_ref=o_ref, part=part):
                o_ref[...] += part

    def win(width, off):
        return pl.BlockSpec((tm, width), lambda i: (i, off // width))

    vec = lambda w: pl.BlockSpec((1, w), lambda i: (0, 0))
    row = lambda w: pl.BlockSpec((tm, w), lambda i: (i, 0))
    in_specs = [win(WIDTH, OFF_QA), win(WIDTH, OFF_KA), win(WIDTH, OFF_QB), win(LANES, OFF_KB),
                vec(WIDTH), vec(WIDTH), vec(WIDTH), vec(LANES)]
    in_specs += [pl.BlockSpec((N_PAIRS, tm, LANES), lambda i: (0, i, 0))] * (3 * n_a + 3) + [row(D_MODEL)] * 2
    return pl.pallas_call(
        body,
        out_shape=[jax.ShapeDtypeStruct((rows, D_IN), BF16), jax.ShapeDtypeStruct((1, WIDTH), F32),
                   jax.ShapeDtypeStruct((1, WIDTH), F32), jax.ShapeDtypeStruct((1, WIDTH), F32),
                   jax.ShapeDtypeStruct((1, LANES), F32)],
        grid=(rows // tm,), in_specs=in_specs,
        out_specs=[row(D_IN), vec(WIDTH), vec(WIDTH), vec(WIDTH), vec(LANES)],
        name="qknorm_bwd", compiler_params=_params("arbitrary"))(
            proj, proj, proj, proj, gqa, gka, gqb, gkb, *dqa, *dka, *dva, dqb, dkb, dvb, dga, dgb)


def _t5_bucket(rel):
    half_b = NUM_BUCKETS // 2
    max_exact = half_b // 2
    sign = jnp.where(rel > 0, half_b, 0)
    n = jnp.abs(rel)
    nf = jnp.maximum(n, 1).astype(F32)
    large = max_exact + (jnp.log(nf / max_exact) / math.log(MAX_DISTANCE / max_exact)
                         * (half_b - max_exact)).astype(jnp.int32)
    large = jnp.minimum(large, half_b - 1)
    return sign + jnp.where(n < max_exact, n, large)


def _band_buckets(blk, dilation):
    i = jnp.arange(blk, dtype=jnp.int32)[:, None]
    j = jnp.arange(3 * blk, dtype=jnp.int32)[None, :]
    rel = j - blk - i
    return jnp.where(jnp.abs(rel) <= blk, _t5_bucket(rel * dilation), -1)


def _bias_tiles(table, buckets, head_off, name):
    blk = buckets.shape[0]

    def body(tab_ref, bk_ref, o_ref):
        h = pl.program_id(0) + head_off
        bk = bk_ref[...]
        acc = jnp.full(bk.shape, NEG_INF, F32)
        for b in range(NUM_BUCKETS):
            acc = jnp.where(bk == b, tab_ref[b, h], acc)
        o_ref[0] = acc

    return pl.pallas_call(
        body, out_shape=jax.ShapeDtypeStruct((N_HEADS, blk, 3 * blk), F32), grid=(N_HEADS,),
        in_specs=[pl.BlockSpec(memory_space=pltpu.SMEM), pl.BlockSpec((blk, 3 * blk), lambda h: (0, 0))],
        out_specs=pl.BlockSpec((1, blk, 3 * blk), lambda h: (h, 0, 0)),
        name=name, compiler_params=_params("parallel"))(table, buckets)


def _table_grad(dbias, buckets, name):
    blk = buckets.shape[0]

    def body(db_ref, bk_ref, o_ref):
        bk = bk_ref[...]
        dbv = db_ref[0]
        lane = lax.broadcasted_iota(jnp.int32, (1, LANES), 1)
        acc = jnp.zeros((1, LANES), F32)
        for b in range(NUM_BUCKETS):
            acc = jnp.where(lane == b, jnp.sum(jnp.where(bk == b, dbv, 0.0)), acc)
        o_ref[0] = acc

    out = pl.pallas_call(
        body, out_shape=jax.ShapeDtypeStruct((N_HEADS, 1, LANES), F32), grid=(N_HEADS,),
        in_specs=[pl.BlockSpec((1, blk, 3 * blk), lambda h: (h, 0, 0)), pl.BlockSpec((blk, 3 * blk), lambda h: (0, 0))],
        out_specs=pl.BlockSpec((1, 1, LANES), lambda h: (h, 0, 0)),
        name=name, compiler_params=_params("parallel"))(dbias, buckets)
    return out[:, 0, :NUM_BUCKETS]


def _dot_nt(a, b):
    return lax.dot_general(a, b, (((1,), (1,)), ((), ())), preferred_element_type=F32)


def _stack_pair(x2, left):
    return jnp.concatenate([jnp.where(left, x2, 0.0), jnp.where(left, 0.0, x2)], axis=0).astype(BF16)


def _attn_geometry(blk, d):
    chunk = blk * ITEMS if d == 1 else blk * d
    groups = 1 if d == 1 else d // ITEMS
    halo = blk if d == 1 else chunk
    return chunk, groups, halo


def _item_rows(ref, j, r0, blk, d):
    if d == 1:
        return ref[j * blk:(j + 1) * blk, :]
    return ref[pl.ds(r0 + j, blk, stride=d), :]


def _item_penalty(t, nct, j, blk, d):
    first_ok, last_ok = t > 0, t < nct - 1
    if d == 1:
        first_ok = True if j > 0 else first_ok
        last_ok = True if j < ITEMS - 1 else last_ok
    col = lax.broadcasted_iota(jnp.int32, (1, 3 * blk), 1)
    ok = jnp.logical_and(jnp.logical_or(col >= blk, first_ok), jnp.logical_or(col < 2 * blk, last_ok))
    return jnp.where(ok, 0.0, NEG_INF).astype(F32)


def _attn_specs(seq, blk, d, step_of):
    chunk, _, halo = _attn_geometry(blk, d)
    per, last = chunk // halo, seq // halo - 1
    cur = pl.BlockSpec((None, chunk, LANES), lambda hp, t: (hp, step_of(t), 0))
    prev = pl.BlockSpec((None, halo, LANES), lambda hp, t: (hp, jnp.clip(step_of(t) * per - 1, 0, last), 0))
    nxt = pl.BlockSpec((None, halo, LANES), lambda hp, t: (hp, jnp.minimum((step_of(t) + 1) * per, last), 0))
    return cur, prev, nxt


def _attn_fwd(q, k, v, bias, sink, blk, d, name):
    _, seq, _ = q.shape
    chunk, groups, _ = _attn_geometry(blk, d)
    nct = seq // chunk
    has_sink = sink is not None
    scale = HEAD_DIM ** -0.5

    def body(*refs):
        q_ref, kp, kc, kn, vp, vc, vn, b_ref = refs[:8]
        s_ref = refs[8] if has_sink else None
        o_ref, l_ref = refs[-2], refs[-1]
        t = pl.program_id(1)
        left = lax.broadcasted_iota(jnp.int32, (1, LANES), 1) < HEAD_DIM
        bias2 = b_ref[...]
        if d == 1:
            kwin = jnp.concatenate([kp[...], kc[...], kn[...]], axis=0).astype(BF16)
            vwin = jnp.concatenate([vp[...], vc[...], vn[...]], axis=0).astype(BF16)

        def group(r0):
            scores, vcats = [], []
            for j in range(ITEMS):
                qs = _stack_pair(_item_rows(q_ref, j, r0, blk, d) * scale, left)
                if d == 1:
                    kcat, vcat = kwin[j * blk:(j + 3) * blk], vwin[j * blk:(j + 3) * blk]
                else:
                    kcat = jnp.concatenate([_item_rows(r, j, r0, blk, d) for r in (kp, kc, kn)], axis=0).astype(BF16)
                    vcat = jnp.concatenate([_item_rows(r, j, r0, blk, d) for r in (vp, vc, vn)], axis=0).astype(BF16)
                scores.append(_dot_nt(qs, kcat) + bias2 + _item_penalty(t, nct, j, blk, d))
                vcats.append(vcat)
            s = jnp.concatenate(scores, axis=0)
            m = jnp.max(s, axis=-1, keepdims=True)
            if has_sink:
                sk = jnp.concatenate([s_ref[...]] * ITEMS, axis=0)
                m = jnp.maximum(m, sk)
            p = jnp.exp(s - m)
            den = jnp.sum(p, axis=-1, keepdims=True)
            if has_sink:
                den = den + jnp.exp(sk - m)
            pn = (p * (1.0 / den)).astype(BF16)
            lse = m + jnp.log(den)
            for j in range(ITEMS):
                top, mid, bot = 2 * j * blk, (2 * j + 1) * blk, (2 * j + 2) * blk
                o2 = jnp.dot(pn[top:bot], vcats[j], preferred_element_type=F32)
                o_val = jnp.where(left, o2[:blk], o2[blk:])
                l_val = jnp.where(left, lse[top:mid], lse[mid:bot])
                if d == 1:
                    o_ref[j * blk:(j + 1) * blk, :] = o_val
                    l_ref[j * blk:(j + 1) * blk, :] = l_val
                else:
                    o_ref[pl.ds(r0 + j, blk, stride=d), :] = o_val
                    l_ref[pl.ds(r0 + j, blk, stride=d), :] = l_val

        if groups == 1:
            group(0)
        else:
            def step(g, carry):
                group(g * ITEMS)
                return carry

            lax.fori_loop(0, groups, step, 0)

    cur, prev, nxt = _attn_specs(seq, blk, d, lambda t: t)
    in_specs = [cur, prev, cur, nxt, prev, cur, nxt, pl.BlockSpec((2 * blk, 3 * blk), lambda hp, t: (hp, 0))]
    args = [q, k, k, k, v, v, v, bias]
    if has_sink:
        in_specs.append(pl.BlockSpec((2 * blk, 1), lambda hp, t: (hp, 0)))
        args.append(sink)
    return pl.pallas_call(
        body, out_shape=[jax.ShapeDtypeStruct(q.shape, F32)] * 2, grid=(N_PAIRS, nct),
        in_specs=in_specs, out_specs=[cur, cur], name=name, compiler_params=_params("parallel", "parallel"))(*args)


def _attn_bwd(q, k, v, do, lse, delta, bias, sink, blk, d, name):
    _, seq, _ = q.shape
    chunk, groups, halo = _attn_geometry(blk, d)
    nct = seq // chunk
    has_sink = sink is not None
    n_in = 12 if has_sink else 11
    scale = HEAD_DIM ** -0.5

    def body(*refs):
        q_ref, kp, kc, kn, vp, vc, vn, do_ref, l_ref, d_ref, b_ref = refs[:11]
        s_ref = refs[11] if has_sink else None
        dq_ref, dk_ref, dv_ref, db_ref = refs[n_in:n_in + 4]
        ds_ref = refs[n_in + 4] if has_sink else None
        wk, wv = refs[-2], refs[-1]
        t = pl.program_id(1)

        @pl.when(t == 0)
        def _():
            wk[...] = jnp.zeros_like(wk)
            wv[...] = jnp.zeros_like(wv)
            db_ref[...] = jnp.zeros_like(db_ref)
            if has_sink:
                ds_ref[...] = jnp.zeros_like(ds_ref)

        @pl.when(t > 0)
        def _():
            for w in (wk, wv):
                keep = w[chunk:2 * chunk + halo]
                w[0:chunk + halo] = keep
                w[chunk + halo:2 * chunk + halo] = jnp.zeros((chunk, LANES), F32)

        @pl.when(t < nct)
        def _():
            lane = lax.broadcasted_iota(jnp.int32, (1, LANES), 1)
            left = lane < HEAD_DIM
            bias2 = b_ref[...]
            if d == 1:
                kwin = jnp.concatenate([kp[...], kc[...], kn[...]], axis=0).astype(BF16)
                vwin = jnp.concatenate([vp[...], vc[...], vn[...]], axis=0).astype(BF16)

            def group(r0):
                qss, doss, kcats, scores, dps, lcols, dcols = [], [], [], [], [], [], []
                for j in range(ITEMS):
                    qs = _stack_pair(_item_rows(q_ref, j, r0, blk, d) * scale, left)
                    dos = _stack_pair(_item_rows(do_ref, j, r0, blk, d), left)
                    if d == 1:
                        kcat, vcat = kwin[j * blk:(j + 3) * blk], vwin[j * blk:(j + 3) * blk]
                    else:
                        kcat = jnp.concatenate([_item_rows(r, j, r0, blk, d) for r in (kp, kc, kn)], axis=0).astype(BF16)
                        vcat = jnp.concatenate([_item_rows(r, j, r0, blk, d) for r in (vp, vc, vn)], axis=0).astype(BF16)
                    l2, d2 = _item_rows(l_ref, j, r0, blk, d), _item_rows(d_ref, j, r0, blk, d)
                    lcols.append(jnp.max(jnp.where(left, l2, NEG_INF), axis=-1, keepdims=True))
                    lcols.append(jnp.max(jnp.where(left, NEG_INF, l2), axis=-1, keepdims=True))
                    dcols.append(jnp.sum(jnp.where(lane == 0, d2, 0.0), axis=-1, keepdims=True))
                    dcols.append(jnp.sum(jnp.where(lane == HEAD_DIM, d2, 0.0), axis=-1, keepdims=True))
                    scores.append(_dot_nt(qs, kcat) + bias2 + _item_penalty(t, nct, j, blk, d))
                    dps.append(_dot_nt(dos, vcat))
                    qss.append(qs)
                    doss.append(dos)
                    kcats.append(kcat)
                lcol = jnp.concatenate(lcols, axis=0)
                dcol = jnp.concatenate(dcols, axis=0)
                p = jnp.exp(jnp.concatenate(scores, axis=0) - lcol)
                ds = p * (jnp.concatenate(dps, axis=0) - dcol)
                if has_sink:
                    sgrad = dcol * jnp.exp(jnp.concatenate([s_ref[...]] * ITEMS, axis=0) - lcol)
                for j in range(ITEMS):
                    top, bot = 2 * j * blk, (2 * j + 2) * blk
                    dsj, pj = ds[top:bot], p[top:bot]
                    db_ref[...] += dsj
                    if has_sink:
                        ds_ref[...] -= sgrad[top:bot]
                    dq2 = jnp.dot(dsj.astype(BF16), kcats[j], preferred_element_type=F32) * scale
                    dq_val = jnp.where(left, dq2[:blk], dq2[blk:])
                    if d == 1:
                        dq_ref[j * blk:(j + 1) * blk, :] = dq_val
                    else:
                        dq_ref[pl.ds(r0 + j, blk, stride=d), :] = dq_val
                    dk_new = jnp.dot(jnp.transpose(dsj).astype(BF16), qss[j], preferred_element_type=F32)
                    dv_new = jnp.dot(jnp.transpose(pj).astype(BF16), doss[j], preferred_element_type=F32)
                    for w, new in ((wk, dk_new), (wv, dv_new)):
                        if d == 1:
                            w[chunk + (j - 1) * blk:chunk + (j + 2) * blk, :] += new
                        else:
                            for c in range(3):
                                w[pl.ds(c * chunk + r0 + j, blk, stride=d), :] += new[c * blk:(c + 1) * blk]

            if groups == 1:
                group(0)
            else:
                def step(g, carry):
                    group(g * ITEMS)
                    return carry

                lax.fori_loop(0, groups, step, 0)

        dk_ref[...] = wk[0:chunk]
        dv_ref[...] = wv[0:chunk]

    cur, prev, nxt = _attn_specs(seq, blk, d, lambda t: jnp.minimum(t, nct - 1))
    lag = pl.BlockSpec((None, chunk, LANES), lambda hp, t: (hp, jnp.maximum(t - 1, 0), 0))
    band = pl.BlockSpec((2 * blk, 3 * blk), lambda hp, t: (hp, 0))
    col = pl.BlockSpec((2 * blk, 1), lambda hp, t: (hp, 0))
    in_specs = [cur, prev, cur, nxt, prev, cur, nxt, cur, cur, cur, band]
    args = [q, k, k, k, v, v, v, do, lse, delta, bias]
    out_shape = [jax.ShapeDtypeStruct(q.shape, F32)] * 3 + [jax.ShapeDtypeStruct((N_HEADS * blk, 3 * blk), F32)]
    out_specs = [cur, lag, lag, band]
    if has_sink:
        in_specs.append(col)
        args.append(sink)
        out_shape.append(jax.ShapeDtypeStruct((N_HEADS * blk, 1), F32))
        out_specs.append(col)
    window = pltpu.VMEM((2 * chunk + halo, LANES), F32)
    return pl.pallas_call(
        body, out_shape=out_shape, grid=(N_PAIRS, nct + 1), in_specs=in_specs, out_specs=out_specs,
        scratch_shapes=[window, window], name=name, compiler_params=_params("arbitrary", "arbitrary"))(*args)


def _combine_patterns(outs, lses):
    _, rows, _ = outs[0].shape
    tm = 256
    n = len(outs)

    def body(*refs):
        o_refs, l_refs = refs[:n], refs[n:2 * n]
        y_ref, lse_ref = refs[2 * n], refs[2 * n + 1]
        for hp in range(N_PAIRS):
            ls = [r[hp] for r in l_refs]
            m = functools.reduce(jnp.maximum, ls)
            es = [jnp.exp(l - m) for l in ls]
            den = functools.reduce(lambda a, b: a + b, es)
            num = functools.reduce(lambda a, b: a + b, [e * r[hp] for e, r in zip(es, o_refs)])
            y_ref[:, hp * LANES:(hp + 1) * LANES] = num / den
            lse_ref[hp] = m + jnp.log(den)

    pm = pl.BlockSpec((N_PAIRS, tm, LANES), lambda i: (0, i, 0))
    return pl.pallas_call(
        body, out_shape=[jax.ShapeDtypeStruct((rows, WIDTH), F32), jax.ShapeDtypeStruct((N_PAIRS, rows, LANES), F32)],
        grid=(rows // tm,), in_specs=[pm] * (2 * n), out_specs=[pl.BlockSpec((tm, WIDTH), lambda i: (i, 0)), pm],
        name="combine_a", compiler_params=_params("parallel"))(*outs, *lses)


def _pairs_to_tokens(a):
    _, rows, _ = a.shape
    tm = 256

    def body(a_ref, o_ref):
        o_ref[...] = _get_pairs(a_ref)

    return pl.pallas_call(
        body, out_shape=jax.ShapeDtypeStruct((rows, WIDTH), a.dtype), grid=(rows // tm,),
        in_specs=[pl.BlockSpec((N_PAIRS, tm, LANES), lambda i: (0, i, 0))],
        out_specs=pl.BlockSpec((tm, WIDTH), lambda i: (i, 0)), name="pairs_to_tokens",
        compiler_params=_params("parallel"))(a)


def _attn_bwd_prep(dy, y, name):
    rows = dy.shape[0]
    tm = 256

    def body(dy_ref, y_ref, do_ref, dl_ref):
        dyv = dy_ref[...]
        _put_pairs(do_ref, dyv)
        _put_pairs(dl_ref, _seg_sum(dyv * y_ref[...]))

    tok = pl.BlockSpec((tm, WIDTH), lambda i: (i, 0))
    pm = pl.BlockSpec((N_PAIRS, tm, LANES), lambda i: (0, i, 0))
    return pl.pallas_call(
        body, out_shape=[jax.ShapeDtypeStruct((N_PAIRS, rows, LANES), F32)] * 2, grid=(rows // tm,),
        in_specs=[tok, tok], out_specs=[pm, pm], name=name, compiler_params=_params("parallel"))(dy, y)


def _tile_gain(g, reps):
    return jnp.tile(g[None, :], (1, reps))


def _local_step(x, p, target, w_in_of, rest_of, small):
    rel_table = small["rel_table"]
    buckets_a = [_band_buckets(blk, d) for blk, d in DILATED]
    buckets_b = _band_buckets(BLK_B, 1)
    bias_a = [_bias_tiles(rel_table, bk, 0, "bias_a").reshape(N_HEADS * bk.shape[0], -1) for bk in buckets_a]
    bias_b = _bias_tiles(rel_table, buckets_b, N_HEADS, "bias_b").reshape(N_HEADS * BLK_B, -1)

    saved = []
    for l in range(DEPTH):
        g_mix, g_ffn, g_ple = (small[n][l][None, :] for n in ("norm_mix_g", "norm_ffn_g", "norm_ple_g"))
        gqa, gka, gqb = (_tile_gain(small[n][l], N_HEADS) for n in ("qnorm_a_g", "knorm_a_g", "qnorm_b_g"))
        gkb = _tile_gain(small["knorm_b_g"][l], N_KV_B)
        sink = jnp.repeat(small["sink_b"][l], BLK_B)[:, None]

        h = _rms_fwd(x, g_mix, "rms_mix")
        w_in = w_in_of(l, h)
        proj = _mm(h, w_in, "nt", F32, "mm_in")
        qa, ka, va, qb, kb, vb = _qknorm_fwd(proj, gqa, gka, gqb, gkb)
        outs, lses = [], []
        for (blk, d), bias in zip(DILATED, bias_a):
            o, ls = _attn_fwd(qa, ka, va, bias, None, blk, d, f"attn_a{d}_fwd")
            outs.append(o)
            lses.append(ls)
        ya, lse_a = _combine_patterns(outs, lses)
        yb, lse_b = _attn_fwd(qb, kb, vb, bias_b, sink, BLK_B, 1, "attn_b_fwd")
        yb = _pairs_to_tokens(yb)
        w = dict(rest_of(l, yb), w_in=w_in)
        def gate(products, extra):
            (ca_, cb_), (ga_, gb_) = products, extra
            return _sigmoid(ga_) * ca_ + _sigmoid(gb_) * cb_, ca_, cb_

        merged, ca, cb = _mm_fused([(ya, w["w_branch_a"], "nn"), (yb, w["w_branch_b"], "nn")],
                                   [(proj, OFF_GA), (proj, OFF_GB)], gate, [BF16, BF16, BF16], "mm_branches_gate")
        x1 = _mm(merged, w["w_out"], "nn", F32, "mm_out", res=x)

        h2 = _rms_fwd(x1, g_ffn, "rms_ffn")

        def swiglu(products, extra):
            a_, u_ = products
            return (a_ * _sigmoid(a_)) * u_, a_, u_

        hid, a, u = _mm_fused([(h2, w["w_ffn_gate"], "nt"), (h2, w["w_ffn_up"], "nt")], [], swiglu,
                              [BF16, BF16, BF16], "mm_ffn_gate_up")
        x2 = _mm(hid, w["w_ffn_down"], "nn", F32, "mm_ffn_down", res=x1)

        h3 = _rms_fwd(x2, g_ple, "rms_ple")

        def ple(products, extra):
            z_, e_ = products
            return extra[0] + _sigmoid(z_) * e_, z_, e_

        x3, z, e = _mm_fused([(h3, w["w_ple_gate"], "nn"), (p[l], w["w_ple_proj"], "nn")], [(x2, 0)], ple,
                             [F32, BF16, BF16], "mm_ple")
        saved.append(dict(w=w, x0=x, h=h, proj=proj, qa=qa, ka=ka, va=va, qb=qb, kb=kb, vb=vb, ya=ya, lse_a=lse_a,
                          yb=yb, lse_b=lse_b, ca=ca, cb=cb, merged=merged, x1=x1, h2=h2, a=a, u=u, hid=hid,
                          x2=x2, h3=h3, z=z, e=e))
        x = x3

    dx, loss_acc = _loss_grad(x, target)
    loss = loss_acc[0, 0]

    gbig = [{} for _ in range(DEPTH)]
    marks = [{} for _ in range(DEPTH)]
    gsmall = {n: [None] * DEPTH for n in SMALL if n != "rel_table"}
    dtable_a = jnp.zeros((N_HEADS, NUM_BUCKETS), F32)
    dtable_b = jnp.zeros((N_HEADS, NUM_BUCKETS), F32)

    for l in reversed(range(DEPTH)):
        sv = saved[l]
        w = sv["w"]
        g_mix, g_ffn, g_ple = (small[n][l][None, :] for n in ("norm_mix_g", "norm_ffn_g", "norm_ple_g"))
        gqa, gka, gqb = (_tile_gain(small[n][l], N_HEADS) for n in ("qnorm_a_g", "knorm_a_g", "qnorm_b_g"))
        gkb = _tile_gain(small["knorm_b_g"][l], N_KV_B)
        sink = jnp.repeat(small["sink_b"][l], BLK_B)[:, None]

        def ple_bwd(dx_, z_, e_):
            s = _sigmoid(z_.astype(F32))
            return dx_ * s, dx_ * e_.astype(F32) * (s * (1.0 - s))

        de, dz = _ew(ple_bwd, [(dx, 0), (sv["z"], 0), (sv["e"], 0)], [BF16, BF16], width=D_MODEL, bw=D_MODEL,
                     name="ple_bwd")
        gbig[l]["w_ple_proj"] = _mm(p[l], de, "tn", F32, "mm_d_ple_proj")
        gbig[l]["w_ple_gate"] = _mm(sv["h3"], dz, "tn", F32, "mm_d_ple_gate")
        dh3 = _mm(dz, w["w_ple_gate"], "nt", F32, "mm_dh3")
        dx, dxb, gsmall["norm_ple_g"][l] = _rms_bwd(sv["x2"], g_ple, dh3, dx, "rms_ple_bwd")

        gbig[l]["w_ffn_down"] = _mm(sv["hid"], dxb, "tn", F32, "mm_d_ffn_down")

        def swiglu_bwd(products, extra):
            dh_, a_, u_ = products[0], extra[0].astype(F32), extra[1].astype(F32)
            s = _sigmoid(a_)
            return dh_ * u_ * (s * (1.0 + a_ * (1.0 - s))), dh_ * (a_ * s)

        da, du = _mm_fused([(dxb, w["w_ffn_down"], "nt")], [(sv["a"], 0), (sv["u"], 0)], swiglu_bwd, [BF16, BF16],
                           "mm_dhid_swiglu_bwd")
        gbig[l]["w_ffn_gate"] = _mm(da, sv["h2"], "tn", F32, "mm_d_ffn_gate")
        gbig[l]["w_ffn_up"] = _mm(du, sv["h2"], "tn", F32, "mm_d_ffn_up")
        dh2 = _mm(da, w["w_ffn_gate"], "nn", F32, "mm_dh2_gate")
        dh2 = _mm(du, w["w_ffn_up"], "nn", F32, "mm_dh2_up", res=dh2)
        dx, dxb, gsmall["norm_ffn_g"][l] = _rms_bwd(sv["x1"], g_ffn, dh2, dx, "rms_ffn_bwd")

        gbig[l]["w_out"] = _mm(sv["merged"], dxb, "tn", F32, "mm_d_out")

        def gate_bwd(products, extra):
            dm_, ca_, cb_ = products[0], extra[0].astype(F32), extra[1].astype(F32)
            sa, sb = _sigmoid(extra[2]), _sigmoid(extra[3])
            return dm_ * sa, dm_ * sb, dm_ * ca_ * (sa * (1.0 - sa)), dm_ * cb_ * (sb * (1.0 - sb))

        dca, dcb, dga, dgb = _mm_fused(
            [(dxb, w["w_out"], "nt")], [(sv["ca"], 0), (sv["cb"], 0), (sv["proj"], OFF_GA), (sv["proj"], OFF_GB)],
            gate_bwd, [BF16, BF16, BF16, BF16], "mm_dmerged_gate_bwd")
        gbig[l]["w_branch_a"] = _mm(sv["ya"], dca, "tn", F32, "mm_d_branch_a")
        gbig[l]["w_branch_b"] = _mm(sv["yb"], dcb, "tn", F32, "mm_d_branch_b")
        dya = _mm(dca, w["w_branch_a"], "nt", F32, "mm_dya")
        dyb = _mm(dcb, w["w_branch_b"], "nt", F32, "mm_dyb")

        dya, delta_a = _attn_bwd_prep(dya, sv["ya"], "attn_a_bwd_prep")
        dyb, delta_b = _attn_bwd_prep(dyb, sv["yb"], "attn_b_bwd_prep")

        dqa, dka, dva = [], [], []
        for (blk, d), bias, bk in zip(DILATED, bias_a, buckets_a):
            dq_, dk_, dv_, db_ = _attn_bwd(sv["qa"], sv["ka"], sv["va"], dya, sv["lse_a"], delta_a, bias, None, blk, d,
                                           f"attn_a{d}_bwd")
            dqa.append(dq_)
            dka.append(dk_)
            dva.append(dv_)
            dtable_a = dtable_a + _table_grad(db_.reshape(N_HEADS, blk, 3 * blk), bk, "table_grad_a")
        dqb, dkb, dvb, db_, dsink = _attn_bwd(sv["qb"], sv["kb"], sv["vb"], dyb, sv["lse_b"], delta_b, bias_b, sink,
                                              BLK_B, 1, "attn_b_bwd")
        dtable_b = dtable_b + _table_grad(db_.reshape(N_HEADS, BLK_B, 3 * BLK_B), buckets_b, "table_grad_b")
        gsmall["sink_b"][l] = dsink.reshape(N_HEADS, BLK_B).sum(axis=1)

        dproj, pqa, pka, pqb, pkb = _qknorm_bwd(sv["proj"], gqa, gka, gqb, gkb, dqa, dka, dva, dqb, dkb, dvb, dga, dgb)
        marks[l]["attn_bwd_done"] = dproj
        gsmall["qnorm_a_g"][l] = pqa.reshape(N_HEADS, HEAD_DIM).sum(0)
        gsmall["knorm_a_g"][l] = pka.reshape(N_HEADS, HEAD_DIM).sum(0)
        gsmall["qnorm_b_g"][l] = pqb.reshape(N_HEADS, HEAD_DIM).sum(0)
        gsmall["knorm_b_g"][l] = pkb.reshape(N_KV_B, HEAD_DIM).sum(0)
        gbig[l]["w_in"] = _mm(dproj, sv["h"], "tn", F32, "mm_d_in")
        dh = _mm(dproj, w["w_in"], "nn", F32, "mm_dh")
        dx, _, gsmall["norm_mix_g"][l] = _rms_bwd(sv["x0"], g_mix, dh, dx, "rms_mix_bwd")
        gsmall["norm_mix_g"][l] = gsmall["norm_mix_g"][l][0]
        gsmall["norm_ffn_g"][l] = gsmall["norm_ffn_g"][l][0]
        gsmall["norm_ple_g"][l] = gsmall["norm_ple_g"][l][0]

    gsmall = {n: jnp.stack(v) for n, v in gsmall.items()}
    gsmall["rel_table"] = jnp.concatenate([dtable_a, dtable_b], axis=0).T
    return loss, dx, gbig, gsmall, marks


def _place():
    return lax.axis_index("x"), lax.axis_index("y"), lax.axis_index("c")


def _flip(v, bit):
    return 1 - v if bit else v


CHIP_RELATIONS = ((0, 1), (1, 0), (1, 1))
ANY = pl.BlockSpec(memory_space=pl.ANY)


def _allgather_body(w_refs, out_refs, send_sems, recv_sems):
    x, y, c = _place()
    chips = [(_flip(x, a), _flip(y, b)) for a, b in CHIP_RELATIONS]

    def make(g):
        w_ref, out_ref = w_refs[g], out_refs[g]
        half = w_ref.shape[0] // 2

        def part(px, py, pc):
            return out_ref.at[2 * px + py, pl.ds(pc * half, half), :]

        def copy(k, block, to, src=None):
            return pltpu.make_async_remote_copy(
                src_ref=part(*block) if src is None else src, dst_ref=part(*block),
                send_sem=send_sems.at[7 * g + k], recv_sem=recv_sems.at[7 * g + k], device_id=to,
                device_id_type=MESH_ID)

        own = pltpu.make_async_remote_copy(
            src_ref=w_ref, dst_ref=out_ref.at[2 * x + y], send_sem=send_sems.at[7 * g + 6],
            recv_sem=recv_sems.at[7 * g + 6], device_id=(x, y, 1 - c), device_id_type=MESH_ID)
        first = [copy(k, (x, y, c), (*chip, c), src=w_ref.at[pl.ds(c * half, half), :]) for k, chip in enumerate(chips)]
        passed = [copy(3 + k, (*chip, c), (x, y, 1 - c)) for k, chip in enumerate(chips)]
        arrive = [copy(k, (*chip, c), (x, y, c)) for k, chip in enumerate(chips)]
        arrive2 = [copy(3 + k, (*chip, 1 - c), (x, y, c)) for k, chip in enumerate(chips)]
        return own, first, passed, arrive, arrive2

    made = [make(g) for g in range(len(w_refs))]
    for own, first, _, _, _ in made:
        own.start()
        for cp in first:
            cp.start()
    for _, _, passed, arrive, _ in made:
        for k in range(3):
            arrive[k].wait_recv()
            passed[k].start()
    for own, first, passed, _, arrive2 in made:
        for k in range(3):
            arrive2[k].wait_recv()
        own.wait_recv()
        for cp in first + passed + [own]:
            cp.wait_send()


def _sibling(x, y, c):
    return [(x, y, 1 - c)]


def _same_core_of_other_chips(x, y, c):
    return [(_flip(x, a), _flip(y, b), c) for a, b in CHIP_RELATIONS]


def _exchange(body, ins, out_types, n_sems, name, sequencer=None):
    n = len(ins)
    sems = (pltpu.SemaphoreType.DMA((n_sems,)), pltpu.SemaphoreType.DMA((n_sems,)))
    if sequencer is None:
        in_place = out_types is None
        out_shape = [jax.ShapeDtypeStruct(a.shape, a.dtype) for a in ins] if in_place else out_types

        def tc_body(*refs):
            body(refs[:n], refs[n:n + len(out_shape)], refs[-2], refs[-1])

        return list(pl.pallas_call(
            tc_body, out_shape=out_shape, in_specs=[ANY] * n, out_specs=[ANY] * len(out_shape),
            input_output_aliases={g: g for g in range(n)} if in_place else {}, scratch_shapes=list(sems), name=name)(*ins))

    collective_id, peers = sequencer
    hbm = pltpu.MemorySpace.HBM
    in_refs = [jax.new_ref(a, memory_space=hbm) for a in ins]
    out_refs = in_refs if out_types is None else [jax.empty_ref(t, memory_space=hbm) for t in out_types]

    @pl.kernel(mesh=plsc.ScalarSubcoreMesh(axis_name="sequencer", num_cores=1), name=name, scratch_types=sems,
               compiler_params=pltpu.CompilerParams(collective_id=collective_id))
    def launch(send_sems, recv_sems):
        barrier = pltpu.get_barrier_semaphore()
        devices = peers(*_place())
        for device in devices:
            pl.semaphore_signal(barrier, inc=1, device_id=device, device_id_type=MESH_ID)
        pl.semaphore_wait(barrier, len(devices))
        body(in_refs, out_refs, send_sems, recv_sems)

    launch()
    return [r[...] for r in out_refs]


def _allgather(shards, name, sequencer=None):
    out_types = [jax.ShapeDtypeStruct((N_CHIPS,) + s.shape, s.dtype) for s in shards]
    if sequencer is not None:
        sequencer = (sequencer, lambda x, y, c: _sibling(x, y, c) + _same_core_of_other_chips(x, y, c))
    return _exchange(_allgather_body, shards, out_types, 7 * len(shards), name, sequencer)


def _half_tile(half):
    return max(t for t in range(16, 1025, 16) if half % t == 0)


def _run_copies(cps):
    for cp in cps:
        cp.start()
    for cp in cps:
        cp.wait_recv()
    for cp in cps:
        cp.wait_send()


def _sibling_halves(gsends, name, sequencer=None):
    def body(g_refs, out_refs, send_sems, recv_sems):
        x, y, c = _place()
        cps = []
        for g, (g_ref, out_ref) in enumerate(zip(g_refs, out_refs)):
            half = g_ref.shape[1] // 2
            cps.append(pltpu.make_async_remote_copy(
                src_ref=g_ref.at[:, pl.ds((1 - c) * half, half), :], dst_ref=out_ref,
                send_sem=send_sems.at[g], recv_sem=recv_sems.at[g], device_id=(x, y, 1 - c), device_id_type=MESH_ID))
        _run_copies(cps)

    out_types = [jax.ShapeDtypeStruct((s.shape[0], s.shape[1] // 2, s.shape[2]), s.dtype) for s in gsends]
    return _exchange(body, gsends, out_types, len(gsends), name, sequencer and (sequencer, _sibling))


def _chip_sums(gsend, sib, place):
    n, rows, cols = gsend.shape
    half = rows // 2
    tm = _half_tile(half)
    nblk = half // tm

    def body(s_ref, g_ref, sib_ref, o_ref):
        o_ref[0] = (g_ref[0].astype(F32) + sib_ref[0].astype(F32)).astype(o_ref.dtype)

    grid_spec = pltpu.PrefetchScalarGridSpec(
        num_scalar_prefetch=1, grid=(n, nblk),
        in_specs=[pl.BlockSpec((1, tm, cols), lambda k, i, s: (jnp.bitwise_xor(s[0], k), s[1] * nblk + i, 0)),
                  pl.BlockSpec((1, tm, cols), lambda k, i, s: (jnp.bitwise_xor(s[0], k), i, 0))],
        out_specs=pl.BlockSpec((1, tm, cols), lambda k, i, s: (k, i, 0)))
    return pl.pallas_call(
        body, out_shape=jax.ShapeDtypeStruct((n, half, cols), BF16), grid_spec=grid_spec,
        name="rs_chip_sums", compiler_params=_params("parallel", "parallel"))(place, gsend, sib)


def _exchange_chip_sums(tsends, name, sequencer=None):
    def body(t_refs, out_refs, send_sems, recv_sems):
        x, y, c = _place()
        cps = []
        for g, (t_ref, out_ref) in enumerate(zip(t_refs, out_refs)):
            for k, device in enumerate(_same_core_of_other_chips(x, y, c)):
                cps.append(pltpu.make_async_remote_copy(
                    src_ref=t_ref.at[k + 1], dst_ref=out_ref.at[k], send_sem=send_sems.at[3 * g + k],
                    recv_sem=recv_sems.at[3 * g + k], device_id=device, device_id_type=MESH_ID))
        _run_copies(cps)

    out_types = [jax.ShapeDtypeStruct((3,) + s.shape[1:], s.dtype) for s in tsends]
    return _exchange(body, tsends, out_types, 3 * len(tsends), name,
                     sequencer and (sequencer, _same_core_of_other_chips))


def _final_sum(tsend, recv, place):
    n, half, cols = tsend.shape
    tm = _half_tile(half)
    nblk = half // tm

    def body(s_ref, t_ref, r_ref, o_ref):
        o_ref[...] = ((t_ref[0].astype(F32) + r_ref[0].astype(F32)) + r_ref[1].astype(F32)) + r_ref[2].astype(F32)

    grid_spec = pltpu.PrefetchScalarGridSpec(
        num_scalar_prefetch=1, grid=(nblk,),
        in_specs=[pl.BlockSpec((1, tm, cols), lambda i, s: (0, i, 0)), pl.BlockSpec((n - 1, tm, cols), lambda i, s: (0, i, 0))],
        out_specs=pl.BlockSpec((tm, cols), lambda i, s: (s[1] * nblk + i, 0)))
    return pl.pallas_call(
        body, out_shape=jax.ShapeDtypeStruct((2 * half, cols), F32), grid_spec=grid_spec, name="rs_final_sum",
        compiler_params=_params("parallel"))(place, tsend, recv)


def _join_halves(gfulls, name, sequencer=None):
    def body(g_refs, out_refs, send_sems, recv_sems):
        x, y, c = _place()
        n = len(g_refs)

        def copy(g, pc):
            half = g_refs[g].shape[0] // 2
            return pltpu.make_async_remote_copy(
                src_ref=g_refs[g].at[pl.ds(pc * half, half), :], dst_ref=out_refs[g].at[pl.ds(pc * half, half), :],
                send_sem=send_sems.at[g], recv_sem=recv_sems.at[g], device_id=(x, y, 1 - c), device_id_type=MESH_ID)

        mine = [copy(g, c) for g in range(n)]
        for cp in mine:
            cp.start()
        for g in range(n):
            copy(g, 1 - c).wait_recv()
        for cp in mine:
            cp.wait_send()

    return _exchange(body, gfulls, None, len(gfulls), name, sequencer and (sequencer, _sibling))


def _allreduce_small(v):
    rows, cols = v.shape

    def body(v_ref, out_ref, buf, send_sems, recv_sems):
        x, y, c = _place()
        cps = []
        for k in range(1, 8):
            peer = (_flip(x, (k >> 2) & 1), _flip(y, (k >> 1) & 1), _flip(c, k & 1))
            cps.append(pltpu.make_async_remote_copy(
                src_ref=v_ref, dst_ref=buf.at[k - 1], send_sem=send_sems.at[k - 1], recv_sem=recv_sems.at[k - 1],
                device_id=peer, device_id_type=MESH_ID))
        for cp in cps:
            cp.start()
        for cp in cps:
            cp.wait_recv()
        for cp in cps:
            cp.wait_send()
        t0 = v_ref[...] + buf[0]
        t1 = buf[1] + buf[2]
        t2 = buf[3] + buf[4]
        t3 = buf[5] + buf[6]
        out_ref[...] = (t0 + t1) + (t2 + t3)

    vm = pl.BlockSpec(memory_space=pltpu.VMEM)
    return pl.pallas_call(
        body, out_shape=jax.ShapeDtypeStruct((rows, cols), F32), in_specs=[vm], out_specs=vm,
        scratch_shapes=[pltpu.VMEM((7, rows, cols), F32), pltpu.SemaphoreType.DMA((7,)), pltpu.SemaphoreType.DMA((7,))],
        name="allreduce_small")(v)


BIG_INFO = {n: (shape, ax) for n, shape, ax in BIG}
GROUPS = (("w_in",), ("w_ffn_gate", "w_ffn_up", "w_ffn_down", "w_out", "w_ple_gate"),
          ("w_branch_a", "w_branch_b", "w_ple_proj"))


def _shard_shape(name):
    (k, m), ax = BIG_INFO[name]
    return (k // N_CHIPS, m) if ax == 0 else (k, m // N_CHIPS)


def _group_rows(group):
    offs, off = {}, 0
    for n in group:
        offs[n] = off
        off += _shard_shape(n)[0]
    return offs, off


def _pack_groups(shards, layer, dtype):
    return [jnp.concatenate([shards[n][layer].astype(dtype) for n in group], axis=0) for group in GROUPS]


def _unpack_full(gathered, groups):
    out = {}
    for group, arr in zip(groups, gathered):
        offs, _ = _group_rows(group)
        for n in group:
            rows, cols = _shard_shape(n)
            (k, m), ax = BIG_INFO[n]
            slab = arr[:, offs[n]:offs[n] + rows]
            out[n] = slab.reshape(k, m) if ax == 0 else jnp.transpose(slab, (1, 0, 2)).reshape(k, m)
    return out


def _pack_grads(gfull):
    out = []
    for group in GROUPS:
        parts = []
        for n in group:
            rows, cols = _shard_shape(n)
            ax = BIG_INFO[n][1]
            slab = (gfull[n].reshape(N_CHIPS, rows, cols) if ax == 0
                    else jnp.transpose(gfull[n].reshape(rows, N_CHIPS, cols), (1, 0, 2)))
            parts.append(slab.astype(BF16))
        out.append(jnp.concatenate(parts, axis=1))
    return out


def _after(values, mark):
    values, _ = lax.optimization_barrier((values, mark))
    return values


def _reduce_scatter_begin(gsends, place, tag, ids):
    sibs = _sibling_halves(gsends, "rs_sibling_halves_" + tag, ids[0])
    tsends = [_chip_sums(g, s, place) for g, s in zip(gsends, sibs)]
    return tsends, _exchange_chip_sums(tsends, "rs_exchange_" + tag, ids[1])


def _reduce_scatter_finish(begun, place, tag, ids, hold):
    tsends, recvs = begun
    recvs = _after(recvs, hold)
    return _join_halves([_final_sum(t, r, place) for t, r in zip(tsends, recvs)], "rs_join_halves_" + tag, ids[2])


SMALL_SHAPES = {"rel_table": (NUM_BUCKETS, 2 * N_HEADS), "norm_mix_g": (DEPTH, D_MODEL), "qnorm_a_g": (DEPTH, HEAD_DIM),
                "knorm_a_g": (DEPTH, HEAD_DIM), "qnorm_b_g": (DEPTH, HEAD_DIM), "knorm_b_g": (DEPTH, HEAD_DIM),
                "sink_b": (DEPTH, N_HEADS), "norm_ffn_g": (DEPTH, D_MODEL), "norm_ple_g": (DEPTH, D_MODEL)}


def _pack_small(vals):
    flat = jnp.concatenate([vals[n].astype(F32).reshape(-1) for n in SMALL])
    flat = jnp.concatenate([flat, jnp.zeros((SMALL_ROWS * LANES - flat.shape[0],), F32)])
    return flat.reshape(SMALL_ROWS, LANES)


def _unpack_small(packed):
    flat, out, off = packed.reshape(-1), {}, 0
    for n in SMALL:
        size = math.prod(SMALL_SHAPES[n])
        out[n] = flat[off:off + size].reshape(SMALL_SHAPES[n])
        off += size
    return out


def _adamw(w, gs, g_row, m, v, name):
    c1 = 1.0 - ADAM_B1 ** ADAM_STEP
    c2 = 1.0 - ADAM_B2 ** ADAM_STEP
    total, width = w.shape
    n_layers = len(gs)
    per = total // n_layers
    tm = max(t for t in range(8, 513, 8) if per % t == 0 and g_row % t == 0)
    nblk = per // tm

    def body(*refs):
        w_ref, g_refs = refs[0], refs[1:1 + n_layers]
        m_ref, v_ref, og, od, om, ov = refs[1 + n_layers:]
        layer = pl.program_id(0) // nblk
        g = g_refs[0][...]
        for l in range(1, n_layers):
            g = jnp.where(layer == l, g_refs[l][...], g)
        m_new = ADAM_B1 * m_ref[...] + (1.0 - ADAM_B1) * g
        v_new = ADAM_B2 * v_ref[...] + (1.0 - ADAM_B2) * (g * g)
        og[...] = g
        od[...] = -ADAM_LR * ((m_new / c1) / (jnp.sqrt(v_new / c2) + ADAM_EPS) + ADAM_WD * w_ref[...])
        om[...] = m_new
        ov[...] = v_new

    row = pl.BlockSpec((tm, width), lambda i: (i, 0))
    g_specs = [pl.BlockSpec((tm, width), lambda i, l=l: (g_row // tm + jnp.clip(i - l * nblk, 0, nblk - 1), 0))
               for l in range(n_layers)]
    return pl.pallas_call(
        body, out_shape=[jax.ShapeDtypeStruct((total, width), F32)] * 4, grid=(total // tm,),
        in_specs=[row] + g_specs + [row, row], out_specs=[row] * 4, name=name,
        compiler_params=_params("parallel"))(w, *gs, m, v)


def kernel(x, p, rel_table, norm_mix_g, w_in, qnorm_a_g, knorm_a_g, qnorm_b_g, knorm_b_g, sink_b, w_branch_a, w_branch_b, w_out, norm_ffn_g, w_ffn_gate, w_ffn_up, w_ffn_down, norm_ple_g, w_ple_gate, w_ple_proj, loss_target, m_rel_table, m_norm_mix_g, m_w_in, m_qnorm_a_g, m_knorm_a_g, m_qnorm_b_g, m_knorm_b_g, m_sink_b, m_w_branch_a, m_w_branch_b, m_w_out, m_norm_ffn_g, m_w_ffn_gate, m_w_ffn_up, m_w_ffn_down, m_norm_ple_g, m_w_ple_gate, m_w_ple_proj, v_rel_table, v_norm_mix_g, v_w_in, v_qnorm_a_g, v_knorm_a_g, v_qnorm_b_g, v_knorm_b_g, v_sink_b, v_w_branch_a, v_w_branch_b, v_w_out, v_norm_ffn_g, v_w_ffn_gate, v_w_ffn_up, v_w_ffn_down, v_norm_ple_g, v_w_ple_gate, v_w_ple_proj):
    given = dict(locals())

    def held(name, a):
        return jnp.swapaxes(a, 1, 2) if name in TRANSPOSED else a

    weights = {n: held(n, given[n]) for n in WEIGHTS}
    moments_m = {n: held(n, given["m_" + n]) for n in WEIGHTS}
    moments_v = {n: held(n, given["v_" + n]) for n in WEIGHTS}
    xi, yi, ci = _place()
    place = jnp.stack([2 * xi + yi, ci]).astype(jnp.int32)

    shards = [_pack_groups(weights, l, BF16) for l in range(DEPTH)]
    w_in0 = _allgather(shards[0][:1], "allgather_w_in_layer0", sequencer=9)
    rest0 = _allgather(_after(shards[0][1:], w_in0), "allgather_rest_layer0", sequencer=1)
    gathered = [w_in0 + rest0, None]
    small = {n: weights[n] for n in SMALL}

    def w_in_of(l, mark):
        return _unpack_full(gathered[l][:1] if l == 0 else _after(gathered[l][:1], mark), GROUPS[:1])["w_in"]

    def rest_of(l, mark):
        if l == 0:
            gathered[1] = _allgather(_after(shards[1], mark), "allgather_layer1", sequencer=2)
        return _unpack_full(_after(gathered[l][1:], mark), GROUPS[1:])

    loss, dx, gbig, gsmall, marks = _local_step(x[0], p[:, 0], loss_target[0], w_in_of, rest_of, small)

    gsends = [_pack_grads(gbig[l]) for l in range(DEPTH)]
    stages = {"layer1": (gsends[1], (3, 4, 5)), "rest_layer0": (gsends[0][1:], (6, 7, 8)),
              "w_in_layer0": (gsends[0][:1], (10, 11, 12))}
    begun = {tag: _reduce_scatter_begin(g, place, tag, ids) for tag, (g, ids) in stages.items()}

    def finish(tag, hold):
        return _reduce_scatter_finish(begun[tag], place, tag, stages[tag][1], hold)

    red1 = finish("layer1", marks[0]["attn_bwd_done"])
    rest0 = finish("rest_layer0", marks[0]["attn_bwd_done"])

    grads, delta, new_m, new_v = {}, {}, {}, {}

    def update(group, reduced):
        offs, _ = _group_rows(group)
        for n in group:
            shape = weights[n].shape
            two_d = lambda a: a.reshape(shape[0] * shape[1], shape[2])
            outs = _adamw(two_d(weights[n]), reduced, offs[n], two_d(moments_m[n]), two_d(moments_v[n]), "adamw_" + n)
            grads[n], delta[n], new_m[n], new_v[n] = (held(n, o.reshape(shape)) for o in outs)

    for gi in (1, 2):
        update(GROUPS[gi], _after([rest0[gi - 1], red1[gi]], begun["w_in_layer0"][0]))
    others_done = [dx] + [delta[n] for gi in (1, 2) for n in GROUPS[gi]]
    update(GROUPS[0], [finish("w_in_layer0", others_done)[0], red1[0]])
    small_grads = _allreduce_small(_pack_small(gsmall))
    g_, d_, m_, v_ = _adamw(_pack_small(weights), [small_grads], 0, _pack_small(moments_m), _pack_small(moments_v),
                            "adamw_small")
    grads.update(_unpack_small(g_))
    delta.update(_unpack_small(d_))
    new_m.update(_unpack_small(m_))
    new_v.update(_unpack_small(v_))

    loss = lax.psum(loss, ("x", "y", "c"))
    return (loss, dx[None], *[grads[n] for n in WEIGHTS], *[delta[n] for n in WEIGHTS],
            *[new_m[n] for n in WEIGHTS], *[new_v[n] for n in WEIGHTS])
```

```python
import functools
import math

import jax
import jax.numpy as jnp
from jax import lax
from jax.experimental import pallas as pl
from jax.experimental.pallas import tpu as pltpu
from jax.experimental.pallas import tpu_sc as plsc

F32 = jnp.float32
BF16 = jnp.bfloat16
MESH_ID = pl.DeviceIdType.MESH

SEQ = 2048
D_MODEL = 1024
DEPTH = 2
HEAD_DIM = 64
N_HEADS = 8
WIDTH = N_HEADS * HEAD_DIM
N_PAIRS = 4
ITEMS = 4
N_KV_B = 2
PLE_DIM = 256
D_FF = 2816
D_IN = 4352
OFF_QA, OFF_KA, OFF_VA, OFF_QB, OFF_KB, OFF_VB, OFF_GA, OFF_GB = 0, 512, 1024, 1536, 2048, 2176, 2304, 3328
DILATED = ((64, 1), (64, 4), (64, 16))
BLK_B = 128
NUM_BUCKETS = 32
MAX_DISTANCE = 1024
RMS_EPS = 1e-6
NEG_INF = -1e30
LANES = 128
VMEM_LIMIT = 48 * 1024 * 1024

ADAM_LR, ADAM_B1, ADAM_B2, ADAM_EPS, ADAM_WD, ADAM_STEP = 0.001, 0.9, 0.999, 1e-08, 0.01, 10

TRANSPOSED = ("w_in", "w_ffn_gate", "w_ffn_up")
BIG = (
    ("w_in", (D_IN, D_MODEL), 0),
    ("w_branch_a", (WIDTH, D_MODEL), 1),
    ("w_branch_b", (WIDTH, D_MODEL), 1),
    ("w_out", (D_MODEL, D_MODEL), 0),
    ("w_ffn_gate", (D_FF, D_MODEL), 0),
    ("w_ffn_up", (D_FF, D_MODEL), 0),
    ("w_ffn_down", (D_FF, D_MODEL), 0),
    ("w_ple_gate", (D_MODEL, D_MODEL), 0),
    ("w_ple_proj", (PLE_DIM, D_MODEL), 1),
)
SMALL = ("rel_table", "norm_mix_g", "qnorm_a_g", "knorm_a_g", "qnorm_b_g", "knorm_b_g", "sink_b",
         "norm_ffn_g", "norm_ple_g")
WEIGHTS = ("rel_table", "norm_mix_g", "w_in", "qnorm_a_g", "knorm_a_g", "qnorm_b_g", "knorm_b_g", "sink_b",
           "w_branch_a", "w_branch_b", "w_out", "norm_ffn_g", "w_ffn_gate", "w_ffn_up", "w_ffn_down",
           "norm_ple_g", "w_ple_gate", "w_ple_proj")
N_CHIPS = 4
SMALL_ROWS = 64


def _params(*sem):
    return pltpu.CompilerParams(dimension_semantics=sem, vmem_limit_bytes=VMEM_LIMIT)


def _in_hbm(*arrays):
    return [pltpu.with_memory_space_constraint(a, pltpu.HBM) for a in arrays]


def _pick(dim, target):
    for t in (target, 512, 256, 128, 64, 32, 16, 8):
        if t <= target and dim % t == 0:
            return t
    return dim


MM_VMEM_BUDGET = 40 * 1024 * 1024
STEP_OVERHEAD_S = 0.4e-6
TILE_DMA_BYTES_PER_S = 1.5e12


def _mm_dims(a, b, mode):
    if mode == "nn":
        return a.shape[0], b.shape[1], a.shape[1]
    if mode == "nt":
        return a.shape[0], b.shape[0], a.shape[1]
    return a.shape[1], b.shape[1], a.shape[0]


def _mm_tiles(m, n, pairs, tile_bytes, col_offsets):
    best = None
    for tm in (t for t in range(LANES, m + 1, LANES) if m % t == 0):
        for tn in (t for t in range(LANES, n + 1, LANES) if n % t == 0 and all(o % t == 0 for o in col_offsets)):
            io = sum(tm * k * ab + tn * k * bb for k, ab, bb in pairs) + tm * tn * sum(tile_bytes)
            casts = sum((tm * k * 2 if ab == 4 else 0) + (tn * k * 2 if bb == 4 else 0) for k, ab, bb in pairs)
            if 2 * io + len(pairs) * tm * tn * 4 + casts > MM_VMEM_BUDGET:
                continue
            cost = (m // tm) * (n // tn) * STEP_OVERHEAD_S + io / TILE_DMA_BYTES_PER_S
            if best is None or (cost, -tm) < best[0]:
                best = ((cost, -tm), tm, tn)
    return best[1], best[2]


def _mm_fused(pairs, extras, epilogue, out_dtypes, name):
    m, n, _ = _mm_dims(*pairs[0])
    assert all(_mm_dims(*p)[:2] == (m, n) for p in pairs)
    tm, tn = _mm_tiles(
        m, n, [(_mm_dims(a, b, mode)[2], a.dtype.itemsize, b.dtype.itemsize) for a, b, mode in pairs],
        [e.dtype.itemsize for e, _ in extras] + [jnp.dtype(d).itemsize for d in out_dtypes], [off for _, off in extras])
    dims = {"nn": (((1,), (0,)), ((), ())), "nt": (((1,), (1,)), ((), ())), "tn": (((0,), (0,)), ((), ()))}
    in_specs, args = [], []
    for a, b, mode in pairs:
        k = _mm_dims(a, b, mode)[2]
        in_specs.append(pl.BlockSpec((k, tm), lambda i, j: (0, i)) if mode == "tn" else pl.BlockSpec((tm, k), lambda i, j: (i, 0)))
        in_specs.append(pl.BlockSpec((tn, k), lambda i, j: (j, 0)) if mode == "nt" else pl.BlockSpec((k, tn), lambda i, j: (0, j)))
        args += [a, b]
    for e, off in extras:
        in_specs.append(pl.BlockSpec((tm, tn), lambda i, j, o=off // tn: (i, o + j)))
        args.append(e)
    n_pairs, n_in = len(pairs), 2 * len(pairs) + len(extras)

    def body(*refs):
        products = [lax.dot_general(refs[2 * p][...].astype(BF16), refs[2 * p + 1][...].astype(BF16), dims[pairs[p][2]],
                                    preferred_element_type=F32) for p in range(n_pairs)]
        outs = epilogue(products, [r[...] for r in refs[2 * n_pairs:n_in]])
        for r, o in zip(refs[n_in:], outs):
            r[...] = o.astype(r.dtype)

    tile = pl.BlockSpec((tm, tn), lambda i, j: (i, j))
    return pl.pallas_call(
        body, out_shape=[jax.ShapeDtypeStruct((m, n), d) for d in out_dtypes], grid=(m // tm, n // tn),
        in_specs=in_specs, out_specs=[tile] * len(out_dtypes), name=name,
        compiler_params=_params("parallel", "parallel"))(*_in_hbm(*args))


def _mm(a, b, mode, out_dtype, name, res=None):
    if res is None:
        return _mm_fused([(a, b, mode)], [], lambda products, extra: products, [out_dtype], name)[0]
    return _mm_fused([(a, b, mode)], [(res, 0)], lambda products, extra: [products[0] + extra[0]], [out_dtype], name)[0]


def _ew(fn, ins, out_dtypes, *, width, bw, name, vecs=(), tm=256):
    rows = ins[0][0].shape[0]
    tm = _pick(rows, tm)
    n_in = len(ins) + len(vecs)

    def col_map(off_blocks):
        return lambda i, j: (i, off_blocks + j)

    in_specs = [pl.BlockSpec((tm, bw), col_map(off // bw)) for _, off in ins]
    in_specs += [pl.BlockSpec((1, bw), lambda i, j: (0, j)) for _ in vecs]

    def body(*refs):
        outs = fn(*[r[...] for r in refs[:n_in]])
        for r, o in zip(refs[n_in:], outs):
            r[...] = o.astype(r.dtype)

    return pl.pallas_call(
        body, out_shape=[jax.ShapeDtypeStruct((rows, width), dt) for dt in out_dtypes],
        grid=(rows // tm, width // bw), in_specs=in_specs,
        out_specs=[pl.BlockSpec((tm, bw), lambda i, j: (i, j)) for _ in out_dtypes],
        name=name, compiler_params=_params("parallel", "parallel"))(*[a for a, _ in ins], *vecs)


def _sigmoid(x):
    return 1.0 / (1.0 + jnp.exp(-x))


def _seg_sum(v):
    outs = []
    for k in range(v.shape[1] // LANES):
        vp = v[:, k * LANES:(k + 1) * LANES]
        left = lax.broadcasted_iota(jnp.int32, vp.shape, 1) < HEAD_DIM
        sl = jnp.sum(jnp.where(left, vp, 0.0), axis=-1, keepdims=True)
        sr = jnp.sum(jnp.where(left, 0.0, vp), axis=-1, keepdims=True)
        outs.append(jnp.where(left, sl, sr))
    return outs[0] if len(outs) == 1 else jnp.concatenate(outs, axis=1)


def _seg_rstd(x):
    return lax.rsqrt(_seg_sum(x * x) * (1.0 / HEAD_DIM) + RMS_EPS)


def _rms_fwd(x, g, name):
    rows, d = x.shape
    tm = 256

    def body(x_ref, g_ref, h_ref):
        xv = x_ref[...]
        r = lax.rsqrt(jnp.mean(xv * xv, axis=-1, keepdims=True) + RMS_EPS)
        h_ref[...] = ((xv * r) * g_ref[...]).astype(BF16)

    return pl.pallas_call(
        body, out_shape=jax.ShapeDtypeStruct((rows, d), BF16), grid=(rows // tm,),
        in_specs=[pl.BlockSpec((tm, d), lambda i: (i, 0)), pl.BlockSpec((1, d), lambda i: (0, 0))],
        out_specs=pl.BlockSpec((tm, d), lambda i: (i, 0)), name=name,
        compiler_params=_params("parallel"))(*_in_hbm(x, g))


def _rms_bwd(x, g, dh, dres, name):
    rows, d = x.shape
    tm = 256

    def body(x_ref, g_ref, dh_ref, dres_ref, dx_ref, dxb_ref, dg_ref):
        xv = x_ref[...]
        r = lax.rsqrt(jnp.mean(xv * xv, axis=-1, keepdims=True) + RMS_EPS)
        xh = xv * r
        dhv = dh_ref[...]
        dxh = dhv * g_ref[...]
        dxv = dres_ref[...] + r * (dxh - xh * jnp.mean(dxh * xh, axis=-1, keepdims=True))
        dx_ref[...] = dxv
        dxb_ref[...] = dxv.astype(BF16)
        part = jnp.sum(dhv * xh, axis=0, keepdims=True)

        @pl.when(pl.program_id(0) == 0)
        def _():
            dg_ref[...] = part

        @pl.when(pl.program_id(0) > 0)
        def _():
            dg_ref[...] += part

    row = pl.BlockSpec((tm, d), lambda i: (i, 0))
    vec = pl.BlockSpec((1, d), lambda i: (0, 0))
    return pl.pallas_call(
        body, out_shape=[jax.ShapeDtypeStruct((rows, d), F32), jax.ShapeDtypeStruct((rows, d), BF16),
                         jax.ShapeDtypeStruct((1, d), F32)],
        grid=(rows // tm,), in_specs=[row, vec, row, row], out_specs=[row, row, vec],
        name=name, compiler_params=_params("arbitrary"))(*_in_hbm(x, g, dh, dres))


def _loss_grad(y, t):
    rows, d = y.shape
    tm = 256

    def body(y_ref, t_ref, dy_ref, l_ref):
        e = y_ref[...] - t_ref[...]
        dy_ref[...] = e * (1.0 / d)
        part = jnp.zeros((1, LANES), F32) + jnp.sum(e * e) * (0.5 / d)

        @pl.when(pl.program_id(0) == 0)
        def _():
            l_ref[...] = part

        @pl.when(pl.program_id(0) > 0)
        def _():
            l_ref[...] += part

    row = pl.BlockSpec((tm, d), lambda i: (i, 0))
    return pl.pallas_call(
        body, out_shape=[jax.ShapeDtypeStruct((rows, d), F32), jax.ShapeDtypeStruct((1, LANES), F32)],
        grid=(rows // tm,), in_specs=[row, row], out_specs=[row, pl.BlockSpec((1, LANES), lambda i: (0, 0))],
        name="loss_grad", compiler_params=_params("arbitrary"))(y, t)


def _put_pairs(ref, val):
    for hp in range(N_PAIRS):
        ref[hp] = val[:, hp * LANES:(hp + 1) * LANES].astype(ref.dtype)


def _get_pairs(ref):
    return jnp.concatenate([ref[hp] for hp in range(N_PAIRS)], axis=1)


def _swap_halves(v):
    return pltpu.roll(v, HEAD_DIM, axis=1)


def _expand_kv(kv):
    left = lax.broadcasted_iota(jnp.int32, kv.shape, 1) < HEAD_DIM
    sw = _swap_halves(kv)
    h0 = jnp.where(left, kv, sw)
    h1 = jnp.where(left, sw, kv)
    return jnp.concatenate([h0, h0, h1, h1], axis=1)


def _reduce_kv(dkv):
    left = lax.broadcasted_iota(jnp.int32, (dkv.shape[0], LANES), 1) < HEAD_DIM
    t = dkv[:, 0:LANES] + dkv[:, LANES:2 * LANES]
    u = dkv[:, 2 * LANES:3 * LANES] + dkv[:, 3 * LANES:4 * LANES]
    t = t + _swap_halves(t)
    u = u + _swap_halves(u)
    return jnp.where(left, t, u)


def _qknorm_fwd(proj, gqa, gka, gqb, gkb):
    rows = proj.shape[0]
    tm = 256

    def body(qa_ref, ka_ref, va_ref, qb_ref, kb_ref, vb_ref, gqa_ref, gka_ref, gqb_ref, gkb_ref,
             oqa, oka, ova, oqb, okb, ovb):
        for src, g_ref, dst in ((qa_ref, gqa_ref, oqa), (ka_ref, gka_ref, oka), (qb_ref, gqb_ref, oqb)):
            xv = src[...]
            _put_pairs(dst, (xv * _seg_rstd(xv)) * g_ref[...])
        _put_pairs(ova, va_ref[...])
        kv = kb_ref[...]
        _put_pairs(okb, _expand_kv((kv * _seg_rstd(kv)) * gkb_ref[...]))
        _put_pairs(ovb, _expand_kv(vb_ref[...]))

    def win(width, off):
        return pl.BlockSpec((tm, width), lambda i: (i, off // width))

    vec = lambda w: pl.BlockSpec((1, w), lambda i: (0, 0))
    out = pl.BlockSpec((N_PAIRS, tm, LANES), lambda i: (0, i, 0))
    return pl.pallas_call(
        body, out_shape=[jax.ShapeDtypeStruct((N_PAIRS, rows, LANES), F32)] * 6, grid=(rows // tm,),
        in_specs=[win(WIDTH, OFF_QA), win(WIDTH, OFF_KA), win(WIDTH, OFF_VA), win(WIDTH, OFF_QB),
                  win(LANES, OFF_KB), win(LANES, OFF_VB), vec(WIDTH), vec(WIDTH), vec(WIDTH), vec(LANES)],
        out_specs=[out] * 6, name="qknorm_fwd", compiler_params=_params("parallel"))(
            *_in_hbm(proj, proj, proj, proj, proj, proj, gqa, gka, gqb, gkb))


def _norm_bwd(xv, g, dy):
    r = _seg_rstd(xv)
    xh = xv * r
    dxh = dy * g
    dx = r * (dxh - xh * (_seg_sum(dxh * xh) * (1.0 / HEAD_DIM)))
    return dx, jnp.sum(dy * xh, axis=0, keepdims=True)


def _qknorm_bwd(proj, gqa, gka, gqb, gkb, dqa, dka, dva, dqb, dkb, dvb, dga, dgb):
    rows = proj.shape[0]
    tm = 256
    n_a = len(dqa)

    def body(*refs):
        qa_ref, ka_ref, qb_ref, kb_ref, gqa_ref, gka_ref, gqb_ref, gkb_ref = refs[:8]
        pos = 8
        dqa_refs, dka_refs, dva_refs = refs[pos:pos + n_a], refs[pos + n_a:pos + 2 * n_a], refs[pos + 2 * n_a:pos + 3 * n_a]
        pos += 3 * n_a
        dqb_ref, dkb_ref, dvb_ref, dga_ref, dgb_ref = refs[pos:pos + 5]
        dproj_ref, ogqa, ogka, ogqb, ogkb = refs[pos + 5:]

        def total(rs):
            acc = _get_pairs(rs[0])
            for r in rs[1:]:
                acc = acc + _get_pairs(r)
            return acc

        dx_qa, p_qa = _norm_bwd(qa_ref[...], gqa_ref[...], total(dqa_refs))
        dx_ka, p_ka = _norm_bwd(ka_ref[...], gka_ref[...], total(dka_refs))
        dx_qb, p_qb = _norm_bwd(qb_ref[...], gqb_ref[...], _get_pairs(dqb_ref))
        dx_kb, p_kb = _norm_bwd(kb_ref[...], gkb_ref[...], _reduce_kv(_get_pairs(dkb_ref)))
        dproj_ref[:, OFF_QA:OFF_QA + WIDTH] = dx_qa.astype(BF16)
        dproj_ref[:, OFF_KA:OFF_KA + WIDTH] = dx_ka.astype(BF16)
        dproj_ref[:, OFF_VA:OFF_VA + WIDTH] = total(dva_refs).astype(BF16)
        dproj_ref[:, OFF_QB:OFF_QB + WIDTH] = dx_qb.astype(BF16)
        dproj_ref[:, OFF_KB:OFF_KB + LANES] = dx_kb.astype(BF16)
        dproj_ref[:, OFF_VB:OFF_VB + LANES] = _reduce_kv(_get_pairs(dvb_ref)).astype(BF16)
        dproj_ref[:, OFF_GA:OFF_GB] = dga_ref[...]
        dproj_ref[:, OFF_GB:D_IN] = dgb_ref[...]
        first = pl.program_id(0) == 0
        for o_ref, part in ((ogqa, p_qa), (ogka, p_ka), (ogqb, p_qb), (ogkb, p_kb)):
            @pl.when(first)
            def _(o_ref=o_ref, part=part):
                o_ref[...] = part

            @pl.when(jnp.logical_not(first))
            def _(o_ref=o_ref, part=part):
                o_ref[...] += part

    def win(width, off):
        return pl.BlockSpec((tm, width), lambda i: (i, off // width))

    vec = lambda w: pl.BlockSpec((1, w), lambda i: (0, 0))
    row = lambda w: pl.BlockSpec((tm, w), lambda i: (i, 0))
    in_specs = [win(WIDTH, OFF_QA), win(WIDTH, OFF_KA), win(WIDTH, OFF_QB), win(LANES, OFF_KB),
                vec(WIDTH), vec(WIDTH), vec(WIDTH), vec(LANES)]
    in_specs += [pl.BlockSpec((N_PAIRS, tm, LANES), lambda i: (0, i, 0))] * (3 * n_a + 3) + [row(D_MODEL)] * 2
    return pl.pallas_call(
        body,
        out_shape=[jax.ShapeDtypeStruct((rows, D_IN), BF16), jax.ShapeDtypeStruct((1, WIDTH), F32),
                   jax.ShapeDtypeStruct((1, WIDTH), F32), jax.ShapeDtypeStruct((1, WIDTH), F32),
                   jax.ShapeDtypeStruct((1, LANES), F32)],
        grid=(rows // tm,), in_specs=in_specs,
        out_specs=[row(D_IN), vec(WIDTH), vec(WIDTH), vec(WIDTH), vec(LANES)],
        name="qknorm_bwd", compiler_params=_params("arbitrary"))(
            *_in_hbm(proj, proj, proj, proj, gqa, gka, gqb, gkb, *dqa, *dka, *dva, dqb, dkb, dvb, dga, dgb))


def _t5_bucket(rel):
    half_b = NUM_BUCKETS // 2
    max_exact = half_b // 2
    sign = jnp.where(rel > 0, half_b, 0)
    n = jnp.abs(rel)
    nf = jnp.maximum(n, 1).astype(F32)
    large = max_exact + (jnp.log(nf / max_exact) / math.log(MAX_DISTANCE / max_exact)
                         * (half_b - max_exact)).astype(jnp.int32)
    large = jnp.minimum(large, half_b - 1)
    return sign + jnp.where(n < max_exact, n, large)


def _band_buckets(blk, dilation):
    i = jnp.arange(blk, dtype=jnp.int32)[:, None]
    j = jnp.arange(3 * blk, dtype=jnp.int32)[None, :]
    rel = j - blk - i
    return jnp.where(jnp.abs(rel) <= blk, _t5_bucket(rel * dilation), -1)


def _bias_tiles(table, buckets, head_off, name):
    blk = buckets.shape[0]

    def body(tab_ref, bk_ref, o_ref):
        h = pl.program_id(0) + head_off
        bk = bk_ref[...]
        acc = jnp.full(bk.shape, NEG_INF, F32)
        for b in range(NUM_BUCKETS):
            acc = jnp.where(bk == b, tab_ref[b, h], acc)
        o_ref[0] = acc

    return pl.pallas_call(
        body, out_shape=jax.ShapeDtypeStruct((N_HEADS, blk, 3 * blk), F32), grid=(N_HEADS,),
        in_specs=[pl.BlockSpec(memory_space=pltpu.SMEM), pl.BlockSpec((blk, 3 * blk), lambda h: (0, 0))],
        out_specs=pl.BlockSpec((1, blk, 3 * blk), lambda h: (h, 0, 0)),
        name=name, compiler_params=_params("parallel"))(table, buckets)


def _table_grad(dbias, buckets, name):
    blk = buckets.shape[0]

    def body(db_ref, bk_ref, o_ref):
        bk = bk_ref[...]
        dbv = db_ref[0]
        lane = lax.broadcasted_iota(jnp.int32, (1, LANES), 1)
        acc = jnp.zeros((1, LANES), F32)
        for b in range(NUM_BUCKETS):
            acc = jnp.where(lane == b, jnp.sum(jnp.where(bk == b, dbv, 0.0)), acc)
        o_ref[0] = acc

    out = pl.pallas_call(
        body, out_shape=jax.ShapeDtypeStruct((N_HEADS, 1, LANES), F32), grid=(N_HEADS,),
        in_specs=[pl.BlockSpec((1, blk, 3 * blk), lambda h: (h, 0, 0)), pl.BlockSpec((blk, 3 * blk), lambda h: (0, 0))],
        out_specs=pl.BlockSpec((1, 1, LANES), lambda h: (h, 0, 0)),
        name=name, compiler_params=_params("parallel"))(dbias, buckets)
    return out[:, 0, :NUM_BUCKETS]


def _dot_nt(a, b):
    return lax.dot_general(a, b, (((1,), (1,)), ((), ())), preferred_element_type=F32)


def _stack_pair(x2, left):
    return jnp.concatenate([jnp.where(left, x2, 0.0), jnp.where(left, 0.0, x2)], axis=0).astype(BF16)


def _attn_geometry(blk, d):
    chunk = blk * ITEMS if d == 1 else blk * d
    groups = 1 if d == 1 else d // ITEMS
    halo = blk if d == 1 else chunk
    return chunk, groups, halo


def _item_rows(ref, j, r0, blk, d):
    if d == 1:
        return ref[j * blk:(j + 1) * blk, :]
    return ref[pl.ds(r0 + j, blk, stride=d), :]


def _item_penalty(t, nct, j, blk, d):
    first_ok, last_ok = t > 0, t < nct - 1
    if d == 1:
        first_ok = True if j > 0 else first_ok
        last_ok = True if j < ITEMS - 1 else last_ok
    col = lax.broadcasted_iota(jnp.int32, (1, 3 * blk), 1)
    ok = jnp.logical_and(jnp.logical_or(col >= blk, first_ok), jnp.logical_or(col < 2 * blk, last_ok))
    return jnp.where(ok, 0.0, NEG_INF).astype(F32)


def _attn_specs(seq, blk, d, step_of):
    chunk, _, halo = _attn_geometry(blk, d)
    per, last = chunk // halo, seq // halo - 1
    cur = pl.BlockSpec((None, chunk, LANES), lambda hp, t: (hp, step_of(t), 0))
    prev = pl.BlockSpec((None, halo, LANES), lambda hp, t: (hp, jnp.clip(step_of(t) * per - 1, 0, last), 0))
    nxt = pl.BlockSpec((None, halo, LANES), lambda hp, t: (hp, jnp.minimum((step_of(t) + 1) * per, last), 0))
    return cur, prev, nxt


def _attn_fwd(q, k, v, bias, sink, blk, d, name):
    _, seq, _ = q.shape
    chunk, groups, _ = _attn_geometry(blk, d)
    nct = seq // chunk
    has_sink = sink is not None
    scale = HEAD_DIM ** -0.5

    def body(*refs):
        q_ref, kp, kc, kn, vp, vc, vn, b_ref = refs[:8]
        s_ref = refs[8] if has_sink else None
        o_ref, l_ref = refs[-2], refs[-1]
        t = pl.program_id(1)
        left = lax.broadcasted_iota(jnp.int32, (1, LANES), 1) < HEAD_DIM
        bias2 = b_ref[...]
        if d == 1:
            kwin = jnp.concatenate([kp[...], kc[...], kn[...]], axis=0).astype(BF16)
            vwin = jnp.concatenate([vp[...], vc[...], vn[...]], axis=0).astype(BF16)

        def group(r0):
            scores, vcats = [], []
            for j in range(ITEMS):
                qs = _stack_pair(_item_rows(q_ref, j, r0, blk, d) * scale, left)
                if d == 1:
                    kcat, vcat = kwin[j * blk:(j + 3) * blk], vwin[j * blk:(j + 3) * blk]
                else:
                    kcat = jnp.concatenate([_item_rows(r, j, r0, blk, d) for r in (kp, kc, kn)], axis=0).astype(BF16)
                    vcat = jnp.concatenate([_item_rows(r, j, r0, blk, d) for r in (vp, vc, vn)], axis=0).astype(BF16)
                scores.append(_dot_nt(qs, kcat) + bias2 + _item_penalty(t, nct, j, blk, d))
                vcats.append(vcat)
            s = jnp.concatenate(scores, axis=0)
            m = jnp.max(s, axis=-1, keepdims=True)
            if has_sink:
                sk = jnp.concatenate([s_ref[...]] * ITEMS, axis=0)
                m = jnp.maximum(m, sk)
            p = jnp.exp(s - m)
            den = jnp.sum(p, axis=-1, keepdims=True)
            if has_sink:
                den = den + jnp.exp(sk - m)
            pn = (p * (1.0 / den)).astype(BF16)
            lse = m + jnp.log(den)
            for j in range(ITEMS):
                top, mid, bot = 2 * j * blk, (2 * j + 1) * blk, (2 * j + 2) * blk
                o2 = jnp.dot(pn[top:bot], vcats[j], preferred_element_type=F32)
                o_val = jnp.where(left, o2[:blk], o2[blk:])
                l_val = jnp.where(left, lse[top:mid], lse[mid:bot])
                if d == 1:
                    o_ref[j * blk:(j + 1) * blk, :] = o_val
                    l_ref[j * blk:(j + 1) * blk, :] = l_val
                else:
                    o_ref[pl.ds(r0 + j, blk, stride=d), :] = o_val
                    l_ref[pl.ds(r0 + j, blk, stride=d), :] = l_val

        if groups == 1:
            group(0)
        else:
            def step(g, carry):
                group(g * ITEMS)
                return carry

            lax.fori_loop(0, groups, step, 0)

    cur, prev, nxt = _attn_specs(seq, blk, d, lambda t: t)
    in_specs = [cur, prev, cur, nxt, prev, cur, nxt, pl.BlockSpec((2 * blk, 3 * blk), lambda hp, t: (hp, 0))]
    args = [q, k, k, k, v, v, v, bias]
    if has_sink:
        in_specs.append(pl.BlockSpec((2 * blk, 1), lambda hp, t: (hp, 0)))
        args.append(sink)
    return pl.pallas_call(
        body, out_shape=[jax.ShapeDtypeStruct(q.shape, F32)] * 2, grid=(N_PAIRS, nct),
        in_specs=in_specs, out_specs=[cur, cur], name=name,
        compiler_params=_params("parallel", "parallel"))(*_in_hbm(*args))


def _attn_bwd(q, k, v, do, lse, delta, bias, sink, blk, d, name):
    _, seq, _ = q.shape
    chunk, groups, halo = _attn_geometry(blk, d)
    nct = seq // chunk
    has_sink = sink is not None
    n_in = 12 if has_sink else 11
    scale = HEAD_DIM ** -0.5

    def body(*refs):
        q_ref, kp, kc, kn, vp, vc, vn, do_ref, l_ref, d_ref, b_ref = refs[:11]
        s_ref = refs[11] if has_sink else None
        dq_ref, dk_ref, dv_ref, db_ref = refs[n_in:n_in + 4]
        ds_ref = refs[n_in + 4] if has_sink else None
        wk, wv = refs[-2], refs[-1]
        t = pl.program_id(1)

        @pl.when(t == 0)
        def _():
            wk[...] = jnp.zeros_like(wk)
            wv[...] = jnp.zeros_like(wv)
            db_ref[...] = jnp.zeros_like(db_ref)
            if has_sink:
                ds_ref[...] = jnp.zeros_like(ds_ref)

        @pl.when(t > 0)
        def _():
            for w in (wk, wv):
                keep = w[chunk:2 * chunk + halo]
                w[0:chunk + halo] = keep
                w[chunk + halo:2 * chunk + halo] = jnp.zeros((chunk, LANES), F32)

        @pl.when(t < nct)
        def _():
            lane = lax.broadcasted_iota(jnp.int32, (1, LANES), 1)
            left = lane < HEAD_DIM
            bias2 = b_ref[...]
            if d == 1:
                kwin = jnp.concatenate([kp[...], kc[...], kn[...]], axis=0).astype(BF16)
                vwin = jnp.concatenate([vp[...], vc[...], vn[...]], axis=0).astype(BF16)

            def group(r0):
                qss, doss, kcats, scores, dps, lcols, dcols = [], [], [], [], [], [], []
                for j in range(ITEMS):
                    qs = _stack_pair(_item_rows(q_ref, j, r0, blk, d) * scale, left)
                    dos = _stack_pair(_item_rows(do_ref, j, r0, blk, d), left)
                    if d == 1:
                        kcat, vcat = kwin[j * blk:(j + 3) * blk], vwin[j * blk:(j + 3) * blk]
                    else:
                        kcat = jnp.concatenate([_item_rows(r, j, r0, blk, d) for r in (kp, kc, kn)], axis=0).astype(BF16)
                        vcat = jnp.concatenate([_item_rows(r, j, r0, blk, d) for r in (vp, vc, vn)], axis=0).astype(BF16)
                    l2, d2 = _item_rows(l_ref, j, r0, blk, d), _item_rows(d_ref, j, r0, blk, d)
                    lcols.append(jnp.max(jnp.where(left, l2, NEG_INF), axis=-1, keepdims=True))
                    lcols.append(jnp.max(jnp.where(left, NEG_INF, l2), axis=-1, keepdims=True))
                    dcols.append(jnp.sum(jnp.where(lane == 0, d2, 0.0), axis=-1, keepdims=True))
                    dcols.append(jnp.sum(jnp.where(lane == HEAD_DIM, d2, 0.0), axis=-1, keepdims=True))
                    scores.append(_dot_nt(qs, kcat) + bias2 + _item_penalty(t, nct, j, blk, d))
                    dps.append(_dot_nt(dos, vcat))
                    qss.append(qs)
                    doss.append(dos)
                    kcats.append(kcat)
                lcol = jnp.concatenate(lcols, axis=0)
                dcol = jnp.concatenate(dcols, axis=0)
                p = jnp.exp(jnp.concatenate(scores, axis=0) - lcol)
                ds = p * (jnp.concatenate(dps, axis=0) - dcol)
                if has_sink:
                    sgrad = dcol * jnp.exp(jnp.concatenate([s_ref[...]] * ITEMS, axis=0) - lcol)
                for j in range(ITEMS):
                    top, bot = 2 * j * blk, (2 * j + 2) * blk
                    dsj, pj = ds[top:bot], p[top:bot]
                    db_ref[...] += dsj
                    if has_sink:
                        ds_ref[...] -= sgrad[top:bot]
                    dq2 = jnp.dot(dsj.astype(BF16), kcats[j], preferred_element_type=F32) * scale
                    dq_val = jnp.where(left, dq2[:blk], dq2[blk:])
                    if d == 1:
                        dq_ref[j * blk:(j + 1) * blk, :] = dq_val
                    else:
                        dq_ref[pl.ds(r0 + j, blk, stride=d), :] = dq_val
                    dk_new = jnp.dot(jnp.transpose(dsj).astype(BF16), qss[j], preferred_element_type=F32)
                    dv_new = jnp.dot(jnp.transpose(pj).astype(BF16), doss[j], preferred_element_type=F32)
                    for w, new in ((wk, dk_new), (wv, dv_new)):
                        if d == 1:
                            w[chunk + (j - 1) * blk:chunk + (j + 2) * blk, :] += new
                        else:
                            for c in range(3):
                                w[pl.ds(c * chunk + r0 + j, blk, stride=d), :] += new[c * blk:(c + 1) * blk]

            if groups == 1:
                group(0)
            else:
                def step(g, carry):
                    group(g * ITEMS)
                    return carry

                lax.fori_loop(0, groups, step, 0)

        dk_ref[...] = wk[0:chunk]
        dv_ref[...] = wv[0:chunk]

    cur, prev, nxt = _attn_specs(seq, blk, d, lambda t: jnp.minimum(t, nct - 1))
    lag = pl.BlockSpec((None, chunk, LANES), lambda hp, t: (hp, jnp.maximum(t - 1, 0), 0))
    band = pl.BlockSpec((2 * blk, 3 * blk), lambda hp, t: (hp, 0))
    col = pl.BlockSpec((2 * blk, 1), lambda hp, t: (hp, 0))
    in_specs = [cur, prev, cur, nxt, prev, cur, nxt, cur, cur, cur, band]
    args = [q, k, k, k, v, v, v, do, lse, delta, bias]
    out_shape = [jax.ShapeDtypeStruct(q.shape, F32)] * 3 + [jax.ShapeDtypeStruct((N_HEADS * blk, 3 * blk), F32)]
    out_specs = [cur, lag, lag, band]
    if has_sink:
        in_specs.append(col)
        args.append(sink)
        out_shape.append(jax.ShapeDtypeStruct((N_HEADS * blk, 1), F32))
        out_specs.append(col)
    window = pltpu.VMEM((2 * chunk + halo, LANES), F32)
    return pl.pallas_call(
        body, out_shape=out_shape, grid=(N_PAIRS, nct + 1), in_specs=in_specs, out_specs=out_specs,
        scratch_shapes=[window, window], name=name,
        compiler_params=_params("arbitrary", "arbitrary"))(*_in_hbm(*args))


def _combine_patterns(outs, lses):
    _, rows, _ = outs[0].shape
    tm = 256
    n = len(outs)

    def body(*refs):
        o_refs, l_refs = refs[:n], refs[n:2 * n]
        y_ref, lse_ref = refs[2 * n], refs[2 * n + 1]
        for hp in range(N_PAIRS):
            ls = [r[hp] for r in l_refs]
            m = functools.reduce(jnp.maximum, ls)
            es = [jnp.exp(l - m) for l in ls]
            den = functools.reduce(lambda a, b: a + b, es)
            num = functools.reduce(lambda a, b: a + b, [e * r[hp] for e, r in zip(es, o_refs)])
            y_ref[:, hp * LANES:(hp + 1) * LANES] = num / den
            lse_ref[hp] = m + jnp.log(den)

    pm = pl.BlockSpec((N_PAIRS, tm, LANES), lambda i: (0, i, 0))
    return pl.pallas_call(
        body, out_shape=[jax.ShapeDtypeStruct((rows, WIDTH), F32), jax.ShapeDtypeStruct((N_PAIRS, rows, LANES), F32)],
        grid=(rows // tm,), in_specs=[pm] * (2 * n), out_specs=[pl.BlockSpec((tm, WIDTH), lambda i: (i, 0)), pm],
        name="combine_a", compiler_params=_params("parallel"))(*_in_hbm(*outs, *lses))


def _pairs_to_tokens(a):
    _, rows, _ = a.shape
    tm = 256

    def body(a_ref, o_ref):
        o_ref[...] = _get_pairs(a_ref)

    return pl.pallas_call(
        body, out_shape=jax.ShapeDtypeStruct((rows, WIDTH), a.dtype), grid=(rows // tm,),
        in_specs=[pl.BlockSpec((N_PAIRS, tm, LANES), lambda i: (0, i, 0))],
        out_specs=pl.BlockSpec((tm, WIDTH), lambda i: (i, 0)), name="pairs_to_tokens",
        compiler_params=_params("parallel"))(*_in_hbm(a))


def _attn_bwd_prep(dy, y, name):
    rows = dy.shape[0]
    tm = 256

    def body(dy_ref, y_ref, do_ref, dl_ref):
        dyv = dy_ref[...]
        _put_pairs(do_ref, dyv)
        _put_pairs(dl_ref, _seg_sum(dyv * y_ref[...]))

    tok = pl.BlockSpec((tm, WIDTH), lambda i: (i, 0))
    pm = pl.BlockSpec((N_PAIRS, tm, LANES), lambda i: (0, i, 0))
    return pl.pallas_call(
        body, out_shape=[jax.ShapeDtypeStruct((N_PAIRS, rows, LANES), F32)] * 2, grid=(rows // tm,),
        in_specs=[tok, tok], out_specs=[pm, pm], name=name, compiler_params=_params("parallel"))(*_in_hbm(dy, y))


def _tile_gain(g, reps):
    return jnp.tile(g[None, :], (1, reps))


def _local_step(x, p, target, w_in_of, rest_of, small):
    rel_table = small["rel_table"]
    buckets_a = [_band_buckets(blk, d) for blk, d in DILATED]
    buckets_b = _band_buckets(BLK_B, 1)
    bias_a = [_bias_tiles(rel_table, bk, 0, "bias_a").reshape(N_HEADS * bk.shape[0], -1) for bk in buckets_a]
    bias_b = _bias_tiles(rel_table, buckets_b, N_HEADS, "bias_b").reshape(N_HEADS * BLK_B, -1)

    saved = []
    for l in range(DEPTH):
        g_mix, g_ffn, g_ple = (small[n][l][None, :] for n in ("norm_mix_g", "norm_ffn_g", "norm_ple_g"))
        gqa, gka, gqb = (_tile_gain(small[n][l], N_HEADS) for n in ("qnorm_a_g", "knorm_a_g", "qnorm_b_g"))
        gkb = _tile_gain(small["knorm_b_g"][l], N_KV_B)
        sink = jnp.repeat(small["sink_b"][l], BLK_B)[:, None]

        h = _rms_fwd(x, g_mix, "rms_mix")
        w_in = w_in_of(l, h)
        proj = _mm(h, w_in, "nt", F32, "mm_in")
        qa, ka, va, qb, kb, vb = _qknorm_fwd(proj, gqa, gka, gqb, gkb)
        outs, lses = [], []
        for (blk, d), bias in zip(DILATED, bias_a):
            o, ls = _attn_fwd(qa, ka, va, bias, None, blk, d, f"attn_a{d}_fwd")
            outs.append(o)
            lses.append(ls)
        ya, lse_a = _combine_patterns(outs, lses)
        yb, lse_b = _attn_fwd(qb, kb, vb, bias_b, sink, BLK_B, 1, "attn_b_fwd")
        yb = _pairs_to_tokens(yb)
        w = dict(rest_of(l, yb), w_in=w_in)
        def gate(products, extra):
            (ca_, cb_), (ga_, gb_) = products, extra
            return _sigmoid(ga_) * ca_ + _sigmoid(gb_) * cb_, ca_, cb_

        merged, ca, cb = _mm_fused([(ya, w["w_branch_a"], "nn"), (yb, w["w_branch_b"], "nn")],
                                   [(proj, OFF_GA), (proj, OFF_GB)], gate, [BF16, BF16, BF16], "mm_branches_gate")
        x1 = _mm(merged, w["w_out"], "nn", F32, "mm_out", res=x)

        h2 = _rms_fwd(x1, g_ffn, "rms_ffn")

        def swiglu(products, extra):
            a_, u_ = products
            return (a_ * _sigmoid(a_)) * u_, a_, u_

        hid, a, u = _mm_fused([(h2, w["w_ffn_gate"], "nt"), (h2, w["w_ffn_up"], "nt")], [], swiglu,
                              [BF16, BF16, BF16], "mm_ffn_gate_up")
        x2 = _mm(hid, w["w_ffn_down"], "nn", F32, "mm_ffn_down", res=x1)

        h3 = _rms_fwd(x2, g_ple, "rms_ple")

        def ple(products, extra):
            z_, e_ = products
            return extra[0] + _sigmoid(z_) * e_, z_, e_

        x3, z, e = _mm_fused([(h3, w["w_ple_gate"], "nn"), (p[l], w["w_ple_proj"], "nn")], [(x2, 0)], ple,
                             [F32, BF16, BF16], "mm_ple")
        saved.append(dict(w=w, x0=x, h=h, proj=proj, qa=qa, ka=ka, va=va, qb=qb, kb=kb, vb=vb, ya=ya, lse_a=lse_a,
                          yb=yb, lse_b=lse_b, ca=ca, cb=cb, merged=merged, x1=x1, h2=h2, a=a, u=u, hid=hid,
                          x2=x2, h3=h3, z=z, e=e))
        x = x3

    dx, loss_acc = _loss_grad(x, target)
    loss = loss_acc[0, 0]

    gbig = [{} for _ in range(DEPTH)]
    marks = [{} for _ in range(DEPTH)]
    gsmall = {n: [None] * DEPTH for n in SMALL if n != "rel_table"}
    dtable_a = jnp.zeros((N_HEADS, NUM_BUCKETS), F32)
    dtable_b = jnp.zeros((N_HEADS, NUM_BUCKETS), F32)

    for l in reversed(range(DEPTH)):
        sv = saved[l]
        w = sv["w"]
        g_mix, g_ffn, g_ple = (small[n][l][None, :] for n in ("norm_mix_g", "norm_ffn_g", "norm_ple_g"))
        gqa, gka, gqb = (_tile_gain(small[n][l], N_HEADS) for n in ("qnorm_a_g", "knorm_a_g", "qnorm_b_g"))
        gkb = _tile_gain(small["knorm_b_g"][l], N_KV_B)
        sink = jnp.repeat(small["sink_b"][l], BLK_B)[:, None]

        def ple_bwd(dx_, z_, e_):
            s = _sigmoid(z_.astype(F32))
            return dx_ * s, dx_ * e_.astype(F32) * (s * (1.0 - s))

        de, dz = _ew(ple_bwd, [(dx, 0), (sv["z"], 0), (sv["e"], 0)], [BF16, BF16], width=D_MODEL, bw=D_MODEL,
                     name="ple_bwd")
        gbig[l]["w_ple_proj"] = _mm(p[l], de, "tn", F32, "mm_d_ple_proj")
        gbig[l]["w_ple_gate"] = _mm(sv["h3"], dz, "tn", F32, "mm_d_ple_gate")
        dh3 = _mm(dz, w["w_ple_gate"], "nt", F32, "mm_dh3")
        dx, dxb, gsmall["norm_ple_g"][l] = _rms_bwd(sv["x2"], g_ple, dh3, dx, "rms_ple_bwd")

        gbig[l]["w_ffn_down"] = _mm(sv["hid"], dxb, "tn", F32, "mm_d_ffn_down")

        def swiglu_bwd(products, extra):
            dh_, a_, u_ = products[0], extra[0].astype(F32), extra[1].astype(F32)
            s = _sigmoid(a_)
            return dh_ * u_ * (s * (1.0 + a_ * (1.0 - s))), dh_ * (a_ * s)

        da, du = _mm_fused([(dxb, w["w_ffn_down"], "nt")], [(sv["a"], 0), (sv["u"], 0)], swiglu_bwd, [BF16, BF16],
                           "mm_dhid_swiglu_bwd")
        gbig[l]["w_ffn_gate"] = _mm(da, sv["h2"], "tn", F32, "mm_d_ffn_gate")
        gbig[l]["w_ffn_up"] = _mm(du, sv["h2"], "tn", F32, "mm_d_ffn_up")
        dh2 = _mm(da, w["w_ffn_gate"], "nn", F32, "mm_dh2_gate")
        dh2 = _mm(du, w["w_ffn_up"], "nn", F32, "mm_dh2_up", res=dh2)
        dx, dxb, gsmall["norm_ffn_g"][l] = _rms_bwd(sv["x1"], g_ffn, dh2, dx, "rms_ffn_bwd")

        gbig[l]["w_out"] = _mm(sv["merged"], dxb, "tn", F32, "mm_d_out")

        def gate_bwd(products, extra):
            dm_, ca_, cb_ = products[0], extra[0].astype(F32), extra[1].astype(F32)
            sa, sb = _sigmoid(extra[2]), _sigmoid(extra[3])
            return dm_ * sa, dm_ * sb, dm_ * ca_ * (sa * (1.0 - sa)), dm_ * cb_ * (sb * (1.0 - sb))

        dca, dcb, dga, dgb = _mm_fused(
            [(dxb, w["w_out"], "nt")], [(sv["ca"], 0), (sv["cb"], 0), (sv["proj"], OFF_GA), (sv["proj"], OFF_GB)],
            gate_bwd, [BF16, BF16, BF16, BF16], "mm_dmerged_gate_bwd")
        gbig[l]["w_branch_a"] = _mm(sv["ya"], dca, "tn", F32, "mm_d_branch_a")
        gbig[l]["w_branch_b"] = _mm(sv["yb"], dcb, "tn", F32, "mm_d_branch_b")
        dya = _mm(dca, w["w_branch_a"], "nt", F32, "mm_dya")
        dyb = _mm(dcb, w["w_branch_b"], "nt", F32, "mm_dyb")

        dya, delta_a = _attn_bwd_prep(dya, sv["ya"], "attn_a_bwd_prep")
        dyb, delta_b = _attn_bwd_prep(dyb, sv["yb"], "attn_b_bwd_prep")

        dqa, dka, dva = [], [], []
        for (blk, d), bias, bk in zip(DILATED, bias_a, buckets_a):
            dq_, dk_, dv_, db_ = _attn_bwd(sv["qa"], sv["ka"], sv["va"], dya, sv["lse_a"], delta_a, bias, None, blk, d,
                                           f"attn_a{d}_bwd")
            dqa.append(dq_)
            dka.append(dk_)
            dva.append(dv_)
            dtable_a = dtable_a + _table_grad(db_.reshape(N_HEADS, blk, 3 * blk), bk, "table_grad_a")
        dqb, dkb, dvb, db_, dsink = _attn_bwd(sv["qb"], sv["kb"], sv["vb"], dyb, sv["lse_b"], delta_b, bias_b, sink,
                                              BLK_B, 1, "attn_b_bwd")
        dtable_b = dtable_b + _table_grad(db_.reshape(N_HEADS, BLK_B, 3 * BLK_B), buckets_b, "table_grad_b")
        gsmall["sink_b"][l] = dsink.reshape(N_HEADS, BLK_B).sum(axis=1)

        dproj, pqa, pka, pqb, pkb = _qknorm_bwd(sv["proj"], gqa, gka, gqb, gkb, dqa, dka, dva, dqb, dkb, dvb, dga, dgb)
        marks[l]["attn_bwd_done"] = dproj
        gsmall["qnorm_a_g"][l] = pqa.reshape(N_HEADS, HEAD_DIM).sum(0)
        gsmall["knorm_a_g"][l] = pka.reshape(N_HEADS, HEAD_DIM).sum(0)
        gsmall["qnorm_b_g"][l] = pqb.reshape(N_HEADS, HEAD_DIM).sum(0)
        gsmall["knorm_b_g"][l] = pkb.reshape(N_KV_B, HEAD_DIM).sum(0)
        gbig[l]["w_in"] = _mm(dproj, sv["h"], "tn", F32, "mm_d_in")
        dh = _mm(dproj, w["w_in"], "nn", F32, "mm_dh")
        dx, _, gsmall["norm_mix_g"][l] = _rms_bwd(sv["x0"], g_mix, dh, dx, "rms_mix_bwd")
        gsmall["norm_mix_g"][l] = gsmall["norm_mix_g"][l][0]
        gsmall["norm_ffn_g"][l] = gsmall["norm_ffn_g"][l][0]
        gsmall["norm_ple_g"][l] = gsmall["norm_ple_g"][l][0]

    gsmall = {n: jnp.stack(v) for n, v in gsmall.items()}
    gsmall["rel_table"] = jnp.concatenate([dtable_a, dtable_b], axis=0).T
    return loss, dx, gbig, gsmall, marks


def _place():
    return lax.axis_index("x"), lax.axis_index("y"), lax.axis_index("c")


def _flip(v, bit):
    return 1 - v if bit else v


CHIP_RELATIONS = ((0, 1), (1, 0), (1, 1))
ANY = pl.BlockSpec(memory_space=pl.ANY)


def _allgather_body(w_refs, out_refs, send_sems, recv_sems):
    x, y, c = _place()
    chips = [(_flip(x, a), _flip(y, b)) for a, b in CHIP_RELATIONS]

    def make(g):
        w_ref, out_ref = w_refs[g], out_refs[g]
        half = w_ref.shape[0] // 2

        def part(px, py, pc):
            return out_ref.at[2 * px + py, pl.ds(pc * half, half), :]

        def copy(k, block, to, src=None):
            return pltpu.make_async_remote_copy(
                src_ref=part(*block) if src is None else src, dst_ref=part(*block),
                send_sem=send_sems.at[7 * g + k], recv_sem=recv_sems.at[7 * g + k], device_id=to,
                device_id_type=MESH_ID)

        own = pltpu.make_async_remote_copy(
            src_ref=w_ref, dst_ref=out_ref.at[2 * x + y], send_sem=send_sems.at[7 * g + 6],
            recv_sem=recv_sems.at[7 * g + 6], device_id=(x, y, 1 - c), device_id_type=MESH_ID)
        first = [copy(k, (x, y, c), (*chip, c), src=w_ref.at[pl.ds(c * half, half), :]) for k, chip in enumerate(chips)]
        passed = [copy(3 + k, (*chip, c), (x, y, 1 - c)) for k, chip in enumerate(chips)]
        arrive = [copy(k, (*chip, c), (x, y, c)) for k, chip in enumerate(chips)]
        arrive2 = [copy(3 + k, (*chip, 1 - c), (x, y, c)) for k, chip in enumerate(chips)]
        return own, first, passed, arrive, arrive2

    made = [make(g) for g in range(len(w_refs))]
    for own, first, _, _, _ in made:
        own.start()
        for cp in first:
            cp.start()
    for _, _, passed, arrive, _ in made:
        for k in range(3):
            arrive[k].wait_recv()
            passed[k].start()
    for own, first, passed, _, arrive2 in made:
        for k in range(3):
            arrive2[k].wait_recv()
        own.wait_recv()
        for cp in first + passed + [own]:
            cp.wait_send()


def _sibling(x, y, c):
    return [(x, y, 1 - c)]


def _same_core_of_other_chips(x, y, c):
    return [(_flip(x, a), _flip(y, b), c) for a, b in CHIP_RELATIONS]


def _exchange(body, ins, out_types, n_sems, name, sequencer=None):
    n = len(ins)
    sems = (pltpu.SemaphoreType.DMA((n_sems,)), pltpu.SemaphoreType.DMA((n_sems,)))
    if sequencer is None:
        in_place = out_types is None
        out_shape = [jax.ShapeDtypeStruct(a.shape, a.dtype) for a in ins] if in_place else out_types

        def tc_body(*refs):
            body(refs[:n], refs[n:n + len(out_shape)], refs[-2], refs[-1])

        return list(pl.pallas_call(
            tc_body, out_shape=out_shape, in_specs=[ANY] * n, out_specs=[ANY] * len(out_shape),
            input_output_aliases={g: g for g in range(n)} if in_place else {}, scratch_shapes=list(sems), name=name)(*ins))

    collective_id, peers = sequencer
    hbm = pltpu.MemorySpace.HBM
    in_refs = [jax.new_ref(a, memory_space=hbm) for a in ins]
    out_refs = in_refs if out_types is None else [jax.empty_ref(t, memory_space=hbm) for t in out_types]

    @pl.kernel(mesh=plsc.ScalarSubcoreMesh(axis_name="sequencer", num_cores=1), name=name, scratch_types=sems,
               compiler_params=pltpu.CompilerParams(collective_id=collective_id))
    def launch(send_sems, recv_sems):
        barrier = pltpu.get_barrier_semaphore()
        devices = peers(*_place())
        for device in devices:
            pl.semaphore_signal(barrier, inc=1, device_id=device, device_id_type=MESH_ID)
        pl.semaphore_wait(barrier, len(devices))
        body(in_refs, out_refs, send_sems, recv_sems)

    launch()
    return [r[...] for r in out_refs]


def _allgather(shards, name, sequencer=None):
    out_types = [jax.ShapeDtypeStruct((N_CHIPS,) + s.shape, s.dtype) for s in shards]
    if sequencer is not None:
        sequencer = (sequencer, lambda x, y, c: _sibling(x, y, c) + _same_core_of_other_chips(x, y, c))
    return _exchange(_allgather_body, shards, out_types, 7 * len(shards), name, sequencer)


def _half_tile(half):
    return max(t for t in range(16, 1025, 16) if half % t == 0)


def _run_copies(cps):
    for cp in cps:
        cp.start()
    for cp in cps:
        cp.wait_recv()
    for cp in cps:
        cp.wait_send()


def _sibling_halves(gsends, name, sequencer=None):
    def body(g_refs, out_refs, send_sems, recv_sems):
        x, y, c = _place()
        cps = []
        for g, (g_ref, out_ref) in enumerate(zip(g_refs, out_refs)):
            half = g_ref.shape[1] // 2
            cps.append(pltpu.make_async_remote_copy(
                src_ref=g_ref.at[:, pl.ds((1 - c) * half, half), :], dst_ref=out_ref,
                send_sem=send_sems.at[g], recv_sem=recv_sems.at[g], device_id=(x, y, 1 - c), device_id_type=MESH_ID))
        _run_copies(cps)

    out_types = [jax.ShapeDtypeStruct((s.shape[0], s.shape[1] // 2, s.shape[2]), s.dtype) for s in gsends]
    return _exchange(body, gsends, out_types, len(gsends), name, sequencer and (sequencer, _sibling))


def _chip_sums(gsend, sib, place):
    n, rows, cols = gsend.shape
    half = rows // 2
    tm = _half_tile(half)
    nblk = half // tm

    def body(s_ref, g_ref, sib_ref, o_ref):
        o_ref[0] = (g_ref[0].astype(F32) + sib_ref[0].astype(F32)).astype(o_ref.dtype)

    grid_spec = pltpu.PrefetchScalarGridSpec(
        num_scalar_prefetch=1, grid=(n, nblk),
        in_specs=[pl.BlockSpec((1, tm, cols), lambda k, i, s: (jnp.bitwise_xor(s[0], k), s[1] * nblk + i, 0)),
                  pl.BlockSpec((1, tm, cols), lambda k, i, s: (jnp.bitwise_xor(s[0], k), i, 0))],
        out_specs=pl.BlockSpec((1, tm, cols), lambda k, i, s: (k, i, 0)))
    return pl.pallas_call(
        body, out_shape=jax.ShapeDtypeStruct((n, half, cols), BF16), grid_spec=grid_spec,
        name="rs_chip_sums", compiler_params=_params("parallel", "parallel"))(place, gsend, sib)


def _exchange_chip_sums(tsends, name, sequencer=None):
    def body(t_refs, out_refs, send_sems, recv_sems):
        x, y, c = _place()
        cps = []
        for g, (t_ref, out_ref) in enumerate(zip(t_refs, out_refs)):
            for k, device in enumerate(_same_core_of_other_chips(x, y, c)):
                cps.append(pltpu.make_async_remote_copy(
                    src_ref=t_ref.at[k + 1], dst_ref=out_ref.at[k], send_sem=send_sems.at[3 * g + k],
                    recv_sem=recv_sems.at[3 * g + k], device_id=device, device_id_type=MESH_ID))
        _run_copies(cps)

    out_types = [jax.ShapeDtypeStruct((3,) + s.shape[1:], s.dtype) for s in tsends]
    return _exchange(body, tsends, out_types, 3 * len(tsends), name,
                     sequencer and (sequencer, _same_core_of_other_chips))


def _final_sum(tsend, recv, place):
    n, half, cols = tsend.shape
    tm = _half_tile(half)
    nblk = half // tm

    def body(s_ref, t_ref, r_ref, o_ref):
        o_ref[...] = ((t_ref[0].astype(F32) + r_ref[0].astype(F32)) + r_ref[1].astype(F32)) + r_ref[2].astype(F32)

    grid_spec = pltpu.PrefetchScalarGridSpec(
        num_scalar_prefetch=1, grid=(nblk,),
        in_specs=[pl.BlockSpec((1, tm, cols), lambda i, s: (0, i, 0)), pl.BlockSpec((n - 1, tm, cols), lambda i, s: (0, i, 0))],
        out_specs=pl.BlockSpec((tm, cols), lambda i, s: (s[1] * nblk + i, 0)))
    return pl.pallas_call(
        body, out_shape=jax.ShapeDtypeStruct((2 * half, cols), F32), grid_spec=grid_spec, name="rs_final_sum",
        compiler_params=_params("parallel"))(place, tsend, recv)


def _join_halves(gfulls, name, sequencer=None):
    def body(g_refs, out_refs, send_sems, recv_sems):
        x, y, c = _place()
        n = len(g_refs)

        def copy(g, pc):
            half = g_refs[g].shape[0] // 2
            return pltpu.make_async_remote_copy(
                src_ref=g_refs[g].at[pl.ds(pc * half, half), :], dst_ref=out_refs[g].at[pl.ds(pc * half, half), :],
                send_sem=send_sems.at[g], recv_sem=recv_sems.at[g], device_id=(x, y, 1 - c), device_id_type=MESH_ID)

        mine = [copy(g, c) for g in range(n)]
        for cp in mine:
            cp.start()
        for g in range(n):
            copy(g, 1 - c).wait_recv()
        for cp in mine:
            cp.wait_send()

    return _exchange(body, gfulls, None, len(gfulls), name, sequencer and (sequencer, _sibling))


def _allreduce_small(v):
    rows, cols = v.shape

    def body(v_ref, out_ref, buf, send_sems, recv_sems):
        x, y, c = _place()
        cps = []
        for k in range(1, 8):
            peer = (_flip(x, (k >> 2) & 1), _flip(y, (k >> 1) & 1), _flip(c, k & 1))
            cps.append(pltpu.make_async_remote_copy(
                src_ref=v_ref, dst_ref=buf.at[k - 1], send_sem=send_sems.at[k - 1], recv_sem=recv_sems.at[k - 1],
                device_id=peer, device_id_type=MESH_ID))
        for cp in cps:
            cp.start()
        for cp in cps:
            cp.wait_recv()
        for cp in cps:
            cp.wait_send()
        t0 = v_ref[...] + buf[0]
        t1 = buf[1] + buf[2]
        t2 = buf[3] + buf[4]
        t3 = buf[5] + buf[6]
        out_ref[...] = (t0 + t1) + (t2 + t3)

    vm = pl.BlockSpec(memory_space=pltpu.VMEM)
    return pl.pallas_call(
        body, out_shape=jax.ShapeDtypeStruct((rows, cols), F32), in_specs=[vm], out_specs=vm,
        scratch_shapes=[pltpu.VMEM((7, rows, cols), F32), pltpu.SemaphoreType.DMA((7,)), pltpu.SemaphoreType.DMA((7,))],
        name="allreduce_small")(v)


BIG_INFO = {n: (shape, ax) for n, shape, ax in BIG}
GROUPS = (("w_in",), ("w_ffn_gate", "w_ffn_up", "w_ffn_down", "w_out", "w_ple_gate"),
          ("w_branch_a", "w_branch_b", "w_ple_proj"))


def _shard_shape(name):
    (k, m), ax = BIG_INFO[name]
    return (k // N_CHIPS, m) if ax == 0 else (k, m // N_CHIPS)


def _group_rows(group):
    offs, off = {}, 0
    for n in group:
        offs[n] = off
        off += _shard_shape(n)[0]
    return offs, off


def _pack_groups(shards, layer, dtype):
    return [jnp.concatenate([shards[n][layer].astype(dtype) for n in group], axis=0) for group in GROUPS]


def _unpack_full(gathered, groups):
    out = {}
    for group, arr in zip(groups, gathered):
        offs, _ = _group_rows(group)
        for n in group:
            rows, cols = _shard_shape(n)
            (k, m), ax = BIG_INFO[n]
            slab = arr[:, offs[n]:offs[n] + rows]
            out[n] = slab.reshape(k, m) if ax == 0 else jnp.transpose(slab, (1, 0, 2)).reshape(k, m)
    return out


def _pack_grads(gfull):
    out = []
    for group in GROUPS:
        parts = []
        for n in group:
            rows, cols = _shard_shape(n)
            ax = BIG_INFO[n][1]
            slab = (gfull[n].reshape(N_CHIPS, rows, cols) if ax == 0
                    else jnp.transpose(gfull[n].reshape(rows, N_CHIPS, cols), (1, 0, 2)))
            parts.append(slab.astype(BF16))
        out.append(jnp.concatenate(parts, axis=1))
    return out


def _after(values, mark):
    values, _ = lax.optimization_barrier((values, mark))
    return values


def _reduce_scatter_begin(gsends, place, tag, ids):
    sibs = _sibling_halves(gsends, "rs_sibling_halves_" + tag, ids[0])
    tsends = [_chip_sums(g, s, place) for g, s in zip(gsends, sibs)]
    return tsends, _exchange_chip_sums(tsends, "rs_exchange_" + tag, ids[1])


def _reduce_scatter_finish(begun, place, tag, ids, hold):
    tsends, recvs = begun
    recvs = _after(recvs, hold)
    return _join_halves([_final_sum(t, r, place) for t, r in zip(tsends, recvs)], "rs_join_halves_" + tag, ids[2])


SMALL_SHAPES = {"rel_table": (NUM_BUCKETS, 2 * N_HEADS), "norm_mix_g": (DEPTH, D_MODEL), "qnorm_a_g": (DEPTH, HEAD_DIM),
                "knorm_a_g": (DEPTH, HEAD_DIM), "qnorm_b_g": (DEPTH, HEAD_DIM), "knorm_b_g": (DEPTH, HEAD_DIM),
                "sink_b": (DEPTH, N_HEADS), "norm_ffn_g": (DEPTH, D_MODEL), "norm_ple_g": (DEPTH, D_MODEL)}


def _pack_small(vals):
    flat = jnp.concatenate([vals[n].astype(F32).reshape(-1) for n in SMALL])
    flat = jnp.concatenate([flat, jnp.zeros((SMALL_ROWS * LANES - flat.shape[0],), F32)])
    return flat.reshape(SMALL_ROWS, LANES)


def _unpack_small(packed):
    flat, out, off = packed.reshape(-1), {}, 0
    for n in SMALL:
        size = math.prod(SMALL_SHAPES[n])
        out[n] = flat[off:off + size].reshape(SMALL_SHAPES[n])
        off += size
    return out


def _adamw(w, gs, g_row, m, v, name):
    c1 = 1.0 - ADAM_B1 ** ADAM_STEP
    c2 = 1.0 - ADAM_B2 ** ADAM_STEP
    total, width = w.shape
    n_layers = len(gs)
    per = total // n_layers
    tm = max(t for t in range(8, 513, 8) if per % t == 0 and g_row % t == 0)
    nblk = per // tm

    def body(*refs):
        w_ref, g_refs = refs[0], refs[1:1 + n_layers]
        m_ref, v_ref, og, od, om, ov = refs[1 + n_layers:]
        layer = pl.program_id(0) // nblk
        g = g_refs[0][...]
        for l in range(1, n_layers):
            g = jnp.where(layer == l, g_refs[l][...], g)
        m_new = ADAM_B1 * m_ref[...] + (1.0 - ADAM_B1) * g
        v_new = ADAM_B2 * v_ref[...] + (1.0 - ADAM_B2) * (g * g)
        og[...] = g
        od[...] = -ADAM_LR * ((m_new / c1) / (jnp.sqrt(v_new / c2) + ADAM_EPS) + ADAM_WD * w_ref[...])
        om[...] = m_new
        ov[...] = v_new

    row = pl.BlockSpec((tm, width), lambda i: (i, 0))
    g_specs = [pl.BlockSpec((tm, width), lambda i, l=l: (g_row // tm + jnp.clip(i - l * nblk, 0, nblk - 1), 0))
               for l in range(n_layers)]
    return pl.pallas_call(
        body, out_shape=[jax.ShapeDtypeStruct((total, width), F32)] * 4, grid=(total // tm,),
        in_specs=[row] + g_specs + [row, row], out_specs=[row] * 4, name=name,
        compiler_params=_params("parallel"))(*_in_hbm(w, *gs, m, v))


def kernel(x, p, rel_table, norm_mix_g, w_in, qnorm_a_g, knorm_a_g, qnorm_b_g, knorm_b_g, sink_b, w_branch_a, w_branch_b, w_out, norm_ffn_g, w_ffn_gate, w_ffn_up, w_ffn_down, norm_ple_g, w_ple_gate, w_ple_proj, loss_target, m_rel_table, m_norm_mix_g, m_w_in, m_qnorm_a_g, m_knorm_a_g, m_qnorm_b_g, m_knorm_b_g, m_sink_b, m_w_branch_a, m_w_branch_b, m_w_out, m_norm_ffn_g, m_w_ffn_gate, m_w_ffn_up, m_w_ffn_down, m_norm_ple_g, m_w_ple_gate, m_w_ple_proj, v_rel_table, v_norm_mix_g, v_w_in, v_qnorm_a_g, v_knorm_a_g, v_qnorm_b_g, v_knorm_b_g, v_sink_b, v_w_branch_a, v_w_branch_b, v_w_out, v_norm_ffn_g, v_w_ffn_gate, v_w_ffn_up, v_w_ffn_down, v_norm_ple_g, v_w_ple_gate, v_w_ple_proj):
    given = dict(locals())

    def held(name, a):
        return jnp.swapaxes(a, 1, 2) if name in TRANSPOSED else a

    weights = {n: held(n, given[n]) for n in WEIGHTS}
    moments_m = {n: held(n, given["m_" + n]) for n in WEIGHTS}
    moments_v = {n: held(n, given["v_" + n]) for n in WEIGHTS}
    xi, yi, ci = _place()
    place = jnp.stack([2 * xi + yi, ci]).astype(jnp.int32)

    shards = [_pack_groups(weights, l, BF16) for l in range(DEPTH)]
    w_in0 = _allgather(shards[0][:1], "allgather_w_in_layer0", sequencer=9)
    rest0 = _allgather(_after(shards[0][1:], w_in0), "allgather_rest_layer0", sequencer=1)
    gathered = [w_in0 + rest0, None]
    small = {n: weights[n] for n in SMALL}

    def w_in_of(l, mark):
        return _unpack_full(gathered[l][:1] if l == 0 else _after(gathered[l][:1], mark), GROUPS[:1])["w_in"]

    def rest_of(l, mark):
        if l == 0:
            gathered[1] = _allgather(_after(shards[1], mark), "allgather_layer1", sequencer=2)
        return _unpack_full(_after(gathered[l][1:], mark), GROUPS[1:])

    loss, dx, gbig, gsmall, marks = _local_step(x[0], p[:, 0], loss_target[0], w_in_of, rest_of, small)

    gsends = [_pack_grads(gbig[l]) for l in range(DEPTH)]
    stages = {"layer1": (gsends[1], (3, 4, 5)), "rest_layer0": (gsends[0][1:], (6, 7, 8)),
              "w_in_layer0": (gsends[0][:1], (10, 11, 12))}
    begun = {tag: _reduce_scatter_begin(g, place, tag, ids) for tag, (g, ids) in stages.items()}

    def finish(tag, hold):
        return _reduce_scatter_finish(begun[tag], place, tag, stages[tag][1], hold)

    red1 = finish("layer1", marks[0]["attn_bwd_done"])
    rest0 = finish("rest_layer0", marks[0]["attn_bwd_done"])

    grads, delta, new_m, new_v = {}, {}, {}, {}

    def update(group, reduced):
        offs, _ = _group_rows(group)
        for n in group:
            shape = weights[n].shape
            two_d = lambda a: a.reshape(shape[0] * shape[1], shape[2])
            outs = _adamw(two_d(weights[n]), reduced, offs[n], two_d(moments_m[n]), two_d(moments_v[n]), "adamw_" + n)
            grads[n], delta[n], new_m[n], new_v[n] = (held(n, o.reshape(shape)) for o in outs)

    for gi in (1, 2):
        update(GROUPS[gi], _after([rest0[gi - 1], red1[gi]], begun["w_in_layer0"][0]))
    others_done = [dx] + [delta[n] for gi in (1, 2) for n in GROUPS[gi]]
    update(GROUPS[0], [finish("w_in_layer0", others_done)[0], red1[0]])
    small_grads = _allreduce_small(_pack_small(gsmall))
    g_, d_, m_, v_ = _adamw(_pack_small(weights), [small_grads], 0, _pack_small(moments_m), _pack_small(moments_v),
                            "adamw_small")
    grads.update(_unpack_small(g_))
    delta.update(_unpack_small(d_))
    new_m.update(_unpack_small(m_))
    new_v.update(_unpack_small(v_))

    loss = lax.psum(loss, ("x", "y", "c"))
    return (loss, dx[None], *[grads[n] for n in WEIGHTS], *[delta[n] for n in WEIGHTS],
            *[new_m[n] for n in WEIGHTS], *[new_v[n] for n in WEIGHTS])
```

```python
import functools
import math

import jax
import jax.numpy as jnp
from jax import lax
from jax.experimental import pallas as pl
from jax.experimental.pallas import tpu as pltpu
from jax.experimental.pallas import tpu_sc as plsc

F32 = jnp.float32
BF16 = jnp.bfloat16
MESH_ID = pl.DeviceIdType.MESH

SEQ = 2048
D_MODEL = 1024
DEPTH = 2
HEAD_DIM = 64
N_HEADS = 8
WIDTH = N_HEADS * HEAD_DIM
N_PAIRS = 4
ITEMS = 4
N_KV_B = 2
PLE_DIM = 256
D_FF = 2816
D_IN = 4352
OFF_QA, OFF_KA, OFF_VA, OFF_QB, OFF_KB, OFF_VB, OFF_GA, OFF_GB = 0, 512, 1024, 1536, 2048, 2176, 2304, 3328
DILATED = ((64, 1), (64, 4), (64, 16))
BLK_B = 128
NUM_BUCKETS = 32
MAX_DISTANCE = 1024
RMS_EPS = 1e-6
NEG_INF = -1e30
LANES = 128
VMEM_LIMIT = 48 * 1024 * 1024

ADAM_LR, ADAM_B1, ADAM_B2, ADAM_EPS, ADAM_WD, ADAM_STEP = 0.001, 0.9, 0.999, 1e-08, 0.01, 10

TRANSPOSED = ("w_in", "w_ffn_gate", "w_ffn_up")
BIG = (
    ("w_in", (D_IN, D_MODEL), 0),
    ("w_branch_a", (WIDTH, D_MODEL), 1),
    ("w_branch_b", (WIDTH, D_MODEL), 1),
    ("w_out", (D_MODEL, D_MODEL), 0),
    ("w_ffn_gate", (D_FF, D_MODEL), 0),
    ("w_ffn_up", (D_FF, D_MODEL), 0),
    ("w_ffn_down", (D_FF, D_MODEL), 0),
    ("w_ple_gate", (D_MODEL, D_MODEL), 0),
    ("w_ple_proj", (PLE_DIM, D_MODEL), 1),
)
SMALL = ("rel_table", "norm_mix_g", "qnorm_a_g", "knorm_a_g", "qnorm_b_g", "knorm_b_g", "sink_b",
         "norm_ffn_g", "norm_ple_g")
WEIGHTS = ("rel_table", "norm_mix_g", "w_in", "qnorm_a_g", "knorm_a_g", "qnorm_b_g", "knorm_b_g", "sink_b",
           "w_branch_a", "w_branch_b", "w_out", "norm_ffn_g", "w_ffn_gate", "w_ffn_up", "w_ffn_down",
           "norm_ple_g", "w_ple_gate", "w_ple_proj")
N_CHIPS = 4
SMALL_ROWS = 64


def _params(*sem):
    return pltpu.CompilerParams(dimension_semantics=sem, vmem_limit_bytes=VMEM_LIMIT)


def _pick(dim, target):
    for t in (target, 512, 256, 128, 64, 32, 16, 8):
        if t <= target and dim % t == 0:
            return t
    return dim


MM_VMEM_BUDGET = 40 * 1024 * 1024
STEP_OVERHEAD_S = 0.4e-6
TILE_DMA_BYTES_PER_S = 1.5e12


def _mm_dims(a, b, mode):
    if mode == "nn":
        return a.shape[0], b.shape[1], a.shape[1]
    if mode == "nt":
        return a.shape[0], b.shape[0], a.shape[1]
    return a.shape[1], b.shape[1], a.shape[0]


def _mm_tiles(m, n, pairs, tile_bytes, col_offsets):
    best = None
    for tm in (t for t in range(LANES, m + 1, LANES) if m % t == 0):
        for tn in (t for t in range(LANES, n + 1, LANES) if n % t == 0 and all(o % t == 0 for o in col_offsets)):
            io = sum(tm * k * ab + tn * k * bb for k, ab, bb in pairs) + tm * tn * sum(tile_bytes)
            casts = sum((tm * k * 2 if ab == 4 else 0) + (tn * k * 2 if bb == 4 else 0) for k, ab, bb in pairs)
            if 2 * io + len(pairs) * tm * tn * 4 + casts > MM_VMEM_BUDGET:
                continue
            cost = (m // tm) * (n // tn) * STEP_OVERHEAD_S + io / TILE_DMA_BYTES_PER_S
            if best is None or (cost, -tm) < best[0]:
                best = ((cost, -tm), tm, tn)
    return best[1], best[2]


def _mm_fused(pairs, extras, epilogue, out_dtypes, name):
    m, n, _ = _mm_dims(*pairs[0])
    assert all(_mm_dims(*p)[:2] == (m, n) for p in pairs)
    tm, tn = _mm_tiles(
        m, n, [(_mm_dims(a, b, mode)[2], a.dtype.itemsize, b.dtype.itemsize) for a, b, mode in pairs],
        [e.dtype.itemsize for e, _ in extras] + [jnp.dtype(d).itemsize for d in out_dtypes], [off for _, off in extras])
    dims = {"nn": (((1,), (0,)), ((), ())), "nt": (((1,), (1,)), ((), ())), "tn": (((0,), (0,)), ((), ()))}
    in_specs, args = [], []
    for a, b, mode in pairs:
        k = _mm_dims(a, b, mode)[2]
        in_specs.append(pl.BlockSpec((k, tm), lambda i, j: (0, i)) if mode == "tn" else pl.BlockSpec((tm, k), lambda i, j: (i, 0)))
        in_specs.append(pl.BlockSpec((tn, k), lambda i, j: (j, 0)) if mode == "nt" else pl.BlockSpec((k, tn), lambda i, j: (0, j)))
        args += [a, b]
    for e, off in extras:
        in_specs.append(pl.BlockSpec((tm, tn), lambda i, j, o=off // tn: (i, o + j)))
        args.append(e)
    n_pairs, n_in = len(pairs), 2 * len(pairs) + len(extras)

    def body(*refs):
        products = [lax.dot_general(refs[2 * p][...].astype(BF16), refs[2 * p + 1][...].astype(BF16), dims[pairs[p][2]],
                                    preferred_element_type=F32) for p in range(n_pairs)]
        outs = epilogue(products, [r[...] for r in refs[2 * n_pairs:n_in]])
        for r, o in zip(refs[n_in:], outs):
            r[...] = o.astype(r.dtype)

    tile = pl.BlockSpec((tm, tn), lambda i, j: (i, j))
    return pl.pallas_call(
        body, out_shape=[jax.ShapeDtypeStruct((m, n), d) for d in out_dtypes], grid=(m // tm, n // tn),
        in_specs=in_specs, out_specs=[tile] * len(out_dtypes), name=name,
        compiler_params=_params("parallel", "parallel"))(*args)


def _mm(a, b, mode, out_dtype, name, res=None):
    if res is None:
        return _mm_fused([(a, b, mode)], [], lambda products, extra: products, [out_dtype], name)[0]
    return _mm_fused([(a, b, mode)], [(res, 0)], lambda products, extra: [products[0] + extra[0]], [out_dtype], name)[0]


def _ew(fn, ins, out_dtypes, *, width, bw, name, vecs=(), tm=256):
    rows = ins[0][0].shape[0]
    tm = _pick(rows, tm)
    n_in = len(ins) + len(vecs)

    def col_map(off_blocks):
        return lambda i, j: (i, off_blocks + j)

    in_specs = [pl.BlockSpec((tm, bw), col_map(off // bw)) for _, off in ins]
    in_specs += [pl.BlockSpec((1, bw), lambda i, j: (0, j)) for _ in vecs]

    def body(*refs):
        outs = fn(*[r[...] for r in refs[:n_in]])
        for r, o in zip(refs[n_in:], outs):
            r[...] = o.astype(r.dtype)

    return pl.pallas_call(
        body, out_shape=[jax.ShapeDtypeStruct((rows, width), dt) for dt in out_dtypes],
        grid=(rows // tm, width // bw), in_specs=in_specs,
        out_specs=[pl.BlockSpec((tm, bw), lambda i, j: (i, j)) for _ in out_dtypes],
        name=name, compiler_params=_params("parallel", "parallel"))(*[a for a, _ in ins], *vecs)


def _sigmoid(x):
    return 1.0 / (1.0 + jnp.exp(-x))


def _seg_sum(v):
    outs = []
    for k in range(v.shape[1] // LANES):
        vp = v[:, k * LANES:(k + 1) * LANES]
        left = lax.broadcasted_iota(jnp.int32, vp.shape, 1) < HEAD_DIM
        sl = jnp.sum(jnp.where(left, vp, 0.0), axis=-1, keepdims=True)
        sr = jnp.sum(jnp.where(left, 0.0, vp), axis=-1, keepdims=True)
        outs.append(jnp.where(left, sl, sr))
    return outs[0] if len(outs) == 1 else jnp.concatenate(outs, axis=1)


def _seg_rstd(x):
    return lax.rsqrt(_seg_sum(x * x) * (1.0 / HEAD_DIM) + RMS_EPS)


def _rms_fwd(x, g, name):
    rows, d = x.shape
    tm = 256

    def body(x_ref, g_ref, h_ref):
        xv = x_ref[...]
        r = lax.rsqrt(jnp.mean(xv * xv, axis=-1, keepdims=True) + RMS_EPS)
        h_ref[...] = ((xv * r) * g_ref[...]).astype(BF16)

    return pl.pallas_call(
        body, out_shape=jax.ShapeDtypeStruct((rows, d), BF16), grid=(rows // tm,),
        in_specs=[pl.BlockSpec((tm, d), lambda i: (i, 0)), pl.BlockSpec((1, d), lambda i: (0, 0))],
        out_specs=pl.BlockSpec((tm, d), lambda i: (i, 0)), name=name,
        compiler_params=_params("parallel"))(x, g)


def _rms_bwd(x, g, dh, dres, name):
    rows, d = x.shape
    tm = 256

    def body(x_ref, g_ref, dh_ref, dres_ref, dx_ref, dxb_ref, dg_ref):
        xv = x_ref[...]
        r = lax.rsqrt(jnp.mean(xv * xv, axis=-1, keepdims=True) + RMS_EPS)
        xh = xv * r
        dhv = dh_ref[...]
        dxh = dhv * g_ref[...]
        dxv = dres_ref[...] + r * (dxh - xh * jnp.mean(dxh * xh, axis=-1, keepdims=True))
        dx_ref[...] = dxv
        dxb_ref[...] = dxv.astype(BF16)
        part = jnp.sum(dhv * xh, axis=0, keepdims=True)

        @pl.when(pl.program_id(0) == 0)
        def _():
            dg_ref[...] = part

        @pl.when(pl.program_id(0) > 0)
        def _():
            dg_ref[...] += part

    row = pl.BlockSpec((tm, d), lambda i: (i, 0))
    vec = pl.BlockSpec((1, d), lambda i: (0, 0))
    return pl.pallas_call(
        body, out_shape=[jax.ShapeDtypeStruct((rows, d), F32), jax.ShapeDtypeStruct((rows, d), BF16),
                         jax.ShapeDtypeStruct((1, d), F32)],
        grid=(rows // tm,), in_specs=[row, vec, row, row], out_specs=[row, row, vec],
        name=name, compiler_params=_params("arbitrary"))(x, g, dh, dres)


def _loss_grad(y, t):
    rows, d = y.shape
    tm = 256

    def body(y_ref, t_ref, dy_ref, l_ref):
        e = y_ref[...] - t_ref[...]
        dy_ref[...] = e * (1.0 / d)
        part = jnp.zeros((1, LANES), F32) + jnp.sum(e * e) * (0.5 / d)

        @pl.when(pl.program_id(0) == 0)
        def _():
            l_ref[...] = part

        @pl.when(pl.program_id(0) > 0)
        def _():
            l_ref[...] += part

    row = pl.BlockSpec((tm, d), lambda i: (i, 0))
    return pl.pallas_call(
        body, out_shape=[jax.ShapeDtypeStruct((rows, d), F32), jax.ShapeDtypeStruct((1, LANES), F32)],
        grid=(rows // tm,), in_specs=[row, row], out_specs=[row, pl.BlockSpec((1, LANES), lambda i: (0, 0))],
        name="loss_grad", compiler_params=_params("arbitrary"))(y, t)


def _put_pairs(ref, val):
    for hp in range(N_PAIRS):
        ref[hp] = val[:, hp * LANES:(hp + 1) * LANES].astype(ref.dtype)


def _get_pairs(ref):
    return jnp.concatenate([ref[hp] for hp in range(N_PAIRS)], axis=1)


def _swap_halves(v):
    return pltpu.roll(v, HEAD_DIM, axis=1)


def _expand_kv(kv):
    left = lax.broadcasted_iota(jnp.int32, kv.shape, 1) < HEAD_DIM
    sw = _swap_halves(kv)
    h0 = jnp.where(left, kv, sw)
    h1 = jnp.where(left, sw, kv)
    return jnp.concatenate([h0, h0, h1, h1], axis=1)


def _reduce_kv(dkv):
    left = lax.broadcasted_iota(jnp.int32, (dkv.shape[0], LANES), 1) < HEAD_DIM
    t = dkv[:, 0:LANES] + dkv[:, LANES:2 * LANES]
    u = dkv[:, 2 * LANES:3 * LANES] + dkv[:, 3 * LANES:4 * LANES]
    t = t + _swap_halves(t)
    u = u + _swap_halves(u)
    return jnp.where(left, t, u)


def _qknorm_fwd(proj, gqa, gka, gqb, gkb):
    rows = proj.shape[0]
    tm = 256

    def body(qa_ref, ka_ref, va_ref, qb_ref, kb_ref, vb_ref, gqa_ref, gka_ref, gqb_ref, gkb_ref,
             oqa, oka, ova, oqb, okb, ovb):
        for src, g_ref, dst in ((qa_ref, gqa_ref, oqa), (ka_ref, gka_ref, oka), (qb_ref, gqb_ref, oqb)):
            xv = src[...]
            _put_pairs(dst, (xv * _seg_rstd(xv)) * g_ref[...])
        _put_pairs(ova, va_ref[...])
        kv = kb_ref[...]
        _put_pairs(okb, _expand_kv((kv * _seg_rstd(kv)) * gkb_ref[...]))
        _put_pairs(ovb, _expand_kv(vb_ref[...]))

    def win(width, off):
        return pl.BlockSpec((tm, width), lambda i: (i, off // width))

    vec = lambda w: pl.BlockSpec((1, w), lambda i: (0, 0))
    out = pl.BlockSpec((N_PAIRS, tm, LANES), lambda i: (0, i, 0))
    return pl.pallas_call(
        body, out_shape=[jax.ShapeDtypeStruct((N_PAIRS, rows, LANES), F32)] * 6, grid=(rows // tm,),
        in_specs=[win(WIDTH, OFF_QA), win(WIDTH, OFF_KA), win(WIDTH, OFF_VA), win(WIDTH, OFF_QB),
                  win(LANES, OFF_KB), win(LANES, OFF_VB), vec(WIDTH), vec(WIDTH), vec(WIDTH), vec(LANES)],
        out_specs=[out] * 6, name="qknorm_fwd", compiler_params=_params("parallel"))(
            proj, proj, proj, proj, proj, proj, gqa, gka, gqb, gkb)


def _norm_bwd(xv, g, dy):
    r = _seg_rstd(xv)
    xh = xv * r
    dxh = dy * g
    dx = r * (dxh - xh * (_seg_sum(dxh * xh) * (1.0 / HEAD_DIM)))
    return dx, jnp.sum(dy * xh, axis=0, keepdims=True)


def _qknorm_bwd(proj, gqa, gka, gqb, gkb, dqa, dka, dva, dqb, dkb, dvb, dga, dgb):
    rows = proj.shape[0]
    tm = 256
    n_a = len(dqa)

    def body(*refs):
        qa_ref, ka_ref, qb_ref, kb_ref, gqa_ref, gka_ref, gqb_ref, gkb_ref = refs[:8]
        pos = 8
        dqa_refs, dka_refs, dva_refs = refs[pos:pos + n_a], refs[pos + n_a:pos + 2 * n_a], refs[pos + 2 * n_a:pos + 3 * n_a]
        pos += 3 * n_a
        dqb_ref, dkb_ref, dvb_ref, dga_ref, dgb_ref = refs[pos:pos + 5]
        dproj_ref, ogqa, ogka, ogqb, ogkb = refs[pos + 5:]

        def total(rs):
            acc = _get_pairs(rs[0])
            for r in rs[1:]:
                acc = acc + _get_pairs(r)
            return acc

        dx_qa, p_qa = _norm_bwd(qa_ref[...], gqa_ref[...], total(dqa_refs))
        dx_ka, p_ka = _norm_bwd(ka_ref[...], gka_ref[...], total(dka_refs))
        dx_qb, p_qb = _norm_bwd(qb_ref[...], gqb_ref[...], _get_pairs(dqb_ref))
        dx_kb, p_kb = _norm_bwd(kb_ref[...], gkb_ref[...], _reduce_kv(_get_pairs(dkb_ref)))
        dproj_ref[:, OFF_QA:OFF_QA + WIDTH] = dx_qa.astype(BF16)
        dproj_ref[:, OFF_KA:OFF_KA + WIDTH] = dx_ka.astype(BF16)
        dproj_ref[:, OFF_VA:OFF_VA + WIDTH] = total(dva_refs).astype(BF16)
        dproj_ref[:, OFF_QB:OFF_QB + WIDTH] = dx_qb.astype(BF16)
        dproj_ref[:, OFF_KB:OFF_KB + LANES] = dx_kb.astype(BF16)
        dproj_ref[:, OFF_VB:OFF_VB + LANES] = _reduce_kv(_get_pairs(dvb_ref)).astype(BF16)
        dproj_ref[:, OFF_GA:OFF_GB] = dga_ref[...]
        dproj_ref[:, OFF_GB:D_IN] = dgb_ref[...]
        first = pl.program_id(0) == 0
        for o_ref, part in ((ogqa, p_qa), (ogka, p_ka), (ogqb, p_qb), (ogkb, p_kb)):
            @pl.when(first)
            def _(o_ref=o_ref, part=part):
                o_ref[...] = part

            @pl.when(jnp.logical_not(first))
            def _(o_ref=o_ref, part=part):
                o_ref[...] += part

    def win(width, off):
        return pl.BlockSpec((tm, width), lambda i: (i, off // width))

    vec = lambda w: pl.BlockSpec((1, w), lambda i: (0, 0))
    row = lambda w: pl.BlockSpec((tm, w), lambda i: (i, 0))
    in_specs = [win(WIDTH, OFF_QA), win(WIDTH, OFF_KA), win(WIDTH, OFF_QB), win(LANES, OFF_KB),
                vec(WIDTH), vec(WIDTH), vec(WIDTH), vec(LANES)]
    in_specs += [pl.BlockSpec((N_PAIRS, tm, LANES), lambda i: (0, i, 0))] * (3 * n_a + 3) + [row(D_MODEL)] * 2
    return pl.pallas_call(
        body,
        out_shape=[jax.ShapeDtypeStruct((rows, D_IN), BF16), jax.ShapeDtypeStruct((1, WIDTH), F32),
                   jax.ShapeDtypeStruct((1, WIDTH), F32), jax.ShapeDtypeStruct((1, WIDTH), F32),
                   jax.ShapeDtypeStruct((1, LANES), F32)],
        grid=(rows // tm,), in_specs=in_specs,
        out_specs=[row(D_IN), vec(WIDTH), vec(WIDTH), vec(WIDTH), vec(LANES)],
        name="qknorm_bwd", compiler_params=_params("arbitrary"))(
            proj, proj, proj, proj, gqa, gka, gqb, gkb, *dqa, *dka, *dva, dqb, dkb, dvb, dga, dgb)


def _t5_bucket(rel):
    half_b = NUM_BUCKETS // 2
    max_exact = half_b // 2
    sign = jnp.where(rel > 0, half_b, 0)
    n = jnp.abs(rel)
    nf = jnp.maximum(n, 1).astype(F32)
    large = max_exact + (jnp.log(nf / max_exact) / math.log(MAX_DISTANCE / max_exact)
                         * (half_b - max_exact)).astype(jnp.int32)
    large = jnp.minimum(large, half_b - 1)
    return sign + jnp.where(n < max_exact, n, large)


def _band_buckets(blk, dilation):
    i = jnp.arange(blk, dtype=jnp.int32)[:, None]
    j = jnp.arange(3 * blk, dtype=jnp.int32)[None, :]
    rel = j - blk - i
    return jnp.where(jnp.abs(rel) <= blk, _t5_bucket(rel * dilation), -1)


def _bias_tiles(table, buckets, head_off, name):
    blk = buckets.shape[0]

    def body(tab_ref, bk_ref, o_ref):
        h = pl.program_id(0) + head_off
        bk = bk_ref[...]
        acc = jnp.full(bk.shape, NEG_INF, F32)
        for b in range(NUM_BUCKETS):
            acc = jnp.where(bk == b, tab_ref[b, h], acc)
        o_ref[0] = acc

    return pl.pallas_call(
        body, out_shape=jax.ShapeDtypeStruct((N_HEADS, blk, 3 * blk), F32), grid=(N_HEADS,),
        in_specs=[pl.BlockSpec(memory_space=pltpu.SMEM), pl.BlockSpec((blk, 3 * blk), lambda h: (0, 0))],
        out_specs=pl.BlockSpec((1, blk, 3 * blk), lambda h: (h, 0, 0)),
        name=name, compiler_params=_params("parallel"))(table, buckets)


def _table_grad(dbias, buckets, name):
    blk = buckets.shape[0]

    def body(db_ref, bk_ref, o_ref):
        bk = bk_ref[...]
        dbv = db_ref[0]
        lane = lax.broadcasted_iota(jnp.int32, (1, LANES), 1)
        acc = jnp.zeros((1, LANES), F32)
        for b in range(NUM_BUCKETS):
            acc = jnp.where(lane == b, jnp.sum(jnp.where(bk == b, dbv, 0.0)), acc)
        o_ref[0] = acc

    out = pl.pallas_call(
        body, out_shape=jax.ShapeDtypeStruct((N_HEADS, 1, LANES), F32), grid=(N_HEADS,),
        in_specs=[pl.BlockSpec((1, blk, 3 * blk), lambda h: (h, 0, 0)), pl.BlockSpec((blk, 3 * blk), lambda h: (0, 0))],
        out_specs=pl.BlockSpec((1, 1, LANES), lambda h: (h, 0, 0)),
        name=name, compiler_params=_params("parallel"))(dbias, buckets)
    return out[:, 0, :NUM_BUCKETS]


def _dot_nt(a, b):
    return lax.dot_general(a, b, (((1,), (1,)), ((), ())), preferred_element_type=F32)


def _dot_tn(a, b):
    return lax.dot_general(a, b, (((0,), (0,)), ((), ())), preferred_element_type=F32)


def _stack_pair(x2, left):
    return jnp.concatenate([jnp.where(left, x2, 0.0), jnp.where(left, 0.0, x2)], axis=0).astype(BF16)


def _attn_geometry(blk, d):
    chunk = blk * ITEMS if d == 1 else blk * d
    groups = 1 if d == 1 else d // ITEMS
    halo = blk if d == 1 else chunk
    return chunk, groups, halo


def _item_rows(ref, j, r0, blk, d):
    if d == 1:
        return ref[j * blk:(j + 1) * blk, :]
    return ref[pl.ds(r0 + j, blk, stride=d), :]


def _item_penalty(t, nct, j, blk, d):
    first_ok, last_ok = t > 0, t < nct - 1
    if d == 1:
        first_ok = True if j > 0 else first_ok
        last_ok = True if j < ITEMS - 1 else last_ok
    col = lax.broadcasted_iota(jnp.int32, (1, 3 * blk), 1)
    ok = jnp.logical_and(jnp.logical_or(col >= blk, first_ok), jnp.logical_or(col < 2 * blk, last_ok))
    return jnp.where(ok, 0.0, NEG_INF).astype(F32)


def _attn_specs(seq, blk, d, step_of):
    chunk, _, halo = _attn_geometry(blk, d)
    per, last = chunk // halo, seq // halo - 1
    cur = pl.BlockSpec((None, chunk, LANES), lambda hp, t: (hp, step_of(t), 0))
    prev = pl.BlockSpec((None, halo, LANES), lambda hp, t: (hp, jnp.clip(step_of(t) * per - 1, 0, last), 0))
    nxt = pl.BlockSpec((None, halo, LANES), lambda hp, t: (hp, jnp.minimum((step_of(t) + 1) * per, last), 0))
    return cur, prev, nxt


def _attn_fwd(q, k, v, bias, sink, blk, d, name):
    _, seq, _ = q.shape
    chunk, groups, _ = _attn_geometry(blk, d)
    nct = seq // chunk
    has_sink = sink is not None
    scale = HEAD_DIM ** -0.5

    def body(*refs):
        q_ref, kp, kc, kn, vp, vc, vn, b_ref = refs[:8]
        s_ref = refs[8] if has_sink else None
        o_ref, l_ref = refs[-2], refs[-1]
        t = pl.program_id(1)
        left = lax.broadcasted_iota(jnp.int32, (1, LANES), 1) < HEAD_DIM
        bias2 = b_ref[...]
        if d == 1:
            kwin = jnp.concatenate([kp[...], kc[...], kn[...]], axis=0).astype(BF16)
            vwin = jnp.concatenate([vp[...], vc[...], vn[...]], axis=0).astype(BF16)

        def group(r0):
            scores, vcats = [], []
            for j in range(ITEMS):
                qs = _stack_pair(_item_rows(q_ref, j, r0, blk, d) * scale, left)
                if d == 1:
                    kcat, vcat = kwin[j * blk:(j + 3) * blk], vwin[j * blk:(j + 3) * blk]
                else:
                    kcat = jnp.concatenate([_item_rows(r, j, r0, blk, d) for r in (kp, kc, kn)], axis=0).astype(BF16)
                    vcat = jnp.concatenate([_item_rows(r, j, r0, blk, d) for r in (vp, vc, vn)], axis=0).astype(BF16)
                scores.append(_dot_nt(qs, kcat) + bias2 + _item_penalty(t, nct, j, blk, d))
                vcats.append(vcat)
            ms = [jnp.max(s, axis=-1, keepdims=True) for s in scores]
            if has_sink:
                sk = s_ref[...]
                ms = [jnp.maximum(m, sk) for m in ms]
            ps = [jnp.exp(s - m) for s, m in zip(scores, ms)]
            dens = [jnp.sum(p, axis=-1, keepdims=True) for p in ps]
            if has_sink:
                dens = [den + jnp.exp(sk - m) for den, m in zip(dens, ms)]
            pns = [(p * (1.0 / den)).astype(BF16) for p, den in zip(ps, dens)]
            lses = [m + jnp.log(den) for m, den in zip(ms, dens)]
            for j in range(ITEMS):
                o2 = jnp.dot(pns[j], vcats[j], preferred_element_type=F32)
                o_val = jnp.where(left, o2[:blk], o2[blk:])
                l_val = jnp.where(left, lses[j][:blk], lses[j][blk:])
                if d == 1:
                    o_ref[j * blk:(j + 1) * blk, :] = o_val
                    l_ref[j * blk:(j + 1) * blk, :] = l_val
                else:
                    o_ref[pl.ds(r0 + j, blk, stride=d), :] = o_val
                    l_ref[pl.ds(r0 + j, blk, stride=d), :] = l_val

        if groups == 1:
            group(0)
        else:
            def step(g, carry):
                group(g * ITEMS)
                return carry

            lax.fori_loop(0, groups, step, 0)

    cur, prev, nxt = _attn_specs(seq, blk, d, lambda t: t)
    in_specs = [cur, prev, cur, nxt, prev, cur, nxt, pl.BlockSpec((2 * blk, 3 * blk), lambda hp, t: (hp, 0))]
    args = [q, k, k, k, v, v, v, bias]
    if has_sink:
        in_specs.append(pl.BlockSpec((2 * blk, 1), lambda hp, t: (hp, 0)))
        args.append(sink)
    return pl.pallas_call(
        body, out_shape=[jax.ShapeDtypeStruct(q.shape, F32)] * 2, grid=(N_PAIRS, nct),
        in_specs=in_specs, out_specs=[cur, cur], name=name,
        compiler_params=_params("parallel", "parallel"))(*args)


def _attn_bwd(q, k, v, do, lse, delta, bias, sink, blk, d, name):
    _, seq, _ = q.shape
    chunk, groups, halo = _attn_geometry(blk, d)
    nct = seq // chunk
    has_sink = sink is not None
    n_in = 12 if has_sink else 11
    scale = HEAD_DIM ** -0.5

    def body(*refs):
        q_ref, kp, kc, kn, vp, vc, vn, do_ref, l_ref, d_ref, b_ref = refs[:11]
        s_ref = refs[11] if has_sink else None
        dq_ref, dk_ref, dv_ref, db_ref = refs[n_in:n_in + 4]
        ds_ref = refs[n_in + 4] if has_sink else None
        wk, wv = refs[-2], refs[-1]
        t = pl.program_id(1)

        @pl.when(t == 0)
        def _():
            wk[...] = jnp.zeros_like(wk)
            wv[...] = jnp.zeros_like(wv)
            db_ref[...] = jnp.zeros_like(db_ref)
            if has_sink:
                ds_ref[...] = jnp.zeros_like(ds_ref)

        @pl.when(t > 0)
        def _():
            for w in (wk, wv):
                keep = w[chunk:2 * chunk + halo]
                w[0:chunk + halo] = keep
                w[chunk + halo:2 * chunk + halo] = jnp.zeros((chunk, LANES), F32)

        @pl.when(t < nct)
        def _():
            lane = lax.broadcasted_iota(jnp.int32, (1, LANES), 1)
            left = lane < HEAD_DIM
            bias2 = b_ref[...]
            if d == 1:
                kwin = jnp.concatenate([kp[...], kc[...], kn[...]], axis=0).astype(BF16)
                vwin = jnp.concatenate([vp[...], vc[...], vn[...]], axis=0).astype(BF16)

            def group(r0):
                qss, doss, kcats, scores, dps, lcols, dcols = [], [], [], [], [], [], []
                for j in range(ITEMS):
                    qs = _stack_pair(_item_rows(q_ref, j, r0, blk, d) * scale, left)
                    dos = _stack_pair(_item_rows(do_ref, j, r0, blk, d), left)
                    if d == 1:
                        kcat, vcat = kwin[j * blk:(j + 3) * blk], vwin[j * blk:(j + 3) * blk]
                    else:
                        kcat = jnp.concatenate([_item_rows(r, j, r0, blk, d) for r in (kp, kc, kn)], axis=0).astype(BF16)
                        vcat = jnp.concatenate([_item_rows(r, j, r0, blk, d) for r in (vp, vc, vn)], axis=0).astype(BF16)
                    l2, d2 = _item_rows(l_ref, j, r0, blk, d), _item_rows(d_ref, j, r0, blk, d)
                    lcols.append(jnp.concatenate([jnp.max(jnp.where(left, l2, NEG_INF), axis=-1, keepdims=True),
                                                  jnp.max(jnp.where(left, NEG_INF, l2), axis=-1, keepdims=True)], axis=0))
                    dcols.append(jnp.concatenate([jnp.sum(jnp.where(lane == 0, d2, 0.0), axis=-1, keepdims=True),
                                                  jnp.sum(jnp.where(lane == HEAD_DIM, d2, 0.0), axis=-1, keepdims=True)],
                                                 axis=0))
                    scores.append(_dot_nt(qs, kcat) + bias2 + _item_penalty(t, nct, j, blk, d))
                    dps.append(_dot_nt(dos, vcat))
                    qss.append(qs)
                    doss.append(dos)
                    kcats.append(kcat)
                ps = [jnp.exp(s - lc) for s, lc in zip(scores, lcols)]
                dss = [p * (dp - dc) for p, dp, dc in zip(ps, dps, dcols)]
                db_ref[...] += functools.reduce(lambda a, b: a + b, dss)
                if has_sink:
                    sk = s_ref[...]
                    ds_ref[...] -= functools.reduce(lambda a, b: a + b, [dc * jnp.exp(sk - lc) for dc, lc in zip(dcols, lcols)])
                dsbs = [ds.astype(BF16) for ds in dss]
                for j in range(ITEMS):
                    dq2 = jnp.dot(dsbs[j], kcats[j], preferred_element_type=F32) * scale
                    dq_val = jnp.where(left, dq2[:blk], dq2[blk:])
                    if d == 1:
                        dq_ref[j * blk:(j + 1) * blk, :] = dq_val
                    else:
                        dq_ref[pl.ds(r0 + j, blk, stride=d), :] = dq_val
                news = [(_dot_tn(dsbs[j], qss[j]), _dot_tn(ps[j].astype(BF16), doss[j])) for j in range(ITEMS)]
                if d == 1:
                    for which, w in enumerate((wk, wv)):
                        for b in range(ITEMS + 2):
                            parts = [news[j][which][(b - j) * blk:(b - j + 1) * blk] for j in range(ITEMS) if 0 <= b - j < 3]
                            w[chunk + (b - 1) * blk:chunk + b * blk, :] += functools.reduce(lambda x, y: x + y, parts)
                else:
                    for j in range(ITEMS):
                        for which, w in enumerate((wk, wv)):
                            for c in range(3):
                                w[pl.ds(c * chunk + r0 + j, blk, stride=d), :] += news[j][which][c * blk:(c + 1) * blk]

            if groups == 1:
                group(0)
            else:
                def step(g, carry):
                    group(g * ITEMS)
                    return carry

                lax.fori_loop(0, groups, step, 0)

        dk_ref[...] = wk[0:chunk]
        dv_ref[...] = wv[0:chunk]

    cur, prev, nxt = _attn_specs(seq, blk, d, lambda t: jnp.minimum(t, nct - 1))
    lag = pl.BlockSpec((None, chunk, LANES), lambda hp, t: (hp, jnp.maximum(t - 1, 0), 0))
    band = pl.BlockSpec((2 * blk, 3 * blk), lambda hp, t: (hp, 0))
    col = pl.BlockSpec((2 * blk, 1), lambda hp, t: (hp, 0))
    in_specs = [cur, prev, cur, nxt, prev, cur, nxt, cur, cur, cur, band]
    args = [q, k, k, k, v, v, v, do, lse, delta, bias]
    out_shape = [jax.ShapeDtypeStruct(q.shape, F32)] * 3 + [jax.ShapeDtypeStruct((N_HEADS * blk, 3 * blk), F32)]
    out_specs = [cur, lag, lag, band]
    if has_sink:
        in_specs.append(col)
        args.append(sink)
        out_shape.append(jax.ShapeDtypeStruct((N_HEADS * blk, 1), F32))
        out_specs.append(col)
    window = pltpu.VMEM((2 * chunk + halo, LANES), F32)
    return pl.pallas_call(
        body, out_shape=out_shape, grid=(N_PAIRS, nct + 1), in_specs=in_specs, out_specs=out_specs,
        scratch_shapes=[window, window], name=name,
        compiler_params=_params("arbitrary", "arbitrary"))(*args)


def _combine_patterns(outs, lses):
    _, rows, _ = outs[0].shape
    tm = 256
    n = len(outs)

    def body(*refs):
        o_refs, l_refs = refs[:n], refs[n:2 * n]
        y_ref, lse_ref = refs[2 * n], refs[2 * n + 1]
        for hp in range(N_PAIRS):
            ls = [r[hp] for r in l_refs]
            m = functools.reduce(jnp.maximum, ls)
            es = [jnp.exp(l - m) for l in ls]
            den = functools.reduce(lambda a, b: a + b, es)
            num = functools.reduce(lambda a, b: a + b, [e * r[hp] for e, r in zip(es, o_refs)])
            y_ref[:, hp * LANES:(hp + 1) * LANES] = num / den
            lse_ref[hp] = m + jnp.log(den)

    pm = pl.BlockSpec((N_PAIRS, tm, LANES), lambda i: (0, i, 0))
    return pl.pallas_call(
        body, out_shape=[jax.ShapeDtypeStruct((rows, WIDTH), F32), jax.ShapeDtypeStruct((N_PAIRS, rows, LANES), F32)],
        grid=(rows // tm,), in_specs=[pm] * (2 * n), out_specs=[pl.BlockSpec((tm, WIDTH), lambda i: (i, 0)), pm],
        name="combine_a", compiler_params=_params("parallel"))(*outs, *lses)


def _pairs_to_tokens(a):
    _, rows, _ = a.shape
    tm = 256

    def body(a_ref, o_ref):
        o_ref[...] = _get_pairs(a_ref)

    return pl.pallas_call(
        body, out_shape=jax.ShapeDtypeStruct((rows, WIDTH), a.dtype), grid=(rows // tm,),
        in_specs=[pl.BlockSpec((N_PAIRS, tm, LANES), lambda i: (0, i, 0))],
        out_specs=pl.BlockSpec((tm, WIDTH), lambda i: (i, 0)), name="pairs_to_tokens",
        compiler_params=_params("parallel"))(a)


def _attn_bwd_prep(dy, y, name):
    rows = dy.shape[0]
    tm = 256

    def body(dy_ref, y_ref, do_ref, dl_ref):
        dyv = dy_ref[...]
        _put_pairs(do_ref, dyv)
        _put_pairs(dl_ref, _seg_sum(dyv * y_ref[...]))

    tok = pl.BlockSpec((tm, WIDTH), lambda i: (i, 0))
    pm = pl.BlockSpec((N_PAIRS, tm, LANES), lambda i: (0, i, 0))
    return pl.pallas_call(
        body, out_shape=[jax.ShapeDtypeStruct((N_PAIRS, rows, LANES), F32)] * 2, grid=(rows // tm,),
        in_specs=[tok, tok], out_specs=[pm, pm], name=name, compiler_params=_params("parallel"))(dy, y)


def _tile_gain(g, reps):
    return jnp.tile(g[None, :], (1, reps))


def _local_step(x, p, target, w_in_of, rest_of, small):
    rel_table = small["rel_table"]
    buckets_a = [_band_buckets(blk, d) for blk, d in DILATED]
    buckets_b = _band_buckets(BLK_B, 1)
    bias_a = [_bias_tiles(rel_table, bk, 0, "bias_a").reshape(N_HEADS * bk.shape[0], -1) for bk in buckets_a]
    bias_b = _bias_tiles(rel_table, buckets_b, N_HEADS, "bias_b").reshape(N_HEADS * BLK_B, -1)

    saved = []
    for l in range(DEPTH):
        g_mix, g_ffn, g_ple = (small[n][l][None, :] for n in ("norm_mix_g", "norm_ffn_g", "norm_ple_g"))
        gqa, gka, gqb = (_tile_gain(small[n][l], N_HEADS) for n in ("qnorm_a_g", "knorm_a_g", "qnorm_b_g"))
        gkb = _tile_gain(small["knorm_b_g"][l], N_KV_B)
        sink = jnp.repeat(small["sink_b"][l], BLK_B)[:, None]

        h = _rms_fwd(x, g_mix, "rms_mix")
        w_in = w_in_of(l, h)
        proj = _mm(h, w_in, "nt", F32, "mm_in")
        qa, ka, va, qb, kb, vb = _qknorm_fwd(proj, gqa, gka, gqb, gkb)
        outs, lses = [], []
        for (blk, d), bias in zip(DILATED, bias_a):
            o, ls = _attn_fwd(qa, ka, va, bias, None, blk, d, f"attn_a{d}_fwd")
            outs.append(o)
            lses.append(ls)
        ya, lse_a = _combine_patterns(outs, lses)
        yb, lse_b = _attn_fwd(qb, kb, vb, bias_b, sink, BLK_B, 1, "attn_b_fwd")
        yb = _pairs_to_tokens(yb)
        w = dict(rest_of(l, yb), w_in=w_in)
        def gate(products, extra):
            (ca_, cb_), (ga_, gb_) = products, extra
            return _sigmoid(ga_) * ca_ + _sigmoid(gb_) * cb_, ca_, cb_

        merged, ca, cb = _mm_fused([(ya, w["w_branch_a"], "nn"), (yb, w["w_branch_b"], "nn")],
                                   [(proj, OFF_GA), (proj, OFF_GB)], gate, [BF16, BF16, BF16], "mm_branches_gate")
        x1 = _mm(merged, w["w_out"], "nn", F32, "mm_out", res=x)

        h2 = _rms_fwd(x1, g_ffn, "rms_ffn")

        def swiglu(products, extra):
            a_, u_ = products
            return (a_ * _sigmoid(a_)) * u_, a_, u_

        hid, a, u = _mm_fused([(h2, w["w_ffn_gate"], "nt"), (h2, w["w_ffn_up"], "nt")], [], swiglu,
                              [BF16, BF16, BF16], "mm_ffn_gate_up")
        x2 = _mm(hid, w["w_ffn_down"], "nn", F32, "mm_ffn_down", res=x1)

        h3 = _rms_fwd(x2, g_ple, "rms_ple")

        def ple(products, extra):
            z_, e_ = products
            return extra[0] + _sigmoid(z_) * e_, z_, e_

        x3, z, e = _mm_fused([(h3, w["w_ple_gate"], "nn"), (p[l], w["w_ple_proj"], "nn")], [(x2, 0)], ple,
                             [F32, BF16, BF16], "mm_ple")
        saved.append(dict(w=w, x0=x, h=h, proj=proj, qa=qa, ka=ka, va=va, qb=qb, kb=kb, vb=vb, ya=ya, lse_a=lse_a,
                          yb=yb, lse_b=lse_b, ca=ca, cb=cb, merged=merged, x1=x1, h2=h2, a=a, u=u, hid=hid,
                          x2=x2, h3=h3, z=z, e=e))
        x = x3

    dx, loss_acc = _loss_grad(x, target)
    loss = loss_acc[0, 0]

    gbig = [{} for _ in range(DEPTH)]
    marks = [{} for _ in range(DEPTH)]
    gsmall = {n: [None] * DEPTH for n in SMALL if n != "rel_table"}
    dtable_a = jnp.zeros((N_HEADS, NUM_BUCKETS), F32)
    dtable_b = jnp.zeros((N_HEADS, NUM_BUCKETS), F32)

    for l in reversed(range(DEPTH)):
        sv = saved[l]
        w = sv["w"]
        g_mix, g_ffn, g_ple = (small[n][l][None, :] for n in ("norm_mix_g", "norm_ffn_g", "norm_ple_g"))
        gqa, gka, gqb = (_tile_gain(small[n][l], N_HEADS) for n in ("qnorm_a_g", "knorm_a_g", "qnorm_b_g"))
        gkb = _tile_gain(small["knorm_b_g"][l], N_KV_B)
        sink = jnp.repeat(small["sink_b"][l], BLK_B)[:, None]

        def ple_bwd(dx_, z_, e_):
            s = _sigmoid(z_.astype(F32))
            return dx_ * s, dx_ * e_.astype(F32) * (s * (1.0 - s))

        de, dz = _ew(ple_bwd, [(dx, 0), (sv["z"], 0), (sv["e"], 0)], [BF16, BF16], width=D_MODEL, bw=D_MODEL,
                     name="ple_bwd")
        gbig[l]["w_ple_proj"] = _mm(p[l], de, "tn", F32, "mm_d_ple_proj")
        gbig[l]["w_ple_gate"] = _mm(sv["h3"], dz, "tn", F32, "mm_d_ple_gate")
        dh3 = _mm(dz, w["w_ple_gate"], "nt", F32, "mm_dh3")
        dx, dxb, gsmall["norm_ple_g"][l] = _rms_bwd(sv["x2"], g_ple, dh3, dx, "rms_ple_bwd")

        gbig[l]["w_ffn_down"] = _mm(sv["hid"], dxb, "tn", F32, "mm_d_ffn_down")

        def swiglu_bwd(products, extra):
            dh_, a_, u_ = products[0], extra[0].astype(F32), extra[1].astype(F32)
            s = _sigmoid(a_)
            return dh_ * u_ * (s * (1.0 + a_ * (1.0 - s))), dh_ * (a_ * s)

        da, du = _mm_fused([(dxb, w["w_ffn_down"], "nt")], [(sv["a"], 0), (sv["u"], 0)], swiglu_bwd, [BF16, BF16],
                           "mm_dhid_swiglu_bwd")
        gbig[l]["w_ffn_gate"] = _mm(da, sv["h2"], "tn", F32, "mm_d_ffn_gate")
        gbig[l]["w_ffn_up"] = _mm(du, sv["h2"], "tn", F32, "mm_d_ffn_up")
        dh2 = _mm(da, w["w_ffn_gate"], "nn", F32, "mm_dh2_gate")
        dh2 = _mm(du, w["w_ffn_up"], "nn", F32, "mm_dh2_up", res=dh2)
        dx, dxb, gsmall["norm_ffn_g"][l] = _rms_bwd(sv["x1"], g_ffn, dh2, dx, "rms_ffn_bwd")

        gbig[l]["w_out"] = _mm(sv["merged"], dxb, "tn", F32, "mm_d_out")

        def gate_bwd(products, extra):
            dm_, ca_, cb_ = products[0], extra[0].astype(F32), extra[1].astype(F32)
            sa, sb = _sigmoid(extra[2]), _sigmoid(extra[3])
            return dm_ * sa, dm_ * sb, dm_ * ca_ * (sa * (1.0 - sa)), dm_ * cb_ * (sb * (1.0 - sb))

        dca, dcb, dga, dgb = _mm_fused(
            [(dxb, w["w_out"], "nt")], [(sv["ca"], 0), (sv["cb"], 0), (sv["proj"], OFF_GA), (sv["proj"], OFF_GB)],
            gate_bwd, [BF16, BF16, BF16, BF16], "mm_dmerged_gate_bwd")
        gbig[l]["w_branch_a"] = _mm(sv["ya"], dca, "tn", F32, "mm_d_branch_a")
        gbig[l]["w_branch_b"] = _mm(sv["yb"], dcb, "tn", F32, "mm_d_branch_b")
        dya = _mm(dca, w["w_branch_a"], "nt", F32, "mm_dya")
        dyb = _mm(dcb, w["w_branch_b"], "nt", F32, "mm_dyb")

        dya, delta_a = _attn_bwd_prep(dya, sv["ya"], "attn_a_bwd_prep")
        dyb, delta_b = _attn_bwd_prep(dyb, sv["yb"], "attn_b_bwd_prep")

        dqa, dka, dva = [], [], []
        for (blk, d), bias, bk in zip(DILATED, bias_a, buckets_a):
            dq_, dk_, dv_, db_ = _attn_bwd(sv["qa"], sv["ka"], sv["va"], dya, sv["lse_a"], delta_a, bias, None, blk, d,
                                           f"attn_a{d}_bwd")
            dqa.append(dq_)
            dka.append(dk_)
            dva.append(dv_)
            dtable_a = dtable_a + _table_grad(db_.reshape(N_HEADS, blk, 3 * blk), bk, "table_grad_a")
        dqb, dkb, dvb, db_, dsink = _attn_bwd(sv["qb"], sv["kb"], sv["vb"], dyb, sv["lse_b"], delta_b, bias_b, sink,
                                              BLK_B, 1, "attn_b_bwd")
        dtable_b = dtable_b + _table_grad(db_.reshape(N_HEADS, BLK_B, 3 * BLK_B), buckets_b, "table_grad_b")
        gsmall["sink_b"][l] = dsink.reshape(N_HEADS, BLK_B).sum(axis=1)

        dproj, pqa, pka, pqb, pkb = _qknorm_bwd(sv["proj"], gqa, gka, gqb, gkb, dqa, dka, dva, dqb, dkb, dvb, dga, dgb)
        marks[l]["attn_bwd_done"] = dproj
        gsmall["qnorm_a_g"][l] = pqa.reshape(N_HEADS, HEAD_DIM).sum(0)
        gsmall["knorm_a_g"][l] = pka.reshape(N_HEADS, HEAD_DIM).sum(0)
        gsmall["qnorm_b_g"][l] = pqb.reshape(N_HEADS, HEAD_DIM).sum(0)
        gsmall["knorm_b_g"][l] = pkb.reshape(N_KV_B, HEAD_DIM).sum(0)
        gbig[l]["w_in"] = _mm(dproj, sv["h"], "tn", F32, "mm_d_in")
        dh = _mm(dproj, w["w_in"], "nn", F32, "mm_dh")
        dx, _, gsmall["norm_mix_g"][l] = _rms_bwd(sv["x0"], g_mix, dh, dx, "rms_mix_bwd")
        gsmall["norm_mix_g"][l] = gsmall["norm_mix_g"][l][0]
        gsmall["norm_ffn_g"][l] = gsmall["norm_ffn_g"][l][0]
        gsmall["norm_ple_g"][l] = gsmall["norm_ple_g"][l][0]

    gsmall = {n: jnp.stack(v) for n, v in gsmall.items()}
    gsmall["rel_table"] = jnp.concatenate([dtable_a, dtable_b], axis=0).T
    return loss, dx, gbig, gsmall, marks


def _place():
    return lax.axis_index("x"), lax.axis_index("y"), lax.axis_index("c")


def _flip(v, bit):
    return 1 - v if bit else v


CHIP_RELATIONS = ((0, 1), (1, 0), (1, 1))
ANY = pl.BlockSpec(memory_space=pl.ANY)


def _allgather_body(w_refs, out_refs, send_sems, recv_sems):
    x, y, c = _place()
    chips = [(_flip(x, a), _flip(y, b)) for a, b in CHIP_RELATIONS]

    def make(g):
        w_ref, out_ref = w_refs[g], out_refs[g]
        half = w_ref.shape[0] // 2

        def part(px, py, pc):
            return out_ref.at[2 * px + py, pl.ds(pc * half, half), :]

        def copy(k, block, to, src=None):
            return pltpu.make_async_remote_copy(
                src_ref=part(*block) if src is None else src, dst_ref=part(*block),
                send_sem=send_sems.at[7 * g + k], recv_sem=recv_sems.at[7 * g + k], device_id=to,
                device_id_type=MESH_ID)

        own = pltpu.make_async_remote_copy(
            src_ref=w_ref, dst_ref=out_ref.at[2 * x + y], send_sem=send_sems.at[7 * g + 6],
            recv_sem=recv_sems.at[7 * g + 6], device_id=(x, y, 1 - c), device_id_type=MESH_ID)
        first = [copy(k, (x, y, c), (*chip, c), src=w_ref.at[pl.ds(c * half, half), :]) for k, chip in enumerate(chips)]
        passed = [copy(3 + k, (*chip, c), (x, y, 1 - c)) for k, chip in enumerate(chips)]
        arrive = [copy(k, (*chip, c), (x, y, c)) for k, chip in enumerate(chips)]
        arrive2 = [copy(3 + k, (*chip, 1 - c), (x, y, c)) for k, chip in enumerate(chips)]
        return own, first, passed, arrive, arrive2

    made = [make(g) for g in range(len(w_refs))]
    for own, first, _, _, _ in made:
        own.start()
        for cp in first:
            cp.start()
    for _, _, passed, arrive, _ in made:
        for k in range(3):
            arrive[k].wait_recv()
            passed[k].start()
    for own, first, passed, _, arrive2 in made:
        for k in range(3):
            arrive2[k].wait_recv()
        own.wait_recv()
        for cp in first + passed + [own]:
            cp.wait_send()


def _sibling(x, y, c):
    return [(x, y, 1 - c)]


def _same_core_of_other_chips(x, y, c):
    return [(_flip(x, a), _flip(y, b), c) for a, b in CHIP_RELATIONS]


def _exchange(body, ins, out_types, n_sems, name, sequencer=None):
    n = len(ins)
    sems = (pltpu.SemaphoreType.DMA((n_sems,)), pltpu.SemaphoreType.DMA((n_sems,)))
    if sequencer is None:
        in_place = out_types is None
        out_shape = [jax.ShapeDtypeStruct(a.shape, a.dtype) for a in ins] if in_place else out_types

        def tc_body(*refs):
            body(refs[:n], refs[n:n + len(out_shape)], refs[-2], refs[-1])

        return list(pl.pallas_call(
            tc_body, out_shape=out_shape, in_specs=[ANY] * n, out_specs=[ANY] * len(out_shape),
            input_output_aliases={g: g for g in range(n)} if in_place else {}, scratch_shapes=list(sems), name=name)(*ins))

    collective_id, peers = sequencer
    hbm = pltpu.MemorySpace.HBM
    in_refs = [jax.new_ref(a, memory_space=hbm) for a in ins]
    out_refs = in_refs if out_types is None else [jax.empty_ref(t, memory_space=hbm) for t in out_types]

    @pl.kernel(mesh=plsc.ScalarSubcoreMesh(axis_name="sequencer", num_cores=1), name=name, scratch_types=sems,
               compiler_params=pltpu.CompilerParams(collective_id=collective_id))
    def launch(send_sems, recv_sems):
        barrier = pltpu.get_barrier_semaphore()
        devices = peers(*_place())
        for device in devices:
            pl.semaphore_signal(barrier, inc=1, device_id=device, device_id_type=MESH_ID)
        pl.semaphore_wait(barrier, len(devices))
        body(in_refs, out_refs, send_sems, recv_sems)

    launch()
    return [r[...] for r in out_refs]


def _allgather(shards, name, sequencer=None):
    out_types = [jax.ShapeDtypeStruct((N_CHIPS,) + s.shape, s.dtype) for s in shards]
    if sequencer is not None:
        sequencer = (sequencer, lambda x, y, c: _sibling(x, y, c) + _same_core_of_other_chips(x, y, c))
    return _exchange(_allgather_body, shards, out_types, 7 * len(shards), name, sequencer)


def _half_tile(half):
    return max(t for t in range(16, 1025, 16) if half % t == 0)


def _run_copies(cps):
    for cp in cps:
        cp.start()
    for cp in cps:
        cp.wait_recv()
    for cp in cps:
        cp.wait_send()


def _sibling_halves(gsends, name, sequencer=None):
    def body(g_refs, out_refs, send_sems, recv_sems):
        x, y, c = _place()
        cps = []
        for g, (g_ref, out_ref) in enumerate(zip(g_refs, out_refs)):
            half = g_ref.shape[1] // 2
            cps.append(pltpu.make_async_remote_copy(
                src_ref=g_ref.at[:, pl.ds((1 - c) * half, half), :], dst_ref=out_ref,
                send_sem=send_sems.at[g], recv_sem=recv_sems.at[g], device_id=(x, y, 1 - c), device_id_type=MESH_ID))
        _run_copies(cps)

    out_types = [jax.ShapeDtypeStruct((s.shape[0], s.shape[1] // 2, s.shape[2]), s.dtype) for s in gsends]
    return _exchange(body, gsends, out_types, len(gsends), name, sequencer and (sequencer, _sibling))


def _chip_sums(gsend, sib, place):
    n, rows, cols = gsend.shape
    half = rows // 2
    tm = _half_tile(half)
    nblk = half // tm

    def body(s_ref, g_ref, sib_ref, o_ref):
        o_ref[0] = (g_ref[0].astype(F32) + sib_ref[0].astype(F32)).astype(o_ref.dtype)

    grid_spec = pltpu.PrefetchScalarGridSpec(
        num_scalar_prefetch=1, grid=(n, nblk),
        in_specs=[pl.BlockSpec((1, tm, cols), lambda k, i, s: (jnp.bitwise_xor(s[0], k), s[1] * nblk + i, 0)),
                  pl.BlockSpec((1, tm, cols), lambda k, i, s: (jnp.bitwise_xor(s[0], k), i, 0))],
        out_specs=pl.BlockSpec((1, tm, cols), lambda k, i, s: (k, i, 0)))
    return pl.pallas_call(
        body, out_shape=jax.ShapeDtypeStruct((n, half, cols), BF16), grid_spec=grid_spec,
        name="rs_chip_sums", compiler_params=_params("parallel", "parallel"))(place, gsend, sib)


def _exchange_chip_sums(tsends, name, sequencer=None):
    def body(t_refs, out_refs, send_sems, recv_sems):
        x, y, c = _place()
        cps = []
        for g, (t_ref, out_ref) in enumerate(zip(t_refs, out_refs)):
            for k, device in enumerate(_same_core_of_other_chips(x, y, c)):
                cps.append(pltpu.make_async_remote_copy(
                    src_ref=t_ref.at[k + 1], dst_ref=out_ref.at[k], send_sem=send_sems.at[3 * g + k],
                    recv_sem=recv_sems.at[3 * g + k], device_id=device, device_id_type=MESH_ID))
        _run_copies(cps)

    out_types = [jax.ShapeDtypeStruct((3,) + s.shape[1:], s.dtype) for s in tsends]
    return _exchange(body, tsends, out_types, 3 * len(tsends), name,
                     sequencer and (sequencer, _same_core_of_other_chips))


def _final_sum(tsend, recv, place):
    n, half, cols = tsend.shape
    tm = _half_tile(half)
    nblk = half // tm

    def body(s_ref, t_ref, r_ref, o_ref):
        o_ref[...] = ((t_ref[0].astype(F32) + r_ref[0].astype(F32)) + r_ref[1].astype(F32)) + r_ref[2].astype(F32)

    grid_spec = pltpu.PrefetchScalarGridSpec(
        num_scalar_prefetch=1, grid=(nblk,),
        in_specs=[pl.BlockSpec((1, tm, cols), lambda i, s: (0, i, 0)), pl.BlockSpec((n - 1, tm, cols), lambda i, s: (0, i, 0))],
        out_specs=pl.BlockSpec((tm, cols), lambda i, s: (s[1] * nblk + i, 0)))
    return pl.pallas_call(
        body, out_shape=jax.ShapeDtypeStruct((2 * half, cols), F32), grid_spec=grid_spec, name="rs_final_sum",
        compiler_params=_params("parallel"))(place, tsend, recv)


def _join_halves(gfulls, name, sequencer=None):
    def body(g_refs, out_refs, send_sems, recv_sems):
        x, y, c = _place()
        n = len(g_refs)

        def copy(g, pc):
            half = g_refs[g].shape[0] // 2
            return pltpu.make_async_remote_copy(
                src_ref=g_refs[g].at[pl.ds(pc * half, half), :], dst_ref=out_refs[g].at[pl.ds(pc * half, half), :],
                send_sem=send_sems.at[g], recv_sem=recv_sems.at[g], device_id=(x, y, 1 - c), device_id_type=MESH_ID)

        mine = [copy(g, c) for g in range(n)]
        for cp in mine:
            cp.start()
        for g in range(n):
            copy(g, 1 - c).wait_recv()
        for cp in mine:
            cp.wait_send()

    return _exchange(body, gfulls, None, len(gfulls), name, sequencer and (sequencer, _sibling))


def _allreduce_small(v):
    rows, cols = v.shape

    def body(v_ref, out_ref, buf, send_sems, recv_sems):
        x, y, c = _place()
        cps = []
        for k in range(1, 8):
            peer = (_flip(x, (k >> 2) & 1), _flip(y, (k >> 1) & 1), _flip(c, k & 1))
            cps.append(pltpu.make_async_remote_copy(
                src_ref=v_ref, dst_ref=buf.at[k - 1], send_sem=send_sems.at[k - 1], recv_sem=recv_sems.at[k - 1],
                device_id=peer, device_id_type=MESH_ID))
        for cp in cps:
            cp.start()
        for cp in cps:
            cp.wait_recv()
        for cp in cps:
            cp.wait_send()
        t0 = v_ref[...] + buf[0]
        t1 = buf[1] + buf[2]
        t2 = buf[3] + buf[4]
        t3 = buf[5] + buf[6]
        out_ref[...] = (t0 + t1) + (t2 + t3)

    vm = pl.BlockSpec(memory_space=pltpu.VMEM)
    return pl.pallas_call(
        body, out_shape=jax.ShapeDtypeStruct((rows, cols), F32), in_specs=[vm], out_specs=vm,
        scratch_shapes=[pltpu.VMEM((7, rows, cols), F32), pltpu.SemaphoreType.DMA((7,)), pltpu.SemaphoreType.DMA((7,))],
        name="allreduce_small")(v)


BIG_INFO = {n: (shape, ax) for n, shape, ax in BIG}
GROUPS = (("w_in",), ("w_ffn_gate", "w_ffn_up", "w_ffn_down", "w_out", "w_ple_gate"),
          ("w_branch_a", "w_branch_b", "w_ple_proj"))


def _shard_shape(name):
    (k, m), ax = BIG_INFO[name]
    return (k // N_CHIPS, m) if ax == 0 else (k, m // N_CHIPS)


def _group_rows(group):
    offs, off = {}, 0
    for n in group:
        offs[n] = off
        off += _shard_shape(n)[0]
    return offs, off


def _pack_groups(shards, layer, dtype):
    return [jnp.concatenate([shards[n][layer].astype(dtype) for n in group], axis=0) for group in GROUPS]


def _unpack_full(gathered, groups):
    out = {}
    for group, arr in zip(groups, gathered):
        offs, _ = _group_rows(group)
        for n in group:
            rows, cols = _shard_shape(n)
            (k, m), ax = BIG_INFO[n]
            slab = arr[:, offs[n]:offs[n] + rows]
            out[n] = slab.reshape(k, m) if ax == 0 else jnp.transpose(slab, (1, 0, 2)).reshape(k, m)
    return out


def _pack_grads(gfull):
    out = []
    for group in GROUPS:
        parts = []
        for n in group:
            rows, cols = _shard_shape(n)
            ax = BIG_INFO[n][1]
            slab = (gfull[n].reshape(N_CHIPS, rows, cols) if ax == 0
                    else jnp.transpose(gfull[n].reshape(rows, N_CHIPS, cols), (1, 0, 2)))
            parts.append(slab.astype(BF16))
        out.append(jnp.concatenate(parts, axis=1))
    return out


def _after(values, mark):
    values, _ = lax.optimization_barrier((values, mark))
    return values


def _reduce_scatter_begin(gsends, place, tag, ids):
    sibs = _sibling_halves(gsends, "rs_sibling_halves_" + tag, ids[0])
    tsends = [_chip_sums(g, s, place) for g, s in zip(gsends, sibs)]
    return tsends, _exchange_chip_sums(tsends, "rs_exchange_" + tag, ids[1])


def _reduce_scatter_finish(begun, place, tag, ids, hold):
    tsends, recvs = begun
    recvs = _after(recvs, hold)
    return _join_halves([_final_sum(t, r, place) for t, r in zip(tsends, recvs)], "rs_join_halves_" + tag, ids[2])


SMALL_SHAPES = {"rel_table": (NUM_BUCKETS, 2 * N_HEADS), "norm_mix_g": (DEPTH, D_MODEL), "qnorm_a_g": (DEPTH, HEAD_DIM),
                "knorm_a_g": (DEPTH, HEAD_DIM), "qnorm_b_g": (DEPTH, HEAD_DIM), "knorm_b_g": (DEPTH, HEAD_DIM),
                "sink_b": (DEPTH, N_HEADS), "norm_ffn_g": (DEPTH, D_MODEL), "norm_ple_g": (DEPTH, D_MODEL)}


def _pack_small(vals):
    flat = jnp.concatenate([vals[n].astype(F32).reshape(-1) for n in SMALL])
    flat = jnp.concatenate([flat, jnp.zeros((SMALL_ROWS * LANES - flat.shape[0],), F32)])
    return flat.reshape(SMALL_ROWS, LANES)


def _unpack_small(packed):
    flat, out, off = packed.reshape(-1), {}, 0
    for n in SMALL:
        size = math.prod(SMALL_SHAPES[n])
        out[n] = flat[off:off + size].reshape(SMALL_SHAPES[n])
        off += size
    return out


def _adamw(w, gs, g_row, m, v, name):
    c1 = 1.0 - ADAM_B1 ** ADAM_STEP
    c2 = 1.0 - ADAM_B2 ** ADAM_STEP
    total, width = w.shape
    n_layers = len(gs)
    per = total // n_layers
    tm = max(t for t in range(8, 513, 8) if per % t == 0 and g_row % t == 0)
    nblk = per // tm

    def body(*refs):
        w_ref, g_refs = refs[0], refs[1:1 + n_layers]
        m_ref, v_ref, og, od, om, ov = refs[1 + n_layers:]
        layer = pl.program_id(0) // nblk
        g = g_refs[0][...]
        for l in range(1, n_layers):
            g = jnp.where(layer == l, g_refs[l][...], g)
        m_new = ADAM_B1 * m_ref[...] + (1.0 - ADAM_B1) * g
        v_new = ADAM_B2 * v_ref[...] + (1.0 - ADAM_B2) * (g * g)
        og[...] = g
        od[...] = -ADAM_LR * ((m_new / c1) / (jnp.sqrt(v_new / c2) + ADAM_EPS) + ADAM_WD * w_ref[...])
        om[...] = m_new
        ov[...] = v_new

    row = pl.BlockSpec((tm, width), lambda i: (i, 0))
    g_specs = [pl.BlockSpec((tm, width), lambda i, l=l: (g_row // tm + jnp.clip(i - l * nblk, 0, nblk - 1), 0))
               for l in range(n_layers)]
    return pl.pallas_call(
        body, out_shape=[jax.ShapeDtypeStruct((total, width), F32)] * 4, grid=(total // tm,),
        in_specs=[row] + g_specs + [row, row], out_specs=[row] * 4, name=name,
        compiler_params=_params("parallel"))(w, *gs, m, v)


def kernel(x, p, rel_table, norm_mix_g, w_in, qnorm_a_g, knorm_a_g, qnorm_b_g, knorm_b_g, sink_b, w_branch_a, w_branch_b, w_out, norm_ffn_g, w_ffn_gate, w_ffn_up, w_ffn_down, norm_ple_g, w_ple_gate, w_ple_proj, loss_target, m_rel_table, m_norm_mix_g, m_w_in, m_qnorm_a_g, m_knorm_a_g, m_qnorm_b_g, m_knorm_b_g, m_sink_b, m_w_branch_a, m_w_branch_b, m_w_out, m_norm_ffn_g, m_w_ffn_gate, m_w_ffn_up, m_w_ffn_down, m_norm_ple_g, m_w_ple_gate, m_w_ple_proj, v_rel_table, v_norm_mix_g, v_w_in, v_qnorm_a_g, v_knorm_a_g, v_qnorm_b_g, v_knorm_b_g, v_sink_b, v_w_branch_a, v_w_branch_b, v_w_out, v_norm_ffn_g, v_w_ffn_gate, v_w_ffn_up, v_w_ffn_down, v_norm_ple_g, v_w_ple_gate, v_w_ple_proj):
    given = dict(locals())

    def held(name, a):
        return jnp.swapaxes(a, 1, 2) if name in TRANSPOSED else a

    weights = {n: held(n, given[n]) for n in WEIGHTS}
    moments_m = {n: held(n, given["m_" + n]) for n in WEIGHTS}
    moments_v = {n: held(n, given["v_" + n]) for n in WEIGHTS}
    xi, yi, ci = _place()
    place = jnp.stack([2 * xi + yi, ci]).astype(jnp.int32)

    shards = [_pack_groups(weights, l, BF16) for l in range(DEPTH)]
    w_in0 = _allgather(shards[0][:1], "allgather_w_in_layer0", sequencer=9)
    rest0 = _allgather(_after(shards[0][1:], w_in0), "allgather_rest_layer0", sequencer=1)
    gathered = [w_in0 + rest0, None]
    small = {n: weights[n] for n in SMALL}

    def w_in_of(l, mark):
        return _unpack_full(gathered[l][:1] if l == 0 else _after(gathered[l][:1], mark), GROUPS[:1])["w_in"]

    def rest_of(l, mark):
        if l == 0:
            gathered[1] = _allgather(_after(shards[1], mark), "allgather_layer1", sequencer=2)
        return _unpack_full(_after(gathered[l][1:], mark), GROUPS[1:])

    loss, dx, gbig, gsmall, marks = _local_step(x[0], p[:, 0], loss_target[0], w_in_of, rest_of, small)

    gsends = [_pack_grads(gbig[l]) for l in range(DEPTH)]
    stages = {"layer1": (gsends[1], (3, 4, 5)), "rest_layer0": (gsends[0][1:], (6, 7, 8)),
              "w_in_layer0": (gsends[0][:1], (10, 11, 12))}
    begun = {tag: _reduce_scatter_begin(g, place, tag, ids) for tag, (g, ids) in stages.items()}

    def finish(tag, hold):
        return _reduce_scatter_finish(begun[tag], place, tag, stages[tag][1], hold)

    red1 = finish("layer1", marks[0]["attn_bwd_done"])
    rest0 = finish("rest_layer0", marks[0]["attn_bwd_done"])

    grads, delta, new_m, new_v = {}, {}, {}, {}

    def update(group, reduced):
        offs, _ = _group_rows(group)
        for n in group:
            shape = weights[n].shape
            two_d = lambda a: a.reshape(shape[0] * shape[1], shape[2])
            outs = _adamw(two_d(weights[n]), reduced, offs[n], two_d(moments_m[n]), two_d(moments_v[n]), "adamw_" + n)
            grads[n], delta[n], new_m[n], new_v[n] = (held(n, o.reshape(shape)) for o in outs)

    for gi in (1, 2):
        update(GROUPS[gi], _after([rest0[gi - 1], red1[gi]], begun["w_in_layer0"][0]))
    others_done = [dx] + [delta[n] for gi in (1, 2) for n in GROUPS[gi]]
    update(GROUPS[0], [finish("w_in_layer0", others_done)[0], red1[0]])
    small_grads = _allreduce_small(_pack_small(gsmall))
    g_, d_, m_, v_ = _adamw(_pack_small(weights), [small_grads], 0, _pack_small(moments_m), _pack_small(moments_v),
                            "adamw_small")
    grads.update(_unpack_small(g_))
    delta.update(_unpack_small(d_))
    new_m.update(_unpack_small(m_))
    new_v.update(_unpack_small(v_))

    loss = lax.psum(loss, ("x", "y", "c"))
    return (loss, dx[None], *[grads[n] for n in WEIGHTS], *[delta[n] for n in WEIGHTS],
            *[new_m[n] for n in WEIGHTS], *[new_v[n] for n in WEIGHTS])
```

```python
import functools
import math

import jax
import jax.numpy as jnp
from jax import lax
from jax.experimental import pallas as pl
from jax.experimental.pallas import tpu as pltpu
from jax.experimental.pallas import tpu_sc as plsc

F32 = jnp.float32
BF16 = jnp.bfloat16
MESH_ID = pl.DeviceIdType.MESH

SEQ = 2048
D_MODEL = 1024
DEPTH = 2
HEAD_DIM = 64
N_HEADS = 8
WIDTH = N_HEADS * HEAD_DIM
N_PAIRS = 4
ITEMS = 4
N_KV_B = 2
PLE_DIM = 256
D_FF = 2816
D_IN = 4352
OFF_QA, OFF_KA, OFF_VA, OFF_QB, OFF_KB, OFF_VB, OFF_GA, OFF_GB = 0, 512, 1024, 1536, 2048, 2176, 2304, 3328
DILATED = ((64, 1), (64, 4), (64, 16))
BLK_B = 128
NUM_BUCKETS = 32
MAX_DISTANCE = 1024
RMS_EPS = 1e-6
NEG_INF = -1e30
LANES = 128
VMEM_LIMIT = 48 * 1024 * 1024

ADAM_LR, ADAM_B1, ADAM_B2, ADAM_EPS, ADAM_WD, ADAM_STEP = 0.001, 0.9, 0.999, 1e-08, 0.01, 10

TRANSPOSED = ("w_in", "w_ffn_gate", "w_ffn_up")
BIG = (
    ("w_in", (D_IN, D_MODEL), 0),
    ("w_branch_a", (WIDTH, D_MODEL), 1),
    ("w_branch_b", (WIDTH, D_MODEL), 1),
    ("w_out", (D_MODEL, D_MODEL), 0),
    ("w_ffn_gate", (D_FF, D_MODEL), 0),
    ("w_ffn_up", (D_FF, D_MODEL), 0),
    ("w_ffn_down", (D_FF, D_MODEL), 0),
    ("w_ple_gate", (D_MODEL, D_MODEL), 0),
    ("w_ple_proj", (PLE_DIM, D_MODEL), 1),
)
SMALL = ("rel_table", "norm_mix_g", "qnorm_a_g", "knorm_a_g", "qnorm_b_g", "knorm_b_g", "sink_b",
         "norm_ffn_g", "norm_ple_g")
WEIGHTS = ("rel_table", "norm_mix_g", "w_in", "qnorm_a_g", "knorm_a_g", "qnorm_b_g", "knorm_b_g", "sink_b",
           "w_branch_a", "w_branch_b", "w_out", "norm_ffn_g", "w_ffn_gate", "w_ffn_up", "w_ffn_down",
           "norm_ple_g", "w_ple_gate", "w_ple_proj")
N_CHIPS = 4
SMALL_ROWS = 64


def _params(*sem):
    return pltpu.CompilerParams(dimension_semantics=sem, vmem_limit_bytes=VMEM_LIMIT)


def _pick(dim, target):
    for t in (target, 512, 256, 128, 64, 32, 16, 8):
        if t <= target and dim % t == 0:
            return t
    return dim


MM_VMEM_BUDGET = 40 * 1024 * 1024
STEP_OVERHEAD_S = 0.4e-6
TILE_DMA_BYTES_PER_S = 1.5e12
MXU_PEAK_FLOPS_PER_S = 1e15
MXU_TILE = 256


def _mm_dims(a, b, mode):
    if mode == "nn":
        return a.shape[0], b.shape[1], a.shape[1]
    if mode == "nt":
        return a.shape[0], b.shape[0], a.shape[1]
    return a.shape[1], b.shape[1], a.shape[0]


def _mm_tiles(m, n, pairs, tile_bytes, col_offsets):
    best = None
    for tm in (t for t in range(LANES, m + 1, LANES) if m % t == 0):
        for tn in (t for t in range(LANES, n + 1, LANES) if n % t == 0 and all(o % t == 0 for o in col_offsets)):
            io = sum(tm * k * ab + tn * k * bb for k, ab, bb in pairs) + tm * tn * sum(tile_bytes)
            casts = sum((tm * k * 2 if ab == 4 else 0) + (tn * k * 2 if bb == 4 else 0) for k, ab, bb in pairs)
            if 2 * io + len(pairs) * tm * tn * 4 + casts > MM_VMEM_BUDGET:
                continue
            mxu = sum(2.0 * m * n * k for k, _, _ in pairs) / MXU_PEAK_FLOPS_PER_S * (1.0 + MXU_TILE / tm)
            cost = (m // tm) * (n // tn) * STEP_OVERHEAD_S + io / TILE_DMA_BYTES_PER_S + mxu
            if best is None or (cost, -tm) < best[0]:
                best = ((cost, -tm), tm, tn)
    return best[1], best[2]


def _mm_fused(pairs, extras, epilogue, out_dtypes, name):
    m, n, _ = _mm_dims(*pairs[0])
    assert all(_mm_dims(*p)[:2] == (m, n) for p in pairs)
    tm, tn = _mm_tiles(
        m, n, [(_mm_dims(a, b, mode)[2], a.dtype.itemsize, b.dtype.itemsize) for a, b, mode in pairs],
        [e.dtype.itemsize for e, _ in extras] + [jnp.dtype(d).itemsize for d in out_dtypes], [off for _, off in extras])
    dims = {"nn": (((1,), (0,)), ((), ())), "nt": (((1,), (1,)), ((), ())), "tn": (((0,), (0,)), ((), ()))}
    in_specs, args = [], []
    for a, b, mode in pairs:
        k = _mm_dims(a, b, mode)[2]
        in_specs.append(pl.BlockSpec((k, tm), lambda i, j: (0, i)) if mode == "tn" else pl.BlockSpec((tm, k), lambda i, j: (i, 0)))
        in_specs.append(pl.BlockSpec((tn, k), lambda i, j: (j, 0)) if mode == "nt" else pl.BlockSpec((k, tn), lambda i, j: (0, j)))
        args += [a, b]
    for e, off in extras:
        in_specs.append(pl.BlockSpec((tm, tn), lambda i, j, o=off // tn: (i, o + j)))
        args.append(e)
    n_pairs, n_in = len(pairs), 2 * len(pairs) + len(extras)

    def body(*refs):
        products = [lax.dot_general(refs[2 * p][...].astype(BF16), refs[2 * p + 1][...].astype(BF16), dims[pairs[p][2]],
                                    preferred_element_type=F32) for p in range(n_pairs)]
        outs = epilogue(products, [r[...] for r in refs[2 * n_pairs:n_in]])
        for r, o in zip(refs[n_in:], outs):
            r[...] = o.astype(r.dtype)

    tile = pl.BlockSpec((tm, tn), lambda i, j: (i, j))
    return pl.pallas_call(
        body, out_shape=[jax.ShapeDtypeStruct((m, n), d) for d in out_dtypes], grid=(m // tm, n // tn),
        in_specs=in_specs, out_specs=[tile] * len(out_dtypes), name=name,
        compiler_params=_params("parallel", "parallel"))(*args)


def _mm(a, b, mode, out_dtype, name, res=None):
    if res is None:
        return _mm_fused([(a, b, mode)], [], lambda products, extra: products, [out_dtype], name)[0]
    return _mm_fused([(a, b, mode)], [(res, 0)], lambda products, extra: [products[0] + extra[0]], [out_dtype], name)[0]


def _ew(fn, ins, out_dtypes, *, width, bw, name, vecs=(), tm=256):
    rows = ins[0][0].shape[0]
    tm = _pick(rows, tm)
    n_in = len(ins) + len(vecs)

    def col_map(off_blocks):
        return lambda i, j: (i, off_blocks + j)

    in_specs = [pl.BlockSpec((tm, bw), col_map(off // bw)) for _, off in ins]
    in_specs += [pl.BlockSpec((1, bw), lambda i, j: (0, j)) for _ in vecs]

    def body(*refs):
        outs = fn(*[r[...] for r in refs[:n_in]])
        for r, o in zip(refs[n_in:], outs):
            r[...] = o.astype(r.dtype)

    return pl.pallas_call(
        body, out_shape=[jax.ShapeDtypeStruct((rows, width), dt) for dt in out_dtypes],
        grid=(rows // tm, width // bw), in_specs=in_specs,
        out_specs=[pl.BlockSpec((tm, bw), lambda i, j: (i, j)) for _ in out_dtypes],
        name=name, compiler_params=_params("parallel", "parallel"))(*[a for a, _ in ins], *vecs)


def _sigmoid(x):
    return 1.0 / (1.0 + jnp.exp(-x))


def _seg_sum(v):
    outs = []
    for k in range(v.shape[1] // LANES):
        vp = v[:, k * LANES:(k + 1) * LANES]
        left = lax.broadcasted_iota(jnp.int32, vp.shape, 1) < HEAD_DIM
        sl = jnp.sum(jnp.where(left, vp, 0.0), axis=-1, keepdims=True)
        sr = jnp.sum(jnp.where(left, 0.0, vp), axis=-1, keepdims=True)
        outs.append(jnp.where(left, sl, sr))
    return outs[0] if len(outs) == 1 else jnp.concatenate(outs, axis=1)


def _seg_rstd(x):
    return lax.rsqrt(_seg_sum(x * x) * (1.0 / HEAD_DIM) + RMS_EPS)


def _rms_fwd(x, g, name):
    rows, d = x.shape
    tm = 256

    def body(x_ref, g_ref, h_ref):
        xv = x_ref[...]
        r = lax.rsqrt(jnp.mean(xv * xv, axis=-1, keepdims=True) + RMS_EPS)
        h_ref[...] = ((xv * r) * g_ref[...]).astype(BF16)

    return pl.pallas_call(
        body, out_shape=jax.ShapeDtypeStruct((rows, d), BF16), grid=(rows // tm,),
        in_specs=[pl.BlockSpec((tm, d), lambda i: (i, 0)), pl.BlockSpec((1, d), lambda i: (0, 0))],
        out_specs=pl.BlockSpec((tm, d), lambda i: (i, 0)), name=name,
        compiler_params=_params("parallel"))(x, g)


def _rms_bwd(x, g, dh, dres, name):
    rows, d = x.shape
    tm = 256

    def body(x_ref, g_ref, dh_ref, dres_ref, dx_ref, dxb_ref, dg_ref):
        xv = x_ref[...]
        r = lax.rsqrt(jnp.mean(xv * xv, axis=-1, keepdims=True) + RMS_EPS)
        xh = xv * r
        dhv = dh_ref[...]
        dxh = dhv * g_ref[...]
        dxv = dres_ref[...] + r * (dxh - xh * jnp.mean(dxh * xh, axis=-1, keepdims=True))
        dx_ref[...] = dxv
        dxb_ref[...] = dxv.astype(BF16)
        part = jnp.sum(dhv * xh, axis=0, keepdims=True)

        @pl.when(pl.program_id(0) == 0)
        def _():
            dg_ref[...] = part

        @pl.when(pl.program_id(0) > 0)
        def _():
            dg_ref[...] += part

    row = pl.BlockSpec((tm, d), lambda i: (i, 0))
    vec = pl.BlockSpec((1, d), lambda i: (0, 0))
    return pl.pallas_call(
        body, out_shape=[jax.ShapeDtypeStruct((rows, d), F32), jax.ShapeDtypeStruct((rows, d), BF16),
                         jax.ShapeDtypeStruct((1, d), F32)],
        grid=(rows // tm,), in_specs=[row, vec, row, row], out_specs=[row, row, vec],
        name=name, compiler_params=_params("arbitrary"))(x, g, dh, dres)


def _loss_grad(y, t):
    rows, d = y.shape
    tm = 256

    def body(y_ref, t_ref, dy_ref, l_ref):
        e = y_ref[...] - t_ref[...]
        dy_ref[...] = e * (1.0 / d)
        part = jnp.zeros((1, LANES), F32) + jnp.sum(e * e) * (0.5 / d)

        @pl.when(pl.program_id(0) == 0)
        def _():
            l_ref[...] = part

        @pl.when(pl.program_id(0) > 0)
        def _():
            l_ref[...] += part

    row = pl.BlockSpec((tm, d), lambda i: (i, 0))
    return pl.pallas_call(
        body, out_shape=[jax.ShapeDtypeStruct((rows, d), F32), jax.ShapeDtypeStruct((1, LANES), F32)],
        grid=(rows // tm,), in_specs=[row, row], out_specs=[row, pl.BlockSpec((1, LANES), lambda i: (0, 0))],
        name="loss_grad", compiler_params=_params("arbitrary"))(y, t)


def _put_pairs(ref, val):
    for hp in range(N_PAIRS):
        ref[hp] = val[:, hp * LANES:(hp + 1) * LANES].astype(ref.dtype)


def _get_pairs(ref):
    return jnp.concatenate([ref[hp] for hp in range(N_PAIRS)], axis=1)


def _swap_halves(v):
    return pltpu.roll(v, HEAD_DIM, axis=1)


def _expand_kv(kv):
    left = lax.broadcasted_iota(jnp.int32, kv.shape, 1) < HEAD_DIM
    sw = _swap_halves(kv)
    h0 = jnp.where(left, kv, sw)
    h1 = jnp.where(left, sw, kv)
    return jnp.concatenate([h0, h0, h1, h1], axis=1)


def _reduce_kv(dkv):
    left = lax.broadcasted_iota(jnp.int32, (dkv.shape[0], LANES), 1) < HEAD_DIM
    t = dkv[:, 0:LANES] + dkv[:, LANES:2 * LANES]
    u = dkv[:, 2 * LANES:3 * LANES] + dkv[:, 3 * LANES:4 * LANES]
    t = t + _swap_halves(t)
    u = u + _swap_halves(u)
    return jnp.where(left, t, u)


def _qknorm_fwd(proj, gqa, gka, gqb, gkb):
    rows = proj.shape[0]
    tm = 256

    def body(qa_ref, ka_ref, va_ref, qb_ref, kb_ref, vb_ref, gqa_ref, gka_ref, gqb_ref, gkb_ref,
             oqa, oka, ova, oqb, okb, ovb):
        for src, g_ref, dst in ((qa_ref, gqa_ref, oqa), (ka_ref, gka_ref, oka), (qb_ref, gqb_ref, oqb)):
            xv = src[...]
            _put_pairs(dst, (xv * _seg_rstd(xv)) * g_ref[...])
        _put_pairs(ova, va_ref[...])
        kv = kb_ref[...]
        _put_pairs(okb, _expand_kv((kv * _seg_rstd(kv)) * gkb_ref[...]))
        _put_pairs(ovb, _expand_kv(vb_ref[...]))

    def win(width, off):
        return pl.BlockSpec((tm, width), lambda i: (i, off // width))

    vec = lambda w: pl.BlockSpec((1, w), lambda i: (0, 0))
    out = pl.BlockSpec((N_PAIRS, tm, LANES), lambda i: (0, i, 0))
    return pl.pallas_call(
        body, out_shape=[jax.ShapeDtypeStruct((N_PAIRS, rows, LANES), F32)] * 6, grid=(rows // tm,),
        in_specs=[win(WIDTH, OFF_QA), win(WIDTH, OFF_KA), win(WIDTH, OFF_VA), win(WIDTH, OFF_QB),
                  win(LANES, OFF_KB), win(LANES, OFF_VB), vec(WIDTH), vec(WIDTH), vec(WIDTH), vec(LANES)],
        out_specs=[out] * 6, name="qknorm_fwd", compiler_params=_params("parallel"))(
            proj, proj, proj, proj, proj, proj, gqa, gka, gqb, gkb)


def _norm_bwd(xv, g, dy):
    r = _seg_rstd(xv)
    xh = xv * r
    dxh = dy * g
    dx = r * (dxh - xh * (_seg_sum(dxh * xh) * (1.0 / HEAD_DIM)))
    return dx, jnp.sum(dy * xh, axis=0, keepdims=True)


def _qknorm_bwd(proj, gqa, gka, gqb, gkb, dqa, dka, dva, dqb, dkb, dvb, dga, dgb):
    rows = proj.shape[0]
    tm = 256
    n_a = len(dqa)

    def body(*refs):
        qa_ref, ka_ref, qb_ref, kb_ref, gqa_ref, gka_ref, gqb_ref, gkb_ref = refs[:8]
        pos = 8
        dqa_refs, dka_refs, dva_refs = refs[pos:pos + n_a], refs[pos + n_a:pos + 2 * n_a], refs[pos + 2 * n_a:pos + 3 * n_a]
        pos += 3 * n_a
        dqb_ref, dkb_ref, dvb_ref, dga_ref, dgb_ref = refs[pos:pos + 5]
        dproj_ref, ogqa, ogka, ogqb, ogkb = refs[pos + 5:]

        def total(rs):
            acc = _get_pairs(rs[0])
            for r in rs[1:]:
                acc = acc + _get_pairs(r)
            return acc

        dx_qa, p_qa = _norm_bwd(qa_ref[...], gqa_ref[...], total(dqa_refs))
        dx_ka, p_ka = _norm_bwd(ka_ref[...], gka_ref[...], total(dka_refs))
        dx_qb, p_qb = _norm_bwd(qb_ref[...], gqb_ref[...], _get_pairs(dqb_ref))
        dx_kb, p_kb = _norm_bwd(kb_ref[...], gkb_ref[...], _reduce_kv(_get_pairs(dkb_ref)))
        dproj_ref[:, OFF_QA:OFF_QA + WIDTH] = dx_qa.astype(BF16)
        dproj_ref[:, OFF_KA:OFF_KA + WIDTH] = dx_ka.astype(BF16)
        dproj_ref[:, OFF_VA:OFF_VA + WIDTH] = total(dva_refs).astype(BF16)
        dproj_ref[:, OFF_QB:OFF_QB + WIDTH] = dx_qb.astype(BF16)
        dproj_ref[:, OFF_KB:OFF_KB + LANES] = dx_kb.astype(BF16)
        dproj_ref[:, OFF_VB:OFF_VB + LANES] = _reduce_kv(_get_pairs(dvb_ref)).astype(BF16)
        dproj_ref[:, OFF_GA:OFF_GB] = dga_ref[...]
        dproj_ref[:, OFF_GB:D_IN] = dgb_ref[...]
        first = pl.program_id(0) == 0
        for o_ref, part in ((ogqa, p_qa), (ogka, p_ka), (ogqb, p_qb), (ogkb, p_kb)):
            @pl.when(first)
            def _(o_ref=o_ref, part=part):
                o_ref[...] = part

            @pl.when(jnp.logical_not(first))
            def _(o_ref=o_ref, part=part):
                o_ref[...] += part

    def win(width, off):
        return pl.BlockSpec((tm, width), lambda i: (i, off // width))

    vec = lambda w: pl.BlockSpec((1, w), lambda i: (0, 0))
    row = lambda w: pl.BlockSpec((tm, w), lambda i: (i, 0))
    in_specs = [win(WIDTH, OFF_QA), win(WIDTH, OFF_KA), win(WIDTH, OFF_QB), win(LANES, OFF_KB),
                vec(WIDTH), vec(WIDTH), vec(WIDTH), vec(LANES)]
    in_specs += [pl.BlockSpec((N_PAIRS, tm, LANES), lambda i: (0, i, 0))] * (3 * n_a + 3) + [row(D_MODEL)] * 2
    return pl.pallas_call(
        body,
        out_shape=[jax.ShapeDtypeStruct((rows, D_IN), BF16), jax.ShapeDtypeStruct((1, WIDTH), F32),
                   jax.ShapeDtypeStruct((1, WIDTH), F32), jax.ShapeDtypeStruct((1, WIDTH), F32),
                   jax.ShapeDtypeStruct((1, LANES), F32)],
        grid=(rows // tm,), in_specs=in_specs,
        out_specs=[row(D_IN), vec(WIDTH), vec(WIDTH), vec(WIDTH), vec(LANES)],
        name="qknorm_bwd", compiler_params=_params("arbitrary"))(
            proj, proj, proj, proj, gqa, gka, gqb, gkb, *dqa, *dka, *dva, dqb, dkb, dvb, dga, dgb)


def _t5_bucket(rel):
    half_b = NUM_BUCKETS // 2
    max_exact = half_b // 2
    sign = jnp.where(rel > 0, half_b, 0)
    n = jnp.abs(rel)
    nf = jnp.maximum(n, 1).astype(F32)
    large = max_exact + (jnp.log(nf / max_exact) / math.log(MAX_DISTANCE / max_exact)
                         * (half_b - max_exact)).astype(jnp.int32)
    large = jnp.minimum(large, half_b - 1)
    return sign + jnp.where(n < max_exact, n, large)


def _band_buckets(blk, dilation):
    i = jnp.arange(blk, dtype=jnp.int32)[:, None]
    j = jnp.arange(3 * blk, dtype=jnp.int32)[None, :]
    rel = j - blk - i
    return jnp.where(jnp.abs(rel) <= blk, _t5_bucket(rel * dilation), -1)


def _bias_tiles(table, buckets, head_off, name):
    blk = buckets.shape[0]

    def body(tab_ref, bk_ref, o_ref):
        h = pl.program_id(0) + head_off
        bk = bk_ref[...]
        acc = jnp.full(bk.shape, NEG_INF, F32)
        for b in range(NUM_BUCKETS):
            acc = jnp.where(bk == b, tab_ref[b, h], acc)
        o_ref[0] = acc

    return pl.pallas_call(
        body, out_shape=jax.ShapeDtypeStruct((N_HEADS, blk, 3 * blk), F32), grid=(N_HEADS,),
        in_specs=[pl.BlockSpec(memory_space=pltpu.SMEM), pl.BlockSpec((blk, 3 * blk), lambda h: (0, 0))],
        out_specs=pl.BlockSpec((1, blk, 3 * blk), lambda h: (h, 0, 0)),
        name=name, compiler_params=_params("parallel"))(table, buckets)


def _table_grad(dbias, buckets, name):
    blk = buckets.shape[0]

    def body(db_ref, bk_ref, o_ref):
        bk = bk_ref[...]
        dbv = db_ref[0]
        lane = lax.broadcasted_iota(jnp.int32, (1, LANES), 1)
        acc = jnp.zeros((1, LANES), F32)
        for b in range(NUM_BUCKETS):
            acc = jnp.where(lane == b, jnp.sum(jnp.where(bk == b, dbv, 0.0)), acc)
        o_ref[0] = acc

    out = pl.pallas_call(
        body, out_shape=jax.ShapeDtypeStruct((N_HEADS, 1, LANES), F32), grid=(N_HEADS,),
        in_specs=[pl.BlockSpec((1, blk, 3 * blk), lambda h: (h, 0, 0)), pl.BlockSpec((blk, 3 * blk), lambda h: (0, 0))],
        out_specs=pl.BlockSpec((1, 1, LANES), lambda h: (h, 0, 0)),
        name=name, compiler_params=_params("parallel"))(dbias, buckets)
    return out[:, 0, :NUM_BUCKETS]


def _dot_nt(a, b):
    return lax.dot_general(a, b, (((1,), (1,)), ((), ())), preferred_element_type=F32)


def _dot_tn(a, b):
    return lax.dot_general(a, b, (((0,), (0,)), ((), ())), preferred_element_type=F32)


def _stack_pair(x2, left):
    return jnp.concatenate([jnp.where(left, x2, 0.0), jnp.where(left, 0.0, x2)], axis=0).astype(BF16)


def _attn_geometry(blk, d):
    chunk = blk * ITEMS if d == 1 else blk * d
    groups = 1 if d == 1 else d // ITEMS
    halo = blk if d == 1 else chunk
    return chunk, groups, halo


def _item_rows(ref, j, r0, blk, d):
    if d == 1:
        return ref[j * blk:(j + 1) * blk, :]
    return ref[pl.ds(r0 + j, blk, stride=d), :]


def _item_penalty(t, nct, j, blk, d):
    first_ok, last_ok = t > 0, t < nct - 1
    if d == 1:
        first_ok = True if j > 0 else first_ok
        last_ok = True if j < ITEMS - 1 else last_ok
    col = lax.broadcasted_iota(jnp.int32, (1, 3 * blk), 1)
    ok = jnp.logical_and(jnp.logical_or(col >= blk, first_ok), jnp.logical_or(col < 2 * blk, last_ok))
    return jnp.where(ok, 0.0, NEG_INF).astype(F32)


def _attn_specs(seq, blk, d, step_of):
    chunk, _, halo = _attn_geometry(blk, d)
    per, last = chunk // halo, seq // halo - 1
    cur = pl.BlockSpec((None, chunk, LANES), lambda hp, t: (hp, step_of(t), 0))
    prev = pl.BlockSpec((None, halo, LANES), lambda hp, t: (hp, jnp.clip(step_of(t) * per - 1, 0, last), 0))
    nxt = pl.BlockSpec((None, halo, LANES), lambda hp, t: (hp, jnp.minimum((step_of(t) + 1) * per, last), 0))
    return cur, prev, nxt


def _attn_fwd(q, k, v, bias, sink, blk, d, name):
    _, seq, _ = q.shape
    chunk, groups, _ = _attn_geometry(blk, d)
    nct = seq // chunk
    has_sink = sink is not None
    scale = HEAD_DIM ** -0.5

    def body(*refs):
        q_ref, kp, kc, kn, vp, vc, vn, b_ref = refs[:8]
        s_ref = refs[8] if has_sink else None
        o_ref, l_ref = refs[-2], refs[-1]
        t = pl.program_id(1)
        left = lax.broadcasted_iota(jnp.int32, (1, LANES), 1) < HEAD_DIM
        bias2 = b_ref[...]
        if d == 1:
            kwin = jnp.concatenate([kp[...], kc[...], kn[...]], axis=0).astype(BF16)
            vwin = jnp.concatenate([vp[...], vc[...], vn[...]], axis=0).astype(BF16)

        def group(r0):
            scores, vcats = [], []
            for j in range(ITEMS):
                qs = _stack_pair(_item_rows(q_ref, j, r0, blk, d) * scale, left)
                if d == 1:
                    kcat, vcat = kwin[j * blk:(j + 3) * blk], vwin[j * blk:(j + 3) * blk]
                else:
                    kcat = jnp.concatenate([_item_rows(r, j, r0, blk, d) for r in (kp, kc, kn)], axis=0).astype(BF16)
                    vcat = jnp.concatenate([_item_rows(r, j, r0, blk, d) for r in (vp, vc, vn)], axis=0).astype(BF16)
                scores.append(_dot_nt(qs, kcat) + bias2 + _item_penalty(t, nct, j, blk, d))
                vcats.append(vcat)
            ms = [jnp.max(s, axis=-1, keepdims=True) for s in scores]
            if has_sink:
                sk = s_ref[...]
                ms = [jnp.maximum(m, sk) for m in ms]
            ps = [jnp.exp(s - m) for s, m in zip(scores, ms)]
            dens = [jnp.sum(p, axis=-1, keepdims=True) for p in ps]
            if has_sink:
                dens = [den + jnp.exp(sk - m) for den, m in zip(dens, ms)]
            pns = [(p * (1.0 / den)).astype(BF16) for p, den in zip(ps, dens)]
            lses = [m + jnp.log(den) for m, den in zip(ms, dens)]
            for j in range(ITEMS):
                o2 = jnp.dot(pns[j], vcats[j], preferred_element_type=F32)
                o_val = jnp.where(left, o2[:blk], o2[blk:])
                l_val = jnp.where(left, lses[j][:blk], lses[j][blk:])
                if d == 1:
                    o_ref[j * blk:(j + 1) * blk, :] = o_val
                    l_ref[j * blk:(j + 1) * blk, :] = l_val
                else:
                    o_ref[pl.ds(r0 + j, blk, stride=d), :] = o_val
                    l_ref[pl.ds(r0 + j, blk, stride=d), :] = l_val

        if groups == 1:
            group(0)
        else:
            def step(g, carry):
                group(g * ITEMS)
                return carry

            lax.fori_loop(0, groups, step, 0)

    cur, prev, nxt = _attn_specs(seq, blk, d, lambda t: t)
    in_specs = [cur, prev, cur, nxt, prev, cur, nxt, pl.BlockSpec((2 * blk, 3 * blk), lambda hp, t: (hp, 0))]
    args = [q, k, k, k, v, v, v, bias]
    if has_sink:
        in_specs.append(pl.BlockSpec((2 * blk, 1), lambda hp, t: (hp, 0)))
        args.append(sink)
    return pl.pallas_call(
        body, out_shape=[jax.ShapeDtypeStruct(q.shape, F32)] * 2, grid=(N_PAIRS, nct),
        in_specs=in_specs, out_specs=[cur, cur], name=name,
        compiler_params=_params("parallel", "parallel"))(*args)


def _attn_bwd(q, k, v, do, lse, delta, bias, sink, blk, d, name):
    _, seq, _ = q.shape
    chunk, groups, halo = _attn_geometry(blk, d)
    nct = seq // chunk
    has_sink = sink is not None
    n_in = 12 if has_sink else 11
    scale = HEAD_DIM ** -0.5

    def body(*refs):
        q_ref, kp, kc, kn, vp, vc, vn, do_ref, l_ref, d_ref, b_ref = refs[:11]
        s_ref = refs[11] if has_sink else None
        dq_ref, dk_ref, dv_ref, db_ref = refs[n_in:n_in + 4]
        ds_ref = refs[n_in + 4] if has_sink else None
        wk, wv = refs[-2], refs[-1]
        t = pl.program_id(1)

        @pl.when(t == 0)
        def _():
            wk[...] = jnp.zeros_like(wk)
            wv[...] = jnp.zeros_like(wv)
            db_ref[...] = jnp.zeros_like(db_ref)
            if has_sink:
                ds_ref[...] = jnp.zeros_like(ds_ref)

        @pl.when(t > 0)
        def _():
            for w in (wk, wv):
                keep = w[chunk:2 * chunk + halo]
                w[0:chunk + halo] = keep
                w[chunk + halo:2 * chunk + halo] = jnp.zeros((chunk, LANES), F32)

        @pl.when(t < nct)
        def _():
            lane = lax.broadcasted_iota(jnp.int32, (1, LANES), 1)
            left = lane < HEAD_DIM
            bias2 = b_ref[...]
            if d == 1:
                kwin = jnp.concatenate([kp[...], kc[...], kn[...]], axis=0).astype(BF16)
                vwin = jnp.concatenate([vp[...], vc[...], vn[...]], axis=0).astype(BF16)

            def group(r0):
                qss, doss, kcats, scores, dps, lcols, dcols = [], [], [], [], [], [], []
                for j in range(ITEMS):
                    qs = _stack_pair(_item_rows(q_ref, j, r0, blk, d) * scale, left)
                    dos = _stack_pair(_item_rows(do_ref, j, r0, blk, d), left)
                    if d == 1:
                        kcat, vcat = kwin[j * blk:(j + 3) * blk], vwin[j * blk:(j + 3) * blk]
                    else:
                        kcat = jnp.concatenate([_item_rows(r, j, r0, blk, d) for r in (kp, kc, kn)], axis=0).astype(BF16)
                        vcat = jnp.concatenate([_item_rows(r, j, r0, blk, d) for r in (vp, vc, vn)], axis=0).astype(BF16)
                    l2, d2 = _item_rows(l_ref, j, r0, blk, d), _item_rows(d_ref, j, r0, blk, d)
                    lcols.append(jnp.concatenate([jnp.max(jnp.where(left, l2, NEG_INF), axis=-1, keepdims=True),
                                                  jnp.max(jnp.where(left, NEG_INF, l2), axis=-1, keepdims=True)], axis=0))
                    dcols.append(jnp.concatenate([jnp.sum(jnp.where(lane == 0, d2, 0.0), axis=-1, keepdims=True),
                                                  jnp.sum(jnp.where(lane == HEAD_DIM, d2, 0.0), axis=-1, keepdims=True)],
                                                 axis=0))
                    scores.append(_dot_nt(qs, kcat) + bias2 + _item_penalty(t, nct, j, blk, d))
                    dps.append(_dot_nt(dos, vcat))
                    qss.append(qs)
                    doss.append(dos)
                    kcats.append(kcat)
                ps = [jnp.exp(s - lc) for s, lc in zip(scores, lcols)]
                dss = [p * (dp - dc) for p, dp, dc in zip(ps, dps, dcols)]
                db_ref[...] += functools.reduce(lambda a, b: a + b, dss)
                if has_sink:
                    sk = s_ref[...]
                    ds_ref[...] -= functools.reduce(lambda a, b: a + b, [dc * jnp.exp(sk - lc) for dc, lc in zip(dcols, lcols)])
                dsbs = [ds.astype(BF16) for ds in dss]
                for j in range(ITEMS):
                    dq2 = jnp.dot(dsbs[j], kcats[j], preferred_element_type=F32) * scale
                    dq_val = jnp.where(left, dq2[:blk], dq2[blk:])
                    if d == 1:
                        dq_ref[j * blk:(j + 1) * blk, :] = dq_val
                    else:
                        dq_ref[pl.ds(r0 + j, blk, stride=d), :] = dq_val
                news = [(_dot_tn(dsbs[j], qss[j]), _dot_tn(ps[j].astype(BF16), doss[j])) for j in range(ITEMS)]
                if d == 1:
                    for which, w in enumerate((wk, wv)):
                        for b in range(ITEMS + 2):
                            parts = [news[j][which][(b - j) * blk:(b - j + 1) * blk] for j in range(ITEMS) if 0 <= b - j < 3]
                            w[chunk + (b - 1) * blk:chunk + b * blk, :] += functools.reduce(lambda x, y: x + y, parts)
                else:
                    for j in range(ITEMS):
                        for which, w in enumerate((wk, wv)):
                            for c in range(3):
                                w[pl.ds(c * chunk + r0 + j, blk, stride=d), :] += news[j][which][c * blk:(c + 1) * blk]

            if groups == 1:
                group(0)
            else:
                def step(g, carry):
                    group(g * ITEMS)
                    return carry

                lax.fori_loop(0, groups, step, 0)

        dk_ref[...] = wk[0:chunk]
        dv_ref[...] = wv[0:chunk]

    cur, prev, nxt = _attn_specs(seq, blk, d, lambda t: jnp.minimum(t, nct - 1))
    lag = pl.BlockSpec((None, chunk, LANES), lambda hp, t: (hp, jnp.maximum(t - 1, 0), 0))
    band = pl.BlockSpec((2 * blk, 3 * blk), lambda hp, t: (hp, 0))
    col = pl.BlockSpec((2 * blk, 1), lambda hp, t: (hp, 0))
    in_specs = [cur, prev, cur, nxt, prev, cur, nxt, cur, cur, cur, band]
    args = [q, k, k, k, v, v, v, do, lse, delta, bias]
    out_shape = [jax.ShapeDtypeStruct(q.shape, F32)] * 3 + [jax.ShapeDtypeStruct((N_HEADS * blk, 3 * blk), F32)]
    out_specs = [cur, lag, lag, band]
    if has_sink:
        in_specs.append(col)
        args.append(sink)
        out_shape.append(jax.ShapeDtypeStruct((N_HEADS * blk, 1), F32))
        out_specs.append(col)
    window = pltpu.VMEM((2 * chunk + halo, LANES), F32)
    return pl.pallas_call(
        body, out_shape=out_shape, grid=(N_PAIRS, nct + 1), in_specs=in_specs, out_specs=out_specs,
        scratch_shapes=[window, window], name=name,
        compiler_params=_params("arbitrary", "arbitrary"))(*args)


def _combine_patterns(outs, lses):
    _, rows, _ = outs[0].shape
    tm = 256
    n = len(outs)

    def body(*refs):
        o_refs, l_refs = refs[:n], refs[n:2 * n]
        y_ref, lse_ref = refs[2 * n], refs[2 * n + 1]
        for hp in range(N_PAIRS):
            ls = [r[hp] for r in l_refs]
            m = functools.reduce(jnp.maximum, ls)
            es = [jnp.exp(l - m) for l in ls]
            den = functools.reduce(lambda a, b: a + b, es)
            num = functools.reduce(lambda a, b: a + b, [e * r[hp] for e, r in zip(es, o_refs)])
            y_ref[:, hp * LANES:(hp + 1) * LANES] = num / den
            lse_ref[hp] = m + jnp.log(den)

    pm = pl.BlockSpec((N_PAIRS, tm, LANES), lambda i: (0, i, 0))
    return pl.pallas_call(
        body, out_shape=[jax.ShapeDtypeStruct((rows, WIDTH), F32), jax.ShapeDtypeStruct((N_PAIRS, rows, LANES), F32)],
        grid=(rows // tm,), in_specs=[pm] * (2 * n), out_specs=[pl.BlockSpec((tm, WIDTH), lambda i: (i, 0)), pm],
        name="combine_a", compiler_params=_params("parallel"))(*outs, *lses)


def _pairs_to_tokens(a):
    _, rows, _ = a.shape
    tm = 256

    def body(a_ref, o_ref):
        o_ref[...] = _get_pairs(a_ref)

    return pl.pallas_call(
        body, out_shape=jax.ShapeDtypeStruct((rows, WIDTH), a.dtype), grid=(rows // tm,),
        in_specs=[pl.BlockSpec((N_PAIRS, tm, LANES), lambda i: (0, i, 0))],
        out_specs=pl.BlockSpec((tm, WIDTH), lambda i: (i, 0)), name="pairs_to_tokens",
        compiler_params=_params("parallel"))(a)


def _attn_bwd_prep(dy, y, name):
    rows = dy.shape[0]
    tm = 256

    def body(dy_ref, y_ref, do_ref, dl_ref):
        dyv = dy_ref[...]
        _put_pairs(do_ref, dyv)
        _put_pairs(dl_ref, _seg_sum(dyv * y_ref[...]))

    tok = pl.BlockSpec((tm, WIDTH), lambda i: (i, 0))
    pm = pl.BlockSpec((N_PAIRS, tm, LANES), lambda i: (0, i, 0))
    return pl.pallas_call(
        body, out_shape=[jax.ShapeDtypeStruct((N_PAIRS, rows, LANES), F32)] * 2, grid=(rows // tm,),
        in_specs=[tok, tok], out_specs=[pm, pm], name=name, compiler_params=_params("parallel"))(dy, y)


def _tile_gain(g, reps):
    return jnp.tile(g[None, :], (1, reps))


def _local_step(x, p, target, w_in_of, rest_of, small):
    rel_table = small["rel_table"]
    buckets_a = [_band_buckets(blk, d) for blk, d in DILATED]
    buckets_b = _band_buckets(BLK_B, 1)
    bias_a = [_bias_tiles(rel_table, bk, 0, "bias_a").reshape(N_HEADS * bk.shape[0], -1) for bk in buckets_a]
    bias_b = _bias_tiles(rel_table, buckets_b, N_HEADS, "bias_b").reshape(N_HEADS * BLK_B, -1)

    saved = []
    for l in range(DEPTH):
        g_mix, g_ffn, g_ple = (small[n][l][None, :] for n in ("norm_mix_g", "norm_ffn_g", "norm_ple_g"))
        gqa, gka, gqb = (_tile_gain(small[n][l], N_HEADS) for n in ("qnorm_a_g", "knorm_a_g", "qnorm_b_g"))
        gkb = _tile_gain(small["knorm_b_g"][l], N_KV_B)
        sink = jnp.repeat(small["sink_b"][l], BLK_B)[:, None]

        h = _rms_fwd(x, g_mix, "rms_mix")
        w_in = w_in_of(l, h)
        proj = _mm(h, w_in, "nt", F32, "mm_in")
        qa, ka, va, qb, kb, vb = _qknorm_fwd(proj, gqa, gka, gqb, gkb)
        outs, lses = [], []
        for (blk, d), bias in zip(DILATED, bias_a):
            o, ls = _attn_fwd(qa, ka, va, bias, None, blk, d, f"attn_a{d}_fwd")
            outs.append(o)
            lses.append(ls)
        ya, lse_a = _combine_patterns(outs, lses)
        yb, lse_b = _attn_fwd(qb, kb, vb, bias_b, sink, BLK_B, 1, "attn_b_fwd")
        yb = _pairs_to_tokens(yb)
        w = dict(rest_of(l, yb), w_in=w_in)
        def gate(products, extra):
            (ca_, cb_), (ga_, gb_) = products, extra
            return _sigmoid(ga_) * ca_ + _sigmoid(gb_) * cb_, ca_, cb_

        merged, ca, cb = _mm_fused([(ya, w["w_branch_a"], "nn"), (yb, w["w_branch_b"], "nn")],
                                   [(proj, OFF_GA), (proj, OFF_GB)], gate, [BF16, BF16, BF16], "mm_branches_gate")
        x1 = _mm(merged, w["w_out"], "nn", F32, "mm_out", res=x)

        h2 = _rms_fwd(x1, g_ffn, "rms_ffn")

        def swiglu(products, extra):
            a_, u_ = products
            return (a_ * _sigmoid(a_)) * u_, a_, u_

        hid, a, u = _mm_fused([(h2, w["w_ffn_gate"], "nt"), (h2, w["w_ffn_up"], "nt")], [], swiglu,
                              [BF16, BF16, BF16], "mm_ffn_gate_up")
        x2 = _mm(hid, w["w_ffn_down"], "nn", F32, "mm_ffn_down", res=x1)

        h3 = _rms_fwd(x2, g_ple, "rms_ple")

        def ple(products, extra):
            z_, e_ = products
            return extra[0] + _sigmoid(z_) * e_, z_, e_

        x3, z, e = _mm_fused([(h3, w["w_ple_gate"], "nn"), (p[l], w["w_ple_proj"], "nn")], [(x2, 0)], ple,
                             [F32, BF16, BF16], "mm_ple")
        saved.append(dict(w=w, x0=x, h=h, proj=proj, qa=qa, ka=ka, va=va, qb=qb, kb=kb, vb=vb, ya=ya, lse_a=lse_a,
                          yb=yb, lse_b=lse_b, ca=ca, cb=cb, merged=merged, x1=x1, h2=h2, a=a, u=u, hid=hid,
                          x2=x2, h3=h3, z=z, e=e))
        x = x3

    dx, loss_acc = _loss_grad(x, target)
    loss = loss_acc[0, 0]

    gbig = [{} for _ in range(DEPTH)]
    marks = [{} for _ in range(DEPTH)]
    gsmall = {n: [None] * DEPTH for n in SMALL if n != "rel_table"}
    dtable_a = jnp.zeros((N_HEADS, NUM_BUCKETS), F32)
    dtable_b = jnp.zeros((N_HEADS, NUM_BUCKETS), F32)

    for l in reversed(range(DEPTH)):
        sv = saved[l]
        w = sv["w"]
        g_mix, g_ffn, g_ple = (small[n][l][None, :] for n in ("norm_mix_g", "norm_ffn_g", "norm_ple_g"))
        gqa, gka, gqb = (_tile_gain(small[n][l], N_HEADS) for n in ("qnorm_a_g", "knorm_a_g", "qnorm_b_g"))
        gkb = _tile_gain(small["knorm_b_g"][l], N_KV_B)
        sink = jnp.repeat(small["sink_b"][l], BLK_B)[:, None]

        def ple_bwd(dx_, z_, e_):
            s = _sigmoid(z_.astype(F32))
            return dx_ * s, dx_ * e_.astype(F32) * (s * (1.0 - s))

        de, dz = _ew(ple_bwd, [(dx, 0), (sv["z"], 0), (sv["e"], 0)], [BF16, BF16], width=D_MODEL, bw=D_MODEL,
                     name="ple_bwd")
        gbig[l]["w_ple_proj"] = _mm(p[l], de, "tn", F32, "mm_d_ple_proj")
        gbig[l]["w_ple_gate"] = _mm(sv["h3"], dz, "tn", F32, "mm_d_ple_gate")
        dh3 = _mm(dz, w["w_ple_gate"], "nt", F32, "mm_dh3")
        dx, dxb, gsmall["norm_ple_g"][l] = _rms_bwd(sv["x2"], g_ple, dh3, dx, "rms_ple_bwd")

        gbig[l]["w_ffn_down"] = _mm(sv["hid"], dxb, "tn", F32, "mm_d_ffn_down")

        def swiglu_bwd(products, extra):
            dh_, a_, u_ = products[0], extra[0].astype(F32), extra[1].astype(F32)
            s = _sigmoid(a_)
            return dh_ * u_ * (s * (1.0 + a_ * (1.0 - s))), dh_ * (a_ * s)

        da, du = _mm_fused([(dxb, w["w_ffn_down"], "nt")], [(sv["a"], 0), (sv["u"], 0)], swiglu_bwd, [BF16, BF16],
                           "mm_dhid_swiglu_bwd")
        gbig[l]["w_ffn_gate"] = _mm(da, sv["h2"], "tn", F32, "mm_d_ffn_gate")
        gbig[l]["w_ffn_up"] = _mm(du, sv["h2"], "tn", F32, "mm_d_ffn_up")
        dh2 = _mm(da, w["w_ffn_gate"], "nn", F32, "mm_dh2_gate")
        dh2 = _mm(du, w["w_ffn_up"], "nn", F32, "mm_dh2_up", res=dh2)
        dx, dxb, gsmall["norm_ffn_g"][l] = _rms_bwd(sv["x1"], g_ffn, dh2, dx, "rms_ffn_bwd")

        gbig[l]["w_out"] = _mm(sv["merged"], dxb, "tn", F32, "mm_d_out")

        def gate_bwd(products, extra):
            dm_, ca_, cb_ = products[0], extra[0].astype(F32), extra[1].astype(F32)
            sa, sb = _sigmoid(extra[2]), _sigmoid(extra[3])
            return dm_ * sa, dm_ * sb, dm_ * ca_ * (sa * (1.0 - sa)), dm_ * cb_ * (sb * (1.0 - sb))

        dca, dcb, dga, dgb = _mm_fused(
            [(dxb, w["w_out"], "nt")], [(sv["ca"], 0), (sv["cb"], 0), (sv["proj"], OFF_GA), (sv["proj"], OFF_GB)],
            gate_bwd, [BF16, BF16, BF16, BF16], "mm_dmerged_gate_bwd")
        gbig[l]["w_branch_a"] = _mm(sv["ya"], dca, "tn", F32, "mm_d_branch_a")
        gbig[l]["w_branch_b"] = _mm(sv["yb"], dcb, "tn", F32, "mm_d_branch_b")
        dya = _mm(dca, w["w_branch_a"], "nt", F32, "mm_dya")
        dyb = _mm(dcb, w["w_branch_b"], "nt", F32, "mm_dyb")

        dya, delta_a = _attn_bwd_prep(dya, sv["ya"], "attn_a_bwd_prep")
        dyb, delta_b = _attn_bwd_prep(dyb, sv["yb"], "attn_b_bwd_prep")

        dqa, dka, dva = [], [], []
        for (blk, d), bias, bk in zip(DILATED, bias_a, buckets_a):
            dq_, dk_, dv_, db_ = _attn_bwd(sv["qa"], sv["ka"], sv["va"], dya, sv["lse_a"], delta_a, bias, None, blk, d,
                                           f"attn_a{d}_bwd")
            dqa.append(dq_)
            dka.append(dk_)
            dva.append(dv_)
            dtable_a = dtable_a + _table_grad(db_.reshape(N_HEADS, blk, 3 * blk), bk, "table_grad_a")
        dqb, dkb, dvb, db_, dsink = _attn_bwd(sv["qb"], sv["kb"], sv["vb"], dyb, sv["lse_b"], delta_b, bias_b, sink,
                                              BLK_B, 1, "attn_b_bwd")
        dtable_b = dtable_b + _table_grad(db_.reshape(N_HEADS, BLK_B, 3 * BLK_B), buckets_b, "table_grad_b")
        gsmall["sink_b"][l] = dsink.reshape(N_HEADS, BLK_B).sum(axis=1)

        dproj, pqa, pka, pqb, pkb = _qknorm_bwd(sv["proj"], gqa, gka, gqb, gkb, dqa, dka, dva, dqb, dkb, dvb, dga, dgb)
        marks[l]["attn_bwd_done"] = dproj
        gsmall["qnorm_a_g"][l] = pqa.reshape(N_HEADS, HEAD_DIM).sum(0)
        gsmall["knorm_a_g"][l] = pka.reshape(N_HEADS, HEAD_DIM).sum(0)
        gsmall["qnorm_b_g"][l] = pqb.reshape(N_HEADS, HEAD_DIM).sum(0)
        gsmall["knorm_b_g"][l] = pkb.reshape(N_KV_B, HEAD_DIM).sum(0)
        gbig[l]["w_in"] = _mm(dproj, sv["h"], "tn", F32, "mm_d_in")
        dh = _mm(dproj, w["w_in"], "nn", F32, "mm_dh")
        dx, _, gsmall["norm_mix_g"][l] = _rms_bwd(sv["x0"], g_mix, dh, dx, "rms_mix_bwd")
        gsmall["norm_mix_g"][l] = gsmall["norm_mix_g"][l][0]
        gsmall["norm_ffn_g"][l] = gsmall["norm_ffn_g"][l][0]
        gsmall["norm_ple_g"][l] = gsmall["norm_ple_g"][l][0]

    gsmall = {n: jnp.stack(v) for n, v in gsmall.items()}
    gsmall["rel_table"] = jnp.concatenate([dtable_a, dtable_b], axis=0).T
    return loss, dx, gbig, gsmall, marks


def _place():
    return lax.axis_index("x"), lax.axis_index("y"), lax.axis_index("c")


def _flip(v, bit):
    return 1 - v if bit else v


CHIP_RELATIONS = ((0, 1), (1, 0), (1, 1))
ANY = pl.BlockSpec(memory_space=pl.ANY)


def _allgather_body(w_refs, out_refs, send_sems, recv_sems):
    x, y, c = _place()
    chips = [(_flip(x, a), _flip(y, b)) for a, b in CHIP_RELATIONS]

    def make(g):
        w_ref, out_ref = w_refs[g], out_refs[g]
        half = w_ref.shape[0] // 2

        def part(px, py, pc):
            return out_ref.at[2 * px + py, pl.ds(pc * half, half), :]

        def copy(k, block, to, src=None):
            return pltpu.make_async_remote_copy(
                src_ref=part(*block) if src is None else src, dst_ref=part(*block),
                send_sem=send_sems.at[7 * g + k], recv_sem=recv_sems.at[7 * g + k], device_id=to,
                device_id_type=MESH_ID)

        own = pltpu.make_async_remote_copy(
            src_ref=w_ref, dst_ref=out_ref.at[2 * x + y], send_sem=send_sems.at[7 * g + 6],
            recv_sem=recv_sems.at[7 * g + 6], device_id=(x, y, 1 - c), device_id_type=MESH_ID)
        first = [copy(k, (x, y, c), (*chip, c), src=w_ref.at[pl.ds(c * half, half), :]) for k, chip in enumerate(chips)]
        passed = [copy(3 + k, (*chip, c), (x, y, 1 - c)) for k, chip in enumerate(chips)]
        arrive = [copy(k, (*chip, c), (x, y, c)) for k, chip in enumerate(chips)]
        arrive2 = [copy(3 + k, (*chip, 1 - c), (x, y, c)) for k, chip in enumerate(chips)]
        return own, first, passed, arrive, arrive2

    made = [make(g) for g in range(len(w_refs))]
    for own, first, _, _, _ in made:
        own.start()
        for cp in first:
            cp.start()
    for _, _, passed, arrive, _ in made:
        for k in range(3):
            arrive[k].wait_recv()
            passed[k].start()
    for own, first, passed, _, arrive2 in made:
        for k in range(3):
            arrive2[k].wait_recv()
        own.wait_recv()
        for cp in first + passed + [own]:
            cp.wait_send()


def _sibling(x, y, c):
    return [(x, y, 1 - c)]


def _same_core_of_other_chips(x, y, c):
    return [(_flip(x, a), _flip(y, b), c) for a, b in CHIP_RELATIONS]


def _exchange(body, ins, out_types, n_sems, name, sequencer=None):
    n = len(ins)
    sems = (pltpu.SemaphoreType.DMA((n_sems,)), pltpu.SemaphoreType.DMA((n_sems,)))
    if sequencer is None:
        in_place = out_types is None
        out_shape = [jax.ShapeDtypeStruct(a.shape, a.dtype) for a in ins] if in_place else out_types

        def tc_body(*refs):
            body(refs[:n], refs[n:n + len(out_shape)], refs[-2], refs[-1])

        return list(pl.pallas_call(
            tc_body, out_shape=out_shape, in_specs=[ANY] * n, out_specs=[ANY] * len(out_shape),
            input_output_aliases={g: g for g in range(n)} if in_place else {}, scratch_shapes=list(sems), name=name)(*ins))

    collective_id, peers = sequencer
    hbm = pltpu.MemorySpace.HBM
    in_refs = [jax.new_ref(a, memory_space=hbm) for a in ins]
    out_refs = in_refs if out_types is None else [jax.empty_ref(t, memory_space=hbm) for t in out_types]

    @pl.kernel(mesh=plsc.ScalarSubcoreMesh(axis_name="sequencer", num_cores=1), name=name, scratch_types=sems,
               compiler_params=pltpu.CompilerParams(collective_id=collective_id))
    def launch(send_sems, recv_sems):
        barrier = pltpu.get_barrier_semaphore()
        devices = peers(*_place())
        for device in devices:
            pl.semaphore_signal(barrier, inc=1, device_id=device, device_id_type=MESH_ID)
        pl.semaphore_wait(barrier, len(devices))
        body(in_refs, out_refs, send_sems, recv_sems)

    launch()
    return [r[...] for r in out_refs]


def _allgather(shards, name, sequencer=None):
    out_types = [jax.ShapeDtypeStruct((N_CHIPS,) + s.shape, s.dtype) for s in shards]
    if sequencer is not None:
        sequencer = (sequencer, lambda x, y, c: _sibling(x, y, c) + _same_core_of_other_chips(x, y, c))
    return _exchange(_allgather_body, shards, out_types, 7 * len(shards), name, sequencer)


def _half_tile(half):
    return max(t for t in range(16, 1025, 16) if half % t == 0)


def _run_copies(cps):
    for cp in cps:
        cp.start()
    for cp in cps:
        cp.wait_recv()
    for cp in cps:
        cp.wait_send()


def _sibling_halves(gsends, name, sequencer=None):
    def body(g_refs, out_refs, send_sems, recv_sems):
        x, y, c = _place()
        cps = []
        for g, (g_ref, out_ref) in enumerate(zip(g_refs, out_refs)):
            half = g_ref.shape[1] // 2
            cps.append(pltpu.make_async_remote_copy(
                src_ref=g_ref.at[:, pl.ds((1 - c) * half, half), :], dst_ref=out_ref,
                send_sem=send_sems.at[g], recv_sem=recv_sems.at[g], device_id=(x, y, 1 - c), device_id_type=MESH_ID))
        _run_copies(cps)

    out_types = [jax.ShapeDtypeStruct((s.shape[0], s.shape[1] // 2, s.shape[2]), s.dtype) for s in gsends]
    return _exchange(body, gsends, out_types, len(gsends), name, sequencer and (sequencer, _sibling))


def _chip_sums(gsend, sib, place):
    n, rows, cols = gsend.shape
    half = rows // 2
    tm = _half_tile(half)
    nblk = half // tm

    def body(s_ref, g_ref, sib_ref, o_ref):
        o_ref[0] = (g_ref[0].astype(F32) + sib_ref[0].astype(F32)).astype(o_ref.dtype)

    grid_spec = pltpu.PrefetchScalarGridSpec(
        num_scalar_prefetch=1, grid=(n, nblk),
        in_specs=[pl.BlockSpec((1, tm, cols), lambda k, i, s: (jnp.bitwise_xor(s[0], k), s[1] * nblk + i, 0)),
                  pl.BlockSpec((1, tm, cols), lambda k, i, s: (jnp.bitwise_xor(s[0], k), i, 0))],
        out_specs=pl.BlockSpec((1, tm, cols), lambda k, i, s: (k, i, 0)))
    return pl.pallas_call(
        body, out_shape=jax.ShapeDtypeStruct((n, half, cols), BF16), grid_spec=grid_spec,
        name="rs_chip_sums", compiler_params=_params("parallel", "parallel"))(place, gsend, sib)


def _exchange_chip_sums(tsends, name, sequencer=None):
    def body(t_refs, out_refs, send_sems, recv_sems):
        x, y, c = _place()
        cps = []
        for g, (t_ref, out_ref) in enumerate(zip(t_refs, out_refs)):
            for k, device in enumerate(_same_core_of_other_chips(x, y, c)):
                cps.append(pltpu.make_async_remote_copy(
                    src_ref=t_ref.at[k + 1], dst_ref=out_ref.at[k], send_sem=send_sems.at[3 * g + k],
                    recv_sem=recv_sems.at[3 * g + k], device_id=device, device_id_type=MESH_ID))
        _run_copies(cps)

    out_types = [jax.ShapeDtypeStruct((3,) + s.shape[1:], s.dtype) for s in tsends]
    return _exchange(body, tsends, out_types, 3 * len(tsends), name,
                     sequencer and (sequencer, _same_core_of_other_chips))


def _final_sum(tsend, recv, place):
    n, half, cols = tsend.shape
    tm = _half_tile(half)
    nblk = half // tm

    def body(s_ref, t_ref, r_ref, o_ref):
        o_ref[...] = ((t_ref[0].astype(F32) + r_ref[0].astype(F32)) + r_ref[1].astype(F32)) + r_ref[2].astype(F32)

    grid_spec = pltpu.PrefetchScalarGridSpec(
        num_scalar_prefetch=1, grid=(nblk,),
        in_specs=[pl.BlockSpec((1, tm, cols), lambda i, s: (0, i, 0)), pl.BlockSpec((n - 1, tm, cols), lambda i, s: (0, i, 0))],
        out_specs=pl.BlockSpec((tm, cols), lambda i, s: (s[1] * nblk + i, 0)))
    return pl.pallas_call(
        body, out_shape=jax.ShapeDtypeStruct((2 * half, cols), F32), grid_spec=grid_spec, name="rs_final_sum",
        compiler_params=_params("parallel"))(place, tsend, recv)


def _join_halves(gfulls, name, sequencer=None):
    def body(g_refs, out_refs, send_sems, recv_sems):
        x, y, c = _place()
        n = len(g_refs)

        def copy(g, pc):
            half = g_refs[g].shape[0] // 2
            return pltpu.make_async_remote_copy(
                src_ref=g_refs[g].at[pl.ds(pc * half, half), :], dst_ref=out_refs[g].at[pl.ds(pc * half, half), :],
                send_sem=send_sems.at[g], recv_sem=recv_sems.at[g], device_id=(x, y, 1 - c), device_id_type=MESH_ID)

        mine = [copy(g, c) for g in range(n)]
        for cp in mine:
            cp.start()
        for g in range(n):
            copy(g, 1 - c).wait_recv()
        for cp in mine:
            cp.wait_send()

    return _exchange(body, gfulls, None, len(gfulls), name, sequencer and (sequencer, _sibling))


def _allreduce_small(v):
    rows, cols = v.shape

    def body(v_ref, out_ref, buf, send_sems, recv_sems):
        x, y, c = _place()
        cps = []
        for k in range(1, 8):
            peer = (_flip(x, (k >> 2) & 1), _flip(y, (k >> 1) & 1), _flip(c, k & 1))
            cps.append(pltpu.make_async_remote_copy(
                src_ref=v_ref, dst_ref=buf.at[k - 1], send_sem=send_sems.at[k - 1], recv_sem=recv_sems.at[k - 1],
                device_id=peer, device_id_type=MESH_ID))
        for cp in cps:
            cp.start()
        for cp in cps:
            cp.wait_recv()
        for cp in cps:
            cp.wait_send()
        t0 = v_ref[...] + buf[0]
        t1 = buf[1] + buf[2]
        t2 = buf[3] + buf[4]
        t3 = buf[5] + buf[6]
        out_ref[...] = (t0 + t1) + (t2 + t3)

    vm = pl.BlockSpec(memory_space=pltpu.VMEM)
    return pl.pallas_call(
        body, out_shape=jax.ShapeDtypeStruct((rows, cols), F32), in_specs=[vm], out_specs=vm,
        scratch_shapes=[pltpu.VMEM((7, rows, cols), F32), pltpu.SemaphoreType.DMA((7,)), pltpu.SemaphoreType.DMA((7,))],
        name="allreduce_small")(v)


BIG_INFO = {n: (shape, ax) for n, shape, ax in BIG}
GROUPS = (("w_in",), ("w_ffn_gate", "w_ffn_up", "w_ffn_down", "w_out", "w_ple_gate"),
          ("w_branch_a", "w_branch_b", "w_ple_proj"))


def _shard_shape(name):
    (k, m), ax = BIG_INFO[name]
    return (k // N_CHIPS, m) if ax == 0 else (k, m // N_CHIPS)


def _group_rows(group):
    offs, off = {}, 0
    for n in group:
        offs[n] = off
        off += _shard_shape(n)[0]
    return offs, off


def _pack_groups(shards, layer, dtype):
    return [jnp.concatenate([shards[n][layer].astype(dtype) for n in group], axis=0) for group in GROUPS]


def _unpack_full(gathered, groups):
    out = {}
    for group, arr in zip(groups, gathered):
        offs, _ = _group_rows(group)
        for n in group:
            rows, cols = _shard_shape(n)
            (k, m), ax = BIG_INFO[n]
            slab = arr[:, offs[n]:offs[n] + rows]
            out[n] = slab.reshape(k, m) if ax == 0 else jnp.transpose(slab, (1, 0, 2)).reshape(k, m)
    return out


def _pack_grads(gfull):
    out = []
    for group in GROUPS:
        parts = []
        for n in group:
            rows, cols = _shard_shape(n)
            ax = BIG_INFO[n][1]
            slab = (gfull[n].reshape(N_CHIPS, rows, cols) if ax == 0
                    else jnp.transpose(gfull[n].reshape(rows, N_CHIPS, cols), (1, 0, 2)))
            parts.append(slab.astype(BF16))
        out.append(jnp.concatenate(parts, axis=1))
    return out


def _after(values, mark):
    values, _ = lax.optimization_barrier((values, mark))
    return values


def _reduce_scatter_begin(gsends, place, tag, ids):
    sibs = _sibling_halves(gsends, "rs_sibling_halves_" + tag, ids[0])
    tsends = [_chip_sums(g, s, place) for g, s in zip(gsends, sibs)]
    return tsends, _exchange_chip_sums(tsends, "rs_exchange_" + tag, ids[1])


def _reduce_scatter_finish(begun, place, tag, ids, hold):
    tsends, recvs = begun
    recvs = _after(recvs, hold)
    return _join_halves([_final_sum(t, r, place) for t, r in zip(tsends, recvs)], "rs_join_halves_" + tag, ids[2])


SMALL_SHAPES = {"rel_table": (NUM_BUCKETS, 2 * N_HEADS), "norm_mix_g": (DEPTH, D_MODEL), "qnorm_a_g": (DEPTH, HEAD_DIM),
                "knorm_a_g": (DEPTH, HEAD_DIM), "qnorm_b_g": (DEPTH, HEAD_DIM), "knorm_b_g": (DEPTH, HEAD_DIM),
                "sink_b": (DEPTH, N_HEADS), "norm_ffn_g": (DEPTH, D_MODEL), "norm_ple_g": (DEPTH, D_MODEL)}


def _pack_small(vals):
    flat = jnp.concatenate([vals[n].astype(F32).reshape(-1) for n in SMALL])
    flat = jnp.concatenate([flat, jnp.zeros((SMALL_ROWS * LANES - flat.shape[0],), F32)])
    return flat.reshape(SMALL_ROWS, LANES)


def _unpack_small(packed):
    flat, out, off = packed.reshape(-1), {}, 0
    for n in SMALL:
        size = math.prod(SMALL_SHAPES[n])
        out[n] = flat[off:off + size].reshape(SMALL_SHAPES[n])
        off += size
    return out


def _adamw(w, gs, g_row, m, v, name):
    c1 = 1.0 - ADAM_B1 ** ADAM_STEP
    c2 = 1.0 - ADAM_B2 ** ADAM_STEP
    total, width = w.shape
    n_layers = len(gs)
    per = total // n_layers
    tm = max(t for t in range(8, 513, 8) if per % t == 0 and g_row % t == 0)
    nblk = per // tm

    def body(*refs):
        w_ref, g_refs = refs[0], refs[1:1 + n_layers]
        m_ref, v_ref, og, od, om, ov = refs[1 + n_layers:]
        layer = pl.program_id(0) // nblk
        g = g_refs[0][...]
        for l in range(1, n_layers):
            g = jnp.where(layer == l, g_refs[l][...], g)
        m_new = ADAM_B1 * m_ref[...] + (1.0 - ADAM_B1) * g
        v_new = ADAM_B2 * v_ref[...] + (1.0 - ADAM_B2) * (g * g)
        og[...] = g
        od[...] = -ADAM_LR * ((m_new / c1) / (jnp.sqrt(v_new / c2) + ADAM_EPS) + ADAM_WD * w_ref[...])
        om[...] = m_new
        ov[...] = v_new

    row = pl.BlockSpec((tm, width), lambda i: (i, 0))
    g_specs = [pl.BlockSpec((tm, width), lambda i, l=l: (g_row // tm + jnp.clip(i - l * nblk, 0, nblk - 1), 0))
               for l in range(n_layers)]
    return pl.pallas_call(
        body, out_shape=[jax.ShapeDtypeStruct((total, width), F32)] * 4, grid=(total // tm,),
        in_specs=[row] + g_specs + [row, row], out_specs=[row] * 4, name=name,
        compiler_params=_params("parallel"))(w, *gs, m, v)


def kernel(x, p, rel_table, norm_mix_g, w_in, qnorm_a_g, knorm_a_g, qnorm_b_g, knorm_b_g, sink_b, w_branch_a, w_branch_b, w_out, norm_ffn_g, w_ffn_gate, w_ffn_up, w_ffn_down, norm_ple_g, w_ple_gate, w_ple_proj, loss_target, m_rel_table, m_norm_mix_g, m_w_in, m_qnorm_a_g, m_knorm_a_g, m_qnorm_b_g, m_knorm_b_g, m_sink_b, m_w_branch_a, m_w_branch_b, m_w_out, m_norm_ffn_g, m_w_ffn_gate, m_w_ffn_up, m_w_ffn_down, m_norm_ple_g, m_w_ple_gate, m_w_ple_proj, v_rel_table, v_norm_mix_g, v_w_in, v_qnorm_a_g, v_knorm_a_g, v_qnorm_b_g, v_knorm_b_g, v_sink_b, v_w_branch_a, v_w_branch_b, v_w_out, v_norm_ffn_g, v_w_ffn_gate, v_w_ffn_up, v_w_ffn_down, v_norm_ple_g, v_w_ple_gate, v_w_ple_proj):
    given = dict(locals())

    def held(name, a):
        return jnp.swapaxes(a, 1, 2) if name in TRANSPOSED else a

    weights = {n: held(n, given[n]) for n in WEIGHTS}
    moments_m = {n: held(n, given["m_" + n]) for n in WEIGHTS}
    moments_v = {n: held(n, given["v_" + n]) for n in WEIGHTS}
    xi, yi, ci = _place()
    place = jnp.stack([2 * xi + yi, ci]).astype(jnp.int32)

    shards = [_pack_groups(weights, l, BF16) for l in range(DEPTH)]
    w_in0 = _allgather(shards[0][:1], "allgather_w_in_layer0", sequencer=9)
    rest0 = _allgather(_after(shards[0][1:], w_in0), "allgather_rest_layer0", sequencer=1)
    gathered = [w_in0 + rest0, None]
    small = {n: weights[n] for n in SMALL}

    def w_in_of(l, mark):
        return _unpack_full(gathered[l][:1] if l == 0 else _after(gathered[l][:1], mark), GROUPS[:1])["w_in"]

    def rest_of(l, mark):
        if l == 0:
            gathered[1] = _allgather(_after(shards[1], mark), "allgather_layer1", sequencer=2)
        return _unpack_full(_after(gathered[l][1:], mark), GROUPS[1:])

    loss, dx, gbig, gsmall, marks = _local_step(x[0], p[:, 0], loss_target[0], w_in_of, rest_of, small)

    gsends = [_pack_grads(gbig[l]) for l in range(DEPTH)]
    stages = {"layer1": (gsends[1], (3, 4, 5)), "rest_layer0": (gsends[0][1:], (6, 7, 8)),
              "w_in_layer0": (gsends[0][:1], (10, 11, 12))}
    begun = {tag: _reduce_scatter_begin(g, place, tag, ids) for tag, (g, ids) in stages.items()}

    def finish(tag, hold):
        return _reduce_scatter_finish(begun[tag], place, tag, stages[tag][1], hold)

    red1 = finish("layer1", marks[0]["attn_bwd_done"])
    rest0 = finish("rest_layer0", marks[0]["attn_bwd_done"])

    grads, delta, new_m, new_v = {}, {}, {}, {}

    def update(group, reduced):
        offs, _ = _group_rows(group)
        for n in group:
            shape = weights[n].shape
            two_d = lambda a: a.reshape(shape[0] * shape[1], shape[2])
            outs = _adamw(two_d(weights[n]), reduced, offs[n], two_d(moments_m[n]), two_d(moments_v[n]), "adamw_" + n)
            grads[n], delta[n], new_m[n], new_v[n] = (held(n, o.reshape(shape)) for o in outs)

    for gi in (1, 2):
        update(GROUPS[gi], _after([rest0[gi - 1], red1[gi]], begun["w_in_layer0"][0]))
    others_done = [dx] + [delta[n] for gi in (1, 2) for n in GROUPS[gi]]
    update(GROUPS[0], [finish("w_in_layer0", others_done)[0], red1[0]])
    small_grads = _allreduce_small(_pack_small(gsmall))
    g_, d_, m_, v_ = _adamw(_pack_small(weights), [small_grads], 0, _pack_small(moments_m), _pack_small(moments_v),
                            "adamw_small")
    grads.update(_unpack_small(g_))
    delta.update(_unpack_small(d_))
    new_m.update(_unpack_small(m_))
    new_v.update(_unpack_small(v_))

    loss = lax.psum(loss, ("x", "y", "c"))
    return (loss, dx[None], *[grads[n] for n in WEIGHTS], *[delta[n] for n in WEIGHTS],
            *[new_m[n] for n in WEIGHTS], *[new_v[n] for n in WEIGHTS])
```

```python
import functools
import math

import jax
import jax.numpy as jnp
from jax import lax
from jax.experimental import pallas as pl
from jax.experimental.pallas import tpu as pltpu
from jax.experimental.pallas import tpu_sc as plsc

F32 = jnp.float32
BF16 = jnp.bfloat16
MESH_ID = pl.DeviceIdType.MESH

SEQ = 2048
D_MODEL = 1024
DEPTH = 2
HEAD_DIM = 64
N_HEADS = 8
WIDTH = N_HEADS * HEAD_DIM
N_PAIRS = 4
ITEMS = 4
N_KV_B = 2
PLE_DIM = 256
D_FF = 2816
D_IN = 4352
OFF_QA, OFF_KA, OFF_VA, OFF_QB, OFF_KB, OFF_VB, OFF_GA, OFF_GB = 0, 512, 1024, 1536, 2048, 2176, 2304, 3328
DILATED = ((64, 1), (64, 4), (64, 16))
BLK_B = 128
NUM_BUCKETS = 32
MAX_DISTANCE = 1024
RMS_EPS = 1e-6
NEG_INF = -1e30
LANES = 128
VMEM_LIMIT = 48 * 1024 * 1024

ADAM_LR, ADAM_B1, ADAM_B2, ADAM_EPS, ADAM_WD, ADAM_STEP = 0.001, 0.9, 0.999, 1e-08, 0.01, 10

TRANSPOSED = ("w_in", "w_ffn_gate", "w_ffn_up")
BIG = (
    ("w_in", (D_IN, D_MODEL), 0),
    ("w_branch_a", (WIDTH, D_MODEL), 1),
    ("w_branch_b", (WIDTH, D_MODEL), 1),
    ("w_out", (D_MODEL, D_MODEL), 0),
    ("w_ffn_gate", (D_FF, D_MODEL), 0),
    ("w_ffn_up", (D_FF, D_MODEL), 0),
    ("w_ffn_down", (D_FF, D_MODEL), 0),
    ("w_ple_gate", (D_MODEL, D_MODEL), 0),
    ("w_ple_proj", (PLE_DIM, D_MODEL), 1),
)
SMALL = ("rel_table", "norm_mix_g", "qnorm_a_g", "knorm_a_g", "qnorm_b_g", "knorm_b_g", "sink_b",
         "norm_ffn_g", "norm_ple_g")
WEIGHTS = ("rel_table", "norm_mix_g", "w_in", "qnorm_a_g", "knorm_a_g", "qnorm_b_g", "knorm_b_g", "sink_b",
           "w_branch_a", "w_branch_b", "w_out", "norm_ffn_g", "w_ffn_gate", "w_ffn_up", "w_ffn_down",
           "norm_ple_g", "w_ple_gate", "w_ple_proj")
N_CHIPS = 4
SMALL_ROWS = 64


def _params(*sem):
    return pltpu.CompilerParams(dimension_semantics=sem, vmem_limit_bytes=VMEM_LIMIT)


def _pick(dim, target):
    for t in (target, 512, 256, 128, 64, 32, 16, 8):
        if t <= target and dim % t == 0:
            return t
    return dim


MM_VMEM_BUDGET = 40 * 1024 * 1024
STEP_OVERHEAD_S = 0.4e-6
TILE_DMA_BYTES_PER_S = 1.5e12


def _mm_dims(a, b, mode):
    if mode == "nn":
        return a.shape[0], b.shape[1], a.shape[1]
    if mode == "nt":
        return a.shape[0], b.shape[0], a.shape[1]
    return a.shape[1], b.shape[1], a.shape[0]


def _mm_tiles(m, n, pairs, tile_bytes, col_offsets):
    best = None
    for tm in (t for t in range(LANES, m + 1, LANES) if m % t == 0):
        for tn in (t for t in range(LANES, n + 1, LANES) if n % t == 0 and all(o % t == 0 for o in col_offsets)):
            io = sum(tm * k * ab + tn * k * bb for k, ab, bb in pairs) + tm * tn * sum(tile_bytes)
            casts = sum((tm * k * 2 if ab == 4 else 0) + (tn * k * 2 if bb == 4 else 0) for k, ab, bb in pairs)
            if 2 * io + len(pairs) * tm * tn * 4 + casts > MM_VMEM_BUDGET:
                continue
            cost = (m // tm) * (n // tn) * STEP_OVERHEAD_S + io / TILE_DMA_BYTES_PER_S
            if best is None or (cost, -tm) < best[0]:
                best = ((cost, -tm), tm, tn)
    return best[1], best[2]


def _mm_fused(pairs, extras, epilogue, out_dtypes, name):
    m, n, _ = _mm_dims(*pairs[0])
    assert all(_mm_dims(*p)[:2] == (m, n) for p in pairs)
    tm, tn = _mm_tiles(
        m, n, [(_mm_dims(a, b, mode)[2], a.dtype.itemsize, b.dtype.itemsize) for a, b, mode in pairs],
        [e.dtype.itemsize for e, _ in extras] + [jnp.dtype(d).itemsize for d in out_dtypes], [off for _, off in extras])
    dims = {"nn": (((1,), (0,)), ((), ())), "nt": (((1,), (1,)), ((), ())), "tn": (((0,), (0,)), ((), ()))}
    in_specs, args = [], []
    for a, b, mode in pairs:
        k = _mm_dims(a, b, mode)[2]
        in_specs.append(pl.BlockSpec((k, tm), lambda i, j: (0, i)) if mode == "tn" else pl.BlockSpec((tm, k), lambda i, j: (i, 0)))
        in_specs.append(pl.BlockSpec((tn, k), lambda i, j: (j, 0)) if mode == "nt" else pl.BlockSpec((k, tn), lambda i, j: (0, j)))
        args += [a, b]
    for e, off in extras:
        in_specs.append(pl.BlockSpec((tm, tn), lambda i, j, o=off // tn: (i, o + j)))
        args.append(e)
    n_pairs, n_in = len(pairs), 2 * len(pairs) + len(extras)

    def body(*refs):
        products = [lax.dot_general(refs[2 * p][...].astype(BF16), refs[2 * p + 1][...].astype(BF16), dims[pairs[p][2]],
                                    preferred_element_type=F32) for p in range(n_pairs)]
        outs = epilogue(products, [r[...] for r in refs[2 * n_pairs:n_in]])
        for r, o in zip(refs[n_in:], outs):
            r[...] = o.astype(r.dtype)

    tile = pl.BlockSpec((tm, tn), lambda i, j: (i, j))
    return pl.pallas_call(
        body, out_shape=[jax.ShapeDtypeStruct((m, n), d) for d in out_dtypes], grid=(m // tm, n // tn),
        in_specs=in_specs, out_specs=[tile] * len(out_dtypes), name=name,
        compiler_params=_params("parallel", "parallel"))(*args)


def _mm(a, b, mode, out_dtype, name, res=None):
    if res is None:
        return _mm_fused([(a, b, mode)], [], lambda products, extra: products, [out_dtype], name)[0]
    return _mm_fused([(a, b, mode)], [(res, 0)], lambda products, extra: [products[0] + extra[0]], [out_dtype], name)[0]


def _ew(fn, ins, out_dtypes, *, width, bw, name, vecs=(), tm=256):
    rows = ins[0][0].shape[0]
    tm = _pick(rows, tm)
    n_in = len(ins) + len(vecs)

    def col_map(off_blocks):
        return lambda i, j: (i, off_blocks + j)

    in_specs = [pl.BlockSpec((tm, bw), col_map(off // bw)) for _, off in ins]
    in_specs += [pl.BlockSpec((1, bw), lambda i, j: (0, j)) for _ in vecs]

    def body(*refs):
        outs = fn(*[r[...] for r in refs[:n_in]])
        for r, o in zip(refs[n_in:], outs):
            r[...] = o.astype(r.dtype)

    return pl.pallas_call(
        body, out_shape=[jax.ShapeDtypeStruct((rows, width), dt) for dt in out_dtypes],
        grid=(rows // tm, width // bw), in_specs=in_specs,
        out_specs=[pl.BlockSpec((tm, bw), lambda i, j: (i, j)) for _ in out_dtypes],
        name=name, compiler_params=_params("parallel", "parallel"))(*[a for a, _ in ins], *vecs)


def _sigmoid(x):
    return 1.0 / (1.0 + jnp.exp(-x))


def _seg_sum(v):
    outs = []
    for k in range(v.shape[1] // LANES):
        vp = v[:, k * LANES:(k + 1) * LANES]
        left = lax.broadcasted_iota(jnp.int32, vp.shape, 1) < HEAD_DIM
        sl = jnp.sum(jnp.where(left, vp, 0.0), axis=-1, keepdims=True)
        sr = jnp.sum(jnp.where(left, 0.0, vp), axis=-1, keepdims=True)
        outs.append(jnp.where(left, sl, sr))
    return outs[0] if len(outs) == 1 else jnp.concatenate(outs, axis=1)


def _seg_rstd(x):
    return lax.rsqrt(_seg_sum(x * x) * (1.0 / HEAD_DIM) + RMS_EPS)


def _rms_fwd(x, g, name):
    rows, d = x.shape
    tm = 256

    def body(x_ref, g_ref, h_ref):
        xv = x_ref[...]
        r = lax.rsqrt(jnp.mean(xv * xv, axis=-1, keepdims=True) + RMS_EPS)
        h_ref[...] = ((xv * r) * g_ref[...]).astype(BF16)

    return pl.pallas_call(
        body, out_shape=jax.ShapeDtypeStruct((rows, d), BF16), grid=(rows // tm,),
        in_specs=[pl.BlockSpec((tm, d), lambda i: (i, 0)), pl.BlockSpec((1, d), lambda i: (0, 0))],
        out_specs=pl.BlockSpec((tm, d), lambda i: (i, 0)), name=name,
        compiler_params=_params("parallel"))(x, g)


def _rms_bwd(x, g, dh, dres, name):
    rows, d = x.shape
    tm = 256

    def body(x_ref, g_ref, dh_ref, dres_ref, dx_ref, dxb_ref, dg_ref):
        xv = x_ref[...]
        r = lax.rsqrt(jnp.mean(xv * xv, axis=-1, keepdims=True) + RMS_EPS)
        xh = xv * r
        dhv = dh_ref[...]
        dxh = dhv * g_ref[...]
        dxv = dres_ref[...] + r * (dxh - xh * jnp.mean(dxh * xh, axis=-1, keepdims=True))
        dx_ref[...] = dxv
        dxb_ref[...] = dxv.astype(BF16)
        part = jnp.sum(dhv * xh, axis=0, keepdims=True)

        @pl.when(pl.program_id(0) == 0)
        def _():
            dg_ref[...] = part

        @pl.when(pl.program_id(0) > 0)
        def _():
            dg_ref[...] += part

    row = pl.BlockSpec((tm, d), lambda i: (i, 0))
    vec = pl.BlockSpec((1, d), lambda i: (0, 0))
    return pl.pallas_call(
        body, out_shape=[jax.ShapeDtypeStruct((rows, d), F32), jax.ShapeDtypeStruct((rows, d), BF16),
                         jax.ShapeDtypeStruct((1, d), F32)],
        grid=(rows // tm,), in_specs=[row, vec, row, row], out_specs=[row, row, vec],
        name=name, compiler_params=_params("arbitrary"))(x, g, dh, dres)


def _loss_grad(y, t):
    rows, d = y.shape
    tm = 256

    def body(y_ref, t_ref, dy_ref, l_ref):
        e = y_ref[...] - t_ref[...]
        dy_ref[...] = e * (1.0 / d)
        part = jnp.zeros((1, LANES), F32) + jnp.sum(e * e) * (0.5 / d)

        @pl.when(pl.program_id(0) == 0)
        def _():
            l_ref[...] = part

        @pl.when(pl.program_id(0) > 0)
        def _():
            l_ref[...] += part

    row = pl.BlockSpec((tm, d), lambda i: (i, 0))
    return pl.pallas_call(
        body, out_shape=[jax.ShapeDtypeStruct((rows, d), F32), jax.ShapeDtypeStruct((1, LANES), F32)],
        grid=(rows // tm,), in_specs=[row, row], out_specs=[row, pl.BlockSpec((1, LANES), lambda i: (0, 0))],
        name="loss_grad", compiler_params=_params("arbitrary"))(y, t)


def _put_pairs(ref, val):
    for hp in range(N_PAIRS):
        ref[hp] = val[:, hp * LANES:(hp + 1) * LANES].astype(ref.dtype)


def _get_pairs(ref):
    return jnp.concatenate([ref[hp] for hp in range(N_PAIRS)], axis=1)


def _swap_halves(v):
    return pltpu.roll(v, HEAD_DIM, axis=1)


def _expand_kv(kv):
    left = lax.broadcasted_iota(jnp.int32, kv.shape, 1) < HEAD_DIM
    sw = _swap_halves(kv)
    h0 = jnp.where(left, kv, sw)
    h1 = jnp.where(left, sw, kv)
    return jnp.concatenate([h0, h0, h1, h1], axis=1)


def _reduce_kv(dkv):
    left = lax.broadcasted_iota(jnp.int32, (dkv.shape[0], LANES), 1) < HEAD_DIM
    t = dkv[:, 0:LANES] + dkv[:, LANES:2 * LANES]
    u = dkv[:, 2 * LANES:3 * LANES] + dkv[:, 3 * LANES:4 * LANES]
    t = t + _swap_halves(t)
    u = u + _swap_halves(u)
    return jnp.where(left, t, u)


def _qknorm_fwd(proj, gqa, gka, gqb, gkb):
    rows = proj.shape[0]
    tm = 256

    def body(qa_ref, ka_ref, va_ref, qb_ref, kb_ref, vb_ref, gqa_ref, gka_ref, gqb_ref, gkb_ref,
             oqa, oka, ova, oqb, okb, ovb):
        for src, g_ref, dst in ((qa_ref, gqa_ref, oqa), (ka_ref, gka_ref, oka), (qb_ref, gqb_ref, oqb)):
            xv = src[...]
            _put_pairs(dst, (xv * _seg_rstd(xv)) * g_ref[...])
        _put_pairs(ova, va_ref[...])
        kv = kb_ref[...]
        _put_pairs(okb, _expand_kv((kv * _seg_rstd(kv)) * gkb_ref[...]))
        _put_pairs(ovb, _expand_kv(vb_ref[...]))

    def win(width, off):
        return pl.BlockSpec((tm, width), lambda i: (i, off // width))

    vec = lambda w: pl.BlockSpec((1, w), lambda i: (0, 0))
    out = pl.BlockSpec((N_PAIRS, tm, LANES), lambda i: (0, i, 0))
    return pl.pallas_call(
        body, out_shape=[jax.ShapeDtypeStruct((N_PAIRS, rows, LANES), F32)] * 6, grid=(rows // tm,),
        in_specs=[win(WIDTH, OFF_QA), win(WIDTH, OFF_KA), win(WIDTH, OFF_VA), win(WIDTH, OFF_QB),
                  win(LANES, OFF_KB), win(LANES, OFF_VB), vec(WIDTH), vec(WIDTH), vec(WIDTH), vec(LANES)],
        out_specs=[out] * 6, name="qknorm_fwd", compiler_params=_params("parallel"))(
            proj, proj, proj, proj, proj, proj, gqa, gka, gqb, gkb)


def _norm_bwd(xv, g, dy):
    r = _seg_rstd(xv)
    xh = xv * r
    dxh = dy * g
    dx = r * (dxh - xh * (_seg_sum(dxh * xh) * (1.0 / HEAD_DIM)))
    return dx, jnp.sum(dy * xh, axis=0, keepdims=True)


def _qknorm_bwd(proj, gqa, gka, gqb, gkb, dqa, dka, dva, dqb, dkb, dvb, dga, dgb):
    rows = proj.shape[0]
    tm = 256
    n_a = len(dqa)

    def body(*refs):
        qa_ref, ka_ref, qb_ref, kb_ref, gqa_ref, gka_ref, gqb_ref, gkb_ref = refs[:8]
        pos = 8
        dqa_refs, dka_refs, dva_refs = refs[pos:pos + n_a], refs[pos + n_a:pos + 2 * n_a], refs[pos + 2 * n_a:pos + 3 * n_a]
        pos += 3 * n_a
        dqb_ref, dkb_ref, dvb_ref, dga_ref, dgb_ref = refs[pos:pos + 5]
        dproj_ref, ogqa, ogka, ogqb, ogkb = refs[pos + 5:]

        def total(rs):
            acc = _get_pairs(rs[0])
            for r in rs[1:]:
                acc = acc + _get_pairs(r)
            return acc

        dx_qa, p_qa = _norm_bwd(qa_ref[...], gqa_ref[...], total(dqa_refs))
        dx_ka, p_ka = _norm_bwd(ka_ref[...], gka_ref[...], total(dka_refs))
        dx_qb, p_qb = _norm_bwd(qb_ref[...], gqb_ref[...], _get_pairs(dqb_ref))
        dx_kb, p_kb = _norm_bwd(kb_ref[...], gkb_ref[...], _reduce_kv(_get_pairs(dkb_ref)))
        dproj_ref[:, OFF_QA:OFF_QA + WIDTH] = dx_qa.astype(BF16)
        dproj_ref[:, OFF_KA:OFF_KA + WIDTH] = dx_ka.astype(BF16)
        dproj_ref[:, OFF_VA:OFF_VA + WIDTH] = total(dva_refs).astype(BF16)
        dproj_ref[:, OFF_QB:OFF_QB + WIDTH] = dx_qb.astype(BF16)
        dproj_ref[:, OFF_KB:OFF_KB + LANES] = dx_kb.astype(BF16)
        dproj_ref[:, OFF_VB:OFF_VB + LANES] = _reduce_kv(_get_pairs(dvb_ref)).astype(BF16)
        dproj_ref[:, OFF_GA:OFF_GB] = dga_ref[...]
        dproj_ref[:, OFF_GB:D_IN] = dgb_ref[...]
        first = pl.program_id(0) == 0
        for o_ref, part in ((ogqa, p_qa), (ogka, p_ka), (ogqb, p_qb), (ogkb, p_kb)):
            @pl.when(first)
            def _(o_ref=o_ref, part=part):
                o_ref[...] = part

            @pl.when(jnp.logical_not(first))
            def _(o_ref=o_ref, part=part):
                o_ref[...] += part

    def win(width, off):
        return pl.BlockSpec((tm, width), lambda i: (i, off // width))

    vec = lambda w: pl.BlockSpec((1, w), lambda i: (0, 0))
    row = lambda w: pl.BlockSpec((tm, w), lambda i: (i, 0))
    in_specs = [win(WIDTH, OFF_QA), win(WIDTH, OFF_KA), win(WIDTH, OFF_QB), win(LANES, OFF_KB),
                vec(WIDTH), vec(WIDTH), vec(WIDTH), vec(LANES)]
    in_specs += [pl.BlockSpec((N_PAIRS, tm, LANES), lambda i: (0, i, 0))] * (3 * n_a + 3) + [row(D_MODEL)] * 2
    return pl.pallas_call(
        body,
        out_shape=[jax.ShapeDtypeStruct((rows, D_IN), BF16), jax.ShapeDtypeStruct((1, WIDTH), F32),
                   jax.ShapeDtypeStruct((1, WIDTH), F32), jax.ShapeDtypeStruct((1, WIDTH), F32),
                   jax.ShapeDtypeStruct((1, LANES), F32)],
        grid=(rows // tm,), in_specs=in_specs,
        out_specs=[row(D_IN), vec(WIDTH), vec(WIDTH), vec(WIDTH), vec(LANES)],
        name="qknorm_bwd", compiler_params=_params("arbitrary"))(
            proj, proj, proj, proj, gqa, gka, gqb, gkb, *dqa, *dka, *dva, dqb, dkb, dvb, dga, dgb)


def _t5_bucket(rel):
    half_b = NUM_BUCKETS // 2
    max_exact = half_b // 2
    sign = jnp.where(rel > 0, half_b, 0)
    n = jnp.abs(rel)
    nf = jnp.maximum(n, 1).astype(F32)
    large = max_exact + (jnp.log(nf / max_exact) / math.log(MAX_DISTANCE / max_exact)
                         * (half_b - max_exact)).astype(jnp.int32)
    large = jnp.minimum(large, half_b - 1)
    return sign + jnp.where(n < max_exact, n, large)


def _band_buckets(blk, dilation):
    i = jnp.arange(blk, dtype=jnp.int32)[:, None]
    j = jnp.arange(3 * blk, dtype=jnp.int32)[None, :]
    rel = j - blk - i
    return jnp.where(jnp.abs(rel) <= blk, _t5_bucket(rel * dilation), -1)


def _bias_tiles(table, buckets, head_off, name):
    blk = buckets.shape[0]

    def body(tab_ref, bk_ref, o_ref):
        h = pl.program_id(0) + head_off
        bk = bk_ref[...]
        acc = jnp.full(bk.shape, NEG_INF, F32)
        for b in range(NUM_BUCKETS):
            acc = jnp.where(bk == b, tab_ref[b, h], acc)
        o_ref[0] = acc

    return pl.pallas_call(
        body, out_shape=jax.ShapeDtypeStruct((N_HEADS, blk, 3 * blk), F32), grid=(N_HEADS,),
        in_specs=[pl.BlockSpec(memory_space=pltpu.SMEM), pl.BlockSpec((blk, 3 * blk), lambda h: (0, 0))],
        out_specs=pl.BlockSpec((1, blk, 3 * blk), lambda h: (h, 0, 0)),
        name=name, compiler_params=_params("parallel"))(table, buckets)


def _table_grad(dbias, buckets, name):
    blk = buckets.shape[0]

    def body(db_ref, bk_ref, o_ref):
        bk = bk_ref[...]
        dbv = db_ref[0]
        lane = lax.broadcasted_iota(jnp.int32, (1, LANES), 1)
        acc = jnp.zeros((1, LANES), F32)
        for b in range(NUM_BUCKETS):
            acc = jnp.where(lane == b, jnp.sum(jnp.where(bk == b, dbv, 0.0)), acc)
        o_ref[0] = acc

    out = pl.pallas_call(
        body, out_shape=jax.ShapeDtypeStruct((N_HEADS, 1, LANES), F32), grid=(N_HEADS,),
        in_specs=[pl.BlockSpec((1, blk, 3 * blk), lambda h: (h, 0, 0)), pl.BlockSpec((blk, 3 * blk), lambda h: (0, 0))],
        out_specs=pl.BlockSpec((1, 1, LANES), lambda h: (h, 0, 0)),
        name=name, compiler_params=_params("parallel"))(dbias, buckets)
    return out[:, 0, :NUM_BUCKETS]


def _dot_nt(a, b):
    return lax.dot_general(a, b, (((1,), (1,)), ((), ())), preferred_element_type=F32)


def _dot_tn(a, b):
    return lax.dot_general(a, b, (((0,), (0,)), ((), ())), preferred_element_type=F32)


def _stack_pair(x2, left):
    return jnp.concatenate([jnp.where(left, x2, 0.0), jnp.where(left, 0.0, x2)], axis=0).astype(BF16)


def _attn_geometry(blk, d):
    chunk = blk * ITEMS if d == 1 else blk * d
    groups = 1 if d == 1 else d // ITEMS
    halo = blk if d == 1 else chunk
    return chunk, groups, halo


def _item_rows(ref, j, r0, blk, d):
    if d == 1:
        return ref[j * blk:(j + 1) * blk, :]
    return ref[pl.ds(r0 + j, blk, stride=d), :]


def _item_penalty(t, nct, j, blk, d):
    first_ok, last_ok = t > 0, t < nct - 1
    if d == 1:
        first_ok = True if j > 0 else first_ok
        last_ok = True if j < ITEMS - 1 else last_ok
    col = lax.broadcasted_iota(jnp.int32, (1, 3 * blk), 1)
    ok = jnp.logical_and(jnp.logical_or(col >= blk, first_ok), jnp.logical_or(col < 2 * blk, last_ok))
    return jnp.where(ok, 0.0, NEG_INF).astype(F32)


def _attn_specs(seq, blk, d, step_of):
    chunk, _, halo = _attn_geometry(blk, d)
    per, last = chunk // halo, seq // halo - 1
    cur = pl.BlockSpec((None, chunk, LANES), lambda hp, t: (hp, step_of(t), 0))
    prev = pl.BlockSpec((None, halo, LANES), lambda hp, t: (hp, jnp.clip(step_of(t) * per - 1, 0, last), 0))
    nxt = pl.BlockSpec((None, halo, LANES), lambda hp, t: (hp, jnp.minimum((step_of(t) + 1) * per, last), 0))
    return cur, prev, nxt


def _attn_fwd(q, k, v, bias, sink, blk, d, name):
    _, seq, _ = q.shape
    chunk, groups, _ = _attn_geometry(blk, d)
    nct = seq // chunk
    has_sink = sink is not None
    scale = HEAD_DIM ** -0.5

    def body(*refs):
        q_ref, kp, kc, kn, vp, vc, vn, b_ref = refs[:8]
        s_ref = refs[8] if has_sink else None
        o_ref, l_ref = refs[-2], refs[-1]
        t = pl.program_id(1)
        left = lax.broadcasted_iota(jnp.int32, (1, LANES), 1) < HEAD_DIM
        bias2 = b_ref[...]
        if d == 1:
            kwin = jnp.concatenate([kp[...], kc[...], kn[...]], axis=0).astype(BF16)
            vwin = jnp.concatenate([vp[...], vc[...], vn[...]], axis=0).astype(BF16)

        def group(r0):
            scores, vcats = [], []
            for j in range(ITEMS):
                qs = _stack_pair(_item_rows(q_ref, j, r0, blk, d) * scale, left)
                if d == 1:
                    kcat, vcat = kwin[j * blk:(j + 3) * blk], vwin[j * blk:(j + 3) * blk]
                else:
                    kcat = jnp.concatenate([_item_rows(r, j, r0, blk, d) for r in (kp, kc, kn)], axis=0).astype(BF16)
                    vcat = jnp.concatenate([_item_rows(r, j, r0, blk, d) for r in (vp, vc, vn)], axis=0).astype(BF16)
                scores.append(_dot_nt(qs, kcat) + bias2 + _item_penalty(t, nct, j, blk, d))
                vcats.append(vcat)
            ms = [jnp.max(s, axis=-1, keepdims=True) for s in scores]
            if has_sink:
                sk = s_ref[...]
                ms = [jnp.maximum(m, sk) for m in ms]
            ps = [jnp.exp(s - m) for s, m in zip(scores, ms)]
            dens = [jnp.sum(p, axis=-1, keepdims=True) for p in ps]
            if has_sink:
                dens = [den + jnp.exp(sk - m) for den, m in zip(dens, ms)]
            pns = [(p * (1.0 / den)).astype(BF16) for p, den in zip(ps, dens)]
            lses = [m + jnp.log(den) for m, den in zip(ms, dens)]
            for j in range(ITEMS):
                o2 = jnp.dot(pns[j], vcats[j], preferred_element_type=F32)
                o_val = jnp.where(left, o2[:blk], o2[blk:])
                l_val = jnp.where(left, lses[j][:blk], lses[j][blk:])
                if d == 1:
                    o_ref[j * blk:(j + 1) * blk, :] = o_val
                    l_ref[j * blk:(j + 1) * blk, :] = l_val
                else:
                    o_ref[pl.ds(r0 + j, blk, stride=d), :] = o_val
                    l_ref[pl.ds(r0 + j, blk, stride=d), :] = l_val

        if groups == 1:
            group(0)
        else:
            def step(g, carry):
                group(g * ITEMS)
                return carry

            lax.fori_loop(0, groups, step, 0)

    cur, prev, nxt = _attn_specs(seq, blk, d, lambda t: t)
    in_specs = [cur, prev, cur, nxt, prev, cur, nxt, pl.BlockSpec((2 * blk, 3 * blk), lambda hp, t: (hp, 0))]
    args = [q, k, k, k, v, v, v, bias]
    if has_sink:
        in_specs.append(pl.BlockSpec((2 * blk, 1), lambda hp, t: (hp, 0)))
        args.append(sink)
    return pl.pallas_call(
        body, out_shape=[jax.ShapeDtypeStruct(q.shape, F32)] * 2, grid=(N_PAIRS, nct),
        in_specs=in_specs, out_specs=[cur, cur], name=name,
        compiler_params=_params("parallel", "parallel"))(*args)


def _attn_bwd(q, k, v, do, lse, delta, bias, sink, blk, d, name):
    _, seq, _ = q.shape
    chunk, groups, halo = _attn_geometry(blk, d)
    nct = seq // chunk
    has_sink = sink is not None
    n_in = 12 if has_sink else 11
    scale = HEAD_DIM ** -0.5

    def body(*refs):
        q_ref, kp, kc, kn, vp, vc, vn, do_ref, l_ref, d_ref, b_ref = refs[:11]
        s_ref = refs[11] if has_sink else None
        dq_ref, dk_ref, dv_ref, db_ref = refs[n_in:n_in + 4]
        ds_ref = refs[n_in + 4] if has_sink else None
        wk, wv = refs[-2], refs[-1]
        t = pl.program_id(1)

        @pl.when(t == 0)
        def _():
            wk[...] = jnp.zeros_like(wk)
            wv[...] = jnp.zeros_like(wv)
            db_ref[...] = jnp.zeros_like(db_ref)
            if has_sink:
                ds_ref[...] = jnp.zeros_like(ds_ref)

        @pl.when(t > 0)
        def _():
            for w in (wk, wv):
                keep = w[chunk:2 * chunk + halo]
                w[0:chunk + halo] = keep
                w[chunk + halo:2 * chunk + halo] = jnp.zeros((chunk, LANES), F32)

        @pl.when(t < nct)
        def _():
            lane = lax.broadcasted_iota(jnp.int32, (1, LANES), 1)
            left = lane < HEAD_DIM
            bias2 = b_ref[...]
            if d == 1:
                kwin = jnp.concatenate([kp[...], kc[...], kn[...]], axis=0).astype(BF16)
                vwin = jnp.concatenate([vp[...], vc[...], vn[...]], axis=0).astype(BF16)

            def group(r0):
                qss, doss, kcats, scores, dps, lcols, dcols = [], [], [], [], [], [], []
                for j in range(ITEMS):
                    qs = _stack_pair(_item_rows(q_ref, j, r0, blk, d) * scale, left)
                    dos = _stack_pair(_item_rows(do_ref, j, r0, blk, d), left)
                    if d == 1:
                        kcat, vcat = kwin[j * blk:(j + 3) * blk], vwin[j * blk:(j + 3) * blk]
                    else:
                        kcat = jnp.concatenate([_item_rows(r, j, r0, blk, d) for r in (kp, kc, kn)], axis=0).astype(BF16)
                        vcat = jnp.concatenate([_item_rows(r, j, r0, blk, d) for r in (vp, vc, vn)], axis=0).astype(BF16)
                    l2, d2 = _item_rows(l_ref, j, r0, blk, d), _item_rows(d_ref, j, r0, blk, d)
                    lcols.append(jnp.concatenate([jnp.max(jnp.where(left, l2, NEG_INF), axis=-1, keepdims=True),
                                                  jnp.max(jnp.where(left, NEG_INF, l2), axis=-1, keepdims=True)], axis=0))
                    dcols.append(jnp.concatenate([jnp.sum(jnp.where(lane == 0, d2, 0.0), axis=-1, keepdims=True),
                                                  jnp.sum(jnp.where(lane == HEAD_DIM, d2, 0.0), axis=-1, keepdims=True)],
                                                 axis=0))
                    scores.append(_dot_nt(qs, kcat) + bias2 + _item_penalty(t, nct, j, blk, d))
                    dps.append(_dot_nt(dos, vcat))
                    qss.append(qs)
                    doss.append(dos)
                    kcats.append(kcat)
                ps = [jnp.exp(s - lc) for s, lc in zip(scores, lcols)]
                dss = [p * (dp - dc) for p, dp, dc in zip(ps, dps, dcols)]
                db_ref[...] += functools.reduce(lambda a, b: a + b, dss)
                if has_sink:
                    sk = s_ref[...]
                    ds_ref[...] -= functools.reduce(lambda a, b: a + b, [dc * jnp.exp(sk - lc) for dc, lc in zip(dcols, lcols)])
                dsbs = [ds.astype(BF16) for ds in dss]
                for j in range(ITEMS):
                    dq2 = jnp.dot(dsbs[j], kcats[j], preferred_element_type=F32) * scale
                    dq_val = jnp.where(left, dq2[:blk], dq2[blk:])
                    if d == 1:
                        dq_ref[j * blk:(j + 1) * blk, :] = dq_val
                    else:
                        dq_ref[pl.ds(r0 + j, blk, stride=d), :] = dq_val
                news = [(_dot_tn(dsbs[j], qss[j]), _dot_tn(ps[j].astype(BF16), doss[j])) for j in range(ITEMS)]
                if d == 1:
                    for which, w in enumerate((wk, wv)):
                        for b in range(ITEMS + 2):
                            parts = [news[j][which][(b - j) * blk:(b - j + 1) * blk] for j in range(ITEMS) if 0 <= b - j < 3]
                            w[chunk + (b - 1) * blk:chunk + b * blk, :] += functools.reduce(lambda x, y: x + y, parts)
                else:
                    for j in range(ITEMS):
                        for which, w in enumerate((wk, wv)):
                            for c in range(3):
                                w[pl.ds(c * chunk + r0 + j, blk, stride=d), :] += news[j][which][c * blk:(c + 1) * blk]

            if groups == 1:
                group(0)
            else:
                def step(g, carry):
                    group(g * ITEMS)
                    return carry

                lax.fori_loop(0, groups, step, 0)

        dk_ref[...] = wk[0:chunk]
        dv_ref[...] = wv[0:chunk]

    cur, prev, nxt = _attn_specs(seq, blk, d, lambda t: jnp.minimum(t, nct - 1))
    lag = pl.BlockSpec((None, chunk, LANES), lambda hp, t: (hp, jnp.maximum(t - 1, 0), 0))
    band = pl.BlockSpec((2 * blk, 3 * blk), lambda hp, t: (hp, 0))
    col = pl.BlockSpec((2 * blk, 1), lambda hp, t: (hp, 0))
    in_specs = [cur, prev, cur, nxt, prev, cur, nxt, cur, cur, cur, band]
    args = [q, k, k, k, v, v, v, do, lse, delta, bias]
    out_shape = [jax.ShapeDtypeStruct(q.shape, F32)] * 3 + [jax.ShapeDtypeStruct((N_HEADS * blk, 3 * blk), F32)]
    out_specs = [cur, lag, lag, band]
    if has_sink:
        in_specs.append(col)
        args.append(sink)
        out_shape.append(jax.ShapeDtypeStruct((N_HEADS * blk, 1), F32))
        out_specs.append(col)
    window = pltpu.VMEM((2 * chunk + halo, LANES), F32)
    return pl.pallas_call(
        body, out_shape=out_shape, grid=(N_PAIRS, nct + 1), in_specs=in_specs, out_specs=out_specs,
        scratch_shapes=[window, window], name=name,
        compiler_params=_params("arbitrary", "arbitrary"))(*args)


def _combine_patterns(outs, lses):
    _, rows, _ = outs[0].shape
    tm = 256
    n = len(outs)

    def body(*refs):
        o_refs, l_refs = refs[:n], refs[n:2 * n]
        y_ref, lse_ref = refs[2 * n], refs[2 * n + 1]
        for hp in range(N_PAIRS):
            ls = [r[hp] for r in l_refs]
            m = functools.reduce(jnp.maximum, ls)
            es = [jnp.exp(l - m) for l in ls]
            den = functools.reduce(lambda a, b: a + b, es)
            num = functools.reduce(lambda a, b: a + b, [e * r[hp] for e, r in zip(es, o_refs)])
            y_ref[:, hp * LANES:(hp + 1) * LANES] = num / den
            lse_ref[hp] = m + jnp.log(den)

    pm = pl.BlockSpec((N_PAIRS, tm, LANES), lambda i: (0, i, 0))
    return pl.pallas_call(
        body, out_shape=[jax.ShapeDtypeStruct((rows, WIDTH), F32), jax.ShapeDtypeStruct((N_PAIRS, rows, LANES), F32)],
        grid=(rows // tm,), in_specs=[pm] * (2 * n), out_specs=[pl.BlockSpec((tm, WIDTH), lambda i: (i, 0)), pm],
        name="combine_a", compiler_params=_params("parallel"))(*outs, *lses)


def _pairs_to_tokens(a):
    _, rows, _ = a.shape
    tm = 256

    def body(a_ref, o_ref):
        o_ref[...] = _get_pairs(a_ref)

    return pl.pallas_call(
        body, out_shape=jax.ShapeDtypeStruct((rows, WIDTH), a.dtype), grid=(rows // tm,),
        in_specs=[pl.BlockSpec((N_PAIRS, tm, LANES), lambda i: (0, i, 0))],
        out_specs=pl.BlockSpec((tm, WIDTH), lambda i: (i, 0)), name="pairs_to_tokens",
        compiler_params=_params("parallel"))(a)


def _attn_bwd_prep(dy, y, name):
    rows = dy.shape[0]
    tm = 256

    def body(dy_ref, y_ref, do_ref, dl_ref):
        dyv = dy_ref[...]
        _put_pairs(do_ref, dyv)
        _put_pairs(dl_ref, _seg_sum(dyv * y_ref[...]))

    tok = pl.BlockSpec((tm, WIDTH), lambda i: (i, 0))
    pm = pl.BlockSpec((N_PAIRS, tm, LANES), lambda i: (0, i, 0))
    return pl.pallas_call(
        body, out_shape=[jax.ShapeDtypeStruct((N_PAIRS, rows, LANES), F32)] * 2, grid=(rows // tm,),
        in_specs=[tok, tok], out_specs=[pm, pm], name=name, compiler_params=_params("parallel"))(dy, y)


def _tile_gain(g, reps):
    return jnp.tile(g[None, :], (1, reps))


def _local_step(x, p, target, w_in_of, rest_of, small):
    rel_table = small["rel_table"]
    buckets_a = [_band_buckets(blk, d) for blk, d in DILATED]
    buckets_b = _band_buckets(BLK_B, 1)
    bias_a = [_bias_tiles(rel_table, bk, 0, "bias_a").reshape(N_HEADS * bk.shape[0], -1) for bk in buckets_a]
    bias_b = _bias_tiles(rel_table, buckets_b, N_HEADS, "bias_b").reshape(N_HEADS * BLK_B, -1)

    saved = []
    for l in range(DEPTH):
        g_mix, g_ffn, g_ple = (small[n][l][None, :] for n in ("norm_mix_g", "norm_ffn_g", "norm_ple_g"))
        gqa, gka, gqb = (_tile_gain(small[n][l], N_HEADS) for n in ("qnorm_a_g", "knorm_a_g", "qnorm_b_g"))
        gkb = _tile_gain(small["knorm_b_g"][l], N_KV_B)
        sink = jnp.repeat(small["sink_b"][l], BLK_B)[:, None]

        h = _rms_fwd(x, g_mix, "rms_mix")
        w_in = w_in_of(l, h)
        proj = _mm(h, w_in, "nt", F32, "mm_in")
        qa, ka, va, qb, kb, vb = _qknorm_fwd(proj, gqa, gka, gqb, gkb)
        outs, lses = [], []
        for (blk, d), bias in zip(DILATED, bias_a):
            o, ls = _attn_fwd(qa, ka, va, bias, None, blk, d, f"attn_a{d}_fwd")
            outs.append(o)
            lses.append(ls)
        ya, lse_a = _combine_patterns(outs, lses)
        yb, lse_b = _attn_fwd(qb, kb, vb, bias_b, sink, BLK_B, 1, "attn_b_fwd")
        yb = _pairs_to_tokens(yb)
        w = dict(rest_of(l, yb), w_in=w_in)
        def gate(products, extra):
            (ca_, cb_), (ga_, gb_) = products, extra
            return _sigmoid(ga_) * ca_ + _sigmoid(gb_) * cb_, ca_, cb_

        merged, ca, cb = _mm_fused([(ya, w["w_branch_a"], "nn"), (yb, w["w_branch_b"], "nn")],
                                   [(proj, OFF_GA), (proj, OFF_GB)], gate, [BF16, BF16, BF16], "mm_branches_gate")
        x1 = _mm(merged, w["w_out"], "nn", F32, "mm_out", res=x)

        h2 = _rms_fwd(x1, g_ffn, "rms_ffn")

        def swiglu(products, extra):
            a_, u_ = products
            return (a_ * _sigmoid(a_)) * u_, a_, u_

        hid, a, u = _mm_fused([(h2, w["w_ffn_gate"], "nt"), (h2, w["w_ffn_up"], "nt")], [], swiglu,
                              [BF16, BF16, BF16], "mm_ffn_gate_up")
        x2 = _mm(hid, w["w_ffn_down"], "nn", F32, "mm_ffn_down", res=x1)

        h3 = _rms_fwd(x2, g_ple, "rms_ple")

        def ple(products, extra):
            z_, e_ = products
            return extra[0] + _sigmoid(z_) * e_, z_, e_

        x3, z, e = _mm_fused([(h3, w["w_ple_gate"], "nn"), (p[l], w["w_ple_proj"], "nn")], [(x2, 0)], ple,
                             [F32, BF16, BF16], "mm_ple")
        saved.append(dict(w=w, x0=x, h=h, proj=proj, qa=qa, ka=ka, va=va, qb=qb, kb=kb, vb=vb, ya=ya, lse_a=lse_a,
                          yb=yb, lse_b=lse_b, ca=ca, cb=cb, merged=merged, x1=x1, h2=h2, a=a, u=u, hid=hid,
                          x2=x2, h3=h3, z=z, e=e))
        x = x3

    dx, loss_acc = _loss_grad(x, target)
    loss = loss_acc[0, 0]

    gbig = [{} for _ in range(DEPTH)]
    marks = [{} for _ in range(DEPTH)]
    gsmall = {n: [None] * DEPTH for n in SMALL if n != "rel_table"}
    dtable_a = jnp.zeros((N_HEADS, NUM_BUCKETS), F32)
    dtable_b = jnp.zeros((N_HEADS, NUM_BUCKETS), F32)

    for l in reversed(range(DEPTH)):
        sv = saved[l]
        w = sv["w"]
        g_mix, g_ffn, g_ple = (small[n][l][None, :] for n in ("norm_mix_g", "norm_ffn_g", "norm_ple_g"))
        gqa, gka, gqb = (_tile_gain(small[n][l], N_HEADS) for n in ("qnorm_a_g", "knorm_a_g", "qnorm_b_g"))
        gkb = _tile_gain(small["knorm_b_g"][l], N_KV_B)
        sink = jnp.repeat(small["sink_b"][l], BLK_B)[:, None]

        def ple_bwd(dx_, z_, e_):
            s = _sigmoid(z_.astype(F32))
            return dx_ * s, dx_ * e_.astype(F32) * (s * (1.0 - s))

        de, dz = _ew(ple_bwd, [(dx, 0), (sv["z"], 0), (sv["e"], 0)], [BF16, BF16], width=D_MODEL, bw=D_MODEL,
                     name="ple_bwd")
        gbig[l]["w_ple_proj"] = _mm(p[l], de, "tn", BF16, "mm_d_ple_proj")
        gbig[l]["w_ple_gate"] = _mm(sv["h3"], dz, "tn", BF16, "mm_d_ple_gate")
        dh3 = _mm(dz, w["w_ple_gate"], "nt", F32, "mm_dh3")
        dx, dxb, gsmall["norm_ple_g"][l] = _rms_bwd(sv["x2"], g_ple, dh3, dx, "rms_ple_bwd")

        gbig[l]["w_ffn_down"] = _mm(sv["hid"], dxb, "tn", BF16, "mm_d_ffn_down")

        def swiglu_bwd(products, extra):
            dh_, a_, u_ = products[0], extra[0].astype(F32), extra[1].astype(F32)
            s = _sigmoid(a_)
            return dh_ * u_ * (s * (1.0 + a_ * (1.0 - s))), dh_ * (a_ * s)

        da, du = _mm_fused([(dxb, w["w_ffn_down"], "nt")], [(sv["a"], 0), (sv["u"], 0)], swiglu_bwd, [BF16, BF16],
                           "mm_dhid_swiglu_bwd")
        gbig[l]["w_ffn_gate"] = _mm(da, sv["h2"], "tn", BF16, "mm_d_ffn_gate")
        gbig[l]["w_ffn_up"] = _mm(du, sv["h2"], "tn", BF16, "mm_d_ffn_up")
        dh2 = _mm(da, w["w_ffn_gate"], "nn", F32, "mm_dh2_gate")
        dh2 = _mm(du, w["w_ffn_up"], "nn", F32, "mm_dh2_up", res=dh2)
        dx, dxb, gsmall["norm_ffn_g"][l] = _rms_bwd(sv["x1"], g_ffn, dh2, dx, "rms_ffn_bwd")

        gbig[l]["w_out"] = _mm(sv["merged"], dxb, "tn", BF16, "mm_d_out")

        def gate_bwd(products, extra):
            dm_, ca_, cb_ = products[0], extra[0].astype(F32), extra[1].astype(F32)
            sa, sb = _sigmoid(extra[2]), _sigmoid(extra[3])
            return dm_ * sa, dm_ * sb, dm_ * ca_ * (sa * (1.0 - sa)), dm_ * cb_ * (sb * (1.0 - sb))

        dca, dcb, dga, dgb = _mm_fused(
            [(dxb, w["w_out"], "nt")], [(sv["ca"], 0), (sv["cb"], 0), (sv["proj"], OFF_GA), (sv["proj"], OFF_GB)],
            gate_bwd, [BF16, BF16, BF16, BF16], "mm_dmerged_gate_bwd")
        gbig[l]["w_branch_a"] = _mm(sv["ya"], dca, "tn", BF16, "mm_d_branch_a")
        gbig[l]["w_branch_b"] = _mm(sv["yb"], dcb, "tn", BF16, "mm_d_branch_b")
        dya = _mm(dca, w["w_branch_a"], "nt", F32, "mm_dya")
        dyb = _mm(dcb, w["w_branch_b"], "nt", F32, "mm_dyb")

        dya, delta_a = _attn_bwd_prep(dya, sv["ya"], "attn_a_bwd_prep")
        dyb, delta_b = _attn_bwd_prep(dyb, sv["yb"], "attn_b_bwd_prep")

        dqa, dka, dva = [], [], []
        for (blk, d), bias, bk in zip(DILATED, bias_a, buckets_a):
            dq_, dk_, dv_, db_ = _attn_bwd(sv["qa"], sv["ka"], sv["va"], dya, sv["lse_a"], delta_a, bias, None, blk, d,
                                           f"attn_a{d}_bwd")
            dqa.append(dq_)
            dka.append(dk_)
            dva.append(dv_)
            dtable_a = dtable_a + _table_grad(db_.reshape(N_HEADS, blk, 3 * blk), bk, "table_grad_a")
        dqb, dkb, dvb, db_, dsink = _attn_bwd(sv["qb"], sv["kb"], sv["vb"], dyb, sv["lse_b"], delta_b, bias_b, sink,
                                              BLK_B, 1, "attn_b_bwd")
        dtable_b = dtable_b + _table_grad(db_.reshape(N_HEADS, BLK_B, 3 * BLK_B), buckets_b, "table_grad_b")
        gsmall["sink_b"][l] = dsink.reshape(N_HEADS, BLK_B).sum(axis=1)

        dproj, pqa, pka, pqb, pkb = _qknorm_bwd(sv["proj"], gqa, gka, gqb, gkb, dqa, dka, dva, dqb, dkb, dvb, dga, dgb)
        marks[l]["attn_bwd_done"] = dproj
        gsmall["qnorm_a_g"][l] = pqa.reshape(N_HEADS, HEAD_DIM).sum(0)
        gsmall["knorm_a_g"][l] = pka.reshape(N_HEADS, HEAD_DIM).sum(0)
        gsmall["qnorm_b_g"][l] = pqb.reshape(N_HEADS, HEAD_DIM).sum(0)
        gsmall["knorm_b_g"][l] = pkb.reshape(N_KV_B, HEAD_DIM).sum(0)
        gbig[l]["w_in"] = _mm(dproj, sv["h"], "tn", BF16, "mm_d_in")
        dh = _mm(dproj, w["w_in"], "nn", F32, "mm_dh")
        dx, _, gsmall["norm_mix_g"][l] = _rms_bwd(sv["x0"], g_mix, dh, dx, "rms_mix_bwd")
        gsmall["norm_mix_g"][l] = gsmall["norm_mix_g"][l][0]
        gsmall["norm_ffn_g"][l] = gsmall["norm_ffn_g"][l][0]
        gsmall["norm_ple_g"][l] = gsmall["norm_ple_g"][l][0]

    gsmall = {n: jnp.stack(v) for n, v in gsmall.items()}
    gsmall["rel_table"] = jnp.concatenate([dtable_a, dtable_b], axis=0).T
    return loss, dx, gbig, gsmall, marks


def _place():
    return lax.axis_index("x"), lax.axis_index("y"), lax.axis_index("c")


def _flip(v, bit):
    return 1 - v if bit else v


CHIP_RELATIONS = ((0, 1), (1, 0), (1, 1))
ANY = pl.BlockSpec(memory_space=pl.ANY)


def _allgather_body(w_refs, out_refs, send_sems, recv_sems):
    x, y, c = _place()
    chips = [(_flip(x, a), _flip(y, b)) for a, b in CHIP_RELATIONS]

    def make(g):
        w_ref, out_ref = w_refs[g], out_refs[g]
        half = w_ref.shape[0] // 2

        def part(px, py, pc):
            return out_ref.at[2 * px + py, pl.ds(pc * half, half), :]

        def copy(k, block, to, src=None):
            return pltpu.make_async_remote_copy(
                src_ref=part(*block) if src is None else src, dst_ref=part(*block),
                send_sem=send_sems.at[7 * g + k], recv_sem=recv_sems.at[7 * g + k], device_id=to,
                device_id_type=MESH_ID)

        own = pltpu.make_async_remote_copy(
            src_ref=w_ref, dst_ref=out_ref.at[2 * x + y], send_sem=send_sems.at[7 * g + 6],
            recv_sem=recv_sems.at[7 * g + 6], device_id=(x, y, 1 - c), device_id_type=MESH_ID)
        first = [copy(k, (x, y, c), (*chip, c), src=w_ref.at[pl.ds(c * half, half), :]) for k, chip in enumerate(chips)]
        passed = [copy(3 + k, (*chip, c), (x, y, 1 - c)) for k, chip in enumerate(chips)]
        arrive = [copy(k, (*chip, c), (x, y, c)) for k, chip in enumerate(chips)]
        arrive2 = [copy(3 + k, (*chip, 1 - c), (x, y, c)) for k, chip in enumerate(chips)]
        return own, first, passed, arrive, arrive2

    made = [make(g) for g in range(len(w_refs))]
    for own, first, _, _, _ in made:
        own.start()
        for cp in first:
            cp.start()
    for _, _, passed, arrive, _ in made:
        for k in range(3):
            arrive[k].wait_recv()
            passed[k].start()
    for own, first, passed, _, arrive2 in made:
        for k in range(3):
            arrive2[k].wait_recv()
        own.wait_recv()
        for cp in first + passed + [own]:
            cp.wait_send()


def _sibling(x, y, c):
    return [(x, y, 1 - c)]


def _same_core_of_other_chips(x, y, c):
    return [(_flip(x, a), _flip(y, b), c) for a, b in CHIP_RELATIONS]


def _exchange(body, ins, out_types, n_sems, name, sequencer=None):
    n = len(ins)
    sems = (pltpu.SemaphoreType.DMA((n_sems,)), pltpu.SemaphoreType.DMA((n_sems,)))
    if sequencer is None:
        in_place = out_types is None
        out_shape = [jax.ShapeDtypeStruct(a.shape, a.dtype) for a in ins] if in_place else out_types

        def tc_body(*refs):
            body(refs[:n], refs[n:n + len(out_shape)], refs[-2], refs[-1])

        return list(pl.pallas_call(
            tc_body, out_shape=out_shape, in_specs=[ANY] * n, out_specs=[ANY] * len(out_shape),
            input_output_aliases={g: g for g in range(n)} if in_place else {}, scratch_shapes=list(sems), name=name)(*ins))

    collective_id, peers = sequencer
    hbm = pltpu.MemorySpace.HBM
    in_refs = [jax.new_ref(a, memory_space=hbm) for a in ins]
    out_refs = in_refs if out_types is None else [jax.empty_ref(t, memory_space=hbm) for t in out_types]

    @pl.kernel(mesh=plsc.ScalarSubcoreMesh(axis_name="sequencer", num_cores=1), name=name, scratch_types=sems,
               compiler_params=pltpu.CompilerParams(collective_id=collective_id))
    def launch(send_sems, recv_sems):
        barrier = pltpu.get_barrier_semaphore()
        devices = peers(*_place())
        for device in devices:
            pl.semaphore_signal(barrier, inc=1, device_id=device, device_id_type=MESH_ID)
        pl.semaphore_wait(barrier, len(devices))
        body(in_refs, out_refs, send_sems, recv_sems)

    launch()
    return [r[...] for r in out_refs]


def _allgather(shards, name, sequencer=None):
    out_types = [jax.ShapeDtypeStruct((N_CHIPS,) + s.shape, s.dtype) for s in shards]
    if sequencer is not None:
        sequencer = (sequencer, lambda x, y, c: _sibling(x, y, c) + _same_core_of_other_chips(x, y, c))
    return _exchange(_allgather_body, shards, out_types, 7 * len(shards), name, sequencer)


def _half_tile(half):
    return max(t for t in range(16, 1025, 16) if half % t == 0)


def _run_copies(cps):
    for cp in cps:
        cp.start()
    for cp in cps:
        cp.wait_recv()
    for cp in cps:
        cp.wait_send()


def _sibling_halves(gsends, name, sequencer=None):
    def body(g_refs, out_refs, send_sems, recv_sems):
        x, y, c = _place()
        cps = []
        for g, (g_ref, out_ref) in enumerate(zip(g_refs, out_refs)):
            half = g_ref.shape[1] // 2
            cps.append(pltpu.make_async_remote_copy(
                src_ref=g_ref.at[:, pl.ds((1 - c) * half, half), :], dst_ref=out_ref,
                send_sem=send_sems.at[g], recv_sem=recv_sems.at[g], device_id=(x, y, 1 - c), device_id_type=MESH_ID))
        _run_copies(cps)

    out_types = [jax.ShapeDtypeStruct((s.shape[0], s.shape[1] // 2, s.shape[2]), s.dtype) for s in gsends]
    return _exchange(body, gsends, out_types, len(gsends), name, sequencer and (sequencer, _sibling))


def _chip_sums(gsend, sib, place):
    n, rows, cols = gsend.shape
    half = rows // 2
    tm = _half_tile(half)
    nblk = half // tm

    def body(s_ref, g_ref, sib_ref, o_ref):
        o_ref[0] = (g_ref[0].astype(F32) + sib_ref[0].astype(F32)).astype(o_ref.dtype)

    grid_spec = pltpu.PrefetchScalarGridSpec(
        num_scalar_prefetch=1, grid=(n, nblk),
        in_specs=[pl.BlockSpec((1, tm, cols), lambda k, i, s: (jnp.bitwise_xor(s[0], k), s[1] * nblk + i, 0)),
                  pl.BlockSpec((1, tm, cols), lambda k, i, s: (jnp.bitwise_xor(s[0], k), i, 0))],
        out_specs=pl.BlockSpec((1, tm, cols), lambda k, i, s: (k, i, 0)))
    return pl.pallas_call(
        body, out_shape=jax.ShapeDtypeStruct((n, half, cols), BF16), grid_spec=grid_spec,
        name="rs_chip_sums", compiler_params=_params("parallel", "parallel"))(place, gsend, sib)


def _exchange_chip_sums(tsends, name, sequencer=None):
    def body(t_refs, out_refs, send_sems, recv_sems):
        x, y, c = _place()
        cps = []
        for g, (t_ref, out_ref) in enumerate(zip(t_refs, out_refs)):
            for k, device in enumerate(_same_core_of_other_chips(x, y, c)):
                cps.append(pltpu.make_async_remote_copy(
                    src_ref=t_ref.at[k + 1], dst_ref=out_ref.at[k], send_sem=send_sems.at[3 * g + k],
                    recv_sem=recv_sems.at[3 * g + k], device_id=device, device_id_type=MESH_ID))
        _run_copies(cps)

    out_types = [jax.ShapeDtypeStruct((3,) + s.shape[1:], s.dtype) for s in tsends]
    return _exchange(body, tsends, out_types, 3 * len(tsends), name,
                     sequencer and (sequencer, _same_core_of_other_chips))


def _final_sum(tsend, recv, place):
    n, half, cols = tsend.shape
    tm = _half_tile(half)
    nblk = half // tm

    def body(s_ref, t_ref, r_ref, o_ref):
        o_ref[...] = ((t_ref[0].astype(F32) + r_ref[0].astype(F32)) + r_ref[1].astype(F32)) + r_ref[2].astype(F32)

    grid_spec = pltpu.PrefetchScalarGridSpec(
        num_scalar_prefetch=1, grid=(nblk,),
        in_specs=[pl.BlockSpec((1, tm, cols), lambda i, s: (0, i, 0)), pl.BlockSpec((n - 1, tm, cols), lambda i, s: (0, i, 0))],
        out_specs=pl.BlockSpec((tm, cols), lambda i, s: (s[1] * nblk + i, 0)))
    return pl.pallas_call(
        body, out_shape=jax.ShapeDtypeStruct((2 * half, cols), F32), grid_spec=grid_spec, name="rs_final_sum",
        compiler_params=_params("parallel"))(place, tsend, recv)


def _join_halves(gfulls, name, sequencer=None):
    def body(g_refs, out_refs, send_sems, recv_sems):
        x, y, c = _place()
        n = len(g_refs)

        def copy(g, pc):
            half = g_refs[g].shape[0] // 2
            return pltpu.make_async_remote_copy(
                src_ref=g_refs[g].at[pl.ds(pc * half, half), :], dst_ref=out_refs[g].at[pl.ds(pc * half, half), :],
                send_sem=send_sems.at[g], recv_sem=recv_sems.at[g], device_id=(x, y, 1 - c), device_id_type=MESH_ID)

        mine = [copy(g, c) for g in range(n)]
        for cp in mine:
            cp.start()
        for g in range(n):
            copy(g, 1 - c).wait_recv()
        for cp in mine:
            cp.wait_send()

    return _exchange(body, gfulls, None, len(gfulls), name, sequencer and (sequencer, _sibling))


def _allreduce_small(v):
    rows, cols = v.shape

    def body(v_ref, out_ref, buf, send_sems, recv_sems):
        x, y, c = _place()
        cps = []
        for k in range(1, 8):
            peer = (_flip(x, (k >> 2) & 1), _flip(y, (k >> 1) & 1), _flip(c, k & 1))
            cps.append(pltpu.make_async_remote_copy(
                src_ref=v_ref, dst_ref=buf.at[k - 1], send_sem=send_sems.at[k - 1], recv_sem=recv_sems.at[k - 1],
                device_id=peer, device_id_type=MESH_ID))
        for cp in cps:
            cp.start()
        for cp in cps:
            cp.wait_recv()
        for cp in cps:
            cp.wait_send()
        t0 = v_ref[...] + buf[0]
        t1 = buf[1] + buf[2]
        t2 = buf[3] + buf[4]
        t3 = buf[5] + buf[6]
        out_ref[...] = (t0 + t1) + (t2 + t3)

    vm = pl.BlockSpec(memory_space=pltpu.VMEM)
    return pl.pallas_call(
        body, out_shape=jax.ShapeDtypeStruct((rows, cols), F32), in_specs=[vm], out_specs=vm,
        scratch_shapes=[pltpu.VMEM((7, rows, cols), F32), pltpu.SemaphoreType.DMA((7,)), pltpu.SemaphoreType.DMA((7,))],
        name="allreduce_small")(v)


BIG_INFO = {n: (shape, ax) for n, shape, ax in BIG}
GROUPS = (("w_in",), ("w_ffn_gate", "w_ffn_up", "w_ffn_down", "w_out", "w_ple_gate"),
          ("w_branch_a", "w_branch_b", "w_ple_proj"))


def _shard_shape(name):
    (k, m), ax = BIG_INFO[name]
    return (k // N_CHIPS, m) if ax == 0 else (k, m // N_CHIPS)


def _group_rows(group):
    offs, off = {}, 0
    for n in group:
        offs[n] = off
        off += _shard_shape(n)[0]
    return offs, off


def _pack_groups(shards, layer, dtype):
    return [jnp.concatenate([shards[n][layer].astype(dtype) for n in group], axis=0) for group in GROUPS]


def _unpack_full(gathered, groups):
    out = {}
    for group, arr in zip(groups, gathered):
        offs, _ = _group_rows(group)
        for n in group:
            rows, cols = _shard_shape(n)
            (k, m), ax = BIG_INFO[n]
            slab = arr[:, offs[n]:offs[n] + rows]
            out[n] = slab.reshape(k, m) if ax == 0 else jnp.transpose(slab, (1, 0, 2)).reshape(k, m)
    return out


def _pack_grads(gfull):
    out = []
    for group in GROUPS:
        parts = []
        for n in group:
            rows, cols = _shard_shape(n)
            ax = BIG_INFO[n][1]
            slab = (gfull[n].reshape(N_CHIPS, rows, cols) if ax == 0
                    else jnp.transpose(gfull[n].reshape(rows, N_CHIPS, cols), (1, 0, 2)))
            parts.append(slab)
        out.append(jnp.concatenate(parts, axis=1))
    return out


def _after(values, mark):
    values, _ = lax.optimization_barrier((values, mark))
    return values


def _reduce_scatter_begin(gsends, place, tag, ids):
    sibs = _sibling_halves(gsends, "rs_sibling_halves_" + tag, ids[0])
    tsends = [_chip_sums(g, s, place) for g, s in zip(gsends, sibs)]
    return tsends, _exchange_chip_sums(tsends, "rs_exchange_" + tag, ids[1])


def _reduce_scatter_finish(begun, place, tag, ids, hold):
    tsends, recvs = begun
    recvs = _after(recvs, hold)
    return _join_halves([_final_sum(t, r, place) for t, r in zip(tsends, recvs)], "rs_join_halves_" + tag, ids[2])


SMALL_SHAPES = {"rel_table": (NUM_BUCKETS, 2 * N_HEADS), "norm_mix_g": (DEPTH, D_MODEL), "qnorm_a_g": (DEPTH, HEAD_DIM),
                "knorm_a_g": (DEPTH, HEAD_DIM), "qnorm_b_g": (DEPTH, HEAD_DIM), "knorm_b_g": (DEPTH, HEAD_DIM),
                "sink_b": (DEPTH, N_HEADS), "norm_ffn_g": (DEPTH, D_MODEL), "norm_ple_g": (DEPTH, D_MODEL)}


def _pack_small(vals, last=None):
    flat = jnp.concatenate([vals[n].astype(F32).reshape(-1) for n in SMALL])
    tail = jnp.zeros((SMALL_ROWS * LANES - flat.shape[0],), F32)
    if last is not None:
        tail = tail.at[-1].set(last)
    return jnp.concatenate([flat, tail]).reshape(SMALL_ROWS, LANES)


def _unpack_small(packed):
    flat, out, off = packed.reshape(-1), {}, 0
    for n in SMALL:
        size = math.prod(SMALL_SHAPES[n])
        out[n] = flat[off:off + size].reshape(SMALL_SHAPES[n])
        off += size
    return out


def _adamw(w, gs, g_row, m, v, name):
    c1 = 1.0 - ADAM_B1 ** ADAM_STEP
    c2 = 1.0 - ADAM_B2 ** ADAM_STEP
    total, width = w.shape
    n_layers = len(gs)
    per = total // n_layers
    tm = max(t for t in range(8, 513, 8) if per % t == 0 and g_row % t == 0)
    nblk = per // tm

    def body(*refs):
        w_ref, g_refs = refs[0], refs[1:1 + n_layers]
        m_ref, v_ref, og, od, om, ov = refs[1 + n_layers:]
        layer = pl.program_id(0) // nblk
        g = g_refs[0][...]
        for l in range(1, n_layers):
            g = jnp.where(layer == l, g_refs[l][...], g)
        m_new = ADAM_B1 * m_ref[...] + (1.0 - ADAM_B1) * g
        v_new = ADAM_B2 * v_ref[...] + (1.0 - ADAM_B2) * (g * g)
        og[...] = g
        od[...] = -ADAM_LR * ((m_new / c1) / (jnp.sqrt(v_new / c2) + ADAM_EPS) + ADAM_WD * w_ref[...])
        om[...] = m_new
        ov[...] = v_new

    row = pl.BlockSpec((tm, width), lambda i: (i, 0))
    g_specs = [pl.BlockSpec((tm, width), lambda i, l=l: (g_row // tm + jnp.clip(i - l * nblk, 0, nblk - 1), 0))
               for l in range(n_layers)]
    return pl.pallas_call(
        body, out_shape=[jax.ShapeDtypeStruct((total, width), F32)] * 4, grid=(total // tm,),
        in_specs=[row] + g_specs + [row, row], out_specs=[row] * 4, name=name,
        compiler_params=_params("parallel"))(w, *gs, m, v)


def kernel(x, p, rel_table, norm_mix_g, w_in, qnorm_a_g, knorm_a_g, qnorm_b_g, knorm_b_g, sink_b, w_branch_a, w_branch_b, w_out, norm_ffn_g, w_ffn_gate, w_ffn_up, w_ffn_down, norm_ple_g, w_ple_gate, w_ple_proj, loss_target, m_rel_table, m_norm_mix_g, m_w_in, m_qnorm_a_g, m_knorm_a_g, m_qnorm_b_g, m_knorm_b_g, m_sink_b, m_w_branch_a, m_w_branch_b, m_w_out, m_norm_ffn_g, m_w_ffn_gate, m_w_ffn_up, m_w_ffn_down, m_norm_ple_g, m_w_ple_gate, m_w_ple_proj, v_rel_table, v_norm_mix_g, v_w_in, v_qnorm_a_g, v_knorm_a_g, v_qnorm_b_g, v_knorm_b_g, v_sink_b, v_w_branch_a, v_w_branch_b, v_w_out, v_norm_ffn_g, v_w_ffn_gate, v_w_ffn_up, v_w_ffn_down, v_norm_ple_g, v_w_ple_gate, v_w_ple_proj):
    given = dict(locals())

    def held(name, a):
        return jnp.swapaxes(a, 1, 2) if name in TRANSPOSED else a

    weights = {n: held(n, given[n]) for n in WEIGHTS}
    moments_m = {n: held(n, given["m_" + n]) for n in WEIGHTS}
    moments_v = {n: held(n, given["v_" + n]) for n in WEIGHTS}
    xi, yi, ci = _place()
    place = jnp.stack([2 * xi + yi, ci]).astype(jnp.int32)

    shards = [_pack_groups(weights, l, BF16) for l in range(DEPTH)]
    w_in0 = _allgather(shards[0][:1], "allgather_w_in_layer0", sequencer=9)
    rest0 = _allgather(_after(shards[0][1:], w_in0), "allgather_rest_layer0", sequencer=1)
    gathered = [w_in0 + rest0, None]
    small = {n: weights[n] for n in SMALL}

    def w_in_of(l, mark):
        return _unpack_full(_after(gathered[l][:1], shards[1] if l == 0 else mark), GROUPS[:1])["w_in"]

    def rest_of(l, mark):
        if l == 0:
            gathered[1] = _allgather(_after(shards[1], mark), "allgather_layer1", sequencer=2)
        return _unpack_full(_after(gathered[l][1:], mark), GROUPS[1:])

    loss, dx, gbig, gsmall, marks = _local_step(x[0], p[:, 0], loss_target[0], w_in_of, rest_of, small)

    gsends = [_pack_grads(gbig[l]) for l in range(DEPTH)]
    stages = {"layer1": (gsends[1], (3, 4, 5)), "rest_layer0": (gsends[0][1:], (6, 7, 8)),
              "w_in_layer0": (gsends[0][:1], (10, 11, 12))}
    begun = {tag: _reduce_scatter_begin(g, place, tag, ids) for tag, (g, ids) in stages.items()}

    def finish(tag, hold):
        return _reduce_scatter_finish(begun[tag], place, tag, stages[tag][1], hold)

    red1 = finish("layer1", marks[0]["attn_bwd_done"])
    rest0 = finish("rest_layer0", marks[0]["attn_bwd_done"])

    grads, delta, new_m, new_v = {}, {}, {}, {}

    def update(group, reduced):
        offs, _ = _group_rows(group)
        for n in group:
            shape = weights[n].shape
            two_d = lambda a: a.reshape(shape[0] * shape[1], shape[2])
            outs = _adamw(two_d(weights[n]), reduced, offs[n], two_d(moments_m[n]), two_d(moments_v[n]), "adamw_" + n)
            grads[n], delta[n], new_m[n], new_v[n] = (held(n, o.reshape(shape)) for o in outs)

    for gi in (1, 2):
        update(GROUPS[gi], _after([rest0[gi - 1], red1[gi]], begun["w_in_layer0"][0]))
    others_done = [dx] + [delta[n] for gi in (1, 2) for n in GROUPS[gi]]
    update(GROUPS[0], [finish("w_in_layer0", others_done)[0], red1[0]])
    small_grads = _allreduce_small(_pack_small(gsmall, last=loss))
    g_, d_, m_, v_ = _adamw(_pack_small(weights), [small_grads], 0, _pack_small(moments_m), _pack_small(moments_v),
                            "adamw_small")
    grads.update(_unpack_small(g_))
    delta.update(_unpack_small(d_))
    new_m.update(_unpack_small(m_))
    new_v.update(_unpack_small(v_))

    return (small_grads[-1, -1], dx[None], *[grads[n] for n in WEIGHTS], *[delta[n] for n in WEIGHTS],
            *[new_m[n] for n in WEIGHTS], *[new_v[n] for n in WEIGHTS])
```

```python
import functools
import math

import jax
import jax.numpy as jnp
from jax import lax
from jax.experimental import pallas as pl
from jax.experimental.pallas import tpu as pltpu
from jax.experimental.pallas import tpu_sc as plsc

F32 = jnp.float32
BF16 = jnp.bfloat16
MESH_ID = pl.DeviceIdType.MESH

SEQ = 2048
D_MODEL = 1024
DEPTH = 2
HEAD_DIM = 64
N_HEADS = 8
WIDTH = N_HEADS * HEAD_DIM
N_PAIRS = 4
ITEMS = 4
N_KV_B = 2
PLE_DIM = 256
D_FF = 2816
D_IN = 4352
OFF_QA, OFF_KA, OFF_VA, OFF_QB, OFF_KB, OFF_VB, OFF_GA, OFF_GB = 0, 512, 1024, 1536, 2048, 2176, 2304, 3328
DILATED = ((64, 1), (64, 4), (64, 16))
BLK_B = 128
NUM_BUCKETS = 32
MAX_DISTANCE = 1024
RMS_EPS = 1e-6
NEG_INF = -1e30
LANES = 128
VMEM_LIMIT = 48 * 1024 * 1024

ADAM_LR, ADAM_B1, ADAM_B2, ADAM_EPS, ADAM_WD, ADAM_STEP = 0.001, 0.9, 0.999, 1e-08, 0.01, 10

TRANSPOSED = ("w_in", "w_ffn_gate", "w_ffn_up")
BIG = (
    ("w_in", (D_IN, D_MODEL), 0),
    ("w_branch_a", (WIDTH, D_MODEL), 1),
    ("w_branch_b", (WIDTH, D_MODEL), 1),
    ("w_out", (D_MODEL, D_MODEL), 0),
    ("w_ffn_gate", (D_FF, D_MODEL), 0),
    ("w_ffn_up", (D_FF, D_MODEL), 0),
    ("w_ffn_down", (D_FF, D_MODEL), 0),
    ("w_ple_gate", (D_MODEL, D_MODEL), 0),
    ("w_ple_proj", (PLE_DIM, D_MODEL), 1),
)
SMALL = ("rel_table", "norm_mix_g", "qnorm_a_g", "knorm_a_g", "qnorm_b_g", "knorm_b_g", "sink_b",
         "norm_ffn_g", "norm_ple_g")
WEIGHTS = ("rel_table", "norm_mix_g", "w_in", "qnorm_a_g", "knorm_a_g", "qnorm_b_g", "knorm_b_g", "sink_b",
           "w_branch_a", "w_branch_b", "w_out", "norm_ffn_g", "w_ffn_gate", "w_ffn_up", "w_ffn_down",
           "norm_ple_g", "w_ple_gate", "w_ple_proj")
N_CHIPS = 4
SMALL_ROWS = 64


def _params(*sem):
    return pltpu.CompilerParams(dimension_semantics=sem, vmem_limit_bytes=VMEM_LIMIT)


def _pick(dim, target):
    for t in (target, 512, 256, 128, 64, 32, 16, 8):
        if t <= target and dim % t == 0:
            return t
    return dim


MM_VMEM_BUDGET = 40 * 1024 * 1024
STEP_OVERHEAD_S = 0.4e-6
TILE_DMA_BYTES_PER_S = 1.5e12


def _mm_dims(a, b, mode):
    if mode == "nn":
        return a.shape[0], b.shape[1], a.shape[1]
    if mode == "nt":
        return a.shape[0], b.shape[0], a.shape[1]
    return a.shape[1], b.shape[1], a.shape[0]


def _mm_tiles(m, n, pairs, tile_bytes, col_offsets):
    best = None
    for tm in (t for t in range(LANES, m + 1, LANES) if m % t == 0):
        for tn in (t for t in range(LANES, n + 1, LANES) if n % t == 0 and all(o % t == 0 for o in col_offsets)):
            io = sum(tm * k * ab + tn * k * bb for k, ab, bb in pairs) + tm * tn * sum(tile_bytes)
            casts = sum((tm * k * 2 if ab == 4 else 0) + (tn * k * 2 if bb == 4 else 0) for k, ab, bb in pairs)
            if 2 * io + len(pairs) * tm * tn * 4 + casts > MM_VMEM_BUDGET:
                continue
            cost = (m // tm) * (n // tn) * STEP_OVERHEAD_S + io / TILE_DMA_BYTES_PER_S
            if best is None or (cost, -tm) < best[0]:
                best = ((cost, -tm), tm, tn)
    return best[1], best[2]


def _mm_fused(pairs, extras, epilogue, out_dtypes, name):
    m, n, _ = _mm_dims(*pairs[0])
    assert all(_mm_dims(*p)[:2] == (m, n) for p in pairs)
    tm, tn = _mm_tiles(
        m, n, [(_mm_dims(a, b, mode)[2], a.dtype.itemsize, b.dtype.itemsize) for a, b, mode in pairs],
        [e.dtype.itemsize for e, _ in extras] + [jnp.dtype(d).itemsize for d in out_dtypes], [off for _, off in extras])
    dims = {"nn": (((1,), (0,)), ((), ())), "nt": (((1,), (1,)), ((), ())), "tn": (((0,), (0,)), ((), ()))}
    in_specs, args = [], []
    for a, b, mode in pairs:
        k = _mm_dims(a, b, mode)[2]
        in_specs.append(pl.BlockSpec((k, tm), lambda i, j: (0, i)) if mode == "tn" else pl.BlockSpec((tm, k), lambda i, j: (i, 0)))
        in_specs.append(pl.BlockSpec((tn, k), lambda i, j: (j, 0)) if mode == "nt" else pl.BlockSpec((k, tn), lambda i, j: (0, j)))
        args += [a, b]
    for e, off in extras:
        in_specs.append(pl.BlockSpec((tm, tn), lambda i, j, o=off // tn: (i, o + j)))
        args.append(e)
    n_pairs, n_in = len(pairs), 2 * len(pairs) + len(extras)

    def body(*refs):
        products = [lax.dot_general(refs[2 * p][...].astype(BF16), refs[2 * p + 1][...].astype(BF16), dims[pairs[p][2]],
                                    preferred_element_type=F32) for p in range(n_pairs)]
        outs = epilogue(products, [r[...] for r in refs[2 * n_pairs:n_in]])
        for r, o in zip(refs[n_in:], outs):
            r[...] = o.astype(r.dtype)

    tile = pl.BlockSpec((tm, tn), lambda i, j: (i, j))
    return pl.pallas_call(
        body, out_shape=[jax.ShapeDtypeStruct((m, n), d) for d in out_dtypes], grid=(m // tm, n // tn),
        in_specs=in_specs, out_specs=[tile] * len(out_dtypes), name=name,
        compiler_params=_params("parallel", "parallel"))(*args)


def _mm(a, b, mode, out_dtype, name, res=None):
    if res is None:
        return _mm_fused([(a, b, mode)], [], lambda products, extra: products, [out_dtype], name)[0]
    return _mm_fused([(a, b, mode)], [(res, 0)], lambda products, extra: [products[0] + extra[0]], [out_dtype], name)[0]


def _ew(fn, ins, out_dtypes, *, width, bw, name, vecs=(), tm=256):
    rows = ins[0][0].shape[0]
    tm = _pick(rows, tm)
    n_in = len(ins) + len(vecs)

    def col_map(off_blocks):
        return lambda i, j: (i, off_blocks + j)

    in_specs = [pl.BlockSpec((tm, bw), col_map(off // bw)) for _, off in ins]
    in_specs += [pl.BlockSpec((1, bw), lambda i, j: (0, j)) for _ in vecs]

    def body(*refs):
        outs = fn(*[r[...] for r in refs[:n_in]])
        for r, o in zip(refs[n_in:], outs):
            r[...] = o.astype(r.dtype)

    return pl.pallas_call(
        body, out_shape=[jax.ShapeDtypeStruct((rows, width), dt) for dt in out_dtypes],
        grid=(rows // tm, width // bw), in_specs=in_specs,
        out_specs=[pl.BlockSpec((tm, bw), lambda i, j: (i, j)) for _ in out_dtypes],
        name=name, compiler_params=_params("parallel", "parallel"))(*[a for a, _ in ins], *vecs)


def _sigmoid(x):
    return 1.0 / (1.0 + jnp.exp(-x))


def _seg_sum(v):
    outs = []
    for k in range(v.shape[1] // LANES):
        vp = v[:, k * LANES:(k + 1) * LANES]
        left = lax.broadcasted_iota(jnp.int32, vp.shape, 1) < HEAD_DIM
        sl = jnp.sum(jnp.where(left, vp, 0.0), axis=-1, keepdims=True)
        sr = jnp.sum(jnp.where(left, 0.0, vp), axis=-1, keepdims=True)
        outs.append(jnp.where(left, sl, sr))
    return outs[0] if len(outs) == 1 else jnp.concatenate(outs, axis=1)


def _seg_rstd(x):
    return lax.rsqrt(_seg_sum(x * x) * (1.0 / HEAD_DIM) + RMS_EPS)


def _rms_fwd(x, g, name):
    rows, d = x.shape
    tm = 256

    def body(x_ref, g_ref, h_ref):
        xv = x_ref[...]
        r = lax.rsqrt(jnp.mean(xv * xv, axis=-1, keepdims=True) + RMS_EPS)
        h_ref[...] = ((xv * r) * g_ref[...]).astype(BF16)

    return pl.pallas_call(
        body, out_shape=jax.ShapeDtypeStruct((rows, d), BF16), grid=(rows // tm,),
        in_specs=[pl.BlockSpec((tm, d), lambda i: (i, 0)), pl.BlockSpec((1, d), lambda i: (0, 0))],
        out_specs=pl.BlockSpec((tm, d), lambda i: (i, 0)), name=name,
        compiler_params=_params("parallel"))(x, g)


def _rms_bwd(x, g, dh, dres, name):
    rows, d = x.shape
    tm = 256

    def body(x_ref, g_ref, dh_ref, dres_ref, dx_ref, dxb_ref, dg_ref):
        xv = x_ref[...]
        r = lax.rsqrt(jnp.mean(xv * xv, axis=-1, keepdims=True) + RMS_EPS)
        xh = xv * r
        dhv = dh_ref[...]
        dxh = dhv * g_ref[...]
        dxv = dres_ref[...] + r * (dxh - xh * jnp.mean(dxh * xh, axis=-1, keepdims=True))
        dx_ref[...] = dxv
        dxb_ref[...] = dxv.astype(BF16)
        part = jnp.sum(dhv * xh, axis=0, keepdims=True)

        @pl.when(pl.program_id(0) == 0)
        def _():
            dg_ref[...] = part

        @pl.when(pl.program_id(0) > 0)
        def _():
            dg_ref[...] += part

    row = pl.BlockSpec((tm, d), lambda i: (i, 0))
    vec = pl.BlockSpec((1, d), lambda i: (0, 0))
    return pl.pallas_call(
        body, out_shape=[jax.ShapeDtypeStruct((rows, d), F32), jax.ShapeDtypeStruct((rows, d), BF16),
                         jax.ShapeDtypeStruct((1, d), F32)],
        grid=(rows // tm,), in_specs=[row, vec, row, row], out_specs=[row, row, vec],
        name=name, compiler_params=_params("arbitrary"))(x, g, dh, dres)


def _loss_grad(y, t):
    rows, d = y.shape
    tm = 256

    def body(y_ref, t_ref, dy_ref, l_ref):
        e = y_ref[...] - t_ref[...]
        dy_ref[...] = e * (1.0 / d)
        part = jnp.zeros((1, LANES), F32) + jnp.sum(e * e) * (0.5 / d)

        @pl.when(pl.program_id(0) == 0)
        def _():
            l_ref[...] = part

        @pl.when(pl.program_id(0) > 0)
        def _():
            l_ref[...] += part

    row = pl.BlockSpec((tm, d), lambda i: (i, 0))
    return pl.pallas_call(
        body, out_shape=[jax.ShapeDtypeStruct((rows, d), F32), jax.ShapeDtypeStruct((1, LANES), F32)],
        grid=(rows // tm,), in_specs=[row, row], out_specs=[row, pl.BlockSpec((1, LANES), lambda i: (0, 0))],
        name="loss_grad", compiler_params=_params("arbitrary"))(y, t)


def _put_pairs(ref, val):
    for hp in range(N_PAIRS):
        ref[hp] = val[:, hp * LANES:(hp + 1) * LANES].astype(ref.dtype)


def _get_pairs(ref):
    return jnp.concatenate([ref[hp] for hp in range(N_PAIRS)], axis=1)


def _swap_halves(v):
    return pltpu.roll(v, HEAD_DIM, axis=1)


def _expand_kv(kv):
    left = lax.broadcasted_iota(jnp.int32, kv.shape, 1) < HEAD_DIM
    sw = _swap_halves(kv)
    h0 = jnp.where(left, kv, sw)
    h1 = jnp.where(left, sw, kv)
    return jnp.concatenate([h0, h0, h1, h1], axis=1)


def _reduce_kv(dkv):
    left = lax.broadcasted_iota(jnp.int32, (dkv.shape[0], LANES), 1) < HEAD_DIM
    t = dkv[:, 0:LANES] + dkv[:, LANES:2 * LANES]
    u = dkv[:, 2 * LANES:3 * LANES] + dkv[:, 3 * LANES:4 * LANES]
    t = t + _swap_halves(t)
    u = u + _swap_halves(u)
    return jnp.where(left, t, u)


def _qknorm_fwd(proj, gqa, gka, gqb, gkb):
    rows = proj.shape[0]
    tm = 256

    def body(qa_ref, ka_ref, va_ref, qb_ref, kb_ref, vb_ref, gqa_ref, gka_ref, gqb_ref, gkb_ref,
             oqa, oka, ova, oqb, okb, ovb):
        for src, g_ref, dst in ((qa_ref, gqa_ref, oqa), (ka_ref, gka_ref, oka), (qb_ref, gqb_ref, oqb)):
            xv = src[...]
            _put_pairs(dst, (xv * _seg_rstd(xv)) * g_ref[...])
        _put_pairs(ova, va_ref[...])
        kv = kb_ref[...]
        _put_pairs(okb, _expand_kv((kv * _seg_rstd(kv)) * gkb_ref[...]))
        _put_pairs(ovb, _expand_kv(vb_ref[...]))

    def win(width, off):
        return pl.BlockSpec((tm, width), lambda i: (i, off // width))

    vec = lambda w: pl.BlockSpec((1, w), lambda i: (0, 0))
    out = pl.BlockSpec((N_PAIRS, tm, LANES), lambda i: (0, i, 0))
    return pl.pallas_call(
        body, out_shape=[jax.ShapeDtypeStruct((N_PAIRS, rows, LANES), F32)] * 6, grid=(rows // tm,),
        in_specs=[win(WIDTH, OFF_QA), win(WIDTH, OFF_KA), win(WIDTH, OFF_VA), win(WIDTH, OFF_QB),
                  win(LANES, OFF_KB), win(LANES, OFF_VB), vec(WIDTH), vec(WIDTH), vec(WIDTH), vec(LANES)],
        out_specs=[out] * 6, name="qknorm_fwd", compiler_params=_params("parallel"))(
            proj, proj, proj, proj, proj, proj, gqa, gka, gqb, gkb)


def _norm_bwd(xv, g, dy):
    r = _seg_rstd(xv)
    xh = xv * r
    dxh = dy * g
    dx = r * (dxh - xh * (_seg_sum(dxh * xh) * (1.0 / HEAD_DIM)))
    return dx, jnp.sum(dy * xh, axis=0, keepdims=True)


def _qknorm_bwd(proj, gqa, gka, gqb, gkb, dqa, dka, dva, dqb, dkb, dvb, dga, dgb):
    rows = proj.shape[0]
    tm = 256
    n_a = len(dqa)

    def body(*refs):
        qa_ref, ka_ref, qb_ref, kb_ref, gqa_ref, gka_ref, gqb_ref, gkb_ref = refs[:8]
        pos = 8
        dqa_refs, dka_refs, dva_refs = refs[pos:pos + n_a], refs[pos + n_a:pos + 2 * n_a], refs[pos + 2 * n_a:pos + 3 * n_a]
        pos += 3 * n_a
        dqb_ref, dkb_ref, dvb_ref, dga_ref, dgb_ref = refs[pos:pos + 5]
        dproj_ref, ogqa, ogka, ogqb, ogkb = refs[pos + 5:]

        def total(rs):
            acc = _get_pairs(rs[0])
            for r in rs[1:]:
                acc = acc + _get_pairs(r)
            return acc

        dx_qa, p_qa = _norm_bwd(qa_ref[...], gqa_ref[...], total(dqa_refs))
        dx_ka, p_ka = _norm_bwd(ka_ref[...], gka_ref[...], total(dka_refs))
        dx_qb, p_qb = _norm_bwd(qb_ref[...], gqb_ref[...], _get_pairs(dqb_ref))
        dx_kb, p_kb = _norm_bwd(kb_ref[...], gkb_ref[...], _reduce_kv(_get_pairs(dkb_ref)))
        dproj_ref[:, OFF_QA:OFF_QA + WIDTH] = dx_qa.astype(BF16)
        dproj_ref[:, OFF_KA:OFF_KA + WIDTH] = dx_ka.astype(BF16)
        dproj_ref[:, OFF_VA:OFF_VA + WIDTH] = total(dva_refs).astype(BF16)
        dproj_ref[:, OFF_QB:OFF_QB + WIDTH] = dx_qb.astype(BF16)
        dproj_ref[:, OFF_KB:OFF_KB + LANES] = dx_kb.astype(BF16)
        dproj_ref[:, OFF_VB:OFF_VB + LANES] = _reduce_kv(_get_pairs(dvb_ref)).astype(BF16)
        dproj_ref[:, OFF_GA:OFF_GB] = dga_ref[...]
        dproj_ref[:, OFF_GB:D_IN] = dgb_ref[...]
        first = pl.program_id(0) == 0
        for o_ref, part in ((ogqa, p_qa), (ogka, p_ka), (ogqb, p_qb), (ogkb, p_kb)):
            @pl.when(first)
            def _(o_ref=o_ref, part=part):
                o_ref[...] = part

            @pl.when(jnp.logical_not(first))
            def _(o_ref=o_ref, part=part):
                o_ref[...] += part

    def win(width, off):
        return pl.BlockSpec((tm, width), lambda i: (i, off // width))

    vec = lambda w: pl.BlockSpec((1, w), lambda i: (0, 0))
    row = lambda w: pl.BlockSpec((tm, w), lambda i: (i, 0))
    in_specs = [win(WIDTH, OFF_QA), win(WIDTH, OFF_KA), win(WIDTH, OFF_QB), win(LANES, OFF_KB),
                vec(WIDTH), vec(WIDTH), vec(WIDTH), vec(LANES)]
    in_specs += [pl.BlockSpec((N_PAIRS, tm, LANES), lambda i: (0, i, 0))] * (3 * n_a + 3) + [row(D_MODEL)] * 2
    return pl.pallas_call(
        body,
        out_shape=[jax.ShapeDtypeStruct((rows, D_IN), BF16), jax.ShapeDtypeStruct((1, WIDTH), F32),
                   jax.ShapeDtypeStruct((1, WIDTH), F32), jax.ShapeDtypeStruct((1, WIDTH), F32),
                   jax.ShapeDtypeStruct((1, LANES), F32)],
        grid=(rows // tm,), in_specs=in_specs,
        out_specs=[row(D_IN), vec(WIDTH), vec(WIDTH), vec(WIDTH), vec(LANES)],
        name="qknorm_bwd", compiler_params=_params("arbitrary"))(
            proj, proj, proj, proj, gqa, gka, gqb, gkb, *dqa, *dka, *dva, dqb, dkb, dvb, dga, dgb)


def _t5_bucket(rel):
    half_b = NUM_BUCKETS // 2
    max_exact = half_b // 2
    sign = jnp.where(rel > 0, half_b, 0)
    n = jnp.abs(rel)
    nf = jnp.maximum(n, 1).astype(F32)
    large = max_exact + (jnp.log(nf / max_exact) / math.log(MAX_DISTANCE / max_exact)
                         * (half_b - max_exact)).astype(jnp.int32)
    large = jnp.minimum(large, half_b - 1)
    return sign + jnp.where(n < max_exact, n, large)


def _band_buckets(blk, dilation):
    i = jnp.arange(blk, dtype=jnp.int32)[:, None]
    j = jnp.arange(3 * blk, dtype=jnp.int32)[None, :]
    rel = j - blk - i
    return jnp.where(jnp.abs(rel) <= blk, _t5_bucket(rel * dilation), -1)


def _bias_tiles(table, buckets, head_off, name):
    blk = buckets.shape[0]

    def body(tab_ref, bk_ref, o_ref):
        h = pl.program_id(0) + head_off
        bk = bk_ref[...]
        acc = jnp.full(bk.shape, NEG_INF, F32)
        for b in range(NUM_BUCKETS):
            acc = jnp.where(bk == b, tab_ref[b, h], acc)
        o_ref[0] = acc

    return pl.pallas_call(
        body, out_shape=jax.ShapeDtypeStruct((N_HEADS, blk, 3 * blk), F32), grid=(N_HEADS,),
        in_specs=[pl.BlockSpec(memory_space=pltpu.SMEM), pl.BlockSpec((blk, 3 * blk), lambda h: (0, 0))],
        out_specs=pl.BlockSpec((1, blk, 3 * blk), lambda h: (h, 0, 0)),
        name=name, compiler_params=_params("parallel"))(table, buckets)


def _table_grad(dbias, buckets, name):
    blk = buckets.shape[0]

    def body(db_ref, bk_ref, o_ref):
        bk = bk_ref[...]
        dbv = db_ref[0]
        lane = lax.broadcasted_iota(jnp.int32, (1, LANES), 1)
        acc = jnp.zeros((1, LANES), F32)
        for b in range(NUM_BUCKETS):
            acc = jnp.where(lane == b, jnp.sum(jnp.where(bk == b, dbv, 0.0)), acc)
        o_ref[0] = acc

    out = pl.pallas_call(
        body, out_shape=jax.ShapeDtypeStruct((N_HEADS, 1, LANES), F32), grid=(N_HEADS,),
        in_specs=[pl.BlockSpec((1, blk, 3 * blk), lambda h: (h, 0, 0)), pl.BlockSpec((blk, 3 * blk), lambda h: (0, 0))],
        out_specs=pl.BlockSpec((1, 1, LANES), lambda h: (h, 0, 0)),
        name=name, compiler_params=_params("parallel"))(dbias, buckets)
    return out[:, 0, :NUM_BUCKETS]


def _dot_nt(a, b):
    return lax.dot_general(a, b, (((1,), (1,)), ((), ())), preferred_element_type=F32)


def _dot_tn(a, b):
    return lax.dot_general(a, b, (((0,), (0,)), ((), ())), preferred_element_type=F32)


def _stack_pair(x2, left):
    return jnp.concatenate([jnp.where(left, x2, 0.0), jnp.where(left, 0.0, x2)], axis=0).astype(BF16)


def _attn_geometry(blk, d):
    chunk = blk * ITEMS if d == 1 else blk * d
    groups = 1 if d == 1 else d // ITEMS
    halo = blk if d == 1 else chunk
    return chunk, groups, halo


def _item_rows(ref, j, r0, blk, d):
    if d == 1:
        return ref[j * blk:(j + 1) * blk, :]
    return ref[pl.ds(r0 + j, blk, stride=d), :]


def _item_penalty(t, nct, j, blk, d):
    first_ok, last_ok = t > 0, t < nct - 1
    if d == 1:
        first_ok = True if j > 0 else first_ok
        last_ok = True if j < ITEMS - 1 else last_ok
    col = lax.broadcasted_iota(jnp.int32, (1, 3 * blk), 1)
    ok = jnp.logical_and(jnp.logical_or(col >= blk, first_ok), jnp.logical_or(col < 2 * blk, last_ok))
    return jnp.where(ok, 0.0, NEG_INF).astype(F32)


def _attn_specs(seq, blk, d, step_of):
    chunk, _, halo = _attn_geometry(blk, d)
    per, last = chunk // halo, seq // halo - 1
    cur = pl.BlockSpec((None, chunk, LANES), lambda hp, t: (hp, step_of(t), 0))
    prev = pl.BlockSpec((None, halo, LANES), lambda hp, t: (hp, jnp.clip(step_of(t) * per - 1, 0, last), 0))
    nxt = pl.BlockSpec((None, halo, LANES), lambda hp, t: (hp, jnp.minimum((step_of(t) + 1) * per, last), 0))
    return cur, prev, nxt


def _attn_fwd(q, k, v, bias, sink, blk, d, name):
    _, seq, _ = q.shape
    chunk, groups, _ = _attn_geometry(blk, d)
    nct = seq // chunk
    has_sink = sink is not None
    scale = HEAD_DIM ** -0.5

    def body(*refs):
        q_ref, kp, kc, kn, vp, vc, vn, b_ref = refs[:8]
        s_ref = refs[8] if has_sink else None
        o_ref, l_ref = refs[-2], refs[-1]
        t = pl.program_id(1)
        left = lax.broadcasted_iota(jnp.int32, (1, LANES), 1) < HEAD_DIM
        bias2 = b_ref[...]
        if d == 1:
            kwin = jnp.concatenate([kp[...], kc[...], kn[...]], axis=0).astype(BF16)
            vwin = jnp.concatenate([vp[...], vc[...], vn[...]], axis=0).astype(BF16)

        def group(r0):
            scores, vcats = [], []
            for j in range(ITEMS):
                qs = _stack_pair(_item_rows(q_ref, j, r0, blk, d) * scale, left)
                if d == 1:
                    kcat, vcat = kwin[j * blk:(j + 3) * blk], vwin[j * blk:(j + 3) * blk]
                else:
                    kcat = jnp.concatenate([_item_rows(r, j, r0, blk, d) for r in (kp, kc, kn)], axis=0).astype(BF16)
                    vcat = jnp.concatenate([_item_rows(r, j, r0, blk, d) for r in (vp, vc, vn)], axis=0).astype(BF16)
                scores.append(_dot_nt(qs, kcat) + bias2 + _item_penalty(t, nct, j, blk, d))
                vcats.append(vcat)
            ms = [jnp.max(s, axis=-1, keepdims=True) for s in scores]
            if has_sink:
                sk = s_ref[...]
                ms = [jnp.maximum(m, sk) for m in ms]
            ps = [jnp.exp(s - m) for s, m in zip(scores, ms)]
            dens = [jnp.sum(p, axis=-1, keepdims=True) for p in ps]
            if has_sink:
                dens = [den + jnp.exp(sk - m) for den, m in zip(dens, ms)]
            pns = [(p * (1.0 / den)).astype(BF16) for p, den in zip(ps, dens)]
            lses = [m + jnp.log(den) for m, den in zip(ms, dens)]
            for j in range(ITEMS):
                o2 = jnp.dot(pns[j], vcats[j], preferred_element_type=F32)
                o_val = jnp.where(left, o2[:blk], o2[blk:])
                l_val = jnp.where(left, lses[j][:blk], lses[j][blk:])
                if d == 1:
                    o_ref[j * blk:(j + 1) * blk, :] = o_val
                    l_ref[j * blk:(j + 1) * blk, :] = l_val
                else:
                    o_ref[pl.ds(r0 + j, blk, stride=d), :] = o_val
                    l_ref[pl.ds(r0 + j, blk, stride=d), :] = l_val

        if groups == 1:
            group(0)
        else:
            def step(g, carry):
                group(g * ITEMS)
                return carry

            lax.fori_loop(0, groups, step, 0)

    cur, prev, nxt = _attn_specs(seq, blk, d, lambda t: t)
    in_specs = [cur, prev, cur, nxt, prev, cur, nxt, pl.BlockSpec((2 * blk, 3 * blk), lambda hp, t: (hp, 0))]
    args = [q, k, k, k, v, v, v, bias]
    if has_sink:
        in_specs.append(pl.BlockSpec((2 * blk, 1), lambda hp, t: (hp, 0)))
        args.append(sink)
    return pl.pallas_call(
        body, out_shape=[jax.ShapeDtypeStruct(q.shape, F32)] * 2, grid=(N_PAIRS, nct),
        in_specs=in_specs, out_specs=[cur, cur], name=name,
        compiler_params=_params("parallel", "parallel"))(*args)


def _attn_bwd(q, k, v, do, lse, delta, bias, sink, blk, d, name):
    _, seq, _ = q.shape
    chunk, groups, halo = _attn_geometry(blk, d)
    nct = seq // chunk
    has_sink = sink is not None
    n_in = 12 if has_sink else 11
    scale = HEAD_DIM ** -0.5

    def body(*refs):
        q_ref, kp, kc, kn, vp, vc, vn, do_ref, l_ref, d_ref, b_ref = refs[:11]
        s_ref = refs[11] if has_sink else None
        dq_ref, dk_ref, dv_ref, db_ref = refs[n_in:n_in + 4]
        ds_ref = refs[n_in + 4] if has_sink else None
        wk, wv = refs[-2], refs[-1]
        t = pl.program_id(1)

        @pl.when(t == 0)
        def _():
            wk[...] = jnp.zeros_like(wk)
            wv[...] = jnp.zeros_like(wv)
            db_ref[...] = jnp.zeros_like(db_ref)
            if has_sink:
                ds_ref[...] = jnp.zeros_like(ds_ref)

        @pl.when(t > 0)
        def _():
            for w in (wk, wv):
                keep = w[chunk:2 * chunk + halo]
                w[0:chunk + halo] = keep
                w[chunk + halo:2 * chunk + halo] = jnp.zeros((chunk, LANES), F32)

        @pl.when(t < nct)
        def _():
            lane = lax.broadcasted_iota(jnp.int32, (1, LANES), 1)
            left = lane < HEAD_DIM
            bias2 = b_ref[...]
            if d == 1:
                kwin = jnp.concatenate([kp[...], kc[...], kn[...]], axis=0).astype(BF16)
                vwin = jnp.concatenate([vp[...], vc[...], vn[...]], axis=0).astype(BF16)

            def group(r0):
                qss, doss, kcats, scores, dps, lcols, dcols = [], [], [], [], [], [], []
                for j in range(ITEMS):
                    qs = _stack_pair(_item_rows(q_ref, j, r0, blk, d) * scale, left)
                    dos = _stack_pair(_item_rows(do_ref, j, r0, blk, d), left)
                    if d == 1:
                        kcat, vcat = kwin[j * blk:(j + 3) * blk], vwin[j * blk:(j + 3) * blk]
                    else:
                        kcat = jnp.concatenate([_item_rows(r, j, r0, blk, d) for r in (kp, kc, kn)], axis=0).astype(BF16)
                        vcat = jnp.concatenate([_item_rows(r, j, r0, blk, d) for r in (vp, vc, vn)], axis=0).astype(BF16)
                    l2, d2 = _item_rows(l_ref, j, r0, blk, d), _item_rows(d_ref, j, r0, blk, d)
                    lcols.append(jnp.concatenate([jnp.max(jnp.where(left, l2, NEG_INF), axis=-1, keepdims=True),
                                                  jnp.max(jnp.where(left, NEG_INF, l2), axis=-1, keepdims=True)], axis=0))
                    dcols.append(jnp.concatenate([jnp.sum(jnp.where(lane == 0, d2, 0.0), axis=-1, keepdims=True),
                                                  jnp.sum(jnp.where(lane == HEAD_DIM, d2, 0.0), axis=-1, keepdims=True)],
                                                 axis=0))
                    scores.append(_dot_nt(qs, kcat) + bias2 + _item_penalty(t, nct, j, blk, d))
                    dps.append(_dot_nt(dos, vcat))
                    qss.append(qs)
                    doss.append(dos)
                    kcats.append(kcat)
                ps = [jnp.exp(s - lc) for s, lc in zip(scores, lcols)]
                dss = [p * (dp - dc) for p, dp, dc in zip(ps, dps, dcols)]
                db_ref[...] += functools.reduce(lambda a, b: a + b, dss)
                if has_sink:
                    sk = s_ref[...]
                    ds_ref[...] -= functools.reduce(lambda a, b: a + b, [dc * jnp.exp(sk - lc) for dc, lc in zip(dcols, lcols)])
                dsbs = [ds.astype(BF16) for ds in dss]
                for j in range(ITEMS):
                    dq2 = jnp.dot(dsbs[j], kcats[j], preferred_element_type=F32) * scale
                    dq_val = jnp.where(left, dq2[:blk], dq2[blk:])
                    if d == 1:
                        dq_ref[j * blk:(j + 1) * blk, :] = dq_val
                    else:
                        dq_ref[pl.ds(r0 + j, blk, stride=d), :] = dq_val
                news = [(_dot_tn(dsbs[j], qss[j]), _dot_tn(ps[j].astype(BF16), doss[j])) for j in range(ITEMS)]
                if d == 1:
                    for which, w in enumerate((wk, wv)):
                        for b in range(ITEMS + 2):
                            parts = [news[j][which][(b - j) * blk:(b - j + 1) * blk] for j in range(ITEMS) if 0 <= b - j < 3]
                            w[chunk + (b - 1) * blk:chunk + b * blk, :] += functools.reduce(lambda x, y: x + y, parts)
                else:
                    for j in range(ITEMS):
                        for which, w in enumerate((wk, wv)):
                            for c in range(3):
                                w[pl.ds(c * chunk + r0 + j, blk, stride=d), :] += news[j][which][c * blk:(c + 1) * blk]

            if groups == 1:
                group(0)
            else:
                def step(g, carry):
                    group(g * ITEMS)
                    return carry

                lax.fori_loop(0, groups, step, 0)

        dk_ref[...] = wk[0:chunk]
        dv_ref[...] = wv[0:chunk]

    cur, prev, nxt = _attn_specs(seq, blk, d, lambda t: jnp.minimum(t, nct - 1))
    lag = pl.BlockSpec((None, chunk, LANES), lambda hp, t: (hp, jnp.maximum(t - 1, 0), 0))
    band = pl.BlockSpec((2 * blk, 3 * blk), lambda hp, t: (hp, 0))
    col = pl.BlockSpec((2 * blk, 1), lambda hp, t: (hp, 0))
    in_specs = [cur, prev, cur, nxt, prev, cur, nxt, cur, cur, cur, band]
    args = [q, k, k, k, v, v, v, do, lse, delta, bias]
    out_shape = [jax.ShapeDtypeStruct(q.shape, F32)] * 3 + [jax.ShapeDtypeStruct((N_HEADS * blk, 3 * blk), F32)]
    out_specs = [cur, lag, lag, band]
    if has_sink:
        in_specs.append(col)
        args.append(sink)
        out_shape.append(jax.ShapeDtypeStruct((N_HEADS * blk, 1), F32))
        out_specs.append(col)
    window = pltpu.VMEM((2 * chunk + halo, LANES), F32)
    return pl.pallas_call(
        body, out_shape=out_shape, grid=(N_PAIRS, nct + 1), in_specs=in_specs, out_specs=out_specs,
        scratch_shapes=[window, window], name=name,
        compiler_params=_params("arbitrary", "arbitrary"))(*args)


def _combine_patterns(outs, lses):
    _, rows, _ = outs[0].shape
    tm = 256
    n = len(outs)

    def body(*refs):
        o_refs, l_refs = refs[:n], refs[n:2 * n]
        y_ref, lse_ref = refs[2 * n], refs[2 * n + 1]
        for hp in range(N_PAIRS):
            ls = [r[hp] for r in l_refs]
            m = functools.reduce(jnp.maximum, ls)
            es = [jnp.exp(l - m) for l in ls]
            den = functools.reduce(lambda a, b: a + b, es)
            num = functools.reduce(lambda a, b: a + b, [e * r[hp] for e, r in zip(es, o_refs)])
            y_ref[:, hp * LANES:(hp + 1) * LANES] = num / den
            lse_ref[hp] = m + jnp.log(den)

    pm = pl.BlockSpec((N_PAIRS, tm, LANES), lambda i: (0, i, 0))
    return pl.pallas_call(
        body, out_shape=[jax.ShapeDtypeStruct((rows, WIDTH), F32), jax.ShapeDtypeStruct((N_PAIRS, rows, LANES), F32)],
        grid=(rows // tm,), in_specs=[pm] * (2 * n), out_specs=[pl.BlockSpec((tm, WIDTH), lambda i: (i, 0)), pm],
        name="combine_a", compiler_params=_params("parallel"))(*outs, *lses)


def _pairs_to_tokens(a):
    _, rows, _ = a.shape
    tm = 256

    def body(a_ref, o_ref):
        o_ref[...] = _get_pairs(a_ref)

    return pl.pallas_call(
        body, out_shape=jax.ShapeDtypeStruct((rows, WIDTH), a.dtype), grid=(rows // tm,),
        in_specs=[pl.BlockSpec((N_PAIRS, tm, LANES), lambda i: (0, i, 0))],
        out_specs=pl.BlockSpec((tm, WIDTH), lambda i: (i, 0)), name="pairs_to_tokens",
        compiler_params=_params("parallel"))(a)


def _attn_bwd_prep(dy, y, name):
    rows = dy.shape[0]
    tm = 256

    def body(dy_ref, y_ref, do_ref, dl_ref):
        dyv = dy_ref[...]
        _put_pairs(do_ref, dyv)
        _put_pairs(dl_ref, _seg_sum(dyv * y_ref[...]))

    tok = pl.BlockSpec((tm, WIDTH), lambda i: (i, 0))
    pm = pl.BlockSpec((N_PAIRS, tm, LANES), lambda i: (0, i, 0))
    return pl.pallas_call(
        body, out_shape=[jax.ShapeDtypeStruct((N_PAIRS, rows, LANES), F32)] * 2, grid=(rows // tm,),
        in_specs=[tok, tok], out_specs=[pm, pm], name=name, compiler_params=_params("parallel"))(dy, y)


def _tile_gain(g, reps):
    return jnp.tile(g[None, :], (1, reps))


def _local_step(x, p, target, w_in_of, rest_of, small):
    rel_table = small["rel_table"]
    buckets_a = [_band_buckets(blk, d) for blk, d in DILATED]
    buckets_b = _band_buckets(BLK_B, 1)
    bias_a = [_bias_tiles(rel_table, bk, 0, "bias_a").reshape(N_HEADS * bk.shape[0], -1) for bk in buckets_a]
    bias_b = _bias_tiles(rel_table, buckets_b, N_HEADS, "bias_b").reshape(N_HEADS * BLK_B, -1)

    saved = []
    for l in range(DEPTH):
        g_mix, g_ffn, g_ple = (small[n][l][None, :] for n in ("norm_mix_g", "norm_ffn_g", "norm_ple_g"))
        gqa, gka, gqb = (_tile_gain(small[n][l], N_HEADS) for n in ("qnorm_a_g", "knorm_a_g", "qnorm_b_g"))
        gkb = _tile_gain(small["knorm_b_g"][l], N_KV_B)
        sink = jnp.repeat(small["sink_b"][l], BLK_B)[:, None]

        h = _rms_fwd(x, g_mix, "rms_mix")
        w_in = w_in_of(l, h)
        proj = _mm(h, w_in, "nt", F32, "mm_in")
        qa, ka, va, qb, kb, vb = _qknorm_fwd(proj, gqa, gka, gqb, gkb)
        outs, lses = [], []
        for (blk, d), bias in zip(DILATED, bias_a):
            o, ls = _attn_fwd(qa, ka, va, bias, None, blk, d, f"attn_a{d}_fwd")
            outs.append(o)
            lses.append(ls)
        ya, lse_a = _combine_patterns(outs, lses)
        yb, lse_b = _attn_fwd(qb, kb, vb, bias_b, sink, BLK_B, 1, "attn_b_fwd")
        yb = _pairs_to_tokens(yb)
        w = dict(rest_of(l, yb), w_in=w_in)
        def gate(products, extra):
            (ca_, cb_), (ga_, gb_) = products, extra
            return _sigmoid(ga_) * ca_ + _sigmoid(gb_) * cb_, ca_, cb_

        merged, ca, cb = _mm_fused([(ya, w["w_branch_a"], "nn"), (yb, w["w_branch_b"], "nn")],
                                   [(proj, OFF_GA), (proj, OFF_GB)], gate, [BF16, BF16, BF16], "mm_branches_gate")
        x1 = _mm(merged, w["w_out"], "nn", F32, "mm_out", res=x)

        h2 = _rms_fwd(x1, g_ffn, "rms_ffn")

        def swiglu(products, extra):
            a_, u_ = products
            return (a_ * _sigmoid(a_)) * u_, a_, u_

        hid, a, u = _mm_fused([(h2, w["w_ffn_gate"], "nt"), (h2, w["w_ffn_up"], "nt")], [], swiglu,
                              [BF16, BF16, BF16], "mm_ffn_gate_up")
        x2 = _mm(hid, w["w_ffn_down"], "nn", F32, "mm_ffn_down", res=x1)

        h3 = _rms_fwd(x2, g_ple, "rms_ple")

        def ple(products, extra):
            z_, e_ = products
            return extra[0] + _sigmoid(z_) * e_, z_, e_

        x3, z, e = _mm_fused([(h3, w["w_ple_gate"], "nn"), (p[l], w["w_ple_proj"], "nn")], [(x2, 0)], ple,
                             [F32, BF16, BF16], "mm_ple")
        saved.append(dict(w=w, x0=x, h=h, proj=proj, qa=qa, ka=ka, va=va, qb=qb, kb=kb, vb=vb, ya=ya, lse_a=lse_a,
                          yb=yb, lse_b=lse_b, ca=ca, cb=cb, merged=merged, x1=x1, h2=h2, a=a, u=u, hid=hid,
                          x2=x2, h3=h3, z=z, e=e))
        x = x3

    dx, loss_acc = _loss_grad(x, target)
    loss = loss_acc[0, 0]

    gbig = [{} for _ in range(DEPTH)]
    marks = [{} for _ in range(DEPTH)]
    gsmall = {n: [None] * DEPTH for n in SMALL if n != "rel_table"}
    dbias_a = [[] for _ in DILATED]
    dbias_b = []

    for l in reversed(range(DEPTH)):
        sv = saved[l]
        w = sv["w"]
        g_mix, g_ffn, g_ple = (small[n][l][None, :] for n in ("norm_mix_g", "norm_ffn_g", "norm_ple_g"))
        gqa, gka, gqb = (_tile_gain(small[n][l], N_HEADS) for n in ("qnorm_a_g", "knorm_a_g", "qnorm_b_g"))
        gkb = _tile_gain(small["knorm_b_g"][l], N_KV_B)
        sink = jnp.repeat(small["sink_b"][l], BLK_B)[:, None]

        def ple_bwd(dx_, z_, e_):
            s = _sigmoid(z_.astype(F32))
            return dx_ * s, dx_ * e_.astype(F32) * (s * (1.0 - s))

        de, dz = _ew(ple_bwd, [(dx, 0), (sv["z"], 0), (sv["e"], 0)], [BF16, BF16], width=D_MODEL, bw=D_MODEL,
                     name="ple_bwd")
        gbig[l]["w_ple_proj"] = _mm(p[l], de, "tn", BF16, "mm_d_ple_proj")
        gbig[l]["w_ple_gate"] = _mm(sv["h3"], dz, "tn", BF16, "mm_d_ple_gate")
        dh3 = _mm(dz, w["w_ple_gate"], "nt", F32, "mm_dh3")
        dx, dxb, gsmall["norm_ple_g"][l] = _rms_bwd(sv["x2"], g_ple, dh3, dx, "rms_ple_bwd")

        gbig[l]["w_ffn_down"] = _mm(sv["hid"], dxb, "tn", BF16, "mm_d_ffn_down")

        def swiglu_bwd(products, extra):
            dh_, a_, u_ = products[0], extra[0].astype(F32), extra[1].astype(F32)
            s = _sigmoid(a_)
            return dh_ * u_ * (s * (1.0 + a_ * (1.0 - s))), dh_ * (a_ * s)

        da, du = _mm_fused([(dxb, w["w_ffn_down"], "nt")], [(sv["a"], 0), (sv["u"], 0)], swiglu_bwd, [BF16, BF16],
                           "mm_dhid_swiglu_bwd")
        gbig[l]["w_ffn_gate"] = _mm(da, sv["h2"], "tn", BF16, "mm_d_ffn_gate")
        gbig[l]["w_ffn_up"] = _mm(du, sv["h2"], "tn", BF16, "mm_d_ffn_up")
        dh2 = _mm(da, w["w_ffn_gate"], "nn", F32, "mm_dh2_gate")
        dh2 = _mm(du, w["w_ffn_up"], "nn", F32, "mm_dh2_up", res=dh2)
        dx, dxb, gsmall["norm_ffn_g"][l] = _rms_bwd(sv["x1"], g_ffn, dh2, dx, "rms_ffn_bwd")

        gbig[l]["w_out"] = _mm(sv["merged"], dxb, "tn", BF16, "mm_d_out")

        def gate_bwd(products, extra):
            dm_, ca_, cb_ = products[0], extra[0].astype(F32), extra[1].astype(F32)
            sa, sb = _sigmoid(extra[2]), _sigmoid(extra[3])
            return dm_ * sa, dm_ * sb, dm_ * ca_ * (sa * (1.0 - sa)), dm_ * cb_ * (sb * (1.0 - sb))

        dca, dcb, dga, dgb = _mm_fused(
            [(dxb, w["w_out"], "nt")], [(sv["ca"], 0), (sv["cb"], 0), (sv["proj"], OFF_GA), (sv["proj"], OFF_GB)],
            gate_bwd, [BF16, BF16, BF16, BF16], "mm_dmerged_gate_bwd")
        gbig[l]["w_branch_a"] = _mm(sv["ya"], dca, "tn", BF16, "mm_d_branch_a")
        gbig[l]["w_branch_b"] = _mm(sv["yb"], dcb, "tn", BF16, "mm_d_branch_b")
        dya = _mm(dca, w["w_branch_a"], "nt", F32, "mm_dya")
        dyb = _mm(dcb, w["w_branch_b"], "nt", F32, "mm_dyb")

        dya, delta_a = _attn_bwd_prep(dya, sv["ya"], "attn_a_bwd_prep")
        dyb, delta_b = _attn_bwd_prep(dyb, sv["yb"], "attn_b_bwd_prep")

        dqa, dka, dva = [], [], []
        for (blk, d), bias, bk in zip(DILATED, bias_a, buckets_a):
            dq_, dk_, dv_, db_ = _attn_bwd(sv["qa"], sv["ka"], sv["va"], dya, sv["lse_a"], delta_a, bias, None, blk, d,
                                           f"attn_a{d}_bwd")
            dqa.append(dq_)
            dka.append(dk_)
            dva.append(dv_)
            dbias_a[len(dqa) - 1].append(db_)
        dqb, dkb, dvb, db_, dsink = _attn_bwd(sv["qb"], sv["kb"], sv["vb"], dyb, sv["lse_b"], delta_b, bias_b, sink,
                                              BLK_B, 1, "attn_b_bwd")
        dbias_b.append(db_)
        gsmall["sink_b"][l] = dsink.reshape(N_HEADS, BLK_B).sum(axis=1)

        dproj, pqa, pka, pqb, pkb = _qknorm_bwd(sv["proj"], gqa, gka, gqb, gkb, dqa, dka, dva, dqb, dkb, dvb, dga, dgb)
        marks[l]["attn_bwd_done"] = dproj
        gsmall["qnorm_a_g"][l] = pqa.reshape(N_HEADS, HEAD_DIM).sum(0)
        gsmall["knorm_a_g"][l] = pka.reshape(N_HEADS, HEAD_DIM).sum(0)
        gsmall["qnorm_b_g"][l] = pqb.reshape(N_HEADS, HEAD_DIM).sum(0)
        gsmall["knorm_b_g"][l] = pkb.reshape(N_KV_B, HEAD_DIM).sum(0)
        gbig[l]["w_in"] = _mm(dproj, sv["h"], "tn", BF16, "mm_d_in")
        dh = _mm(dproj, w["w_in"], "nn", F32, "mm_dh")
        dx, _, gsmall["norm_mix_g"][l] = _rms_bwd(sv["x0"], g_mix, dh, dx, "rms_mix_bwd")
        gsmall["norm_mix_g"][l] = gsmall["norm_mix_g"][l][0]
        gsmall["norm_ffn_g"][l] = gsmall["norm_ffn_g"][l][0]
        gsmall["norm_ple_g"][l] = gsmall["norm_ple_g"][l][0]

    gsmall = {n: jnp.stack(v) for n, v in gsmall.items()}
    dtable_a = sum(_table_grad(sum(dbs).reshape(N_HEADS, blk, 3 * blk), bk, "table_grad_a")
                   for dbs, (blk, _), bk in zip(dbias_a, DILATED, buckets_a))
    dtable_b = _table_grad(sum(dbias_b).reshape(N_HEADS, BLK_B, 3 * BLK_B), buckets_b, "table_grad_b")
    gsmall["rel_table"] = jnp.concatenate([dtable_a, dtable_b], axis=0).T
    return loss, dx, gbig, gsmall, marks


def _place():
    return lax.axis_index("x"), lax.axis_index("y"), lax.axis_index("c")


def _flip(v, bit):
    return 1 - v if bit else v


CHIP_RELATIONS = ((0, 1), (1, 0), (1, 1))
ANY = pl.BlockSpec(memory_space=pl.ANY)


def _allgather_body(w_refs, out_refs, send_sems, recv_sems):
    x, y, c = _place()
    chips = [(_flip(x, a), _flip(y, b)) for a, b in CHIP_RELATIONS]

    def make(g):
        w_ref, out_ref = w_refs[g], out_refs[g]
        half = w_ref.shape[0] // 2

        def part(px, py, pc):
            return out_ref.at[2 * px + py, pl.ds(pc * half, half), :]

        def copy(k, block, to, src=None):
            return pltpu.make_async_remote_copy(
                src_ref=part(*block) if src is None else src, dst_ref=part(*block),
                send_sem=send_sems.at[7 * g + k], recv_sem=recv_sems.at[7 * g + k], device_id=to,
                device_id_type=MESH_ID)

        own = pltpu.make_async_remote_copy(
            src_ref=w_ref, dst_ref=out_ref.at[2 * x + y], send_sem=send_sems.at[7 * g + 6],
            recv_sem=recv_sems.at[7 * g + 6], device_id=(x, y, 1 - c), device_id_type=MESH_ID)
        first = [copy(k, (x, y, c), (*chip, c), src=w_ref.at[pl.ds(c * half, half), :]) for k, chip in enumerate(chips)]
        passed = [copy(3 + k, (*chip, c), (x, y, 1 - c)) for k, chip in enumerate(chips)]
        arrive = [copy(k, (*chip, c), (x, y, c)) for k, chip in enumerate(chips)]
        arrive2 = [copy(3 + k, (*chip, 1 - c), (x, y, c)) for k, chip in enumerate(chips)]
        return own, first, passed, arrive, arrive2

    made = [make(g) for g in range(len(w_refs))]
    for own, first, _, _, _ in made:
        own.start()
        for cp in first:
            cp.start()
    for _, _, passed, arrive, _ in made:
        for k in range(3):
            arrive[k].wait_recv()
            passed[k].start()
    for own, first, passed, _, arrive2 in made:
        for k in range(3):
            arrive2[k].wait_recv()
        own.wait_recv()
        for cp in first + passed + [own]:
            cp.wait_send()


def _sibling(x, y, c):
    return [(x, y, 1 - c)]


def _same_core_of_other_chips(x, y, c):
    return [(_flip(x, a), _flip(y, b), c) for a, b in CHIP_RELATIONS]


def _exchange(body, ins, out_types, n_sems, name, sequencer=None):
    n = len(ins)
    sems = (pltpu.SemaphoreType.DMA((n_sems,)), pltpu.SemaphoreType.DMA((n_sems,)))
    if sequencer is None:
        in_place = out_types is None
        out_shape = [jax.ShapeDtypeStruct(a.shape, a.dtype) for a in ins] if in_place else out_types

        def tc_body(*refs):
            body(refs[:n], refs[n:n + len(out_shape)], refs[-2], refs[-1])

        return list(pl.pallas_call(
            tc_body, out_shape=out_shape, in_specs=[ANY] * n, out_specs=[ANY] * len(out_shape),
            input_output_aliases={g: g for g in range(n)} if in_place else {}, scratch_shapes=list(sems), name=name)(*ins))

    collective_id, peers = sequencer
    hbm = pltpu.MemorySpace.HBM
    in_refs = [jax.new_ref(a, memory_space=hbm) for a in ins]
    out_refs = in_refs if out_types is None else [jax.empty_ref(t, memory_space=hbm) for t in out_types]

    @pl.kernel(mesh=plsc.ScalarSubcoreMesh(axis_name="sequencer", num_cores=1), name=name, scratch_types=sems,
               compiler_params=pltpu.CompilerParams(collective_id=collective_id))
    def launch(send_sems, recv_sems):
        barrier = pltpu.get_barrier_semaphore()
        devices = peers(*_place())
        for device in devices:
            pl.semaphore_signal(barrier, inc=1, device_id=device, device_id_type=MESH_ID)
        pl.semaphore_wait(barrier, len(devices))
        body(in_refs, out_refs, send_sems, recv_sems)

    launch()
    return [r[...] for r in out_refs]


def _allgather(shards, name, sequencer=None):
    out_types = [jax.ShapeDtypeStruct((N_CHIPS,) + s.shape, s.dtype) for s in shards]
    if sequencer is not None:
        sequencer = (sequencer, lambda x, y, c: _sibling(x, y, c) + _same_core_of_other_chips(x, y, c))
    return _exchange(_allgather_body, shards, out_types, 7 * len(shards), name, sequencer)


def _half_tile(half):
    return max(t for t in range(16, 1025, 16) if half % t == 0)


def _run_copies(cps):
    for cp in cps:
        cp.start()
    for cp in cps:
        cp.wait_recv()
    for cp in cps:
        cp.wait_send()


def _sibling_halves(gsends, name, sequencer=None):
    def body(g_refs, out_refs, send_sems, recv_sems):
        x, y, c = _place()
        cps = []
        for g, (g_ref, out_ref) in enumerate(zip(g_refs, out_refs)):
            half = g_ref.shape[1] // 2
            cps.append(pltpu.make_async_remote_copy(
                src_ref=g_ref.at[:, pl.ds((1 - c) * half, half), :], dst_ref=out_ref,
                send_sem=send_sems.at[g], recv_sem=recv_sems.at[g], device_id=(x, y, 1 - c), device_id_type=MESH_ID))
        _run_copies(cps)

    out_types = [jax.ShapeDtypeStruct((s.shape[0], s.shape[1] // 2, s.shape[2]), s.dtype) for s in gsends]
    return _exchange(body, gsends, out_types, len(gsends), name, sequencer and (sequencer, _sibling))


def _chip_sums(gsend, sib, place):
    n, rows, cols = gsend.shape
    half = rows // 2
    tm = _half_tile(half)
    nblk = half // tm

    def body(s_ref, g_ref, sib_ref, o_ref):
        o_ref[0] = (g_ref[0].astype(F32) + sib_ref[0].astype(F32)).astype(o_ref.dtype)

    grid_spec = pltpu.PrefetchScalarGridSpec(
        num_scalar_prefetch=1, grid=(n, nblk),
        in_specs=[pl.BlockSpec((1, tm, cols), lambda k, i, s: (jnp.bitwise_xor(s[0], k), s[1] * nblk + i, 0)),
                  pl.BlockSpec((1, tm, cols), lambda k, i, s: (jnp.bitwise_xor(s[0], k), i, 0))],
        out_specs=pl.BlockSpec((1, tm, cols), lambda k, i, s: (k, i, 0)))
    return pl.pallas_call(
        body, out_shape=jax.ShapeDtypeStruct((n, half, cols), BF16), grid_spec=grid_spec,
        name="rs_chip_sums", compiler_params=_params("parallel", "parallel"))(place, gsend, sib)


def _exchange_chip_sums(tsends, name, sequencer=None):
    def body(t_refs, out_refs, send_sems, recv_sems):
        x, y, c = _place()
        cps = []
        for g, (t_ref, out_ref) in enumerate(zip(t_refs, out_refs)):
            for k, device in enumerate(_same_core_of_other_chips(x, y, c)):
                cps.append(pltpu.make_async_remote_copy(
                    src_ref=t_ref.at[k + 1], dst_ref=out_ref.at[k], send_sem=send_sems.at[3 * g + k],
                    recv_sem=recv_sems.at[3 * g + k], device_id=device, device_id_type=MESH_ID))
        _run_copies(cps)

    out_types = [jax.ShapeDtypeStruct((3,) + s.shape[1:], s.dtype) for s in tsends]
    return _exchange(body, tsends, out_types, 3 * len(tsends), name,
                     sequencer and (sequencer, _same_core_of_other_chips))


def _final_sum(tsend, recv, place):
    n, half, cols = tsend.shape
    tm = _half_tile(half)
    nblk = half // tm

    def body(s_ref, t_ref, r_ref, o_ref):
        o_ref[...] = ((t_ref[0].astype(F32) + r_ref[0].astype(F32)) + r_ref[1].astype(F32)) + r_ref[2].astype(F32)

    grid_spec = pltpu.PrefetchScalarGridSpec(
        num_scalar_prefetch=1, grid=(nblk,),
        in_specs=[pl.BlockSpec((1, tm, cols), lambda i, s: (0, i, 0)), pl.BlockSpec((n - 1, tm, cols), lambda i, s: (0, i, 0))],
        out_specs=pl.BlockSpec((tm, cols), lambda i, s: (s[1] * nblk + i, 0)))
    return pl.pallas_call(
        body, out_shape=jax.ShapeDtypeStruct((2 * half, cols), F32), grid_spec=grid_spec, name="rs_final_sum",
        compiler_params=_params("parallel"))(place, tsend, recv)


def _join_halves(gfulls, name, sequencer=None):
    def body(g_refs, out_refs, send_sems, recv_sems):
        x, y, c = _place()
        n = len(g_refs)

        def copy(g, pc):
            half = g_refs[g].shape[0] // 2
            return pltpu.make_async_remote_copy(
                src_ref=g_refs[g].at[pl.ds(pc * half, half), :], dst_ref=out_refs[g].at[pl.ds(pc * half, half), :],
                send_sem=send_sems.at[g], recv_sem=recv_sems.at[g], device_id=(x, y, 1 - c), device_id_type=MESH_ID)

        mine = [copy(g, c) for g in range(n)]
        for cp in mine:
            cp.start()
        for g in range(n):
            copy(g, 1 - c).wait_recv()
        for cp in mine:
            cp.wait_send()

    return _exchange(body, gfulls, None, len(gfulls), name, sequencer and (sequencer, _sibling))


def _allreduce_small(v):
    rows, cols = v.shape

    def body(v_ref, out_ref, buf, send_sems, recv_sems):
        x, y, c = _place()
        cps = []
        for k in range(1, 8):
            peer = (_flip(x, (k >> 2) & 1), _flip(y, (k >> 1) & 1), _flip(c, k & 1))
            cps.append(pltpu.make_async_remote_copy(
                src_ref=v_ref, dst_ref=buf.at[k - 1], send_sem=send_sems.at[k - 1], recv_sem=recv_sems.at[k - 1],
                device_id=peer, device_id_type=MESH_ID))
        for cp in cps:
            cp.start()
        for cp in cps:
            cp.wait_recv()
        for cp in cps:
            cp.wait_send()
        t0 = v_ref[...] + buf[0]
        t1 = buf[1] + buf[2]
        t2 = buf[3] + buf[4]
        t3 = buf[5] + buf[6]
        out_ref[...] = (t0 + t1) + (t2 + t3)

    vm = pl.BlockSpec(memory_space=pltpu.VMEM)
    return pl.pallas_call(
        body, out_shape=jax.ShapeDtypeStruct((rows, cols), F32), in_specs=[vm], out_specs=vm,
        scratch_shapes=[pltpu.VMEM((7, rows, cols), F32), pltpu.SemaphoreType.DMA((7,)), pltpu.SemaphoreType.DMA((7,))],
        name="allreduce_small")(v)


BIG_INFO = {n: (shape, ax) for n, shape, ax in BIG}
GROUPS = (("w_in",), ("w_ffn_gate", "w_ffn_up", "w_ffn_down", "w_out", "w_ple_gate"),
          ("w_branch_a", "w_branch_b", "w_ple_proj"))


def _shard_shape(name):
    (k, m), ax = BIG_INFO[name]
    return (k // N_CHIPS, m) if ax == 0 else (k, m // N_CHIPS)


def _group_rows(group):
    offs, off = {}, 0
    for n in group:
        offs[n] = off
        off += _shard_shape(n)[0]
    return offs, off


def _pack_groups(shards, layer, dtype):
    return [jnp.concatenate([shards[n][layer].astype(dtype) for n in group], axis=0) for group in GROUPS]


def _unpack_full(gathered, groups):
    out = {}
    for group, arr in zip(groups, gathered):
        offs, _ = _group_rows(group)
        for n in group:
            rows, cols = _shard_shape(n)
            (k, m), ax = BIG_INFO[n]
            slab = arr[:, offs[n]:offs[n] + rows]
            out[n] = slab.reshape(k, m) if ax == 0 else jnp.transpose(slab, (1, 0, 2)).reshape(k, m)
    return out


def _pack_grads(gfull):
    out = []
    for group in GROUPS:
        parts = []
        for n in group:
            rows, cols = _shard_shape(n)
            ax = BIG_INFO[n][1]
            slab = (gfull[n].reshape(N_CHIPS, rows, cols) if ax == 0
                    else jnp.transpose(gfull[n].reshape(rows, N_CHIPS, cols), (1, 0, 2)))
            parts.append(slab)
        out.append(jnp.concatenate(parts, axis=1))
    return out


def _after(values, mark):
    values, _ = lax.optimization_barrier((values, mark))
    return values


def _reduce_scatter_begin(gsends, place, tag, ids):
    sibs = _sibling_halves(gsends, "rs_sibling_halves_" + tag, ids[0])
    tsends = [_chip_sums(g, s, place) for g, s in zip(gsends, sibs)]
    return tsends, _exchange_chip_sums(tsends, "rs_exchange_" + tag, ids[1])


def _reduce_scatter_finish(begun, place, tag, ids, hold):
    tsends, recvs = begun
    recvs = _after(recvs, hold)
    return _join_halves([_final_sum(t, r, place) for t, r in zip(tsends, recvs)], "rs_join_halves_" + tag, ids[2])


SMALL_SHAPES = {"rel_table": (NUM_BUCKETS, 2 * N_HEADS), "norm_mix_g": (DEPTH, D_MODEL), "qnorm_a_g": (DEPTH, HEAD_DIM),
                "knorm_a_g": (DEPTH, HEAD_DIM), "qnorm_b_g": (DEPTH, HEAD_DIM), "knorm_b_g": (DEPTH, HEAD_DIM),
                "sink_b": (DEPTH, N_HEADS), "norm_ffn_g": (DEPTH, D_MODEL), "norm_ple_g": (DEPTH, D_MODEL)}


def _pack_small(vals, last=None):
    flat = jnp.concatenate([vals[n].astype(F32).reshape(-1) for n in SMALL])
    tail = jnp.zeros((SMALL_ROWS * LANES - flat.shape[0],), F32)
    if last is not None:
        tail = tail.at[-1].set(last)
    return jnp.concatenate([flat, tail]).reshape(SMALL_ROWS, LANES)


def _unpack_small(packed):
    flat, out, off = packed.reshape(-1), {}, 0
    for n in SMALL:
        size = math.prod(SMALL_SHAPES[n])
        out[n] = flat[off:off + size].reshape(SMALL_SHAPES[n])
        off += size
    return out


def _adamw(w, gs, g_row, m, v, name):
    c1 = 1.0 - ADAM_B1 ** ADAM_STEP
    c2 = 1.0 - ADAM_B2 ** ADAM_STEP
    total, width = w.shape
    n_layers = len(gs)
    per = total // n_layers
    tm = max(t for t in range(8, 513, 8) if per % t == 0 and g_row % t == 0)
    nblk = per // tm

    def body(*refs):
        w_ref, g_refs = refs[0], refs[1:1 + n_layers]
        m_ref, v_ref, og, od, om, ov = refs[1 + n_layers:]
        layer = pl.program_id(0) // nblk
        g = g_refs[0][...]
        for l in range(1, n_layers):
            g = jnp.where(layer == l, g_refs[l][...], g)
        m_new = ADAM_B1 * m_ref[...] + (1.0 - ADAM_B1) * g
        v_new = ADAM_B2 * v_ref[...] + (1.0 - ADAM_B2) * (g * g)
        og[...] = g
        od[...] = -ADAM_LR * ((m_new / c1) / (jnp.sqrt(v_new / c2) + ADAM_EPS) + ADAM_WD * w_ref[...])
        om[...] = m_new
        ov[...] = v_new

    row = pl.BlockSpec((tm, width), lambda i: (i, 0))
    g_specs = [pl.BlockSpec((tm, width), lambda i, l=l: (g_row // tm + jnp.clip(i - l * nblk, 0, nblk - 1), 0))
               for l in range(n_layers)]
    return pl.pallas_call(
        body, out_shape=[jax.ShapeDtypeStruct((total, width), F32)] * 4, grid=(total // tm,),
        in_specs=[row] + g_specs + [row, row], out_specs=[row] * 4, name=name,
        compiler_params=_params("parallel"))(w, *gs, m, v)


def kernel(x, p, rel_table, norm_mix_g, w_in, qnorm_a_g, knorm_a_g, qnorm_b_g, knorm_b_g, sink_b, w_branch_a, w_branch_b, w_out, norm_ffn_g, w_ffn_gate, w_ffn_up, w_ffn_down, norm_ple_g, w_ple_gate, w_ple_proj, loss_target, m_rel_table, m_norm_mix_g, m_w_in, m_qnorm_a_g, m_knorm_a_g, m_qnorm_b_g, m_knorm_b_g, m_sink_b, m_w_branch_a, m_w_branch_b, m_w_out, m_norm_ffn_g, m_w_ffn_gate, m_w_ffn_up, m_w_ffn_down, m_norm_ple_g, m_w_ple_gate, m_w_ple_proj, v_rel_table, v_norm_mix_g, v_w_in, v_qnorm_a_g, v_knorm_a_g, v_qnorm_b_g, v_knorm_b_g, v_sink_b, v_w_branch_a, v_w_branch_b, v_w_out, v_norm_ffn_g, v_w_ffn_gate, v_w_ffn_up, v_w_ffn_down, v_norm_ple_g, v_w_ple_gate, v_w_ple_proj):
    given = dict(locals())

    def held(name, a):
        return jnp.swapaxes(a, 1, 2) if name in TRANSPOSED else a

    weights = {n: held(n, given[n]) for n in WEIGHTS}
    moments_m = {n: held(n, given["m_" + n]) for n in WEIGHTS}
    moments_v = {n: held(n, given["v_" + n]) for n in WEIGHTS}
    xi, yi, ci = _place()
    place = jnp.stack([2 * xi + yi, ci]).astype(jnp.int32)

    shards = [_pack_groups(weights, l, BF16) for l in range(DEPTH)]
    w_in0 = _allgather(shards[0][:1], "allgather_w_in_layer0", sequencer=9)
    rest0 = _allgather(_after(shards[0][1:], w_in0), "allgather_rest_layer0", sequencer=1)
    gathered = [w_in0 + rest0, None]
    small = {n: weights[n] for n in SMALL}

    def w_in_of(l, mark):
        return _unpack_full(_after(gathered[l][:1], shards[1] if l == 0 else mark), GROUPS[:1])["w_in"]

    def rest_of(l, mark):
        if l == 0:
            gathered[1] = _allgather(_after(shards[1], mark), "allgather_layer1", sequencer=2)
        return _unpack_full(_after(gathered[l][1:], mark), GROUPS[1:])

    loss, dx, gbig, gsmall, marks = _local_step(x[0], p[:, 0], loss_target[0], w_in_of, rest_of, small)

    gsends = [_pack_grads(gbig[l]) for l in range(DEPTH)]
    stages = {"layer1": (gsends[1], (3, 4, 5)), "rest_layer0": (gsends[0][1:], (6, 7, 8)),
              "w_in_layer0": (gsends[0][:1], (10, 11, 12))}
    begun = {tag: _reduce_scatter_begin(g, place, tag, ids) for tag, (g, ids) in stages.items()}

    def finish(tag, hold):
        return _reduce_scatter_finish(begun[tag], place, tag, stages[tag][1], hold)

    red1 = finish("layer1", marks[0]["attn_bwd_done"])
    rest0 = finish("rest_layer0", marks[0]["attn_bwd_done"])

    grads, delta, new_m, new_v = {}, {}, {}, {}

    def update(group, reduced):
        offs, _ = _group_rows(group)
        for n in group:
            shape = weights[n].shape
            two_d = lambda a: a.reshape(shape[0] * shape[1], shape[2])
            outs = _adamw(two_d(weights[n]), reduced, offs[n], two_d(moments_m[n]), two_d(moments_v[n]), "adamw_" + n)
            grads[n], delta[n], new_m[n], new_v[n] = (held(n, o.reshape(shape)) for o in outs)

    for gi in (1, 2):
        update(GROUPS[gi], _after([rest0[gi - 1], red1[gi]], begun["w_in_layer0"][0]))
    others_done = [dx] + [delta[n] for gi in (1, 2) for n in GROUPS[gi]]
    update(GROUPS[0], [finish("w_in_layer0", others_done)[0], red1[0]])
    small_grads = _allreduce_small(_pack_small(gsmall, last=loss))
    g_, d_, m_, v_ = _adamw(_pack_small(weights), [small_grads], 0, _pack_small(moments_m), _pack_small(moments_v),
                            "adamw_small")
    grads.update(_unpack_small(g_))
    delta.update(_unpack_small(d_))
    new_m.update(_unpack_small(m_))
    new_v.update(_unpack_small(v_))

    return (small_grads[-1, -1], dx[None], *[grads[n] for n in WEIGHTS], *[delta[n] for n in WEIGHTS],
            *[new_m[n] for n in WEIGHTS], *[new_v[n] for n in WEIGHTS])
```

```python
import functools
import math

import jax
import jax.numpy as jnp
from jax import lax
from jax.experimental import pallas as pl
from jax.experimental.pallas import tpu as pltpu
from jax.experimental.pallas import tpu_sc as plsc

F32 = jnp.float32
BF16 = jnp.bfloat16
MESH_ID = pl.DeviceIdType.MESH

D_MODEL = 1024
DEPTH = 2
HEAD_DIM = 64
N_HEADS = 8
WIDTH = N_HEADS * HEAD_DIM
N_PAIRS = 4
ITEMS = 4
N_KV_B = 2
PLE_DIM = 256
D_FF = 2816
D_IN = 4352
OFF_QA, OFF_KA, OFF_VA, OFF_QB, OFF_KB, OFF_VB, OFF_GA, OFF_GB = 0, 512, 1024, 1536, 2048, 2176, 2304, 3328
DILATED = ((64, 1), (64, 4), (64, 16))
BLK_B = 128
NUM_BUCKETS = 32
MAX_DISTANCE = 1024
RMS_EPS = 1e-6
NEG_INF = -1e30
LANES = 128
ROW_TILE = 256
VMEM_LIMIT = 48 * 1024 * 1024

ADAM_LR, ADAM_B1, ADAM_B2, ADAM_EPS, ADAM_WD, ADAM_STEP = 0.001, 0.9, 0.999, 1e-08, 0.01, 10

TRANSPOSED = ("w_in", "w_ffn_gate", "w_ffn_up")
BIG = (
    ("w_in", (D_IN, D_MODEL), 0),
    ("w_branch_a", (WIDTH, D_MODEL), 1),
    ("w_branch_b", (WIDTH, D_MODEL), 1),
    ("w_out", (D_MODEL, D_MODEL), 0),
    ("w_ffn_gate", (D_FF, D_MODEL), 0),
    ("w_ffn_up", (D_FF, D_MODEL), 0),
    ("w_ffn_down", (D_FF, D_MODEL), 0),
    ("w_ple_gate", (D_MODEL, D_MODEL), 0),
    ("w_ple_proj", (PLE_DIM, D_MODEL), 1),
)
SMALL = ("rel_table", "norm_mix_g", "qnorm_a_g", "knorm_a_g", "qnorm_b_g", "knorm_b_g", "sink_b",
         "norm_ffn_g", "norm_ple_g")
WEIGHTS = ("rel_table", "norm_mix_g", "w_in", "qnorm_a_g", "knorm_a_g", "qnorm_b_g", "knorm_b_g", "sink_b",
           "w_branch_a", "w_branch_b", "w_out", "norm_ffn_g", "w_ffn_gate", "w_ffn_up", "w_ffn_down",
           "norm_ple_g", "w_ple_gate", "w_ple_proj")
N_CHIPS = 4
SMALL_ROWS = 64


def _params(*sem):
    return pltpu.CompilerParams(dimension_semantics=sem, vmem_limit_bytes=VMEM_LIMIT)


MM_VMEM_BUDGET = 40 * 1024 * 1024
STEP_OVERHEAD_S = 0.4e-6
TILE_DMA_BYTES_PER_S = 1.5e12


def _mm_dims(a, b, mode):
    if mode == "nn":
        return a.shape[0], b.shape[1], a.shape[1]
    if mode == "nt":
        return a.shape[0], b.shape[0], a.shape[1]
    return a.shape[1], b.shape[1], a.shape[0]


def _mm_tiles(m, n, pairs, tile_bytes, col_offsets):
    best = None
    for tm in (t for t in range(LANES, m + 1, LANES) if m % t == 0):
        for tn in (t for t in range(LANES, n + 1, LANES) if n % t == 0 and all(o % t == 0 for o in col_offsets)):
            io = sum(tm * k * ab + tn * k * bb for k, ab, bb in pairs) + tm * tn * sum(tile_bytes)
            casts = sum((tm * k * 2 if ab == 4 else 0) + (tn * k * 2 if bb == 4 else 0) for k, ab, bb in pairs)
            if 2 * io + len(pairs) * tm * tn * 4 + casts > MM_VMEM_BUDGET:
                continue
            cost = (m // tm) * (n // tn) * STEP_OVERHEAD_S + io / TILE_DMA_BYTES_PER_S
            if best is None or (cost, -tm) < best[0]:
                best = ((cost, -tm), tm, tn)
    return best[1], best[2]


def _mm_fused(pairs, extras, epilogue, out_dtypes, name):
    m, n, _ = _mm_dims(*pairs[0])
    assert all(_mm_dims(*p)[:2] == (m, n) for p in pairs)
    tm, tn = _mm_tiles(
        m, n, [(_mm_dims(a, b, mode)[2], a.dtype.itemsize, b.dtype.itemsize) for a, b, mode in pairs],
        [e.dtype.itemsize for e, _ in extras] + [jnp.dtype(d).itemsize for d in out_dtypes], [off for _, off in extras])
    dims = {"nn": (((1,), (0,)), ((), ())), "nt": (((1,), (1,)), ((), ())), "tn": (((0,), (0,)), ((), ()))}
    in_specs, args = [], []
    for a, b, mode in pairs:
        k = _mm_dims(a, b, mode)[2]
        in_specs.append(pl.BlockSpec((k, tm), lambda i, j: (0, i)) if mode == "tn" else pl.BlockSpec((tm, k), lambda i, j: (i, 0)))
        in_specs.append(pl.BlockSpec((tn, k), lambda i, j: (j, 0)) if mode == "nt" else pl.BlockSpec((k, tn), lambda i, j: (0, j)))
        args += [a, b]
    for e, off in extras:
        in_specs.append(pl.BlockSpec((tm, tn), lambda i, j, o=off // tn: (i, o + j)))
        args.append(e)
    n_pairs, n_in = len(pairs), 2 * len(pairs) + len(extras)

    def body(*refs):
        products = [lax.dot_general(refs[2 * p][...].astype(BF16), refs[2 * p + 1][...].astype(BF16), dims[pairs[p][2]],
                                    preferred_element_type=F32) for p in range(n_pairs)]
        outs = epilogue(products, [r[...] for r in refs[2 * n_pairs:n_in]])
        for r, o in zip(refs[n_in:], outs):
            r[...] = o.astype(r.dtype)

    tile = pl.BlockSpec((tm, tn), lambda i, j: (i, j))
    return pl.pallas_call(
        body, out_shape=[jax.ShapeDtypeStruct((m, n), d) for d in out_dtypes], grid=(m // tm, n // tn),
        in_specs=in_specs, out_specs=[tile] * len(out_dtypes), name=name,
        compiler_params=_params("parallel", "parallel"))(*args)


def _mm(a, b, mode, out_dtype, name, res=None):
    if res is None:
        return _mm_fused([(a, b, mode)], [], lambda products, extra: products, [out_dtype], name)[0]
    return _mm_fused([(a, b, mode)], [(res, 0)], lambda products, extra: [products[0] + extra[0]], [out_dtype], name)[0]


def _ew(fn, ins, out_dtypes, name):
    rows, width = ins[0].shape
    n_in = len(ins)

    def body(*refs):
        outs = fn(*[r[...] for r in refs[:n_in]])
        for r, o in zip(refs[n_in:], outs):
            r[...] = o.astype(r.dtype)

    row = pl.BlockSpec((ROW_TILE, width), lambda i: (i, 0))
    return pl.pallas_call(
        body, out_shape=[jax.ShapeDtypeStruct((rows, width), dt) for dt in out_dtypes], grid=(rows // ROW_TILE,),
        in_specs=[row] * n_in, out_specs=[row] * len(out_dtypes), name=name, compiler_params=_params("parallel"))(*ins)


def _sigmoid(x):
    return 1.0 / (1.0 + jnp.exp(-x))


def _seg_sum(v):
    outs = []
    for k in range(v.shape[1] // LANES):
        vp = v[:, k * LANES:(k + 1) * LANES]
        left = lax.broadcasted_iota(jnp.int32, vp.shape, 1) < HEAD_DIM
        sl = jnp.sum(jnp.where(left, vp, 0.0), axis=-1, keepdims=True)
        sr = jnp.sum(jnp.where(left, 0.0, vp), axis=-1, keepdims=True)
        outs.append(jnp.where(left, sl, sr))
    return outs[0] if len(outs) == 1 else jnp.concatenate(outs, axis=1)


def _seg_rstd(x):
    return lax.rsqrt(_seg_sum(x * x) * (1.0 / HEAD_DIM) + RMS_EPS)


def _rms_fwd(x, g, name):
    rows, d = x.shape
    tm = ROW_TILE

    def body(x_ref, g_ref, h_ref):
        xv = x_ref[...]
        r = lax.rsqrt(jnp.mean(xv * xv, axis=-1, keepdims=True) + RMS_EPS)
        h_ref[...] = ((xv * r) * g_ref[...]).astype(BF16)

    return pl.pallas_call(
        body, out_shape=jax.ShapeDtypeStruct((rows, d), BF16), grid=(rows // tm,),
        in_specs=[pl.BlockSpec((tm, d), lambda i: (i, 0)), pl.BlockSpec((1, d), lambda i: (0, 0))],
        out_specs=pl.BlockSpec((tm, d), lambda i: (i, 0)), name=name,
        compiler_params=_params("parallel"))(x, g)


def _rms_bwd(x, g, dh, dres, name):
    rows, d = x.shape
    tm = ROW_TILE

    def body(x_ref, g_ref, dh_ref, dres_ref, dx_ref, dxb_ref, dg_ref):
        xv = x_ref[...]
        r = lax.rsqrt(jnp.mean(xv * xv, axis=-1, keepdims=True) + RMS_EPS)
        xh = xv * r
        dhv = dh_ref[...]
        dxh = dhv * g_ref[...]
        dxv = dres_ref[...] + r * (dxh - xh * jnp.mean(dxh * xh, axis=-1, keepdims=True))
        dx_ref[...] = dxv
        dxb_ref[...] = dxv.astype(BF16)
        part = jnp.sum(dhv * xh, axis=0, keepdims=True)

        @pl.when(pl.program_id(0) == 0)
        def _():
            dg_ref[...] = part

        @pl.when(pl.program_id(0) > 0)
        def _():
            dg_ref[...] += part

    row = pl.BlockSpec((tm, d), lambda i: (i, 0))
    vec = pl.BlockSpec((1, d), lambda i: (0, 0))
    return pl.pallas_call(
        body, out_shape=[jax.ShapeDtypeStruct((rows, d), F32), jax.ShapeDtypeStruct((rows, d), BF16),
                         jax.ShapeDtypeStruct((1, d), F32)],
        grid=(rows // tm,), in_specs=[row, vec, row, row], out_specs=[row, row, vec],
        name=name, compiler_params=_params("arbitrary"))(x, g, dh, dres)


def _loss_grad(y, t):
    rows, d = y.shape
    tm = ROW_TILE

    def body(y_ref, t_ref, dy_ref, l_ref):
        e = y_ref[...] - t_ref[...]
        dy_ref[...] = e * (1.0 / d)
        part = jnp.zeros((1, LANES), F32) + jnp.sum(e * e) * (0.5 / d)

        @pl.when(pl.program_id(0) == 0)
        def _():
            l_ref[...] = part

        @pl.when(pl.program_id(0) > 0)
        def _():
            l_ref[...] += part

    row = pl.BlockSpec((tm, d), lambda i: (i, 0))
    return pl.pallas_call(
        body, out_shape=[jax.ShapeDtypeStruct((rows, d), F32), jax.ShapeDtypeStruct((1, LANES), F32)],
        grid=(rows // tm,), in_specs=[row, row], out_specs=[row, pl.BlockSpec((1, LANES), lambda i: (0, 0))],
        name="loss_grad", compiler_params=_params("arbitrary"))(y, t)


def _swap_halves(v):
    return pltpu.roll(v, HEAD_DIM, axis=1)


def _expand_kv(kv):
    left = lax.broadcasted_iota(jnp.int32, kv.shape, 1) < HEAD_DIM
    sw = _swap_halves(kv)
    h0 = jnp.where(left, kv, sw)
    h1 = jnp.where(left, sw, kv)
    return jnp.concatenate([h0, h0, h1, h1], axis=1)


def _reduce_kv(dkv):
    left = lax.broadcasted_iota(jnp.int32, (dkv.shape[0], LANES), 1) < HEAD_DIM
    t = dkv[:, 0:LANES] + dkv[:, LANES:2 * LANES]
    u = dkv[:, 2 * LANES:3 * LANES] + dkv[:, 3 * LANES:4 * LANES]
    t = t + _swap_halves(t)
    u = u + _swap_halves(u)
    return jnp.where(left, t, u)


def _qknorm_fwd(proj, gqa, gka, gqb, gkb):
    rows = proj.shape[0]
    tm = ROW_TILE

    def body(qa_ref, ka_ref, qb_ref, kb_ref, vb_ref, gqa_ref, gka_ref, gqb_ref, gkb_ref, oqa, oka, oqb, okb, ovb):
        for src, g_ref, dst in ((qa_ref, gqa_ref, oqa), (ka_ref, gka_ref, oka), (qb_ref, gqb_ref, oqb)):
            xv = src[...]
            dst[...] = (xv * _seg_rstd(xv)) * g_ref[...]
        kv = kb_ref[...]
        okb[...] = _expand_kv((kv * _seg_rstd(kv)) * gkb_ref[...])
        ovb[...] = _expand_kv(vb_ref[...])

    def win(width, off):
        return pl.BlockSpec((tm, width), lambda i: (i, off // width))

    vec = lambda w: pl.BlockSpec((1, w), lambda i: (0, 0))
    out = pl.BlockSpec((tm, WIDTH), lambda i: (i, 0))
    return pl.pallas_call(
        body, out_shape=[jax.ShapeDtypeStruct((rows, WIDTH), F32)] * 5, grid=(rows // tm,),
        in_specs=[win(WIDTH, OFF_QA), win(WIDTH, OFF_KA), win(WIDTH, OFF_QB), win(LANES, OFF_KB), win(LANES, OFF_VB),
                  vec(WIDTH), vec(WIDTH), vec(WIDTH), vec(LANES)],
        out_specs=[out] * 5, name="qknorm_fwd", compiler_params=_params("parallel"))(
            proj, proj, proj, proj, proj, gqa, gka, gqb, gkb)


def _norm_bwd(xv, g, dy):
    r = _seg_rstd(xv)
    xh = xv * r
    dxh = dy * g
    dx = r * (dxh - xh * (_seg_sum(dxh * xh) * (1.0 / HEAD_DIM)))
    return dx, jnp.sum(dy * xh, axis=0, keepdims=True)


def _qknorm_bwd(proj, gqa, gka, gqb, gkb, dqa, dka, dva, dqb, dkb, dvb, dga, dgb):
    rows = proj.shape[0]
    tm = ROW_TILE
    n_a = len(dqa)

    def body(*refs):
        qa_ref, ka_ref, qb_ref, kb_ref, gqa_ref, gka_ref, gqb_ref, gkb_ref = refs[:8]
        pos = 8
        dqa_refs, dka_refs, dva_refs = refs[pos:pos + n_a], refs[pos + n_a:pos + 2 * n_a], refs[pos + 2 * n_a:pos + 3 * n_a]
        pos += 3 * n_a
        dqb_ref, dkb_ref, dvb_ref, dga_ref, dgb_ref = refs[pos:pos + 5]
        dproj_ref, ogqa, ogka, ogqb, ogkb = refs[pos + 5:]

        def total(rs):
            acc = rs[0][...]
            for r in rs[1:]:
                acc = acc + r[...]
            return acc

        dx_qa, p_qa = _norm_bwd(qa_ref[...], gqa_ref[...], total(dqa_refs))
        dx_ka, p_ka = _norm_bwd(ka_ref[...], gka_ref[...], total(dka_refs))
        dx_qb, p_qb = _norm_bwd(qb_ref[...], gqb_ref[...], dqb_ref[...])
        dx_kb, p_kb = _norm_bwd(kb_ref[...], gkb_ref[...], _reduce_kv(dkb_ref[...]))
        dproj_ref[:, OFF_QA:OFF_QA + WIDTH] = dx_qa.astype(BF16)
        dproj_ref[:, OFF_KA:OFF_KA + WIDTH] = dx_ka.astype(BF16)
        dproj_ref[:, OFF_VA:OFF_VA + WIDTH] = total(dva_refs).astype(BF16)
        dproj_ref[:, OFF_QB:OFF_QB + WIDTH] = dx_qb.astype(BF16)
        dproj_ref[:, OFF_KB:OFF_KB + LANES] = dx_kb.astype(BF16)
        dproj_ref[:, OFF_VB:OFF_VB + LANES] = _reduce_kv(dvb_ref[...]).astype(BF16)
        dproj_ref[:, OFF_GA:OFF_GB] = dga_ref[...]
        dproj_ref[:, OFF_GB:D_IN] = dgb_ref[...]
        first = pl.program_id(0) == 0
        for o_ref, part in ((ogqa, p_qa), (ogka, p_ka), (ogqb, p_qb), (ogkb, p_kb)):
            @pl.when(first)
            def _(o_ref=o_ref, part=part):
                o_ref[...] = part

            @pl.when(jnp.logical_not(first))
            def _(o_ref=o_ref, part=part):
                o_ref[...] += part

    def win(width, off):
        return pl.BlockSpec((tm, width), lambda i: (i, off // width))

    vec = lambda w: pl.BlockSpec((1, w), lambda i: (0, 0))
    row = lambda w: pl.BlockSpec((tm, w), lambda i: (i, 0))
    in_specs = [win(WIDTH, OFF_QA), win(WIDTH, OFF_KA), win(WIDTH, OFF_QB), win(LANES, OFF_KB),
                vec(WIDTH), vec(WIDTH), vec(WIDTH), vec(LANES)]
    in_specs += [row(WIDTH)] * (3 * n_a + 3) + [row(D_MODEL)] * 2
    return pl.pallas_call(
        body,
        out_shape=[jax.ShapeDtypeStruct((rows, D_IN), BF16), jax.ShapeDtypeStruct((1, WIDTH), F32),
                   jax.ShapeDtypeStruct((1, WIDTH), F32), jax.ShapeDtypeStruct((1, WIDTH), F32),
                   jax.ShapeDtypeStruct((1, LANES), F32)],
        grid=(rows // tm,), in_specs=in_specs,
        out_specs=[row(D_IN), vec(WIDTH), vec(WIDTH), vec(WIDTH), vec(LANES)],
        name="qknorm_bwd", compiler_params=_params("arbitrary"))(
            proj, proj, proj, proj, gqa, gka, gqb, gkb, *dqa, *dka, *dva, dqb, dkb, dvb, dga, dgb)


def _t5_bucket(rel):
    half_b = NUM_BUCKETS // 2
    max_exact = half_b // 2
    sign = jnp.where(rel > 0, half_b, 0)
    n = jnp.abs(rel)
    nf = jnp.maximum(n, 1).astype(F32)
    large = max_exact + (jnp.log(nf / max_exact) / math.log(MAX_DISTANCE / max_exact)
                         * (half_b - max_exact)).astype(jnp.int32)
    large = jnp.minimum(large, half_b - 1)
    return sign + jnp.where(n < max_exact, n, large)


def _band_buckets(blk, dilation):
    i = jnp.arange(blk, dtype=jnp.int32)[:, None]
    j = jnp.arange(3 * blk, dtype=jnp.int32)[None, :]
    rel = j - blk - i
    return jnp.where(jnp.abs(rel) <= blk, _t5_bucket(rel * dilation), -1)


def _bias_tiles(table, buckets, head_off, name):
    blk = buckets.shape[0]

    def body(tab_ref, bk_ref, o_ref):
        h = pl.program_id(0) + head_off
        bk = bk_ref[...]
        acc = jnp.full(bk.shape, NEG_INF, F32)
        for b in range(NUM_BUCKETS):
            acc = jnp.where(bk == b, tab_ref[b, h], acc)
        o_ref[0] = acc

    return pl.pallas_call(
        body, out_shape=jax.ShapeDtypeStruct((N_HEADS, blk, 3 * blk), F32), grid=(N_HEADS,),
        in_specs=[pl.BlockSpec(memory_space=pltpu.SMEM), pl.BlockSpec((blk, 3 * blk), lambda h: (0, 0))],
        out_specs=pl.BlockSpec((1, blk, 3 * blk), lambda h: (h, 0, 0)),
        name=name, compiler_params=_params("parallel"))(table, buckets)


def _table_grad(dbias, buckets, name):
    blk = buckets.shape[0]

    def body(db_ref, bk_ref, o_ref):
        bk = bk_ref[...]
        dbv = db_ref[0]
        lane = lax.broadcasted_iota(jnp.int32, (1, LANES), 1)
        acc = jnp.zeros((1, LANES), F32)
        for b in range(NUM_BUCKETS):
            acc = jnp.where(lane == b, jnp.sum(jnp.where(bk == b, dbv, 0.0)), acc)
        o_ref[0] = acc

    out = pl.pallas_call(
        body, out_shape=jax.ShapeDtypeStruct((N_HEADS, 1, LANES), F32), grid=(N_HEADS,),
        in_specs=[pl.BlockSpec((1, blk, 3 * blk), lambda h: (h, 0, 0)), pl.BlockSpec((blk, 3 * blk), lambda h: (0, 0))],
        out_specs=pl.BlockSpec((1, 1, LANES), lambda h: (h, 0, 0)),
        name=name, compiler_params=_params("parallel"))(dbias, buckets)
    return out[:, 0, :NUM_BUCKETS]


def _dot_nt(a, b):
    return lax.dot_general(a, b, (((1,), (1,)), ((), ())), preferred_element_type=F32)


def _dot_tn(a, b):
    return lax.dot_general(a, b, (((0,), (0,)), ((), ())), preferred_element_type=F32)


def _stack_pair(x2, left):
    return jnp.concatenate([jnp.where(left, x2, 0.0), jnp.where(left, 0.0, x2)], axis=0).astype(BF16)


def _attn_geometry(blk, d):
    chunk = blk * ITEMS if d == 1 else blk * d
    groups = 1 if d == 1 else d // ITEMS
    halo = blk if d == 1 else chunk
    return chunk, groups, halo


def _item_rows(ref, j, r0, blk, d):
    if d == 1:
        return ref[j * blk:(j + 1) * blk, :]
    return ref[pl.ds(r0 + j, blk, stride=d), :]


def _item_penalty(t, nct, j, blk, d):
    first_ok, last_ok = t > 0, t < nct - 1
    if d == 1:
        first_ok = True if j > 0 else first_ok
        last_ok = True if j < ITEMS - 1 else last_ok
    col = lax.broadcasted_iota(jnp.int32, (1, 3 * blk), 1)
    ok = jnp.logical_and(jnp.logical_or(col >= blk, first_ok), jnp.logical_or(col < 2 * blk, last_ok))
    return jnp.where(ok, 0.0, NEG_INF).astype(F32)


def _attn_specs(seq, blk, d, step_of, col=0):
    chunk, _, halo = _attn_geometry(blk, d)
    per, last, first = chunk // halo, seq // halo - 1, col // LANES
    cur = pl.BlockSpec((chunk, LANES), lambda hp, t: (step_of(t), first + hp))
    prev = pl.BlockSpec((halo, LANES), lambda hp, t: (jnp.clip(step_of(t) * per - 1, 0, last), first + hp))
    nxt = pl.BlockSpec((halo, LANES), lambda hp, t: (jnp.minimum((step_of(t) + 1) * per, last), first + hp))
    return cur, prev, nxt


def _attn_fwd(q, k, v, bias, sink, blk, d, name, v_col=0):
    seq = q.shape[0]
    chunk, groups, _ = _attn_geometry(blk, d)
    nct = seq // chunk
    has_sink = sink is not None
    scale = HEAD_DIM ** -0.5

    def body(*refs):
        q_ref, kp, kc, kn, vp, vc, vn, b_ref = refs[:8]
        s_ref = refs[8] if has_sink else None
        o_ref, l_ref = refs[-2], refs[-1]
        t = pl.program_id(1)
        left = lax.broadcasted_iota(jnp.int32, (1, LANES), 1) < HEAD_DIM
        bias2 = b_ref[...]
        if d == 1:
            kwin = jnp.concatenate([kp[...], kc[...], kn[...]], axis=0).astype(BF16)
            vwin = jnp.concatenate([vp[...], vc[...], vn[...]], axis=0).astype(BF16)

        def group(r0):
            scores, vcats = [], []
            for j in range(ITEMS):
                qs = _stack_pair(_item_rows(q_ref, j, r0, blk, d) * scale, left)
                if d == 1:
                    kcat, vcat = kwin[j * blk:(j + 3) * blk], vwin[j * blk:(j + 3) * blk]
                else:
                    kcat = jnp.concatenate([_item_rows(r, j, r0, blk, d) for r in (kp, kc, kn)], axis=0).astype(BF16)
                    vcat = jnp.concatenate([_item_rows(r, j, r0, blk, d) for r in (vp, vc, vn)], axis=0).astype(BF16)
                scores.append(_dot_nt(qs, kcat) + bias2 + _item_penalty(t, nct, j, blk, d))
                vcats.append(vcat)
            ms = [jnp.max(s, axis=-1, keepdims=True) for s in scores]
            if has_sink:
                sk = s_ref[...]
                ms = [jnp.maximum(m, sk) for m in ms]
            ps = [jnp.exp(s - m) for s, m in zip(scores, ms)]
            dens = [jnp.sum(p, axis=-1, keepdims=True) for p in ps]
            if has_sink:
                dens = [den + jnp.exp(sk - m) for den, m in zip(dens, ms)]
            pns = [(p * (1.0 / den)).astype(BF16) for p, den in zip(ps, dens)]
            lses = [m + jnp.log(den) for m, den in zip(ms, dens)]
            for j in range(ITEMS):
                o2 = jnp.dot(pns[j], vcats[j], preferred_element_type=F32)
                o_val = jnp.where(left, o2[:blk], o2[blk:])
                l_val = jnp.where(left, lses[j][:blk], lses[j][blk:])
                if d == 1:
                    o_ref[j * blk:(j + 1) * blk, :] = o_val
                    l_ref[j * blk:(j + 1) * blk, :] = l_val
                else:
                    o_ref[pl.ds(r0 + j, blk, stride=d), :] = o_val
                    l_ref[pl.ds(r0 + j, blk, stride=d), :] = l_val

        if groups == 1:
            group(0)
        else:
            def step(g, carry):
                group(g * ITEMS)
                return carry

            lax.fori_loop(0, groups, step, 0)

    cur, prev, nxt = _attn_specs(seq, blk, d, lambda t: t)
    v_cur, v_prev, v_nxt = _attn_specs(seq, blk, d, lambda t: t, v_col)
    in_specs = [cur, prev, cur, nxt, v_prev, v_cur, v_nxt, pl.BlockSpec((2 * blk, 3 * blk), lambda hp, t: (hp, 0))]
    args = [q, k, k, k, v, v, v, bias]
    if has_sink:
        in_specs.append(pl.BlockSpec((2 * blk, 1), lambda hp, t: (hp, 0)))
        args.append(sink)
    return pl.pallas_call(
        body, out_shape=[jax.ShapeDtypeStruct((seq, WIDTH), F32)] * 2, grid=(N_PAIRS, nct),
        in_specs=in_specs, out_specs=[cur, cur], name=name,
        compiler_params=_params("parallel", "parallel"))(*args)


def _attn_bwd(q, k, v, do, lse, delta, bias, sink, blk, d, name, v_col=0):
    seq = q.shape[0]
    chunk, groups, halo = _attn_geometry(blk, d)
    nct = seq // chunk
    has_sink = sink is not None
    n_in = 12 if has_sink else 11
    scale = HEAD_DIM ** -0.5

    def body(*refs):
        q_ref, kp, kc, kn, vp, vc, vn, do_ref, l_ref, d_ref, b_ref = refs[:11]
        s_ref = refs[11] if has_sink else None
        dq_ref, dk_ref, dv_ref, db_ref = refs[n_in:n_in + 4]
        ds_ref = refs[n_in + 4] if has_sink else None
        wk, wv = refs[-2], refs[-1]
        t = pl.program_id(1)

        @pl.when(t == 0)
        def _():
            wk[...] = jnp.zeros_like(wk)
            wv[...] = jnp.zeros_like(wv)
            db_ref[...] = jnp.zeros_like(db_ref)
            if has_sink:
                ds_ref[...] = jnp.zeros_like(ds_ref)

        @pl.when(t > 0)
        def _():
            for w in (wk, wv):
                keep = w[chunk:2 * chunk + halo]
                w[0:chunk + halo] = keep
                w[chunk + halo:2 * chunk + halo] = jnp.zeros((chunk, LANES), F32)

        @pl.when(t < nct)
        def _():
            lane = lax.broadcasted_iota(jnp.int32, (1, LANES), 1)
            left = lane < HEAD_DIM
            bias2 = b_ref[...]
            if d == 1:
                kwin = jnp.concatenate([kp[...], kc[...], kn[...]], axis=0).astype(BF16)
                vwin = jnp.concatenate([vp[...], vc[...], vn[...]], axis=0).astype(BF16)

            def group(r0):
                qss, doss, kcats, scores, dps, lcols, dcols = [], [], [], [], [], [], []
                for j in range(ITEMS):
                    qs = _stack_pair(_item_rows(q_ref, j, r0, blk, d) * scale, left)
                    dos = _stack_pair(_item_rows(do_ref, j, r0, blk, d), left)
                    if d == 1:
                        kcat, vcat = kwin[j * blk:(j + 3) * blk], vwin[j * blk:(j + 3) * blk]
                    else:
                        kcat = jnp.concatenate([_item_rows(r, j, r0, blk, d) for r in (kp, kc, kn)], axis=0).astype(BF16)
                        vcat = jnp.concatenate([_item_rows(r, j, r0, blk, d) for r in (vp, vc, vn)], axis=0).astype(BF16)
                    l2, d2 = _item_rows(l_ref, j, r0, blk, d), _item_rows(d_ref, j, r0, blk, d)
                    lcols.append(jnp.concatenate([jnp.max(jnp.where(left, l2, NEG_INF), axis=-1, keepdims=True),
                                                  jnp.max(jnp.where(left, NEG_INF, l2), axis=-1, keepdims=True)], axis=0))
                    dcols.append(jnp.concatenate([jnp.sum(jnp.where(lane == 0, d2, 0.0), axis=-1, keepdims=True),
                                                  jnp.sum(jnp.where(lane == HEAD_DIM, d2, 0.0), axis=-1, keepdims=True)],
                                                 axis=0))
                    scores.append(_dot_nt(qs, kcat) + bias2 + _item_penalty(t, nct, j, blk, d))
                    dps.append(_dot_nt(dos, vcat))
                    qss.append(qs)
                    doss.append(dos)
                    kcats.append(kcat)
                ps = [jnp.exp(s - lc) for s, lc in zip(scores, lcols)]
                dss = [p * (dp - dc) for p, dp, dc in zip(ps, dps, dcols)]
                db_ref[...] += functools.reduce(lambda a, b: a + b, dss)
                if has_sink:
                    sk = s_ref[...]
                    ds_ref[...] -= functools.reduce(lambda a, b: a + b, [dc * jnp.exp(sk - lc) for dc, lc in zip(dcols, lcols)])
                dsbs = [ds.astype(BF16) for ds in dss]
                for j in range(ITEMS):
                    dq2 = jnp.dot(dsbs[j], kcats[j], preferred_element_type=F32) * scale
                    dq_val = jnp.where(left, dq2[:blk], dq2[blk:])
                    if d == 1:
                        dq_ref[j * blk:(j + 1) * blk, :] = dq_val
                    else:
                        dq_ref[pl.ds(r0 + j, blk, stride=d), :] = dq_val
                news = [(_dot_tn(dsbs[j], qss[j]), _dot_tn(ps[j].astype(BF16), doss[j])) for j in range(ITEMS)]
                if d == 1:
                    for which, w in enumerate((wk, wv)):
                        for b in range(ITEMS + 2):
                            parts = [news[j][which][(b - j) * blk:(b - j + 1) * blk] for j in range(ITEMS) if 0 <= b - j < 3]
                            w[chunk + (b - 1) * blk:chunk + b * blk, :] += functools.reduce(lambda x, y: x + y, parts)
                else:
                    for j in range(ITEMS):
                        for which, w in enumerate((wk, wv)):
                            for c in range(3):
                                w[pl.ds(c * chunk + r0 + j, blk, stride=d), :] += news[j][which][c * blk:(c + 1) * blk]

            if groups == 1:
                group(0)
            else:
                def step(g, carry):
                    group(g * ITEMS)
                    return carry

                lax.fori_loop(0, groups, step, 0)

        dk_ref[...] = wk[0:chunk]
        dv_ref[...] = wv[0:chunk]

    cur, prev, nxt = _attn_specs(seq, blk, d, lambda t: jnp.minimum(t, nct - 1))
    v_cur, v_prev, v_nxt = _attn_specs(seq, blk, d, lambda t: jnp.minimum(t, nct - 1), v_col)
    lag = pl.BlockSpec((chunk, LANES), lambda hp, t: (jnp.maximum(t - 1, 0), hp))
    band = pl.BlockSpec((2 * blk, 3 * blk), lambda hp, t: (hp, 0))
    col = pl.BlockSpec((2 * blk, 1), lambda hp, t: (hp, 0))
    in_specs = [cur, prev, cur, nxt, v_prev, v_cur, v_nxt, cur, cur, cur, band]
    args = [q, k, k, k, v, v, v, do, lse, delta, bias]
    out_shape = [jax.ShapeDtypeStruct((seq, WIDTH), F32)] * 3 + [jax.ShapeDtypeStruct((N_HEADS * blk, 3 * blk), F32)]
    out_specs = [cur, lag, lag, band]
    if has_sink:
        in_specs.append(col)
        args.append(sink)
        out_shape.append(jax.ShapeDtypeStruct((N_HEADS * blk, 1), F32))
        out_specs.append(col)
    window = pltpu.VMEM((2 * chunk + halo, LANES), F32)
    return pl.pallas_call(
        body, out_shape=out_shape, grid=(N_PAIRS, nct + 1), in_specs=in_specs, out_specs=out_specs,
        scratch_shapes=[window, window], name=name,
        compiler_params=_params("arbitrary", "arbitrary"))(*args)


def _combine_patterns(outs, lses):
    def combine(*tiles):
        os_, ls = tiles[:len(outs)], tiles[len(outs):]
        m = functools.reduce(jnp.maximum, ls)
        es = [jnp.exp(l - m) for l in ls]
        den = functools.reduce(lambda a, b: a + b, es)
        num = functools.reduce(lambda a, b: a + b, [e * o for e, o in zip(es, os_)])
        return num / den, m + jnp.log(den)

    return _ew(combine, [*outs, *lses], [F32, F32], "combine_a")


def _row_dots(dy, y, name):
    return _ew(lambda dy_, y_: (_seg_sum(dy_ * y_),), [dy, y], [F32], name)[0]


def _tile_gain(g, reps):
    return jnp.tile(g[None, :], (1, reps))


def _local_step(x, p, target, w_in_of, rest_of, small):
    rel_table = small["rel_table"]
    buckets_a = [_band_buckets(blk, d) for blk, d in DILATED]
    buckets_b = _band_buckets(BLK_B, 1)
    bias_a = [_bias_tiles(rel_table, bk, 0, "bias_a").reshape(N_HEADS * bk.shape[0], -1) for bk in buckets_a]
    bias_b = _bias_tiles(rel_table, buckets_b, N_HEADS, "bias_b").reshape(N_HEADS * BLK_B, -1)

    saved = []
    for l in range(DEPTH):
        g_mix, g_ffn, g_ple = (small[n][l][None, :] for n in ("norm_mix_g", "norm_ffn_g", "norm_ple_g"))
        gqa, gka, gqb = (_tile_gain(small[n][l], N_HEADS) for n in ("qnorm_a_g", "knorm_a_g", "qnorm_b_g"))
        gkb = _tile_gain(small["knorm_b_g"][l], N_KV_B)
        sink = jnp.repeat(small["sink_b"][l], BLK_B)[:, None]

        h = _rms_fwd(x, g_mix, "rms_mix")
        w_in = w_in_of(l, h)
        proj = _mm(h, w_in, "nt", F32, "mm_in")
        qa, ka, qb, kb, vb = _qknorm_fwd(proj, gqa, gka, gqb, gkb)
        outs, lses = [], []
        for (blk, d), bias in zip(DILATED, bias_a):
            o, ls = _attn_fwd(qa, ka, proj, bias, None, blk, d, f"attn_a{d}_fwd", v_col=OFF_VA)
            outs.append(o)
            lses.append(ls)
        ya, lse_a = _combine_patterns(outs, lses)
        yb, lse_b = _attn_fwd(qb, kb, vb, bias_b, sink, BLK_B, 1, "attn_b_fwd")
        w = dict(rest_of(l, yb), w_in=w_in)
        def gate(products, extra):
            (ca_, cb_), (ga_, gb_) = products, extra
            return _sigmoid(ga_) * ca_ + _sigmoid(gb_) * cb_, ca_, cb_

        merged, ca, cb = _mm_fused([(ya, w["w_branch_a"], "nn"), (yb, w["w_branch_b"], "nn")],
                                   [(proj, OFF_GA), (proj, OFF_GB)], gate, [BF16, BF16, BF16], "mm_branches_gate")
        x1 = _mm(merged, w["w_out"], "nn", F32, "mm_out", res=x)

        h2 = _rms_fwd(x1, g_ffn, "rms_ffn")

        def swiglu(products, extra):
            a_, u_ = products
            return (a_ * _sigmoid(a_)) * u_, a_, u_

        hid, a, u = _mm_fused([(h2, w["w_ffn_gate"], "nt"), (h2, w["w_ffn_up"], "nt")], [], swiglu,
                              [BF16, BF16, BF16], "mm_ffn_gate_up")
        x2 = _mm(hid, w["w_ffn_down"], "nn", F32, "mm_ffn_down", res=x1)

        h3 = _rms_fwd(x2, g_ple, "rms_ple")

        def ple(products, extra):
            z_, e_ = products
            return extra[0] + _sigmoid(z_) * e_, z_, e_

        x3, z, e = _mm_fused([(h3, w["w_ple_gate"], "nn"), (p[l], w["w_ple_proj"], "nn")], [(x2, 0)], ple,
                             [F32, BF16, BF16], "mm_ple")
        saved.append(dict(w=w, x0=x, h=h, proj=proj, qa=qa, ka=ka, qb=qb, kb=kb, vb=vb, ya=ya, lse_a=lse_a,
                          yb=yb, lse_b=lse_b, ca=ca, cb=cb, merged=merged, x1=x1, h2=h2, a=a, u=u, hid=hid,
                          x2=x2, h3=h3, z=z, e=e))
        x = x3

    dx, loss_acc = _loss_grad(x, target)
    loss = loss_acc[0, 0]

    gbig = [{} for _ in range(DEPTH)]
    marks = [{} for _ in range(DEPTH)]
    gsmall = {n: [None] * DEPTH for n in SMALL if n != "rel_table"}
    dbias_a = [[] for _ in DILATED]
    dbias_b = []

    for l in reversed(range(DEPTH)):
        sv = saved[l]
        w = sv["w"]
        g_mix, g_ffn, g_ple = (small[n][l][None, :] for n in ("norm_mix_g", "norm_ffn_g", "norm_ple_g"))
        gqa, gka, gqb = (_tile_gain(small[n][l], N_HEADS) for n in ("qnorm_a_g", "knorm_a_g", "qnorm_b_g"))
        gkb = _tile_gain(small["knorm_b_g"][l], N_KV_B)
        sink = jnp.repeat(small["sink_b"][l], BLK_B)[:, None]

        def ple_bwd(dx_, z_, e_):
            s = _sigmoid(z_.astype(F32))
            return dx_ * s, dx_ * e_.astype(F32) * (s * (1.0 - s))

        de, dz = _ew(ple_bwd, [dx, sv["z"], sv["e"]], [BF16, BF16], "ple_bwd")
        gbig[l]["w_ple_proj"] = _mm(p[l], de, "tn", BF16, "mm_d_ple_proj")
        gbig[l]["w_ple_gate"] = _mm(sv["h3"], dz, "tn", BF16, "mm_d_ple_gate")
        dh3 = _mm(dz, w["w_ple_gate"], "nt", F32, "mm_dh3")
        dx, dxb, gsmall["norm_ple_g"][l] = _rms_bwd(sv["x2"], g_ple, dh3, dx, "rms_ple_bwd")

        gbig[l]["w_ffn_down"] = _mm(sv["hid"], dxb, "tn", BF16, "mm_d_ffn_down")

        def swiglu_bwd(products, extra):
            dh_, a_, u_ = products[0], extra[0].astype(F32), extra[1].astype(F32)
            s = _sigmoid(a_)
            return dh_ * u_ * (s * (1.0 + a_ * (1.0 - s))), dh_ * (a_ * s)

        da, du = _mm_fused([(dxb, w["w_ffn_down"], "nt")], [(sv["a"], 0), (sv["u"], 0)], swiglu_bwd, [BF16, BF16],
                           "mm_dhid_swiglu_bwd")
        gbig[l]["w_ffn_gate"] = _mm(da, sv["h2"], "tn", BF16, "mm_d_ffn_gate")
        gbig[l]["w_ffn_up"] = _mm(du, sv["h2"], "tn", BF16, "mm_d_ffn_up")
        dh2 = _mm(da, w["w_ffn_gate"], "nn", F32, "mm_dh2_gate")
        dh2 = _mm(du, w["w_ffn_up"], "nn", F32, "mm_dh2_up", res=dh2)
        dx, dxb, gsmall["norm_ffn_g"][l] = _rms_bwd(sv["x1"], g_ffn, dh2, dx, "rms_ffn_bwd")

        gbig[l]["w_out"] = _mm(sv["merged"], dxb, "tn", BF16, "mm_d_out")

        def gate_bwd(products, extra):
            dm_, ca_, cb_ = products[0], extra[0].astype(F32), extra[1].astype(F32)
            sa, sb = _sigmoid(extra[2]), _sigmoid(extra[3])
            return dm_ * sa, dm_ * sb, dm_ * ca_ * (sa * (1.0 - sa)), dm_ * cb_ * (sb * (1.0 - sb))

        dca, dcb, dga, dgb = _mm_fused(
            [(dxb, w["w_out"], "nt")], [(sv["ca"], 0), (sv["cb"], 0), (sv["proj"], OFF_GA), (sv["proj"], OFF_GB)],
            gate_bwd, [BF16, BF16, BF16, BF16], "mm_dmerged_gate_bwd")
        gbig[l]["w_branch_a"] = _mm(sv["ya"], dca, "tn", BF16, "mm_d_branch_a")
        gbig[l]["w_branch_b"] = _mm(sv["yb"], dcb, "tn", BF16, "mm_d_branch_b")
        dya = _mm(dca, w["w_branch_a"], "nt", F32, "mm_dya")
        dyb = _mm(dcb, w["w_branch_b"], "nt", F32, "mm_dyb")

        delta_a = _row_dots(dya, sv["ya"], "attn_a_row_dots")
        delta_b = _row_dots(dyb, sv["yb"], "attn_b_row_dots")

        dqa, dka, dva = [], [], []
        for (blk, d), bias, bk in zip(DILATED, bias_a, buckets_a):
            dq_, dk_, dv_, db_ = _attn_bwd(sv["qa"], sv["ka"], sv["proj"], dya, sv["lse_a"], delta_a, bias, None, blk, d,
                                           f"attn_a{d}_bwd", v_col=OFF_VA)
            dqa.append(dq_)
            dka.append(dk_)
            dva.append(dv_)
            dbias_a[len(dqa) - 1].append(db_)
        dqb, dkb, dvb, db_, dsink = _attn_bwd(sv["qb"], sv["kb"], sv["vb"], dyb, sv["lse_b"], delta_b, bias_b, sink,
                                              BLK_B, 1, "attn_b_bwd")
        dbias_b.append(db_)
        gsmall["sink_b"][l] = dsink.reshape(N_HEADS, BLK_B).sum(axis=1)

        dproj, pqa, pka, pqb, pkb = _qknorm_bwd(sv["proj"], gqa, gka, gqb, gkb, dqa, dka, dva, dqb, dkb, dvb, dga, dgb)
        marks[l]["attn_bwd_done"] = dproj
        gsmall["qnorm_a_g"][l] = pqa.reshape(N_HEADS, HEAD_DIM).sum(0)
        gsmall["knorm_a_g"][l] = pka.reshape(N_HEADS, HEAD_DIM).sum(0)
        gsmall["qnorm_b_g"][l] = pqb.reshape(N_HEADS, HEAD_DIM).sum(0)
        gsmall["knorm_b_g"][l] = pkb.reshape(N_KV_B, HEAD_DIM).sum(0)
        gbig[l]["w_in"] = _mm(dproj, sv["h"], "tn", BF16, "mm_d_in")
        dh = _mm(dproj, w["w_in"], "nn", F32, "mm_dh")
        dx, _, gsmall["norm_mix_g"][l] = _rms_bwd(sv["x0"], g_mix, dh, dx, "rms_mix_bwd")
        gsmall["norm_mix_g"][l] = gsmall["norm_mix_g"][l][0]
        gsmall["norm_ffn_g"][l] = gsmall["norm_ffn_g"][l][0]
        gsmall["norm_ple_g"][l] = gsmall["norm_ple_g"][l][0]

    gsmall = {n: jnp.stack(v) for n, v in gsmall.items()}
    dtable_a = sum(_table_grad(sum(dbs).reshape(N_HEADS, blk, 3 * blk), bk, "table_grad_a")
                   for dbs, (blk, _), bk in zip(dbias_a, DILATED, buckets_a))
    dtable_b = _table_grad(sum(dbias_b).reshape(N_HEADS, BLK_B, 3 * BLK_B), buckets_b, "table_grad_b")
    gsmall["rel_table"] = jnp.concatenate([dtable_a, dtable_b], axis=0).T
    return loss, dx, gbig, gsmall, marks


def _place():
    return lax.axis_index("x"), lax.axis_index("y"), lax.axis_index("c")


def _flip(v, bit):
    return 1 - v if bit else v


CHIP_RELATIONS = ((0, 1), (1, 0), (1, 1))
ANY = pl.BlockSpec(memory_space=pl.ANY)


def _allgather_body(w_refs, out_refs, send_sems, recv_sems):
    x, y, c = _place()
    chips = [(_flip(x, a), _flip(y, b)) for a, b in CHIP_RELATIONS]

    def make(g):
        w_ref, out_ref = w_refs[g], out_refs[g]
        half = w_ref.shape[0] // 2

        def part(px, py, pc):
            return out_ref.at[2 * px + py, pl.ds(pc * half, half), :]

        def copy(k, block, to, src=None):
            return pltpu.make_async_remote_copy(
                src_ref=part(*block) if src is None else src, dst_ref=part(*block),
                send_sem=send_sems.at[7 * g + k], recv_sem=recv_sems.at[7 * g + k], device_id=to,
                device_id_type=MESH_ID)

        own = pltpu.make_async_remote_copy(
            src_ref=w_ref, dst_ref=out_ref.at[2 * x + y], send_sem=send_sems.at[7 * g + 6],
            recv_sem=recv_sems.at[7 * g + 6], device_id=(x, y, 1 - c), device_id_type=MESH_ID)
        first = [copy(k, (x, y, c), (*chip, c), src=w_ref.at[pl.ds(c * half, half), :]) for k, chip in enumerate(chips)]
        passed = [copy(3 + k, (*chip, c), (x, y, 1 - c)) for k, chip in enumerate(chips)]
        arrive = [copy(k, (*chip, c), (x, y, c)) for k, chip in enumerate(chips)]
        arrive2 = [copy(3 + k, (*chip, 1 - c), (x, y, c)) for k, chip in enumerate(chips)]
        return own, first, passed, arrive, arrive2

    made = [make(g) for g in range(len(w_refs))]
    for own, first, _, _, _ in made:
        own.start()
        for cp in first:
            cp.start()
    for _, _, passed, arrive, _ in made:
        for k in range(3):
            arrive[k].wait_recv()
            passed[k].start()
    for own, first, passed, _, arrive2 in made:
        for k in range(3):
            arrive2[k].wait_recv()
        own.wait_recv()
        for cp in first + passed + [own]:
            cp.wait_send()


def _sibling(x, y, c):
    return [(x, y, 1 - c)]


def _same_core_of_other_chips(x, y, c):
    return [(_flip(x, a), _flip(y, b), c) for a, b in CHIP_RELATIONS]


def _exchange(body, ins, out_types, n_sems, name, sequencer=None):
    n = len(ins)
    sems = (pltpu.SemaphoreType.DMA((n_sems,)), pltpu.SemaphoreType.DMA((n_sems,)))
    if sequencer is None:
        in_place = out_types is None
        out_shape = [jax.ShapeDtypeStruct(a.shape, a.dtype) for a in ins] if in_place else out_types

        def tc_body(*refs):
            body(refs[:n], refs[n:n + len(out_shape)], refs[-2], refs[-1])

        return list(pl.pallas_call(
            tc_body, out_shape=out_shape, in_specs=[ANY] * n, out_specs=[ANY] * len(out_shape),
            input_output_aliases={g: g for g in range(n)} if in_place else {}, scratch_shapes=list(sems), name=name)(*ins))

    collective_id, peers = sequencer
    hbm = pltpu.MemorySpace.HBM
    in_refs = [jax.new_ref(a, memory_space=hbm) for a in ins]
    out_refs = in_refs if out_types is None else [jax.empty_ref(t, memory_space=hbm) for t in out_types]

    @pl.kernel(mesh=plsc.ScalarSubcoreMesh(axis_name="sequencer", num_cores=1), name=name, scratch_types=sems,
               compiler_params=pltpu.CompilerParams(collective_id=collective_id))
    def launch(send_sems, recv_sems):
        barrier = pltpu.get_barrier_semaphore()
        devices = peers(*_place())
        for device in devices:
            pl.semaphore_signal(barrier, inc=1, device_id=device, device_id_type=MESH_ID)
        pl.semaphore_wait(barrier, len(devices))
        body(in_refs, out_refs, send_sems, recv_sems)

    launch()
    return [r[...] for r in out_refs]


def _allgather(shards, name, sequencer=None):
    out_types = [jax.ShapeDtypeStruct((N_CHIPS,) + s.shape, s.dtype) for s in shards]
    if sequencer is not None:
        sequencer = (sequencer, lambda x, y, c: _sibling(x, y, c) + _same_core_of_other_chips(x, y, c))
    return _exchange(_allgather_body, shards, out_types, 7 * len(shards), name, sequencer)


def _half_tile(half):
    return max(t for t in range(16, 1025, 16) if half % t == 0)


def _run_copies(cps):
    for cp in cps:
        cp.start()
    for cp in cps:
        cp.wait_recv()
    for cp in cps:
        cp.wait_send()


def _sibling_halves(gsends, name, sequencer=None):
    def body(g_refs, out_refs, send_sems, recv_sems):
        x, y, c = _place()
        cps = []
        for g, (g_ref, out_ref) in enumerate(zip(g_refs, out_refs)):
            half = g_ref.shape[1] // 2
            cps.append(pltpu.make_async_remote_copy(
                src_ref=g_ref.at[:, pl.ds((1 - c) * half, half), :], dst_ref=out_ref,
                send_sem=send_sems.at[g], recv_sem=recv_sems.at[g], device_id=(x, y, 1 - c), device_id_type=MESH_ID))
        _run_copies(cps)

    out_types = [jax.ShapeDtypeStruct((s.shape[0], s.shape[1] // 2, s.shape[2]), s.dtype) for s in gsends]
    return _exchange(body, gsends, out_types, len(gsends), name, sequencer and (sequencer, _sibling))


def _chip_sums(gsend, sib, place):
    n, rows, cols = gsend.shape
    half = rows // 2
    tm = _half_tile(half)
    nblk = half // tm

    def body(s_ref, g_ref, sib_ref, o_ref):
        o_ref[0] = (g_ref[0].astype(F32) + sib_ref[0].astype(F32)).astype(o_ref.dtype)

    grid_spec = pltpu.PrefetchScalarGridSpec(
        num_scalar_prefetch=1, grid=(n, nblk),
        in_specs=[pl.BlockSpec((1, tm, cols), lambda k, i, s: (jnp.bitwise_xor(s[0], k), s[1] * nblk + i, 0)),
                  pl.BlockSpec((1, tm, cols), lambda k, i, s: (jnp.bitwise_xor(s[0], k), i, 0))],
        out_specs=pl.BlockSpec((1, tm, cols), lambda k, i, s: (k, i, 0)))
    return pl.pallas_call(
        body, out_shape=jax.ShapeDtypeStruct((n, half, cols), BF16), grid_spec=grid_spec,
        name="rs_chip_sums", compiler_params=_params("parallel", "parallel"))(place, gsend, sib)


def _exchange_chip_sums(tsends, name, sequencer=None):
    def body(t_refs, out_refs, send_sems, recv_sems):
        x, y, c = _place()
        cps = []
        for g, (t_ref, out_ref) in enumerate(zip(t_refs, out_refs)):
            for k, device in enumerate(_same_core_of_other_chips(x, y, c)):
                cps.append(pltpu.make_async_remote_copy(
                    src_ref=t_ref.at[k + 1], dst_ref=out_ref.at[k], send_sem=send_sems.at[3 * g + k],
                    recv_sem=recv_sems.at[3 * g + k], device_id=device, device_id_type=MESH_ID))
        _run_copies(cps)

    out_types = [jax.ShapeDtypeStruct((3,) + s.shape[1:], s.dtype) for s in tsends]
    return _exchange(body, tsends, out_types, 3 * len(tsends), name,
                     sequencer and (sequencer, _same_core_of_other_chips))


def _final_sum(tsend, recv, place):
    n, half, cols = tsend.shape
    tm = _half_tile(half)
    nblk = half // tm

    def body(s_ref, t_ref, r_ref, o_ref):
        o_ref[...] = ((t_ref[0].astype(F32) + r_ref[0].astype(F32)) + r_ref[1].astype(F32)) + r_ref[2].astype(F32)

    grid_spec = pltpu.PrefetchScalarGridSpec(
        num_scalar_prefetch=1, grid=(nblk,),
        in_specs=[pl.BlockSpec((1, tm, cols), lambda i, s: (0, i, 0)), pl.BlockSpec((n - 1, tm, cols), lambda i, s: (0, i, 0))],
        out_specs=pl.BlockSpec((tm, cols), lambda i, s: (s[1] * nblk + i, 0)))
    return pl.pallas_call(
        body, out_shape=jax.ShapeDtypeStruct((2 * half, cols), F32), grid_spec=grid_spec, name="rs_final_sum",
        compiler_params=_params("parallel"))(place, tsend, recv)


def _join_halves(gfulls, name, sequencer=None):
    def body(g_refs, out_refs, send_sems, recv_sems):
        x, y, c = _place()
        n = len(g_refs)

        def copy(g, pc):
            half = g_refs[g].shape[0] // 2
            return pltpu.make_async_remote_copy(
                src_ref=g_refs[g].at[pl.ds(pc * half, half), :], dst_ref=out_refs[g].at[pl.ds(pc * half, half), :],
                send_sem=send_sems.at[g], recv_sem=recv_sems.at[g], device_id=(x, y, 1 - c), device_id_type=MESH_ID)

        mine = [copy(g, c) for g in range(n)]
        for cp in mine:
            cp.start()
        for g in range(n):
            copy(g, 1 - c).wait_recv()
        for cp in mine:
            cp.wait_send()

    return _exchange(body, gfulls, None, len(gfulls), name, sequencer and (sequencer, _sibling))


def _allreduce_small(v):
    rows, cols = v.shape

    def body(v_ref, out_ref, buf, send_sems, recv_sems):
        x, y, c = _place()
        cps = []
        for k in range(1, 8):
            peer = (_flip(x, (k >> 2) & 1), _flip(y, (k >> 1) & 1), _flip(c, k & 1))
            cps.append(pltpu.make_async_remote_copy(
                src_ref=v_ref, dst_ref=buf.at[k - 1], send_sem=send_sems.at[k - 1], recv_sem=recv_sems.at[k - 1],
                device_id=peer, device_id_type=MESH_ID))
        for cp in cps:
            cp.start()
        for cp in cps:
            cp.wait_recv()
        for cp in cps:
            cp.wait_send()
        t0 = v_ref[...] + buf[0]
        t1 = buf[1] + buf[2]
        t2 = buf[3] + buf[4]
        t3 = buf[5] + buf[6]
        out_ref[...] = (t0 + t1) + (t2 + t3)

    vm = pl.BlockSpec(memory_space=pltpu.VMEM)
    return pl.pallas_call(
        body, out_shape=jax.ShapeDtypeStruct((rows, cols), F32), in_specs=[vm], out_specs=vm,
        scratch_shapes=[pltpu.VMEM((7, rows, cols), F32), pltpu.SemaphoreType.DMA((7,)), pltpu.SemaphoreType.DMA((7,))],
        name="allreduce_small")(v)


BIG_INFO = {n: (shape, ax) for n, shape, ax in BIG}
GROUPS = (("w_in",), ("w_ffn_gate", "w_ffn_up", "w_ffn_down", "w_out", "w_ple_gate"),
          ("w_branch_a", "w_branch_b", "w_ple_proj"))


def _shard_shape(name):
    (k, m), ax = BIG_INFO[name]
    return (k // N_CHIPS, m) if ax == 0 else (k, m // N_CHIPS)


def _group_rows(group):
    offs, off = {}, 0
    for n in group:
        offs[n] = off
        off += _shard_shape(n)[0]
    return offs, off


def _pack_groups(shards, layer, dtype):
    return [jnp.concatenate([shards[n][layer].astype(dtype) for n in group], axis=0) for group in GROUPS]


def _unpack_full(gathered, groups):
    out = {}
    for group, arr in zip(groups, gathered):
        offs, _ = _group_rows(group)
        for n in group:
            rows, cols = _shard_shape(n)
            (k, m), ax = BIG_INFO[n]
            slab = arr[:, offs[n]:offs[n] + rows]
            out[n] = slab.reshape(k, m) if ax == 0 else jnp.transpose(slab, (1, 0, 2)).reshape(k, m)
    return out


def _pack_grads(gfull):
    out = []
    for group in GROUPS:
        parts = []
        for n in group:
            rows, cols = _shard_shape(n)
            ax = BIG_INFO[n][1]
            slab = (gfull[n].reshape(N_CHIPS, rows, cols) if ax == 0
                    else jnp.transpose(gfull[n].reshape(rows, N_CHIPS, cols), (1, 0, 2)))
            parts.append(slab)
        out.append(jnp.concatenate(parts, axis=1))
    return out


def _after(values, mark):
    values, _ = lax.optimization_barrier((values, mark))
    return values


def _reduce_scatter_begin(gsends, place, tag, ids):
    sibs = _sibling_halves(gsends, "rs_sibling_halves_" + tag, ids[0])
    tsends = [_chip_sums(g, s, place) for g, s in zip(gsends, sibs)]
    return tsends, _exchange_chip_sums(tsends, "rs_exchange_" + tag, ids[1])


def _reduce_scatter_finish(begun, place, tag, ids, hold):
    tsends, recvs = begun
    recvs = _after(recvs, hold)
    return _join_halves([_final_sum(t, r, place) for t, r in zip(tsends, recvs)], "rs_join_halves_" + tag, ids[2])


SMALL_SHAPES = {"rel_table": (NUM_BUCKETS, 2 * N_HEADS), "norm_mix_g": (DEPTH, D_MODEL), "qnorm_a_g": (DEPTH, HEAD_DIM),
                "knorm_a_g": (DEPTH, HEAD_DIM), "qnorm_b_g": (DEPTH, HEAD_DIM), "knorm_b_g": (DEPTH, HEAD_DIM),
                "sink_b": (DEPTH, N_HEADS), "norm_ffn_g": (DEPTH, D_MODEL), "norm_ple_g": (DEPTH, D_MODEL)}


def _pack_small(vals, last=None):
    flat = jnp.concatenate([vals[n].astype(F32).reshape(-1) for n in SMALL])
    tail = jnp.zeros((SMALL_ROWS * LANES - flat.shape[0],), F32)
    if last is not None:
        tail = tail.at[-1].set(last)
    return jnp.concatenate([flat, tail]).reshape(SMALL_ROWS, LANES)


def _unpack_small(packed):
    flat, out, off = packed.reshape(-1), {}, 0
    for n in SMALL:
        size = math.prod(SMALL_SHAPES[n])
        out[n] = flat[off:off + size].reshape(SMALL_SHAPES[n])
        off += size
    return out


def _adamw(w, gs, g_row, m, v, name):
    c1 = 1.0 - ADAM_B1 ** ADAM_STEP
    c2 = 1.0 - ADAM_B2 ** ADAM_STEP
    total, width = w.shape
    n_layers = len(gs)
    per = total // n_layers
    tm = max(t for t in range(8, 513, 8) if per % t == 0 and g_row % t == 0)
    nblk = per // tm

    def body(*refs):
        w_ref, g_refs = refs[0], refs[1:1 + n_layers]
        m_ref, v_ref, og, od, om, ov = refs[1 + n_layers:]
        layer = pl.program_id(0) // nblk
        g = g_refs[0][...]
        for l in range(1, n_layers):
            g = jnp.where(layer == l, g_refs[l][...], g)
        m_new = ADAM_B1 * m_ref[...] + (1.0 - ADAM_B1) * g
        v_new = ADAM_B2 * v_ref[...] + (1.0 - ADAM_B2) * (g * g)
        og[...] = g
        od[...] = -ADAM_LR * ((m_new / c1) / (jnp.sqrt(v_new / c2) + ADAM_EPS) + ADAM_WD * w_ref[...])
        om[...] = m_new
        ov[...] = v_new

    row = pl.BlockSpec((tm, width), lambda i: (i, 0))
    g_specs = [pl.BlockSpec((tm, width), lambda i, l=l: (g_row // tm + jnp.clip(i - l * nblk, 0, nblk - 1), 0))
               for l in range(n_layers)]
    return pl.pallas_call(
        body, out_shape=[jax.ShapeDtypeStruct((total, width), F32)] * 4, grid=(total // tm,),
        in_specs=[row] + g_specs + [row, row], out_specs=[row] * 4, name=name,
        compiler_params=_params("parallel"))(w, *gs, m, v)


def kernel(x, p, rel_table, norm_mix_g, w_in, qnorm_a_g, knorm_a_g, qnorm_b_g, knorm_b_g, sink_b, w_branch_a, w_branch_b, w_out, norm_ffn_g, w_ffn_gate, w_ffn_up, w_ffn_down, norm_ple_g, w_ple_gate, w_ple_proj, loss_target, m_rel_table, m_norm_mix_g, m_w_in, m_qnorm_a_g, m_knorm_a_g, m_qnorm_b_g, m_knorm_b_g, m_sink_b, m_w_branch_a, m_w_branch_b, m_w_out, m_norm_ffn_g, m_w_ffn_gate, m_w_ffn_up, m_w_ffn_down, m_norm_ple_g, m_w_ple_gate, m_w_ple_proj, v_rel_table, v_norm_mix_g, v_w_in, v_qnorm_a_g, v_knorm_a_g, v_qnorm_b_g, v_knorm_b_g, v_sink_b, v_w_branch_a, v_w_branch_b, v_w_out, v_norm_ffn_g, v_w_ffn_gate, v_w_ffn_up, v_w_ffn_down, v_norm_ple_g, v_w_ple_gate, v_w_ple_proj):
    given = dict(locals())

    def held(name, a):
        return jnp.swapaxes(a, 1, 2) if name in TRANSPOSED else a

    weights = {n: held(n, given[n]) for n in WEIGHTS}
    moments_m = {n: held(n, given["m_" + n]) for n in WEIGHTS}
    moments_v = {n: held(n, given["v_" + n]) for n in WEIGHTS}
    xi, yi, ci = _place()
    place = jnp.stack([2 * xi + yi, ci]).astype(jnp.int32)

    shards = [_pack_groups(weights, l, BF16) for l in range(DEPTH)]
    w_in0 = _allgather(shards[0][:1], "allgather_w_in_layer0", sequencer=9)
    rest0 = _allgather(_after(shards[0][1:], w_in0), "allgather_rest_layer0", sequencer=1)
    gathered = [w_in0 + rest0, None]
    small = {n: weights[n] for n in SMALL}

    def w_in_of(l, mark):
        return _unpack_full(_after(gathered[l][:1], shards[1] if l == 0 else mark), GROUPS[:1])["w_in"]

    def rest_of(l, mark):
        if l == 0:
            gathered[1] = _allgather(_after(shards[1], mark), "allgather_layer1", sequencer=2)
        return _unpack_full(_after(gathered[l][1:], mark), GROUPS[1:])

    loss, dx, gbig, gsmall, marks = _local_step(x[0], p[:, 0], loss_target[0], w_in_of, rest_of, small)

    gsends = [_pack_grads(gbig[l]) for l in range(DEPTH)]
    stages = {"layer1": (gsends[1], (3, 4, 5)), "rest_layer0": (gsends[0][1:], (6, 7, 8)),
              "w_in_layer0": (gsends[0][:1], (10, 11, 12))}
    begun = {tag: _reduce_scatter_begin(g, place, tag, ids) for tag, (g, ids) in stages.items()}

    def finish(tag, hold):
        return _reduce_scatter_finish(begun[tag], place, tag, stages[tag][1], hold)

    red1 = finish("layer1", marks[0]["attn_bwd_done"])
    rest0 = finish("rest_layer0", marks[0]["attn_bwd_done"])

    grads, delta, new_m, new_v = {}, {}, {}, {}

    def update(group, reduced):
        offs, _ = _group_rows(group)
        for n in group:
            shape = weights[n].shape
            two_d = lambda a: a.reshape(shape[0] * shape[1], shape[2])
            outs = _adamw(two_d(weights[n]), reduced, offs[n], two_d(moments_m[n]), two_d(moments_v[n]), "adamw_" + n)
            grads[n], delta[n], new_m[n], new_v[n] = (held(n, o.reshape(shape)) for o in outs)

    for gi in (1, 2):
        update(GROUPS[gi], _after([rest0[gi - 1], red1[gi]], begun["w_in_layer0"][0]))
    others_done = [dx] + [delta[n] for gi in (1, 2) for n in GROUPS[gi]]
    update(GROUPS[0], [finish("w_in_layer0", others_done)[0], red1[0]])
    small_grads = _allreduce_small(_pack_small(gsmall, last=loss))
    g_, d_, m_, v_ = _adamw(_pack_small(weights), [small_grads], 0, _pack_small(moments_m), _pack_small(moments_v),
                            "adamw_small")
    grads.update(_unpack_small(g_))
    delta.update(_unpack_small(d_))
    new_m.update(_unpack_small(m_))
    new_v.update(_unpack_small(v_))

    return (small_grads[-1, -1], dx[None], *[grads[n] for n in WEIGHTS], *[delta[n] for n in WEIGHTS],
            *[new_m[n] for n in WEIGHTS], *[new_v[n] for n in WEIGHTS])
```

```python
import functools
import math

import jax
import jax.numpy as jnp
from jax import lax
from jax.experimental import pallas as pl
from jax.experimental.pallas import tpu as pltpu
from jax.experimental.pallas import tpu_sc as plsc

F32 = jnp.float32
BF16 = jnp.bfloat16
MESH_ID = pl.DeviceIdType.MESH

D_MODEL = 1024
DEPTH = 2
HEAD_DIM = 64
N_HEADS = 8
WIDTH = N_HEADS * HEAD_DIM
N_PAIRS = 4
ITEMS = 8
N_KV_B = 2
PLE_DIM = 256
D_FF = 2816
D_IN = 4352
OFF_QA, OFF_KA, OFF_VA, OFF_QB, OFF_KB, OFF_VB, OFF_GA, OFF_GB = 0, 512, 1024, 1536, 2048, 2176, 2304, 3328
DILATED = ((64, 1), (64, 4), (64, 16))
BLK_B = 128
NUM_BUCKETS = 32
MAX_DISTANCE = 1024
RMS_EPS = 1e-6
NEG_INF = -1e30
LANES = 128
ROW_TILE = 256
VMEM_LIMIT = 48 * 1024 * 1024

ADAM_LR, ADAM_B1, ADAM_B2, ADAM_EPS, ADAM_WD, ADAM_STEP = 0.001, 0.9, 0.999, 1e-08, 0.01, 10

TRANSPOSED = ("w_in", "w_ffn_gate", "w_ffn_up")
BIG = (
    ("w_in", (D_IN, D_MODEL), 0),
    ("w_branch_a", (WIDTH, D_MODEL), 1),
    ("w_branch_b", (WIDTH, D_MODEL), 1),
    ("w_out", (D_MODEL, D_MODEL), 0),
    ("w_ffn_gate", (D_FF, D_MODEL), 0),
    ("w_ffn_up", (D_FF, D_MODEL), 0),
    ("w_ffn_down", (D_FF, D_MODEL), 0),
    ("w_ple_gate", (D_MODEL, D_MODEL), 0),
    ("w_ple_proj", (PLE_DIM, D_MODEL), 1),
)
SMALL = ("rel_table", "norm_mix_g", "qnorm_a_g", "knorm_a_g", "qnorm_b_g", "knorm_b_g", "sink_b",
         "norm_ffn_g", "norm_ple_g")
WEIGHTS = ("rel_table", "norm_mix_g", "w_in", "qnorm_a_g", "knorm_a_g", "qnorm_b_g", "knorm_b_g", "sink_b",
           "w_branch_a", "w_branch_b", "w_out", "norm_ffn_g", "w_ffn_gate", "w_ffn_up", "w_ffn_down",
           "norm_ple_g", "w_ple_gate", "w_ple_proj")
N_CHIPS = 4
SMALL_ROWS = 64


def _params(*sem):
    return pltpu.CompilerParams(dimension_semantics=sem, vmem_limit_bytes=VMEM_LIMIT)


MM_VMEM_BUDGET = 40 * 1024 * 1024
STEP_OVERHEAD_S = 0.4e-6
TILE_DMA_BYTES_PER_S = 1.5e12


def _mm_dims(a, b, mode):
    if mode == "nn":
        return a.shape[0], b.shape[1], a.shape[1]
    if mode == "nt":
        return a.shape[0], b.shape[0], a.shape[1]
    return a.shape[1], b.shape[1], a.shape[0]


def _mm_tiles(m, n, pairs, tile_bytes, col_offsets):
    best = None
    for tm in (t for t in range(LANES, m + 1, LANES) if m % t == 0):
        for tn in (t for t in range(LANES, n + 1, LANES) if n % t == 0 and all(o % t == 0 for o in col_offsets)):
            io = sum(tm * k * ab + tn * k * bb for k, ab, bb in pairs) + tm * tn * sum(tile_bytes)
            casts = sum((tm * k * 2 if ab == 4 else 0) + (tn * k * 2 if bb == 4 else 0) for k, ab, bb in pairs)
            if 2 * io + len(pairs) * tm * tn * 4 + casts > MM_VMEM_BUDGET:
                continue
            cost = (m // tm) * (n // tn) * STEP_OVERHEAD_S + io / TILE_DMA_BYTES_PER_S
            if best is None or (cost, -tm) < best[0]:
                best = ((cost, -tm), tm, tn)
    return best[1], best[2]


def _mm_fused(pairs, extras, epilogue, out_dtypes, name):
    m, n, _ = _mm_dims(*pairs[0])
    assert all(_mm_dims(*p)[:2] == (m, n) for p in pairs)
    tm, tn = _mm_tiles(
        m, n, [(_mm_dims(a, b, mode)[2], a.dtype.itemsize, b.dtype.itemsize) for a, b, mode in pairs],
        [e.dtype.itemsize for e, _ in extras] + [jnp.dtype(d).itemsize for d in out_dtypes], [off for _, off in extras])
    dims = {"nn": (((1,), (0,)), ((), ())), "nt": (((1,), (1,)), ((), ())), "tn": (((0,), (0,)), ((), ()))}
    in_specs, args = [], []
    for a, b, mode in pairs:
        k = _mm_dims(a, b, mode)[2]
        in_specs.append(pl.BlockSpec((k, tm), lambda i, j: (0, i)) if mode == "tn" else pl.BlockSpec((tm, k), lambda i, j: (i, 0)))
        in_specs.append(pl.BlockSpec((tn, k), lambda i, j: (j, 0)) if mode == "nt" else pl.BlockSpec((k, tn), lambda i, j: (0, j)))
        args += [a, b]
    for e, off in extras:
        in_specs.append(pl.BlockSpec((tm, tn), lambda i, j, o=off // tn: (i, o + j)))
        args.append(e)
    n_pairs, n_in = len(pairs), 2 * len(pairs) + len(extras)

    def body(*refs):
        products = [lax.dot_general(refs[2 * p][...].astype(BF16), refs[2 * p + 1][...].astype(BF16), dims[pairs[p][2]],
                                    preferred_element_type=F32) for p in range(n_pairs)]
        outs = epilogue(products, [r[...] for r in refs[2 * n_pairs:n_in]])
        for r, o in zip(refs[n_in:], outs):
            r[...] = o.astype(r.dtype)

    tile = pl.BlockSpec((tm, tn), lambda i, j: (i, j))
    return pl.pallas_call(
        body, out_shape=[jax.ShapeDtypeStruct((m, n), d) for d in out_dtypes], grid=(m // tm, n // tn),
        in_specs=in_specs, out_specs=[tile] * len(out_dtypes), name=name,
        compiler_params=_params("parallel", "parallel"))(*args)


def _mm(a, b, mode, out_dtype, name, res=None):
    if res is None:
        return _mm_fused([(a, b, mode)], [], lambda products, extra: products, [out_dtype], name)[0]
    return _mm_fused([(a, b, mode)], [(res, 0)], lambda products, extra: [products[0] + extra[0]], [out_dtype], name)[0]


def _ew(fn, ins, out_dtypes, name):
    rows, width = ins[0].shape
    n_in = len(ins)

    def body(*refs):
        outs = fn(*[r[...] for r in refs[:n_in]])
        for r, o in zip(refs[n_in:], outs):
            r[...] = o.astype(r.dtype)

    row = pl.BlockSpec((ROW_TILE, width), lambda i: (i, 0))
    return pl.pallas_call(
        body, out_shape=[jax.ShapeDtypeStruct((rows, width), dt) for dt in out_dtypes], grid=(rows // ROW_TILE,),
        in_specs=[row] * n_in, out_specs=[row] * len(out_dtypes), name=name, compiler_params=_params("parallel"))(*ins)


def _sigmoid(x):
    return 1.0 / (1.0 + jnp.exp(-x))


def _seg_sum(v):
    outs = []
    for k in range(v.shape[1] // LANES):
        vp = v[:, k * LANES:(k + 1) * LANES]
        left = lax.broadcasted_iota(jnp.int32, vp.shape, 1) < HEAD_DIM
        sl = jnp.sum(jnp.where(left, vp, 0.0), axis=-1, keepdims=True)
        sr = jnp.sum(jnp.where(left, 0.0, vp), axis=-1, keepdims=True)
        outs.append(jnp.where(left, sl, sr))
    return outs[0] if len(outs) == 1 else jnp.concatenate(outs, axis=1)


def _seg_rstd(x):
    return lax.rsqrt(_seg_sum(x * x) * (1.0 / HEAD_DIM) + RMS_EPS)


def _rms_fwd(x, g, name):
    rows, d = x.shape
    tm = ROW_TILE

    def body(x_ref, g_ref, h_ref):
        xv = x_ref[...]
        r = lax.rsqrt(jnp.mean(xv * xv, axis=-1, keepdims=True) + RMS_EPS)
        h_ref[...] = ((xv * r) * g_ref[...]).astype(BF16)

    return pl.pallas_call(
        body, out_shape=jax.ShapeDtypeStruct((rows, d), BF16), grid=(rows // tm,),
        in_specs=[pl.BlockSpec((tm, d), lambda i: (i, 0)), pl.BlockSpec((1, d), lambda i: (0, 0))],
        out_specs=pl.BlockSpec((tm, d), lambda i: (i, 0)), name=name,
        compiler_params=_params("parallel"))(x, g)


def _rms_bwd(x, g, dh, dres, name):
    rows, d = x.shape
    tm = ROW_TILE

    def body(x_ref, g_ref, dh_ref, dres_ref, dx_ref, dxb_ref, dg_ref):
        xv = x_ref[...]
        r = lax.rsqrt(jnp.mean(xv * xv, axis=-1, keepdims=True) + RMS_EPS)
        xh = xv * r
        dhv = dh_ref[...]
        dxh = dhv * g_ref[...]
        dxv = dres_ref[...] + r * (dxh - xh * jnp.mean(dxh * xh, axis=-1, keepdims=True))
        dx_ref[...] = dxv
        dxb_ref[...] = dxv.astype(BF16)
        part = jnp.sum(dhv * xh, axis=0, keepdims=True)

        @pl.when(pl.program_id(0) == 0)
        def _():
            dg_ref[...] = part

        @pl.when(pl.program_id(0) > 0)
        def _():
            dg_ref[...] += part

    row = pl.BlockSpec((tm, d), lambda i: (i, 0))
    vec = pl.BlockSpec((1, d), lambda i: (0, 0))
    return pl.pallas_call(
        body, out_shape=[jax.ShapeDtypeStruct((rows, d), F32), jax.ShapeDtypeStruct((rows, d), BF16),
                         jax.ShapeDtypeStruct((1, d), F32)],
        grid=(rows // tm,), in_specs=[row, vec, row, row], out_specs=[row, row, vec],
        name=name, compiler_params=_params("arbitrary"))(x, g, dh, dres)


def _loss_grad(y, t):
    rows, d = y.shape
    tm = ROW_TILE

    def body(y_ref, t_ref, dy_ref, l_ref):
        e = y_ref[...] - t_ref[...]
        dy_ref[...] = e * (1.0 / d)
        part = jnp.zeros((1, LANES), F32) + jnp.sum(e * e) * (0.5 / d)

        @pl.when(pl.program_id(0) == 0)
        def _():
            l_ref[...] = part

        @pl.when(pl.program_id(0) > 0)
        def _():
            l_ref[...] += part

    row = pl.BlockSpec((tm, d), lambda i: (i, 0))
    return pl.pallas_call(
        body, out_shape=[jax.ShapeDtypeStruct((rows, d), F32), jax.ShapeDtypeStruct((1, LANES), F32)],
        grid=(rows // tm,), in_specs=[row, row], out_specs=[row, pl.BlockSpec((1, LANES), lambda i: (0, 0))],
        name="loss_grad", compiler_params=_params("arbitrary"))(y, t)


def _swap_halves(v):
    return pltpu.roll(v, HEAD_DIM, axis=1)


def _expand_kv(kv):
    left = lax.broadcasted_iota(jnp.int32, kv.shape, 1) < HEAD_DIM
    sw = _swap_halves(kv)
    h0 = jnp.where(left, kv, sw)
    h1 = jnp.where(left, sw, kv)
    return jnp.concatenate([h0, h0, h1, h1], axis=1)


def _reduce_kv(dkv):
    left = lax.broadcasted_iota(jnp.int32, (dkv.shape[0], LANES), 1) < HEAD_DIM
    t = dkv[:, 0:LANES] + dkv[:, LANES:2 * LANES]
    u = dkv[:, 2 * LANES:3 * LANES] + dkv[:, 3 * LANES:4 * LANES]
    t = t + _swap_halves(t)
    u = u + _swap_halves(u)
    return jnp.where(left, t, u)


def _qknorm_fwd(proj, gqa, gka, gqb, gkb):
    rows = proj.shape[0]
    tm = ROW_TILE

    def body(qa_ref, ka_ref, qb_ref, kb_ref, vb_ref, gqa_ref, gka_ref, gqb_ref, gkb_ref, oqa, oka, oqb, okb, ovb):
        for src, g_ref, dst in ((qa_ref, gqa_ref, oqa), (ka_ref, gka_ref, oka), (qb_ref, gqb_ref, oqb)):
            xv = src[...]
            dst[...] = (xv * _seg_rstd(xv)) * g_ref[...]
        kv = kb_ref[...]
        okb[...] = _expand_kv((kv * _seg_rstd(kv)) * gkb_ref[...])
        ovb[...] = _expand_kv(vb_ref[...])

    def win(width, off):
        return pl.BlockSpec((tm, width), lambda i: (i, off // width))

    vec = lambda w: pl.BlockSpec((1, w), lambda i: (0, 0))
    out = pl.BlockSpec((tm, WIDTH), lambda i: (i, 0))
    return pl.pallas_call(
        body, out_shape=[jax.ShapeDtypeStruct((rows, WIDTH), F32)] * 5, grid=(rows // tm,),
        in_specs=[win(WIDTH, OFF_QA), win(WIDTH, OFF_KA), win(WIDTH, OFF_QB), win(LANES, OFF_KB), win(LANES, OFF_VB),
                  vec(WIDTH), vec(WIDTH), vec(WIDTH), vec(LANES)],
        out_specs=[out] * 5, name="qknorm_fwd", compiler_params=_params("parallel"))(
            proj, proj, proj, proj, proj, gqa, gka, gqb, gkb)


def _norm_bwd(xv, g, dy):
    r = _seg_rstd(xv)
    xh = xv * r
    dxh = dy * g
    dx = r * (dxh - xh * (_seg_sum(dxh * xh) * (1.0 / HEAD_DIM)))
    return dx, jnp.sum(dy * xh, axis=0, keepdims=True)


def _qknorm_bwd(proj, gqa, gka, gqb, gkb, dqa, dka, dva, dqb, dkb, dvb, dga, dgb):
    rows = proj.shape[0]
    tm = ROW_TILE
    n_a = len(dqa)

    def body(*refs):
        qa_ref, ka_ref, qb_ref, kb_ref, gqa_ref, gka_ref, gqb_ref, gkb_ref = refs[:8]
        pos = 8
        dqa_refs, dka_refs, dva_refs = refs[pos:pos + n_a], refs[pos + n_a:pos + 2 * n_a], refs[pos + 2 * n_a:pos + 3 * n_a]
        pos += 3 * n_a
        dqb_ref, dkb_ref, dvb_ref, dga_ref, dgb_ref = refs[pos:pos + 5]
        dproj_ref, ogqa, ogka, ogqb, ogkb = refs[pos + 5:]

        def total(rs):
            acc = rs[0][...]
            for r in rs[1:]:
                acc = acc + r[...]
            return acc

        dx_qa, p_qa = _norm_bwd(qa_ref[...], gqa_ref[...], total(dqa_refs))
        dx_ka, p_ka = _norm_bwd(ka_ref[...], gka_ref[...], total(dka_refs))
        dx_qb, p_qb = _norm_bwd(qb_ref[...], gqb_ref[...], dqb_ref[...])
        dx_kb, p_kb = _norm_bwd(kb_ref[...], gkb_ref[...], _reduce_kv(dkb_ref[...]))
        dproj_ref[:, OFF_QA:OFF_QA + WIDTH] = dx_qa.astype(BF16)
        dproj_ref[:, OFF_KA:OFF_KA + WIDTH] = dx_ka.astype(BF16)
        dproj_ref[:, OFF_VA:OFF_VA + WIDTH] = total(dva_refs).astype(BF16)
        dproj_ref[:, OFF_QB:OFF_QB + WIDTH] = dx_qb.astype(BF16)
        dproj_ref[:, OFF_KB:OFF_KB + LANES] = dx_kb.astype(BF16)
        dproj_ref[:, OFF_VB:OFF_VB + LANES] = _reduce_kv(dvb_ref[...]).astype(BF16)
        dproj_ref[:, OFF_GA:OFF_GB] = dga_ref[...]
        dproj_ref[:, OFF_GB:D_IN] = dgb_ref[...]
        first = pl.program_id(0) == 0
        for o_ref, part in ((ogqa, p_qa), (ogka, p_ka), (ogqb, p_qb), (ogkb, p_kb)):
            @pl.when(first)
            def _(o_ref=o_ref, part=part):
                o_ref[...] = part

            @pl.when(jnp.logical_not(first))
            def _(o_ref=o_ref, part=part):
                o_ref[...] += part

    def win(width, off):
        return pl.BlockSpec((tm, width), lambda i: (i, off // width))

    vec = lambda w: pl.BlockSpec((1, w), lambda i: (0, 0))
    row = lambda w: pl.BlockSpec((tm, w), lambda i: (i, 0))
    in_specs = [win(WIDTH, OFF_QA), win(WIDTH, OFF_KA), win(WIDTH, OFF_QB), win(LANES, OFF_KB),
                vec(WIDTH), vec(WIDTH), vec(WIDTH), vec(LANES)]
    in_specs += [row(WIDTH)] * (3 * n_a + 3) + [row(D_MODEL)] * 2
    return pl.pallas_call(
        body,
        out_shape=[jax.ShapeDtypeStruct((rows, D_IN), BF16), jax.ShapeDtypeStruct((1, WIDTH), F32),
                   jax.ShapeDtypeStruct((1, WIDTH), F32), jax.ShapeDtypeStruct((1, WIDTH), F32),
                   jax.ShapeDtypeStruct((1, LANES), F32)],
        grid=(rows // tm,), in_specs=in_specs,
        out_specs=[row(D_IN), vec(WIDTH), vec(WIDTH), vec(WIDTH), vec(LANES)],
        name="qknorm_bwd", compiler_params=_params("arbitrary"))(
            proj, proj, proj, proj, gqa, gka, gqb, gkb, *dqa, *dka, *dva, dqb, dkb, dvb, dga, dgb)


def _t5_bucket(rel):
    half_b = NUM_BUCKETS // 2
    max_exact = half_b // 2
    sign = jnp.where(rel > 0, half_b, 0)
    n = jnp.abs(rel)
    nf = jnp.maximum(n, 1).astype(F32)
    large = max_exact + (jnp.log(nf / max_exact) / math.log(MAX_DISTANCE / max_exact)
                         * (half_b - max_exact)).astype(jnp.int32)
    large = jnp.minimum(large, half_b - 1)
    return sign + jnp.where(n < max_exact, n, large)


def _band_buckets(blk, dilation):
    i = jnp.arange(blk, dtype=jnp.int32)[:, None]
    j = jnp.arange(3 * blk, dtype=jnp.int32)[None, :]
    rel = j - blk - i
    return jnp.where(jnp.abs(rel) <= blk, _t5_bucket(rel * dilation), -1)


def _bias_tiles(table, buckets, head_off, name):
    blk = buckets.shape[0]

    def body(tab_ref, bk_ref, o_ref):
        h = pl.program_id(0) + head_off
        bk = bk_ref[...]
        acc = jnp.full(bk.shape, NEG_INF, F32)
        for b in range(NUM_BUCKETS):
            acc = jnp.where(bk == b, tab_ref[b, h], acc)
        o_ref[0] = acc

    return pl.pallas_call(
        body, out_shape=jax.ShapeDtypeStruct((N_HEADS, blk, 3 * blk), F32), grid=(N_HEADS,),
        in_specs=[pl.BlockSpec(memory_space=pltpu.SMEM), pl.BlockSpec((blk, 3 * blk), lambda h: (0, 0))],
        out_specs=pl.BlockSpec((1, blk, 3 * blk), lambda h: (h, 0, 0)),
        name=name, compiler_params=_params("parallel"))(table, buckets)


def _table_grad(dbias, buckets, name):
    blk = buckets.shape[0]

    def body(db_ref, bk_ref, o_ref):
        bk = bk_ref[...]
        dbv = db_ref[0]
        lane = lax.broadcasted_iota(jnp.int32, (1, LANES), 1)
        acc = jnp.zeros((1, LANES), F32)
        for b in range(NUM_BUCKETS):
            acc = jnp.where(lane == b, jnp.sum(jnp.where(bk == b, dbv, 0.0)), acc)
        o_ref[0] = acc

    out = pl.pallas_call(
        body, out_shape=jax.ShapeDtypeStruct((N_HEADS, 1, LANES), F32), grid=(N_HEADS,),
        in_specs=[pl.BlockSpec((1, blk, 3 * blk), lambda h: (h, 0, 0)), pl.BlockSpec((blk, 3 * blk), lambda h: (0, 0))],
        out_specs=pl.BlockSpec((1, 1, LANES), lambda h: (h, 0, 0)),
        name=name, compiler_params=_params("parallel"))(dbias, buckets)
    return out[:, 0, :NUM_BUCKETS]


def _dot_nt(a, b):
    return lax.dot_general(a, b, (((1,), (1,)), ((), ())), preferred_element_type=F32)


def _dot_tn(a, b):
    return lax.dot_general(a, b, (((0,), (0,)), ((), ())), preferred_element_type=F32)


def _stack_pair(x2, left):
    return jnp.concatenate([jnp.where(left, x2, 0.0), jnp.where(left, 0.0, x2)], axis=0).astype(BF16)


def _attn_geometry(blk, d):
    items = ITEMS if d == 1 else min(ITEMS, d)
    chunk = blk * items if d == 1 else blk * d
    groups = 1 if d == 1 else d // items
    halo = blk if d == 1 else chunk
    return items, chunk, groups, halo


def _item_rows(ref, j, r0, blk, d):
    if d == 1:
        return ref[j * blk:(j + 1) * blk, :]
    return ref[pl.ds(r0 + j, blk, stride=d), :]


def _item_penalty(t, nct, j, items, blk, d):
    first_ok, last_ok = t > 0, t < nct - 1
    if d == 1:
        first_ok = True if j > 0 else first_ok
        last_ok = True if j < items - 1 else last_ok
    col = lax.broadcasted_iota(jnp.int32, (1, 3 * blk), 1)
    ok = jnp.logical_and(jnp.logical_or(col >= blk, first_ok), jnp.logical_or(col < 2 * blk, last_ok))
    return jnp.where(ok, 0.0, NEG_INF).astype(F32)


def _attn_specs(seq, blk, d, step_of, col=0):
    _, chunk, _, halo = _attn_geometry(blk, d)
    per, last, first = chunk // halo, seq // halo - 1, col // LANES
    cur = pl.BlockSpec((chunk, LANES), lambda hp, t: (step_of(t), first + hp))
    prev = pl.BlockSpec((halo, LANES), lambda hp, t: (jnp.clip(step_of(t) * per - 1, 0, last), first + hp))
    nxt = pl.BlockSpec((halo, LANES), lambda hp, t: (jnp.minimum((step_of(t) + 1) * per, last), first + hp))
    return cur, prev, nxt


def _attn_fwd(q, k, v, bias, sink, blk, d, name, v_col=0):
    seq = q.shape[0]
    items, chunk, groups, _ = _attn_geometry(blk, d)
    nct = seq // chunk
    has_sink = sink is not None
    scale = HEAD_DIM ** -0.5

    def body(*refs):
        q_ref, kp, kc, kn, vp, vc, vn, b_ref = refs[:8]
        s_ref = refs[8] if has_sink else None
        o_ref, l_ref = refs[-2], refs[-1]
        t = pl.program_id(1)
        left = lax.broadcasted_iota(jnp.int32, (1, LANES), 1) < HEAD_DIM
        bias2 = b_ref[...]
        if d == 1:
            kwin = jnp.concatenate([kp[...], kc[...], kn[...]], axis=0).astype(BF16)
            vwin = jnp.concatenate([vp[...], vc[...], vn[...]], axis=0).astype(BF16)

        def group(r0):
            scores, vcats = [], []
            for j in range(items):
                qs = _stack_pair(_item_rows(q_ref, j, r0, blk, d) * scale, left)
                if d == 1:
                    kcat, vcat = kwin[j * blk:(j + 3) * blk], vwin[j * blk:(j + 3) * blk]
                else:
                    kcat = jnp.concatenate([_item_rows(r, j, r0, blk, d) for r in (kp, kc, kn)], axis=0).astype(BF16)
                    vcat = jnp.concatenate([_item_rows(r, j, r0, blk, d) for r in (vp, vc, vn)], axis=0).astype(BF16)
                scores.append(_dot_nt(qs, kcat) + bias2 + _item_penalty(t, nct, j, items, blk, d))
                vcats.append(vcat)
            ms = [jnp.max(s, axis=-1, keepdims=True) for s in scores]
            if has_sink:
                sk = s_ref[...]
                ms = [jnp.maximum(m, sk) for m in ms]
            ps = [jnp.exp(s - m) for s, m in zip(scores, ms)]
            dens = [jnp.sum(p, axis=-1, keepdims=True) for p in ps]
            if has_sink:
                dens = [den + jnp.exp(sk - m) for den, m in zip(dens, ms)]
            pns = [(p * (1.0 / den)).astype(BF16) for p, den in zip(ps, dens)]
            lses = [m + jnp.log(den) for m, den in zip(ms, dens)]
            for j in range(items):
                o2 = jnp.dot(pns[j], vcats[j], preferred_element_type=F32)
                o_val = jnp.where(left, o2[:blk], o2[blk:])
                l_val = jnp.where(left, lses[j][:blk], lses[j][blk:])
                if d == 1:
                    o_ref[j * blk:(j + 1) * blk, :] = o_val
                    l_ref[j * blk:(j + 1) * blk, :] = l_val
                else:
                    o_ref[pl.ds(r0 + j, blk, stride=d), :] = o_val
                    l_ref[pl.ds(r0 + j, blk, stride=d), :] = l_val

        if groups == 1:
            group(0)
        else:
            def step(g, carry):
                group(g * items)
                return carry

            lax.fori_loop(0, groups, step, 0)

    cur, prev, nxt = _attn_specs(seq, blk, d, lambda t: t)
    v_cur, v_prev, v_nxt = _attn_specs(seq, blk, d, lambda t: t, v_col)
    in_specs = [cur, prev, cur, nxt, v_prev, v_cur, v_nxt, pl.BlockSpec((2 * blk, 3 * blk), lambda hp, t: (hp, 0))]
    args = [q, k, k, k, v, v, v, bias]
    if has_sink:
        in_specs.append(pl.BlockSpec((2 * blk, 1), lambda hp, t: (hp, 0)))
        args.append(sink)
    return pl.pallas_call(
        body, out_shape=[jax.ShapeDtypeStruct((seq, WIDTH), F32)] * 2, grid=(N_PAIRS, nct),
        in_specs=in_specs, out_specs=[cur, cur], name=name,
        compiler_params=_params("parallel", "parallel"))(*args)


def _attn_bwd(q, k, v, do, lse, delta, bias, sink, blk, d, name, v_col=0):
    seq = q.shape[0]
    items, chunk, groups, halo = _attn_geometry(blk, d)
    nct = seq // chunk
    has_sink = sink is not None
    n_in = 12 if has_sink else 11
    scale = HEAD_DIM ** -0.5

    def body(*refs):
        q_ref, kp, kc, kn, vp, vc, vn, do_ref, l_ref, d_ref, b_ref = refs[:11]
        s_ref = refs[11] if has_sink else None
        dq_ref, dk_ref, dv_ref, db_ref = refs[n_in:n_in + 4]
        ds_ref = refs[n_in + 4] if has_sink else None
        wk, wv = refs[-2], refs[-1]
        t = pl.program_id(1)

        @pl.when(t == 0)
        def _():
            wk[...] = jnp.zeros_like(wk)
            wv[...] = jnp.zeros_like(wv)
            db_ref[...] = jnp.zeros_like(db_ref)
            if has_sink:
                ds_ref[...] = jnp.zeros_like(ds_ref)

        @pl.when(t > 0)
        def _():
            for w in (wk, wv):
                keep = w[chunk:2 * chunk + halo]
                w[0:chunk + halo] = keep
                w[chunk + halo:2 * chunk + halo] = jnp.zeros((chunk, LANES), F32)

        @pl.when(t < nct)
        def _():
            lane = lax.broadcasted_iota(jnp.int32, (1, LANES), 1)
            left = lane < HEAD_DIM
            bias2 = b_ref[...]
            if d == 1:
                kwin = jnp.concatenate([kp[...], kc[...], kn[...]], axis=0).astype(BF16)
                vwin = jnp.concatenate([vp[...], vc[...], vn[...]], axis=0).astype(BF16)

            def group(r0):
                qss, doss, kcats, scores, dps, lcols, dcols = [], [], [], [], [], [], []
                for j in range(items):
                    qs = _stack_pair(_item_rows(q_ref, j, r0, blk, d) * scale, left)
                    dos = _stack_pair(_item_rows(do_ref, j, r0, blk, d), left)
                    if d == 1:
                        kcat, vcat = kwin[j * blk:(j + 3) * blk], vwin[j * blk:(j + 3) * blk]
                    else:
                        kcat = jnp.concatenate([_item_rows(r, j, r0, blk, d) for r in (kp, kc, kn)], axis=0).astype(BF16)
                        vcat = jnp.concatenate([_item_rows(r, j, r0, blk, d) for r in (vp, vc, vn)], axis=0).astype(BF16)
                    l2, d2 = _item_rows(l_ref, j, r0, blk, d), _item_rows(d_ref, j, r0, blk, d)
                    lcols.append(jnp.concatenate([jnp.max(jnp.where(left, l2, NEG_INF), axis=-1, keepdims=True),
                                                  jnp.max(jnp.where(left, NEG_INF, l2), axis=-1, keepdims=True)], axis=0))
                    dcols.append(jnp.concatenate([jnp.sum(jnp.where(lane == 0, d2, 0.0), axis=-1, keepdims=True),
                                                  jnp.sum(jnp.where(lane == HEAD_DIM, d2, 0.0), axis=-1, keepdims=True)],
                                                 axis=0))
                    scores.append(_dot_nt(qs, kcat) + bias2 + _item_penalty(t, nct, j, items, blk, d))
                    dps.append(_dot_nt(dos, vcat))
                    qss.append(qs)
                    doss.append(dos)
                    kcats.append(kcat)
                ps = [jnp.exp(s - lc) for s, lc in zip(scores, lcols)]
                dss = [p * (dp - dc) for p, dp, dc in zip(ps, dps, dcols)]
                db_ref[...] += functools.reduce(lambda a, b: a + b, dss)
                if has_sink:
                    sk = s_ref[...]
                    ds_ref[...] -= functools.reduce(lambda a, b: a + b, [dc * jnp.exp(sk - lc) for dc, lc in zip(dcols, lcols)])
                dsbs = [ds.astype(BF16) for ds in dss]
                for j in range(items):
                    dq2 = jnp.dot(dsbs[j], kcats[j], preferred_element_type=F32) * scale
                    dq_val = jnp.where(left, dq2[:blk], dq2[blk:])
                    if d == 1:
                        dq_ref[j * blk:(j + 1) * blk, :] = dq_val
                    else:
                        dq_ref[pl.ds(r0 + j, blk, stride=d), :] = dq_val
                news = [(_dot_tn(dsbs[j], qss[j]), _dot_tn(ps[j].astype(BF16), doss[j])) for j in range(items)]
                if d == 1:
                    for which, w in enumerate((wk, wv)):
                        for b in range(items + 2):
                            parts = [news[j][which][(b - j) * blk:(b - j + 1) * blk] for j in range(items) if 0 <= b - j < 3]
                            w[chunk + (b - 1) * blk:chunk + b * blk, :] += functools.reduce(lambda x, y: x + y, parts)
                else:
                    for j in range(items):
                        for which, w in enumerate((wk, wv)):
                            for c in range(3):
                                w[pl.ds(c * chunk + r0 + j, blk, stride=d), :] += news[j][which][c * blk:(c + 1) * blk]

            if groups == 1:
                group(0)
            else:
                def step(g, carry):
                    group(g * items)
                    return carry

                lax.fori_loop(0, groups, step, 0)

        dk_ref[...] = wk[0:chunk]
        dv_ref[...] = wv[0:chunk]

    cur, prev, nxt = _attn_specs(seq, blk, d, lambda t: jnp.minimum(t, nct - 1))
    v_cur, v_prev, v_nxt = _attn_specs(seq, blk, d, lambda t: jnp.minimum(t, nct - 1), v_col)
    lag = pl.BlockSpec((chunk, LANES), lambda hp, t: (jnp.maximum(t - 1, 0), hp))
    band = pl.BlockSpec((2 * blk, 3 * blk), lambda hp, t: (hp, 0))
    col = pl.BlockSpec((2 * blk, 1), lambda hp, t: (hp, 0))
    in_specs = [cur, prev, cur, nxt, v_prev, v_cur, v_nxt, cur, cur, cur, band]
    args = [q, k, k, k, v, v, v, do, lse, delta, bias]
    out_shape = [jax.ShapeDtypeStruct((seq, WIDTH), F32)] * 3 + [jax.ShapeDtypeStruct((N_HEADS * blk, 3 * blk), F32)]
    out_specs = [cur, lag, lag, band]
    if has_sink:
        in_specs.append(col)
        args.append(sink)
        out_shape.append(jax.ShapeDtypeStruct((N_HEADS * blk, 1), F32))
        out_specs.append(col)
    window = pltpu.VMEM((2 * chunk + halo, LANES), F32)
    return pl.pallas_call(
        body, out_shape=out_shape, grid=(N_PAIRS, nct + 1), in_specs=in_specs, out_specs=out_specs,
        scratch_shapes=[window, window], name=name,
        compiler_params=_params("arbitrary", "arbitrary"))(*args)


def _combine_patterns(outs, lses):
    def combine(*tiles):
        os_, ls = tiles[:len(outs)], tiles[len(outs):]
        m = functools.reduce(jnp.maximum, ls)
        es = [jnp.exp(l - m) for l in ls]
        den = functools.reduce(lambda a, b: a + b, es)
        num = functools.reduce(lambda a, b: a + b, [e * o for e, o in zip(es, os_)])
        return num / den, m + jnp.log(den)

    return _ew(combine, [*outs, *lses], [F32, F32], "combine_a")


def _row_dots(dy, y, name):
    return _ew(lambda dy_, y_: (_seg_sum(dy_ * y_),), [dy, y], [F32], name)[0]


def _tile_gain(g, reps):
    return jnp.tile(g[None, :], (1, reps))


def _local_step(x, p, target, w_in_of, rest_of, small):
    rel_table = small["rel_table"]
    buckets_a = [_band_buckets(blk, d) for blk, d in DILATED]
    buckets_b = _band_buckets(BLK_B, 1)
    bias_a = [_bias_tiles(rel_table, bk, 0, "bias_a").reshape(N_HEADS * bk.shape[0], -1) for bk in buckets_a]
    bias_b = _bias_tiles(rel_table, buckets_b, N_HEADS, "bias_b").reshape(N_HEADS * BLK_B, -1)

    saved = []
    for l in range(DEPTH):
        g_mix, g_ffn, g_ple = (small[n][l][None, :] for n in ("norm_mix_g", "norm_ffn_g", "norm_ple_g"))
        gqa, gka, gqb = (_tile_gain(small[n][l], N_HEADS) for n in ("qnorm_a_g", "knorm_a_g", "qnorm_b_g"))
        gkb = _tile_gain(small["knorm_b_g"][l], N_KV_B)
        sink = jnp.repeat(small["sink_b"][l], BLK_B)[:, None]

        h = _rms_fwd(x, g_mix, "rms_mix")
        w_in = w_in_of(l, (h, bias_a, bias_b))
        proj = _mm(h, w_in, "nt", F32, "mm_in")
        qa, ka, qb, kb, vb = _qknorm_fwd(proj, gqa, gka, gqb, gkb)
        outs, lses = [], []
        for (blk, d), bias in zip(DILATED, bias_a):
            o, ls = _attn_fwd(qa, ka, proj, bias, None, blk, d, f"attn_a{d}_fwd", v_col=OFF_VA)
            outs.append(o)
            lses.append(ls)
        ya, lse_a = _combine_patterns(outs, lses)
        yb, lse_b = _attn_fwd(qb, kb, vb, bias_b, sink, BLK_B, 1, "attn_b_fwd")
        w = dict(rest_of(l, yb), w_in=w_in)
        def gate(products, extra):
            (ca_, cb_), (ga_, gb_) = products, extra
            return _sigmoid(ga_) * ca_ + _sigmoid(gb_) * cb_, ca_, cb_

        merged, ca, cb = _mm_fused([(ya, w["w_branch_a"], "nn"), (yb, w["w_branch_b"], "nn")],
                                   [(proj, OFF_GA), (proj, OFF_GB)], gate, [BF16, BF16, BF16], "mm_branches_gate")
        x1 = _mm(merged, w["w_out"], "nn", F32, "mm_out", res=x)

        h2 = _rms_fwd(x1, g_ffn, "rms_ffn")

        def swiglu(products, extra):
            a_, u_ = products
            return (a_ * _sigmoid(a_)) * u_, a_, u_

        hid, a, u = _mm_fused([(h2, w["w_ffn_gate"], "nt"), (h2, w["w_ffn_up"], "nt")], [], swiglu,
                              [BF16, BF16, BF16], "mm_ffn_gate_up")
        x2 = _mm(hid, w["w_ffn_down"], "nn", F32, "mm_ffn_down", res=x1)

        h3 = _rms_fwd(x2, g_ple, "rms_ple")

        def ple(products, extra):
            z_, e_ = products
            return extra[0] + _sigmoid(z_) * e_, z_, e_

        x3, z, e = _mm_fused([(h3, w["w_ple_gate"], "nn"), (p[l], w["w_ple_proj"], "nn")], [(x2, 0)], ple,
                             [F32, BF16, BF16], "mm_ple")
        saved.append(dict(w=w, x0=x, h=h, proj=proj, qa=qa, ka=ka, qb=qb, kb=kb, vb=vb, ya=ya, lse_a=lse_a,
                          yb=yb, lse_b=lse_b, ca=ca, cb=cb, merged=merged, x1=x1, h2=h2, a=a, u=u, hid=hid,
                          x2=x2, h3=h3, z=z, e=e))
        x = x3

    dx, loss_acc = _loss_grad(x, target)
    loss = loss_acc[0, 0]

    gbig = [{} for _ in range(DEPTH)]
    marks = [{} for _ in range(DEPTH)]
    gsmall = {n: [None] * DEPTH for n in SMALL if n != "rel_table"}
    dbias_a = [[] for _ in DILATED]
    dbias_b = []

    for l in reversed(range(DEPTH)):
        sv = saved[l]
        w = sv["w"]
        g_mix, g_ffn, g_ple = (small[n][l][None, :] for n in ("norm_mix_g", "norm_ffn_g", "norm_ple_g"))
        gqa, gka, gqb = (_tile_gain(small[n][l], N_HEADS) for n in ("qnorm_a_g", "knorm_a_g", "qnorm_b_g"))
        gkb = _tile_gain(small["knorm_b_g"][l], N_KV_B)
        sink = jnp.repeat(small["sink_b"][l], BLK_B)[:, None]

        def ple_bwd(dx_, z_, e_):
            s = _sigmoid(z_.astype(F32))
            return dx_ * s, dx_ * e_.astype(F32) * (s * (1.0 - s))

        de, dz = _ew(ple_bwd, [dx, sv["z"], sv["e"]], [BF16, BF16], "ple_bwd")
        gbig[l]["w_ple_proj"] = _mm(p[l], de, "tn", BF16, "mm_d_ple_proj")
        gbig[l]["w_ple_gate"] = _mm(sv["h3"], dz, "tn", BF16, "mm_d_ple_gate")
        dh3 = _mm(dz, w["w_ple_gate"], "nt", F32, "mm_dh3")
        dx, dxb, gsmall["norm_ple_g"][l] = _rms_bwd(sv["x2"], g_ple, dh3, dx, "rms_ple_bwd")

        gbig[l]["w_ffn_down"] = _mm(sv["hid"], dxb, "tn", BF16, "mm_d_ffn_down")

        def swiglu_bwd(products, extra):
            dh_, a_, u_ = products[0], extra[0].astype(F32), extra[1].astype(F32)
            s = _sigmoid(a_)
            return dh_ * u_ * (s * (1.0 + a_ * (1.0 - s))), dh_ * (a_ * s)

        da, du = _mm_fused([(dxb, w["w_ffn_down"], "nt")], [(sv["a"], 0), (sv["u"], 0)], swiglu_bwd, [BF16, BF16],
                           "mm_dhid_swiglu_bwd")
        gbig[l]["w_ffn_gate"] = _mm(da, sv["h2"], "tn", BF16, "mm_d_ffn_gate")
        gbig[l]["w_ffn_up"] = _mm(du, sv["h2"], "tn", BF16, "mm_d_ffn_up")
        dh2 = _mm(da, w["w_ffn_gate"], "nn", F32, "mm_dh2_gate")
        dh2 = _mm(du, w["w_ffn_up"], "nn", F32, "mm_dh2_up", res=dh2)
        dx, dxb, gsmall["norm_ffn_g"][l] = _rms_bwd(sv["x1"], g_ffn, dh2, dx, "rms_ffn_bwd")

        gbig[l]["w_out"] = _mm(sv["merged"], dxb, "tn", BF16, "mm_d_out")

        def gate_bwd(products, extra):
            dm_, ca_, cb_ = products[0], extra[0].astype(F32), extra[1].astype(F32)
            sa, sb = _sigmoid(extra[2]), _sigmoid(extra[3])
            return dm_ * sa, dm_ * sb, dm_ * ca_ * (sa * (1.0 - sa)), dm_ * cb_ * (sb * (1.0 - sb))

        dca, dcb, dga, dgb = _mm_fused(
            [(dxb, w["w_out"], "nt")], [(sv["ca"], 0), (sv["cb"], 0), (sv["proj"], OFF_GA), (sv["proj"], OFF_GB)],
            gate_bwd, [BF16, BF16, BF16, BF16], "mm_dmerged_gate_bwd")
        gbig[l]["w_branch_a"] = _mm(sv["ya"], dca, "tn", BF16, "mm_d_branch_a")
        gbig[l]["w_branch_b"] = _mm(sv["yb"], dcb, "tn", BF16, "mm_d_branch_b")
        dya = _mm(dca, w["w_branch_a"], "nt", F32, "mm_dya")
        dyb = _mm(dcb, w["w_branch_b"], "nt", F32, "mm_dyb")

        delta_a = _row_dots(dya, sv["ya"], "attn_a_row_dots")
        delta_b = _row_dots(dyb, sv["yb"], "attn_b_row_dots")

        dqa, dka, dva = [], [], []
        for (blk, d), bias, bk in zip(DILATED, bias_a, buckets_a):
            dq_, dk_, dv_, db_ = _attn_bwd(sv["qa"], sv["ka"], sv["proj"], dya, sv["lse_a"], delta_a, bias, None, blk, d,
                                           f"attn_a{d}_bwd", v_col=OFF_VA)
            dqa.append(dq_)
            dka.append(dk_)
            dva.append(dv_)
            dbias_a[len(dqa) - 1].append(db_)
        dqb, dkb, dvb, db_, dsink = _attn_bwd(sv["qb"], sv["kb"], sv["vb"], dyb, sv["lse_b"], delta_b, bias_b, sink,
                                              BLK_B, 1, "attn_b_bwd")
        dbias_b.append(db_)
        gsmall["sink_b"][l] = dsink.reshape(N_HEADS, BLK_B).sum(axis=1)

        dproj, pqa, pka, pqb, pkb = _qknorm_bwd(sv["proj"], gqa, gka, gqb, gkb, dqa, dka, dva, dqb, dkb, dvb, dga, dgb)
        marks[l]["attn_bwd_done"] = dproj
        gsmall["qnorm_a_g"][l] = pqa.reshape(N_HEADS, HEAD_DIM).sum(0)
        gsmall["knorm_a_g"][l] = pka.reshape(N_HEADS, HEAD_DIM).sum(0)
        gsmall["qnorm_b_g"][l] = pqb.reshape(N_HEADS, HEAD_DIM).sum(0)
        gsmall["knorm_b_g"][l] = pkb.reshape(N_KV_B, HEAD_DIM).sum(0)
        gbig[l]["w_in"] = _mm(dproj, sv["h"], "tn", BF16, "mm_d_in")
        dh = _mm(dproj, w["w_in"], "nn", F32, "mm_dh")
        dx, _, gsmall["norm_mix_g"][l] = _rms_bwd(sv["x0"], g_mix, dh, dx, "rms_mix_bwd")
        gsmall["norm_mix_g"][l] = gsmall["norm_mix_g"][l][0]
        gsmall["norm_ffn_g"][l] = gsmall["norm_ffn_g"][l][0]
        gsmall["norm_ple_g"][l] = gsmall["norm_ple_g"][l][0]

    gsmall = {n: jnp.stack(v) for n, v in gsmall.items()}
    dtable_a = sum(_table_grad(sum(dbs).reshape(N_HEADS, blk, 3 * blk), bk, "table_grad_a")
                   for dbs, (blk, _), bk in zip(dbias_a, DILATED, buckets_a))
    dtable_b = _table_grad(sum(dbias_b).reshape(N_HEADS, BLK_B, 3 * BLK_B), buckets_b, "table_grad_b")
    gsmall["rel_table"] = jnp.concatenate([dtable_a, dtable_b], axis=0).T
    return loss, dx, gbig, gsmall, marks


def _place():
    return lax.axis_index("x"), lax.axis_index("y"), lax.axis_index("c")


def _flip(v, bit):
    return 1 - v if bit else v


CHIP_RELATIONS = ((0, 1), (1, 0), (1, 1))
ANY = pl.BlockSpec(memory_space=pl.ANY)


def _allgather_body(w_refs, out_refs, send_sems, recv_sems):
    x, y, c = _place()
    chips = [(_flip(x, a), _flip(y, b)) for a, b in CHIP_RELATIONS]

    def make(g):
        w_ref, out_ref = w_refs[g], out_refs[g]
        half = w_ref.shape[0] // 2

        def part(px, py, pc):
            return out_ref.at[2 * px + py, pl.ds(pc * half, half), :]

        def copy(k, block, to, src=None):
            return pltpu.make_async_remote_copy(
                src_ref=part(*block) if src is None else src, dst_ref=part(*block),
                send_sem=send_sems.at[7 * g + k], recv_sem=recv_sems.at[7 * g + k], device_id=to,
                device_id_type=MESH_ID)

        own = pltpu.make_async_remote_copy(
            src_ref=w_ref, dst_ref=out_ref.at[2 * x + y], send_sem=send_sems.at[7 * g + 6],
            recv_sem=recv_sems.at[7 * g + 6], device_id=(x, y, 1 - c), device_id_type=MESH_ID)
        first = [copy(k, (x, y, c), (*chip, c), src=w_ref.at[pl.ds(c * half, half), :]) for k, chip in enumerate(chips)]
        passed = [copy(3 + k, (*chip, c), (x, y, 1 - c)) for k, chip in enumerate(chips)]
        arrive = [copy(k, (*chip, c), (x, y, c)) for k, chip in enumerate(chips)]
        arrive2 = [copy(3 + k, (*chip, 1 - c), (x, y, c)) for k, chip in enumerate(chips)]
        return own, first, passed, arrive, arrive2

    made = [make(g) for g in range(len(w_refs))]
    for own, first, _, _, _ in made:
        own.start()
        for cp in first:
            cp.start()
    for _, _, passed, arrive, _ in made:
        for k in range(3):
            arrive[k].wait_recv()
            passed[k].start()
    for own, first, passed, _, arrive2 in made:
        for k in range(3):
            arrive2[k].wait_recv()
        own.wait_recv()
        for cp in first + passed + [own]:
            cp.wait_send()


def _sibling(x, y, c):
    return [(x, y, 1 - c)]


def _same_core_of_other_chips(x, y, c):
    return [(_flip(x, a), _flip(y, b), c) for a, b in CHIP_RELATIONS]


def _exchange(body, ins, out_types, n_sems, name, sequencer=None):
    n = len(ins)
    sems = (pltpu.SemaphoreType.DMA((n_sems,)), pltpu.SemaphoreType.DMA((n_sems,)))
    if sequencer is None:
        in_place = out_types is None
        out_shape = [jax.ShapeDtypeStruct(a.shape, a.dtype) for a in ins] if in_place else out_types

        def tc_body(*refs):
            body(refs[:n], refs[n:n + len(out_shape)], refs[-2], refs[-1])

        return list(pl.pallas_call(
            tc_body, out_shape=out_shape, in_specs=[ANY] * n, out_specs=[ANY] * len(out_shape),
            input_output_aliases={g: g for g in range(n)} if in_place else {}, scratch_shapes=list(sems), name=name)(*ins))

    collective_id, peers = sequencer
    hbm = pltpu.MemorySpace.HBM
    in_refs = [jax.new_ref(a, memory_space=hbm) for a in ins]
    out_refs = in_refs if out_types is None else [jax.empty_ref(t, memory_space=hbm) for t in out_types]

    @pl.kernel(mesh=plsc.ScalarSubcoreMesh(axis_name="sequencer", num_cores=1), name=name, scratch_types=sems,
               compiler_params=pltpu.CompilerParams(collective_id=collective_id))
    def launch(send_sems, recv_sems):
        barrier = pltpu.get_barrier_semaphore()
        devices = peers(*_place())
        for device in devices:
            pl.semaphore_signal(barrier, inc=1, device_id=device, device_id_type=MESH_ID)
        pl.semaphore_wait(barrier, len(devices))
        body(in_refs, out_refs, send_sems, recv_sems)

    launch()
    return [r[...] for r in out_refs]


def _allgather(shards, name, sequencer=None):
    out_types = [jax.ShapeDtypeStruct((N_CHIPS,) + s.shape, s.dtype) for s in shards]
    if sequencer is not None:
        sequencer = (sequencer, lambda x, y, c: _sibling(x, y, c) + _same_core_of_other_chips(x, y, c))
    return _exchange(_allgather_body, shards, out_types, 7 * len(shards), name, sequencer)


def _half_tile(half):
    return max(t for t in range(16, 1025, 16) if half % t == 0)


def _run_copies(cps):
    for cp in cps:
        cp.start()
    for cp in cps:
        cp.wait_recv()
    for cp in cps:
        cp.wait_send()


def _sibling_halves(gsends, name, sequencer=None):
    def body(g_refs, out_refs, send_sems, recv_sems):
        x, y, c = _place()
        cps = []
        for g, (g_ref, out_ref) in enumerate(zip(g_refs, out_refs)):
            half = g_ref.shape[1] // 2
            cps.append(pltpu.make_async_remote_copy(
                src_ref=g_ref.at[:, pl.ds((1 - c) * half, half), :], dst_ref=out_ref,
                send_sem=send_sems.at[g], recv_sem=recv_sems.at[g], device_id=(x, y, 1 - c), device_id_type=MESH_ID))
        _run_copies(cps)

    out_types = [jax.ShapeDtypeStruct((s.shape[0], s.shape[1] // 2, s.shape[2]), s.dtype) for s in gsends]
    return _exchange(body, gsends, out_types, len(gsends), name, sequencer and (sequencer, _sibling))


def _chip_sums(gsend, sib, place):
    n, rows, cols = gsend.shape
    half = rows // 2
    tm = _half_tile(half)
    nblk = half // tm

    def body(s_ref, g_ref, sib_ref, o_ref):
        o_ref[0] = (g_ref[0].astype(F32) + sib_ref[0].astype(F32)).astype(o_ref.dtype)

    grid_spec = pltpu.PrefetchScalarGridSpec(
        num_scalar_prefetch=1, grid=(n, nblk),
        in_specs=[pl.BlockSpec((1, tm, cols), lambda k, i, s: (jnp.bitwise_xor(s[0], k), s[1] * nblk + i, 0)),
                  pl.BlockSpec((1, tm, cols), lambda k, i, s: (jnp.bitwise_xor(s[0], k), i, 0))],
        out_specs=pl.BlockSpec((1, tm, cols), lambda k, i, s: (k, i, 0)))
    return pl.pallas_call(
        body, out_shape=jax.ShapeDtypeStruct((n, half, cols), BF16), grid_spec=grid_spec,
        name="rs_chip_sums", compiler_params=_params("parallel", "parallel"))(place, gsend, sib)


def _exchange_chip_sums(tsends, name, sequencer=None):
    def body(t_refs, out_refs, send_sems, recv_sems):
        x, y, c = _place()
        cps = []
        for g, (t_ref, out_ref) in enumerate(zip(t_refs, out_refs)):
            for k, device in enumerate(_same_core_of_other_chips(x, y, c)):
                cps.append(pltpu.make_async_remote_copy(
                    src_ref=t_ref.at[k + 1], dst_ref=out_ref.at[k], send_sem=send_sems.at[3 * g + k],
                    recv_sem=recv_sems.at[3 * g + k], device_id=device, device_id_type=MESH_ID))
        _run_copies(cps)

    out_types = [jax.ShapeDtypeStruct((3,) + s.shape[1:], s.dtype) for s in tsends]
    return _exchange(body, tsends, out_types, 3 * len(tsends), name,
                     sequencer and (sequencer, _same_core_of_other_chips))


def _final_sum(tsend, recv, place):
    n, half, cols = tsend.shape
    tm = _half_tile(half)
    nblk = half // tm

    def body(s_ref, t_ref, r_ref, o_ref):
        o_ref[...] = ((t_ref[0].astype(F32) + r_ref[0].astype(F32)) + r_ref[1].astype(F32)) + r_ref[2].astype(F32)

    grid_spec = pltpu.PrefetchScalarGridSpec(
        num_scalar_prefetch=1, grid=(nblk,),
        in_specs=[pl.BlockSpec((1, tm, cols), lambda i, s: (0, i, 0)), pl.BlockSpec((n - 1, tm, cols), lambda i, s: (0, i, 0))],
        out_specs=pl.BlockSpec((tm, cols), lambda i, s: (s[1] * nblk + i, 0)))
    return pl.pallas_call(
        body, out_shape=jax.ShapeDtypeStruct((2 * half, cols), F32), grid_spec=grid_spec, name="rs_final_sum",
        compiler_params=_params("parallel"))(place, tsend, recv)


def _join_halves(gfulls, name, sequencer=None):
    def body(g_refs, out_refs, send_sems, recv_sems):
        x, y, c = _place()
        n = len(g_refs)

        def copy(g, pc):
            half = g_refs[g].shape[0] // 2
            return pltpu.make_async_remote_copy(
                src_ref=g_refs[g].at[pl.ds(pc * half, half), :], dst_ref=out_refs[g].at[pl.ds(pc * half, half), :],
                send_sem=send_sems.at[g], recv_sem=recv_sems.at[g], device_id=(x, y, 1 - c), device_id_type=MESH_ID)

        mine = [copy(g, c) for g in range(n)]
        for cp in mine:
            cp.start()
        for g in range(n):
            copy(g, 1 - c).wait_recv()
        for cp in mine:
            cp.wait_send()

    return _exchange(body, gfulls, None, len(gfulls), name, sequencer and (sequencer, _sibling))


def _allreduce_small(v):
    rows, cols = v.shape

    def body(v_ref, out_ref, buf, send_sems, recv_sems):
        x, y, c = _place()
        cps = []
        for k in range(1, 8):
            peer = (_flip(x, (k >> 2) & 1), _flip(y, (k >> 1) & 1), _flip(c, k & 1))
            cps.append(pltpu.make_async_remote_copy(
                src_ref=v_ref, dst_ref=buf.at[k - 1], send_sem=send_sems.at[k - 1], recv_sem=recv_sems.at[k - 1],
                device_id=peer, device_id_type=MESH_ID))
        for cp in cps:
            cp.start()
        for cp in cps:
            cp.wait_recv()
        for cp in cps:
            cp.wait_send()
        t0 = v_ref[...] + buf[0]
        t1 = buf[1] + buf[2]
        t2 = buf[3] + buf[4]
        t3 = buf[5] + buf[6]
        out_ref[...] = (t0 + t1) + (t2 + t3)

    vm = pl.BlockSpec(memory_space=pltpu.VMEM)
    return pl.pallas_call(
        body, out_shape=jax.ShapeDtypeStruct((rows, cols), F32), in_specs=[vm], out_specs=vm,
        scratch_shapes=[pltpu.VMEM((7, rows, cols), F32), pltpu.SemaphoreType.DMA((7,)), pltpu.SemaphoreType.DMA((7,))],
        name="allreduce_small")(v)


BIG_INFO = {n: (shape, ax) for n, shape, ax in BIG}
GROUPS = (("w_in",), ("w_ffn_gate", "w_ffn_up", "w_ffn_down", "w_out", "w_ple_gate"),
          ("w_branch_a", "w_branch_b", "w_ple_proj"))


def _shard_shape(name):
    (k, m), ax = BIG_INFO[name]
    return (k // N_CHIPS, m) if ax == 0 else (k, m // N_CHIPS)


def _group_rows(group):
    offs, off = {}, 0
    for n in group:
        offs[n] = off
        off += _shard_shape(n)[0]
    return offs, off


def _pack_groups(shards, layer, dtype):
    return [jnp.concatenate([shards[n][layer].astype(dtype) for n in group], axis=0) for group in GROUPS]


def _unpack_full(gathered, groups):
    out = {}
    for group, arr in zip(groups, gathered):
        offs, _ = _group_rows(group)
        for n in group:
            rows, cols = _shard_shape(n)
            (k, m), ax = BIG_INFO[n]
            slab = arr[:, offs[n]:offs[n] + rows]
            out[n] = slab.reshape(k, m) if ax == 0 else jnp.transpose(slab, (1, 0, 2)).reshape(k, m)
    return out


def _pack_grads(gfull):
    out = []
    for group in GROUPS:
        parts = []
        for n in group:
            rows, cols = _shard_shape(n)
            ax = BIG_INFO[n][1]
            slab = (gfull[n].reshape(N_CHIPS, rows, cols) if ax == 0
                    else jnp.transpose(gfull[n].reshape(rows, N_CHIPS, cols), (1, 0, 2)))
            parts.append(slab)
        out.append(jnp.concatenate(parts, axis=1))
    return out


def _after(values, mark):
    values, _ = lax.optimization_barrier((values, mark))
    return values


def _reduce_scatter_begin(gsends, place, tag, ids):
    sibs = _sibling_halves(gsends, "rs_sibling_halves_" + tag, ids[0])
    tsends = [_chip_sums(g, s, place) for g, s in zip(gsends, sibs)]
    return tsends, _exchange_chip_sums(tsends, "rs_exchange_" + tag, ids[1])


def _reduce_scatter_finish(begun, place, tag, ids, hold):
    tsends, recvs = begun
    recvs = _after(recvs, hold)
    return _join_halves([_final_sum(t, r, place) for t, r in zip(tsends, recvs)], "rs_join_halves_" + tag, ids[2])


SMALL_SHAPES = {"rel_table": (NUM_BUCKETS, 2 * N_HEADS), "norm_mix_g": (DEPTH, D_MODEL), "qnorm_a_g": (DEPTH, HEAD_DIM),
                "knorm_a_g": (DEPTH, HEAD_DIM), "qnorm_b_g": (DEPTH, HEAD_DIM), "knorm_b_g": (DEPTH, HEAD_DIM),
                "sink_b": (DEPTH, N_HEADS), "norm_ffn_g": (DEPTH, D_MODEL), "norm_ple_g": (DEPTH, D_MODEL)}


def _pack_small(vals, last=None):
    flat = jnp.concatenate([vals[n].astype(F32).reshape(-1) for n in SMALL])
    tail = jnp.zeros((SMALL_ROWS * LANES - flat.shape[0],), F32)
    if last is not None:
        tail = tail.at[-1].set(last)
    return jnp.concatenate([flat, tail]).reshape(SMALL_ROWS, LANES)


def _unpack_small(packed):
    flat, out, off = packed.reshape(-1), {}, 0
    for n in SMALL:
        size = math.prod(SMALL_SHAPES[n])
        out[n] = flat[off:off + size].reshape(SMALL_SHAPES[n])
        off += size
    return out


def _adamw(w, gs, g_row, m, v, name):
    c1 = 1.0 - ADAM_B1 ** ADAM_STEP
    c2 = 1.0 - ADAM_B2 ** ADAM_STEP
    total, width = w.shape
    n_layers = len(gs)
    per = total // n_layers
    tm = max(t for t in range(8, 513, 8) if per % t == 0 and g_row % t == 0)
    nblk = per // tm

    def body(*refs):
        w_ref, g_refs = refs[0], refs[1:1 + n_layers]
        m_ref, v_ref, og, od, om, ov = refs[1 + n_layers:]
        layer = pl.program_id(0) // nblk
        g = g_refs[0][...]
        for l in range(1, n_layers):
            g = jnp.where(layer == l, g_refs[l][...], g)
        m_new = ADAM_B1 * m_ref[...] + (1.0 - ADAM_B1) * g
        v_new = ADAM_B2 * v_ref[...] + (1.0 - ADAM_B2) * (g * g)
        og[...] = g
        od[...] = -ADAM_LR * ((m_new / c1) / (jnp.sqrt(v_new / c2) + ADAM_EPS) + ADAM_WD * w_ref[...])
        om[...] = m_new
        ov[...] = v_new

    row = pl.BlockSpec((tm, width), lambda i: (i, 0))
    g_specs = [pl.BlockSpec((tm, width), lambda i, l=l: (g_row // tm + jnp.clip(i - l * nblk, 0, nblk - 1), 0))
               for l in range(n_layers)]
    return pl.pallas_call(
        body, out_shape=[jax.ShapeDtypeStruct((total, width), F32)] * 4, grid=(total // tm,),
        in_specs=[row] + g_specs + [row, row], out_specs=[row] * 4, name=name,
        compiler_params=_params("parallel"))(w, *gs, m, v)


def kernel(x, p, rel_table, norm_mix_g, w_in, qnorm_a_g, knorm_a_g, qnorm_b_g, knorm_b_g, sink_b, w_branch_a, w_branch_b, w_out, norm_ffn_g, w_ffn_gate, w_ffn_up, w_ffn_down, norm_ple_g, w_ple_gate, w_ple_proj, loss_target, m_rel_table, m_norm_mix_g, m_w_in, m_qnorm_a_g, m_knorm_a_g, m_qnorm_b_g, m_knorm_b_g, m_sink_b, m_w_branch_a, m_w_branch_b, m_w_out, m_norm_ffn_g, m_w_ffn_gate, m_w_ffn_up, m_w_ffn_down, m_norm_ple_g, m_w_ple_gate, m_w_ple_proj, v_rel_table, v_norm_mix_g, v_w_in, v_qnorm_a_g, v_knorm_a_g, v_qnorm_b_g, v_knorm_b_g, v_sink_b, v_w_branch_a, v_w_branch_b, v_w_out, v_norm_ffn_g, v_w_ffn_gate, v_w_ffn_up, v_w_ffn_down, v_norm_ple_g, v_w_ple_gate, v_w_ple_proj):
    given = dict(locals())

    def held(name, a):
        return jnp.swapaxes(a, 1, 2) if name in TRANSPOSED else a

    weights = {n: held(n, given[n]) for n in WEIGHTS}
    moments_m = {n: held(n, given["m_" + n]) for n in WEIGHTS}
    moments_v = {n: held(n, given["v_" + n]) for n in WEIGHTS}
    xi, yi, ci = _place()
    place = jnp.stack([2 * xi + yi, ci]).astype(jnp.int32)

    shards = [_pack_groups(weights, l, BF16) for l in range(DEPTH)]
    w_in0 = _allgather(shards[0][:1], "allgather_w_in_layer0", sequencer=9)
    rest0 = _allgather(_after(shards[0][1:], w_in0), "allgather_rest_layer0", sequencer=1)
    gathered = [w_in0 + rest0, None]
    small = {n: weights[n] for n in SMALL}

    def w_in_of(l, mark):
        return _unpack_full(_after(gathered[l][:1], (shards[1], mark) if l == 0 else mark), GROUPS[:1])["w_in"]

    def rest_of(l, mark):
        if l == 0:
            gathered[1] = _allgather(_after(shards[1], mark), "allgather_layer1", sequencer=2)
        return _unpack_full(_after(gathered[l][1:], mark), GROUPS[1:])

    loss, dx, gbig, gsmall, marks = _local_step(x[0], p[:, 0], loss_target[0], w_in_of, rest_of, small)

    gsends = [_pack_grads(gbig[l]) for l in range(DEPTH)]
    stages = {"layer1": (gsends[1], (3, 4, 5)), "rest_layer0": (gsends[0][1:], (6, 7, 8)),
              "w_in_layer0": (gsends[0][:1], (10, 11, 12))}
    begun = {tag: _reduce_scatter_begin(g, place, tag, ids) for tag, (g, ids) in stages.items()}

    def finish(tag, hold):
        return _reduce_scatter_finish(begun[tag], place, tag, stages[tag][1], hold)

    red1 = finish("layer1", marks[0]["attn_bwd_done"])
    rest0 = finish("rest_layer0", marks[0]["attn_bwd_done"])

    grads, delta, new_m, new_v = {}, {}, {}, {}

    def update(group, reduced):
        offs, _ = _group_rows(group)
        for n in group:
            shape = weights[n].shape
            two_d = lambda a: a.reshape(shape[0] * shape[1], shape[2])
            outs = _adamw(two_d(weights[n]), reduced, offs[n], two_d(moments_m[n]), two_d(moments_v[n]), "adamw_" + n)
            grads[n], delta[n], new_m[n], new_v[n] = (held(n, o.reshape(shape)) for o in outs)

    for gi in (1, 2):
        update(GROUPS[gi], _after([rest0[gi - 1], red1[gi]], begun["w_in_layer0"][0]))
    small_grads = _allreduce_small(_pack_small(gsmall, last=loss))
    g_, d_, m_, v_ = _adamw(_pack_small(weights), [small_grads], 0, _pack_small(moments_m), _pack_small(moments_v),
                            "adamw_small")
    grads.update(_unpack_small(g_))
    delta.update(_unpack_small(d_))
    new_m.update(_unpack_small(m_))
    new_v.update(_unpack_small(v_))
    others_done = [dx, d_] + [delta[n] for gi in (1, 2) for n in GROUPS[gi]]
    update(GROUPS[0], [finish("w_in_layer0", others_done)[0], red1[0]])

    return (small_grads[-1, -1], dx[None], *[grads[n] for n in WEIGHTS], *[delta[n] for n in WEIGHTS],
            *[new_m[n] for n in WEIGHTS], *[new_v[n] for n in WEIGHTS])
```

```python
import functools
import math

import jax
import jax.numpy as jnp
from jax import lax
from jax.experimental import pallas as pl
from jax.experimental.pallas import tpu as pltpu
from jax.experimental.pallas import tpu_sc as plsc

F32 = jnp.float32
BF16 = jnp.bfloat16
MESH_ID = pl.DeviceIdType.MESH

D_MODEL = 1024
DEPTH = 2
HEAD_DIM = 64
N_HEADS = 8
WIDTH = N_HEADS * HEAD_DIM
N_PAIRS = 4
ITEMS = 8
N_KV_B = 2
PLE_DIM = 256
D_FF = 2816
D_IN = 4352
OFF_QA, OFF_KA, OFF_VA, OFF_QB, OFF_KB, OFF_VB, OFF_GA, OFF_GB = 0, 512, 1024, 1536, 2048, 2176, 2304, 3328
DILATED = ((64, 1), (64, 4), (64, 16))
BLK_B = 128
NUM_BUCKETS = 32
MAX_DISTANCE = 1024
RMS_EPS = 1e-6
NEG_INF = -1e30
LANES = 128
ROW_TILE = 256
VMEM_LIMIT = 48 * 1024 * 1024

ADAM_LR, ADAM_B1, ADAM_B2, ADAM_EPS, ADAM_WD, ADAM_STEP = 0.001, 0.9, 0.999, 1e-08, 0.01, 10

TRANSPOSED = ("w_in", "w_ffn_gate", "w_ffn_up")
BIG = (
    ("w_in", (D_IN, D_MODEL), 0),
    ("w_branch_a", (WIDTH, D_MODEL), 1),
    ("w_branch_b", (WIDTH, D_MODEL), 1),
    ("w_out", (D_MODEL, D_MODEL), 0),
    ("w_ffn_gate", (D_FF, D_MODEL), 0),
    ("w_ffn_up", (D_FF, D_MODEL), 0),
    ("w_ffn_down", (D_FF, D_MODEL), 0),
    ("w_ple_gate", (D_MODEL, D_MODEL), 0),
    ("w_ple_proj", (PLE_DIM, D_MODEL), 1),
)
SMALL = ("rel_table", "norm_mix_g", "qnorm_a_g", "knorm_a_g", "qnorm_b_g", "knorm_b_g", "sink_b",
         "norm_ffn_g", "norm_ple_g")
WEIGHTS = ("rel_table", "norm_mix_g", "w_in", "qnorm_a_g", "knorm_a_g", "qnorm_b_g", "knorm_b_g", "sink_b",
           "w_branch_a", "w_branch_b", "w_out", "norm_ffn_g", "w_ffn_gate", "w_ffn_up", "w_ffn_down",
           "norm_ple_g", "w_ple_gate", "w_ple_proj")
N_CHIPS = 4
SMALL_ROWS = 64


def _params(*sem):
    return pltpu.CompilerParams(dimension_semantics=sem, vmem_limit_bytes=VMEM_LIMIT)


MM_VMEM_BUDGET = 40 * 1024 * 1024
STEP_OVERHEAD_S = 0.4e-6
TILE_DMA_BYTES_PER_S = 1.5e12


def _mm_dims(a, b, mode):
    if mode == "nn":
        return a.shape[0], b.shape[1], a.shape[1]
    if mode == "nt":
        return a.shape[0], b.shape[0], a.shape[1]
    return a.shape[1], b.shape[1], a.shape[0]


def _mm_tiles(m, n, pairs, tile_bytes, col_offsets):
    best = None
    for tm in (t for t in range(LANES, m + 1, LANES) if m % t == 0):
        for tn in (t for t in range(LANES, n + 1, LANES) if n % t == 0 and all(o % t == 0 for o in col_offsets)):
            io = sum(tm * k * ab + tn * k * bb for k, ab, bb in pairs) + tm * tn * sum(tile_bytes)
            casts = sum((tm * k * 2 if ab == 4 else 0) + (tn * k * 2 if bb == 4 else 0) for k, ab, bb in pairs)
            if 2 * io + len(pairs) * tm * tn * 4 + casts > MM_VMEM_BUDGET:
                continue
            cost = (m // tm) * (n // tn) * STEP_OVERHEAD_S + io / TILE_DMA_BYTES_PER_S
            if best is None or (cost, -tm) < best[0]:
                best = ((cost, -tm), tm, tn)
    return best[1], best[2]


def _mm_fused(pairs, extras, epilogue, out_dtypes, name):
    m, n, _ = _mm_dims(*pairs[0])
    assert all(_mm_dims(*p)[:2] == (m, n) for p in pairs)
    tm, tn = _mm_tiles(
        m, n, [(_mm_dims(a, b, mode)[2], a.dtype.itemsize, b.dtype.itemsize) for a, b, mode in pairs],
        [e.dtype.itemsize for e, _ in extras] + [jnp.dtype(d).itemsize for d in out_dtypes], [off for _, off in extras])
    dims = {"nn": (((1,), (0,)), ((), ())), "nt": (((1,), (1,)), ((), ())), "tn": (((0,), (0,)), ((), ()))}
    in_specs, args = [], []
    for a, b, mode in pairs:
        k = _mm_dims(a, b, mode)[2]
        in_specs.append(pl.BlockSpec((k, tm), lambda i, j: (0, i)) if mode == "tn" else pl.BlockSpec((tm, k), lambda i, j: (i, 0)))
        in_specs.append(pl.BlockSpec((tn, k), lambda i, j: (j, 0)) if mode == "nt" else pl.BlockSpec((k, tn), lambda i, j: (0, j)))
        args += [a, b]
    for e, off in extras:
        in_specs.append(pl.BlockSpec((tm, tn), lambda i, j, o=off // tn: (i, o + j)))
        args.append(e)
    n_pairs, n_in = len(pairs), 2 * len(pairs) + len(extras)

    def body(*refs):
        products = [lax.dot_general(refs[2 * p][...].astype(BF16), refs[2 * p + 1][...].astype(BF16), dims[pairs[p][2]],
                                    preferred_element_type=F32) for p in range(n_pairs)]
        outs = epilogue(products, [r[...] for r in refs[2 * n_pairs:n_in]])
        for r, o in zip(refs[n_in:], outs):
            r[...] = o.astype(r.dtype)

    tile = pl.BlockSpec((tm, tn), lambda i, j: (i, j))
    return pl.pallas_call(
        body, out_shape=[jax.ShapeDtypeStruct((m, n), d) for d in out_dtypes], grid=(m // tm, n // tn),
        in_specs=in_specs, out_specs=[tile] * len(out_dtypes), name=name,
        compiler_params=_params("parallel", "parallel"))(*args)


def _mm(a, b, mode, out_dtype, name, res=None):
    if res is None:
        return _mm_fused([(a, b, mode)], [], lambda products, extra: products, [out_dtype], name)[0]
    return _mm_fused([(a, b, mode)], [(res, 0)], lambda products, extra: [products[0] + extra[0]], [out_dtype], name)[0]


def _ew(fn, ins, out_dtypes, name):
    rows, width = ins[0].shape
    n_in = len(ins)

    def body(*refs):
        outs = fn(*[r[...] for r in refs[:n_in]])
        for r, o in zip(refs[n_in:], outs):
            r[...] = o.astype(r.dtype)

    row = pl.BlockSpec((ROW_TILE, width), lambda i: (i, 0))
    return pl.pallas_call(
        body, out_shape=[jax.ShapeDtypeStruct((rows, width), dt) for dt in out_dtypes], grid=(rows // ROW_TILE,),
        in_specs=[row] * n_in, out_specs=[row] * len(out_dtypes), name=name, compiler_params=_params("parallel"))(*ins)


def _sigmoid(x):
    return 1.0 / (1.0 + jnp.exp(-x))


def _seg_sum(v):
    outs = []
    for k in range(v.shape[1] // LANES):
        vp = v[:, k * LANES:(k + 1) * LANES]
        left = lax.broadcasted_iota(jnp.int32, vp.shape, 1) < HEAD_DIM
        sl = jnp.sum(jnp.where(left, vp, 0.0), axis=-1, keepdims=True)
        sr = jnp.sum(jnp.where(left, 0.0, vp), axis=-1, keepdims=True)
        outs.append(jnp.where(left, sl, sr))
    return outs[0] if len(outs) == 1 else jnp.concatenate(outs, axis=1)


def _seg_rstd(x):
    return lax.rsqrt(_seg_sum(x * x) * (1.0 / HEAD_DIM) + RMS_EPS)


def _rms_fwd(x, g, name):
    rows, d = x.shape
    tm = ROW_TILE

    def body(x_ref, g_ref, h_ref):
        xv = x_ref[...]
        r = lax.rsqrt(jnp.mean(xv * xv, axis=-1, keepdims=True) + RMS_EPS)
        h_ref[...] = ((xv * r) * g_ref[...]).astype(BF16)

    return pl.pallas_call(
        body, out_shape=jax.ShapeDtypeStruct((rows, d), BF16), grid=(rows // tm,),
        in_specs=[pl.BlockSpec((tm, d), lambda i: (i, 0)), pl.BlockSpec((1, d), lambda i: (0, 0))],
        out_specs=pl.BlockSpec((tm, d), lambda i: (i, 0)), name=name,
        compiler_params=_params("parallel"))(x, g)


def _rms_bwd(x, g, dh, dres, name):
    rows, d = x.shape
    tm = ROW_TILE

    def body(x_ref, g_ref, dh_ref, dres_ref, dx_ref, dxb_ref, dg_ref):
        xv = x_ref[...]
        r = lax.rsqrt(jnp.mean(xv * xv, axis=-1, keepdims=True) + RMS_EPS)
        xh = xv * r
        dhv = dh_ref[...]
        dxh = dhv * g_ref[...]
        dxv = dres_ref[...] + r * (dxh - xh * jnp.mean(dxh * xh, axis=-1, keepdims=True))
        dx_ref[...] = dxv
        dxb_ref[...] = dxv.astype(BF16)
        part = jnp.sum(dhv * xh, axis=0, keepdims=True)

        @pl.when(pl.program_id(0) == 0)
        def _():
            dg_ref[...] = part

        @pl.when(pl.program_id(0) > 0)
        def _():
            dg_ref[...] += part

    row = pl.BlockSpec((tm, d), lambda i: (i, 0))
    vec = pl.BlockSpec((1, d), lambda i: (0, 0))
    return pl.pallas_call(
        body, out_shape=[jax.ShapeDtypeStruct((rows, d), F32), jax.ShapeDtypeStruct((rows, d), BF16),
                         jax.ShapeDtypeStruct((1, d), F32)],
        grid=(rows // tm,), in_specs=[row, vec, row, row], out_specs=[row, row, vec],
        name=name, compiler_params=_params("arbitrary"))(x, g, dh, dres)


def _loss_grad(y, t):
    rows, d = y.shape
    tm = ROW_TILE

    def body(y_ref, t_ref, dy_ref, l_ref):
        e = y_ref[...] - t_ref[...]
        dy_ref[...] = e * (1.0 / d)
        part = jnp.zeros((1, LANES), F32) + jnp.sum(e * e) * (0.5 / d)

        @pl.when(pl.program_id(0) == 0)
        def _():
            l_ref[...] = part

        @pl.when(pl.program_id(0) > 0)
        def _():
            l_ref[...] += part

    row = pl.BlockSpec((tm, d), lambda i: (i, 0))
    return pl.pallas_call(
        body, out_shape=[jax.ShapeDtypeStruct((rows, d), F32), jax.ShapeDtypeStruct((1, LANES), F32)],
        grid=(rows // tm,), in_specs=[row, row], out_specs=[row, pl.BlockSpec((1, LANES), lambda i: (0, 0))],
        name="loss_grad", compiler_params=_params("arbitrary"))(y, t)


def _swap_halves(v):
    return pltpu.roll(v, HEAD_DIM, axis=1)


def _expand_kv(kv):
    left = lax.broadcasted_iota(jnp.int32, kv.shape, 1) < HEAD_DIM
    sw = _swap_halves(kv)
    h0 = jnp.where(left, kv, sw)
    h1 = jnp.where(left, sw, kv)
    return jnp.concatenate([h0, h0, h1, h1], axis=1)


def _reduce_kv(dkv):
    left = lax.broadcasted_iota(jnp.int32, (dkv.shape[0], LANES), 1) < HEAD_DIM
    t = dkv[:, 0:LANES] + dkv[:, LANES:2 * LANES]
    u = dkv[:, 2 * LANES:3 * LANES] + dkv[:, 3 * LANES:4 * LANES]
    t = t + _swap_halves(t)
    u = u + _swap_halves(u)
    return jnp.where(left, t, u)


def _qknorm_fwd(proj, gqa, gka, gqb, gkb):
    rows = proj.shape[0]
    tm = ROW_TILE

    def body(qa_ref, ka_ref, qb_ref, kb_ref, vb_ref, gqa_ref, gka_ref, gqb_ref, gkb_ref, oqa, oka, oqb, okb, ovb):
        for src, g_ref, dst in ((qa_ref, gqa_ref, oqa), (ka_ref, gka_ref, oka), (qb_ref, gqb_ref, oqb)):
            xv = src[...]
            dst[...] = (xv * _seg_rstd(xv)) * g_ref[...]
        kv = kb_ref[...]
        okb[...] = _expand_kv((kv * _seg_rstd(kv)) * gkb_ref[...])
        ovb[...] = _expand_kv(vb_ref[...])

    def win(width, off):
        return pl.BlockSpec((tm, width), lambda i: (i, off // width))

    vec = lambda w: pl.BlockSpec((1, w), lambda i: (0, 0))
    out = pl.BlockSpec((tm, WIDTH), lambda i: (i, 0))
    return pl.pallas_call(
        body, out_shape=[jax.ShapeDtypeStruct((rows, WIDTH), F32)] * 5, grid=(rows // tm,),
        in_specs=[win(WIDTH, OFF_QA), win(WIDTH, OFF_KA), win(WIDTH, OFF_QB), win(LANES, OFF_KB), win(LANES, OFF_VB),
                  vec(WIDTH), vec(WIDTH), vec(WIDTH), vec(LANES)],
        out_specs=[out] * 5, name="qknorm_fwd", compiler_params=_params("parallel"))(
            proj, proj, proj, proj, proj, gqa, gka, gqb, gkb)


def _norm_bwd(xv, g, dy):
    r = _seg_rstd(xv)
    xh = xv * r
    dxh = dy * g
    dx = r * (dxh - xh * (_seg_sum(dxh * xh) * (1.0 / HEAD_DIM)))
    return dx, jnp.sum(dy * xh, axis=0, keepdims=True)


def _qknorm_bwd(proj, gqa, gka, gqb, gkb, dqa, dka, dva, dqb, dkb, dvb, dga, dgb):
    rows = proj.shape[0]
    tm = ROW_TILE
    n_a = len(dqa)

    def body(*refs):
        qa_ref, ka_ref, qb_ref, kb_ref, gqa_ref, gka_ref, gqb_ref, gkb_ref = refs[:8]
        pos = 8
        dqa_refs, dka_refs, dva_refs = refs[pos:pos + n_a], refs[pos + n_a:pos + 2 * n_a], refs[pos + 2 * n_a:pos + 3 * n_a]
        pos += 3 * n_a
        dqb_ref, dkb_ref, dvb_ref, dga_ref, dgb_ref = refs[pos:pos + 5]
        dproj_ref, ogqa, ogka, ogqb, ogkb = refs[pos + 5:]

        def total(rs):
            acc = rs[0][...]
            for r in rs[1:]:
                acc = acc + r[...]
            return acc

        dx_qa, p_qa = _norm_bwd(qa_ref[...], gqa_ref[...], total(dqa_refs))
        dx_ka, p_ka = _norm_bwd(ka_ref[...], gka_ref[...], total(dka_refs))
        dx_qb, p_qb = _norm_bwd(qb_ref[...], gqb_ref[...], dqb_ref[...])
        dx_kb, p_kb = _norm_bwd(kb_ref[...], gkb_ref[...], _reduce_kv(dkb_ref[...]))
        dproj_ref[:, OFF_QA:OFF_QA + WIDTH] = dx_qa.astype(BF16)
        dproj_ref[:, OFF_KA:OFF_KA + WIDTH] = dx_ka.astype(BF16)
        dproj_ref[:, OFF_VA:OFF_VA + WIDTH] = total(dva_refs).astype(BF16)
        dproj_ref[:, OFF_QB:OFF_QB + WIDTH] = dx_qb.astype(BF16)
        dproj_ref[:, OFF_KB:OFF_KB + LANES] = dx_kb.astype(BF16)
        dproj_ref[:, OFF_VB:OFF_VB + LANES] = _reduce_kv(dvb_ref[...]).astype(BF16)
        dproj_ref[:, OFF_GA:OFF_GB] = dga_ref[...]
        dproj_ref[:, OFF_GB:D_IN] = dgb_ref[...]
        first = pl.program_id(0) == 0
        for o_ref, part in ((ogqa, p_qa), (ogka, p_ka), (ogqb, p_qb), (ogkb, p_kb)):
            @pl.when(first)
            def _(o_ref=o_ref, part=part):
                o_ref[...] = part

            @pl.when(jnp.logical_not(first))
            def _(o_ref=o_ref, part=part):
                o_ref[...] += part

    def win(width, off):
        return pl.BlockSpec((tm, width), lambda i: (i, off // width))

    vec = lambda w: pl.BlockSpec((1, w), lambda i: (0, 0))
    row = lambda w: pl.BlockSpec((tm, w), lambda i: (i, 0))
    in_specs = [win(WIDTH, OFF_QA), win(WIDTH, OFF_KA), win(WIDTH, OFF_QB), win(LANES, OFF_KB),
                vec(WIDTH), vec(WIDTH), vec(WIDTH), vec(LANES)]
    in_specs += [row(WIDTH)] * (3 * n_a + 3) + [row(D_MODEL)] * 2
    return pl.pallas_call(
        body,
        out_shape=[jax.ShapeDtypeStruct((rows, D_IN), BF16), jax.ShapeDtypeStruct((1, WIDTH), F32),
                   jax.ShapeDtypeStruct((1, WIDTH), F32), jax.ShapeDtypeStruct((1, WIDTH), F32),
                   jax.ShapeDtypeStruct((1, LANES), F32)],
        grid=(rows // tm,), in_specs=in_specs,
        out_specs=[row(D_IN), vec(WIDTH), vec(WIDTH), vec(WIDTH), vec(LANES)],
        name="qknorm_bwd", compiler_params=_params("arbitrary"))(
            proj, proj, proj, proj, gqa, gka, gqb, gkb, *dqa, *dka, *dva, dqb, dkb, dvb, dga, dgb)


def _t5_bucket(rel):
    half_b = NUM_BUCKETS // 2
    max_exact = half_b // 2
    sign = jnp.where(rel > 0, half_b, 0)
    n = jnp.abs(rel)
    nf = jnp.maximum(n, 1).astype(F32)
    large = max_exact + (jnp.log(nf / max_exact) / math.log(MAX_DISTANCE / max_exact)
                         * (half_b - max_exact)).astype(jnp.int32)
    large = jnp.minimum(large, half_b - 1)
    return sign + jnp.where(n < max_exact, n, large)


def _band_buckets(blk, dilation):
    i = jnp.arange(blk, dtype=jnp.int32)[:, None]
    j = jnp.arange(3 * blk, dtype=jnp.int32)[None, :]
    rel = j - blk - i
    return jnp.where(jnp.abs(rel) <= blk, _t5_bucket(rel * dilation), -1)


def _bias_tiles(table, buckets, head_off, name):
    blk = buckets.shape[0]

    def body(tab_ref, bk_ref, o_ref):
        h = pl.program_id(0) + head_off
        bk = bk_ref[...]
        acc = jnp.full(bk.shape, NEG_INF, F32)
        for b in range(NUM_BUCKETS):
            acc = jnp.where(bk == b, tab_ref[b, h], acc)
        o_ref[0] = acc

    return pl.pallas_call(
        body, out_shape=jax.ShapeDtypeStruct((N_HEADS, blk, 3 * blk), F32), grid=(N_HEADS,),
        in_specs=[pl.BlockSpec(memory_space=pltpu.SMEM), pl.BlockSpec((blk, 3 * blk), lambda h: (0, 0))],
        out_specs=pl.BlockSpec((1, blk, 3 * blk), lambda h: (h, 0, 0)),
        name=name, compiler_params=_params("parallel"))(table, buckets)


def _table_grad(dbias, buckets, name):
    blk = buckets.shape[0]

    def body(db_ref, bk_ref, o_ref):
        bk = bk_ref[...]
        dbv = db_ref[0]
        lane = lax.broadcasted_iota(jnp.int32, (1, LANES), 1)
        acc = jnp.zeros((1, LANES), F32)
        for b in range(NUM_BUCKETS):
            acc = jnp.where(lane == b, jnp.sum(jnp.where(bk == b, dbv, 0.0)), acc)
        o_ref[0] = acc

    out = pl.pallas_call(
        body, out_shape=jax.ShapeDtypeStruct((N_HEADS, 1, LANES), F32), grid=(N_HEADS,),
        in_specs=[pl.BlockSpec((1, blk, 3 * blk), lambda h: (h, 0, 0)), pl.BlockSpec((blk, 3 * blk), lambda h: (0, 0))],
        out_specs=pl.BlockSpec((1, 1, LANES), lambda h: (h, 0, 0)),
        name=name, compiler_params=_params("parallel"))(dbias, buckets)
    return out[:, 0, :NUM_BUCKETS]


def _dot_nt(a, b):
    return lax.dot_general(a, b, (((1,), (1,)), ((), ())), preferred_element_type=F32)


def _dot_tn(a, b):
    return lax.dot_general(a, b, (((0,), (0,)), ((), ())), preferred_element_type=F32)


def _stack_pair(x2, left):
    return jnp.concatenate([jnp.where(left, x2, 0.0), jnp.where(left, 0.0, x2)], axis=0).astype(BF16)


def _attn_geometry(blk, d):
    subs = max(1, ITEMS // d)
    return subs, ITEMS // subs, blk * d


def _item_of(j, r0, per_group):
    return j // per_group, r0 + j % per_group


def _span_rows(ref, first, r, blk, d):
    if d == 1:
        return ref[first:first + blk, :]
    return ref[pl.ds(first + r, blk, stride=d), :]


def _set_span_rows(ref, first, r, blk, d, val, add=False):
    idx = slice(first, first + blk) if d == 1 else pl.ds(first + r, blk, stride=d)
    ref[idx, :] = ref[idx, :] + val if add else val


def _for_groups(group, groups, per_group):
    if groups == 1:
        group(0)
    else:
        def step(g, carry):
            group(g * per_group)
            return carry

        lax.fori_loop(0, groups, step, 0)


def _key_rows(p_ref, c_ref, n_ref, s, r, subs, halo, blk, d):
    parts = []
    for span in (s - 1, s, s + 1):
        if span < 0:
            parts.append(_span_rows(p_ref, 0, r, blk, d))
        elif span == subs:
            parts.append(_span_rows(n_ref, 0, r, blk, d))
        else:
            parts.append(_span_rows(c_ref, span * halo, r, blk, d))
    return jnp.concatenate(parts, axis=0)


def _item_penalty(t, nct, s, subs, blk):
    first_ok = True if s > 0 else t > 0
    last_ok = True if s < subs - 1 else t < nct - 1
    col = lax.broadcasted_iota(jnp.int32, (1, 3 * blk), 1)
    ok = jnp.logical_and(jnp.logical_or(col >= blk, first_ok), jnp.logical_or(col < 2 * blk, last_ok))
    return jnp.where(ok, 0.0, NEG_INF).astype(F32)


def _attn_specs(seq, blk, d, step_of, col=0):
    subs, _, halo = _attn_geometry(blk, d)
    last, first = seq // halo - 1, col // LANES
    cur = pl.BlockSpec((subs * halo, LANES), lambda hp, t: (step_of(t), first + hp))
    prev = pl.BlockSpec((halo, LANES), lambda hp, t: (jnp.clip(step_of(t) * subs - 1, 0, last), first + hp))
    nxt = pl.BlockSpec((halo, LANES), lambda hp, t: (jnp.minimum((step_of(t) + 1) * subs, last), first + hp))
    return cur, prev, nxt


def _attn_fwd(q, k, v, bias, sink, blk, d, name, v_col=0):
    seq = q.shape[0]
    subs, per_group, halo = _attn_geometry(blk, d)
    chunk, groups = subs * halo, d // per_group
    nct = seq // chunk
    has_sink = sink is not None
    scale = HEAD_DIM ** -0.5

    def body(*refs):
        q_ref, kp, kc, kn, vp, vc, vn, b_ref = refs[:8]
        s_ref = refs[8] if has_sink else None
        o_ref, l_ref = refs[-2], refs[-1]
        t = pl.program_id(1)
        left = lax.broadcasted_iota(jnp.int32, (1, LANES), 1) < HEAD_DIM
        bias2 = b_ref[...]

        def group(r0):
            scores, vcats = [], []
            for j in range(ITEMS):
                s, r = _item_of(j, r0, per_group)
                qs = _stack_pair(_span_rows(q_ref, s * halo, r, blk, d) * scale, left)
                kcat = _key_rows(kp, kc, kn, s, r, subs, halo, blk, d).astype(BF16)
                scores.append(_dot_nt(qs, kcat) + bias2 + _item_penalty(t, nct, s, subs, blk))
                vcats.append(_key_rows(vp, vc, vn, s, r, subs, halo, blk, d).astype(BF16))
            ms = [jnp.max(s, axis=-1, keepdims=True) for s in scores]
            if has_sink:
                sk = s_ref[...]
                ms = [jnp.maximum(m, sk) for m in ms]
            ps = [jnp.exp(s - m) for s, m in zip(scores, ms)]
            dens = [jnp.sum(p, axis=-1, keepdims=True) for p in ps]
            if has_sink:
                dens = [den + jnp.exp(sk - m) for den, m in zip(dens, ms)]
            pns = [(p * (1.0 / den)).astype(BF16) for p, den in zip(ps, dens)]
            lses = [m + jnp.log(den) for m, den in zip(ms, dens)]
            for j in range(ITEMS):
                s, r = _item_of(j, r0, per_group)
                o2 = jnp.dot(pns[j], vcats[j], preferred_element_type=F32)
                _set_span_rows(o_ref, s * halo, r, blk, d, jnp.where(left, o2[:blk], o2[blk:]))
                _set_span_rows(l_ref, s * halo, r, blk, d, jnp.where(left, lses[j][:blk], lses[j][blk:]))

        _for_groups(group, groups, per_group)

    cur, prev, nxt = _attn_specs(seq, blk, d, lambda t: t)
    v_cur, v_prev, v_nxt = _attn_specs(seq, blk, d, lambda t: t, v_col)
    in_specs = [cur, prev, cur, nxt, v_prev, v_cur, v_nxt, pl.BlockSpec((2 * blk, 3 * blk), lambda hp, t: (hp, 0))]
    args = [q, k, k, k, v, v, v, bias]
    if has_sink:
        in_specs.append(pl.BlockSpec((2 * blk, 1), lambda hp, t: (hp, 0)))
        args.append(sink)
    return pl.pallas_call(
        body, out_shape=[jax.ShapeDtypeStruct((seq, WIDTH), F32)] * 2, grid=(N_PAIRS, nct),
        in_specs=in_specs, out_specs=[cur, cur], name=name,
        compiler_params=_params("parallel", "parallel"))(*args)


def _attn_bwd(q, k, v, do, lse, delta, bias, sink, blk, d, name, v_col=0):
    seq = q.shape[0]
    subs, per_group, halo = _attn_geometry(blk, d)
    chunk, groups = subs * halo, d // per_group
    nct = seq // chunk
    has_sink = sink is not None
    n_in = 12 if has_sink else 11
    scale = HEAD_DIM ** -0.5

    def body(*refs):
        q_ref, kp, kc, kn, vp, vc, vn, do_ref, l_ref, d_ref, b_ref = refs[:11]
        s_ref = refs[11] if has_sink else None
        dq_ref, dk_ref, dv_ref, db_ref = refs[n_in:n_in + 4]
        ds_ref = refs[n_in + 4] if has_sink else None
        wk, wv = refs[-2], refs[-1]
        t = pl.program_id(1)

        @pl.when(t == 0)
        def _():
            wk[...] = jnp.zeros_like(wk)
            wv[...] = jnp.zeros_like(wv)
            db_ref[...] = jnp.zeros_like(db_ref)
            if has_sink:
                ds_ref[...] = jnp.zeros_like(ds_ref)

        @pl.when(t > 0)
        def _():
            for w in (wk, wv):
                keep = w[chunk:2 * chunk + halo]
                w[0:chunk + halo] = keep
                w[chunk + halo:2 * chunk + halo] = jnp.zeros((chunk, LANES), F32)

        @pl.when(t < nct)
        def _():
            lane = lax.broadcasted_iota(jnp.int32, (1, LANES), 1)
            left = lane < HEAD_DIM
            bias2 = b_ref[...]

            def group(r0):
                qss, doss, kcats, scores, dps, lcols, dcols = [], [], [], [], [], [], []
                for j in range(ITEMS):
                    s, r = _item_of(j, r0, per_group)
                    qs = _stack_pair(_span_rows(q_ref, s * halo, r, blk, d) * scale, left)
                    dos = _stack_pair(_span_rows(do_ref, s * halo, r, blk, d), left)
                    kcat = _key_rows(kp, kc, kn, s, r, subs, halo, blk, d).astype(BF16)
                    vcat = _key_rows(vp, vc, vn, s, r, subs, halo, blk, d).astype(BF16)
                    l2, d2 = _span_rows(l_ref, s * halo, r, blk, d), _span_rows(d_ref, s * halo, r, blk, d)
                    lcols.append(jnp.concatenate([jnp.max(jnp.where(left, l2, NEG_INF), axis=-1, keepdims=True),
                                                  jnp.max(jnp.where(left, NEG_INF, l2), axis=-1, keepdims=True)], axis=0))
                    dcols.append(jnp.concatenate([jnp.sum(jnp.where(lane == 0, d2, 0.0), axis=-1, keepdims=True),
                                                  jnp.sum(jnp.where(lane == HEAD_DIM, d2, 0.0), axis=-1, keepdims=True)],
                                                 axis=0))
                    scores.append(_dot_nt(qs, kcat) + bias2 + _item_penalty(t, nct, s, subs, blk))
                    dps.append(_dot_nt(dos, vcat))
                    qss.append(qs)
                    doss.append(dos)
                    kcats.append(kcat)
                ps = [jnp.exp(s - lc) for s, lc in zip(scores, lcols)]
                dss = [p * (dp - dc) for p, dp, dc in zip(ps, dps, dcols)]
                db_ref[...] += functools.reduce(lambda a, b: a + b, dss)
                if has_sink:
                    sk = s_ref[...]
                    ds_ref[...] -= functools.reduce(lambda a, b: a + b, [dc * jnp.exp(sk - lc) for dc, lc in zip(dcols, lcols)])
                dsbs = [ds.astype(BF16) for ds in dss]
                for j in range(ITEMS):
                    s, r = _item_of(j, r0, per_group)
                    dq2 = jnp.dot(dsbs[j], kcats[j], preferred_element_type=F32) * scale
                    _set_span_rows(dq_ref, s * halo, r, blk, d, jnp.where(left, dq2[:blk], dq2[blk:]))
                news = [(_dot_tn(dsbs[j], qss[j]), _dot_tn(ps[j].astype(BF16), doss[j])) for j in range(ITEMS)]
                for which, w in enumerate((wk, wv)):
                    for jr in range(per_group):
                        for sp in range(-1, subs + 1):
                            parts = [news[s * per_group + jr][which][(sp - s + 1) * blk:(sp - s + 2) * blk]
                                     for s in range(subs) if 0 <= sp - s + 1 < 3]
                            _set_span_rows(w, chunk + sp * halo, r0 + jr, blk, d,
                                           functools.reduce(lambda x, y: x + y, parts), add=True)

            _for_groups(group, groups, per_group)

        dk_ref[...] = wk[0:chunk]
        dv_ref[...] = wv[0:chunk]

    cur, prev, nxt = _attn_specs(seq, blk, d, lambda t: jnp.minimum(t, nct - 1))
    v_cur, v_prev, v_nxt = _attn_specs(seq, blk, d, lambda t: jnp.minimum(t, nct - 1), v_col)
    lag = pl.BlockSpec((chunk, LANES), lambda hp, t: (jnp.maximum(t - 1, 0), hp))
    band = pl.BlockSpec((2 * blk, 3 * blk), lambda hp, t: (hp, 0))
    col = pl.BlockSpec((2 * blk, 1), lambda hp, t: (hp, 0))
    in_specs = [cur, prev, cur, nxt, v_prev, v_cur, v_nxt, cur, cur, cur, band]
    args = [q, k, k, k, v, v, v, do, lse, delta, bias]
    out_shape = [jax.ShapeDtypeStruct((seq, WIDTH), F32)] * 3 + [jax.ShapeDtypeStruct((N_HEADS * blk, 3 * blk), F32)]
    out_specs = [cur, lag, lag, band]
    if has_sink:
        in_specs.append(col)
        args.append(sink)
        out_shape.append(jax.ShapeDtypeStruct((N_HEADS * blk, 1), F32))
        out_specs.append(col)
    window = pltpu.VMEM((2 * chunk + halo, LANES), F32)
    return pl.pallas_call(
        body, out_shape=out_shape, grid=(N_PAIRS, nct + 1), in_specs=in_specs, out_specs=out_specs,
        scratch_shapes=[window, window], name=name,
        compiler_params=_params("arbitrary", "arbitrary"))(*args)


def _combine_patterns(outs, lses):
    def combine(*tiles):
        os_, ls = tiles[:len(outs)], tiles[len(outs):]
        m = functools.reduce(jnp.maximum, ls)
        es = [jnp.exp(l - m) for l in ls]
        den = functools.reduce(lambda a, b: a + b, es)
        num = functools.reduce(lambda a, b: a + b, [e * o for e, o in zip(es, os_)])
        return num / den, m + jnp.log(den)

    return _ew(combine, [*outs, *lses], [F32, F32], "combine_a")


def _row_dots(dy, y, name):
    return _ew(lambda dy_, y_: (_seg_sum(dy_ * y_),), [dy, y], [F32], name)[0]


def _tile_gain(g, reps):
    return jnp.tile(g[None, :], (1, reps))


def _local_step(x, p, target, w_in_of, rest_of, small):
    rel_table = small["rel_table"]
    buckets_a = [_band_buckets(blk, d) for blk, d in DILATED]
    buckets_b = _band_buckets(BLK_B, 1)
    bias_a = [_bias_tiles(rel_table, bk, 0, "bias_a").reshape(N_HEADS * bk.shape[0], -1) for bk in buckets_a]
    bias_b = _bias_tiles(rel_table, buckets_b, N_HEADS, "bias_b").reshape(N_HEADS * BLK_B, -1)

    saved = []
    for l in range(DEPTH):
        g_mix, g_ffn, g_ple = (small[n][l][None, :] for n in ("norm_mix_g", "norm_ffn_g", "norm_ple_g"))
        gqa, gka, gqb = (_tile_gain(small[n][l], N_HEADS) for n in ("qnorm_a_g", "knorm_a_g", "qnorm_b_g"))
        gkb = _tile_gain(small["knorm_b_g"][l], N_KV_B)
        sink = jnp.repeat(small["sink_b"][l], BLK_B)[:, None]

        h = _rms_fwd(x, g_mix, "rms_mix")
        w_in = w_in_of(l, (h, bias_a, bias_b))
        proj = _mm(h, w_in, "nt", F32, "mm_in")
        qa, ka, qb, kb, vb = _qknorm_fwd(proj, gqa, gka, gqb, gkb)
        outs, lses = [], []
        for (blk, d), bias in zip(DILATED, bias_a):
            o, ls = _attn_fwd(qa, ka, proj, bias, None, blk, d, f"attn_a{d}_fwd", v_col=OFF_VA)
            outs.append(o)
            lses.append(ls)
        ya, lse_a = _combine_patterns(outs, lses)
        yb, lse_b = _attn_fwd(qb, kb, vb, bias_b, sink, BLK_B, 1, "attn_b_fwd")
        w = dict(rest_of(l, yb), w_in=w_in)
        def gate(products, extra):
            (ca_, cb_), (ga_, gb_) = products, extra
            return _sigmoid(ga_) * ca_ + _sigmoid(gb_) * cb_, ca_, cb_

        merged, ca, cb = _mm_fused([(ya, w["w_branch_a"], "nn"), (yb, w["w_branch_b"], "nn")],
                                   [(proj, OFF_GA), (proj, OFF_GB)], gate, [BF16, BF16, BF16], "mm_branches_gate")
        x1 = _mm(merged, w["w_out"], "nn", F32, "mm_out", res=x)

        h2 = _rms_fwd(x1, g_ffn, "rms_ffn")

        def swiglu(products, extra):
            a_, u_ = products
            return (a_ * _sigmoid(a_)) * u_, a_, u_

        hid, a, u = _mm_fused([(h2, w["w_ffn_gate"], "nt"), (h2, w["w_ffn_up"], "nt")], [], swiglu,
                              [BF16, BF16, BF16], "mm_ffn_gate_up")
        x2 = _mm(hid, w["w_ffn_down"], "nn", F32, "mm_ffn_down", res=x1)

        h3 = _rms_fwd(x2, g_ple, "rms_ple")

        def ple(products, extra):
            z_, e_ = products
            return extra[0] + _sigmoid(z_) * e_, z_, e_

        x3, z, e = _mm_fused([(h3, w["w_ple_gate"], "nn"), (p[l], w["w_ple_proj"], "nn")], [(x2, 0)], ple,
                             [F32, BF16, BF16], "mm_ple")
        saved.append(dict(w=w, x0=x, h=h, proj=proj, qa=qa, ka=ka, qb=qb, kb=kb, vb=vb, ya=ya, lse_a=lse_a,
                          yb=yb, lse_b=lse_b, ca=ca, cb=cb, merged=merged, x1=x1, h2=h2, a=a, u=u, hid=hid,
                          x2=x2, h3=h3, z=z, e=e))
        x = x3

    dx, loss_acc = _loss_grad(x, target)
    loss = loss_acc[0, 0]

    gbig = [{} for _ in range(DEPTH)]
    marks = [{} for _ in range(DEPTH)]
    gsmall = {n: [None] * DEPTH for n in SMALL if n != "rel_table"}
    dbias_a = [[] for _ in DILATED]
    dbias_b = []

    for l in reversed(range(DEPTH)):
        sv = saved[l]
        w = sv["w"]
        g_mix, g_ffn, g_ple = (small[n][l][None, :] for n in ("norm_mix_g", "norm_ffn_g", "norm_ple_g"))
        gqa, gka, gqb = (_tile_gain(small[n][l], N_HEADS) for n in ("qnorm_a_g", "knorm_a_g", "qnorm_b_g"))
        gkb = _tile_gain(small["knorm_b_g"][l], N_KV_B)
        sink = jnp.repeat(small["sink_b"][l], BLK_B)[:, None]

        def ple_bwd(dx_, z_, e_):
            s = _sigmoid(z_.astype(F32))
            return dx_ * s, dx_ * e_.astype(F32) * (s * (1.0 - s))

        de, dz = _ew(ple_bwd, [dx, sv["z"], sv["e"]], [BF16, BF16], "ple_bwd")
        gbig[l]["w_ple_proj"] = _mm(p[l], de, "tn", BF16, "mm_d_ple_proj")
        gbig[l]["w_ple_gate"] = _mm(sv["h3"], dz, "tn", BF16, "mm_d_ple_gate")
        dh3 = _mm(dz, w["w_ple_gate"], "nt", F32, "mm_dh3")
        dx, dxb, gsmall["norm_ple_g"][l] = _rms_bwd(sv["x2"], g_ple, dh3, dx, "rms_ple_bwd")

        gbig[l]["w_ffn_down"] = _mm(sv["hid"], dxb, "tn", BF16, "mm_d_ffn_down")

        def swiglu_bwd(products, extra):
            dh_, a_, u_ = products[0], extra[0].astype(F32), extra[1].astype(F32)
            s = _sigmoid(a_)
            return dh_ * u_ * (s * (1.0 + a_ * (1.0 - s))), dh_ * (a_ * s)

        da, du = _mm_fused([(dxb, w["w_ffn_down"], "nt")], [(sv["a"], 0), (sv["u"], 0)], swiglu_bwd, [BF16, BF16],
                           "mm_dhid_swiglu_bwd")
        gbig[l]["w_ffn_gate"] = _mm(da, sv["h2"], "tn", BF16, "mm_d_ffn_gate")
        gbig[l]["w_ffn_up"] = _mm(du, sv["h2"], "tn", BF16, "mm_d_ffn_up")
        dh2 = _mm(da, w["w_ffn_gate"], "nn", F32, "mm_dh2_gate")
        dh2 = _mm(du, w["w_ffn_up"], "nn", F32, "mm_dh2_up", res=dh2)
        dx, dxb, gsmall["norm_ffn_g"][l] = _rms_bwd(sv["x1"], g_ffn, dh2, dx, "rms_ffn_bwd")

        gbig[l]["w_out"] = _mm(sv["merged"], dxb, "tn", BF16, "mm_d_out")

        def gate_bwd(products, extra):
            dm_, ca_, cb_ = products[0], extra[0].astype(F32), extra[1].astype(F32)
            sa, sb = _sigmoid(extra[2]), _sigmoid(extra[3])
            return dm_ * sa, dm_ * sb, dm_ * ca_ * (sa * (1.0 - sa)), dm_ * cb_ * (sb * (1.0 - sb))

        dca, dcb, dga, dgb = _mm_fused(
            [(dxb, w["w_out"], "nt")], [(sv["ca"], 0), (sv["cb"], 0), (sv["proj"], OFF_GA), (sv["proj"], OFF_GB)],
            gate_bwd, [BF16, BF16, BF16, BF16], "mm_dmerged_gate_bwd")
        gbig[l]["w_branch_a"] = _mm(sv["ya"], dca, "tn", BF16, "mm_d_branch_a")
        gbig[l]["w_branch_b"] = _mm(sv["yb"], dcb, "tn", BF16, "mm_d_branch_b")
        dya = _mm(dca, w["w_branch_a"], "nt", F32, "mm_dya")
        dyb = _mm(dcb, w["w_branch_b"], "nt", F32, "mm_dyb")

        delta_a = _row_dots(dya, sv["ya"], "attn_a_row_dots")
        delta_b = _row_dots(dyb, sv["yb"], "attn_b_row_dots")

        dqa, dka, dva = [], [], []
        for (blk, d), bias, bk in zip(DILATED, bias_a, buckets_a):
            dq_, dk_, dv_, db_ = _attn_bwd(sv["qa"], sv["ka"], sv["proj"], dya, sv["lse_a"], delta_a, bias, None, blk, d,
                                           f"attn_a{d}_bwd", v_col=OFF_VA)
            dqa.append(dq_)
            dka.append(dk_)
            dva.append(dv_)
            dbias_a[len(dqa) - 1].append(db_)
        dqb, dkb, dvb, db_, dsink = _attn_bwd(sv["qb"], sv["kb"], sv["vb"], dyb, sv["lse_b"], delta_b, bias_b, sink,
                                              BLK_B, 1, "attn_b_bwd")
        dbias_b.append(db_)
        gsmall["sink_b"][l] = dsink.reshape(N_HEADS, BLK_B).sum(axis=1)

        dproj, pqa, pka, pqb, pkb = _qknorm_bwd(sv["proj"], gqa, gka, gqb, gkb, dqa, dka, dva, dqb, dkb, dvb, dga, dgb)
        marks[l]["attn_bwd_done"] = dproj
        gsmall["qnorm_a_g"][l] = pqa.reshape(N_HEADS, HEAD_DIM).sum(0)
        gsmall["knorm_a_g"][l] = pka.reshape(N_HEADS, HEAD_DIM).sum(0)
        gsmall["qnorm_b_g"][l] = pqb.reshape(N_HEADS, HEAD_DIM).sum(0)
        gsmall["knorm_b_g"][l] = pkb.reshape(N_KV_B, HEAD_DIM).sum(0)
        gbig[l]["w_in"] = _mm(dproj, sv["h"], "tn", BF16, "mm_d_in")
        dh = _mm(dproj, w["w_in"], "nn", F32, "mm_dh")
        dx, _, gsmall["norm_mix_g"][l] = _rms_bwd(sv["x0"], g_mix, dh, dx, "rms_mix_bwd")
        gsmall["norm_mix_g"][l] = gsmall["norm_mix_g"][l][0]
        gsmall["norm_ffn_g"][l] = gsmall["norm_ffn_g"][l][0]
        gsmall["norm_ple_g"][l] = gsmall["norm_ple_g"][l][0]

    gsmall = {n: jnp.stack(v) for n, v in gsmall.items()}
    dtable_a = sum(_table_grad(sum(dbs).reshape(N_HEADS, blk, 3 * blk), bk, "table_grad_a")
                   for dbs, (blk, _), bk in zip(dbias_a, DILATED, buckets_a))
    dtable_b = _table_grad(sum(dbias_b).reshape(N_HEADS, BLK_B, 3 * BLK_B), buckets_b, "table_grad_b")
    gsmall["rel_table"] = jnp.concatenate([dtable_a, dtable_b], axis=0).T
    return loss, dx, gbig, gsmall, marks


def _place():
    return lax.axis_index("x"), lax.axis_index("y"), lax.axis_index("c")


def _flip(v, bit):
    return 1 - v if bit else v


CHIP_RELATIONS = ((0, 1), (1, 0), (1, 1))
ANY = pl.BlockSpec(memory_space=pl.ANY)


def _allgather_body(w_refs, out_refs, send_sems, recv_sems):
    x, y, c = _place()
    chips = [(_flip(x, a), _flip(y, b)) for a, b in CHIP_RELATIONS]

    def make(g):
        w_ref, out_ref = w_refs[g], out_refs[g]
        half = w_ref.shape[0] // 2

        def part(px, py, pc):
            return out_ref.at[2 * px + py, pl.ds(pc * half, half), :]

        def copy(k, block, to, src=None):
            return pltpu.make_async_remote_copy(
                src_ref=part(*block) if src is None else src, dst_ref=part(*block),
                send_sem=send_sems.at[7 * g + k], recv_sem=recv_sems.at[7 * g + k], device_id=to,
                device_id_type=MESH_ID)

        own = pltpu.make_async_remote_copy(
            src_ref=w_ref, dst_ref=out_ref.at[2 * x + y], send_sem=send_sems.at[7 * g + 6],
            recv_sem=recv_sems.at[7 * g + 6], device_id=(x, y, 1 - c), device_id_type=MESH_ID)
        first = [copy(k, (x, y, c), (*chip, c), src=w_ref.at[pl.ds(c * half, half), :]) for k, chip in enumerate(chips)]
        passed = [copy(3 + k, (*chip, c), (x, y, 1 - c)) for k, chip in enumerate(chips)]
        arrive = [copy(k, (*chip, c), (x, y, c)) for k, chip in enumerate(chips)]
        arrive2 = [copy(3 + k, (*chip, 1 - c), (x, y, c)) for k, chip in enumerate(chips)]
        return own, first, passed, arrive, arrive2

    made = [make(g) for g in range(len(w_refs))]
    for own, first, _, _, _ in made:
        own.start()
        for cp in first:
            cp.start()
    for _, _, passed, arrive, _ in made:
        for k in range(3):
            arrive[k].wait_recv()
            passed[k].start()
    for own, first, passed, _, arrive2 in made:
        for k in range(3):
            arrive2[k].wait_recv()
        own.wait_recv()
        for cp in first + passed + [own]:
            cp.wait_send()


def _sibling(x, y, c):
    return [(x, y, 1 - c)]


def _same_core_of_other_chips(x, y, c):
    return [(_flip(x, a), _flip(y, b), c) for a, b in CHIP_RELATIONS]


def _exchange(body, ins, out_types, n_sems, name, sequencer=None):
    n = len(ins)
    sems = (pltpu.SemaphoreType.DMA((n_sems,)), pltpu.SemaphoreType.DMA((n_sems,)))
    if sequencer is None:
        in_place = out_types is None
        out_shape = [jax.ShapeDtypeStruct(a.shape, a.dtype) for a in ins] if in_place else out_types

        def tc_body(*refs):
            body(refs[:n], refs[n:n + len(out_shape)], refs[-2], refs[-1])

        return list(pl.pallas_call(
            tc_body, out_shape=out_shape, in_specs=[ANY] * n, out_specs=[ANY] * len(out_shape),
            input_output_aliases={g: g for g in range(n)} if in_place else {}, scratch_shapes=list(sems), name=name)(*ins))

    collective_id, peers = sequencer
    hbm = pltpu.MemorySpace.HBM
    in_refs = [jax.new_ref(a, memory_space=hbm) for a in ins]
    out_refs = in_refs if out_types is None else [jax.empty_ref(t, memory_space=hbm) for t in out_types]

    @pl.kernel(mesh=plsc.ScalarSubcoreMesh(axis_name="sequencer", num_cores=1), name=name, scratch_types=sems,
               compiler_params=pltpu.CompilerParams(collective_id=collective_id))
    def launch(send_sems, recv_sems):
        barrier = pltpu.get_barrier_semaphore()
        devices = peers(*_place())
        for device in devices:
            pl.semaphore_signal(barrier, inc=1, device_id=device, device_id_type=MESH_ID)
        pl.semaphore_wait(barrier, len(devices))
        body(in_refs, out_refs, send_sems, recv_sems)

    launch()
    return [r[...] for r in out_refs]


def _allgather(shards, name, sequencer=None):
    out_types = [jax.ShapeDtypeStruct((N_CHIPS,) + s.shape, s.dtype) for s in shards]
    if sequencer is not None:
        sequencer = (sequencer, lambda x, y, c: _sibling(x, y, c) + _same_core_of_other_chips(x, y, c))
    return _exchange(_allgather_body, shards, out_types, 7 * len(shards), name, sequencer)


def _half_tile(half):
    return max(t for t in range(16, 1025, 16) if half % t == 0)


def _run_copies(cps):
    for cp in cps:
        cp.start()
    for cp in cps:
        cp.wait_recv()
    for cp in cps:
        cp.wait_send()


def _sibling_halves(gsends, name, sequencer=None):
    def body(g_refs, out_refs, send_sems, recv_sems):
        x, y, c = _place()
        cps = []
        for g, (g_ref, out_ref) in enumerate(zip(g_refs, out_refs)):
            half = g_ref.shape[1] // 2
            cps.append(pltpu.make_async_remote_copy(
                src_ref=g_ref.at[:, pl.ds((1 - c) * half, half), :], dst_ref=out_ref,
                send_sem=send_sems.at[g], recv_sem=recv_sems.at[g], device_id=(x, y, 1 - c), device_id_type=MESH_ID))
        _run_copies(cps)

    out_types = [jax.ShapeDtypeStruct((s.shape[0], s.shape[1] // 2, s.shape[2]), s.dtype) for s in gsends]
    return _exchange(body, gsends, out_types, len(gsends), name, sequencer and (sequencer, _sibling))


def _chip_sums(gsend, sib, place):
    n, rows, cols = gsend.shape
    half = rows // 2
    tm = _half_tile(half)
    nblk = half // tm

    def body(s_ref, g_ref, sib_ref, o_ref):
        o_ref[0] = (g_ref[0].astype(F32) + sib_ref[0].astype(F32)).astype(o_ref.dtype)

    grid_spec = pltpu.PrefetchScalarGridSpec(
        num_scalar_prefetch=1, grid=(n, nblk),
        in_specs=[pl.BlockSpec((1, tm, cols), lambda k, i, s: (jnp.bitwise_xor(s[0], k), s[1] * nblk + i, 0)),
                  pl.BlockSpec((1, tm, cols), lambda k, i, s: (jnp.bitwise_xor(s[0], k), i, 0))],
        out_specs=pl.BlockSpec((1, tm, cols), lambda k, i, s: (k, i, 0)))
    return pl.pallas_call(
        body, out_shape=jax.ShapeDtypeStruct((n, half, cols), BF16), grid_spec=grid_spec,
        name="rs_chip_sums", compiler_params=_params("parallel", "parallel"))(place, gsend, sib)


def _exchange_chip_sums(tsends, name, sequencer=None):
    def body(t_refs, out_refs, send_sems, recv_sems):
        x, y, c = _place()
        cps = []
        for g, (t_ref, out_ref) in enumerate(zip(t_refs, out_refs)):
            for k, device in enumerate(_same_core_of_other_chips(x, y, c)):
                cps.append(pltpu.make_async_remote_copy(
                    src_ref=t_ref.at[k + 1], dst_ref=out_ref.at[k], send_sem=send_sems.at[3 * g + k],
                    recv_sem=recv_sems.at[3 * g + k], device_id=device, device_id_type=MESH_ID))
        _run_copies(cps)

    out_types = [jax.ShapeDtypeStruct((3,) + s.shape[1:], s.dtype) for s in tsends]
    return _exchange(body, tsends, out_types, 3 * len(tsends), name,
                     sequencer and (sequencer, _same_core_of_other_chips))


def _final_sum(tsend, recv, place):
    n, half, cols = tsend.shape
    tm = _half_tile(half)
    nblk = half // tm

    def body(s_ref, t_ref, r_ref, o_ref):
        o_ref[...] = ((t_ref[0].astype(F32) + r_ref[0].astype(F32)) + r_ref[1].astype(F32)) + r_ref[2].astype(F32)

    grid_spec = pltpu.PrefetchScalarGridSpec(
        num_scalar_prefetch=1, grid=(nblk,),
        in_specs=[pl.BlockSpec((1, tm, cols), lambda i, s: (0, i, 0)), pl.BlockSpec((n - 1, tm, cols), lambda i, s: (0, i, 0))],
        out_specs=pl.BlockSpec((tm, cols), lambda i, s: (s[1] * nblk + i, 0)))
    return pl.pallas_call(
        body, out_shape=jax.ShapeDtypeStruct((2 * half, cols), F32), grid_spec=grid_spec, name="rs_final_sum",
        compiler_params=_params("parallel"))(place, tsend, recv)


def _join_halves(gfulls, name, sequencer=None):
    def body(g_refs, out_refs, send_sems, recv_sems):
        x, y, c = _place()
        n = len(g_refs)

        def copy(g, pc):
            half = g_refs[g].shape[0] // 2
            return pltpu.make_async_remote_copy(
                src_ref=g_refs[g].at[pl.ds(pc * half, half), :], dst_ref=out_refs[g].at[pl.ds(pc * half, half), :],
                send_sem=send_sems.at[g], recv_sem=recv_sems.at[g], device_id=(x, y, 1 - c), device_id_type=MESH_ID)

        mine = [copy(g, c) for g in range(n)]
        for cp in mine:
            cp.start()
        for g in range(n):
            copy(g, 1 - c).wait_recv()
        for cp in mine:
            cp.wait_send()

    return _exchange(body, gfulls, None, len(gfulls), name, sequencer and (sequencer, _sibling))


def _allreduce_small(v):
    rows, cols = v.shape

    def body(v_ref, out_ref, buf, send_sems, recv_sems):
        x, y, c = _place()
        cps = []
        for k in range(1, 8):
            peer = (_flip(x, (k >> 2) & 1), _flip(y, (k >> 1) & 1), _flip(c, k & 1))
            cps.append(pltpu.make_async_remote_copy(
                src_ref=v_ref, dst_ref=buf.at[k - 1], send_sem=send_sems.at[k - 1], recv_sem=recv_sems.at[k - 1],
                device_id=peer, device_id_type=MESH_ID))
        for cp in cps:
            cp.start()
        for cp in cps:
            cp.wait_recv()
        for cp in cps:
            cp.wait_send()
        t0 = v_ref[...] + buf[0]
        t1 = buf[1] + buf[2]
        t2 = buf[3] + buf[4]
        t3 = buf[5] + buf[6]
        out_ref[...] = (t0 + t1) + (t2 + t3)

    vm = pl.BlockSpec(memory_space=pltpu.VMEM)
    return pl.pallas_call(
        body, out_shape=jax.ShapeDtypeStruct((rows, cols), F32), in_specs=[vm], out_specs=vm,
        scratch_shapes=[pltpu.VMEM((7, rows, cols), F32), pltpu.SemaphoreType.DMA((7,)), pltpu.SemaphoreType.DMA((7,))],
        name="allreduce_small")(v)


BIG_INFO = {n: (shape, ax) for n, shape, ax in BIG}
GROUPS = (("w_in",), ("w_ffn_gate", "w_ffn_up", "w_ffn_down", "w_out", "w_ple_gate"),
          ("w_branch_a", "w_branch_b", "w_ple_proj"))


def _shard_shape(name):
    (k, m), ax = BIG_INFO[name]
    return (k // N_CHIPS, m) if ax == 0 else (k, m // N_CHIPS)


def _group_rows(group):
    offs, off = {}, 0
    for n in group:
        offs[n] = off
        off += _shard_shape(n)[0]
    return offs, off


def _pack_groups(shards, layer, dtype):
    return [jnp.concatenate([shards[n][layer].astype(dtype) for n in group], axis=0) for group in GROUPS]


def _unpack_full(gathered, groups):
    out = {}
    for group, arr in zip(groups, gathered):
        offs, _ = _group_rows(group)
        for n in group:
            rows, cols = _shard_shape(n)
            (k, m), ax = BIG_INFO[n]
            slab = arr[:, offs[n]:offs[n] + rows]
            out[n] = slab.reshape(k, m) if ax == 0 else jnp.transpose(slab, (1, 0, 2)).reshape(k, m)
    return out


def _pack_grads(gfull):
    out = []
    for group in GROUPS:
        parts = []
        for n in group:
            rows, cols = _shard_shape(n)
            ax = BIG_INFO[n][1]
            slab = (gfull[n].reshape(N_CHIPS, rows, cols) if ax == 0
                    else jnp.transpose(gfull[n].reshape(rows, N_CHIPS, cols), (1, 0, 2)))
            parts.append(slab)
        out.append(jnp.concatenate(parts, axis=1))
    return out


def _after(values, mark):
    values, _ = lax.optimization_barrier((values, mark))
    return values


def _reduce_scatter_begin(gsends, place, tag, ids):
    sibs = _sibling_halves(gsends, "rs_sibling_halves_" + tag, ids[0])
    tsends = [_chip_sums(g, s, place) for g, s in zip(gsends, sibs)]
    return tsends, _exchange_chip_sums(tsends, "rs_exchange_" + tag, ids[1])


def _reduce_scatter_finish(begun, place, tag, ids, hold):
    tsends, recvs = begun
    recvs = _after(recvs, hold)
    return _join_halves([_final_sum(t, r, place) for t, r in zip(tsends, recvs)], "rs_join_halves_" + tag, ids[2])


SMALL_SHAPES = {"rel_table": (NUM_BUCKETS, 2 * N_HEADS), "norm_mix_g": (DEPTH, D_MODEL), "qnorm_a_g": (DEPTH, HEAD_DIM),
                "knorm_a_g": (DEPTH, HEAD_DIM), "qnorm_b_g": (DEPTH, HEAD_DIM), "knorm_b_g": (DEPTH, HEAD_DIM),
                "sink_b": (DEPTH, N_HEADS), "norm_ffn_g": (DEPTH, D_MODEL), "norm_ple_g": (DEPTH, D_MODEL)}


def _pack_small(vals, last=None):
    flat = jnp.concatenate([vals[n].astype(F32).reshape(-1) for n in SMALL])
    tail = jnp.zeros((SMALL_ROWS * LANES - flat.shape[0],), F32)
    if last is not None:
        tail = tail.at[-1].set(last)
    return jnp.concatenate([flat, tail]).reshape(SMALL_ROWS, LANES)


def _unpack_small(packed):
    flat, out, off = packed.reshape(-1), {}, 0
    for n in SMALL:
        size = math.prod(SMALL_SHAPES[n])
        out[n] = flat[off:off + size].reshape(SMALL_SHAPES[n])
        off += size
    return out


def _adamw(w, gs, g_row, m, v, name):
    c1 = 1.0 - ADAM_B1 ** ADAM_STEP
    c2 = 1.0 - ADAM_B2 ** ADAM_STEP
    total, width = w.shape
    n_layers = len(gs)
    per = total // n_layers
    tm = max(t for t in range(8, 513, 8) if per % t == 0 and g_row % t == 0)
    nblk = per // tm

    def body(*refs):
        w_ref, g_refs = refs[0], refs[1:1 + n_layers]
        m_ref, v_ref, og, od, om, ov = refs[1 + n_layers:]
        layer = pl.program_id(0) // nblk
        g = g_refs[0][...]
        for l in range(1, n_layers):
            g = jnp.where(layer == l, g_refs[l][...], g)
        m_new = ADAM_B1 * m_ref[...] + (1.0 - ADAM_B1) * g
        v_new = ADAM_B2 * v_ref[...] + (1.0 - ADAM_B2) * (g * g)
        og[...] = g
        od[...] = -ADAM_LR * ((m_new / c1) / (jnp.sqrt(v_new / c2) + ADAM_EPS) + ADAM_WD * w_ref[...])
        om[...] = m_new
        ov[...] = v_new

    row = pl.BlockSpec((tm, width), lambda i: (i, 0))
    g_specs = [pl.BlockSpec((tm, width), lambda i, l=l: (g_row // tm + jnp.clip(i - l * nblk, 0, nblk - 1), 0))
               for l in range(n_layers)]
    return pl.pallas_call(
        body, out_shape=[jax.ShapeDtypeStruct((total, width), F32)] * 4, grid=(total // tm,),
        in_specs=[row] + g_specs + [row, row], out_specs=[row] * 4, name=name,
        compiler_params=_params("parallel"))(w, *gs, m, v)


def kernel(x, p, rel_table, norm_mix_g, w_in, qnorm_a_g, knorm_a_g, qnorm_b_g, knorm_b_g, sink_b, w_branch_a, w_branch_b, w_out, norm_ffn_g, w_ffn_gate, w_ffn_up, w_ffn_down, norm_ple_g, w_ple_gate, w_ple_proj, loss_target, m_rel_table, m_norm_mix_g, m_w_in, m_qnorm_a_g, m_knorm_a_g, m_qnorm_b_g, m_knorm_b_g, m_sink_b, m_w_branch_a, m_w_branch_b, m_w_out, m_norm_ffn_g, m_w_ffn_gate, m_w_ffn_up, m_w_ffn_down, m_norm_ple_g, m_w_ple_gate, m_w_ple_proj, v_rel_table, v_norm_mix_g, v_w_in, v_qnorm_a_g, v_knorm_a_g, v_qnorm_b_g, v_knorm_b_g, v_sink_b, v_w_branch_a, v_w_branch_b, v_w_out, v_norm_ffn_g, v_w_ffn_gate, v_w_ffn_up, v_w_ffn_down, v_norm_ple_g, v_w_ple_gate, v_w_ple_proj):
    given = dict(locals())

    def held(name, a):
        return jnp.swapaxes(a, 1, 2) if name in TRANSPOSED else a

    weights = {n: held(n, given[n]) for n in WEIGHTS}
    moments_m = {n: held(n, given["m_" + n]) for n in WEIGHTS}
    moments_v = {n: held(n, given["v_" + n]) for n in WEIGHTS}
    xi, yi, ci = _place()
    place = jnp.stack([2 * xi + yi, ci]).astype(jnp.int32)

    shards = [_pack_groups(weights, l, BF16) for l in range(DEPTH)]
    w_in0 = _allgather(shards[0][:1], "allgather_w_in_layer0", sequencer=9)
    rest0 = _allgather(_after(shards[0][1:], w_in0), "allgather_rest_layer0", sequencer=1)
    gathered = [w_in0 + rest0, None]
    small = {n: weights[n] for n in SMALL}

    def w_in_of(l, mark):
        return _unpack_full(_after(gathered[l][:1], (shards[1], mark) if l == 0 else mark), GROUPS[:1])["w_in"]

    def rest_of(l, mark):
        if l == 0:
            gathered[1] = _allgather(_after(shards[1], mark), "allgather_layer1", sequencer=2)
        return _unpack_full(_after(gathered[l][1:], mark), GROUPS[1:])

    loss, dx, gbig, gsmall, marks = _local_step(x[0], p[:, 0], loss_target[0], w_in_of, rest_of, small)

    gsends = [_pack_grads(gbig[l]) for l in range(DEPTH)]
    stages = {"layer1": (gsends[1], (3, 4, 5)), "rest_layer0": (gsends[0][1:], (6, 7, 8)),
              "w_in_layer0": (gsends[0][:1], (10, 11, 12))}
    begun = {tag: _reduce_scatter_begin(g, place, tag, ids) for tag, (g, ids) in stages.items()}

    def finish(tag, hold):
        return _reduce_scatter_finish(begun[tag], place, tag, stages[tag][1], hold)

    red1 = finish("layer1", marks[0]["attn_bwd_done"])
    rest0 = finish("rest_layer0", marks[0]["attn_bwd_done"])

    grads, delta, new_m, new_v = {}, {}, {}, {}

    def update(group, reduced):
        offs, _ = _group_rows(group)
        for n in group:
            shape = weights[n].shape
            two_d = lambda a: a.reshape(shape[0] * shape[1], shape[2])
            outs = _adamw(two_d(weights[n]), reduced, offs[n], two_d(moments_m[n]), two_d(moments_v[n]), "adamw_" + n)
            grads[n], delta[n], new_m[n], new_v[n] = (held(n, o.reshape(shape)) for o in outs)

    for gi in (1, 2):
        update(GROUPS[gi], _after([rest0[gi - 1], red1[gi]], begun["w_in_layer0"][0]))
    small_grads = _allreduce_small(_pack_small(gsmall, last=loss))
    g_, d_, m_, v_ = _adamw(_pack_small(weights), [small_grads], 0, _pack_small(moments_m), _pack_small(moments_v),
                            "adamw_small")
    grads.update(_unpack_small(g_))
    delta.update(_unpack_small(d_))
    new_m.update(_unpack_small(m_))
    new_v.update(_unpack_small(v_))
    others_done = [dx, d_] + [delta[n] for gi in (1, 2) for n in GROUPS[gi]]
    update(GROUPS[0], [finish("w_in_layer0", others_done)[0], red1[0]])

    return (small_grads[-1, -1], dx[None], *[grads[n] for n in WEIGHTS], *[delta[n] for n in WEIGHTS],
            *[new_m[n] for n in WEIGHTS], *[new_v[n] for n in WEIGHTS])
```

```python
import functools
import math

import jax
import jax.numpy as jnp
from jax import lax
from jax.experimental import pallas as pl
from jax.experimental.pallas import tpu as pltpu
from jax.experimental.pallas import tpu_sc as plsc

F32 = jnp.float32
BF16 = jnp.bfloat16
MESH_ID = pl.DeviceIdType.MESH

D_MODEL = 1024
DEPTH = 2
HEAD_DIM = 64
N_HEADS = 8
WIDTH = N_HEADS * HEAD_DIM
N_PAIRS = 4
ITEMS = 16
N_KV_B = 2
PLE_DIM = 256
D_FF = 2816
D_IN = 4352
OFF_QA, OFF_KA, OFF_VA, OFF_QB, OFF_KB, OFF_VB, OFF_GA, OFF_GB = 0, 512, 1024, 1536, 2048, 2176, 2304, 3328
DILATED = ((64, 1), (64, 4), (64, 16))
BLK_B = 128
NUM_BUCKETS = 32
MAX_DISTANCE = 1024
RMS_EPS = 1e-6
NEG_INF = -1e30
LANES = 128
ROW_TILE = 256
VMEM_LIMIT = 48 * 1024 * 1024

ADAM_LR, ADAM_B1, ADAM_B2, ADAM_EPS, ADAM_WD, ADAM_STEP = 0.001, 0.9, 0.999, 1e-08, 0.01, 10

TRANSPOSED = ("w_in", "w_ffn_gate", "w_ffn_up")
BIG = (
    ("w_in", (D_IN, D_MODEL), 0),
    ("w_branch_a", (WIDTH, D_MODEL), 1),
    ("w_branch_b", (WIDTH, D_MODEL), 1),
    ("w_out", (D_MODEL, D_MODEL), 0),
    ("w_ffn_gate", (D_FF, D_MODEL), 0),
    ("w_ffn_up", (D_FF, D_MODEL), 0),
    ("w_ffn_down", (D_FF, D_MODEL), 0),
    ("w_ple_gate", (D_MODEL, D_MODEL), 0),
    ("w_ple_proj", (PLE_DIM, D_MODEL), 1),
)
SMALL = ("rel_table", "norm_mix_g", "qnorm_a_g", "knorm_a_g", "qnorm_b_g", "knorm_b_g", "sink_b",
         "norm_ffn_g", "norm_ple_g")
WEIGHTS = ("rel_table", "norm_mix_g", "w_in", "qnorm_a_g", "knorm_a_g", "qnorm_b_g", "knorm_b_g", "sink_b",
           "w_branch_a", "w_branch_b", "w_out", "norm_ffn_g", "w_ffn_gate", "w_ffn_up", "w_ffn_down",
           "norm_ple_g", "w_ple_gate", "w_ple_proj")
N_CHIPS = 4
SMALL_ROWS = 64


def _params(*sem):
    return pltpu.CompilerParams(dimension_semantics=sem, vmem_limit_bytes=VMEM_LIMIT)


MM_VMEM_BUDGET = 40 * 1024 * 1024
STEP_OVERHEAD_S = 0.4e-6
TILE_DMA_BYTES_PER_S = 1.5e12


def _mm_dims(a, b, mode):
    if mode == "nn":
        return a.shape[0], b.shape[1], a.shape[1]
    if mode == "nt":
        return a.shape[0], b.shape[0], a.shape[1]
    return a.shape[1], b.shape[1], a.shape[0]


def _mm_tiles(m, n, pairs, tile_bytes, col_offsets):
    best = None
    for tm in (t for t in range(LANES, m + 1, LANES) if m % t == 0):
        for tn in (t for t in range(LANES, n + 1, LANES) if n % t == 0 and all(o % t == 0 for o in col_offsets)):
            io = sum(tm * k * ab + tn * k * bb for k, ab, bb in pairs) + tm * tn * sum(tile_bytes)
            casts = sum((tm * k * 2 if ab == 4 else 0) + (tn * k * 2 if bb == 4 else 0) for k, ab, bb in pairs)
            if 2 * io + len(pairs) * tm * tn * 4 + casts > MM_VMEM_BUDGET:
                continue
            cost = (m // tm) * (n // tn) * STEP_OVERHEAD_S + io / TILE_DMA_BYTES_PER_S
            if best is None or (cost, -tm) < best[0]:
                best = ((cost, -tm), tm, tn)
    return best[1], best[2]


def _mm_fused(pairs, extras, epilogue, out_dtypes, name):
    m, n, _ = _mm_dims(*pairs[0])
    assert all(_mm_dims(*p)[:2] == (m, n) for p in pairs)
    tm, tn = _mm_tiles(
        m, n, [(_mm_dims(a, b, mode)[2], a.dtype.itemsize, b.dtype.itemsize) for a, b, mode in pairs],
        [e.dtype.itemsize for e, _ in extras] + [jnp.dtype(d).itemsize for d in out_dtypes], [off for _, off in extras])
    dims = {"nn": (((1,), (0,)), ((), ())), "nt": (((1,), (1,)), ((), ())), "tn": (((0,), (0,)), ((), ()))}
    in_specs, args = [], []
    for a, b, mode in pairs:
        k = _mm_dims(a, b, mode)[2]
        in_specs.append(pl.BlockSpec((k, tm), lambda i, j: (0, i)) if mode == "tn" else pl.BlockSpec((tm, k), lambda i, j: (i, 0)))
        in_specs.append(pl.BlockSpec((tn, k), lambda i, j: (j, 0)) if mode == "nt" else pl.BlockSpec((k, tn), lambda i, j: (0, j)))
        args += [a, b]
    for e, off in extras:
        in_specs.append(pl.BlockSpec((tm, tn), lambda i, j, o=off // tn: (i, o + j)))
        args.append(e)
    n_pairs, n_in = len(pairs), 2 * len(pairs) + len(extras)

    def body(*refs):
        products = [lax.dot_general(refs[2 * p][...].astype(BF16), refs[2 * p + 1][...].astype(BF16), dims[pairs[p][2]],
                                    preferred_element_type=F32) for p in range(n_pairs)]
        outs = epilogue(products, [r[...] for r in refs[2 * n_pairs:n_in]])
        for r, o in zip(refs[n_in:], outs):
            r[...] = o.astype(r.dtype)

    tile = pl.BlockSpec((tm, tn), lambda i, j: (i, j))
    return pl.pallas_call(
        body, out_shape=[jax.ShapeDtypeStruct((m, n), d) for d in out_dtypes], grid=(m // tm, n // tn),
        in_specs=in_specs, out_specs=[tile] * len(out_dtypes), name=name,
        compiler_params=_params("parallel", "parallel"))(*args)


def _mm(a, b, mode, out_dtype, name, res=None):
    if res is None:
        return _mm_fused([(a, b, mode)], [], lambda products, extra: products, [out_dtype], name)[0]
    return _mm_fused([(a, b, mode)], [(res, 0)], lambda products, extra: [products[0] + extra[0]], [out_dtype], name)[0]


def _ew(fn, ins, out_dtypes, name):
    rows, width = ins[0].shape
    n_in = len(ins)

    def body(*refs):
        outs = fn(*[r[...] for r in refs[:n_in]])
        for r, o in zip(refs[n_in:], outs):
            r[...] = o.astype(r.dtype)

    row = pl.BlockSpec((ROW_TILE, width), lambda i: (i, 0))
    return pl.pallas_call(
        body, out_shape=[jax.ShapeDtypeStruct((rows, width), dt) for dt in out_dtypes], grid=(rows // ROW_TILE,),
        in_specs=[row] * n_in, out_specs=[row] * len(out_dtypes), name=name, compiler_params=_params("parallel"))(*ins)


def _sigmoid(x):
    return 1.0 / (1.0 + jnp.exp(-x))


def _seg_sum(v):
    outs = []
    for k in range(v.shape[1] // LANES):
        vp = v[:, k * LANES:(k + 1) * LANES]
        left = lax.broadcasted_iota(jnp.int32, vp.shape, 1) < HEAD_DIM
        sl = jnp.sum(jnp.where(left, vp, 0.0), axis=-1, keepdims=True)
        sr = jnp.sum(jnp.where(left, 0.0, vp), axis=-1, keepdims=True)
        outs.append(jnp.where(left, sl, sr))
    return outs[0] if len(outs) == 1 else jnp.concatenate(outs, axis=1)


def _seg_rstd(x):
    return lax.rsqrt(_seg_sum(x * x) * (1.0 / HEAD_DIM) + RMS_EPS)


def _rms_fwd(x, g, name):
    rows, d = x.shape
    tm = ROW_TILE

    def body(x_ref, g_ref, h_ref):
        xv = x_ref[...]
        r = lax.rsqrt(jnp.mean(xv * xv, axis=-1, keepdims=True) + RMS_EPS)
        h_ref[...] = ((xv * r) * g_ref[...]).astype(BF16)

    return pl.pallas_call(
        body, out_shape=jax.ShapeDtypeStruct((rows, d), BF16), grid=(rows // tm,),
        in_specs=[pl.BlockSpec((tm, d), lambda i: (i, 0)), pl.BlockSpec((1, d), lambda i: (0, 0))],
        out_specs=pl.BlockSpec((tm, d), lambda i: (i, 0)), name=name,
        compiler_params=_params("parallel"))(x, g)


def _rms_bwd(x, g, dh, dres, name):
    rows, d = x.shape
    tm = ROW_TILE

    def body(x_ref, g_ref, dh_ref, dres_ref, dx_ref, dxb_ref, dg_ref):
        xv = x_ref[...]
        r = lax.rsqrt(jnp.mean(xv * xv, axis=-1, keepdims=True) + RMS_EPS)
        xh = xv * r
        dhv = dh_ref[...]
        dxh = dhv * g_ref[...]
        dxv = dres_ref[...] + r * (dxh - xh * jnp.mean(dxh * xh, axis=-1, keepdims=True))
        dx_ref[...] = dxv
        dxb_ref[...] = dxv.astype(BF16)
        part = jnp.sum(dhv * xh, axis=0, keepdims=True)

        @pl.when(pl.program_id(0) == 0)
        def _():
            dg_ref[...] = part

        @pl.when(pl.program_id(0) > 0)
        def _():
            dg_ref[...] += part

    row = pl.BlockSpec((tm, d), lambda i: (i, 0))
    vec = pl.BlockSpec((1, d), lambda i: (0, 0))
    return pl.pallas_call(
        body, out_shape=[jax.ShapeDtypeStruct((rows, d), F32), jax.ShapeDtypeStruct((rows, d), BF16),
                         jax.ShapeDtypeStruct((1, d), F32)],
        grid=(rows // tm,), in_specs=[row, vec, row, row], out_specs=[row, row, vec],
        name=name, compiler_params=_params("arbitrary"))(x, g, dh, dres)


def _loss_grad(y, t):
    rows, d = y.shape
    tm = ROW_TILE

    def body(y_ref, t_ref, dy_ref, l_ref):
        e = y_ref[...] - t_ref[...]
        dy_ref[...] = e * (1.0 / d)
        part = jnp.zeros((1, LANES), F32) + jnp.sum(e * e) * (0.5 / d)

        @pl.when(pl.program_id(0) == 0)
        def _():
            l_ref[...] = part

        @pl.when(pl.program_id(0) > 0)
        def _():
            l_ref[...] += part

    row = pl.BlockSpec((tm, d), lambda i: (i, 0))
    return pl.pallas_call(
        body, out_shape=[jax.ShapeDtypeStruct((rows, d), F32), jax.ShapeDtypeStruct((1, LANES), F32)],
        grid=(rows // tm,), in_specs=[row, row], out_specs=[row, pl.BlockSpec((1, LANES), lambda i: (0, 0))],
        name="loss_grad", compiler_params=_params("arbitrary"))(y, t)


def _swap_halves(v):
    return pltpu.roll(v, HEAD_DIM, axis=1)


def _expand_kv(kv):
    left = lax.broadcasted_iota(jnp.int32, kv.shape, 1) < HEAD_DIM
    sw = _swap_halves(kv)
    h0 = jnp.where(left, kv, sw)
    h1 = jnp.where(left, sw, kv)
    return jnp.concatenate([h0, h0, h1, h1], axis=1)


def _reduce_kv(dkv):
    left = lax.broadcasted_iota(jnp.int32, (dkv.shape[0], LANES), 1) < HEAD_DIM
    t = dkv[:, 0:LANES] + dkv[:, LANES:2 * LANES]
    u = dkv[:, 2 * LANES:3 * LANES] + dkv[:, 3 * LANES:4 * LANES]
    t = t + _swap_halves(t)
    u = u + _swap_halves(u)
    return jnp.where(left, t, u)


def _qknorm_fwd(proj, gqa, gka, gqb, gkb):
    rows = proj.shape[0]
    tm = ROW_TILE

    def body(qa_ref, ka_ref, qb_ref, kb_ref, vb_ref, gqa_ref, gka_ref, gqb_ref, gkb_ref, oqa, oka, oqb, okb, ovb):
        for src, g_ref, dst in ((qa_ref, gqa_ref, oqa), (ka_ref, gka_ref, oka), (qb_ref, gqb_ref, oqb)):
            xv = src[...]
            dst[...] = (xv * _seg_rstd(xv)) * g_ref[...]
        kv = kb_ref[...]
        okb[...] = _expand_kv((kv * _seg_rstd(kv)) * gkb_ref[...])
        ovb[...] = _expand_kv(vb_ref[...])

    def win(width, off):
        return pl.BlockSpec((tm, width), lambda i: (i, off // width))

    vec = lambda w: pl.BlockSpec((1, w), lambda i: (0, 0))
    out = pl.BlockSpec((tm, WIDTH), lambda i: (i, 0))
    return pl.pallas_call(
        body, out_shape=[jax.ShapeDtypeStruct((rows, WIDTH), F32)] * 5, grid=(rows // tm,),
        in_specs=[win(WIDTH, OFF_QA), win(WIDTH, OFF_KA), win(WIDTH, OFF_QB), win(LANES, OFF_KB), win(LANES, OFF_VB),
                  vec(WIDTH), vec(WIDTH), vec(WIDTH), vec(LANES)],
        out_specs=[out] * 5, name="qknorm_fwd", compiler_params=_params("parallel"))(
            proj, proj, proj, proj, proj, gqa, gka, gqb, gkb)


def _norm_bwd(xv, g, dy):
    r = _seg_rstd(xv)
    xh = xv * r
    dxh = dy * g
    dx = r * (dxh - xh * (_seg_sum(dxh * xh) * (1.0 / HEAD_DIM)))
    return dx, jnp.sum(dy * xh, axis=0, keepdims=True)


def _qknorm_bwd(proj, gqa, gka, gqb, gkb, dqa, dka, dva, dqb, dkb, dvb, dga, dgb):
    rows = proj.shape[0]
    tm = ROW_TILE
    n_a = len(dqa)

    def body(*refs):
        qa_ref, ka_ref, qb_ref, kb_ref, gqa_ref, gka_ref, gqb_ref, gkb_ref = refs[:8]
        pos = 8
        dqa_refs, dka_refs, dva_refs = refs[pos:pos + n_a], refs[pos + n_a:pos + 2 * n_a], refs[pos + 2 * n_a:pos + 3 * n_a]
        pos += 3 * n_a
        dqb_ref, dkb_ref, dvb_ref, dga_ref, dgb_ref = refs[pos:pos + 5]
        dproj_ref, ogqa, ogka, ogqb, ogkb = refs[pos + 5:]

        def total(rs):
            acc = rs[0][...]
            for r in rs[1:]:
                acc = acc + r[...]
            return acc

        dx_qa, p_qa = _norm_bwd(qa_ref[...], gqa_ref[...], total(dqa_refs))
        dx_ka, p_ka = _norm_bwd(ka_ref[...], gka_ref[...], total(dka_refs))
        dx_qb, p_qb = _norm_bwd(qb_ref[...], gqb_ref[...], dqb_ref[...])
        dx_kb, p_kb = _norm_bwd(kb_ref[...], gkb_ref[...], _reduce_kv(dkb_ref[...]))
        dproj_ref[:, OFF_QA:OFF_QA + WIDTH] = dx_qa.astype(BF16)
        dproj_ref[:, OFF_KA:OFF_KA + WIDTH] = dx_ka.astype(BF16)
        dproj_ref[:, OFF_VA:OFF_VA + WIDTH] = total(dva_refs).astype(BF16)
        dproj_ref[:, OFF_QB:OFF_QB + WIDTH] = dx_qb.astype(BF16)
        dproj_ref[:, OFF_KB:OFF_KB + LANES] = dx_kb.astype(BF16)
        dproj_ref[:, OFF_VB:OFF_VB + LANES] = _reduce_kv(dvb_ref[...]).astype(BF16)
        dproj_ref[:, OFF_GA:OFF_GB] = dga_ref[...]
        dproj_ref[:, OFF_GB:D_IN] = dgb_ref[...]
        first = pl.program_id(0) == 0
        for o_ref, part in ((ogqa, p_qa), (ogka, p_ka), (ogqb, p_qb), (ogkb, p_kb)):
            @pl.when(first)
            def _(o_ref=o_ref, part=part):
                o_ref[...] = part

            @pl.when(jnp.logical_not(first))
            def _(o_ref=o_ref, part=part):
                o_ref[...] += part

    def win(width, off):
        return pl.BlockSpec((tm, width), lambda i: (i, off // width))

    vec = lambda w: pl.BlockSpec((1, w), lambda i: (0, 0))
    row = lambda w: pl.BlockSpec((tm, w), lambda i: (i, 0))
    in_specs = [win(WIDTH, OFF_QA), win(WIDTH, OFF_KA), win(WIDTH, OFF_QB), win(LANES, OFF_KB),
                vec(WIDTH), vec(WIDTH), vec(WIDTH), vec(LANES)]
    in_specs += [row(WIDTH)] * (3 * n_a + 3) + [row(D_MODEL)] * 2
    return pl.pallas_call(
        body,
        out_shape=[jax.ShapeDtypeStruct((rows, D_IN), BF16), jax.ShapeDtypeStruct((1, WIDTH), F32),
                   jax.ShapeDtypeStruct((1, WIDTH), F32), jax.ShapeDtypeStruct((1, WIDTH), F32),
                   jax.ShapeDtypeStruct((1, LANES), F32)],
        grid=(rows // tm,), in_specs=in_specs,
        out_specs=[row(D_IN), vec(WIDTH), vec(WIDTH), vec(WIDTH), vec(LANES)],
        name="qknorm_bwd", compiler_params=_params("arbitrary"))(
            proj, proj, proj, proj, gqa, gka, gqb, gkb, *dqa, *dka, *dva, dqb, dkb, dvb, dga, dgb)


def _t5_bucket(rel):
    half_b = NUM_BUCKETS // 2
    max_exact = half_b // 2
    sign = jnp.where(rel > 0, half_b, 0)
    n = jnp.abs(rel)
    nf = jnp.maximum(n, 1).astype(F32)
    large = max_exact + (jnp.log(nf / max_exact) / math.log(MAX_DISTANCE / max_exact)
                         * (half_b - max_exact)).astype(jnp.int32)
    large = jnp.minimum(large, half_b - 1)
    return sign + jnp.where(n < max_exact, n, large)


def _band_buckets(blk, dilation):
    i = jnp.arange(blk, dtype=jnp.int32)[:, None]
    j = jnp.arange(3 * blk, dtype=jnp.int32)[None, :]
    rel = j - blk - i
    return jnp.where(jnp.abs(rel) <= blk, _t5_bucket(rel * dilation), -1)


def _bias_tiles(table, buckets, head_off, name):
    blk = buckets.shape[0]

    def body(tab_ref, bk_ref, o_ref):
        h = pl.program_id(0) + head_off
        bk = bk_ref[...]
        acc = jnp.full(bk.shape, NEG_INF, F32)
        for b in range(NUM_BUCKETS):
            acc = jnp.where(bk == b, tab_ref[b, h], acc)
        o_ref[0] = acc

    return pl.pallas_call(
        body, out_shape=jax.ShapeDtypeStruct((N_HEADS, blk, 3 * blk), F32), grid=(N_HEADS,),
        in_specs=[pl.BlockSpec(memory_space=pltpu.SMEM), pl.BlockSpec((blk, 3 * blk), lambda h: (0, 0))],
        out_specs=pl.BlockSpec((1, blk, 3 * blk), lambda h: (h, 0, 0)),
        name=name, compiler_params=_params("parallel"))(table, buckets)


def _table_grad(dbias, buckets, name):
    blk = buckets.shape[0]

    def body(db_ref, bk_ref, o_ref):
        bk = bk_ref[...]
        dbv = db_ref[0]
        lane = lax.broadcasted_iota(jnp.int32, (1, LANES), 1)
        acc = jnp.zeros((1, LANES), F32)
        for b in range(NUM_BUCKETS):
            acc = jnp.where(lane == b, jnp.sum(jnp.where(bk == b, dbv, 0.0)), acc)
        o_ref[0] = acc

    out = pl.pallas_call(
        body, out_shape=jax.ShapeDtypeStruct((N_HEADS, 1, LANES), F32), grid=(N_HEADS,),
        in_specs=[pl.BlockSpec((1, blk, 3 * blk), lambda h: (h, 0, 0)), pl.BlockSpec((blk, 3 * blk), lambda h: (0, 0))],
        out_specs=pl.BlockSpec((1, 1, LANES), lambda h: (h, 0, 0)),
        name=name, compiler_params=_params("parallel"))(dbias, buckets)
    return out[:, 0, :NUM_BUCKETS]


def _dot_nt(a, b):
    return lax.dot_general(a, b, (((1,), (1,)), ((), ())), preferred_element_type=F32)


def _dot_tn(a, b):
    return lax.dot_general(a, b, (((0,), (0,)), ((), ())), preferred_element_type=F32)


def _stack_pair(x2, left):
    return jnp.concatenate([jnp.where(left, x2, 0.0), jnp.where(left, 0.0, x2)], axis=0).astype(BF16)


def _attn_geometry(blk, d):
    subs = max(1, ITEMS // d)
    return subs, ITEMS // subs, blk * d


def _item_of(j, r0, per_group):
    return j // per_group, r0 + j % per_group


def _span_rows(ref, first, r, blk, d):
    if d == 1:
        return ref[first:first + blk, :]
    return ref[pl.ds(first + r, blk, stride=d), :]


def _set_span_rows(ref, first, r, blk, d, val, add=False):
    idx = slice(first, first + blk) if d == 1 else pl.ds(first + r, blk, stride=d)
    ref[idx, :] = ref[idx, :] + val if add else val


def _for_groups(group, groups, per_group):
    if groups == 1:
        group(0)
    else:
        def step(g, carry):
            group(g * per_group)
            return carry

        lax.fori_loop(0, groups, step, 0)


def _key_rows(p_ref, c_ref, n_ref, s, r, subs, halo, blk, d):
    parts = []
    for span in (s - 1, s, s + 1):
        if span < 0:
            parts.append(_span_rows(p_ref, 0, r, blk, d))
        elif span == subs:
            parts.append(_span_rows(n_ref, 0, r, blk, d))
        else:
            parts.append(_span_rows(c_ref, span * halo, r, blk, d))
    return jnp.concatenate(parts, axis=0)


def _item_penalty(t, nct, s, subs, blk):
    first_ok = True if s > 0 else t > 0
    last_ok = True if s < subs - 1 else t < nct - 1
    col = lax.broadcasted_iota(jnp.int32, (1, 3 * blk), 1)
    ok = jnp.logical_and(jnp.logical_or(col >= blk, first_ok), jnp.logical_or(col < 2 * blk, last_ok))
    return jnp.where(ok, 0.0, NEG_INF).astype(F32)


def _attn_specs(seq, blk, d, step_of, col=0):
    subs, _, halo = _attn_geometry(blk, d)
    last, first = seq // halo - 1, col // LANES
    cur = pl.BlockSpec((subs * halo, LANES), lambda hp, t: (step_of(t), first + hp))
    prev = pl.BlockSpec((halo, LANES), lambda hp, t: (jnp.clip(step_of(t) * subs - 1, 0, last), first + hp))
    nxt = pl.BlockSpec((halo, LANES), lambda hp, t: (jnp.minimum((step_of(t) + 1) * subs, last), first + hp))
    return cur, prev, nxt


def _attn_fwd(q, k, v, bias, sink, blk, d, name, v_col=0):
    seq = q.shape[0]
    subs, per_group, halo = _attn_geometry(blk, d)
    chunk, groups = subs * halo, d // per_group
    nct = seq // chunk
    has_sink = sink is not None
    scale = HEAD_DIM ** -0.5

    def body(*refs):
        q_ref, kp, kc, kn, vp, vc, vn, b_ref = refs[:8]
        s_ref = refs[8] if has_sink else None
        o_ref, l_ref = refs[-2], refs[-1]
        t = pl.program_id(1)
        left = lax.broadcasted_iota(jnp.int32, (1, LANES), 1) < HEAD_DIM
        bias2 = b_ref[...]

        def group(r0):
            scores, vcats = [], []
            for j in range(ITEMS):
                s, r = _item_of(j, r0, per_group)
                qs = _stack_pair(_span_rows(q_ref, s * halo, r, blk, d) * scale, left)
                kcat = _key_rows(kp, kc, kn, s, r, subs, halo, blk, d).astype(BF16)
                scores.append(_dot_nt(qs, kcat) + bias2 + _item_penalty(t, nct, s, subs, blk))
                vcats.append(_key_rows(vp, vc, vn, s, r, subs, halo, blk, d).astype(BF16))
            ms = [jnp.max(s, axis=-1, keepdims=True) for s in scores]
            if has_sink:
                sk = s_ref[...]
                ms = [jnp.maximum(m, sk) for m in ms]
            ps = [jnp.exp(s - m) for s, m in zip(scores, ms)]
            dens = [jnp.sum(p, axis=-1, keepdims=True) for p in ps]
            if has_sink:
                dens = [den + jnp.exp(sk - m) for den, m in zip(dens, ms)]
            pns = [(p * (1.0 / den)).astype(BF16) for p, den in zip(ps, dens)]
            lses = [m + jnp.log(den) for m, den in zip(ms, dens)]
            for j in range(ITEMS):
                s, r = _item_of(j, r0, per_group)
                o2 = jnp.dot(pns[j], vcats[j], preferred_element_type=F32)
                _set_span_rows(o_ref, s * halo, r, blk, d, jnp.where(left, o2[:blk], o2[blk:]))
                _set_span_rows(l_ref, s * halo, r, blk, d, jnp.where(left, lses[j][:blk], lses[j][blk:]))

        _for_groups(group, groups, per_group)

    cur, prev, nxt = _attn_specs(seq, blk, d, lambda t: t)
    v_cur, v_prev, v_nxt = _attn_specs(seq, blk, d, lambda t: t, v_col)
    in_specs = [cur, prev, cur, nxt, v_prev, v_cur, v_nxt, pl.BlockSpec((2 * blk, 3 * blk), lambda hp, t: (hp, 0))]
    args = [q, k, k, k, v, v, v, bias]
    if has_sink:
        in_specs.append(pl.BlockSpec((2 * blk, 1), lambda hp, t: (hp, 0)))
        args.append(sink)
    return pl.pallas_call(
        body, out_shape=[jax.ShapeDtypeStruct((seq, WIDTH), F32)] * 2, grid=(N_PAIRS, nct),
        in_specs=in_specs, out_specs=[cur, cur], name=name,
        compiler_params=_params("parallel", "parallel"))(*args)


def _attn_bwd(q, k, v, do, lse, delta, bias, sink, blk, d, name, v_col=0):
    seq = q.shape[0]
    subs, per_group, halo = _attn_geometry(blk, d)
    chunk, groups = subs * halo, d // per_group
    nct = seq // chunk
    has_sink = sink is not None
    n_in = 12 if has_sink else 11
    scale = HEAD_DIM ** -0.5

    def body(*refs):
        q_ref, kp, kc, kn, vp, vc, vn, do_ref, l_ref, d_ref, b_ref = refs[:11]
        s_ref = refs[11] if has_sink else None
        dq_ref, dk_ref, dv_ref, db_ref = refs[n_in:n_in + 4]
        ds_ref = refs[n_in + 4] if has_sink else None
        wk, wv = refs[-2], refs[-1]
        t = pl.program_id(1)

        @pl.when(t == 0)
        def _():
            wk[...] = jnp.zeros_like(wk)
            wv[...] = jnp.zeros_like(wv)
            db_ref[...] = jnp.zeros_like(db_ref)
            if has_sink:
                ds_ref[...] = jnp.zeros_like(ds_ref)

        @pl.when(t > 0)
        def _():
            for w in (wk, wv):
                keep = w[chunk:2 * chunk + halo]
                w[0:chunk + halo] = keep
                w[chunk + halo:2 * chunk + halo] = jnp.zeros((chunk, LANES), F32)

        @pl.when(t < nct)
        def _():
            lane = lax.broadcasted_iota(jnp.int32, (1, LANES), 1)
            left = lane < HEAD_DIM
            bias2 = b_ref[...]

            def group(r0):
                qss, doss, kcats, scores, dps, lcols, dcols = [], [], [], [], [], [], []
                for j in range(ITEMS):
                    s, r = _item_of(j, r0, per_group)
                    qs = _stack_pair(_span_rows(q_ref, s * halo, r, blk, d) * scale, left)
                    dos = _stack_pair(_span_rows(do_ref, s * halo, r, blk, d), left)
                    kcat = _key_rows(kp, kc, kn, s, r, subs, halo, blk, d).astype(BF16)
                    vcat = _key_rows(vp, vc, vn, s, r, subs, halo, blk, d).astype(BF16)
                    l2, d2 = _span_rows(l_ref, s * halo, r, blk, d), _span_rows(d_ref, s * halo, r, blk, d)
                    lcols.append(jnp.concatenate([jnp.max(jnp.where(left, l2, NEG_INF), axis=-1, keepdims=True),
                                                  jnp.max(jnp.where(left, NEG_INF, l2), axis=-1, keepdims=True)], axis=0))
                    dcols.append(jnp.concatenate([jnp.sum(jnp.where(lane == 0, d2, 0.0), axis=-1, keepdims=True),
                                                  jnp.sum(jnp.where(lane == HEAD_DIM, d2, 0.0), axis=-1, keepdims=True)],
                                                 axis=0))
                    scores.append(_dot_nt(qs, kcat) + bias2 + _item_penalty(t, nct, s, subs, blk))
                    dps.append(_dot_nt(dos, vcat))
                    qss.append(qs)
                    doss.append(dos)
                    kcats.append(kcat)
                ps = [jnp.exp(s - lc) for s, lc in zip(scores, lcols)]
                dss = [p * (dp - dc) for p, dp, dc in zip(ps, dps, dcols)]
                db_ref[...] += functools.reduce(lambda a, b: a + b, dss)
                if has_sink:
                    sk = s_ref[...]
                    ds_ref[...] -= functools.reduce(lambda a, b: a + b, [dc * jnp.exp(sk - lc) for dc, lc in zip(dcols, lcols)])
                dsbs = [ds.astype(BF16) for ds in dss]
                for j in range(ITEMS):
                    s, r = _item_of(j, r0, per_group)
                    dq2 = jnp.dot(dsbs[j], kcats[j], preferred_element_type=F32) * scale
                    _set_span_rows(dq_ref, s * halo, r, blk, d, jnp.where(left, dq2[:blk], dq2[blk:]))
                news = [(_dot_tn(dsbs[j], qss[j]), _dot_tn(ps[j].astype(BF16), doss[j])) for j in range(ITEMS)]
                for which, w in enumerate((wk, wv)):
                    for jr in range(per_group):
                        for sp in range(-1, subs + 1):
                            parts = [news[s * per_group + jr][which][(sp - s + 1) * blk:(sp - s + 2) * blk]
                                     for s in range(subs) if 0 <= sp - s + 1 < 3]
                            _set_span_rows(w, chunk + sp * halo, r0 + jr, blk, d,
                                           functools.reduce(lambda x, y: x + y, parts), add=True)

            _for_groups(group, groups, per_group)

        dk_ref[...] = wk[0:chunk]
        dv_ref[...] = wv[0:chunk]

    cur, prev, nxt = _attn_specs(seq, blk, d, lambda t: jnp.minimum(t, nct - 1))
    v_cur, v_prev, v_nxt = _attn_specs(seq, blk, d, lambda t: jnp.minimum(t, nct - 1), v_col)
    lag = pl.BlockSpec((chunk, LANES), lambda hp, t: (jnp.maximum(t - 1, 0), hp))
    band = pl.BlockSpec((2 * blk, 3 * blk), lambda hp, t: (hp, 0))
    col = pl.BlockSpec((2 * blk, 1), lambda hp, t: (hp, 0))
    in_specs = [cur, prev, cur, nxt, v_prev, v_cur, v_nxt, cur, cur, cur, band]
    args = [q, k, k, k, v, v, v, do, lse, delta, bias]
    out_shape = [jax.ShapeDtypeStruct((seq, WIDTH), F32)] * 3 + [jax.ShapeDtypeStruct((N_HEADS * blk, 3 * blk), F32)]
    out_specs = [cur, lag, lag, band]
    if has_sink:
        in_specs.append(col)
        args.append(sink)
        out_shape.append(jax.ShapeDtypeStruct((N_HEADS * blk, 1), F32))
        out_specs.append(col)
    window = pltpu.VMEM((2 * chunk + halo, LANES), F32)
    return pl.pallas_call(
        body, out_shape=out_shape, grid=(N_PAIRS, nct + 1), in_specs=in_specs, out_specs=out_specs,
        scratch_shapes=[window, window], name=name,
        compiler_params=_params("arbitrary", "arbitrary"))(*args)


def _combine_patterns(outs, lses):
    def combine(*tiles):
        os_, ls = tiles[:len(outs)], tiles[len(outs):]
        m = functools.reduce(jnp.maximum, ls)
        es = [jnp.exp(l - m) for l in ls]
        den = functools.reduce(lambda a, b: a + b, es)
        num = functools.reduce(lambda a, b: a + b, [e * o for e, o in zip(es, os_)])
        return num / den, m + jnp.log(den)

    return _ew(combine, [*outs, *lses], [F32, F32], "combine_a")


def _row_dots(dy, y, name):
    return _ew(lambda dy_, y_: (_seg_sum(dy_ * y_),), [dy, y], [F32], name)[0]


def _tile_gain(g, reps):
    return jnp.tile(g[None, :], (1, reps))


def _local_step(x, p, target, w_in_of, rest_of, small):
    rel_table = small["rel_table"]
    buckets_a = [_band_buckets(blk, d) for blk, d in DILATED]
    buckets_b = _band_buckets(BLK_B, 1)
    bias_a = [_bias_tiles(rel_table, bk, 0, "bias_a").reshape(N_HEADS * bk.shape[0], -1) for bk in buckets_a]
    bias_b = _bias_tiles(rel_table, buckets_b, N_HEADS, "bias_b").reshape(N_HEADS * BLK_B, -1)

    saved = []
    for l in range(DEPTH):
        g_mix, g_ffn, g_ple = (small[n][l][None, :] for n in ("norm_mix_g", "norm_ffn_g", "norm_ple_g"))
        gqa, gka, gqb = (_tile_gain(small[n][l], N_HEADS) for n in ("qnorm_a_g", "knorm_a_g", "qnorm_b_g"))
        gkb = _tile_gain(small["knorm_b_g"][l], N_KV_B)
        sink = jnp.repeat(small["sink_b"][l], BLK_B)[:, None]

        h = _rms_fwd(x, g_mix, "rms_mix")
        w_in = w_in_of(l, (h, bias_a, bias_b))
        proj = _mm(h, w_in, "nt", F32, "mm_in")
        qa, ka, qb, kb, vb = _qknorm_fwd(proj, gqa, gka, gqb, gkb)
        outs, lses = [], []
        for (blk, d), bias in zip(DILATED, bias_a):
            o, ls = _attn_fwd(qa, ka, proj, bias, None, blk, d, f"attn_a{d}_fwd", v_col=OFF_VA)
            outs.append(o)
            lses.append(ls)
        ya, lse_a = _combine_patterns(outs, lses)
        yb, lse_b = _attn_fwd(qb, kb, vb, bias_b, sink, BLK_B, 1, "attn_b_fwd")
        w = dict(rest_of(l, yb), w_in=w_in)
        def gate(products, extra):
            (ca_, cb_), (ga_, gb_) = products, extra
            return _sigmoid(ga_) * ca_ + _sigmoid(gb_) * cb_, ca_, cb_

        merged, ca, cb = _mm_fused([(ya, w["w_branch_a"], "nn"), (yb, w["w_branch_b"], "nn")],
                                   [(proj, OFF_GA), (proj, OFF_GB)], gate, [BF16, BF16, BF16], "mm_branches_gate")
        x1 = _mm(merged, w["w_out"], "nn", F32, "mm_out", res=x)

        h2 = _rms_fwd(x1, g_ffn, "rms_ffn")

        def swiglu(products, extra):
            a_, u_ = products
            return (a_ * _sigmoid(a_)) * u_, a_, u_

        hid, a, u = _mm_fused([(h2, w["w_ffn_gate"], "nt"), (h2, w["w_ffn_up"], "nt")], [], swiglu,
                              [BF16, BF16, BF16], "mm_ffn_gate_up")
        x2 = _mm(hid, w["w_ffn_down"], "nn", F32, "mm_ffn_down", res=x1)

        h3 = _rms_fwd(x2, g_ple, "rms_ple")

        def ple(products, extra):
            z_, e_ = products
            return extra[0] + _sigmoid(z_) * e_, z_, e_

        x3, z, e = _mm_fused([(h3, w["w_ple_gate"], "nn"), (p[l], w["w_ple_proj"], "nn")], [(x2, 0)], ple,
                             [F32, BF16, BF16], "mm_ple")
        saved.append(dict(w=w, x0=x, h=h, proj=proj, qa=qa, ka=ka, qb=qb, kb=kb, vb=vb, ya=ya, lse_a=lse_a,
                          yb=yb, lse_b=lse_b, ca=ca, cb=cb, merged=merged, x1=x1, h2=h2, a=a, u=u, hid=hid,
                          x2=x2, h3=h3, z=z, e=e))
        x = x3

    dx, loss_acc = _loss_grad(x, target)
    loss = loss_acc[0, 0]

    gbig = [{} for _ in range(DEPTH)]
    marks = [{} for _ in range(DEPTH)]
    gsmall = {n: [None] * DEPTH for n in SMALL if n != "rel_table"}
    dbias_a = [[] for _ in DILATED]
    dbias_b = []

    for l in reversed(range(DEPTH)):
        sv = saved[l]
        w = sv["w"]
        g_mix, g_ffn, g_ple = (small[n][l][None, :] for n in ("norm_mix_g", "norm_ffn_g", "norm_ple_g"))
        gqa, gka, gqb = (_tile_gain(small[n][l], N_HEADS) for n in ("qnorm_a_g", "knorm_a_g", "qnorm_b_g"))
        gkb = _tile_gain(small["knorm_b_g"][l], N_KV_B)
        sink = jnp.repeat(small["sink_b"][l], BLK_B)[:, None]

        def ple_bwd(dx_, z_, e_):
            s = _sigmoid(z_.astype(F32))
            return dx_ * s, dx_ * e_.astype(F32) * (s * (1.0 - s))

        de, dz = _ew(ple_bwd, [dx, sv["z"], sv["e"]], [BF16, BF16], "ple_bwd")
        gbig[l]["w_ple_proj"] = _mm(p[l], de, "tn", BF16, "mm_d_ple_proj")
        gbig[l]["w_ple_gate"] = _mm(sv["h3"], dz, "tn", BF16, "mm_d_ple_gate")
        dh3 = _mm(dz, w["w_ple_gate"], "nt", F32, "mm_dh3")
        dx, dxb, gsmall["norm_ple_g"][l] = _rms_bwd(sv["x2"], g_ple, dh3, dx, "rms_ple_bwd")

        gbig[l]["w_ffn_down"] = _mm(sv["hid"], dxb, "tn", BF16, "mm_d_ffn_down")

        def swiglu_bwd(products, extra):
            dh_, a_, u_ = products[0], extra[0].astype(F32), extra[1].astype(F32)
            s = _sigmoid(a_)
            return dh_ * u_ * (s * (1.0 + a_ * (1.0 - s))), dh_ * (a_ * s)

        da, du = _mm_fused([(dxb, w["w_ffn_down"], "nt")], [(sv["a"], 0), (sv["u"], 0)], swiglu_bwd, [BF16, BF16],
                           "mm_dhid_swiglu_bwd")
        gbig[l]["w_ffn_gate"] = _mm(da, sv["h2"], "tn", BF16, "mm_d_ffn_gate")
        gbig[l]["w_ffn_up"] = _mm(du, sv["h2"], "tn", BF16, "mm_d_ffn_up")
        dh2 = _mm(da, w["w_ffn_gate"], "nn", F32, "mm_dh2_gate")
        dh2 = _mm(du, w["w_ffn_up"], "nn", F32, "mm_dh2_up", res=dh2)
        dx, dxb, gsmall["norm_ffn_g"][l] = _rms_bwd(sv["x1"], g_ffn, dh2, dx, "rms_ffn_bwd")

        gbig[l]["w_out"] = _mm(sv["merged"], dxb, "tn", BF16, "mm_d_out")

        def gate_bwd(products, extra):
            dm_, ca_, cb_ = products[0], extra[0].astype(F32), extra[1].astype(F32)
            sa, sb = _sigmoid(extra[2]), _sigmoid(extra[3])
            return dm_ * sa, dm_ * sb, dm_ * ca_ * (sa * (1.0 - sa)), dm_ * cb_ * (sb * (1.0 - sb))

        dca, dcb, dga, dgb = _mm_fused(
            [(dxb, w["w_out"], "nt")], [(sv["ca"], 0), (sv["cb"], 0), (sv["proj"], OFF_GA), (sv["proj"], OFF_GB)],
            gate_bwd, [BF16, BF16, BF16, BF16], "mm_dmerged_gate_bwd")
        gbig[l]["w_branch_a"] = _mm(sv["ya"], dca, "tn", BF16, "mm_d_branch_a")
        gbig[l]["w_branch_b"] = _mm(sv["yb"], dcb, "tn", BF16, "mm_d_branch_b")
        dya = _mm(dca, w["w_branch_a"], "nt", F32, "mm_dya")
        dyb = _mm(dcb, w["w_branch_b"], "nt", F32, "mm_dyb")

        delta_a = _row_dots(dya, sv["ya"], "attn_a_row_dots")
        delta_b = _row_dots(dyb, sv["yb"], "attn_b_row_dots")

        dqa, dka, dva = [], [], []
        for (blk, d), bias, bk in zip(DILATED, bias_a, buckets_a):
            dq_, dk_, dv_, db_ = _attn_bwd(sv["qa"], sv["ka"], sv["proj"], dya, sv["lse_a"], delta_a, bias, None, blk, d,
                                           f"attn_a{d}_bwd", v_col=OFF_VA)
            dqa.append(dq_)
            dka.append(dk_)
            dva.append(dv_)
            dbias_a[len(dqa) - 1].append(db_)
        dqb, dkb, dvb, db_, dsink = _attn_bwd(sv["qb"], sv["kb"], sv["vb"], dyb, sv["lse_b"], delta_b, bias_b, sink,
                                              BLK_B, 1, "attn_b_bwd")
        dbias_b.append(db_)
        gsmall["sink_b"][l] = dsink.reshape(N_HEADS, BLK_B).sum(axis=1)

        dproj, pqa, pka, pqb, pkb = _qknorm_bwd(sv["proj"], gqa, gka, gqb, gkb, dqa, dka, dva, dqb, dkb, dvb, dga, dgb)
        marks[l]["attn_bwd_done"] = dproj
        gsmall["qnorm_a_g"][l] = pqa.reshape(N_HEADS, HEAD_DIM).sum(0)
        gsmall["knorm_a_g"][l] = pka.reshape(N_HEADS, HEAD_DIM).sum(0)
        gsmall["qnorm_b_g"][l] = pqb.reshape(N_HEADS, HEAD_DIM).sum(0)
        gsmall["knorm_b_g"][l] = pkb.reshape(N_KV_B, HEAD_DIM).sum(0)
        gbig[l]["w_in"] = _mm(dproj, sv["h"], "tn", BF16, "mm_d_in")
        dh = _mm(dproj, w["w_in"], "nn", F32, "mm_dh")
        dx, _, gsmall["norm_mix_g"][l] = _rms_bwd(sv["x0"], g_mix, dh, dx, "rms_mix_bwd")
        gsmall["norm_mix_g"][l] = gsmall["norm_mix_g"][l][0]
        gsmall["norm_ffn_g"][l] = gsmall["norm_ffn_g"][l][0]
        gsmall["norm_ple_g"][l] = gsmall["norm_ple_g"][l][0]

    gsmall = {n: jnp.stack(v) for n, v in gsmall.items()}
    dtable_a = sum(_table_grad(sum(dbs).reshape(N_HEADS, blk, 3 * blk), bk, "table_grad_a")
                   for dbs, (blk, _), bk in zip(dbias_a, DILATED, buckets_a))
    dtable_b = _table_grad(sum(dbias_b).reshape(N_HEADS, BLK_B, 3 * BLK_B), buckets_b, "table_grad_b")
    gsmall["rel_table"] = jnp.concatenate([dtable_a, dtable_b], axis=0).T
    return loss, dx, gbig, gsmall, marks


def _place():
    return lax.axis_index("x"), lax.axis_index("y"), lax.axis_index("c")


def _flip(v, bit):
    return 1 - v if bit else v


CHIP_RELATIONS = ((0, 1), (1, 0), (1, 1))
ANY = pl.BlockSpec(memory_space=pl.ANY)


def _allgather_body(w_refs, out_refs, send_sems, recv_sems):
    x, y, c = _place()
    chips = [(_flip(x, a), _flip(y, b)) for a, b in CHIP_RELATIONS]

    def make(g):
        w_ref, out_ref = w_refs[g], out_refs[g]
        half = w_ref.shape[0] // 2

        def part(px, py, pc):
            return out_ref.at[2 * px + py, pl.ds(pc * half, half), :]

        def copy(k, block, to, src=None):
            return pltpu.make_async_remote_copy(
                src_ref=part(*block) if src is None else src, dst_ref=part(*block),
                send_sem=send_sems.at[7 * g + k], recv_sem=recv_sems.at[7 * g + k], device_id=to,
                device_id_type=MESH_ID)

        own = pltpu.make_async_remote_copy(
            src_ref=w_ref, dst_ref=out_ref.at[2 * x + y], send_sem=send_sems.at[7 * g + 6],
            recv_sem=recv_sems.at[7 * g + 6], device_id=(x, y, 1 - c), device_id_type=MESH_ID)
        first = [copy(k, (x, y, c), (*chip, c), src=w_ref.at[pl.ds(c * half, half), :]) for k, chip in enumerate(chips)]
        passed = [copy(3 + k, (*chip, c), (x, y, 1 - c)) for k, chip in enumerate(chips)]
        arrive = [copy(k, (*chip, c), (x, y, c)) for k, chip in enumerate(chips)]
        arrive2 = [copy(3 + k, (*chip, 1 - c), (x, y, c)) for k, chip in enumerate(chips)]
        return own, first, passed, arrive, arrive2

    made = [make(g) for g in range(len(w_refs))]
    for own, first, _, _, _ in made:
        own.start()
        for cp in first:
            cp.start()
    for _, _, passed, arrive, _ in made:
        for k in range(3):
            arrive[k].wait_recv()
            passed[k].start()
    for own, first, passed, _, arrive2 in made:
        for k in range(3):
            arrive2[k].wait_recv()
        own.wait_recv()
        for cp in first + passed + [own]:
            cp.wait_send()


def _sibling(x, y, c):
    return [(x, y, 1 - c)]


def _same_core_of_other_chips(x, y, c):
    return [(_flip(x, a), _flip(y, b), c) for a, b in CHIP_RELATIONS]


def _exchange(body, ins, out_types, n_sems, name, sequencer=None):
    n = len(ins)
    sems = (pltpu.SemaphoreType.DMA((n_sems,)), pltpu.SemaphoreType.DMA((n_sems,)))
    if sequencer is None:
        in_place = out_types is None
        out_shape = [jax.ShapeDtypeStruct(a.shape, a.dtype) for a in ins] if in_place else out_types

        def tc_body(*refs):
            body(refs[:n], refs[n:n + len(out_shape)], refs[-2], refs[-1])

        return list(pl.pallas_call(
            tc_body, out_shape=out_shape, in_specs=[ANY] * n, out_specs=[ANY] * len(out_shape),
            input_output_aliases={g: g for g in range(n)} if in_place else {}, scratch_shapes=list(sems), name=name)(*ins))

    collective_id, peers = sequencer
    hbm = pltpu.MemorySpace.HBM
    in_refs = [jax.new_ref(a, memory_space=hbm) for a in ins]
    out_refs = in_refs if out_types is None else [jax.empty_ref(t, memory_space=hbm) for t in out_types]

    @pl.kernel(mesh=plsc.ScalarSubcoreMesh(axis_name="sequencer", num_cores=1), name=name, scratch_types=sems,
               compiler_params=pltpu.CompilerParams(collective_id=collective_id))
    def launch(send_sems, recv_sems):
        barrier = pltpu.get_barrier_semaphore()
        devices = peers(*_place())
        for device in devices:
            pl.semaphore_signal(barrier, inc=1, device_id=device, device_id_type=MESH_ID)
        pl.semaphore_wait(barrier, len(devices))
        body(in_refs, out_refs, send_sems, recv_sems)

    launch()
    return [r[...] for r in out_refs]


def _allgather(shards, name, sequencer=None):
    out_types = [jax.ShapeDtypeStruct((N_CHIPS,) + s.shape, s.dtype) for s in shards]
    if sequencer is not None:
        sequencer = (sequencer, lambda x, y, c: _sibling(x, y, c) + _same_core_of_other_chips(x, y, c))
    return _exchange(_allgather_body, shards, out_types, 7 * len(shards), name, sequencer)


def _half_tile(half):
    return max(t for t in range(16, 1025, 16) if half % t == 0)


def _run_copies(cps):
    for cp in cps:
        cp.start()
    for cp in cps:
        cp.wait_recv()
    for cp in cps:
        cp.wait_send()


def _sibling_halves(gsends, name, sequencer=None):
    def body(g_refs, out_refs, send_sems, recv_sems):
        x, y, c = _place()
        cps = []
        for g, (g_ref, out_ref) in enumerate(zip(g_refs, out_refs)):
            half = g_ref.shape[1] // 2
            cps.append(pltpu.make_async_remote_copy(
                src_ref=g_ref.at[:, pl.ds((1 - c) * half, half), :], dst_ref=out_ref,
                send_sem=send_sems.at[g], recv_sem=recv_sems.at[g], device_id=(x, y, 1 - c), device_id_type=MESH_ID))
        _run_copies(cps)

    out_types = [jax.ShapeDtypeStruct((s.shape[0], s.shape[1] // 2, s.shape[2]), s.dtype) for s in gsends]
    return _exchange(body, gsends, out_types, len(gsends), name, sequencer and (sequencer, _sibling))


def _chip_sums(gsend, sib, place):
    n, rows, cols = gsend.shape
    half = rows // 2
    tm = _half_tile(half)
    nblk = half // tm

    def body(s_ref, g_ref, sib_ref, o_ref):
        o_ref[0] = (g_ref[0].astype(F32) + sib_ref[0].astype(F32)).astype(o_ref.dtype)

    grid_spec = pltpu.PrefetchScalarGridSpec(
        num_scalar_prefetch=1, grid=(n, nblk),
        in_specs=[pl.BlockSpec((1, tm, cols), lambda k, i, s: (jnp.bitwise_xor(s[0], k), s[1] * nblk + i, 0)),
                  pl.BlockSpec((1, tm, cols), lambda k, i, s: (jnp.bitwise_xor(s[0], k), i, 0))],
        out_specs=pl.BlockSpec((1, tm, cols), lambda k, i, s: (k, i, 0)))
    return pl.pallas_call(
        body, out_shape=jax.ShapeDtypeStruct((n, half, cols), BF16), grid_spec=grid_spec,
        name="rs_chip_sums", compiler_params=_params("parallel", "parallel"))(place, gsend, sib)


def _exchange_chip_sums(tsends, name, sequencer=None):
    def body(t_refs, out_refs, send_sems, recv_sems):
        x, y, c = _place()
        cps = []
        for g, (t_ref, out_ref) in enumerate(zip(t_refs, out_refs)):
            for k, device in enumerate(_same_core_of_other_chips(x, y, c)):
                cps.append(pltpu.make_async_remote_copy(
                    src_ref=t_ref.at[k + 1], dst_ref=out_ref.at[k], send_sem=send_sems.at[3 * g + k],
                    recv_sem=recv_sems.at[3 * g + k], device_id=device, device_id_type=MESH_ID))
        _run_copies(cps)

    out_types = [jax.ShapeDtypeStruct((3,) + s.shape[1:], s.dtype) for s in tsends]
    return _exchange(body, tsends, out_types, 3 * len(tsends), name,
                     sequencer and (sequencer, _same_core_of_other_chips))


def _final_sum(tsend, recv, place):
    n, half, cols = tsend.shape
    tm = _half_tile(half)
    nblk = half // tm

    def body(s_ref, t_ref, r_ref, o_ref):
        o_ref[...] = ((t_ref[0].astype(F32) + r_ref[0].astype(F32)) + r_ref[1].astype(F32)) + r_ref[2].astype(F32)

    grid_spec = pltpu.PrefetchScalarGridSpec(
        num_scalar_prefetch=1, grid=(nblk,),
        in_specs=[pl.BlockSpec((1, tm, cols), lambda i, s: (0, i, 0)), pl.BlockSpec((n - 1, tm, cols), lambda i, s: (0, i, 0))],
        out_specs=pl.BlockSpec((tm, cols), lambda i, s: (s[1] * nblk + i, 0)))
    return pl.pallas_call(
        body, out_shape=jax.ShapeDtypeStruct((2 * half, cols), F32), grid_spec=grid_spec, name="rs_final_sum",
        compiler_params=_params("parallel"))(place, tsend, recv)


def _join_halves(gfulls, name, sequencer=None):
    def body(g_refs, out_refs, send_sems, recv_sems):
        x, y, c = _place()
        n = len(g_refs)

        def copy(g, pc):
            half = g_refs[g].shape[0] // 2
            return pltpu.make_async_remote_copy(
                src_ref=g_refs[g].at[pl.ds(pc * half, half), :], dst_ref=out_refs[g].at[pl.ds(pc * half, half), :],
                send_sem=send_sems.at[g], recv_sem=recv_sems.at[g], device_id=(x, y, 1 - c), device_id_type=MESH_ID)

        mine = [copy(g, c) for g in range(n)]
        for cp in mine:
            cp.start()
        for g in range(n):
            copy(g, 1 - c).wait_recv()
        for cp in mine:
            cp.wait_send()

    return _exchange(body, gfulls, None, len(gfulls), name, sequencer and (sequencer, _sibling))


def _allreduce_small(v):
    rows, cols = v.shape

    def body(v_ref, out_ref, buf, send_sems, recv_sems):
        x, y, c = _place()
        cps = []
        for k in range(1, 8):
            peer = (_flip(x, (k >> 2) & 1), _flip(y, (k >> 1) & 1), _flip(c, k & 1))
            cps.append(pltpu.make_async_remote_copy(
                src_ref=v_ref, dst_ref=buf.at[k - 1], send_sem=send_sems.at[k - 1], recv_sem=recv_sems.at[k - 1],
                device_id=peer, device_id_type=MESH_ID))
        for cp in cps:
            cp.start()
        for cp in cps:
            cp.wait_recv()
        for cp in cps:
            cp.wait_send()
        t0 = v_ref[...] + buf[0]
        t1 = buf[1] + buf[2]
        t2 = buf[3] + buf[4]
        t3 = buf[5] + buf[6]
        out_ref[...] = (t0 + t1) + (t2 + t3)

    vm = pl.BlockSpec(memory_space=pltpu.VMEM)
    return pl.pallas_call(
        body, out_shape=jax.ShapeDtypeStruct((rows, cols), F32), in_specs=[vm], out_specs=vm,
        scratch_shapes=[pltpu.VMEM((7, rows, cols), F32), pltpu.SemaphoreType.DMA((7,)), pltpu.SemaphoreType.DMA((7,))],
        name="allreduce_small")(v)


BIG_INFO = {n: (shape, ax) for n, shape, ax in BIG}
GROUPS = (("w_in",), ("w_ffn_gate", "w_ffn_up", "w_ffn_down", "w_out", "w_ple_gate"),
          ("w_branch_a", "w_branch_b", "w_ple_proj"))


def _shard_shape(name):
    (k, m), ax = BIG_INFO[name]
    return (k // N_CHIPS, m) if ax == 0 else (k, m // N_CHIPS)


def _group_rows(group):
    offs, off = {}, 0
    for n in group:
        offs[n] = off
        off += _shard_shape(n)[0]
    return offs, off


def _pack_groups(shards, layer, dtype):
    return [jnp.concatenate([shards[n][layer].astype(dtype) for n in group], axis=0) for group in GROUPS]


def _unpack_full(gathered, groups):
    out = {}
    for group, arr in zip(groups, gathered):
        offs, _ = _group_rows(group)
        for n in group:
            rows, cols = _shard_shape(n)
            (k, m), ax = BIG_INFO[n]
            slab = arr[:, offs[n]:offs[n] + rows]
            out[n] = slab.reshape(k, m) if ax == 0 else jnp.transpose(slab, (1, 0, 2)).reshape(k, m)
    return out


def _pack_grads(gfull):
    out = []
    for group in GROUPS:
        parts = []
        for n in group:
            rows, cols = _shard_shape(n)
            ax = BIG_INFO[n][1]
            slab = (gfull[n].reshape(N_CHIPS, rows, cols) if ax == 0
                    else jnp.transpose(gfull[n].reshape(rows, N_CHIPS, cols), (1, 0, 2)))
            parts.append(slab)
        out.append(jnp.concatenate(parts, axis=1))
    return out


def _after(values, mark):
    values, _ = lax.optimization_barrier((values, mark))
    return values


def _reduce_scatter_begin(gsends, place, tag, ids):
    sibs = _sibling_halves(gsends, "rs_sibling_halves_" + tag, ids[0])
    tsends = [_chip_sums(g, s, place) for g, s in zip(gsends, sibs)]
    return tsends, _exchange_chip_sums(tsends, "rs_exchange_" + tag, ids[1])


def _reduce_scatter_finish(begun, place, tag, ids, hold):
    tsends, recvs = begun
    recvs = _after(recvs, hold)
    return _join_halves([_final_sum(t, r, place) for t, r in zip(tsends, recvs)], "rs_join_halves_" + tag, ids[2])


SMALL_SHAPES = {"rel_table": (NUM_BUCKETS, 2 * N_HEADS), "norm_mix_g": (DEPTH, D_MODEL), "qnorm_a_g": (DEPTH, HEAD_DIM),
                "knorm_a_g": (DEPTH, HEAD_DIM), "qnorm_b_g": (DEPTH, HEAD_DIM), "knorm_b_g": (DEPTH, HEAD_DIM),
                "sink_b": (DEPTH, N_HEADS), "norm_ffn_g": (DEPTH, D_MODEL), "norm_ple_g": (DEPTH, D_MODEL)}


def _pack_small(vals, last=None):
    flat = jnp.concatenate([vals[n].astype(F32).reshape(-1) for n in SMALL])
    tail = jnp.zeros((SMALL_ROWS * LANES - flat.shape[0],), F32)
    if last is not None:
        tail = tail.at[-1].set(last)
    return jnp.concatenate([flat, tail]).reshape(SMALL_ROWS, LANES)


def _unpack_small(packed):
    flat, out, off = packed.reshape(-1), {}, 0
    for n in SMALL:
        size = math.prod(SMALL_SHAPES[n])
        out[n] = flat[off:off + size].reshape(SMALL_SHAPES[n])
        off += size
    return out


def _adamw(w, gs, g_row, m, v, name):
    c1 = 1.0 - ADAM_B1 ** ADAM_STEP
    c2 = 1.0 - ADAM_B2 ** ADAM_STEP
    total, width = w.shape
    n_layers = len(gs)
    per = total // n_layers
    tm = max(t for t in range(8, 513, 8) if per % t == 0 and g_row % t == 0)
    nblk = per // tm

    def body(*refs):
        w_ref, g_refs = refs[0], refs[1:1 + n_layers]
        m_ref, v_ref, og, od, om, ov = refs[1 + n_layers:]
        layer = pl.program_id(0) // nblk
        g = g_refs[0][...]
        for l in range(1, n_layers):
            g = jnp.where(layer == l, g_refs[l][...], g)
        m_new = ADAM_B1 * m_ref[...] + (1.0 - ADAM_B1) * g
        v_new = ADAM_B2 * v_ref[...] + (1.0 - ADAM_B2) * (g * g)
        og[...] = g
        od[...] = -ADAM_LR * ((m_new / c1) / (jnp.sqrt(v_new / c2) + ADAM_EPS) + ADAM_WD * w_ref[...])
        om[...] = m_new
        ov[...] = v_new

    row = pl.BlockSpec((tm, width), lambda i: (i, 0))
    g_specs = [pl.BlockSpec((tm, width), lambda i, l=l: (g_row // tm + jnp.clip(i - l * nblk, 0, nblk - 1), 0))
               for l in range(n_layers)]
    return pl.pallas_call(
        body, out_shape=[jax.ShapeDtypeStruct((total, width), F32)] * 4, grid=(total // tm,),
        in_specs=[row] + g_specs + [row, row], out_specs=[row] * 4, name=name,
        compiler_params=_params("parallel"))(w, *gs, m, v)


def kernel(x, p, rel_table, norm_mix_g, w_in, qnorm_a_g, knorm_a_g, qnorm_b_g, knorm_b_g, sink_b, w_branch_a, w_branch_b, w_out, norm_ffn_g, w_ffn_gate, w_ffn_up, w_ffn_down, norm_ple_g, w_ple_gate, w_ple_proj, loss_target, m_rel_table, m_norm_mix_g, m_w_in, m_qnorm_a_g, m_knorm_a_g, m_qnorm_b_g, m_knorm_b_g, m_sink_b, m_w_branch_a, m_w_branch_b, m_w_out, m_norm_ffn_g, m_w_ffn_gate, m_w_ffn_up, m_w_ffn_down, m_norm_ple_g, m_w_ple_gate, m_w_ple_proj, v_rel_table, v_norm_mix_g, v_w_in, v_qnorm_a_g, v_knorm_a_g, v_qnorm_b_g, v_knorm_b_g, v_sink_b, v_w_branch_a, v_w_branch_b, v_w_out, v_norm_ffn_g, v_w_ffn_gate, v_w_ffn_up, v_w_ffn_down, v_norm_ple_g, v_w_ple_gate, v_w_ple_proj):
    given = dict(locals())

    def held(name, a):
        return jnp.swapaxes(a, 1, 2) if name in TRANSPOSED else a

    weights = {n: held(n, given[n]) for n in WEIGHTS}
    moments_m = {n: held(n, given["m_" + n]) for n in WEIGHTS}
    moments_v = {n: held(n, given["v_" + n]) for n in WEIGHTS}
    xi, yi, ci = _place()
    place = jnp.stack([2 * xi + yi, ci]).astype(jnp.int32)

    shards = [_pack_groups(weights, l, BF16) for l in range(DEPTH)]
    w_in0 = _allgather(shards[0][:1], "allgather_w_in_layer0", sequencer=9)
    rest0 = _allgather(_after(shards[0][1:], w_in0), "allgather_rest_layer0", sequencer=1)
    gathered = [w_in0 + rest0, None]
    small = {n: weights[n] for n in SMALL}

    def w_in_of(l, mark):
        return _unpack_full(_after(gathered[l][:1], (shards[1], mark) if l == 0 else mark), GROUPS[:1])["w_in"]

    def rest_of(l, mark):
        if l == 0:
            gathered[1] = _allgather(_after(shards[1], mark), "allgather_layer1", sequencer=2)
        return _unpack_full(_after(gathered[l][1:], mark), GROUPS[1:])

    loss, dx, gbig, gsmall, marks = _local_step(x[0], p[:, 0], loss_target[0], w_in_of, rest_of, small)

    gsends = [_pack_grads(gbig[l]) for l in range(DEPTH)]
    stages = {"layer1": (gsends[1], (3, 4, 5)), "rest_layer0": (gsends[0][1:], (6, 7, 8)),
              "w_in_layer0": (gsends[0][:1], (10, 11, 12))}
    begun = {tag: _reduce_scatter_begin(g, place, tag, ids) for tag, (g, ids) in stages.items()}

    def finish(tag, hold):
        return _reduce_scatter_finish(begun[tag], place, tag, stages[tag][1], hold)

    red1 = finish("layer1", marks[0]["attn_bwd_done"])
    rest0 = finish("rest_layer0", marks[0]["attn_bwd_done"])

    grads, delta, new_m, new_v = {}, {}, {}, {}

    def update(group, reduced):
        offs, _ = _group_rows(group)
        for n in group:
            shape = weights[n].shape
            two_d = lambda a: a.reshape(shape[0] * shape[1], shape[2])
            outs = _adamw(two_d(weights[n]), reduced, offs[n], two_d(moments_m[n]), two_d(moments_v[n]), "adamw_" + n)
            grads[n], delta[n], new_m[n], new_v[n] = (held(n, o.reshape(shape)) for o in outs)

    for gi in (1, 2):
        update(GROUPS[gi], _after([rest0[gi - 1], red1[gi]], begun["w_in_layer0"][0]))
    small_grads = _allreduce_small(_pack_small(gsmall, last=loss))
    g_, d_, m_, v_ = _adamw(_pack_small(weights), [small_grads], 0, _pack_small(moments_m), _pack_small(moments_v),
                            "adamw_small")
    grads.update(_unpack_small(g_))
    delta.update(_unpack_small(d_))
    new_m.update(_unpack_small(m_))
    new_v.update(_unpack_small(v_))
    others_done = [dx, d_] + [delta[n] for gi in (1, 2) for n in GROUPS[gi]]
    update(GROUPS[0], [finish("w_in_layer0", others_done)[0], red1[0]])

    return (small_grads[-1, -1], dx[None], *[grads[n] for n in WEIGHTS], *[delta[n] for n in WEIGHTS],
            *[new_m[n] for n in WEIGHTS], *[new_v[n] for n in WEIGHTS])
```

```python
import functools
import math

import jax
import jax.numpy as jnp
from jax import lax
from jax.experimental import pallas as pl
from jax.experimental.pallas import tpu as pltpu
from jax.experimental.pallas import tpu_sc as plsc

F32 = jnp.float32
BF16 = jnp.bfloat16
MESH_ID = pl.DeviceIdType.MESH

D_MODEL = 1024
DEPTH = 2
HEAD_DIM = 64
N_HEADS = 8
WIDTH = N_HEADS * HEAD_DIM
N_PAIRS = 4
ITEMS = 16
MAX_CHUNK = 1024
N_KV_B = 2
PLE_DIM = 256
D_FF = 2816
D_IN = 4352
OFF_QA, OFF_KA, OFF_VA, OFF_QB, OFF_KB, OFF_VB, OFF_GA, OFF_GB = 0, 512, 1024, 1536, 2048, 2176, 2304, 3328
DILATED = ((64, 1), (64, 4), (64, 16))
BLK_B = 128
NUM_BUCKETS = 32
MAX_DISTANCE = 1024
RMS_EPS = 1e-6
NEG_INF = -1e30
LANES = 128
ROW_TILE = 256
VMEM_LIMIT = 48 * 1024 * 1024

ADAM_LR, ADAM_B1, ADAM_B2, ADAM_EPS, ADAM_WD, ADAM_STEP = 0.001, 0.9, 0.999, 1e-08, 0.01, 10

TRANSPOSED = ("w_in", "w_ffn_gate", "w_ffn_up")
BIG = (
    ("w_in", (D_IN, D_MODEL), 0),
    ("w_branch_a", (WIDTH, D_MODEL), 1),
    ("w_branch_b", (WIDTH, D_MODEL), 1),
    ("w_out", (D_MODEL, D_MODEL), 0),
    ("w_ffn_gate", (D_FF, D_MODEL), 0),
    ("w_ffn_up", (D_FF, D_MODEL), 0),
    ("w_ffn_down", (D_FF, D_MODEL), 0),
    ("w_ple_gate", (D_MODEL, D_MODEL), 0),
    ("w_ple_proj", (PLE_DIM, D_MODEL), 1),
)
SMALL = ("rel_table", "norm_mix_g", "qnorm_a_g", "knorm_a_g", "qnorm_b_g", "knorm_b_g", "sink_b",
         "norm_ffn_g", "norm_ple_g")
WEIGHTS = ("rel_table", "norm_mix_g", "w_in", "qnorm_a_g", "knorm_a_g", "qnorm_b_g", "knorm_b_g", "sink_b",
           "w_branch_a", "w_branch_b", "w_out", "norm_ffn_g", "w_ffn_gate", "w_ffn_up", "w_ffn_down",
           "norm_ple_g", "w_ple_gate", "w_ple_proj")
N_CHIPS = 4
SMALL_ROWS = 64


def _params(*sem):
    return pltpu.CompilerParams(dimension_semantics=sem, vmem_limit_bytes=VMEM_LIMIT)


MM_VMEM_BUDGET = 40 * 1024 * 1024
STEP_OVERHEAD_S = 0.4e-6
TILE_DMA_BYTES_PER_S = 1.5e12


def _mm_dims(a, b, mode):
    if mode == "nn":
        return a.shape[0], b.shape[1], a.shape[1]
    if mode == "nt":
        return a.shape[0], b.shape[0], a.shape[1]
    return a.shape[1], b.shape[1], a.shape[0]


def _mm_tiles(m, n, pairs, tile_bytes, col_offsets):
    best = None
    for tm in (t for t in range(LANES, m + 1, LANES) if m % t == 0):
        for tn in (t for t in range(LANES, n + 1, LANES) if n % t == 0 and all(o % t == 0 for o in col_offsets)):
            io = sum(tm * k * ab + tn * k * bb for k, ab, bb in pairs) + tm * tn * sum(tile_bytes)
            casts = sum((tm * k * 2 if ab == 4 else 0) + (tn * k * 2 if bb == 4 else 0) for k, ab, bb in pairs)
            if 2 * io + len(pairs) * tm * tn * 4 + casts > MM_VMEM_BUDGET:
                continue
            cost = (m // tm) * (n // tn) * STEP_OVERHEAD_S + io / TILE_DMA_BYTES_PER_S
            if best is None or (cost, -tm) < best[0]:
                best = ((cost, -tm), tm, tn)
    return best[1], best[2]


def _mm_fused(pairs, extras, epilogue, out_dtypes, name):
    m, n, _ = _mm_dims(*pairs[0])
    assert all(_mm_dims(*p)[:2] == (m, n) for p in pairs)
    tm, tn = _mm_tiles(
        m, n, [(_mm_dims(a, b, mode)[2], a.dtype.itemsize, b.dtype.itemsize) for a, b, mode in pairs],
        [e.dtype.itemsize for e, _ in extras] + [jnp.dtype(d).itemsize for d in out_dtypes], [off for _, off in extras])
    dims = {"nn": (((1,), (0,)), ((), ())), "nt": (((1,), (1,)), ((), ())), "tn": (((0,), (0,)), ((), ()))}
    in_specs, args = [], []
    for a, b, mode in pairs:
        k = _mm_dims(a, b, mode)[2]
        in_specs.append(pl.BlockSpec((k, tm), lambda i, j: (0, i)) if mode == "tn" else pl.BlockSpec((tm, k), lambda i, j: (i, 0)))
        in_specs.append(pl.BlockSpec((tn, k), lambda i, j: (j, 0)) if mode == "nt" else pl.BlockSpec((k, tn), lambda i, j: (0, j)))
        args += [a, b]
    for e, off in extras:
        in_specs.append(pl.BlockSpec((tm, tn), lambda i, j, o=off // tn: (i, o + j)))
        args.append(e)
    n_pairs, n_in = len(pairs), 2 * len(pairs) + len(extras)

    def body(*refs):
        products = [lax.dot_general(refs[2 * p][...].astype(BF16), refs[2 * p + 1][...].astype(BF16), dims[pairs[p][2]],
                                    preferred_element_type=F32) for p in range(n_pairs)]
        outs = epilogue(products, [r[...] for r in refs[2 * n_pairs:n_in]])
        for r, o in zip(refs[n_in:], outs):
            r[...] = o.astype(r.dtype)

    tile = pl.BlockSpec((tm, tn), lambda i, j: (i, j))
    return pl.pallas_call(
        body, out_shape=[jax.ShapeDtypeStruct((m, n), d) for d in out_dtypes], grid=(m // tm, n // tn),
        in_specs=in_specs, out_specs=[tile] * len(out_dtypes), name=name,
        compiler_params=_params("parallel", "parallel"))(*args)


def _mm(a, b, mode, out_dtype, name, res=None):
    if res is None:
        return _mm_fused([(a, b, mode)], [], lambda products, extra: products, [out_dtype], name)[0]
    return _mm_fused([(a, b, mode)], [(res, 0)], lambda products, extra: [products[0] + extra[0]], [out_dtype], name)[0]


def _ew(fn, ins, out_dtypes, name):
    rows, width = ins[0].shape
    n_in = len(ins)

    def body(*refs):
        outs = fn(*[r[...] for r in refs[:n_in]])
        for r, o in zip(refs[n_in:], outs):
            r[...] = o.astype(r.dtype)

    row = pl.BlockSpec((ROW_TILE, width), lambda i: (i, 0))
    return pl.pallas_call(
        body, out_shape=[jax.ShapeDtypeStruct((rows, width), dt) for dt in out_dtypes], grid=(rows // ROW_TILE,),
        in_specs=[row] * n_in, out_specs=[row] * len(out_dtypes), name=name, compiler_params=_params("parallel"))(*ins)


def _sigmoid(x):
    return 1.0 / (1.0 + jnp.exp(-x))


def _seg_sum(v):
    outs = []
    for k in range(v.shape[1] // LANES):
        vp = v[:, k * LANES:(k + 1) * LANES]
        left = lax.broadcasted_iota(jnp.int32, vp.shape, 1) < HEAD_DIM
        sl = jnp.sum(jnp.where(left, vp, 0.0), axis=-1, keepdims=True)
        sr = jnp.sum(jnp.where(left, 0.0, vp), axis=-1, keepdims=True)
        outs.append(jnp.where(left, sl, sr))
    return outs[0] if len(outs) == 1 else jnp.concatenate(outs, axis=1)


def _seg_rstd(x):
    return lax.rsqrt(_seg_sum(x * x) * (1.0 / HEAD_DIM) + RMS_EPS)


def _rms_fwd(x, g, name):
    rows, d = x.shape
    tm = ROW_TILE

    def body(x_ref, g_ref, h_ref):
        xv = x_ref[...]
        r = lax.rsqrt(jnp.mean(xv * xv, axis=-1, keepdims=True) + RMS_EPS)
        h_ref[...] = ((xv * r) * g_ref[...]).astype(BF16)

    return pl.pallas_call(
        body, out_shape=jax.ShapeDtypeStruct((rows, d), BF16), grid=(rows // tm,),
        in_specs=[pl.BlockSpec((tm, d), lambda i: (i, 0)), pl.BlockSpec((1, d), lambda i: (0, 0))],
        out_specs=pl.BlockSpec((tm, d), lambda i: (i, 0)), name=name,
        compiler_params=_params("parallel"))(x, g)


def _rms_bwd(x, g, dh, dres, name):
    rows, d = x.shape
    tm = ROW_TILE

    def body(x_ref, g_ref, dh_ref, dres_ref, dx_ref, dxb_ref, dg_ref):
        xv = x_ref[...]
        r = lax.rsqrt(jnp.mean(xv * xv, axis=-1, keepdims=True) + RMS_EPS)
        xh = xv * r
        dhv = dh_ref[...]
        dxh = dhv * g_ref[...]
        dxv = dres_ref[...] + r * (dxh - xh * jnp.mean(dxh * xh, axis=-1, keepdims=True))
        dx_ref[...] = dxv
        dxb_ref[...] = dxv.astype(BF16)
        part = jnp.sum(dhv * xh, axis=0, keepdims=True)

        @pl.when(pl.program_id(0) == 0)
        def _():
            dg_ref[...] = part

        @pl.when(pl.program_id(0) > 0)
        def _():
            dg_ref[...] += part

    row = pl.BlockSpec((tm, d), lambda i: (i, 0))
    vec = pl.BlockSpec((1, d), lambda i: (0, 0))
    return pl.pallas_call(
        body, out_shape=[jax.ShapeDtypeStruct((rows, d), F32), jax.ShapeDtypeStruct((rows, d), BF16),
                         jax.ShapeDtypeStruct((1, d), F32)],
        grid=(rows // tm,), in_specs=[row, vec, row, row], out_specs=[row, row, vec],
        name=name, compiler_params=_params("arbitrary"))(x, g, dh, dres)


def _loss_grad(y, t):
    rows, d = y.shape
    tm = ROW_TILE

    def body(y_ref, t_ref, dy_ref, l_ref):
        e = y_ref[...] - t_ref[...]
        dy_ref[...] = e * (1.0 / d)
        part = jnp.zeros((1, LANES), F32) + jnp.sum(e * e) * (0.5 / d)

        @pl.when(pl.program_id(0) == 0)
        def _():
            l_ref[...] = part

        @pl.when(pl.program_id(0) > 0)
        def _():
            l_ref[...] += part

    row = pl.BlockSpec((tm, d), lambda i: (i, 0))
    return pl.pallas_call(
        body, out_shape=[jax.ShapeDtypeStruct((rows, d), F32), jax.ShapeDtypeStruct((1, LANES), F32)],
        grid=(rows // tm,), in_specs=[row, row], out_specs=[row, pl.BlockSpec((1, LANES), lambda i: (0, 0))],
        name="loss_grad", compiler_params=_params("arbitrary"))(y, t)


def _swap_halves(v):
    return pltpu.roll(v, HEAD_DIM, axis=1)


def _expand_kv(kv):
    left = lax.broadcasted_iota(jnp.int32, kv.shape, 1) < HEAD_DIM
    sw = _swap_halves(kv)
    h0 = jnp.where(left, kv, sw)
    h1 = jnp.where(left, sw, kv)
    return jnp.concatenate([h0, h0, h1, h1], axis=1)


def _reduce_kv(dkv):
    left = lax.broadcasted_iota(jnp.int32, (dkv.shape[0], LANES), 1) < HEAD_DIM
    t = dkv[:, 0:LANES] + dkv[:, LANES:2 * LANES]
    u = dkv[:, 2 * LANES:3 * LANES] + dkv[:, 3 * LANES:4 * LANES]
    t = t + _swap_halves(t)
    u = u + _swap_halves(u)
    return jnp.where(left, t, u)


def _qknorm_fwd(proj, gqa, gka, gqb, gkb):
    rows = proj.shape[0]
    tm = ROW_TILE

    def body(qa_ref, ka_ref, qb_ref, kb_ref, vb_ref, gqa_ref, gka_ref, gqb_ref, gkb_ref, oqa, oka, oqb, okb, ovb):
        for src, g_ref, dst in ((qa_ref, gqa_ref, oqa), (ka_ref, gka_ref, oka), (qb_ref, gqb_ref, oqb)):
            xv = src[...]
            dst[...] = (xv * _seg_rstd(xv)) * g_ref[...]
        kv = kb_ref[...]
        okb[...] = _expand_kv((kv * _seg_rstd(kv)) * gkb_ref[...])
        ovb[...] = _expand_kv(vb_ref[...])

    def win(width, off):
        return pl.BlockSpec((tm, width), lambda i: (i, off // width))

    vec = lambda w: pl.BlockSpec((1, w), lambda i: (0, 0))
    out = pl.BlockSpec((tm, WIDTH), lambda i: (i, 0))
    return pl.pallas_call(
        body, out_shape=[jax.ShapeDtypeStruct((rows, WIDTH), F32)] * 5, grid=(rows // tm,),
        in_specs=[win(WIDTH, OFF_QA), win(WIDTH, OFF_KA), win(WIDTH, OFF_QB), win(LANES, OFF_KB), win(LANES, OFF_VB),
                  vec(WIDTH), vec(WIDTH), vec(WIDTH), vec(LANES)],
        out_specs=[out] * 5, name="qknorm_fwd", compiler_params=_params("parallel"))(
            proj, proj, proj, proj, proj, gqa, gka, gqb, gkb)


def _norm_bwd(xv, g, dy):
    r = _seg_rstd(xv)
    xh = xv * r
    dxh = dy * g
    dx = r * (dxh - xh * (_seg_sum(dxh * xh) * (1.0 / HEAD_DIM)))
    return dx, jnp.sum(dy * xh, axis=0, keepdims=True)


def _qknorm_bwd(proj, gqa, gka, gqb, gkb, dqa, dka, dva, dqb, dkb, dvb, dga, dgb):
    rows = proj.shape[0]
    tm = ROW_TILE
    n_a = len(dqa)

    def body(*refs):
        qa_ref, ka_ref, qb_ref, kb_ref, gqa_ref, gka_ref, gqb_ref, gkb_ref = refs[:8]
        pos = 8
        dqa_refs, dka_refs, dva_refs = refs[pos:pos + n_a], refs[pos + n_a:pos + 2 * n_a], refs[pos + 2 * n_a:pos + 3 * n_a]
        pos += 3 * n_a
        dqb_ref, dkb_ref, dvb_ref, dga_ref, dgb_ref = refs[pos:pos + 5]
        dproj_ref, ogqa, ogka, ogqb, ogkb = refs[pos + 5:]

        def total(rs):
            acc = rs[0][...]
            for r in rs[1:]:
                acc = acc + r[...]
            return acc

        dx_qa, p_qa = _norm_bwd(qa_ref[...], gqa_ref[...], total(dqa_refs))
        dx_ka, p_ka = _norm_bwd(ka_ref[...], gka_ref[...], total(dka_refs))
        dx_qb, p_qb = _norm_bwd(qb_ref[...], gqb_ref[...], dqb_ref[...])
        dx_kb, p_kb = _norm_bwd(kb_ref[...], gkb_ref[...], _reduce_kv(dkb_ref[...]))
        dproj_ref[:, OFF_QA:OFF_QA + WIDTH] = dx_qa.astype(BF16)
        dproj_ref[:, OFF_KA:OFF_KA + WIDTH] = dx_ka.astype(BF16)
        dproj_ref[:, OFF_VA:OFF_VA + WIDTH] = total(dva_refs).astype(BF16)
        dproj_ref[:, OFF_QB:OFF_QB + WIDTH] = dx_qb.astype(BF16)
        dproj_ref[:, OFF_KB:OFF_KB + LANES] = dx_kb.astype(BF16)
        dproj_ref[:, OFF_VB:OFF_VB + LANES] = _reduce_kv(dvb_ref[...]).astype(BF16)
        dproj_ref[:, OFF_GA:OFF_GB] = dga_ref[...]
        dproj_ref[:, OFF_GB:D_IN] = dgb_ref[...]
        first = pl.program_id(0) == 0
        for o_ref, part in ((ogqa, p_qa), (ogka, p_ka), (ogqb, p_qb), (ogkb, p_kb)):
            @pl.when(first)
            def _(o_ref=o_ref, part=part):
                o_ref[...] = part

            @pl.when(jnp.logical_not(first))
            def _(o_ref=o_ref, part=part):
                o_ref[...] += part

    def win(width, off):
        return pl.BlockSpec((tm, width), lambda i: (i, off // width))

    vec = lambda w: pl.BlockSpec((1, w), lambda i: (0, 0))
    row = lambda w: pl.BlockSpec((tm, w), lambda i: (i, 0))
    in_specs = [win(WIDTH, OFF_QA), win(WIDTH, OFF_KA), win(WIDTH, OFF_QB), win(LANES, OFF_KB),
                vec(WIDTH), vec(WIDTH), vec(WIDTH), vec(LANES)]
    in_specs += [row(WIDTH)] * (3 * n_a + 3) + [row(D_MODEL)] * 2
    return pl.pallas_call(
        body,
        out_shape=[jax.ShapeDtypeStruct((rows, D_IN), BF16), jax.ShapeDtypeStruct((1, WIDTH), F32),
                   jax.ShapeDtypeStruct((1, WIDTH), F32), jax.ShapeDtypeStruct((1, WIDTH), F32),
                   jax.ShapeDtypeStruct((1, LANES), F32)],
        grid=(rows // tm,), in_specs=in_specs,
        out_specs=[row(D_IN), vec(WIDTH), vec(WIDTH), vec(WIDTH), vec(LANES)],
        name="qknorm_bwd", compiler_params=_params("arbitrary"))(
            proj, proj, proj, proj, gqa, gka, gqb, gkb, *dqa, *dka, *dva, dqb, dkb, dvb, dga, dgb)


def _t5_bucket(rel):
    half_b = NUM_BUCKETS // 2
    max_exact = half_b // 2
    sign = jnp.where(rel > 0, half_b, 0)
    n = jnp.abs(rel)
    nf = jnp.maximum(n, 1).astype(F32)
    large = max_exact + (jnp.log(nf / max_exact) / math.log(MAX_DISTANCE / max_exact)
                         * (half_b - max_exact)).astype(jnp.int32)
    large = jnp.minimum(large, half_b - 1)
    return sign + jnp.where(n < max_exact, n, large)


def _band_buckets(blk, dilation):
    i = jnp.arange(blk, dtype=jnp.int32)[:, None]
    j = jnp.arange(3 * blk, dtype=jnp.int32)[None, :]
    rel = j - blk - i
    return jnp.where(jnp.abs(rel) <= blk, _t5_bucket(rel * dilation), -1)


def _bias_tiles(table, buckets, head_off, name):
    blk = buckets.shape[0]

    def body(tab_ref, bk_ref, o_ref):
        h = pl.program_id(0) + head_off
        bk = bk_ref[...]
        acc = jnp.full(bk.shape, NEG_INF, F32)
        for b in range(NUM_BUCKETS):
            acc = jnp.where(bk == b, tab_ref[b, h], acc)
        o_ref[0] = acc

    return pl.pallas_call(
        body, out_shape=jax.ShapeDtypeStruct((N_HEADS, blk, 3 * blk), F32), grid=(N_HEADS,),
        in_specs=[pl.BlockSpec(memory_space=pltpu.SMEM), pl.BlockSpec((blk, 3 * blk), lambda h: (0, 0))],
        out_specs=pl.BlockSpec((1, blk, 3 * blk), lambda h: (h, 0, 0)),
        name=name, compiler_params=_params("parallel"))(table, buckets)


def _table_grad(dbias, buckets, name):
    blk = buckets.shape[0]

    def body(db_ref, bk_ref, o_ref):
        bk = bk_ref[...]
        dbv = db_ref[0]
        lane = lax.broadcasted_iota(jnp.int32, (1, LANES), 1)
        acc = jnp.zeros((1, LANES), F32)
        for b in range(NUM_BUCKETS):
            acc = jnp.where(lane == b, jnp.sum(jnp.where(bk == b, dbv, 0.0)), acc)
        o_ref[0] = acc

    out = pl.pallas_call(
        body, out_shape=jax.ShapeDtypeStruct((N_HEADS, 1, LANES), F32), grid=(N_HEADS,),
        in_specs=[pl.BlockSpec((1, blk, 3 * blk), lambda h: (h, 0, 0)), pl.BlockSpec((blk, 3 * blk), lambda h: (0, 0))],
        out_specs=pl.BlockSpec((1, 1, LANES), lambda h: (h, 0, 0)),
        name=name, compiler_params=_params("parallel"))(dbias, buckets)
    return out[:, 0, :NUM_BUCKETS]


def _dot_nt(a, b):
    return lax.dot_general(a, b, (((1,), (1,)), ((), ())), preferred_element_type=F32)


def _dot_tn(a, b):
    return lax.dot_general(a, b, (((0,), (0,)), ((), ())), preferred_element_type=F32)


def _stack_pair(x2, left):
    return jnp.concatenate([jnp.where(left, x2, 0.0), jnp.where(left, 0.0, x2)], axis=0).astype(BF16)


def _attn_geometry(blk, d):
    halo = blk * d
    subs = max(1, min(ITEMS // d, MAX_CHUNK // halo))
    return subs, min(d, ITEMS // subs), halo


def _item_of(j, r0, per_group):
    return j // per_group, r0 + j % per_group


def _span_rows(ref, first, r, blk, d):
    if d == 1:
        return ref[first:first + blk, :]
    return ref[pl.ds(first + r, blk, stride=d), :]


def _set_span_rows(ref, first, r, blk, d, val, add=False):
    idx = slice(first, first + blk) if d == 1 else pl.ds(first + r, blk, stride=d)
    ref[idx, :] = ref[idx, :] + val if add else val


def _for_groups(group, groups, per_group):
    if groups == 1:
        group(0)
    else:
        def step(g, carry):
            group(g * per_group)
            return carry

        lax.fori_loop(0, groups, step, 0)


def _key_rows(p_ref, c_ref, n_ref, s, r, subs, halo, blk, d):
    parts = []
    for span in (s - 1, s, s + 1):
        if span < 0:
            parts.append(_span_rows(p_ref, 0, r, blk, d))
        elif span == subs:
            parts.append(_span_rows(n_ref, 0, r, blk, d))
        else:
            parts.append(_span_rows(c_ref, span * halo, r, blk, d))
    return jnp.concatenate(parts, axis=0)


def _item_penalty(t, nct, s, subs, blk):
    first_ok = True if s > 0 else t > 0
    last_ok = True if s < subs - 1 else t < nct - 1
    col = lax.broadcasted_iota(jnp.int32, (1, 3 * blk), 1)
    ok = jnp.logical_and(jnp.logical_or(col >= blk, first_ok), jnp.logical_or(col < 2 * blk, last_ok))
    return jnp.where(ok, 0.0, NEG_INF).astype(F32)


def _attn_specs(seq, blk, d, step_of, col=0):
    subs, _, halo = _attn_geometry(blk, d)
    last, first = seq // halo - 1, col // LANES
    cur = pl.BlockSpec((subs * halo, LANES), lambda hp, t: (step_of(t), first + hp))
    prev = pl.BlockSpec((halo, LANES), lambda hp, t: (jnp.clip(step_of(t) * subs - 1, 0, last), first + hp))
    nxt = pl.BlockSpec((halo, LANES), lambda hp, t: (jnp.minimum((step_of(t) + 1) * subs, last), first + hp))
    return cur, prev, nxt


def _attn_fwd(q, k, v, bias, sink, blk, d, name, v_col=0):
    seq = q.shape[0]
    subs, per_group, halo = _attn_geometry(blk, d)
    items, chunk, groups = subs * per_group, subs * halo, d // per_group
    nct = seq // chunk
    has_sink = sink is not None
    scale = HEAD_DIM ** -0.5

    def body(*refs):
        q_ref, kp, kc, kn, vp, vc, vn, b_ref = refs[:8]
        s_ref = refs[8] if has_sink else None
        o_ref, l_ref = refs[-2], refs[-1]
        t = pl.program_id(1)
        left = lax.broadcasted_iota(jnp.int32, (1, LANES), 1) < HEAD_DIM
        bias2 = b_ref[...]

        def group(r0):
            scores, vcats = [], []
            for j in range(items):
                s, r = _item_of(j, r0, per_group)
                qs = _stack_pair(_span_rows(q_ref, s * halo, r, blk, d) * scale, left)
                kcat = _key_rows(kp, kc, kn, s, r, subs, halo, blk, d).astype(BF16)
                scores.append(_dot_nt(qs, kcat) + bias2 + _item_penalty(t, nct, s, subs, blk))
                vcats.append(_key_rows(vp, vc, vn, s, r, subs, halo, blk, d).astype(BF16))
            ms = [jnp.max(s, axis=-1, keepdims=True) for s in scores]
            if has_sink:
                sk = s_ref[...]
                ms = [jnp.maximum(m, sk) for m in ms]
            ps = [jnp.exp(s - m) for s, m in zip(scores, ms)]
            dens = [jnp.sum(p, axis=-1, keepdims=True) for p in ps]
            if has_sink:
                dens = [den + jnp.exp(sk - m) for den, m in zip(dens, ms)]
            pns = [(p * (1.0 / den)).astype(BF16) for p, den in zip(ps, dens)]
            lses = [m + jnp.log(den) for m, den in zip(ms, dens)]
            for j in range(items):
                s, r = _item_of(j, r0, per_group)
                o2 = jnp.dot(pns[j], vcats[j], preferred_element_type=F32)
                _set_span_rows(o_ref, s * halo, r, blk, d, jnp.where(left, o2[:blk], o2[blk:]))
                _set_span_rows(l_ref, s * halo, r, blk, d, jnp.where(left, lses[j][:blk], lses[j][blk:]))

        _for_groups(group, groups, per_group)

    cur, prev, nxt = _attn_specs(seq, blk, d, lambda t: t)
    v_cur, v_prev, v_nxt = _attn_specs(seq, blk, d, lambda t: t, v_col)
    in_specs = [cur, prev, cur, nxt, v_prev, v_cur, v_nxt, pl.BlockSpec((2 * blk, 3 * blk), lambda hp, t: (hp, 0))]
    args = [q, k, k, k, v, v, v, bias]
    if has_sink:
        in_specs.append(pl.BlockSpec((2 * blk, 1), lambda hp, t: (hp, 0)))
        args.append(sink)
    return pl.pallas_call(
        body, out_shape=[jax.ShapeDtypeStruct((seq, WIDTH), F32)] * 2, grid=(N_PAIRS, nct),
        in_specs=in_specs, out_specs=[cur, cur], name=name,
        compiler_params=_params("parallel", "parallel"))(*args)


def _attn_bwd(q, k, v, do, lse, delta, bias, sink, blk, d, name, v_col=0):
    seq = q.shape[0]
    subs, per_group, halo = _attn_geometry(blk, d)
    items, chunk, groups = subs * per_group, subs * halo, d // per_group
    nct = seq // chunk
    has_sink = sink is not None
    n_in = 12 if has_sink else 11
    scale = HEAD_DIM ** -0.5

    def body(*refs):
        q_ref, kp, kc, kn, vp, vc, vn, do_ref, l_ref, d_ref, b_ref = refs[:11]
        s_ref = refs[11] if has_sink else None
        dq_ref, dk_ref, dv_ref, db_ref = refs[n_in:n_in + 4]
        ds_ref = refs[n_in + 4] if has_sink else None
        wk, wv = refs[-2], refs[-1]
        t = pl.program_id(1)

        @pl.when(t == 0)
        def _():
            wk[...] = jnp.zeros_like(wk)
            wv[...] = jnp.zeros_like(wv)
            db_ref[...] = jnp.zeros_like(db_ref)
            if has_sink:
                ds_ref[...] = jnp.zeros_like(ds_ref)

        @pl.when(t > 0)
        def _():
            for w in (wk, wv):
                keep = w[chunk:2 * chunk + halo]
                w[0:chunk + halo] = keep
                w[chunk + halo:2 * chunk + halo] = jnp.zeros((chunk, LANES), F32)

        @pl.when(t < nct)
        def _():
            lane = lax.broadcasted_iota(jnp.int32, (1, LANES), 1)
            left = lane < HEAD_DIM
            bias2 = b_ref[...]

            def group(r0):
                qss, doss, kcats, scores, dps, lcols, dcols = [], [], [], [], [], [], []
                for j in range(items):
                    s, r = _item_of(j, r0, per_group)
                    qs = _stack_pair(_span_rows(q_ref, s * halo, r, blk, d) * scale, left)
                    dos = _stack_pair(_span_rows(do_ref, s * halo, r, blk, d), left)
                    kcat = _key_rows(kp, kc, kn, s, r, subs, halo, blk, d).astype(BF16)
                    vcat = _key_rows(vp, vc, vn, s, r, subs, halo, blk, d).astype(BF16)
                    l2, d2 = _span_rows(l_ref, s * halo, r, blk, d), _span_rows(d_ref, s * halo, r, blk, d)
                    lcols.append(jnp.concatenate([jnp.max(jnp.where(left, l2, NEG_INF), axis=-1, keepdims=True),
                                                  jnp.max(jnp.where(left, NEG_INF, l2), axis=-1, keepdims=True)], axis=0))
                    dcols.append(jnp.concatenate([jnp.sum(jnp.where(lane == 0, d2, 0.0), axis=-1, keepdims=True),
                                                  jnp.sum(jnp.where(lane == HEAD_DIM, d2, 0.0), axis=-1, keepdims=True)],
                                                 axis=0))
                    scores.append(_dot_nt(qs, kcat) + bias2 + _item_penalty(t, nct, s, subs, blk))
                    dps.append(_dot_nt(dos, vcat))
                    qss.append(qs)
                    doss.append(dos)
                    kcats.append(kcat)
                ps = [jnp.exp(s - lc) for s, lc in zip(scores, lcols)]
                dss = [p * (dp - dc) for p, dp, dc in zip(ps, dps, dcols)]
                db_ref[...] += functools.reduce(lambda a, b: a + b, dss)
                if has_sink:
                    sk = s_ref[...]
                    ds_ref[...] -= functools.reduce(lambda a, b: a + b, [dc * jnp.exp(sk - lc) for dc, lc in zip(dcols, lcols)])
                dsbs = [ds.astype(BF16) for ds in dss]
                for j in range(items):
                    s, r = _item_of(j, r0, per_group)
                    dq2 = jnp.dot(dsbs[j], kcats[j], preferred_element_type=F32) * scale
                    _set_span_rows(dq_ref, s * halo, r, blk, d, jnp.where(left, dq2[:blk], dq2[blk:]))
                news = [(_dot_tn(dsbs[j], qss[j]), _dot_tn(ps[j].astype(BF16), doss[j])) for j in range(items)]
                for which, w in enumerate((wk, wv)):
                    for jr in range(per_group):
                        for sp in range(-1, subs + 1):
                            parts = [news[s * per_group + jr][which][(sp - s + 1) * blk:(sp - s + 2) * blk]
                                     for s in range(subs) if 0 <= sp - s + 1 < 3]
                            _set_span_rows(w, chunk + sp * halo, r0 + jr, blk, d,
                                           functools.reduce(lambda x, y: x + y, parts), add=True)

            _for_groups(group, groups, per_group)

        dk_ref[...] = wk[0:chunk]
        dv_ref[...] = wv[0:chunk]

    cur, prev, nxt = _attn_specs(seq, blk, d, lambda t: jnp.minimum(t, nct - 1))
    v_cur, v_prev, v_nxt = _attn_specs(seq, blk, d, lambda t: jnp.minimum(t, nct - 1), v_col)
    lag = pl.BlockSpec((chunk, LANES), lambda hp, t: (jnp.maximum(t - 1, 0), hp))
    band = pl.BlockSpec((2 * blk, 3 * blk), lambda hp, t: (hp, 0))
    col = pl.BlockSpec((2 * blk, 1), lambda hp, t: (hp, 0))
    in_specs = [cur, prev, cur, nxt, v_prev, v_cur, v_nxt, cur, cur, cur, band]
    args = [q, k, k, k, v, v, v, do, lse, delta, bias]
    out_shape = [jax.ShapeDtypeStruct((seq, WIDTH), F32)] * 3 + [jax.ShapeDtypeStruct((N_HEADS * blk, 3 * blk), F32)]
    out_specs = [cur, lag, lag, band]
    if has_sink:
        in_specs.append(col)
        args.append(sink)
        out_shape.append(jax.ShapeDtypeStruct((N_HEADS * blk, 1), F32))
        out_specs.append(col)
    window = pltpu.VMEM((2 * chunk + halo, LANES), F32)
    return pl.pallas_call(
        body, out_shape=out_shape, grid=(N_PAIRS, nct + 1), in_specs=in_specs, out_specs=out_specs,
        scratch_shapes=[window, window], name=name,
        compiler_params=_params("arbitrary", "arbitrary"))(*args)


def _combine_patterns(outs, lses):
    def combine(*tiles):
        os_, ls = tiles[:len(outs)], tiles[len(outs):]
        m = functools.reduce(jnp.maximum, ls)
        es = [jnp.exp(l - m) for l in ls]
        den = functools.reduce(lambda a, b: a + b, es)
        num = functools.reduce(lambda a, b: a + b, [e * o for e, o in zip(es, os_)])
        return num / den, m + jnp.log(den)

    return _ew(combine, [*outs, *lses], [F32, F32], "combine_a")


def _row_dots(dy, y, name):
    return _ew(lambda dy_, y_: (_seg_sum(dy_ * y_),), [dy, y], [F32], name)[0]


def _tile_gain(g, reps):
    return jnp.tile(g[None, :], (1, reps))


def _local_step(x, p, target, w_in_of, rest_of, small):
    rel_table = small["rel_table"]
    buckets_a = [_band_buckets(blk, d) for blk, d in DILATED]
    buckets_b = _band_buckets(BLK_B, 1)
    bias_a = [_bias_tiles(rel_table, bk, 0, "bias_a").reshape(N_HEADS * bk.shape[0], -1) for bk in buckets_a]
    bias_b = _bias_tiles(rel_table, buckets_b, N_HEADS, "bias_b").reshape(N_HEADS * BLK_B, -1)

    saved = []
    for l in range(DEPTH):
        g_mix, g_ffn, g_ple = (small[n][l][None, :] for n in ("norm_mix_g", "norm_ffn_g", "norm_ple_g"))
        gqa, gka, gqb = (_tile_gain(small[n][l], N_HEADS) for n in ("qnorm_a_g", "knorm_a_g", "qnorm_b_g"))
        gkb = _tile_gain(small["knorm_b_g"][l], N_KV_B)
        sink = jnp.repeat(small["sink_b"][l], BLK_B)[:, None]

        h = _rms_fwd(x, g_mix, "rms_mix")
        w_in = w_in_of(l, (h, bias_a, bias_b))
        proj = _mm(h, w_in, "nt", F32, "mm_in")
        qa, ka, qb, kb, vb = _qknorm_fwd(proj, gqa, gka, gqb, gkb)
        outs, lses = [], []
        for (blk, d), bias in zip(DILATED, bias_a):
            o, ls = _attn_fwd(qa, ka, proj, bias, None, blk, d, f"attn_a{d}_fwd", v_col=OFF_VA)
            outs.append(o)
            lses.append(ls)
        ya, lse_a = _combine_patterns(outs, lses)
        yb, lse_b = _attn_fwd(qb, kb, vb, bias_b, sink, BLK_B, 1, "attn_b_fwd")
        w = dict(rest_of(l, yb), w_in=w_in)
        def gate(products, extra):
            (ca_, cb_), (ga_, gb_) = products, extra
            return _sigmoid(ga_) * ca_ + _sigmoid(gb_) * cb_, ca_, cb_

        merged, ca, cb = _mm_fused([(ya, w["w_branch_a"], "nn"), (yb, w["w_branch_b"], "nn")],
                                   [(proj, OFF_GA), (proj, OFF_GB)], gate, [BF16, BF16, BF16], "mm_branches_gate")
        x1 = _mm(merged, w["w_out"], "nn", F32, "mm_out", res=x)

        h2 = _rms_fwd(x1, g_ffn, "rms_ffn")

        def swiglu(products, extra):
            a_, u_ = products
            return (a_ * _sigmoid(a_)) * u_, a_, u_

        hid, a, u = _mm_fused([(h2, w["w_ffn_gate"], "nt"), (h2, w["w_ffn_up"], "nt")], [], swiglu,
                              [BF16, BF16, BF16], "mm_ffn_gate_up")
        x2 = _mm(hid, w["w_ffn_down"], "nn", F32, "mm_ffn_down", res=x1)

        h3 = _rms_fwd(x2, g_ple, "rms_ple")

        def ple(products, extra):
            z_, e_ = products
            return extra[0] + _sigmoid(z_) * e_, z_, e_

        x3, z, e = _mm_fused([(h3, w["w_ple_gate"], "nn"), (p[l], w["w_ple_proj"], "nn")], [(x2, 0)], ple,
                             [F32, BF16, BF16], "mm_ple")
        saved.append(dict(w=w, x0=x, h=h, proj=proj, qa=qa, ka=ka, qb=qb, kb=kb, vb=vb, ya=ya, lse_a=lse_a,
                          yb=yb, lse_b=lse_b, ca=ca, cb=cb, merged=merged, x1=x1, h2=h2, a=a, u=u, hid=hid,
                          x2=x2, h3=h3, z=z, e=e))
        x = x3

    dx, loss_acc = _loss_grad(x, target)
    loss = loss_acc[0, 0]

    gbig = [{} for _ in range(DEPTH)]
    marks = [{} for _ in range(DEPTH)]
    gsmall = {n: [None] * DEPTH for n in SMALL if n != "rel_table"}
    dbias_a = [[] for _ in DILATED]
    dbias_b = []

    for l in reversed(range(DEPTH)):
        sv = saved[l]
        w = sv["w"]
        g_mix, g_ffn, g_ple = (small[n][l][None, :] for n in ("norm_mix_g", "norm_ffn_g", "norm_ple_g"))
        gqa, gka, gqb = (_tile_gain(small[n][l], N_HEADS) for n in ("qnorm_a_g", "knorm_a_g", "qnorm_b_g"))
        gkb = _tile_gain(small["knorm_b_g"][l], N_KV_B)
        sink = jnp.repeat(small["sink_b"][l], BLK_B)[:, None]

        def ple_bwd(dx_, z_, e_):
            s = _sigmoid(z_.astype(F32))
            return dx_ * s, dx_ * e_.astype(F32) * (s * (1.0 - s))

        de, dz = _ew(ple_bwd, [dx, sv["z"], sv["e"]], [BF16, BF16], "ple_bwd")
        gbig[l]["w_ple_proj"] = _mm(p[l], de, "tn", BF16, "mm_d_ple_proj")
        gbig[l]["w_ple_gate"] = _mm(sv["h3"], dz, "tn", BF16, "mm_d_ple_gate")
        dh3 = _mm(dz, w["w_ple_gate"], "nt", F32, "mm_dh3")
        dx, dxb, gsmall["norm_ple_g"][l] = _rms_bwd(sv["x2"], g_ple, dh3, dx, "rms_ple_bwd")

        gbig[l]["w_ffn_down"] = _mm(sv["hid"], dxb, "tn", BF16, "mm_d_ffn_down")

        def swiglu_bwd(products, extra):
            dh_, a_, u_ = products[0], extra[0].astype(F32), extra[1].astype(F32)
            s = _sigmoid(a_)
            return dh_ * u_ * (s * (1.0 + a_ * (1.0 - s))), dh_ * (a_ * s)

        da, du = _mm_fused([(dxb, w["w_ffn_down"], "nt")], [(sv["a"], 0), (sv["u"], 0)], swiglu_bwd, [BF16, BF16],
                           "mm_dhid_swiglu_bwd")
        gbig[l]["w_ffn_gate"] = _mm(da, sv["h2"], "tn", BF16, "mm_d_ffn_gate")
        gbig[l]["w_ffn_up"] = _mm(du, sv["h2"], "tn", BF16, "mm_d_ffn_up")
        dh2 = _mm(da, w["w_ffn_gate"], "nn", F32, "mm_dh2_gate")
        dh2 = _mm(du, w["w_ffn_up"], "nn", F32, "mm_dh2_up", res=dh2)
        dx, dxb, gsmall["norm_ffn_g"][l] = _rms_bwd(sv["x1"], g_ffn, dh2, dx, "rms_ffn_bwd")

        gbig[l]["w_out"] = _mm(sv["merged"], dxb, "tn", BF16, "mm_d_out")

        def gate_bwd(products, extra):
            dm_, ca_, cb_ = products[0], extra[0].astype(F32), extra[1].astype(F32)
            sa, sb = _sigmoid(extra[2]), _sigmoid(extra[3])
            return dm_ * sa, dm_ * sb, dm_ * ca_ * (sa * (1.0 - sa)), dm_ * cb_ * (sb * (1.0 - sb))

        dca, dcb, dga, dgb = _mm_fused(
            [(dxb, w["w_out"], "nt")], [(sv["ca"], 0), (sv["cb"], 0), (sv["proj"], OFF_GA), (sv["proj"], OFF_GB)],
            gate_bwd, [BF16, BF16, BF16, BF16], "mm_dmerged_gate_bwd")
        gbig[l]["w_branch_a"] = _mm(sv["ya"], dca, "tn", BF16, "mm_d_branch_a")
        gbig[l]["w_branch_b"] = _mm(sv["yb"], dcb, "tn", BF16, "mm_d_branch_b")
        dya = _mm(dca, w["w_branch_a"], "nt", F32, "mm_dya")
        dyb = _mm(dcb, w["w_branch_b"], "nt", F32, "mm_dyb")

        delta_a = _row_dots(dya, sv["ya"], "attn_a_row_dots")
        delta_b = _row_dots(dyb, sv["yb"], "attn_b_row_dots")

        dqa, dka, dva = [], [], []
        for (blk, d), bias, bk in zip(DILATED, bias_a, buckets_a):
            dq_, dk_, dv_, db_ = _attn_bwd(sv["qa"], sv["ka"], sv["proj"], dya, sv["lse_a"], delta_a, bias, None, blk, d,
                                           f"attn_a{d}_bwd", v_col=OFF_VA)
            dqa.append(dq_)
            dka.append(dk_)
            dva.append(dv_)
            dbias_a[len(dqa) - 1].append(db_)
        dqb, dkb, dvb, db_, dsink = _attn_bwd(sv["qb"], sv["kb"], sv["vb"], dyb, sv["lse_b"], delta_b, bias_b, sink,
                                              BLK_B, 1, "attn_b_bwd")
        dbias_b.append(db_)
        gsmall["sink_b"][l] = dsink.reshape(N_HEADS, BLK_B).sum(axis=1)

        dproj, pqa, pka, pqb, pkb = _qknorm_bwd(sv["proj"], gqa, gka, gqb, gkb, dqa, dka, dva, dqb, dkb, dvb, dga, dgb)
        marks[l]["attn_bwd_done"] = dproj
        gsmall["qnorm_a_g"][l] = pqa.reshape(N_HEADS, HEAD_DIM).sum(0)
        gsmall["knorm_a_g"][l] = pka.reshape(N_HEADS, HEAD_DIM).sum(0)
        gsmall["qnorm_b_g"][l] = pqb.reshape(N_HEADS, HEAD_DIM).sum(0)
        gsmall["knorm_b_g"][l] = pkb.reshape(N_KV_B, HEAD_DIM).sum(0)
        gbig[l]["w_in"] = _mm(dproj, sv["h"], "tn", BF16, "mm_d_in")
        dh = _mm(dproj, w["w_in"], "nn", F32, "mm_dh")
        dx, _, gsmall["norm_mix_g"][l] = _rms_bwd(sv["x0"], g_mix, dh, dx, "rms_mix_bwd")
        gsmall["norm_mix_g"][l] = gsmall["norm_mix_g"][l][0]
        gsmall["norm_ffn_g"][l] = gsmall["norm_ffn_g"][l][0]
        gsmall["norm_ple_g"][l] = gsmall["norm_ple_g"][l][0]

    gsmall = {n: jnp.stack(v) for n, v in gsmall.items()}
    dtable_a = sum(_table_grad(sum(dbs).reshape(N_HEADS, blk, 3 * blk), bk, "table_grad_a")
                   for dbs, (blk, _), bk in zip(dbias_a, DILATED, buckets_a))
    dtable_b = _table_grad(sum(dbias_b).reshape(N_HEADS, BLK_B, 3 * BLK_B), buckets_b, "table_grad_b")
    gsmall["rel_table"] = jnp.concatenate([dtable_a, dtable_b], axis=0).T
    return loss, dx, gbig, gsmall, marks


def _place():
    return lax.axis_index("x"), lax.axis_index("y"), lax.axis_index("c")


def _flip(v, bit):
    return 1 - v if bit else v


CHIP_RELATIONS = ((0, 1), (1, 0), (1, 1))
ANY = pl.BlockSpec(memory_space=pl.ANY)


def _allgather_body(w_refs, out_refs, send_sems, recv_sems):
    x, y, c = _place()
    chips = [(_flip(x, a), _flip(y, b)) for a, b in CHIP_RELATIONS]

    def make(g):
        w_ref, out_ref = w_refs[g], out_refs[g]
        half = w_ref.shape[0] // 2

        def part(px, py, pc):
            return out_ref.at[2 * px + py, pl.ds(pc * half, half), :]

        def copy(k, block, to, src=None):
            return pltpu.make_async_remote_copy(
                src_ref=part(*block) if src is None else src, dst_ref=part(*block),
                send_sem=send_sems.at[7 * g + k], recv_sem=recv_sems.at[7 * g + k], device_id=to,
                device_id_type=MESH_ID)

        own = pltpu.make_async_remote_copy(
            src_ref=w_ref, dst_ref=out_ref.at[2 * x + y], send_sem=send_sems.at[7 * g + 6],
            recv_sem=recv_sems.at[7 * g + 6], device_id=(x, y, 1 - c), device_id_type=MESH_ID)
        first = [copy(k, (x, y, c), (*chip, c), src=w_ref.at[pl.ds(c * half, half), :]) for k, chip in enumerate(chips)]
        passed = [copy(3 + k, (*chip, c), (x, y, 1 - c)) for k, chip in enumerate(chips)]
        arrive = [copy(k, (*chip, c), (x, y, c)) for k, chip in enumerate(chips)]
        arrive2 = [copy(3 + k, (*chip, 1 - c), (x, y, c)) for k, chip in enumerate(chips)]
        return own, first, passed, arrive, arrive2

    made = [make(g) for g in range(len(w_refs))]
    for own, first, _, _, _ in made:
        own.start()
        for cp in first:
            cp.start()
    for _, _, passed, arrive, _ in made:
        for k in range(3):
            arrive[k].wait_recv()
            passed[k].start()
    for own, first, passed, _, arrive2 in made:
        for k in range(3):
            arrive2[k].wait_recv()
        own.wait_recv()
        for cp in first + passed + [own]:
            cp.wait_send()


def _sibling(x, y, c):
    return [(x, y, 1 - c)]


def _same_core_of_other_chips(x, y, c):
    return [(_flip(x, a), _flip(y, b), c) for a, b in CHIP_RELATIONS]


def _exchange(body, ins, out_types, n_sems, name, sequencer=None):
    n = len(ins)
    sems = (pltpu.SemaphoreType.DMA((n_sems,)), pltpu.SemaphoreType.DMA((n_sems,)))
    if sequencer is None:
        in_place = out_types is None
        out_shape = [jax.ShapeDtypeStruct(a.shape, a.dtype) for a in ins] if in_place else out_types

        def tc_body(*refs):
            body(refs[:n], refs[n:n + len(out_shape)], refs[-2], refs[-1])

        return list(pl.pallas_call(
            tc_body, out_shape=out_shape, in_specs=[ANY] * n, out_specs=[ANY] * len(out_shape),
            input_output_aliases={g: g for g in range(n)} if in_place else {}, scratch_shapes=list(sems), name=name)(*ins))

    collective_id, peers = sequencer
    hbm = pltpu.MemorySpace.HBM
    in_refs = [jax.new_ref(a, memory_space=hbm) for a in ins]
    out_refs = in_refs if out_types is None else [jax.empty_ref(t, memory_space=hbm) for t in out_types]

    @pl.kernel(mesh=plsc.ScalarSubcoreMesh(axis_name="sequencer", num_cores=1), name=name, scratch_types=sems,
               compiler_params=pltpu.CompilerParams(collective_id=collective_id))
    def launch(send_sems, recv_sems):
        barrier = pltpu.get_barrier_semaphore()
        devices = peers(*_place())
        for device in devices:
            pl.semaphore_signal(barrier, inc=1, device_id=device, device_id_type=MESH_ID)
        pl.semaphore_wait(barrier, len(devices))
        body(in_refs, out_refs, send_sems, recv_sems)

    launch()
    return [r[...] for r in out_refs]


def _allgather(shards, name, sequencer=None):
    out_types = [jax.ShapeDtypeStruct((N_CHIPS,) + s.shape, s.dtype) for s in shards]
    if sequencer is not None:
        sequencer = (sequencer, lambda x, y, c: _sibling(x, y, c) + _same_core_of_other_chips(x, y, c))
    return _exchange(_allgather_body, shards, out_types, 7 * len(shards), name, sequencer)


def _half_tile(half):
    return max(t for t in range(16, 1025, 16) if half % t == 0)


def _run_copies(cps):
    for cp in cps:
        cp.start()
    for cp in cps:
        cp.wait_recv()
    for cp in cps:
        cp.wait_send()


def _sibling_halves(gsends, name, sequencer=None):
    def body(g_refs, out_refs, send_sems, recv_sems):
        x, y, c = _place()
        cps = []
        for g, (g_ref, out_ref) in enumerate(zip(g_refs, out_refs)):
            half = g_ref.shape[1] // 2
            cps.append(pltpu.make_async_remote_copy(
                src_ref=g_ref.at[:, pl.ds((1 - c) * half, half), :], dst_ref=out_ref,
                send_sem=send_sems.at[g], recv_sem=recv_sems.at[g], device_id=(x, y, 1 - c), device_id_type=MESH_ID))
        _run_copies(cps)

    out_types = [jax.ShapeDtypeStruct((s.shape[0], s.shape[1] // 2, s.shape[2]), s.dtype) for s in gsends]
    return _exchange(body, gsends, out_types, len(gsends), name, sequencer and (sequencer, _sibling))


def _chip_sums(gsend, sib, place):
    n, rows, cols = gsend.shape
    half = rows // 2
    tm = _half_tile(half)
    nblk = half // tm

    def body(s_ref, g_ref, sib_ref, o_ref):
        o_ref[0] = (g_ref[0].astype(F32) + sib_ref[0].astype(F32)).astype(o_ref.dtype)

    grid_spec = pltpu.PrefetchScalarGridSpec(
        num_scalar_prefetch=1, grid=(n, nblk),
        in_specs=[pl.BlockSpec((1, tm, cols), lambda k, i, s: (jnp.bitwise_xor(s[0], k), s[1] * nblk + i, 0)),
                  pl.BlockSpec((1, tm, cols), lambda k, i, s: (jnp.bitwise_xor(s[0], k), i, 0))],
        out_specs=pl.BlockSpec((1, tm, cols), lambda k, i, s: (k, i, 0)))
    return pl.pallas_call(
        body, out_shape=jax.ShapeDtypeStruct((n, half, cols), BF16), grid_spec=grid_spec,
        name="rs_chip_sums", compiler_params=_params("parallel", "parallel"))(place, gsend, sib)


def _exchange_chip_sums(tsends, name, sequencer=None):
    def body(t_refs, out_refs, send_sems, recv_sems):
        x, y, c = _place()
        cps = []
        for g, (t_ref, out_ref) in enumerate(zip(t_refs, out_refs)):
            for k, device in enumerate(_same_core_of_other_chips(x, y, c)):
                cps.append(pltpu.make_async_remote_copy(
                    src_ref=t_ref.at[k + 1], dst_ref=out_ref.at[k], send_sem=send_sems.at[3 * g + k],
                    recv_sem=recv_sems.at[3 * g + k], device_id=device, device_id_type=MESH_ID))
        _run_copies(cps)

    out_types = [jax.ShapeDtypeStruct((3,) + s.shape[1:], s.dtype) for s in tsends]
    return _exchange(body, tsends, out_types, 3 * len(tsends), name,
                     sequencer and (sequencer, _same_core_of_other_chips))


def _final_sum(tsend, recv, place):
    n, half, cols = tsend.shape
    tm = _half_tile(half)
    nblk = half // tm

    def body(s_ref, t_ref, r_ref, o_ref):
        o_ref[...] = ((t_ref[0].astype(F32) + r_ref[0].astype(F32)) + r_ref[1].astype(F32)) + r_ref[2].astype(F32)

    grid_spec = pltpu.PrefetchScalarGridSpec(
        num_scalar_prefetch=1, grid=(nblk,),
        in_specs=[pl.BlockSpec((1, tm, cols), lambda i, s: (0, i, 0)), pl.BlockSpec((n - 1, tm, cols), lambda i, s: (0, i, 0))],
        out_specs=pl.BlockSpec((tm, cols), lambda i, s: (s[1] * nblk + i, 0)))
    return pl.pallas_call(
        body, out_shape=jax.ShapeDtypeStruct((2 * half, cols), F32), grid_spec=grid_spec, name="rs_final_sum",
        compiler_params=_params("parallel"))(place, tsend, recv)


def _join_halves(gfulls, name, sequencer=None):
    def body(g_refs, out_refs, send_sems, recv_sems):
        x, y, c = _place()
        n = len(g_refs)

        def copy(g, pc):
            half = g_refs[g].shape[0] // 2
            return pltpu.make_async_remote_copy(
                src_ref=g_refs[g].at[pl.ds(pc * half, half), :], dst_ref=out_refs[g].at[pl.ds(pc * half, half), :],
                send_sem=send_sems.at[g], recv_sem=recv_sems.at[g], device_id=(x, y, 1 - c), device_id_type=MESH_ID)

        mine = [copy(g, c) for g in range(n)]
        for cp in mine:
            cp.start()
        for g in range(n):
            copy(g, 1 - c).wait_recv()
        for cp in mine:
            cp.wait_send()

    return _exchange(body, gfulls, None, len(gfulls), name, sequencer and (sequencer, _sibling))


def _allreduce_small(v):
    rows, cols = v.shape

    def body(v_ref, out_ref, buf, send_sems, recv_sems):
        x, y, c = _place()
        cps = []
        for k in range(1, 8):
            peer = (_flip(x, (k >> 2) & 1), _flip(y, (k >> 1) & 1), _flip(c, k & 1))
            cps.append(pltpu.make_async_remote_copy(
                src_ref=v_ref, dst_ref=buf.at[k - 1], send_sem=send_sems.at[k - 1], recv_sem=recv_sems.at[k - 1],
                device_id=peer, device_id_type=MESH_ID))
        for cp in cps:
            cp.start()
        for cp in cps:
            cp.wait_recv()
        for cp in cps:
            cp.wait_send()
        t0 = v_ref[...] + buf[0]
        t1 = buf[1] + buf[2]
        t2 = buf[3] + buf[4]
        t3 = buf[5] + buf[6]
        out_ref[...] = (t0 + t1) + (t2 + t3)

    vm = pl.BlockSpec(memory_space=pltpu.VMEM)
    return pl.pallas_call(
        body, out_shape=jax.ShapeDtypeStruct((rows, cols), F32), in_specs=[vm], out_specs=vm,
        scratch_shapes=[pltpu.VMEM((7, rows, cols), F32), pltpu.SemaphoreType.DMA((7,)), pltpu.SemaphoreType.DMA((7,))],
        name="allreduce_small")(v)


BIG_INFO = {n: (shape, ax) for n, shape, ax in BIG}
GROUPS = (("w_in",), ("w_ffn_gate", "w_ffn_up", "w_ffn_down", "w_out", "w_ple_gate"),
          ("w_branch_a", "w_branch_b", "w_ple_proj"))


def _shard_shape(name):
    (k, m), ax = BIG_INFO[name]
    return (k // N_CHIPS, m) if ax == 0 else (k, m // N_CHIPS)


def _group_rows(group):
    offs, off = {}, 0
    for n in group:
        offs[n] = off
        off += _shard_shape(n)[0]
    return offs, off


def _pack_groups(shards, layer, dtype):
    return [jnp.concatenate([shards[n][layer].astype(dtype) for n in group], axis=0) for group in GROUPS]


def _unpack_full(gathered, groups):
    out = {}
    for group, arr in zip(groups, gathered):
        offs, _ = _group_rows(group)
        for n in group:
            rows, cols = _shard_shape(n)
            (k, m), ax = BIG_INFO[n]
            slab = arr[:, offs[n]:offs[n] + rows]
            out[n] = slab.reshape(k, m) if ax == 0 else jnp.transpose(slab, (1, 0, 2)).reshape(k, m)
    return out


def _pack_grads(gfull):
    out = []
    for group in GROUPS:
        parts = []
        for n in group:
            rows, cols = _shard_shape(n)
            ax = BIG_INFO[n][1]
            slab = (gfull[n].reshape(N_CHIPS, rows, cols) if ax == 0
                    else jnp.transpose(gfull[n].reshape(rows, N_CHIPS, cols), (1, 0, 2)))
            parts.append(slab)
        out.append(jnp.concatenate(parts, axis=1))
    return out


def _after(values, mark):
    values, _ = lax.optimization_barrier((values, mark))
    return values


def _reduce_scatter_begin(gsends, place, tag, ids):
    sibs = _sibling_halves(gsends, "rs_sibling_halves_" + tag, ids[0])
    tsends = [_chip_sums(g, s, place) for g, s in zip(gsends, sibs)]
    return tsends, _exchange_chip_sums(tsends, "rs_exchange_" + tag, ids[1])


def _reduce_scatter_finish(begun, place, tag, ids, hold):
    tsends, recvs = begun
    recvs = _after(recvs, hold)
    return _join_halves([_final_sum(t, r, place) for t, r in zip(tsends, recvs)], "rs_join_halves_" + tag, ids[2])


SMALL_SHAPES = {"rel_table": (NUM_BUCKETS, 2 * N_HEADS), "norm_mix_g": (DEPTH, D_MODEL), "qnorm_a_g": (DEPTH, HEAD_DIM),
                "knorm_a_g": (DEPTH, HEAD_DIM), "qnorm_b_g": (DEPTH, HEAD_DIM), "knorm_b_g": (DEPTH, HEAD_DIM),
                "sink_b": (DEPTH, N_HEADS), "norm_ffn_g": (DEPTH, D_MODEL), "norm_ple_g": (DEPTH, D_MODEL)}


def _pack_small(vals, last=None):
    flat = jnp.concatenate([vals[n].astype(F32).reshape(-1) for n in SMALL])
    tail = jnp.zeros((SMALL_ROWS * LANES - flat.shape[0],), F32)
    if last is not None:
        tail = tail.at[-1].set(last)
    return jnp.concatenate([flat, tail]).reshape(SMALL_ROWS, LANES)


def _unpack_small(packed):
    flat, out, off = packed.reshape(-1), {}, 0
    for n in SMALL:
        size = math.prod(SMALL_SHAPES[n])
        out[n] = flat[off:off + size].reshape(SMALL_SHAPES[n])
        off += size
    return out


def _adamw(w, gs, g_row, m, v, name):
    c1 = 1.0 - ADAM_B1 ** ADAM_STEP
    c2 = 1.0 - ADAM_B2 ** ADAM_STEP
    total, width = w.shape
    n_layers = len(gs)
    per = total // n_layers
    tm = max(t for t in range(8, 513, 8) if per % t == 0 and g_row % t == 0)
    nblk = per // tm

    def body(*refs):
        w_ref, g_refs = refs[0], refs[1:1 + n_layers]
        m_ref, v_ref, og, od, om, ov = refs[1 + n_layers:]
        layer = pl.program_id(0) // nblk
        g = g_refs[0][...]
        for l in range(1, n_layers):
            g = jnp.where(layer == l, g_refs[l][...], g)
        m_new = ADAM_B1 * m_ref[...] + (1.0 - ADAM_B1) * g
        v_new = ADAM_B2 * v_ref[...] + (1.0 - ADAM_B2) * (g * g)
        og[...] = g
        od[...] = -ADAM_LR * ((m_new / c1) / (jnp.sqrt(v_new / c2) + ADAM_EPS) + ADAM_WD * w_ref[...])
        om[...] = m_new
        ov[...] = v_new

    row = pl.BlockSpec((tm, width), lambda i: (i, 0))
    g_specs = [pl.BlockSpec((tm, width), lambda i, l=l: (g_row // tm + jnp.clip(i - l * nblk, 0, nblk - 1), 0))
               for l in range(n_layers)]
    return pl.pallas_call(
        body, out_shape=[jax.ShapeDtypeStruct((total, width), F32)] * 4, grid=(total // tm,),
        in_specs=[row] + g_specs + [row, row], out_specs=[row] * 4, name=name,
        compiler_params=_params("parallel"))(w, *gs, m, v)


def kernel(x, p, rel_table, norm_mix_g, w_in, qnorm_a_g, knorm_a_g, qnorm_b_g, knorm_b_g, sink_b, w_branch_a, w_branch_b, w_out, norm_ffn_g, w_ffn_gate, w_ffn_up, w_ffn_down, norm_ple_g, w_ple_gate, w_ple_proj, loss_target, m_rel_table, m_norm_mix_g, m_w_in, m_qnorm_a_g, m_knorm_a_g, m_qnorm_b_g, m_knorm_b_g, m_sink_b, m_w_branch_a, m_w_branch_b, m_w_out, m_norm_ffn_g, m_w_ffn_gate, m_w_ffn_up, m_w_ffn_down, m_norm_ple_g, m_w_ple_gate, m_w_ple_proj, v_rel_table, v_norm_mix_g, v_w_in, v_qnorm_a_g, v_knorm_a_g, v_qnorm_b_g, v_knorm_b_g, v_sink_b, v_w_branch_a, v_w_branch_b, v_w_out, v_norm_ffn_g, v_w_ffn_gate, v_w_ffn_up, v_w_ffn_down, v_norm_ple_g, v_w_ple_gate, v_w_ple_proj):
    given = dict(locals())

    def held(name, a):
        return jnp.swapaxes(a, 1, 2) if name in TRANSPOSED else a

    weights = {n: held(n, given[n]) for n in WEIGHTS}
    moments_m = {n: held(n, given["m_" + n]) for n in WEIGHTS}
    moments_v = {n: held(n, given["v_" + n]) for n in WEIGHTS}
    xi, yi, ci = _place()
    place = jnp.stack([2 * xi + yi, ci]).astype(jnp.int32)

    shards = [_pack_groups(weights, l, BF16) for l in range(DEPTH)]
    w_in0 = _allgather(shards[0][:1], "allgather_w_in_layer0", sequencer=9)
    rest0 = _allgather(_after(shards[0][1:], w_in0), "allgather_rest_layer0", sequencer=1)
    gathered = [w_in0 + rest0, None]
    small = {n: weights[n] for n in SMALL}

    def w_in_of(l, mark):
        return _unpack_full(_after(gathered[l][:1], (shards[1], mark) if l == 0 else mark), GROUPS[:1])["w_in"]

    def rest_of(l, mark):
        if l == 0:
            gathered[1] = _allgather(_after(shards[1], mark), "allgather_layer1", sequencer=2)
        return _unpack_full(_after(gathered[l][1:], mark), GROUPS[1:])

    loss, dx, gbig, gsmall, marks = _local_step(x[0], p[:, 0], loss_target[0], w_in_of, rest_of, small)

    gsends = [_pack_grads(gbig[l]) for l in range(DEPTH)]
    stages = {"layer1": (gsends[1], (3, 4, 5)), "rest_layer0": (gsends[0][1:], (6, 7, 8)),
              "w_in_layer0": (gsends[0][:1], (10, 11, 12))}
    begun = {tag: _reduce_scatter_begin(g, place, tag, ids) for tag, (g, ids) in stages.items()}

    def finish(tag, hold):
        return _reduce_scatter_finish(begun[tag], place, tag, stages[tag][1], hold)

    red1 = finish("layer1", marks[0]["attn_bwd_done"])
    rest0 = finish("rest_layer0", marks[0]["attn_bwd_done"])

    grads, delta, new_m, new_v = {}, {}, {}, {}

    def update(group, reduced):
        offs, _ = _group_rows(group)
        for n in group:
            shape = weights[n].shape
            two_d = lambda a: a.reshape(shape[0] * shape[1], shape[2])
            outs = _adamw(two_d(weights[n]), reduced, offs[n], two_d(moments_m[n]), two_d(moments_v[n]), "adamw_" + n)
            grads[n], delta[n], new_m[n], new_v[n] = (held(n, o.reshape(shape)) for o in outs)

    for gi in (1, 2):
        update(GROUPS[gi], _after([rest0[gi - 1], red1[gi]], begun["w_in_layer0"][0]))
    small_grads = _allreduce_small(_pack_small(gsmall, last=loss))
    g_, d_, m_, v_ = _adamw(_pack_small(weights), [small_grads], 0, _pack_small(moments_m), _pack_small(moments_v),
                            "adamw_small")
    grads.update(_unpack_small(g_))
    delta.update(_unpack_small(d_))
    new_m.update(_unpack_small(m_))
    new_v.update(_unpack_small(v_))
    others_done = [dx, d_] + [delta[n] for gi in (1, 2) for n in GROUPS[gi]]
    update(GROUPS[0], [finish("w_in_layer0", others_done)[0], red1[0]])

    return (small_grads[-1, -1], dx[None], *[grads[n] for n in WEIGHTS], *[delta[n] for n in WEIGHTS],
            *[new_m[n] for n in WEIGHTS], *[new_v[n] for n in WEIGHTS])
```

```python
import functools
import math

import jax
import jax.numpy as jnp
from jax import lax
from jax.experimental import pallas as pl
from jax.experimental.pallas import tpu as pltpu
from jax.experimental.pallas import tpu_sc as plsc

F32 = jnp.float32
BF16 = jnp.bfloat16
MESH_ID = pl.DeviceIdType.MESH

D_MODEL = 1024
DEPTH = 2
HEAD_DIM = 64
N_HEADS = 8
WIDTH = N_HEADS * HEAD_DIM
N_PAIRS = 4
ITEMS = 16
MAX_CHUNK = 1024
N_KV_B = 2
PLE_DIM = 256
D_FF = 2816
D_IN = 4352
OFF_QA, OFF_KA, OFF_VA, OFF_QB, OFF_KB, OFF_VB, OFF_GA, OFF_GB = 0, 512, 1024, 1536, 2048, 2176, 2304, 3328
DILATED = ((64, 1), (64, 4), (64, 16))
BLK_B = 128
NUM_BUCKETS = 32
MAX_DISTANCE = 1024
RMS_EPS = 1e-6
NEG_INF = -1e30
LANES = 128
ROW_TILE = 256
VMEM_LIMIT = 48 * 1024 * 1024

ADAM_LR, ADAM_B1, ADAM_B2, ADAM_EPS, ADAM_WD, ADAM_STEP = 0.001, 0.9, 0.999, 1e-08, 0.01, 10

TRANSPOSED = ("w_in", "w_ffn_gate", "w_ffn_up")
BIG = (
    ("w_in", (D_IN, D_MODEL), 0),
    ("w_branch_a", (WIDTH, D_MODEL), 1),
    ("w_branch_b", (WIDTH, D_MODEL), 1),
    ("w_out", (D_MODEL, D_MODEL), 0),
    ("w_ffn_gate", (D_FF, D_MODEL), 0),
    ("w_ffn_up", (D_FF, D_MODEL), 0),
    ("w_ffn_down", (D_FF, D_MODEL), 0),
    ("w_ple_gate", (D_MODEL, D_MODEL), 0),
    ("w_ple_proj", (PLE_DIM, D_MODEL), 1),
)
SMALL = ("rel_table", "norm_mix_g", "qnorm_a_g", "knorm_a_g", "qnorm_b_g", "knorm_b_g", "sink_b",
         "norm_ffn_g", "norm_ple_g")
WEIGHTS = ("rel_table", "norm_mix_g", "w_in", "qnorm_a_g", "knorm_a_g", "qnorm_b_g", "knorm_b_g", "sink_b",
           "w_branch_a", "w_branch_b", "w_out", "norm_ffn_g", "w_ffn_gate", "w_ffn_up", "w_ffn_down",
           "norm_ple_g", "w_ple_gate", "w_ple_proj")
N_CHIPS = 4
SMALL_ROWS = 64


def _params(*sem):
    return pltpu.CompilerParams(dimension_semantics=sem, vmem_limit_bytes=VMEM_LIMIT)


MM_VMEM_BUDGET = 40 * 1024 * 1024
STEP_OVERHEAD_S = 0.4e-6
TILE_DMA_BYTES_PER_S = 1.5e12


def _mm_dims(a, b, mode):
    if mode == "nn":
        return a.shape[0], b.shape[1], a.shape[1]
    if mode == "nt":
        return a.shape[0], b.shape[0], a.shape[1]
    return a.shape[1], b.shape[1], a.shape[0]


def _mm_tiles(m, n, pairs, tile_bytes, col_offsets):
    best = None
    for tm in (t for t in range(LANES, m + 1, LANES) if m % t == 0):
        for tn in (t for t in range(LANES, n + 1, LANES) if n % t == 0 and all(o % t == 0 for o in col_offsets)):
            io = sum(tm * k * ab + tn * k * bb for k, ab, bb in pairs) + tm * tn * sum(tile_bytes)
            casts = sum((tm * k * 2 if ab == 4 else 0) + (tn * k * 2 if bb == 4 else 0) for k, ab, bb in pairs)
            if 2 * io + len(pairs) * tm * tn * 4 + casts > MM_VMEM_BUDGET:
                continue
            cost = (m // tm) * (n // tn) * STEP_OVERHEAD_S + io / TILE_DMA_BYTES_PER_S
            if best is None or (cost, -tm) < best[0]:
                best = ((cost, -tm), tm, tn)
    return best[1], best[2]


def _mm_fused(pairs, extras, epilogue, out_dtypes, name):
    m, n, _ = _mm_dims(*pairs[0])
    assert all(_mm_dims(*p)[:2] == (m, n) for p in pairs)
    tm, tn = _mm_tiles(
        m, n, [(_mm_dims(a, b, mode)[2], a.dtype.itemsize, b.dtype.itemsize) for a, b, mode in pairs],
        [e.dtype.itemsize for e, _ in extras] + [jnp.dtype(d).itemsize for d in out_dtypes], [off for _, off in extras])
    dims = {"nn": (((1,), (0,)), ((), ())), "nt": (((1,), (1,)), ((), ())), "tn": (((0,), (0,)), ((), ()))}
    in_specs, args = [], []
    for a, b, mode in pairs:
        k = _mm_dims(a, b, mode)[2]
        in_specs.append(pl.BlockSpec((k, tm), lambda i, j: (0, i)) if mode == "tn" else pl.BlockSpec((tm, k), lambda i, j: (i, 0)))
        in_specs.append(pl.BlockSpec((tn, k), lambda i, j: (j, 0)) if mode == "nt" else pl.BlockSpec((k, tn), lambda i, j: (0, j)))
        args += [a, b]
    for e, off in extras:
        in_specs.append(pl.BlockSpec((tm, tn), lambda i, j, o=off // tn: (i, o + j)))
        args.append(e)
    n_pairs, n_in = len(pairs), 2 * len(pairs) + len(extras)

    def body(*refs):
        products = [lax.dot_general(refs[2 * p][...].astype(BF16), refs[2 * p + 1][...].astype(BF16), dims[pairs[p][2]],
                                    preferred_element_type=F32) for p in range(n_pairs)]
        outs = epilogue(products, [r[...] for r in refs[2 * n_pairs:n_in]])
        for r, o in zip(refs[n_in:], outs):
            r[...] = o.astype(r.dtype)

    tile = pl.BlockSpec((tm, tn), lambda i, j: (i, j))
    return pl.pallas_call(
        body, out_shape=[jax.ShapeDtypeStruct((m, n), d) for d in out_dtypes], grid=(m // tm, n // tn),
        in_specs=in_specs, out_specs=[tile] * len(out_dtypes), name=name,
        compiler_params=_params("parallel", "parallel"))(*args)


def _mm(a, b, mode, out_dtype, name, res=None):
    if res is None:
        return _mm_fused([(a, b, mode)], [], lambda products, extra: products, [out_dtype], name)[0]
    return _mm_fused([(a, b, mode)], [(res, 0)], lambda products, extra: [products[0] + extra[0]], [out_dtype], name)[0]


def _ew(fn, ins, out_dtypes, name):
    rows, width = ins[0].shape
    n_in = len(ins)

    def body(*refs):
        outs = fn(*[r[...] for r in refs[:n_in]])
        for r, o in zip(refs[n_in:], outs):
            r[...] = o.astype(r.dtype)

    row = pl.BlockSpec((ROW_TILE, width), lambda i: (i, 0))
    return pl.pallas_call(
        body, out_shape=[jax.ShapeDtypeStruct((rows, width), dt) for dt in out_dtypes], grid=(rows // ROW_TILE,),
        in_specs=[row] * n_in, out_specs=[row] * len(out_dtypes), name=name, compiler_params=_params("parallel"))(*ins)


def _sigmoid(x):
    return 1.0 / (1.0 + jnp.exp(-x))


def _seg_sum(v):
    outs = []
    for k in range(v.shape[1] // LANES):
        vp = v[:, k * LANES:(k + 1) * LANES]
        left = lax.broadcasted_iota(jnp.int32, vp.shape, 1) < HEAD_DIM
        sl = jnp.sum(jnp.where(left, vp, 0.0), axis=-1, keepdims=True)
        sr = jnp.sum(jnp.where(left, 0.0, vp), axis=-1, keepdims=True)
        outs.append(jnp.where(left, sl, sr))
    return outs[0] if len(outs) == 1 else jnp.concatenate(outs, axis=1)


def _seg_rstd(x):
    return lax.rsqrt(_seg_sum(x * x) * (1.0 / HEAD_DIM) + RMS_EPS)


def _rms_fwd(x, g, name):
    rows, d = x.shape
    tm = ROW_TILE

    def body(x_ref, g_ref, h_ref):
        xv = x_ref[...]
        r = lax.rsqrt(jnp.mean(xv * xv, axis=-1, keepdims=True) + RMS_EPS)
        h_ref[...] = ((xv * r) * g_ref[...]).astype(BF16)

    return pl.pallas_call(
        body, out_shape=jax.ShapeDtypeStruct((rows, d), BF16), grid=(rows // tm,),
        in_specs=[pl.BlockSpec((tm, d), lambda i: (i, 0)), pl.BlockSpec((1, d), lambda i: (0, 0))],
        out_specs=pl.BlockSpec((tm, d), lambda i: (i, 0)), name=name,
        compiler_params=_params("parallel"))(x, g)


def _mm_rms_bwd(pairs, x, g, dres, name):
    rows, d = x.shape
    assert all(_mm_dims(*p)[:2] == (rows, d) for p in pairs)
    kbytes = [(_mm_dims(a, b, mode)[2], a.dtype.itemsize, b.dtype.itemsize) for a, b, mode in pairs]
    tm = max(t for t in (512, 256, 128) if rows % t == 0 and
             2 * (sum(t * k * ab + d * k * bb for k, ab, bb in kbytes) + t * d * 14) + 2 * t * d * 4 <= MM_VMEM_BUDGET)
    dims = {"nn": (((1,), (0,)), ((), ())), "nt": (((1,), (1,)), ((), ()))}
    n_pairs = len(pairs)

    def body(*refs):
        x_ref, g_ref, dres_ref = refs[2 * n_pairs:2 * n_pairs + 3]
        dx_ref, dxb_ref, dg_ref = refs[2 * n_pairs + 3:]
        dhv = functools.reduce(lambda u, v: u + v, [
            lax.dot_general(refs[2 * p][...].astype(BF16), refs[2 * p + 1][...].astype(BF16), dims[pairs[p][2]],
                            preferred_element_type=F32) for p in range(n_pairs)])
        xv = x_ref[...]
        r = lax.rsqrt(jnp.mean(xv * xv, axis=-1, keepdims=True) + RMS_EPS)
        xh = xv * r
        dxh = dhv * g_ref[...]
        dxv = dres_ref[...] + r * (dxh - xh * jnp.mean(dxh * xh, axis=-1, keepdims=True))
        dx_ref[...] = dxv
        dxb_ref[...] = dxv.astype(BF16)
        part = jnp.sum(dhv * xh, axis=0, keepdims=True)

        @pl.when(pl.program_id(0) == 0)
        def _():
            dg_ref[...] = part

        @pl.when(pl.program_id(0) > 0)
        def _():
            dg_ref[...] += part

    row = pl.BlockSpec((tm, d), lambda i: (i, 0))
    vec = pl.BlockSpec((1, d), lambda i: (0, 0))
    in_specs, args = [], []
    for a, b, mode in pairs:
        in_specs += [pl.BlockSpec((tm, a.shape[1]), lambda i: (i, 0)), pl.BlockSpec(b.shape, lambda i: (0, 0))]
        args += [a, b]
    return pl.pallas_call(
        body, out_shape=[jax.ShapeDtypeStruct((rows, d), F32), jax.ShapeDtypeStruct((rows, d), BF16),
                         jax.ShapeDtypeStruct((1, d), F32)],
        grid=(rows // tm,), in_specs=in_specs + [row, vec, row], out_specs=[row, row, vec],
        name=name, compiler_params=_params("arbitrary"))(*args, x, g, dres)


def _loss_grad(y, t):
    rows, d = y.shape
    tm = ROW_TILE

    def body(y_ref, t_ref, dy_ref, l_ref):
        e = y_ref[...] - t_ref[...]
        dy_ref[...] = e * (1.0 / d)
        part = jnp.zeros((1, LANES), F32) + jnp.sum(e * e) * (0.5 / d)

        @pl.when(pl.program_id(0) == 0)
        def _():
            l_ref[...] = part

        @pl.when(pl.program_id(0) > 0)
        def _():
            l_ref[...] += part

    row = pl.BlockSpec((tm, d), lambda i: (i, 0))
    return pl.pallas_call(
        body, out_shape=[jax.ShapeDtypeStruct((rows, d), F32), jax.ShapeDtypeStruct((1, LANES), F32)],
        grid=(rows // tm,), in_specs=[row, row], out_specs=[row, pl.BlockSpec((1, LANES), lambda i: (0, 0))],
        name="loss_grad", compiler_params=_params("arbitrary"))(y, t)


def _swap_halves(v):
    return pltpu.roll(v, HEAD_DIM, axis=1)


def _expand_kv(kv):
    left = lax.broadcasted_iota(jnp.int32, kv.shape, 1) < HEAD_DIM
    sw = _swap_halves(kv)
    h0 = jnp.where(left, kv, sw)
    h1 = jnp.where(left, sw, kv)
    return jnp.concatenate([h0, h0, h1, h1], axis=1)


def _reduce_kv(dkv):
    left = lax.broadcasted_iota(jnp.int32, (dkv.shape[0], LANES), 1) < HEAD_DIM
    t = dkv[:, 0:LANES] + dkv[:, LANES:2 * LANES]
    u = dkv[:, 2 * LANES:3 * LANES] + dkv[:, 3 * LANES:4 * LANES]
    t = t + _swap_halves(t)
    u = u + _swap_halves(u)
    return jnp.where(left, t, u)


def _qknorm_fwd(proj, gqa, gka, gqb, gkb):
    rows = proj.shape[0]
    tm = ROW_TILE

    def body(qa_ref, ka_ref, qb_ref, kb_ref, vb_ref, gqa_ref, gka_ref, gqb_ref, gkb_ref, oqa, oka, oqb, okb, ovb):
        for src, g_ref, dst in ((qa_ref, gqa_ref, oqa), (ka_ref, gka_ref, oka), (qb_ref, gqb_ref, oqb)):
            xv = src[...]
            dst[...] = (xv * _seg_rstd(xv)) * g_ref[...]
        kv = kb_ref[...]
        okb[...] = _expand_kv((kv * _seg_rstd(kv)) * gkb_ref[...])
        ovb[...] = _expand_kv(vb_ref[...])

    def win(width, off):
        return pl.BlockSpec((tm, width), lambda i: (i, off // width))

    vec = lambda w: pl.BlockSpec((1, w), lambda i: (0, 0))
    out = pl.BlockSpec((tm, WIDTH), lambda i: (i, 0))
    return pl.pallas_call(
        body, out_shape=[jax.ShapeDtypeStruct((rows, WIDTH), F32)] * 5, grid=(rows // tm,),
        in_specs=[win(WIDTH, OFF_QA), win(WIDTH, OFF_KA), win(WIDTH, OFF_QB), win(LANES, OFF_KB), win(LANES, OFF_VB),
                  vec(WIDTH), vec(WIDTH), vec(WIDTH), vec(LANES)],
        out_specs=[out] * 5, name="qknorm_fwd", compiler_params=_params("parallel"))(
            proj, proj, proj, proj, proj, gqa, gka, gqb, gkb)


def _norm_bwd(xv, g, dy):
    r = _seg_rstd(xv)
    xh = xv * r
    dxh = dy * g
    dx = r * (dxh - xh * (_seg_sum(dxh * xh) * (1.0 / HEAD_DIM)))
    return dx, jnp.sum(dy * xh, axis=0, keepdims=True)


def _qknorm_bwd(proj, gqa, gka, gqb, gkb, dqa, dka, dva, dqb, dkb, dvb, dga, dgb):
    rows = proj.shape[0]
    tm = ROW_TILE
    n_a = len(dqa)

    def body(*refs):
        qa_ref, ka_ref, qb_ref, kb_ref, gqa_ref, gka_ref, gqb_ref, gkb_ref = refs[:8]
        pos = 8
        dqa_refs, dka_refs, dva_refs = refs[pos:pos + n_a], refs[pos + n_a:pos + 2 * n_a], refs[pos + 2 * n_a:pos + 3 * n_a]
        pos += 3 * n_a
        dqb_ref, dkb_ref, dvb_ref, dga_ref, dgb_ref = refs[pos:pos + 5]
        dproj_ref, ogqa, ogka, ogqb, ogkb = refs[pos + 5:]

        def total(rs):
            acc = rs[0][...]
            for r in rs[1:]:
                acc = acc + r[...]
            return acc

        dx_qa, p_qa = _norm_bwd(qa_ref[...], gqa_ref[...], total(dqa_refs))
        dx_ka, p_ka = _norm_bwd(ka_ref[...], gka_ref[...], total(dka_refs))
        dx_qb, p_qb = _norm_bwd(qb_ref[...], gqb_ref[...], dqb_ref[...])
        dx_kb, p_kb = _norm_bwd(kb_ref[...], gkb_ref[...], _reduce_kv(dkb_ref[...]))
        dproj_ref[:, OFF_QA:OFF_QA + WIDTH] = dx_qa.astype(BF16)
        dproj_ref[:, OFF_KA:OFF_KA + WIDTH] = dx_ka.astype(BF16)
        dproj_ref[:, OFF_VA:OFF_VA + WIDTH] = total(dva_refs).astype(BF16)
        dproj_ref[:, OFF_QB:OFF_QB + WIDTH] = dx_qb.astype(BF16)
        dproj_ref[:, OFF_KB:OFF_KB + LANES] = dx_kb.astype(BF16)
        dproj_ref[:, OFF_VB:OFF_VB + LANES] = _reduce_kv(dvb_ref[...]).astype(BF16)
        dproj_ref[:, OFF_GA:OFF_GB] = dga_ref[...]
        dproj_ref[:, OFF_GB:D_IN] = dgb_ref[...]
        first = pl.program_id(0) == 0
        for o_ref, part in ((ogqa, p_qa), (ogka, p_ka), (ogqb, p_qb), (ogkb, p_kb)):
            @pl.when(first)
            def _(o_ref=o_ref, part=part):
                o_ref[...] = part

            @pl.when(jnp.logical_not(first))
            def _(o_ref=o_ref, part=part):
                o_ref[...] += part

    def win(width, off):
        return pl.BlockSpec((tm, width), lambda i: (i, off // width))

    vec = lambda w: pl.BlockSpec((1, w), lambda i: (0, 0))
    row = lambda w: pl.BlockSpec((tm, w), lambda i: (i, 0))
    in_specs = [win(WIDTH, OFF_QA), win(WIDTH, OFF_KA), win(WIDTH, OFF_QB), win(LANES, OFF_KB),
                vec(WIDTH), vec(WIDTH), vec(WIDTH), vec(LANES)]
    in_specs += [row(WIDTH)] * (3 * n_a + 3) + [row(D_MODEL)] * 2
    return pl.pallas_call(
        body,
        out_shape=[jax.ShapeDtypeStruct((rows, D_IN), BF16), jax.ShapeDtypeStruct((1, WIDTH), F32),
                   jax.ShapeDtypeStruct((1, WIDTH), F32), jax.ShapeDtypeStruct((1, WIDTH), F32),
                   jax.ShapeDtypeStruct((1, LANES), F32)],
        grid=(rows // tm,), in_specs=in_specs,
        out_specs=[row(D_IN), vec(WIDTH), vec(WIDTH), vec(WIDTH), vec(LANES)],
        name="qknorm_bwd", compiler_params=_params("arbitrary"))(
            proj, proj, proj, proj, gqa, gka, gqb, gkb, *dqa, *dka, *dva, dqb, dkb, dvb, dga, dgb)


def _t5_bucket(rel):
    half_b = NUM_BUCKETS // 2
    max_exact = half_b // 2
    sign = jnp.where(rel > 0, half_b, 0)
    n = jnp.abs(rel)
    nf = jnp.maximum(n, 1).astype(F32)
    large = max_exact + (jnp.log(nf / max_exact) / math.log(MAX_DISTANCE / max_exact)
                         * (half_b - max_exact)).astype(jnp.int32)
    large = jnp.minimum(large, half_b - 1)
    return sign + jnp.where(n < max_exact, n, large)


def _band_buckets(blk, dilation):
    i = jnp.arange(blk, dtype=jnp.int32)[:, None]
    j = jnp.arange(3 * blk, dtype=jnp.int32)[None, :]
    rel = j - blk - i
    return jnp.where(jnp.abs(rel) <= blk, _t5_bucket(rel * dilation), -1)


def _bias_tiles(table, buckets, head_off, name):
    blk = buckets.shape[0]

    def body(tab_ref, bk_ref, o_ref):
        h = pl.program_id(0) + head_off
        bk = bk_ref[...]
        acc = jnp.full(bk.shape, NEG_INF, F32)
        for b in range(NUM_BUCKETS):
            acc = jnp.where(bk == b, tab_ref[b, h], acc)
        o_ref[0] = acc

    return pl.pallas_call(
        body, out_shape=jax.ShapeDtypeStruct((N_HEADS, blk, 3 * blk), F32), grid=(N_HEADS,),
        in_specs=[pl.BlockSpec(memory_space=pltpu.SMEM), pl.BlockSpec((blk, 3 * blk), lambda h: (0, 0))],
        out_specs=pl.BlockSpec((1, blk, 3 * blk), lambda h: (h, 0, 0)),
        name=name, compiler_params=_params("parallel"))(table, buckets)


def _table_grad(dbias, buckets, name):
    blk = buckets.shape[0]

    def body(db_ref, bk_ref, o_ref):
        bk = bk_ref[...]
        dbv = db_ref[0]
        lane = lax.broadcasted_iota(jnp.int32, (1, LANES), 1)
        acc = jnp.zeros((1, LANES), F32)
        for b in range(NUM_BUCKETS):
            acc = jnp.where(lane == b, jnp.sum(jnp.where(bk == b, dbv, 0.0)), acc)
        o_ref[0] = acc

    out = pl.pallas_call(
        body, out_shape=jax.ShapeDtypeStruct((N_HEADS, 1, LANES), F32), grid=(N_HEADS,),
        in_specs=[pl.BlockSpec((1, blk, 3 * blk), lambda h: (h, 0, 0)), pl.BlockSpec((blk, 3 * blk), lambda h: (0, 0))],
        out_specs=pl.BlockSpec((1, 1, LANES), lambda h: (h, 0, 0)),
        name=name, compiler_params=_params("parallel"))(dbias, buckets)
    return out[:, 0, :NUM_BUCKETS]


def _dot_nt(a, b):
    return lax.dot_general(a, b, (((1,), (1,)), ((), ())), preferred_element_type=F32)


def _dot_tn(a, b):
    return lax.dot_general(a, b, (((0,), (0,)), ((), ())), preferred_element_type=F32)


def _stack_pair(x2, left):
    return jnp.concatenate([jnp.where(left, x2, 0.0), jnp.where(left, 0.0, x2)], axis=0).astype(BF16)


def _attn_geometry(blk, d):
    halo = blk * d
    subs = max(1, min(ITEMS // d, MAX_CHUNK // halo))
    return subs, min(d, ITEMS // subs), halo


def _item_of(j, r0, per_group):
    return j // per_group, r0 + j % per_group


def _span_rows(ref, first, r, blk, d):
    if d == 1:
        return ref[first:first + blk, :]
    return ref[pl.ds(first + r, blk, stride=d), :]


def _set_span_rows(ref, first, r, blk, d, val, add=False):
    idx = slice(first, first + blk) if d == 1 else pl.ds(first + r, blk, stride=d)
    ref[idx, :] = ref[idx, :] + val if add else val


def _for_groups(group, groups, per_group):
    if groups == 1:
        group(0)
    else:
        def step(g, carry):
            group(g * per_group)
            return carry

        lax.fori_loop(0, groups, step, 0)


def _key_rows(p_ref, c_ref, n_ref, s, r, subs, halo, blk, d):
    parts = []
    for span in (s - 1, s, s + 1):
        if span < 0:
            parts.append(_span_rows(p_ref, 0, r, blk, d))
        elif span == subs:
            parts.append(_span_rows(n_ref, 0, r, blk, d))
        else:
            parts.append(_span_rows(c_ref, span * halo, r, blk, d))
    return jnp.concatenate(parts, axis=0)


def _item_penalty(t, nct, s, subs, blk):
    first_ok = True if s > 0 else t > 0
    last_ok = True if s < subs - 1 else t < nct - 1
    col = lax.broadcasted_iota(jnp.int32, (1, 3 * blk), 1)
    ok = jnp.logical_and(jnp.logical_or(col >= blk, first_ok), jnp.logical_or(col < 2 * blk, last_ok))
    return jnp.where(ok, 0.0, NEG_INF).astype(F32)


def _attn_specs(seq, blk, d, step_of, col=0):
    subs, _, halo = _attn_geometry(blk, d)
    last, first = seq // halo - 1, col // LANES
    cur = pl.BlockSpec((subs * halo, LANES), lambda hp, t: (step_of(t), first + hp))
    prev = pl.BlockSpec((halo, LANES), lambda hp, t: (jnp.clip(step_of(t) * subs - 1, 0, last), first + hp))
    nxt = pl.BlockSpec((halo, LANES), lambda hp, t: (jnp.minimum((step_of(t) + 1) * subs, last), first + hp))
    return cur, prev, nxt


def _attn_fwd(q, k, v, bias, sink, blk, d, name, v_col=0):
    seq = q.shape[0]
    subs, per_group, halo = _attn_geometry(blk, d)
    items, chunk, groups = subs * per_group, subs * halo, d // per_group
    nct = seq // chunk
    has_sink = sink is not None
    scale = HEAD_DIM ** -0.5

    def body(*refs):
        q_ref, kp, kc, kn, vp, vc, vn, b_ref = refs[:8]
        s_ref = refs[8] if has_sink else None
        o_ref, l_ref = refs[-2], refs[-1]
        t = pl.program_id(1)
        left = lax.broadcasted_iota(jnp.int32, (1, LANES), 1) < HEAD_DIM
        bias2 = b_ref[...]

        def group(r0):
            scores, vcats = [], []
            for j in range(items):
                s, r = _item_of(j, r0, per_group)
                qs = _stack_pair(_span_rows(q_ref, s * halo, r, blk, d) * scale, left)
                kcat = _key_rows(kp, kc, kn, s, r, subs, halo, blk, d).astype(BF16)
                scores.append(_dot_nt(qs, kcat) + bias2 + _item_penalty(t, nct, s, subs, blk))
                vcats.append(_key_rows(vp, vc, vn, s, r, subs, halo, blk, d).astype(BF16))
            ms = [jnp.max(s, axis=-1, keepdims=True) for s in scores]
            if has_sink:
                sk = s_ref[...]
                ms = [jnp.maximum(m, sk) for m in ms]
            ps = [jnp.exp(s - m) for s, m in zip(scores, ms)]
            dens = [jnp.sum(p, axis=-1, keepdims=True) for p in ps]
            if has_sink:
                dens = [den + jnp.exp(sk - m) for den, m in zip(dens, ms)]
            pns = [(p * (1.0 / den)).astype(BF16) for p, den in zip(ps, dens)]
            lses = [m + jnp.log(den) for m, den in zip(ms, dens)]
            for j in range(items):
                s, r = _item_of(j, r0, per_group)
                o2 = jnp.dot(pns[j], vcats[j], preferred_element_type=F32)
                _set_span_rows(o_ref, s * halo, r, blk, d, jnp.where(left, o2[:blk], o2[blk:]))
                _set_span_rows(l_ref, s * halo, r, blk, d, jnp.where(left, lses[j][:blk], lses[j][blk:]))

        _for_groups(group, groups, per_group)

    cur, prev, nxt = _attn_specs(seq, blk, d, lambda t: t)
    v_cur, v_prev, v_nxt = _attn_specs(seq, blk, d, lambda t: t, v_col)
    in_specs = [cur, prev, cur, nxt, v_prev, v_cur, v_nxt, pl.BlockSpec((2 * blk, 3 * blk), lambda hp, t: (hp, 0))]
    args = [q, k, k, k, v, v, v, bias]
    if has_sink:
        in_specs.append(pl.BlockSpec((2 * blk, 1), lambda hp, t: (hp, 0)))
        args.append(sink)
    return pl.pallas_call(
        body, out_shape=[jax.ShapeDtypeStruct((seq, WIDTH), F32)] * 2, grid=(N_PAIRS, nct),
        in_specs=in_specs, out_specs=[cur, cur], name=name,
        compiler_params=_params("parallel", "parallel"))(*args)


def _attn_bwd(q, k, v, do, lse, delta, bias, sink, blk, d, name, v_col=0):
    seq = q.shape[0]
    subs, per_group, halo = _attn_geometry(blk, d)
    items, chunk, groups = subs * per_group, subs * halo, d // per_group
    nct = seq // chunk
    has_sink = sink is not None
    n_in = 12 if has_sink else 11
    scale = HEAD_DIM ** -0.5

    def body(*refs):
        q_ref, kp, kc, kn, vp, vc, vn, do_ref, l_ref, d_ref, b_ref = refs[:11]
        s_ref = refs[11] if has_sink else None
        dq_ref, dk_ref, dv_ref, db_ref = refs[n_in:n_in + 4]
        ds_ref = refs[n_in + 4] if has_sink else None
        wk, wv = refs[-2], refs[-1]
        t = pl.program_id(1)

        @pl.when(t == 0)
        def _():
            wk[...] = jnp.zeros_like(wk)
            wv[...] = jnp.zeros_like(wv)
            db_ref[...] = jnp.zeros_like(db_ref)
            if has_sink:
                ds_ref[...] = jnp.zeros_like(ds_ref)

        @pl.when(t > 0)
        def _():
            for w in (wk, wv):
                keep = w[chunk:2 * chunk + halo]
                w[0:chunk + halo] = keep
                w[chunk + halo:2 * chunk + halo] = jnp.zeros((chunk, LANES), F32)

        @pl.when(t < nct)
        def _():
            lane = lax.broadcasted_iota(jnp.int32, (1, LANES), 1)
            left = lane < HEAD_DIM
            bias2 = b_ref[...]

            def group(r0):
                qss, doss, kcats, scores, dps, lcols, dcols = [], [], [], [], [], [], []
                for j in range(items):
                    s, r = _item_of(j, r0, per_group)
                    qs = _stack_pair(_span_rows(q_ref, s * halo, r, blk, d) * scale, left)
                    dos = _stack_pair(_span_rows(do_ref, s * halo, r, blk, d), left)
                    kcat = _key_rows(kp, kc, kn, s, r, subs, halo, blk, d).astype(BF16)
                    vcat = _key_rows(vp, vc, vn, s, r, subs, halo, blk, d).astype(BF16)
                    l2, d2 = _span_rows(l_ref, s * halo, r, blk, d), _span_rows(d_ref, s * halo, r, blk, d)
                    lcols.append(jnp.concatenate([jnp.max(jnp.where(left, l2, NEG_INF), axis=-1, keepdims=True),
                                                  jnp.max(jnp.where(left, NEG_INF, l2), axis=-1, keepdims=True)], axis=0))
                    dcols.append(jnp.concatenate([jnp.sum(jnp.where(lane == 0, d2, 0.0), axis=-1, keepdims=True),
                                                  jnp.sum(jnp.where(lane == HEAD_DIM, d2, 0.0), axis=-1, keepdims=True)],
                                                 axis=0))
                    scores.append(_dot_nt(qs, kcat) + bias2 + _item_penalty(t, nct, s, subs, blk))
                    dps.append(_dot_nt(dos, vcat))
                    qss.append(qs)
                    doss.append(dos)
                    kcats.append(kcat)
                ps = [jnp.exp(s - lc) for s, lc in zip(scores, lcols)]
                dss = [p * (dp - dc) for p, dp, dc in zip(ps, dps, dcols)]
                db_ref[...] += functools.reduce(lambda a, b: a + b, dss)
                if has_sink:
                    sk = s_ref[...]
                    ds_ref[...] -= functools.reduce(lambda a, b: a + b, [dc * jnp.exp(sk - lc) for dc, lc in zip(dcols, lcols)])
                dsbs = [ds.astype(BF16) for ds in dss]
                for j in range(items):
                    s, r = _item_of(j, r0, per_group)
                    dq2 = jnp.dot(dsbs[j], kcats[j], preferred_element_type=F32) * scale
                    _set_span_rows(dq_ref, s * halo, r, blk, d, jnp.where(left, dq2[:blk], dq2[blk:]))
                news = [(_dot_tn(dsbs[j], qss[j]), _dot_tn(ps[j].astype(BF16), doss[j])) for j in range(items)]
                for which, w in enumerate((wk, wv)):
                    for jr in range(per_group):
                        for sp in range(-1, subs + 1):
                            parts = [news[s * per_group + jr][which][(sp - s + 1) * blk:(sp - s + 2) * blk]
                                     for s in range(subs) if 0 <= sp - s + 1 < 3]
                            _set_span_rows(w, chunk + sp * halo, r0 + jr, blk, d,
                                           functools.reduce(lambda x, y: x + y, parts), add=True)

            _for_groups(group, groups, per_group)

        dk_ref[...] = wk[0:chunk]
        dv_ref[...] = wv[0:chunk]

    cur, prev, nxt = _attn_specs(seq, blk, d, lambda t: jnp.minimum(t, nct - 1))
    v_cur, v_prev, v_nxt = _attn_specs(seq, blk, d, lambda t: jnp.minimum(t, nct - 1), v_col)
    lag = pl.BlockSpec((chunk, LANES), lambda hp, t: (jnp.maximum(t - 1, 0), hp))
    band = pl.BlockSpec((2 * blk, 3 * blk), lambda hp, t: (hp, 0))
    col = pl.BlockSpec((2 * blk, 1), lambda hp, t: (hp, 0))
    in_specs = [cur, prev, cur, nxt, v_prev, v_cur, v_nxt, cur, cur, cur, band]
    args = [q, k, k, k, v, v, v, do, lse, delta, bias]
    out_shape = [jax.ShapeDtypeStruct((seq, WIDTH), F32)] * 3 + [jax.ShapeDtypeStruct((N_HEADS * blk, 3 * blk), F32)]
    out_specs = [cur, lag, lag, band]
    if has_sink:
        in_specs.append(col)
        args.append(sink)
        out_shape.append(jax.ShapeDtypeStruct((N_HEADS * blk, 1), F32))
        out_specs.append(col)
    window = pltpu.VMEM((2 * chunk + halo, LANES), F32)
    return pl.pallas_call(
        body, out_shape=out_shape, grid=(N_PAIRS, nct + 1), in_specs=in_specs, out_specs=out_specs,
        scratch_shapes=[window, window], name=name,
        compiler_params=_params("arbitrary", "arbitrary"))(*args)


def _combine_patterns(outs, lses):
    def combine(*tiles):
        os_, ls = tiles[:len(outs)], tiles[len(outs):]
        m = functools.reduce(jnp.maximum, ls)
        es = [jnp.exp(l - m) for l in ls]
        den = functools.reduce(lambda a, b: a + b, es)
        num = functools.reduce(lambda a, b: a + b, [e * o for e, o in zip(es, os_)])
        return num / den, m + jnp.log(den)

    return _ew(combine, [*outs, *lses], [F32, F32], "combine_a")


def _row_dots(dy, y, name):
    return _ew(lambda dy_, y_: (_seg_sum(dy_ * y_),), [dy, y], [F32], name)[0]


def _tile_gain(g, reps):
    return jnp.tile(g[None, :], (1, reps))


def _local_step(x, p, target, w_in_of, rest_of, small):
    rel_table = small["rel_table"]
    buckets_a = [_band_buckets(blk, d) for blk, d in DILATED]
    buckets_b = _band_buckets(BLK_B, 1)
    bias_a = [_bias_tiles(rel_table, bk, 0, "bias_a").reshape(N_HEADS * bk.shape[0], -1) for bk in buckets_a]
    bias_b = _bias_tiles(rel_table, buckets_b, N_HEADS, "bias_b").reshape(N_HEADS * BLK_B, -1)

    saved = []
    for l in range(DEPTH):
        g_mix, g_ffn, g_ple = (small[n][l][None, :] for n in ("norm_mix_g", "norm_ffn_g", "norm_ple_g"))
        gqa, gka, gqb = (_tile_gain(small[n][l], N_HEADS) for n in ("qnorm_a_g", "knorm_a_g", "qnorm_b_g"))
        gkb = _tile_gain(small["knorm_b_g"][l], N_KV_B)
        sink = jnp.repeat(small["sink_b"][l], BLK_B)[:, None]

        h = _rms_fwd(x, g_mix, "rms_mix")
        w_in = w_in_of(l, (h, bias_a, bias_b))
        proj = _mm(h, w_in, "nt", F32, "mm_in")
        qa, ka, qb, kb, vb = _qknorm_fwd(proj, gqa, gka, gqb, gkb)
        outs, lses = [], []
        for (blk, d), bias in zip(DILATED, bias_a):
            o, ls = _attn_fwd(qa, ka, proj, bias, None, blk, d, f"attn_a{d}_fwd", v_col=OFF_VA)
            outs.append(o)
            lses.append(ls)
        ya, lse_a = _combine_patterns(outs, lses)
        yb, lse_b = _attn_fwd(qb, kb, vb, bias_b, sink, BLK_B, 1, "attn_b_fwd")
        w = dict(rest_of(l, yb), w_in=w_in)
        def gate(products, extra):
            (ca_, cb_), (ga_, gb_) = products, extra
            return _sigmoid(ga_) * ca_ + _sigmoid(gb_) * cb_, ca_, cb_

        merged, ca, cb = _mm_fused([(ya, w["w_branch_a"], "nn"), (yb, w["w_branch_b"], "nn")],
                                   [(proj, OFF_GA), (proj, OFF_GB)], gate, [BF16, BF16, BF16], "mm_branches_gate")
        x1 = _mm(merged, w["w_out"], "nn", F32, "mm_out", res=x)

        h2 = _rms_fwd(x1, g_ffn, "rms_ffn")

        def swiglu(products, extra):
            a_, u_ = products
            return (a_ * _sigmoid(a_)) * u_, a_, u_

        hid, a, u = _mm_fused([(h2, w["w_ffn_gate"], "nt"), (h2, w["w_ffn_up"], "nt")], [], swiglu,
                              [BF16, BF16, BF16], "mm_ffn_gate_up")
        x2 = _mm(hid, w["w_ffn_down"], "nn", F32, "mm_ffn_down", res=x1)

        h3 = _rms_fwd(x2, g_ple, "rms_ple")

        def ple(products, extra):
            z_, e_ = products
            return extra[0] + _sigmoid(z_) * e_, z_, e_

        x3, z, e = _mm_fused([(h3, w["w_ple_gate"], "nn"), (p[l], w["w_ple_proj"], "nn")], [(x2, 0)], ple,
                             [F32, BF16, BF16], "mm_ple")
        saved.append(dict(w=w, x0=x, h=h, proj=proj, qa=qa, ka=ka, qb=qb, kb=kb, vb=vb, ya=ya, lse_a=lse_a,
                          yb=yb, lse_b=lse_b, ca=ca, cb=cb, merged=merged, x1=x1, h2=h2, a=a, u=u, hid=hid,
                          x2=x2, h3=h3, z=z, e=e))
        x = x3

    dx, loss_acc = _loss_grad(x, target)
    loss = loss_acc[0, 0]

    gbig = [{} for _ in range(DEPTH)]
    marks = [{} for _ in range(DEPTH)]
    gsmall = {n: [None] * DEPTH for n in SMALL if n != "rel_table"}
    dbias_a = [[] for _ in DILATED]
    dbias_b = []

    for l in reversed(range(DEPTH)):
        sv = saved[l]
        w = sv["w"]
        g_mix, g_ffn, g_ple = (small[n][l][None, :] for n in ("norm_mix_g", "norm_ffn_g", "norm_ple_g"))
        gqa, gka, gqb = (_tile_gain(small[n][l], N_HEADS) for n in ("qnorm_a_g", "knorm_a_g", "qnorm_b_g"))
        gkb = _tile_gain(small["knorm_b_g"][l], N_KV_B)
        sink = jnp.repeat(small["sink_b"][l], BLK_B)[:, None]

        def ple_bwd(dx_, z_, e_):
            s = _sigmoid(z_.astype(F32))
            return dx_ * s, dx_ * e_.astype(F32) * (s * (1.0 - s))

        de, dz = _ew(ple_bwd, [dx, sv["z"], sv["e"]], [BF16, BF16], "ple_bwd")
        gbig[l]["w_ple_proj"] = _mm(p[l], de, "tn", BF16, "mm_d_ple_proj")
        gbig[l]["w_ple_gate"] = _mm(sv["h3"], dz, "tn", BF16, "mm_d_ple_gate")
        dx, dxb, gsmall["norm_ple_g"][l] = _mm_rms_bwd([(dz, w["w_ple_gate"], "nt")], sv["x2"], g_ple, dx,
                                                      "mm_dh3_rms_bwd")

        gbig[l]["w_ffn_down"] = _mm(sv["hid"], dxb, "tn", BF16, "mm_d_ffn_down")

        def swiglu_bwd(products, extra):
            dh_, a_, u_ = products[0], extra[0].astype(F32), extra[1].astype(F32)
            s = _sigmoid(a_)
            return dh_ * u_ * (s * (1.0 + a_ * (1.0 - s))), dh_ * (a_ * s)

        da, du = _mm_fused([(dxb, w["w_ffn_down"], "nt")], [(sv["a"], 0), (sv["u"], 0)], swiglu_bwd, [BF16, BF16],
                           "mm_dhid_swiglu_bwd")
        gbig[l]["w_ffn_gate"] = _mm(da, sv["h2"], "tn", BF16, "mm_d_ffn_gate")
        gbig[l]["w_ffn_up"] = _mm(du, sv["h2"], "tn", BF16, "mm_d_ffn_up")
        dx, dxb, gsmall["norm_ffn_g"][l] = _mm_rms_bwd([(da, w["w_ffn_gate"], "nn"), (du, w["w_ffn_up"], "nn")],
                                                      sv["x1"], g_ffn, dx, "mm_dh2_rms_bwd")

        gbig[l]["w_out"] = _mm(sv["merged"], dxb, "tn", BF16, "mm_d_out")

        def gate_bwd(products, extra):
            dm_, ca_, cb_ = products[0], extra[0].astype(F32), extra[1].astype(F32)
            sa, sb = _sigmoid(extra[2]), _sigmoid(extra[3])
            return dm_ * sa, dm_ * sb, dm_ * ca_ * (sa * (1.0 - sa)), dm_ * cb_ * (sb * (1.0 - sb))

        dca, dcb, dga, dgb = _mm_fused(
            [(dxb, w["w_out"], "nt")], [(sv["ca"], 0), (sv["cb"], 0), (sv["proj"], OFF_GA), (sv["proj"], OFF_GB)],
            gate_bwd, [BF16, BF16, BF16, BF16], "mm_dmerged_gate_bwd")
        gbig[l]["w_branch_a"] = _mm(sv["ya"], dca, "tn", BF16, "mm_d_branch_a")
        gbig[l]["w_branch_b"] = _mm(sv["yb"], dcb, "tn", BF16, "mm_d_branch_b")
        dya = _mm(dca, w["w_branch_a"], "nt", F32, "mm_dya")
        dyb = _mm(dcb, w["w_branch_b"], "nt", F32, "mm_dyb")

        delta_a = _row_dots(dya, sv["ya"], "attn_a_row_dots")
        delta_b = _row_dots(dyb, sv["yb"], "attn_b_row_dots")

        dqa, dka, dva = [], [], []
        for (blk, d), bias, bk in zip(DILATED, bias_a, buckets_a):
            dq_, dk_, dv_, db_ = _attn_bwd(sv["qa"], sv["ka"], sv["proj"], dya, sv["lse_a"], delta_a, bias, None, blk, d,
                                           f"attn_a{d}_bwd", v_col=OFF_VA)
            dqa.append(dq_)
            dka.append(dk_)
            dva.append(dv_)
            dbias_a[len(dqa) - 1].append(db_)
        dqb, dkb, dvb, db_, dsink = _attn_bwd(sv["qb"], sv["kb"], sv["vb"], dyb, sv["lse_b"], delta_b, bias_b, sink,
                                              BLK_B, 1, "attn_b_bwd")
        dbias_b.append(db_)
        gsmall["sink_b"][l] = dsink.reshape(N_HEADS, BLK_B).sum(axis=1)

        dproj, pqa, pka, pqb, pkb = _qknorm_bwd(sv["proj"], gqa, gka, gqb, gkb, dqa, dka, dva, dqb, dkb, dvb, dga, dgb)
        marks[l]["attn_bwd_done"] = dproj
        gsmall["qnorm_a_g"][l] = pqa.reshape(N_HEADS, HEAD_DIM).sum(0)
        gsmall["knorm_a_g"][l] = pka.reshape(N_HEADS, HEAD_DIM).sum(0)
        gsmall["qnorm_b_g"][l] = pqb.reshape(N_HEADS, HEAD_DIM).sum(0)
        gsmall["knorm_b_g"][l] = pkb.reshape(N_KV_B, HEAD_DIM).sum(0)
        gbig[l]["w_in"] = _mm(dproj, sv["h"], "tn", BF16, "mm_d_in")
        dx, _, gsmall["norm_mix_g"][l] = _mm_rms_bwd([(dproj, w["w_in"], "nn")], sv["x0"], g_mix, dx, "mm_dh_rms_bwd")
        gsmall["norm_mix_g"][l] = gsmall["norm_mix_g"][l][0]
        gsmall["norm_ffn_g"][l] = gsmall["norm_ffn_g"][l][0]
        gsmall["norm_ple_g"][l] = gsmall["norm_ple_g"][l][0]

    gsmall = {n: jnp.stack(v) for n, v in gsmall.items()}
    dtable_a = sum(_table_grad(sum(dbs).reshape(N_HEADS, blk, 3 * blk), bk, "table_grad_a")
                   for dbs, (blk, _), bk in zip(dbias_a, DILATED, buckets_a))
    dtable_b = _table_grad(sum(dbias_b).reshape(N_HEADS, BLK_B, 3 * BLK_B), buckets_b, "table_grad_b")
    gsmall["rel_table"] = jnp.concatenate([dtable_a, dtable_b], axis=0).T
    return loss, dx, gbig, gsmall, marks


def _place():
    return lax.axis_index("x"), lax.axis_index("y"), lax.axis_index("c")


def _flip(v, bit):
    return 1 - v if bit else v


CHIP_RELATIONS = ((0, 1), (1, 0), (1, 1))
ANY = pl.BlockSpec(memory_space=pl.ANY)


def _allgather_body(w_refs, out_refs, send_sems, recv_sems):
    x, y, c = _place()
    chips = [(_flip(x, a), _flip(y, b)) for a, b in CHIP_RELATIONS]

    def make(g):
        w_ref, out_ref = w_refs[g], out_refs[g]
        half = w_ref.shape[0] // 2

        def part(px, py, pc):
            return out_ref.at[2 * px + py, pl.ds(pc * half, half), :]

        def copy(k, block, to, src=None):
            return pltpu.make_async_remote_copy(
                src_ref=part(*block) if src is None else src, dst_ref=part(*block),
                send_sem=send_sems.at[7 * g + k], recv_sem=recv_sems.at[7 * g + k], device_id=to,
                device_id_type=MESH_ID)

        own = pltpu.make_async_remote_copy(
            src_ref=w_ref, dst_ref=out_ref.at[2 * x + y], send_sem=send_sems.at[7 * g + 6],
            recv_sem=recv_sems.at[7 * g + 6], device_id=(x, y, 1 - c), device_id_type=MESH_ID)
        first = [copy(k, (x, y, c), (*chip, c), src=w_ref.at[pl.ds(c * half, half), :]) for k, chip in enumerate(chips)]
        passed = [copy(3 + k, (*chip, c), (x, y, 1 - c)) for k, chip in enumerate(chips)]
        arrive = [copy(k, (*chip, c), (x, y, c)) for k, chip in enumerate(chips)]
        arrive2 = [copy(3 + k, (*chip, 1 - c), (x, y, c)) for k, chip in enumerate(chips)]
        return own, first, passed, arrive, arrive2

    made = [make(g) for g in range(len(w_refs))]
    for own, first, _, _, _ in made:
        own.start()
        for cp in first:
            cp.start()
    for _, _, passed, arrive, _ in made:
        for k in range(3):
            arrive[k].wait_recv()
            passed[k].start()
    for own, first, passed, _, arrive2 in made:
        for k in range(3):
            arrive2[k].wait_recv()
        own.wait_recv()
        for cp in first + passed + [own]:
            cp.wait_send()


def _sibling(x, y, c):
    return [(x, y, 1 - c)]


def _same_core_of_other_chips(x, y, c):
    return [(_flip(x, a), _flip(y, b), c) for a, b in CHIP_RELATIONS]


def _exchange(body, ins, out_types, n_sems, name, sequencer=None):
    n = len(ins)
    sems = (pltpu.SemaphoreType.DMA((n_sems,)), pltpu.SemaphoreType.DMA((n_sems,)))
    if sequencer is None:
        in_place = out_types is None
        out_shape = [jax.ShapeDtypeStruct(a.shape, a.dtype) for a in ins] if in_place else out_types

        def tc_body(*refs):
            body(refs[:n], refs[n:n + len(out_shape)], refs[-2], refs[-1])

        return list(pl.pallas_call(
            tc_body, out_shape=out_shape, in_specs=[ANY] * n, out_specs=[ANY] * len(out_shape),
            input_output_aliases={g: g for g in range(n)} if in_place else {}, scratch_shapes=list(sems), name=name)(*ins))

    collective_id, peers = sequencer
    hbm = pltpu.MemorySpace.HBM
    in_refs = [jax.new_ref(a, memory_space=hbm) for a in ins]
    out_refs = in_refs if out_types is None else [jax.empty_ref(t, memory_space=hbm) for t in out_types]

    @pl.kernel(mesh=plsc.ScalarSubcoreMesh(axis_name="sequencer", num_cores=1), name=name, scratch_types=sems,
               compiler_params=pltpu.CompilerParams(collective_id=collective_id))
    def launch(send_sems, recv_sems):
        barrier = pltpu.get_barrier_semaphore()
        devices = peers(*_place())
        for device in devices:
            pl.semaphore_signal(barrier, inc=1, device_id=device, device_id_type=MESH_ID)
        pl.semaphore_wait(barrier, len(devices))
        body(in_refs, out_refs, send_sems, recv_sems)

    launch()
    return [r[...] for r in out_refs]


def _allgather(shards, name, sequencer=None):
    out_types = [jax.ShapeDtypeStruct((N_CHIPS,) + s.shape, s.dtype) for s in shards]
    if sequencer is not None:
        sequencer = (sequencer, lambda x, y, c: _sibling(x, y, c) + _same_core_of_other_chips(x, y, c))
    return _exchange(_allgather_body, shards, out_types, 7 * len(shards), name, sequencer)


def _half_tile(half):
    return max(t for t in range(16, 1025, 16) if half % t == 0)


def _run_copies(cps):
    for cp in cps:
        cp.start()
    for cp in cps:
        cp.wait_recv()
    for cp in cps:
        cp.wait_send()


def _sibling_halves(gsends, name, sequencer=None):
    def body(g_refs, out_refs, send_sems, recv_sems):
        x, y, c = _place()
        cps = []
        for g, (g_ref, out_ref) in enumerate(zip(g_refs, out_refs)):
            half = g_ref.shape[1] // 2
            cps.append(pltpu.make_async_remote_copy(
                src_ref=g_ref.at[:, pl.ds((1 - c) * half, half), :], dst_ref=out_ref,
                send_sem=send_sems.at[g], recv_sem=recv_sems.at[g], device_id=(x, y, 1 - c), device_id_type=MESH_ID))
        _run_copies(cps)

    out_types = [jax.ShapeDtypeStruct((s.shape[0], s.shape[1] // 2, s.shape[2]), s.dtype) for s in gsends]
    return _exchange(body, gsends, out_types, len(gsends), name, sequencer and (sequencer, _sibling))


def _chip_sums(gsend, sib, place):
    n, rows, cols = gsend.shape
    half = rows // 2
    tm = _half_tile(half)
    nblk = half // tm

    def body(s_ref, g_ref, sib_ref, o_ref):
        o_ref[0] = (g_ref[0].astype(F32) + sib_ref[0].astype(F32)).astype(o_ref.dtype)

    grid_spec = pltpu.PrefetchScalarGridSpec(
        num_scalar_prefetch=1, grid=(n, nblk),
        in_specs=[pl.BlockSpec((1, tm, cols), lambda k, i, s: (jnp.bitwise_xor(s[0], k), s[1] * nblk + i, 0)),
                  pl.BlockSpec((1, tm, cols), lambda k, i, s: (jnp.bitwise_xor(s[0], k), i, 0))],
        out_specs=pl.BlockSpec((1, tm, cols), lambda k, i, s: (k, i, 0)))
    return pl.pallas_call(
        body, out_shape=jax.ShapeDtypeStruct((n, half, cols), BF16), grid_spec=grid_spec,
        name="rs_chip_sums", compiler_params=_params("parallel", "parallel"))(place, gsend, sib)


def _exchange_chip_sums(tsends, name, sequencer=None):
    def body(t_refs, out_refs, send_sems, recv_sems):
        x, y, c = _place()
        cps = []
        for g, (t_ref, out_ref) in enumerate(zip(t_refs, out_refs)):
            for k, device in enumerate(_same_core_of_other_chips(x, y, c)):
                cps.append(pltpu.make_async_remote_copy(
                    src_ref=t_ref.at[k + 1], dst_ref=out_ref.at[k], send_sem=send_sems.at[3 * g + k],
                    recv_sem=recv_sems.at[3 * g + k], device_id=device, device_id_type=MESH_ID))
        _run_copies(cps)

    out_types = [jax.ShapeDtypeStruct((3,) + s.shape[1:], s.dtype) for s in tsends]
    return _exchange(body, tsends, out_types, 3 * len(tsends), name,
                     sequencer and (sequencer, _same_core_of_other_chips))


def _final_sum(tsend, recv, place):
    n, half, cols = tsend.shape
    tm = _half_tile(half)
    nblk = half // tm

    def body(s_ref, t_ref, r_ref, o_ref):
        o_ref[...] = ((t_ref[0].astype(F32) + r_ref[0].astype(F32)) + r_ref[1].astype(F32)) + r_ref[2].astype(F32)

    grid_spec = pltpu.PrefetchScalarGridSpec(
        num_scalar_prefetch=1, grid=(nblk,),
        in_specs=[pl.BlockSpec((1, tm, cols), lambda i, s: (0, i, 0)), pl.BlockSpec((n - 1, tm, cols), lambda i, s: (0, i, 0))],
        out_specs=pl.BlockSpec((tm, cols), lambda i, s: (s[1] * nblk + i, 0)))
    return pl.pallas_call(
        body, out_shape=jax.ShapeDtypeStruct((2 * half, cols), F32), grid_spec=grid_spec, name="rs_final_sum",
        compiler_params=_params("parallel"))(place, tsend, recv)


def _join_halves(gfulls, name, sequencer=None):
    def body(g_refs, out_refs, send_sems, recv_sems):
        x, y, c = _place()
        n = len(g_refs)

        def copy(g, pc):
            half = g_refs[g].shape[0] // 2
            return pltpu.make_async_remote_copy(
                src_ref=g_refs[g].at[pl.ds(pc * half, half), :], dst_ref=out_refs[g].at[pl.ds(pc * half, half), :],
                send_sem=send_sems.at[g], recv_sem=recv_sems.at[g], device_id=(x, y, 1 - c), device_id_type=MESH_ID)

        mine = [copy(g, c) for g in range(n)]
        for cp in mine:
            cp.start()
        for g in range(n):
            copy(g, 1 - c).wait_recv()
        for cp in mine:
            cp.wait_send()

    return _exchange(body, gfulls, None, len(gfulls), name, sequencer and (sequencer, _sibling))


def _allreduce_small(v):
    rows, cols = v.shape

    def body(v_ref, out_ref, buf, send_sems, recv_sems):
        x, y, c = _place()
        cps = []
        for k in range(1, 8):
            peer = (_flip(x, (k >> 2) & 1), _flip(y, (k >> 1) & 1), _flip(c, k & 1))
            cps.append(pltpu.make_async_remote_copy(
                src_ref=v_ref, dst_ref=buf.at[k - 1], send_sem=send_sems.at[k - 1], recv_sem=recv_sems.at[k - 1],
                device_id=peer, device_id_type=MESH_ID))
        for cp in cps:
            cp.start()
        for cp in cps:
            cp.wait_recv()
        for cp in cps:
            cp.wait_send()
        t0 = v_ref[...] + buf[0]
        t1 = buf[1] + buf[2]
        t2 = buf[3] + buf[4]
        t3 = buf[5] + buf[6]
        out_ref[...] = (t0 + t1) + (t2 + t3)

    vm = pl.BlockSpec(memory_space=pltpu.VMEM)
    return pl.pallas_call(
        body, out_shape=jax.ShapeDtypeStruct((rows, cols), F32), in_specs=[vm], out_specs=vm,
        scratch_shapes=[pltpu.VMEM((7, rows, cols), F32), pltpu.SemaphoreType.DMA((7,)), pltpu.SemaphoreType.DMA((7,))],
        name="allreduce_small")(v)


BIG_INFO = {n: (shape, ax) for n, shape, ax in BIG}
GROUPS = (("w_in",), ("w_ffn_gate", "w_ffn_up", "w_ffn_down", "w_out", "w_ple_gate"),
          ("w_branch_a", "w_branch_b", "w_ple_proj"))


def _shard_shape(name):
    (k, m), ax = BIG_INFO[name]
    return (k // N_CHIPS, m) if ax == 0 else (k, m // N_CHIPS)


def _group_rows(group):
    offs, off = {}, 0
    for n in group:
        offs[n] = off
        off += _shard_shape(n)[0]
    return offs, off


def _pack_groups(shards, layer, dtype):
    return [jnp.concatenate([shards[n][layer].astype(dtype) for n in group], axis=0) for group in GROUPS]


def _unpack_full(gathered, groups):
    out = {}
    for group, arr in zip(groups, gathered):
        offs, _ = _group_rows(group)
        for n in group:
            rows, cols = _shard_shape(n)
            (k, m), ax = BIG_INFO[n]
            slab = arr[:, offs[n]:offs[n] + rows]
            out[n] = slab.reshape(k, m) if ax == 0 else jnp.transpose(slab, (1, 0, 2)).reshape(k, m)
    return out


def _pack_grads(gfull):
    out = []
    for group in GROUPS:
        parts = []
        for n in group:
            rows, cols = _shard_shape(n)
            ax = BIG_INFO[n][1]
            slab = (gfull[n].reshape(N_CHIPS, rows, cols) if ax == 0
                    else jnp.transpose(gfull[n].reshape(rows, N_CHIPS, cols), (1, 0, 2)))
            parts.append(slab)
        out.append(jnp.concatenate(parts, axis=1))
    return out


def _after(values, mark):
    values, _ = lax.optimization_barrier((values, mark))
    return values


def _reduce_scatter_begin(gsends, place, tag, ids):
    sibs = _sibling_halves(gsends, "rs_sibling_halves_" + tag, ids[0])
    tsends = [_chip_sums(g, s, place) for g, s in zip(gsends, sibs)]
    return tsends, _exchange_chip_sums(tsends, "rs_exchange_" + tag, ids[1])


def _reduce_scatter_finish(begun, place, tag, ids, hold):
    tsends, recvs = begun
    recvs = _after(recvs, hold)
    return _join_halves([_final_sum(t, r, place) for t, r in zip(tsends, recvs)], "rs_join_halves_" + tag, ids[2])


SMALL_SHAPES = {"rel_table": (NUM_BUCKETS, 2 * N_HEADS), "norm_mix_g": (DEPTH, D_MODEL), "qnorm_a_g": (DEPTH, HEAD_DIM),
                "knorm_a_g": (DEPTH, HEAD_DIM), "qnorm_b_g": (DEPTH, HEAD_DIM), "knorm_b_g": (DEPTH, HEAD_DIM),
                "sink_b": (DEPTH, N_HEADS), "norm_ffn_g": (DEPTH, D_MODEL), "norm_ple_g": (DEPTH, D_MODEL)}


def _pack_small(vals, last=None):
    flat = jnp.concatenate([vals[n].astype(F32).reshape(-1) for n in SMALL])
    tail = jnp.zeros((SMALL_ROWS * LANES - flat.shape[0],), F32)
    if last is not None:
        tail = tail.at[-1].set(last)
    return jnp.concatenate([flat, tail]).reshape(SMALL_ROWS, LANES)


def _unpack_small(packed):
    flat, out, off = packed.reshape(-1), {}, 0
    for n in SMALL:
        size = math.prod(SMALL_SHAPES[n])
        out[n] = flat[off:off + size].reshape(SMALL_SHAPES[n])
        off += size
    return out


def _adamw(w, gs, g_row, m, v, name):
    c1 = 1.0 - ADAM_B1 ** ADAM_STEP
    c2 = 1.0 - ADAM_B2 ** ADAM_STEP
    total, width = w.shape
    n_layers = len(gs)
    per = total // n_layers
    tm = max(t for t in range(8, 513, 8) if per % t == 0 and g_row % t == 0)
    nblk = per // tm

    def body(*refs):
        w_ref, g_refs = refs[0], refs[1:1 + n_layers]
        m_ref, v_ref, og, od, om, ov = refs[1 + n_layers:]
        layer = pl.program_id(0) // nblk
        g = g_refs[0][...]
        for l in range(1, n_layers):
            g = jnp.where(layer == l, g_refs[l][...], g)
        m_new = ADAM_B1 * m_ref[...] + (1.0 - ADAM_B1) * g
        v_new = ADAM_B2 * v_ref[...] + (1.0 - ADAM_B2) * (g * g)
        og[...] = g
        od[...] = -ADAM_LR * ((m_new / c1) / (jnp.sqrt(v_new / c2) + ADAM_EPS) + ADAM_WD * w_ref[...])
        om[...] = m_new
        ov[...] = v_new

    row = pl.BlockSpec((tm, width), lambda i: (i, 0))
    g_specs = [pl.BlockSpec((tm, width), lambda i, l=l: (g_row // tm + jnp.clip(i - l * nblk, 0, nblk - 1), 0))
               for l in range(n_layers)]
    return pl.pallas_call(
        body, out_shape=[jax.ShapeDtypeStruct((total, width), F32)] * 4, grid=(total // tm,),
        in_specs=[row] + g_specs + [row, row], out_specs=[row] * 4, name=name,
        compiler_params=_params("parallel"))(w, *gs, m, v)


def kernel(x, p, rel_table, norm_mix_g, w_in, qnorm_a_g, knorm_a_g, qnorm_b_g, knorm_b_g, sink_b, w_branch_a, w_branch_b, w_out, norm_ffn_g, w_ffn_gate, w_ffn_up, w_ffn_down, norm_ple_g, w_ple_gate, w_ple_proj, loss_target, m_rel_table, m_norm_mix_g, m_w_in, m_qnorm_a_g, m_knorm_a_g, m_qnorm_b_g, m_knorm_b_g, m_sink_b, m_w_branch_a, m_w_branch_b, m_w_out, m_norm_ffn_g, m_w_ffn_gate, m_w_ffn_up, m_w_ffn_down, m_norm_ple_g, m_w_ple_gate, m_w_ple_proj, v_rel_table, v_norm_mix_g, v_w_in, v_qnorm_a_g, v_knorm_a_g, v_qnorm_b_g, v_knorm_b_g, v_sink_b, v_w_branch_a, v_w_branch_b, v_w_out, v_norm_ffn_g, v_w_ffn_gate, v_w_ffn_up, v_w_ffn_down, v_norm_ple_g, v_w_ple_gate, v_w_ple_proj):
    given = dict(locals())

    def held(name, a):
        return jnp.swapaxes(a, 1, 2) if name in TRANSPOSED else a

    weights = {n: held(n, given[n]) for n in WEIGHTS}
    moments_m = {n: held(n, given["m_" + n]) for n in WEIGHTS}
    moments_v = {n: held(n, given["v_" + n]) for n in WEIGHTS}
    xi, yi, ci = _place()
    place = jnp.stack([2 * xi + yi, ci]).astype(jnp.int32)

    shards = [_pack_groups(weights, l, BF16) for l in range(DEPTH)]
    w_in0 = _allgather(shards[0][:1], "allgather_w_in_layer0", sequencer=9)
    rest0 = _allgather(_after(shards[0][1:], w_in0), "allgather_rest_layer0", sequencer=1)
    gathered = [w_in0 + rest0, None]
    small = {n: weights[n] for n in SMALL}

    def w_in_of(l, mark):
        return _unpack_full(_after(gathered[l][:1], (shards[1], mark) if l == 0 else mark), GROUPS[:1])["w_in"]

    def rest_of(l, mark):
        if l == 0:
            gathered[1] = _allgather(_after(shards[1], mark), "allgather_layer1", sequencer=2)
        return _unpack_full(_after(gathered[l][1:], mark), GROUPS[1:])

    loss, dx, gbig, gsmall, marks = _local_step(x[0], p[:, 0], loss_target[0], w_in_of, rest_of, small)

    gsends = [_pack_grads(gbig[l]) for l in range(DEPTH)]
    stages = {"layer1": (gsends[1], (3, 4, 5)), "rest_layer0": (gsends[0][1:], (6, 7, 8)),
              "w_in_layer0": (gsends[0][:1], (10, 11, 12))}
    begun = {tag: _reduce_scatter_begin(g, place, tag, ids) for tag, (g, ids) in stages.items()}

    def finish(tag, hold):
        return _reduce_scatter_finish(begun[tag], place, tag, stages[tag][1], hold)

    red1 = finish("layer1", marks[0]["attn_bwd_done"])
    rest0 = finish("rest_layer0", marks[0]["attn_bwd_done"])

    grads, delta, new_m, new_v = {}, {}, {}, {}

    def update(group, reduced):
        offs, _ = _group_rows(group)
        for n in group:
            shape = weights[n].shape
            two_d = lambda a: a.reshape(shape[0] * shape[1], shape[2])
            outs = _adamw(two_d(weights[n]), reduced, offs[n], two_d(moments_m[n]), two_d(moments_v[n]), "adamw_" + n)
            grads[n], delta[n], new_m[n], new_v[n] = (held(n, o.reshape(shape)) for o in outs)

    for gi in (1, 2):
        update(GROUPS[gi], _after([rest0[gi - 1], red1[gi]], begun["w_in_layer0"][0]))
    small_grads = _allreduce_small(_pack_small(gsmall, last=loss))
    g_, d_, m_, v_ = _adamw(_pack_small(weights), [small_grads], 0, _pack_small(moments_m), _pack_small(moments_v),
                            "adamw_small")
    grads.update(_unpack_small(g_))
    delta.update(_unpack_small(d_))
    new_m.update(_unpack_small(m_))
    new_v.update(_unpack_small(v_))
    others_done = [dx, d_] + [delta[n] for gi in (1, 2) for n in GROUPS[gi]]
    update(GROUPS[0], [finish("w_in_layer0", others_done)[0], red1[0]])

    return (small_grads[-1, -1], dx[None], *[grads[n] for n in WEIGHTS], *[delta[n] for n in WEIGHTS],
            *[new_m[n] for n in WEIGHTS], *[new_v[n] for n in WEIGHTS])
```

```python
import functools
import math

import jax
import jax.numpy as jnp
from jax import lax
from jax.experimental import pallas as pl
from jax.experimental.pallas import tpu as pltpu
from jax.experimental.pallas import tpu_sc as plsc

F32 = jnp.float32
BF16 = jnp.bfloat16
MESH_ID = pl.DeviceIdType.MESH

D_MODEL = 1024
DEPTH = 2
HEAD_DIM = 64
N_HEADS = 8
WIDTH = N_HEADS * HEAD_DIM
N_PAIRS = 4
ITEMS = 16
MAX_CHUNK = 1024
N_KV_B = 2
PLE_DIM = 256
D_FF = 2816
D_IN = 4352
OFF_QA, OFF_KA, OFF_VA, OFF_QB, OFF_KB, OFF_VB, OFF_GA, OFF_GB = 0, 512, 1024, 1536, 2048, 2176, 2304, 3328
DILATED = ((64, 1), (64, 4), (64, 16))
BLK_B = 128
NUM_BUCKETS = 32
MAX_DISTANCE = 1024
RMS_EPS = 1e-6
NEG_INF = -1e30
LANES = 128
ROW_TILE = 256
VMEM_LIMIT = 48 * 1024 * 1024

ADAM_LR, ADAM_B1, ADAM_B2, ADAM_EPS, ADAM_WD, ADAM_STEP = 0.001, 0.9, 0.999, 1e-08, 0.01, 10

TRANSPOSED = ("w_in", "w_ffn_gate", "w_ffn_up")
BIG = (
    ("w_in", (D_IN, D_MODEL), 0),
    ("w_branch_a", (WIDTH, D_MODEL), 1),
    ("w_branch_b", (WIDTH, D_MODEL), 1),
    ("w_out", (D_MODEL, D_MODEL), 0),
    ("w_ffn_gate", (D_FF, D_MODEL), 0),
    ("w_ffn_up", (D_FF, D_MODEL), 0),
    ("w_ffn_down", (D_FF, D_MODEL), 0),
    ("w_ple_gate", (D_MODEL, D_MODEL), 0),
    ("w_ple_proj", (PLE_DIM, D_MODEL), 1),
)
SMALL = ("rel_table", "norm_mix_g", "qnorm_a_g", "knorm_a_g", "qnorm_b_g", "knorm_b_g", "sink_b",
         "norm_ffn_g", "norm_ple_g")
WEIGHTS = ("rel_table", "norm_mix_g", "w_in", "qnorm_a_g", "knorm_a_g", "qnorm_b_g", "knorm_b_g", "sink_b",
           "w_branch_a", "w_branch_b", "w_out", "norm_ffn_g", "w_ffn_gate", "w_ffn_up", "w_ffn_down",
           "norm_ple_g", "w_ple_gate", "w_ple_proj")
N_CHIPS = 4
SMALL_ROWS = 64


def _params(*sem):
    return pltpu.CompilerParams(dimension_semantics=sem, vmem_limit_bytes=VMEM_LIMIT)


MM_VMEM_BUDGET = 40 * 1024 * 1024
STEP_OVERHEAD_S = 0.4e-6
TILE_DMA_BYTES_PER_S = 1.5e12


def _mm_dims(a, b, mode):
    if mode == "nn":
        return a.shape[0], b.shape[1], a.shape[1]
    if mode == "nt":
        return a.shape[0], b.shape[0], a.shape[1]
    return a.shape[1], b.shape[1], a.shape[0]


def _mm_tiles(m, n, pairs, tile_bytes, col_offsets, full_rows=False):
    best = None
    widths = [n] if full_rows else [t for t in range(LANES, n + 1, LANES) if n % t == 0 and all(o % t == 0 for o in col_offsets)]
    for tm in (t for t in range(LANES, m + 1, LANES) if m % t == 0):
        for tn in widths:
            io = sum(tm * k * ab + tn * k * bb for k, ab, bb in pairs) + tm * tn * sum(tile_bytes)
            casts = sum((tm * k * 2 if ab == 4 else 0) + (tn * k * 2 if bb == 4 else 0) for k, ab, bb in pairs)
            if 2 * io + len(pairs) * tm * tn * 4 + casts > MM_VMEM_BUDGET:
                continue
            cost = (m // tm) * (n // tn) * STEP_OVERHEAD_S + io / TILE_DMA_BYTES_PER_S
            if best is None or (cost, -tm) < best[0]:
                best = ((cost, -tm), tm, tn)
    return best[1], best[2]


def _mm_fused(pairs, extras, epilogue, out_dtypes, name, next_gain=None):
    m, n, _ = _mm_dims(*pairs[0])
    assert all(_mm_dims(*p)[:2] == (m, n) for p in pairs)
    with_norm = next_gain is not None
    out_dtypes = list(out_dtypes) + ([BF16] if with_norm else [])
    tm, tn = _mm_tiles(
        m, n, [(_mm_dims(a, b, mode)[2], a.dtype.itemsize, b.dtype.itemsize) for a, b, mode in pairs],
        [e.dtype.itemsize for e, _ in extras] + [jnp.dtype(d).itemsize for d in out_dtypes], [off for _, off in extras],
        full_rows=with_norm)
    dims = {"nn": (((1,), (0,)), ((), ())), "nt": (((1,), (1,)), ((), ())), "tn": (((0,), (0,)), ((), ()))}
    in_specs, args = [], []
    for a, b, mode in pairs:
        k = _mm_dims(a, b, mode)[2]
        in_specs.append(pl.BlockSpec((k, tm), lambda i, j: (0, i)) if mode == "tn" else pl.BlockSpec((tm, k), lambda i, j: (i, 0)))
        in_specs.append(pl.BlockSpec((tn, k), lambda i, j: (j, 0)) if mode == "nt" else pl.BlockSpec((k, tn), lambda i, j: (0, j)))
        args += [a, b]
    for e, off in extras:
        in_specs.append(pl.BlockSpec((tm, tn), lambda i, j, o=off // tn: (i, o + j)))
        args.append(e)
    n_pairs, n_tiles = len(pairs), 2 * len(pairs) + len(extras)
    if with_norm:
        in_specs.append(pl.BlockSpec((1, n), lambda i, j: (0, 0)))
        args.append(next_gain)
    n_in = len(args)

    def body(*refs):
        products = [lax.dot_general(refs[2 * p][...].astype(BF16), refs[2 * p + 1][...].astype(BF16), dims[pairs[p][2]],
                                    preferred_element_type=F32) for p in range(n_pairs)]
        outs = list(epilogue(products, [r[...] for r in refs[2 * n_pairs:n_tiles]]))
        if with_norm:
            y = outs[0]
            outs.append((y * lax.rsqrt(jnp.mean(y * y, axis=-1, keepdims=True) + RMS_EPS)) * refs[n_tiles][...])
        for r, o in zip(refs[n_in:], outs):
            r[...] = o.astype(r.dtype)

    tile = pl.BlockSpec((tm, tn), lambda i, j: (i, j))
    return pl.pallas_call(
        body, out_shape=[jax.ShapeDtypeStruct((m, n), d) for d in out_dtypes], grid=(m // tm, n // tn),
        in_specs=in_specs, out_specs=[tile] * len(out_dtypes), name=name,
        compiler_params=_params("parallel", "parallel"))(*args)


def _mm(a, b, mode, out_dtype, name, res=None):
    if res is None:
        return _mm_fused([(a, b, mode)], [], lambda products, extra: products, [out_dtype], name)[0]
    return _mm_fused([(a, b, mode)], [(res, 0)], lambda products, extra: [products[0] + extra[0]], [out_dtype], name)[0]


def _mm_res_norm(a, b, mode, res, gain, name):
    return _mm_fused([(a, b, mode)], [(res, 0)], lambda products, extra: [products[0] + extra[0]], [F32], name,
                     next_gain=gain)


def _ew(fn, ins, out_dtypes, name):
    rows, width = ins[0].shape
    n_in = len(ins)

    def body(*refs):
        outs = fn(*[r[...] for r in refs[:n_in]])
        for r, o in zip(refs[n_in:], outs):
            r[...] = o.astype(r.dtype)

    row = pl.BlockSpec((ROW_TILE, width), lambda i: (i, 0))
    return pl.pallas_call(
        body, out_shape=[jax.ShapeDtypeStruct((rows, width), dt) for dt in out_dtypes], grid=(rows // ROW_TILE,),
        in_specs=[row] * n_in, out_specs=[row] * len(out_dtypes), name=name, compiler_params=_params("parallel"))(*ins)


def _sigmoid(x):
    return 1.0 / (1.0 + jnp.exp(-x))


def _seg_sum(v):
    outs = []
    for k in range(v.shape[1] // LANES):
        vp = v[:, k * LANES:(k + 1) * LANES]
        left = lax.broadcasted_iota(jnp.int32, vp.shape, 1) < HEAD_DIM
        sl = jnp.sum(jnp.where(left, vp, 0.0), axis=-1, keepdims=True)
        sr = jnp.sum(jnp.where(left, 0.0, vp), axis=-1, keepdims=True)
        outs.append(jnp.where(left, sl, sr))
    return outs[0] if len(outs) == 1 else jnp.concatenate(outs, axis=1)


def _seg_rstd(x):
    return lax.rsqrt(_seg_sum(x * x) * (1.0 / HEAD_DIM) + RMS_EPS)


def _rms_fwd(x, g, name):
    rows, d = x.shape
    tm = ROW_TILE

    def body(x_ref, g_ref, h_ref):
        xv = x_ref[...]
        r = lax.rsqrt(jnp.mean(xv * xv, axis=-1, keepdims=True) + RMS_EPS)
        h_ref[...] = ((xv * r) * g_ref[...]).astype(BF16)

    return pl.pallas_call(
        body, out_shape=jax.ShapeDtypeStruct((rows, d), BF16), grid=(rows // tm,),
        in_specs=[pl.BlockSpec((tm, d), lambda i: (i, 0)), pl.BlockSpec((1, d), lambda i: (0, 0))],
        out_specs=pl.BlockSpec((tm, d), lambda i: (i, 0)), name=name,
        compiler_params=_params("parallel"))(x, g)


def _mm_rms_bwd(pairs, x, g, dres, name):
    rows, d = x.shape
    assert all(_mm_dims(*p)[:2] == (rows, d) for p in pairs)
    kbytes = [(_mm_dims(a, b, mode)[2], a.dtype.itemsize, b.dtype.itemsize) for a, b, mode in pairs]
    tm = max(t for t in (512, 256, 128) if rows % t == 0 and
             2 * (sum(t * k * ab + d * k * bb for k, ab, bb in kbytes) + t * d * 14) + 2 * t * d * 4 <= MM_VMEM_BUDGET)
    dims = {"nn": (((1,), (0,)), ((), ())), "nt": (((1,), (1,)), ((), ()))}
    n_pairs = len(pairs)

    def body(*refs):
        x_ref, g_ref, dres_ref = refs[2 * n_pairs:2 * n_pairs + 3]
        dx_ref, dxb_ref, dg_ref = refs[2 * n_pairs + 3:]
        dhv = functools.reduce(lambda u, v: u + v, [
            lax.dot_general(refs[2 * p][...].astype(BF16), refs[2 * p + 1][...].astype(BF16), dims[pairs[p][2]],
                            preferred_element_type=F32) for p in range(n_pairs)])
        xv = x_ref[...]
        r = lax.rsqrt(jnp.mean(xv * xv, axis=-1, keepdims=True) + RMS_EPS)
        xh = xv * r
        dxh = dhv * g_ref[...]
        dxv = dres_ref[...] + r * (dxh - xh * jnp.mean(dxh * xh, axis=-1, keepdims=True))
        dx_ref[...] = dxv
        dxb_ref[...] = dxv.astype(BF16)
        part = jnp.sum(dhv * xh, axis=0, keepdims=True)

        @pl.when(pl.program_id(0) == 0)
        def _():
            dg_ref[...] = part

        @pl.when(pl.program_id(0) > 0)
        def _():
            dg_ref[...] += part

    row = pl.BlockSpec((tm, d), lambda i: (i, 0))
    vec = pl.BlockSpec((1, d), lambda i: (0, 0))
    in_specs, args = [], []
    for a, b, mode in pairs:
        in_specs += [pl.BlockSpec((tm, a.shape[1]), lambda i: (i, 0)), pl.BlockSpec(b.shape, lambda i: (0, 0))]
        args += [a, b]
    return pl.pallas_call(
        body, out_shape=[jax.ShapeDtypeStruct((rows, d), F32), jax.ShapeDtypeStruct((rows, d), BF16),
                         jax.ShapeDtypeStruct((1, d), F32)],
        grid=(rows // tm,), in_specs=in_specs + [row, vec, row], out_specs=[row, row, vec],
        name=name, compiler_params=_params("arbitrary"))(*args, x, g, dres)


def _loss_grad(y, t):
    rows, d = y.shape
    tm = ROW_TILE

    def body(y_ref, t_ref, dy_ref, l_ref):
        e = y_ref[...] - t_ref[...]
        dy_ref[...] = e * (1.0 / d)
        part = jnp.zeros((1, LANES), F32) + jnp.sum(e * e) * (0.5 / d)

        @pl.when(pl.program_id(0) == 0)
        def _():
            l_ref[...] = part

        @pl.when(pl.program_id(0) > 0)
        def _():
            l_ref[...] += part

    row = pl.BlockSpec((tm, d), lambda i: (i, 0))
    return pl.pallas_call(
        body, out_shape=[jax.ShapeDtypeStruct((rows, d), F32), jax.ShapeDtypeStruct((1, LANES), F32)],
        grid=(rows // tm,), in_specs=[row, row], out_specs=[row, pl.BlockSpec((1, LANES), lambda i: (0, 0))],
        name="loss_grad", compiler_params=_params("arbitrary"))(y, t)


def _swap_halves(v):
    return pltpu.roll(v, HEAD_DIM, axis=1)


def _expand_kv(kv):
    left = lax.broadcasted_iota(jnp.int32, kv.shape, 1) < HEAD_DIM
    sw = _swap_halves(kv)
    h0 = jnp.where(left, kv, sw)
    h1 = jnp.where(left, sw, kv)
    return jnp.concatenate([h0, h0, h1, h1], axis=1)


def _reduce_kv(dkv):
    left = lax.broadcasted_iota(jnp.int32, (dkv.shape[0], LANES), 1) < HEAD_DIM
    t = dkv[:, 0:LANES] + dkv[:, LANES:2 * LANES]
    u = dkv[:, 2 * LANES:3 * LANES] + dkv[:, 3 * LANES:4 * LANES]
    t = t + _swap_halves(t)
    u = u + _swap_halves(u)
    return jnp.where(left, t, u)


def _qknorm_fwd(proj, gqa, gka, gqb, gkb):
    rows = proj.shape[0]
    tm = ROW_TILE

    def body(qa_ref, ka_ref, qb_ref, kb_ref, vb_ref, gqa_ref, gka_ref, gqb_ref, gkb_ref, oqa, oka, oqb, okb, ovb):
        for src, g_ref, dst in ((qa_ref, gqa_ref, oqa), (ka_ref, gka_ref, oka), (qb_ref, gqb_ref, oqb)):
            xv = src[...]
            dst[...] = (xv * _seg_rstd(xv)) * g_ref[...]
        kv = kb_ref[...]
        okb[...] = _expand_kv((kv * _seg_rstd(kv)) * gkb_ref[...])
        ovb[...] = _expand_kv(vb_ref[...])

    def win(width, off):
        return pl.BlockSpec((tm, width), lambda i: (i, off // width))

    vec = lambda w: pl.BlockSpec((1, w), lambda i: (0, 0))
    out = pl.BlockSpec((tm, WIDTH), lambda i: (i, 0))
    return pl.pallas_call(
        body, out_shape=[jax.ShapeDtypeStruct((rows, WIDTH), F32)] * 5, grid=(rows // tm,),
        in_specs=[win(WIDTH, OFF_QA), win(WIDTH, OFF_KA), win(WIDTH, OFF_QB), win(LANES, OFF_KB), win(LANES, OFF_VB),
                  vec(WIDTH), vec(WIDTH), vec(WIDTH), vec(LANES)],
        out_specs=[out] * 5, name="qknorm_fwd", compiler_params=_params("parallel"))(
            proj, proj, proj, proj, proj, gqa, gka, gqb, gkb)


def _norm_bwd(xv, g, dy):
    r = _seg_rstd(xv)
    xh = xv * r
    dxh = dy * g
    dx = r * (dxh - xh * (_seg_sum(dxh * xh) * (1.0 / HEAD_DIM)))
    return dx, jnp.sum(dy * xh, axis=0, keepdims=True)


def _qknorm_bwd(proj, gqa, gka, gqb, gkb, dqa, dka, dva, dqb, dkb, dvb, dga, dgb):
    rows = proj.shape[0]
    tm = ROW_TILE
    n_a = len(dqa)

    def body(*refs):
        qa_ref, ka_ref, qb_ref, kb_ref, gqa_ref, gka_ref, gqb_ref, gkb_ref = refs[:8]
        pos = 8
        dqa_refs, dka_refs, dva_refs = refs[pos:pos + n_a], refs[pos + n_a:pos + 2 * n_a], refs[pos + 2 * n_a:pos + 3 * n_a]
        pos += 3 * n_a
        dqb_ref, dkb_ref, dvb_ref, dga_ref, dgb_ref = refs[pos:pos + 5]
        dproj_ref, ogqa, ogka, ogqb, ogkb = refs[pos + 5:]

        def total(rs):
            acc = rs[0][...]
            for r in rs[1:]:
                acc = acc + r[...]
            return acc

        dx_qa, p_qa = _norm_bwd(qa_ref[...], gqa_ref[...], total(dqa_refs))
        dx_ka, p_ka = _norm_bwd(ka_ref[...], gka_ref[...], total(dka_refs))
        dx_qb, p_qb = _norm_bwd(qb_ref[...], gqb_ref[...], dqb_ref[...])
        dx_kb, p_kb = _norm_bwd(kb_ref[...], gkb_ref[...], _reduce_kv(dkb_ref[...]))
        dproj_ref[:, OFF_QA:OFF_QA + WIDTH] = dx_qa.astype(BF16)
        dproj_ref[:, OFF_KA:OFF_KA + WIDTH] = dx_ka.astype(BF16)
        dproj_ref[:, OFF_VA:OFF_VA + WIDTH] = total(dva_refs).astype(BF16)
        dproj_ref[:, OFF_QB:OFF_QB + WIDTH] = dx_qb.astype(BF16)
        dproj_ref[:, OFF_KB:OFF_KB + LANES] = dx_kb.astype(BF16)
        dproj_ref[:, OFF_VB:OFF_VB + LANES] = _reduce_kv(dvb_ref[...]).astype(BF16)
        dproj_ref[:, OFF_GA:OFF_GB] = dga_ref[...]
        dproj_ref[:, OFF_GB:D_IN] = dgb_ref[...]
        first = pl.program_id(0) == 0
        for o_ref, part in ((ogqa, p_qa), (ogka, p_ka), (ogqb, p_qb), (ogkb, p_kb)):
            @pl.when(first)
            def _(o_ref=o_ref, part=part):
                o_ref[...] = part

            @pl.when(jnp.logical_not(first))
            def _(o_ref=o_ref, part=part):
                o_ref[...] += part

    def win(width, off):
        return pl.BlockSpec((tm, width), lambda i: (i, off // width))

    vec = lambda w: pl.BlockSpec((1, w), lambda i: (0, 0))
    row = lambda w: pl.BlockSpec((tm, w), lambda i: (i, 0))
    in_specs = [win(WIDTH, OFF_QA), win(WIDTH, OFF_KA), win(WIDTH, OFF_QB), win(LANES, OFF_KB),
                vec(WIDTH), vec(WIDTH), vec(WIDTH), vec(LANES)]
    in_specs += [row(WIDTH)] * (3 * n_a + 3) + [row(D_MODEL)] * 2
    return pl.pallas_call(
        body,
        out_shape=[jax.ShapeDtypeStruct((rows, D_IN), BF16), jax.ShapeDtypeStruct((1, WIDTH), F32),
                   jax.ShapeDtypeStruct((1, WIDTH), F32), jax.ShapeDtypeStruct((1, WIDTH), F32),
                   jax.ShapeDtypeStruct((1, LANES), F32)],
        grid=(rows // tm,), in_specs=in_specs,
        out_specs=[row(D_IN), vec(WIDTH), vec(WIDTH), vec(WIDTH), vec(LANES)],
        name="qknorm_bwd", compiler_params=_params("arbitrary"))(
            proj, proj, proj, proj, gqa, gka, gqb, gkb, *dqa, *dka, *dva, dqb, dkb, dvb, dga, dgb)


def _t5_bucket(rel):
    half_b = NUM_BUCKETS // 2
    max_exact = half_b // 2
    sign = jnp.where(rel > 0, half_b, 0)
    n = jnp.abs(rel)
    nf = jnp.maximum(n, 1).astype(F32)
    large = max_exact + (jnp.log(nf / max_exact) / math.log(MAX_DISTANCE / max_exact)
                         * (half_b - max_exact)).astype(jnp.int32)
    large = jnp.minimum(large, half_b - 1)
    return sign + jnp.where(n < max_exact, n, large)


def _band_buckets(blk, dilation):
    i = jnp.arange(blk, dtype=jnp.int32)[:, None]
    j = jnp.arange(3 * blk, dtype=jnp.int32)[None, :]
    rel = j - blk - i
    return jnp.where(jnp.abs(rel) <= blk, _t5_bucket(rel * dilation), -1)


def _bias_tiles(table, buckets, head_off, name):
    blk = buckets.shape[0]

    def body(tab_ref, bk_ref, o_ref):
        h = pl.program_id(0) + head_off
        bk = bk_ref[...]
        acc = jnp.full(bk.shape, NEG_INF, F32)
        for b in range(NUM_BUCKETS):
            acc = jnp.where(bk == b, tab_ref[b, h], acc)
        o_ref[0] = acc

    return pl.pallas_call(
        body, out_shape=jax.ShapeDtypeStruct((N_HEADS, blk, 3 * blk), F32), grid=(N_HEADS,),
        in_specs=[pl.BlockSpec(memory_space=pltpu.SMEM), pl.BlockSpec((blk, 3 * blk), lambda h: (0, 0))],
        out_specs=pl.BlockSpec((1, blk, 3 * blk), lambda h: (h, 0, 0)),
        name=name, compiler_params=_params("parallel"))(table, buckets)


def _table_grad(dbias, buckets, name):
    blk = buckets.shape[0]

    def body(db_ref, bk_ref, o_ref):
        bk = bk_ref[...]
        dbv = db_ref[0]
        lane = lax.broadcasted_iota(jnp.int32, (1, LANES), 1)
        acc = jnp.zeros((1, LANES), F32)
        for b in range(NUM_BUCKETS):
            acc = jnp.where(lane == b, jnp.sum(jnp.where(bk == b, dbv, 0.0)), acc)
        o_ref[0] = acc

    out = pl.pallas_call(
        body, out_shape=jax.ShapeDtypeStruct((N_HEADS, 1, LANES), F32), grid=(N_HEADS,),
        in_specs=[pl.BlockSpec((1, blk, 3 * blk), lambda h: (h, 0, 0)), pl.BlockSpec((blk, 3 * blk), lambda h: (0, 0))],
        out_specs=pl.BlockSpec((1, 1, LANES), lambda h: (h, 0, 0)),
        name=name, compiler_params=_params("parallel"))(dbias, buckets)
    return out[:, 0, :NUM_BUCKETS]


def _dot_nt(a, b):
    return lax.dot_general(a, b, (((1,), (1,)), ((), ())), preferred_element_type=F32)


def _dot_tn(a, b):
    return lax.dot_general(a, b, (((0,), (0,)), ((), ())), preferred_element_type=F32)


def _stack_pair(x2, left):
    return jnp.concatenate([jnp.where(left, x2, 0.0), jnp.where(left, 0.0, x2)], axis=0).astype(BF16)


def _attn_geometry(blk, d):
    halo = blk * d
    subs = max(1, min(ITEMS // d, MAX_CHUNK // halo))
    return subs, min(d, ITEMS // subs), halo


def _item_of(j, r0, per_group):
    return j // per_group, r0 + j % per_group


def _span_rows(ref, first, r, blk, d):
    if d == 1:
        return ref[first:first + blk, :]
    return ref[pl.ds(first + r, blk, stride=d), :]


def _set_span_rows(ref, first, r, blk, d, val, add=False):
    idx = slice(first, first + blk) if d == 1 else pl.ds(first + r, blk, stride=d)
    ref[idx, :] = ref[idx, :] + val if add else val


def _for_groups(group, groups, per_group):
    if groups == 1:
        group(0)
    else:
        def step(g, carry):
            group(g * per_group)
            return carry

        lax.fori_loop(0, groups, step, 0)


def _key_rows(p_ref, c_ref, n_ref, s, r, subs, halo, blk, d):
    parts = []
    for span in (s - 1, s, s + 1):
        if span < 0:
            parts.append(_span_rows(p_ref, 0, r, blk, d))
        elif span == subs:
            parts.append(_span_rows(n_ref, 0, r, blk, d))
        else:
            parts.append(_span_rows(c_ref, span * halo, r, blk, d))
    return jnp.concatenate(parts, axis=0)


def _item_penalty(t, nct, s, subs, blk):
    first_ok = True if s > 0 else t > 0
    last_ok = True if s < subs - 1 else t < nct - 1
    col = lax.broadcasted_iota(jnp.int32, (1, 3 * blk), 1)
    ok = jnp.logical_and(jnp.logical_or(col >= blk, first_ok), jnp.logical_or(col < 2 * blk, last_ok))
    return jnp.where(ok, 0.0, NEG_INF).astype(F32)


def _attn_specs(seq, blk, d, step_of, col=0):
    subs, _, halo = _attn_geometry(blk, d)
    last, first = seq // halo - 1, col // LANES
    cur = pl.BlockSpec((subs * halo, LANES), lambda hp, t: (step_of(t), first + hp))
    prev = pl.BlockSpec((halo, LANES), lambda hp, t: (jnp.clip(step_of(t) * subs - 1, 0, last), first + hp))
    nxt = pl.BlockSpec((halo, LANES), lambda hp, t: (jnp.minimum((step_of(t) + 1) * subs, last), first + hp))
    return cur, prev, nxt


def _attn_fwd(q, k, v, bias, sink, blk, d, name, v_col=0):
    seq = q.shape[0]
    subs, per_group, halo = _attn_geometry(blk, d)
    items, chunk, groups = subs * per_group, subs * halo, d // per_group
    nct = seq // chunk
    has_sink = sink is not None
    scale = HEAD_DIM ** -0.5

    def body(*refs):
        q_ref, kp, kc, kn, vp, vc, vn, b_ref = refs[:8]
        s_ref = refs[8] if has_sink else None
        o_ref, l_ref = refs[-2], refs[-1]
        t = pl.program_id(1)
        left = lax.broadcasted_iota(jnp.int32, (1, LANES), 1) < HEAD_DIM
        bias2 = b_ref[...]

        def group(r0):
            scores, vcats = [], []
            for j in range(items):
                s, r = _item_of(j, r0, per_group)
                qs = _stack_pair(_span_rows(q_ref, s * halo, r, blk, d) * scale, left)
                kcat = _key_rows(kp, kc, kn, s, r, subs, halo, blk, d).astype(BF16)
                scores.append(_dot_nt(qs, kcat) + bias2 + _item_penalty(t, nct, s, subs, blk))
                vcats.append(_key_rows(vp, vc, vn, s, r, subs, halo, blk, d).astype(BF16))
            ms = [jnp.max(s, axis=-1, keepdims=True) for s in scores]
            if has_sink:
                sk = s_ref[...]
                ms = [jnp.maximum(m, sk) for m in ms]
            ps = [jnp.exp(s - m) for s, m in zip(scores, ms)]
            dens = [jnp.sum(p, axis=-1, keepdims=True) for p in ps]
            if has_sink:
                dens = [den + jnp.exp(sk - m) for den, m in zip(dens, ms)]
            pns = [(p * (1.0 / den)).astype(BF16) for p, den in zip(ps, dens)]
            lses = [m + jnp.log(den) for m, den in zip(ms, dens)]
            for j in range(items):
                s, r = _item_of(j, r0, per_group)
                o2 = jnp.dot(pns[j], vcats[j], preferred_element_type=F32)
                _set_span_rows(o_ref, s * halo, r, blk, d, jnp.where(left, o2[:blk], o2[blk:]))
                _set_span_rows(l_ref, s * halo, r, blk, d, jnp.where(left, lses[j][:blk], lses[j][blk:]))

        _for_groups(group, groups, per_group)

    cur, prev, nxt = _attn_specs(seq, blk, d, lambda t: t)
    v_cur, v_prev, v_nxt = _attn_specs(seq, blk, d, lambda t: t, v_col)
    in_specs = [cur, prev, cur, nxt, v_prev, v_cur, v_nxt, pl.BlockSpec((2 * blk, 3 * blk), lambda hp, t: (hp, 0))]
    args = [q, k, k, k, v, v, v, bias]
    if has_sink:
        in_specs.append(pl.BlockSpec((2 * blk, 1), lambda hp, t: (hp, 0)))
        args.append(sink)
    return pl.pallas_call(
        body, out_shape=[jax.ShapeDtypeStruct((seq, WIDTH), F32)] * 2, grid=(N_PAIRS, nct),
        in_specs=in_specs, out_specs=[cur, cur], name=name,
        compiler_params=_params("parallel", "parallel"))(*args)


def _attn_bwd(q, k, v, do, lse, delta, bias, sink, blk, d, name, v_col=0):
    seq = q.shape[0]
    subs, per_group, halo = _attn_geometry(blk, d)
    items, chunk, groups = subs * per_group, subs * halo, d // per_group
    nct = seq // chunk
    has_sink = sink is not None
    n_in = 12 if has_sink else 11
    scale = HEAD_DIM ** -0.5

    def body(*refs):
        q_ref, kp, kc, kn, vp, vc, vn, do_ref, l_ref, d_ref, b_ref = refs[:11]
        s_ref = refs[11] if has_sink else None
        dq_ref, dk_ref, dv_ref, db_ref = refs[n_in:n_in + 4]
        ds_ref = refs[n_in + 4] if has_sink else None
        wk, wv = refs[-2], refs[-1]
        t = pl.program_id(1)

        @pl.when(t == 0)
        def _():
            wk[...] = jnp.zeros_like(wk)
            wv[...] = jnp.zeros_like(wv)
            db_ref[...] = jnp.zeros_like(db_ref)
            if has_sink:
                ds_ref[...] = jnp.zeros_like(ds_ref)

        @pl.when(t > 0)
        def _():
            for w in (wk, wv):
                keep = w[chunk:2 * chunk + halo]
                w[0:chunk + halo] = keep
                w[chunk + halo:2 * chunk + halo] = jnp.zeros((chunk, LANES), F32)

        @pl.when(t < nct)
        def _():
            lane = lax.broadcasted_iota(jnp.int32, (1, LANES), 1)
            left = lane < HEAD_DIM
            bias2 = b_ref[...]

            def group(r0):
                qss, doss, kcats, scores, dps, lcols, dcols = [], [], [], [], [], [], []
                for j in range(items):
                    s, r = _item_of(j, r0, per_group)
                    qs = _stack_pair(_span_rows(q_ref, s * halo, r, blk, d) * scale, left)
                    dos = _stack_pair(_span_rows(do_ref, s * halo, r, blk, d), left)
                    kcat = _key_rows(kp, kc, kn, s, r, subs, halo, blk, d).astype(BF16)
                    vcat = _key_rows(vp, vc, vn, s, r, subs, halo, blk, d).astype(BF16)
                    l2, d2 = _span_rows(l_ref, s * halo, r, blk, d), _span_rows(d_ref, s * halo, r, blk, d)
                    lcols.append(jnp.concatenate([jnp.max(jnp.where(left, l2, NEG_INF), axis=-1, keepdims=True),
                                                  jnp.max(jnp.where(left, NEG_INF, l2), axis=-1, keepdims=True)], axis=0))
                    dcols.append(jnp.concatenate([jnp.sum(jnp.where(lane == 0, d2, 0.0), axis=-1, keepdims=True),
                                                  jnp.sum(jnp.where(lane == HEAD_DIM, d2, 0.0), axis=-1, keepdims=True)],
                                                 axis=0))
                    scores.append(_dot_nt(qs, kcat) + bias2 + _item_penalty(t, nct, s, subs, blk))
                    dps.append(_dot_nt(dos, vcat))
                    qss.append(qs)
                    doss.append(dos)
                    kcats.append(kcat)
                ps = [jnp.exp(s - lc) for s, lc in zip(scores, lcols)]
                dss = [p * (dp - dc) for p, dp, dc in zip(ps, dps, dcols)]
                db_ref[...] += functools.reduce(lambda a, b: a + b, dss)
                if has_sink:
                    sk = s_ref[...]
                    ds_ref[...] -= functools.reduce(lambda a, b: a + b, [dc * jnp.exp(sk - lc) for dc, lc in zip(dcols, lcols)])
                dsbs = [ds.astype(BF16) for ds in dss]
                for j in range(items):
                    s, r = _item_of(j, r0, per_group)
                    dq2 = jnp.dot(dsbs[j], kcats[j], preferred_element_type=F32) * scale
                    _set_span_rows(dq_ref, s * halo, r, blk, d, jnp.where(left, dq2[:blk], dq2[blk:]))
                news = [(_dot_tn(dsbs[j], qss[j]), _dot_tn(ps[j].astype(BF16), doss[j])) for j in range(items)]
                for which, w in enumerate((wk, wv)):
                    for jr in range(per_group):
                        for sp in range(-1, subs + 1):
                            parts = [news[s * per_group + jr][which][(sp - s + 1) * blk:(sp - s + 2) * blk]
                                     for s in range(subs) if 0 <= sp - s + 1 < 3]
                            _set_span_rows(w, chunk + sp * halo, r0 + jr, blk, d,
                                           functools.reduce(lambda x, y: x + y, parts), add=True)

            _for_groups(group, groups, per_group)

        dk_ref[...] = wk[0:chunk]
        dv_ref[...] = wv[0:chunk]

    cur, prev, nxt = _attn_specs(seq, blk, d, lambda t: jnp.minimum(t, nct - 1))
    v_cur, v_prev, v_nxt = _attn_specs(seq, blk, d, lambda t: jnp.minimum(t, nct - 1), v_col)
    lag = pl.BlockSpec((chunk, LANES), lambda hp, t: (jnp.maximum(t - 1, 0), hp))
    band = pl.BlockSpec((2 * blk, 3 * blk), lambda hp, t: (hp, 0))
    col = pl.BlockSpec((2 * blk, 1), lambda hp, t: (hp, 0))
    in_specs = [cur, prev, cur, nxt, v_prev, v_cur, v_nxt, cur, cur, cur, band]
    args = [q, k, k, k, v, v, v, do, lse, delta, bias]
    out_shape = [jax.ShapeDtypeStruct((seq, WIDTH), F32)] * 3 + [jax.ShapeDtypeStruct((N_HEADS * blk, 3 * blk), F32)]
    out_specs = [cur, lag, lag, band]
    if has_sink:
        in_specs.append(col)
        args.append(sink)
        out_shape.append(jax.ShapeDtypeStruct((N_HEADS * blk, 1), F32))
        out_specs.append(col)
    window = pltpu.VMEM((2 * chunk + halo, LANES), F32)
    return pl.pallas_call(
        body, out_shape=out_shape, grid=(N_PAIRS, nct + 1), in_specs=in_specs, out_specs=out_specs,
        scratch_shapes=[window, window], name=name,
        compiler_params=_params("arbitrary", "arbitrary"))(*args)


def _combine_patterns(outs, lses):
    def combine(*tiles):
        os_, ls = tiles[:len(outs)], tiles[len(outs):]
        m = functools.reduce(jnp.maximum, ls)
        es = [jnp.exp(l - m) for l in ls]
        den = functools.reduce(lambda a, b: a + b, es)
        num = functools.reduce(lambda a, b: a + b, [e * o for e, o in zip(es, os_)])
        return num / den, m + jnp.log(den)

    return _ew(combine, [*outs, *lses], [F32, F32], "combine_a")


def _row_dots(dy, y, name):
    return _ew(lambda dy_, y_: (_seg_sum(dy_ * y_),), [dy, y], [F32], name)[0]


def _tile_gain(g, reps):
    return jnp.tile(g[None, :], (1, reps))


def _local_step(x, p, target, w_in_of, rest_of, small):
    rel_table = small["rel_table"]
    buckets_a = [_band_buckets(blk, d) for blk, d in DILATED]
    buckets_b = _band_buckets(BLK_B, 1)
    bias_a = [_bias_tiles(rel_table, bk, 0, "bias_a").reshape(N_HEADS * bk.shape[0], -1) for bk in buckets_a]
    bias_b = _bias_tiles(rel_table, buckets_b, N_HEADS, "bias_b").reshape(N_HEADS * BLK_B, -1)

    saved = []
    for l in range(DEPTH):
        g_mix, g_ffn, g_ple = (small[n][l][None, :] for n in ("norm_mix_g", "norm_ffn_g", "norm_ple_g"))
        gqa, gka, gqb = (_tile_gain(small[n][l], N_HEADS) for n in ("qnorm_a_g", "knorm_a_g", "qnorm_b_g"))
        gkb = _tile_gain(small["knorm_b_g"][l], N_KV_B)
        sink = jnp.repeat(small["sink_b"][l], BLK_B)[:, None]

        h = _rms_fwd(x, g_mix, "rms_mix") if l == 0 else h_next
        w_in = w_in_of(l, (h, bias_a, bias_b))
        proj = _mm(h, w_in, "nt", F32, "mm_in")
        qa, ka, qb, kb, vb = _qknorm_fwd(proj, gqa, gka, gqb, gkb)
        outs, lses = [], []
        for (blk, d), bias in zip(DILATED, bias_a):
            o, ls = _attn_fwd(qa, ka, proj, bias, None, blk, d, f"attn_a{d}_fwd", v_col=OFF_VA)
            outs.append(o)
            lses.append(ls)
        ya, lse_a = _combine_patterns(outs, lses)
        yb, lse_b = _attn_fwd(qb, kb, vb, bias_b, sink, BLK_B, 1, "attn_b_fwd")
        w = dict(rest_of(l, yb), w_in=w_in)
        def gate(products, extra):
            (ca_, cb_), (ga_, gb_) = products, extra
            return _sigmoid(ga_) * ca_ + _sigmoid(gb_) * cb_, ca_, cb_

        merged, ca, cb = _mm_fused([(ya, w["w_branch_a"], "nn"), (yb, w["w_branch_b"], "nn")],
                                   [(proj, OFF_GA), (proj, OFF_GB)], gate, [BF16, BF16, BF16], "mm_branches_gate")
        x1, h2 = _mm_res_norm(merged, w["w_out"], "nn", x, g_ffn, "mm_out_norm")

        def swiglu(products, extra):
            a_, u_ = products
            return (a_ * _sigmoid(a_)) * u_, a_, u_

        hid, a, u = _mm_fused([(h2, w["w_ffn_gate"], "nt"), (h2, w["w_ffn_up"], "nt")], [], swiglu,
                              [BF16, BF16, BF16], "mm_ffn_gate_up")
        x2, h3 = _mm_res_norm(hid, w["w_ffn_down"], "nn", x1, g_ple, "mm_ffn_down_norm")

        def ple(products, extra):
            z_, e_ = products
            return extra[0] + _sigmoid(z_) * e_, z_, e_

        next_gain = small["norm_mix_g"][l + 1][None, :] if l + 1 < DEPTH else None
        x3, z, e, *rest = _mm_fused([(h3, w["w_ple_gate"], "nn"), (p[l], w["w_ple_proj"], "nn")], [(x2, 0)], ple,
                                    [F32, BF16, BF16], "mm_ple", next_gain=next_gain)
        h_next = rest[0] if rest else None
        saved.append(dict(w=w, x0=x, h=h, proj=proj, qa=qa, ka=ka, qb=qb, kb=kb, vb=vb, ya=ya, lse_a=lse_a,
                          yb=yb, lse_b=lse_b, ca=ca, cb=cb, merged=merged, x1=x1, h2=h2, a=a, u=u, hid=hid,
                          x2=x2, h3=h3, z=z, e=e))
        x = x3

    dx, loss_acc = _loss_grad(x, target)
    loss = loss_acc[0, 0]

    gbig = [{} for _ in range(DEPTH)]
    marks = [{} for _ in range(DEPTH)]
    gsmall = {n: [None] * DEPTH for n in SMALL if n != "rel_table"}
    dbias_a = [[] for _ in DILATED]
    dbias_b = []

    for l in reversed(range(DEPTH)):
        sv = saved[l]
        w = sv["w"]
        g_mix, g_ffn, g_ple = (small[n][l][None, :] for n in ("norm_mix_g", "norm_ffn_g", "norm_ple_g"))
        gqa, gka, gqb = (_tile_gain(small[n][l], N_HEADS) for n in ("qnorm_a_g", "knorm_a_g", "qnorm_b_g"))
        gkb = _tile_gain(small["knorm_b_g"][l], N_KV_B)
        sink = jnp.repeat(small["sink_b"][l], BLK_B)[:, None]

        def ple_bwd(dx_, z_, e_):
            s = _sigmoid(z_.astype(F32))
            return dx_ * s, dx_ * e_.astype(F32) * (s * (1.0 - s))

        de, dz = _ew(ple_bwd, [dx, sv["z"], sv["e"]], [BF16, BF16], "ple_bwd")
        gbig[l]["w_ple_proj"] = _mm(p[l], de, "tn", BF16, "mm_d_ple_proj")
        gbig[l]["w_ple_gate"] = _mm(sv["h3"], dz, "tn", BF16, "mm_d_ple_gate")
        dx, dxb, gsmall["norm_ple_g"][l] = _mm_rms_bwd([(dz, w["w_ple_gate"], "nt")], sv["x2"], g_ple, dx,
                                                      "mm_dh3_rms_bwd")

        gbig[l]["w_ffn_down"] = _mm(sv["hid"], dxb, "tn", BF16, "mm_d_ffn_down")

        def swiglu_bwd(products, extra):
            dh_, a_, u_ = products[0], extra[0].astype(F32), extra[1].astype(F32)
            s = _sigmoid(a_)
            return dh_ * u_ * (s * (1.0 + a_ * (1.0 - s))), dh_ * (a_ * s)

        da, du = _mm_fused([(dxb, w["w_ffn_down"], "nt")], [(sv["a"], 0), (sv["u"], 0)], swiglu_bwd, [BF16, BF16],
                           "mm_dhid_swiglu_bwd")
        gbig[l]["w_ffn_gate"] = _mm(da, sv["h2"], "tn", BF16, "mm_d_ffn_gate")
        gbig[l]["w_ffn_up"] = _mm(du, sv["h2"], "tn", BF16, "mm_d_ffn_up")
        dx, dxb, gsmall["norm_ffn_g"][l] = _mm_rms_bwd([(da, w["w_ffn_gate"], "nn"), (du, w["w_ffn_up"], "nn")],
                                                      sv["x1"], g_ffn, dx, "mm_dh2_rms_bwd")

        gbig[l]["w_out"] = _mm(sv["merged"], dxb, "tn", BF16, "mm_d_out")

        def gate_bwd(products, extra):
            dm_, ca_, cb_ = products[0], extra[0].astype(F32), extra[1].astype(F32)
            sa, sb = _sigmoid(extra[2]), _sigmoid(extra[3])
            return dm_ * sa, dm_ * sb, dm_ * ca_ * (sa * (1.0 - sa)), dm_ * cb_ * (sb * (1.0 - sb))

        dca, dcb, dga, dgb = _mm_fused(
            [(dxb, w["w_out"], "nt")], [(sv["ca"], 0), (sv["cb"], 0), (sv["proj"], OFF_GA), (sv["proj"], OFF_GB)],
            gate_bwd, [BF16, BF16, BF16, BF16], "mm_dmerged_gate_bwd")
        gbig[l]["w_branch_a"] = _mm(sv["ya"], dca, "tn", BF16, "mm_d_branch_a")
        gbig[l]["w_branch_b"] = _mm(sv["yb"], dcb, "tn", BF16, "mm_d_branch_b")
        dya = _mm(dca, w["w_branch_a"], "nt", F32, "mm_dya")
        dyb = _mm(dcb, w["w_branch_b"], "nt", F32, "mm_dyb")

        delta_a = _row_dots(dya, sv["ya"], "attn_a_row_dots")
        delta_b = _row_dots(dyb, sv["yb"], "attn_b_row_dots")

        dqa, dka, dva = [], [], []
        for (blk, d), bias, bk in zip(DILATED, bias_a, buckets_a):
            dq_, dk_, dv_, db_ = _attn_bwd(sv["qa"], sv["ka"], sv["proj"], dya, sv["lse_a"], delta_a, bias, None, blk, d,
                                           f"attn_a{d}_bwd", v_col=OFF_VA)
            dqa.append(dq_)
            dka.append(dk_)
            dva.append(dv_)
            dbias_a[len(dqa) - 1].append(db_)
        dqb, dkb, dvb, db_, dsink = _attn_bwd(sv["qb"], sv["kb"], sv["vb"], dyb, sv["lse_b"], delta_b, bias_b, sink,
                                              BLK_B, 1, "attn_b_bwd")
        dbias_b.append(db_)
        gsmall["sink_b"][l] = dsink.reshape(N_HEADS, BLK_B).sum(axis=1)

        dproj, pqa, pka, pqb, pkb = _qknorm_bwd(sv["proj"], gqa, gka, gqb, gkb, dqa, dka, dva, dqb, dkb, dvb, dga, dgb)
        marks[l]["attn_bwd_done"] = dproj
        gsmall["qnorm_a_g"][l] = pqa.reshape(N_HEADS, HEAD_DIM).sum(0)
        gsmall["knorm_a_g"][l] = pka.reshape(N_HEADS, HEAD_DIM).sum(0)
        gsmall["qnorm_b_g"][l] = pqb.reshape(N_HEADS, HEAD_DIM).sum(0)
        gsmall["knorm_b_g"][l] = pkb.reshape(N_KV_B, HEAD_DIM).sum(0)
        gbig[l]["w_in"] = _mm(dproj, sv["h"], "tn", BF16, "mm_d_in")
        dx, _, gsmall["norm_mix_g"][l] = _mm_rms_bwd([(dproj, w["w_in"], "nn")], sv["x0"], g_mix, dx, "mm_dh_rms_bwd")
        gsmall["norm_mix_g"][l] = gsmall["norm_mix_g"][l][0]
        gsmall["norm_ffn_g"][l] = gsmall["norm_ffn_g"][l][0]
        gsmall["norm_ple_g"][l] = gsmall["norm_ple_g"][l][0]

    gsmall = {n: jnp.stack(v) for n, v in gsmall.items()}
    dtable_a = sum(_table_grad(sum(dbs).reshape(N_HEADS, blk, 3 * blk), bk, "table_grad_a")
                   for dbs, (blk, _), bk in zip(dbias_a, DILATED, buckets_a))
    dtable_b = _table_grad(sum(dbias_b).reshape(N_HEADS, BLK_B, 3 * BLK_B), buckets_b, "table_grad_b")
    gsmall["rel_table"] = jnp.concatenate([dtable_a, dtable_b], axis=0).T
    return loss, dx, gbig, gsmall, marks


def _place():
    return lax.axis_index("x"), lax.axis_index("y"), lax.axis_index("c")


def _flip(v, bit):
    return 1 - v if bit else v


CHIP_RELATIONS = ((0, 1), (1, 0), (1, 1))
ANY = pl.BlockSpec(memory_space=pl.ANY)


def _allgather_body(w_refs, out_refs, send_sems, recv_sems):
    x, y, c = _place()
    chips = [(_flip(x, a), _flip(y, b)) for a, b in CHIP_RELATIONS]

    def make(g):
        w_ref, out_ref = w_refs[g], out_refs[g]
        half = w_ref.shape[0] // 2

        def part(px, py, pc):
            return out_ref.at[2 * px + py, pl.ds(pc * half, half), :]

        def copy(k, block, to, src=None):
            return pltpu.make_async_remote_copy(
                src_ref=part(*block) if src is None else src, dst_ref=part(*block),
                send_sem=send_sems.at[7 * g + k], recv_sem=recv_sems.at[7 * g + k], device_id=to,
                device_id_type=MESH_ID)

        own = pltpu.make_async_remote_copy(
            src_ref=w_ref, dst_ref=out_ref.at[2 * x + y], send_sem=send_sems.at[7 * g + 6],
            recv_sem=recv_sems.at[7 * g + 6], device_id=(x, y, 1 - c), device_id_type=MESH_ID)
        first = [copy(k, (x, y, c), (*chip, c), src=w_ref.at[pl.ds(c * half, half), :]) for k, chip in enumerate(chips)]
        passed = [copy(3 + k, (*chip, c), (x, y, 1 - c)) for k, chip in enumerate(chips)]
        arrive = [copy(k, (*chip, c), (x, y, c)) for k, chip in enumerate(chips)]
        arrive2 = [copy(3 + k, (*chip, 1 - c), (x, y, c)) for k, chip in enumerate(chips)]
        return own, first, passed, arrive, arrive2

    made = [make(g) for g in range(len(w_refs))]
    for own, first, _, _, _ in made:
        own.start()
        for cp in first:
            cp.start()
    for _, _, passed, arrive, _ in made:
        for k in range(3):
            arrive[k].wait_recv()
            passed[k].start()
    for own, first, passed, _, arrive2 in made:
        for k in range(3):
            arrive2[k].wait_recv()
        own.wait_recv()
        for cp in first + passed + [own]:
            cp.wait_send()


def _sibling(x, y, c):
    return [(x, y, 1 - c)]


def _same_core_of_other_chips(x, y, c):
    return [(_flip(x, a), _flip(y, b), c) for a, b in CHIP_RELATIONS]


def _exchange(body, ins, out_types, n_sems, name, sequencer=None):
    n = len(ins)
    sems = (pltpu.SemaphoreType.DMA((n_sems,)), pltpu.SemaphoreType.DMA((n_sems,)))
    if sequencer is None:
        in_place = out_types is None
        out_shape = [jax.ShapeDtypeStruct(a.shape, a.dtype) for a in ins] if in_place else out_types

        def tc_body(*refs):
            body(refs[:n], refs[n:n + len(out_shape)], refs[-2], refs[-1])

        return list(pl.pallas_call(
            tc_body, out_shape=out_shape, in_specs=[ANY] * n, out_specs=[ANY] * len(out_shape),
            input_output_aliases={g: g for g in range(n)} if in_place else {}, scratch_shapes=list(sems), name=name)(*ins))

    collective_id, peers = sequencer
    hbm = pltpu.MemorySpace.HBM
    in_refs = [jax.new_ref(a, memory_space=hbm) for a in ins]
    out_refs = in_refs if out_types is None else [jax.empty_ref(t, memory_space=hbm) for t in out_types]

    @pl.kernel(mesh=plsc.ScalarSubcoreMesh(axis_name="sequencer", num_cores=1), name=name, scratch_types=sems,
               compiler_params=pltpu.CompilerParams(collective_id=collective_id))
    def launch(send_sems, recv_sems):
        barrier = pltpu.get_barrier_semaphore()
        devices = peers(*_place())
        for device in devices:
            pl.semaphore_signal(barrier, inc=1, device_id=device, device_id_type=MESH_ID)
        pl.semaphore_wait(barrier, len(devices))
        body(in_refs, out_refs, send_sems, recv_sems)

    launch()
    return [r[...] for r in out_refs]


def _allgather(shards, name, sequencer=None):
    out_types = [jax.ShapeDtypeStruct((N_CHIPS,) + s.shape, s.dtype) for s in shards]
    if sequencer is not None:
        sequencer = (sequencer, lambda x, y, c: _sibling(x, y, c) + _same_core_of_other_chips(x, y, c))
    return _exchange(_allgather_body, shards, out_types, 7 * len(shards), name, sequencer)


def _half_tile(half):
    return max(t for t in range(16, 1025, 16) if half % t == 0)


def _run_copies(cps):
    for cp in cps:
        cp.start()
    for cp in cps:
        cp.wait_recv()
    for cp in cps:
        cp.wait_send()


def _sibling_halves(gsends, name, sequencer=None):
    def body(g_refs, out_refs, send_sems, recv_sems):
        x, y, c = _place()
        cps = []
        for g, (g_ref, out_ref) in enumerate(zip(g_refs, out_refs)):
            half = g_ref.shape[1] // 2
            cps.append(pltpu.make_async_remote_copy(
                src_ref=g_ref.at[:, pl.ds((1 - c) * half, half), :], dst_ref=out_ref,
                send_sem=send_sems.at[g], recv_sem=recv_sems.at[g], device_id=(x, y, 1 - c), device_id_type=MESH_ID))
        _run_copies(cps)

    out_types = [jax.ShapeDtypeStruct((s.shape[0], s.shape[1] // 2, s.shape[2]), s.dtype) for s in gsends]
    return _exchange(body, gsends, out_types, len(gsends), name, sequencer and (sequencer, _sibling))


def _chip_sums(gsend, sib, place):
    n, rows, cols = gsend.shape
    half = rows // 2
    tm = _half_tile(half)
    nblk = half // tm

    def body(s_ref, g_ref, sib_ref, o_ref):
        o_ref[0] = (g_ref[0].astype(F32) + sib_ref[0].astype(F32)).astype(o_ref.dtype)

    grid_spec = pltpu.PrefetchScalarGridSpec(
        num_scalar_prefetch=1, grid=(n, nblk),
        in_specs=[pl.BlockSpec((1, tm, cols), lambda k, i, s: (jnp.bitwise_xor(s[0], k), s[1] * nblk + i, 0)),
                  pl.BlockSpec((1, tm, cols), lambda k, i, s: (jnp.bitwise_xor(s[0], k), i, 0))],
        out_specs=pl.BlockSpec((1, tm, cols), lambda k, i, s: (k, i, 0)))
    return pl.pallas_call(
        body, out_shape=jax.ShapeDtypeStruct((n, half, cols), BF16), grid_spec=grid_spec,
        name="rs_chip_sums", compiler_params=_params("parallel", "parallel"))(place, gsend, sib)


def _exchange_chip_sums(tsends, name, sequencer=None):
    def body(t_refs, out_refs, send_sems, recv_sems):
        x, y, c = _place()
        cps = []
        for g, (t_ref, out_ref) in enumerate(zip(t_refs, out_refs)):
            for k, device in enumerate(_same_core_of_other_chips(x, y, c)):
                cps.append(pltpu.make_async_remote_copy(
                    src_ref=t_ref.at[k + 1], dst_ref=out_ref.at[k], send_sem=send_sems.at[3 * g + k],
                    recv_sem=recv_sems.at[3 * g + k], device_id=device, device_id_type=MESH_ID))
        _run_copies(cps)

    out_types = [jax.ShapeDtypeStruct((3,) + s.shape[1:], s.dtype) for s in tsends]
    return _exchange(body, tsends, out_types, 3 * len(tsends), name,
                     sequencer and (sequencer, _same_core_of_other_chips))


def _final_sum(tsend, recv, place):
    n, half, cols = tsend.shape
    tm = _half_tile(half)
    nblk = half // tm

    def body(s_ref, t_ref, r_ref, o_ref):
        o_ref[...] = ((t_ref[0].astype(F32) + r_ref[0].astype(F32)) + r_ref[1].astype(F32)) + r_ref[2].astype(F32)

    grid_spec = pltpu.PrefetchScalarGridSpec(
        num_scalar_prefetch=1, grid=(nblk,),
        in_specs=[pl.BlockSpec((1, tm, cols), lambda i, s: (0, i, 0)), pl.BlockSpec((n - 1, tm, cols), lambda i, s: (0, i, 0))],
        out_specs=pl.BlockSpec((tm, cols), lambda i, s: (s[1] * nblk + i, 0)))
    return pl.pallas_call(
        body, out_shape=jax.ShapeDtypeStruct((2 * half, cols), F32), grid_spec=grid_spec, name="rs_final_sum",
        compiler_params=_params("parallel"))(place, tsend, recv)


def _join_halves(gfulls, name, sequencer=None):
    def body(g_refs, out_refs, send_sems, recv_sems):
        x, y, c = _place()
        n = len(g_refs)

        def copy(g, pc):
            half = g_refs[g].shape[0] // 2
            return pltpu.make_async_remote_copy(
                src_ref=g_refs[g].at[pl.ds(pc * half, half), :], dst_ref=out_refs[g].at[pl.ds(pc * half, half), :],
                send_sem=send_sems.at[g], recv_sem=recv_sems.at[g], device_id=(x, y, 1 - c), device_id_type=MESH_ID)

        mine = [copy(g, c) for g in range(n)]
        for cp in mine:
            cp.start()
        for g in range(n):
            copy(g, 1 - c).wait_recv()
        for cp in mine:
            cp.wait_send()

    return _exchange(body, gfulls, None, len(gfulls), name, sequencer and (sequencer, _sibling))


def _allreduce_small(v):
    rows, cols = v.shape

    def body(v_ref, out_ref, buf, send_sems, recv_sems):
        x, y, c = _place()
        cps = []
        for k in range(1, 8):
            peer = (_flip(x, (k >> 2) & 1), _flip(y, (k >> 1) & 1), _flip(c, k & 1))
            cps.append(pltpu.make_async_remote_copy(
                src_ref=v_ref, dst_ref=buf.at[k - 1], send_sem=send_sems.at[k - 1], recv_sem=recv_sems.at[k - 1],
                device_id=peer, device_id_type=MESH_ID))
        for cp in cps:
            cp.start()
        for cp in cps:
            cp.wait_recv()
        for cp in cps:
            cp.wait_send()
        t0 = v_ref[...] + buf[0]
        t1 = buf[1] + buf[2]
        t2 = buf[3] + buf[4]
        t3 = buf[5] + buf[6]
        out_ref[...] = (t0 + t1) + (t2 + t3)

    vm = pl.BlockSpec(memory_space=pltpu.VMEM)
    return pl.pallas_call(
        body, out_shape=jax.ShapeDtypeStruct((rows, cols), F32), in_specs=[vm], out_specs=vm,
        scratch_shapes=[pltpu.VMEM((7, rows, cols), F32), pltpu.SemaphoreType.DMA((7,)), pltpu.SemaphoreType.DMA((7,))],
        name="allreduce_small")(v)


BIG_INFO = {n: (shape, ax) for n, shape, ax in BIG}
GROUPS = (("w_in",), ("w_ffn_gate", "w_ffn_up", "w_ffn_down", "w_out", "w_ple_gate"),
          ("w_branch_a", "w_branch_b", "w_ple_proj"))


def _shard_shape(name):
    (k, m), ax = BIG_INFO[name]
    return (k // N_CHIPS, m) if ax == 0 else (k, m // N_CHIPS)


def _group_rows(group):
    offs, off = {}, 0
    for n in group:
        offs[n] = off
        off += _shard_shape(n)[0]
    return offs, off


def _pack_groups(shards, layer, dtype):
    return [jnp.concatenate([shards[n][layer].astype(dtype) for n in group], axis=0) for group in GROUPS]


def _unpack_full(gathered, groups):
    out = {}
    for group, arr in zip(groups, gathered):
        offs, _ = _group_rows(group)
        for n in group:
            rows, cols = _shard_shape(n)
            (k, m), ax = BIG_INFO[n]
            slab = arr[:, offs[n]:offs[n] + rows]
            out[n] = slab.reshape(k, m) if ax == 0 else jnp.transpose(slab, (1, 0, 2)).reshape(k, m)
    return out


def _pack_grads(gfull):
    out = []
    for group in GROUPS:
        parts = []
        for n in group:
            rows, cols = _shard_shape(n)
            ax = BIG_INFO[n][1]
            slab = (gfull[n].reshape(N_CHIPS, rows, cols) if ax == 0
                    else jnp.transpose(gfull[n].reshape(rows, N_CHIPS, cols), (1, 0, 2)))
            parts.append(slab)
        out.append(jnp.concatenate(parts, axis=1))
    return out


def _after(values, mark):
    values, _ = lax.optimization_barrier((values, mark))
    return values


def _reduce_scatter_begin(gsends, place, tag, ids):
    sibs = _sibling_halves(gsends, "rs_sibling_halves_" + tag, ids[0])
    tsends = [_chip_sums(g, s, place) for g, s in zip(gsends, sibs)]
    return tsends, _exchange_chip_sums(tsends, "rs_exchange_" + tag, ids[1])


def _reduce_scatter_finish(begun, place, tag, ids, hold):
    tsends, recvs = begun
    recvs = _after(recvs, hold)
    return _join_halves([_final_sum(t, r, place) for t, r in zip(tsends, recvs)], "rs_join_halves_" + tag, ids[2])


SMALL_SHAPES = {"rel_table": (NUM_BUCKETS, 2 * N_HEADS), "norm_mix_g": (DEPTH, D_MODEL), "qnorm_a_g": (DEPTH, HEAD_DIM),
                "knorm_a_g": (DEPTH, HEAD_DIM), "qnorm_b_g": (DEPTH, HEAD_DIM), "knorm_b_g": (DEPTH, HEAD_DIM),
                "sink_b": (DEPTH, N_HEADS), "norm_ffn_g": (DEPTH, D_MODEL), "norm_ple_g": (DEPTH, D_MODEL)}


def _pack_small(vals, last=None):
    flat = jnp.concatenate([vals[n].astype(F32).reshape(-1) for n in SMALL])
    tail = jnp.zeros((SMALL_ROWS * LANES - flat.shape[0],), F32)
    if last is not None:
        tail = tail.at[-1].set(last)
    return jnp.concatenate([flat, tail]).reshape(SMALL_ROWS, LANES)


def _unpack_small(packed):
    flat, out, off = packed.reshape(-1), {}, 0
    for n in SMALL:
        size = math.prod(SMALL_SHAPES[n])
        out[n] = flat[off:off + size].reshape(SMALL_SHAPES[n])
        off += size
    return out


def _adamw(w, gs, g_row, m, v, name):
    c1 = 1.0 - ADAM_B1 ** ADAM_STEP
    c2 = 1.0 - ADAM_B2 ** ADAM_STEP
    total, width = w.shape
    n_layers = len(gs)
    per = total // n_layers
    tm = max(t for t in range(8, 513, 8) if per % t == 0 and g_row % t == 0)
    nblk = per // tm

    def body(*refs):
        w_ref, g_refs = refs[0], refs[1:1 + n_layers]
        m_ref, v_ref, og, od, om, ov = refs[1 + n_layers:]
        layer = pl.program_id(0) // nblk
        g = g_refs[0][...]
        for l in range(1, n_layers):
            g = jnp.where(layer == l, g_refs[l][...], g)
        m_new = ADAM_B1 * m_ref[...] + (1.0 - ADAM_B1) * g
        v_new = ADAM_B2 * v_ref[...] + (1.0 - ADAM_B2) * (g * g)
        og[...] = g
        od[...] = -ADAM_LR * ((m_new / c1) / (jnp.sqrt(v_new / c2) + ADAM_EPS) + ADAM_WD * w_ref[...])
        om[...] = m_new
        ov[...] = v_new

    row = pl.BlockSpec((tm, width), lambda i: (i, 0))
    g_specs = [pl.BlockSpec((tm, width), lambda i, l=l: (g_row // tm + jnp.clip(i - l * nblk, 0, nblk - 1), 0))
               for l in range(n_layers)]
    return pl.pallas_call(
        body, out_shape=[jax.ShapeDtypeStruct((total, width), F32)] * 4, grid=(total // tm,),
        in_specs=[row] + g_specs + [row, row], out_specs=[row] * 4, name=name,
        compiler_params=_params("parallel"))(w, *gs, m, v)


def kernel(x, p, rel_table, norm_mix_g, w_in, qnorm_a_g, knorm_a_g, qnorm_b_g, knorm_b_g, sink_b, w_branch_a, w_branch_b, w_out, norm_ffn_g, w_ffn_gate, w_ffn_up, w_ffn_down, norm_ple_g, w_ple_gate, w_ple_proj, loss_target, m_rel_table, m_norm_mix_g, m_w_in, m_qnorm_a_g, m_knorm_a_g, m_qnorm_b_g, m_knorm_b_g, m_sink_b, m_w_branch_a, m_w_branch_b, m_w_out, m_norm_ffn_g, m_w_ffn_gate, m_w_ffn_up, m_w_ffn_down, m_norm_ple_g, m_w_ple_gate, m_w_ple_proj, v_rel_table, v_norm_mix_g, v_w_in, v_qnorm_a_g, v_knorm_a_g, v_qnorm_b_g, v_knorm_b_g, v_sink_b, v_w_branch_a, v_w_branch_b, v_w_out, v_norm_ffn_g, v_w_ffn_gate, v_w_ffn_up, v_w_ffn_down, v_norm_ple_g, v_w_ple_gate, v_w_ple_proj):
    given = dict(locals())

    def held(name, a):
        return jnp.swapaxes(a, 1, 2) if name in TRANSPOSED else a

    weights = {n: held(n, given[n]) for n in WEIGHTS}
    moments_m = {n: held(n, given["m_" + n]) for n in WEIGHTS}
    moments_v = {n: held(n, given["v_" + n]) for n in WEIGHTS}
    xi, yi, ci = _place()
    place = jnp.stack([2 * xi + yi, ci]).astype(jnp.int32)

    shards = [_pack_groups(weights, l, BF16) for l in range(DEPTH)]
    w_in0 = _allgather(shards[0][:1], "allgather_w_in_layer0", sequencer=9)
    rest0 = _allgather(_after(shards[0][1:], w_in0), "allgather_rest_layer0", sequencer=1)
    gathered = [w_in0 + rest0, None]
    small = {n: weights[n] for n in SMALL}

    def w_in_of(l, mark):
        return _unpack_full(_after(gathered[l][:1], (shards[1], mark) if l == 0 else mark), GROUPS[:1])["w_in"]

    def rest_of(l, mark):
        if l == 0:
            gathered[1] = _allgather(_after(shards[1], mark), "allgather_layer1", sequencer=2)
        return _unpack_full(_after(gathered[l][1:], mark), GROUPS[1:])

    loss, dx, gbig, gsmall, marks = _local_step(x[0], p[:, 0], loss_target[0], w_in_of, rest_of, small)

    gsends = [_pack_grads(gbig[l]) for l in range(DEPTH)]
    stages = {"layer1": (gsends[1], (3, 4, 5)), "rest_layer0": (gsends[0][1:], (6, 7, 8)),
              "w_in_layer0": (gsends[0][:1], (10, 11, 12))}
    begun = {tag: _reduce_scatter_begin(g, place, tag, ids) for tag, (g, ids) in stages.items()}

    def finish(tag, hold):
        return _reduce_scatter_finish(begun[tag], place, tag, stages[tag][1], hold)

    red1 = finish("layer1", marks[0]["attn_bwd_done"])
    rest0 = finish("rest_layer0", marks[0]["attn_bwd_done"])

    grads, delta, new_m, new_v = {}, {}, {}, {}

    def update(group, reduced):
        offs, _ = _group_rows(group)
        for n in group:
            shape = weights[n].shape
            two_d = lambda a: a.reshape(shape[0] * shape[1], shape[2])
            outs = _adamw(two_d(weights[n]), reduced, offs[n], two_d(moments_m[n]), two_d(moments_v[n]), "adamw_" + n)
            grads[n], delta[n], new_m[n], new_v[n] = (held(n, o.reshape(shape)) for o in outs)

    for gi in (1, 2):
        update(GROUPS[gi], _after([rest0[gi - 1], red1[gi]], begun["w_in_layer0"][0]))
    small_grads = _allreduce_small(_pack_small(gsmall, last=loss))
    g_, d_, m_, v_ = _adamw(_pack_small(weights), [small_grads], 0, _pack_small(moments_m), _pack_small(moments_v),
                            "adamw_small")
    grads.update(_unpack_small(g_))
    delta.update(_unpack_small(d_))
    new_m.update(_unpack_small(m_))
    new_v.update(_unpack_small(v_))
    others_done = [dx, d_] + [delta[n] for gi in (1, 2) for n in GROUPS[gi]]
    update(GROUPS[0], [finish("w_in_layer0", others_done)[0], red1[0]])

    return (small_grads[-1, -1], dx[None], *[grads[n] for n in WEIGHTS], *[delta[n] for n in WEIGHTS],
            *[new_m[n] for n in WEIGHTS], *[new_v[n] for n in WEIGHTS])
```

```python
import functools
import math

import jax
import jax.numpy as jnp
from jax import lax
from jax.experimental import pallas as pl
from jax.experimental.pallas import tpu as pltpu
from jax.experimental.pallas import tpu_sc as plsc

F32 = jnp.float32
BF16 = jnp.bfloat16
MESH_ID = pl.DeviceIdType.MESH

D_MODEL = 1024
DEPTH = 2
HEAD_DIM = 64
N_HEADS = 8
WIDTH = N_HEADS * HEAD_DIM
N_PAIRS = 4
ITEMS = 16
MAX_CHUNK = 1024
N_KV_B = 2
PLE_DIM = 256
D_FF = 2816
D_IN = 4352
OFF_QA, OFF_KA, OFF_VA, OFF_QB, OFF_KB, OFF_VB, OFF_GA, OFF_GB = 0, 512, 1024, 1536, 2048, 2176, 2304, 3328
DILATED = ((64, 1), (64, 4), (64, 16))
BLK_B = 128
NUM_BUCKETS = 32
MAX_DISTANCE = 1024
RMS_EPS = 1e-6
NEG_INF = -1e30
LANES = 128
ROW_TILE = 256
VMEM_LIMIT = 48 * 1024 * 1024

ADAM_LR, ADAM_B1, ADAM_B2, ADAM_EPS, ADAM_WD, ADAM_STEP = 0.001, 0.9, 0.999, 1e-08, 0.01, 10

TRANSPOSED = ("w_in", "w_ffn_gate", "w_ffn_up")
BIG = (
    ("w_in", (D_IN, D_MODEL), 0),
    ("w_branch_a", (WIDTH, D_MODEL), 1),
    ("w_branch_b", (WIDTH, D_MODEL), 1),
    ("w_out", (D_MODEL, D_MODEL), 0),
    ("w_ffn_gate", (D_FF, D_MODEL), 0),
    ("w_ffn_up", (D_FF, D_MODEL), 0),
    ("w_ffn_down", (D_FF, D_MODEL), 0),
    ("w_ple_gate", (D_MODEL, D_MODEL), 0),
    ("w_ple_proj", (PLE_DIM, D_MODEL), 1),
)
SMALL = ("rel_table", "norm_mix_g", "qnorm_a_g", "knorm_a_g", "qnorm_b_g", "knorm_b_g", "sink_b",
         "norm_ffn_g", "norm_ple_g")
WEIGHTS = ("rel_table", "norm_mix_g", "w_in", "qnorm_a_g", "knorm_a_g", "qnorm_b_g", "knorm_b_g", "sink_b",
           "w_branch_a", "w_branch_b", "w_out", "norm_ffn_g", "w_ffn_gate", "w_ffn_up", "w_ffn_down",
           "norm_ple_g", "w_ple_gate", "w_ple_proj")
N_CHIPS = 4
SMALL_ROWS = 64


def _params(*sem):
    return pltpu.CompilerParams(dimension_semantics=sem, vmem_limit_bytes=VMEM_LIMIT)


MM_VMEM_BUDGET = 40 * 1024 * 1024
STEP_OVERHEAD_S = 0.4e-6
TILE_DMA_BYTES_PER_S = 1.5e12


def _mm_dims(a, b, mode):
    if mode == "nn":
        return a.shape[0], b.shape[1], a.shape[1]
    if mode == "nt":
        return a.shape[0], b.shape[0], a.shape[1]
    return a.shape[1], b.shape[1], a.shape[0]


def _mm_tiles(m, n, pairs, tile_bytes, col_offsets, full_rows=False):
    best = None
    widths = [n] if full_rows else [t for t in range(LANES, n + 1, LANES) if n % t == 0 and all(o % t == 0 for o in col_offsets)]
    for tm in (t for t in range(LANES, m + 1, LANES) if m % t == 0):
        for tn in widths:
            io = sum(tm * k * ab + tn * k * bb for k, ab, bb in pairs) + tm * tn * sum(tile_bytes)
            casts = sum((tm * k * 2 if ab == 4 else 0) + (tn * k * 2 if bb == 4 else 0) for k, ab, bb in pairs)
            if 2 * io + len(pairs) * tm * tn * 4 + casts > MM_VMEM_BUDGET:
                continue
            cost = (m // tm) * (n // tn) * STEP_OVERHEAD_S + io / TILE_DMA_BYTES_PER_S
            if best is None or (cost, -tm) < best[0]:
                best = ((cost, -tm), tm, tn)
    return best[1], best[2]


def _mm_fused(pairs, extras, epilogue, out_dtypes, name, next_gain=None):
    m, n, _ = _mm_dims(*pairs[0])
    assert all(_mm_dims(*p)[:2] == (m, n) for p in pairs)
    with_norm = next_gain is not None
    out_dtypes = list(out_dtypes) + ([BF16] if with_norm else [])
    tm, tn = _mm_tiles(
        m, n, [(_mm_dims(a, b, mode)[2], a.dtype.itemsize, b.dtype.itemsize) for a, b, mode in pairs],
        [e.dtype.itemsize for e, _ in extras] + [jnp.dtype(d).itemsize for d in out_dtypes], [off for _, off in extras],
        full_rows=with_norm)
    dims = {"nn": (((1,), (0,)), ((), ())), "nt": (((1,), (1,)), ((), ())), "tn": (((0,), (0,)), ((), ()))}
    in_specs, args = [], []
    for a, b, mode in pairs:
        k = _mm_dims(a, b, mode)[2]
        in_specs.append(pl.BlockSpec((k, tm), lambda i, j: (0, i)) if mode == "tn" else pl.BlockSpec((tm, k), lambda i, j: (i, 0)))
        in_specs.append(pl.BlockSpec((tn, k), lambda i, j: (j, 0)) if mode == "nt" else pl.BlockSpec((k, tn), lambda i, j: (0, j)))
        args += [a, b]
    for e, off in extras:
        in_specs.append(pl.BlockSpec((tm, tn), lambda i, j, o=off // tn: (i, o + j)))
        args.append(e)
    n_pairs, n_tiles = len(pairs), 2 * len(pairs) + len(extras)
    if with_norm:
        in_specs.append(pl.BlockSpec((1, n), lambda i, j: (0, 0)))
        args.append(next_gain)
    n_in = len(args)

    def body(*refs):
        products = [lax.dot_general(refs[2 * p][...].astype(BF16), refs[2 * p + 1][...].astype(BF16), dims[pairs[p][2]],
                                    preferred_element_type=F32) for p in range(n_pairs)]
        outs = list(epilogue(products, [r[...] for r in refs[2 * n_pairs:n_tiles]]))
        if with_norm:
            y = outs[0]
            outs.append((y * lax.rsqrt(jnp.mean(y * y, axis=-1, keepdims=True) + RMS_EPS)) * refs[n_tiles][...])
        for r, o in zip(refs[n_in:], outs):
            r[...] = o.astype(r.dtype)

    tile = pl.BlockSpec((tm, tn), lambda i, j: (i, j))
    return pl.pallas_call(
        body, out_shape=[jax.ShapeDtypeStruct((m, n), d) for d in out_dtypes], grid=(m // tm, n // tn),
        in_specs=in_specs, out_specs=[tile] * len(out_dtypes), name=name,
        compiler_params=_params("parallel", "parallel"))(*args)


def _mm(a, b, mode, out_dtype, name, res=None):
    if res is None:
        return _mm_fused([(a, b, mode)], [], lambda products, extra: products, [out_dtype], name)[0]
    return _mm_fused([(a, b, mode)], [(res, 0)], lambda products, extra: [products[0] + extra[0]], [out_dtype], name)[0]


def _mm_res_norm(a, b, mode, res, gain, name):
    return _mm_fused([(a, b, mode)], [(res, 0)], lambda products, extra: [products[0] + extra[0]], [F32], name,
                     next_gain=gain)


def _ew(fn, ins, out_dtypes, name):
    rows, width = ins[0].shape
    n_in = len(ins)

    def body(*refs):
        outs = fn(*[r[...] for r in refs[:n_in]])
        for r, o in zip(refs[n_in:], outs):
            r[...] = o.astype(r.dtype)

    row = pl.BlockSpec((ROW_TILE, width), lambda i: (i, 0))
    return pl.pallas_call(
        body, out_shape=[jax.ShapeDtypeStruct((rows, width), dt) for dt in out_dtypes], grid=(rows // ROW_TILE,),
        in_specs=[row] * n_in, out_specs=[row] * len(out_dtypes), name=name, compiler_params=_params("parallel"))(*ins)


def _sigmoid(x):
    return 1.0 / (1.0 + jnp.exp(-x))


def _seg_sum(v):
    outs = []
    for k in range(v.shape[1] // LANES):
        vp = v[:, k * LANES:(k + 1) * LANES]
        left = lax.broadcasted_iota(jnp.int32, vp.shape, 1) < HEAD_DIM
        sl = jnp.sum(jnp.where(left, vp, 0.0), axis=-1, keepdims=True)
        sr = jnp.sum(jnp.where(left, 0.0, vp), axis=-1, keepdims=True)
        outs.append(jnp.where(left, sl, sr))
    return outs[0] if len(outs) == 1 else jnp.concatenate(outs, axis=1)


def _seg_rstd(x):
    return lax.rsqrt(_seg_sum(x * x) * (1.0 / HEAD_DIM) + RMS_EPS)


def _rms_fwd(x, g, name):
    rows, d = x.shape
    tm = ROW_TILE

    def body(x_ref, g_ref, h_ref):
        xv = x_ref[...]
        r = lax.rsqrt(jnp.mean(xv * xv, axis=-1, keepdims=True) + RMS_EPS)
        h_ref[...] = ((xv * r) * g_ref[...]).astype(BF16)

    return pl.pallas_call(
        body, out_shape=jax.ShapeDtypeStruct((rows, d), BF16), grid=(rows // tm,),
        in_specs=[pl.BlockSpec((tm, d), lambda i: (i, 0)), pl.BlockSpec((1, d), lambda i: (0, 0))],
        out_specs=pl.BlockSpec((tm, d), lambda i: (i, 0)), name=name,
        compiler_params=_params("parallel"))(x, g)


def _mm_rms_bwd(pairs, x, g, dres, name):
    rows, d = x.shape
    assert all(_mm_dims(*p)[:2] == (rows, d) for p in pairs)
    kbytes = [(_mm_dims(a, b, mode)[2], a.dtype.itemsize, b.dtype.itemsize) for a, b, mode in pairs]
    tm = max(t for t in (512, 256, 128) if rows % t == 0 and
             2 * (sum(t * k * ab + d * k * bb for k, ab, bb in kbytes) + t * d * 14) + 2 * t * d * 4 <= MM_VMEM_BUDGET)
    dims = {"nn": (((1,), (0,)), ((), ())), "nt": (((1,), (1,)), ((), ()))}
    n_pairs = len(pairs)

    def body(*refs):
        x_ref, g_ref, dres_ref = refs[2 * n_pairs:2 * n_pairs + 3]
        dx_ref, dxb_ref, dg_ref = refs[2 * n_pairs + 3:]
        dhv = functools.reduce(lambda u, v: u + v, [
            lax.dot_general(refs[2 * p][...].astype(BF16), refs[2 * p + 1][...].astype(BF16), dims[pairs[p][2]],
                            preferred_element_type=F32) for p in range(n_pairs)])
        xv = x_ref[...]
        r = lax.rsqrt(jnp.mean(xv * xv, axis=-1, keepdims=True) + RMS_EPS)
        xh = xv * r
        dxh = dhv * g_ref[...]
        dxv = dres_ref[...] + r * (dxh - xh * jnp.mean(dxh * xh, axis=-1, keepdims=True))
        dx_ref[...] = dxv
        dxb_ref[...] = dxv.astype(BF16)
        part = jnp.sum(dhv * xh, axis=0, keepdims=True)

        @pl.when(pl.program_id(0) == 0)
        def _():
            dg_ref[...] = part

        @pl.when(pl.program_id(0) > 0)
        def _():
            dg_ref[...] += part

    row = pl.BlockSpec((tm, d), lambda i: (i, 0))
    vec = pl.BlockSpec((1, d), lambda i: (0, 0))
    in_specs, args = [], []
    for a, b, mode in pairs:
        in_specs += [pl.BlockSpec((tm, a.shape[1]), lambda i: (i, 0)), pl.BlockSpec(b.shape, lambda i: (0, 0))]
        args += [a, b]
    return pl.pallas_call(
        body, out_shape=[jax.ShapeDtypeStruct((rows, d), F32), jax.ShapeDtypeStruct((rows, d), BF16),
                         jax.ShapeDtypeStruct((1, d), F32)],
        grid=(rows // tm,), in_specs=in_specs + [row, vec, row], out_specs=[row, row, vec],
        name=name, compiler_params=_params("arbitrary"))(*args, x, g, dres)


def _ple_bwd(dx, z, e):
    s = _sigmoid(z.astype(F32))
    return dx * s, dx * e.astype(F32) * (s * (1.0 - s))


def _loss_grad(y, t, z, e):
    rows, d = y.shape
    tm = ROW_TILE

    def body(y_ref, t_ref, z_ref, e_ref, dy_ref, de_ref, dz_ref, l_ref):
        err = y_ref[...] - t_ref[...]
        dy = err * (1.0 / d)
        dy_ref[...] = dy
        de, dz = _ple_bwd(dy, z_ref[...], e_ref[...])
        de_ref[...] = de.astype(BF16)
        dz_ref[...] = dz.astype(BF16)
        part = jnp.zeros((1, LANES), F32) + jnp.sum(err * err) * (0.5 / d)

        @pl.when(pl.program_id(0) == 0)
        def _():
            l_ref[...] = part

        @pl.when(pl.program_id(0) > 0)
        def _():
            l_ref[...] += part

    row = pl.BlockSpec((tm, d), lambda i: (i, 0))
    return pl.pallas_call(
        body, out_shape=[jax.ShapeDtypeStruct((rows, d), F32), jax.ShapeDtypeStruct((rows, d), BF16),
                         jax.ShapeDtypeStruct((rows, d), BF16), jax.ShapeDtypeStruct((1, LANES), F32)],
        grid=(rows // tm,), in_specs=[row] * 4, out_specs=[row, row, row, pl.BlockSpec((1, LANES), lambda i: (0, 0))],
        name="loss_grad", compiler_params=_params("arbitrary"))(y, t, z, e)


def _swap_halves(v):
    return pltpu.roll(v, HEAD_DIM, axis=1)


def _expand_kv(kv):
    left = lax.broadcasted_iota(jnp.int32, kv.shape, 1) < HEAD_DIM
    sw = _swap_halves(kv)
    h0 = jnp.where(left, kv, sw)
    h1 = jnp.where(left, sw, kv)
    return jnp.concatenate([h0, h0, h1, h1], axis=1)


def _reduce_kv(dkv):
    left = lax.broadcasted_iota(jnp.int32, (dkv.shape[0], LANES), 1) < HEAD_DIM
    t = dkv[:, 0:LANES] + dkv[:, LANES:2 * LANES]
    u = dkv[:, 2 * LANES:3 * LANES] + dkv[:, 3 * LANES:4 * LANES]
    t = t + _swap_halves(t)
    u = u + _swap_halves(u)
    return jnp.where(left, t, u)


def _qknorm_fwd(proj, gqa, gka, gqb, gkb):
    rows = proj.shape[0]
    tm = ROW_TILE

    def body(qa_ref, ka_ref, qb_ref, kb_ref, vb_ref, gqa_ref, gka_ref, gqb_ref, gkb_ref, oqa, oka, oqb, okb, ovb):
        for src, g_ref, dst in ((qa_ref, gqa_ref, oqa), (ka_ref, gka_ref, oka), (qb_ref, gqb_ref, oqb)):
            xv = src[...]
            dst[...] = (xv * _seg_rstd(xv)) * g_ref[...]
        kv = kb_ref[...]
        okb[...] = _expand_kv((kv * _seg_rstd(kv)) * gkb_ref[...])
        ovb[...] = _expand_kv(vb_ref[...])

    def win(width, off):
        return pl.BlockSpec((tm, width), lambda i: (i, off // width))

    vec = lambda w: pl.BlockSpec((1, w), lambda i: (0, 0))
    out = pl.BlockSpec((tm, WIDTH), lambda i: (i, 0))
    return pl.pallas_call(
        body, out_shape=[jax.ShapeDtypeStruct((rows, WIDTH), F32)] * 5, grid=(rows // tm,),
        in_specs=[win(WIDTH, OFF_QA), win(WIDTH, OFF_KA), win(WIDTH, OFF_QB), win(LANES, OFF_KB), win(LANES, OFF_VB),
                  vec(WIDTH), vec(WIDTH), vec(WIDTH), vec(LANES)],
        out_specs=[out] * 5, name="qknorm_fwd", compiler_params=_params("parallel"))(
            proj, proj, proj, proj, proj, gqa, gka, gqb, gkb)


def _norm_bwd(xv, g, dy):
    r = _seg_rstd(xv)
    xh = xv * r
    dxh = dy * g
    dx = r * (dxh - xh * (_seg_sum(dxh * xh) * (1.0 / HEAD_DIM)))
    return dx, jnp.sum(dy * xh, axis=0, keepdims=True)


def _qknorm_bwd(proj, gqa, gka, gqb, gkb, dqa, dka, dva, dqb, dkb, dvb, dga, dgb):
    rows = proj.shape[0]
    tm = ROW_TILE
    n_a = len(dqa)

    def body(*refs):
        qa_ref, ka_ref, qb_ref, kb_ref, gqa_ref, gka_ref, gqb_ref, gkb_ref = refs[:8]
        pos = 8
        dqa_refs, dka_refs, dva_refs = refs[pos:pos + n_a], refs[pos + n_a:pos + 2 * n_a], refs[pos + 2 * n_a:pos + 3 * n_a]
        pos += 3 * n_a
        dqb_ref, dkb_ref, dvb_ref, dga_ref, dgb_ref = refs[pos:pos + 5]
        dproj_ref, ogqa, ogka, ogqb, ogkb = refs[pos + 5:]

        def total(rs):
            acc = rs[0][...]
            for r in rs[1:]:
                acc = acc + r[...]
            return acc

        dx_qa, p_qa = _norm_bwd(qa_ref[...], gqa_ref[...], total(dqa_refs))
        dx_ka, p_ka = _norm_bwd(ka_ref[...], gka_ref[...], total(dka_refs))
        dx_qb, p_qb = _norm_bwd(qb_ref[...], gqb_ref[...], dqb_ref[...])
        dx_kb, p_kb = _norm_bwd(kb_ref[...], gkb_ref[...], _reduce_kv(dkb_ref[...]))
        dproj_ref[:, OFF_QA:OFF_QA + WIDTH] = dx_qa.astype(BF16)
        dproj_ref[:, OFF_KA:OFF_KA + WIDTH] = dx_ka.astype(BF16)
        dproj_ref[:, OFF_VA:OFF_VA + WIDTH] = total(dva_refs).astype(BF16)
        dproj_ref[:, OFF_QB:OFF_QB + WIDTH] = dx_qb.astype(BF16)
        dproj_ref[:, OFF_KB:OFF_KB + LANES] = dx_kb.astype(BF16)
        dproj_ref[:, OFF_VB:OFF_VB + LANES] = _reduce_kv(dvb_ref[...]).astype(BF16)
        dproj_ref[:, OFF_GA:OFF_GB] = dga_ref[...]
        dproj_ref[:, OFF_GB:D_IN] = dgb_ref[...]
        first = pl.program_id(0) == 0
        for o_ref, part in ((ogqa, p_qa), (ogka, p_ka), (ogqb, p_qb), (ogkb, p_kb)):
            @pl.when(first)
            def _(o_ref=o_ref, part=part):
                o_ref[...] = part

            @pl.when(jnp.logical_not(first))
            def _(o_ref=o_ref, part=part):
                o_ref[...] += part

    def win(width, off):
        return pl.BlockSpec((tm, width), lambda i: (i, off // width))

    vec = lambda w: pl.BlockSpec((1, w), lambda i: (0, 0))
    row = lambda w: pl.BlockSpec((tm, w), lambda i: (i, 0))
    in_specs = [win(WIDTH, OFF_QA), win(WIDTH, OFF_KA), win(WIDTH, OFF_QB), win(LANES, OFF_KB),
                vec(WIDTH), vec(WIDTH), vec(WIDTH), vec(LANES)]
    in_specs += [row(WIDTH)] * (3 * n_a + 3) + [row(D_MODEL)] * 2
    return pl.pallas_call(
        body,
        out_shape=[jax.ShapeDtypeStruct((rows, D_IN), BF16), jax.ShapeDtypeStruct((1, WIDTH), F32),
                   jax.ShapeDtypeStruct((1, WIDTH), F32), jax.ShapeDtypeStruct((1, WIDTH), F32),
                   jax.ShapeDtypeStruct((1, LANES), F32)],
        grid=(rows // tm,), in_specs=in_specs,
        out_specs=[row(D_IN), vec(WIDTH), vec(WIDTH), vec(WIDTH), vec(LANES)],
        name="qknorm_bwd", compiler_params=_params("arbitrary"))(
            proj, proj, proj, proj, gqa, gka, gqb, gkb, *dqa, *dka, *dva, dqb, dkb, dvb, dga, dgb)


def _t5_bucket(rel):
    half_b = NUM_BUCKETS // 2
    max_exact = half_b // 2
    sign = jnp.where(rel > 0, half_b, 0)
    n = jnp.abs(rel)
    nf = jnp.maximum(n, 1).astype(F32)
    large = max_exact + (jnp.log(nf / max_exact) / math.log(MAX_DISTANCE / max_exact)
                         * (half_b - max_exact)).astype(jnp.int32)
    large = jnp.minimum(large, half_b - 1)
    return sign + jnp.where(n < max_exact, n, large)


def _band_buckets(blk, dilation):
    i = jnp.arange(blk, dtype=jnp.int32)[:, None]
    j = jnp.arange(3 * blk, dtype=jnp.int32)[None, :]
    rel = j - blk - i
    return jnp.where(jnp.abs(rel) <= blk, _t5_bucket(rel * dilation), -1)


def _bias_tiles(table, buckets, head_off, name):
    blk = buckets.shape[0]

    def body(tab_ref, bk_ref, o_ref):
        h = pl.program_id(0) + head_off
        bk = bk_ref[...]
        acc = jnp.full(bk.shape, NEG_INF, F32)
        for b in range(NUM_BUCKETS):
            acc = jnp.where(bk == b, tab_ref[b, h], acc)
        o_ref[0] = acc

    return pl.pallas_call(
        body, out_shape=jax.ShapeDtypeStruct((N_HEADS, blk, 3 * blk), F32), grid=(N_HEADS,),
        in_specs=[pl.BlockSpec(memory_space=pltpu.SMEM), pl.BlockSpec((blk, 3 * blk), lambda h: (0, 0))],
        out_specs=pl.BlockSpec((1, blk, 3 * blk), lambda h: (h, 0, 0)),
        name=name, compiler_params=_params("parallel"))(table, buckets)


def _table_grad(dbias, buckets, name):
    blk = buckets.shape[0]

    def body(db_ref, bk_ref, o_ref):
        bk = bk_ref[...]
        dbv = db_ref[0]
        lane = lax.broadcasted_iota(jnp.int32, (1, LANES), 1)
        acc = jnp.zeros((1, LANES), F32)
        for b in range(NUM_BUCKETS):
            acc = jnp.where(lane == b, jnp.sum(jnp.where(bk == b, dbv, 0.0)), acc)
        o_ref[0] = acc

    out = pl.pallas_call(
        body, out_shape=jax.ShapeDtypeStruct((N_HEADS, 1, LANES), F32), grid=(N_HEADS,),
        in_specs=[pl.BlockSpec((1, blk, 3 * blk), lambda h: (h, 0, 0)), pl.BlockSpec((blk, 3 * blk), lambda h: (0, 0))],
        out_specs=pl.BlockSpec((1, 1, LANES), lambda h: (h, 0, 0)),
        name=name, compiler_params=_params("parallel"))(dbias, buckets)
    return out[:, 0, :NUM_BUCKETS]


def _dot_nt(a, b):
    return lax.dot_general(a, b, (((1,), (1,)), ((), ())), preferred_element_type=F32)


def _dot_tn(a, b):
    return lax.dot_general(a, b, (((0,), (0,)), ((), ())), preferred_element_type=F32)


def _stack_pair(x2, left):
    return jnp.concatenate([jnp.where(left, x2, 0.0), jnp.where(left, 0.0, x2)], axis=0).astype(BF16)


def _attn_geometry(blk, d):
    halo = blk * d
    subs = max(1, min(ITEMS // d, MAX_CHUNK // halo))
    return subs, min(d, ITEMS // subs), halo


def _item_of(j, r0, per_group):
    return j // per_group, r0 + j % per_group


def _span_rows(ref, first, r, blk, d):
    if d == 1:
        return ref[first:first + blk, :]
    return ref[pl.ds(first + r, blk, stride=d), :]


def _set_span_rows(ref, first, r, blk, d, val, add=False):
    idx = slice(first, first + blk) if d == 1 else pl.ds(first + r, blk, stride=d)
    ref[idx, :] = ref[idx, :] + val if add else val


def _for_groups(group, groups, per_group):
    if groups == 1:
        group(0)
    else:
        def step(g, carry):
            group(g * per_group)
            return carry

        lax.fori_loop(0, groups, step, 0)


def _key_rows(p_ref, c_ref, n_ref, s, r, subs, halo, blk, d):
    parts = []
    for span in (s - 1, s, s + 1):
        if span < 0:
            parts.append(_span_rows(p_ref, 0, r, blk, d))
        elif span == subs:
            parts.append(_span_rows(n_ref, 0, r, blk, d))
        else:
            parts.append(_span_rows(c_ref, span * halo, r, blk, d))
    return jnp.concatenate(parts, axis=0)


def _item_penalty(t, nct, s, subs, blk):
    first_ok = True if s > 0 else t > 0
    last_ok = True if s < subs - 1 else t < nct - 1
    col = lax.broadcasted_iota(jnp.int32, (1, 3 * blk), 1)
    ok = jnp.logical_and(jnp.logical_or(col >= blk, first_ok), jnp.logical_or(col < 2 * blk, last_ok))
    return jnp.where(ok, 0.0, NEG_INF).astype(F32)


def _attn_specs(seq, blk, d, step_of, col=0):
    subs, _, halo = _attn_geometry(blk, d)
    last, first = seq // halo - 1, col // LANES
    cur = pl.BlockSpec((subs * halo, LANES), lambda hp, t: (step_of(t), first + hp))
    prev = pl.BlockSpec((halo, LANES), lambda hp, t: (jnp.clip(step_of(t) * subs - 1, 0, last), first + hp))
    nxt = pl.BlockSpec((halo, LANES), lambda hp, t: (jnp.minimum((step_of(t) + 1) * subs, last), first + hp))
    return cur, prev, nxt


def _attn_fwd(q, k, v, bias, sink, blk, d, name, v_col=0):
    seq = q.shape[0]
    subs, per_group, halo = _attn_geometry(blk, d)
    items, chunk, groups = subs * per_group, subs * halo, d // per_group
    nct = seq // chunk
    has_sink = sink is not None
    scale = HEAD_DIM ** -0.5

    def body(*refs):
        q_ref, kp, kc, kn, vp, vc, vn, b_ref = refs[:8]
        s_ref = refs[8] if has_sink else None
        o_ref, l_ref = refs[-2], refs[-1]
        t = pl.program_id(1)
        left = lax.broadcasted_iota(jnp.int32, (1, LANES), 1) < HEAD_DIM
        bias2 = b_ref[...]

        def group(r0):
            scores, vcats = [], []
            for j in range(items):
                s, r = _item_of(j, r0, per_group)
                qs = _stack_pair(_span_rows(q_ref, s * halo, r, blk, d) * scale, left)
                kcat = _key_rows(kp, kc, kn, s, r, subs, halo, blk, d).astype(BF16)
                scores.append(_dot_nt(qs, kcat) + bias2 + _item_penalty(t, nct, s, subs, blk))
                vcats.append(_key_rows(vp, vc, vn, s, r, subs, halo, blk, d).astype(BF16))
            ms = [jnp.max(s, axis=-1, keepdims=True) for s in scores]
            if has_sink:
                sk = s_ref[...]
                ms = [jnp.maximum(m, sk) for m in ms]
            ps = [jnp.exp(s - m) for s, m in zip(scores, ms)]
            dens = [jnp.sum(p, axis=-1, keepdims=True) for p in ps]
            if has_sink:
                dens = [den + jnp.exp(sk - m) for den, m in zip(dens, ms)]
            pns = [(p * (1.0 / den)).astype(BF16) for p, den in zip(ps, dens)]
            lses = [m + jnp.log(den) for m, den in zip(ms, dens)]
            for j in range(items):
                s, r = _item_of(j, r0, per_group)
                o2 = jnp.dot(pns[j], vcats[j], preferred_element_type=F32)
                _set_span_rows(o_ref, s * halo, r, blk, d, jnp.where(left, o2[:blk], o2[blk:]))
                _set_span_rows(l_ref, s * halo, r, blk, d, jnp.where(left, lses[j][:blk], lses[j][blk:]))

        _for_groups(group, groups, per_group)

    cur, prev, nxt = _attn_specs(seq, blk, d, lambda t: t)
    v_cur, v_prev, v_nxt = _attn_specs(seq, blk, d, lambda t: t, v_col)
    in_specs = [cur, prev, cur, nxt, v_prev, v_cur, v_nxt, pl.BlockSpec((2 * blk, 3 * blk), lambda hp, t: (hp, 0))]
    args = [q, k, k, k, v, v, v, bias]
    if has_sink:
        in_specs.append(pl.BlockSpec((2 * blk, 1), lambda hp, t: (hp, 0)))
        args.append(sink)
    return pl.pallas_call(
        body, out_shape=[jax.ShapeDtypeStruct((seq, WIDTH), F32)] * 2, grid=(N_PAIRS, nct),
        in_specs=in_specs, out_specs=[cur, cur], name=name,
        compiler_params=_params("parallel", "parallel"))(*args)


def _attn_bwd(q, k, v, do, lse, delta, bias, sink, blk, d, name, v_col=0):
    seq = q.shape[0]
    subs, per_group, halo = _attn_geometry(blk, d)
    items, chunk, groups = subs * per_group, subs * halo, d // per_group
    nct = seq // chunk
    has_sink = sink is not None
    n_in = 12 if has_sink else 11
    scale = HEAD_DIM ** -0.5

    def body(*refs):
        q_ref, kp, kc, kn, vp, vc, vn, do_ref, l_ref, d_ref, b_ref = refs[:11]
        s_ref = refs[11] if has_sink else None
        dq_ref, dk_ref, dv_ref, db_ref = refs[n_in:n_in + 4]
        ds_ref = refs[n_in + 4] if has_sink else None
        wk, wv = refs[-2], refs[-1]
        t = pl.program_id(1)

        @pl.when(t == 0)
        def _():
            wk[...] = jnp.zeros_like(wk)
            wv[...] = jnp.zeros_like(wv)
            db_ref[...] = jnp.zeros_like(db_ref)
            if has_sink:
                ds_ref[...] = jnp.zeros_like(ds_ref)

        @pl.when(t > 0)
        def _():
            for w in (wk, wv):
                keep = w[chunk:2 * chunk + halo]
                w[0:chunk + halo] = keep
                w[chunk + halo:2 * chunk + halo] = jnp.zeros((chunk, LANES), F32)

        @pl.when(t < nct)
        def _():
            lane = lax.broadcasted_iota(jnp.int32, (1, LANES), 1)
            left = lane < HEAD_DIM
            bias2 = b_ref[...]

            def group(r0):
                qss, doss, kcats, scores, dps, lcols, dcols = [], [], [], [], [], [], []
                for j in range(items):
                    s, r = _item_of(j, r0, per_group)
                    qs = _stack_pair(_span_rows(q_ref, s * halo, r, blk, d) * scale, left)
                    dos = _stack_pair(_span_rows(do_ref, s * halo, r, blk, d), left)
                    kcat = _key_rows(kp, kc, kn, s, r, subs, halo, blk, d).astype(BF16)
                    vcat = _key_rows(vp, vc, vn, s, r, subs, halo, blk, d).astype(BF16)
                    l2, d2 = _span_rows(l_ref, s * halo, r, blk, d), _span_rows(d_ref, s * halo, r, blk, d)
                    lcols.append(jnp.concatenate([jnp.max(jnp.where(left, l2, NEG_INF), axis=-1, keepdims=True),
                                                  jnp.max(jnp.where(left, NEG_INF, l2), axis=-1, keepdims=True)], axis=0))
                    dcols.append(jnp.concatenate([jnp.sum(jnp.where(lane == 0, d2, 0.0), axis=-1, keepdims=True),
                                                  jnp.sum(jnp.where(lane == HEAD_DIM, d2, 0.0), axis=-1, keepdims=True)],
                                                 axis=0))
                    scores.append(_dot_nt(qs, kcat) + bias2 + _item_penalty(t, nct, s, subs, blk))
                    dps.append(_dot_nt(dos, vcat))
                    qss.append(qs)
                    doss.append(dos)
                    kcats.append(kcat)
                ps = [jnp.exp(s - lc) for s, lc in zip(scores, lcols)]
                dss = [p * (dp - dc) for p, dp, dc in zip(ps, dps, dcols)]
                db_ref[...] += functools.reduce(lambda a, b: a + b, dss)
                if has_sink:
                    sk = s_ref[...]
                    ds_ref[...] -= functools.reduce(lambda a, b: a + b, [dc * jnp.exp(sk - lc) for dc, lc in zip(dcols, lcols)])
                dsbs = [ds.astype(BF16) for ds in dss]
                for j in range(items):
                    s, r = _item_of(j, r0, per_group)
                    dq2 = jnp.dot(dsbs[j], kcats[j], preferred_element_type=F32) * scale
                    _set_span_rows(dq_ref, s * halo, r, blk, d, jnp.where(left, dq2[:blk], dq2[blk:]))
                news = [(_dot_tn(dsbs[j], qss[j]), _dot_tn(ps[j].astype(BF16), doss[j])) for j in range(items)]
                for which, w in enumerate((wk, wv)):
                    for jr in range(per_group):
                        for sp in range(-1, subs + 1):
                            parts = [news[s * per_group + jr][which][(sp - s + 1) * blk:(sp - s + 2) * blk]
                                     for s in range(subs) if 0 <= sp - s + 1 < 3]
                            _set_span_rows(w, chunk + sp * halo, r0 + jr, blk, d,
                                           functools.reduce(lambda x, y: x + y, parts), add=True)

            _for_groups(group, groups, per_group)

        dk_ref[...] = wk[0:chunk]
        dv_ref[...] = wv[0:chunk]

    cur, prev, nxt = _attn_specs(seq, blk, d, lambda t: jnp.minimum(t, nct - 1))
    v_cur, v_prev, v_nxt = _attn_specs(seq, blk, d, lambda t: jnp.minimum(t, nct - 1), v_col)
    lag = pl.BlockSpec((chunk, LANES), lambda hp, t: (jnp.maximum(t - 1, 0), hp))
    band = pl.BlockSpec((2 * blk, 3 * blk), lambda hp, t: (hp, 0))
    col = pl.BlockSpec((2 * blk, 1), lambda hp, t: (hp, 0))
    in_specs = [cur, prev, cur, nxt, v_prev, v_cur, v_nxt, cur, cur, cur, band]
    args = [q, k, k, k, v, v, v, do, lse, delta, bias]
    out_shape = [jax.ShapeDtypeStruct((seq, WIDTH), F32)] * 3 + [jax.ShapeDtypeStruct((N_HEADS * blk, 3 * blk), F32)]
    out_specs = [cur, lag, lag, band]
    if has_sink:
        in_specs.append(col)
        args.append(sink)
        out_shape.append(jax.ShapeDtypeStruct((N_HEADS * blk, 1), F32))
        out_specs.append(col)
    window = pltpu.VMEM((2 * chunk + halo, LANES), F32)
    return pl.pallas_call(
        body, out_shape=out_shape, grid=(N_PAIRS, nct + 1), in_specs=in_specs, out_specs=out_specs,
        scratch_shapes=[window, window], name=name,
        compiler_params=_params("arbitrary", "arbitrary"))(*args)


def _combine_patterns(outs, lses):
    def combine(*tiles):
        os_, ls = tiles[:len(outs)], tiles[len(outs):]
        m = functools.reduce(jnp.maximum, ls)
        es = [jnp.exp(l - m) for l in ls]
        den = functools.reduce(lambda a, b: a + b, es)
        num = functools.reduce(lambda a, b: a + b, [e * o for e, o in zip(es, os_)])
        return num / den, m + jnp.log(den)

    return _ew(combine, [*outs, *lses], [F32, F32], "combine_a")


def _tile_gain(g, reps):
    return jnp.tile(g[None, :], (1, reps))


def _local_step(x, p, target, w_in_of, rest_of, small):
    rel_table = small["rel_table"]
    buckets_a = [_band_buckets(blk, d) for blk, d in DILATED]
    buckets_b = _band_buckets(BLK_B, 1)
    bias_a = [_bias_tiles(rel_table, bk, 0, "bias_a").reshape(N_HEADS * bk.shape[0], -1) for bk in buckets_a]
    bias_b = _bias_tiles(rel_table, buckets_b, N_HEADS, "bias_b").reshape(N_HEADS * BLK_B, -1)

    saved = []
    for l in range(DEPTH):
        g_mix, g_ffn, g_ple = (small[n][l][None, :] for n in ("norm_mix_g", "norm_ffn_g", "norm_ple_g"))
        gqa, gka, gqb = (_tile_gain(small[n][l], N_HEADS) for n in ("qnorm_a_g", "knorm_a_g", "qnorm_b_g"))
        gkb = _tile_gain(small["knorm_b_g"][l], N_KV_B)
        sink = jnp.repeat(small["sink_b"][l], BLK_B)[:, None]

        h = _rms_fwd(x, g_mix, "rms_mix") if l == 0 else h_next
        w_in = w_in_of(l, (h, bias_a, bias_b))
        proj = _mm(h, w_in, "nt", F32, "mm_in")
        qa, ka, qb, kb, vb = _qknorm_fwd(proj, gqa, gka, gqb, gkb)
        outs, lses = [], []
        for (blk, d), bias in zip(DILATED, bias_a):
            o, ls = _attn_fwd(qa, ka, proj, bias, None, blk, d, f"attn_a{d}_fwd", v_col=OFF_VA)
            outs.append(o)
            lses.append(ls)
        ya, lse_a = _combine_patterns(outs, lses)
        yb, lse_b = _attn_fwd(qb, kb, vb, bias_b, sink, BLK_B, 1, "attn_b_fwd")
        w = dict(rest_of(l, yb), w_in=w_in)
        def gate(products, extra):
            (ca_, cb_), (ga_, gb_) = products, extra
            return _sigmoid(ga_) * ca_ + _sigmoid(gb_) * cb_, ca_, cb_

        merged, ca, cb = _mm_fused([(ya, w["w_branch_a"], "nn"), (yb, w["w_branch_b"], "nn")],
                                   [(proj, OFF_GA), (proj, OFF_GB)], gate, [BF16, BF16, BF16], "mm_branches_gate")
        x1, h2 = _mm_res_norm(merged, w["w_out"], "nn", x, g_ffn, "mm_out_norm")

        def swiglu(products, extra):
            a_, u_ = products
            return (a_ * _sigmoid(a_)) * u_, a_, u_

        hid, a, u = _mm_fused([(h2, w["w_ffn_gate"], "nt"), (h2, w["w_ffn_up"], "nt")], [], swiglu,
                              [BF16, BF16, BF16], "mm_ffn_gate_up")
        x2, h3 = _mm_res_norm(hid, w["w_ffn_down"], "nn", x1, g_ple, "mm_ffn_down_norm")

        def ple(products, extra):
            z_, e_ = products
            return extra[0] + _sigmoid(z_) * e_, z_, e_

        next_gain = small["norm_mix_g"][l + 1][None, :] if l + 1 < DEPTH else None
        x3, z, e, *rest = _mm_fused([(h3, w["w_ple_gate"], "nn"), (p[l], w["w_ple_proj"], "nn")], [(x2, 0)], ple,
                                    [F32, BF16, BF16], "mm_ple", next_gain=next_gain)
        h_next = rest[0] if rest else None
        saved.append(dict(w=w, x0=x, h=h, proj=proj, qa=qa, ka=ka, qb=qb, kb=kb, vb=vb, ya=ya, lse_a=lse_a,
                          yb=yb, lse_b=lse_b, ca=ca, cb=cb, merged=merged, x1=x1, h2=h2, a=a, u=u, hid=hid,
                          x2=x2, h3=h3, z=z, e=e))
        x = x3

    dx, de, dz, loss_acc = _loss_grad(x, target, saved[-1]["z"], saved[-1]["e"])
    loss = loss_acc[0, 0]

    gbig = [{} for _ in range(DEPTH)]
    marks = [{} for _ in range(DEPTH)]
    gsmall = {n: [None] * DEPTH for n in SMALL if n != "rel_table"}
    dbias_a = [[] for _ in DILATED]
    dbias_b = []

    for l in reversed(range(DEPTH)):
        sv = saved[l]
        w = sv["w"]
        g_mix, g_ffn, g_ple = (small[n][l][None, :] for n in ("norm_mix_g", "norm_ffn_g", "norm_ple_g"))
        gqa, gka, gqb = (_tile_gain(small[n][l], N_HEADS) for n in ("qnorm_a_g", "knorm_a_g", "qnorm_b_g"))
        gkb = _tile_gain(small["knorm_b_g"][l], N_KV_B)
        sink = jnp.repeat(small["sink_b"][l], BLK_B)[:, None]

        if l < DEPTH - 1:
            de, dz = _ew(_ple_bwd, [dx, sv["z"], sv["e"]], [BF16, BF16], "ple_bwd")
        gbig[l]["w_ple_proj"] = _mm(p[l], de, "tn", BF16, "mm_d_ple_proj")
        gbig[l]["w_ple_gate"] = _mm(sv["h3"], dz, "tn", BF16, "mm_d_ple_gate")
        dx, dxb, gsmall["norm_ple_g"][l] = _mm_rms_bwd([(dz, w["w_ple_gate"], "nt")], sv["x2"], g_ple, dx,
                                                      "mm_dh3_rms_bwd")

        gbig[l]["w_ffn_down"] = _mm(sv["hid"], dxb, "tn", BF16, "mm_d_ffn_down")

        def swiglu_bwd(products, extra):
            dh_, a_, u_ = products[0], extra[0].astype(F32), extra[1].astype(F32)
            s = _sigmoid(a_)
            return dh_ * u_ * (s * (1.0 + a_ * (1.0 - s))), dh_ * (a_ * s)

        da, du = _mm_fused([(dxb, w["w_ffn_down"], "nt")], [(sv["a"], 0), (sv["u"], 0)], swiglu_bwd, [BF16, BF16],
                           "mm_dhid_swiglu_bwd")
        gbig[l]["w_ffn_gate"] = _mm(da, sv["h2"], "tn", BF16, "mm_d_ffn_gate")
        gbig[l]["w_ffn_up"] = _mm(du, sv["h2"], "tn", BF16, "mm_d_ffn_up")
        dx, dxb, gsmall["norm_ffn_g"][l] = _mm_rms_bwd([(da, w["w_ffn_gate"], "nn"), (du, w["w_ffn_up"], "nn")],
                                                      sv["x1"], g_ffn, dx, "mm_dh2_rms_bwd")

        gbig[l]["w_out"] = _mm(sv["merged"], dxb, "tn", BF16, "mm_d_out")

        def gate_bwd(products, extra):
            dm_, ca_, cb_ = products[0], extra[0].astype(F32), extra[1].astype(F32)
            sa, sb = _sigmoid(extra[2]), _sigmoid(extra[3])
            return dm_ * sa, dm_ * sb, dm_ * ca_ * (sa * (1.0 - sa)), dm_ * cb_ * (sb * (1.0 - sb))

        dca, dcb, dga, dgb = _mm_fused(
            [(dxb, w["w_out"], "nt")], [(sv["ca"], 0), (sv["cb"], 0), (sv["proj"], OFF_GA), (sv["proj"], OFF_GB)],
            gate_bwd, [BF16, BF16, BF16, BF16], "mm_dmerged_gate_bwd")
        gbig[l]["w_branch_a"] = _mm(sv["ya"], dca, "tn", BF16, "mm_d_branch_a")
        gbig[l]["w_branch_b"] = _mm(sv["yb"], dcb, "tn", BF16, "mm_d_branch_b")
        def with_row_dots(products, extra):
            return products[0], _seg_sum(products[0] * extra[0])

        dya, delta_a = _mm_fused([(dca, w["w_branch_a"], "nt")], [(sv["ya"], 0)], with_row_dots, [F32, F32], "mm_dya")
        dyb, delta_b = _mm_fused([(dcb, w["w_branch_b"], "nt")], [(sv["yb"], 0)], with_row_dots, [F32, F32], "mm_dyb")

        dqa, dka, dva = [], [], []
        for (blk, d), bias, bk in zip(DILATED, bias_a, buckets_a):
            dq_, dk_, dv_, db_ = _attn_bwd(sv["qa"], sv["ka"], sv["proj"], dya, sv["lse_a"], delta_a, bias, None, blk, d,
                                           f"attn_a{d}_bwd", v_col=OFF_VA)
            dqa.append(dq_)
            dka.append(dk_)
            dva.append(dv_)
            dbias_a[len(dqa) - 1].append(db_)
        dqb, dkb, dvb, db_, dsink = _attn_bwd(sv["qb"], sv["kb"], sv["vb"], dyb, sv["lse_b"], delta_b, bias_b, sink,
                                              BLK_B, 1, "attn_b_bwd")
        dbias_b.append(db_)
        gsmall["sink_b"][l] = dsink.reshape(N_HEADS, BLK_B).sum(axis=1)

        dproj, pqa, pka, pqb, pkb = _qknorm_bwd(sv["proj"], gqa, gka, gqb, gkb, dqa, dka, dva, dqb, dkb, dvb, dga, dgb)
        marks[l]["attn_bwd_done"] = dproj
        gsmall["qnorm_a_g"][l] = pqa.reshape(N_HEADS, HEAD_DIM).sum(0)
        gsmall["knorm_a_g"][l] = pka.reshape(N_HEADS, HEAD_DIM).sum(0)
        gsmall["qnorm_b_g"][l] = pqb.reshape(N_HEADS, HEAD_DIM).sum(0)
        gsmall["knorm_b_g"][l] = pkb.reshape(N_KV_B, HEAD_DIM).sum(0)
        gbig[l]["w_in"] = _mm(dproj, sv["h"], "tn", BF16, "mm_d_in")
        dx, _, gsmall["norm_mix_g"][l] = _mm_rms_bwd([(dproj, w["w_in"], "nn")], sv["x0"], g_mix, dx, "mm_dh_rms_bwd")
        gsmall["norm_mix_g"][l] = gsmall["norm_mix_g"][l][0]
        gsmall["norm_ffn_g"][l] = gsmall["norm_ffn_g"][l][0]
        gsmall["norm_ple_g"][l] = gsmall["norm_ple_g"][l][0]

    gsmall = {n: jnp.stack(v) for n, v in gsmall.items()}
    dtable_a = sum(_table_grad(sum(dbs).reshape(N_HEADS, blk, 3 * blk), bk, "table_grad_a")
                   for dbs, (blk, _), bk in zip(dbias_a, DILATED, buckets_a))
    dtable_b = _table_grad(sum(dbias_b).reshape(N_HEADS, BLK_B, 3 * BLK_B), buckets_b, "table_grad_b")
    gsmall["rel_table"] = jnp.concatenate([dtable_a, dtable_b], axis=0).T
    return loss, dx, gbig, gsmall, marks


def _place():
    return lax.axis_index("x"), lax.axis_index("y"), lax.axis_index("c")


def _flip(v, bit):
    return 1 - v if bit else v


CHIP_RELATIONS = ((0, 1), (1, 0), (1, 1))
ANY = pl.BlockSpec(memory_space=pl.ANY)


def _allgather_body(w_refs, out_refs, send_sems, recv_sems):
    x, y, c = _place()
    chips = [(_flip(x, a), _flip(y, b)) for a, b in CHIP_RELATIONS]

    def make(g):
        w_ref, out_ref = w_refs[g], out_refs[g]
        half = w_ref.shape[0] // 2

        def part(px, py, pc):
            return out_ref.at[2 * px + py, pl.ds(pc * half, half), :]

        def copy(k, block, to, src=None):
            return pltpu.make_async_remote_copy(
                src_ref=part(*block) if src is None else src, dst_ref=part(*block),
                send_sem=send_sems.at[7 * g + k], recv_sem=recv_sems.at[7 * g + k], device_id=to,
                device_id_type=MESH_ID)

        own = pltpu.make_async_remote_copy(
            src_ref=w_ref, dst_ref=out_ref.at[2 * x + y], send_sem=send_sems.at[7 * g + 6],
            recv_sem=recv_sems.at[7 * g + 6], device_id=(x, y, 1 - c), device_id_type=MESH_ID)
        first = [copy(k, (x, y, c), (*chip, c), src=w_ref.at[pl.ds(c * half, half), :]) for k, chip in enumerate(chips)]
        passed = [copy(3 + k, (*chip, c), (x, y, 1 - c)) for k, chip in enumerate(chips)]
        arrive = [copy(k, (*chip, c), (x, y, c)) for k, chip in enumerate(chips)]
        arrive2 = [copy(3 + k, (*chip, 1 - c), (x, y, c)) for k, chip in enumerate(chips)]
        return own, first, passed, arrive, arrive2

    made = [make(g) for g in range(len(w_refs))]
    for own, first, _, _, _ in made:
        own.start()
        for cp in first:
            cp.start()
    for _, _, passed, arrive, _ in made:
        for k in range(3):
            arrive[k].wait_recv()
            passed[k].start()
    for own, first, passed, _, arrive2 in made:
        for k in range(3):
            arrive2[k].wait_recv()
        own.wait_recv()
        for cp in first + passed + [own]:
            cp.wait_send()


def _sibling(x, y, c):
    return [(x, y, 1 - c)]


def _same_core_of_other_chips(x, y, c):
    return [(_flip(x, a), _flip(y, b), c) for a, b in CHIP_RELATIONS]


def _exchange(body, ins, out_types, n_sems, name, sequencer=None):
    n = len(ins)
    sems = (pltpu.SemaphoreType.DMA((n_sems,)), pltpu.SemaphoreType.DMA((n_sems,)))
    if sequencer is None:
        in_place = out_types is None
        out_shape = [jax.ShapeDtypeStruct(a.shape, a.dtype) for a in ins] if in_place else out_types

        def tc_body(*refs):
            body(refs[:n], refs[n:n + len(out_shape)], refs[-2], refs[-1])

        return list(pl.pallas_call(
            tc_body, out_shape=out_shape, in_specs=[ANY] * n, out_specs=[ANY] * len(out_shape),
            input_output_aliases={g: g for g in range(n)} if in_place else {}, scratch_shapes=list(sems), name=name)(*ins))

    collective_id, peers = sequencer
    hbm = pltpu.MemorySpace.HBM
    in_refs = [jax.new_ref(a, memory_space=hbm) for a in ins]
    out_refs = in_refs if out_types is None else [jax.empty_ref(t, memory_space=hbm) for t in out_types]

    @pl.kernel(mesh=plsc.ScalarSubcoreMesh(axis_name="sequencer", num_cores=1), name=name, scratch_types=sems,
               compiler_params=pltpu.CompilerParams(collective_id=collective_id))
    def launch(send_sems, recv_sems):
        barrier = pltpu.get_barrier_semaphore()
        devices = peers(*_place())
        for device in devices:
            pl.semaphore_signal(barrier, inc=1, device_id=device, device_id_type=MESH_ID)
        pl.semaphore_wait(barrier, len(devices))
        body(in_refs, out_refs, send_sems, recv_sems)

    launch()
    return [r[...] for r in out_refs]


def _allgather(shards, name, sequencer=None):
    out_types = [jax.ShapeDtypeStruct((N_CHIPS,) + s.shape, s.dtype) for s in shards]
    if sequencer is not None:
        sequencer = (sequencer, lambda x, y, c: _sibling(x, y, c) + _same_core_of_other_chips(x, y, c))
    return _exchange(_allgather_body, shards, out_types, 7 * len(shards), name, sequencer)


def _half_tile(half):
    return max(t for t in range(16, 1025, 16) if half % t == 0)


def _run_copies(cps):
    for cp in cps:
        cp.start()
    for cp in cps:
        cp.wait_recv()
    for cp in cps:
        cp.wait_send()


def _sibling_halves(gsends, name, sequencer=None):
    def body(g_refs, out_refs, send_sems, recv_sems):
        x, y, c = _place()
        cps = []
        for g, (g_ref, out_ref) in enumerate(zip(g_refs, out_refs)):
            half = g_ref.shape[1] // 2
            cps.append(pltpu.make_async_remote_copy(
                src_ref=g_ref.at[:, pl.ds((1 - c) * half, half), :], dst_ref=out_ref,
                send_sem=send_sems.at[g], recv_sem=recv_sems.at[g], device_id=(x, y, 1 - c), device_id_type=MESH_ID))
        _run_copies(cps)

    out_types = [jax.ShapeDtypeStruct((s.shape[0], s.shape[1] // 2, s.shape[2]), s.dtype) for s in gsends]
    return _exchange(body, gsends, out_types, len(gsends), name, sequencer and (sequencer, _sibling))


def _chip_sums(gsend, sib, place):
    n, rows, cols = gsend.shape
    half = rows // 2
    tm = _half_tile(half)
    nblk = half // tm

    def body(s_ref, g_ref, sib_ref, o_ref):
        o_ref[0] = (g_ref[0].astype(F32) + sib_ref[0].astype(F32)).astype(o_ref.dtype)

    grid_spec = pltpu.PrefetchScalarGridSpec(
        num_scalar_prefetch=1, grid=(n, nblk),
        in_specs=[pl.BlockSpec((1, tm, cols), lambda k, i, s: (jnp.bitwise_xor(s[0], k), s[1] * nblk + i, 0)),
                  pl.BlockSpec((1, tm, cols), lambda k, i, s: (jnp.bitwise_xor(s[0], k), i, 0))],
        out_specs=pl.BlockSpec((1, tm, cols), lambda k, i, s: (k, i, 0)))
    return pl.pallas_call(
        body, out_shape=jax.ShapeDtypeStruct((n, half, cols), BF16), grid_spec=grid_spec,
        name="rs_chip_sums", compiler_params=_params("parallel", "parallel"))(place, gsend, sib)


def _exchange_chip_sums(tsends, name, sequencer=None):
    def body(t_refs, out_refs, send_sems, recv_sems):
        x, y, c = _place()
        cps = []
        for g, (t_ref, out_ref) in enumerate(zip(t_refs, out_refs)):
            for k, device in enumerate(_same_core_of_other_chips(x, y, c)):
                cps.append(pltpu.make_async_remote_copy(
                    src_ref=t_ref.at[k + 1], dst_ref=out_ref.at[k], send_sem=send_sems.at[3 * g + k],
                    recv_sem=recv_sems.at[3 * g + k], device_id=device, device_id_type=MESH_ID))
        _run_copies(cps)

    out_types = [jax.ShapeDtypeStruct((3,) + s.shape[1:], s.dtype) for s in tsends]
    return _exchange(body, tsends, out_types, 3 * len(tsends), name,
                     sequencer and (sequencer, _same_core_of_other_chips))


def _final_sum(tsend, recv, place):
    n, half, cols = tsend.shape
    tm = _half_tile(half)
    nblk = half // tm

    def body(s_ref, t_ref, r_ref, o_ref):
        o_ref[...] = ((t_ref[0].astype(F32) + r_ref[0].astype(F32)) + r_ref[1].astype(F32)) + r_ref[2].astype(F32)

    grid_spec = pltpu.PrefetchScalarGridSpec(
        num_scalar_prefetch=1, grid=(nblk,),
        in_specs=[pl.BlockSpec((1, tm, cols), lambda i, s: (0, i, 0)), pl.BlockSpec((n - 1, tm, cols), lambda i, s: (0, i, 0))],
        out_specs=pl.BlockSpec((tm, cols), lambda i, s: (s[1] * nblk + i, 0)))
    return pl.pallas_call(
        body, out_shape=jax.ShapeDtypeStruct((2 * half, cols), F32), grid_spec=grid_spec, name="rs_final_sum",
        compiler_params=_params("parallel"))(place, tsend, recv)


def _join_halves(gfulls, name, sequencer=None):
    def body(g_refs, out_refs, send_sems, recv_sems):
        x, y, c = _place()
        n = len(g_refs)

        def copy(g, pc):
            half = g_refs[g].shape[0] // 2
            return pltpu.make_async_remote_copy(
                src_ref=g_refs[g].at[pl.ds(pc * half, half), :], dst_ref=out_refs[g].at[pl.ds(pc * half, half), :],
                send_sem=send_sems.at[g], recv_sem=recv_sems.at[g], device_id=(x, y, 1 - c), device_id_type=MESH_ID)

        mine = [copy(g, c) for g in range(n)]
        for cp in mine:
            cp.start()
        for g in range(n):
            copy(g, 1 - c).wait_recv()
        for cp in mine:
            cp.wait_send()

    return _exchange(body, gfulls, None, len(gfulls), name, sequencer and (sequencer, _sibling))


def _allreduce_small(v):
    rows, cols = v.shape

    def body(v_ref, out_ref, buf, send_sems, recv_sems):
        x, y, c = _place()
        cps = []
        for k in range(1, 8):
            peer = (_flip(x, (k >> 2) & 1), _flip(y, (k >> 1) & 1), _flip(c, k & 1))
            cps.append(pltpu.make_async_remote_copy(
                src_ref=v_ref, dst_ref=buf.at[k - 1], send_sem=send_sems.at[k - 1], recv_sem=recv_sems.at[k - 1],
                device_id=peer, device_id_type=MESH_ID))
        for cp in cps:
            cp.start()
        for cp in cps:
            cp.wait_recv()
        for cp in cps:
            cp.wait_send()
        t0 = v_ref[...] + buf[0]
        t1 = buf[1] + buf[2]
        t2 = buf[3] + buf[4]
        t3 = buf[5] + buf[6]
        out_ref[...] = (t0 + t1) + (t2 + t3)

    vm = pl.BlockSpec(memory_space=pltpu.VMEM)
    return pl.pallas_call(
        body, out_shape=jax.ShapeDtypeStruct((rows, cols), F32), in_specs=[vm], out_specs=vm,
        scratch_shapes=[pltpu.VMEM((7, rows, cols), F32), pltpu.SemaphoreType.DMA((7,)), pltpu.SemaphoreType.DMA((7,))],
        name="allreduce_small")(v)


BIG_INFO = {n: (shape, ax) for n, shape, ax in BIG}
GROUPS = (("w_in",), ("w_ffn_gate", "w_ffn_up", "w_ffn_down", "w_out", "w_ple_gate"),
          ("w_branch_a", "w_branch_b", "w_ple_proj"))


def _shard_shape(name):
    (k, m), ax = BIG_INFO[name]
    return (k // N_CHIPS, m) if ax == 0 else (k, m // N_CHIPS)


def _group_rows(group):
    offs, off = {}, 0
    for n in group:
        offs[n] = off
        off += _shard_shape(n)[0]
    return offs, off


def _pack_groups(shards, layer, dtype):
    return [jnp.concatenate([shards[n][layer].astype(dtype) for n in group], axis=0) for group in GROUPS]


def _unpack_full(gathered, groups):
    out = {}
    for group, arr in zip(groups, gathered):
        offs, _ = _group_rows(group)
        for n in group:
            rows, cols = _shard_shape(n)
            (k, m), ax = BIG_INFO[n]
            slab = arr[:, offs[n]:offs[n] + rows]
            out[n] = slab.reshape(k, m) if ax == 0 else jnp.transpose(slab, (1, 0, 2)).reshape(k, m)
    return out


def _pack_grads(gfull):
    out = []
    for group in GROUPS:
        parts = []
        for n in group:
            rows, cols = _shard_shape(n)
            ax = BIG_INFO[n][1]
            slab = (gfull[n].reshape(N_CHIPS, rows, cols) if ax == 0
                    else jnp.transpose(gfull[n].reshape(rows, N_CHIPS, cols), (1, 0, 2)))
            parts.append(slab)
        out.append(jnp.concatenate(parts, axis=1))
    return out


def _after(values, mark):
    values, _ = lax.optimization_barrier((values, mark))
    return values


def _reduce_scatter_begin(gsends, place, tag, ids):
    sibs = _sibling_halves(gsends, "rs_sibling_halves_" + tag, ids[0])
    tsends = [_chip_sums(g, s, place) for g, s in zip(gsends, sibs)]
    return tsends, _exchange_chip_sums(tsends, "rs_exchange_" + tag, ids[1])


def _reduce_scatter_finish(begun, place, tag, ids, hold):
    tsends, recvs = begun
    recvs = _after(recvs, hold)
    return _join_halves([_final_sum(t, r, place) for t, r in zip(tsends, recvs)], "rs_join_halves_" + tag, ids[2])


SMALL_SHAPES = {"rel_table": (NUM_BUCKETS, 2 * N_HEADS), "norm_mix_g": (DEPTH, D_MODEL), "qnorm_a_g": (DEPTH, HEAD_DIM),
                "knorm_a_g": (DEPTH, HEAD_DIM), "qnorm_b_g": (DEPTH, HEAD_DIM), "knorm_b_g": (DEPTH, HEAD_DIM),
                "sink_b": (DEPTH, N_HEADS), "norm_ffn_g": (DEPTH, D_MODEL), "norm_ple_g": (DEPTH, D_MODEL)}


def _pack_small(vals, last=None):
    flat = jnp.concatenate([vals[n].astype(F32).reshape(-1) for n in SMALL])
    tail = jnp.zeros((SMALL_ROWS * LANES - flat.shape[0],), F32)
    if last is not None:
        tail = tail.at[-1].set(last)
    return jnp.concatenate([flat, tail]).reshape(SMALL_ROWS, LANES)


def _unpack_small(packed):
    flat, out, off = packed.reshape(-1), {}, 0
    for n in SMALL:
        size = math.prod(SMALL_SHAPES[n])
        out[n] = flat[off:off + size].reshape(SMALL_SHAPES[n])
        off += size
    return out


def _adamw(w, gs, g_row, m, v, name):
    c1 = 1.0 - ADAM_B1 ** ADAM_STEP
    c2 = 1.0 - ADAM_B2 ** ADAM_STEP
    total, width = w.shape
    n_layers = len(gs)
    per = total // n_layers
    tm = max(t for t in range(8, 513, 8) if per % t == 0 and g_row % t == 0)
    nblk = per // tm

    def body(*refs):
        w_ref, g_refs = refs[0], refs[1:1 + n_layers]
        m_ref, v_ref, og, od, om, ov = refs[1 + n_layers:]
        layer = pl.program_id(0) // nblk
        g = g_refs[0][...]
        for l in range(1, n_layers):
            g = jnp.where(layer == l, g_refs[l][...], g)
        m_new = ADAM_B1 * m_ref[...] + (1.0 - ADAM_B1) * g
        v_new = ADAM_B2 * v_ref[...] + (1.0 - ADAM_B2) * (g * g)
        og[...] = g
        od[...] = -ADAM_LR * ((m_new / c1) / (jnp.sqrt(v_new / c2) + ADAM_EPS) + ADAM_WD * w_ref[...])
        om[...] = m_new
        ov[...] = v_new

    row = pl.BlockSpec((tm, width), lambda i: (i, 0))
    g_specs = [pl.BlockSpec((tm, width), lambda i, l=l: (g_row // tm + jnp.clip(i - l * nblk, 0, nblk - 1), 0))
               for l in range(n_layers)]
    return pl.pallas_call(
        body, out_shape=[jax.ShapeDtypeStruct((total, width), F32)] * 4, grid=(total // tm,),
        in_specs=[row] + g_specs + [row, row], out_specs=[row] * 4, name=name,
        compiler_params=_params("parallel"))(w, *gs, m, v)


def kernel(x, p, rel_table, norm_mix_g, w_in, qnorm_a_g, knorm_a_g, qnorm_b_g, knorm_b_g, sink_b, w_branch_a, w_branch_b, w_out, norm_ffn_g, w_ffn_gate, w_ffn_up, w_ffn_down, norm_ple_g, w_ple_gate, w_ple_proj, loss_target, m_rel_table, m_norm_mix_g, m_w_in, m_qnorm_a_g, m_knorm_a_g, m_qnorm_b_g, m_knorm_b_g, m_sink_b, m_w_branch_a, m_w_branch_b, m_w_out, m_norm_ffn_g, m_w_ffn_gate, m_w_ffn_up, m_w_ffn_down, m_norm_ple_g, m_w_ple_gate, m_w_ple_proj, v_rel_table, v_norm_mix_g, v_w_in, v_qnorm_a_g, v_knorm_a_g, v_qnorm_b_g, v_knorm_b_g, v_sink_b, v_w_branch_a, v_w_branch_b, v_w_out, v_norm_ffn_g, v_w_ffn_gate, v_w_ffn_up, v_w_ffn_down, v_norm_ple_g, v_w_ple_gate, v_w_ple_proj):
    given = dict(locals())

    def held(name, a):
        return jnp.swapaxes(a, 1, 2) if name in TRANSPOSED else a

    weights = {n: held(n, given[n]) for n in WEIGHTS}
    moments_m = {n: held(n, given["m_" + n]) for n in WEIGHTS}
    moments_v = {n: held(n, given["v_" + n]) for n in WEIGHTS}
    xi, yi, ci = _place()
    place = jnp.stack([2 * xi + yi, ci]).astype(jnp.int32)

    shards = [_pack_groups(weights, l, BF16) for l in range(DEPTH)]
    w_in0 = _allgather(shards[0][:1], "allgather_w_in_layer0", sequencer=9)
    rest0 = _allgather(_after(shards[0][1:], w_in0), "allgather_rest_layer0", sequencer=1)
    gathered = [w_in0 + rest0, None]
    small = {n: weights[n] for n in SMALL}

    def w_in_of(l, mark):
        return _unpack_full(_after(gathered[l][:1], (shards[1], mark) if l == 0 else mark), GROUPS[:1])["w_in"]

    def rest_of(l, mark):
        if l == 0:
            gathered[1] = _allgather(_after(shards[1], mark), "allgather_layer1", sequencer=2)
        return _unpack_full(_after(gathered[l][1:], mark), GROUPS[1:])

    loss, dx, gbig, gsmall, marks = _local_step(x[0], p[:, 0], loss_target[0], w_in_of, rest_of, small)

    gsends = [_pack_grads(gbig[l]) for l in range(DEPTH)]
    stages = {"layer1": (gsends[1], (3, 4, 5)), "rest_layer0": (gsends[0][1:], (6, 7, 8)),
              "w_in_layer0": (gsends[0][:1], (10, 11, 12))}
    begun = {tag: _reduce_scatter_begin(g, place, tag, ids) for tag, (g, ids) in stages.items()}

    def finish(tag, hold):
        return _reduce_scatter_finish(begun[tag], place, tag, stages[tag][1], hold)

    red1 = finish("layer1", marks[0]["attn_bwd_done"])
    rest0 = finish("rest_layer0", marks[0]["attn_bwd_done"])

    grads, delta, new_m, new_v = {}, {}, {}, {}

    def update(group, reduced):
        offs, _ = _group_rows(group)
        for n in group:
            shape = weights[n].shape
            two_d = lambda a: a.reshape(shape[0] * shape[1], shape[2])
            outs = _adamw(two_d(weights[n]), reduced, offs[n], two_d(moments_m[n]), two_d(moments_v[n]), "adamw_" + n)
            grads[n], delta[n], new_m[n], new_v[n] = (held(n, o.reshape(shape)) for o in outs)

    for gi in (1, 2):
        update(GROUPS[gi], _after([rest0[gi - 1], red1[gi]], begun["w_in_layer0"][0]))
    small_grads = _allreduce_small(_pack_small(gsmall, last=loss))
    g_, d_, m_, v_ = _adamw(_pack_small(weights), [small_grads], 0, _pack_small(moments_m), _pack_small(moments_v),
                            "adamw_small")
    grads.update(_unpack_small(g_))
    delta.update(_unpack_small(d_))
    new_m.update(_unpack_small(m_))
    new_v.update(_unpack_small(v_))
    others_done = [dx, d_] + [delta[n] for gi in (1, 2) for n in GROUPS[gi]]
    update(GROUPS[0], [finish("w_in_layer0", others_done)[0], red1[0]])

    return (small_grads[-1, -1], dx[None], *[grads[n] for n in WEIGHTS], *[delta[n] for n in WEIGHTS],
            *[new_m[n] for n in WEIGHTS], *[new_v[n] for n in WEIGHTS])
```

```python
import functools
import math

import jax
import jax.numpy as jnp
from jax import lax
from jax.experimental import pallas as pl
from jax.experimental.pallas import tpu as pltpu
from jax.experimental.pallas import tpu_sc as plsc

F32 = jnp.float32
BF16 = jnp.bfloat16
MESH_ID = pl.DeviceIdType.MESH

D_MODEL = 1024
DEPTH = 2
HEAD_DIM = 64
N_HEADS = 8
WIDTH = N_HEADS * HEAD_DIM
N_PAIRS = 4
ITEMS = 16
MAX_CHUNK = 1024
N_KV_B = 2
PLE_DIM = 256
D_FF = 2816
D_IN = 4352
OFF_QA, OFF_KA, OFF_VA, OFF_QB, OFF_KB, OFF_VB, OFF_GA, OFF_GB = 0, 512, 1024, 1536, 2048, 2176, 2304, 3328
DILATED = ((64, 1), (64, 4), (64, 16))
BLK_B = 128
NUM_BUCKETS = 32
MAX_DISTANCE = 1024
RMS_EPS = 1e-6
NEG_INF = -1e30
LANES = 128
ROW_TILE = 256
VMEM_LIMIT = 48 * 1024 * 1024

ADAM_LR, ADAM_B1, ADAM_B2, ADAM_EPS, ADAM_WD, ADAM_STEP = 0.001, 0.9, 0.999, 1e-08, 0.01, 10

TRANSPOSED = ("w_in", "w_ffn_gate", "w_ffn_up")
BIG = (
    ("w_in", (D_IN, D_MODEL), 0),
    ("w_branch_a", (WIDTH, D_MODEL), 1),
    ("w_branch_b", (WIDTH, D_MODEL), 1),
    ("w_out", (D_MODEL, D_MODEL), 0),
    ("w_ffn_gate", (D_FF, D_MODEL), 0),
    ("w_ffn_up", (D_FF, D_MODEL), 0),
    ("w_ffn_down", (D_FF, D_MODEL), 0),
    ("w_ple_gate", (D_MODEL, D_MODEL), 0),
    ("w_ple_proj", (PLE_DIM, D_MODEL), 1),
)
SMALL = ("rel_table", "norm_mix_g", "qnorm_a_g", "knorm_a_g", "qnorm_b_g", "knorm_b_g", "sink_b",
         "norm_ffn_g", "norm_ple_g")
WEIGHTS = ("rel_table", "norm_mix_g", "w_in", "qnorm_a_g", "knorm_a_g", "qnorm_b_g", "knorm_b_g", "sink_b",
           "w_branch_a", "w_branch_b", "w_out", "norm_ffn_g", "w_ffn_gate", "w_ffn_up", "w_ffn_down",
           "norm_ple_g", "w_ple_gate", "w_ple_proj")
N_CHIPS = 4
SMALL_ROWS = 64


def _params(*sem):
    return pltpu.CompilerParams(dimension_semantics=sem, vmem_limit_bytes=VMEM_LIMIT)


MM_VMEM_BUDGET = 40 * 1024 * 1024
STEP_OVERHEAD_S = 0.4e-6
TILE_DMA_BYTES_PER_S = 1.5e12


def _mm_dims(a, b, mode):
    if mode == "nn":
        return a.shape[0], b.shape[1], a.shape[1]
    if mode == "nt":
        return a.shape[0], b.shape[0], a.shape[1]
    return a.shape[1], b.shape[1], a.shape[0]


def _mm_tiles(m, n, pairs, tile_bytes, col_offsets, full_rows=False):
    best = None
    widths = [n] if full_rows else [t for t in range(LANES, n + 1, LANES) if n % t == 0 and all(o % t == 0 for o in col_offsets)]
    for tm in (t for t in range(LANES, m + 1, LANES) if m % t == 0):
        for tn in widths:
            io = sum(tm * k * ab + tn * k * bb for k, ab, bb in pairs) + tm * tn * sum(tile_bytes)
            casts = sum((tm * k * 2 if ab == 4 else 0) + (tn * k * 2 if bb == 4 else 0) for k, ab, bb in pairs)
            if 2 * io + len(pairs) * tm * tn * 4 + casts > MM_VMEM_BUDGET:
                continue
            cost = (m // tm) * (n // tn) * STEP_OVERHEAD_S + io / TILE_DMA_BYTES_PER_S
            if best is None or (cost, -tm) < best[0]:
                best = ((cost, -tm), tm, tn)
    return best[1], best[2]


def _mm_fused(pairs, extras, epilogue, out_dtypes, name, next_gain=None):
    m, n, _ = _mm_dims(*pairs[0])
    assert all(_mm_dims(*p)[:2] == (m, n) for p in pairs)
    with_norm = next_gain is not None
    out_dtypes = list(out_dtypes) + ([BF16] if with_norm else [])
    tm, tn = _mm_tiles(
        m, n, [(_mm_dims(a, b, mode)[2], a.dtype.itemsize, b.dtype.itemsize) for a, b, mode in pairs],
        [e.dtype.itemsize for e, _ in extras] + [jnp.dtype(d).itemsize for d in out_dtypes], [off for _, off in extras],
        full_rows=with_norm)
    dims = {"nn": (((1,), (0,)), ((), ())), "nt": (((1,), (1,)), ((), ())), "tn": (((0,), (0,)), ((), ()))}
    in_specs, args = [], []
    for a, b, mode in pairs:
        k = _mm_dims(a, b, mode)[2]
        in_specs.append(pl.BlockSpec((k, tm), lambda i, j: (0, i)) if mode == "tn" else pl.BlockSpec((tm, k), lambda i, j: (i, 0)))
        in_specs.append(pl.BlockSpec((tn, k), lambda i, j: (j, 0)) if mode == "nt" else pl.BlockSpec((k, tn), lambda i, j: (0, j)))
        args += [a, b]
    for e, off in extras:
        in_specs.append(pl.BlockSpec((tm, tn), lambda i, j, o=off // tn: (i, o + j)))
        args.append(e)
    n_pairs, n_tiles = len(pairs), 2 * len(pairs) + len(extras)
    if with_norm:
        in_specs.append(pl.BlockSpec((1, n), lambda i, j: (0, 0)))
        args.append(next_gain)
    n_in = len(args)

    def body(*refs):
        products = [lax.dot_general(refs[2 * p][...].astype(BF16), refs[2 * p + 1][...].astype(BF16), dims[pairs[p][2]],
                                    preferred_element_type=F32) for p in range(n_pairs)]
        outs = list(epilogue(products, [r[...] for r in refs[2 * n_pairs:n_tiles]]))
        if with_norm:
            y = outs[0]
            outs.append((y * lax.rsqrt(jnp.mean(y * y, axis=-1, keepdims=True) + RMS_EPS)) * refs[n_tiles][...])
        for r, o in zip(refs[n_in:], outs):
            r[...] = o.astype(r.dtype)

    tile = pl.BlockSpec((tm, tn), lambda i, j: (i, j))
    return pl.pallas_call(
        body, out_shape=[jax.ShapeDtypeStruct((m, n), d) for d in out_dtypes], grid=(m // tm, n // tn),
        in_specs=in_specs, out_specs=[tile] * len(out_dtypes), name=name,
        compiler_params=_params("parallel", "parallel"))(*args)


def _mm(a, b, mode, out_dtype, name, res=None):
    if res is None:
        return _mm_fused([(a, b, mode)], [], lambda products, extra: products, [out_dtype], name)[0]
    return _mm_fused([(a, b, mode)], [(res, 0)], lambda products, extra: [products[0] + extra[0]], [out_dtype], name)[0]


def _mm_res_norm(a, b, mode, res, gain, name):
    return _mm_fused([(a, b, mode)], [(res, 0)], lambda products, extra: [products[0] + extra[0]], [F32], name,
                     next_gain=gain)


def _ew(fn, ins, out_dtypes, name):
    rows, width = ins[0].shape
    n_in = len(ins)

    def body(*refs):
        outs = fn(*[r[...] for r in refs[:n_in]])
        for r, o in zip(refs[n_in:], outs):
            r[...] = o.astype(r.dtype)

    row = pl.BlockSpec((ROW_TILE, width), lambda i: (i, 0))
    return pl.pallas_call(
        body, out_shape=[jax.ShapeDtypeStruct((rows, width), dt) for dt in out_dtypes], grid=(rows // ROW_TILE,),
        in_specs=[row] * n_in, out_specs=[row] * len(out_dtypes), name=name, compiler_params=_params("parallel"))(*ins)


def _sigmoid(x):
    return 1.0 / (1.0 + jnp.exp(-x))


def _seg_sum(v):
    outs = []
    for k in range(v.shape[1] // LANES):
        vp = v[:, k * LANES:(k + 1) * LANES]
        left = lax.broadcasted_iota(jnp.int32, vp.shape, 1) < HEAD_DIM
        sl = jnp.sum(jnp.where(left, vp, 0.0), axis=-1, keepdims=True)
        sr = jnp.sum(jnp.where(left, 0.0, vp), axis=-1, keepdims=True)
        outs.append(jnp.where(left, sl, sr))
    return outs[0] if len(outs) == 1 else jnp.concatenate(outs, axis=1)


def _seg_rstd(x):
    return lax.rsqrt(_seg_sum(x * x) * (1.0 / HEAD_DIM) + RMS_EPS)


def _rms_fwd(x, g, name):
    rows, d = x.shape
    tm = ROW_TILE

    def body(x_ref, g_ref, h_ref):
        xv = x_ref[...]
        r = lax.rsqrt(jnp.mean(xv * xv, axis=-1, keepdims=True) + RMS_EPS)
        h_ref[...] = ((xv * r) * g_ref[...]).astype(BF16)

    return pl.pallas_call(
        body, out_shape=jax.ShapeDtypeStruct((rows, d), BF16), grid=(rows // tm,),
        in_specs=[pl.BlockSpec((tm, d), lambda i: (i, 0)), pl.BlockSpec((1, d), lambda i: (0, 0))],
        out_specs=pl.BlockSpec((tm, d), lambda i: (i, 0)), name=name,
        compiler_params=_params("parallel"))(x, g)


def _mm_rms_bwd(pairs, x, g, dres, name):
    rows, d = x.shape
    assert all(_mm_dims(*p)[:2] == (rows, d) for p in pairs)
    kbytes = [(_mm_dims(a, b, mode)[2], a.dtype.itemsize, b.dtype.itemsize) for a, b, mode in pairs]
    tm = max(t for t in (512, 256, 128) if rows % t == 0 and
             2 * (sum(t * k * ab + d * k * bb for k, ab, bb in kbytes) + t * d * 14) + 2 * t * d * 4 <= MM_VMEM_BUDGET)
    dims = {"nn": (((1,), (0,)), ((), ())), "nt": (((1,), (1,)), ((), ()))}
    n_pairs = len(pairs)

    def body(*refs):
        x_ref, g_ref, dres_ref = refs[2 * n_pairs:2 * n_pairs + 3]
        dx_ref, dxb_ref, dg_ref = refs[2 * n_pairs + 3:]
        dhv = functools.reduce(lambda u, v: u + v, [
            lax.dot_general(refs[2 * p][...].astype(BF16), refs[2 * p + 1][...].astype(BF16), dims[pairs[p][2]],
                            preferred_element_type=F32) for p in range(n_pairs)])
        xv = x_ref[...]
        r = lax.rsqrt(jnp.mean(xv * xv, axis=-1, keepdims=True) + RMS_EPS)
        xh = xv * r
        dxh = dhv * g_ref[...]
        dxv = dres_ref[...] + r * (dxh - xh * jnp.mean(dxh * xh, axis=-1, keepdims=True))
        dx_ref[...] = dxv
        dxb_ref[...] = dxv.astype(BF16)
        part = jnp.sum(dhv * xh, axis=0, keepdims=True)

        @pl.when(pl.program_id(0) == 0)
        def _():
            dg_ref[...] = part

        @pl.when(pl.program_id(0) > 0)
        def _():
            dg_ref[...] += part

    row = pl.BlockSpec((tm, d), lambda i: (i, 0))
    vec = pl.BlockSpec((1, d), lambda i: (0, 0))
    in_specs, args = [], []
    for a, b, mode in pairs:
        in_specs += [pl.BlockSpec((tm, a.shape[1]), lambda i: (i, 0)), pl.BlockSpec(b.shape, lambda i: (0, 0))]
        args += [a, b]
    return pl.pallas_call(
        body, out_shape=[jax.ShapeDtypeStruct((rows, d), F32), jax.ShapeDtypeStruct((rows, d), BF16),
                         jax.ShapeDtypeStruct((1, d), F32)],
        grid=(rows // tm,), in_specs=in_specs + [row, vec, row], out_specs=[row, row, vec],
        name=name, compiler_params=_params("arbitrary"))(*args, x, g, dres)


def _ple_bwd(dx, z, e):
    s = _sigmoid(z.astype(F32))
    return dx * s, dx * e.astype(F32) * (s * (1.0 - s))


def _loss_grad(y, t, z, e):
    rows, d = y.shape
    tm = ROW_TILE

    def body(y_ref, t_ref, z_ref, e_ref, dy_ref, de_ref, dz_ref, l_ref):
        err = y_ref[...] - t_ref[...]
        dy = err * (1.0 / d)
        dy_ref[...] = dy
        de, dz = _ple_bwd(dy, z_ref[...], e_ref[...])
        de_ref[...] = de.astype(BF16)
        dz_ref[...] = dz.astype(BF16)
        part = jnp.zeros((1, LANES), F32) + jnp.sum(err * err) * (0.5 / d)

        @pl.when(pl.program_id(0) == 0)
        def _():
            l_ref[...] = part

        @pl.when(pl.program_id(0) > 0)
        def _():
            l_ref[...] += part

    row = pl.BlockSpec((tm, d), lambda i: (i, 0))
    return pl.pallas_call(
        body, out_shape=[jax.ShapeDtypeStruct((rows, d), F32), jax.ShapeDtypeStruct((rows, d), BF16),
                         jax.ShapeDtypeStruct((rows, d), BF16), jax.ShapeDtypeStruct((1, LANES), F32)],
        grid=(rows // tm,), in_specs=[row] * 4, out_specs=[row, row, row, pl.BlockSpec((1, LANES), lambda i: (0, 0))],
        name="loss_grad", compiler_params=_params("arbitrary"))(y, t, z, e)


def _swap_halves(v):
    return pltpu.roll(v, HEAD_DIM, axis=1)


def _expand_kv(kv):
    left = lax.broadcasted_iota(jnp.int32, kv.shape, 1) < HEAD_DIM
    sw = _swap_halves(kv)
    h0 = jnp.where(left, kv, sw)
    h1 = jnp.where(left, sw, kv)
    return jnp.concatenate([h0, h0, h1, h1], axis=1)


def _reduce_kv(dkv):
    left = lax.broadcasted_iota(jnp.int32, (dkv.shape[0], LANES), 1) < HEAD_DIM
    t = dkv[:, 0:LANES] + dkv[:, LANES:2 * LANES]
    u = dkv[:, 2 * LANES:3 * LANES] + dkv[:, 3 * LANES:4 * LANES]
    t = t + _swap_halves(t)
    u = u + _swap_halves(u)
    return jnp.where(left, t, u)


def _qknorm_fwd(proj, gqa, gka, gqb, gkb):
    rows = proj.shape[0]
    tm = ROW_TILE

    def body(qa_ref, ka_ref, qb_ref, kb_ref, vb_ref, gqa_ref, gka_ref, gqb_ref, gkb_ref, oqa, oka, oqb, okb, ovb):
        for src, g_ref, dst in ((qa_ref, gqa_ref, oqa), (ka_ref, gka_ref, oka), (qb_ref, gqb_ref, oqb)):
            xv = src[...]
            dst[...] = (xv * _seg_rstd(xv)) * g_ref[...]
        kv = kb_ref[...]
        okb[...] = _expand_kv((kv * _seg_rstd(kv)) * gkb_ref[...])
        ovb[...] = _expand_kv(vb_ref[...])

    def win(width, off):
        return pl.BlockSpec((tm, width), lambda i: (i, off // width))

    vec = lambda w: pl.BlockSpec((1, w), lambda i: (0, 0))
    out = pl.BlockSpec((tm, WIDTH), lambda i: (i, 0))
    return pl.pallas_call(
        body, out_shape=[jax.ShapeDtypeStruct((rows, WIDTH), F32)] * 5, grid=(rows // tm,),
        in_specs=[win(WIDTH, OFF_QA), win(WIDTH, OFF_KA), win(WIDTH, OFF_QB), win(LANES, OFF_KB), win(LANES, OFF_VB),
                  vec(WIDTH), vec(WIDTH), vec(WIDTH), vec(LANES)],
        out_specs=[out] * 5, name="qknorm_fwd", compiler_params=_params("parallel"))(
            proj, proj, proj, proj, proj, gqa, gka, gqb, gkb)


def _norm_bwd(xv, g, dy):
    r = _seg_rstd(xv)
    xh = xv * r
    dxh = dy * g
    dx = r * (dxh - xh * (_seg_sum(dxh * xh) * (1.0 / HEAD_DIM)))
    return dx, jnp.sum(dy * xh, axis=0, keepdims=True)


def _qknorm_bwd(proj, gqa, gka, gqb, gkb, dqa, dka, dva, dqb, dkb, dvb, dga, dgb):
    rows = proj.shape[0]
    tm = ROW_TILE
    n_a = len(dqa)

    def body(*refs):
        qa_ref, ka_ref, qb_ref, kb_ref, gqa_ref, gka_ref, gqb_ref, gkb_ref = refs[:8]
        pos = 8
        dqa_refs, dka_refs, dva_refs = refs[pos:pos + n_a], refs[pos + n_a:pos + 2 * n_a], refs[pos + 2 * n_a:pos + 3 * n_a]
        pos += 3 * n_a
        dqb_ref, dkb_ref, dvb_ref, dga_ref, dgb_ref = refs[pos:pos + 5]
        dproj_ref, ogqa, ogka, ogqb, ogkb = refs[pos + 5:]

        def total(rs):
            acc = rs[0][...]
            for r in rs[1:]:
                acc = acc + r[...]
            return acc

        dx_qa, p_qa = _norm_bwd(qa_ref[...], gqa_ref[...], total(dqa_refs))
        dx_ka, p_ka = _norm_bwd(ka_ref[...], gka_ref[...], total(dka_refs))
        dx_qb, p_qb = _norm_bwd(qb_ref[...], gqb_ref[...], dqb_ref[...])
        dx_kb, p_kb = _norm_bwd(kb_ref[...], gkb_ref[...], _reduce_kv(dkb_ref[...]))
        dproj_ref[:, OFF_QA:OFF_QA + WIDTH] = dx_qa.astype(BF16)
        dproj_ref[:, OFF_KA:OFF_KA + WIDTH] = dx_ka.astype(BF16)
        dproj_ref[:, OFF_VA:OFF_VA + WIDTH] = total(dva_refs).astype(BF16)
        dproj_ref[:, OFF_QB:OFF_QB + WIDTH] = dx_qb.astype(BF16)
        dproj_ref[:, OFF_KB:OFF_KB + LANES] = dx_kb.astype(BF16)
        dproj_ref[:, OFF_VB:OFF_VB + LANES] = _reduce_kv(dvb_ref[...]).astype(BF16)
        dproj_ref[:, OFF_GA:OFF_GB] = dga_ref[...]
        dproj_ref[:, OFF_GB:D_IN] = dgb_ref[...]
        first = pl.program_id(0) == 0
        for o_ref, part in ((ogqa, p_qa), (ogka, p_ka), (ogqb, p_qb), (ogkb, p_kb)):
            @pl.when(first)
            def _(o_ref=o_ref, part=part):
                o_ref[...] = part

            @pl.when(jnp.logical_not(first))
            def _(o_ref=o_ref, part=part):
                o_ref[...] += part

    def win(width, off):
        return pl.BlockSpec((tm, width), lambda i: (i, off // width))

    vec = lambda w: pl.BlockSpec((1, w), lambda i: (0, 0))
    row = lambda w: pl.BlockSpec((tm, w), lambda i: (i, 0))
    in_specs = [win(WIDTH, OFF_QA), win(WIDTH, OFF_KA), win(WIDTH, OFF_QB), win(LANES, OFF_KB),
                vec(WIDTH), vec(WIDTH), vec(WIDTH), vec(LANES)]
    in_specs += [row(WIDTH)] * (3 * n_a + 3) + [row(D_MODEL)] * 2
    return pl.pallas_call(
        body,
        out_shape=[jax.ShapeDtypeStruct((rows, D_IN), BF16), jax.ShapeDtypeStruct((1, WIDTH), F32),
                   jax.ShapeDtypeStruct((1, WIDTH), F32), jax.ShapeDtypeStruct((1, WIDTH), F32),
                   jax.ShapeDtypeStruct((1, LANES), F32)],
        grid=(rows // tm,), in_specs=in_specs,
        out_specs=[row(D_IN), vec(WIDTH), vec(WIDTH), vec(WIDTH), vec(LANES)],
        name="qknorm_bwd", compiler_params=_params("arbitrary"))(
            proj, proj, proj, proj, gqa, gka, gqb, gkb, *dqa, *dka, *dva, dqb, dkb, dvb, dga, dgb)


def _t5_bucket(rel):
    half_b = NUM_BUCKETS // 2
    max_exact = half_b // 2
    sign = jnp.where(rel > 0, half_b, 0)
    n = jnp.abs(rel)
    nf = jnp.maximum(n, 1).astype(F32)
    large = max_exact + (jnp.log(nf / max_exact) / math.log(MAX_DISTANCE / max_exact)
                         * (half_b - max_exact)).astype(jnp.int32)
    large = jnp.minimum(large, half_b - 1)
    return sign + jnp.where(n < max_exact, n, large)


def _band_buckets(blk, dilation):
    i = jnp.arange(blk, dtype=jnp.int32)[:, None]
    j = jnp.arange(3 * blk, dtype=jnp.int32)[None, :]
    rel = j - blk - i
    return jnp.where(jnp.abs(rel) <= blk, _t5_bucket(rel * dilation), -1)


def _bias_tiles(table, buckets, head_off, name):
    blk = buckets.shape[0]

    def body(tab_ref, bk_ref, o_ref):
        h = pl.program_id(0) + head_off
        bk = bk_ref[...]
        acc = jnp.full(bk.shape, NEG_INF, F32)
        for b in range(NUM_BUCKETS):
            acc = jnp.where(bk == b, tab_ref[b, h], acc)
        o_ref[0] = acc

    return pl.pallas_call(
        body, out_shape=jax.ShapeDtypeStruct((N_HEADS, blk, 3 * blk), F32), grid=(N_HEADS,),
        in_specs=[pl.BlockSpec(memory_space=pltpu.SMEM), pl.BlockSpec((blk, 3 * blk), lambda h: (0, 0))],
        out_specs=pl.BlockSpec((1, blk, 3 * blk), lambda h: (h, 0, 0)),
        name=name, compiler_params=_params("parallel"))(table, buckets)


def _table_grad(dbias, buckets, name):
    blk = buckets.shape[0]

    def body(db_ref, bk_ref, o_ref):
        bk = bk_ref[...]
        dbv = db_ref[0]
        lane = lax.broadcasted_iota(jnp.int32, (1, LANES), 1)
        acc = jnp.zeros((1, LANES), F32)
        for b in range(NUM_BUCKETS):
            acc = jnp.where(lane == b, jnp.sum(jnp.where(bk == b, dbv, 0.0)), acc)
        o_ref[0] = acc

    out = pl.pallas_call(
        body, out_shape=jax.ShapeDtypeStruct((N_HEADS, 1, LANES), F32), grid=(N_HEADS,),
        in_specs=[pl.BlockSpec((1, blk, 3 * blk), lambda h: (h, 0, 0)), pl.BlockSpec((blk, 3 * blk), lambda h: (0, 0))],
        out_specs=pl.BlockSpec((1, 1, LANES), lambda h: (h, 0, 0)),
        name=name, compiler_params=_params("parallel"))(dbias, buckets)
    return out[:, 0, :NUM_BUCKETS]


def _dot_nt(a, b):
    return lax.dot_general(a, b, (((1,), (1,)), ((), ())), preferred_element_type=F32)


def _dot_tn(a, b):
    return lax.dot_general(a, b, (((0,), (0,)), ((), ())), preferred_element_type=F32)


def _stack_pair(x2, left):
    return jnp.concatenate([jnp.where(left, x2, 0.0), jnp.where(left, 0.0, x2)], axis=0).astype(BF16)


def _attn_geometry(blk, d):
    halo = blk * d
    subs = max(1, min(ITEMS // d, MAX_CHUNK // halo))
    return subs, min(d, ITEMS // subs), halo


def _item_of(j, r0, per_group):
    return j // per_group, r0 + j % per_group


def _span_rows(ref, first, r, blk, d):
    if d == 1:
        return ref[first:first + blk, :]
    return ref[pl.ds(first + r, blk, stride=d), :]


def _set_span_rows(ref, first, r, blk, d, val, add=False):
    idx = slice(first, first + blk) if d == 1 else pl.ds(first + r, blk, stride=d)
    ref[idx, :] = ref[idx, :] + val if add else val


def _for_groups(group, groups, per_group):
    if groups == 1:
        group(0)
    else:
        def step(g, carry):
            group(g * per_group)
            return carry

        lax.fori_loop(0, groups, step, 0)


def _key_rows(p_ref, c_ref, n_ref, s, r, subs, halo, blk, d):
    parts = []
    for span in (s - 1, s, s + 1):
        if span < 0:
            parts.append(_span_rows(p_ref, 0, r, blk, d))
        elif span == subs:
            parts.append(_span_rows(n_ref, 0, r, blk, d))
        else:
            parts.append(_span_rows(c_ref, span * halo, r, blk, d))
    return jnp.concatenate(parts, axis=0)


def _item_penalty(t, nct, s, subs, blk):
    first_ok = True if s > 0 else t > 0
    last_ok = True if s < subs - 1 else t < nct - 1
    col = lax.broadcasted_iota(jnp.int32, (1, 3 * blk), 1)
    ok = jnp.logical_and(jnp.logical_or(col >= blk, first_ok), jnp.logical_or(col < 2 * blk, last_ok))
    return jnp.where(ok, 0.0, NEG_INF).astype(F32)


def _attn_specs(seq, blk, d, step_of, col=0):
    subs, _, halo = _attn_geometry(blk, d)
    last, first = seq // halo - 1, col // LANES
    cur = pl.BlockSpec((subs * halo, LANES), lambda hp, t: (step_of(t), first + hp))
    prev = pl.BlockSpec((halo, LANES), lambda hp, t: (jnp.clip(step_of(t) * subs - 1, 0, last), first + hp))
    nxt = pl.BlockSpec((halo, LANES), lambda hp, t: (jnp.minimum((step_of(t) + 1) * subs, last), first + hp))
    return cur, prev, nxt


def _attn_fwd(q, k, v, bias, sink, blk, d, name, v_col=0):
    seq = q.shape[0]
    subs, per_group, halo = _attn_geometry(blk, d)
    items, chunk, groups = subs * per_group, subs * halo, d // per_group
    nct = seq // chunk
    has_sink = sink is not None
    scale = HEAD_DIM ** -0.5

    def body(*refs):
        q_ref, kp, kc, kn, vp, vc, vn, b_ref = refs[:8]
        s_ref = refs[8] if has_sink else None
        o_ref, l_ref = refs[-2], refs[-1]
        t = pl.program_id(1)
        left = lax.broadcasted_iota(jnp.int32, (1, LANES), 1) < HEAD_DIM
        bias2 = b_ref[...]

        def group(r0):
            scores, vcats = [], []
            for j in range(items):
                s, r = _item_of(j, r0, per_group)
                qs = _stack_pair(_span_rows(q_ref, s * halo, r, blk, d) * scale, left)
                kcat = _key_rows(kp, kc, kn, s, r, subs, halo, blk, d).astype(BF16)
                scores.append(_dot_nt(qs, kcat) + bias2 + _item_penalty(t, nct, s, subs, blk))
                vcats.append(_key_rows(vp, vc, vn, s, r, subs, halo, blk, d).astype(BF16))
            ms = [jnp.max(s, axis=-1, keepdims=True) for s in scores]
            if has_sink:
                sk = s_ref[...]
                ms = [jnp.maximum(m, sk) for m in ms]
            ps = [jnp.exp(s - m) for s, m in zip(scores, ms)]
            dens = [jnp.sum(p, axis=-1, keepdims=True) for p in ps]
            if has_sink:
                dens = [den + jnp.exp(sk - m) for den, m in zip(dens, ms)]
            pns = [(p * (1.0 / den)).astype(BF16) for p, den in zip(ps, dens)]
            lses = [m + jnp.log(den) for m, den in zip(ms, dens)]
            for j in range(items):
                s, r = _item_of(j, r0, per_group)
                o2 = jnp.dot(pns[j], vcats[j], preferred_element_type=F32)
                _set_span_rows(o_ref, s * halo, r, blk, d, jnp.where(left, o2[:blk], o2[blk:]))
                _set_span_rows(l_ref, s * halo, r, blk, d, jnp.where(left, lses[j][:blk], lses[j][blk:]))

        _for_groups(group, groups, per_group)

    cur, prev, nxt = _attn_specs(seq, blk, d, lambda t: t)
    v_cur, v_prev, v_nxt = _attn_specs(seq, blk, d, lambda t: t, v_col)
    in_specs = [cur, prev, cur, nxt, v_prev, v_cur, v_nxt, pl.BlockSpec((2 * blk, 3 * blk), lambda hp, t: (hp, 0))]
    args = [q, k, k, k, v, v, v, bias]
    if has_sink:
        in_specs.append(pl.BlockSpec((2 * blk, 1), lambda hp, t: (hp, 0)))
        args.append(sink)
    return pl.pallas_call(
        body, out_shape=[jax.ShapeDtypeStruct((seq, WIDTH), F32)] * 2, grid=(N_PAIRS, nct),
        in_specs=in_specs, out_specs=[cur, cur], name=name,
        compiler_params=_params("parallel", "parallel"))(*args)


def _attn_bwd(q, k, v, do, lse, delta, bias, sink, blk, d, name, v_col=0):
    seq = q.shape[0]
    subs, per_group, halo = _attn_geometry(blk, d)
    items, chunk, groups = subs * per_group, subs * halo, d // per_group
    nct = seq // chunk
    has_sink = sink is not None
    n_in = 12 if has_sink else 11
    scale = HEAD_DIM ** -0.5

    def body(*refs):
        q_ref, kp, kc, kn, vp, vc, vn, do_ref, l_ref, d_ref, b_ref = refs[:11]
        s_ref = refs[11] if has_sink else None
        dq_ref, dk_ref, dv_ref, db_ref = refs[n_in:n_in + 4]
        ds_ref = refs[n_in + 4] if has_sink else None
        wk, wv = refs[-2], refs[-1]
        t = pl.program_id(1)

        @pl.when(t == 0)
        def _():
            wk[...] = jnp.zeros_like(wk)
            wv[...] = jnp.zeros_like(wv)
            db_ref[...] = jnp.zeros_like(db_ref)
            if has_sink:
                ds_ref[...] = jnp.zeros_like(ds_ref)

        @pl.when(t > 0)
        def _():
            for w in (wk, wv):
                keep = w[chunk:2 * chunk + halo]
                w[0:chunk + halo] = keep
                w[chunk + halo:2 * chunk + halo] = jnp.zeros((chunk, LANES), F32)

        @pl.when(t < nct)
        def _():
            lane = lax.broadcasted_iota(jnp.int32, (1, LANES), 1)
            left = lane < HEAD_DIM
            bias2 = b_ref[...]

            def group(r0):
                qss, doss, kcats, scores, dps, lcols, dcols = [], [], [], [], [], [], []
                for j in range(items):
                    s, r = _item_of(j, r0, per_group)
                    qs = _stack_pair(_span_rows(q_ref, s * halo, r, blk, d) * scale, left)
                    dos = _stack_pair(_span_rows(do_ref, s * halo, r, blk, d), left)
                    kcat = _key_rows(kp, kc, kn, s, r, subs, halo, blk, d).astype(BF16)
                    vcat = _key_rows(vp, vc, vn, s, r, subs, halo, blk, d).astype(BF16)
                    l2, d2 = _span_rows(l_ref, s * halo, r, blk, d), _span_rows(d_ref, s * halo, r, blk, d)
                    lcols.append(jnp.concatenate([jnp.max(jnp.where(left, l2, NEG_INF), axis=-1, keepdims=True),
                                                  jnp.max(jnp.where(left, NEG_INF, l2), axis=-1, keepdims=True)], axis=0))
                    dcols.append(jnp.concatenate([jnp.sum(jnp.where(lane == 0, d2, 0.0), axis=-1, keepdims=True),
                                                  jnp.sum(jnp.where(lane == HEAD_DIM, d2, 0.0), axis=-1, keepdims=True)],
                                                 axis=0))
                    scores.append(_dot_nt(qs, kcat) + bias2 + _item_penalty(t, nct, s, subs, blk))
                    dps.append(_dot_nt(dos, vcat))
                    qss.append(qs)
                    doss.append(dos)
                    kcats.append(kcat)
                ps = [jnp.exp(s - lc) for s, lc in zip(scores, lcols)]
                dss = [p * (dp - dc) for p, dp, dc in zip(ps, dps, dcols)]
                db_ref[...] += functools.reduce(lambda a, b: a + b, dss)
                if has_sink:
                    sk = s_ref[...]
                    ds_ref[...] -= functools.reduce(lambda a, b: a + b, [dc * jnp.exp(sk - lc) for dc, lc in zip(dcols, lcols)])
                dsbs = [ds.astype(BF16) for ds in dss]
                for j in range(items):
                    s, r = _item_of(j, r0, per_group)
                    dq2 = jnp.dot(dsbs[j], kcats[j], preferred_element_type=F32) * scale
                    _set_span_rows(dq_ref, s * halo, r, blk, d, jnp.where(left, dq2[:blk], dq2[blk:]))
                news = [(_dot_tn(dsbs[j], qss[j]), _dot_tn(ps[j].astype(BF16), doss[j])) for j in range(items)]
                for which, w in enumerate((wk, wv)):
                    for jr in range(per_group):
                        for sp in range(-1, subs + 1):
                            parts = [news[s * per_group + jr][which][(sp - s + 1) * blk:(sp - s + 2) * blk]
                                     for s in range(subs) if 0 <= sp - s + 1 < 3]
                            _set_span_rows(w, chunk + sp * halo, r0 + jr, blk, d,
                                           functools.reduce(lambda x, y: x + y, parts), add=True)

            _for_groups(group, groups, per_group)

        dk_ref[...] = wk[0:chunk]
        dv_ref[...] = wv[0:chunk]

    cur, prev, nxt = _attn_specs(seq, blk, d, lambda t: jnp.minimum(t, nct - 1))
    v_cur, v_prev, v_nxt = _attn_specs(seq, blk, d, lambda t: jnp.minimum(t, nct - 1), v_col)
    lag = pl.BlockSpec((chunk, LANES), lambda hp, t: (jnp.maximum(t - 1, 0), hp))
    band = pl.BlockSpec((2 * blk, 3 * blk), lambda hp, t: (hp, 0))
    col = pl.BlockSpec((2 * blk, 1), lambda hp, t: (hp, 0))
    in_specs = [cur, prev, cur, nxt, v_prev, v_cur, v_nxt, cur, cur, cur, band]
    args = [q, k, k, k, v, v, v, do, lse, delta, bias]
    out_shape = [jax.ShapeDtypeStruct((seq, WIDTH), F32)] * 3 + [jax.ShapeDtypeStruct((N_HEADS * blk, 3 * blk), F32)]
    out_specs = [cur, lag, lag, band]
    if has_sink:
        in_specs.append(col)
        args.append(sink)
        out_shape.append(jax.ShapeDtypeStruct((N_HEADS * blk, 1), F32))
        out_specs.append(col)
    window = pltpu.VMEM((2 * chunk + halo, LANES), F32)
    return pl.pallas_call(
        body, out_shape=out_shape, grid=(N_PAIRS, nct + 1), in_specs=in_specs, out_specs=out_specs,
        scratch_shapes=[window, window], name=name,
        compiler_params=_params("arbitrary", "arbitrary"))(*args)


def _combine_patterns(outs, lses):
    def combine(*tiles):
        os_, ls = tiles[:len(outs)], tiles[len(outs):]
        m = functools.reduce(jnp.maximum, ls)
        es = [jnp.exp(l - m) for l in ls]
        den = functools.reduce(lambda a, b: a + b, es)
        num = functools.reduce(lambda a, b: a + b, [e * o for e, o in zip(es, os_)])
        return num / den, m + jnp.log(den)

    return _ew(combine, [*outs, *lses], [F32, F32], "combine_a")


def _tile_gain(g, reps):
    return jnp.tile(g[None, :], (1, reps))


def _local_step(x, p, target, w_in_of, rest_of, small):
    rel_table = small["rel_table"]
    buckets_a = [_band_buckets(blk, d) for blk, d in DILATED]
    buckets_b = _band_buckets(BLK_B, 1)
    bias_a = [_bias_tiles(rel_table, bk, 0, "bias_a").reshape(N_HEADS * bk.shape[0], -1) for bk in buckets_a]
    bias_b = _bias_tiles(rel_table, buckets_b, N_HEADS, "bias_b").reshape(N_HEADS * BLK_B, -1)

    saved = []
    for l in range(DEPTH):
        g_mix, g_ffn, g_ple = (small[n][l][None, :] for n in ("norm_mix_g", "norm_ffn_g", "norm_ple_g"))
        gqa, gka, gqb = (_tile_gain(small[n][l], N_HEADS) for n in ("qnorm_a_g", "knorm_a_g", "qnorm_b_g"))
        gkb = _tile_gain(small["knorm_b_g"][l], N_KV_B)
        sink = jnp.repeat(small["sink_b"][l], BLK_B)[:, None]

        h = _rms_fwd(x, g_mix, "rms_mix") if l == 0 else h_next
        w_in = w_in_of(l, (h, bias_a, bias_b))
        proj = _mm(h, w_in, "nt", F32, "mm_in")
        qa, ka, qb, kb, vb = _qknorm_fwd(proj, gqa, gka, gqb, gkb)
        outs, lses = [], []
        for (blk, d), bias in zip(DILATED, bias_a):
            o, ls = _attn_fwd(qa, ka, proj, bias, None, blk, d, f"attn_a{d}_fwd", v_col=OFF_VA)
            outs.append(o)
            lses.append(ls)
        ya, lse_a = _combine_patterns(outs, lses)
        yb, lse_b = _attn_fwd(qb, kb, vb, bias_b, sink, BLK_B, 1, "attn_b_fwd")
        w = dict(rest_of(l, yb), w_in=w_in)
        def gate(products, extra):
            (ca_, cb_), (ga_, gb_) = products, extra
            return _sigmoid(ga_) * ca_ + _sigmoid(gb_) * cb_, ca_, cb_

        merged, ca, cb = _mm_fused([(ya, w["w_branch_a"], "nn"), (yb, w["w_branch_b"], "nn")],
                                   [(proj, OFF_GA), (proj, OFF_GB)], gate, [BF16, BF16, BF16], "mm_branches_gate")
        x1, h2 = _mm_res_norm(merged, w["w_out"], "nn", x, g_ffn, "mm_out_norm")

        def swiglu(products, extra):
            a_, u_ = products
            return (a_ * _sigmoid(a_)) * u_, a_, u_

        hid, a, u = _mm_fused([(h2, w["w_ffn_gate"], "nt"), (h2, w["w_ffn_up"], "nt")], [], swiglu,
                              [BF16, BF16, BF16], "mm_ffn_gate_up")
        x2, h3 = _mm_res_norm(hid, w["w_ffn_down"], "nn", x1, g_ple, "mm_ffn_down_norm")

        def ple(products, extra):
            z_, e_ = products
            return extra[0] + _sigmoid(z_) * e_, z_, e_

        next_gain = small["norm_mix_g"][l + 1][None, :] if l + 1 < DEPTH else None
        x3, z, e, *rest = _mm_fused([(h3, w["w_ple_gate"], "nn"), (p[l], w["w_ple_proj"], "nn")], [(x2, 0)], ple,
                                    [F32, BF16, BF16], "mm_ple", next_gain=next_gain)
        h_next = rest[0] if rest else None
        saved.append(dict(w=w, x0=x, h=h, proj=proj, qa=qa, ka=ka, qb=qb, kb=kb, vb=vb, ya=ya, lse_a=lse_a,
                          yb=yb, lse_b=lse_b, ca=ca, cb=cb, merged=merged, x1=x1, h2=h2, a=a, u=u, hid=hid,
                          x2=x2, h3=h3, z=z, e=e))
        x = x3

    dx, de, dz, loss_acc = _loss_grad(x, target, saved[-1]["z"], saved[-1]["e"])
    loss = loss_acc[0, 0]

    gbig = [{} for _ in range(DEPTH)]
    marks = [{} for _ in range(DEPTH)]
    gsmall = {n: [None] * DEPTH for n in SMALL if n != "rel_table"}
    dbias_a = [[] for _ in DILATED]
    dbias_b = []

    for l in reversed(range(DEPTH)):
        sv = saved[l]
        w = sv["w"]
        g_mix, g_ffn, g_ple = (small[n][l][None, :] for n in ("norm_mix_g", "norm_ffn_g", "norm_ple_g"))
        gqa, gka, gqb = (_tile_gain(small[n][l], N_HEADS) for n in ("qnorm_a_g", "knorm_a_g", "qnorm_b_g"))
        gkb = _tile_gain(small["knorm_b_g"][l], N_KV_B)
        sink = jnp.repeat(small["sink_b"][l], BLK_B)[:, None]

        if l < DEPTH - 1:
            de, dz = _ew(_ple_bwd, [dx, sv["z"], sv["e"]], [BF16, BF16], "ple_bwd")
        gbig[l]["w_ple_proj"] = _mm(p[l], de, "tn", BF16, "mm_d_ple_proj")
        gbig[l]["w_ple_gate"] = _mm(sv["h3"], dz, "tn", BF16, "mm_d_ple_gate")
        dx, dxb, gsmall["norm_ple_g"][l] = _mm_rms_bwd([(dz, w["w_ple_gate"], "nt")], sv["x2"], g_ple, dx,
                                                      "mm_dh3_rms_bwd")

        gbig[l]["w_ffn_down"] = _mm(sv["hid"], dxb, "tn", BF16, "mm_d_ffn_down")

        def swiglu_bwd(products, extra):
            dh_, a_, u_ = products[0], extra[0].astype(F32), extra[1].astype(F32)
            s = _sigmoid(a_)
            return dh_ * u_ * (s * (1.0 + a_ * (1.0 - s))), dh_ * (a_ * s)

        da, du = _mm_fused([(dxb, w["w_ffn_down"], "nt")], [(sv["a"], 0), (sv["u"], 0)], swiglu_bwd, [BF16, BF16],
                           "mm_dhid_swiglu_bwd")
        gbig[l]["w_ffn_gate"] = _mm(da, sv["h2"], "tn", BF16, "mm_d_ffn_gate")
        gbig[l]["w_ffn_up"] = _mm(du, sv["h2"], "tn", BF16, "mm_d_ffn_up")
        dx, dxb, gsmall["norm_ffn_g"][l] = _mm_rms_bwd([(da, w["w_ffn_gate"], "nn"), (du, w["w_ffn_up"], "nn")],
                                                      sv["x1"], g_ffn, dx, "mm_dh2_rms_bwd")

        gbig[l]["w_out"] = _mm(sv["merged"], dxb, "tn", BF16, "mm_d_out")

        def gate_bwd(products, extra):
            dm_, ca_, cb_ = products[0], extra[0].astype(F32), extra[1].astype(F32)
            sa, sb = _sigmoid(extra[2]), _sigmoid(extra[3])
            return dm_ * sa, dm_ * sb, dm_ * ca_ * (sa * (1.0 - sa)), dm_ * cb_ * (sb * (1.0 - sb))

        dca, dcb, dga, dgb = _mm_fused(
            [(dxb, w["w_out"], "nt")], [(sv["ca"], 0), (sv["cb"], 0), (sv["proj"], OFF_GA), (sv["proj"], OFF_GB)],
            gate_bwd, [BF16, BF16, BF16, BF16], "mm_dmerged_gate_bwd")
        gbig[l]["w_branch_a"] = _mm(sv["ya"], dca, "tn", BF16, "mm_d_branch_a")
        gbig[l]["w_branch_b"] = _mm(sv["yb"], dcb, "tn", BF16, "mm_d_branch_b")
        def with_row_dots(products, extra):
            return products[0], _seg_sum(products[0] * extra[0])

        dya, delta_a = _mm_fused([(dca, w["w_branch_a"], "nt")], [(sv["ya"], 0)], with_row_dots, [F32, F32], "mm_dya")
        dyb, delta_b = _mm_fused([(dcb, w["w_branch_b"], "nt")], [(sv["yb"], 0)], with_row_dots, [F32, F32], "mm_dyb")

        dqa, dka, dva = [], [], []
        for (blk, d), bias, bk in zip(DILATED, bias_a, buckets_a):
            dq_, dk_, dv_, db_ = _attn_bwd(sv["qa"], sv["ka"], sv["proj"], dya, sv["lse_a"], delta_a, bias, None, blk, d,
                                           f"attn_a{d}_bwd", v_col=OFF_VA)
            dqa.append(dq_)
            dka.append(dk_)
            dva.append(dv_)
            dbias_a[len(dqa) - 1].append(db_)
        dqb, dkb, dvb, db_, dsink = _attn_bwd(sv["qb"], sv["kb"], sv["vb"], dyb, sv["lse_b"], delta_b, bias_b, sink,
                                              BLK_B, 1, "attn_b_bwd")
        dbias_b.append(db_)
        gsmall["sink_b"][l] = dsink.reshape(N_HEADS, BLK_B).sum(axis=1)

        dproj, pqa, pka, pqb, pkb = _qknorm_bwd(sv["proj"], gqa, gka, gqb, gkb, dqa, dka, dva, dqb, dkb, dvb, dga, dgb)
        marks[l]["attn_bwd_done"] = dproj
        gsmall["qnorm_a_g"][l] = pqa.reshape(N_HEADS, HEAD_DIM).sum(0)
        gsmall["knorm_a_g"][l] = pka.reshape(N_HEADS, HEAD_DIM).sum(0)
        gsmall["qnorm_b_g"][l] = pqb.reshape(N_HEADS, HEAD_DIM).sum(0)
        gsmall["knorm_b_g"][l] = pkb.reshape(N_KV_B, HEAD_DIM).sum(0)
        gbig[l]["w_in"] = _mm(dproj, sv["h"], "tn", BF16, "mm_d_in")
        dx, _, gsmall["norm_mix_g"][l] = _mm_rms_bwd([(dproj, w["w_in"], "nn")], sv["x0"], g_mix, dx, "mm_dh_rms_bwd")
        gsmall["norm_mix_g"][l] = gsmall["norm_mix_g"][l][0]
        gsmall["norm_ffn_g"][l] = gsmall["norm_ffn_g"][l][0]
        gsmall["norm_ple_g"][l] = gsmall["norm_ple_g"][l][0]

    gsmall = {n: jnp.stack(v) for n, v in gsmall.items()}
    dtable_a = sum(_table_grad(sum(dbs).reshape(N_HEADS, blk, 3 * blk), bk, "table_grad_a")
                   for dbs, (blk, _), bk in zip(dbias_a, DILATED, buckets_a))
    dtable_b = _table_grad(sum(dbias_b).reshape(N_HEADS, BLK_B, 3 * BLK_B), buckets_b, "table_grad_b")
    gsmall["rel_table"] = jnp.concatenate([dtable_a, dtable_b], axis=0).T
    return loss, dx, gbig, gsmall, marks


def _place():
    return lax.axis_index("x"), lax.axis_index("y"), lax.axis_index("c")


def _flip(v, bit):
    return 1 - v if bit else v


CHIP_RELATIONS = ((0, 1), (1, 0), (1, 1))
ANY = pl.BlockSpec(memory_space=pl.ANY)


def _allgather_body(w_refs, out_refs, send_sems, recv_sems):
    x, y, c = _place()
    chips = [(_flip(x, a), _flip(y, b)) for a, b in CHIP_RELATIONS]

    def make(g):
        w_ref, out_ref = w_refs[g], out_refs[g]
        half = w_ref.shape[0] // 2

        def part(px, py, pc):
            return out_ref.at[2 * px + py, pl.ds(pc * half, half), :]

        def copy(k, block, to, src=None):
            return pltpu.make_async_remote_copy(
                src_ref=part(*block) if src is None else src, dst_ref=part(*block),
                send_sem=send_sems.at[7 * g + k], recv_sem=recv_sems.at[7 * g + k], device_id=to,
                device_id_type=MESH_ID)

        own = pltpu.make_async_remote_copy(
            src_ref=w_ref, dst_ref=out_ref.at[2 * x + y], send_sem=send_sems.at[7 * g + 6],
            recv_sem=recv_sems.at[7 * g + 6], device_id=(x, y, 1 - c), device_id_type=MESH_ID)
        first = [copy(k, (x, y, c), (*chip, c), src=w_ref.at[pl.ds(c * half, half), :]) for k, chip in enumerate(chips)]
        passed = [copy(3 + k, (*chip, c), (x, y, 1 - c)) for k, chip in enumerate(chips)]
        arrive = [copy(k, (*chip, c), (x, y, c)) for k, chip in enumerate(chips)]
        arrive2 = [copy(3 + k, (*chip, 1 - c), (x, y, c)) for k, chip in enumerate(chips)]
        return own, first, passed, arrive, arrive2

    made = [make(g) for g in range(len(w_refs))]
    for own, first, _, _, _ in made:
        own.start()
        for cp in first:
            cp.start()
    for _, _, passed, arrive, _ in made:
        for k in range(3):
            arrive[k].wait_recv()
            passed[k].start()
    for own, first, passed, _, arrive2 in made:
        for k in range(3):
            arrive2[k].wait_recv()
        own.wait_recv()
        for cp in first + passed + [own]:
            cp.wait_send()


def _sibling(x, y, c):
    return [(x, y, 1 - c)]


def _same_core_of_other_chips(x, y, c):
    return [(_flip(x, a), _flip(y, b), c) for a, b in CHIP_RELATIONS]


def _exchange(body, ins, out_types, n_sems, name, sequencer=None):
    n = len(ins)
    sems = (pltpu.SemaphoreType.DMA((n_sems,)), pltpu.SemaphoreType.DMA((n_sems,)))
    if sequencer is None:
        in_place = out_types is None
        out_shape = [jax.ShapeDtypeStruct(a.shape, a.dtype) for a in ins] if in_place else out_types

        def tc_body(*refs):
            body(refs[:n], refs[n:n + len(out_shape)], refs[-2], refs[-1])

        return list(pl.pallas_call(
            tc_body, out_shape=out_shape, in_specs=[ANY] * n, out_specs=[ANY] * len(out_shape),
            input_output_aliases={g: g for g in range(n)} if in_place else {}, scratch_shapes=list(sems), name=name)(*ins))

    collective_id, peers = sequencer
    hbm = pltpu.MemorySpace.HBM
    in_refs = [jax.new_ref(a, memory_space=hbm) for a in ins]
    out_refs = in_refs if out_types is None else [jax.empty_ref(t, memory_space=hbm) for t in out_types]

    @pl.kernel(mesh=plsc.ScalarSubcoreMesh(axis_name="sequencer", num_cores=1), name=name, scratch_types=sems,
               compiler_params=pltpu.CompilerParams(collective_id=collective_id))
    def launch(send_sems, recv_sems):
        barrier = pltpu.get_barrier_semaphore()
        devices = peers(*_place())
        for device in devices:
            pl.semaphore_signal(barrier, inc=1, device_id=device, device_id_type=MESH_ID)
        pl.semaphore_wait(barrier, len(devices))
        body(in_refs, out_refs, send_sems, recv_sems)

    launch()
    return [r[...] for r in out_refs]


def _allgather(shards, name, sequencer=None):
    out_types = [jax.ShapeDtypeStruct((N_CHIPS,) + s.shape, s.dtype) for s in shards]
    if sequencer is not None:
        sequencer = (sequencer, lambda x, y, c: _sibling(x, y, c) + _same_core_of_other_chips(x, y, c))
    return _exchange(_allgather_body, shards, out_types, 7 * len(shards), name, sequencer)


def _half_tile(half):
    return max(t for t in range(16, 1025, 16) if half % t == 0)


def _run_copies(cps):
    for cp in cps:
        cp.start()
    for cp in cps:
        cp.wait_recv()
    for cp in cps:
        cp.wait_send()


def _sibling_halves(gsends, name, sequencer=None):
    def body(g_refs, out_refs, send_sems, recv_sems):
        x, y, c = _place()
        cps = []
        for g, (g_ref, out_ref) in enumerate(zip(g_refs, out_refs)):
            half = g_ref.shape[1] // 2
            cps.append(pltpu.make_async_remote_copy(
                src_ref=g_ref.at[:, pl.ds((1 - c) * half, half), :], dst_ref=out_ref,
                send_sem=send_sems.at[g], recv_sem=recv_sems.at[g], device_id=(x, y, 1 - c), device_id_type=MESH_ID))
        _run_copies(cps)

    out_types = [jax.ShapeDtypeStruct((s.shape[0], s.shape[1] // 2, s.shape[2]), s.dtype) for s in gsends]
    return _exchange(body, gsends, out_types, len(gsends), name, sequencer and (sequencer, _sibling))


def _chip_sums(gsend, sib, place):
    n, rows, cols = gsend.shape
    half = rows // 2
    tm = _half_tile(half)
    nblk = half // tm

    def body(s_ref, g_ref, sib_ref, o_ref):
        o_ref[0] = (g_ref[0].astype(F32) + sib_ref[0].astype(F32)).astype(o_ref.dtype)

    grid_spec = pltpu.PrefetchScalarGridSpec(
        num_scalar_prefetch=1, grid=(n, nblk),
        in_specs=[pl.BlockSpec((1, tm, cols), lambda k, i, s: (jnp.bitwise_xor(s[0], k), s[1] * nblk + i, 0)),
                  pl.BlockSpec((1, tm, cols), lambda k, i, s: (jnp.bitwise_xor(s[0], k), i, 0))],
        out_specs=pl.BlockSpec((1, tm, cols), lambda k, i, s: (k, i, 0)))
    return pl.pallas_call(
        body, out_shape=jax.ShapeDtypeStruct((n, half, cols), BF16), grid_spec=grid_spec,
        name="rs_chip_sums", compiler_params=_params("parallel", "parallel"))(place, gsend, sib)


def _exchange_chip_sums(tsends, name, sequencer=None):
    def body(t_refs, out_refs, send_sems, recv_sems):
        x, y, c = _place()
        cps = []
        for g, (t_ref, out_ref) in enumerate(zip(t_refs, out_refs)):
            for k, device in enumerate(_same_core_of_other_chips(x, y, c)):
                cps.append(pltpu.make_async_remote_copy(
                    src_ref=t_ref.at[k + 1], dst_ref=out_ref.at[k], send_sem=send_sems.at[3 * g + k],
                    recv_sem=recv_sems.at[3 * g + k], device_id=device, device_id_type=MESH_ID))
        _run_copies(cps)

    out_types = [jax.ShapeDtypeStruct((3,) + s.shape[1:], s.dtype) for s in tsends]
    return _exchange(body, tsends, out_types, 3 * len(tsends), name,
                     sequencer and (sequencer, _same_core_of_other_chips))


def _final_sum(tsend, recv, place):
    n, half, cols = tsend.shape
    tm = _half_tile(half)
    nblk = half // tm

    def body(s_ref, t_ref, r_ref, o_ref):
        o_ref[...] = ((t_ref[0].astype(F32) + r_ref[0].astype(F32)) + r_ref[1].astype(F32)) + r_ref[2].astype(F32)

    grid_spec = pltpu.PrefetchScalarGridSpec(
        num_scalar_prefetch=1, grid=(nblk,),
        in_specs=[pl.BlockSpec((1, tm, cols), lambda i, s: (0, i, 0)), pl.BlockSpec((n - 1, tm, cols), lambda i, s: (0, i, 0))],
        out_specs=pl.BlockSpec((tm, cols), lambda i, s: (s[1] * nblk + i, 0)))
    return pl.pallas_call(
        body, out_shape=jax.ShapeDtypeStruct((2 * half, cols), F32), grid_spec=grid_spec, name="rs_final_sum",
        compiler_params=_params("parallel"))(place, tsend, recv)


def _join_halves(gfulls, name, sequencer=None):
    def body(g_refs, out_refs, send_sems, recv_sems):
        x, y, c = _place()
        n = len(g_refs)

        def copy(g, pc):
            half = g_refs[g].shape[0] // 2
            return pltpu.make_async_remote_copy(
                src_ref=g_refs[g].at[pl.ds(pc * half, half), :], dst_ref=out_refs[g].at[pl.ds(pc * half, half), :],
                send_sem=send_sems.at[g], recv_sem=recv_sems.at[g], device_id=(x, y, 1 - c), device_id_type=MESH_ID)

        mine = [copy(g, c) for g in range(n)]
        for cp in mine:
            cp.start()
        for g in range(n):
            copy(g, 1 - c).wait_recv()
        for cp in mine:
            cp.wait_send()

    return _exchange(body, gfulls, None, len(gfulls), name, sequencer and (sequencer, _sibling))


def _allreduce_small(v):
    rows, cols = v.shape

    def body(v_ref, out_ref, buf, send_sems, recv_sems):
        x, y, c = _place()
        cps = []
        for k in range(1, 8):
            peer = (_flip(x, (k >> 2) & 1), _flip(y, (k >> 1) & 1), _flip(c, k & 1))
            cps.append(pltpu.make_async_remote_copy(
                src_ref=v_ref, dst_ref=buf.at[k - 1], send_sem=send_sems.at[k - 1], recv_sem=recv_sems.at[k - 1],
                device_id=peer, device_id_type=MESH_ID))
        for cp in cps:
            cp.start()
        for cp in cps:
            cp.wait_recv()
        for cp in cps:
            cp.wait_send()
        t0 = v_ref[...] + buf[0]
        t1 = buf[1] + buf[2]
        t2 = buf[3] + buf[4]
        t3 = buf[5] + buf[6]
        out_ref[...] = (t0 + t1) + (t2 + t3)

    vm = pl.BlockSpec(memory_space=pltpu.VMEM)
    return pl.pallas_call(
        body, out_shape=jax.ShapeDtypeStruct((rows, cols), F32), in_specs=[vm], out_specs=vm,
        scratch_shapes=[pltpu.VMEM((7, rows, cols), F32), pltpu.SemaphoreType.DMA((7,)), pltpu.SemaphoreType.DMA((7,))],
        name="allreduce_small")(v)


BIG_INFO = {n: (shape, ax) for n, shape, ax in BIG}
GROUPS = (("w_in",), ("w_ffn_gate", "w_ffn_up", "w_ffn_down", "w_out", "w_ple_gate"),
          ("w_branch_a", "w_branch_b", "w_ple_proj"))


def _shard_shape(name):
    (k, m), ax = BIG_INFO[name]
    return (k // N_CHIPS, m) if ax == 0 else (k, m // N_CHIPS)


def _group_rows(group):
    offs, off = {}, 0
    for n in group:
        offs[n] = off
        off += _shard_shape(n)[0]
    return offs, off


def _pack_groups(shards, layer, dtype):
    return [jnp.concatenate([shards[n][layer].astype(dtype) for n in group], axis=0) for group in GROUPS]


def _unpack_full(gathered, groups):
    out = {}
    for group, arr in zip(groups, gathered):
        offs, _ = _group_rows(group)
        for n in group:
            rows, cols = _shard_shape(n)
            (k, m), ax = BIG_INFO[n]
            slab = arr[:, offs[n]:offs[n] + rows]
            out[n] = slab.reshape(k, m) if ax == 0 else jnp.transpose(slab, (1, 0, 2)).reshape(k, m)
    return out


def _pack_grads(gfull):
    out = []
    for group in GROUPS:
        parts = []
        for n in group:
            rows, cols = _shard_shape(n)
            ax = BIG_INFO[n][1]
            slab = (gfull[n].reshape(N_CHIPS, rows, cols) if ax == 0
                    else jnp.transpose(gfull[n].reshape(rows, N_CHIPS, cols), (1, 0, 2)))
            parts.append(slab)
        out.append(jnp.concatenate(parts, axis=1))
    return out


def _after(values, mark):
    values, _ = lax.optimization_barrier((values, mark))
    return values


def _reduce_scatter_begin(gsends, place, tag, ids):
    sibs = _sibling_halves(gsends, "rs_sibling_halves_" + tag, ids[0])
    tsends = [_chip_sums(g, s, place) for g, s in zip(gsends, sibs)]
    return tsends, _exchange_chip_sums(tsends, "rs_exchange_" + tag, ids[1])


def _reduce_scatter_finish(begun, place, tag, ids, hold):
    tsends, recvs = begun
    recvs = _after(recvs, hold)
    return _join_halves([_final_sum(t, r, place) for t, r in zip(tsends, recvs)], "rs_join_halves_" + tag, ids[2])


SMALL_SHAPES = {"rel_table": (NUM_BUCKETS, 2 * N_HEADS), "norm_mix_g": (DEPTH, D_MODEL), "qnorm_a_g": (DEPTH, HEAD_DIM),
                "knorm_a_g": (DEPTH, HEAD_DIM), "qnorm_b_g": (DEPTH, HEAD_DIM), "knorm_b_g": (DEPTH, HEAD_DIM),
                "sink_b": (DEPTH, N_HEADS), "norm_ffn_g": (DEPTH, D_MODEL), "norm_ple_g": (DEPTH, D_MODEL)}


def _pack_small(vals, last=None):
    flat = jnp.concatenate([vals[n].astype(F32).reshape(-1) for n in SMALL])
    tail = jnp.zeros((SMALL_ROWS * LANES - flat.shape[0],), F32)
    if last is not None:
        tail = tail.at[-1].set(last)
    return jnp.concatenate([flat, tail]).reshape(SMALL_ROWS, LANES)


def _unpack_small(packed):
    flat, out, off = packed.reshape(-1), {}, 0
    for n in SMALL:
        size = math.prod(SMALL_SHAPES[n])
        out[n] = flat[off:off + size].reshape(SMALL_SHAPES[n])
        off += size
    return out


def _adamw(w, gs, g_row, m, v, name):
    c1 = 1.0 - ADAM_B1 ** ADAM_STEP
    c2 = 1.0 - ADAM_B2 ** ADAM_STEP
    total, width = w.shape
    n_layers = len(gs)
    per = total // n_layers
    tm = max(t for t in range(8, 513, 8) if per % t == 0 and g_row % t == 0)
    nblk = per // tm

    def body(*refs):
        w_ref, g_refs = refs[0], refs[1:1 + n_layers]
        m_ref, v_ref, og, od, om, ov = refs[1 + n_layers:]
        layer = pl.program_id(0) // nblk
        g = g_refs[0][...]
        for l in range(1, n_layers):
            g = jnp.where(layer == l, g_refs[l][...], g)
        m_new = ADAM_B1 * m_ref[...] + (1.0 - ADAM_B1) * g
        v_new = ADAM_B2 * v_ref[...] + (1.0 - ADAM_B2) * (g * g)
        og[...] = g
        od[...] = -ADAM_LR * ((m_new / c1) / (jnp.sqrt(v_new / c2) + ADAM_EPS) + ADAM_WD * w_ref[...])
        om[...] = m_new
        ov[...] = v_new

    row = pl.BlockSpec((tm, width), lambda i: (i, 0))
    g_specs = [pl.BlockSpec((tm, width), lambda i, l=l: (g_row // tm + jnp.clip(i - l * nblk, 0, nblk - 1), 0))
               for l in range(n_layers)]
    return pl.pallas_call(
        body, out_shape=[jax.ShapeDtypeStruct((total, width), F32)] * 4, grid=(total // tm,),
        in_specs=[row] + g_specs + [row, row], out_specs=[row] * 4, name=name,
        compiler_params=_params("parallel"))(w, *gs, m, v)


def kernel(x, p, rel_table, norm_mix_g, w_in, qnorm_a_g, knorm_a_g, qnorm_b_g, knorm_b_g, sink_b, w_branch_a, w_branch_b, w_out, norm_ffn_g, w_ffn_gate, w_ffn_up, w_ffn_down, norm_ple_g, w_ple_gate, w_ple_proj, loss_target, m_rel_table, m_norm_mix_g, m_w_in, m_qnorm_a_g, m_knorm_a_g, m_qnorm_b_g, m_knorm_b_g, m_sink_b, m_w_branch_a, m_w_branch_b, m_w_out, m_norm_ffn_g, m_w_ffn_gate, m_w_ffn_up, m_w_ffn_down, m_norm_ple_g, m_w_ple_gate, m_w_ple_proj, v_rel_table, v_norm_mix_g, v_w_in, v_qnorm_a_g, v_knorm_a_g, v_qnorm_b_g, v_knorm_b_g, v_sink_b, v_w_branch_a, v_w_branch_b, v_w_out, v_norm_ffn_g, v_w_ffn_gate, v_w_ffn_up, v_w_ffn_down, v_norm_ple_g, v_w_ple_gate, v_w_ple_proj):
    given = dict(locals())

    def held(name, a):
        return jnp.swapaxes(a, 1, 2) if name in TRANSPOSED else a

    weights = {n: held(n, given[n]) for n in WEIGHTS}
    moments_m = {n: held(n, given["m_" + n]) for n in WEIGHTS}
    moments_v = {n: held(n, given["v_" + n]) for n in WEIGHTS}
    xi, yi, ci = _place()
    place = jnp.stack([2 * xi + yi, ci]).astype(jnp.int32)

    shards = [_pack_groups(weights, l, BF16) for l in range(DEPTH)]
    w_in0 = _allgather(shards[0][:1], "allgather_w_in_layer0", sequencer=9)
    rest0 = _allgather(_after(shards[0][1:], w_in0), "allgather_rest_layer0", sequencer=1)
    gathered = [w_in0 + rest0, None]
    small = {n: weights[n] for n in SMALL}

    def w_in_of(l, mark):
        return _unpack_full(_after(gathered[l][:1], (shards[1], mark) if l == 0 else mark), GROUPS[:1])["w_in"]

    def rest_of(l, mark):
        if l == 0:
            gathered[1] = _allgather(_after(shards[1], mark), "allgather_layer1", sequencer=2)
        return _unpack_full(_after(gathered[l][1:], mark), GROUPS[1:])

    loss, dx, gbig, gsmall, marks = _local_step(x[0], p[:, 0], loss_target[0], w_in_of, rest_of, small)

    gsends = [_pack_grads(gbig[l]) for l in range(DEPTH)]
    stages = {"layer1": (gsends[1], (3, 4, 5)), "rest_layer0": (gsends[0][1:], (6, 7, 8)),
              "w_in_layer0": (gsends[0][:1], (10, 11, 12))}
    begun = {tag: _reduce_scatter_begin(g, place, tag, ids) for tag, (g, ids) in stages.items()}

    def finish(tag, hold):
        return _reduce_scatter_finish(begun[tag], place, tag, stages[tag][1], hold)

    red1 = finish("layer1", marks[0]["attn_bwd_done"])
    rest0 = finish("rest_layer0", gbig[0]["w_in"])

    grads, delta, new_m, new_v = {}, {}, {}, {}

    def update(group, reduced):
        offs, _ = _group_rows(group)
        for n in group:
            shape = weights[n].shape
            two_d = lambda a: a.reshape(shape[0] * shape[1], shape[2])
            outs = _adamw(two_d(weights[n]), reduced, offs[n], two_d(moments_m[n]), two_d(moments_v[n]), "adamw_" + n)
            grads[n], delta[n], new_m[n], new_v[n] = (held(n, o.reshape(shape)) for o in outs)

    for gi in (1, 2):
        update(GROUPS[gi], _after([rest0[gi - 1], red1[gi]], begun["w_in_layer0"][0]))
    small_grads = _allreduce_small(_pack_small(gsmall, last=loss))
    g_, d_, m_, v_ = _adamw(_pack_small(weights), [small_grads], 0, _pack_small(moments_m), _pack_small(moments_v),
                            "adamw_small")
    grads.update(_unpack_small(g_))
    delta.update(_unpack_small(d_))
    new_m.update(_unpack_small(m_))
    new_v.update(_unpack_small(v_))
    others_done = [dx, d_] + [delta[n] for gi in (1, 2) for n in GROUPS[gi]]
    update(GROUPS[0], [finish("w_in_layer0", others_done)[0], red1[0]])

    return (small_grads[-1, -1], dx[None], *[grads[n] for n in WEIGHTS], *[delta[n] for n in WEIGHTS],
            *[new_m[n] for n in WEIGHTS], *[new_v[n] for n in WEIGHTS])
```

```python
import functools
import math

import jax
import jax.numpy as jnp
from jax import lax
from jax.experimental import pallas as pl
from jax.experimental.pallas import tpu as pltpu
from jax.experimental.pallas import tpu_sc as plsc

F32 = jnp.float32
BF16 = jnp.bfloat16
MESH_ID = pl.DeviceIdType.MESH

D_MODEL = 1024
DEPTH = 2
HEAD_DIM = 64
N_HEADS = 8
WIDTH = N_HEADS * HEAD_DIM
N_PAIRS = 4
ITEMS = 16
MAX_CHUNK = 1024
N_KV_B = 2
PLE_DIM = 256
D_FF = 2816
D_IN = 4352
OFF_QA, OFF_KA, OFF_VA, OFF_QB, OFF_KB, OFF_VB, OFF_GA, OFF_GB = 0, 512, 1024, 1536, 2048, 2176, 2304, 3328
DILATED = ((64, 1), (64, 4), (64, 16))
BLK_B = 128
NUM_BUCKETS = 32
MAX_DISTANCE = 1024
RMS_EPS = 1e-6
NEG_INF = -1e30
LANES = 128
ROW_TILE = 256
VMEM_LIMIT = 48 * 1024 * 1024

ADAM_LR, ADAM_B1, ADAM_B2, ADAM_EPS, ADAM_WD, ADAM_STEP = 0.001, 0.9, 0.999, 1e-08, 0.01, 10

TRANSPOSED = ("w_in", "w_ffn_gate", "w_ffn_up")
BIG = (
    ("w_in", (D_IN, D_MODEL), 0),
    ("w_branch_a", (WIDTH, D_MODEL), 1),
    ("w_branch_b", (WIDTH, D_MODEL), 1),
    ("w_out", (D_MODEL, D_MODEL), 0),
    ("w_ffn_gate", (D_FF, D_MODEL), 0),
    ("w_ffn_up", (D_FF, D_MODEL), 0),
    ("w_ffn_down", (D_FF, D_MODEL), 0),
    ("w_ple_gate", (D_MODEL, D_MODEL), 0),
    ("w_ple_proj", (PLE_DIM, D_MODEL), 1),
)
SMALL = ("rel_table", "norm_mix_g", "qnorm_a_g", "knorm_a_g", "qnorm_b_g", "knorm_b_g", "sink_b",
         "norm_ffn_g", "norm_ple_g")
WEIGHTS = ("rel_table", "norm_mix_g", "w_in", "qnorm_a_g", "knorm_a_g", "qnorm_b_g", "knorm_b_g", "sink_b",
           "w_branch_a", "w_branch_b", "w_out", "norm_ffn_g", "w_ffn_gate", "w_ffn_up", "w_ffn_down",
           "norm_ple_g", "w_ple_gate", "w_ple_proj")
N_CHIPS = 4
SMALL_ROWS = 64


def _params(*sem):
    return pltpu.CompilerParams(dimension_semantics=sem, vmem_limit_bytes=VMEM_LIMIT)


MM_VMEM_BUDGET = 40 * 1024 * 1024
STEP_OVERHEAD_S = 0.4e-6
TILE_DMA_BYTES_PER_S = 1.5e12


def _mm_dims(a, b, mode):
    if mode == "nn":
        return a.shape[0], b.shape[1], a.shape[1]
    if mode == "nt":
        return a.shape[0], b.shape[0], a.shape[1]
    return a.shape[1], b.shape[1], a.shape[0]


def _mm_tiles(m, n, pairs, tile_bytes, col_offsets, full_rows=False):
    best = None
    widths = [n] if full_rows else [t for t in range(LANES, n + 1, LANES) if n % t == 0 and all(o % t == 0 for o in col_offsets)]
    for tm in (t for t in range(LANES, m + 1, LANES) if m % t == 0):
        for tn in widths:
            io = sum(tm * k * ab + tn * k * bb for k, ab, bb in pairs) + tm * tn * sum(tile_bytes)
            casts = sum((tm * k * 2 if ab == 4 else 0) + (tn * k * 2 if bb == 4 else 0) for k, ab, bb in pairs)
            if 2 * io + len(pairs) * tm * tn * 4 + casts > MM_VMEM_BUDGET:
                continue
            cost = (m // tm) * (n // tn) * STEP_OVERHEAD_S + io / TILE_DMA_BYTES_PER_S
            if best is None or (cost, -tm) < best[0]:
                best = ((cost, -tm), tm, tn)
    return best[1], best[2]


def _mm_fused(pairs, extras, epilogue, out_dtypes, name, next_gain=None):
    m, n, _ = _mm_dims(*pairs[0])
    assert all(_mm_dims(*p)[:2] == (m, n) for p in pairs)
    with_norm = next_gain is not None
    out_dtypes = list(out_dtypes) + ([BF16] if with_norm else [])
    tm, tn = _mm_tiles(
        m, n, [(_mm_dims(a, b, mode)[2], a.dtype.itemsize, b.dtype.itemsize) for a, b, mode in pairs],
        [e.dtype.itemsize for e, _ in extras] + [jnp.dtype(d).itemsize for d in out_dtypes], [off for _, off in extras],
        full_rows=with_norm)
    dims = {"nn": (((1,), (0,)), ((), ())), "nt": (((1,), (1,)), ((), ())), "tn": (((0,), (0,)), ((), ()))}
    in_specs, args = [], []
    for a, b, mode in pairs:
        k = _mm_dims(a, b, mode)[2]
        in_specs.append(pl.BlockSpec((k, tm), lambda i, j: (0, i)) if mode == "tn" else pl.BlockSpec((tm, k), lambda i, j: (i, 0)))
        in_specs.append(pl.BlockSpec((tn, k), lambda i, j: (j, 0)) if mode == "nt" else pl.BlockSpec((k, tn), lambda i, j: (0, j)))
        args += [a, b]
    for e, off in extras:
        in_specs.append(pl.BlockSpec((tm, tn), lambda i, j, o=off // tn: (i, o + j)))
        args.append(e)
    n_pairs, n_tiles = len(pairs), 2 * len(pairs) + len(extras)
    if with_norm:
        in_specs.append(pl.BlockSpec((1, n), lambda i, j: (0, 0)))
        args.append(next_gain)
    n_in = len(args)

    def body(*refs):
        products = [lax.dot_general(refs[2 * p][...].astype(BF16), refs[2 * p + 1][...].astype(BF16), dims[pairs[p][2]],
                                    preferred_element_type=F32) for p in range(n_pairs)]
        outs = list(epilogue(products, [r[...] for r in refs[2 * n_pairs:n_tiles]]))
        if with_norm:
            y = outs[0]
            outs.append((y * lax.rsqrt(jnp.mean(y * y, axis=-1, keepdims=True) + RMS_EPS)) * refs[n_tiles][...])
        for r, o in zip(refs[n_in:], outs):
            r[...] = o.astype(r.dtype)

    tile = pl.BlockSpec((tm, tn), lambda i, j: (i, j))
    return pl.pallas_call(
        body, out_shape=[jax.ShapeDtypeStruct((m, n), d) for d in out_dtypes], grid=(m // tm, n // tn),
        in_specs=in_specs, out_specs=[tile] * len(out_dtypes), name=name,
        compiler_params=_params("parallel", "parallel"))(*args)


def _mm(a, b, mode, out_dtype, name, res=None):
    if res is None:
        return _mm_fused([(a, b, mode)], [], lambda products, extra: products, [out_dtype], name)[0]
    return _mm_fused([(a, b, mode)], [(res, 0)], lambda products, extra: [products[0] + extra[0]], [out_dtype], name)[0]


def _mm_res_norm(a, b, mode, res, gain, name):
    return _mm_fused([(a, b, mode)], [(res, 0)], lambda products, extra: [products[0] + extra[0]], [F32], name,
                     next_gain=gain)


def _ew(fn, ins, out_dtypes, name):
    rows, width = ins[0].shape
    n_in = len(ins)

    def body(*refs):
        outs = fn(*[r[...] for r in refs[:n_in]])
        for r, o in zip(refs[n_in:], outs):
            r[...] = o.astype(r.dtype)

    row = pl.BlockSpec((ROW_TILE, width), lambda i: (i, 0))
    return pl.pallas_call(
        body, out_shape=[jax.ShapeDtypeStruct((rows, width), dt) for dt in out_dtypes], grid=(rows // ROW_TILE,),
        in_specs=[row] * n_in, out_specs=[row] * len(out_dtypes), name=name, compiler_params=_params("parallel"))(*ins)


def _sigmoid(x):
    return 1.0 / (1.0 + jnp.exp(-x))


def _seg_sum(v):
    outs = []
    for k in range(v.shape[1] // LANES):
        vp = v[:, k * LANES:(k + 1) * LANES]
        left = lax.broadcasted_iota(jnp.int32, vp.shape, 1) < HEAD_DIM
        sl = jnp.sum(jnp.where(left, vp, 0.0), axis=-1, keepdims=True)
        sr = jnp.sum(jnp.where(left, 0.0, vp), axis=-1, keepdims=True)
        outs.append(jnp.where(left, sl, sr))
    return outs[0] if len(outs) == 1 else jnp.concatenate(outs, axis=1)


def _seg_rstd(x):
    return lax.rsqrt(_seg_sum(x * x) * (1.0 / HEAD_DIM) + RMS_EPS)


def _rms_fwd(x, g, name):
    rows, d = x.shape
    tm = ROW_TILE

    def body(x_ref, g_ref, h_ref):
        xv = x_ref[...]
        r = lax.rsqrt(jnp.mean(xv * xv, axis=-1, keepdims=True) + RMS_EPS)
        h_ref[...] = ((xv * r) * g_ref[...]).astype(BF16)

    return pl.pallas_call(
        body, out_shape=jax.ShapeDtypeStruct((rows, d), BF16), grid=(rows // tm,),
        in_specs=[pl.BlockSpec((tm, d), lambda i: (i, 0)), pl.BlockSpec((1, d), lambda i: (0, 0))],
        out_specs=pl.BlockSpec((tm, d), lambda i: (i, 0)), name=name,
        compiler_params=_params("parallel"))(x, g)


def _mm_rms_bwd(pairs, x, g, dres, name):
    rows, d = x.shape
    assert all(_mm_dims(*p)[:2] == (rows, d) for p in pairs)
    kbytes = [(_mm_dims(a, b, mode)[2], a.dtype.itemsize, b.dtype.itemsize) for a, b, mode in pairs]
    tm = max(t for t in (512, 256, 128) if rows % t == 0 and
             2 * (sum(t * k * ab + d * k * bb for k, ab, bb in kbytes) + t * d * 14) + 2 * t * d * 4 <= MM_VMEM_BUDGET)
    dims = {"nn": (((1,), (0,)), ((), ())), "nt": (((1,), (1,)), ((), ()))}
    n_pairs = len(pairs)

    def body(*refs):
        x_ref, g_ref, dres_ref = refs[2 * n_pairs:2 * n_pairs + 3]
        dx_ref, dxb_ref, dg_ref = refs[2 * n_pairs + 3:]
        dhv = functools.reduce(lambda u, v: u + v, [
            lax.dot_general(refs[2 * p][...].astype(BF16), refs[2 * p + 1][...].astype(BF16), dims[pairs[p][2]],
                            preferred_element_type=F32) for p in range(n_pairs)])
        xv = x_ref[...]
        r = lax.rsqrt(jnp.mean(xv * xv, axis=-1, keepdims=True) + RMS_EPS)
        xh = xv * r
        dxh = dhv * g_ref[...]
        dxv = dres_ref[...] + r * (dxh - xh * jnp.mean(dxh * xh, axis=-1, keepdims=True))
        dx_ref[...] = dxv
        dxb_ref[...] = dxv.astype(BF16)
        part = jnp.sum(dhv * xh, axis=0, keepdims=True)

        @pl.when(pl.program_id(0) == 0)
        def _():
            dg_ref[...] = part

        @pl.when(pl.program_id(0) > 0)
        def _():
            dg_ref[...] += part

    row = pl.BlockSpec((tm, d), lambda i: (i, 0))
    vec = pl.BlockSpec((1, d), lambda i: (0, 0))
    in_specs, args = [], []
    for a, b, mode in pairs:
        in_specs += [pl.BlockSpec((tm, a.shape[1]), lambda i: (i, 0)), pl.BlockSpec(b.shape, lambda i: (0, 0))]
        args += [a, b]
    return pl.pallas_call(
        body, out_shape=[jax.ShapeDtypeStruct((rows, d), F32), jax.ShapeDtypeStruct((rows, d), BF16),
                         jax.ShapeDtypeStruct((1, d), F32)],
        grid=(rows // tm,), in_specs=in_specs + [row, vec, row], out_specs=[row, row, vec],
        name=name, compiler_params=_params("arbitrary"))(*args, x, g, dres)


def _ple_bwd(dx, z, e):
    s = _sigmoid(z.astype(F32))
    return dx * s, dx * e.astype(F32) * (s * (1.0 - s))


def _loss_grad(y, t, z, e):
    rows, d = y.shape
    tm = ROW_TILE

    def body(y_ref, t_ref, z_ref, e_ref, dy_ref, de_ref, dz_ref, l_ref):
        err = y_ref[...] - t_ref[...]
        dy = err * (1.0 / d)
        dy_ref[...] = dy
        de, dz = _ple_bwd(dy, z_ref[...], e_ref[...])
        de_ref[...] = de.astype(BF16)
        dz_ref[...] = dz.astype(BF16)
        part = jnp.zeros((1, LANES), F32) + jnp.sum(err * err) * (0.5 / d)

        @pl.when(pl.program_id(0) == 0)
        def _():
            l_ref[...] = part

        @pl.when(pl.program_id(0) > 0)
        def _():
            l_ref[...] += part

    row = pl.BlockSpec((tm, d), lambda i: (i, 0))
    return pl.pallas_call(
        body, out_shape=[jax.ShapeDtypeStruct((rows, d), F32), jax.ShapeDtypeStruct((rows, d), BF16),
                         jax.ShapeDtypeStruct((rows, d), BF16), jax.ShapeDtypeStruct((1, LANES), F32)],
        grid=(rows // tm,), in_specs=[row] * 4, out_specs=[row, row, row, pl.BlockSpec((1, LANES), lambda i: (0, 0))],
        name="loss_grad", compiler_params=_params("arbitrary"))(y, t, z, e)


def _swap_halves(v):
    return pltpu.roll(v, HEAD_DIM, axis=1)


def _expand_kv(kv):
    left = lax.broadcasted_iota(jnp.int32, kv.shape, 1) < HEAD_DIM
    sw = _swap_halves(kv)
    h0 = jnp.where(left, kv, sw)
    h1 = jnp.where(left, sw, kv)
    return jnp.concatenate([h0, h0, h1, h1], axis=1)


def _reduce_kv(dkv):
    left = lax.broadcasted_iota(jnp.int32, (dkv.shape[0], LANES), 1) < HEAD_DIM
    t = dkv[:, 0:LANES] + dkv[:, LANES:2 * LANES]
    u = dkv[:, 2 * LANES:3 * LANES] + dkv[:, 3 * LANES:4 * LANES]
    t = t + _swap_halves(t)
    u = u + _swap_halves(u)
    return jnp.where(left, t, u)


def _qknorm_fwd(proj, gqa, gka, gqb, gkb):
    rows = proj.shape[0]
    tm = ROW_TILE

    def body(qa_ref, ka_ref, qb_ref, kb_ref, vb_ref, gqa_ref, gka_ref, gqb_ref, gkb_ref, oqa, oka, oqb, okb, ovb):
        for src, g_ref, dst in ((qa_ref, gqa_ref, oqa), (ka_ref, gka_ref, oka), (qb_ref, gqb_ref, oqb)):
            xv = src[...]
            dst[...] = (xv * _seg_rstd(xv)) * g_ref[...]
        kv = kb_ref[...]
        okb[...] = _expand_kv((kv * _seg_rstd(kv)) * gkb_ref[...])
        ovb[...] = _expand_kv(vb_ref[...])

    def win(width, off):
        return pl.BlockSpec((tm, width), lambda i: (i, off // width))

    vec = lambda w: pl.BlockSpec((1, w), lambda i: (0, 0))
    out = pl.BlockSpec((tm, WIDTH), lambda i: (i, 0))
    return pl.pallas_call(
        body, out_shape=[jax.ShapeDtypeStruct((rows, WIDTH), F32)] * 5, grid=(rows // tm,),
        in_specs=[win(WIDTH, OFF_QA), win(WIDTH, OFF_KA), win(WIDTH, OFF_QB), win(LANES, OFF_KB), win(LANES, OFF_VB),
                  vec(WIDTH), vec(WIDTH), vec(WIDTH), vec(LANES)],
        out_specs=[out] * 5, name="qknorm_fwd", compiler_params=_params("parallel"))(
            proj, proj, proj, proj, proj, gqa, gka, gqb, gkb)


def _norm_bwd(xv, g, dy):
    r = _seg_rstd(xv)
    xh = xv * r
    dxh = dy * g
    dx = r * (dxh - xh * (_seg_sum(dxh * xh) * (1.0 / HEAD_DIM)))
    return dx, jnp.sum(dy * xh, axis=0, keepdims=True)


def _qknorm_bwd(proj, gqa, gka, gqb, gkb, dqa, dka, dva, dqb, dkb, dvb, dga, dgb):
    rows = proj.shape[0]
    tm = ROW_TILE
    n_a = len(dqa)

    def body(*refs):
        qa_ref, ka_ref, qb_ref, kb_ref, gqa_ref, gka_ref, gqb_ref, gkb_ref = refs[:8]
        pos = 8
        dqa_refs, dka_refs, dva_refs = refs[pos:pos + n_a], refs[pos + n_a:pos + 2 * n_a], refs[pos + 2 * n_a:pos + 3 * n_a]
        pos += 3 * n_a
        dqb_ref, dkb_ref, dvb_ref, dga_ref, dgb_ref = refs[pos:pos + 5]
        dproj_ref, ogqa, ogka, ogqb, ogkb = refs[pos + 5:]

        def total(rs):
            acc = rs[0][...]
            for r in rs[1:]:
                acc = acc + r[...]
            return acc

        dx_qa, p_qa = _norm_bwd(qa_ref[...], gqa_ref[...], total(dqa_refs))
        dx_ka, p_ka = _norm_bwd(ka_ref[...], gka_ref[...], total(dka_refs))
        dx_qb, p_qb = _norm_bwd(qb_ref[...], gqb_ref[...], dqb_ref[...])
        dx_kb, p_kb = _norm_bwd(kb_ref[...], gkb_ref[...], _reduce_kv(dkb_ref[...]))
        dproj_ref[:, OFF_QA:OFF_QA + WIDTH] = dx_qa.astype(BF16)
        dproj_ref[:, OFF_KA:OFF_KA + WIDTH] = dx_ka.astype(BF16)
        dproj_ref[:, OFF_VA:OFF_VA + WIDTH] = total(dva_refs).astype(BF16)
        dproj_ref[:, OFF_QB:OFF_QB + WIDTH] = dx_qb.astype(BF16)
        dproj_ref[:, OFF_KB:OFF_KB + LANES] = dx_kb.astype(BF16)
        dproj_ref[:, OFF_VB:OFF_VB + LANES] = _reduce_kv(dvb_ref[...]).astype(BF16)
        dproj_ref[:, OFF_GA:OFF_GB] = dga_ref[...]
        dproj_ref[:, OFF_GB:D_IN] = dgb_ref[...]
        first = pl.program_id(0) == 0
        for o_ref, part in ((ogqa, p_qa), (ogka, p_ka), (ogqb, p_qb), (ogkb, p_kb)):
            @pl.when(first)
            def _(o_ref=o_ref, part=part):
                o_ref[...] = part

            @pl.when(jnp.logical_not(first))
            def _(o_ref=o_ref, part=part):
                o_ref[...] += part

    def win(width, off):
        return pl.BlockSpec((tm, width), lambda i: (i, off // width))

    vec = lambda w: pl.BlockSpec((1, w), lambda i: (0, 0))
    row = lambda w: pl.BlockSpec((tm, w), lambda i: (i, 0))
    in_specs = [win(WIDTH, OFF_QA), win(WIDTH, OFF_KA), win(WIDTH, OFF_QB), win(LANES, OFF_KB),
                vec(WIDTH), vec(WIDTH), vec(WIDTH), vec(LANES)]
    in_specs += [row(WIDTH)] * (3 * n_a + 3) + [row(D_MODEL)] * 2
    return pl.pallas_call(
        body,
        out_shape=[jax.ShapeDtypeStruct((rows, D_IN), BF16), jax.ShapeDtypeStruct((1, WIDTH), F32),
                   jax.ShapeDtypeStruct((1, WIDTH), F32), jax.ShapeDtypeStruct((1, WIDTH), F32),
                   jax.ShapeDtypeStruct((1, LANES), F32)],
        grid=(rows // tm,), in_specs=in_specs,
        out_specs=[row(D_IN), vec(WIDTH), vec(WIDTH), vec(WIDTH), vec(LANES)],
        name="qknorm_bwd", compiler_params=_params("arbitrary"))(
            proj, proj, proj, proj, gqa, gka, gqb, gkb, *dqa, *dka, *dva, dqb, dkb, dvb, dga, dgb)


def _t5_bucket(rel):
    half_b = NUM_BUCKETS // 2
    max_exact = half_b // 2
    sign = jnp.where(rel > 0, half_b, 0)
    n = jnp.abs(rel)
    nf = jnp.maximum(n, 1).astype(F32)
    large = max_exact + (jnp.log(nf / max_exact) / math.log(MAX_DISTANCE / max_exact)
                         * (half_b - max_exact)).astype(jnp.int32)
    large = jnp.minimum(large, half_b - 1)
    return sign + jnp.where(n < max_exact, n, large)


def _band_buckets(blk, dilation):
    i = jnp.arange(blk, dtype=jnp.int32)[:, None]
    j = jnp.arange(3 * blk, dtype=jnp.int32)[None, :]
    rel = j - blk - i
    return jnp.where(jnp.abs(rel) <= blk, _t5_bucket(rel * dilation), -1)


def _bias_tiles(table, buckets, head_off, name):
    blk = buckets.shape[0]

    def body(tab_ref, bk_ref, o_ref):
        h = pl.program_id(0) + head_off
        bk = bk_ref[...]
        acc = jnp.full(bk.shape, NEG_INF, F32)
        for b in range(NUM_BUCKETS):
            acc = jnp.where(bk == b, tab_ref[b, h], acc)
        o_ref[0] = acc

    return pl.pallas_call(
        body, out_shape=jax.ShapeDtypeStruct((N_HEADS, blk, 3 * blk), F32), grid=(N_HEADS,),
        in_specs=[pl.BlockSpec(memory_space=pltpu.SMEM), pl.BlockSpec((blk, 3 * blk), lambda h: (0, 0))],
        out_specs=pl.BlockSpec((1, blk, 3 * blk), lambda h: (h, 0, 0)),
        name=name, compiler_params=_params("parallel"))(table, buckets)


def _table_grad(dbias, buckets, name):
    blk = buckets.shape[0]

    def body(db_ref, bk_ref, o_ref):
        bk = bk_ref[...]
        dbv = db_ref[0]
        lane = lax.broadcasted_iota(jnp.int32, (1, LANES), 1)
        acc = jnp.zeros((1, LANES), F32)
        for b in range(NUM_BUCKETS):
            acc = jnp.where(lane == b, jnp.sum(jnp.where(bk == b, dbv, 0.0)), acc)
        o_ref[0] = acc

    out = pl.pallas_call(
        body, out_shape=jax.ShapeDtypeStruct((N_HEADS, 1, LANES), F32), grid=(N_HEADS,),
        in_specs=[pl.BlockSpec((1, blk, 3 * blk), lambda h: (h, 0, 0)), pl.BlockSpec((blk, 3 * blk), lambda h: (0, 0))],
        out_specs=pl.BlockSpec((1, 1, LANES), lambda h: (h, 0, 0)),
        name=name, compiler_params=_params("parallel"))(dbias, buckets)
    return out[:, 0, :NUM_BUCKETS]


def _in_hbm(arrays):
    return [pltpu.with_memory_space_constraint(a, pltpu.HBM) for a in arrays]


def _dot_nt(a, b):
    return lax.dot_general(a, b, (((1,), (1,)), ((), ())), preferred_element_type=F32)


def _dot_tn(a, b):
    return lax.dot_general(a, b, (((0,), (0,)), ((), ())), preferred_element_type=F32)


def _stack_pair(x2, left):
    return jnp.concatenate([jnp.where(left, x2, 0.0), jnp.where(left, 0.0, x2)], axis=0).astype(BF16)


def _attn_geometry(blk, d):
    halo = blk * d
    subs = max(1, min(ITEMS // d, MAX_CHUNK // halo))
    return subs, min(d, ITEMS // subs), halo


def _item_of(j, r0, per_group):
    return j // per_group, r0 + j % per_group


def _span_rows(ref, first, r, blk, d):
    if d == 1:
        return ref[first:first + blk, :]
    return ref[pl.ds(first + r, blk, stride=d), :]


def _set_span_rows(ref, first, r, blk, d, val, add=False):
    idx = slice(first, first + blk) if d == 1 else pl.ds(first + r, blk, stride=d)
    ref[idx, :] = ref[idx, :] + val if add else val


def _for_groups(group, groups, per_group):
    if groups == 1:
        group(0)
    else:
        def step(g, carry):
            group(g * per_group)
            return carry

        lax.fori_loop(0, groups, step, 0)


def _key_rows(p_ref, c_ref, n_ref, s, r, subs, halo, blk, d):
    parts = []
    for span in (s - 1, s, s + 1):
        if span < 0:
            parts.append(_span_rows(p_ref, 0, r, blk, d))
        elif span == subs:
            parts.append(_span_rows(n_ref, 0, r, blk, d))
        else:
            parts.append(_span_rows(c_ref, span * halo, r, blk, d))
    return jnp.concatenate(parts, axis=0)


def _item_penalty(t, nct, s, subs, blk):
    first_ok = True if s > 0 else t > 0
    last_ok = True if s < subs - 1 else t < nct - 1
    col = lax.broadcasted_iota(jnp.int32, (1, 3 * blk), 1)
    ok = jnp.logical_and(jnp.logical_or(col >= blk, first_ok), jnp.logical_or(col < 2 * blk, last_ok))
    return jnp.where(ok, 0.0, NEG_INF).astype(F32)


def _attn_specs(seq, blk, d, step_of, col=0):
    subs, _, halo = _attn_geometry(blk, d)
    last, first = seq // halo - 1, col // LANES
    cur = pl.BlockSpec((subs * halo, LANES), lambda hp, t: (step_of(t), first + hp))
    prev = pl.BlockSpec((halo, LANES), lambda hp, t: (jnp.clip(step_of(t) * subs - 1, 0, last), first + hp))
    nxt = pl.BlockSpec((halo, LANES), lambda hp, t: (jnp.minimum((step_of(t) + 1) * subs, last), first + hp))
    return cur, prev, nxt


def _attn_fwd(q, k, v, bias, sink, blk, d, name, v_col=0):
    seq = q.shape[0]
    subs, per_group, halo = _attn_geometry(blk, d)
    items, chunk, groups = subs * per_group, subs * halo, d // per_group
    nct = seq // chunk
    has_sink = sink is not None
    scale = HEAD_DIM ** -0.5

    def body(*refs):
        q_ref, kp, kc, kn, vp, vc, vn, b_ref = refs[:8]
        s_ref = refs[8] if has_sink else None
        o_ref, l_ref = refs[-2], refs[-1]
        t = pl.program_id(1)
        left = lax.broadcasted_iota(jnp.int32, (1, LANES), 1) < HEAD_DIM
        bias2 = b_ref[...]

        def group(r0):
            scores, vcats = [], []
            for j in range(items):
                s, r = _item_of(j, r0, per_group)
                qs = _stack_pair(_span_rows(q_ref, s * halo, r, blk, d) * scale, left)
                kcat = _key_rows(kp, kc, kn, s, r, subs, halo, blk, d).astype(BF16)
                scores.append(_dot_nt(qs, kcat) + bias2 + _item_penalty(t, nct, s, subs, blk))
                vcats.append(_key_rows(vp, vc, vn, s, r, subs, halo, blk, d).astype(BF16))
            ms = [jnp.max(s, axis=-1, keepdims=True) for s in scores]
            if has_sink:
                sk = s_ref[...]
                ms = [jnp.maximum(m, sk) for m in ms]
            ps = [jnp.exp(s - m) for s, m in zip(scores, ms)]
            dens = [jnp.sum(p, axis=-1, keepdims=True) for p in ps]
            if has_sink:
                dens = [den + jnp.exp(sk - m) for den, m in zip(dens, ms)]
            pns = [(p * (1.0 / den)).astype(BF16) for p, den in zip(ps, dens)]
            lses = [m + jnp.log(den) for m, den in zip(ms, dens)]
            for j in range(items):
                s, r = _item_of(j, r0, per_group)
                o2 = jnp.dot(pns[j], vcats[j], preferred_element_type=F32)
                _set_span_rows(o_ref, s * halo, r, blk, d, jnp.where(left, o2[:blk], o2[blk:]))
                _set_span_rows(l_ref, s * halo, r, blk, d, jnp.where(left, lses[j][:blk], lses[j][blk:]))

        _for_groups(group, groups, per_group)

    cur, prev, nxt = _attn_specs(seq, blk, d, lambda t: t)
    v_cur, v_prev, v_nxt = _attn_specs(seq, blk, d, lambda t: t, v_col)
    in_specs = [cur, prev, cur, nxt, v_prev, v_cur, v_nxt, pl.BlockSpec((2 * blk, 3 * blk), lambda hp, t: (hp, 0))]
    args = [q, k, k, k, v, v, v, bias]
    if has_sink:
        in_specs.append(pl.BlockSpec((2 * blk, 1), lambda hp, t: (hp, 0)))
        args.append(sink)
    return pl.pallas_call(
        body, out_shape=[jax.ShapeDtypeStruct((seq, WIDTH), F32)] * 2, grid=(N_PAIRS, nct),
        in_specs=in_specs, out_specs=[cur, cur], name=name,
        compiler_params=_params("parallel", "parallel"))(*_in_hbm(args))


def _attn_bwd(q, k, v, do, lse, delta, bias, sink, blk, d, name, v_col=0):
    seq = q.shape[0]
    subs, per_group, halo = _attn_geometry(blk, d)
    items, chunk, groups = subs * per_group, subs * halo, d // per_group
    nct = seq // chunk
    has_sink = sink is not None
    n_in = 12 if has_sink else 11
    scale = HEAD_DIM ** -0.5

    def body(*refs):
        q_ref, kp, kc, kn, vp, vc, vn, do_ref, l_ref, d_ref, b_ref = refs[:11]
        s_ref = refs[11] if has_sink else None
        dq_ref, dk_ref, dv_ref, db_ref = refs[n_in:n_in + 4]
        ds_ref = refs[n_in + 4] if has_sink else None
        wk, wv = refs[-2], refs[-1]
        t = pl.program_id(1)

        @pl.when(t == 0)
        def _():
            wk[...] = jnp.zeros_like(wk)
            wv[...] = jnp.zeros_like(wv)
            db_ref[...] = jnp.zeros_like(db_ref)
            if has_sink:
                ds_ref[...] = jnp.zeros_like(ds_ref)

        @pl.when(t > 0)
        def _():
            for w in (wk, wv):
                keep = w[chunk:2 * chunk + halo]
                w[0:chunk + halo] = keep
                w[chunk + halo:2 * chunk + halo] = jnp.zeros((chunk, LANES), F32)

        @pl.when(t < nct)
        def _():
            lane = lax.broadcasted_iota(jnp.int32, (1, LANES), 1)
            left = lane < HEAD_DIM
            bias2 = b_ref[...]

            def group(r0):
                qss, doss, kcats, scores, dps, lcols, dcols = [], [], [], [], [], [], []
                for j in range(items):
                    s, r = _item_of(j, r0, per_group)
                    qs = _stack_pair(_span_rows(q_ref, s * halo, r, blk, d) * scale, left)
                    dos = _stack_pair(_span_rows(do_ref, s * halo, r, blk, d), left)
                    kcat = _key_rows(kp, kc, kn, s, r, subs, halo, blk, d).astype(BF16)
                    vcat = _key_rows(vp, vc, vn, s, r, subs, halo, blk, d).astype(BF16)
                    l2, d2 = _span_rows(l_ref, s * halo, r, blk, d), _span_rows(d_ref, s * halo, r, blk, d)
                    lcols.append(jnp.concatenate([jnp.max(jnp.where(left, l2, NEG_INF), axis=-1, keepdims=True),
                                                  jnp.max(jnp.where(left, NEG_INF, l2), axis=-1, keepdims=True)], axis=0))
                    dcols.append(jnp.concatenate([jnp.sum(jnp.where(lane == 0, d2, 0.0), axis=-1, keepdims=True),
                                                  jnp.sum(jnp.where(lane == HEAD_DIM, d2, 0.0), axis=-1, keepdims=True)],
                                                 axis=0))
                    scores.append(_dot_nt(qs, kcat) + bias2 + _item_penalty(t, nct, s, subs, blk))
                    dps.append(_dot_nt(dos, vcat))
                    qss.append(qs)
                    doss.append(dos)
                    kcats.append(kcat)
                ps = [jnp.exp(s - lc) for s, lc in zip(scores, lcols)]
                dss = [p * (dp - dc) for p, dp, dc in zip(ps, dps, dcols)]
                db_ref[...] += functools.reduce(lambda a, b: a + b, dss)
                if has_sink:
                    sk = s_ref[...]
                    ds_ref[...] -= functools.reduce(lambda a, b: a + b, [dc * jnp.exp(sk - lc) for dc, lc in zip(dcols, lcols)])
                dsbs = [ds.astype(BF16) for ds in dss]
                for j in range(items):
                    s, r = _item_of(j, r0, per_group)
                    dq2 = jnp.dot(dsbs[j], kcats[j], preferred_element_type=F32) * scale
                    _set_span_rows(dq_ref, s * halo, r, blk, d, jnp.where(left, dq2[:blk], dq2[blk:]))
                news = [(_dot_tn(dsbs[j], qss[j]), _dot_tn(ps[j].astype(BF16), doss[j])) for j in range(items)]
                for which, w in enumerate((wk, wv)):
                    for jr in range(per_group):
                        for sp in range(-1, subs + 1):
                            parts = [news[s * per_group + jr][which][(sp - s + 1) * blk:(sp - s + 2) * blk]
                                     for s in range(subs) if 0 <= sp - s + 1 < 3]
                            _set_span_rows(w, chunk + sp * halo, r0 + jr, blk, d,
                                           functools.reduce(lambda x, y: x + y, parts), add=True)

            _for_groups(group, groups, per_group)

        dk_ref[...] = wk[0:chunk]
        dv_ref[...] = wv[0:chunk]

    cur, prev, nxt = _attn_specs(seq, blk, d, lambda t: jnp.minimum(t, nct - 1))
    v_cur, v_prev, v_nxt = _attn_specs(seq, blk, d, lambda t: jnp.minimum(t, nct - 1), v_col)
    lag = pl.BlockSpec((chunk, LANES), lambda hp, t: (jnp.maximum(t - 1, 0), hp))
    band = pl.BlockSpec((2 * blk, 3 * blk), lambda hp, t: (hp, 0))
    col = pl.BlockSpec((2 * blk, 1), lambda hp, t: (hp, 0))
    in_specs = [cur, prev, cur, nxt, v_prev, v_cur, v_nxt, cur, cur, cur, band]
    args = [q, k, k, k, v, v, v, do, lse, delta, bias]
    out_shape = [jax.ShapeDtypeStruct((seq, WIDTH), F32)] * 3 + [jax.ShapeDtypeStruct((N_HEADS * blk, 3 * blk), F32)]
    out_specs = [cur, lag, lag, band]
    if has_sink:
        in_specs.append(col)
        args.append(sink)
        out_shape.append(jax.ShapeDtypeStruct((N_HEADS * blk, 1), F32))
        out_specs.append(col)
    window = pltpu.VMEM((2 * chunk + halo, LANES), F32)
    return pl.pallas_call(
        body, out_shape=out_shape, grid=(N_PAIRS, nct + 1), in_specs=in_specs, out_specs=out_specs,
        scratch_shapes=[window, window], name=name,
        compiler_params=_params("arbitrary", "arbitrary"))(*_in_hbm(args))


def _combine_patterns(outs, lses):
    def combine(*tiles):
        os_, ls = tiles[:len(outs)], tiles[len(outs):]
        m = functools.reduce(jnp.maximum, ls)
        es = [jnp.exp(l - m) for l in ls]
        den = functools.reduce(lambda a, b: a + b, es)
        num = functools.reduce(lambda a, b: a + b, [e * o for e, o in zip(es, os_)])
        return num / den, m + jnp.log(den)

    return _ew(combine, [*outs, *lses], [F32, F32], "combine_a")


def _tile_gain(g, reps):
    return jnp.tile(g[None, :], (1, reps))


def _local_step(x, p, target, w_in_of, rest_of, small):
    rel_table = small["rel_table"]
    buckets_a = [_band_buckets(blk, d) for blk, d in DILATED]
    buckets_b = _band_buckets(BLK_B, 1)
    bias_a = [_bias_tiles(rel_table, bk, 0, "bias_a").reshape(N_HEADS * bk.shape[0], -1) for bk in buckets_a]
    bias_b = _bias_tiles(rel_table, buckets_b, N_HEADS, "bias_b").reshape(N_HEADS * BLK_B, -1)

    saved = []
    for l in range(DEPTH):
        g_mix, g_ffn, g_ple = (small[n][l][None, :] for n in ("norm_mix_g", "norm_ffn_g", "norm_ple_g"))
        gqa, gka, gqb = (_tile_gain(small[n][l], N_HEADS) for n in ("qnorm_a_g", "knorm_a_g", "qnorm_b_g"))
        gkb = _tile_gain(small["knorm_b_g"][l], N_KV_B)
        sink = jnp.repeat(small["sink_b"][l], BLK_B)[:, None]

        h = _rms_fwd(x, g_mix, "rms_mix") if l == 0 else h_next
        w_in = w_in_of(l, (h, bias_a, bias_b))
        proj = _mm(h, w_in, "nt", F32, "mm_in")
        qa, ka, qb, kb, vb = _qknorm_fwd(proj, gqa, gka, gqb, gkb)
        outs, lses = [], []
        for (blk, d), bias in zip(DILATED, bias_a):
            o, ls = _attn_fwd(qa, ka, proj, bias, None, blk, d, f"attn_a{d}_fwd", v_col=OFF_VA)
            outs.append(o)
            lses.append(ls)
        ya, lse_a = _combine_patterns(outs, lses)
        yb, lse_b = _attn_fwd(qb, kb, vb, bias_b, sink, BLK_B, 1, "attn_b_fwd")
        w = dict(rest_of(l, yb), w_in=w_in)
        def gate(products, extra):
            (ca_, cb_), (ga_, gb_) = products, extra
            return _sigmoid(ga_) * ca_ + _sigmoid(gb_) * cb_, ca_, cb_

        merged, ca, cb = _mm_fused([(ya, w["w_branch_a"], "nn"), (yb, w["w_branch_b"], "nn")],
                                   [(proj, OFF_GA), (proj, OFF_GB)], gate, [BF16, BF16, BF16], "mm_branches_gate")
        x1, h2 = _mm_res_norm(merged, w["w_out"], "nn", x, g_ffn, "mm_out_norm")

        def swiglu(products, extra):
            a_, u_ = products
            return (a_ * _sigmoid(a_)) * u_, a_, u_

        hid, a, u = _mm_fused([(h2, w["w_ffn_gate"], "nt"), (h2, w["w_ffn_up"], "nt")], [], swiglu,
                              [BF16, BF16, BF16], "mm_ffn_gate_up")
        x2, h3 = _mm_res_norm(hid, w["w_ffn_down"], "nn", x1, g_ple, "mm_ffn_down_norm")

        def ple(products, extra):
            z_, e_ = products
            return extra[0] + _sigmoid(z_) * e_, z_, e_

        next_gain = small["norm_mix_g"][l + 1][None, :] if l + 1 < DEPTH else None
        x3, z, e, *rest = _mm_fused([(h3, w["w_ple_gate"], "nn"), (p[l], w["w_ple_proj"], "nn")], [(x2, 0)], ple,
                                    [F32, BF16, BF16], "mm_ple", next_gain=next_gain)
        h_next = rest[0] if rest else None
        saved.append(dict(w=w, x0=x, h=h, proj=proj, qa=qa, ka=ka, qb=qb, kb=kb, vb=vb, ya=ya, lse_a=lse_a,
                          yb=yb, lse_b=lse_b, ca=ca, cb=cb, merged=merged, x1=x1, h2=h2, a=a, u=u, hid=hid,
                          x2=x2, h3=h3, z=z, e=e))
        x = x3

    dx, de, dz, loss_acc = _loss_grad(x, target, saved[-1]["z"], saved[-1]["e"])
    loss = loss_acc[0, 0]

    gbig = [{} for _ in range(DEPTH)]
    marks = [{} for _ in range(DEPTH)]
    gsmall = {n: [None] * DEPTH for n in SMALL if n != "rel_table"}
    dbias_a = [[] for _ in DILATED]
    dbias_b = []

    for l in reversed(range(DEPTH)):
        sv = saved[l]
        w = sv["w"]
        g_mix, g_ffn, g_ple = (small[n][l][None, :] for n in ("norm_mix_g", "norm_ffn_g", "norm_ple_g"))
        gqa, gka, gqb = (_tile_gain(small[n][l], N_HEADS) for n in ("qnorm_a_g", "knorm_a_g", "qnorm_b_g"))
        gkb = _tile_gain(small["knorm_b_g"][l], N_KV_B)
        sink = jnp.repeat(small["sink_b"][l], BLK_B)[:, None]

        if l < DEPTH - 1:
            de, dz = _ew(_ple_bwd, [dx, sv["z"], sv["e"]], [BF16, BF16], "ple_bwd")
        gbig[l]["w_ple_proj"] = _mm(p[l], de, "tn", BF16, "mm_d_ple_proj")
        gbig[l]["w_ple_gate"] = _mm(sv["h3"], dz, "tn", BF16, "mm_d_ple_gate")
        dx, dxb, gsmall["norm_ple_g"][l] = _mm_rms_bwd([(dz, w["w_ple_gate"], "nt")], sv["x2"], g_ple, dx,
                                                      "mm_dh3_rms_bwd")

        gbig[l]["w_ffn_down"] = _mm(sv["hid"], dxb, "tn", BF16, "mm_d_ffn_down")

        def swiglu_bwd(products, extra):
            dh_, a_, u_ = products[0], extra[0].astype(F32), extra[1].astype(F32)
            s = _sigmoid(a_)
            return dh_ * u_ * (s * (1.0 + a_ * (1.0 - s))), dh_ * (a_ * s)

        da, du = _mm_fused([(dxb, w["w_ffn_down"], "nt")], [(sv["a"], 0), (sv["u"], 0)], swiglu_bwd, [BF16, BF16],
                           "mm_dhid_swiglu_bwd")
        gbig[l]["w_ffn_gate"] = _mm(da, sv["h2"], "tn", BF16, "mm_d_ffn_gate")
        gbig[l]["w_ffn_up"] = _mm(du, sv["h2"], "tn", BF16, "mm_d_ffn_up")
        dx, dxb, gsmall["norm_ffn_g"][l] = _mm_rms_bwd([(da, w["w_ffn_gate"], "nn"), (du, w["w_ffn_up"], "nn")],
                                                      sv["x1"], g_ffn, dx, "mm_dh2_rms_bwd")

        gbig[l]["w_out"] = _mm(sv["merged"], dxb, "tn", BF16, "mm_d_out")

        def gate_bwd(products, extra):
            dm_, ca_, cb_ = products[0], extra[0].astype(F32), extra[1].astype(F32)
            sa, sb = _sigmoid(extra[2]), _sigmoid(extra[3])
            return dm_ * sa, dm_ * sb, dm_ * ca_ * (sa * (1.0 - sa)), dm_ * cb_ * (sb * (1.0 - sb))

        dca, dcb, dga, dgb = _mm_fused(
            [(dxb, w["w_out"], "nt")], [(sv["ca"], 0), (sv["cb"], 0), (sv["proj"], OFF_GA), (sv["proj"], OFF_GB)],
            gate_bwd, [BF16, BF16, BF16, BF16], "mm_dmerged_gate_bwd")
        gbig[l]["w_branch_a"] = _mm(sv["ya"], dca, "tn", BF16, "mm_d_branch_a")
        gbig[l]["w_branch_b"] = _mm(sv["yb"], dcb, "tn", BF16, "mm_d_branch_b")
        def with_row_dots(products, extra):
            return products[0], _seg_sum(products[0] * extra[0])

        dya, delta_a = _mm_fused([(dca, w["w_branch_a"], "nt")], [(sv["ya"], 0)], with_row_dots, [F32, F32], "mm_dya")
        dyb, delta_b = _mm_fused([(dcb, w["w_branch_b"], "nt")], [(sv["yb"], 0)], with_row_dots, [F32, F32], "mm_dyb")

        dqa, dka, dva = [], [], []
        for (blk, d), bias, bk in zip(DILATED, bias_a, buckets_a):
            dq_, dk_, dv_, db_ = _attn_bwd(sv["qa"], sv["ka"], sv["proj"], dya, sv["lse_a"], delta_a, bias, None, blk, d,
                                           f"attn_a{d}_bwd", v_col=OFF_VA)
            dqa.append(dq_)
            dka.append(dk_)
            dva.append(dv_)
            dbias_a[len(dqa) - 1].append(db_)
        dqb, dkb, dvb, db_, dsink = _attn_bwd(sv["qb"], sv["kb"], sv["vb"], dyb, sv["lse_b"], delta_b, bias_b, sink,
                                              BLK_B, 1, "attn_b_bwd")
        dbias_b.append(db_)
        gsmall["sink_b"][l] = dsink.reshape(N_HEADS, BLK_B).sum(axis=1)

        dproj, pqa, pka, pqb, pkb = _qknorm_bwd(sv["proj"], gqa, gka, gqb, gkb, dqa, dka, dva, dqb, dkb, dvb, dga, dgb)
        marks[l]["attn_bwd_done"] = dproj
        gsmall["qnorm_a_g"][l] = pqa.reshape(N_HEADS, HEAD_DIM).sum(0)
        gsmall["knorm_a_g"][l] = pka.reshape(N_HEADS, HEAD_DIM).sum(0)
        gsmall["qnorm_b_g"][l] = pqb.reshape(N_HEADS, HEAD_DIM).sum(0)
        gsmall["knorm_b_g"][l] = pkb.reshape(N_KV_B, HEAD_DIM).sum(0)
        gbig[l]["w_in"] = _mm(dproj, sv["h"], "tn", BF16, "mm_d_in")
        dx, _, gsmall["norm_mix_g"][l] = _mm_rms_bwd([(dproj, w["w_in"], "nn")], sv["x0"], g_mix, dx, "mm_dh_rms_bwd")
        gsmall["norm_mix_g"][l] = gsmall["norm_mix_g"][l][0]
        gsmall["norm_ffn_g"][l] = gsmall["norm_ffn_g"][l][0]
        gsmall["norm_ple_g"][l] = gsmall["norm_ple_g"][l][0]

    gsmall = {n: jnp.stack(v) for n, v in gsmall.items()}
    dtable_a = sum(_table_grad(sum(dbs).reshape(N_HEADS, blk, 3 * blk), bk, "table_grad_a")
                   for dbs, (blk, _), bk in zip(dbias_a, DILATED, buckets_a))
    dtable_b = _table_grad(sum(dbias_b).reshape(N_HEADS, BLK_B, 3 * BLK_B), buckets_b, "table_grad_b")
    gsmall["rel_table"] = jnp.concatenate([dtable_a, dtable_b], axis=0).T
    return loss, dx, gbig, gsmall, marks


def _place():
    return lax.axis_index("x"), lax.axis_index("y"), lax.axis_index("c")


def _flip(v, bit):
    return 1 - v if bit else v


CHIP_RELATIONS = ((0, 1), (1, 0), (1, 1))
ANY = pl.BlockSpec(memory_space=pl.ANY)


def _allgather_body(w_refs, out_refs, send_sems, recv_sems):
    x, y, c = _place()
    chips = [(_flip(x, a), _flip(y, b)) for a, b in CHIP_RELATIONS]

    def make(g):
        w_ref, out_ref = w_refs[g], out_refs[g]
        half = w_ref.shape[0] // 2

        def part(px, py, pc):
            return out_ref.at[2 * px + py, pl.ds(pc * half, half), :]

        def copy(k, block, to, src=None):
            return pltpu.make_async_remote_copy(
                src_ref=part(*block) if src is None else src, dst_ref=part(*block),
                send_sem=send_sems.at[7 * g + k], recv_sem=recv_sems.at[7 * g + k], device_id=to,
                device_id_type=MESH_ID)

        own = pltpu.make_async_remote_copy(
            src_ref=w_ref, dst_ref=out_ref.at[2 * x + y], send_sem=send_sems.at[7 * g + 6],
            recv_sem=recv_sems.at[7 * g + 6], device_id=(x, y, 1 - c), device_id_type=MESH_ID)
        first = [copy(k, (x, y, c), (*chip, c), src=w_ref.at[pl.ds(c * half, half), :]) for k, chip in enumerate(chips)]
        passed = [copy(3 + k, (*chip, c), (x, y, 1 - c)) for k, chip in enumerate(chips)]
        arrive = [copy(k, (*chip, c), (x, y, c)) for k, chip in enumerate(chips)]
        arrive2 = [copy(3 + k, (*chip, 1 - c), (x, y, c)) for k, chip in enumerate(chips)]
        return own, first, passed, arrive, arrive2

    made = [make(g) for g in range(len(w_refs))]
    for own, first, _, _, _ in made:
        own.start()
        for cp in first:
            cp.start()
    for _, _, passed, arrive, _ in made:
        for k in range(3):
            arrive[k].wait_recv()
            passed[k].start()
    for own, first, passed, _, arrive2 in made:
        for k in range(3):
            arrive2[k].wait_recv()
        own.wait_recv()
        for cp in first + passed + [own]:
            cp.wait_send()


def _sibling(x, y, c):
    return [(x, y, 1 - c)]


def _same_core_of_other_chips(x, y, c):
    return [(_flip(x, a), _flip(y, b), c) for a, b in CHIP_RELATIONS]


def _exchange(body, ins, out_types, n_sems, name, sequencer=None):
    n = len(ins)
    sems = (pltpu.SemaphoreType.DMA((n_sems,)), pltpu.SemaphoreType.DMA((n_sems,)))
    if sequencer is None:
        in_place = out_types is None
        out_shape = [jax.ShapeDtypeStruct(a.shape, a.dtype) for a in ins] if in_place else out_types

        def tc_body(*refs):
            body(refs[:n], refs[n:n + len(out_shape)], refs[-2], refs[-1])

        return list(pl.pallas_call(
            tc_body, out_shape=out_shape, in_specs=[ANY] * n, out_specs=[ANY] * len(out_shape),
            input_output_aliases={g: g for g in range(n)} if in_place else {}, scratch_shapes=list(sems), name=name)(*ins))

    collective_id, peers = sequencer
    hbm = pltpu.MemorySpace.HBM
    in_refs = [jax.new_ref(a, memory_space=hbm) for a in ins]
    out_refs = in_refs if out_types is None else [jax.empty_ref(t, memory_space=hbm) for t in out_types]

    @pl.kernel(mesh=plsc.ScalarSubcoreMesh(axis_name="sequencer", num_cores=1), name=name, scratch_types=sems,
               compiler_params=pltpu.CompilerParams(collective_id=collective_id))
    def launch(send_sems, recv_sems):
        barrier = pltpu.get_barrier_semaphore()
        devices = peers(*_place())
        for device in devices:
            pl.semaphore_signal(barrier, inc=1, device_id=device, device_id_type=MESH_ID)
        pl.semaphore_wait(barrier, len(devices))
        body(in_refs, out_refs, send_sems, recv_sems)

    launch()
    return [r[...] for r in out_refs]


def _allgather(shards, name, sequencer=None):
    out_types = [jax.ShapeDtypeStruct((N_CHIPS,) + s.shape, s.dtype) for s in shards]
    if sequencer is not None:
        sequencer = (sequencer, lambda x, y, c: _sibling(x, y, c) + _same_core_of_other_chips(x, y, c))
    return _exchange(_allgather_body, shards, out_types, 7 * len(shards), name, sequencer)


def _half_tile(half):
    return max(t for t in range(16, 1025, 16) if half % t == 0)


def _run_copies(cps):
    for cp in cps:
        cp.start()
    for cp in cps:
        cp.wait_recv()
    for cp in cps:
        cp.wait_send()


def _sibling_halves(gsends, name, sequencer=None):
    def body(g_refs, out_refs, send_sems, recv_sems):
        x, y, c = _place()
        cps = []
        for g, (g_ref, out_ref) in enumerate(zip(g_refs, out_refs)):
            half = g_ref.shape[1] // 2
            cps.append(pltpu.make_async_remote_copy(
                src_ref=g_ref.at[:, pl.ds((1 - c) * half, half), :], dst_ref=out_ref,
                send_sem=send_sems.at[g], recv_sem=recv_sems.at[g], device_id=(x, y, 1 - c), device_id_type=MESH_ID))
        _run_copies(cps)

    out_types = [jax.ShapeDtypeStruct((s.shape[0], s.shape[1] // 2, s.shape[2]), s.dtype) for s in gsends]
    return _exchange(body, gsends, out_types, len(gsends), name, sequencer and (sequencer, _sibling))


def _chip_sums(gsend, sib, place):
    n, rows, cols = gsend.shape
    half = rows // 2
    tm = _half_tile(half)
    nblk = half // tm

    def body(s_ref, g_ref, sib_ref, o_ref):
        o_ref[0] = (g_ref[0].astype(F32) + sib_ref[0].astype(F32)).astype(o_ref.dtype)

    grid_spec = pltpu.PrefetchScalarGridSpec(
        num_scalar_prefetch=1, grid=(n, nblk),
        in_specs=[pl.BlockSpec((1, tm, cols), lambda k, i, s: (jnp.bitwise_xor(s[0], k), s[1] * nblk + i, 0)),
                  pl.BlockSpec((1, tm, cols), lambda k, i, s: (jnp.bitwise_xor(s[0], k), i, 0))],
        out_specs=pl.BlockSpec((1, tm, cols), lambda k, i, s: (k, i, 0)))
    return pl.pallas_call(
        body, out_shape=jax.ShapeDtypeStruct((n, half, cols), BF16), grid_spec=grid_spec,
        name="rs_chip_sums", compiler_params=_params("parallel", "parallel"))(place, gsend, sib)


def _exchange_chip_sums(tsends, name, sequencer=None):
    def body(t_refs, out_refs, send_sems, recv_sems):
        x, y, c = _place()
        cps = []
        for g, (t_ref, out_ref) in enumerate(zip(t_refs, out_refs)):
            for k, device in enumerate(_same_core_of_other_chips(x, y, c)):
                cps.append(pltpu.make_async_remote_copy(
                    src_ref=t_ref.at[k + 1], dst_ref=out_ref.at[k], send_sem=send_sems.at[3 * g + k],
                    recv_sem=recv_sems.at[3 * g + k], device_id=device, device_id_type=MESH_ID))
        _run_copies(cps)

    out_types = [jax.ShapeDtypeStruct((3,) + s.shape[1:], s.dtype) for s in tsends]
    return _exchange(body, tsends, out_types, 3 * len(tsends), name,
                     sequencer and (sequencer, _same_core_of_other_chips))


def _final_sum(tsend, recv, place):
    n, half, cols = tsend.shape
    tm = _half_tile(half)
    nblk = half // tm

    def body(s_ref, t_ref, r_ref, o_ref):
        o_ref[...] = ((t_ref[0].astype(F32) + r_ref[0].astype(F32)) + r_ref[1].astype(F32)) + r_ref[2].astype(F32)

    grid_spec = pltpu.PrefetchScalarGridSpec(
        num_scalar_prefetch=1, grid=(nblk,),
        in_specs=[pl.BlockSpec((1, tm, cols), lambda i, s: (0, i, 0)), pl.BlockSpec((n - 1, tm, cols), lambda i, s: (0, i, 0))],
        out_specs=pl.BlockSpec((tm, cols), lambda i, s: (s[1] * nblk + i, 0)))
    return pl.pallas_call(
        body, out_shape=jax.ShapeDtypeStruct((2 * half, cols), F32), grid_spec=grid_spec, name="rs_final_sum",
        compiler_params=_params("parallel"))(place, tsend, recv)


def _join_halves(gfulls, name, sequencer=None):
    def body(g_refs, out_refs, send_sems, recv_sems):
        x, y, c = _place()
        n = len(g_refs)

        def copy(g, pc):
            half = g_refs[g].shape[0] // 2
            return pltpu.make_async_remote_copy(
                src_ref=g_refs[g].at[pl.ds(pc * half, half), :], dst_ref=out_refs[g].at[pl.ds(pc * half, half), :],
                send_sem=send_sems.at[g], recv_sem=recv_sems.at[g], device_id=(x, y, 1 - c), device_id_type=MESH_ID)

        mine = [copy(g, c) for g in range(n)]
        for cp in mine:
            cp.start()
        for g in range(n):
            copy(g, 1 - c).wait_recv()
        for cp in mine:
            cp.wait_send()

    return _exchange(body, gfulls, None, len(gfulls), name, sequencer and (sequencer, _sibling))


def _allreduce_small(v):
    rows, cols = v.shape

    def body(v_ref, out_ref, buf, send_sems, recv_sems):
        x, y, c = _place()
        cps = []
        for k in range(1, 8):
            peer = (_flip(x, (k >> 2) & 1), _flip(y, (k >> 1) & 1), _flip(c, k & 1))
            cps.append(pltpu.make_async_remote_copy(
                src_ref=v_ref, dst_ref=buf.at[k - 1], send_sem=send_sems.at[k - 1], recv_sem=recv_sems.at[k - 1],
                device_id=peer, device_id_type=MESH_ID))
        for cp in cps:
            cp.start()
        for cp in cps:
            cp.wait_recv()
        for cp in cps:
            cp.wait_send()
        t0 = v_ref[...] + buf[0]
        t1 = buf[1] + buf[2]
        t2 = buf[3] + buf[4]
        t3 = buf[5] + buf[6]
        out_ref[...] = (t0 + t1) + (t2 + t3)

    vm = pl.BlockSpec(memory_space=pltpu.VMEM)
    return pl.pallas_call(
        body, out_shape=jax.ShapeDtypeStruct((rows, cols), F32), in_specs=[vm], out_specs=vm,
        scratch_shapes=[pltpu.VMEM((7, rows, cols), F32), pltpu.SemaphoreType.DMA((7,)), pltpu.SemaphoreType.DMA((7,))],
        name="allreduce_small")(v)


BIG_INFO = {n: (shape, ax) for n, shape, ax in BIG}
GROUPS = (("w_in",), ("w_ffn_gate", "w_ffn_up", "w_ffn_down", "w_out", "w_ple_gate"),
          ("w_branch_a", "w_branch_b", "w_ple_proj"))


def _shard_shape(name):
    (k, m), ax = BIG_INFO[name]
    return (k // N_CHIPS, m) if ax == 0 else (k, m // N_CHIPS)


def _group_rows(group):
    offs, off = {}, 0
    for n in group:
        offs[n] = off
        off += _shard_shape(n)[0]
    return offs, off


def _pack_groups(shards, layer, dtype):
    return [jnp.concatenate([shards[n][layer].astype(dtype) for n in group], axis=0) for group in GROUPS]


def _unpack_full(gathered, groups):
    out = {}
    for group, arr in zip(groups, gathered):
        offs, _ = _group_rows(group)
        for n in group:
            rows, cols = _shard_shape(n)
            (k, m), ax = BIG_INFO[n]
            slab = arr[:, offs[n]:offs[n] + rows]
            out[n] = slab.reshape(k, m) if ax == 0 else jnp.transpose(slab, (1, 0, 2)).reshape(k, m)
    return out


def _pack_grads(gfull):
    out = []
    for group in GROUPS:
        parts = []
        for n in group:
            rows, cols = _shard_shape(n)
            ax = BIG_INFO[n][1]
            slab = (gfull[n].reshape(N_CHIPS, rows, cols) if ax == 0
                    else jnp.transpose(gfull[n].reshape(rows, N_CHIPS, cols), (1, 0, 2)))
            parts.append(slab)
        out.append(jnp.concatenate(parts, axis=1))
    return out


def _after(values, mark):
    values, _ = lax.optimization_barrier((values, mark))
    return values


def _reduce_scatter_begin(gsends, place, tag, ids):
    sibs = _sibling_halves(gsends, "rs_sibling_halves_" + tag, ids[0])
    tsends = [_chip_sums(g, s, place) for g, s in zip(gsends, sibs)]
    return tsends, _exchange_chip_sums(tsends, "rs_exchange_" + tag, ids[1])


def _reduce_scatter_finish(begun, place, tag, ids, hold):
    tsends, recvs = begun
    recvs = _after(recvs, hold)
    return _join_halves([_final_sum(t, r, place) for t, r in zip(tsends, recvs)], "rs_join_halves_" + tag, ids[2])


SMALL_SHAPES = {"rel_table": (NUM_BUCKETS, 2 * N_HEADS), "norm_mix_g": (DEPTH, D_MODEL), "qnorm_a_g": (DEPTH, HEAD_DIM),
                "knorm_a_g": (DEPTH, HEAD_DIM), "qnorm_b_g": (DEPTH, HEAD_DIM), "knorm_b_g": (DEPTH, HEAD_DIM),
                "sink_b": (DEPTH, N_HEADS), "norm_ffn_g": (DEPTH, D_MODEL), "norm_ple_g": (DEPTH, D_MODEL)}


def _pack_small(vals, last=None):
    flat = jnp.concatenate([vals[n].astype(F32).reshape(-1) for n in SMALL])
    tail = jnp.zeros((SMALL_ROWS * LANES - flat.shape[0],), F32)
    if last is not None:
        tail = tail.at[-1].set(last)
    return jnp.concatenate([flat, tail]).reshape(SMALL_ROWS, LANES)


def _unpack_small(packed):
    flat, out, off = packed.reshape(-1), {}, 0
    for n in SMALL:
        size = math.prod(SMALL_SHAPES[n])
        out[n] = flat[off:off + size].reshape(SMALL_SHAPES[n])
        off += size
    return out


def _adamw(w, gs, g_row, m, v, name):
    c1 = 1.0 - ADAM_B1 ** ADAM_STEP
    c2 = 1.0 - ADAM_B2 ** ADAM_STEP
    total, width = w.shape
    n_layers = len(gs)
    per = total // n_layers
    tm = max(t for t in range(8, 513, 8) if per % t == 0 and g_row % t == 0)
    nblk = per // tm

    def body(*refs):
        w_ref, g_refs = refs[0], refs[1:1 + n_layers]
        m_ref, v_ref, og, od, om, ov = refs[1 + n_layers:]
        layer = pl.program_id(0) // nblk
        g = g_refs[0][...]
        for l in range(1, n_layers):
            g = jnp.where(layer == l, g_refs[l][...], g)
        m_new = ADAM_B1 * m_ref[...] + (1.0 - ADAM_B1) * g
        v_new = ADAM_B2 * v_ref[...] + (1.0 - ADAM_B2) * (g * g)
        og[...] = g
        od[...] = -ADAM_LR * ((m_new / c1) / (jnp.sqrt(v_new / c2) + ADAM_EPS) + ADAM_WD * w_ref[...])
        om[...] = m_new
        ov[...] = v_new

    row = pl.BlockSpec((tm, width), lambda i: (i, 0))
    g_specs = [pl.BlockSpec((tm, width), lambda i, l=l: (g_row // tm + jnp.clip(i - l * nblk, 0, nblk - 1), 0))
               for l in range(n_layers)]
    return pl.pallas_call(
        body, out_shape=[jax.ShapeDtypeStruct((total, width), F32)] * 4, grid=(total // tm,),
        in_specs=[row] + g_specs + [row, row], out_specs=[row] * 4, name=name,
        compiler_params=_params("parallel"))(w, *gs, m, v)


def kernel(x, p, rel_table, norm_mix_g, w_in, qnorm_a_g, knorm_a_g, qnorm_b_g, knorm_b_g, sink_b, w_branch_a, w_branch_b, w_out, norm_ffn_g, w_ffn_gate, w_ffn_up, w_ffn_down, norm_ple_g, w_ple_gate, w_ple_proj, loss_target, m_rel_table, m_norm_mix_g, m_w_in, m_qnorm_a_g, m_knorm_a_g, m_qnorm_b_g, m_knorm_b_g, m_sink_b, m_w_branch_a, m_w_branch_b, m_w_out, m_norm_ffn_g, m_w_ffn_gate, m_w_ffn_up, m_w_ffn_down, m_norm_ple_g, m_w_ple_gate, m_w_ple_proj, v_rel_table, v_norm_mix_g, v_w_in, v_qnorm_a_g, v_knorm_a_g, v_qnorm_b_g, v_knorm_b_g, v_sink_b, v_w_branch_a, v_w_branch_b, v_w_out, v_norm_ffn_g, v_w_ffn_gate, v_w_ffn_up, v_w_ffn_down, v_norm_ple_g, v_w_ple_gate, v_w_ple_proj):
    given = dict(locals())

    def held(name, a):
        return jnp.swapaxes(a, 1, 2) if name in TRANSPOSED else a

    weights = {n: held(n, given[n]) for n in WEIGHTS}
    moments_m = {n: held(n, given["m_" + n]) for n in WEIGHTS}
    moments_v = {n: held(n, given["v_" + n]) for n in WEIGHTS}
    xi, yi, ci = _place()
    place = jnp.stack([2 * xi + yi, ci]).astype(jnp.int32)

    shards = [_pack_groups(weights, l, BF16) for l in range(DEPTH)]
    w_in0 = _allgather(shards[0][:1], "allgather_w_in_layer0", sequencer=9)
    rest0 = _allgather(_after(shards[0][1:], w_in0), "allgather_rest_layer0", sequencer=1)
    gathered = [w_in0 + rest0, None]
    small = {n: weights[n] for n in SMALL}

    def w_in_of(l, mark):
        return _unpack_full(_after(gathered[l][:1], (shards[1], mark) if l == 0 else mark), GROUPS[:1])["w_in"]

    def rest_of(l, mark):
        if l == 0:
            gathered[1] = _allgather(_after(shards[1], mark), "allgather_layer1", sequencer=2)
        return _unpack_full(_after(gathered[l][1:], mark), GROUPS[1:])

    loss, dx, gbig, gsmall, marks = _local_step(x[0], p[:, 0], loss_target[0], w_in_of, rest_of, small)

    gsends = [_pack_grads(gbig[l]) for l in range(DEPTH)]
    stages = {"layer1": (gsends[1], (3, 4, 5)), "rest_layer0": (gsends[0][1:], (6, 7, 8)),
              "w_in_layer0": (gsends[0][:1], (10, 11, 12))}
    begun = {tag: _reduce_scatter_begin(g, place, tag, ids) for tag, (g, ids) in stages.items()}

    def finish(tag, hold):
        return _reduce_scatter_finish(begun[tag], place, tag, stages[tag][1], hold)

    red1 = finish("layer1", marks[0]["attn_bwd_done"])
    rest0 = finish("rest_layer0", gbig[0]["w_in"])

    grads, delta, new_m, new_v = {}, {}, {}, {}

    def update(group, reduced):
        offs, _ = _group_rows(group)
        for n in group:
            shape = weights[n].shape
            two_d = lambda a: a.reshape(shape[0] * shape[1], shape[2])
            outs = _adamw(two_d(weights[n]), reduced, offs[n], two_d(moments_m[n]), two_d(moments_v[n]), "adamw_" + n)
            grads[n], delta[n], new_m[n], new_v[n] = (held(n, o.reshape(shape)) for o in outs)

    for gi in (1, 2):
        update(GROUPS[gi], _after([rest0[gi - 1], red1[gi]], begun["w_in_layer0"][0]))
    small_grads = _allreduce_small(_pack_small(gsmall, last=loss))
    g_, d_, m_, v_ = _adamw(_pack_small(weights), [small_grads], 0, _pack_small(moments_m), _pack_small(moments_v),
                            "adamw_small")
    grads.update(_unpack_small(g_))
    delta.update(_unpack_small(d_))
    new_m.update(_unpack_small(m_))
    new_v.update(_unpack_small(v_))
    others_done = [dx, d_] + [delta[n] for gi in (1, 2) for n in GROUPS[gi]]
    update(GROUPS[0], [finish("w_in_layer0", others_done)[0], red1[0]])

    return (small_grads[-1, -1], dx[None], *[grads[n] for n in WEIGHTS], *[delta[n] for n in WEIGHTS],
            *[new_m[n] for n in WEIGHTS], *[new_v[n] for n in WEIGHTS])
```

```python
import functools
import math

import jax
import jax.numpy as jnp
from jax import lax
from jax.experimental import pallas as pl
from jax.experimental.pallas import tpu as pltpu
from jax.experimental.pallas import tpu_sc as plsc

F32 = jnp.float32
BF16 = jnp.bfloat16
MESH_ID = pl.DeviceIdType.MESH

D_MODEL = 1024
DEPTH = 2
HEAD_DIM = 64
N_HEADS = 8
WIDTH = N_HEADS * HEAD_DIM
N_PAIRS = 4
ITEMS = 16
MAX_CHUNK = 1024
N_KV_B = 2
PLE_DIM = 256
D_FF = 2816
D_IN = 4352
OFF_QA, OFF_KA, OFF_VA, OFF_QB, OFF_KB, OFF_VB, OFF_GA, OFF_GB = 0, 512, 1024, 1536, 2048, 2176, 2304, 3328
DILATED = ((64, 1), (64, 4), (64, 16))
BLK_B = 128
NUM_BUCKETS = 32
MAX_DISTANCE = 1024
RMS_EPS = 1e-6
NEG_INF = -1e30
LANES = 128
ROW_TILE = 256
VMEM_LIMIT = 48 * 1024 * 1024

ADAM_LR, ADAM_B1, ADAM_B2, ADAM_EPS, ADAM_WD, ADAM_STEP = 0.001, 0.9, 0.999, 1e-08, 0.01, 10

TRANSPOSED = ("w_in", "w_ffn_gate", "w_ffn_up")
BIG = (
    ("w_in", (D_IN, D_MODEL), 0),
    ("w_branch_a", (WIDTH, D_MODEL), 1),
    ("w_branch_b", (WIDTH, D_MODEL), 1),
    ("w_out", (D_MODEL, D_MODEL), 0),
    ("w_ffn_gate", (D_FF, D_MODEL), 0),
    ("w_ffn_up", (D_FF, D_MODEL), 0),
    ("w_ffn_down", (D_FF, D_MODEL), 0),
    ("w_ple_gate", (D_MODEL, D_MODEL), 0),
    ("w_ple_proj", (PLE_DIM, D_MODEL), 1),
)
SMALL = ("rel_table", "norm_mix_g", "qnorm_a_g", "knorm_a_g", "qnorm_b_g", "knorm_b_g", "sink_b",
         "norm_ffn_g", "norm_ple_g")
WEIGHTS = ("rel_table", "norm_mix_g", "w_in", "qnorm_a_g", "knorm_a_g", "qnorm_b_g", "knorm_b_g", "sink_b",
           "w_branch_a", "w_branch_b", "w_out", "norm_ffn_g", "w_ffn_gate", "w_ffn_up", "w_ffn_down",
           "norm_ple_g", "w_ple_gate", "w_ple_proj")
N_CHIPS = 4
SMALL_ROWS = 64


def _params(*sem):
    return pltpu.CompilerParams(dimension_semantics=sem, vmem_limit_bytes=VMEM_LIMIT)


MM_VMEM_BUDGET = 40 * 1024 * 1024
STEP_OVERHEAD_S = 0.4e-6
TILE_DMA_BYTES_PER_S = 1.5e12


def _mm_dims(a, b, mode):
    if mode == "nn":
        return a.shape[0], b.shape[1], a.shape[1]
    if mode == "nt":
        return a.shape[0], b.shape[0], a.shape[1]
    return a.shape[1], b.shape[1], a.shape[0]


def _mm_tiles(m, n, pairs, tile_bytes, col_offsets, full_rows=False):
    best = None
    widths = [n] if full_rows else [t for t in range(LANES, n + 1, LANES) if n % t == 0 and all(o % t == 0 for o in col_offsets)]
    for tm in (t for t in range(LANES, m + 1, LANES) if m % t == 0):
        for tn in widths:
            io = sum(tm * k * ab + tn * k * bb for k, ab, bb in pairs) + tm * tn * sum(tile_bytes)
            casts = sum((tm * k * 2 if ab == 4 else 0) + (tn * k * 2 if bb == 4 else 0) for k, ab, bb in pairs)
            if 2 * io + len(pairs) * tm * tn * 4 + casts > MM_VMEM_BUDGET:
                continue
            cost = (m // tm) * (n // tn) * STEP_OVERHEAD_S + io / TILE_DMA_BYTES_PER_S
            if best is None or (cost, -tm) < best[0]:
                best = ((cost, -tm), tm, tn)
    return best[1], best[2]


def _mm_fused(pairs, extras, epilogue, out_dtypes, name, next_gain=None):
    m, n, _ = _mm_dims(*pairs[0])
    assert all(_mm_dims(*p)[:2] == (m, n) for p in pairs)
    with_norm = next_gain is not None
    out_dtypes = list(out_dtypes) + ([BF16] if with_norm else [])
    tm, tn = _mm_tiles(
        m, n, [(_mm_dims(a, b, mode)[2], a.dtype.itemsize, b.dtype.itemsize) for a, b, mode in pairs],
        [e.dtype.itemsize for e, _ in extras] + [jnp.dtype(d).itemsize for d in out_dtypes], [off for _, off in extras],
        full_rows=with_norm)
    dims = {"nn": (((1,), (0,)), ((), ())), "nt": (((1,), (1,)), ((), ())), "tn": (((0,), (0,)), ((), ()))}
    in_specs, args = [], []
    for a, b, mode in pairs:
        k = _mm_dims(a, b, mode)[2]
        in_specs.append(pl.BlockSpec((k, tm), lambda i, j: (0, i)) if mode == "tn" else pl.BlockSpec((tm, k), lambda i, j: (i, 0)))
        in_specs.append(pl.BlockSpec((tn, k), lambda i, j: (j, 0)) if mode == "nt" else pl.BlockSpec((k, tn), lambda i, j: (0, j)))
        args += [a, b]
    for e, off in extras:
        in_specs.append(pl.BlockSpec((tm, tn), lambda i, j, o=off // tn: (i, o + j)))
        args.append(e)
    n_pairs, n_tiles = len(pairs), 2 * len(pairs) + len(extras)
    if with_norm:
        in_specs.append(pl.BlockSpec((1, n), lambda i, j: (0, 0)))
        args.append(next_gain)
    n_in = len(args)

    def body(*refs):
        products = [lax.dot_general(refs[2 * p][...].astype(BF16), refs[2 * p + 1][...].astype(BF16), dims[pairs[p][2]],
                                    preferred_element_type=F32) for p in range(n_pairs)]
        outs = list(epilogue(products, [r[...] for r in refs[2 * n_pairs:n_tiles]]))
        if with_norm:
            y = outs[0]
            outs.append((y * lax.rsqrt(jnp.mean(y * y, axis=-1, keepdims=True) + RMS_EPS)) * refs[n_tiles][...])
        for r, o in zip(refs[n_in:], outs):
            r[...] = o.astype(r.dtype)

    tile = pl.BlockSpec((tm, tn), lambda i, j: (i, j))
    return pl.pallas_call(
        body, out_shape=[jax.ShapeDtypeStruct((m, n), d) for d in out_dtypes], grid=(m // tm, n // tn),
        in_specs=in_specs, out_specs=[tile] * len(out_dtypes), name=name,
        compiler_params=_params("parallel", "parallel"))(*args)


def _mm(a, b, mode, out_dtype, name, res=None):
    if res is None:
        return _mm_fused([(a, b, mode)], [], lambda products, extra: products, [out_dtype], name)[0]
    return _mm_fused([(a, b, mode)], [(res, 0)], lambda products, extra: [products[0] + extra[0]], [out_dtype], name)[0]


def _mm_res_norm(a, b, mode, res, gain, name):
    return _mm_fused([(a, b, mode)], [(res, 0)], lambda products, extra: [products[0] + extra[0]], [F32], name,
                     next_gain=gain)


def _ew(fn, ins, out_dtypes, name):
    rows, width = ins[0].shape
    n_in = len(ins)

    def body(*refs):
        outs = fn(*[r[...] for r in refs[:n_in]])
        for r, o in zip(refs[n_in:], outs):
            r[...] = o.astype(r.dtype)

    row = pl.BlockSpec((ROW_TILE, width), lambda i: (i, 0))
    return pl.pallas_call(
        body, out_shape=[jax.ShapeDtypeStruct((rows, width), dt) for dt in out_dtypes], grid=(rows // ROW_TILE,),
        in_specs=[row] * n_in, out_specs=[row] * len(out_dtypes), name=name, compiler_params=_params("parallel"))(*ins)


def _sigmoid(x):
    return 1.0 / (1.0 + jnp.exp(-x))


def _seg_sum(v):
    outs = []
    for k in range(v.shape[1] // LANES):
        vp = v[:, k * LANES:(k + 1) * LANES]
        left = lax.broadcasted_iota(jnp.int32, vp.shape, 1) < HEAD_DIM
        sl = jnp.sum(jnp.where(left, vp, 0.0), axis=-1, keepdims=True)
        sr = jnp.sum(jnp.where(left, 0.0, vp), axis=-1, keepdims=True)
        outs.append(jnp.where(left, sl, sr))
    return outs[0] if len(outs) == 1 else jnp.concatenate(outs, axis=1)


def _seg_rstd(x):
    return lax.rsqrt(_seg_sum(x * x) * (1.0 / HEAD_DIM) + RMS_EPS)


def _rms_fwd(x, g, name):
    rows, d = x.shape
    tm = ROW_TILE

    def body(x_ref, g_ref, h_ref):
        xv = x_ref[...]
        r = lax.rsqrt(jnp.mean(xv * xv, axis=-1, keepdims=True) + RMS_EPS)
        h_ref[...] = ((xv * r) * g_ref[...]).astype(BF16)

    return pl.pallas_call(
        body, out_shape=jax.ShapeDtypeStruct((rows, d), BF16), grid=(rows // tm,),
        in_specs=[pl.BlockSpec((tm, d), lambda i: (i, 0)), pl.BlockSpec((1, d), lambda i: (0, 0))],
        out_specs=pl.BlockSpec((tm, d), lambda i: (i, 0)), name=name,
        compiler_params=_params("parallel"))(x, g)


def _mm_rms_bwd(pairs, x, g, dres, name):
    rows, d = x.shape
    assert all(_mm_dims(*p)[:2] == (rows, d) for p in pairs)
    kbytes = [(_mm_dims(a, b, mode)[2], a.dtype.itemsize, b.dtype.itemsize) for a, b, mode in pairs]
    tm = max(t for t in (512, 256, 128) if rows % t == 0 and
             2 * (sum(t * k * ab + d * k * bb for k, ab, bb in kbytes) + t * d * 14) + 2 * t * d * 4 <= MM_VMEM_BUDGET)
    dims = {"nn": (((1,), (0,)), ((), ())), "nt": (((1,), (1,)), ((), ()))}
    n_pairs = len(pairs)

    def body(*refs):
        x_ref, g_ref, dres_ref = refs[2 * n_pairs:2 * n_pairs + 3]
        dx_ref, dxb_ref, dg_ref = refs[2 * n_pairs + 3:]
        dhv = functools.reduce(lambda u, v: u + v, [
            lax.dot_general(refs[2 * p][...].astype(BF16), refs[2 * p + 1][...].astype(BF16), dims[pairs[p][2]],
                            preferred_element_type=F32) for p in range(n_pairs)])
        xv = x_ref[...]
        r = lax.rsqrt(jnp.mean(xv * xv, axis=-1, keepdims=True) + RMS_EPS)
        xh = xv * r
        dxh = dhv * g_ref[...]
        dxv = dres_ref[...] + r * (dxh - xh * jnp.mean(dxh * xh, axis=-1, keepdims=True))
        dx_ref[...] = dxv
        dxb_ref[...] = dxv.astype(BF16)
        part = jnp.sum(dhv * xh, axis=0, keepdims=True)

        @pl.when(pl.program_id(0) == 0)
        def _():
            dg_ref[...] = part

        @pl.when(pl.program_id(0) > 0)
        def _():
            dg_ref[...] += part

    row = pl.BlockSpec((tm, d), lambda i: (i, 0))
    vec = pl.BlockSpec((1, d), lambda i: (0, 0))
    in_specs, args = [], []
    for a, b, mode in pairs:
        in_specs += [pl.BlockSpec((tm, a.shape[1]), lambda i: (i, 0)), pl.BlockSpec(b.shape, lambda i: (0, 0))]
        args += [a, b]
    return pl.pallas_call(
        body, out_shape=[jax.ShapeDtypeStruct((rows, d), F32), jax.ShapeDtypeStruct((rows, d), BF16),
                         jax.ShapeDtypeStruct((1, d), F32)],
        grid=(rows // tm,), in_specs=in_specs + [row, vec, row], out_specs=[row, row, vec],
        name=name, compiler_params=_params("arbitrary"))(*args, x, g, dres)


def _ple_bwd(dx, z, e):
    s = _sigmoid(z.astype(F32))
    return dx * s, dx * e.astype(F32) * (s * (1.0 - s))


def _loss_grad(y, t, z, e):
    rows, d = y.shape
    tm = ROW_TILE

    def body(y_ref, t_ref, z_ref, e_ref, dy_ref, de_ref, dz_ref, l_ref):
        err = y_ref[...] - t_ref[...]
        dy = err * (1.0 / d)
        dy_ref[...] = dy
        de, dz = _ple_bwd(dy, z_ref[...], e_ref[...])
        de_ref[...] = de.astype(BF16)
        dz_ref[...] = dz.astype(BF16)
        part = jnp.zeros((1, LANES), F32) + jnp.sum(err * err) * (0.5 / d)

        @pl.when(pl.program_id(0) == 0)
        def _():
            l_ref[...] = part

        @pl.when(pl.program_id(0) > 0)
        def _():
            l_ref[...] += part

    row = pl.BlockSpec((tm, d), lambda i: (i, 0))
    return pl.pallas_call(
        body, out_shape=[jax.ShapeDtypeStruct((rows, d), F32), jax.ShapeDtypeStruct((rows, d), BF16),
                         jax.ShapeDtypeStruct((rows, d), BF16), jax.ShapeDtypeStruct((1, LANES), F32)],
        grid=(rows // tm,), in_specs=[row] * 4, out_specs=[row, row, row, pl.BlockSpec((1, LANES), lambda i: (0, 0))],
        name="loss_grad", compiler_params=_params("arbitrary"))(y, t, z, e)


def _swap_halves(v):
    return pltpu.roll(v, HEAD_DIM, axis=1)


def _expand_kv(kv):
    left = lax.broadcasted_iota(jnp.int32, kv.shape, 1) < HEAD_DIM
    sw = _swap_halves(kv)
    h0 = jnp.where(left, kv, sw)
    h1 = jnp.where(left, sw, kv)
    return jnp.concatenate([h0, h0, h1, h1], axis=1)


def _reduce_kv(dkv):
    left = lax.broadcasted_iota(jnp.int32, (dkv.shape[0], LANES), 1) < HEAD_DIM
    t = dkv[:, 0:LANES] + dkv[:, LANES:2 * LANES]
    u = dkv[:, 2 * LANES:3 * LANES] + dkv[:, 3 * LANES:4 * LANES]
    t = t + _swap_halves(t)
    u = u + _swap_halves(u)
    return jnp.where(left, t, u)


def _qknorm_fwd(proj, gqa, gka, gqb, gkb):
    rows = proj.shape[0]
    tm = ROW_TILE

    def body(qa_ref, ka_ref, qb_ref, kb_ref, vb_ref, gqa_ref, gka_ref, gqb_ref, gkb_ref, oqa, oka, oqb, okb, ovb):
        for src, g_ref, dst in ((qa_ref, gqa_ref, oqa), (ka_ref, gka_ref, oka), (qb_ref, gqb_ref, oqb)):
            xv = src[...]
            dst[...] = (xv * _seg_rstd(xv)) * g_ref[...]
        kv = kb_ref[...]
        okb[...] = _expand_kv((kv * _seg_rstd(kv)) * gkb_ref[...])
        ovb[...] = _expand_kv(vb_ref[...])

    def win(width, off):
        return pl.BlockSpec((tm, width), lambda i: (i, off // width))

    vec = lambda w: pl.BlockSpec((1, w), lambda i: (0, 0))
    out = pl.BlockSpec((tm, WIDTH), lambda i: (i, 0))
    return pl.pallas_call(
        body, out_shape=[jax.ShapeDtypeStruct((rows, WIDTH), F32)] * 5, grid=(rows // tm,),
        in_specs=[win(WIDTH, OFF_QA), win(WIDTH, OFF_KA), win(WIDTH, OFF_QB), win(LANES, OFF_KB), win(LANES, OFF_VB),
                  vec(WIDTH), vec(WIDTH), vec(WIDTH), vec(LANES)],
        out_specs=[out] * 5, name="qknorm_fwd", compiler_params=_params("parallel"))(
            proj, proj, proj, proj, proj, gqa, gka, gqb, gkb)


def _norm_bwd(xv, g, dy):
    r = _seg_rstd(xv)
    xh = xv * r
    dxh = dy * g
    dx = r * (dxh - xh * (_seg_sum(dxh * xh) * (1.0 / HEAD_DIM)))
    return dx, jnp.sum(dy * xh, axis=0, keepdims=True)


def _qknorm_bwd(proj, gqa, gka, gqb, gkb, dqa, dka, dva, dqb, dkb, dvb, dga, dgb):
    rows = proj.shape[0]
    tm = ROW_TILE
    n_a = len(dqa)

    def body(*refs):
        qa_ref, ka_ref, qb_ref, kb_ref, gqa_ref, gka_ref, gqb_ref, gkb_ref = refs[:8]
        pos = 8
        dqa_refs, dka_refs, dva_refs = refs[pos:pos + n_a], refs[pos + n_a:pos + 2 * n_a], refs[pos + 2 * n_a:pos + 3 * n_a]
        pos += 3 * n_a
        dqb_ref, dkb_ref, dvb_ref, dga_ref, dgb_ref = refs[pos:pos + 5]
        dproj_ref, ogqa, ogka, ogqb, ogkb = refs[pos + 5:]

        def total(rs):
            acc = rs[0][...]
            for r in rs[1:]:
                acc = acc + r[...]
            return acc

        dx_qa, p_qa = _norm_bwd(qa_ref[...], gqa_ref[...], total(dqa_refs))
        dx_ka, p_ka = _norm_bwd(ka_ref[...], gka_ref[...], total(dka_refs))
        dx_qb, p_qb = _norm_bwd(qb_ref[...], gqb_ref[...], dqb_ref[...])
        dx_kb, p_kb = _norm_bwd(kb_ref[...], gkb_ref[...], _reduce_kv(dkb_ref[...]))
        dproj_ref[:, OFF_QA:OFF_QA + WIDTH] = dx_qa.astype(BF16)
        dproj_ref[:, OFF_KA:OFF_KA + WIDTH] = dx_ka.astype(BF16)
        dproj_ref[:, OFF_VA:OFF_VA + WIDTH] = total(dva_refs).astype(BF16)
        dproj_ref[:, OFF_QB:OFF_QB + WIDTH] = dx_qb.astype(BF16)
        dproj_ref[:, OFF_KB:OFF_KB + LANES] = dx_kb.astype(BF16)
        dproj_ref[:, OFF_VB:OFF_VB + LANES] = _reduce_kv(dvb_ref[...]).astype(BF16)
        dproj_ref[:, OFF_GA:OFF_GB] = dga_ref[...]
        dproj_ref[:, OFF_GB:D_IN] = dgb_ref[...]
        first = pl.program_id(0) == 0
        for o_ref, part in ((ogqa, p_qa), (ogka, p_ka), (ogqb, p_qb), (ogkb, p_kb)):
            @pl.when(first)
            def _(o_ref=o_ref, part=part):
                o_ref[...] = part

            @pl.when(jnp.logical_not(first))
            def _(o_ref=o_ref, part=part):
                o_ref[...] += part

    def win(width, off):
        return pl.BlockSpec((tm, width), lambda i: (i, off // width))

    vec = lambda w: pl.BlockSpec((1, w), lambda i: (0, 0))
    row = lambda w: pl.BlockSpec((tm, w), lambda i: (i, 0))
    in_specs = [win(WIDTH, OFF_QA), win(WIDTH, OFF_KA), win(WIDTH, OFF_QB), win(LANES, OFF_KB),
                vec(WIDTH), vec(WIDTH), vec(WIDTH), vec(LANES)]
    in_specs += [row(WIDTH)] * (3 * n_a + 3) + [row(D_MODEL)] * 2
    return pl.pallas_call(
        body,
        out_shape=[jax.ShapeDtypeStruct((rows, D_IN), BF16), jax.ShapeDtypeStruct((1, WIDTH), F32),
                   jax.ShapeDtypeStruct((1, WIDTH), F32), jax.ShapeDtypeStruct((1, WIDTH), F32),
                   jax.ShapeDtypeStruct((1, LANES), F32)],
        grid=(rows // tm,), in_specs=in_specs,
        out_specs=[row(D_IN), vec(WIDTH), vec(WIDTH), vec(WIDTH), vec(LANES)],
        name="qknorm_bwd", compiler_params=_params("arbitrary"))(
            proj, proj, proj, proj, gqa, gka, gqb, gkb, *dqa, *dka, *dva, dqb, dkb, dvb, dga, dgb)


def _t5_bucket(rel):
    half_b = NUM_BUCKETS // 2
    max_exact = half_b // 2
    sign = jnp.where(rel > 0, half_b, 0)
    n = jnp.abs(rel)
    nf = jnp.maximum(n, 1).astype(F32)
    large = max_exact + (jnp.log(nf / max_exact) / math.log(MAX_DISTANCE / max_exact)
                         * (half_b - max_exact)).astype(jnp.int32)
    large = jnp.minimum(large, half_b - 1)
    return sign + jnp.where(n < max_exact, n, large)


def _band_buckets(blk, dilation):
    i = jnp.arange(blk, dtype=jnp.int32)[:, None]
    j = jnp.arange(3 * blk, dtype=jnp.int32)[None, :]
    rel = j - blk - i
    return jnp.where(jnp.abs(rel) <= blk, _t5_bucket(rel * dilation), -1)


def _bias_tiles(table, buckets, head_off, name):
    blk = buckets.shape[0]

    def body(tab_ref, bk_ref, o_ref):
        h = pl.program_id(0) + head_off
        bk = bk_ref[...]
        acc = jnp.full(bk.shape, NEG_INF, F32)
        for b in range(NUM_BUCKETS):
            acc = jnp.where(bk == b, tab_ref[b, h], acc)
        o_ref[0] = acc

    return pl.pallas_call(
        body, out_shape=jax.ShapeDtypeStruct((N_HEADS, blk, 3 * blk), F32), grid=(N_HEADS,),
        in_specs=[pl.BlockSpec(memory_space=pltpu.SMEM), pl.BlockSpec((blk, 3 * blk), lambda h: (0, 0))],
        out_specs=pl.BlockSpec((1, blk, 3 * blk), lambda h: (h, 0, 0)),
        name=name, compiler_params=_params("parallel"))(table, buckets)


def _table_grad(dbias, buckets, name):
    blk = buckets.shape[0]

    def body(db_ref, bk_ref, o_ref):
        bk = bk_ref[...]
        dbv = db_ref[0]
        lane = lax.broadcasted_iota(jnp.int32, (1, LANES), 1)
        acc = jnp.zeros((1, LANES), F32)
        for b in range(NUM_BUCKETS):
            acc = jnp.where(lane == b, jnp.sum(jnp.where(bk == b, dbv, 0.0)), acc)
        o_ref[0] = acc

    out = pl.pallas_call(
        body, out_shape=jax.ShapeDtypeStruct((N_HEADS, 1, LANES), F32), grid=(N_HEADS,),
        in_specs=[pl.BlockSpec((1, blk, 3 * blk), lambda h: (h, 0, 0)), pl.BlockSpec((blk, 3 * blk), lambda h: (0, 0))],
        out_specs=pl.BlockSpec((1, 1, LANES), lambda h: (h, 0, 0)),
        name=name, compiler_params=_params("parallel"))(dbias, buckets)
    return out[:, 0, :NUM_BUCKETS]


def _dot_nt(a, b):
    return lax.dot_general(a, b, (((1,), (1,)), ((), ())), preferred_element_type=F32)


def _dot_tn(a, b):
    return lax.dot_general(a, b, (((0,), (0,)), ((), ())), preferred_element_type=F32)


def _stack_pair(x2, left):
    return jnp.concatenate([jnp.where(left, x2, 0.0), jnp.where(left, 0.0, x2)], axis=0).astype(BF16)


def _attn_geometry(blk, d):
    halo = blk * d
    subs = max(1, min(ITEMS // d, MAX_CHUNK // halo))
    return subs, min(d, ITEMS // subs), halo


def _item_of(j, r0, per_group):
    return j // per_group, r0 + j % per_group


def _span_rows(ref, first, r, blk, d):
    if d == 1:
        return ref[first:first + blk, :]
    return ref[pl.ds(first + r, blk, stride=d), :]


def _set_span_rows(ref, first, r, blk, d, val, add=False):
    idx = slice(first, first + blk) if d == 1 else pl.ds(first + r, blk, stride=d)
    ref[idx, :] = ref[idx, :] + val if add else val


def _for_groups(group, groups, per_group):
    if groups == 1:
        group(0)
    else:
        def step(g, carry):
            group(g * per_group)
            return carry

        lax.fori_loop(0, groups, step, 0)


def _key_rows(p_ref, c_ref, n_ref, s, r, subs, halo, blk, d):
    parts = []
    for span in (s - 1, s, s + 1):
        if span < 0:
            parts.append(_span_rows(p_ref, 0, r, blk, d))
        elif span == subs:
            parts.append(_span_rows(n_ref, 0, r, blk, d))
        else:
            parts.append(_span_rows(c_ref, span * halo, r, blk, d))
    return jnp.concatenate(parts, axis=0)


def _item_penalty(t, nct, s, subs, blk):
    first_ok = True if s > 0 else t > 0
    last_ok = True if s < subs - 1 else t < nct - 1
    col = lax.broadcasted_iota(jnp.int32, (1, 3 * blk), 1)
    ok = jnp.logical_and(jnp.logical_or(col >= blk, first_ok), jnp.logical_or(col < 2 * blk, last_ok))
    return jnp.where(ok, 0.0, NEG_INF).astype(F32)


def _attn_specs(seq, blk, d, step_of, col=0):
    subs, _, halo = _attn_geometry(blk, d)
    last, first = seq // halo - 1, col // LANES
    cur = pl.BlockSpec((subs * halo, LANES), lambda hp, t: (step_of(t), first + hp))
    prev = pl.BlockSpec((halo, LANES), lambda hp, t: (jnp.clip(step_of(t) * subs - 1, 0, last), first + hp))
    nxt = pl.BlockSpec((halo, LANES), lambda hp, t: (jnp.minimum((step_of(t) + 1) * subs, last), first + hp))
    return cur, prev, nxt


def _attn_fwd(q, k, v, bias, sink, blk, d, name, v_col=0):
    seq = q.shape[0]
    subs, per_group, halo = _attn_geometry(blk, d)
    items, chunk, groups = subs * per_group, subs * halo, d // per_group
    nct = seq // chunk
    has_sink = sink is not None
    scale = HEAD_DIM ** -0.5

    def body(*refs):
        q_ref, kp, kc, kn, vp, vc, vn, b_ref = refs[:8]
        s_ref = refs[8] if has_sink else None
        o_ref, l_ref = refs[-2], refs[-1]
        t = pl.program_id(1)
        left = lax.broadcasted_iota(jnp.int32, (1, LANES), 1) < HEAD_DIM
        bias2 = b_ref[...]

        def group(r0):
            scores, vcats = [], []
            for j in range(items):
                s, r = _item_of(j, r0, per_group)
                qs = _stack_pair(_span_rows(q_ref, s * halo, r, blk, d) * scale, left)
                kcat = _key_rows(kp, kc, kn, s, r, subs, halo, blk, d).astype(BF16)
                scores.append(_dot_nt(qs, kcat) + bias2 + _item_penalty(t, nct, s, subs, blk))
                vcats.append(_key_rows(vp, vc, vn, s, r, subs, halo, blk, d).astype(BF16))
            ms = [jnp.max(s, axis=-1, keepdims=True) for s in scores]
            if has_sink:
                sk = s_ref[...]
                ms = [jnp.maximum(m, sk) for m in ms]
            ps = [jnp.exp(s - m) for s, m in zip(scores, ms)]
            dens = [jnp.sum(p, axis=-1, keepdims=True) for p in ps]
            if has_sink:
                dens = [den + jnp.exp(sk - m) for den, m in zip(dens, ms)]
            pns = [(p * (1.0 / den)).astype(BF16) for p, den in zip(ps, dens)]
            lses = [m + jnp.log(den) for m, den in zip(ms, dens)]
            for j in range(items):
                s, r = _item_of(j, r0, per_group)
                o2 = jnp.dot(pns[j], vcats[j], preferred_element_type=F32)
                _set_span_rows(o_ref, s * halo, r, blk, d, jnp.where(left, o2[:blk], o2[blk:]))
                _set_span_rows(l_ref, s * halo, r, blk, d, jnp.where(left, lses[j][:blk], lses[j][blk:]))

        _for_groups(group, groups, per_group)

    cur, prev, nxt = _attn_specs(seq, blk, d, lambda t: t)
    v_cur, v_prev, v_nxt = _attn_specs(seq, blk, d, lambda t: t, v_col)
    in_specs = [cur, prev, cur, nxt, v_prev, v_cur, v_nxt, pl.BlockSpec((2 * blk, 3 * blk), lambda hp, t: (hp, 0))]
    args = [q, k, k, k, v, v, v, bias]
    if has_sink:
        in_specs.append(pl.BlockSpec((2 * blk, 1), lambda hp, t: (hp, 0)))
        args.append(sink)
    return pl.pallas_call(
        body, out_shape=[jax.ShapeDtypeStruct((seq, WIDTH), F32)] * 2, grid=(N_PAIRS, nct),
        in_specs=in_specs, out_specs=[cur, cur], name=name,
        compiler_params=_params("parallel", "parallel"))(*args)


def _attn_bwd(q, k, v, do, lse, delta, bias, sink, blk, d, name, v_col=0):
    seq = q.shape[0]
    subs, per_group, halo = _attn_geometry(blk, d)
    items, chunk, groups = subs * per_group, subs * halo, d // per_group
    nct = seq // chunk
    has_sink = sink is not None
    n_in = 12 if has_sink else 11
    scale = HEAD_DIM ** -0.5

    def body(*refs):
        q_ref, kp, kc, kn, vp, vc, vn, do_ref, l_ref, d_ref, b_ref = refs[:11]
        s_ref = refs[11] if has_sink else None
        dq_ref, dk_ref, dv_ref, db_ref = refs[n_in:n_in + 4]
        ds_ref = refs[n_in + 4] if has_sink else None
        wk, wv = refs[-2], refs[-1]
        t = pl.program_id(1)

        @pl.when(t == 0)
        def _():
            wk[...] = jnp.zeros_like(wk)
            wv[...] = jnp.zeros_like(wv)
            db_ref[...] = jnp.zeros_like(db_ref)
            if has_sink:
                ds_ref[...] = jnp.zeros_like(ds_ref)

        @pl.when(t > 0)
        def _():
            for w in (wk, wv):
                keep = w[chunk:2 * chunk + halo]
                w[0:chunk + halo] = keep
                w[chunk + halo:2 * chunk + halo] = jnp.zeros((chunk, LANES), F32)

        @pl.when(t < nct)
        def _():
            lane = lax.broadcasted_iota(jnp.int32, (1, LANES), 1)
            left = lane < HEAD_DIM
            bias2 = b_ref[...]

            def group(r0):
                qss, doss, kcats, scores, dps, lcols, dcols = [], [], [], [], [], [], []
                for j in range(items):
                    s, r = _item_of(j, r0, per_group)
                    qs = _stack_pair(_span_rows(q_ref, s * halo, r, blk, d) * scale, left)
                    dos = _stack_pair(_span_rows(do_ref, s * halo, r, blk, d), left)
                    kcat = _key_rows(kp, kc, kn, s, r, subs, halo, blk, d).astype(BF16)
                    vcat = _key_rows(vp, vc, vn, s, r, subs, halo, blk, d).astype(BF16)
                    l2, d2 = _span_rows(l_ref, s * halo, r, blk, d), _span_rows(d_ref, s * halo, r, blk, d)
                    lcols.append(jnp.concatenate([jnp.max(jnp.where(left, l2, NEG_INF), axis=-1, keepdims=True),
                                                  jnp.max(jnp.where(left, NEG_INF, l2), axis=-1, keepdims=True)], axis=0))
                    dcols.append(jnp.concatenate([jnp.sum(jnp.where(lane == 0, d2, 0.0), axis=-1, keepdims=True),
                                                  jnp.sum(jnp.where(lane == HEAD_DIM, d2, 0.0), axis=-1, keepdims=True)],
                                                 axis=0))
                    scores.append(_dot_nt(qs, kcat) + bias2 + _item_penalty(t, nct, s, subs, blk))
                    dps.append(_dot_nt(dos, vcat))
                    qss.append(qs)
                    doss.append(dos)
                    kcats.append(kcat)
                ps = [jnp.exp(s - lc) for s, lc in zip(scores, lcols)]
                dss = [p * (dp - dc) for p, dp, dc in zip(ps, dps, dcols)]
                db_ref[...] += functools.reduce(lambda a, b: a + b, dss)
                if has_sink:
                    sk = s_ref[...]
                    ds_ref[...] -= functools.reduce(lambda a, b: a + b, [dc * jnp.exp(sk - lc) for dc, lc in zip(dcols, lcols)])
                dsbs = [ds.astype(BF16) for ds in dss]
                for j in range(items):
                    s, r = _item_of(j, r0, per_group)
                    dq2 = jnp.dot(dsbs[j], kcats[j], preferred_element_type=F32) * scale
                    _set_span_rows(dq_ref, s * halo, r, blk, d, jnp.where(left, dq2[:blk], dq2[blk:]))
                news = [(_dot_tn(dsbs[j], qss[j]), _dot_tn(ps[j].astype(BF16), doss[j])) for j in range(items)]
                for which, w in enumerate((wk, wv)):
                    for jr in range(per_group):
                        for sp in range(-1, subs + 1):
                            parts = [news[s * per_group + jr][which][(sp - s + 1) * blk:(sp - s + 2) * blk]
                                     for s in range(subs) if 0 <= sp - s + 1 < 3]
                            _set_span_rows(w, chunk + sp * halo, r0 + jr, blk, d,
                                           functools.reduce(lambda x, y: x + y, parts), add=True)

            _for_groups(group, groups, per_group)

        dk_ref[...] = wk[0:chunk]
        dv_ref[...] = wv[0:chunk]

    cur, prev, nxt = _attn_specs(seq, blk, d, lambda t: jnp.minimum(t, nct - 1))
    v_cur, v_prev, v_nxt = _attn_specs(seq, blk, d, lambda t: jnp.minimum(t, nct - 1), v_col)
    lag = pl.BlockSpec((chunk, LANES), lambda hp, t: (jnp.maximum(t - 1, 0), hp))
    band = pl.BlockSpec((2 * blk, 3 * blk), lambda hp, t: (hp, 0))
    col = pl.BlockSpec((2 * blk, 1), lambda hp, t: (hp, 0))
    in_specs = [cur, prev, cur, nxt, v_prev, v_cur, v_nxt, cur, cur, cur, band]
    args = [q, k, k, k, v, v, v, do, lse, delta, bias]
    out_shape = [jax.ShapeDtypeStruct((seq, WIDTH), F32)] * 3 + [jax.ShapeDtypeStruct((N_HEADS * blk, 3 * blk), F32)]
    out_specs = [cur, lag, lag, band]
    if has_sink:
        in_specs.append(col)
        args.append(sink)
        out_shape.append(jax.ShapeDtypeStruct((N_HEADS * blk, 1), F32))
        out_specs.append(col)
    window = pltpu.VMEM((2 * chunk + halo, LANES), F32)
    return pl.pallas_call(
        body, out_shape=out_shape, grid=(N_PAIRS, nct + 1), in_specs=in_specs, out_specs=out_specs,
        scratch_shapes=[window, window], name=name,
        compiler_params=_params("arbitrary", "arbitrary"))(*args)


def _combine_patterns(outs, lses):
    def combine(*tiles):
        os_, ls = tiles[:len(outs)], tiles[len(outs):]
        m = functools.reduce(jnp.maximum, ls)
        es = [jnp.exp(l - m) for l in ls]
        den = functools.reduce(lambda a, b: a + b, es)
        num = functools.reduce(lambda a, b: a + b, [e * o for e, o in zip(es, os_)])
        return num / den, m + jnp.log(den)

    return _ew(combine, [*outs, *lses], [F32, F32], "combine_a")


def _tile_gain(g, reps):
    return jnp.tile(g[None, :], (1, reps))


def _local_step(x, p, target, w_in_of, rest_of, small):
    rel_table = small["rel_table"]
    buckets_a = [_band_buckets(blk, d) for blk, d in DILATED]
    buckets_b = _band_buckets(BLK_B, 1)
    bias_a = [_bias_tiles(rel_table, bk, 0, "bias_a").reshape(N_HEADS * bk.shape[0], -1) for bk in buckets_a]
    bias_b = _bias_tiles(rel_table, buckets_b, N_HEADS, "bias_b").reshape(N_HEADS * BLK_B, -1)

    saved = []
    for l in range(DEPTH):
        g_mix, g_ffn, g_ple = (small[n][l][None, :] for n in ("norm_mix_g", "norm_ffn_g", "norm_ple_g"))
        gqa, gka, gqb = (_tile_gain(small[n][l], N_HEADS) for n in ("qnorm_a_g", "knorm_a_g", "qnorm_b_g"))
        gkb = _tile_gain(small["knorm_b_g"][l], N_KV_B)
        sink = jnp.repeat(small["sink_b"][l], BLK_B)[:, None]

        h = _rms_fwd(x, g_mix, "rms_mix") if l == 0 else h_next
        w_in = w_in_of(l, (h, bias_a, bias_b))
        proj = _mm(h, w_in, "nt", F32, "mm_in")
        qa, ka, qb, kb, vb = _qknorm_fwd(proj, gqa, gka, gqb, gkb)
        outs, lses = [], []
        for (blk, d), bias in zip(DILATED, bias_a):
            o, ls = _attn_fwd(qa, ka, proj, bias, None, blk, d, f"attn_a{d}_fwd", v_col=OFF_VA)
            outs.append(o)
            lses.append(ls)
        ya, lse_a = _combine_patterns(outs, lses)
        yb, lse_b = _attn_fwd(qb, kb, vb, bias_b, sink, BLK_B, 1, "attn_b_fwd")
        w = dict(rest_of(l, yb), w_in=w_in)
        def gate(products, extra):
            (ca_, cb_), (ga_, gb_) = products, extra
            return _sigmoid(ga_) * ca_ + _sigmoid(gb_) * cb_, ca_, cb_

        merged, ca, cb = _mm_fused([(ya, w["w_branch_a"], "nn"), (yb, w["w_branch_b"], "nn")],
                                   [(proj, OFF_GA), (proj, OFF_GB)], gate, [BF16, BF16, BF16], "mm_branches_gate")
        x1, h2 = _mm_res_norm(merged, w["w_out"], "nn", x, g_ffn, "mm_out_norm")

        def swiglu(products, extra):
            a_, u_ = products
            return (a_ * _sigmoid(a_)) * u_, a_, u_

        hid, a, u = _mm_fused([(h2, w["w_ffn_gate"], "nt"), (h2, w["w_ffn_up"], "nt")], [], swiglu,
                              [BF16, BF16, BF16], "mm_ffn_gate_up")
        x2, h3 = _mm_res_norm(hid, w["w_ffn_down"], "nn", x1, g_ple, "mm_ffn_down_norm")

        def ple(products, extra):
            z_, e_ = products
            return extra[0] + _sigmoid(z_) * e_, z_, e_

        next_gain = small["norm_mix_g"][l + 1][None, :] if l + 1 < DEPTH else None
        x3, z, e, *rest = _mm_fused([(h3, w["w_ple_gate"], "nn"), (p[l], w["w_ple_proj"], "nn")], [(x2, 0)], ple,
                                    [F32, BF16, BF16], "mm_ple", next_gain=next_gain)
        h_next = rest[0] if rest else None
        saved.append(dict(w=w, x0=x, h=h, proj=proj, qa=qa, ka=ka, qb=qb, kb=kb, vb=vb, ya=ya, lse_a=lse_a,
                          yb=yb, lse_b=lse_b, ca=ca, cb=cb, merged=merged, x1=x1, h2=h2, a=a, u=u, hid=hid,
                          x2=x2, h3=h3, z=z, e=e))
        x = x3

    dx, de, dz, loss_acc = _loss_grad(x, target, saved[-1]["z"], saved[-1]["e"])
    loss = loss_acc[0, 0]

    gbig = [{} for _ in range(DEPTH)]
    marks = [{} for _ in range(DEPTH)]
    gsmall = {n: [None] * DEPTH for n in SMALL if n != "rel_table"}
    dbias_a = [[] for _ in DILATED]
    dbias_b = []

    for l in reversed(range(DEPTH)):
        sv = saved[l]
        w = sv["w"]
        g_mix, g_ffn, g_ple = (small[n][l][None, :] for n in ("norm_mix_g", "norm_ffn_g", "norm_ple_g"))
        gqa, gka, gqb = (_tile_gain(small[n][l], N_HEADS) for n in ("qnorm_a_g", "knorm_a_g", "qnorm_b_g"))
        gkb = _tile_gain(small["knorm_b_g"][l], N_KV_B)
        sink = jnp.repeat(small["sink_b"][l], BLK_B)[:, None]

        if l < DEPTH - 1:
            de, dz = _ew(_ple_bwd, [dx, sv["z"], sv["e"]], [BF16, BF16], "ple_bwd")
        gbig[l]["w_ple_proj"] = _mm(p[l], de, "tn", BF16, "mm_d_ple_proj")
        gbig[l]["w_ple_gate"] = _mm(sv["h3"], dz, "tn", BF16, "mm_d_ple_gate")
        dx, dxb, gsmall["norm_ple_g"][l] = _mm_rms_bwd([(dz, w["w_ple_gate"], "nt")], sv["x2"], g_ple, dx,
                                                      "mm_dh3_rms_bwd")

        gbig[l]["w_ffn_down"] = _mm(sv["hid"], dxb, "tn", BF16, "mm_d_ffn_down")

        def swiglu_bwd(products, extra):
            dh_, a_, u_ = products[0], extra[0].astype(F32), extra[1].astype(F32)
            s = _sigmoid(a_)
            return dh_ * u_ * (s * (1.0 + a_ * (1.0 - s))), dh_ * (a_ * s)

        da, du = _mm_fused([(dxb, w["w_ffn_down"], "nt")], [(sv["a"], 0), (sv["u"], 0)], swiglu_bwd, [BF16, BF16],
                           "mm_dhid_swiglu_bwd")
        gbig[l]["w_ffn_gate"], gbig[l]["w_ffn_up"] = _mm_fused(
            [(da, sv["h2"], "tn"), (du, sv["h2"], "tn")], [], lambda products, extra: products, [BF16, BF16],
            "mm_d_ffn_gate_up")
        dx, dxb, gsmall["norm_ffn_g"][l] = _mm_rms_bwd([(da, w["w_ffn_gate"], "nn"), (du, w["w_ffn_up"], "nn")],
                                                      sv["x1"], g_ffn, dx, "mm_dh2_rms_bwd")

        gbig[l]["w_out"] = _mm(sv["merged"], dxb, "tn", BF16, "mm_d_out")

        def gate_bwd(products, extra):
            dm_, ca_, cb_ = products[0], extra[0].astype(F32), extra[1].astype(F32)
            sa, sb = _sigmoid(extra[2]), _sigmoid(extra[3])
            return dm_ * sa, dm_ * sb, dm_ * ca_ * (sa * (1.0 - sa)), dm_ * cb_ * (sb * (1.0 - sb))

        dca, dcb, dga, dgb = _mm_fused(
            [(dxb, w["w_out"], "nt")], [(sv["ca"], 0), (sv["cb"], 0), (sv["proj"], OFF_GA), (sv["proj"], OFF_GB)],
            gate_bwd, [BF16, BF16, BF16, BF16], "mm_dmerged_gate_bwd")
        gbig[l]["w_branch_a"], gbig[l]["w_branch_b"] = _mm_fused(
            [(sv["ya"], dca, "tn"), (sv["yb"], dcb, "tn")], [], lambda products, extra: products, [BF16, BF16],
            "mm_d_branches")
        def with_row_dots(products, extra):
            return products[0], _seg_sum(products[0] * extra[0])

        dya, delta_a = _mm_fused([(dca, w["w_branch_a"], "nt")], [(sv["ya"], 0)], with_row_dots, [F32, F32], "mm_dya")
        dyb, delta_b = _mm_fused([(dcb, w["w_branch_b"], "nt")], [(sv["yb"], 0)], with_row_dots, [F32, F32], "mm_dyb")

        dqa, dka, dva = [], [], []
        for (blk, d), bias, bk in zip(DILATED, bias_a, buckets_a):
            dq_, dk_, dv_, db_ = _attn_bwd(sv["qa"], sv["ka"], sv["proj"], dya, sv["lse_a"], delta_a, bias, None, blk, d,
                                           f"attn_a{d}_bwd", v_col=OFF_VA)
            dqa.append(dq_)
            dka.append(dk_)
            dva.append(dv_)
            dbias_a[len(dqa) - 1].append(db_)
        dqb, dkb, dvb, db_, dsink = _attn_bwd(sv["qb"], sv["kb"], sv["vb"], dyb, sv["lse_b"], delta_b, bias_b, sink,
                                              BLK_B, 1, "attn_b_bwd")
        dbias_b.append(db_)
        gsmall["sink_b"][l] = dsink.reshape(N_HEADS, BLK_B).sum(axis=1)

        dproj, pqa, pka, pqb, pkb = _qknorm_bwd(sv["proj"], gqa, gka, gqb, gkb, dqa, dka, dva, dqb, dkb, dvb, dga, dgb)
        marks[l]["attn_bwd_done"] = dproj
        gsmall["qnorm_a_g"][l] = pqa.reshape(N_HEADS, HEAD_DIM).sum(0)
        gsmall["knorm_a_g"][l] = pka.reshape(N_HEADS, HEAD_DIM).sum(0)
        gsmall["qnorm_b_g"][l] = pqb.reshape(N_HEADS, HEAD_DIM).sum(0)
        gsmall["knorm_b_g"][l] = pkb.reshape(N_KV_B, HEAD_DIM).sum(0)
        gbig[l]["w_in"] = _mm(dproj, sv["h"], "tn", BF16, "mm_d_in")
        dx, _, gsmall["norm_mix_g"][l] = _mm_rms_bwd([(dproj, w["w_in"], "nn")], sv["x0"], g_mix, dx, "mm_dh_rms_bwd")
        gsmall["norm_mix_g"][l] = gsmall["norm_mix_g"][l][0]
        gsmall["norm_ffn_g"][l] = gsmall["norm_ffn_g"][l][0]
        gsmall["norm_ple_g"][l] = gsmall["norm_ple_g"][l][0]

    gsmall = {n: jnp.stack(v) for n, v in gsmall.items()}
    dtable_a = sum(_table_grad(sum(dbs).reshape(N_HEADS, blk, 3 * blk), bk, "table_grad_a")
                   for dbs, (blk, _), bk in zip(dbias_a, DILATED, buckets_a))
    dtable_b = _table_grad(sum(dbias_b).reshape(N_HEADS, BLK_B, 3 * BLK_B), buckets_b, "table_grad_b")
    gsmall["rel_table"] = jnp.concatenate([dtable_a, dtable_b], axis=0).T
    return loss, dx, gbig, gsmall, marks


def _place():
    return lax.axis_index("x"), lax.axis_index("y"), lax.axis_index("c")


def _flip(v, bit):
    return 1 - v if bit else v


CHIP_RELATIONS = ((0, 1), (1, 0), (1, 1))
ANY = pl.BlockSpec(memory_space=pl.ANY)


def _allgather_body(w_refs, out_refs, send_sems, recv_sems):
    x, y, c = _place()
    chips = [(_flip(x, a), _flip(y, b)) for a, b in CHIP_RELATIONS]

    def make(g):
        w_ref, out_ref = w_refs[g], out_refs[g]
        half = w_ref.shape[0] // 2

        def part(px, py, pc):
            return out_ref.at[2 * px + py, pl.ds(pc * half, half), :]

        def copy(k, block, to, src=None):
            return pltpu.make_async_remote_copy(
                src_ref=part(*block) if src is None else src, dst_ref=part(*block),
                send_sem=send_sems.at[7 * g + k], recv_sem=recv_sems.at[7 * g + k], device_id=to,
                device_id_type=MESH_ID)

        own = pltpu.make_async_remote_copy(
            src_ref=w_ref, dst_ref=out_ref.at[2 * x + y], send_sem=send_sems.at[7 * g + 6],
            recv_sem=recv_sems.at[7 * g + 6], device_id=(x, y, 1 - c), device_id_type=MESH_ID)
        first = [copy(k, (x, y, c), (*chip, c), src=w_ref.at[pl.ds(c * half, half), :]) for k, chip in enumerate(chips)]
        passed = [copy(3 + k, (*chip, c), (x, y, 1 - c)) for k, chip in enumerate(chips)]
        arrive = [copy(k, (*chip, c), (x, y, c)) for k, chip in enumerate(chips)]
        arrive2 = [copy(3 + k, (*chip, 1 - c), (x, y, c)) for k, chip in enumerate(chips)]
        return own, first, passed, arrive, arrive2

    made = [make(g) for g in range(len(w_refs))]
    for own, first, _, _, _ in made:
        own.start()
        for cp in first:
            cp.start()
    for _, _, passed, arrive, _ in made:
        for k in range(3):
            arrive[k].wait_recv()
            passed[k].start()
    for own, first, passed, _, arrive2 in made:
        for k in range(3):
            arrive2[k].wait_recv()
        own.wait_recv()
        for cp in first + passed + [own]:
            cp.wait_send()


def _sibling(x, y, c):
    return [(x, y, 1 - c)]


def _same_core_of_other_chips(x, y, c):
    return [(_flip(x, a), _flip(y, b), c) for a, b in CHIP_RELATIONS]


def _exchange(body, ins, out_types, n_sems, name, sequencer=None):
    n = len(ins)
    sems = (pltpu.SemaphoreType.DMA((n_sems,)), pltpu.SemaphoreType.DMA((n_sems,)))
    if sequencer is None:
        in_place = out_types is None
        out_shape = [jax.ShapeDtypeStruct(a.shape, a.dtype) for a in ins] if in_place else out_types

        def tc_body(*refs):
            body(refs[:n], refs[n:n + len(out_shape)], refs[-2], refs[-1])

        return list(pl.pallas_call(
            tc_body, out_shape=out_shape, in_specs=[ANY] * n, out_specs=[ANY] * len(out_shape),
            input_output_aliases={g: g for g in range(n)} if in_place else {}, scratch_shapes=list(sems), name=name)(*ins))

    collective_id, peers = sequencer
    hbm = pltpu.MemorySpace.HBM
    in_refs = [jax.new_ref(a, memory_space=hbm) for a in ins]
    out_refs = in_refs if out_types is None else [jax.empty_ref(t, memory_space=hbm) for t in out_types]

    @pl.kernel(mesh=plsc.ScalarSubcoreMesh(axis_name="sequencer", num_cores=1), name=name, scratch_types=sems,
               compiler_params=pltpu.CompilerParams(collective_id=collective_id))
    def launch(send_sems, recv_sems):
        barrier = pltpu.get_barrier_semaphore()
        devices = peers(*_place())
        for device in devices:
            pl.semaphore_signal(barrier, inc=1, device_id=device, device_id_type=MESH_ID)
        pl.semaphore_wait(barrier, len(devices))
        body(in_refs, out_refs, send_sems, recv_sems)

    launch()
    return [r[...] for r in out_refs]


def _allgather(shards, name, sequencer=None):
    out_types = [jax.ShapeDtypeStruct((N_CHIPS,) + s.shape, s.dtype) for s in shards]
    if sequencer is not None:
        sequencer = (sequencer, lambda x, y, c: _sibling(x, y, c) + _same_core_of_other_chips(x, y, c))
    return _exchange(_allgather_body, shards, out_types, 7 * len(shards), name, sequencer)


def _half_tile(half):
    return max(t for t in range(16, 1025, 16) if half % t == 0)


def _run_copies(cps):
    for cp in cps:
        cp.start()
    for cp in cps:
        cp.wait_recv()
    for cp in cps:
        cp.wait_send()


def _sibling_halves(gsends, name, sequencer=None):
    def body(g_refs, out_refs, send_sems, recv_sems):
        x, y, c = _place()
        cps = []
        for g, (g_ref, out_ref) in enumerate(zip(g_refs, out_refs)):
            half = g_ref.shape[1] // 2
            cps.append(pltpu.make_async_remote_copy(
                src_ref=g_ref.at[:, pl.ds((1 - c) * half, half), :], dst_ref=out_ref,
                send_sem=send_sems.at[g], recv_sem=recv_sems.at[g], device_id=(x, y, 1 - c), device_id_type=MESH_ID))
        _run_copies(cps)

    out_types = [jax.ShapeDtypeStruct((s.shape[0], s.shape[1] // 2, s.shape[2]), s.dtype) for s in gsends]
    return _exchange(body, gsends, out_types, len(gsends), name, sequencer and (sequencer, _sibling))


def _chip_sums(gsend, sib, place):
    n, rows, cols = gsend.shape
    half = rows // 2
    tm = _half_tile(half)
    nblk = half // tm

    def body(s_ref, g_ref, sib_ref, o_ref):
        o_ref[0] = (g_ref[0].astype(F32) + sib_ref[0].astype(F32)).astype(o_ref.dtype)

    grid_spec = pltpu.PrefetchScalarGridSpec(
        num_scalar_prefetch=1, grid=(n, nblk),
        in_specs=[pl.BlockSpec((1, tm, cols), lambda k, i, s: (jnp.bitwise_xor(s[0], k), s[1] * nblk + i, 0)),
                  pl.BlockSpec((1, tm, cols), lambda k, i, s: (jnp.bitwise_xor(s[0], k), i, 0))],
        out_specs=pl.BlockSpec((1, tm, cols), lambda k, i, s: (k, i, 0)))
    return pl.pallas_call(
        body, out_shape=jax.ShapeDtypeStruct((n, half, cols), BF16), grid_spec=grid_spec,
        name="rs_chip_sums", compiler_params=_params("parallel", "parallel"))(place, gsend, sib)


def _exchange_chip_sums(tsends, name, sequencer=None):
    def body(t_refs, out_refs, send_sems, recv_sems):
        x, y, c = _place()
        cps = []
        for g, (t_ref, out_ref) in enumerate(zip(t_refs, out_refs)):
            for k, device in enumerate(_same_core_of_other_chips(x, y, c)):
                cps.append(pltpu.make_async_remote_copy(
                    src_ref=t_ref.at[k + 1], dst_ref=out_ref.at[k], send_sem=send_sems.at[3 * g + k],
                    recv_sem=recv_sems.at[3 * g + k], device_id=device, device_id_type=MESH_ID))
        _run_copies(cps)

    out_types = [jax.ShapeDtypeStruct((3,) + s.shape[1:], s.dtype) for s in tsends]
    return _exchange(body, tsends, out_types, 3 * len(tsends), name,
                     sequencer and (sequencer, _same_core_of_other_chips))


def _final_sum(tsend, recv, place):
    n, half, cols = tsend.shape
    tm = _half_tile(half)
    nblk = half // tm

    def body(s_ref, t_ref, r_ref, o_ref):
        o_ref[...] = ((t_ref[0].astype(F32) + r_ref[0].astype(F32)) + r_ref[1].astype(F32)) + r_ref[2].astype(F32)

    grid_spec = pltpu.PrefetchScalarGridSpec(
        num_scalar_prefetch=1, grid=(nblk,),
        in_specs=[pl.BlockSpec((1, tm, cols), lambda i, s: (0, i, 0)), pl.BlockSpec((n - 1, tm, cols), lambda i, s: (0, i, 0))],
        out_specs=pl.BlockSpec((tm, cols), lambda i, s: (s[1] * nblk + i, 0)))
    return pl.pallas_call(
        body, out_shape=jax.ShapeDtypeStruct((2 * half, cols), F32), grid_spec=grid_spec, name="rs_final_sum",
        compiler_params=_params("parallel"))(place, tsend, recv)


def _join_halves(gfulls, name, sequencer=None):
    def body(g_refs, out_refs, send_sems, recv_sems):
        x, y, c = _place()
        n = len(g_refs)

        def copy(g, pc):
            half = g_refs[g].shape[0] // 2
            return pltpu.make_async_remote_copy(
                src_ref=g_refs[g].at[pl.ds(pc * half, half), :], dst_ref=out_refs[g].at[pl.ds(pc * half, half), :],
                send_sem=send_sems.at[g], recv_sem=recv_sems.at[g], device_id=(x, y, 1 - c), device_id_type=MESH_ID)

        mine = [copy(g, c) for g in range(n)]
        for cp in mine:
            cp.start()
        for g in range(n):
            copy(g, 1 - c).wait_recv()
        for cp in mine:
            cp.wait_send()

    return _exchange(body, gfulls, None, len(gfulls), name, sequencer and (sequencer, _sibling))


def _allreduce_small(v):
    rows, cols = v.shape

    def body(v_ref, out_ref, buf, send_sems, recv_sems):
        x, y, c = _place()
        cps = []
        for k in range(1, 8):
            peer = (_flip(x, (k >> 2) & 1), _flip(y, (k >> 1) & 1), _flip(c, k & 1))
            cps.append(pltpu.make_async_remote_copy(
                src_ref=v_ref, dst_ref=buf.at[k - 1], send_sem=send_sems.at[k - 1], recv_sem=recv_sems.at[k - 1],
                device_id=peer, device_id_type=MESH_ID))
        for cp in cps:
            cp.start()
        for cp in cps:
            cp.wait_recv()
        for cp in cps:
            cp.wait_send()
        t0 = v_ref[...] + buf[0]
        t1 = buf[1] + buf[2]
        t2 = buf[3] + buf[4]
        t3 = buf[5] + buf[6]
        out_ref[...] = (t0 + t1) + (t2 + t3)

    vm = pl.BlockSpec(memory_space=pltpu.VMEM)
    return pl.pallas_call(
        body, out_shape=jax.ShapeDtypeStruct((rows, cols), F32), in_specs=[vm], out_specs=vm,
        scratch_shapes=[pltpu.VMEM((7, rows, cols), F32), pltpu.SemaphoreType.DMA((7,)), pltpu.SemaphoreType.DMA((7,))],
        name="allreduce_small")(v)


BIG_INFO = {n: (shape, ax) for n, shape, ax in BIG}
GROUPS = (("w_in",), ("w_ffn_gate", "w_ffn_up", "w_ffn_down", "w_out", "w_ple_gate"),
          ("w_branch_a", "w_branch_b", "w_ple_proj"))


def _shard_shape(name):
    (k, m), ax = BIG_INFO[name]
    return (k // N_CHIPS, m) if ax == 0 else (k, m // N_CHIPS)


def _group_rows(group):
    offs, off = {}, 0
    for n in group:
        offs[n] = off
        off += _shard_shape(n)[0]
    return offs, off


def _pack_groups(shards, layer, dtype):
    return [jnp.concatenate([shards[n][layer].astype(dtype) for n in group], axis=0) for group in GROUPS]


def _unpack_full(gathered, groups):
    out = {}
    for group, arr in zip(groups, gathered):
        offs, _ = _group_rows(group)
        for n in group:
            rows, cols = _shard_shape(n)
            (k, m), ax = BIG_INFO[n]
            slab = arr[:, offs[n]:offs[n] + rows]
            out[n] = slab.reshape(k, m) if ax == 0 else jnp.transpose(slab, (1, 0, 2)).reshape(k, m)
    return out


def _pack_grads(gfull):
    out = []
    for group in GROUPS:
        parts = []
        for n in group:
            rows, cols = _shard_shape(n)
            ax = BIG_INFO[n][1]
            slab = (gfull[n].reshape(N_CHIPS, rows, cols) if ax == 0
                    else jnp.transpose(gfull[n].reshape(rows, N_CHIPS, cols), (1, 0, 2)))
            parts.append(slab)
        out.append(jnp.concatenate(parts, axis=1))
    return out


def _after(values, mark):
    values, _ = lax.optimization_barrier((values, mark))
    return values


def _reduce_scatter_begin(gsends, place, tag, ids):
    sibs = _sibling_halves(gsends, "rs_sibling_halves_" + tag, ids[0])
    tsends = [_chip_sums(g, s, place) for g, s in zip(gsends, sibs)]
    return tsends, _exchange_chip_sums(tsends, "rs_exchange_" + tag, ids[1])


def _reduce_scatter_finish(begun, place, tag, ids, hold):
    tsends, recvs = begun
    recvs = _after(recvs, hold)
    return _join_halves([_final_sum(t, r, place) for t, r in zip(tsends, recvs)], "rs_join_halves_" + tag, ids[2])


SMALL_SHAPES = {"rel_table": (NUM_BUCKETS, 2 * N_HEADS), "norm_mix_g": (DEPTH, D_MODEL), "qnorm_a_g": (DEPTH, HEAD_DIM),
                "knorm_a_g": (DEPTH, HEAD_DIM), "qnorm_b_g": (DEPTH, HEAD_DIM), "knorm_b_g": (DEPTH, HEAD_DIM),
                "sink_b": (DEPTH, N_HEADS), "norm_ffn_g": (DEPTH, D_MODEL), "norm_ple_g": (DEPTH, D_MODEL)}


def _pack_small(vals, last=None):
    flat = jnp.concatenate([vals[n].astype(F32).reshape(-1) for n in SMALL])
    tail = jnp.zeros((SMALL_ROWS * LANES - flat.shape[0],), F32)
    if last is not None:
        tail = tail.at[-1].set(last)
    return jnp.concatenate([flat, tail]).reshape(SMALL_ROWS, LANES)


def _unpack_small(packed):
    flat, out, off = packed.reshape(-1), {}, 0
    for n in SMALL:
        size = math.prod(SMALL_SHAPES[n])
        out[n] = flat[off:off + size].reshape(SMALL_SHAPES[n])
        off += size
    return out


def _adamw(w, gs, g_row, m, v, name):
    c1 = 1.0 - ADAM_B1 ** ADAM_STEP
    c2 = 1.0 - ADAM_B2 ** ADAM_STEP
    total, width = w.shape
    n_layers = len(gs)
    per = total // n_layers
    tm = max(t for t in range(8, 513, 8) if per % t == 0 and g_row % t == 0)
    nblk = per // tm

    def body(*refs):
        w_ref, g_refs = refs[0], refs[1:1 + n_layers]
        m_ref, v_ref, og, od, om, ov = refs[1 + n_layers:]
        layer = pl.program_id(0) // nblk
        g = g_refs[0][...]
        for l in range(1, n_layers):
            g = jnp.where(layer == l, g_refs[l][...], g)
        m_new = ADAM_B1 * m_ref[...] + (1.0 - ADAM_B1) * g
        v_new = ADAM_B2 * v_ref[...] + (1.0 - ADAM_B2) * (g * g)
        og[...] = g
        od[...] = -ADAM_LR * ((m_new / c1) / (jnp.sqrt(v_new / c2) + ADAM_EPS) + ADAM_WD * w_ref[...])
        om[...] = m_new
        ov[...] = v_new

    row = pl.BlockSpec((tm, width), lambda i: (i, 0))
    g_specs = [pl.BlockSpec((tm, width), lambda i, l=l: (g_row // tm + jnp.clip(i - l * nblk, 0, nblk - 1), 0))
               for l in range(n_layers)]
    return pl.pallas_call(
        body, out_shape=[jax.ShapeDtypeStruct((total, width), F32)] * 4, grid=(total // tm,),
        in_specs=[row] + g_specs + [row, row], out_specs=[row] * 4, name=name,
        compiler_params=_params("parallel"))(w, *gs, m, v)


def kernel(x, p, rel_table, norm_mix_g, w_in, qnorm_a_g, knorm_a_g, qnorm_b_g, knorm_b_g, sink_b, w_branch_a, w_branch_b, w_out, norm_ffn_g, w_ffn_gate, w_ffn_up, w_ffn_down, norm_ple_g, w_ple_gate, w_ple_proj, loss_target, m_rel_table, m_norm_mix_g, m_w_in, m_qnorm_a_g, m_knorm_a_g, m_qnorm_b_g, m_knorm_b_g, m_sink_b, m_w_branch_a, m_w_branch_b, m_w_out, m_norm_ffn_g, m_w_ffn_gate, m_w_ffn_up, m_w_ffn_down, m_norm_ple_g, m_w_ple_gate, m_w_ple_proj, v_rel_table, v_norm_mix_g, v_w_in, v_qnorm_a_g, v_knorm_a_g, v_qnorm_b_g, v_knorm_b_g, v_sink_b, v_w_branch_a, v_w_branch_b, v_w_out, v_norm_ffn_g, v_w_ffn_gate, v_w_ffn_up, v_w_ffn_down, v_norm_ple_g, v_w_ple_gate, v_w_ple_proj):
    given = dict(locals())

    def held(name, a):
        return jnp.swapaxes(a, 1, 2) if name in TRANSPOSED else a

    weights = {n: held(n, given[n]) for n in WEIGHTS}
    moments_m = {n: held(n, given["m_" + n]) for n in WEIGHTS}
    moments_v = {n: held(n, given["v_" + n]) for n in WEIGHTS}
    xi, yi, ci = _place()
    place = jnp.stack([2 * xi + yi, ci]).astype(jnp.int32)

    shards = [_pack_groups(weights, l, BF16) for l in range(DEPTH)]
    w_in0 = _allgather(shards[0][:1], "allgather_w_in_layer0", sequencer=9)
    rest0 = _allgather(_after(shards[0][1:], w_in0), "allgather_rest_layer0", sequencer=1)
    gathered = [w_in0 + rest0, None]
    small = {n: weights[n] for n in SMALL}

    def w_in_of(l, mark):
        return _unpack_full(_after(gathered[l][:1], (shards[1], mark) if l == 0 else mark), GROUPS[:1])["w_in"]

    def rest_of(l, mark):
        if l == 0:
            gathered[1] = _allgather(_after(shards[1], mark), "allgather_layer1", sequencer=2)
        return _unpack_full(_after(gathered[l][1:], mark), GROUPS[1:])

    loss, dx, gbig, gsmall, marks = _local_step(x[0], p[:, 0], loss_target[0], w_in_of, rest_of, small)

    gsends = [_pack_grads(gbig[l]) for l in range(DEPTH)]
    stages = {"layer1": (gsends[1], (3, 4, 5)), "rest_layer0": (gsends[0][1:], (6, 7, 8)),
              "w_in_layer0": (gsends[0][:1], (10, 11, 12))}
    begun = {tag: _reduce_scatter_begin(g, place, tag, ids) for tag, (g, ids) in stages.items()}

    def finish(tag, hold):
        return _reduce_scatter_finish(begun[tag], place, tag, stages[tag][1], hold)

    red1 = finish("layer1", marks[0]["attn_bwd_done"])
    rest0 = finish("rest_layer0", gbig[0]["w_in"])

    grads, delta, new_m, new_v = {}, {}, {}, {}

    def update(group, reduced):
        offs, _ = _group_rows(group)
        for n in group:
            shape = weights[n].shape
            two_d = lambda a: a.reshape(shape[0] * shape[1], shape[2])
            outs = _adamw(two_d(weights[n]), reduced, offs[n], two_d(moments_m[n]), two_d(moments_v[n]), "adamw_" + n)
            grads[n], delta[n], new_m[n], new_v[n] = (held(n, o.reshape(shape)) for o in outs)

    for gi in (1, 2):
        update(GROUPS[gi], _after([rest0[gi - 1], red1[gi]], begun["w_in_layer0"][0]))
    small_grads = _allreduce_small(_pack_small(gsmall, last=loss))
    g_, d_, m_, v_ = _adamw(_pack_small(weights), [small_grads], 0, _pack_small(moments_m), _pack_small(moments_v),
                            "adamw_small")
    grads.update(_unpack_small(g_))
    delta.update(_unpack_small(d_))
    new_m.update(_unpack_small(m_))
    new_v.update(_unpack_small(v_))
    others_done = [dx, d_] + [delta[n] for gi in (1, 2) for n in GROUPS[gi]]
    update(GROUPS[0], [finish("w_in_layer0", others_done)[0], red1[0]])

    return (small_grads[-1, -1], dx[None], *[grads[n] for n in WEIGHTS], *[delta[n] for n in WEIGHTS],
            *[new_m[n] for n in WEIGHTS], *[new_v[n] for n in WEIGHTS])
```

```python
import functools
import math

import jax
import jax.numpy as jnp
from jax import lax
from jax.experimental import pallas as pl
from jax.experimental.pallas import tpu as pltpu
from jax.experimental.pallas import tpu_sc as plsc

F32 = jnp.float32
BF16 = jnp.bfloat16
MESH_ID = pl.DeviceIdType.MESH

D_MODEL = 1024
DEPTH = 2
HEAD_DIM = 64
N_HEADS = 8
WIDTH = N_HEADS * HEAD_DIM
N_PAIRS = 4
ITEMS = 16
MAX_CHUNK = 1024
N_KV_B = 2
PLE_DIM = 256
D_FF = 2816
D_IN = 4352
OFF_QA, OFF_KA, OFF_VA, OFF_QB, OFF_KB, OFF_VB, OFF_GA, OFF_GB = 0, 512, 1024, 1536, 2048, 2176, 2304, 3328
DILATED = ((64, 1), (64, 4), (64, 16))
BLK_B = 128
NUM_BUCKETS = 32
MAX_DISTANCE = 1024
RMS_EPS = 1e-6
NEG_INF = -1e30
LANES = 128
ROW_TILE = 256
VMEM_LIMIT = 48 * 1024 * 1024

ADAM_LR, ADAM_B1, ADAM_B2, ADAM_EPS, ADAM_WD, ADAM_STEP = 0.001, 0.9, 0.999, 1e-08, 0.01, 10

TRANSPOSED = ("w_in", "w_ffn_gate", "w_ffn_up")
BIG = (
    ("w_in", (D_IN, D_MODEL), 0),
    ("w_branch_a", (WIDTH, D_MODEL), 1),
    ("w_branch_b", (WIDTH, D_MODEL), 1),
    ("w_out", (D_MODEL, D_MODEL), 0),
    ("w_ffn_gate", (D_FF, D_MODEL), 0),
    ("w_ffn_up", (D_FF, D_MODEL), 0),
    ("w_ffn_down", (D_FF, D_MODEL), 0),
    ("w_ple_gate", (D_MODEL, D_MODEL), 0),
    ("w_ple_proj", (PLE_DIM, D_MODEL), 1),
)
SMALL = ("rel_table", "norm_mix_g", "qnorm_a_g", "knorm_a_g", "qnorm_b_g", "knorm_b_g", "sink_b",
         "norm_ffn_g", "norm_ple_g")
WEIGHTS = ("rel_table", "norm_mix_g", "w_in", "qnorm_a_g", "knorm_a_g", "qnorm_b_g", "knorm_b_g", "sink_b",
           "w_branch_a", "w_branch_b", "w_out", "norm_ffn_g", "w_ffn_gate", "w_ffn_up", "w_ffn_down",
           "norm_ple_g", "w_ple_gate", "w_ple_proj")
N_CHIPS = 4
SMALL_ROWS = 64


def _params(*sem):
    return pltpu.CompilerParams(dimension_semantics=sem, vmem_limit_bytes=VMEM_LIMIT)


MM_VMEM_BUDGET = 40 * 1024 * 1024
STEP_OVERHEAD_S = 0.4e-6
TILE_DMA_BYTES_PER_S = 1.5e12


def _mm_dims(a, b, mode):
    if mode == "nn":
        return a.shape[0], b.shape[1], a.shape[1]
    if mode == "nt":
        return a.shape[0], b.shape[0], a.shape[1]
    return a.shape[1], b.shape[1], a.shape[0]


def _mm_tiles(m, n, pairs, tile_bytes, col_offsets, full_rows=False):
    best = None
    widths = [n] if full_rows else [t for t in range(LANES, n + 1, LANES) if n % t == 0 and all(o % t == 0 for o in col_offsets)]
    for tm in (t for t in range(LANES, m + 1, LANES) if m % t == 0):
        for tn in widths:
            io = sum(tm * k * ab + tn * k * bb for k, ab, bb in pairs) + tm * tn * sum(tile_bytes)
            casts = sum((tm * k * 2 if ab == 4 else 0) + (tn * k * 2 if bb == 4 else 0) for k, ab, bb in pairs)
            if 2 * io + len(pairs) * tm * tn * 4 + casts > MM_VMEM_BUDGET:
                continue
            cost = (m // tm) * (n // tn) * STEP_OVERHEAD_S + io / TILE_DMA_BYTES_PER_S
            if best is None or (cost, -tm) < best[0]:
                best = ((cost, -tm), tm, tn)
    return best[1], best[2]


def _mm_fused(pairs, extras, epilogue, out_dtypes, name, next_gain=None):
    m, n, _ = _mm_dims(*pairs[0])
    assert all(_mm_dims(*p)[:2] == (m, n) for p in pairs)
    with_norm = next_gain is not None
    out_dtypes = list(out_dtypes) + ([BF16] if with_norm else [])
    tm, tn = _mm_tiles(
        m, n, [(_mm_dims(a, b, mode)[2], a.dtype.itemsize, b.dtype.itemsize) for a, b, mode in pairs],
        [e.dtype.itemsize for e, _ in extras] + [jnp.dtype(d).itemsize for d in out_dtypes], [off for _, off in extras],
        full_rows=with_norm)
    dims = {"nn": (((1,), (0,)), ((), ())), "nt": (((1,), (1,)), ((), ())), "tn": (((0,), (0,)), ((), ()))}
    in_specs, args = [], []
    for a, b, mode in pairs:
        k = _mm_dims(a, b, mode)[2]
        in_specs.append(pl.BlockSpec((k, tm), lambda i, j: (0, i)) if mode == "tn" else pl.BlockSpec((tm, k), lambda i, j: (i, 0)))
        in_specs.append(pl.BlockSpec((tn, k), lambda i, j: (j, 0)) if mode == "nt" else pl.BlockSpec((k, tn), lambda i, j: (0, j)))
        args += [a, b]
    for e, off in extras:
        in_specs.append(pl.BlockSpec((tm, tn), lambda i, j, o=off // tn: (i, o + j)))
        args.append(e)
    n_pairs, n_tiles = len(pairs), 2 * len(pairs) + len(extras)
    if with_norm:
        in_specs.append(pl.BlockSpec((1, n), lambda i, j: (0, 0)))
        args.append(next_gain)
    n_in = len(args)

    def body(*refs):
        products = [lax.dot_general(refs[2 * p][...].astype(BF16), refs[2 * p + 1][...].astype(BF16), dims[pairs[p][2]],
                                    preferred_element_type=F32) for p in range(n_pairs)]
        outs = list(epilogue(products, [r[...] for r in refs[2 * n_pairs:n_tiles]]))
        if with_norm:
            y = outs[0]
            outs.append((y * lax.rsqrt(jnp.mean(y * y, axis=-1, keepdims=True) + RMS_EPS)) * refs[n_tiles][...])
        for r, o in zip(refs[n_in:], outs):
            r[...] = o.astype(r.dtype)

    tile = pl.BlockSpec((tm, tn), lambda i, j: (i, j))
    return pl.pallas_call(
        body, out_shape=[jax.ShapeDtypeStruct((m, n), d) for d in out_dtypes], grid=(m // tm, n // tn),
        in_specs=in_specs, out_specs=[tile] * len(out_dtypes), name=name,
        compiler_params=_params("parallel", "parallel"))(*args)


def _mm(a, b, mode, out_dtype, name, res=None):
    if res is None:
        return _mm_fused([(a, b, mode)], [], lambda products, extra: products, [out_dtype], name)[0]
    return _mm_fused([(a, b, mode)], [(res, 0)], lambda products, extra: [products[0] + extra[0]], [out_dtype], name)[0]


def _mm_res_norm(a, b, mode, res, gain, name):
    return _mm_fused([(a, b, mode)], [(res, 0)], lambda products, extra: [products[0] + extra[0]], [F32], name,
                     next_gain=gain)


def _ew(fn, ins, out_dtypes, name):
    rows, width = ins[0].shape
    n_in = len(ins)

    def body(*refs):
        outs = fn(*[r[...] for r in refs[:n_in]])
        for r, o in zip(refs[n_in:], outs):
            r[...] = o.astype(r.dtype)

    row = pl.BlockSpec((ROW_TILE, width), lambda i: (i, 0))
    return pl.pallas_call(
        body, out_shape=[jax.ShapeDtypeStruct((rows, width), dt) for dt in out_dtypes], grid=(rows // ROW_TILE,),
        in_specs=[row] * n_in, out_specs=[row] * len(out_dtypes), name=name, compiler_params=_params("parallel"))(*ins)


def _sigmoid(x):
    return 1.0 / (1.0 + jnp.exp(-x))


def _seg_sum(v):
    outs = []
    for k in range(v.shape[1] // LANES):
        vp = v[:, k * LANES:(k + 1) * LANES]
        left = lax.broadcasted_iota(jnp.int32, vp.shape, 1) < HEAD_DIM
        sl = jnp.sum(jnp.where(left, vp, 0.0), axis=-1, keepdims=True)
        sr = jnp.sum(jnp.where(left, 0.0, vp), axis=-1, keepdims=True)
        outs.append(jnp.where(left, sl, sr))
    return outs[0] if len(outs) == 1 else jnp.concatenate(outs, axis=1)


def _seg_rstd(x):
    return lax.rsqrt(_seg_sum(x * x) * (1.0 / HEAD_DIM) + RMS_EPS)


def _rms_fwd(x, g, name):
    rows, d = x.shape
    tm = ROW_TILE

    def body(x_ref, g_ref, h_ref):
        xv = x_ref[...]
        r = lax.rsqrt(jnp.mean(xv * xv, axis=-1, keepdims=True) + RMS_EPS)
        h_ref[...] = ((xv * r) * g_ref[...]).astype(BF16)

    return pl.pallas_call(
        body, out_shape=jax.ShapeDtypeStruct((rows, d), BF16), grid=(rows // tm,),
        in_specs=[pl.BlockSpec((tm, d), lambda i: (i, 0)), pl.BlockSpec((1, d), lambda i: (0, 0))],
        out_specs=pl.BlockSpec((tm, d), lambda i: (i, 0)), name=name,
        compiler_params=_params("parallel"))(x, g)


def _mm_rms_bwd(pairs, x, g, dres, name):
    rows, d = x.shape
    assert all(_mm_dims(*p)[:2] == (rows, d) for p in pairs)
    kbytes = [(_mm_dims(a, b, mode)[2], a.dtype.itemsize, b.dtype.itemsize) for a, b, mode in pairs]
    tm = max(t for t in (512, 256, 128) if rows % t == 0 and
             2 * (sum(t * k * ab + d * k * bb for k, ab, bb in kbytes) + t * d * 14) + 2 * t * d * 4 <= MM_VMEM_BUDGET)
    dims = {"nn": (((1,), (0,)), ((), ())), "nt": (((1,), (1,)), ((), ()))}
    n_pairs = len(pairs)

    def body(*refs):
        x_ref, g_ref, dres_ref = refs[2 * n_pairs:2 * n_pairs + 3]
        dx_ref, dxb_ref, dg_ref = refs[2 * n_pairs + 3:]
        dhv = functools.reduce(lambda u, v: u + v, [
            lax.dot_general(refs[2 * p][...].astype(BF16), refs[2 * p + 1][...].astype(BF16), dims[pairs[p][2]],
                            preferred_element_type=F32) for p in range(n_pairs)])
        xv = x_ref[...]
        r = lax.rsqrt(jnp.mean(xv * xv, axis=-1, keepdims=True) + RMS_EPS)
        xh = xv * r
        dxh = dhv * g_ref[...]
        dxv = dres_ref[...] + r * (dxh - xh * jnp.mean(dxh * xh, axis=-1, keepdims=True))
        dx_ref[...] = dxv
        dxb_ref[...] = dxv.astype(BF16)
        part = jnp.sum(dhv * xh, axis=0, keepdims=True)

        @pl.when(pl.program_id(0) == 0)
        def _():
            dg_ref[...] = part

        @pl.when(pl.program_id(0) > 0)
        def _():
            dg_ref[...] += part

    row = pl.BlockSpec((tm, d), lambda i: (i, 0))
    vec = pl.BlockSpec((1, d), lambda i: (0, 0))
    in_specs, args = [], []
    for a, b, mode in pairs:
        in_specs += [pl.BlockSpec((tm, a.shape[1]), lambda i: (i, 0)), pl.BlockSpec(b.shape, lambda i: (0, 0))]
        args += [a, b]
    return pl.pallas_call(
        body, out_shape=[jax.ShapeDtypeStruct((rows, d), F32), jax.ShapeDtypeStruct((rows, d), BF16),
                         jax.ShapeDtypeStruct((1, d), F32)],
        grid=(rows // tm,), in_specs=in_specs + [row, vec, row], out_specs=[row, row, vec],
        name=name, compiler_params=_params("arbitrary"))(*args, x, g, dres)


def _ple_bwd(dx, z, e):
    s = _sigmoid(z.astype(F32))
    return dx * s, dx * e.astype(F32) * (s * (1.0 - s))


def _loss_grad(y, t, z, e):
    rows, d = y.shape
    tm = ROW_TILE

    def body(y_ref, t_ref, z_ref, e_ref, dy_ref, de_ref, dz_ref, l_ref):
        err = y_ref[...] - t_ref[...]
        dy = err * (1.0 / d)
        dy_ref[...] = dy
        de, dz = _ple_bwd(dy, z_ref[...], e_ref[...])
        de_ref[...] = de.astype(BF16)
        dz_ref[...] = dz.astype(BF16)
        part = jnp.zeros((1, LANES), F32) + jnp.sum(err * err) * (0.5 / d)

        @pl.when(pl.program_id(0) == 0)
        def _():
            l_ref[...] = part

        @pl.when(pl.program_id(0) > 0)
        def _():
            l_ref[...] += part

    row = pl.BlockSpec((tm, d), lambda i: (i, 0))
    return pl.pallas_call(
        body, out_shape=[jax.ShapeDtypeStruct((rows, d), F32), jax.ShapeDtypeStruct((rows, d), BF16),
                         jax.ShapeDtypeStruct((rows, d), BF16), jax.ShapeDtypeStruct((1, LANES), F32)],
        grid=(rows // tm,), in_specs=[row] * 4, out_specs=[row, row, row, pl.BlockSpec((1, LANES), lambda i: (0, 0))],
        name="loss_grad", compiler_params=_params("arbitrary"))(y, t, z, e)


def _swap_halves(v):
    return pltpu.roll(v, HEAD_DIM, axis=1)


def _expand_kv(kv):
    left = lax.broadcasted_iota(jnp.int32, kv.shape, 1) < HEAD_DIM
    sw = _swap_halves(kv)
    h0 = jnp.where(left, kv, sw)
    h1 = jnp.where(left, sw, kv)
    return jnp.concatenate([h0, h0, h1, h1], axis=1)


def _reduce_kv(dkv):
    left = lax.broadcasted_iota(jnp.int32, (dkv.shape[0], LANES), 1) < HEAD_DIM
    t = dkv[:, 0:LANES] + dkv[:, LANES:2 * LANES]
    u = dkv[:, 2 * LANES:3 * LANES] + dkv[:, 3 * LANES:4 * LANES]
    t = t + _swap_halves(t)
    u = u + _swap_halves(u)
    return jnp.where(left, t, u)


def _qknorm_fwd(proj, gqa, gka, gqb, gkb):
    rows = proj.shape[0]
    tm = ROW_TILE

    def body(qa_ref, ka_ref, qb_ref, kb_ref, vb_ref, gqa_ref, gka_ref, gqb_ref, gkb_ref, oqa, oka, oqb, okb, ovb):
        for src, g_ref, dst in ((qa_ref, gqa_ref, oqa), (ka_ref, gka_ref, oka), (qb_ref, gqb_ref, oqb)):
            xv = src[...]
            dst[...] = (xv * _seg_rstd(xv)) * g_ref[...]
        kv = kb_ref[...]
        okb[...] = _expand_kv((kv * _seg_rstd(kv)) * gkb_ref[...])
        ovb[...] = _expand_kv(vb_ref[...])

    def win(width, off):
        return pl.BlockSpec((tm, width), lambda i: (i, off // width))

    vec = lambda w: pl.BlockSpec((1, w), lambda i: (0, 0))
    out = pl.BlockSpec((tm, WIDTH), lambda i: (i, 0))
    return pl.pallas_call(
        body, out_shape=[jax.ShapeDtypeStruct((rows, WIDTH), F32)] * 5, grid=(rows // tm,),
        in_specs=[win(WIDTH, OFF_QA), win(WIDTH, OFF_KA), win(WIDTH, OFF_QB), win(LANES, OFF_KB), win(LANES, OFF_VB),
                  vec(WIDTH), vec(WIDTH), vec(WIDTH), vec(LANES)],
        out_specs=[out] * 5, name="qknorm_fwd", compiler_params=_params("parallel"))(
            proj, proj, proj, proj, proj, gqa, gka, gqb, gkb)


def _norm_bwd(xv, g, dy):
    r = _seg_rstd(xv)
    xh = xv * r
    dxh = dy * g
    dx = r * (dxh - xh * (_seg_sum(dxh * xh) * (1.0 / HEAD_DIM)))
    return dx, jnp.sum(dy * xh, axis=0, keepdims=True)


def _qknorm_bwd(proj, gqa, gka, gqb, gkb, dqa, dka, dva, dqb, dkb, dvb, dga, dgb):
    rows = proj.shape[0]
    tm = ROW_TILE
    n_a = len(dqa)

    def body(*refs):
        qa_ref, ka_ref, qb_ref, kb_ref, gqa_ref, gka_ref, gqb_ref, gkb_ref = refs[:8]
        pos = 8
        dqa_refs, dka_refs, dva_refs = refs[pos:pos + n_a], refs[pos + n_a:pos + 2 * n_a], refs[pos + 2 * n_a:pos + 3 * n_a]
        pos += 3 * n_a
        dqb_ref, dkb_ref, dvb_ref, dga_ref, dgb_ref = refs[pos:pos + 5]
        dproj_ref, ogqa, ogka, ogqb, ogkb = refs[pos + 5:]

        def total(rs):
            acc = rs[0][...]
            for r in rs[1:]:
                acc = acc + r[...]
            return acc

        dx_qa, p_qa = _norm_bwd(qa_ref[...], gqa_ref[...], total(dqa_refs))
        dx_ka, p_ka = _norm_bwd(ka_ref[...], gka_ref[...], total(dka_refs))
        dx_qb, p_qb = _norm_bwd(qb_ref[...], gqb_ref[...], dqb_ref[...])
        dx_kb, p_kb = _norm_bwd(kb_ref[...], gkb_ref[...], _reduce_kv(dkb_ref[...]))
        dproj_ref[:, OFF_QA:OFF_QA + WIDTH] = dx_qa.astype(BF16)
        dproj_ref[:, OFF_KA:OFF_KA + WIDTH] = dx_ka.astype(BF16)
        dproj_ref[:, OFF_VA:OFF_VA + WIDTH] = total(dva_refs).astype(BF16)
        dproj_ref[:, OFF_QB:OFF_QB + WIDTH] = dx_qb.astype(BF16)
        dproj_ref[:, OFF_KB:OFF_KB + LANES] = dx_kb.astype(BF16)
        dproj_ref[:, OFF_VB:OFF_VB + LANES] = _reduce_kv(dvb_ref[...]).astype(BF16)
        dproj_ref[:, OFF_GA:OFF_GB] = dga_ref[...]
        dproj_ref[:, OFF_GB:D_IN] = dgb_ref[...]
        first = pl.program_id(0) == 0
        for o_ref, part in ((ogqa, p_qa), (ogka, p_ka), (ogqb, p_qb), (ogkb, p_kb)):
            @pl.when(first)
            def _(o_ref=o_ref, part=part):
                o_ref[...] = part

            @pl.when(jnp.logical_not(first))
            def _(o_ref=o_ref, part=part):
                o_ref[...] += part

    def win(width, off):
        return pl.BlockSpec((tm, width), lambda i: (i, off // width))

    vec = lambda w: pl.BlockSpec((1, w), lambda i: (0, 0))
    row = lambda w: pl.BlockSpec((tm, w), lambda i: (i, 0))
    in_specs = [win(WIDTH, OFF_QA), win(WIDTH, OFF_KA), win(WIDTH, OFF_QB), win(LANES, OFF_KB),
                vec(WIDTH), vec(WIDTH), vec(WIDTH), vec(LANES)]
    in_specs += [row(WIDTH)] * (3 * n_a + 3) + [row(D_MODEL)] * 2
    return pl.pallas_call(
        body,
        out_shape=[jax.ShapeDtypeStruct((rows, D_IN), BF16), jax.ShapeDtypeStruct((1, WIDTH), F32),
                   jax.ShapeDtypeStruct((1, WIDTH), F32), jax.ShapeDtypeStruct((1, WIDTH), F32),
                   jax.ShapeDtypeStruct((1, LANES), F32)],
        grid=(rows // tm,), in_specs=in_specs,
        out_specs=[row(D_IN), vec(WIDTH), vec(WIDTH), vec(WIDTH), vec(LANES)],
        name="qknorm_bwd", compiler_params=_params("arbitrary"))(
            proj, proj, proj, proj, gqa, gka, gqb, gkb, *dqa, *dka, *dva, dqb, dkb, dvb, dga, dgb)


def _t5_bucket(rel):
    half_b = NUM_BUCKETS // 2
    max_exact = half_b // 2
    sign = jnp.where(rel > 0, half_b, 0)
    n = jnp.abs(rel)
    nf = jnp.maximum(n, 1).astype(F32)
    large = max_exact + (jnp.log(nf / max_exact) / math.log(MAX_DISTANCE / max_exact)
                         * (half_b - max_exact)).astype(jnp.int32)
    large = jnp.minimum(large, half_b - 1)
    return sign + jnp.where(n < max_exact, n, large)


def _band_buckets(blk, dilation):
    i = jnp.arange(blk, dtype=jnp.int32)[:, None]
    j = jnp.arange(3 * blk, dtype=jnp.int32)[None, :]
    rel = j - blk - i
    return jnp.where(jnp.abs(rel) <= blk, _t5_bucket(rel * dilation), -1)


def _bias_tiles(table, buckets, head_off, name):
    blk = buckets.shape[0]

    def body(tab_ref, bk_ref, o_ref):
        h = pl.program_id(0) + head_off
        bk = bk_ref[...]
        acc = jnp.full(bk.shape, NEG_INF, F32)
        for b in range(NUM_BUCKETS):
            acc = jnp.where(bk == b, tab_ref[b, h], acc)
        o_ref[0] = acc

    return pl.pallas_call(
        body, out_shape=jax.ShapeDtypeStruct((N_HEADS, blk, 3 * blk), F32), grid=(N_HEADS,),
        in_specs=[pl.BlockSpec(memory_space=pltpu.SMEM), pl.BlockSpec((blk, 3 * blk), lambda h: (0, 0))],
        out_specs=pl.BlockSpec((1, blk, 3 * blk), lambda h: (h, 0, 0)),
        name=name, compiler_params=_params("parallel"))(table, buckets)


def _table_grad(dbias, buckets, name):
    blk = buckets.shape[0]

    def body(db_ref, bk_ref, o_ref):
        bk = bk_ref[...]
        dbv = db_ref[0]
        lane = lax.broadcasted_iota(jnp.int32, (1, LANES), 1)
        acc = jnp.zeros((1, LANES), F32)
        for b in range(NUM_BUCKETS):
            acc = jnp.where(lane == b, jnp.sum(jnp.where(bk == b, dbv, 0.0)), acc)
        o_ref[0] = acc

    out = pl.pallas_call(
        body, out_shape=jax.ShapeDtypeStruct((N_HEADS, 1, LANES), F32), grid=(N_HEADS,),
        in_specs=[pl.BlockSpec((1, blk, 3 * blk), lambda h: (h, 0, 0)), pl.BlockSpec((blk, 3 * blk), lambda h: (0, 0))],
        out_specs=pl.BlockSpec((1, 1, LANES), lambda h: (h, 0, 0)),
        name=name, compiler_params=_params("parallel"))(dbias, buckets)
    return out[:, 0, :NUM_BUCKETS]


def _dot_nt(a, b):
    return lax.dot_general(a, b, (((1,), (1,)), ((), ())), preferred_element_type=F32)


def _dot_tn(a, b):
    return lax.dot_general(a, b, (((0,), (0,)), ((), ())), preferred_element_type=F32)


def _stack_pair(x2, left):
    return jnp.concatenate([jnp.where(left, x2, 0.0), jnp.where(left, 0.0, x2)], axis=0).astype(BF16)


def _attn_geometry(blk, d):
    halo = blk * d
    subs = max(1, min(ITEMS // d, MAX_CHUNK // halo))
    return subs, min(d, ITEMS // subs), halo


def _item_of(j, r0, per_group):
    return j // per_group, r0 + j % per_group


def _span_rows(ref, first, r, blk, d):
    if d == 1:
        return ref[first:first + blk, :]
    return ref[pl.ds(first + r, blk, stride=d), :]


def _set_span_rows(ref, first, r, blk, d, val, add=False):
    idx = slice(first, first + blk) if d == 1 else pl.ds(first + r, blk, stride=d)
    ref[idx, :] = ref[idx, :] + val if add else val


def _for_groups(group, groups, per_group):
    if groups == 1:
        group(0)
    else:
        def step(g, carry):
            group(g * per_group)
            return carry

        lax.fori_loop(0, groups, step, 0)


def _key_rows(p_ref, c_ref, n_ref, s, r, subs, halo, blk, d):
    parts = []
    for span in (s - 1, s, s + 1):
        if span < 0:
            parts.append(_span_rows(p_ref, 0, r, blk, d))
        elif span == subs:
            parts.append(_span_rows(n_ref, 0, r, blk, d))
        else:
            parts.append(_span_rows(c_ref, span * halo, r, blk, d))
    return jnp.concatenate(parts, axis=0)


def _item_penalty(t, nct, s, subs, blk):
    first_ok = True if s > 0 else t > 0
    last_ok = True if s < subs - 1 else t < nct - 1
    col = lax.broadcasted_iota(jnp.int32, (1, 3 * blk), 1)
    ok = jnp.logical_and(jnp.logical_or(col >= blk, first_ok), jnp.logical_or(col < 2 * blk, last_ok))
    return jnp.where(ok, 0.0, NEG_INF).astype(F32)


def _attn_specs(seq, blk, d, step_of, col=0):
    subs, _, halo = _attn_geometry(blk, d)
    last, first = seq // halo - 1, col // LANES
    cur = pl.BlockSpec((subs * halo, LANES), lambda hp, t: (step_of(t), first + hp))
    prev = pl.BlockSpec((halo, LANES), lambda hp, t: (jnp.clip(step_of(t) * subs - 1, 0, last), first + hp))
    nxt = pl.BlockSpec((halo, LANES), lambda hp, t: (jnp.minimum((step_of(t) + 1) * subs, last), first + hp))
    return cur, prev, nxt


def _attn_fwd(q, k, v, bias, sink, blk, d, name, v_col=0):
    seq = q.shape[0]
    subs, per_group, halo = _attn_geometry(blk, d)
    items, chunk, groups = subs * per_group, subs * halo, d // per_group
    nct = seq // chunk
    has_sink = sink is not None
    scale = HEAD_DIM ** -0.5

    def body(*refs):
        q_ref, kp, kc, kn, vp, vc, vn, b_ref = refs[:8]
        s_ref = refs[8] if has_sink else None
        o_ref, l_ref = refs[-2], refs[-1]
        t = pl.program_id(1)
        left = lax.broadcasted_iota(jnp.int32, (1, LANES), 1) < HEAD_DIM
        bias2 = b_ref[...]

        def group(r0):
            scores, vcats = [], []
            for j in range(items):
                s, r = _item_of(j, r0, per_group)
                qs = _stack_pair(_span_rows(q_ref, s * halo, r, blk, d) * scale, left)
                kcat = _key_rows(kp, kc, kn, s, r, subs, halo, blk, d).astype(BF16)
                scores.append(_dot_nt(qs, kcat) + bias2 + _item_penalty(t, nct, s, subs, blk))
                vcats.append(_key_rows(vp, vc, vn, s, r, subs, halo, blk, d).astype(BF16))
            ms = [jnp.max(s, axis=-1, keepdims=True) for s in scores]
            if has_sink:
                sk = s_ref[...]
                ms = [jnp.maximum(m, sk) for m in ms]
            ps = [jnp.exp(s - m) for s, m in zip(scores, ms)]
            dens = [jnp.sum(p, axis=-1, keepdims=True) for p in ps]
            if has_sink:
                dens = [den + jnp.exp(sk - m) for den, m in zip(dens, ms)]
            pns = [(p * (1.0 / den)).astype(BF16) for p, den in zip(ps, dens)]
            lses = [m + jnp.log(den) for m, den in zip(ms, dens)]
            for j in range(items):
                s, r = _item_of(j, r0, per_group)
                o2 = jnp.dot(pns[j], vcats[j], preferred_element_type=F32)
                _set_span_rows(o_ref, s * halo, r, blk, d, jnp.where(left, o2[:blk], o2[blk:]))
                _set_span_rows(l_ref, s * halo, r, blk, d, jnp.where(left, lses[j][:blk], lses[j][blk:]))

        _for_groups(group, groups, per_group)

    cur, prev, nxt = _attn_specs(seq, blk, d, lambda t: t)
    v_cur, v_prev, v_nxt = _attn_specs(seq, blk, d, lambda t: t, v_col)
    in_specs = [cur, prev, cur, nxt, v_prev, v_cur, v_nxt, pl.BlockSpec((2 * blk, 3 * blk), lambda hp, t: (hp, 0))]
    args = [q, k, k, k, v, v, v, bias]
    if has_sink:
        in_specs.append(pl.BlockSpec((2 * blk, 1), lambda hp, t: (hp, 0)))
        args.append(sink)
    return pl.pallas_call(
        body, out_shape=[jax.ShapeDtypeStruct((seq, WIDTH), F32)] * 2, grid=(N_PAIRS, nct),
        in_specs=in_specs, out_specs=[cur, cur], name=name,
        compiler_params=_params("parallel", "parallel"))(*args)


def _attn_bwd(q, k, v, do, lse, delta, bias, sink, blk, d, name, v_col=0):
    seq = q.shape[0]
    subs, per_group, halo = _attn_geometry(blk, d)
    items, chunk, groups = subs * per_group, subs * halo, d // per_group
    nct = seq // chunk
    has_sink = sink is not None
    n_in = 12 if has_sink else 11
    scale = HEAD_DIM ** -0.5

    def body(*refs):
        q_ref, kp, kc, kn, vp, vc, vn, do_ref, l_ref, d_ref, b_ref = refs[:11]
        s_ref = refs[11] if has_sink else None
        dq_ref, dk_ref, dv_ref, db_ref = refs[n_in:n_in + 4]
        ds_ref = refs[n_in + 4] if has_sink else None
        wk, wv = refs[-2], refs[-1]
        t = pl.program_id(1)

        @pl.when(t == 0)
        def _():
            wk[...] = jnp.zeros_like(wk)
            wv[...] = jnp.zeros_like(wv)
            db_ref[...] = jnp.zeros_like(db_ref)
            if has_sink:
                ds_ref[...] = jnp.zeros_like(ds_ref)

        @pl.when(t > 0)
        def _():
            for w in (wk, wv):
                keep = w[chunk:2 * chunk + halo]
                w[0:chunk + halo] = keep
                w[chunk + halo:2 * chunk + halo] = jnp.zeros((chunk, LANES), F32)

        @pl.when(t < nct)
        def _():
            lane = lax.broadcasted_iota(jnp.int32, (1, LANES), 1)
            left = lane < HEAD_DIM
            bias2 = b_ref[...]

            def group(r0):
                qss, doss, kcats, scores, dps, lcols, dcols = [], [], [], [], [], [], []
                for j in range(items):
                    s, r = _item_of(j, r0, per_group)
                    qs = _stack_pair(_span_rows(q_ref, s * halo, r, blk, d) * scale, left)
                    dos = _stack_pair(_span_rows(do_ref, s * halo, r, blk, d), left)
                    kcat = _key_rows(kp, kc, kn, s, r, subs, halo, blk, d).astype(BF16)
                    vcat = _key_rows(vp, vc, vn, s, r, subs, halo, blk, d).astype(BF16)
                    l2, d2 = _span_rows(l_ref, s * halo, r, blk, d), _span_rows(d_ref, s * halo, r, blk, d)
                    lcols.append(jnp.concatenate([jnp.max(jnp.where(left, l2, NEG_INF), axis=-1, keepdims=True),
                                                  jnp.max(jnp.where(left, NEG_INF, l2), axis=-1, keepdims=True)], axis=0))
                    dcols.append(jnp.concatenate([jnp.sum(jnp.where(lane == 0, d2, 0.0), axis=-1, keepdims=True),
                                                  jnp.sum(jnp.where(lane == HEAD_DIM, d2, 0.0), axis=-1, keepdims=True)],
                                                 axis=0))
                    scores.append(_dot_nt(qs, kcat) + bias2 + _item_penalty(t, nct, s, subs, blk))
                    dps.append(_dot_nt(dos, vcat))
                    qss.append(qs)
                    doss.append(dos)
                    kcats.append(kcat)
                ps = [jnp.exp(s - lc) for s, lc in zip(scores, lcols)]
                dss = [p * (dp - dc) for p, dp, dc in zip(ps, dps, dcols)]
                db_ref[...] += functools.reduce(lambda a, b: a + b, dss)
                if has_sink:
                    sk = s_ref[...]
                    ds_ref[...] -= functools.reduce(lambda a, b: a + b, [dc * jnp.exp(sk - lc) for dc, lc in zip(dcols, lcols)])
                dsbs = [ds.astype(BF16) for ds in dss]
                for j in range(items):
                    s, r = _item_of(j, r0, per_group)
                    dq2 = jnp.dot(dsbs[j], kcats[j], preferred_element_type=F32) * scale
                    _set_span_rows(dq_ref, s * halo, r, blk, d, jnp.where(left, dq2[:blk], dq2[blk:]))
                news = [(_dot_tn(dsbs[j], qss[j]), _dot_tn(ps[j].astype(BF16), doss[j])) for j in range(items)]
                for which, w in enumerate((wk, wv)):
                    for jr in range(per_group):
                        for sp in range(-1, subs + 1):
                            parts = [news[s * per_group + jr][which][(sp - s + 1) * blk:(sp - s + 2) * blk]
                                     for s in range(subs) if 0 <= sp - s + 1 < 3]
                            _set_span_rows(w, chunk + sp * halo, r0 + jr, blk, d,
                                           functools.reduce(lambda x, y: x + y, parts), add=True)

            _for_groups(group, groups, per_group)

        dk_ref[...] = wk[0:chunk]
        dv_ref[...] = wv[0:chunk]

    cur, prev, nxt = _attn_specs(seq, blk, d, lambda t: jnp.minimum(t, nct - 1))
    v_cur, v_prev, v_nxt = _attn_specs(seq, blk, d, lambda t: jnp.minimum(t, nct - 1), v_col)
    lag = pl.BlockSpec((chunk, LANES), lambda hp, t: (jnp.maximum(t - 1, 0), hp))
    band = pl.BlockSpec((2 * blk, 3 * blk), lambda hp, t: (hp, 0))
    col = pl.BlockSpec((2 * blk, 1), lambda hp, t: (hp, 0))
    in_specs = [cur, prev, cur, nxt, v_prev, v_cur, v_nxt, cur, cur, cur, band]
    args = [q, k, k, k, v, v, v, do, lse, delta, bias]
    out_shape = [jax.ShapeDtypeStruct((seq, WIDTH), F32)] * 3 + [jax.ShapeDtypeStruct((N_HEADS * blk, 3 * blk), F32)]
    out_specs = [cur, lag, lag, band]
    if has_sink:
        in_specs.append(col)
        args.append(sink)
        out_shape.append(jax.ShapeDtypeStruct((N_HEADS * blk, 1), F32))
        out_specs.append(col)
    window = pltpu.VMEM((2 * chunk + halo, LANES), F32)
    return pl.pallas_call(
        body, out_shape=out_shape, grid=(N_PAIRS, nct + 1), in_specs=in_specs, out_specs=out_specs,
        scratch_shapes=[window, window], name=name,
        compiler_params=_params("arbitrary", "arbitrary"))(*args)


def _combine_patterns(outs, lses):
    def combine(*tiles):
        os_, ls = tiles[:len(outs)], tiles[len(outs):]
        m = functools.reduce(jnp.maximum, ls)
        es = [jnp.exp(l - m) for l in ls]
        den = functools.reduce(lambda a, b: a + b, es)
        num = functools.reduce(lambda a, b: a + b, [e * o for e, o in zip(es, os_)])
        return num / den, m + jnp.log(den)

    return _ew(combine, [*outs, *lses], [F32, F32], "combine_a")


def _tile_gain(g, reps):
    return jnp.tile(g[None, :], (1, reps))


def _local_step(x, p, target, w_in_of, rest_of, small):
    rel_table = small["rel_table"]
    buckets_a = [_band_buckets(blk, d) for blk, d in DILATED]
    buckets_b = _band_buckets(BLK_B, 1)
    bias_a = [_bias_tiles(rel_table, bk, 0, "bias_a").reshape(N_HEADS * bk.shape[0], -1) for bk in buckets_a]
    bias_b = _bias_tiles(rel_table, buckets_b, N_HEADS, "bias_b").reshape(N_HEADS * BLK_B, -1)

    saved = []
    for l in range(DEPTH):
        g_mix, g_ffn, g_ple = (small[n][l][None, :] for n in ("norm_mix_g", "norm_ffn_g", "norm_ple_g"))
        gqa, gka, gqb = (_tile_gain(small[n][l], N_HEADS) for n in ("qnorm_a_g", "knorm_a_g", "qnorm_b_g"))
        gkb = _tile_gain(small["knorm_b_g"][l], N_KV_B)
        sink = jnp.repeat(small["sink_b"][l], BLK_B)[:, None]

        h = _rms_fwd(x, g_mix, "rms_mix") if l == 0 else h_next
        w_in = w_in_of(l, (h, bias_a, bias_b))
        proj = _mm(h, w_in, "nt", F32, "mm_in")
        qa, ka, qb, kb, vb = _qknorm_fwd(proj, gqa, gka, gqb, gkb)
        outs, lses = [], []
        for (blk, d), bias in zip(DILATED, bias_a):
            o, ls = _attn_fwd(qa, ka, proj, bias, None, blk, d, f"attn_a{d}_fwd", v_col=OFF_VA)
            outs.append(o)
            lses.append(ls)
        ya, lse_a = _combine_patterns(outs, lses)
        yb, lse_b = _attn_fwd(qb, kb, vb, bias_b, sink, BLK_B, 1, "attn_b_fwd")
        w = dict(rest_of(l, yb), w_in=w_in)
        def gate(products, extra):
            (ca_, cb_), (ga_, gb_) = products, extra
            return _sigmoid(ga_) * ca_ + _sigmoid(gb_) * cb_, ca_, cb_

        merged, ca, cb = _mm_fused([(ya, w["w_branch_a"], "nn"), (yb, w["w_branch_b"], "nn")],
                                   [(proj, OFF_GA), (proj, OFF_GB)], gate, [BF16, BF16, BF16], "mm_branches_gate")
        x1, h2 = _mm_res_norm(merged, w["w_out"], "nn", x, g_ffn, "mm_out_norm")

        def swiglu(products, extra):
            a_, u_ = products
            return (a_ * _sigmoid(a_)) * u_, a_, u_

        hid, a, u = _mm_fused([(h2, w["w_ffn_gate"], "nt"), (h2, w["w_ffn_up"], "nt")], [], swiglu,
                              [BF16, BF16, BF16], "mm_ffn_gate_up")
        x2, h3 = _mm_res_norm(hid, w["w_ffn_down"], "nn", x1, g_ple, "mm_ffn_down_norm")

        def ple(products, extra):
            z_, e_ = products
            return extra[0] + _sigmoid(z_) * e_, z_, e_

        next_gain = small["norm_mix_g"][l + 1][None, :] if l + 1 < DEPTH else None
        x3, z, e, *rest = _mm_fused([(h3, w["w_ple_gate"], "nn"), (p[l], w["w_ple_proj"], "nn")], [(x2, 0)], ple,
                                    [F32, BF16, BF16], "mm_ple", next_gain=next_gain)
        h_next = rest[0] if rest else None
        saved.append(dict(w=w, x0=x, h=h, proj=proj, qa=qa, ka=ka, qb=qb, kb=kb, vb=vb, ya=ya, lse_a=lse_a,
                          yb=yb, lse_b=lse_b, ca=ca, cb=cb, merged=merged, x1=x1, h2=h2, a=a, u=u, hid=hid,
                          x2=x2, h3=h3, z=z, e=e))
        x = x3

    dx, de, dz, loss_acc = _loss_grad(x, target, saved[-1]["z"], saved[-1]["e"])
    loss = loss_acc[0, 0]

    gbig = [{} for _ in range(DEPTH)]
    marks = [{} for _ in range(DEPTH)]
    gsmall = {n: [None] * DEPTH for n in SMALL if n != "rel_table"}
    dbias_a = [[] for _ in DILATED]
    dbias_b = []

    for l in reversed(range(DEPTH)):
        sv = saved[l]
        w = sv["w"]
        g_mix, g_ffn, g_ple = (small[n][l][None, :] for n in ("norm_mix_g", "norm_ffn_g", "norm_ple_g"))
        gqa, gka, gqb = (_tile_gain(small[n][l], N_HEADS) for n in ("qnorm_a_g", "knorm_a_g", "qnorm_b_g"))
        gkb = _tile_gain(small["knorm_b_g"][l], N_KV_B)
        sink = jnp.repeat(small["sink_b"][l], BLK_B)[:, None]

        if l < DEPTH - 1:
            de, dz = _ew(_ple_bwd, [dx, sv["z"], sv["e"]], [BF16, BF16], "ple_bwd")
        gbig[l]["w_ple_proj"] = _mm(p[l], de, "tn", BF16, "mm_d_ple_proj")
        gbig[l]["w_ple_gate"] = _mm(sv["h3"], dz, "tn", BF16, "mm_d_ple_gate")
        dx, dxb, gsmall["norm_ple_g"][l] = _mm_rms_bwd([(dz, w["w_ple_gate"], "nt")], sv["x2"], g_ple, dx,
                                                      "mm_dh3_rms_bwd")

        gbig[l]["w_ffn_down"] = _mm(sv["hid"], dxb, "tn", BF16, "mm_d_ffn_down")

        def swiglu_bwd(products, extra):
            dh_, a_, u_ = products[0], extra[0].astype(F32), extra[1].astype(F32)
            s = _sigmoid(a_)
            return dh_ * u_ * (s * (1.0 + a_ * (1.0 - s))), dh_ * (a_ * s)

        da, du = _mm_fused([(dxb, w["w_ffn_down"], "nt")], [(sv["a"], 0), (sv["u"], 0)], swiglu_bwd, [BF16, BF16],
                           "mm_dhid_swiglu_bwd")
        gbig[l]["w_ffn_gate"], gbig[l]["w_ffn_up"] = _mm_fused(
            [(da, sv["h2"], "tn"), (du, sv["h2"], "tn")], [], lambda products, extra: products, [BF16, BF16],
            "mm_d_ffn_gate_up")
        dx, dxb, gsmall["norm_ffn_g"][l] = _mm_rms_bwd([(da, w["w_ffn_gate"], "nn"), (du, w["w_ffn_up"], "nn")],
                                                      sv["x1"], g_ffn, dx, "mm_dh2_rms_bwd")

        gbig[l]["w_out"] = _mm(sv["merged"], dxb, "tn", BF16, "mm_d_out")

        def gate_bwd(products, extra):
            dm_, ca_, cb_ = products[0], extra[0].astype(F32), extra[1].astype(F32)
            sa, sb = _sigmoid(extra[2]), _sigmoid(extra[3])
            return dm_ * sa, dm_ * sb, dm_ * ca_ * (sa * (1.0 - sa)), dm_ * cb_ * (sb * (1.0 - sb))

        dca, dcb, dga, dgb = _mm_fused(
            [(dxb, w["w_out"], "nt")], [(sv["ca"], 0), (sv["cb"], 0), (sv["proj"], OFF_GA), (sv["proj"], OFF_GB)],
            gate_bwd, [BF16, BF16, BF16, BF16], "mm_dmerged_gate_bwd")
        gbig[l]["w_branch_a"], gbig[l]["w_branch_b"] = _mm_fused(
            [(sv["ya"], dca, "tn"), (sv["yb"], dcb, "tn")], [], lambda products, extra: products, [BF16, BF16],
            "mm_d_branches")

        def with_row_dots(products, extra):
            (dya_, dyb_), (ya_, yb_) = products, extra
            return dya_, _seg_sum(dya_ * ya_), dyb_, _seg_sum(dyb_ * yb_)

        dya, delta_a, dyb, delta_b = _mm_fused(
            [(dca, w["w_branch_a"], "nt"), (dcb, w["w_branch_b"], "nt")], [(sv["ya"], 0), (sv["yb"], 0)], with_row_dots,
            [F32, F32, F32, F32], "mm_dy_branches")

        dqa, dka, dva = [], [], []
        for (blk, d), bias, bk in zip(DILATED, bias_a, buckets_a):
            dq_, dk_, dv_, db_ = _attn_bwd(sv["qa"], sv["ka"], sv["proj"], dya, sv["lse_a"], delta_a, bias, None, blk, d,
                                           f"attn_a{d}_bwd", v_col=OFF_VA)
            dqa.append(dq_)
            dka.append(dk_)
            dva.append(dv_)
            dbias_a[len(dqa) - 1].append(db_)
        dqb, dkb, dvb, db_, dsink = _attn_bwd(sv["qb"], sv["kb"], sv["vb"], dyb, sv["lse_b"], delta_b, bias_b, sink,
                                              BLK_B, 1, "attn_b_bwd")
        dbias_b.append(db_)
        gsmall["sink_b"][l] = dsink.reshape(N_HEADS, BLK_B).sum(axis=1)

        dproj, pqa, pka, pqb, pkb = _qknorm_bwd(sv["proj"], gqa, gka, gqb, gkb, dqa, dka, dva, dqb, dkb, dvb, dga, dgb)
        marks[l]["attn_bwd_done"] = dproj
        gsmall["qnorm_a_g"][l] = pqa.reshape(N_HEADS, HEAD_DIM).sum(0)
        gsmall["knorm_a_g"][l] = pka.reshape(N_HEADS, HEAD_DIM).sum(0)
        gsmall["qnorm_b_g"][l] = pqb.reshape(N_HEADS, HEAD_DIM).sum(0)
        gsmall["knorm_b_g"][l] = pkb.reshape(N_KV_B, HEAD_DIM).sum(0)
        gbig[l]["w_in"] = _mm(dproj, sv["h"], "tn", BF16, "mm_d_in")
        dx, _, gsmall["norm_mix_g"][l] = _mm_rms_bwd([(dproj, w["w_in"], "nn")], sv["x0"], g_mix, dx, "mm_dh_rms_bwd")
        gsmall["norm_mix_g"][l] = gsmall["norm_mix_g"][l][0]
        gsmall["norm_ffn_g"][l] = gsmall["norm_ffn_g"][l][0]
        gsmall["norm_ple_g"][l] = gsmall["norm_ple_g"][l][0]

    gsmall = {n: jnp.stack(v) for n, v in gsmall.items()}
    dtable_a = sum(_table_grad(sum(dbs).reshape(N_HEADS, blk, 3 * blk), bk, "table_grad_a")
                   for dbs, (blk, _), bk in zip(dbias_a, DILATED, buckets_a))
    dtable_b = _table_grad(sum(dbias_b).reshape(N_HEADS, BLK_B, 3 * BLK_B), buckets_b, "table_grad_b")
    gsmall["rel_table"] = jnp.concatenate([dtable_a, dtable_b], axis=0).T
    return loss, dx, gbig, gsmall, marks


def _place():
    return lax.axis_index("x"), lax.axis_index("y"), lax.axis_index("c")


def _flip(v, bit):
    return 1 - v if bit else v


CHIP_RELATIONS = ((0, 1), (1, 0), (1, 1))
ANY = pl.BlockSpec(memory_space=pl.ANY)


def _allgather_body(w_refs, out_refs, send_sems, recv_sems):
    x, y, c = _place()
    chips = [(_flip(x, a), _flip(y, b)) for a, b in CHIP_RELATIONS]

    def make(g):
        w_ref, out_ref = w_refs[g], out_refs[g]
        half = w_ref.shape[0] // 2

        def part(px, py, pc):
            return out_ref.at[2 * px + py, pl.ds(pc * half, half), :]

        def copy(k, block, to, src=None):
            return pltpu.make_async_remote_copy(
                src_ref=part(*block) if src is None else src, dst_ref=part(*block),
                send_sem=send_sems.at[7 * g + k], recv_sem=recv_sems.at[7 * g + k], device_id=to,
                device_id_type=MESH_ID)

        own = pltpu.make_async_remote_copy(
            src_ref=w_ref, dst_ref=out_ref.at[2 * x + y], send_sem=send_sems.at[7 * g + 6],
            recv_sem=recv_sems.at[7 * g + 6], device_id=(x, y, 1 - c), device_id_type=MESH_ID)
        first = [copy(k, (x, y, c), (*chip, c), src=w_ref.at[pl.ds(c * half, half), :]) for k, chip in enumerate(chips)]
        passed = [copy(3 + k, (*chip, c), (x, y, 1 - c)) for k, chip in enumerate(chips)]
        arrive = [copy(k, (*chip, c), (x, y, c)) for k, chip in enumerate(chips)]
        arrive2 = [copy(3 + k, (*chip, 1 - c), (x, y, c)) for k, chip in enumerate(chips)]
        return own, first, passed, arrive, arrive2

    made = [make(g) for g in range(len(w_refs))]
    for own, first, _, _, _ in made:
        own.start()
        for cp in first:
            cp.start()
    for _, _, passed, arrive, _ in made:
        for k in range(3):
            arrive[k].wait_recv()
            passed[k].start()
    for own, first, passed, _, arrive2 in made:
        for k in range(3):
            arrive2[k].wait_recv()
        own.wait_recv()
        for cp in first + passed + [own]:
            cp.wait_send()


def _sibling(x, y, c):
    return [(x, y, 1 - c)]


def _same_core_of_other_chips(x, y, c):
    return [(_flip(x, a), _flip(y, b), c) for a, b in CHIP_RELATIONS]


def _exchange(body, ins, out_types, n_sems, name, sequencer=None):
    n = len(ins)
    sems = (pltpu.SemaphoreType.DMA((n_sems,)), pltpu.SemaphoreType.DMA((n_sems,)))
    if sequencer is None:
        in_place = out_types is None
        out_shape = [jax.ShapeDtypeStruct(a.shape, a.dtype) for a in ins] if in_place else out_types

        def tc_body(*refs):
            body(refs[:n], refs[n:n + len(out_shape)], refs[-2], refs[-1])

        return list(pl.pallas_call(
            tc_body, out_shape=out_shape, in_specs=[ANY] * n, out_specs=[ANY] * len(out_shape),
            input_output_aliases={g: g for g in range(n)} if in_place else {}, scratch_shapes=list(sems), name=name)(*ins))

    collective_id, peers = sequencer
    hbm = pltpu.MemorySpace.HBM
    in_refs = [jax.new_ref(a, memory_space=hbm) for a in ins]
    out_refs = in_refs if out_types is None else [jax.empty_ref(t, memory_space=hbm) for t in out_types]

    @pl.kernel(mesh=plsc.ScalarSubcoreMesh(axis_name="sequencer", num_cores=1), name=name, scratch_types=sems,
               compiler_params=pltpu.CompilerParams(collective_id=collective_id))
    def launch(send_sems, recv_sems):
        barrier = pltpu.get_barrier_semaphore()
        devices = peers(*_place())
        for device in devices:
            pl.semaphore_signal(barrier, inc=1, device_id=device, device_id_type=MESH_ID)
        pl.semaphore_wait(barrier, len(devices))
        body(in_refs, out_refs, send_sems, recv_sems)

    launch()
    return [r[...] for r in out_refs]


def _allgather(shards, name, sequencer=None):
    out_types = [jax.ShapeDtypeStruct((N_CHIPS,) + s.shape, s.dtype) for s in shards]
    if sequencer is not None:
        sequencer = (sequencer, lambda x, y, c: _sibling(x, y, c) + _same_core_of_other_chips(x, y, c))
    return _exchange(_allgather_body, shards, out_types, 7 * len(shards), name, sequencer)


def _half_tile(half):
    return max(t for t in range(16, 1025, 16) if half % t == 0)


def _run_copies(cps):
    for cp in cps:
        cp.start()
    for cp in cps:
        cp.wait_recv()
    for cp in cps:
        cp.wait_send()


def _sibling_halves(gsends, name, sequencer=None):
    def body(g_refs, out_refs, send_sems, recv_sems):
        x, y, c = _place()
        cps = []
        for g, (g_ref, out_ref) in enumerate(zip(g_refs, out_refs)):
            half = g_ref.shape[1] // 2
            cps.append(pltpu.make_async_remote_copy(
                src_ref=g_ref.at[:, pl.ds((1 - c) * half, half), :], dst_ref=out_ref,
                send_sem=send_sems.at[g], recv_sem=recv_sems.at[g], device_id=(x, y, 1 - c), device_id_type=MESH_ID))
        _run_copies(cps)

    out_types = [jax.ShapeDtypeStruct((s.shape[0], s.shape[1] // 2, s.shape[2]), s.dtype) for s in gsends]
    return _exchange(body, gsends, out_types, len(gsends), name, sequencer and (sequencer, _sibling))


def _chip_sums(gsend, sib, place):
    n, rows, cols = gsend.shape
    half = rows // 2
    tm = _half_tile(half)
    nblk = half // tm

    def body(s_ref, g_ref, sib_ref, o_ref):
        o_ref[0] = (g_ref[0].astype(F32) + sib_ref[0].astype(F32)).astype(o_ref.dtype)

    grid_spec = pltpu.PrefetchScalarGridSpec(
        num_scalar_prefetch=1, grid=(n, nblk),
        in_specs=[pl.BlockSpec((1, tm, cols), lambda k, i, s: (jnp.bitwise_xor(s[0], k), s[1] * nblk + i, 0)),
                  pl.BlockSpec((1, tm, cols), lambda k, i, s: (jnp.bitwise_xor(s[0], k), i, 0))],
        out_specs=pl.BlockSpec((1, tm, cols), lambda k, i, s: (k, i, 0)))
    return pl.pallas_call(
        body, out_shape=jax.ShapeDtypeStruct((n, half, cols), BF16), grid_spec=grid_spec,
        name="rs_chip_sums", compiler_params=_params("parallel", "parallel"))(place, gsend, sib)


def _exchange_chip_sums(tsends, name, sequencer=None):
    def body(t_refs, out_refs, send_sems, recv_sems):
        x, y, c = _place()
        cps = []
        for g, (t_ref, out_ref) in enumerate(zip(t_refs, out_refs)):
            for k, device in enumerate(_same_core_of_other_chips(x, y, c)):
                cps.append(pltpu.make_async_remote_copy(
                    src_ref=t_ref.at[k + 1], dst_ref=out_ref.at[k], send_sem=send_sems.at[3 * g + k],
                    recv_sem=recv_sems.at[3 * g + k], device_id=device, device_id_type=MESH_ID))
        _run_copies(cps)

    out_types = [jax.ShapeDtypeStruct((3,) + s.shape[1:], s.dtype) for s in tsends]
    return _exchange(body, tsends, out_types, 3 * len(tsends), name,
                     sequencer and (sequencer, _same_core_of_other_chips))


def _final_sum(tsend, recv, place):
    n, half, cols = tsend.shape
    tm = _half_tile(half)
    nblk = half // tm

    def body(s_ref, t_ref, r_ref, o_ref):
        o_ref[...] = ((t_ref[0].astype(F32) + r_ref[0].astype(F32)) + r_ref[1].astype(F32)) + r_ref[2].astype(F32)

    grid_spec = pltpu.PrefetchScalarGridSpec(
        num_scalar_prefetch=1, grid=(nblk,),
        in_specs=[pl.BlockSpec((1, tm, cols), lambda i, s: (0, i, 0)), pl.BlockSpec((n - 1, tm, cols), lambda i, s: (0, i, 0))],
        out_specs=pl.BlockSpec((tm, cols), lambda i, s: (s[1] * nblk + i, 0)))
    return pl.pallas_call(
        body, out_shape=jax.ShapeDtypeStruct((2 * half, cols), F32), grid_spec=grid_spec, name="rs_final_sum",
        compiler_params=_params("parallel"))(place, tsend, recv)


def _join_halves(gfulls, name, sequencer=None):
    def body(g_refs, out_refs, send_sems, recv_sems):
        x, y, c = _place()
        n = len(g_refs)

        def copy(g, pc):
            half = g_refs[g].shape[0] // 2
            return pltpu.make_async_remote_copy(
                src_ref=g_refs[g].at[pl.ds(pc * half, half), :], dst_ref=out_refs[g].at[pl.ds(pc * half, half), :],
                send_sem=send_sems.at[g], recv_sem=recv_sems.at[g], device_id=(x, y, 1 - c), device_id_type=MESH_ID)

        mine = [copy(g, c) for g in range(n)]
        for cp in mine:
            cp.start()
        for g in range(n):
            copy(g, 1 - c).wait_recv()
        for cp in mine:
            cp.wait_send()

    return _exchange(body, gfulls, None, len(gfulls), name, sequencer and (sequencer, _sibling))


def _allreduce_small(v):
    rows, cols = v.shape

    def body(v_ref, out_ref, buf, send_sems, recv_sems):
        x, y, c = _place()
        cps = []
        for k in range(1, 8):
            peer = (_flip(x, (k >> 2) & 1), _flip(y, (k >> 1) & 1), _flip(c, k & 1))
            cps.append(pltpu.make_async_remote_copy(
                src_ref=v_ref, dst_ref=buf.at[k - 1], send_sem=send_sems.at[k - 1], recv_sem=recv_sems.at[k - 1],
                device_id=peer, device_id_type=MESH_ID))
        for cp in cps:
            cp.start()
        for cp in cps:
            cp.wait_recv()
        for cp in cps:
            cp.wait_send()
        t0 = v_ref[...] + buf[0]
        t1 = buf[1] + buf[2]
        t2 = buf[3] + buf[4]
        t3 = buf[5] + buf[6]
        out_ref[...] = (t0 + t1) + (t2 + t3)

    vm = pl.BlockSpec(memory_space=pltpu.VMEM)
    return pl.pallas_call(
        body, out_shape=jax.ShapeDtypeStruct((rows, cols), F32), in_specs=[vm], out_specs=vm,
        scratch_shapes=[pltpu.VMEM((7, rows, cols), F32), pltpu.SemaphoreType.DMA((7,)), pltpu.SemaphoreType.DMA((7,))],
        name="allreduce_small")(v)


BIG_INFO = {n: (shape, ax) for n, shape, ax in BIG}
GROUPS = (("w_in",), ("w_ffn_gate", "w_ffn_up", "w_ffn_down", "w_out", "w_ple_gate"),
          ("w_branch_a", "w_branch_b", "w_ple_proj"))


def _shard_shape(name):
    (k, m), ax = BIG_INFO[name]
    return (k // N_CHIPS, m) if ax == 0 else (k, m // N_CHIPS)


def _group_rows(group):
    offs, off = {}, 0
    for n in group:
        offs[n] = off
        off += _shard_shape(n)[0]
    return offs, off


def _pack_groups(shards, layer, dtype):
    return [jnp.concatenate([shards[n][layer].astype(dtype) for n in group], axis=0) for group in GROUPS]


def _unpack_full(gathered, groups):
    out = {}
    for group, arr in zip(groups, gathered):
        offs, _ = _group_rows(group)
        for n in group:
            rows, cols = _shard_shape(n)
            (k, m), ax = BIG_INFO[n]
            slab = arr[:, offs[n]:offs[n] + rows]
            out[n] = slab.reshape(k, m) if ax == 0 else jnp.transpose(slab, (1, 0, 2)).reshape(k, m)
    return out


def _pack_grads(gfull):
    out = []
    for group in GROUPS:
        parts = []
        for n in group:
            rows, cols = _shard_shape(n)
            ax = BIG_INFO[n][1]
            slab = (gfull[n].reshape(N_CHIPS, rows, cols) if ax == 0
                    else jnp.transpose(gfull[n].reshape(rows, N_CHIPS, cols), (1, 0, 2)))
            parts.append(slab)
        out.append(jnp.concatenate(parts, axis=1))
    return out


def _after(values, mark):
    values, _ = lax.optimization_barrier((values, mark))
    return values


def _reduce_scatter_begin(gsends, place, tag, ids):
    sibs = _sibling_halves(gsends, "rs_sibling_halves_" + tag, ids[0])
    tsends = [_chip_sums(g, s, place) for g, s in zip(gsends, sibs)]
    return tsends, _exchange_chip_sums(tsends, "rs_exchange_" + tag, ids[1])


def _reduce_scatter_finish(begun, place, tag, ids, hold):
    tsends, recvs = begun
    recvs = _after(recvs, hold)
    return _join_halves([_final_sum(t, r, place) for t, r in zip(tsends, recvs)], "rs_join_halves_" + tag, ids[2])


SMALL_SHAPES = {"rel_table": (NUM_BUCKETS, 2 * N_HEADS), "norm_mix_g": (DEPTH, D_MODEL), "qnorm_a_g": (DEPTH, HEAD_DIM),
                "knorm_a_g": (DEPTH, HEAD_DIM), "qnorm_b_g": (DEPTH, HEAD_DIM), "knorm_b_g": (DEPTH, HEAD_DIM),
                "sink_b": (DEPTH, N_HEADS), "norm_ffn_g": (DEPTH, D_MODEL), "norm_ple_g": (DEPTH, D_MODEL)}


def _pack_small(vals, last=None):
    flat = jnp.concatenate([vals[n].astype(F32).reshape(-1) for n in SMALL])
    tail = jnp.zeros((SMALL_ROWS * LANES - flat.shape[0],), F32)
    if last is not None:
        tail = tail.at[-1].set(last)
    return jnp.concatenate([flat, tail]).reshape(SMALL_ROWS, LANES)


def _unpack_small(packed):
    flat, out, off = packed.reshape(-1), {}, 0
    for n in SMALL:
        size = math.prod(SMALL_SHAPES[n])
        out[n] = flat[off:off + size].reshape(SMALL_SHAPES[n])
        off += size
    return out


def _adamw(w, gs, g_row, m, v, name):
    c1 = 1.0 - ADAM_B1 ** ADAM_STEP
    c2 = 1.0 - ADAM_B2 ** ADAM_STEP
    total, width = w.shape
    n_layers = len(gs)
    per = total // n_layers
    tm = max(t for t in range(8, 513, 8) if per % t == 0 and g_row % t == 0)
    nblk = per // tm

    def body(*refs):
        w_ref, g_refs = refs[0], refs[1:1 + n_layers]
        m_ref, v_ref, og, od, om, ov = refs[1 + n_layers:]
        layer = pl.program_id(0) // nblk
        g = g_refs[0][...]
        for l in range(1, n_layers):
            g = jnp.where(layer == l, g_refs[l][...], g)
        m_new = ADAM_B1 * m_ref[...] + (1.0 - ADAM_B1) * g
        v_new = ADAM_B2 * v_ref[...] + (1.0 - ADAM_B2) * (g * g)
        og[...] = g
        od[...] = -ADAM_LR * ((m_new / c1) / (jnp.sqrt(v_new / c2) + ADAM_EPS) + ADAM_WD * w_ref[...])
        om[...] = m_new
        ov[...] = v_new

    row = pl.BlockSpec((tm, width), lambda i: (i, 0))
    g_specs = [pl.BlockSpec((tm, width), lambda i, l=l: (g_row // tm + jnp.clip(i - l * nblk, 0, nblk - 1), 0))
               for l in range(n_layers)]
    return pl.pallas_call(
        body, out_shape=[jax.ShapeDtypeStruct((total, width), F32)] * 4, grid=(total // tm,),
        in_specs=[row] + g_specs + [row, row], out_specs=[row] * 4, name=name,
        compiler_params=_params("parallel"))(w, *gs, m, v)


def kernel(x, p, rel_table, norm_mix_g, w_in, qnorm_a_g, knorm_a_g, qnorm_b_g, knorm_b_g, sink_b, w_branch_a, w_branch_b, w_out, norm_ffn_g, w_ffn_gate, w_ffn_up, w_ffn_down, norm_ple_g, w_ple_gate, w_ple_proj, loss_target, m_rel_table, m_norm_mix_g, m_w_in, m_qnorm_a_g, m_knorm_a_g, m_qnorm_b_g, m_knorm_b_g, m_sink_b, m_w_branch_a, m_w_branch_b, m_w_out, m_norm_ffn_g, m_w_ffn_gate, m_w_ffn_up, m_w_ffn_down, m_norm_ple_g, m_w_ple_gate, m_w_ple_proj, v_rel_table, v_norm_mix_g, v_w_in, v_qnorm_a_g, v_knorm_a_g, v_qnorm_b_g, v_knorm_b_g, v_sink_b, v_w_branch_a, v_w_branch_b, v_w_out, v_norm_ffn_g, v_w_ffn_gate, v_w_ffn_up, v_w_ffn_down, v_norm_ple_g, v_w_ple_gate, v_w_ple_proj):
    given = dict(locals())

    def held(name, a):
        return jnp.swapaxes(a, 1, 2) if name in TRANSPOSED else a

    weights = {n: held(n, given[n]) for n in WEIGHTS}
    moments_m = {n: held(n, given["m_" + n]) for n in WEIGHTS}
    moments_v = {n: held(n, given["v_" + n]) for n in WEIGHTS}
    xi, yi, ci = _place()
    place = jnp.stack([2 * xi + yi, ci]).astype(jnp.int32)

    shards = [_pack_groups(weights, l, BF16) for l in range(DEPTH)]
    w_in0 = _allgather(shards[0][:1], "allgather_w_in_layer0", sequencer=9)
    rest0 = _allgather(_after(shards[0][1:], w_in0), "allgather_rest_layer0", sequencer=1)
    gathered = [w_in0 + rest0, None]
    small = {n: weights[n] for n in SMALL}

    def w_in_of(l, mark):
        return _unpack_full(_after(gathered[l][:1], (shards[1], mark) if l == 0 else mark), GROUPS[:1])["w_in"]

    def rest_of(l, mark):
        if l == 0:
            gathered[1] = _allgather(_after(shards[1], mark), "allgather_layer1", sequencer=2)
        return _unpack_full(_after(gathered[l][1:], mark), GROUPS[1:])

    loss, dx, gbig, gsmall, marks = _local_step(x[0], p[:, 0], loss_target[0], w_in_of, rest_of, small)

    gsends = [_pack_grads(gbig[l]) for l in range(DEPTH)]
    stages = {"layer1": (gsends[1], (3, 4, 5)), "rest_layer0": (gsends[0][1:], (6, 7, 8)),
              "w_in_layer0": (gsends[0][:1], (10, 11, 12))}
    begun = {tag: _reduce_scatter_begin(g, place, tag, ids) for tag, (g, ids) in stages.items()}

    def finish(tag, hold):
        return _reduce_scatter_finish(begun[tag], place, tag, stages[tag][1], hold)

    red1 = finish("layer1", marks[0]["attn_bwd_done"])
    rest0 = finish("rest_layer0", gbig[0]["w_in"])

    grads, delta, new_m, new_v = {}, {}, {}, {}

    def update(group, reduced):
        offs, _ = _group_rows(group)
        for n in group:
            shape = weights[n].shape
            two_d = lambda a: a.reshape(shape[0] * shape[1], shape[2])
            outs = _adamw(two_d(weights[n]), reduced, offs[n], two_d(moments_m[n]), two_d(moments_v[n]), "adamw_" + n)
            grads[n], delta[n], new_m[n], new_v[n] = (held(n, o.reshape(shape)) for o in outs)

    for gi in (1, 2):
        update(GROUPS[gi], _after([rest0[gi - 1], red1[gi]], begun["w_in_layer0"][0]))
    small_grads = _allreduce_small(_pack_small(gsmall, last=loss))
    g_, d_, m_, v_ = _adamw(_pack_small(weights), [small_grads], 0, _pack_small(moments_m), _pack_small(moments_v),
                            "adamw_small")
    grads.update(_unpack_small(g_))
    delta.update(_unpack_small(d_))
    new_m.update(_unpack_small(m_))
    new_v.update(_unpack_small(v_))
    others_done = [dx, d_] + [delta[n] for gi in (1, 2) for n in GROUPS[gi]]
    update(GROUPS[0], [finish("w_in_layer0", others_done)[0], red1[0]])

    return (small_grads[-1, -1], dx[None], *[grads[n] for n in WEIGHTS], *[delta[n] for n in WEIGHTS],
            *[new_m[n] for n in WEIGHTS], *[new_v[n] for n in WEIGHTS])
```

```python
import functools
import math

import jax
import jax.numpy as jnp
from jax import lax
from jax.experimental import pallas as pl
from jax.experimental.pallas import tpu as pltpu
from jax.experimental.pallas import tpu_sc as plsc

F32 = jnp.float32
BF16 = jnp.bfloat16
MESH_ID = pl.DeviceIdType.MESH

D_MODEL = 1024
DEPTH = 2
HEAD_DIM = 64
N_HEADS = 8
WIDTH = N_HEADS * HEAD_DIM
N_PAIRS = 4
ITEMS = 16
MAX_CHUNK = 1024
N_KV_B = 2
PLE_DIM = 256
D_FF = 2816
D_IN = 4352
OFF_QA, OFF_KA, OFF_VA, OFF_QB, OFF_KB, OFF_VB, OFF_GA, OFF_GB = 0, 512, 1024, 1536, 2048, 2176, 2304, 3328
DILATED = ((64, 1), (64, 4), (64, 16))
BLK_B = 128
NUM_BUCKETS = 32
MAX_DISTANCE = 1024
RMS_EPS = 1e-6
NEG_INF = -1e30
LANES = 128
ROW_TILE = 256
VMEM_LIMIT = 48 * 1024 * 1024

ADAM_LR, ADAM_B1, ADAM_B2, ADAM_EPS, ADAM_WD, ADAM_STEP = 0.001, 0.9, 0.999, 1e-08, 0.01, 10

TRANSPOSED = ("w_in", "w_ffn_gate", "w_ffn_up")
BIG = (
    ("w_in", (D_IN, D_MODEL), 0),
    ("w_branch_a", (WIDTH, D_MODEL), 1),
    ("w_branch_b", (WIDTH, D_MODEL), 1),
    ("w_out", (D_MODEL, D_MODEL), 0),
    ("w_ffn_gate", (D_FF, D_MODEL), 0),
    ("w_ffn_up", (D_FF, D_MODEL), 0),
    ("w_ffn_down", (D_FF, D_MODEL), 0),
    ("w_ple_gate", (D_MODEL, D_MODEL), 0),
    ("w_ple_proj", (PLE_DIM, D_MODEL), 1),
)
SMALL = ("rel_table", "norm_mix_g", "qnorm_a_g", "knorm_a_g", "qnorm_b_g", "knorm_b_g", "sink_b",
         "norm_ffn_g", "norm_ple_g")
WEIGHTS = ("rel_table", "norm_mix_g", "w_in", "qnorm_a_g", "knorm_a_g", "qnorm_b_g", "knorm_b_g", "sink_b",
           "w_branch_a", "w_branch_b", "w_out", "norm_ffn_g", "w_ffn_gate", "w_ffn_up", "w_ffn_down",
           "norm_ple_g", "w_ple_gate", "w_ple_proj")
N_CHIPS = 4
SMALL_ROWS = 64


def _params(*sem):
    return pltpu.CompilerParams(dimension_semantics=sem, vmem_limit_bytes=VMEM_LIMIT)


MM_VMEM_BUDGET = 40 * 1024 * 1024
STEP_OVERHEAD_S = 0.4e-6
TILE_DMA_BYTES_PER_S = 1.5e12


def _mm_dims(a, b, mode):
    if mode == "nn":
        return a.shape[0], b.shape[1], a.shape[1]
    if mode == "nt":
        return a.shape[0], b.shape[0], a.shape[1]
    return a.shape[1], b.shape[1], a.shape[0]


def _mm_tiles(m, n, pairs, tile_bytes, col_offsets, full_rows=False):
    best = None
    widths = [n] if full_rows else [t for t in range(LANES, n + 1, LANES) if n % t == 0 and all(o % t == 0 for o in col_offsets)]
    for tm in (t for t in range(LANES, m + 1, LANES) if m % t == 0):
        for tn in widths:
            io = sum(tm * k * ab + tn * k * bb for k, ab, bb in pairs) + tm * tn * sum(tile_bytes)
            casts = sum((tm * k * 2 if ab == 4 else 0) + (tn * k * 2 if bb == 4 else 0) for k, ab, bb in pairs)
            if 2 * io + len(pairs) * tm * tn * 4 + casts > MM_VMEM_BUDGET:
                continue
            cost = (m // tm) * (n // tn) * STEP_OVERHEAD_S + io / TILE_DMA_BYTES_PER_S
            if best is None or (cost, -tm) < best[0]:
                best = ((cost, -tm), tm, tn)
    return best[1], best[2]


def _mm_fused(pairs, extras, epilogue, out_dtypes, name, next_gain=None):
    m, n, _ = _mm_dims(*pairs[0])
    assert all(_mm_dims(*p)[:2] == (m, n) for p in pairs)
    with_norm = next_gain is not None
    out_dtypes = list(out_dtypes) + ([BF16] if with_norm else [])
    tm, tn = _mm_tiles(
        m, n, [(_mm_dims(a, b, mode)[2], a.dtype.itemsize, b.dtype.itemsize) for a, b, mode in pairs],
        [e.dtype.itemsize for e, _ in extras] + [jnp.dtype(d).itemsize for d in out_dtypes], [off for _, off in extras],
        full_rows=with_norm)
    dims = {"nn": (((1,), (0,)), ((), ())), "nt": (((1,), (1,)), ((), ())), "tn": (((0,), (0,)), ((), ()))}
    in_specs, args = [], []
    for a, b, mode in pairs:
        k = _mm_dims(a, b, mode)[2]
        in_specs.append(pl.BlockSpec((k, tm), lambda i, j: (0, i)) if mode == "tn" else pl.BlockSpec((tm, k), lambda i, j: (i, 0)))
        in_specs.append(pl.BlockSpec((tn, k), lambda i, j: (j, 0)) if mode == "nt" else pl.BlockSpec((k, tn), lambda i, j: (0, j)))
        args += [a, b]
    for e, off in extras:
        in_specs.append(pl.BlockSpec((tm, tn), lambda i, j, o=off // tn: (i, o + j)))
        args.append(e)
    n_pairs, n_tiles = len(pairs), 2 * len(pairs) + len(extras)
    if with_norm:
        in_specs.append(pl.BlockSpec((1, n), lambda i, j: (0, 0)))
        args.append(next_gain)
    n_in = len(args)

    def body(*refs):
        products = [lax.dot_general(refs[2 * p][...].astype(BF16), refs[2 * p + 1][...].astype(BF16), dims[pairs[p][2]],
                                    preferred_element_type=F32) for p in range(n_pairs)]
        outs = list(epilogue(products, [r[...] for r in refs[2 * n_pairs:n_tiles]]))
        if with_norm:
            y = outs[0]
            outs.append((y * lax.rsqrt(jnp.mean(y * y, axis=-1, keepdims=True) + RMS_EPS)) * refs[n_tiles][...])
        for r, o in zip(refs[n_in:], outs):
            r[...] = o.astype(r.dtype)

    tile = pl.BlockSpec((tm, tn), lambda i, j: (i, j))
    return pl.pallas_call(
        body, out_shape=[jax.ShapeDtypeStruct((m, n), d) for d in out_dtypes], grid=(m // tm, n // tn),
        in_specs=in_specs, out_specs=[tile] * len(out_dtypes), name=name,
        compiler_params=_params("parallel", "parallel"))(*args)


def _mm(a, b, mode, out_dtype, name, res=None):
    if res is None:
        return _mm_fused([(a, b, mode)], [], lambda products, extra: products, [out_dtype], name)[0]
    return _mm_fused([(a, b, mode)], [(res, 0)], lambda products, extra: [products[0] + extra[0]], [out_dtype], name)[0]


def _mm_res_norm(a, b, mode, res, gain, name):
    return _mm_fused([(a, b, mode)], [(res, 0)], lambda products, extra: [products[0] + extra[0]], [F32], name,
                     next_gain=gain)


def _ew(fn, ins, out_dtypes, name):
    rows, width = ins[0].shape
    n_in = len(ins)

    def body(*refs):
        outs = fn(*[r[...] for r in refs[:n_in]])
        for r, o in zip(refs[n_in:], outs):
            r[...] = o.astype(r.dtype)

    row = pl.BlockSpec((ROW_TILE, width), lambda i: (i, 0))
    return pl.pallas_call(
        body, out_shape=[jax.ShapeDtypeStruct((rows, width), dt) for dt in out_dtypes], grid=(rows // ROW_TILE,),
        in_specs=[row] * n_in, out_specs=[row] * len(out_dtypes), name=name, compiler_params=_params("parallel"))(*ins)


def _sigmoid(x):
    return 1.0 / (1.0 + jnp.exp(-x))


def _seg_sum(v):
    outs = []
    for k in range(v.shape[1] // LANES):
        vp = v[:, k * LANES:(k + 1) * LANES]
        left = lax.broadcasted_iota(jnp.int32, vp.shape, 1) < HEAD_DIM
        sl = jnp.sum(jnp.where(left, vp, 0.0), axis=-1, keepdims=True)
        sr = jnp.sum(jnp.where(left, 0.0, vp), axis=-1, keepdims=True)
        outs.append(jnp.where(left, sl, sr))
    return outs[0] if len(outs) == 1 else jnp.concatenate(outs, axis=1)


def _seg_rstd(x):
    return lax.rsqrt(_seg_sum(x * x) * (1.0 / HEAD_DIM) + RMS_EPS)


def _rms_fwd(x, g, name):
    rows, d = x.shape
    tm = ROW_TILE

    def body(x_ref, g_ref, h_ref):
        xv = x_ref[...]
        r = lax.rsqrt(jnp.mean(xv * xv, axis=-1, keepdims=True) + RMS_EPS)
        h_ref[...] = ((xv * r) * g_ref[...]).astype(BF16)

    return pl.pallas_call(
        body, out_shape=jax.ShapeDtypeStruct((rows, d), BF16), grid=(rows // tm,),
        in_specs=[pl.BlockSpec((tm, d), lambda i: (i, 0)), pl.BlockSpec((1, d), lambda i: (0, 0))],
        out_specs=pl.BlockSpec((tm, d), lambda i: (i, 0)), name=name,
        compiler_params=_params("parallel"))(x, g)


def _mm_rms_bwd(pairs, x, g, dres, name):
    rows, d = x.shape
    assert all(_mm_dims(*p)[:2] == (rows, d) for p in pairs)
    kbytes = [(_mm_dims(a, b, mode)[2], a.dtype.itemsize, b.dtype.itemsize) for a, b, mode in pairs]
    tm = max(t for t in (512, 256, 128) if rows % t == 0 and
             2 * (sum(t * k * ab + d * k * bb for k, ab, bb in kbytes) + t * d * 14) + 2 * t * d * 4 <= MM_VMEM_BUDGET)
    dims = {"nn": (((1,), (0,)), ((), ())), "nt": (((1,), (1,)), ((), ()))}
    n_pairs = len(pairs)

    def body(*refs):
        x_ref, g_ref, dres_ref = refs[2 * n_pairs:2 * n_pairs + 3]
        dx_ref, dxb_ref, dg_ref = refs[2 * n_pairs + 3:]
        dhv = functools.reduce(lambda u, v: u + v, [
            lax.dot_general(refs[2 * p][...].astype(BF16), refs[2 * p + 1][...].astype(BF16), dims[pairs[p][2]],
                            preferred_element_type=F32) for p in range(n_pairs)])
        xv = x_ref[...]
        r = lax.rsqrt(jnp.mean(xv * xv, axis=-1, keepdims=True) + RMS_EPS)
        xh = xv * r
        dxh = dhv * g_ref[...]
        dxv = dres_ref[...] + r * (dxh - xh * jnp.mean(dxh * xh, axis=-1, keepdims=True))
        dx_ref[...] = dxv
        dxb_ref[...] = dxv.astype(BF16)
        part = jnp.sum(dhv * xh, axis=0, keepdims=True)

        @pl.when(pl.program_id(0) == 0)
        def _():
            dg_ref[...] = part

        @pl.when(pl.program_id(0) > 0)
        def _():
            dg_ref[...] += part

    row = pl.BlockSpec((tm, d), lambda i: (i, 0))
    vec = pl.BlockSpec((1, d), lambda i: (0, 0))
    in_specs, args = [], []
    for a, b, mode in pairs:
        in_specs += [pl.BlockSpec((tm, a.shape[1]), lambda i: (i, 0)), pl.BlockSpec(b.shape, lambda i: (0, 0))]
        args += [a, b]
    return pl.pallas_call(
        body, out_shape=[jax.ShapeDtypeStruct((rows, d), F32), jax.ShapeDtypeStruct((rows, d), BF16),
                         jax.ShapeDtypeStruct((1, d), F32)],
        grid=(rows // tm,), in_specs=in_specs + [row, vec, row], out_specs=[row, row, vec],
        name=name, compiler_params=_params("arbitrary"))(*args, x, g, dres)


def _ple_bwd(dx, z, e):
    s = _sigmoid(z.astype(F32))
    return dx * s, dx * e.astype(F32) * (s * (1.0 - s))


def _loss_grad(y, t, z, e):
    rows, d = y.shape
    tm = ROW_TILE

    def body(y_ref, t_ref, z_ref, e_ref, dy_ref, de_ref, dz_ref, l_ref):
        err = y_ref[...] - t_ref[...]
        dy = err * (1.0 / d)
        dy_ref[...] = dy
        de, dz = _ple_bwd(dy, z_ref[...], e_ref[...])
        de_ref[...] = de.astype(BF16)
        dz_ref[...] = dz.astype(BF16)
        part = jnp.zeros((1, LANES), F32) + jnp.sum(err * err) * (0.5 / d)

        @pl.when(pl.program_id(0) == 0)
        def _():
            l_ref[...] = part

        @pl.when(pl.program_id(0) > 0)
        def _():
            l_ref[...] += part

    row = pl.BlockSpec((tm, d), lambda i: (i, 0))
    return pl.pallas_call(
        body, out_shape=[jax.ShapeDtypeStruct((rows, d), F32), jax.ShapeDtypeStruct((rows, d), BF16),
                         jax.ShapeDtypeStruct((rows, d), BF16), jax.ShapeDtypeStruct((1, LANES), F32)],
        grid=(rows // tm,), in_specs=[row] * 4, out_specs=[row, row, row, pl.BlockSpec((1, LANES), lambda i: (0, 0))],
        name="loss_grad", compiler_params=_params("arbitrary"))(y, t, z, e)


def _swap_halves(v):
    return pltpu.roll(v, HEAD_DIM, axis=1)


def _expand_kv(kv):
    left = lax.broadcasted_iota(jnp.int32, kv.shape, 1) < HEAD_DIM
    sw = _swap_halves(kv)
    h0 = jnp.where(left, kv, sw)
    h1 = jnp.where(left, sw, kv)
    return jnp.concatenate([h0, h0, h1, h1], axis=1)


def _reduce_kv(dkv):
    left = lax.broadcasted_iota(jnp.int32, (dkv.shape[0], LANES), 1) < HEAD_DIM
    t = dkv[:, 0:LANES] + dkv[:, LANES:2 * LANES]
    u = dkv[:, 2 * LANES:3 * LANES] + dkv[:, 3 * LANES:4 * LANES]
    t = t + _swap_halves(t)
    u = u + _swap_halves(u)
    return jnp.where(left, t, u)


def _qknorm_fwd(proj, gqa, gka, gqb, gkb):
    rows = proj.shape[0]
    tm = ROW_TILE

    def body(qa_ref, ka_ref, qb_ref, kb_ref, vb_ref, gqa_ref, gka_ref, gqb_ref, gkb_ref, oqa, oka, oqb, okb, ovb):
        for src, g_ref, dst in ((qa_ref, gqa_ref, oqa), (ka_ref, gka_ref, oka), (qb_ref, gqb_ref, oqb)):
            xv = src[...]
            dst[...] = (xv * _seg_rstd(xv)) * g_ref[...]
        kv = kb_ref[...]
        okb[...] = _expand_kv((kv * _seg_rstd(kv)) * gkb_ref[...])
        ovb[...] = _expand_kv(vb_ref[...])

    def win(width, off):
        return pl.BlockSpec((tm, width), lambda i: (i, off // width))

    vec = lambda w: pl.BlockSpec((1, w), lambda i: (0, 0))
    out = pl.BlockSpec((tm, WIDTH), lambda i: (i, 0))
    return pl.pallas_call(
        body, out_shape=[jax.ShapeDtypeStruct((rows, WIDTH), F32)] * 5, grid=(rows // tm,),
        in_specs=[win(WIDTH, OFF_QA), win(WIDTH, OFF_KA), win(WIDTH, OFF_QB), win(LANES, OFF_KB), win(LANES, OFF_VB),
                  vec(WIDTH), vec(WIDTH), vec(WIDTH), vec(LANES)],
        out_specs=[out] * 5, name="qknorm_fwd", compiler_params=_params("parallel"))(
            proj, proj, proj, proj, proj, gqa, gka, gqb, gkb)


def _norm_bwd(xv, g, dy):
    r = _seg_rstd(xv)
    xh = xv * r
    dxh = dy * g
    dx = r * (dxh - xh * (_seg_sum(dxh * xh) * (1.0 / HEAD_DIM)))
    return dx, jnp.sum(dy * xh, axis=0, keepdims=True)


def _qknorm_bwd(proj, gqa, gka, gqb, gkb, dqa, dka, dva, dqb, dkb, dvb, dga, dgb):
    rows = proj.shape[0]
    tm = ROW_TILE
    n_a = len(dqa)

    def body(*refs):
        qa_ref, ka_ref, qb_ref, kb_ref, gqa_ref, gka_ref, gqb_ref, gkb_ref = refs[:8]
        pos = 8
        dqa_refs, dka_refs, dva_refs = refs[pos:pos + n_a], refs[pos + n_a:pos + 2 * n_a], refs[pos + 2 * n_a:pos + 3 * n_a]
        pos += 3 * n_a
        dqb_ref, dkb_ref, dvb_ref, dga_ref, dgb_ref = refs[pos:pos + 5]
        dproj_ref, ogqa, ogka, ogqb, ogkb = refs[pos + 5:]

        def total(rs):
            acc = rs[0][...]
            for r in rs[1:]:
                acc = acc + r[...]
            return acc

        dx_qa, p_qa = _norm_bwd(qa_ref[...], gqa_ref[...], total(dqa_refs))
        dx_ka, p_ka = _norm_bwd(ka_ref[...], gka_ref[...], total(dka_refs))
        dx_qb, p_qb = _norm_bwd(qb_ref[...], gqb_ref[...], dqb_ref[...])
        dx_kb, p_kb = _norm_bwd(kb_ref[...], gkb_ref[...], _reduce_kv(dkb_ref[...]))
        dproj_ref[:, OFF_QA:OFF_QA + WIDTH] = dx_qa.astype(BF16)
        dproj_ref[:, OFF_KA:OFF_KA + WIDTH] = dx_ka.astype(BF16)
        dproj_ref[:, OFF_VA:OFF_VA + WIDTH] = total(dva_refs).astype(BF16)
        dproj_ref[:, OFF_QB:OFF_QB + WIDTH] = dx_qb.astype(BF16)
        dproj_ref[:, OFF_KB:OFF_KB + LANES] = dx_kb.astype(BF16)
        dproj_ref[:, OFF_VB:OFF_VB + LANES] = _reduce_kv(dvb_ref[...]).astype(BF16)
        dproj_ref[:, OFF_GA:OFF_GB] = dga_ref[...]
        dproj_ref[:, OFF_GB:D_IN] = dgb_ref[...]
        first = pl.program_id(0) == 0
        for o_ref, part in ((ogqa, p_qa), (ogka, p_ka), (ogqb, p_qb), (ogkb, p_kb)):
            @pl.when(first)
            def _(o_ref=o_ref, part=part):
                o_ref[...] = part

            @pl.when(jnp.logical_not(first))
            def _(o_ref=o_ref, part=part):
                o_ref[...] += part

    def win(width, off):
        return pl.BlockSpec((tm, width), lambda i: (i, off // width))

    vec = lambda w: pl.BlockSpec((1, w), lambda i: (0, 0))
    row = lambda w: pl.BlockSpec((tm, w), lambda i: (i, 0))
    in_specs = [win(WIDTH, OFF_QA), win(WIDTH, OFF_KA), win(WIDTH, OFF_QB), win(LANES, OFF_KB),
                vec(WIDTH), vec(WIDTH), vec(WIDTH), vec(LANES)]
    in_specs += [row(WIDTH)] * (3 * n_a + 3) + [row(D_MODEL)] * 2
    return pl.pallas_call(
        body,
        out_shape=[jax.ShapeDtypeStruct((rows, D_IN), BF16), jax.ShapeDtypeStruct((1, WIDTH), F32),
                   jax.ShapeDtypeStruct((1, WIDTH), F32), jax.ShapeDtypeStruct((1, WIDTH), F32),
                   jax.ShapeDtypeStruct((1, LANES), F32)],
        grid=(rows // tm,), in_specs=in_specs,
        out_specs=[row(D_IN), vec(WIDTH), vec(WIDTH), vec(WIDTH), vec(LANES)],
        name="qknorm_bwd", compiler_params=_params("arbitrary"))(
            proj, proj, proj, proj, gqa, gka, gqb, gkb, *dqa, *dka, *dva, dqb, dkb, dvb, dga, dgb)


def _t5_bucket(rel):
    half_b = NUM_BUCKETS // 2
    max_exact = half_b // 2
    sign = jnp.where(rel > 0, half_b, 0)
    n = jnp.abs(rel)
    nf = jnp.maximum(n, 1).astype(F32)
    large = max_exact + (jnp.log(nf / max_exact) / math.log(MAX_DISTANCE / max_exact)
                         * (half_b - max_exact)).astype(jnp.int32)
    large = jnp.minimum(large, half_b - 1)
    return sign + jnp.where(n < max_exact, n, large)


def _band_buckets(blk, dilation):
    i = jnp.arange(blk, dtype=jnp.int32)[:, None]
    j = jnp.arange(3 * blk, dtype=jnp.int32)[None, :]
    rel = j - blk - i
    return jnp.where(jnp.abs(rel) <= blk, _t5_bucket(rel * dilation), -1)


def _bias_tiles(table, buckets, head_off, name):
    blk = buckets.shape[0]

    def body(tab_ref, bk_ref, o_ref):
        h = pl.program_id(0) + head_off
        bk = bk_ref[...]
        acc = jnp.full(bk.shape, NEG_INF, F32)
        for b in range(NUM_BUCKETS):
            acc = jnp.where(bk == b, tab_ref[b, h], acc)
        o_ref[0] = acc

    return pl.pallas_call(
        body, out_shape=jax.ShapeDtypeStruct((N_HEADS, blk, 3 * blk), F32), grid=(N_HEADS,),
        in_specs=[pl.BlockSpec(memory_space=pltpu.SMEM), pl.BlockSpec((blk, 3 * blk), lambda h: (0, 0))],
        out_specs=pl.BlockSpec((1, blk, 3 * blk), lambda h: (h, 0, 0)),
        name=name, compiler_params=_params("parallel"))(table, buckets)


def _table_grad(dbias, buckets, name):
    blk = buckets.shape[0]

    def body(db_ref, bk_ref, o_ref):
        bk = bk_ref[...]
        dbv = db_ref[0]
        lane = lax.broadcasted_iota(jnp.int32, (1, LANES), 1)
        acc = jnp.zeros((1, LANES), F32)
        for b in range(NUM_BUCKETS):
            acc = jnp.where(lane == b, jnp.sum(jnp.where(bk == b, dbv, 0.0)), acc)
        o_ref[0] = acc

    out = pl.pallas_call(
        body, out_shape=jax.ShapeDtypeStruct((N_HEADS, 1, LANES), F32), grid=(N_HEADS,),
        in_specs=[pl.BlockSpec((1, blk, 3 * blk), lambda h: (h, 0, 0)), pl.BlockSpec((blk, 3 * blk), lambda h: (0, 0))],
        out_specs=pl.BlockSpec((1, 1, LANES), lambda h: (h, 0, 0)),
        name=name, compiler_params=_params("parallel"))(dbias, buckets)
    return out[:, 0, :NUM_BUCKETS]


def _dot_nt(a, b):
    return lax.dot_general(a, b, (((1,), (1,)), ((), ())), preferred_element_type=F32)


def _dot_tn(a, b):
    return lax.dot_general(a, b, (((0,), (0,)), ((), ())), preferred_element_type=F32)


def _stack_pair(x2, left):
    return jnp.concatenate([jnp.where(left, x2, 0.0), jnp.where(left, 0.0, x2)], axis=0).astype(BF16)


def _attn_geometry(blk, d):
    halo = blk * d
    subs = max(1, min(ITEMS // d, MAX_CHUNK // halo))
    return subs, min(d, ITEMS // subs), halo


def _item_of(j, r0, per_group):
    return j // per_group, r0 + j % per_group


def _span_rows(ref, first, r, blk, d):
    if d == 1:
        return ref[first:first + blk, :]
    return ref[pl.ds(first + r, blk, stride=d), :]


def _set_span_rows(ref, first, r, blk, d, val, add=False):
    idx = slice(first, first + blk) if d == 1 else pl.ds(first + r, blk, stride=d)
    ref[idx, :] = ref[idx, :] + val if add else val


def _for_groups(group, groups, per_group):
    if groups == 1:
        group(0)
    else:
        def step(g, carry):
            group(g * per_group)
            return carry

        lax.fori_loop(0, groups, step, 0)


def _key_rows(p_ref, c_ref, n_ref, s, r, subs, halo, blk, d):
    parts = []
    for span in (s - 1, s, s + 1):
        if span < 0:
            parts.append(_span_rows(p_ref, 0, r, blk, d))
        elif span == subs:
            parts.append(_span_rows(n_ref, 0, r, blk, d))
        else:
            parts.append(_span_rows(c_ref, span * halo, r, blk, d))
    return jnp.concatenate(parts, axis=0)


def _item_penalty(t, nct, s, subs, blk):
    first_ok = True if s > 0 else t > 0
    last_ok = True if s < subs - 1 else t < nct - 1
    col = lax.broadcasted_iota(jnp.int32, (1, 3 * blk), 1)
    ok = jnp.logical_and(jnp.logical_or(col >= blk, first_ok), jnp.logical_or(col < 2 * blk, last_ok))
    return jnp.where(ok, 0.0, NEG_INF).astype(F32)


def _attn_specs(seq, blk, d, step_of, col=0):
    subs, _, halo = _attn_geometry(blk, d)
    last, first = seq // halo - 1, col // LANES
    cur = pl.BlockSpec((subs * halo, LANES), lambda hp, t: (step_of(t), first + hp))
    prev = pl.BlockSpec((halo, LANES), lambda hp, t: (jnp.clip(step_of(t) * subs - 1, 0, last), first + hp))
    nxt = pl.BlockSpec((halo, LANES), lambda hp, t: (jnp.minimum((step_of(t) + 1) * subs, last), first + hp))
    return cur, prev, nxt


def _attn_fwd(q, k, v, bias, sink, blk, d, name, v_col=0):
    seq = q.shape[0]
    subs, per_group, halo = _attn_geometry(blk, d)
    items, chunk, groups = subs * per_group, subs * halo, d // per_group
    nct = seq // chunk
    has_sink = sink is not None
    scale = HEAD_DIM ** -0.5

    def body(*refs):
        q_ref, kp, kc, kn, vp, vc, vn, b_ref = refs[:8]
        s_ref = refs[8] if has_sink else None
        o_ref, l_ref = refs[-2], refs[-1]
        t = pl.program_id(1)
        left = lax.broadcasted_iota(jnp.int32, (1, LANES), 1) < HEAD_DIM
        bias2 = b_ref[...]

        def group(r0):
            scores, vcats = [], []
            for j in range(items):
                s, r = _item_of(j, r0, per_group)
                qs = _stack_pair(_span_rows(q_ref, s * halo, r, blk, d) * scale, left)
                kcat = _key_rows(kp, kc, kn, s, r, subs, halo, blk, d).astype(BF16)
                scores.append(_dot_nt(qs, kcat) + bias2 + _item_penalty(t, nct, s, subs, blk))
                vcats.append(_key_rows(vp, vc, vn, s, r, subs, halo, blk, d).astype(BF16))
            ms = [jnp.max(s, axis=-1, keepdims=True) for s in scores]
            if has_sink:
                sk = s_ref[...]
                ms = [jnp.maximum(m, sk) for m in ms]
            ps = [jnp.exp(s - m) for s, m in zip(scores, ms)]
            dens = [jnp.sum(p, axis=-1, keepdims=True) for p in ps]
            if has_sink:
                dens = [den + jnp.exp(sk - m) for den, m in zip(dens, ms)]
            pns = [(p * (1.0 / den)).astype(BF16) for p, den in zip(ps, dens)]
            lses = [m + jnp.log(den) for m, den in zip(ms, dens)]
            for j in range(items):
                s, r = _item_of(j, r0, per_group)
                o2 = jnp.dot(pns[j], vcats[j], preferred_element_type=F32)
                _set_span_rows(o_ref, s * halo, r, blk, d, jnp.where(left, o2[:blk], o2[blk:]))
                _set_span_rows(l_ref, s * halo, r, blk, d, jnp.where(left, lses[j][:blk], lses[j][blk:]))

        _for_groups(group, groups, per_group)

    cur, prev, nxt = _attn_specs(seq, blk, d, lambda t: t)
    v_cur, v_prev, v_nxt = _attn_specs(seq, blk, d, lambda t: t, v_col)
    in_specs = [cur, prev, cur, nxt, v_prev, v_cur, v_nxt, pl.BlockSpec((2 * blk, 3 * blk), lambda hp, t: (hp, 0))]
    args = [q, k, k, k, v, v, v, bias]
    if has_sink:
        in_specs.append(pl.BlockSpec((2 * blk, 1), lambda hp, t: (hp, 0)))
        args.append(sink)
    return pl.pallas_call(
        body, out_shape=[jax.ShapeDtypeStruct((seq, WIDTH), F32)] * 2, grid=(N_PAIRS, nct),
        in_specs=in_specs, out_specs=[cur, cur], name=name,
        compiler_params=_params("parallel", "parallel"))(*args)


def _attn_bwd(q, k, v, do, lse, delta, bias, sink, blk, d, name, v_col=0):
    seq = q.shape[0]
    subs, per_group, halo = _attn_geometry(blk, d)
    items, chunk, groups = subs * per_group, subs * halo, d // per_group
    nct = seq // chunk
    has_sink = sink is not None
    n_in = 12 if has_sink else 11
    scale = HEAD_DIM ** -0.5

    def body(*refs):
        q_ref, kp, kc, kn, vp, vc, vn, do_ref, l_ref, d_ref, b_ref = refs[:11]
        s_ref = refs[11] if has_sink else None
        dq_ref, dk_ref, dv_ref, db_ref = refs[n_in:n_in + 4]
        ds_ref = refs[n_in + 4] if has_sink else None
        wk, wv = refs[-2], refs[-1]
        t = pl.program_id(1)

        @pl.when(t == 0)
        def _():
            wk[...] = jnp.zeros_like(wk)
            wv[...] = jnp.zeros_like(wv)
            db_ref[...] = jnp.zeros_like(db_ref)
            if has_sink:
                ds_ref[...] = jnp.zeros_like(ds_ref)

        @pl.when(t > 0)
        def _():
            for w in (wk, wv):
                keep = w[chunk:2 * chunk + halo]
                w[0:chunk + halo] = keep
                w[chunk + halo:2 * chunk + halo] = jnp.zeros((chunk, LANES), F32)

        @pl.when(t < nct)
        def _():
            lane = lax.broadcasted_iota(jnp.int32, (1, LANES), 1)
            left = lane < HEAD_DIM
            bias2 = b_ref[...]

            def group(r0):
                qss, doss, kcats, scores, dps, lcols, dcols = [], [], [], [], [], [], []
                for j in range(items):
                    s, r = _item_of(j, r0, per_group)
                    qs = _stack_pair(_span_rows(q_ref, s * halo, r, blk, d) * scale, left)
                    dos = _stack_pair(_span_rows(do_ref, s * halo, r, blk, d), left)
                    kcat = _key_rows(kp, kc, kn, s, r, subs, halo, blk, d).astype(BF16)
                    vcat = _key_rows(vp, vc, vn, s, r, subs, halo, blk, d).astype(BF16)
                    l2, d2 = _span_rows(l_ref, s * halo, r, blk, d), _span_rows(d_ref, s * halo, r, blk, d)
                    lcols.append(jnp.concatenate([jnp.max(jnp.where(left, l2, NEG_INF), axis=-1, keepdims=True),
                                                  jnp.max(jnp.where(left, NEG_INF, l2), axis=-1, keepdims=True)], axis=0))
                    dcols.append(jnp.concatenate([jnp.sum(jnp.where(lane == 0, d2, 0.0), axis=-1, keepdims=True),
                                                  jnp.sum(jnp.where(lane == HEAD_DIM, d2, 0.0), axis=-1, keepdims=True)],
                                                 axis=0))
                    scores.append(_dot_nt(qs, kcat) + bias2 + _item_penalty(t, nct, s, subs, blk))
                    dps.append(_dot_nt(dos, vcat))
                    qss.append(qs)
                    doss.append(dos)
                    kcats.append(kcat)
                ps = [jnp.exp(s - lc) for s, lc in zip(scores, lcols)]
                dss = [p * (dp - dc) for p, dp, dc in zip(ps, dps, dcols)]
                db_ref[...] += functools.reduce(lambda a, b: a + b, dss)
                if has_sink:
                    sk = s_ref[...]
                    ds_ref[...] -= functools.reduce(lambda a, b: a + b, [dc * jnp.exp(sk - lc) for dc, lc in zip(dcols, lcols)])
                dsbs = [ds.astype(BF16) for ds in dss]
                for j in range(items):
                    s, r = _item_of(j, r0, per_group)
                    dq2 = jnp.dot(dsbs[j], kcats[j], preferred_element_type=F32) * scale
                    _set_span_rows(dq_ref, s * halo, r, blk, d, jnp.where(left, dq2[:blk], dq2[blk:]))
                news = [(_dot_tn(dsbs[j], qss[j]), _dot_tn(ps[j].astype(BF16), doss[j])) for j in range(items)]
                for which, w in enumerate((wk, wv)):
                    for jr in range(per_group):
                        for sp in range(-1, subs + 1):
                            parts = [news[s * per_group + jr][which][(sp - s + 1) * blk:(sp - s + 2) * blk]
                                     for s in range(subs) if 0 <= sp - s + 1 < 3]
                            _set_span_rows(w, chunk + sp * halo, r0 + jr, blk, d,
                                           functools.reduce(lambda x, y: x + y, parts), add=True)

            _for_groups(group, groups, per_group)

        dk_ref[...] = wk[0:chunk]
        dv_ref[...] = wv[0:chunk]

    cur, prev, nxt = _attn_specs(seq, blk, d, lambda t: jnp.minimum(t, nct - 1))
    v_cur, v_prev, v_nxt = _attn_specs(seq, blk, d, lambda t: jnp.minimum(t, nct - 1), v_col)
    lag = pl.BlockSpec((chunk, LANES), lambda hp, t: (jnp.maximum(t - 1, 0), hp))
    band = pl.BlockSpec((2 * blk, 3 * blk), lambda hp, t: (hp, 0))
    col = pl.BlockSpec((2 * blk, 1), lambda hp, t: (hp, 0))
    in_specs = [cur, prev, cur, nxt, v_prev, v_cur, v_nxt, cur, cur, cur, band]
    args = [q, k, k, k, v, v, v, do, lse, delta, bias]
    out_shape = [jax.ShapeDtypeStruct((seq, WIDTH), F32)] * 3 + [jax.ShapeDtypeStruct((N_HEADS * blk, 3 * blk), F32)]
    out_specs = [cur, lag, lag, band]
    if has_sink:
        in_specs.append(col)
        args.append(sink)
        out_shape.append(jax.ShapeDtypeStruct((N_HEADS * blk, 1), F32))
        out_specs.append(col)
    window = pltpu.VMEM((2 * chunk + halo, LANES), F32)
    return pl.pallas_call(
        body, out_shape=out_shape, grid=(N_PAIRS, nct + 1), in_specs=in_specs, out_specs=out_specs,
        scratch_shapes=[window, window], name=name,
        compiler_params=_params("arbitrary", "arbitrary"))(*args)


def _combine_patterns(outs, lses):
    def combine(*tiles):
        os_, ls = tiles[:len(outs)], tiles[len(outs):]
        m = functools.reduce(jnp.maximum, ls)
        es = [jnp.exp(l - m) for l in ls]
        den = functools.reduce(lambda a, b: a + b, es)
        num = functools.reduce(lambda a, b: a + b, [e * o for e, o in zip(es, os_)])
        return num / den, m + jnp.log(den)

    return _ew(combine, [*outs, *lses], [F32, F32], "combine_a")


def _tile_gain(g, reps):
    return jnp.tile(g[None, :], (1, reps))


def _local_step(x, p, target, w_in_of, rest_of, small):
    rel_table = small["rel_table"]
    buckets_a = [_band_buckets(blk, d) for blk, d in DILATED]
    buckets_b = _band_buckets(BLK_B, 1)
    bias_a = [_bias_tiles(rel_table, bk, 0, "bias_a").reshape(N_HEADS * bk.shape[0], -1) for bk in buckets_a]
    bias_b = _bias_tiles(rel_table, buckets_b, N_HEADS, "bias_b").reshape(N_HEADS * BLK_B, -1)

    saved = []
    for l in range(DEPTH):
        g_mix, g_ffn, g_ple = (small[n][l][None, :] for n in ("norm_mix_g", "norm_ffn_g", "norm_ple_g"))
        gqa, gka, gqb = (_tile_gain(small[n][l], N_HEADS) for n in ("qnorm_a_g", "knorm_a_g", "qnorm_b_g"))
        gkb = _tile_gain(small["knorm_b_g"][l], N_KV_B)
        sink = jnp.repeat(small["sink_b"][l], BLK_B)[:, None]

        h = _rms_fwd(x, g_mix, "rms_mix") if l == 0 else h_next
        w_in = w_in_of(l, (h, bias_a, bias_b))
        proj = _mm(h, w_in, "nt", F32, "mm_in")
        qa, ka, qb, kb, vb = _qknorm_fwd(proj, gqa, gka, gqb, gkb)
        outs, lses = [], []
        for (blk, d), bias in zip(DILATED, bias_a):
            o, ls = _attn_fwd(qa, ka, proj, bias, None, blk, d, f"attn_a{d}_fwd", v_col=OFF_VA)
            outs.append(o)
            lses.append(ls)
        ya, lse_a = _combine_patterns(outs, lses)
        yb, lse_b = _attn_fwd(qb, kb, vb, bias_b, sink, BLK_B, 1, "attn_b_fwd")
        w = dict(rest_of(l, yb), w_in=w_in)
        def gate(products, extra):
            (ca_, cb_), (ga_, gb_) = products, extra
            return _sigmoid(ga_) * ca_ + _sigmoid(gb_) * cb_, ca_, cb_

        merged, ca, cb = _mm_fused([(ya, w["w_branch_a"], "nn"), (yb, w["w_branch_b"], "nn")],
                                   [(proj, OFF_GA), (proj, OFF_GB)], gate, [BF16, BF16, BF16], "mm_branches_gate")
        x1, h2 = _mm_res_norm(merged, w["w_out"], "nn", x, g_ffn, "mm_out_norm")

        def swiglu(products, extra):
            a_, u_ = products
            return (a_ * _sigmoid(a_)) * u_, a_, u_

        hid, a, u = _mm_fused([(h2, w["w_ffn_gate"], "nt"), (h2, w["w_ffn_up"], "nt")], [], swiglu,
                              [BF16, BF16, BF16], "mm_ffn_gate_up")
        x2, h3 = _mm_res_norm(hid, w["w_ffn_down"], "nn", x1, g_ple, "mm_ffn_down_norm")

        def ple(products, extra):
            z_, e_ = products
            return extra[0] + _sigmoid(z_) * e_, z_, e_

        next_gain = small["norm_mix_g"][l + 1][None, :] if l + 1 < DEPTH else None
        x3, z, e, *rest = _mm_fused([(h3, w["w_ple_gate"], "nn"), (p[l], w["w_ple_proj"], "nn")], [(x2, 0)], ple,
                                    [F32, BF16, BF16], "mm_ple", next_gain=next_gain)
        h_next = rest[0] if rest else None
        saved.append(dict(w=w, x0=x, h=h, proj=proj, qa=qa, ka=ka, qb=qb, kb=kb, vb=vb, ya=ya, lse_a=lse_a,
                          yb=yb, lse_b=lse_b, ca=ca, cb=cb, merged=merged, x1=x1, h2=h2, a=a, u=u, hid=hid,
                          x2=x2, h3=h3, z=z, e=e))
        x = x3

    dx, de, dz, loss_acc = _loss_grad(x, target, saved[-1]["z"], saved[-1]["e"])
    loss = loss_acc[0, 0]

    gbig = [{} for _ in range(DEPTH)]
    marks = [{} for _ in range(DEPTH)]
    gsmall = {n: [None] * DEPTH for n in SMALL if n != "rel_table"}
    dbias_a = [[] for _ in DILATED]
    dbias_b = []

    for l in reversed(range(DEPTH)):
        sv = saved[l]
        w = sv["w"]
        g_mix, g_ffn, g_ple = (small[n][l][None, :] for n in ("norm_mix_g", "norm_ffn_g", "norm_ple_g"))
        gqa, gka, gqb = (_tile_gain(small[n][l], N_HEADS) for n in ("qnorm_a_g", "knorm_a_g", "qnorm_b_g"))
        gkb = _tile_gain(small["knorm_b_g"][l], N_KV_B)
        sink = jnp.repeat(small["sink_b"][l], BLK_B)[:, None]

        if l < DEPTH - 1:
            de, dz = _ew(_ple_bwd, [dx, sv["z"], sv["e"]], [BF16, BF16], "ple_bwd")
        gbig[l]["w_ple_proj"] = _mm(p[l], de, "tn", BF16, "mm_d_ple_proj")
        gbig[l]["w_ple_gate"] = _mm(sv["h3"], dz, "tn", BF16, "mm_d_ple_gate")
        dx, dxb, gsmall["norm_ple_g"][l] = _mm_rms_bwd([(dz, w["w_ple_gate"], "nt")], sv["x2"], g_ple, dx,
                                                      "mm_dh3_rms_bwd")

        gbig[l]["w_ffn_down"] = _mm(sv["hid"], dxb, "tn", BF16, "mm_d_ffn_down")

        def swiglu_bwd(products, extra):
            dh_, a_, u_ = products[0], extra[0].astype(F32), extra[1].astype(F32)
            s = _sigmoid(a_)
            return dh_ * u_ * (s * (1.0 + a_ * (1.0 - s))), dh_ * (a_ * s)

        da, du = _mm_fused([(dxb, w["w_ffn_down"], "nt")], [(sv["a"], 0), (sv["u"], 0)], swiglu_bwd, [BF16, BF16],
                           "mm_dhid_swiglu_bwd")
        gbig[l]["w_ffn_gate"], gbig[l]["w_ffn_up"] = _mm_fused(
            [(da, sv["h2"], "tn"), (du, sv["h2"], "tn")], [], lambda products, extra: products, [BF16, BF16],
            "mm_d_ffn_gate_up")
        dx, dxb, gsmall["norm_ffn_g"][l] = _mm_rms_bwd([(da, w["w_ffn_gate"], "nn"), (du, w["w_ffn_up"], "nn")],
                                                      sv["x1"], g_ffn, dx, "mm_dh2_rms_bwd")

        gbig[l]["w_out"] = _mm(sv["merged"], dxb, "tn", BF16, "mm_d_out")

        def gate_bwd(products, extra):
            dm_, ca_, cb_ = products[0], extra[0].astype(F32), extra[1].astype(F32)
            sa, sb = _sigmoid(extra[2]), _sigmoid(extra[3])
            return dm_ * sa, dm_ * sb, dm_ * ca_ * (sa * (1.0 - sa)), dm_ * cb_ * (sb * (1.0 - sb))

        dca, dcb, dga, dgb = _mm_fused(
            [(dxb, w["w_out"], "nt")], [(sv["ca"], 0), (sv["cb"], 0), (sv["proj"], OFF_GA), (sv["proj"], OFF_GB)],
            gate_bwd, [BF16, BF16, BF16, BF16], "mm_dmerged_gate_bwd")
        gbig[l]["w_branch_a"], gbig[l]["w_branch_b"] = _mm_fused(
            [(sv["ya"], dca, "tn"), (sv["yb"], dcb, "tn")], [], lambda products, extra: products, [BF16, BF16],
            "mm_d_branches")

        def with_row_dots(products, extra):
            (dya_, dyb_), (ya_, yb_) = products, extra
            return dya_, _seg_sum(dya_ * ya_), dyb_, _seg_sum(dyb_ * yb_)

        dya, delta_a, dyb, delta_b = _mm_fused(
            [(dca, w["w_branch_a"], "nt"), (dcb, w["w_branch_b"], "nt")], [(sv["ya"], 0), (sv["yb"], 0)], with_row_dots,
            [F32, F32, F32, F32], "mm_dy_branches")

        dqa, dka, dva = [], [], []
        for (blk, d), bias, bk in zip(DILATED, bias_a, buckets_a):
            dq_, dk_, dv_, db_ = _attn_bwd(sv["qa"], sv["ka"], sv["proj"], dya, sv["lse_a"], delta_a, bias, None, blk, d,
                                           f"attn_a{d}_bwd", v_col=OFF_VA)
            dqa.append(dq_)
            dka.append(dk_)
            dva.append(dv_)
            dbias_a[len(dqa) - 1].append(db_)
        dqb, dkb, dvb, db_, dsink = _attn_bwd(sv["qb"], sv["kb"], sv["vb"], dyb, sv["lse_b"], delta_b, bias_b, sink,
                                              BLK_B, 1, "attn_b_bwd")
        dbias_b.append(db_)
        gsmall["sink_b"][l] = dsink.reshape(N_HEADS, BLK_B).sum(axis=1)

        dproj, pqa, pka, pqb, pkb = _qknorm_bwd(sv["proj"], gqa, gka, gqb, gkb, dqa, dka, dva, dqb, dkb, dvb, dga, dgb)
        marks[l]["attn_bwd_done"] = dproj
        gsmall["qnorm_a_g"][l] = pqa.reshape(N_HEADS, HEAD_DIM).sum(0)
        gsmall["knorm_a_g"][l] = pka.reshape(N_HEADS, HEAD_DIM).sum(0)
        gsmall["qnorm_b_g"][l] = pqb.reshape(N_HEADS, HEAD_DIM).sum(0)
        gsmall["knorm_b_g"][l] = pkb.reshape(N_KV_B, HEAD_DIM).sum(0)
        gbig[l]["w_in"] = _mm(dproj, sv["h"], "tn", BF16, "mm_d_in")
        dx, _, gsmall["norm_mix_g"][l] = _mm_rms_bwd([(dproj, w["w_in"], "nn")], sv["x0"], g_mix, dx, "mm_dh_rms_bwd")
        gsmall["norm_mix_g"][l] = gsmall["norm_mix_g"][l][0]
        gsmall["norm_ffn_g"][l] = gsmall["norm_ffn_g"][l][0]
        gsmall["norm_ple_g"][l] = gsmall["norm_ple_g"][l][0]

    gsmall = {n: jnp.stack(v) for n, v in gsmall.items()}
    dtable_a = sum(_table_grad(sum(dbs).reshape(N_HEADS, blk, 3 * blk), bk, "table_grad_a")
                   for dbs, (blk, _), bk in zip(dbias_a, DILATED, buckets_a))
    dtable_b = _table_grad(sum(dbias_b).reshape(N_HEADS, BLK_B, 3 * BLK_B), buckets_b, "table_grad_b")
    gsmall["rel_table"] = jnp.concatenate([dtable_a, dtable_b], axis=0).T
    return loss, dx, gbig, gsmall, marks


def _place():
    return lax.axis_index("x"), lax.axis_index("y"), lax.axis_index("c")


def _flip(v, bit):
    return 1 - v if bit else v


CHIP_RELATIONS = ((0, 1), (1, 0), (1, 1))
ANY = pl.BlockSpec(memory_space=pl.ANY)


def _allgather_body(w_refs, out_refs, send_sems, recv_sems):
    x, y, c = _place()
    chips = [(_flip(x, a), _flip(y, b)) for a, b in CHIP_RELATIONS]

    def make(g):
        w_ref, out_ref = w_refs[g], out_refs[g]
        half = w_ref.shape[0] // 2

        def part(px, py, pc):
            return out_ref.at[2 * px + py, pl.ds(pc * half, half), :]

        def copy(k, block, to, src=None):
            return pltpu.make_async_remote_copy(
                src_ref=part(*block) if src is None else src, dst_ref=part(*block),
                send_sem=send_sems.at[7 * g + k], recv_sem=recv_sems.at[7 * g + k], device_id=to,
                device_id_type=MESH_ID)

        own = pltpu.make_async_remote_copy(
            src_ref=w_ref, dst_ref=out_ref.at[2 * x + y], send_sem=send_sems.at[7 * g + 6],
            recv_sem=recv_sems.at[7 * g + 6], device_id=(x, y, 1 - c), device_id_type=MESH_ID)
        first = [copy(k, (x, y, c), (*chip, c), src=w_ref.at[pl.ds(c * half, half), :]) for k, chip in enumerate(chips)]
        passed = [copy(3 + k, (*chip, c), (x, y, 1 - c)) for k, chip in enumerate(chips)]
        arrive = [copy(k, (*chip, c), (x, y, c)) for k, chip in enumerate(chips)]
        arrive2 = [copy(3 + k, (*chip, 1 - c), (x, y, c)) for k, chip in enumerate(chips)]
        return own, first, passed, arrive, arrive2

    made = [make(g) for g in range(len(w_refs))]
    for own, first, _, _, _ in made:
        own.start()
        for cp in first:
            cp.start()
    for _, _, passed, arrive, _ in made:
        for k in range(3):
            arrive[k].wait_recv()
            passed[k].start()
    for own, first, passed, _, arrive2 in made:
        for k in range(3):
            arrive2[k].wait_recv()
        own.wait_recv()
        for cp in first + passed + [own]:
            cp.wait_send()


def _sibling(x, y, c):
    return [(x, y, 1 - c)]


def _same_core_of_other_chips(x, y, c):
    return [(_flip(x, a), _flip(y, b), c) for a, b in CHIP_RELATIONS]


def _exchange(body, ins, out_types, n_sems, name, sequencer=None):
    n = len(ins)
    sems = (pltpu.SemaphoreType.DMA((n_sems,)), pltpu.SemaphoreType.DMA((n_sems,)))
    if sequencer is None:
        in_place = out_types is None
        out_shape = [jax.ShapeDtypeStruct(a.shape, a.dtype) for a in ins] if in_place else out_types

        def tc_body(*refs):
            body(refs[:n], refs[n:n + len(out_shape)], refs[-2], refs[-1])

        return list(pl.pallas_call(
            tc_body, out_shape=out_shape, in_specs=[ANY] * n, out_specs=[ANY] * len(out_shape),
            input_output_aliases={g: g for g in range(n)} if in_place else {}, scratch_shapes=list(sems), name=name)(*ins))

    collective_id, peers = sequencer
    hbm = pltpu.MemorySpace.HBM
    in_refs = [jax.new_ref(a, memory_space=hbm) for a in ins]
    out_refs = in_refs if out_types is None else [jax.empty_ref(t, memory_space=hbm) for t in out_types]

    @pl.kernel(mesh=plsc.ScalarSubcoreMesh(axis_name="sequencer", num_cores=1), name=name, scratch_types=sems,
               compiler_params=pltpu.CompilerParams(collective_id=collective_id))
    def launch(send_sems, recv_sems):
        barrier = pltpu.get_barrier_semaphore()
        devices = peers(*_place())
        for device in devices:
            pl.semaphore_signal(barrier, inc=1, device_id=device, device_id_type=MESH_ID)
        pl.semaphore_wait(barrier, len(devices))
        body(in_refs, out_refs, send_sems, recv_sems)

    launch()
    return [r[...] for r in out_refs]


def _allgather(shards, name, sequencer=None):
    out_types = [jax.ShapeDtypeStruct((N_CHIPS,) + s.shape, s.dtype) for s in shards]
    if sequencer is not None:
        sequencer = (sequencer, lambda x, y, c: _sibling(x, y, c) + _same_core_of_other_chips(x, y, c))
    return _exchange(_allgather_body, shards, out_types, 7 * len(shards), name, sequencer)


def _half_tile(half):
    return max(t for t in range(16, 1025, 16) if half % t == 0)


def _run_copies(cps):
    for cp in cps:
        cp.start()
    for cp in cps:
        cp.wait_recv()
    for cp in cps:
        cp.wait_send()


def _sibling_halves(gsends, name, sequencer=None):
    def body(g_refs, out_refs, send_sems, recv_sems):
        x, y, c = _place()
        cps = []
        for g, (g_ref, out_ref) in enumerate(zip(g_refs, out_refs)):
            half = g_ref.shape[1] // 2
            cps.append(pltpu.make_async_remote_copy(
                src_ref=g_ref.at[:, pl.ds((1 - c) * half, half), :], dst_ref=out_ref,
                send_sem=send_sems.at[g], recv_sem=recv_sems.at[g], device_id=(x, y, 1 - c), device_id_type=MESH_ID))
        _run_copies(cps)

    out_types = [jax.ShapeDtypeStruct((s.shape[0], s.shape[1] // 2, s.shape[2]), s.dtype) for s in gsends]
    return _exchange(body, gsends, out_types, len(gsends), name, sequencer and (sequencer, _sibling))


def _chip_sums(gsend, sib, place):
    n, rows, cols = gsend.shape
    half = rows // 2
    tm = _half_tile(half)
    nblk = half // tm

    def body(s_ref, g_ref, sib_ref, o_ref):
        o_ref[0] = (g_ref[0].astype(F32) + sib_ref[0].astype(F32)).astype(o_ref.dtype)

    grid_spec = pltpu.PrefetchScalarGridSpec(
        num_scalar_prefetch=1, grid=(n, nblk),
        in_specs=[pl.BlockSpec((1, tm, cols), lambda k, i, s: (jnp.bitwise_xor(s[0], k), s[1] * nblk + i, 0)),
                  pl.BlockSpec((1, tm, cols), lambda k, i, s: (jnp.bitwise_xor(s[0], k), i, 0))],
        out_specs=pl.BlockSpec((1, tm, cols), lambda k, i, s: (k, i, 0)))
    return pl.pallas_call(
        body, out_shape=jax.ShapeDtypeStruct((n, half, cols), BF16), grid_spec=grid_spec,
        name="rs_chip_sums", compiler_params=_params("parallel", "parallel"))(place, gsend, sib)


def _exchange_chip_sums(tsends, name, sequencer=None):
    def body(t_refs, out_refs, send_sems, recv_sems):
        x, y, c = _place()
        cps = []
        for g, (t_ref, out_ref) in enumerate(zip(t_refs, out_refs)):
            for k, device in enumerate(_same_core_of_other_chips(x, y, c)):
                cps.append(pltpu.make_async_remote_copy(
                    src_ref=t_ref.at[k + 1], dst_ref=out_ref.at[k], send_sem=send_sems.at[3 * g + k],
                    recv_sem=recv_sems.at[3 * g + k], device_id=device, device_id_type=MESH_ID))
        _run_copies(cps)

    out_types = [jax.ShapeDtypeStruct((3,) + s.shape[1:], s.dtype) for s in tsends]
    return _exchange(body, tsends, out_types, 3 * len(tsends), name,
                     sequencer and (sequencer, _same_core_of_other_chips))


def _final_sum(tsend, recv, place):
    n, half, cols = tsend.shape
    tm = _half_tile(half)
    nblk = half // tm

    def body(s_ref, t_ref, r_ref, o_ref):
        o_ref[...] = ((t_ref[0].astype(F32) + r_ref[0].astype(F32)) + r_ref[1].astype(F32)) + r_ref[2].astype(F32)

    grid_spec = pltpu.PrefetchScalarGridSpec(
        num_scalar_prefetch=1, grid=(nblk,),
        in_specs=[pl.BlockSpec((1, tm, cols), lambda i, s: (0, i, 0)), pl.BlockSpec((n - 1, tm, cols), lambda i, s: (0, i, 0))],
        out_specs=pl.BlockSpec((tm, cols), lambda i, s: (s[1] * nblk + i, 0)))
    return pl.pallas_call(
        body, out_shape=jax.ShapeDtypeStruct((2 * half, cols), F32), grid_spec=grid_spec, name="rs_final_sum",
        compiler_params=_params("parallel"))(place, tsend, recv)


def _join_halves(gfulls, name, sequencer=None):
    def body(g_refs, out_refs, send_sems, recv_sems):
        x, y, c = _place()
        n = len(g_refs)

        def copy(g, pc):
            half = g_refs[g].shape[0] // 2
            return pltpu.make_async_remote_copy(
                src_ref=g_refs[g].at[pl.ds(pc * half, half), :], dst_ref=out_refs[g].at[pl.ds(pc * half, half), :],
                send_sem=send_sems.at[g], recv_sem=recv_sems.at[g], device_id=(x, y, 1 - c), device_id_type=MESH_ID)

        mine = [copy(g, c) for g in range(n)]
        for cp in mine:
            cp.start()
        for g in range(n):
            copy(g, 1 - c).wait_recv()
        for cp in mine:
            cp.wait_send()

    return _exchange(body, gfulls, None, len(gfulls), name, sequencer and (sequencer, _sibling))


def _allreduce_small(v):
    rows, cols = v.shape

    def body(v_ref, out_ref, buf, send_sems, recv_sems):
        x, y, c = _place()
        cps = []
        for k in range(1, 8):
            peer = (_flip(x, (k >> 2) & 1), _flip(y, (k >> 1) & 1), _flip(c, k & 1))
            cps.append(pltpu.make_async_remote_copy(
                src_ref=v_ref, dst_ref=buf.at[k - 1], send_sem=send_sems.at[k - 1], recv_sem=recv_sems.at[k - 1],
                device_id=peer, device_id_type=MESH_ID))
        for cp in cps:
            cp.start()
        for cp in cps:
            cp.wait_recv()
        for cp in cps:
            cp.wait_send()
        t0 = v_ref[...] + buf[0]
        t1 = buf[1] + buf[2]
        t2 = buf[3] + buf[4]
        t3 = buf[5] + buf[6]
        out_ref[...] = (t0 + t1) + (t2 + t3)

    vm = pl.BlockSpec(memory_space=pltpu.VMEM)
    return pl.pallas_call(
        body, out_shape=jax.ShapeDtypeStruct((rows, cols), F32), in_specs=[vm], out_specs=vm,
        scratch_shapes=[pltpu.VMEM((7, rows, cols), F32), pltpu.SemaphoreType.DMA((7,)), pltpu.SemaphoreType.DMA((7,))],
        name="allreduce_small")(v)


BIG_INFO = {n: (shape, ax) for n, shape, ax in BIG}
GROUPS = (("w_in",), ("w_ffn_gate", "w_ffn_up", "w_ffn_down", "w_out", "w_ple_gate"),
          ("w_branch_a", "w_branch_b", "w_ple_proj"))


def _shard_shape(name):
    (k, m), ax = BIG_INFO[name]
    return (k // N_CHIPS, m) if ax == 0 else (k, m // N_CHIPS)


def _group_rows(group):
    offs, off = {}, 0
    for n in group:
        offs[n] = off
        off += _shard_shape(n)[0]
    return offs, off


def _pack_groups(shards, layer, dtype):
    return [jnp.concatenate([shards[n][layer].astype(dtype) for n in group], axis=0) for group in GROUPS]


def _unpack_full(gathered, groups):
    out = {}
    for group, arr in zip(groups, gathered):
        offs, _ = _group_rows(group)
        for n in group:
            rows, cols = _shard_shape(n)
            (k, m), ax = BIG_INFO[n]
            slab = arr[:, offs[n]:offs[n] + rows]
            out[n] = slab.reshape(k, m) if ax == 0 else jnp.transpose(slab, (1, 0, 2)).reshape(k, m)
    return out


def _pack_grads(gfull):
    out = []
    for group in GROUPS:
        parts = []
        for n in group:
            rows, cols = _shard_shape(n)
            ax = BIG_INFO[n][1]
            slab = (gfull[n].reshape(N_CHIPS, rows, cols) if ax == 0
                    else jnp.transpose(gfull[n].reshape(rows, N_CHIPS, cols), (1, 0, 2)))
            parts.append(slab)
        out.append(jnp.concatenate(parts, axis=1))
    return out


def _after(values, mark):
    values, _ = lax.optimization_barrier((values, mark))
    return values


def _reduce_scatter_begin(gsends, place, tag, ids):
    sibs = _sibling_halves(gsends, "rs_sibling_halves_" + tag, ids[0])
    tsends = [_chip_sums(g, s, place) for g, s in zip(gsends, sibs)]
    return tsends, _exchange_chip_sums(tsends, "rs_exchange_" + tag, ids[1])


def _reduce_scatter_finish(begun, place, tag, ids, hold):
    tsends, recvs = begun
    recvs = _after(recvs, hold)
    return _join_halves([_final_sum(t, r, place) for t, r in zip(tsends, recvs)], "rs_join_halves_" + tag, ids[2])


SMALL_SHAPES = {"rel_table": (NUM_BUCKETS, 2 * N_HEADS), "norm_mix_g": (DEPTH, D_MODEL), "qnorm_a_g": (DEPTH, HEAD_DIM),
                "knorm_a_g": (DEPTH, HEAD_DIM), "qnorm_b_g": (DEPTH, HEAD_DIM), "knorm_b_g": (DEPTH, HEAD_DIM),
                "sink_b": (DEPTH, N_HEADS), "norm_ffn_g": (DEPTH, D_MODEL), "norm_ple_g": (DEPTH, D_MODEL)}


def _pack_small(vals, last=None):
    flat = jnp.concatenate([vals[n].astype(F32).reshape(-1) for n in SMALL])
    tail = jnp.zeros((SMALL_ROWS * LANES - flat.shape[0],), F32)
    if last is not None:
        tail = tail.at[-1].set(last)
    return jnp.concatenate([flat, tail]).reshape(SMALL_ROWS, LANES)


def _unpack_small(packed):
    flat, out, off = packed.reshape(-1), {}, 0
    for n in SMALL:
        size = math.prod(SMALL_SHAPES[n])
        out[n] = flat[off:off + size].reshape(SMALL_SHAPES[n])
        off += size
    return out


def _adamw(w, gs, g_row, m, v, name):
    c1 = 1.0 - ADAM_B1 ** ADAM_STEP
    c2 = 1.0 - ADAM_B2 ** ADAM_STEP
    total, width = w.shape
    n_layers = len(gs)
    per = total // n_layers
    tm = max(t for t in range(8, 513, 8) if per % t == 0 and g_row % t == 0)
    nblk = per // tm

    def body(*refs):
        w_ref, g_refs = refs[0], refs[1:1 + n_layers]
        m_ref, v_ref, og, od, om, ov = refs[1 + n_layers:]
        layer = pl.program_id(0) // nblk
        g = g_refs[0][...]
        for l in range(1, n_layers):
            g = jnp.where(layer == l, g_refs[l][...], g)
        m_new = ADAM_B1 * m_ref[...] + (1.0 - ADAM_B1) * g
        v_new = ADAM_B2 * v_ref[...] + (1.0 - ADAM_B2) * (g * g)
        og[...] = g
        od[...] = -ADAM_LR * ((m_new / c1) / (jnp.sqrt(v_new / c2) + ADAM_EPS) + ADAM_WD * w_ref[...])
        om[...] = m_new
        ov[...] = v_new

    row = pl.BlockSpec((tm, width), lambda i: (i, 0))
    g_specs = [pl.BlockSpec((tm, width), lambda i, l=l: (g_row // tm + jnp.clip(i - l * nblk, 0, nblk - 1), 0))
               for l in range(n_layers)]
    return pl.pallas_call(
        body, out_shape=[jax.ShapeDtypeStruct((total, width), F32)] * 4, grid=(total // tm,),
        in_specs=[row] + g_specs + [row, row], out_specs=[row] * 4, name=name,
        compiler_params=_params("parallel"))(w, *gs, m, v)


def kernel(x, p, rel_table, norm_mix_g, w_in, qnorm_a_g, knorm_a_g, qnorm_b_g, knorm_b_g, sink_b, w_branch_a, w_branch_b, w_out, norm_ffn_g, w_ffn_gate, w_ffn_up, w_ffn_down, norm_ple_g, w_ple_gate, w_ple_proj, loss_target, m_rel_table, m_norm_mix_g, m_w_in, m_qnorm_a_g, m_knorm_a_g, m_qnorm_b_g, m_knorm_b_g, m_sink_b, m_w_branch_a, m_w_branch_b, m_w_out, m_norm_ffn_g, m_w_ffn_gate, m_w_ffn_up, m_w_ffn_down, m_norm_ple_g, m_w_ple_gate, m_w_ple_proj, v_rel_table, v_norm_mix_g, v_w_in, v_qnorm_a_g, v_knorm_a_g, v_qnorm_b_g, v_knorm_b_g, v_sink_b, v_w_branch_a, v_w_branch_b, v_w_out, v_norm_ffn_g, v_w_ffn_gate, v_w_ffn_up, v_w_ffn_down, v_norm_ple_g, v_w_ple_gate, v_w_ple_proj):
    given = dict(locals())

    def held(name, a):
        return jnp.swapaxes(a, 1, 2) if name in TRANSPOSED else a

    weights = {n: held(n, given[n]) for n in WEIGHTS}
    moments_m = {n: held(n, given["m_" + n]) for n in WEIGHTS}
    moments_v = {n: held(n, given["v_" + n]) for n in WEIGHTS}
    xi, yi, ci = _place()
    place = jnp.stack([2 * xi + yi, ci]).astype(jnp.int32)

    shards = [_pack_groups(weights, l, BF16) for l in range(DEPTH)]
    w_in0 = _allgather(shards[0][:1], "allgather_w_in_layer0", sequencer=9)
    gathered = [None, None]
    small = {n: weights[n] for n in SMALL}

    def w_in_of(l, mark):
        if l == 0:
            landed = _after(w_in0, (shards[1], mark))
            gathered[0] = landed + _allgather(_after(shards[0][1:], landed), "allgather_rest_layer0", sequencer=1)
            return _unpack_full(landed, GROUPS[:1])["w_in"]
        return _unpack_full(_after(gathered[l][:1], mark), GROUPS[:1])["w_in"]

    def rest_of(l, mark):
        if l == 0:
            gathered[1] = _allgather(_after(shards[1], mark), "allgather_layer1", sequencer=2)
        return _unpack_full(_after(gathered[l][1:], mark), GROUPS[1:])

    loss, dx, gbig, gsmall, marks = _local_step(x[0], p[:, 0], loss_target[0], w_in_of, rest_of, small)

    gsends = [_pack_grads(gbig[l]) for l in range(DEPTH)]
    stages = {"layer1": (gsends[1], (3, 4, 5)), "rest_layer0": (gsends[0][1:], (6, 7, 8)),
              "w_in_layer0": (gsends[0][:1], (10, 11, 12))}
    begun = {tag: _reduce_scatter_begin(g, place, tag, ids) for tag, (g, ids) in stages.items()}

    def finish(tag, hold):
        return _reduce_scatter_finish(begun[tag], place, tag, stages[tag][1], hold)

    red1 = finish("layer1", marks[0]["attn_bwd_done"])
    rest0 = finish("rest_layer0", gbig[0]["w_in"])

    grads, delta, new_m, new_v = {}, {}, {}, {}

    def update(group, reduced):
        offs, _ = _group_rows(group)
        for n in group:
            shape = weights[n].shape
            two_d = lambda a: a.reshape(shape[0] * shape[1], shape[2])
            outs = _adamw(two_d(weights[n]), reduced, offs[n], two_d(moments_m[n]), two_d(moments_v[n]), "adamw_" + n)
            grads[n], delta[n], new_m[n], new_v[n] = (held(n, o.reshape(shape)) for o in outs)

    for gi in (1, 2):
        update(GROUPS[gi], _after([rest0[gi - 1], red1[gi]], begun["w_in_layer0"][0]))
    small_grads = _allreduce_small(_pack_small(gsmall, last=loss))
    g_, d_, m_, v_ = _adamw(_pack_small(weights), [small_grads], 0, _pack_small(moments_m), _pack_small(moments_v),
                            "adamw_small")
    grads.update(_unpack_small(g_))
    delta.update(_unpack_small(d_))
    new_m.update(_unpack_small(m_))
    new_v.update(_unpack_small(v_))
    others_done = [dx, d_] + [delta[n] for gi in (1, 2) for n in GROUPS[gi]]
    update(GROUPS[0], [finish("w_in_layer0", others_done)[0], red1[0]])

    return (small_grads[-1, -1], dx[None], *[grads[n] for n in WEIGHTS], *[delta[n] for n in WEIGHTS],
            *[new_m[n] for n in WEIGHTS], *[new_v[n] for n in WEIGHTS])
```

```python
import functools
import math

import jax
import jax.numpy as jnp
from jax import lax
from jax.experimental import pallas as pl
from jax.experimental.pallas import tpu as pltpu
from jax.experimental.pallas import tpu_sc as plsc

F32 = jnp.float32
BF16 = jnp.bfloat16
MESH_ID = pl.DeviceIdType.MESH

D_MODEL = 1024
DEPTH = 2
HEAD_DIM = 64
N_HEADS = 8
WIDTH = N_HEADS * HEAD_DIM
N_PAIRS = 4
ITEMS = 16
MAX_CHUNK = 1024
N_KV_B = 2
PLE_DIM = 256
D_FF = 2816
D_IN = 4352
OFF_QA, OFF_KA, OFF_VA, OFF_QB, OFF_KB, OFF_VB, OFF_GA, OFF_GB = 0, 512, 1024, 1536, 2048, 2176, 2304, 3328
DILATED = ((64, 1), (64, 4), (64, 16))
BLK_B = 128
NUM_BUCKETS = 32
MAX_DISTANCE = 1024
RMS_EPS = 1e-6
NEG_INF = -1e30
LANES = 128
ROW_TILE = 256
VMEM_LIMIT = 48 * 1024 * 1024

ADAM_LR, ADAM_B1, ADAM_B2, ADAM_EPS, ADAM_WD, ADAM_STEP = 0.001, 0.9, 0.999, 1e-08, 0.01, 10

TRANSPOSED = ("w_in", "w_ffn_gate", "w_ffn_up")
BIG = (
    ("w_in", (D_IN, D_MODEL), 0),
    ("w_branch_a", (WIDTH, D_MODEL), 1),
    ("w_branch_b", (WIDTH, D_MODEL), 1),
    ("w_out", (D_MODEL, D_MODEL), 0),
    ("w_ffn_gate", (D_FF, D_MODEL), 0),
    ("w_ffn_up", (D_FF, D_MODEL), 0),
    ("w_ffn_down", (D_FF, D_MODEL), 0),
    ("w_ple_gate", (D_MODEL, D_MODEL), 0),
    ("w_ple_proj", (PLE_DIM, D_MODEL), 1),
)
SMALL = ("rel_table", "norm_mix_g", "qnorm_a_g", "knorm_a_g", "qnorm_b_g", "knorm_b_g", "sink_b",
         "norm_ffn_g", "norm_ple_g")
WEIGHTS = ("rel_table", "norm_mix_g", "w_in", "qnorm_a_g", "knorm_a_g", "qnorm_b_g", "knorm_b_g", "sink_b",
           "w_branch_a", "w_branch_b", "w_out", "norm_ffn_g", "w_ffn_gate", "w_ffn_up", "w_ffn_down",
           "norm_ple_g", "w_ple_gate", "w_ple_proj")
N_CHIPS = 4
SMALL_ROWS = 64


def _params(*sem):
    return pltpu.CompilerParams(dimension_semantics=sem, vmem_limit_bytes=VMEM_LIMIT)


MM_VMEM_BUDGET = 40 * 1024 * 1024
STEP_OVERHEAD_S = 0.4e-6
TILE_DMA_BYTES_PER_S = 1.5e12


def _mm_dims(a, b, mode):
    if mode == "nn":
        return a.shape[0], b.shape[1], a.shape[1]
    if mode == "nt":
        return a.shape[0], b.shape[0], a.shape[1]
    return a.shape[1], b.shape[1], a.shape[0]


def _mm_tiles(m, n, pairs, tile_bytes, col_offsets, full_rows=False):
    best = None
    widths = [n] if full_rows else [t for t in range(LANES, n + 1, LANES) if n % t == 0 and all(o % t == 0 for o in col_offsets)]
    for tm in (t for t in range(LANES, m + 1, LANES) if m % t == 0):
        for tn in widths:
            io = sum(tm * k * ab + tn * k * bb for k, ab, bb in pairs) + tm * tn * sum(tile_bytes)
            casts = sum((tm * k * 2 if ab == 4 else 0) + (tn * k * 2 if bb == 4 else 0) for k, ab, bb in pairs)
            if 2 * io + len(pairs) * tm * tn * 4 + casts > MM_VMEM_BUDGET:
                continue
            cost = (m // tm) * (n // tn) * STEP_OVERHEAD_S + io / TILE_DMA_BYTES_PER_S
            if best is None or (cost, -tm) < best[0]:
                best = ((cost, -tm), tm, tn)
    return best[1], best[2]


def _mm_fused(pairs, extras, epilogue, out_dtypes, name, next_gain=None):
    m, n, _ = _mm_dims(*pairs[0])
    assert all(_mm_dims(*p)[:2] == (m, n) for p in pairs)
    with_norm = next_gain is not None
    out_dtypes = list(out_dtypes) + ([BF16] if with_norm else [])
    tm, tn = _mm_tiles(
        m, n, [(_mm_dims(a, b, mode)[2], a.dtype.itemsize, b.dtype.itemsize) for a, b, mode in pairs],
        [e.dtype.itemsize for e, _ in extras] + [jnp.dtype(d).itemsize for d in out_dtypes], [off for _, off in extras],
        full_rows=with_norm)
    dims = {"nn": (((1,), (0,)), ((), ())), "nt": (((1,), (1,)), ((), ())), "tn": (((0,), (0,)), ((), ()))}
    in_specs, args = [], []
    for a, b, mode in pairs:
        k = _mm_dims(a, b, mode)[2]
        in_specs.append(pl.BlockSpec((k, tm), lambda i, j: (0, i)) if mode == "tn" else pl.BlockSpec((tm, k), lambda i, j: (i, 0)))
        in_specs.append(pl.BlockSpec((tn, k), lambda i, j: (j, 0)) if mode == "nt" else pl.BlockSpec((k, tn), lambda i, j: (0, j)))
        args += [a, b]
    for e, off in extras:
        in_specs.append(pl.BlockSpec((tm, tn), lambda i, j, o=off // tn: (i, o + j)))
        args.append(e)
    n_pairs, n_tiles = len(pairs), 2 * len(pairs) + len(extras)
    if with_norm:
        in_specs.append(pl.BlockSpec((1, n), lambda i, j: (0, 0)))
        args.append(next_gain)
    n_in = len(args)

    def body(*refs):
        products = [lax.dot_general(refs[2 * p][...].astype(BF16), refs[2 * p + 1][...].astype(BF16), dims[pairs[p][2]],
                                    preferred_element_type=F32) for p in range(n_pairs)]
        outs = list(epilogue(products, [r[...] for r in refs[2 * n_pairs:n_tiles]]))
        if with_norm:
            y = outs[0]
            outs.append((y * lax.rsqrt(jnp.mean(y * y, axis=-1, keepdims=True) + RMS_EPS)) * refs[n_tiles][...])
        for r, o in zip(refs[n_in:], outs):
            r[...] = o.astype(r.dtype)

    tile = pl.BlockSpec((tm, tn), lambda i, j: (i, j))
    return pl.pallas_call(
        body, out_shape=[jax.ShapeDtypeStruct((m, n), d) for d in out_dtypes], grid=(m // tm, n // tn),
        in_specs=in_specs, out_specs=[tile] * len(out_dtypes), name=name,
        compiler_params=_params("parallel", "parallel"))(*args)


def _mm(a, b, mode, out_dtype, name, res=None):
    if res is None:
        return _mm_fused([(a, b, mode)], [], lambda products, extra: products, [out_dtype], name)[0]
    return _mm_fused([(a, b, mode)], [(res, 0)], lambda products, extra: [products[0] + extra[0]], [out_dtype], name)[0]


def _mm_res_norm(a, b, mode, res, gain, name):
    return _mm_fused([(a, b, mode)], [(res, 0)], lambda products, extra: [products[0] + extra[0]], [F32], name,
                     next_gain=gain)


def _ew(fn, ins, out_dtypes, name):
    rows, width = ins[0].shape
    n_in = len(ins)

    def body(*refs):
        outs = fn(*[r[...] for r in refs[:n_in]])
        for r, o in zip(refs[n_in:], outs):
            r[...] = o.astype(r.dtype)

    row = pl.BlockSpec((ROW_TILE, width), lambda i: (i, 0))
    return pl.pallas_call(
        body, out_shape=[jax.ShapeDtypeStruct((rows, width), dt) for dt in out_dtypes], grid=(rows // ROW_TILE,),
        in_specs=[row] * n_in, out_specs=[row] * len(out_dtypes), name=name, compiler_params=_params("parallel"))(*ins)


def _sigmoid(x):
    return 1.0 / (1.0 + jnp.exp(-x))


def _seg_sum(v):
    outs = []
    for k in range(v.shape[1] // LANES):
        vp = v[:, k * LANES:(k + 1) * LANES]
        left = lax.broadcasted_iota(jnp.int32, vp.shape, 1) < HEAD_DIM
        sl = jnp.sum(jnp.where(left, vp, 0.0), axis=-1, keepdims=True)
        sr = jnp.sum(jnp.where(left, 0.0, vp), axis=-1, keepdims=True)
        outs.append(jnp.where(left, sl, sr))
    return outs[0] if len(outs) == 1 else jnp.concatenate(outs, axis=1)


def _seg_rstd(x):
    return lax.rsqrt(_seg_sum(x * x) * (1.0 / HEAD_DIM) + RMS_EPS)


def _rms_fwd(x, g, name):
    rows, d = x.shape
    tm = ROW_TILE

    def body(x_ref, g_ref, h_ref):
        xv = x_ref[...]
        r = lax.rsqrt(jnp.mean(xv * xv, axis=-1, keepdims=True) + RMS_EPS)
        h_ref[...] = ((xv * r) * g_ref[...]).astype(BF16)

    return pl.pallas_call(
        body, out_shape=jax.ShapeDtypeStruct((rows, d), BF16), grid=(rows // tm,),
        in_specs=[pl.BlockSpec((tm, d), lambda i: (i, 0)), pl.BlockSpec((1, d), lambda i: (0, 0))],
        out_specs=pl.BlockSpec((tm, d), lambda i: (i, 0)), name=name,
        compiler_params=_params("parallel"))(x, g)


def _mm_rms_bwd(pairs, x, g, dres, name):
    rows, d = x.shape
    assert all(_mm_dims(*p)[:2] == (rows, d) for p in pairs)
    kbytes = [(_mm_dims(a, b, mode)[2], a.dtype.itemsize, b.dtype.itemsize) for a, b, mode in pairs]
    tm = max(t for t in (512, 256, 128) if rows % t == 0 and
             2 * (sum(t * k * ab + d * k * bb for k, ab, bb in kbytes) + t * d * 14) + 2 * t * d * 4 <= MM_VMEM_BUDGET)
    dims = {"nn": (((1,), (0,)), ((), ())), "nt": (((1,), (1,)), ((), ()))}
    n_pairs = len(pairs)

    def body(*refs):
        x_ref, g_ref, dres_ref = refs[2 * n_pairs:2 * n_pairs + 3]
        dx_ref, dxb_ref, dg_ref = refs[2 * n_pairs + 3:]
        dhv = functools.reduce(lambda u, v: u + v, [
            lax.dot_general(refs[2 * p][...].astype(BF16), refs[2 * p + 1][...].astype(BF16), dims[pairs[p][2]],
                            preferred_element_type=F32) for p in range(n_pairs)])
        xv = x_ref[...]
        r = lax.rsqrt(jnp.mean(xv * xv, axis=-1, keepdims=True) + RMS_EPS)
        xh = xv * r
        dxh = dhv * g_ref[...]
        dxv = dres_ref[...] + r * (dxh - xh * jnp.mean(dxh * xh, axis=-1, keepdims=True))
        dx_ref[...] = dxv
        dxb_ref[...] = dxv.astype(BF16)
        part = jnp.sum(dhv * xh, axis=0, keepdims=True)

        @pl.when(pl.program_id(0) == 0)
        def _():
            dg_ref[...] = part

        @pl.when(pl.program_id(0) > 0)
        def _():
            dg_ref[...] += part

    row = pl.BlockSpec((tm, d), lambda i: (i, 0))
    vec = pl.BlockSpec((1, d), lambda i: (0, 0))
    in_specs, args = [], []
    for a, b, mode in pairs:
        in_specs += [pl.BlockSpec((tm, a.shape[1]), lambda i: (i, 0)), pl.BlockSpec(b.shape, lambda i: (0, 0))]
        args += [a, b]
    return pl.pallas_call(
        body, out_shape=[jax.ShapeDtypeStruct((rows, d), F32), jax.ShapeDtypeStruct((rows, d), BF16),
                         jax.ShapeDtypeStruct((1, d), F32)],
        grid=(rows // tm,), in_specs=in_specs + [row, vec, row], out_specs=[row, row, vec],
        name=name, compiler_params=_params("arbitrary"))(*args, x, g, dres)


def _ple_bwd(dx, z, e):
    s = _sigmoid(z.astype(F32))
    return dx * s, dx * e.astype(F32) * (s * (1.0 - s))


def _loss_grad(y, t, z, e):
    rows, d = y.shape
    tm = ROW_TILE

    def body(y_ref, t_ref, z_ref, e_ref, dy_ref, de_ref, dz_ref, l_ref):
        err = y_ref[...] - t_ref[...]
        dy = err * (1.0 / d)
        dy_ref[...] = dy
        de, dz = _ple_bwd(dy, z_ref[...], e_ref[...])
        de_ref[...] = de.astype(BF16)
        dz_ref[...] = dz.astype(BF16)
        part = jnp.zeros((1, LANES), F32) + jnp.sum(err * err) * (0.5 / d)

        @pl.when(pl.program_id(0) == 0)
        def _():
            l_ref[...] = part

        @pl.when(pl.program_id(0) > 0)
        def _():
            l_ref[...] += part

    row = pl.BlockSpec((tm, d), lambda i: (i, 0))
    return pl.pallas_call(
        body, out_shape=[jax.ShapeDtypeStruct((rows, d), F32), jax.ShapeDtypeStruct((rows, d), BF16),
                         jax.ShapeDtypeStruct((rows, d), BF16), jax.ShapeDtypeStruct((1, LANES), F32)],
        grid=(rows // tm,), in_specs=[row] * 4, out_specs=[row, row, row, pl.BlockSpec((1, LANES), lambda i: (0, 0))],
        name="loss_grad", compiler_params=_params("arbitrary"))(y, t, z, e)


def _swap_halves(v):
    return pltpu.roll(v, HEAD_DIM, axis=1)


def _expand_kv(kv):
    left = lax.broadcasted_iota(jnp.int32, kv.shape, 1) < HEAD_DIM
    sw = _swap_halves(kv)
    h0 = jnp.where(left, kv, sw)
    h1 = jnp.where(left, sw, kv)
    return jnp.concatenate([h0, h0, h1, h1], axis=1)


def _reduce_kv(dkv):
    left = lax.broadcasted_iota(jnp.int32, (dkv.shape[0], LANES), 1) < HEAD_DIM
    t = dkv[:, 0:LANES] + dkv[:, LANES:2 * LANES]
    u = dkv[:, 2 * LANES:3 * LANES] + dkv[:, 3 * LANES:4 * LANES]
    t = t + _swap_halves(t)
    u = u + _swap_halves(u)
    return jnp.where(left, t, u)


def _qknorm_fwd(proj, gqa, gka, gqb, gkb):
    rows = proj.shape[0]
    tm = ROW_TILE

    def body(qa_ref, ka_ref, qb_ref, kb_ref, vb_ref, gqa_ref, gka_ref, gqb_ref, gkb_ref, oqa, oka, oqb, okb, ovb):
        for src, g_ref, dst in ((qa_ref, gqa_ref, oqa), (ka_ref, gka_ref, oka), (qb_ref, gqb_ref, oqb)):
            xv = src[...]
            dst[...] = (xv * _seg_rstd(xv)) * g_ref[...]
        kv = kb_ref[...]
        okb[...] = _expand_kv((kv * _seg_rstd(kv)) * gkb_ref[...])
        ovb[...] = _expand_kv(vb_ref[...])

    def win(width, off):
        return pl.BlockSpec((tm, width), lambda i: (i, off // width))

    vec = lambda w: pl.BlockSpec((1, w), lambda i: (0, 0))
    out = pl.BlockSpec((tm, WIDTH), lambda i: (i, 0))
    return pl.pallas_call(
        body, out_shape=[jax.ShapeDtypeStruct((rows, WIDTH), F32)] * 5, grid=(rows // tm,),
        in_specs=[win(WIDTH, OFF_QA), win(WIDTH, OFF_KA), win(WIDTH, OFF_QB), win(LANES, OFF_KB), win(LANES, OFF_VB),
                  vec(WIDTH), vec(WIDTH), vec(WIDTH), vec(LANES)],
        out_specs=[out] * 5, name="qknorm_fwd", compiler_params=_params("parallel"))(
            proj, proj, proj, proj, proj, gqa, gka, gqb, gkb)


def _norm_bwd(xv, g, dy):
    r = _seg_rstd(xv)
    xh = xv * r
    dxh = dy * g
    dx = r * (dxh - xh * (_seg_sum(dxh * xh) * (1.0 / HEAD_DIM)))
    return dx, jnp.sum(dy * xh, axis=0, keepdims=True)


def _qknorm_bwd(proj, gqa, gka, gqb, gkb, dqa, dka, dva, dqb, dkb, dvb, dga, dgb):
    rows = proj.shape[0]
    tm = ROW_TILE
    n_a = len(dqa)

    def body(*refs):
        qa_ref, ka_ref, qb_ref, kb_ref, gqa_ref, gka_ref, gqb_ref, gkb_ref = refs[:8]
        pos = 8
        dqa_refs, dka_refs, dva_refs = refs[pos:pos + n_a], refs[pos + n_a:pos + 2 * n_a], refs[pos + 2 * n_a:pos + 3 * n_a]
        pos += 3 * n_a
        dqb_ref, dkb_ref, dvb_ref, dga_ref, dgb_ref = refs[pos:pos + 5]
        dproj_ref, ogqa, ogka, ogqb, ogkb = refs[pos + 5:]

        def total(rs):
            acc = rs[0][...]
            for r in rs[1:]:
                acc = acc + r[...]
            return acc

        dx_qa, p_qa = _norm_bwd(qa_ref[...], gqa_ref[...], total(dqa_refs))
        dx_ka, p_ka = _norm_bwd(ka_ref[...], gka_ref[...], total(dka_refs))
        dx_qb, p_qb = _norm_bwd(qb_ref[...], gqb_ref[...], dqb_ref[...])
        dx_kb, p_kb = _norm_bwd(kb_ref[...], gkb_ref[...], _reduce_kv(dkb_ref[...]))
        dproj_ref[:, OFF_QA:OFF_QA + WIDTH] = dx_qa.astype(BF16)
        dproj_ref[:, OFF_KA:OFF_KA + WIDTH] = dx_ka.astype(BF16)
        dproj_ref[:, OFF_VA:OFF_VA + WIDTH] = total(dva_refs).astype(BF16)
        dproj_ref[:, OFF_QB:OFF_QB + WIDTH] = dx_qb.astype(BF16)
        dproj_ref[:, OFF_KB:OFF_KB + LANES] = dx_kb.astype(BF16)
        dproj_ref[:, OFF_VB:OFF_VB + LANES] = _reduce_kv(dvb_ref[...]).astype(BF16)
        dproj_ref[:, OFF_GA:OFF_GB] = dga_ref[...]
        dproj_ref[:, OFF_GB:D_IN] = dgb_ref[...]
        first = pl.program_id(0) == 0
        for o_ref, part in ((ogqa, p_qa), (ogka, p_ka), (ogqb, p_qb), (ogkb, p_kb)):
            @pl.when(first)
            def _(o_ref=o_ref, part=part):
                o_ref[...] = part

            @pl.when(jnp.logical_not(first))
            def _(o_ref=o_ref, part=part):
                o_ref[...] += part

    def win(width, off):
        return pl.BlockSpec((tm, width), lambda i: (i, off // width))

    vec = lambda w: pl.BlockSpec((1, w), lambda i: (0, 0))
    row = lambda w: pl.BlockSpec((tm, w), lambda i: (i, 0))
    in_specs = [win(WIDTH, OFF_QA), win(WIDTH, OFF_KA), win(WIDTH, OFF_QB), win(LANES, OFF_KB),
                vec(WIDTH), vec(WIDTH), vec(WIDTH), vec(LANES)]
    in_specs += [row(WIDTH)] * (3 * n_a + 3) + [row(D_MODEL)] * 2
    return pl.pallas_call(
        body,
        out_shape=[jax.ShapeDtypeStruct((rows, D_IN), BF16), jax.ShapeDtypeStruct((1, WIDTH), F32),
                   jax.ShapeDtypeStruct((1, WIDTH), F32), jax.ShapeDtypeStruct((1, WIDTH), F32),
                   jax.ShapeDtypeStruct((1, LANES), F32)],
        grid=(rows // tm,), in_specs=in_specs,
        out_specs=[row(D_IN), vec(WIDTH), vec(WIDTH), vec(WIDTH), vec(LANES)],
        name="qknorm_bwd", compiler_params=_params("arbitrary"))(
            proj, proj, proj, proj, gqa, gka, gqb, gkb, *dqa, *dka, *dva, dqb, dkb, dvb, dga, dgb)


def _t5_bucket(rel):
    half_b = NUM_BUCKETS // 2
    max_exact = half_b // 2
    sign = jnp.where(rel > 0, half_b, 0)
    n = jnp.abs(rel)
    nf = jnp.maximum(n, 1).astype(F32)
    large = max_exact + (jnp.log(nf / max_exact) / math.log(MAX_DISTANCE / max_exact)
                         * (half_b - max_exact)).astype(jnp.int32)
    large = jnp.minimum(large, half_b - 1)
    return sign + jnp.where(n < max_exact, n, large)


def _band_buckets(blk, dilation):
    i = jnp.arange(blk, dtype=jnp.int32)[:, None]
    j = jnp.arange(3 * blk, dtype=jnp.int32)[None, :]
    rel = j - blk - i
    return jnp.where(jnp.abs(rel) <= blk, _t5_bucket(rel * dilation), -1)


def _bias_tiles(table, buckets, head_off, name):
    blk = buckets.shape[0]

    def body(tab_ref, bk_ref, o_ref):
        h = pl.program_id(0) + head_off
        bk = bk_ref[...]
        acc = jnp.full(bk.shape, NEG_INF, F32)
        for b in range(NUM_BUCKETS):
            acc = jnp.where(bk == b, tab_ref[b, h], acc)
        o_ref[0] = acc

    return pl.pallas_call(
        body, out_shape=jax.ShapeDtypeStruct((N_HEADS, blk, 3 * blk), F32), grid=(N_HEADS,),
        in_specs=[pl.BlockSpec(memory_space=pltpu.SMEM), pl.BlockSpec((blk, 3 * blk), lambda h: (0, 0))],
        out_specs=pl.BlockSpec((1, blk, 3 * blk), lambda h: (h, 0, 0)),
        name=name, compiler_params=_params("parallel"))(table, buckets)


def _table_grad(dbias, buckets, name):
    blk = buckets.shape[0]

    def body(db_ref, bk_ref, o_ref):
        bk = bk_ref[...]
        dbv = db_ref[0]
        lane = lax.broadcasted_iota(jnp.int32, (1, LANES), 1)
        acc = jnp.zeros((1, LANES), F32)
        for b in range(NUM_BUCKETS):
            acc = jnp.where(lane == b, jnp.sum(jnp.where(bk == b, dbv, 0.0)), acc)
        o_ref[0] = acc

    out = pl.pallas_call(
        body, out_shape=jax.ShapeDtypeStruct((N_HEADS, 1, LANES), F32), grid=(N_HEADS,),
        in_specs=[pl.BlockSpec((1, blk, 3 * blk), lambda h: (h, 0, 0)), pl.BlockSpec((blk, 3 * blk), lambda h: (0, 0))],
        out_specs=pl.BlockSpec((1, 1, LANES), lambda h: (h, 0, 0)),
        name=name, compiler_params=_params("parallel"))(dbias, buckets)
    return out[:, 0, :NUM_BUCKETS]


def _dot_nt(a, b):
    return lax.dot_general(a, b, (((1,), (1,)), ((), ())), preferred_element_type=F32)


def _dot_tn(a, b):
    return lax.dot_general(a, b, (((0,), (0,)), ((), ())), preferred_element_type=F32)


def _stack_pair(x2, left):
    return jnp.concatenate([jnp.where(left, x2, 0.0), jnp.where(left, 0.0, x2)], axis=0).astype(BF16)


def _attn_geometry(blk, d):
    halo = blk * d
    subs = max(1, min(ITEMS // d, MAX_CHUNK // halo))
    return subs, min(d, ITEMS // subs), halo


def _item_of(j, r0, per_group):
    return j // per_group, r0 + j % per_group


def _span_rows(ref, first, r, blk, d):
    if d == 1:
        return ref[first:first + blk, :]
    return ref[pl.ds(first + r, blk, stride=d), :]


def _set_span_rows(ref, first, r, blk, d, val, add=False):
    idx = slice(first, first + blk) if d == 1 else pl.ds(first + r, blk, stride=d)
    ref[idx, :] = ref[idx, :] + val if add else val


def _for_groups(group, groups, per_group):
    if groups == 1:
        group(0)
    else:
        def step(g, carry):
            group(g * per_group)
            return carry

        lax.fori_loop(0, groups, step, 0)


def _key_rows(p_ref, c_ref, n_ref, s, r, subs, halo, blk, d):
    parts = []
    for span in (s - 1, s, s + 1):
        if span < 0:
            parts.append(_span_rows(p_ref, 0, r, blk, d))
        elif span == subs:
            parts.append(_span_rows(n_ref, 0, r, blk, d))
        else:
            parts.append(_span_rows(c_ref, span * halo, r, blk, d))
    return jnp.concatenate(parts, axis=0)


def _item_penalty(t, nct, s, subs, blk):
    first_ok = True if s > 0 else t > 0
    last_ok = True if s < subs - 1 else t < nct - 1
    col = lax.broadcasted_iota(jnp.int32, (1, 3 * blk), 1)
    ok = jnp.logical_and(jnp.logical_or(col >= blk, first_ok), jnp.logical_or(col < 2 * blk, last_ok))
    return jnp.where(ok, 0.0, NEG_INF).astype(F32)


def _attn_specs(seq, blk, d, step_of, col=0):
    subs, _, halo = _attn_geometry(blk, d)
    last, first = seq // halo - 1, col // LANES
    cur = pl.BlockSpec((subs * halo, LANES), lambda hp, t: (step_of(t), first + hp))
    prev = pl.BlockSpec((halo, LANES), lambda hp, t: (jnp.clip(step_of(t) * subs - 1, 0, last), first + hp))
    nxt = pl.BlockSpec((halo, LANES), lambda hp, t: (jnp.minimum((step_of(t) + 1) * subs, last), first + hp))
    return cur, prev, nxt


def _attn_fwd(q, k, v, bias, sink, blk, d, name, v_col=0):
    seq = q.shape[0]
    subs, per_group, halo = _attn_geometry(blk, d)
    items, chunk, groups = subs * per_group, subs * halo, d // per_group
    nct = seq // chunk
    has_sink = sink is not None
    scale = HEAD_DIM ** -0.5

    def body(*refs):
        q_ref, kp, kc, kn, vp, vc, vn, b_ref = refs[:8]
        s_ref = refs[8] if has_sink else None
        o_ref, l_ref = refs[-2], refs[-1]
        t = pl.program_id(1)
        left = lax.broadcasted_iota(jnp.int32, (1, LANES), 1) < HEAD_DIM
        bias2 = b_ref[...]

        def group(r0):
            scores, vcats = [], []
            for j in range(items):
                s, r = _item_of(j, r0, per_group)
                qs = _stack_pair(_span_rows(q_ref, s * halo, r, blk, d) * scale, left)
                kcat = _key_rows(kp, kc, kn, s, r, subs, halo, blk, d).astype(BF16)
                scores.append(_dot_nt(qs, kcat) + bias2 + _item_penalty(t, nct, s, subs, blk))
                vcats.append(_key_rows(vp, vc, vn, s, r, subs, halo, blk, d).astype(BF16))
            ms = [jnp.max(s, axis=-1, keepdims=True) for s in scores]
            if has_sink:
                sk = s_ref[...]
                ms = [jnp.maximum(m, sk) for m in ms]
            ps = [jnp.exp(s - m) for s, m in zip(scores, ms)]
            dens = [jnp.sum(p, axis=-1, keepdims=True) for p in ps]
            if has_sink:
                dens = [den + jnp.exp(sk - m) for den, m in zip(dens, ms)]
            pns = [(p * (1.0 / den)).astype(BF16) for p, den in zip(ps, dens)]
            lses = [m + jnp.log(den) for m, den in zip(ms, dens)]
            for j in range(items):
                s, r = _item_of(j, r0, per_group)
                o2 = jnp.dot(pns[j], vcats[j], preferred_element_type=F32)
                _set_span_rows(o_ref, s * halo, r, blk, d, jnp.where(left, o2[:blk], o2[blk:]))
                _set_span_rows(l_ref, s * halo, r, blk, d, jnp.where(left, lses[j][:blk], lses[j][blk:]))

        _for_groups(group, groups, per_group)

    cur, prev, nxt = _attn_specs(seq, blk, d, lambda t: t)
    v_cur, v_prev, v_nxt = _attn_specs(seq, blk, d, lambda t: t, v_col)
    in_specs = [cur, prev, cur, nxt, v_prev, v_cur, v_nxt, pl.BlockSpec((2 * blk, 3 * blk), lambda hp, t: (hp, 0))]
    args = [q, k, k, k, v, v, v, bias]
    if has_sink:
        in_specs.append(pl.BlockSpec((2 * blk, 1), lambda hp, t: (hp, 0)))
        args.append(sink)
    return pl.pallas_call(
        body, out_shape=[jax.ShapeDtypeStruct((seq, WIDTH), F32)] * 2, grid=(N_PAIRS, nct),
        in_specs=in_specs, out_specs=[cur, cur], name=name,
        compiler_params=_params("parallel", "parallel"))(*args)


def _attn_bwd(q, k, v, do, lse, delta, bias, sink, blk, d, name, v_col=0):
    seq = q.shape[0]
    subs, per_group, halo = _attn_geometry(blk, d)
    items, chunk, groups = subs * per_group, subs * halo, d // per_group
    nct = seq // chunk
    has_sink = sink is not None
    n_in = 12 if has_sink else 11
    scale = HEAD_DIM ** -0.5

    def body(*refs):
        q_ref, kp, kc, kn, vp, vc, vn, do_ref, l_ref, d_ref, b_ref = refs[:11]
        s_ref = refs[11] if has_sink else None
        dq_ref, dk_ref, dv_ref, db_ref = refs[n_in:n_in + 4]
        ds_ref = refs[n_in + 4] if has_sink else None
        wk, wv = refs[-2], refs[-1]
        t = pl.program_id(1)

        @pl.when(t == 0)
        def _():
            wk[...] = jnp.zeros_like(wk)
            wv[...] = jnp.zeros_like(wv)
            db_ref[...] = jnp.zeros_like(db_ref)
            if has_sink:
                ds_ref[...] = jnp.zeros_like(ds_ref)

        @pl.when(t > 0)
        def _():
            for w in (wk, wv):
                keep = w[chunk:2 * chunk + halo]
                w[0:chunk + halo] = keep
                w[chunk + halo:2 * chunk + halo] = jnp.zeros((chunk, LANES), F32)

        @pl.when(t < nct)
        def _():
            lane = lax.broadcasted_iota(jnp.int32, (1, LANES), 1)
            left = lane < HEAD_DIM
            bias2 = b_ref[...]

            def group(r0):
                qss, doss, kcats, scores, dps, lcols, dcols = [], [], [], [], [], [], []
                for j in range(items):
                    s, r = _item_of(j, r0, per_group)
                    qs = _stack_pair(_span_rows(q_ref, s * halo, r, blk, d) * scale, left)
                    dos = _stack_pair(_span_rows(do_ref, s * halo, r, blk, d), left)
                    kcat = _key_rows(kp, kc, kn, s, r, subs, halo, blk, d).astype(BF16)
                    vcat = _key_rows(vp, vc, vn, s, r, subs, halo, blk, d).astype(BF16)
                    l2, d2 = _span_rows(l_ref, s * halo, r, blk, d), _span_rows(d_ref, s * halo, r, blk, d)
                    lcols.append(jnp.concatenate([jnp.max(jnp.where(left, l2, NEG_INF), axis=-1, keepdims=True),
                                                  jnp.max(jnp.where(left, NEG_INF, l2), axis=-1, keepdims=True)], axis=0))
                    dcols.append(jnp.concatenate([jnp.sum(jnp.where(lane == 0, d2, 0.0), axis=-1, keepdims=True),
                                                  jnp.sum(jnp.where(lane == HEAD_DIM, d2, 0.0), axis=-1, keepdims=True)],
                                                 axis=0))
                    scores.append(_dot_nt(qs, kcat) + bias2 + _item_penalty(t, nct, s, subs, blk))
                    dps.append(_dot_nt(dos, vcat))
                    qss.append(qs)
                    doss.append(dos)
                    kcats.append(kcat)
                ps = [jnp.exp(s - lc) for s, lc in zip(scores, lcols)]
                dss = [p * (dp - dc) for p, dp, dc in zip(ps, dps, dcols)]
                db_ref[...] += functools.reduce(lambda a, b: a + b, dss)
                if has_sink:
                    sk = s_ref[...]
                    ds_ref[...] -= functools.reduce(lambda a, b: a + b, [dc * jnp.exp(sk - lc) for dc, lc in zip(dcols, lcols)])
                dsbs = [ds.astype(BF16) for ds in dss]
                for j in range(items):
                    s, r = _item_of(j, r0, per_group)
                    dq2 = jnp.dot(dsbs[j], kcats[j], preferred_element_type=F32) * scale
                    _set_span_rows(dq_ref, s * halo, r, blk, d, jnp.where(left, dq2[:blk], dq2[blk:]))
                news = [(_dot_tn(dsbs[j], qss[j]), _dot_tn(ps[j].astype(BF16), doss[j])) for j in range(items)]
                for which, w in enumerate((wk, wv)):
                    for jr in range(per_group):
                        for sp in range(-1, subs + 1):
                            parts = [news[s * per_group + jr][which][(sp - s + 1) * blk:(sp - s + 2) * blk]
                                     for s in range(subs) if 0 <= sp - s + 1 < 3]
                            _set_span_rows(w, chunk + sp * halo, r0 + jr, blk, d,
                                           functools.reduce(lambda x, y: x + y, parts), add=True)

            _for_groups(group, groups, per_group)

        dk_ref[...] = wk[0:chunk]
        dv_ref[...] = wv[0:chunk]

    cur, prev, nxt = _attn_specs(seq, blk, d, lambda t: jnp.minimum(t, nct - 1))
    v_cur, v_prev, v_nxt = _attn_specs(seq, blk, d, lambda t: jnp.minimum(t, nct - 1), v_col)
    lag = pl.BlockSpec((chunk, LANES), lambda hp, t: (jnp.maximum(t - 1, 0), hp))
    band = pl.BlockSpec((2 * blk, 3 * blk), lambda hp, t: (hp, 0))
    col = pl.BlockSpec((2 * blk, 1), lambda hp, t: (hp, 0))
    in_specs = [cur, prev, cur, nxt, v_prev, v_cur, v_nxt, cur, cur, cur, band]
    args = [q, k, k, k, v, v, v, do, lse, delta, bias]
    out_shape = [jax.ShapeDtypeStruct((seq, WIDTH), F32)] * 3 + [jax.ShapeDtypeStruct((N_HEADS * blk, 3 * blk), F32)]
    out_specs = [cur, lag, lag, band]
    if has_sink:
        in_specs.append(col)
        args.append(sink)
        out_shape.append(jax.ShapeDtypeStruct((N_HEADS * blk, 1), F32))
        out_specs.append(col)
    window = pltpu.VMEM((2 * chunk + halo, LANES), F32)
    return pl.pallas_call(
        body, out_shape=out_shape, grid=(N_PAIRS, nct + 1), in_specs=in_specs, out_specs=out_specs,
        scratch_shapes=[window, window], name=name,
        compiler_params=_params("arbitrary", "arbitrary"))(*args)


def _combine_patterns(outs, lses):
    def combine(*tiles):
        os_, ls = tiles[:len(outs)], tiles[len(outs):]
        m = functools.reduce(jnp.maximum, ls)
        es = [jnp.exp(l - m) for l in ls]
        den = functools.reduce(lambda a, b: a + b, es)
        num = functools.reduce(lambda a, b: a + b, [e * o for e, o in zip(es, os_)])
        return num / den, m + jnp.log(den)

    return _ew(combine, [*outs, *lses], [F32, F32], "combine_a")


def _tile_gain(g, reps):
    return jnp.tile(g[None, :], (1, reps))


def _local_step(x, p, target, w_in_of, rest_of, small):
    rel_table = small["rel_table"]
    buckets_a = [_band_buckets(blk, d) for blk, d in DILATED]
    buckets_b = _band_buckets(BLK_B, 1)
    bias_a = [_bias_tiles(rel_table, bk, 0, "bias_a").reshape(N_HEADS * bk.shape[0], -1) for bk in buckets_a]
    bias_b = _bias_tiles(rel_table, buckets_b, N_HEADS, "bias_b").reshape(N_HEADS * BLK_B, -1)

    saved = []
    for l in range(DEPTH):
        g_mix, g_ffn, g_ple = (small[n][l][None, :] for n in ("norm_mix_g", "norm_ffn_g", "norm_ple_g"))
        gqa, gka, gqb = (_tile_gain(small[n][l], N_HEADS) for n in ("qnorm_a_g", "knorm_a_g", "qnorm_b_g"))
        gkb = _tile_gain(small["knorm_b_g"][l], N_KV_B)
        sink = jnp.repeat(small["sink_b"][l], BLK_B)[:, None]

        h = _rms_fwd(x, g_mix, "rms_mix") if l == 0 else h_next
        w_in = w_in_of(l, (h, bias_a, bias_b))
        proj = _mm(h, w_in, "nt", F32, "mm_in")
        qa, ka, qb, kb, vb = _qknorm_fwd(proj, gqa, gka, gqb, gkb)
        outs, lses = [], []
        for (blk, d), bias in zip(DILATED, bias_a):
            o, ls = _attn_fwd(qa, ka, proj, bias, None, blk, d, f"attn_a{d}_fwd", v_col=OFF_VA)
            outs.append(o)
            lses.append(ls)
        ya, lse_a = _combine_patterns(outs, lses)
        yb, lse_b = _attn_fwd(qb, kb, vb, bias_b, sink, BLK_B, 1, "attn_b_fwd")
        w = dict(rest_of(l, yb), w_in=w_in)
        def gate(products, extra):
            (ca_, cb_), (ga_, gb_) = products, extra
            return _sigmoid(ga_) * ca_ + _sigmoid(gb_) * cb_, ca_, cb_

        merged, ca, cb = _mm_fused([(ya, w["w_branch_a"], "nn"), (yb, w["w_branch_b"], "nn")],
                                   [(proj, OFF_GA), (proj, OFF_GB)], gate, [BF16, BF16, BF16], "mm_branches_gate")
        x1, h2 = _mm_res_norm(merged, w["w_out"], "nn", x, g_ffn, "mm_out_norm")

        def swiglu(products, extra):
            a_, u_ = products
            return (a_ * _sigmoid(a_)) * u_, a_, u_

        hid, a, u = _mm_fused([(h2, w["w_ffn_gate"], "nt"), (h2, w["w_ffn_up"], "nt")], [], swiglu,
                              [BF16, BF16, BF16], "mm_ffn_gate_up")
        x2, h3 = _mm_res_norm(hid, w["w_ffn_down"], "nn", x1, g_ple, "mm_ffn_down_norm")

        def ple(products, extra):
            z_, e_ = products
            return extra[0] + _sigmoid(z_) * e_, z_, e_

        next_gain = small["norm_mix_g"][l + 1][None, :] if l + 1 < DEPTH else None
        x3, z, e, *rest = _mm_fused([(h3, w["w_ple_gate"], "nn"), (p[l], w["w_ple_proj"], "nn")], [(x2, 0)], ple,
                                    [F32, BF16, BF16], "mm_ple", next_gain=next_gain)
        h_next = rest[0] if rest else None
        saved.append(dict(w=w, x0=x, h=h, proj=proj, qa=qa, ka=ka, qb=qb, kb=kb, vb=vb, ya=ya, lse_a=lse_a,
                          yb=yb, lse_b=lse_b, ca=ca, cb=cb, merged=merged, x1=x1, h2=h2, a=a, u=u, hid=hid,
                          x2=x2, h3=h3, z=z, e=e))
        x = x3

    dx, de, dz, loss_acc = _loss_grad(x, target, saved[-1]["z"], saved[-1]["e"])
    loss = loss_acc[0, 0]

    gbig = [{} for _ in range(DEPTH)]
    marks = [{} for _ in range(DEPTH)]
    gsmall = {n: [None] * DEPTH for n in SMALL if n != "rel_table"}
    dbias_a = [[] for _ in DILATED]
    dbias_b = []

    for l in reversed(range(DEPTH)):
        sv = saved[l]
        w = sv["w"]
        g_mix, g_ffn, g_ple = (small[n][l][None, :] for n in ("norm_mix_g", "norm_ffn_g", "norm_ple_g"))
        gqa, gka, gqb = (_tile_gain(small[n][l], N_HEADS) for n in ("qnorm_a_g", "knorm_a_g", "qnorm_b_g"))
        gkb = _tile_gain(small["knorm_b_g"][l], N_KV_B)
        sink = jnp.repeat(small["sink_b"][l], BLK_B)[:, None]

        if l < DEPTH - 1:
            de, dz = _ew(_ple_bwd, [dx, sv["z"], sv["e"]], [BF16, BF16], "ple_bwd")
        gbig[l]["w_ple_proj"] = _mm(p[l], de, "tn", BF16, "mm_d_ple_proj")
        gbig[l]["w_ple_gate"] = _mm(sv["h3"], dz, "tn", BF16, "mm_d_ple_gate")
        dx, dxb, gsmall["norm_ple_g"][l] = _mm_rms_bwd([(dz, w["w_ple_gate"], "nt")], sv["x2"], g_ple, dx,
                                                      "mm_dh3_rms_bwd")

        gbig[l]["w_ffn_down"] = _mm(sv["hid"], dxb, "tn", BF16, "mm_d_ffn_down")

        def swiglu_bwd(products, extra):
            dh_, a_, u_ = products[0], extra[0].astype(F32), extra[1].astype(F32)
            s = _sigmoid(a_)
            return dh_ * u_ * (s * (1.0 + a_ * (1.0 - s))), dh_ * (a_ * s)

        da, du = _mm_fused([(dxb, w["w_ffn_down"], "nt")], [(sv["a"], 0), (sv["u"], 0)], swiglu_bwd, [BF16, BF16],
                           "mm_dhid_swiglu_bwd")
        gbig[l]["w_ffn_gate"], gbig[l]["w_ffn_up"] = _mm_fused(
            [(da, sv["h2"], "tn"), (du, sv["h2"], "tn")], [], lambda products, extra: products, [BF16, BF16],
            "mm_d_ffn_gate_up")
        dx, dxb, gsmall["norm_ffn_g"][l] = _mm_rms_bwd([(da, w["w_ffn_gate"], "nn"), (du, w["w_ffn_up"], "nn")],
                                                      sv["x1"], g_ffn, dx, "mm_dh2_rms_bwd")

        gbig[l]["w_out"] = _mm(sv["merged"], dxb, "tn", BF16, "mm_d_out")

        def gate_bwd(products, extra):
            dm_, ca_, cb_ = products[0], extra[0].astype(F32), extra[1].astype(F32)
            sa, sb = _sigmoid(extra[2]), _sigmoid(extra[3])
            return dm_ * sa, dm_ * sb, dm_ * ca_ * (sa * (1.0 - sa)), dm_ * cb_ * (sb * (1.0 - sb))

        dca, dcb, dga, dgb = _mm_fused(
            [(dxb, w["w_out"], "nt")], [(sv["ca"], 0), (sv["cb"], 0), (sv["proj"], OFF_GA), (sv["proj"], OFF_GB)],
            gate_bwd, [BF16, BF16, BF16, BF16], "mm_dmerged_gate_bwd")
        gbig[l]["w_branch_a"], gbig[l]["w_branch_b"] = _mm_fused(
            [(sv["ya"], dca, "tn"), (sv["yb"], dcb, "tn")], [], lambda products, extra: products, [BF16, BF16],
            "mm_d_branches")

        def with_row_dots(products, extra):
            (dya_, dyb_), (ya_, yb_) = products, extra
            return dya_, _seg_sum(dya_ * ya_), dyb_, _seg_sum(dyb_ * yb_)

        dya, delta_a, dyb, delta_b = _mm_fused(
            [(dca, w["w_branch_a"], "nt"), (dcb, w["w_branch_b"], "nt")], [(sv["ya"], 0), (sv["yb"], 0)], with_row_dots,
            [F32, F32, F32, F32], "mm_dy_branches")

        dqa, dka, dva = [], [], []
        for (blk, d), bias, bk in zip(DILATED, bias_a, buckets_a):
            dq_, dk_, dv_, db_ = _attn_bwd(sv["qa"], sv["ka"], sv["proj"], dya, sv["lse_a"], delta_a, bias, None, blk, d,
                                           f"attn_a{d}_bwd", v_col=OFF_VA)
            dqa.append(dq_)
            dka.append(dk_)
            dva.append(dv_)
            dbias_a[len(dqa) - 1].append(db_)
        dqb, dkb, dvb, db_, dsink = _attn_bwd(sv["qb"], sv["kb"], sv["vb"], dyb, sv["lse_b"], delta_b, bias_b, sink,
                                              BLK_B, 1, "attn_b_bwd")
        dbias_b.append(db_)
        gsmall["sink_b"][l] = dsink.reshape(N_HEADS, BLK_B).sum(axis=1)

        dproj, pqa, pka, pqb, pkb = _qknorm_bwd(sv["proj"], gqa, gka, gqb, gkb, dqa, dka, dva, dqb, dkb, dvb, dga, dgb)
        marks[l]["attn_bwd_done"] = dproj
        gsmall["qnorm_a_g"][l] = pqa.reshape(N_HEADS, HEAD_DIM).sum(0)
        gsmall["knorm_a_g"][l] = pka.reshape(N_HEADS, HEAD_DIM).sum(0)
        gsmall["qnorm_b_g"][l] = pqb.reshape(N_HEADS, HEAD_DIM).sum(0)
        gsmall["knorm_b_g"][l] = pkb.reshape(N_KV_B, HEAD_DIM).sum(0)
        gbig[l]["w_in"] = _mm(dproj, sv["h"], "tn", BF16, "mm_d_in")
        dx, _, gsmall["norm_mix_g"][l] = _mm_rms_bwd([(dproj, w["w_in"], "nn")], sv["x0"], g_mix, dx, "mm_dh_rms_bwd")
        gsmall["norm_mix_g"][l] = gsmall["norm_mix_g"][l][0]
        gsmall["norm_ffn_g"][l] = gsmall["norm_ffn_g"][l][0]
        gsmall["norm_ple_g"][l] = gsmall["norm_ple_g"][l][0]

    gsmall = {n: jnp.stack(v) for n, v in gsmall.items()}
    dtable_a = sum(_table_grad(sum(dbs).reshape(N_HEADS, blk, 3 * blk), bk, "table_grad_a")
                   for dbs, (blk, _), bk in zip(dbias_a, DILATED, buckets_a))
    dtable_b = _table_grad(sum(dbias_b).reshape(N_HEADS, BLK_B, 3 * BLK_B), buckets_b, "table_grad_b")
    gsmall["rel_table"] = jnp.concatenate([dtable_a, dtable_b], axis=0).T
    return loss, dx, gbig, gsmall, marks


def _place():
    return lax.axis_index("x"), lax.axis_index("y"), lax.axis_index("c")


def _flip(v, bit):
    return 1 - v if bit else v


CHIP_RELATIONS = ((0, 1), (1, 0), (1, 1))
ANY = pl.BlockSpec(memory_space=pl.ANY)


def _allgather_body(w_refs, out_refs, send_sems, recv_sems):
    x, y, c = _place()
    chips = [(_flip(x, a), _flip(y, b)) for a, b in CHIP_RELATIONS]

    def make(g):
        w_ref, out_ref = w_refs[g], out_refs[g]
        half = w_ref.shape[0] // 2

        def part(px, py, pc):
            return out_ref.at[2 * px + py, pl.ds(pc * half, half), :]

        def copy(k, block, to, src=None):
            return pltpu.make_async_remote_copy(
                src_ref=part(*block) if src is None else src, dst_ref=part(*block),
                send_sem=send_sems.at[7 * g + k], recv_sem=recv_sems.at[7 * g + k], device_id=to,
                device_id_type=MESH_ID)

        own = pltpu.make_async_remote_copy(
            src_ref=w_ref, dst_ref=out_ref.at[2 * x + y], send_sem=send_sems.at[7 * g + 6],
            recv_sem=recv_sems.at[7 * g + 6], device_id=(x, y, 1 - c), device_id_type=MESH_ID)
        first = [copy(k, (x, y, c), (*chip, c), src=w_ref.at[pl.ds(c * half, half), :]) for k, chip in enumerate(chips)]
        passed = [copy(3 + k, (*chip, c), (x, y, 1 - c)) for k, chip in enumerate(chips)]
        arrive = [copy(k, (*chip, c), (x, y, c)) for k, chip in enumerate(chips)]
        arrive2 = [copy(3 + k, (*chip, 1 - c), (x, y, c)) for k, chip in enumerate(chips)]
        return own, first, passed, arrive, arrive2

    made = [make(g) for g in range(len(w_refs))]
    for own, first, _, _, _ in made:
        own.start()
        for cp in first:
            cp.start()
    for _, _, passed, arrive, _ in made:
        for k in range(3):
            arrive[k].wait_recv()
            passed[k].start()
    for own, first, passed, _, arrive2 in made:
        for k in range(3):
            arrive2[k].wait_recv()
        own.wait_recv()
        for cp in first + passed + [own]:
            cp.wait_send()


def _sibling(x, y, c):
    return [(x, y, 1 - c)]


def _same_core_of_other_chips(x, y, c):
    return [(_flip(x, a), _flip(y, b), c) for a, b in CHIP_RELATIONS]


def _exchange(body, ins, out_types, n_sems, name, sequencer=None):
    n = len(ins)
    sems = (pltpu.SemaphoreType.DMA((n_sems,)), pltpu.SemaphoreType.DMA((n_sems,)))
    if sequencer is None:
        in_place = out_types is None
        out_shape = [jax.ShapeDtypeStruct(a.shape, a.dtype) for a in ins] if in_place else out_types

        def tc_body(*refs):
            body(refs[:n], refs[n:n + len(out_shape)], refs[-2], refs[-1])

        return list(pl.pallas_call(
            tc_body, out_shape=out_shape, in_specs=[ANY] * n, out_specs=[ANY] * len(out_shape),
            input_output_aliases={g: g for g in range(n)} if in_place else {}, scratch_shapes=list(sems), name=name)(*ins))

    collective_id, peers = sequencer
    hbm = pltpu.MemorySpace.HBM
    in_refs = [jax.new_ref(a, memory_space=hbm) for a in ins]
    out_refs = in_refs if out_types is None else [jax.empty_ref(t, memory_space=hbm) for t in out_types]

    @pl.kernel(mesh=plsc.ScalarSubcoreMesh(axis_name="sequencer", num_cores=1), name=name, scratch_types=sems,
               compiler_params=pltpu.CompilerParams(collective_id=collective_id))
    def launch(send_sems, recv_sems):
        barrier = pltpu.get_barrier_semaphore()
        devices = peers(*_place())
        for device in devices:
            pl.semaphore_signal(barrier, inc=1, device_id=device, device_id_type=MESH_ID)
        pl.semaphore_wait(barrier, len(devices))
        body(in_refs, out_refs, send_sems, recv_sems)

    launch()
    return [r[...] for r in out_refs]


def _allgather(shards, name, sequencer=None):
    out_types = [jax.ShapeDtypeStruct((N_CHIPS,) + s.shape, s.dtype) for s in shards]
    if sequencer is not None:
        sequencer = (sequencer, lambda x, y, c: _sibling(x, y, c) + _same_core_of_other_chips(x, y, c))
    return _exchange(_allgather_body, shards, out_types, 7 * len(shards), name, sequencer)


def _half_tile(half):
    return max(t for t in range(16, 1025, 16) if half % t == 0)


def _run_copies(cps):
    for cp in cps:
        cp.start()
    for cp in cps:
        cp.wait_recv()
    for cp in cps:
        cp.wait_send()


def _sibling_halves(gsends, name, sequencer=None):
    def body(g_refs, out_refs, send_sems, recv_sems):
        x, y, c = _place()
        cps = []
        for g, (g_ref, out_ref) in enumerate(zip(g_refs, out_refs)):
            half = g_ref.shape[1] // 2
            cps.append(pltpu.make_async_remote_copy(
                src_ref=g_ref.at[:, pl.ds((1 - c) * half, half), :], dst_ref=out_ref,
                send_sem=send_sems.at[g], recv_sem=recv_sems.at[g], device_id=(x, y, 1 - c), device_id_type=MESH_ID))
        _run_copies(cps)

    out_types = [jax.ShapeDtypeStruct((s.shape[0], s.shape[1] // 2, s.shape[2]), s.dtype) for s in gsends]
    return _exchange(body, gsends, out_types, len(gsends), name, sequencer and (sequencer, _sibling))


def _chip_sums(gsend, sib, place):
    n, rows, cols = gsend.shape
    half = rows // 2
    tm = _half_tile(half)
    nblk = half // tm

    def body(s_ref, g_ref, sib_ref, o_ref):
        o_ref[0] = (g_ref[0].astype(F32) + sib_ref[0].astype(F32)).astype(o_ref.dtype)

    grid_spec = pltpu.PrefetchScalarGridSpec(
        num_scalar_prefetch=1, grid=(n, nblk),
        in_specs=[pl.BlockSpec((1, tm, cols), lambda k, i, s: (jnp.bitwise_xor(s[0], k), s[1] * nblk + i, 0)),
                  pl.BlockSpec((1, tm, cols), lambda k, i, s: (jnp.bitwise_xor(s[0], k), i, 0))],
        out_specs=pl.BlockSpec((1, tm, cols), lambda k, i, s: (k, i, 0)))
    return pl.pallas_call(
        body, out_shape=jax.ShapeDtypeStruct((n, half, cols), BF16), grid_spec=grid_spec,
        name="rs_chip_sums", compiler_params=_params("parallel", "parallel"))(place, gsend, sib)


def _exchange_chip_sums(tsends, name, sequencer=None):
    def body(t_refs, out_refs, send_sems, recv_sems):
        x, y, c = _place()
        cps = []
        for g, (t_ref, out_ref) in enumerate(zip(t_refs, out_refs)):
            for k, device in enumerate(_same_core_of_other_chips(x, y, c)):
                cps.append(pltpu.make_async_remote_copy(
                    src_ref=t_ref.at[k + 1], dst_ref=out_ref.at[k], send_sem=send_sems.at[3 * g + k],
                    recv_sem=recv_sems.at[3 * g + k], device_id=device, device_id_type=MESH_ID))
        _run_copies(cps)

    out_types = [jax.ShapeDtypeStruct((3,) + s.shape[1:], s.dtype) for s in tsends]
    return _exchange(body, tsends, out_types, 3 * len(tsends), name,
                     sequencer and (sequencer, _same_core_of_other_chips))


def _final_sum(tsend, recv, place):
    n, half, cols = tsend.shape
    tm = _half_tile(half)
    nblk = half // tm

    def body(s_ref, t_ref, r_ref, o_ref):
        o_ref[...] = ((t_ref[0].astype(F32) + r_ref[0].astype(F32)) + r_ref[1].astype(F32)) + r_ref[2].astype(F32)

    grid_spec = pltpu.PrefetchScalarGridSpec(
        num_scalar_prefetch=1, grid=(nblk,),
        in_specs=[pl.BlockSpec((1, tm, cols), lambda i, s: (0, i, 0)), pl.BlockSpec((n - 1, tm, cols), lambda i, s: (0, i, 0))],
        out_specs=pl.BlockSpec((tm, cols), lambda i, s: (s[1] * nblk + i, 0)))
    return pl.pallas_call(
        body, out_shape=jax.ShapeDtypeStruct((2 * half, cols), F32), grid_spec=grid_spec, name="rs_final_sum",
        compiler_params=_params("parallel"))(place, tsend, recv)


def _join_halves(gfulls, name, sequencer=None):
    def body(g_refs, out_refs, send_sems, recv_sems):
        x, y, c = _place()
        n = len(g_refs)

        def copy(g, pc):
            half = g_refs[g].shape[0] // 2
            return pltpu.make_async_remote_copy(
                src_ref=g_refs[g].at[pl.ds(pc * half, half), :], dst_ref=out_refs[g].at[pl.ds(pc * half, half), :],
                send_sem=send_sems.at[g], recv_sem=recv_sems.at[g], device_id=(x, y, 1 - c), device_id_type=MESH_ID)

        mine = [copy(g, c) for g in range(n)]
        for cp in mine:
            cp.start()
        for g in range(n):
            copy(g, 1 - c).wait_recv()
        for cp in mine:
            cp.wait_send()

    return _exchange(body, gfulls, None, len(gfulls), name, sequencer and (sequencer, _sibling))


def _allreduce_small(v):
    rows, cols = v.shape

    def body(v_ref, out_ref, buf, send_sems, recv_sems):
        x, y, c = _place()
        cps = []
        for k in range(1, 8):
            peer = (_flip(x, (k >> 2) & 1), _flip(y, (k >> 1) & 1), _flip(c, k & 1))
            cps.append(pltpu.make_async_remote_copy(
                src_ref=v_ref, dst_ref=buf.at[k - 1], send_sem=send_sems.at[k - 1], recv_sem=recv_sems.at[k - 1],
                device_id=peer, device_id_type=MESH_ID))
        for cp in cps:
            cp.start()
        for cp in cps:
            cp.wait_recv()
        for cp in cps:
            cp.wait_send()
        t0 = v_ref[...] + buf[0]
        t1 = buf[1] + buf[2]
        t2 = buf[3] + buf[4]
        t3 = buf[5] + buf[6]
        out_ref[...] = (t0 + t1) + (t2 + t3)

    vm = pl.BlockSpec(memory_space=pltpu.VMEM)
    return pl.pallas_call(
        body, out_shape=jax.ShapeDtypeStruct((rows, cols), F32), in_specs=[vm], out_specs=vm,
        scratch_shapes=[pltpu.VMEM((7, rows, cols), F32), pltpu.SemaphoreType.DMA((7,)), pltpu.SemaphoreType.DMA((7,))],
        name="allreduce_small")(v)


BIG_INFO = {n: (shape, ax) for n, shape, ax in BIG}
GROUPS = (("w_in",), ("w_ffn_gate", "w_ffn_up", "w_ffn_down", "w_out", "w_ple_gate"),
          ("w_branch_a", "w_branch_b", "w_ple_proj"))
GATHER_GROUPS = tuple((n,) for n in GROUPS[0] + GROUPS[1]) + GROUPS[2:]


def _shard_shape(name):
    (k, m), ax = BIG_INFO[name]
    return (k // N_CHIPS, m) if ax == 0 else (k, m // N_CHIPS)


def _group_rows(group):
    offs, off = {}, 0
    for n in group:
        offs[n] = off
        off += _shard_shape(n)[0]
    return offs, off


def _pack_groups(shards, layer, dtype):
    return [jnp.concatenate([shards[n][layer].astype(dtype) for n in group], axis=0) for group in GATHER_GROUPS]


def _unpack_full(gathered, groups):
    out = {}
    for group, arr in zip(groups, gathered):
        offs, _ = _group_rows(group)
        for n in group:
            rows, cols = _shard_shape(n)
            (k, m), ax = BIG_INFO[n]
            slab = arr[:, offs[n]:offs[n] + rows]
            out[n] = slab.reshape(k, m) if ax == 0 else jnp.transpose(slab, (1, 0, 2)).reshape(k, m)
    return out


def _pack_grads(gfull):
    out = []
    for group in GROUPS:
        parts = []
        for n in group:
            rows, cols = _shard_shape(n)
            ax = BIG_INFO[n][1]
            slab = (gfull[n].reshape(N_CHIPS, rows, cols) if ax == 0
                    else jnp.transpose(gfull[n].reshape(rows, N_CHIPS, cols), (1, 0, 2)))
            parts.append(slab)
        out.append(jnp.concatenate(parts, axis=1))
    return out


def _after(values, mark):
    values, _ = lax.optimization_barrier((values, mark))
    return values


def _reduce_scatter_begin(gsends, place, tag, ids):
    sibs = _sibling_halves(gsends, "rs_sibling_halves_" + tag, ids[0])
    tsends = [_chip_sums(g, s, place) for g, s in zip(gsends, sibs)]
    return tsends, _exchange_chip_sums(tsends, "rs_exchange_" + tag, ids[1])


def _reduce_scatter_finish(begun, place, tag, ids, hold):
    tsends, recvs = begun
    recvs = _after(recvs, hold)
    return _join_halves([_final_sum(t, r, place) for t, r in zip(tsends, recvs)], "rs_join_halves_" + tag, ids[2])


SMALL_SHAPES = {"rel_table": (NUM_BUCKETS, 2 * N_HEADS), "norm_mix_g": (DEPTH, D_MODEL), "qnorm_a_g": (DEPTH, HEAD_DIM),
                "knorm_a_g": (DEPTH, HEAD_DIM), "qnorm_b_g": (DEPTH, HEAD_DIM), "knorm_b_g": (DEPTH, HEAD_DIM),
                "sink_b": (DEPTH, N_HEADS), "norm_ffn_g": (DEPTH, D_MODEL), "norm_ple_g": (DEPTH, D_MODEL)}


def _pack_small(vals, last=None):
    flat = jnp.concatenate([vals[n].astype(F32).reshape(-1) for n in SMALL])
    tail = jnp.zeros((SMALL_ROWS * LANES - flat.shape[0],), F32)
    if last is not None:
        tail = tail.at[-1].set(last)
    return jnp.concatenate([flat, tail]).reshape(SMALL_ROWS, LANES)


def _unpack_small(packed):
    flat, out, off = packed.reshape(-1), {}, 0
    for n in SMALL:
        size = math.prod(SMALL_SHAPES[n])
        out[n] = flat[off:off + size].reshape(SMALL_SHAPES[n])
        off += size
    return out


def _adamw(w, gs, g_row, m, v, name):
    c1 = 1.0 - ADAM_B1 ** ADAM_STEP
    c2 = 1.0 - ADAM_B2 ** ADAM_STEP
    total, width = w.shape
    n_layers = len(gs)
    per = total // n_layers
    tm = max(t for t in range(8, 513, 8) if per % t == 0 and g_row % t == 0)
    nblk = per // tm

    def body(*refs):
        w_ref, g_refs = refs[0], refs[1:1 + n_layers]
        m_ref, v_ref, og, od, om, ov = refs[1 + n_layers:]
        layer = pl.program_id(0) // nblk
        g = g_refs[0][...]
        for l in range(1, n_layers):
            g = jnp.where(layer == l, g_refs[l][...], g)
        m_new = ADAM_B1 * m_ref[...] + (1.0 - ADAM_B1) * g
        v_new = ADAM_B2 * v_ref[...] + (1.0 - ADAM_B2) * (g * g)
        og[...] = g
        od[...] = -ADAM_LR * ((m_new / c1) / (jnp.sqrt(v_new / c2) + ADAM_EPS) + ADAM_WD * w_ref[...])
        om[...] = m_new
        ov[...] = v_new

    row = pl.BlockSpec((tm, width), lambda i: (i, 0))
    g_specs = [pl.BlockSpec((tm, width), lambda i, l=l: (g_row // tm + jnp.clip(i - l * nblk, 0, nblk - 1), 0))
               for l in range(n_layers)]
    return pl.pallas_call(
        body, out_shape=[jax.ShapeDtypeStruct((total, width), F32)] * 4, grid=(total // tm,),
        in_specs=[row] + g_specs + [row, row], out_specs=[row] * 4, name=name,
        compiler_params=_params("parallel"))(w, *gs, m, v)


def kernel(x, p, rel_table, norm_mix_g, w_in, qnorm_a_g, knorm_a_g, qnorm_b_g, knorm_b_g, sink_b, w_branch_a, w_branch_b, w_out, norm_ffn_g, w_ffn_gate, w_ffn_up, w_ffn_down, norm_ple_g, w_ple_gate, w_ple_proj, loss_target, m_rel_table, m_norm_mix_g, m_w_in, m_qnorm_a_g, m_knorm_a_g, m_qnorm_b_g, m_knorm_b_g, m_sink_b, m_w_branch_a, m_w_branch_b, m_w_out, m_norm_ffn_g, m_w_ffn_gate, m_w_ffn_up, m_w_ffn_down, m_norm_ple_g, m_w_ple_gate, m_w_ple_proj, v_rel_table, v_norm_mix_g, v_w_in, v_qnorm_a_g, v_knorm_a_g, v_qnorm_b_g, v_knorm_b_g, v_sink_b, v_w_branch_a, v_w_branch_b, v_w_out, v_norm_ffn_g, v_w_ffn_gate, v_w_ffn_up, v_w_ffn_down, v_norm_ple_g, v_w_ple_gate, v_w_ple_proj):
    given = dict(locals())

    def held(name, a):
        return jnp.swapaxes(a, 1, 2) if name in TRANSPOSED else a

    weights = {n: held(n, given[n]) for n in WEIGHTS}
    moments_m = {n: held(n, given["m_" + n]) for n in WEIGHTS}
    moments_v = {n: held(n, given["v_" + n]) for n in WEIGHTS}
    xi, yi, ci = _place()
    place = jnp.stack([2 * xi + yi, ci]).astype(jnp.int32)

    shards = [_pack_groups(weights, l, BF16) for l in range(DEPTH)]
    w_in0 = _allgather(shards[0][:1], "allgather_w_in_layer0", sequencer=9)
    gathered = [None, None]
    small = {n: weights[n] for n in SMALL}

    def w_in_of(l, mark):
        if l == 0:
            landed = _after(w_in0, (shards[1], mark))
            gathered[0] = landed + _allgather(_after(shards[0][1:], landed), "allgather_rest_layer0", sequencer=1)
            return _unpack_full(landed, GATHER_GROUPS[:1])["w_in"]
        return _unpack_full(_after(gathered[l][:1], mark), GATHER_GROUPS[:1])["w_in"]

    def rest_of(l, mark):
        if l == 0:
            gathered[1] = _allgather(_after(shards[1], mark), "allgather_layer1", sequencer=2)
        return _unpack_full(_after(gathered[l][1:], mark), GATHER_GROUPS[1:])

    loss, dx, gbig, gsmall, marks = _local_step(x[0], p[:, 0], loss_target[0], w_in_of, rest_of, small)

    gsends = [_pack_grads(gbig[l]) for l in range(DEPTH)]
    stages = {"layer1": (gsends[1], (3, 4, 5)), "rest_layer0": (gsends[0][1:], (6, 7, 8)),
              "w_in_layer0": (gsends[0][:1], (10, 11, 12))}
    begun = {tag: _reduce_scatter_begin(g, place, tag, ids) for tag, (g, ids) in stages.items()}

    def finish(tag, hold):
        return _reduce_scatter_finish(begun[tag], place, tag, stages[tag][1], hold)

    red1 = finish("layer1", marks[0]["attn_bwd_done"])
    rest0 = finish("rest_layer0", gbig[0]["w_in"])

    grads, delta, new_m, new_v = {}, {}, {}, {}

    def update(group, reduced):
        offs, _ = _group_rows(group)
        for n in group:
            shape = weights[n].shape
            two_d = lambda a: a.reshape(shape[0] * shape[1], shape[2])
            outs = _adamw(two_d(weights[n]), reduced, offs[n], two_d(moments_m[n]), two_d(moments_v[n]), "adamw_" + n)
            grads[n], delta[n], new_m[n], new_v[n] = (held(n, o.reshape(shape)) for o in outs)

    for gi in (1, 2):
        update(GROUPS[gi], _after([rest0[gi - 1], red1[gi]], begun["w_in_layer0"][0]))
    small_grads = _allreduce_small(_pack_small(gsmall, last=loss))
    g_, d_, m_, v_ = _adamw(_pack_small(weights), [small_grads], 0, _pack_small(moments_m), _pack_small(moments_v),
                            "adamw_small")
    grads.update(_unpack_small(g_))
    delta.update(_unpack_small(d_))
    new_m.update(_unpack_small(m_))
    new_v.update(_unpack_small(v_))
    others_done = [dx, d_] + [delta[n] for gi in (1, 2) for n in GROUPS[gi]]
    update(GROUPS[0], [finish("w_in_layer0", others_done)[0], red1[0]])

    return (small_grads[-1, -1], dx[None], *[grads[n] for n in WEIGHTS], *[delta[n] for n in WEIGHTS],
            *[new_m[n] for n in WEIGHTS], *[new_v[n] for n in WEIGHTS])
```

```python
import functools
import math

import jax
import jax.numpy as jnp
from jax import lax
from jax.experimental import pallas as pl
from jax.experimental.pallas import tpu as pltpu
from jax.experimental.pallas import tpu_sc as plsc

F32 = jnp.float32
BF16 = jnp.bfloat16
MESH_ID = pl.DeviceIdType.MESH

D_MODEL = 1024
DEPTH = 2
HEAD_DIM = 64
N_HEADS = 8
WIDTH = N_HEADS * HEAD_DIM
N_PAIRS = 4
ITEMS = 16
MAX_CHUNK = 1024
N_KV_B = 2
PLE_DIM = 256
D_FF = 2816
D_IN = 4352
OFF_QA, OFF_KA, OFF_VA, OFF_QB, OFF_KB, OFF_VB, OFF_GA, OFF_GB = 0, 512, 1024, 1536, 2048, 2176, 2304, 3328
DILATED = ((64, 1), (64, 4), (64, 16))
BLK_B = 128
NUM_BUCKETS = 32
MAX_DISTANCE = 1024
RMS_EPS = 1e-6
NEG_INF = -1e30
LANES = 128
ROW_TILE = 256
VMEM_LIMIT = 48 * 1024 * 1024

ADAM_LR, ADAM_B1, ADAM_B2, ADAM_EPS, ADAM_WD, ADAM_STEP = 0.001, 0.9, 0.999, 1e-08, 0.01, 10

TRANSPOSED = ("w_in", "w_ffn_gate", "w_ffn_up")
BIG = (
    ("w_in", (D_IN, D_MODEL), 0),
    ("w_branch_a", (WIDTH, D_MODEL), 1),
    ("w_branch_b", (WIDTH, D_MODEL), 1),
    ("w_out", (D_MODEL, D_MODEL), 0),
    ("w_ffn_gate", (D_FF, D_MODEL), 0),
    ("w_ffn_up", (D_FF, D_MODEL), 0),
    ("w_ffn_down", (D_FF, D_MODEL), 0),
    ("w_ple_gate", (D_MODEL, D_MODEL), 0),
    ("w_ple_proj", (PLE_DIM, D_MODEL), 1),
)
SMALL = ("rel_table", "norm_mix_g", "qnorm_a_g", "knorm_a_g", "qnorm_b_g", "knorm_b_g", "sink_b",
         "norm_ffn_g", "norm_ple_g")
WEIGHTS = ("rel_table", "norm_mix_g", "w_in", "qnorm_a_g", "knorm_a_g", "qnorm_b_g", "knorm_b_g", "sink_b",
           "w_branch_a", "w_branch_b", "w_out", "norm_ffn_g", "w_ffn_gate", "w_ffn_up", "w_ffn_down",
           "norm_ple_g", "w_ple_gate", "w_ple_proj")
N_CHIPS = 4
SMALL_ROWS = 64


def _params(*sem):
    return pltpu.CompilerParams(dimension_semantics=sem, vmem_limit_bytes=VMEM_LIMIT)


MM_VMEM_BUDGET = 40 * 1024 * 1024
STEP_OVERHEAD_S = 0.4e-6
TILE_DMA_BYTES_PER_S = 1.5e12


def _mm_dims(a, b, mode):
    if mode == "nn":
        return a.shape[0], b.shape[1], a.shape[1]
    if mode == "nt":
        return a.shape[0], b.shape[0], a.shape[1]
    return a.shape[1], b.shape[1], a.shape[0]


def _mm_tiles(m, n, pairs, tile_bytes, col_offsets, full_rows=False):
    best = None
    widths = [n] if full_rows else [t for t in range(LANES, n + 1, LANES) if n % t == 0 and all(o % t == 0 for o in col_offsets)]
    for tm in (t for t in range(LANES, m + 1, LANES) if m % t == 0):
        for tn in widths:
            io = sum(tm * k * ab + tn * k * bb for k, ab, bb in pairs) + tm * tn * sum(tile_bytes)
            casts = sum((tm * k * 2 if ab == 4 else 0) + (tn * k * 2 if bb == 4 else 0) for k, ab, bb in pairs)
            if 2 * io + len(pairs) * tm * tn * 4 + casts > MM_VMEM_BUDGET:
                continue
            cost = (m // tm) * (n // tn) * STEP_OVERHEAD_S + io / TILE_DMA_BYTES_PER_S
            if best is None or (cost, -tm) < best[0]:
                best = ((cost, -tm), tm, tn)
    return best[1], best[2]


def _mm_fused(pairs, extras, epilogue, out_dtypes, name, next_gain=None):
    m, n, _ = _mm_dims(*pairs[0])
    assert all(_mm_dims(*p)[:2] == (m, n) for p in pairs)
    with_norm = next_gain is not None
    out_dtypes = list(out_dtypes) + ([BF16] if with_norm else [])
    tm, tn = _mm_tiles(
        m, n, [(_mm_dims(a, b, mode)[2], a.dtype.itemsize, b.dtype.itemsize) for a, b, mode in pairs],
        [e.dtype.itemsize for e, _ in extras] + [jnp.dtype(d).itemsize for d in out_dtypes], [off for _, off in extras],
        full_rows=with_norm)
    dims = {"nn": (((1,), (0,)), ((), ())), "nt": (((1,), (1,)), ((), ())), "tn": (((0,), (0,)), ((), ()))}
    in_specs, args = [], []
    for a, b, mode in pairs:
        k = _mm_dims(a, b, mode)[2]
        in_specs.append(pl.BlockSpec((k, tm), lambda i, j: (0, i)) if mode == "tn" else pl.BlockSpec((tm, k), lambda i, j: (i, 0)))
        in_specs.append(pl.BlockSpec((tn, k), lambda i, j: (j, 0)) if mode == "nt" else pl.BlockSpec((k, tn), lambda i, j: (0, j)))
        args += [a, b]
    for e, off in extras:
        in_specs.append(pl.BlockSpec((tm, tn), lambda i, j, o=off // tn: (i, o + j)))
        args.append(e)
    n_pairs, n_tiles = len(pairs), 2 * len(pairs) + len(extras)
    if with_norm:
        in_specs.append(pl.BlockSpec((1, n), lambda i, j: (0, 0)))
        args.append(next_gain)
    n_in = len(args)

    def body(*refs):
        products = [lax.dot_general(refs[2 * p][...].astype(BF16), refs[2 * p + 1][...].astype(BF16), dims[pairs[p][2]],
                                    preferred_element_type=F32) for p in range(n_pairs)]
        outs = list(epilogue(products, [r[...] for r in refs[2 * n_pairs:n_tiles]]))
        if with_norm:
            y = outs[0]
            outs.append((y * lax.rsqrt(jnp.mean(y * y, axis=-1, keepdims=True) + RMS_EPS)) * refs[n_tiles][...])
        for r, o in zip(refs[n_in:], outs):
            r[...] = o.astype(r.dtype)

    tile = pl.BlockSpec((tm, tn), lambda i, j: (i, j))
    return pl.pallas_call(
        body, out_shape=[jax.ShapeDtypeStruct((m, n), d) for d in out_dtypes], grid=(m // tm, n // tn),
        in_specs=in_specs, out_specs=[tile] * len(out_dtypes), name=name,
        compiler_params=_params("parallel", "parallel"))(*args)


def _mm(a, b, mode, out_dtype, name, res=None):
    if res is None:
        return _mm_fused([(a, b, mode)], [], lambda products, extra: products, [out_dtype], name)[0]
    return _mm_fused([(a, b, mode)], [(res, 0)], lambda products, extra: [products[0] + extra[0]], [out_dtype], name)[0]


def _mm_res_norm(a, b, mode, res, gain, name):
    return _mm_fused([(a, b, mode)], [(res, 0)], lambda products, extra: [products[0] + extra[0]], [F32], name,
                     next_gain=gain)


def _ew(fn, ins, out_dtypes, name):
    rows, width = ins[0].shape
    n_in = len(ins)

    def body(*refs):
        outs = fn(*[r[...] for r in refs[:n_in]])
        for r, o in zip(refs[n_in:], outs):
            r[...] = o.astype(r.dtype)

    row = pl.BlockSpec((ROW_TILE, width), lambda i: (i, 0))
    return pl.pallas_call(
        body, out_shape=[jax.ShapeDtypeStruct((rows, width), dt) for dt in out_dtypes], grid=(rows // ROW_TILE,),
        in_specs=[row] * n_in, out_specs=[row] * len(out_dtypes), name=name, compiler_params=_params("parallel"))(*ins)


def _sigmoid(x):
    return 1.0 / (1.0 + jnp.exp(-x))


def _seg_sum(v):
    outs = []
    for k in range(v.shape[1] // LANES):
        vp = v[:, k * LANES:(k + 1) * LANES]
        left = lax.broadcasted_iota(jnp.int32, vp.shape, 1) < HEAD_DIM
        sl = jnp.sum(jnp.where(left, vp, 0.0), axis=-1, keepdims=True)
        sr = jnp.sum(jnp.where(left, 0.0, vp), axis=-1, keepdims=True)
        outs.append(jnp.where(left, sl, sr))
    return outs[0] if len(outs) == 1 else jnp.concatenate(outs, axis=1)


def _seg_rstd(x):
    return lax.rsqrt(_seg_sum(x * x) * (1.0 / HEAD_DIM) + RMS_EPS)


def _rms_fwd(x, g, name):
    rows, d = x.shape
    tm = ROW_TILE

    def body(x_ref, g_ref, h_ref):
        xv = x_ref[...]
        r = lax.rsqrt(jnp.mean(xv * xv, axis=-1, keepdims=True) + RMS_EPS)
        h_ref[...] = ((xv * r) * g_ref[...]).astype(BF16)

    return pl.pallas_call(
        body, out_shape=jax.ShapeDtypeStruct((rows, d), BF16), grid=(rows // tm,),
        in_specs=[pl.BlockSpec((tm, d), lambda i: (i, 0)), pl.BlockSpec((1, d), lambda i: (0, 0))],
        out_specs=pl.BlockSpec((tm, d), lambda i: (i, 0)), name=name,
        compiler_params=_params("parallel"))(x, g)


def _mm_rms_bwd(pairs, x, g, dres, name):
    rows, d = x.shape
    assert all(_mm_dims(*p)[:2] == (rows, d) for p in pairs)
    kbytes = [(_mm_dims(a, b, mode)[2], a.dtype.itemsize, b.dtype.itemsize) for a, b, mode in pairs]
    tm = max(t for t in (512, 256, 128) if rows % t == 0 and
             2 * (sum(t * k * ab + d * k * bb for k, ab, bb in kbytes) + t * d * 14) + 2 * t * d * 4 <= MM_VMEM_BUDGET)
    dims = {"nn": (((1,), (0,)), ((), ())), "nt": (((1,), (1,)), ((), ()))}
    n_pairs = len(pairs)

    def body(*refs):
        x_ref, g_ref, dres_ref = refs[2 * n_pairs:2 * n_pairs + 3]
        dx_ref, dxb_ref, dg_ref = refs[2 * n_pairs + 3:]
        dhv = functools.reduce(lambda u, v: u + v, [
            lax.dot_general(refs[2 * p][...].astype(BF16), refs[2 * p + 1][...].astype(BF16), dims[pairs[p][2]],
                            preferred_element_type=F32) for p in range(n_pairs)])
        xv = x_ref[...]
        r = lax.rsqrt(jnp.mean(xv * xv, axis=-1, keepdims=True) + RMS_EPS)
        xh = xv * r
        dxh = dhv * g_ref[...]
        dxv = dres_ref[...] + r * (dxh - xh * jnp.mean(dxh * xh, axis=-1, keepdims=True))
        dx_ref[...] = dxv
        dxb_ref[...] = dxv.astype(BF16)
        part = jnp.sum(dhv * xh, axis=0, keepdims=True)

        @pl.when(pl.program_id(0) == 0)
        def _():
            dg_ref[...] = part

        @pl.when(pl.program_id(0) > 0)
        def _():
            dg_ref[...] += part

    row = pl.BlockSpec((tm, d), lambda i: (i, 0))
    vec = pl.BlockSpec((1, d), lambda i: (0, 0))
    in_specs, args = [], []
    for a, b, mode in pairs:
        in_specs += [pl.BlockSpec((tm, a.shape[1]), lambda i: (i, 0)), pl.BlockSpec(b.shape, lambda i: (0, 0))]
        args += [a, b]
    return pl.pallas_call(
        body, out_shape=[jax.ShapeDtypeStruct((rows, d), F32), jax.ShapeDtypeStruct((rows, d), BF16),
                         jax.ShapeDtypeStruct((1, d), F32)],
        grid=(rows // tm,), in_specs=in_specs + [row, vec, row], out_specs=[row, row, vec],
        name=name, compiler_params=_params("arbitrary"))(*args, x, g, dres)


def _ple_bwd(dx, z, e):
    s = _sigmoid(z.astype(F32))
    return dx * s, dx * e.astype(F32) * (s * (1.0 - s))


def _loss_grad(y, t, z, e):
    rows, d = y.shape
    tm = ROW_TILE

    def body(y_ref, t_ref, z_ref, e_ref, dy_ref, de_ref, dz_ref, l_ref):
        err = y_ref[...] - t_ref[...]
        dy = err * (1.0 / d)
        dy_ref[...] = dy
        de, dz = _ple_bwd(dy, z_ref[...], e_ref[...])
        de_ref[...] = de.astype(BF16)
        dz_ref[...] = dz.astype(BF16)
        part = jnp.zeros((1, LANES), F32) + jnp.sum(err * err) * (0.5 / d)

        @pl.when(pl.program_id(0) == 0)
        def _():
            l_ref[...] = part

        @pl.when(pl.program_id(0) > 0)
        def _():
            l_ref[...] += part

    row = pl.BlockSpec((tm, d), lambda i: (i, 0))
    return pl.pallas_call(
        body, out_shape=[jax.ShapeDtypeStruct((rows, d), F32), jax.ShapeDtypeStruct((rows, d), BF16),
                         jax.ShapeDtypeStruct((rows, d), BF16), jax.ShapeDtypeStruct((1, LANES), F32)],
        grid=(rows // tm,), in_specs=[row] * 4, out_specs=[row, row, row, pl.BlockSpec((1, LANES), lambda i: (0, 0))],
        name="loss_grad", compiler_params=_params("arbitrary"))(y, t, z, e)


def _swap_halves(v):
    return pltpu.roll(v, HEAD_DIM, axis=1)


def _expand_kv(kv):
    left = lax.broadcasted_iota(jnp.int32, kv.shape, 1) < HEAD_DIM
    sw = _swap_halves(kv)
    h0 = jnp.where(left, kv, sw)
    h1 = jnp.where(left, sw, kv)
    return jnp.concatenate([h0, h0, h1, h1], axis=1)


def _reduce_kv(dkv):
    left = lax.broadcasted_iota(jnp.int32, (dkv.shape[0], LANES), 1) < HEAD_DIM
    t = dkv[:, 0:LANES] + dkv[:, LANES:2 * LANES]
    u = dkv[:, 2 * LANES:3 * LANES] + dkv[:, 3 * LANES:4 * LANES]
    t = t + _swap_halves(t)
    u = u + _swap_halves(u)
    return jnp.where(left, t, u)


def _qknorm_fwd(proj, gqa, gka, gqb, gkb):
    rows = proj.shape[0]
    tm = ROW_TILE

    def body(qa_ref, ka_ref, qb_ref, kb_ref, vb_ref, gqa_ref, gka_ref, gqb_ref, gkb_ref, oqa, oka, oqb, okb, ovb):
        for src, g_ref, dst in ((qa_ref, gqa_ref, oqa), (ka_ref, gka_ref, oka), (qb_ref, gqb_ref, oqb)):
            xv = src[...]
            dst[...] = (xv * _seg_rstd(xv)) * g_ref[...]
        kv = kb_ref[...]
        okb[...] = _expand_kv((kv * _seg_rstd(kv)) * gkb_ref[...])
        ovb[...] = _expand_kv(vb_ref[...])

    def win(width, off):
        return pl.BlockSpec((tm, width), lambda i: (i, off // width))

    vec = lambda w: pl.BlockSpec((1, w), lambda i: (0, 0))
    out = pl.BlockSpec((tm, WIDTH), lambda i: (i, 0))
    return pl.pallas_call(
        body, out_shape=[jax.ShapeDtypeStruct((rows, WIDTH), F32)] * 5, grid=(rows // tm,),
        in_specs=[win(WIDTH, OFF_QA), win(WIDTH, OFF_KA), win(WIDTH, OFF_QB), win(LANES, OFF_KB), win(LANES, OFF_VB),
                  vec(WIDTH), vec(WIDTH), vec(WIDTH), vec(LANES)],
        out_specs=[out] * 5, name="qknorm_fwd", compiler_params=_params("parallel"))(
            proj, proj, proj, proj, proj, gqa, gka, gqb, gkb)


def _norm_bwd(xv, g, dy):
    r = _seg_rstd(xv)
    xh = xv * r
    dxh = dy * g
    dx = r * (dxh - xh * (_seg_sum(dxh * xh) * (1.0 / HEAD_DIM)))
    return dx, jnp.sum(dy * xh, axis=0, keepdims=True)


def _qknorm_bwd(proj, gqa, gka, gqb, gkb, dqa, dka, dva, dqb, dkb, dvb, dga, dgb):
    rows = proj.shape[0]
    tm = ROW_TILE
    n_a = len(dqa)

    def body(*refs):
        qa_ref, ka_ref, qb_ref, kb_ref, gqa_ref, gka_ref, gqb_ref, gkb_ref = refs[:8]
        pos = 8
        dqa_refs, dka_refs, dva_refs = refs[pos:pos + n_a], refs[pos + n_a:pos + 2 * n_a], refs[pos + 2 * n_a:pos + 3 * n_a]
        pos += 3 * n_a
        dqb_ref, dkb_ref, dvb_ref, dga_ref, dgb_ref = refs[pos:pos + 5]
        dproj_ref, ogqa, ogka, ogqb, ogkb = refs[pos + 5:]

        def total(rs):
            acc = rs[0][...]
            for r in rs[1:]:
                acc = acc + r[...]
            return acc

        dx_qa, p_qa = _norm_bwd(qa_ref[...], gqa_ref[...], total(dqa_refs))
        dx_ka, p_ka = _norm_bwd(ka_ref[...], gka_ref[...], total(dka_refs))
        dx_qb, p_qb = _norm_bwd(qb_ref[...], gqb_ref[...], dqb_ref[...])
        dx_kb, p_kb = _norm_bwd(kb_ref[...], gkb_ref[...], _reduce_kv(dkb_ref[...]))
        dproj_ref[:, OFF_QA:OFF_QA + WIDTH] = dx_qa.astype(BF16)
        dproj_ref[:, OFF_KA:OFF_KA + WIDTH] = dx_ka.astype(BF16)
        dproj_ref[:, OFF_VA:OFF_VA + WIDTH] = total(dva_refs).astype(BF16)
        dproj_ref[:, OFF_QB:OFF_QB + WIDTH] = dx_qb.astype(BF16)
        dproj_ref[:, OFF_KB:OFF_KB + LANES] = dx_kb.astype(BF16)
        dproj_ref[:, OFF_VB:OFF_VB + LANES] = _reduce_kv(dvb_ref[...]).astype(BF16)
        dproj_ref[:, OFF_GA:OFF_GB] = dga_ref[...]
        dproj_ref[:, OFF_GB:D_IN] = dgb_ref[...]
        first = pl.program_id(0) == 0
        for o_ref, part in ((ogqa, p_qa), (ogka, p_ka), (ogqb, p_qb), (ogkb, p_kb)):
            @pl.when(first)
            def _(o_ref=o_ref, part=part):
                o_ref[...] = part

            @pl.when(jnp.logical_not(first))
            def _(o_ref=o_ref, part=part):
                o_ref[...] += part

    def win(width, off):
        return pl.BlockSpec((tm, width), lambda i: (i, off // width))

    vec = lambda w: pl.BlockSpec((1, w), lambda i: (0, 0))
    row = lambda w: pl.BlockSpec((tm, w), lambda i: (i, 0))
    in_specs = [win(WIDTH, OFF_QA), win(WIDTH, OFF_KA), win(WIDTH, OFF_QB), win(LANES, OFF_KB),
                vec(WIDTH), vec(WIDTH), vec(WIDTH), vec(LANES)]
    in_specs += [row(WIDTH)] * (3 * n_a + 3) + [row(D_MODEL)] * 2
    return pl.pallas_call(
        body,
        out_shape=[jax.ShapeDtypeStruct((rows, D_IN), BF16), jax.ShapeDtypeStruct((1, WIDTH), F32),
                   jax.ShapeDtypeStruct((1, WIDTH), F32), jax.ShapeDtypeStruct((1, WIDTH), F32),
                   jax.ShapeDtypeStruct((1, LANES), F32)],
        grid=(rows // tm,), in_specs=in_specs,
        out_specs=[row(D_IN), vec(WIDTH), vec(WIDTH), vec(WIDTH), vec(LANES)],
        name="qknorm_bwd", compiler_params=_params("arbitrary"))(
            proj, proj, proj, proj, gqa, gka, gqb, gkb, *dqa, *dka, *dva, dqb, dkb, dvb, dga, dgb)


def _t5_bucket(rel):
    half_b = NUM_BUCKETS // 2
    max_exact = half_b // 2
    sign = jnp.where(rel > 0, half_b, 0)
    n = jnp.abs(rel)
    nf = jnp.maximum(n, 1).astype(F32)
    large = max_exact + (jnp.log(nf / max_exact) / math.log(MAX_DISTANCE / max_exact)
                         * (half_b - max_exact)).astype(jnp.int32)
    large = jnp.minimum(large, half_b - 1)
    return sign + jnp.where(n < max_exact, n, large)


def _band_buckets(blk, dilation):
    i = jnp.arange(blk, dtype=jnp.int32)[:, None]
    j = jnp.arange(3 * blk, dtype=jnp.int32)[None, :]
    rel = j - blk - i
    return jnp.where(jnp.abs(rel) <= blk, _t5_bucket(rel * dilation), -1)


def _bias_tiles(table, buckets, head_off, name):
    blk = buckets.shape[0]

    def body(tab_ref, bk_ref, o_ref):
        h = pl.program_id(0) + head_off
        bk = bk_ref[...]
        acc = jnp.full(bk.shape, NEG_INF, F32)
        for b in range(NUM_BUCKETS):
            acc = jnp.where(bk == b, tab_ref[b, h], acc)
        o_ref[0] = acc

    return pl.pallas_call(
        body, out_shape=jax.ShapeDtypeStruct((N_HEADS, blk, 3 * blk), F32), grid=(N_HEADS,),
        in_specs=[pl.BlockSpec(memory_space=pltpu.SMEM), pl.BlockSpec((blk, 3 * blk), lambda h: (0, 0))],
        out_specs=pl.BlockSpec((1, blk, 3 * blk), lambda h: (h, 0, 0)),
        name=name, compiler_params=_params("parallel"))(table, buckets)


def _table_grad(dbias, buckets, name):
    blk = buckets.shape[0]

    def body(db_ref, bk_ref, o_ref):
        bk = bk_ref[...]
        dbv = db_ref[0]
        lane = lax.broadcasted_iota(jnp.int32, (1, LANES), 1)
        acc = jnp.zeros((1, LANES), F32)
        for b in range(NUM_BUCKETS):
            acc = jnp.where(lane == b, jnp.sum(jnp.where(bk == b, dbv, 0.0)), acc)
        o_ref[0] = acc

    out = pl.pallas_call(
        body, out_shape=jax.ShapeDtypeStruct((N_HEADS, 1, LANES), F32), grid=(N_HEADS,),
        in_specs=[pl.BlockSpec((1, blk, 3 * blk), lambda h: (h, 0, 0)), pl.BlockSpec((blk, 3 * blk), lambda h: (0, 0))],
        out_specs=pl.BlockSpec((1, 1, LANES), lambda h: (h, 0, 0)),
        name=name, compiler_params=_params("parallel"))(dbias, buckets)
    return out[:, 0, :NUM_BUCKETS]


def _dot_nt(a, b):
    return lax.dot_general(a, b, (((1,), (1,)), ((), ())), preferred_element_type=F32)


def _dot_tn(a, b):
    return lax.dot_general(a, b, (((0,), (0,)), ((), ())), preferred_element_type=F32)


def _stack_pair(x2, left):
    return jnp.concatenate([jnp.where(left, x2, 0.0), jnp.where(left, 0.0, x2)], axis=0).astype(BF16)


def _attn_geometry(blk, d):
    halo = blk * d
    subs = max(1, min(ITEMS // d, MAX_CHUNK // halo))
    return subs, min(d, ITEMS // subs), halo


def _item_of(j, r0, per_group):
    return j // per_group, r0 + j % per_group


def _span_rows(ref, first, r, blk, d):
    if d == 1:
        return ref[first:first + blk, :]
    return ref[pl.ds(first + r, blk, stride=d), :]


def _set_span_rows(ref, first, r, blk, d, val, add=False):
    idx = slice(first, first + blk) if d == 1 else pl.ds(first + r, blk, stride=d)
    ref[idx, :] = ref[idx, :] + val if add else val


def _for_groups(group, groups, per_group):
    if groups == 1:
        group(0)
    else:
        def step(g, carry):
            group(g * per_group)
            return carry

        lax.fori_loop(0, groups, step, 0)


def _key_rows(p_ref, c_ref, n_ref, s, r, subs, halo, blk, d):
    parts = []
    for span in (s - 1, s, s + 1):
        if span < 0:
            parts.append(_span_rows(p_ref, 0, r, blk, d))
        elif span == subs:
            parts.append(_span_rows(n_ref, 0, r, blk, d))
        else:
            parts.append(_span_rows(c_ref, span * halo, r, blk, d))
    return jnp.concatenate(parts, axis=0)


def _item_penalty(t, nct, s, subs, blk):
    first_ok = True if s > 0 else t > 0
    last_ok = True if s < subs - 1 else t < nct - 1
    col = lax.broadcasted_iota(jnp.int32, (1, 3 * blk), 1)
    ok = jnp.logical_and(jnp.logical_or(col >= blk, first_ok), jnp.logical_or(col < 2 * blk, last_ok))
    return jnp.where(ok, 0.0, NEG_INF).astype(F32)


def _attn_specs(seq, blk, d, step_of, col=0):
    subs, _, halo = _attn_geometry(blk, d)
    last, first = seq // halo - 1, col // LANES
    cur = pl.BlockSpec((subs * halo, LANES), lambda hp, t: (step_of(t), first + hp))
    prev = pl.BlockSpec((halo, LANES), lambda hp, t: (jnp.clip(step_of(t) * subs - 1, 0, last), first + hp))
    nxt = pl.BlockSpec((halo, LANES), lambda hp, t: (jnp.minimum((step_of(t) + 1) * subs, last), first + hp))
    return cur, prev, nxt


def _attn_fwd(q, k, v, bias, sink, blk, d, name, v_col=0):
    seq = q.shape[0]
    subs, per_group, halo = _attn_geometry(blk, d)
    items, chunk, groups = subs * per_group, subs * halo, d // per_group
    nct = seq // chunk
    has_sink = sink is not None
    scale = HEAD_DIM ** -0.5

    def body(*refs):
        q_ref, kp, kc, kn, vp, vc, vn, b_ref = refs[:8]
        s_ref = refs[8] if has_sink else None
        o_ref, l_ref = refs[-2], refs[-1]
        t = pl.program_id(1)
        left = lax.broadcasted_iota(jnp.int32, (1, LANES), 1) < HEAD_DIM
        bias2 = b_ref[...]

        def group(r0):
            scores, vcats = [], []
            for j in range(items):
                s, r = _item_of(j, r0, per_group)
                qs = _stack_pair(_span_rows(q_ref, s * halo, r, blk, d) * scale, left)
                kcat = _key_rows(kp, kc, kn, s, r, subs, halo, blk, d).astype(BF16)
                scores.append(_dot_nt(qs, kcat) + bias2 + _item_penalty(t, nct, s, subs, blk))
                vcats.append(_key_rows(vp, vc, vn, s, r, subs, halo, blk, d).astype(BF16))
            ms = [jnp.max(s, axis=-1, keepdims=True) for s in scores]
            if has_sink:
                sk = s_ref[...]
                ms = [jnp.maximum(m, sk) for m in ms]
            ps = [jnp.exp(s - m) for s, m in zip(scores, ms)]
            dens = [jnp.sum(p, axis=-1, keepdims=True) for p in ps]
            if has_sink:
                dens = [den + jnp.exp(sk - m) for den, m in zip(dens, ms)]
            pns = [(p * (1.0 / den)).astype(BF16) for p, den in zip(ps, dens)]
            lses = [m + jnp.log(den) for m, den in zip(ms, dens)]
            for j in range(items):
                s, r = _item_of(j, r0, per_group)
                o2 = jnp.dot(pns[j], vcats[j], preferred_element_type=F32)
                _set_span_rows(o_ref, s * halo, r, blk, d, jnp.where(left, o2[:blk], o2[blk:]))
                _set_span_rows(l_ref, s * halo, r, blk, d, jnp.where(left, lses[j][:blk], lses[j][blk:]))

        _for_groups(group, groups, per_group)

    cur, prev, nxt = _attn_specs(seq, blk, d, lambda t: t)
    v_cur, v_prev, v_nxt = _attn_specs(seq, blk, d, lambda t: t, v_col)
    in_specs = [cur, prev, cur, nxt, v_prev, v_cur, v_nxt, pl.BlockSpec((2 * blk, 3 * blk), lambda hp, t: (hp, 0))]
    args = [q, k, k, k, v, v, v, bias]
    if has_sink:
        in_specs.append(pl.BlockSpec((2 * blk, 1), lambda hp, t: (hp, 0)))
        args.append(sink)
    return pl.pallas_call(
        body, out_shape=[jax.ShapeDtypeStruct((seq, WIDTH), F32)] * 2, grid=(N_PAIRS, nct),
        in_specs=in_specs, out_specs=[cur, cur], name=name,
        compiler_params=_params("parallel", "parallel"))(*args)


def _attn_bwd(q, k, v, do, lse, delta, bias, sink, blk, d, name, v_col=0):
    seq = q.shape[0]
    subs, per_group, halo = _attn_geometry(blk, d)
    items, chunk, groups = subs * per_group, subs * halo, d // per_group
    nct = seq // chunk
    has_sink = sink is not None
    n_in = 12 if has_sink else 11
    scale = HEAD_DIM ** -0.5

    def body(*refs):
        q_ref, kp, kc, kn, vp, vc, vn, do_ref, l_ref, d_ref, b_ref = refs[:11]
        s_ref = refs[11] if has_sink else None
        dq_ref, dk_ref, dv_ref, db_ref = refs[n_in:n_in + 4]
        ds_ref = refs[n_in + 4] if has_sink else None
        wk, wv = refs[-2], refs[-1]
        t = pl.program_id(1)

        @pl.when(t == 0)
        def _():
            wk[...] = jnp.zeros_like(wk)
            wv[...] = jnp.zeros_like(wv)
            db_ref[...] = jnp.zeros_like(db_ref)
            if has_sink:
                ds_ref[...] = jnp.zeros_like(ds_ref)

        @pl.when(t > 0)
        def _():
            for w in (wk, wv):
                keep = w[chunk:2 * chunk + halo]
                w[0:chunk + halo] = keep
                w[chunk + halo:2 * chunk + halo] = jnp.zeros((chunk, LANES), F32)

        @pl.when(t < nct)
        def _():
            lane = lax.broadcasted_iota(jnp.int32, (1, LANES), 1)
            left = lane < HEAD_DIM
            bias2 = b_ref[...]

            def group(r0):
                qss, doss, kcats, scores, dps, lcols, dcols = [], [], [], [], [], [], []
                for j in range(items):
                    s, r = _item_of(j, r0, per_group)
                    qs = _stack_pair(_span_rows(q_ref, s * halo, r, blk, d) * scale, left)
                    dos = _stack_pair(_span_rows(do_ref, s * halo, r, blk, d), left)
                    kcat = _key_rows(kp, kc, kn, s, r, subs, halo, blk, d).astype(BF16)
                    vcat = _key_rows(vp, vc, vn, s, r, subs, halo, blk, d).astype(BF16)
                    l2, d2 = _span_rows(l_ref, s * halo, r, blk, d), _span_rows(d_ref, s * halo, r, blk, d)
                    lcols.append(jnp.concatenate([jnp.max(jnp.where(left, l2, NEG_INF), axis=-1, keepdims=True),
                                                  jnp.max(jnp.where(left, NEG_INF, l2), axis=-1, keepdims=True)], axis=0))
                    dcols.append(jnp.concatenate([jnp.sum(jnp.where(lane == 0, d2, 0.0), axis=-1, keepdims=True),
                                                  jnp.sum(jnp.where(lane == HEAD_DIM, d2, 0.0), axis=-1, keepdims=True)],
                                                 axis=0))
                    scores.append(_dot_nt(qs, kcat) + bias2 + _item_penalty(t, nct, s, subs, blk))
                    dps.append(_dot_nt(dos, vcat))
                    qss.append(qs)
                    doss.append(dos)
                    kcats.append(kcat)
                ps = [jnp.exp(s - lc) for s, lc in zip(scores, lcols)]
                dss = [p * (dp - dc) for p, dp, dc in zip(ps, dps, dcols)]
                db_ref[...] += functools.reduce(lambda a, b: a + b, dss)
                if has_sink:
                    sk = s_ref[...]
                    ds_ref[...] -= functools.reduce(lambda a, b: a + b, [dc * jnp.exp(sk - lc) for dc, lc in zip(dcols, lcols)])
                dsbs = [ds.astype(BF16) for ds in dss]
                for j in range(items):
                    s, r = _item_of(j, r0, per_group)
                    dq2 = jnp.dot(dsbs[j], kcats[j], preferred_element_type=F32) * scale
                    _set_span_rows(dq_ref, s * halo, r, blk, d, jnp.where(left, dq2[:blk], dq2[blk:]))
                news = [(_dot_tn(dsbs[j], qss[j]), _dot_tn(ps[j].astype(BF16), doss[j])) for j in range(items)]
                for which, w in enumerate((wk, wv)):
                    for jr in range(per_group):
                        for sp in range(-1, subs + 1):
                            parts = [news[s * per_group + jr][which][(sp - s + 1) * blk:(sp - s + 2) * blk]
                                     for s in range(subs) if 0 <= sp - s + 1 < 3]
                            _set_span_rows(w, chunk + sp * halo, r0 + jr, blk, d,
                                           functools.reduce(lambda x, y: x + y, parts), add=True)

            _for_groups(group, groups, per_group)

        dk_ref[...] = wk[0:chunk]
        dv_ref[...] = wv[0:chunk]

    cur, prev, nxt = _attn_specs(seq, blk, d, lambda t: jnp.minimum(t, nct - 1))
    v_cur, v_prev, v_nxt = _attn_specs(seq, blk, d, lambda t: jnp.minimum(t, nct - 1), v_col)
    lag = pl.BlockSpec((chunk, LANES), lambda hp, t: (jnp.maximum(t - 1, 0), hp))
    band = pl.BlockSpec((2 * blk, 3 * blk), lambda hp, t: (hp, 0))
    col = pl.BlockSpec((2 * blk, 1), lambda hp, t: (hp, 0))
    in_specs = [cur, prev, cur, nxt, v_prev, v_cur, v_nxt, cur, cur, cur, band]
    args = [q, k, k, k, v, v, v, do, lse, delta, bias]
    out_shape = [jax.ShapeDtypeStruct((seq, WIDTH), F32)] * 3 + [jax.ShapeDtypeStruct((N_HEADS * blk, 3 * blk), F32)]
    out_specs = [cur, lag, lag, band]
    if has_sink:
        in_specs.append(col)
        args.append(sink)
        out_shape.append(jax.ShapeDtypeStruct((N_HEADS * blk, 1), F32))
        out_specs.append(col)
    window = pltpu.VMEM((2 * chunk + halo, LANES), F32)
    return pl.pallas_call(
        body, out_shape=out_shape, grid=(N_PAIRS, nct + 1), in_specs=in_specs, out_specs=out_specs,
        scratch_shapes=[window, window], name=name,
        compiler_params=_params("arbitrary", "arbitrary"))(*args)


def _combine_patterns(outs, lses):
    def combine(*tiles):
        os_, ls = tiles[:len(outs)], tiles[len(outs):]
        m = functools.reduce(jnp.maximum, ls)
        es = [jnp.exp(l - m) for l in ls]
        den = functools.reduce(lambda a, b: a + b, es)
        num = functools.reduce(lambda a, b: a + b, [e * o for e, o in zip(es, os_)])
        return num / den, m + jnp.log(den)

    return _ew(combine, [*outs, *lses], [F32, F32], "combine_a")


def _tile_gain(g, reps):
    return jnp.tile(g[None, :], (1, reps))


def _local_step(x, p, target, w_in_of, rest_of, small):
    rel_table = small["rel_table"]
    buckets_a = [_band_buckets(blk, d) for blk, d in DILATED]
    buckets_b = _band_buckets(BLK_B, 1)
    bias_a = [_bias_tiles(rel_table, bk, 0, "bias_a").reshape(N_HEADS * bk.shape[0], -1) for bk in buckets_a]
    bias_b = _bias_tiles(rel_table, buckets_b, N_HEADS, "bias_b").reshape(N_HEADS * BLK_B, -1)

    saved = []
    for l in range(DEPTH):
        g_mix, g_ffn, g_ple = (small[n][l][None, :] for n in ("norm_mix_g", "norm_ffn_g", "norm_ple_g"))
        gqa, gka, gqb = (_tile_gain(small[n][l], N_HEADS) for n in ("qnorm_a_g", "knorm_a_g", "qnorm_b_g"))
        gkb = _tile_gain(small["knorm_b_g"][l], N_KV_B)
        sink = jnp.repeat(small["sink_b"][l], BLK_B)[:, None]

        h = _rms_fwd(x, g_mix, "rms_mix") if l == 0 else h_next
        w_in = w_in_of(l, (h, bias_a, bias_b)) if l == 0 else w_in_next
        proj = _mm(h, w_in, "nt", F32, "mm_in")
        qa, ka, qb, kb, vb = _qknorm_fwd(proj, gqa, gka, gqb, gkb)
        outs, lses = [], []
        for (blk, d), bias in zip(DILATED, bias_a):
            o, ls = _attn_fwd(qa, ka, proj, bias, None, blk, d, f"attn_a{d}_fwd", v_col=OFF_VA)
            outs.append(o)
            lses.append(ls)
        ya, lse_a = _combine_patterns(outs, lses)
        yb, lse_b = _attn_fwd(qb, kb, vb, bias_b, sink, BLK_B, 1, "attn_b_fwd")
        w = dict(rest_of(l, yb), w_in=w_in)
        def gate(products, extra):
            (ca_, cb_), (ga_, gb_) = products, extra
            return _sigmoid(ga_) * ca_ + _sigmoid(gb_) * cb_, ca_, cb_

        merged, ca, cb = _mm_fused([(ya, w["w_branch_a"], "nn"), (yb, w["w_branch_b"], "nn")],
                                   [(proj, OFF_GA), (proj, OFF_GB)], gate, [BF16, BF16, BF16], "mm_branches_gate")
        x1, h2 = _mm_res_norm(merged, w["w_out"], "nn", x, g_ffn, "mm_out_norm")

        def swiglu(products, extra):
            a_, u_ = products
            return (a_ * _sigmoid(a_)) * u_, a_, u_

        hid, a, u = _mm_fused([(h2, w["w_ffn_gate"], "nt"), (h2, w["w_ffn_up"], "nt")], [], swiglu,
                              [BF16, BF16, BF16], "mm_ffn_gate_up")
        if l + 1 < DEPTH:
            w_in_next = w_in_of(l + 1, hid)
        x2, h3 = _mm_res_norm(hid, w["w_ffn_down"], "nn", x1, g_ple, "mm_ffn_down_norm")

        def ple(products, extra):
            z_, e_ = products
            return extra[0] + _sigmoid(z_) * e_, z_, e_

        next_gain = small["norm_mix_g"][l + 1][None, :] if l + 1 < DEPTH else None
        x3, z, e, *rest = _mm_fused([(h3, w["w_ple_gate"], "nn"), (p[l], w["w_ple_proj"], "nn")], [(x2, 0)], ple,
                                    [F32, BF16, BF16], "mm_ple", next_gain=next_gain)
        h_next = rest[0] if rest else None
        saved.append(dict(w=w, x0=x, h=h, proj=proj, qa=qa, ka=ka, qb=qb, kb=kb, vb=vb, ya=ya, lse_a=lse_a,
                          yb=yb, lse_b=lse_b, ca=ca, cb=cb, merged=merged, x1=x1, h2=h2, a=a, u=u, hid=hid,
                          x2=x2, h3=h3, z=z, e=e))
        x = x3

    dx, de, dz, loss_acc = _loss_grad(x, target, saved[-1]["z"], saved[-1]["e"])
    loss = loss_acc[0, 0]

    gbig = [{} for _ in range(DEPTH)]
    marks = [{} for _ in range(DEPTH)]
    gsmall = {n: [None] * DEPTH for n in SMALL if n != "rel_table"}
    dbias_a = [[] for _ in DILATED]
    dbias_b = []

    for l in reversed(range(DEPTH)):
        sv = saved[l]
        w = sv["w"]
        g_mix, g_ffn, g_ple = (small[n][l][None, :] for n in ("norm_mix_g", "norm_ffn_g", "norm_ple_g"))
        gqa, gka, gqb = (_tile_gain(small[n][l], N_HEADS) for n in ("qnorm_a_g", "knorm_a_g", "qnorm_b_g"))
        gkb = _tile_gain(small["knorm_b_g"][l], N_KV_B)
        sink = jnp.repeat(small["sink_b"][l], BLK_B)[:, None]

        if l < DEPTH - 1:
            de, dz = _ew(_ple_bwd, [dx, sv["z"], sv["e"]], [BF16, BF16], "ple_bwd")
        gbig[l]["w_ple_proj"] = _mm(p[l], de, "tn", BF16, "mm_d_ple_proj")
        gbig[l]["w_ple_gate"] = _mm(sv["h3"], dz, "tn", BF16, "mm_d_ple_gate")
        dx, dxb, gsmall["norm_ple_g"][l] = _mm_rms_bwd([(dz, w["w_ple_gate"], "nt")], sv["x2"], g_ple, dx,
                                                      "mm_dh3_rms_bwd")

        gbig[l]["w_ffn_down"] = _mm(sv["hid"], dxb, "tn", BF16, "mm_d_ffn_down")

        def swiglu_bwd(products, extra):
            dh_, a_, u_ = products[0], extra[0].astype(F32), extra[1].astype(F32)
            s = _sigmoid(a_)
            return dh_ * u_ * (s * (1.0 + a_ * (1.0 - s))), dh_ * (a_ * s)

        da, du = _mm_fused([(dxb, w["w_ffn_down"], "nt")], [(sv["a"], 0), (sv["u"], 0)], swiglu_bwd, [BF16, BF16],
                           "mm_dhid_swiglu_bwd")
        gbig[l]["w_ffn_gate"], gbig[l]["w_ffn_up"] = _mm_fused(
            [(da, sv["h2"], "tn"), (du, sv["h2"], "tn")], [], lambda products, extra: products, [BF16, BF16],
            "mm_d_ffn_gate_up")
        dx, dxb, gsmall["norm_ffn_g"][l] = _mm_rms_bwd([(da, w["w_ffn_gate"], "nn"), (du, w["w_ffn_up"], "nn")],
                                                      sv["x1"], g_ffn, dx, "mm_dh2_rms_bwd")

        gbig[l]["w_out"] = _mm(sv["merged"], dxb, "tn", BF16, "mm_d_out")

        def gate_bwd(products, extra):
            dm_, ca_, cb_ = products[0], extra[0].astype(F32), extra[1].astype(F32)
            sa, sb = _sigmoid(extra[2]), _sigmoid(extra[3])
            return dm_ * sa, dm_ * sb, dm_ * ca_ * (sa * (1.0 - sa)), dm_ * cb_ * (sb * (1.0 - sb))

        dca, dcb, dga, dgb = _mm_fused(
            [(dxb, w["w_out"], "nt")], [(sv["ca"], 0), (sv["cb"], 0), (sv["proj"], OFF_GA), (sv["proj"], OFF_GB)],
            gate_bwd, [BF16, BF16, BF16, BF16], "mm_dmerged_gate_bwd")
        gbig[l]["w_branch_a"], gbig[l]["w_branch_b"] = _mm_fused(
            [(sv["ya"], dca, "tn"), (sv["yb"], dcb, "tn")], [], lambda products, extra: products, [BF16, BF16],
            "mm_d_branches")

        def with_row_dots(products, extra):
            (dya_, dyb_), (ya_, yb_) = products, extra
            return dya_, _seg_sum(dya_ * ya_), dyb_, _seg_sum(dyb_ * yb_)

        dya, delta_a, dyb, delta_b = _mm_fused(
            [(dca, w["w_branch_a"], "nt"), (dcb, w["w_branch_b"], "nt")], [(sv["ya"], 0), (sv["yb"], 0)], with_row_dots,
            [F32, F32, F32, F32], "mm_dy_branches")

        dqa, dka, dva = [], [], []
        for (blk, d), bias, bk in zip(DILATED, bias_a, buckets_a):
            dq_, dk_, dv_, db_ = _attn_bwd(sv["qa"], sv["ka"], sv["proj"], dya, sv["lse_a"], delta_a, bias, None, blk, d,
                                           f"attn_a{d}_bwd", v_col=OFF_VA)
            dqa.append(dq_)
            dka.append(dk_)
            dva.append(dv_)
            dbias_a[len(dqa) - 1].append(db_)
        dqb, dkb, dvb, db_, dsink = _attn_bwd(sv["qb"], sv["kb"], sv["vb"], dyb, sv["lse_b"], delta_b, bias_b, sink,
                                              BLK_B, 1, "attn_b_bwd")
        dbias_b.append(db_)
        gsmall["sink_b"][l] = dsink.reshape(N_HEADS, BLK_B).sum(axis=1)

        dproj, pqa, pka, pqb, pkb = _qknorm_bwd(sv["proj"], gqa, gka, gqb, gkb, dqa, dka, dva, dqb, dkb, dvb, dga, dgb)
        marks[l]["attn_bwd_done"] = dproj
        gsmall["qnorm_a_g"][l] = pqa.reshape(N_HEADS, HEAD_DIM).sum(0)
        gsmall["knorm_a_g"][l] = pka.reshape(N_HEADS, HEAD_DIM).sum(0)
        gsmall["qnorm_b_g"][l] = pqb.reshape(N_HEADS, HEAD_DIM).sum(0)
        gsmall["knorm_b_g"][l] = pkb.reshape(N_KV_B, HEAD_DIM).sum(0)
        gbig[l]["w_in"] = _mm(dproj, sv["h"], "tn", BF16, "mm_d_in")
        dx, _, gsmall["norm_mix_g"][l] = _mm_rms_bwd([(dproj, w["w_in"], "nn")], sv["x0"], g_mix, dx, "mm_dh_rms_bwd")
        gsmall["norm_mix_g"][l] = gsmall["norm_mix_g"][l][0]
        gsmall["norm_ffn_g"][l] = gsmall["norm_ffn_g"][l][0]
        gsmall["norm_ple_g"][l] = gsmall["norm_ple_g"][l][0]

    gsmall = {n: jnp.stack(v) for n, v in gsmall.items()}
    dtable_a = sum(_table_grad(sum(dbs).reshape(N_HEADS, blk, 3 * blk), bk, "table_grad_a")
                   for dbs, (blk, _), bk in zip(dbias_a, DILATED, buckets_a))
    dtable_b = _table_grad(sum(dbias_b).reshape(N_HEADS, BLK_B, 3 * BLK_B), buckets_b, "table_grad_b")
    gsmall["rel_table"] = jnp.concatenate([dtable_a, dtable_b], axis=0).T
    return loss, dx, gbig, gsmall, marks


def _place():
    return lax.axis_index("x"), lax.axis_index("y"), lax.axis_index("c")


def _flip(v, bit):
    return 1 - v if bit else v


CHIP_RELATIONS = ((0, 1), (1, 0), (1, 1))
ANY = pl.BlockSpec(memory_space=pl.ANY)


def _allgather_body(w_refs, out_refs, send_sems, recv_sems):
    x, y, c = _place()
    chips = [(_flip(x, a), _flip(y, b)) for a, b in CHIP_RELATIONS]

    def make(g):
        w_ref, out_ref = w_refs[g], out_refs[g]
        half = w_ref.shape[0] // 2

        def part(px, py, pc):
            return out_ref.at[2 * px + py, pl.ds(pc * half, half), :]

        def copy(k, block, to, src=None):
            return pltpu.make_async_remote_copy(
                src_ref=part(*block) if src is None else src, dst_ref=part(*block),
                send_sem=send_sems.at[7 * g + k], recv_sem=recv_sems.at[7 * g + k], device_id=to,
                device_id_type=MESH_ID)

        own = pltpu.make_async_remote_copy(
            src_ref=w_ref, dst_ref=out_ref.at[2 * x + y], send_sem=send_sems.at[7 * g + 6],
            recv_sem=recv_sems.at[7 * g + 6], device_id=(x, y, 1 - c), device_id_type=MESH_ID)
        first = [copy(k, (x, y, c), (*chip, c), src=w_ref.at[pl.ds(c * half, half), :]) for k, chip in enumerate(chips)]
        passed = [copy(3 + k, (*chip, c), (x, y, 1 - c)) for k, chip in enumerate(chips)]
        arrive = [copy(k, (*chip, c), (x, y, c)) for k, chip in enumerate(chips)]
        arrive2 = [copy(3 + k, (*chip, 1 - c), (x, y, c)) for k, chip in enumerate(chips)]
        return own, first, passed, arrive, arrive2

    made = [make(g) for g in range(len(w_refs))]
    for own, first, _, _, _ in made:
        own.start()
        for cp in first:
            cp.start()
    for _, _, passed, arrive, _ in made:
        for k in range(3):
            arrive[k].wait_recv()
            passed[k].start()
    for own, first, passed, _, arrive2 in made:
        for k in range(3):
            arrive2[k].wait_recv()
        own.wait_recv()
        for cp in first + passed + [own]:
            cp.wait_send()


def _sibling(x, y, c):
    return [(x, y, 1 - c)]


def _same_core_of_other_chips(x, y, c):
    return [(_flip(x, a), _flip(y, b), c) for a, b in CHIP_RELATIONS]


def _exchange(body, ins, out_types, n_sems, name, sequencer=None):
    n = len(ins)
    sems = (pltpu.SemaphoreType.DMA((n_sems,)), pltpu.SemaphoreType.DMA((n_sems,)))
    if sequencer is None:
        in_place = out_types is None
        out_shape = [jax.ShapeDtypeStruct(a.shape, a.dtype) for a in ins] if in_place else out_types

        def tc_body(*refs):
            body(refs[:n], refs[n:n + len(out_shape)], refs[-2], refs[-1])

        return list(pl.pallas_call(
            tc_body, out_shape=out_shape, in_specs=[ANY] * n, out_specs=[ANY] * len(out_shape),
            input_output_aliases={g: g for g in range(n)} if in_place else {}, scratch_shapes=list(sems), name=name)(*ins))

    collective_id, peers = sequencer
    hbm = pltpu.MemorySpace.HBM
    in_refs = [jax.new_ref(a, memory_space=hbm) for a in ins]
    out_refs = in_refs if out_types is None else [jax.empty_ref(t, memory_space=hbm) for t in out_types]

    @pl.kernel(mesh=plsc.ScalarSubcoreMesh(axis_name="sequencer", num_cores=1), name=name, scratch_types=sems,
               compiler_params=pltpu.CompilerParams(collective_id=collective_id))
    def launch(send_sems, recv_sems):
        barrier = pltpu.get_barrier_semaphore()
        devices = peers(*_place())
        for device in devices:
            pl.semaphore_signal(barrier, inc=1, device_id=device, device_id_type=MESH_ID)
        pl.semaphore_wait(barrier, len(devices))
        body(in_refs, out_refs, send_sems, recv_sems)

    launch()
    return [r[...] for r in out_refs]


def _allgather(shards, name, sequencer=None):
    out_types = [jax.ShapeDtypeStruct((N_CHIPS,) + s.shape, s.dtype) for s in shards]
    if sequencer is not None:
        sequencer = (sequencer, lambda x, y, c: _sibling(x, y, c) + _same_core_of_other_chips(x, y, c))
    return _exchange(_allgather_body, shards, out_types, 7 * len(shards), name, sequencer)


def _half_tile(half):
    return max(t for t in range(16, 1025, 16) if half % t == 0)


def _run_copies(cps):
    for cp in cps:
        cp.start()
    for cp in cps:
        cp.wait_recv()
    for cp in cps:
        cp.wait_send()


def _sibling_halves(gsends, name, sequencer=None):
    def body(g_refs, out_refs, send_sems, recv_sems):
        x, y, c = _place()
        cps = []
        for g, (g_ref, out_ref) in enumerate(zip(g_refs, out_refs)):
            half = g_ref.shape[1] // 2
            cps.append(pltpu.make_async_remote_copy(
                src_ref=g_ref.at[:, pl.ds((1 - c) * half, half), :], dst_ref=out_ref,
                send_sem=send_sems.at[g], recv_sem=recv_sems.at[g], device_id=(x, y, 1 - c), device_id_type=MESH_ID))
        _run_copies(cps)

    out_types = [jax.ShapeDtypeStruct((s.shape[0], s.shape[1] // 2, s.shape[2]), s.dtype) for s in gsends]
    return _exchange(body, gsends, out_types, len(gsends), name, sequencer and (sequencer, _sibling))


def _chip_sums(gsend, sib, place):
    n, rows, cols = gsend.shape
    half = rows // 2
    tm = _half_tile(half)
    nblk = half // tm

    def body(s_ref, g_ref, sib_ref, o_ref):
        o_ref[0] = (g_ref[0].astype(F32) + sib_ref[0].astype(F32)).astype(o_ref.dtype)

    grid_spec = pltpu.PrefetchScalarGridSpec(
        num_scalar_prefetch=1, grid=(n, nblk),
        in_specs=[pl.BlockSpec((1, tm, cols), lambda k, i, s: (jnp.bitwise_xor(s[0], k), s[1] * nblk + i, 0)),
                  pl.BlockSpec((1, tm, cols), lambda k, i, s: (jnp.bitwise_xor(s[0], k), i, 0))],
        out_specs=pl.BlockSpec((1, tm, cols), lambda k, i, s: (k, i, 0)))
    return pl.pallas_call(
        body, out_shape=jax.ShapeDtypeStruct((n, half, cols), BF16), grid_spec=grid_spec,
        name="rs_chip_sums", compiler_params=_params("parallel", "parallel"))(place, gsend, sib)


def _exchange_chip_sums(tsends, name, sequencer=None):
    def body(t_refs, out_refs, send_sems, recv_sems):
        x, y, c = _place()
        cps = []
        for g, (t_ref, out_ref) in enumerate(zip(t_refs, out_refs)):
            for k, device in enumerate(_same_core_of_other_chips(x, y, c)):
                cps.append(pltpu.make_async_remote_copy(
                    src_ref=t_ref.at[k + 1], dst_ref=out_ref.at[k], send_sem=send_sems.at[3 * g + k],
                    recv_sem=recv_sems.at[3 * g + k], device_id=device, device_id_type=MESH_ID))
        _run_copies(cps)

    out_types = [jax.ShapeDtypeStruct((3,) + s.shape[1:], s.dtype) for s in tsends]
    return _exchange(body, tsends, out_types, 3 * len(tsends), name,
                     sequencer and (sequencer, _same_core_of_other_chips))


def _final_sum(tsend, recv, place):
    n, half, cols = tsend.shape
    tm = _half_tile(half)
    nblk = half // tm

    def body(s_ref, t_ref, r_ref, o_ref):
        o_ref[...] = ((t_ref[0].astype(F32) + r_ref[0].astype(F32)) + r_ref[1].astype(F32)) + r_ref[2].astype(F32)

    grid_spec = pltpu.PrefetchScalarGridSpec(
        num_scalar_prefetch=1, grid=(nblk,),
        in_specs=[pl.BlockSpec((1, tm, cols), lambda i, s: (0, i, 0)), pl.BlockSpec((n - 1, tm, cols), lambda i, s: (0, i, 0))],
        out_specs=pl.BlockSpec((tm, cols), lambda i, s: (s[1] * nblk + i, 0)))
    return pl.pallas_call(
        body, out_shape=jax.ShapeDtypeStruct((2 * half, cols), F32), grid_spec=grid_spec, name="rs_final_sum",
        compiler_params=_params("parallel"))(place, tsend, recv)


def _join_halves(gfulls, name, sequencer=None):
    def body(g_refs, out_refs, send_sems, recv_sems):
        x, y, c = _place()
        n = len(g_refs)

        def copy(g, pc):
            half = g_refs[g].shape[0] // 2
            return pltpu.make_async_remote_copy(
                src_ref=g_refs[g].at[pl.ds(pc * half, half), :], dst_ref=out_refs[g].at[pl.ds(pc * half, half), :],
                send_sem=send_sems.at[g], recv_sem=recv_sems.at[g], device_id=(x, y, 1 - c), device_id_type=MESH_ID)

        mine = [copy(g, c) for g in range(n)]
        for cp in mine:
            cp.start()
        for g in range(n):
            copy(g, 1 - c).wait_recv()
        for cp in mine:
            cp.wait_send()

    return _exchange(body, gfulls, None, len(gfulls), name, sequencer and (sequencer, _sibling))


def _allreduce_small(v):
    rows, cols = v.shape

    def body(v_ref, out_ref, buf, send_sems, recv_sems):
        x, y, c = _place()
        cps = []
        for k in range(1, 8):
            peer = (_flip(x, (k >> 2) & 1), _flip(y, (k >> 1) & 1), _flip(c, k & 1))
            cps.append(pltpu.make_async_remote_copy(
                src_ref=v_ref, dst_ref=buf.at[k - 1], send_sem=send_sems.at[k - 1], recv_sem=recv_sems.at[k - 1],
                device_id=peer, device_id_type=MESH_ID))
        for cp in cps:
            cp.start()
        for cp in cps:
            cp.wait_recv()
        for cp in cps:
            cp.wait_send()
        t0 = v_ref[...] + buf[0]
        t1 = buf[1] + buf[2]
        t2 = buf[3] + buf[4]
        t3 = buf[5] + buf[6]
        out_ref[...] = (t0 + t1) + (t2 + t3)

    vm = pl.BlockSpec(memory_space=pltpu.VMEM)
    return pl.pallas_call(
        body, out_shape=jax.ShapeDtypeStruct((rows, cols), F32), in_specs=[vm], out_specs=vm,
        scratch_shapes=[pltpu.VMEM((7, rows, cols), F32), pltpu.SemaphoreType.DMA((7,)), pltpu.SemaphoreType.DMA((7,))],
        name="allreduce_small")(v)


BIG_INFO = {n: (shape, ax) for n, shape, ax in BIG}
GROUPS = (("w_in",), ("w_ffn_gate", "w_ffn_up", "w_ffn_down", "w_out", "w_ple_gate"),
          ("w_branch_a", "w_branch_b", "w_ple_proj"))
GATHER_GROUPS = tuple((n,) for n in GROUPS[0] + GROUPS[1]) + GROUPS[2:]


def _shard_shape(name):
    (k, m), ax = BIG_INFO[name]
    return (k // N_CHIPS, m) if ax == 0 else (k, m // N_CHIPS)


def _group_rows(group):
    offs, off = {}, 0
    for n in group:
        offs[n] = off
        off += _shard_shape(n)[0]
    return offs, off


def _pack_groups(shards, layer, dtype):
    return [jnp.concatenate([shards[n][layer].astype(dtype) for n in group], axis=0) for group in GATHER_GROUPS]


def _unpack_full(gathered, groups):
    out = {}
    for group, arr in zip(groups, gathered):
        offs, _ = _group_rows(group)
        for n in group:
            rows, cols = _shard_shape(n)
            (k, m), ax = BIG_INFO[n]
            slab = arr[:, offs[n]:offs[n] + rows]
            out[n] = slab.reshape(k, m) if ax == 0 else jnp.transpose(slab, (1, 0, 2)).reshape(k, m)
    return out


def _pack_grads(gfull):
    out = []
    for group in GROUPS:
        parts = []
        for n in group:
            rows, cols = _shard_shape(n)
            ax = BIG_INFO[n][1]
            slab = (gfull[n].reshape(N_CHIPS, rows, cols) if ax == 0
                    else jnp.transpose(gfull[n].reshape(rows, N_CHIPS, cols), (1, 0, 2)))
            parts.append(slab)
        out.append(jnp.concatenate(parts, axis=1))
    return out


def _after(values, mark):
    values, _ = lax.optimization_barrier((values, mark))
    return values


def _reduce_scatter_begin(gsends, place, tag, ids):
    sibs = _sibling_halves(gsends, "rs_sibling_halves_" + tag, ids[0])
    tsends = [_chip_sums(g, s, place) for g, s in zip(gsends, sibs)]
    return tsends, _exchange_chip_sums(tsends, "rs_exchange_" + tag, ids[1])


def _reduce_scatter_finish(begun, place, tag, ids, hold):
    tsends, recvs = begun
    recvs = _after(recvs, hold)
    return _join_halves([_final_sum(t, r, place) for t, r in zip(tsends, recvs)], "rs_join_halves_" + tag, ids[2])


SMALL_SHAPES = {"rel_table": (NUM_BUCKETS, 2 * N_HEADS), "norm_mix_g": (DEPTH, D_MODEL), "qnorm_a_g": (DEPTH, HEAD_DIM),
                "knorm_a_g": (DEPTH, HEAD_DIM), "qnorm_b_g": (DEPTH, HEAD_DIM), "knorm_b_g": (DEPTH, HEAD_DIM),
                "sink_b": (DEPTH, N_HEADS), "norm_ffn_g": (DEPTH, D_MODEL), "norm_ple_g": (DEPTH, D_MODEL)}


def _pack_small(vals, last=None):
    flat = jnp.concatenate([vals[n].astype(F32).reshape(-1) for n in SMALL])
    tail = jnp.zeros((SMALL_ROWS * LANES - flat.shape[0],), F32)
    if last is not None:
        tail = tail.at[-1].set(last)
    return jnp.concatenate([flat, tail]).reshape(SMALL_ROWS, LANES)


def _unpack_small(packed):
    flat, out, off = packed.reshape(-1), {}, 0
    for n in SMALL:
        size = math.prod(SMALL_SHAPES[n])
        out[n] = flat[off:off + size].reshape(SMALL_SHAPES[n])
        off += size
    return out


def _adamw(w, gs, g_row, m, v, name):
    c1 = 1.0 - ADAM_B1 ** ADAM_STEP
    c2 = 1.0 - ADAM_B2 ** ADAM_STEP
    total, width = w.shape
    n_layers = len(gs)
    per = total // n_layers
    tm = max(t for t in range(8, 513, 8) if per % t == 0 and g_row % t == 0)
    nblk = per // tm

    def body(*refs):
        w_ref, g_refs = refs[0], refs[1:1 + n_layers]
        m_ref, v_ref, og, od, om, ov = refs[1 + n_layers:]
        layer = pl.program_id(0) // nblk
        g = g_refs[0][...]
        for l in range(1, n_layers):
            g = jnp.where(layer == l, g_refs[l][...], g)
        m_new = ADAM_B1 * m_ref[...] + (1.0 - ADAM_B1) * g
        v_new = ADAM_B2 * v_ref[...] + (1.0 - ADAM_B2) * (g * g)
        og[...] = g
        od[...] = -ADAM_LR * ((m_new / c1) / (jnp.sqrt(v_new / c2) + ADAM_EPS) + ADAM_WD * w_ref[...])
        om[...] = m_new
        ov[...] = v_new

    row = pl.BlockSpec((tm, width), lambda i: (i, 0))
    g_specs = [pl.BlockSpec((tm, width), lambda i, l=l: (g_row // tm + jnp.clip(i - l * nblk, 0, nblk - 1), 0))
               for l in range(n_layers)]
    return pl.pallas_call(
        body, out_shape=[jax.ShapeDtypeStruct((total, width), F32)] * 4, grid=(total // tm,),
        in_specs=[row] + g_specs + [row, row], out_specs=[row] * 4, name=name,
        compiler_params=_params("parallel"))(w, *gs, m, v)


def kernel(x, p, rel_table, norm_mix_g, w_in, qnorm_a_g, knorm_a_g, qnorm_b_g, knorm_b_g, sink_b, w_branch_a, w_branch_b, w_out, norm_ffn_g, w_ffn_gate, w_ffn_up, w_ffn_down, norm_ple_g, w_ple_gate, w_ple_proj, loss_target, m_rel_table, m_norm_mix_g, m_w_in, m_qnorm_a_g, m_knorm_a_g, m_qnorm_b_g, m_knorm_b_g, m_sink_b, m_w_branch_a, m_w_branch_b, m_w_out, m_norm_ffn_g, m_w_ffn_gate, m_w_ffn_up, m_w_ffn_down, m_norm_ple_g, m_w_ple_gate, m_w_ple_proj, v_rel_table, v_norm_mix_g, v_w_in, v_qnorm_a_g, v_knorm_a_g, v_qnorm_b_g, v_knorm_b_g, v_sink_b, v_w_branch_a, v_w_branch_b, v_w_out, v_norm_ffn_g, v_w_ffn_gate, v_w_ffn_up, v_w_ffn_down, v_norm_ple_g, v_w_ple_gate, v_w_ple_proj):
    given = dict(locals())

    def held(name, a):
        return jnp.swapaxes(a, 1, 2) if name in TRANSPOSED else a

    weights = {n: held(n, given[n]) for n in WEIGHTS}
    moments_m = {n: held(n, given["m_" + n]) for n in WEIGHTS}
    moments_v = {n: held(n, given["v_" + n]) for n in WEIGHTS}
    xi, yi, ci = _place()
    place = jnp.stack([2 * xi + yi, ci]).astype(jnp.int32)

    shards = [_pack_groups(weights, l, BF16) for l in range(DEPTH)]
    w_in_flight = [_allgather(shards[0][:1], "allgather_w_in_layer0", sequencer=9), None]
    rest_flight = [None, None]
    small = {n: weights[n] for n in SMALL}

    def w_in_of(l, mark):
        landed = _after(w_in_flight[l], (shards[1], mark) if l == 0 else mark)
        rest_flight[l] = _allgather(_after(shards[l][1:], landed), f"allgather_rest_layer{l}", sequencer=(1, 2)[l])
        return _unpack_full(landed, GATHER_GROUPS[:1])["w_in"]

    def rest_of(l, mark):
        if l + 1 < DEPTH:
            w_in_flight[l + 1] = _allgather(_after(shards[l + 1][:1], mark), f"allgather_w_in_layer{l + 1}", sequencer=16)
        return _unpack_full(_after(rest_flight[l], mark), GATHER_GROUPS[1:])

    loss, dx, gbig, gsmall, marks = _local_step(x[0], p[:, 0], loss_target[0], w_in_of, rest_of, small)

    gsends = [_pack_grads(gbig[l]) for l in range(DEPTH)]
    stages = {"rest_layer1": (gsends[1][1:], (3, 4, 5)), "w_in_layer1": (gsends[1][:1], (13, 14, 15)),
              "rest_layer0": (gsends[0][1:], (6, 7, 8)), "w_in_layer0": (gsends[0][:1], (10, 11, 12))}
    begun = {tag: _reduce_scatter_begin(g, place, tag, ids) for tag, (g, ids) in stages.items()}

    def finish(tag, hold):
        return _reduce_scatter_finish(begun[tag], place, tag, stages[tag][1], hold)

    red1 = finish("w_in_layer1", marks[0]["attn_bwd_done"]) + finish("rest_layer1", marks[0]["attn_bwd_done"])
    rest0 = finish("rest_layer0", gbig[0]["w_in"])

    grads, delta, new_m, new_v = {}, {}, {}, {}

    def update(group, reduced):
        offs, _ = _group_rows(group)
        for n in group:
            shape = weights[n].shape
            two_d = lambda a: a.reshape(shape[0] * shape[1], shape[2])
            outs = _adamw(two_d(weights[n]), reduced, offs[n], two_d(moments_m[n]), two_d(moments_v[n]), "adamw_" + n)
            grads[n], delta[n], new_m[n], new_v[n] = (held(n, o.reshape(shape)) for o in outs)

    for gi in (1, 2):
        update(GROUPS[gi], _after([rest0[gi - 1], red1[gi]], begun["w_in_layer0"][0]))
    small_grads = _allreduce_small(_pack_small(gsmall, last=loss))
    g_, d_, m_, v_ = _adamw(_pack_small(weights), [small_grads], 0, _pack_small(moments_m), _pack_small(moments_v),
                            "adamw_small")
    grads.update(_unpack_small(g_))
    delta.update(_unpack_small(d_))
    new_m.update(_unpack_small(m_))
    new_v.update(_unpack_small(v_))
    others_done = [dx, d_] + [delta[n] for gi in (1, 2) for n in GROUPS[gi]]
    update(GROUPS[0], [finish("w_in_layer0", others_done)[0], red1[0]])

    return (small_grads[-1, -1], dx[None], *[grads[n] for n in WEIGHTS], *[delta[n] for n in WEIGHTS],
            *[new_m[n] for n in WEIGHTS], *[new_v[n] for n in WEIGHTS])
```

```python
import functools
import math

import jax
import jax.numpy as jnp
from jax import lax
from jax.experimental import pallas as pl
from jax.experimental.pallas import tpu as pltpu
from jax.experimental.pallas import tpu_sc as plsc

F32 = jnp.float32
BF16 = jnp.bfloat16
MESH_ID = pl.DeviceIdType.MESH

D_MODEL = 1024
DEPTH = 2
HEAD_DIM = 64
N_HEADS = 8
WIDTH = N_HEADS * HEAD_DIM
N_PAIRS = 4
ITEMS = 16
MAX_CHUNK = 1024
N_KV_B = 2
PLE_DIM = 256
D_FF = 2816
D_IN = 4352
OFF_QA, OFF_KA, OFF_VA, OFF_QB, OFF_KB, OFF_VB, OFF_GA, OFF_GB = 0, 512, 1024, 1536, 2048, 2176, 2304, 3328
DILATED = ((64, 1), (64, 4), (64, 16))
BLK_B = 128
NUM_BUCKETS = 32
MAX_DISTANCE = 1024
RMS_EPS = 1e-6
NEG_INF = -1e30
LANES = 128
ROW_TILE = 256
VMEM_LIMIT = 48 * 1024 * 1024

ADAM_LR, ADAM_B1, ADAM_B2, ADAM_EPS, ADAM_WD, ADAM_STEP = 0.001, 0.9, 0.999, 1e-08, 0.01, 10

TRANSPOSED = ("w_in", "w_ffn_gate", "w_ffn_up")
BIG = (
    ("w_in", (D_IN, D_MODEL), 0),
    ("w_branch_a", (WIDTH, D_MODEL), 1),
    ("w_branch_b", (WIDTH, D_MODEL), 1),
    ("w_out", (D_MODEL, D_MODEL), 0),
    ("w_ffn_gate", (D_FF, D_MODEL), 0),
    ("w_ffn_up", (D_FF, D_MODEL), 0),
    ("w_ffn_down", (D_FF, D_MODEL), 0),
    ("w_ple_gate", (D_MODEL, D_MODEL), 0),
    ("w_ple_proj", (PLE_DIM, D_MODEL), 1),
)
SMALL = ("rel_table", "norm_mix_g", "qnorm_a_g", "knorm_a_g", "qnorm_b_g", "knorm_b_g", "sink_b",
         "norm_ffn_g", "norm_ple_g")
WEIGHTS = ("rel_table", "norm_mix_g", "w_in", "qnorm_a_g", "knorm_a_g", "qnorm_b_g", "knorm_b_g", "sink_b",
           "w_branch_a", "w_branch_b", "w_out", "norm_ffn_g", "w_ffn_gate", "w_ffn_up", "w_ffn_down",
           "norm_ple_g", "w_ple_gate", "w_ple_proj")
N_CHIPS = 4
SMALL_ROWS = 64


def _params(*sem):
    return pltpu.CompilerParams(dimension_semantics=sem, vmem_limit_bytes=VMEM_LIMIT)


MM_VMEM_BUDGET = 40 * 1024 * 1024
STEP_OVERHEAD_S = 0.4e-6
TILE_DMA_BYTES_PER_S = 1.5e12


def _mm_dims(a, b, mode):
    if mode == "nn":
        return a.shape[0], b.shape[1], a.shape[1]
    if mode == "nt":
        return a.shape[0], b.shape[0], a.shape[1]
    return a.shape[1], b.shape[1], a.shape[0]


def _mm_tiles(m, n, pairs, tile_bytes, col_offsets, full_rows=False):
    best = None
    widths = [n] if full_rows else [t for t in range(LANES, n + 1, LANES) if n % t == 0 and all(o % t == 0 for o in col_offsets)]
    for tm in (t for t in range(LANES, m + 1, LANES) if m % t == 0):
        for tn in widths:
            io = sum(tm * k * ab + tn * k * bb for k, ab, bb in pairs) + tm * tn * sum(tile_bytes)
            casts = sum((tm * k * 2 if ab == 4 else 0) + (tn * k * 2 if bb == 4 else 0) for k, ab, bb in pairs)
            if 2 * io + len(pairs) * tm * tn * 4 + casts > MM_VMEM_BUDGET:
                continue
            cost = (m // tm) * (n // tn) * STEP_OVERHEAD_S + io / TILE_DMA_BYTES_PER_S
            if best is None or (cost, -tm) < best[0]:
                best = ((cost, -tm), tm, tn)
    return best[1], best[2]


def _mm_fused(pairs, extras, epilogue, out_dtypes, name, next_gain=None):
    m, n, _ = _mm_dims(*pairs[0])
    assert all(_mm_dims(*p)[:2] == (m, n) for p in pairs)
    with_norm = next_gain is not None
    out_dtypes = list(out_dtypes) + ([BF16] if with_norm else [])
    tm, tn = _mm_tiles(
        m, n, [(_mm_dims(a, b, mode)[2], a.dtype.itemsize, b.dtype.itemsize) for a, b, mode in pairs],
        [e.dtype.itemsize for e, _ in extras] + [jnp.dtype(d).itemsize for d in out_dtypes], [off for _, off in extras],
        full_rows=with_norm)
    dims = {"nn": (((1,), (0,)), ((), ())), "nt": (((1,), (1,)), ((), ())), "tn": (((0,), (0,)), ((), ()))}
    in_specs, args = [], []
    for a, b, mode in pairs:
        k = _mm_dims(a, b, mode)[2]
        in_specs.append(pl.BlockSpec((k, tm), lambda i, j: (0, i)) if mode == "tn" else pl.BlockSpec((tm, k), lambda i, j: (i, 0)))
        in_specs.append(pl.BlockSpec((tn, k), lambda i, j: (j, 0)) if mode == "nt" else pl.BlockSpec((k, tn), lambda i, j: (0, j)))
        args += [a, b]
    for e, off in extras:
        in_specs.append(pl.BlockSpec((tm, tn), lambda i, j, o=off // tn: (i, o + j)))
        args.append(e)
    n_pairs, n_tiles = len(pairs), 2 * len(pairs) + len(extras)
    if with_norm:
        in_specs.append(pl.BlockSpec((1, n), lambda i, j: (0, 0)))
        args.append(next_gain)
    n_in = len(args)

    def body(*refs):
        products = [lax.dot_general(refs[2 * p][...].astype(BF16), refs[2 * p + 1][...].astype(BF16), dims[pairs[p][2]],
                                    preferred_element_type=F32) for p in range(n_pairs)]
        outs = list(epilogue(products, [r[...] for r in refs[2 * n_pairs:n_tiles]]))
        if with_norm:
            y = outs[0]
            outs.append((y * lax.rsqrt(jnp.mean(y * y, axis=-1, keepdims=True) + RMS_EPS)) * refs[n_tiles][...])
        for r, o in zip(refs[n_in:], outs):
            r[...] = o.astype(r.dtype)

    tile = pl.BlockSpec((tm, tn), lambda i, j: (i, j))
    return pl.pallas_call(
        body, out_shape=[jax.ShapeDtypeStruct((m, n), d) for d in out_dtypes], grid=(m // tm, n // tn),
        in_specs=in_specs, out_specs=[tile] * len(out_dtypes), name=name,
        compiler_params=_params("parallel", "parallel"))(*args)


def _mm(a, b, mode, out_dtype, name, res=None):
    if res is None:
        return _mm_fused([(a, b, mode)], [], lambda products, extra: products, [out_dtype], name)[0]
    return _mm_fused([(a, b, mode)], [(res, 0)], lambda products, extra: [products[0] + extra[0]], [out_dtype], name)[0]


def _mm_res_norm(a, b, mode, res, gain, name):
    return _mm_fused([(a, b, mode)], [(res, 0)], lambda products, extra: [products[0] + extra[0]], [F32], name,
                     next_gain=gain)


def _ew(fn, ins, out_dtypes, name):
    rows, width = ins[0].shape
    n_in = len(ins)

    def body(*refs):
        outs = fn(*[r[...] for r in refs[:n_in]])
        for r, o in zip(refs[n_in:], outs):
            r[...] = o.astype(r.dtype)

    row = pl.BlockSpec((ROW_TILE, width), lambda i: (i, 0))
    return pl.pallas_call(
        body, out_shape=[jax.ShapeDtypeStruct((rows, width), dt) for dt in out_dtypes], grid=(rows // ROW_TILE,),
        in_specs=[row] * n_in, out_specs=[row] * len(out_dtypes), name=name, compiler_params=_params("parallel"))(*ins)


def _sigmoid(x):
    return 1.0 / (1.0 + jnp.exp(-x))


def _seg_sum(v):
    outs = []
    for k in range(v.shape[1] // LANES):
        vp = v[:, k * LANES:(k + 1) * LANES]
        left = lax.broadcasted_iota(jnp.int32, vp.shape, 1) < HEAD_DIM
        sl = jnp.sum(jnp.where(left, vp, 0.0), axis=-1, keepdims=True)
        sr = jnp.sum(jnp.where(left, 0.0, vp), axis=-1, keepdims=True)
        outs.append(jnp.where(left, sl, sr))
    return outs[0] if len(outs) == 1 else jnp.concatenate(outs, axis=1)


def _seg_rstd(x):
    return lax.rsqrt(_seg_sum(x * x) * (1.0 / HEAD_DIM) + RMS_EPS)


def _rms_fwd(x, g, name):
    rows, d = x.shape
    tm = ROW_TILE

    def body(x_ref, g_ref, h_ref):
        xv = x_ref[...]
        r = lax.rsqrt(jnp.mean(xv * xv, axis=-1, keepdims=True) + RMS_EPS)
        h_ref[...] = ((xv * r) * g_ref[...]).astype(BF16)

    return pl.pallas_call(
        body, out_shape=jax.ShapeDtypeStruct((rows, d), BF16), grid=(rows // tm,),
        in_specs=[pl.BlockSpec((tm, d), lambda i: (i, 0)), pl.BlockSpec((1, d), lambda i: (0, 0))],
        out_specs=pl.BlockSpec((tm, d), lambda i: (i, 0)), name=name,
        compiler_params=_params("parallel"))(x, g)


def _mm_rms_bwd(pairs, x, g, dres, name):
    rows, d = x.shape
    assert all(_mm_dims(*p)[:2] == (rows, d) for p in pairs)
    kbytes = [(_mm_dims(a, b, mode)[2], a.dtype.itemsize, b.dtype.itemsize) for a, b, mode in pairs]
    tm = max(t for t in (512, 256, 128) if rows % t == 0 and
             2 * (sum(t * k * ab + d * k * bb for k, ab, bb in kbytes) + t * d * 14) + 2 * t * d * 4 <= MM_VMEM_BUDGET)
    dims = {"nn": (((1,), (0,)), ((), ())), "nt": (((1,), (1,)), ((), ()))}
    n_pairs = len(pairs)

    def body(*refs):
        x_ref, g_ref, dres_ref = refs[2 * n_pairs:2 * n_pairs + 3]
        dx_ref, dxb_ref, dg_ref = refs[2 * n_pairs + 3:]
        dhv = functools.reduce(lambda u, v: u + v, [
            lax.dot_general(refs[2 * p][...].astype(BF16), refs[2 * p + 1][...].astype(BF16), dims[pairs[p][2]],
                            preferred_element_type=F32) for p in range(n_pairs)])
        xv = x_ref[...]
        r = lax.rsqrt(jnp.mean(xv * xv, axis=-1, keepdims=True) + RMS_EPS)
        xh = xv * r
        dxh = dhv * g_ref[...]
        dxv = dres_ref[...] + r * (dxh - xh * jnp.mean(dxh * xh, axis=-1, keepdims=True))
        dx_ref[...] = dxv
        dxb_ref[...] = dxv.astype(BF16)
        part = jnp.sum(dhv * xh, axis=0, keepdims=True)

        @pl.when(pl.program_id(0) == 0)
        def _():
            dg_ref[...] = part

        @pl.when(pl.program_id(0) > 0)
        def _():
            dg_ref[...] += part

    row = pl.BlockSpec((tm, d), lambda i: (i, 0))
    vec = pl.BlockSpec((1, d), lambda i: (0, 0))
    in_specs, args = [], []
    for a, b, mode in pairs:
        in_specs += [pl.BlockSpec((tm, a.shape[1]), lambda i: (i, 0)), pl.BlockSpec(b.shape, lambda i: (0, 0))]
        args += [a, b]
    return pl.pallas_call(
        body, out_shape=[jax.ShapeDtypeStruct((rows, d), F32), jax.ShapeDtypeStruct((rows, d), BF16),
                         jax.ShapeDtypeStruct((1, d), F32)],
        grid=(rows // tm,), in_specs=in_specs + [row, vec, row], out_specs=[row, row, vec],
        name=name, compiler_params=_params("arbitrary"))(*args, x, g, dres)


def _ple_bwd(dx, z, e):
    s = _sigmoid(z.astype(F32))
    return dx * s, dx * e.astype(F32) * (s * (1.0 - s))


def _loss_grad(y, t, z, e):
    rows, d = y.shape
    tm = ROW_TILE

    def body(y_ref, t_ref, z_ref, e_ref, dy_ref, de_ref, dz_ref, l_ref):
        err = y_ref[...] - t_ref[...]
        dy = err * (1.0 / d)
        dy_ref[...] = dy
        de, dz = _ple_bwd(dy, z_ref[...], e_ref[...])
        de_ref[...] = de.astype(BF16)
        dz_ref[...] = dz.astype(BF16)
        part = jnp.zeros((1, LANES), F32) + jnp.sum(err * err) * (0.5 / d)

        @pl.when(pl.program_id(0) == 0)
        def _():
            l_ref[...] = part

        @pl.when(pl.program_id(0) > 0)
        def _():
            l_ref[...] += part

    row = pl.BlockSpec((tm, d), lambda i: (i, 0))
    return pl.pallas_call(
        body, out_shape=[jax.ShapeDtypeStruct((rows, d), F32), jax.ShapeDtypeStruct((rows, d), BF16),
                         jax.ShapeDtypeStruct((rows, d), BF16), jax.ShapeDtypeStruct((1, LANES), F32)],
        grid=(rows // tm,), in_specs=[row] * 4, out_specs=[row, row, row, pl.BlockSpec((1, LANES), lambda i: (0, 0))],
        name="loss_grad", compiler_params=_params("arbitrary"))(y, t, z, e)


def _swap_halves(v):
    return pltpu.roll(v, HEAD_DIM, axis=1)


def _expand_kv(kv):
    left = lax.broadcasted_iota(jnp.int32, kv.shape, 1) < HEAD_DIM
    sw = _swap_halves(kv)
    h0 = jnp.where(left, kv, sw)
    h1 = jnp.where(left, sw, kv)
    return jnp.concatenate([h0, h0, h1, h1], axis=1)


def _reduce_kv(dkv):
    left = lax.broadcasted_iota(jnp.int32, (dkv.shape[0], LANES), 1) < HEAD_DIM
    t = dkv[:, 0:LANES] + dkv[:, LANES:2 * LANES]
    u = dkv[:, 2 * LANES:3 * LANES] + dkv[:, 3 * LANES:4 * LANES]
    t = t + _swap_halves(t)
    u = u + _swap_halves(u)
    return jnp.where(left, t, u)


def _qknorm_fwd(proj, gqa, gka, gqb, gkb):
    rows = proj.shape[0]
    tm = ROW_TILE

    def body(qa_ref, ka_ref, qb_ref, kb_ref, vb_ref, gqa_ref, gka_ref, gqb_ref, gkb_ref, oqa, oka, oqb, okb, ovb):
        for src, g_ref, dst in ((qa_ref, gqa_ref, oqa), (ka_ref, gka_ref, oka), (qb_ref, gqb_ref, oqb)):
            xv = src[...]
            dst[...] = (xv * _seg_rstd(xv)) * g_ref[...]
        kv = kb_ref[...]
        okb[...] = _expand_kv((kv * _seg_rstd(kv)) * gkb_ref[...])
        ovb[...] = _expand_kv(vb_ref[...])

    def win(width, off):
        return pl.BlockSpec((tm, width), lambda i: (i, off // width))

    vec = lambda w: pl.BlockSpec((1, w), lambda i: (0, 0))
    out = pl.BlockSpec((tm, WIDTH), lambda i: (i, 0))
    return pl.pallas_call(
        body, out_shape=[jax.ShapeDtypeStruct((rows, WIDTH), F32)] * 5, grid=(rows // tm,),
        in_specs=[win(WIDTH, OFF_QA), win(WIDTH, OFF_KA), win(WIDTH, OFF_QB), win(LANES, OFF_KB), win(LANES, OFF_VB),
                  vec(WIDTH), vec(WIDTH), vec(WIDTH), vec(LANES)],
        out_specs=[out] * 5, name="qknorm_fwd", compiler_params=_params("parallel"))(
            proj, proj, proj, proj, proj, gqa, gka, gqb, gkb)


def _norm_bwd(xv, g, dy):
    r = _seg_rstd(xv)
    xh = xv * r
    dxh = dy * g
    dx = r * (dxh - xh * (_seg_sum(dxh * xh) * (1.0 / HEAD_DIM)))
    return dx, jnp.sum(dy * xh, axis=0, keepdims=True)


def _qknorm_bwd(proj, gqa, gka, gqb, gkb, dqa, dka, dva, dqb, dkb, dvb, dga, dgb):
    rows = proj.shape[0]
    tm = ROW_TILE
    n_a = len(dqa)

    def body(*refs):
        qa_ref, ka_ref, qb_ref, kb_ref, gqa_ref, gka_ref, gqb_ref, gkb_ref = refs[:8]
        pos = 8
        dqa_refs, dka_refs, dva_refs = refs[pos:pos + n_a], refs[pos + n_a:pos + 2 * n_a], refs[pos + 2 * n_a:pos + 3 * n_a]
        pos += 3 * n_a
        dqb_ref, dkb_ref, dvb_ref, dga_ref, dgb_ref = refs[pos:pos + 5]
        dproj_ref, ogqa, ogka, ogqb, ogkb = refs[pos + 5:]

        def total(rs):
            acc = rs[0][...]
            for r in rs[1:]:
                acc = acc + r[...]
            return acc

        dx_qa, p_qa = _norm_bwd(qa_ref[...], gqa_ref[...], total(dqa_refs))
        dx_ka, p_ka = _norm_bwd(ka_ref[...], gka_ref[...], total(dka_refs))
        dx_qb, p_qb = _norm_bwd(qb_ref[...], gqb_ref[...], dqb_ref[...])
        dx_kb, p_kb = _norm_bwd(kb_ref[...], gkb_ref[...], _reduce_kv(dkb_ref[...]))
        dproj_ref[:, OFF_QA:OFF_QA + WIDTH] = dx_qa.astype(BF16)
        dproj_ref[:, OFF_KA:OFF_KA + WIDTH] = dx_ka.astype(BF16)
        dproj_ref[:, OFF_VA:OFF_VA + WIDTH] = total(dva_refs).astype(BF16)
        dproj_ref[:, OFF_QB:OFF_QB + WIDTH] = dx_qb.astype(BF16)
        dproj_ref[:, OFF_KB:OFF_KB + LANES] = dx_kb.astype(BF16)
        dproj_ref[:, OFF_VB:OFF_VB + LANES] = _reduce_kv(dvb_ref[...]).astype(BF16)
        dproj_ref[:, OFF_GA:OFF_GB] = dga_ref[...]
        dproj_ref[:, OFF_GB:D_IN] = dgb_ref[...]
        first = pl.program_id(0) == 0
        for o_ref, part in ((ogqa, p_qa), (ogka, p_ka), (ogqb, p_qb), (ogkb, p_kb)):
            @pl.when(first)
            def _(o_ref=o_ref, part=part):
                o_ref[...] = part

            @pl.when(jnp.logical_not(first))
            def _(o_ref=o_ref, part=part):
                o_ref[...] += part

    def win(width, off):
        return pl.BlockSpec((tm, width), lambda i: (i, off // width))

    vec = lambda w: pl.BlockSpec((1, w), lambda i: (0, 0))
    row = lambda w: pl.BlockSpec((tm, w), lambda i: (i, 0))
    in_specs = [win(WIDTH, OFF_QA), win(WIDTH, OFF_KA), win(WIDTH, OFF_QB), win(LANES, OFF_KB),
                vec(WIDTH), vec(WIDTH), vec(WIDTH), vec(LANES)]
    in_specs += [row(WIDTH)] * (3 * n_a + 3) + [row(D_MODEL)] * 2
    return pl.pallas_call(
        body,
        out_shape=[jax.ShapeDtypeStruct((rows, D_IN), BF16), jax.ShapeDtypeStruct((1, WIDTH), F32),
                   jax.ShapeDtypeStruct((1, WIDTH), F32), jax.ShapeDtypeStruct((1, WIDTH), F32),
                   jax.ShapeDtypeStruct((1, LANES), F32)],
        grid=(rows // tm,), in_specs=in_specs,
        out_specs=[row(D_IN), vec(WIDTH), vec(WIDTH), vec(WIDTH), vec(LANES)],
        name="qknorm_bwd", compiler_params=_params("arbitrary"))(
            proj, proj, proj, proj, gqa, gka, gqb, gkb, *dqa, *dka, *dva, dqb, dkb, dvb, dga, dgb)


def _t5_bucket(rel):
    half_b = NUM_BUCKETS // 2
    max_exact = half_b // 2
    sign = jnp.where(rel > 0, half_b, 0)
    n = jnp.abs(rel)
    nf = jnp.maximum(n, 1).astype(F32)
    large = max_exact + (jnp.log(nf / max_exact) / math.log(MAX_DISTANCE / max_exact)
                         * (half_b - max_exact)).astype(jnp.int32)
    large = jnp.minimum(large, half_b - 1)
    return sign + jnp.where(n < max_exact, n, large)


def _band_buckets(blk, dilation):
    i = jnp.arange(blk, dtype=jnp.int32)[:, None]
    j = jnp.arange(3 * blk, dtype=jnp.int32)[None, :]
    rel = j - blk - i
    return jnp.where(jnp.abs(rel) <= blk, _t5_bucket(rel * dilation), -1)


def _bias_tiles(table, buckets, head_off, name):
    blk = buckets.shape[0]

    def body(tab_ref, bk_ref, o_ref):
        h = pl.program_id(0) + head_off
        bk = bk_ref[...]
        acc = jnp.full(bk.shape, NEG_INF, F32)
        for b in range(NUM_BUCKETS):
            acc = jnp.where(bk == b, tab_ref[b, h], acc)
        o_ref[0] = acc

    return pl.pallas_call(
        body, out_shape=jax.ShapeDtypeStruct((N_HEADS, blk, 3 * blk), F32), grid=(N_HEADS,),
        in_specs=[pl.BlockSpec(memory_space=pltpu.SMEM), pl.BlockSpec((blk, 3 * blk), lambda h: (0, 0))],
        out_specs=pl.BlockSpec((1, blk, 3 * blk), lambda h: (h, 0, 0)),
        name=name, compiler_params=_params("parallel"))(table, buckets)


def _table_grad(dbias, buckets, name):
    blk = buckets.shape[0]

    def body(db_ref, bk_ref, o_ref):
        bk = bk_ref[...]
        dbv = db_ref[0]
        lane = lax.broadcasted_iota(jnp.int32, (1, LANES), 1)
        acc = jnp.zeros((1, LANES), F32)
        for b in range(NUM_BUCKETS):
            acc = jnp.where(lane == b, jnp.sum(jnp.where(bk == b, dbv, 0.0)), acc)
        o_ref[0] = acc

    out = pl.pallas_call(
        body, out_shape=jax.ShapeDtypeStruct((N_HEADS, 1, LANES), F32), grid=(N_HEADS,),
        in_specs=[pl.BlockSpec((1, blk, 3 * blk), lambda h: (h, 0, 0)), pl.BlockSpec((blk, 3 * blk), lambda h: (0, 0))],
        out_specs=pl.BlockSpec((1, 1, LANES), lambda h: (h, 0, 0)),
        name=name, compiler_params=_params("parallel"))(dbias, buckets)
    return out[:, 0, :NUM_BUCKETS]


def _dot_nt(a, b):
    return lax.dot_general(a, b, (((1,), (1,)), ((), ())), preferred_element_type=F32)


def _dot_tn(a, b):
    return lax.dot_general(a, b, (((0,), (0,)), ((), ())), preferred_element_type=F32)


def _stack_pair(x2, left):
    return jnp.concatenate([jnp.where(left, x2, 0.0), jnp.where(left, 0.0, x2)], axis=0).astype(BF16)


def _attn_geometry(blk, d):
    halo = blk * d
    subs = max(1, min(ITEMS // d, MAX_CHUNK // halo))
    return subs, min(d, ITEMS // subs), halo


def _item_of(j, r0, per_group):
    return j // per_group, r0 + j % per_group


def _span_rows(ref, first, r, blk, d):
    if d == 1:
        return ref[first:first + blk, :]
    return ref[pl.ds(first + r, blk, stride=d), :]


def _set_span_rows(ref, first, r, blk, d, val, add=False):
    idx = slice(first, first + blk) if d == 1 else pl.ds(first + r, blk, stride=d)
    ref[idx, :] = ref[idx, :] + val if add else val


def _for_groups(group, groups, per_group):
    if groups == 1:
        group(0)
    else:
        def step(g, carry):
            group(g * per_group)
            return carry

        lax.fori_loop(0, groups, step, 0)


def _key_rows(p_ref, c_ref, n_ref, s, r, subs, halo, blk, d):
    parts = []
    for span in (s - 1, s, s + 1):
        if span < 0:
            parts.append(_span_rows(p_ref, 0, r, blk, d))
        elif span == subs:
            parts.append(_span_rows(n_ref, 0, r, blk, d))
        else:
            parts.append(_span_rows(c_ref, span * halo, r, blk, d))
    return jnp.concatenate(parts, axis=0)


def _item_penalty(t, nct, s, subs, blk):
    first_ok = True if s > 0 else t > 0
    last_ok = True if s < subs - 1 else t < nct - 1
    col = lax.broadcasted_iota(jnp.int32, (1, 3 * blk), 1)
    ok = jnp.logical_and(jnp.logical_or(col >= blk, first_ok), jnp.logical_or(col < 2 * blk, last_ok))
    return jnp.where(ok, 0.0, NEG_INF).astype(F32)


def _attn_specs(seq, blk, d, step_of, col=0):
    subs, _, halo = _attn_geometry(blk, d)
    last, first = seq // halo - 1, col // LANES
    cur = pl.BlockSpec((subs * halo, LANES), lambda hp, t: (step_of(t), first + hp))
    prev = pl.BlockSpec((halo, LANES), lambda hp, t: (jnp.clip(step_of(t) * subs - 1, 0, last), first + hp))
    nxt = pl.BlockSpec((halo, LANES), lambda hp, t: (jnp.minimum((step_of(t) + 1) * subs, last), first + hp))
    return cur, prev, nxt


def _attn_fwd(q, k, v, bias, sink, blk, d, name, v_col=0):
    seq = q.shape[0]
    subs, per_group, halo = _attn_geometry(blk, d)
    items, chunk, groups = subs * per_group, subs * halo, d // per_group
    nct = seq // chunk
    has_sink = sink is not None
    scale = HEAD_DIM ** -0.5

    def body(*refs):
        q_ref, kp, kc, kn, vp, vc, vn, b_ref = refs[:8]
        s_ref = refs[8] if has_sink else None
        o_ref, l_ref = refs[-2], refs[-1]
        t = pl.program_id(1)
        left = lax.broadcasted_iota(jnp.int32, (1, LANES), 1) < HEAD_DIM
        bias2 = b_ref[...]

        def group(r0):
            scores, vcats = [], []
            for j in range(items):
                s, r = _item_of(j, r0, per_group)
                qs = _stack_pair(_span_rows(q_ref, s * halo, r, blk, d) * scale, left)
                kcat = _key_rows(kp, kc, kn, s, r, subs, halo, blk, d).astype(BF16)
                scores.append(_dot_nt(qs, kcat) + bias2 + _item_penalty(t, nct, s, subs, blk))
                vcats.append(_key_rows(vp, vc, vn, s, r, subs, halo, blk, d).astype(BF16))
            ms = [jnp.max(s, axis=-1, keepdims=True) for s in scores]
            if has_sink:
                sk = s_ref[...]
                ms = [jnp.maximum(m, sk) for m in ms]
            ps = [jnp.exp(s - m) for s, m in zip(scores, ms)]
            dens = [jnp.sum(p, axis=-1, keepdims=True) for p in ps]
            if has_sink:
                dens = [den + jnp.exp(sk - m) for den, m in zip(dens, ms)]
            pns = [(p * (1.0 / den)).astype(BF16) for p, den in zip(ps, dens)]
            lses = [m + jnp.log(den) for m, den in zip(ms, dens)]
            for j in range(items):
                s, r = _item_of(j, r0, per_group)
                o2 = jnp.dot(pns[j], vcats[j], preferred_element_type=F32)
                _set_span_rows(o_ref, s * halo, r, blk, d, jnp.where(left, o2[:blk], o2[blk:]))
                _set_span_rows(l_ref, s * halo, r, blk, d, jnp.where(left, lses[j][:blk], lses[j][blk:]))

        _for_groups(group, groups, per_group)

    cur, prev, nxt = _attn_specs(seq, blk, d, lambda t: t)
    v_cur, v_prev, v_nxt = _attn_specs(seq, blk, d, lambda t: t, v_col)
    in_specs = [cur, prev, cur, nxt, v_prev, v_cur, v_nxt, pl.BlockSpec((2 * blk, 3 * blk), lambda hp, t: (hp, 0))]
    args = [q, k, k, k, v, v, v, bias]
    if has_sink:
        in_specs.append(pl.BlockSpec((2 * blk, 1), lambda hp, t: (hp, 0)))
        args.append(sink)
    return pl.pallas_call(
        body, out_shape=[jax.ShapeDtypeStruct((seq, WIDTH), F32)] * 2, grid=(N_PAIRS, nct),
        in_specs=in_specs, out_specs=[cur, cur], name=name,
        compiler_params=_params("parallel", "parallel"))(*args)


def _attn_bwd(q, k, v, do, lse, delta, bias, sink, blk, d, name, v_col=0):
    seq = q.shape[0]
    subs, per_group, halo = _attn_geometry(blk, d)
    items, chunk, groups = subs * per_group, subs * halo, d // per_group
    nct = seq // chunk
    has_sink = sink is not None
    n_in = 12 if has_sink else 11
    scale = HEAD_DIM ** -0.5

    def body(*refs):
        q_ref, kp, kc, kn, vp, vc, vn, do_ref, l_ref, d_ref, b_ref = refs[:11]
        s_ref = refs[11] if has_sink else None
        dq_ref, dk_ref, dv_ref, db_ref = refs[n_in:n_in + 4]
        ds_ref = refs[n_in + 4] if has_sink else None
        wk, wv = refs[-2], refs[-1]
        t = pl.program_id(1)

        @pl.when(t == 0)
        def _():
            wk[...] = jnp.zeros_like(wk)
            wv[...] = jnp.zeros_like(wv)
            db_ref[...] = jnp.zeros_like(db_ref)
            if has_sink:
                ds_ref[...] = jnp.zeros_like(ds_ref)

        @pl.when(t > 0)
        def _():
            for w in (wk, wv):
                keep = w[chunk:2 * chunk + halo]
                w[0:chunk + halo] = keep
                w[chunk + halo:2 * chunk + halo] = jnp.zeros((chunk, LANES), F32)

        @pl.when(t < nct)
        def _():
            lane = lax.broadcasted_iota(jnp.int32, (1, LANES), 1)
            left = lane < HEAD_DIM
            bias2 = b_ref[...]

            def group(r0):
                qss, doss, kcats, scores, dps, lcols, dcols = [], [], [], [], [], [], []
                for j in range(items):
                    s, r = _item_of(j, r0, per_group)
                    qs = _stack_pair(_span_rows(q_ref, s * halo, r, blk, d) * scale, left)
                    dos = _stack_pair(_span_rows(do_ref, s * halo, r, blk, d), left)
                    kcat = _key_rows(kp, kc, kn, s, r, subs, halo, blk, d).astype(BF16)
                    vcat = _key_rows(vp, vc, vn, s, r, subs, halo, blk, d).astype(BF16)
                    l2, d2 = _span_rows(l_ref, s * halo, r, blk, d), _span_rows(d_ref, s * halo, r, blk, d)
                    lcols.append(jnp.concatenate([jnp.max(jnp.where(left, l2, NEG_INF), axis=-1, keepdims=True),
                                                  jnp.max(jnp.where(left, NEG_INF, l2), axis=-1, keepdims=True)], axis=0))
                    dcols.append(jnp.concatenate([jnp.sum(jnp.where(lane == 0, d2, 0.0), axis=-1, keepdims=True),
                                                  jnp.sum(jnp.where(lane == HEAD_DIM, d2, 0.0), axis=-1, keepdims=True)],
                                                 axis=0))
                    scores.append(_dot_nt(qs, kcat) + bias2 + _item_penalty(t, nct, s, subs, blk))
                    dps.append(_dot_nt(dos, vcat))
                    qss.append(qs)
                    doss.append(dos)
                    kcats.append(kcat)
                ps = [jnp.exp(s - lc) for s, lc in zip(scores, lcols)]
                dss = [p * (dp - dc) for p, dp, dc in zip(ps, dps, dcols)]
                db_ref[...] += functools.reduce(lambda a, b: a + b, dss)
                if has_sink:
                    sk = s_ref[...]
                    ds_ref[...] -= functools.reduce(lambda a, b: a + b, [dc * jnp.exp(sk - lc) for dc, lc in zip(dcols, lcols)])
                dsbs = [ds.astype(BF16) for ds in dss]
                for j in range(items):
                    s, r = _item_of(j, r0, per_group)
                    dq2 = jnp.dot(dsbs[j], kcats[j], preferred_element_type=F32) * scale
                    _set_span_rows(dq_ref, s * halo, r, blk, d, jnp.where(left, dq2[:blk], dq2[blk:]))
                news = [(_dot_tn(dsbs[j], qss[j]), _dot_tn(ps[j].astype(BF16), doss[j])) for j in range(items)]
                for which, w in enumerate((wk, wv)):
                    for jr in range(per_group):
                        for sp in range(-1, subs + 1):
                            parts = [news[s * per_group + jr][which][(sp - s + 1) * blk:(sp - s + 2) * blk]
                                     for s in range(subs) if 0 <= sp - s + 1 < 3]
                            _set_span_rows(w, chunk + sp * halo, r0 + jr, blk, d,
                                           functools.reduce(lambda x, y: x + y, parts), add=True)

            _for_groups(group, groups, per_group)

        dk_ref[...] = wk[0:chunk]
        dv_ref[...] = wv[0:chunk]

    cur, prev, nxt = _attn_specs(seq, blk, d, lambda t: jnp.minimum(t, nct - 1))
    v_cur, v_prev, v_nxt = _attn_specs(seq, blk, d, lambda t: jnp.minimum(t, nct - 1), v_col)
    lag = pl.BlockSpec((chunk, LANES), lambda hp, t: (jnp.maximum(t - 1, 0), hp))
    band = pl.BlockSpec((2 * blk, 3 * blk), lambda hp, t: (hp, 0))
    col = pl.BlockSpec((2 * blk, 1), lambda hp, t: (hp, 0))
    in_specs = [cur, prev, cur, nxt, v_prev, v_cur, v_nxt, cur, cur, cur, band]
    args = [q, k, k, k, v, v, v, do, lse, delta, bias]
    out_shape = [jax.ShapeDtypeStruct((seq, WIDTH), F32)] * 3 + [jax.ShapeDtypeStruct((N_HEADS * blk, 3 * blk), F32)]
    out_specs = [cur, lag, lag, band]
    if has_sink:
        in_specs.append(col)
        args.append(sink)
        out_shape.append(jax.ShapeDtypeStruct((N_HEADS * blk, 1), F32))
        out_specs.append(col)
    window = pltpu.VMEM((2 * chunk + halo, LANES), F32)
    return pl.pallas_call(
        body, out_shape=out_shape, grid=(N_PAIRS, nct + 1), in_specs=in_specs, out_specs=out_specs,
        scratch_shapes=[window, window], name=name,
        compiler_params=_params("arbitrary", "arbitrary"))(*args)


def _combine_patterns(outs, lses):
    def combine(*tiles):
        os_, ls = tiles[:len(outs)], tiles[len(outs):]
        m = functools.reduce(jnp.maximum, ls)
        es = [jnp.exp(l - m) for l in ls]
        den = functools.reduce(lambda a, b: a + b, es)
        num = functools.reduce(lambda a, b: a + b, [e * o for e, o in zip(es, os_)])
        return num / den, m + jnp.log(den)

    return _ew(combine, [*outs, *lses], [F32, F32], "combine_a")


def _tile_gain(g, reps):
    return jnp.tile(g[None, :], (1, reps))


def _local_step(x, p, target, w_in_of, rest_of, small, progress):
    rel_table = small["rel_table"]
    buckets_a = [_band_buckets(blk, d) for blk, d in DILATED]
    buckets_b = _band_buckets(BLK_B, 1)
    bias_a = [_bias_tiles(rel_table, bk, 0, "bias_a").reshape(N_HEADS * bk.shape[0], -1) for bk in buckets_a]
    bias_b = _bias_tiles(rel_table, buckets_b, N_HEADS, "bias_b").reshape(N_HEADS * BLK_B, -1)

    saved = []
    for l in range(DEPTH):
        g_mix, g_ffn, g_ple = (small[n][l][None, :] for n in ("norm_mix_g", "norm_ffn_g", "norm_ple_g"))
        gqa, gka, gqb = (_tile_gain(small[n][l], N_HEADS) for n in ("qnorm_a_g", "knorm_a_g", "qnorm_b_g"))
        gkb = _tile_gain(small["knorm_b_g"][l], N_KV_B)
        sink = jnp.repeat(small["sink_b"][l], BLK_B)[:, None]

        h = _rms_fwd(x, g_mix, "rms_mix") if l == 0 else h_next
        w_in = w_in_of(l, (h, bias_a, bias_b)) if l == 0 else w_in_next
        proj = _mm(h, w_in, "nt", F32, "mm_in")
        qa, ka, qb, kb, vb = _qknorm_fwd(proj, gqa, gka, gqb, gkb)
        outs, lses = [], []
        for (blk, d), bias in zip(DILATED, bias_a):
            o, ls = _attn_fwd(qa, ka, proj, bias, None, blk, d, f"attn_a{d}_fwd", v_col=OFF_VA)
            outs.append(o)
            lses.append(ls)
        ya, lse_a = _combine_patterns(outs, lses)
        yb, lse_b = _attn_fwd(qb, kb, vb, bias_b, sink, BLK_B, 1, "attn_b_fwd")
        w = dict(rest_of(l, yb), w_in=w_in)
        def gate(products, extra):
            (ca_, cb_), (ga_, gb_) = products, extra
            return _sigmoid(ga_) * ca_ + _sigmoid(gb_) * cb_, ca_, cb_

        merged, ca, cb = _mm_fused([(ya, w["w_branch_a"], "nn"), (yb, w["w_branch_b"], "nn")],
                                   [(proj, OFF_GA), (proj, OFF_GB)], gate, [BF16, BF16, BF16], "mm_branches_gate")
        x1, h2 = _mm_res_norm(merged, w["w_out"], "nn", x, g_ffn, "mm_out_norm")

        def swiglu(products, extra):
            a_, u_ = products
            return (a_ * _sigmoid(a_)) * u_, a_, u_

        hid, a, u = _mm_fused([(h2, w["w_ffn_gate"], "nt"), (h2, w["w_ffn_up"], "nt")], [], swiglu,
                              [BF16, BF16, BF16], "mm_ffn_gate_up")
        if l + 1 < DEPTH:
            w_in_next = w_in_of(l + 1, hid)
        x2, h3 = _mm_res_norm(hid, w["w_ffn_down"], "nn", x1, g_ple, "mm_ffn_down_norm")

        def ple(products, extra):
            z_, e_ = products
            return extra[0] + _sigmoid(z_) * e_, z_, e_

        next_gain = small["norm_mix_g"][l + 1][None, :] if l + 1 < DEPTH else None
        x3, z, e, *rest = _mm_fused([(h3, w["w_ple_gate"], "nn"), (p[l], w["w_ple_proj"], "nn")], [(x2, 0)], ple,
                                    [F32, BF16, BF16], "mm_ple", next_gain=next_gain)
        h_next = rest[0] if rest else None
        saved.append(dict(w=w, x0=x, h=h, proj=proj, qa=qa, ka=ka, qb=qb, kb=kb, vb=vb, ya=ya, lse_a=lse_a,
                          yb=yb, lse_b=lse_b, ca=ca, cb=cb, merged=merged, x1=x1, h2=h2, a=a, u=u, hid=hid,
                          x2=x2, h3=h3, z=z, e=e))
        x = x3

    dx, de, dz, loss_acc = _loss_grad(x, target, saved[-1]["z"], saved[-1]["e"])
    loss = loss_acc[0, 0]

    gbig = [{} for _ in range(DEPTH)]
    marks = [{} for _ in range(DEPTH)]
    gsmall = {n: [None] * DEPTH for n in SMALL if n != "rel_table"}
    dbias_a = [[] for _ in DILATED]
    dbias_b = []

    for l in reversed(range(DEPTH)):
        sv = saved[l]
        w = sv["w"]
        g_mix, g_ffn, g_ple = (small[n][l][None, :] for n in ("norm_mix_g", "norm_ffn_g", "norm_ple_g"))
        gqa, gka, gqb = (_tile_gain(small[n][l], N_HEADS) for n in ("qnorm_a_g", "knorm_a_g", "qnorm_b_g"))
        gkb = _tile_gain(small["knorm_b_g"][l], N_KV_B)
        sink = jnp.repeat(small["sink_b"][l], BLK_B)[:, None]

        if l < DEPTH - 1:
            de, dz = _ew(_ple_bwd, [dx, sv["z"], sv["e"]], [BF16, BF16], "ple_bwd")
        gbig[l]["w_ple_proj"] = _mm(p[l], de, "tn", BF16, "mm_d_ple_proj")
        gbig[l]["w_ple_gate"] = _mm(sv["h3"], dz, "tn", BF16, "mm_d_ple_gate")
        dx, dxb, gsmall["norm_ple_g"][l] = _mm_rms_bwd([(dz, w["w_ple_gate"], "nt")], sv["x2"], g_ple, dx,
                                                      "mm_dh3_rms_bwd")

        gbig[l]["w_ffn_down"] = _mm(sv["hid"], dxb, "tn", BF16, "mm_d_ffn_down")

        def swiglu_bwd(products, extra):
            dh_, a_, u_ = products[0], extra[0].astype(F32), extra[1].astype(F32)
            s = _sigmoid(a_)
            return dh_ * u_ * (s * (1.0 + a_ * (1.0 - s))), dh_ * (a_ * s)

        da, du = _mm_fused([(dxb, w["w_ffn_down"], "nt")], [(sv["a"], 0), (sv["u"], 0)], swiglu_bwd, [BF16, BF16],
                           "mm_dhid_swiglu_bwd")
        gbig[l]["w_ffn_gate"], gbig[l]["w_ffn_up"] = _mm_fused(
            [(da, sv["h2"], "tn"), (du, sv["h2"], "tn")], [], lambda products, extra: products, [BF16, BF16],
            "mm_d_ffn_gate_up")
        dx, dxb, gsmall["norm_ffn_g"][l] = _mm_rms_bwd([(da, w["w_ffn_gate"], "nn"), (du, w["w_ffn_up"], "nn")],
                                                      sv["x1"], g_ffn, dx, "mm_dh2_rms_bwd")

        gbig[l]["w_out"] = _mm(sv["merged"], dxb, "tn", BF16, "mm_d_out")

        def gate_bwd(products, extra):
            dm_, ca_, cb_ = products[0], extra[0].astype(F32), extra[1].astype(F32)
            sa, sb = _sigmoid(extra[2]), _sigmoid(extra[3])
            return dm_ * sa, dm_ * sb, dm_ * ca_ * (sa * (1.0 - sa)), dm_ * cb_ * (sb * (1.0 - sb))

        dca, dcb, dga, dgb = _mm_fused(
            [(dxb, w["w_out"], "nt")], [(sv["ca"], 0), (sv["cb"], 0), (sv["proj"], OFF_GA), (sv["proj"], OFF_GB)],
            gate_bwd, [BF16, BF16, BF16, BF16], "mm_dmerged_gate_bwd")
        gbig[l]["w_branch_a"], gbig[l]["w_branch_b"] = _mm_fused(
            [(sv["ya"], dca, "tn"), (sv["yb"], dcb, "tn")], [], lambda products, extra: products, [BF16, BF16],
            "mm_d_branches")
        dca = progress(l, ("rest", "ready"), dca, gbig[l])

        def with_row_dots(products, extra):
            (dya_, dyb_), (ya_, yb_) = products, extra
            return dya_, _seg_sum(dya_ * ya_), dyb_, _seg_sum(dyb_ * yb_)

        dya, delta_a, dyb, delta_b = _mm_fused(
            [(dca, w["w_branch_a"], "nt"), (dcb, w["w_branch_b"], "nt")], [(sv["ya"], 0), (sv["yb"], 0)], with_row_dots,
            [F32, F32, F32, F32], "mm_dy_branches")

        dqa, dka, dva = [], [], []
        for (blk, d), bias, bk in zip(DILATED, bias_a, buckets_a):
            dq_, dk_, dv_, db_ = _attn_bwd(sv["qa"], sv["ka"], sv["proj"], dya, sv["lse_a"], delta_a, bias, None, blk, d,
                                           f"attn_a{d}_bwd", v_col=OFF_VA)
            dqa.append(dq_)
            dka.append(dk_)
            dva.append(dv_)
            dbias_a[len(dqa) - 1].append(db_)
        dyb = progress(l, ("rest", "under way"), dyb, None)
        dqb, dkb, dvb, db_, dsink = _attn_bwd(sv["qb"], sv["kb"], sv["vb"], dyb, sv["lse_b"], delta_b, bias_b, sink,
                                              BLK_B, 1, "attn_b_bwd")
        dbias_b.append(db_)
        gsmall["sink_b"][l] = dsink.reshape(N_HEADS, BLK_B).sum(axis=1)

        dproj, pqa, pka, pqb, pkb = _qknorm_bwd(sv["proj"], gqa, gka, gqb, gkb, dqa, dka, dva, dqb, dkb, dvb, dga, dgb)
        marks[l]["attn_bwd_done"] = dproj
        gsmall["qnorm_a_g"][l] = pqa.reshape(N_HEADS, HEAD_DIM).sum(0)
        gsmall["knorm_a_g"][l] = pka.reshape(N_HEADS, HEAD_DIM).sum(0)
        gsmall["qnorm_b_g"][l] = pqb.reshape(N_HEADS, HEAD_DIM).sum(0)
        gsmall["knorm_b_g"][l] = pkb.reshape(N_KV_B, HEAD_DIM).sum(0)
        gbig[l]["w_in"] = _mm(dproj, sv["h"], "tn", BF16, "mm_d_in")
        dproj = progress(l, ("w_in", "ready"), dproj, gbig[l])
        dx, _, gsmall["norm_mix_g"][l] = _mm_rms_bwd([(dproj, w["w_in"], "nn")], sv["x0"], g_mix, dx, "mm_dh_rms_bwd")
        dx = progress(l, ("w_in", "under way"), dx, None)
        gsmall["norm_mix_g"][l] = gsmall["norm_mix_g"][l][0]
        gsmall["norm_ffn_g"][l] = gsmall["norm_ffn_g"][l][0]
        gsmall["norm_ple_g"][l] = gsmall["norm_ple_g"][l][0]

    gsmall = {n: jnp.stack(v) for n, v in gsmall.items()}
    dtable_a = sum(_table_grad(sum(dbs).reshape(N_HEADS, blk, 3 * blk), bk, "table_grad_a")
                   for dbs, (blk, _), bk in zip(dbias_a, DILATED, buckets_a))
    dtable_b = _table_grad(sum(dbias_b).reshape(N_HEADS, BLK_B, 3 * BLK_B), buckets_b, "table_grad_b")
    gsmall["rel_table"] = jnp.concatenate([dtable_a, dtable_b], axis=0).T
    return loss, dx, gbig, gsmall, marks


def _place():
    return lax.axis_index("x"), lax.axis_index("y"), lax.axis_index("c")


def _flip(v, bit):
    return 1 - v if bit else v


CHIP_RELATIONS = ((0, 1), (1, 0), (1, 1))
ANY = pl.BlockSpec(memory_space=pl.ANY)


def _allgather_body(w_refs, out_refs, send_sems, recv_sems):
    x, y, c = _place()
    chips = [(_flip(x, a), _flip(y, b)) for a, b in CHIP_RELATIONS]

    def make(g):
        w_ref, out_ref = w_refs[g], out_refs[g]
        half = w_ref.shape[0] // 2

        def part(px, py, pc):
            return out_ref.at[2 * px + py, pl.ds(pc * half, half), :]

        def copy(k, block, to, src=None):
            return pltpu.make_async_remote_copy(
                src_ref=part(*block) if src is None else src, dst_ref=part(*block),
                send_sem=send_sems.at[7 * g + k], recv_sem=recv_sems.at[7 * g + k], device_id=to,
                device_id_type=MESH_ID)

        own = pltpu.make_async_remote_copy(
            src_ref=w_ref, dst_ref=out_ref.at[2 * x + y], send_sem=send_sems.at[7 * g + 6],
            recv_sem=recv_sems.at[7 * g + 6], device_id=(x, y, 1 - c), device_id_type=MESH_ID)
        first = [copy(k, (x, y, c), (*chip, c), src=w_ref.at[pl.ds(c * half, half), :]) for k, chip in enumerate(chips)]
        passed = [copy(3 + k, (*chip, c), (x, y, 1 - c)) for k, chip in enumerate(chips)]
        arrive = [copy(k, (*chip, c), (x, y, c)) for k, chip in enumerate(chips)]
        arrive2 = [copy(3 + k, (*chip, 1 - c), (x, y, c)) for k, chip in enumerate(chips)]
        return own, first, passed, arrive, arrive2

    made = [make(g) for g in range(len(w_refs))]
    for own, first, _, _, _ in made:
        own.start()
        for cp in first:
            cp.start()
    for _, _, passed, arrive, _ in made:
        for k in range(3):
            arrive[k].wait_recv()
            passed[k].start()
    for own, first, passed, _, arrive2 in made:
        for k in range(3):
            arrive2[k].wait_recv()
        own.wait_recv()
        for cp in first + passed + [own]:
            cp.wait_send()


def _sibling(x, y, c):
    return [(x, y, 1 - c)]


def _same_core_of_other_chips(x, y, c):
    return [(_flip(x, a), _flip(y, b), c) for a, b in CHIP_RELATIONS]


def _exchange(body, ins, out_types, n_sems, name, sequencer=None):
    n = len(ins)
    sems = (pltpu.SemaphoreType.DMA((n_sems,)), pltpu.SemaphoreType.DMA((n_sems,)))
    if sequencer is None:
        in_place = out_types is None
        out_shape = [jax.ShapeDtypeStruct(a.shape, a.dtype) for a in ins] if in_place else out_types

        def tc_body(*refs):
            body(refs[:n], refs[n:n + len(out_shape)], refs[-2], refs[-1])

        return list(pl.pallas_call(
            tc_body, out_shape=out_shape, in_specs=[ANY] * n, out_specs=[ANY] * len(out_shape),
            input_output_aliases={g: g for g in range(n)} if in_place else {}, scratch_shapes=list(sems), name=name)(*ins))

    collective_id, peers = sequencer
    hbm = pltpu.MemorySpace.HBM
    in_refs = [jax.new_ref(a, memory_space=hbm) for a in ins]
    out_refs = in_refs if out_types is None else [jax.empty_ref(t, memory_space=hbm) for t in out_types]

    @pl.kernel(mesh=plsc.ScalarSubcoreMesh(axis_name="sequencer", num_cores=1), name=name, scratch_types=sems,
               compiler_params=pltpu.CompilerParams(collective_id=collective_id))
    def launch(send_sems, recv_sems):
        barrier = pltpu.get_barrier_semaphore()
        devices = peers(*_place())
        for device in devices:
            pl.semaphore_signal(barrier, inc=1, device_id=device, device_id_type=MESH_ID)
        pl.semaphore_wait(barrier, len(devices))
        body(in_refs, out_refs, send_sems, recv_sems)

    launch()
    return [r[...] for r in out_refs]


def _allgather(shards, name, sequencer=None):
    out_types = [jax.ShapeDtypeStruct((N_CHIPS,) + s.shape, s.dtype) for s in shards]
    if sequencer is not None:
        sequencer = (sequencer, lambda x, y, c: _sibling(x, y, c) + _same_core_of_other_chips(x, y, c))
    return _exchange(_allgather_body, shards, out_types, 7 * len(shards), name, sequencer)


def _half_tile(half):
    return max(t for t in range(16, 1025, 16) if half % t == 0)


def _run_copies(cps):
    for cp in cps:
        cp.start()
    for cp in cps:
        cp.wait_recv()
    for cp in cps:
        cp.wait_send()


def _sibling_halves(gsends, name, sequencer=None):
    def body(g_refs, out_refs, send_sems, recv_sems):
        x, y, c = _place()
        cps = []
        for g, (g_ref, out_ref) in enumerate(zip(g_refs, out_refs)):
            half = g_ref.shape[1] // 2
            cps.append(pltpu.make_async_remote_copy(
                src_ref=g_ref.at[:, pl.ds((1 - c) * half, half), :], dst_ref=out_ref,
                send_sem=send_sems.at[g], recv_sem=recv_sems.at[g], device_id=(x, y, 1 - c), device_id_type=MESH_ID))
        _run_copies(cps)

    out_types = [jax.ShapeDtypeStruct((s.shape[0], s.shape[1] // 2, s.shape[2]), s.dtype) for s in gsends]
    return _exchange(body, gsends, out_types, len(gsends), name, sequencer and (sequencer, _sibling))


def _chip_sums(gsend, sib, place):
    n, rows, cols = gsend.shape
    half = rows // 2
    tm = _half_tile(half)
    nblk = half // tm

    def body(s_ref, g_ref, sib_ref, o_ref):
        o_ref[0] = (g_ref[0].astype(F32) + sib_ref[0].astype(F32)).astype(o_ref.dtype)

    grid_spec = pltpu.PrefetchScalarGridSpec(
        num_scalar_prefetch=1, grid=(n, nblk),
        in_specs=[pl.BlockSpec((1, tm, cols), lambda k, i, s: (jnp.bitwise_xor(s[0], k), s[1] * nblk + i, 0)),
                  pl.BlockSpec((1, tm, cols), lambda k, i, s: (jnp.bitwise_xor(s[0], k), i, 0))],
        out_specs=pl.BlockSpec((1, tm, cols), lambda k, i, s: (k, i, 0)))
    return pl.pallas_call(
        body, out_shape=jax.ShapeDtypeStruct((n, half, cols), BF16), grid_spec=grid_spec,
        name="rs_chip_sums", compiler_params=_params("parallel", "parallel"))(place, gsend, sib)


def _exchange_chip_sums(tsends, name, sequencer=None):
    def body(t_refs, out_refs, send_sems, recv_sems):
        x, y, c = _place()
        cps = []
        for g, (t_ref, out_ref) in enumerate(zip(t_refs, out_refs)):
            for k, device in enumerate(_same_core_of_other_chips(x, y, c)):
                cps.append(pltpu.make_async_remote_copy(
                    src_ref=t_ref.at[k + 1], dst_ref=out_ref.at[k], send_sem=send_sems.at[3 * g + k],
                    recv_sem=recv_sems.at[3 * g + k], device_id=device, device_id_type=MESH_ID))
        _run_copies(cps)

    out_types = [jax.ShapeDtypeStruct((3,) + s.shape[1:], s.dtype) for s in tsends]
    return _exchange(body, tsends, out_types, 3 * len(tsends), name,
                     sequencer and (sequencer, _same_core_of_other_chips))


def _final_sum(tsend, recv, place):
    n, half, cols = tsend.shape
    tm = _half_tile(half)
    nblk = half // tm

    def body(s_ref, t_ref, r_ref, o_ref):
        o_ref[...] = ((t_ref[0].astype(F32) + r_ref[0].astype(F32)) + r_ref[1].astype(F32)) + r_ref[2].astype(F32)

    grid_spec = pltpu.PrefetchScalarGridSpec(
        num_scalar_prefetch=1, grid=(nblk,),
        in_specs=[pl.BlockSpec((1, tm, cols), lambda i, s: (0, i, 0)), pl.BlockSpec((n - 1, tm, cols), lambda i, s: (0, i, 0))],
        out_specs=pl.BlockSpec((tm, cols), lambda i, s: (s[1] * nblk + i, 0)))
    return pl.pallas_call(
        body, out_shape=jax.ShapeDtypeStruct((2 * half, cols), F32), grid_spec=grid_spec, name="rs_final_sum",
        compiler_params=_params("parallel"))(place, tsend, recv)


def _join_halves(gfulls, name, sequencer=None):
    def body(g_refs, out_refs, send_sems, recv_sems):
        x, y, c = _place()
        n = len(g_refs)

        def copy(g, pc):
            half = g_refs[g].shape[0] // 2
            return pltpu.make_async_remote_copy(
                src_ref=g_refs[g].at[pl.ds(pc * half, half), :], dst_ref=out_refs[g].at[pl.ds(pc * half, half), :],
                send_sem=send_sems.at[g], recv_sem=recv_sems.at[g], device_id=(x, y, 1 - c), device_id_type=MESH_ID)

        mine = [copy(g, c) for g in range(n)]
        for cp in mine:
            cp.start()
        for g in range(n):
            copy(g, 1 - c).wait_recv()
        for cp in mine:
            cp.wait_send()

    return _exchange(body, gfulls, None, len(gfulls), name, sequencer and (sequencer, _sibling))


def _allreduce_small(v):
    rows, cols = v.shape

    def body(v_ref, out_ref, buf, send_sems, recv_sems):
        x, y, c = _place()
        cps = []
        for k in range(1, 8):
            peer = (_flip(x, (k >> 2) & 1), _flip(y, (k >> 1) & 1), _flip(c, k & 1))
            cps.append(pltpu.make_async_remote_copy(
                src_ref=v_ref, dst_ref=buf.at[k - 1], send_sem=send_sems.at[k - 1], recv_sem=recv_sems.at[k - 1],
                device_id=peer, device_id_type=MESH_ID))
        for cp in cps:
            cp.start()
        for cp in cps:
            cp.wait_recv()
        for cp in cps:
            cp.wait_send()
        t0 = v_ref[...] + buf[0]
        t1 = buf[1] + buf[2]
        t2 = buf[3] + buf[4]
        t3 = buf[5] + buf[6]
        out_ref[...] = (t0 + t1) + (t2 + t3)

    vm = pl.BlockSpec(memory_space=pltpu.VMEM)
    return pl.pallas_call(
        body, out_shape=jax.ShapeDtypeStruct((rows, cols), F32), in_specs=[vm], out_specs=vm,
        scratch_shapes=[pltpu.VMEM((7, rows, cols), F32), pltpu.SemaphoreType.DMA((7,)), pltpu.SemaphoreType.DMA((7,))],
        name="allreduce_small")(v)


BIG_INFO = {n: (shape, ax) for n, shape, ax in BIG}
GROUPS = (("w_in",), ("w_ffn_gate", "w_ffn_up", "w_ffn_down", "w_out", "w_ple_gate"),
          ("w_branch_a", "w_branch_b", "w_ple_proj"))
GATHER_GROUPS = tuple((n,) for n in GROUPS[0] + GROUPS[1]) + GROUPS[2:]


def _shard_shape(name):
    (k, m), ax = BIG_INFO[name]
    return (k // N_CHIPS, m) if ax == 0 else (k, m // N_CHIPS)


def _group_rows(group):
    offs, off = {}, 0
    for n in group:
        offs[n] = off
        off += _shard_shape(n)[0]
    return offs, off


def _pack_groups(shards, layer, dtype):
    return [jnp.concatenate([shards[n][layer].astype(dtype) for n in group], axis=0) for group in GATHER_GROUPS]


def _unpack_full(gathered, groups):
    out = {}
    for group, arr in zip(groups, gathered):
        offs, _ = _group_rows(group)
        for n in group:
            rows, cols = _shard_shape(n)
            (k, m), ax = BIG_INFO[n]
            slab = arr[:, offs[n]:offs[n] + rows]
            out[n] = slab.reshape(k, m) if ax == 0 else jnp.transpose(slab, (1, 0, 2)).reshape(k, m)
    return out


def _pack_grads(gfull, groups):
    out = []
    for group in groups:
        parts = []
        for n in group:
            rows, cols = _shard_shape(n)
            ax = BIG_INFO[n][1]
            slab = (gfull[n].reshape(N_CHIPS, rows, cols) if ax == 0
                    else jnp.transpose(gfull[n].reshape(rows, N_CHIPS, cols), (1, 0, 2)))
            parts.append(slab)
        out.append(jnp.concatenate(parts, axis=1))
    return out


def _after(values, mark):
    values, _ = lax.optimization_barrier((values, mark))
    return values


def _reduce_scatter_finish(begun, place, tag, ids, hold):
    tsends, recvs = begun
    recvs = _after(recvs, hold)
    return _join_halves([_final_sum(t, r, place) for t, r in zip(tsends, recvs)], "rs_join_halves_" + tag, ids[2])


SMALL_SHAPES = {"rel_table": (NUM_BUCKETS, 2 * N_HEADS), "norm_mix_g": (DEPTH, D_MODEL), "qnorm_a_g": (DEPTH, HEAD_DIM),
                "knorm_a_g": (DEPTH, HEAD_DIM), "qnorm_b_g": (DEPTH, HEAD_DIM), "knorm_b_g": (DEPTH, HEAD_DIM),
                "sink_b": (DEPTH, N_HEADS), "norm_ffn_g": (DEPTH, D_MODEL), "norm_ple_g": (DEPTH, D_MODEL)}


def _pack_small(vals, last=None):
    flat = jnp.concatenate([vals[n].astype(F32).reshape(-1) for n in SMALL])
    tail = jnp.zeros((SMALL_ROWS * LANES - flat.shape[0],), F32)
    if last is not None:
        tail = tail.at[-1].set(last)
    return jnp.concatenate([flat, tail]).reshape(SMALL_ROWS, LANES)


def _unpack_small(packed):
    flat, out, off = packed.reshape(-1), {}, 0
    for n in SMALL:
        size = math.prod(SMALL_SHAPES[n])
        out[n] = flat[off:off + size].reshape(SMALL_SHAPES[n])
        off += size
    return out


def _adamw(w, gs, g_row, m, v, name):
    c1 = 1.0 - ADAM_B1 ** ADAM_STEP
    c2 = 1.0 - ADAM_B2 ** ADAM_STEP
    total, width = w.shape
    n_layers = len(gs)
    per = total // n_layers
    tm = max(t for t in range(8, 513, 8) if per % t == 0 and g_row % t == 0)
    nblk = per // tm

    def body(*refs):
        w_ref, g_refs = refs[0], refs[1:1 + n_layers]
        m_ref, v_ref, og, od, om, ov = refs[1 + n_layers:]
        layer = pl.program_id(0) // nblk
        g = g_refs[0][...]
        for l in range(1, n_layers):
            g = jnp.where(layer == l, g_refs[l][...], g)
        m_new = ADAM_B1 * m_ref[...] + (1.0 - ADAM_B1) * g
        v_new = ADAM_B2 * v_ref[...] + (1.0 - ADAM_B2) * (g * g)
        og[...] = g
        od[...] = -ADAM_LR * ((m_new / c1) / (jnp.sqrt(v_new / c2) + ADAM_EPS) + ADAM_WD * w_ref[...])
        om[...] = m_new
        ov[...] = v_new

    row = pl.BlockSpec((tm, width), lambda i: (i, 0))
    g_specs = [pl.BlockSpec((tm, width), lambda i, l=l: (g_row // tm + jnp.clip(i - l * nblk, 0, nblk - 1), 0))
               for l in range(n_layers)]
    return pl.pallas_call(
        body, out_shape=[jax.ShapeDtypeStruct((total, width), F32)] * 4, grid=(total // tm,),
        in_specs=[row] + g_specs + [row, row], out_specs=[row] * 4, name=name,
        compiler_params=_params("parallel"))(w, *gs, m, v)


def kernel(x, p, rel_table, norm_mix_g, w_in, qnorm_a_g, knorm_a_g, qnorm_b_g, knorm_b_g, sink_b, w_branch_a, w_branch_b, w_out, norm_ffn_g, w_ffn_gate, w_ffn_up, w_ffn_down, norm_ple_g, w_ple_gate, w_ple_proj, loss_target, m_rel_table, m_norm_mix_g, m_w_in, m_qnorm_a_g, m_knorm_a_g, m_qnorm_b_g, m_knorm_b_g, m_sink_b, m_w_branch_a, m_w_branch_b, m_w_out, m_norm_ffn_g, m_w_ffn_gate, m_w_ffn_up, m_w_ffn_down, m_norm_ple_g, m_w_ple_gate, m_w_ple_proj, v_rel_table, v_norm_mix_g, v_w_in, v_qnorm_a_g, v_knorm_a_g, v_qnorm_b_g, v_knorm_b_g, v_sink_b, v_w_branch_a, v_w_branch_b, v_w_out, v_norm_ffn_g, v_w_ffn_gate, v_w_ffn_up, v_w_ffn_down, v_norm_ple_g, v_w_ple_gate, v_w_ple_proj):
    given = dict(locals())

    def held(name, a):
        return jnp.swapaxes(a, 1, 2) if name in TRANSPOSED else a

    weights = {n: held(n, given[n]) for n in WEIGHTS}
    moments_m = {n: held(n, given["m_" + n]) for n in WEIGHTS}
    moments_v = {n: held(n, given["v_" + n]) for n in WEIGHTS}
    xi, yi, ci = _place()
    place = jnp.stack([2 * xi + yi, ci]).astype(jnp.int32)

    shards = [_pack_groups(weights, l, BF16) for l in range(DEPTH)]
    w_in_flight = [_allgather(shards[0][:1], "allgather_w_in_layer0", sequencer=9), None]
    rest_flight = [None, None]
    small = {n: weights[n] for n in SMALL}

    def w_in_of(l, mark):
        landed = _after(w_in_flight[l], (shards[1], mark) if l == 0 else mark)
        rest_flight[l] = _allgather(_after(shards[l][1:], landed), f"allgather_rest_layer{l}", sequencer=(1, 2)[l])
        return _unpack_full(landed, GATHER_GROUPS[:1])["w_in"]

    def rest_of(l, mark):
        if l + 1 < DEPTH:
            w_in_flight[l + 1] = _allgather(_after(shards[l + 1][:1], mark), f"allgather_w_in_layer{l + 1}", sequencer=16)
        return _unpack_full(_after(rest_flight[l], mark), GATHER_GROUPS[1:])

    stages = {"rest_layer1": (3, 4, 5), "w_in_layer1": (13, 14, 15), "rest_layer0": (6, 7, 8), "w_in_layer0": (10, 11, 12)}
    sent, begun = {}, {}

    def progress(l, point, value, grads):
        part, stage = point
        tag = f"{part}_layer{l}"
        if stage == "ready":
            gsends = _pack_grads(grads, GROUPS[:1] if part == "w_in" else GROUPS[1:])
            sent[tag] = gsends, _sibling_halves(gsends, "rs_sibling_halves_" + tag, stages[tag][0])
            return _after(value, gsends)
        tsends = [_chip_sums(g, s, place) for g, s in zip(*sent[tag])]
        begun[tag] = tsends, _exchange_chip_sums(tsends, "rs_exchange_" + tag, stages[tag][1])
        return _after(value, tsends)

    loss, dx, gbig, gsmall, marks = _local_step(x[0], p[:, 0], loss_target[0], w_in_of, rest_of, small, progress)

    def finish(tag, hold):
        return _reduce_scatter_finish(begun[tag], place, tag, stages[tag], hold)

    red1 = finish("w_in_layer1", marks[0]["attn_bwd_done"]) + finish("rest_layer1", marks[0]["attn_bwd_done"])
    rest0 = finish("rest_layer0", gbig[0]["w_in"])

    grads, delta, new_m, new_v = {}, {}, {}, {}

    def update(group, reduced):
        offs, _ = _group_rows(group)
        for n in group:
            shape = weights[n].shape
            two_d = lambda a: a.reshape(shape[0] * shape[1], shape[2])
            outs = _adamw(two_d(weights[n]), reduced, offs[n], two_d(moments_m[n]), two_d(moments_v[n]), "adamw_" + n)
            grads[n], delta[n], new_m[n], new_v[n] = (held(n, o.reshape(shape)) for o in outs)

    for gi in (1, 2):
        update(GROUPS[gi], _after([rest0[gi - 1], red1[gi]], begun["w_in_layer0"][0]))
    small_grads = _allreduce_small(_pack_small(gsmall, last=loss))
    g_, d_, m_, v_ = _adamw(_pack_small(weights), [small_grads], 0, _pack_small(moments_m), _pack_small(moments_v),
                            "adamw_small")
    grads.update(_unpack_small(g_))
    delta.update(_unpack_small(d_))
    new_m.update(_unpack_small(m_))
    new_v.update(_unpack_small(v_))
    others_done = [dx, d_] + [delta[n] for gi in (1, 2) for n in GROUPS[gi]]
    update(GROUPS[0], [finish("w_in_layer0", others_done)[0], red1[0]])

    return (small_grads[-1, -1], dx[None], *[grads[n] for n in WEIGHTS], *[delta[n] for n in WEIGHTS],
            *[new_m[n] for n in WEIGHTS], *[new_v[n] for n in WEIGHTS])
```

```python
import functools
import math

import jax
import jax.numpy as jnp
from jax import lax
from jax.experimental import pallas as pl
from jax.experimental.pallas import tpu as pltpu
from jax.experimental.pallas import tpu_sc as plsc

F32 = jnp.float32
BF16 = jnp.bfloat16
MESH_ID = pl.DeviceIdType.MESH

D_MODEL = 1024
DEPTH = 2
HEAD_DIM = 64
N_HEADS = 8
WIDTH = N_HEADS * HEAD_DIM
N_PAIRS = 4
ITEMS = 16
MAX_CHUNK = 1024
N_KV_B = 2
PLE_DIM = 256
D_FF = 2816
D_IN = 4352
OFF_QA, OFF_KA, OFF_VA, OFF_QB, OFF_KB, OFF_VB, OFF_GA, OFF_GB = 0, 512, 1024, 1536, 2048, 2176, 2304, 3328
DILATED = ((64, 1), (64, 4), (64, 16))
BLK_B = 128
NUM_BUCKETS = 32
MAX_DISTANCE = 1024
RMS_EPS = 1e-6
NEG_INF = -1e30
LANES = 128
ROW_TILE = 256
VMEM_LIMIT = 48 * 1024 * 1024

ADAM_LR, ADAM_B1, ADAM_B2, ADAM_EPS, ADAM_WD, ADAM_STEP = 0.001, 0.9, 0.999, 1e-08, 0.01, 10

TRANSPOSED = ("w_in", "w_ffn_gate", "w_ffn_up")
BIG = (
    ("w_in", (D_IN, D_MODEL), 0),
    ("w_branch_a", (WIDTH, D_MODEL), 1),
    ("w_branch_b", (WIDTH, D_MODEL), 1),
    ("w_out", (D_MODEL, D_MODEL), 0),
    ("w_ffn_gate", (D_FF, D_MODEL), 0),
    ("w_ffn_up", (D_FF, D_MODEL), 0),
    ("w_ffn_down", (D_FF, D_MODEL), 0),
    ("w_ple_gate", (D_MODEL, D_MODEL), 0),
    ("w_ple_proj", (PLE_DIM, D_MODEL), 1),
)
SMALL = ("rel_table", "norm_mix_g", "qnorm_a_g", "knorm_a_g", "qnorm_b_g", "knorm_b_g", "sink_b",
         "norm_ffn_g", "norm_ple_g")
WEIGHTS = ("rel_table", "norm_mix_g", "w_in", "qnorm_a_g", "knorm_a_g", "qnorm_b_g", "knorm_b_g", "sink_b",
           "w_branch_a", "w_branch_b", "w_out", "norm_ffn_g", "w_ffn_gate", "w_ffn_up", "w_ffn_down",
           "norm_ple_g", "w_ple_gate", "w_ple_proj")
N_CHIPS = 4
SMALL_ROWS = 64


def _params(*sem):
    return pltpu.CompilerParams(dimension_semantics=sem, vmem_limit_bytes=VMEM_LIMIT)


MM_VMEM_BUDGET = 40 * 1024 * 1024
STEP_OVERHEAD_S = 0.4e-6
TILE_DMA_BYTES_PER_S = 1.5e12


def _mm_dims(a, b, mode):
    if mode == "nn":
        return a.shape[0], b.shape[1], a.shape[1]
    if mode == "nt":
        return a.shape[0], b.shape[0], a.shape[1]
    return a.shape[1], b.shape[1], a.shape[0]


def _mm_tiles(m, n, pairs, tile_bytes, col_offsets, full_rows=False):
    best = None
    widths = [n] if full_rows else [t for t in range(LANES, n + 1, LANES) if n % t == 0 and all(o % t == 0 for o in col_offsets)]
    for tm in (t for t in range(LANES, m + 1, LANES) if m % t == 0):
        for tn in widths:
            io = sum(tm * k * ab + tn * k * bb for k, ab, bb in pairs) + tm * tn * sum(tile_bytes)
            casts = sum((tm * k * 2 if ab == 4 else 0) + (tn * k * 2 if bb == 4 else 0) for k, ab, bb in pairs)
            if 2 * io + len(pairs) * tm * tn * 4 + casts > MM_VMEM_BUDGET:
                continue
            cost = (m // tm) * (n // tn) * STEP_OVERHEAD_S + io / TILE_DMA_BYTES_PER_S
            if best is None or (cost, -tm) < best[0]:
                best = ((cost, -tm), tm, tn)
    return best[1], best[2]


def _mm_fused(pairs, extras, epilogue, out_dtypes, name, next_gain=None):
    m, n, _ = _mm_dims(*pairs[0])
    assert all(_mm_dims(*p)[:2] == (m, n) for p in pairs)
    with_norm = next_gain is not None
    out_dtypes = list(out_dtypes) + ([BF16] if with_norm else [])
    tm, tn = _mm_tiles(
        m, n, [(_mm_dims(a, b, mode)[2], a.dtype.itemsize, b.dtype.itemsize) for a, b, mode in pairs],
        [e.dtype.itemsize for e, _ in extras] + [jnp.dtype(d).itemsize for d in out_dtypes], [off for _, off in extras],
        full_rows=with_norm)
    dims = {"nn": (((1,), (0,)), ((), ())), "nt": (((1,), (1,)), ((), ())), "tn": (((0,), (0,)), ((), ()))}
    in_specs, args = [], []
    for a, b, mode in pairs:
        k = _mm_dims(a, b, mode)[2]
        in_specs.append(pl.BlockSpec((k, tm), lambda i, j: (0, i)) if mode == "tn" else pl.BlockSpec((tm, k), lambda i, j: (i, 0)))
        in_specs.append(pl.BlockSpec((tn, k), lambda i, j: (j, 0)) if mode == "nt" else pl.BlockSpec((k, tn), lambda i, j: (0, j)))
        args += [a, b]
    for e, off in extras:
        in_specs.append(pl.BlockSpec((tm, tn), lambda i, j, o=off // tn: (i, o + j)))
        args.append(e)
    n_pairs, n_tiles = len(pairs), 2 * len(pairs) + len(extras)
    if with_norm:
        in_specs.append(pl.BlockSpec((1, n), lambda i, j: (0, 0)))
        args.append(next_gain)
    n_in = len(args)

    def body(*refs):
        products = [lax.dot_general(refs[2 * p][...].astype(BF16), refs[2 * p + 1][...].astype(BF16), dims[pairs[p][2]],
                                    preferred_element_type=F32) for p in range(n_pairs)]
        outs = list(epilogue(products, [r[...] for r in refs[2 * n_pairs:n_tiles]]))
        if with_norm:
            y = outs[0]
            outs.append((y * lax.rsqrt(jnp.mean(y * y, axis=-1, keepdims=True) + RMS_EPS)) * refs[n_tiles][...])
        for r, o in zip(refs[n_in:], outs):
            r[...] = o.astype(r.dtype)

    tile = pl.BlockSpec((tm, tn), lambda i, j: (i, j))
    return pl.pallas_call(
        body, out_shape=[jax.ShapeDtypeStruct((m, n), d) for d in out_dtypes], grid=(m // tm, n // tn),
        in_specs=in_specs, out_specs=[tile] * len(out_dtypes), name=name,
        compiler_params=_params("parallel", "parallel"))(*args)


def _mm(a, b, mode, out_dtype, name, res=None):
    if res is None:
        return _mm_fused([(a, b, mode)], [], lambda products, extra: products, [out_dtype], name)[0]
    return _mm_fused([(a, b, mode)], [(res, 0)], lambda products, extra: [products[0] + extra[0]], [out_dtype], name)[0]


def _mm_res_norm(a, b, mode, res, gain, name):
    return _mm_fused([(a, b, mode)], [(res, 0)], lambda products, extra: [products[0] + extra[0]], [F32], name,
                     next_gain=gain)


def _ew(fn, ins, out_dtypes, name):
    rows, width = ins[0].shape
    n_in = len(ins)

    def body(*refs):
        outs = fn(*[r[...] for r in refs[:n_in]])
        for r, o in zip(refs[n_in:], outs):
            r[...] = o.astype(r.dtype)

    row = pl.BlockSpec((ROW_TILE, width), lambda i: (i, 0))
    return pl.pallas_call(
        body, out_shape=[jax.ShapeDtypeStruct((rows, width), dt) for dt in out_dtypes], grid=(rows // ROW_TILE,),
        in_specs=[row] * n_in, out_specs=[row] * len(out_dtypes), name=name, compiler_params=_params("parallel"))(*ins)


def _sigmoid(x):
    return 1.0 / (1.0 + jnp.exp(-x))


def _seg_sum(v):
    outs = []
    for k in range(v.shape[1] // LANES):
        vp = v[:, k * LANES:(k + 1) * LANES]
        left = lax.broadcasted_iota(jnp.int32, vp.shape, 1) < HEAD_DIM
        sl = jnp.sum(jnp.where(left, vp, 0.0), axis=-1, keepdims=True)
        sr = jnp.sum(jnp.where(left, 0.0, vp), axis=-1, keepdims=True)
        outs.append(jnp.where(left, sl, sr))
    return outs[0] if len(outs) == 1 else jnp.concatenate(outs, axis=1)


def _seg_rstd(x):
    return lax.rsqrt(_seg_sum(x * x) * (1.0 / HEAD_DIM) + RMS_EPS)


def _rms_fwd(x, g, name):
    rows, d = x.shape
    tm = ROW_TILE

    def body(x_ref, g_ref, h_ref):
        xv = x_ref[...]
        r = lax.rsqrt(jnp.mean(xv * xv, axis=-1, keepdims=True) + RMS_EPS)
        h_ref[...] = ((xv * r) * g_ref[...]).astype(BF16)

    return pl.pallas_call(
        body, out_shape=jax.ShapeDtypeStruct((rows, d), BF16), grid=(rows // tm,),
        in_specs=[pl.BlockSpec((tm, d), lambda i: (i, 0)), pl.BlockSpec((1, d), lambda i: (0, 0))],
        out_specs=pl.BlockSpec((tm, d), lambda i: (i, 0)), name=name,
        compiler_params=_params("parallel"))(x, g)


def _mm_rms_bwd(pairs, x, g, dres, name):
    rows, d = x.shape
    assert all(_mm_dims(*p)[:2] == (rows, d) for p in pairs)
    kbytes = [(_mm_dims(a, b, mode)[2], a.dtype.itemsize, b.dtype.itemsize) for a, b, mode in pairs]
    tm = max(t for t in (512, 256, 128) if rows % t == 0 and
             2 * (sum(t * k * ab + d * k * bb for k, ab, bb in kbytes) + t * d * 14) + 2 * t * d * 4 <= MM_VMEM_BUDGET)
    dims = {"nn": (((1,), (0,)), ((), ())), "nt": (((1,), (1,)), ((), ()))}
    n_pairs = len(pairs)

    def body(*refs):
        x_ref, g_ref, dres_ref = refs[2 * n_pairs:2 * n_pairs + 3]
        dx_ref, dxb_ref, dg_ref = refs[2 * n_pairs + 3:]
        dhv = functools.reduce(lambda u, v: u + v, [
            lax.dot_general(refs[2 * p][...].astype(BF16), refs[2 * p + 1][...].astype(BF16), dims[pairs[p][2]],
                            preferred_element_type=F32) for p in range(n_pairs)])
        xv = x_ref[...]
        r = lax.rsqrt(jnp.mean(xv * xv, axis=-1, keepdims=True) + RMS_EPS)
        xh = xv * r
        dxh = dhv * g_ref[...]
        dxv = dres_ref[...] + r * (dxh - xh * jnp.mean(dxh * xh, axis=-1, keepdims=True))
        dx_ref[...] = dxv
        dxb_ref[...] = dxv.astype(BF16)
        part = jnp.sum(dhv * xh, axis=0, keepdims=True)

        @pl.when(pl.program_id(0) == 0)
        def _():
            dg_ref[...] = part

        @pl.when(pl.program_id(0) > 0)
        def _():
            dg_ref[...] += part

    row = pl.BlockSpec((tm, d), lambda i: (i, 0))
    vec = pl.BlockSpec((1, d), lambda i: (0, 0))
    in_specs, args = [], []
    for a, b, mode in pairs:
        in_specs += [pl.BlockSpec((tm, a.shape[1]), lambda i: (i, 0)), pl.BlockSpec(b.shape, lambda i: (0, 0))]
        args += [a, b]
    return pl.pallas_call(
        body, out_shape=[jax.ShapeDtypeStruct((rows, d), F32), jax.ShapeDtypeStruct((rows, d), BF16),
                         jax.ShapeDtypeStruct((1, d), F32)],
        grid=(rows // tm,), in_specs=in_specs + [row, vec, row], out_specs=[row, row, vec],
        name=name, compiler_params=_params("arbitrary"))(*args, x, g, dres)


def _ple_bwd(dx, z, e):
    s = _sigmoid(z.astype(F32))
    return dx * s, dx * e.astype(F32) * (s * (1.0 - s))


def _loss_grad(y, t, z, e):
    rows, d = y.shape
    tm = ROW_TILE

    def body(y_ref, t_ref, z_ref, e_ref, dy_ref, de_ref, dz_ref, l_ref):
        err = y_ref[...] - t_ref[...]
        dy = err * (1.0 / d)
        dy_ref[...] = dy
        de, dz = _ple_bwd(dy, z_ref[...], e_ref[...])
        de_ref[...] = de.astype(BF16)
        dz_ref[...] = dz.astype(BF16)
        part = jnp.zeros((1, LANES), F32) + jnp.sum(err * err) * (0.5 / d)

        @pl.when(pl.program_id(0) == 0)
        def _():
            l_ref[...] = part

        @pl.when(pl.program_id(0) > 0)
        def _():
            l_ref[...] += part

    row = pl.BlockSpec((tm, d), lambda i: (i, 0))
    return pl.pallas_call(
        body, out_shape=[jax.ShapeDtypeStruct((rows, d), F32), jax.ShapeDtypeStruct((rows, d), BF16),
                         jax.ShapeDtypeStruct((rows, d), BF16), jax.ShapeDtypeStruct((1, LANES), F32)],
        grid=(rows // tm,), in_specs=[row] * 4, out_specs=[row, row, row, pl.BlockSpec((1, LANES), lambda i: (0, 0))],
        name="loss_grad", compiler_params=_params("arbitrary"))(y, t, z, e)


def _swap_halves(v):
    return pltpu.roll(v, HEAD_DIM, axis=1)


def _expand_kv(kv):
    left = lax.broadcasted_iota(jnp.int32, kv.shape, 1) < HEAD_DIM
    sw = _swap_halves(kv)
    h0 = jnp.where(left, kv, sw)
    h1 = jnp.where(left, sw, kv)
    return jnp.concatenate([h0, h0, h1, h1], axis=1)


def _reduce_kv(dkv):
    left = lax.broadcasted_iota(jnp.int32, (dkv.shape[0], LANES), 1) < HEAD_DIM
    t = dkv[:, 0:LANES] + dkv[:, LANES:2 * LANES]
    u = dkv[:, 2 * LANES:3 * LANES] + dkv[:, 3 * LANES:4 * LANES]
    t = t + _swap_halves(t)
    u = u + _swap_halves(u)
    return jnp.where(left, t, u)


def _qknorm_fwd(proj, gqa, gka, gqb, gkb):
    rows = proj.shape[0]
    tm = ROW_TILE

    def body(qa_ref, ka_ref, qb_ref, kb_ref, vb_ref, gqa_ref, gka_ref, gqb_ref, gkb_ref, oqa, oka, oqb, okb, ovb):
        for src, g_ref, dst in ((qa_ref, gqa_ref, oqa), (ka_ref, gka_ref, oka), (qb_ref, gqb_ref, oqb)):
            xv = src[...]
            dst[...] = (xv * _seg_rstd(xv)) * g_ref[...]
        kv = kb_ref[...]
        okb[...] = _expand_kv((kv * _seg_rstd(kv)) * gkb_ref[...])
        ovb[...] = _expand_kv(vb_ref[...])

    def win(width, off):
        return pl.BlockSpec((tm, width), lambda i: (i, off // width))

    vec = lambda w: pl.BlockSpec((1, w), lambda i: (0, 0))
    out = pl.BlockSpec((tm, WIDTH), lambda i: (i, 0))
    return pl.pallas_call(
        body, out_shape=[jax.ShapeDtypeStruct((rows, WIDTH), F32)] * 5, grid=(rows // tm,),
        in_specs=[win(WIDTH, OFF_QA), win(WIDTH, OFF_KA), win(WIDTH, OFF_QB), win(LANES, OFF_KB), win(LANES, OFF_VB),
                  vec(WIDTH), vec(WIDTH), vec(WIDTH), vec(LANES)],
        out_specs=[out] * 5, name="qknorm_fwd", compiler_params=_params("parallel"))(
            proj, proj, proj, proj, proj, gqa, gka, gqb, gkb)


def _norm_bwd(xv, g, dy):
    r = _seg_rstd(xv)
    xh = xv * r
    dxh = dy * g
    dx = r * (dxh - xh * (_seg_sum(dxh * xh) * (1.0 / HEAD_DIM)))
    return dx, jnp.sum(dy * xh, axis=0, keepdims=True)


def _qknorm_bwd(proj, gqa, gka, gqb, gkb, dqa, dka, dva, dqb, dkb, dvb, dga, dgb):
    rows = proj.shape[0]
    tm = ROW_TILE
    n_a = len(dqa)

    def body(*refs):
        qa_ref, ka_ref, qb_ref, kb_ref, gqa_ref, gka_ref, gqb_ref, gkb_ref = refs[:8]
        pos = 8
        dqa_refs, dka_refs, dva_refs = refs[pos:pos + n_a], refs[pos + n_a:pos + 2 * n_a], refs[pos + 2 * n_a:pos + 3 * n_a]
        pos += 3 * n_a
        dqb_ref, dkb_ref, dvb_ref, dga_ref, dgb_ref = refs[pos:pos + 5]
        dproj_ref, ogqa, ogka, ogqb, ogkb = refs[pos + 5:]

        def total(rs):
            acc = rs[0][...]
            for r in rs[1:]:
                acc = acc + r[...]
            return acc

        dx_qa, p_qa = _norm_bwd(qa_ref[...], gqa_ref[...], total(dqa_refs))
        dx_ka, p_ka = _norm_bwd(ka_ref[...], gka_ref[...], total(dka_refs))
        dx_qb, p_qb = _norm_bwd(qb_ref[...], gqb_ref[...], dqb_ref[...])
        dx_kb, p_kb = _norm_bwd(kb_ref[...], gkb_ref[...], _reduce_kv(dkb_ref[...]))
        dproj_ref[:, OFF_QA:OFF_QA + WIDTH] = dx_qa.astype(BF16)
        dproj_ref[:, OFF_KA:OFF_KA + WIDTH] = dx_ka.astype(BF16)
        dproj_ref[:, OFF_VA:OFF_VA + WIDTH] = total(dva_refs).astype(BF16)
        dproj_ref[:, OFF_QB:OFF_QB + WIDTH] = dx_qb.astype(BF16)
        dproj_ref[:, OFF_KB:OFF_KB + LANES] = dx_kb.astype(BF16)
        dproj_ref[:, OFF_VB:OFF_VB + LANES] = _reduce_kv(dvb_ref[...]).astype(BF16)
        dproj_ref[:, OFF_GA:OFF_GB] = dga_ref[...]
        dproj_ref[:, OFF_GB:D_IN] = dgb_ref[...]
        first = pl.program_id(0) == 0
        for o_ref, part in ((ogqa, p_qa), (ogka, p_ka), (ogqb, p_qb), (ogkb, p_kb)):
            @pl.when(first)
            def _(o_ref=o_ref, part=part):
                o_ref[...] = part

            @pl.when(jnp.logical_not(first))
            def _(o_ref=o_ref, part=part):
                o_ref[...] += part

    def win(width, off):
        return pl.BlockSpec((tm, width), lambda i: (i, off // width))

    vec = lambda w: pl.BlockSpec((1, w), lambda i: (0, 0))
    row = lambda w: pl.BlockSpec((tm, w), lambda i: (i, 0))
    in_specs = [win(WIDTH, OFF_QA), win(WIDTH, OFF_KA), win(WIDTH, OFF_QB), win(LANES, OFF_KB),
                vec(WIDTH), vec(WIDTH), vec(WIDTH), vec(LANES)]
    in_specs += [row(WIDTH)] * (3 * n_a + 3) + [row(D_MODEL)] * 2
    return pl.pallas_call(
        body,
        out_shape=[jax.ShapeDtypeStruct((rows, D_IN), BF16), jax.ShapeDtypeStruct((1, WIDTH), F32),
                   jax.ShapeDtypeStruct((1, WIDTH), F32), jax.ShapeDtypeStruct((1, WIDTH), F32),
                   jax.ShapeDtypeStruct((1, LANES), F32)],
        grid=(rows // tm,), in_specs=in_specs,
        out_specs=[row(D_IN), vec(WIDTH), vec(WIDTH), vec(WIDTH), vec(LANES)],
        name="qknorm_bwd", compiler_params=_params("arbitrary"))(
            proj, proj, proj, proj, gqa, gka, gqb, gkb, *dqa, *dka, *dva, dqb, dkb, dvb, dga, dgb)


def _t5_bucket(rel):
    half_b = NUM_BUCKETS // 2
    max_exact = half_b // 2
    sign = jnp.where(rel > 0, half_b, 0)
    n = jnp.abs(rel)
    nf = jnp.maximum(n, 1).astype(F32)
    large = max_exact + (jnp.log(nf / max_exact) / math.log(MAX_DISTANCE / max_exact)
                         * (half_b - max_exact)).astype(jnp.int32)
    large = jnp.minimum(large, half_b - 1)
    return sign + jnp.where(n < max_exact, n, large)


def _band_buckets(blk, dilation):
    i = jnp.arange(blk, dtype=jnp.int32)[:, None]
    j = jnp.arange(3 * blk, dtype=jnp.int32)[None, :]
    rel = j - blk - i
    return jnp.where(jnp.abs(rel) <= blk, _t5_bucket(rel * dilation), -1)


def _bias_tiles(table, buckets, head_off, name):
    blk = buckets.shape[0]

    def body(tab_ref, bk_ref, o_ref):
        h = pl.program_id(0) + head_off
        bk = bk_ref[...]
        acc = jnp.full(bk.shape, NEG_INF, F32)
        for b in range(NUM_BUCKETS):
            acc = jnp.where(bk == b, tab_ref[b, h], acc)
        o_ref[0] = acc

    return pl.pallas_call(
        body, out_shape=jax.ShapeDtypeStruct((N_HEADS, blk, 3 * blk), F32), grid=(N_HEADS,),
        in_specs=[pl.BlockSpec(memory_space=pltpu.SMEM), pl.BlockSpec((blk, 3 * blk), lambda h: (0, 0))],
        out_specs=pl.BlockSpec((1, blk, 3 * blk), lambda h: (h, 0, 0)),
        name=name, compiler_params=_params("parallel"))(table, buckets)


def _table_grad(dbias, buckets, name):
    blk = buckets.shape[0]

    def body(db_ref, bk_ref, o_ref):
        bk = bk_ref[...]
        dbv = db_ref[0]
        lane = lax.broadcasted_iota(jnp.int32, (1, LANES), 1)
        acc = jnp.zeros((1, LANES), F32)
        for b in range(NUM_BUCKETS):
            acc = jnp.where(lane == b, jnp.sum(jnp.where(bk == b, dbv, 0.0)), acc)
        o_ref[0] = acc

    out = pl.pallas_call(
        body, out_shape=jax.ShapeDtypeStruct((N_HEADS, 1, LANES), F32), grid=(N_HEADS,),
        in_specs=[pl.BlockSpec((1, blk, 3 * blk), lambda h: (h, 0, 0)), pl.BlockSpec((blk, 3 * blk), lambda h: (0, 0))],
        out_specs=pl.BlockSpec((1, 1, LANES), lambda h: (h, 0, 0)),
        name=name, compiler_params=_params("parallel"))(dbias, buckets)
    return out[:, 0, :NUM_BUCKETS]


def _dot_nt(a, b):
    return lax.dot_general(a, b, (((1,), (1,)), ((), ())), preferred_element_type=F32)


def _dot_tn(a, b):
    return lax.dot_general(a, b, (((0,), (0,)), ((), ())), preferred_element_type=F32)


def _stack_pair(x2, left):
    return jnp.concatenate([jnp.where(left, x2, 0.0), jnp.where(left, 0.0, x2)], axis=0).astype(BF16)


def _attn_geometry(blk, d):
    halo = blk * d
    subs = max(1, min(ITEMS // d, MAX_CHUNK // halo))
    return subs, min(d, ITEMS // subs), halo


def _item_of(j, r0, per_group):
    return j // per_group, r0 + j % per_group


def _span_rows(ref, first, r, blk, d):
    if d == 1:
        return ref[first:first + blk, :]
    return ref[pl.ds(first + r, blk, stride=d), :]


def _set_span_rows(ref, first, r, blk, d, val, add=False):
    idx = slice(first, first + blk) if d == 1 else pl.ds(first + r, blk, stride=d)
    ref[idx, :] = ref[idx, :] + val if add else val


def _for_groups(group, groups, per_group):
    if groups == 1:
        group(0)
    else:
        def step(g, carry):
            group(g * per_group)
            return carry

        lax.fori_loop(0, groups, step, 0)


def _key_rows(p_ref, c_ref, n_ref, s, r, subs, halo, blk, d):
    parts = []
    for span in (s - 1, s, s + 1):
        if span < 0:
            parts.append(_span_rows(p_ref, 0, r, blk, d))
        elif span == subs:
            parts.append(_span_rows(n_ref, 0, r, blk, d))
        else:
            parts.append(_span_rows(c_ref, span * halo, r, blk, d))
    return jnp.concatenate(parts, axis=0)


def _item_penalty(t, nct, s, subs, blk):
    first_ok = True if s > 0 else t > 0
    last_ok = True if s < subs - 1 else t < nct - 1
    col = lax.broadcasted_iota(jnp.int32, (1, 3 * blk), 1)
    ok = jnp.logical_and(jnp.logical_or(col >= blk, first_ok), jnp.logical_or(col < 2 * blk, last_ok))
    return jnp.where(ok, 0.0, NEG_INF).astype(F32)


def _attn_specs(seq, blk, d, step_of, col=0):
    subs, _, halo = _attn_geometry(blk, d)
    last, first = seq // halo - 1, col // LANES
    cur = pl.BlockSpec((subs * halo, LANES), lambda hp, t: (step_of(t), first + hp))
    prev = pl.BlockSpec((halo, LANES), lambda hp, t: (jnp.clip(step_of(t) * subs - 1, 0, last), first + hp))
    nxt = pl.BlockSpec((halo, LANES), lambda hp, t: (jnp.minimum((step_of(t) + 1) * subs, last), first + hp))
    return cur, prev, nxt


def _attn_fwd(q, k, v, bias, sink, blk, d, name, v_col=0):
    seq = q.shape[0]
    subs, per_group, halo = _attn_geometry(blk, d)
    items, chunk, groups = subs * per_group, subs * halo, d // per_group
    nct = seq // chunk
    has_sink = sink is not None
    scale = HEAD_DIM ** -0.5

    def body(*refs):
        q_ref, kp, kc, kn, vp, vc, vn, b_ref = refs[:8]
        s_ref = refs[8] if has_sink else None
        o_ref, l_ref = refs[-2], refs[-1]
        t = pl.program_id(1)
        left = lax.broadcasted_iota(jnp.int32, (1, LANES), 1) < HEAD_DIM
        bias2 = b_ref[...]

        def group(r0):
            scores, vcats = [], []
            for j in range(items):
                s, r = _item_of(j, r0, per_group)
                qs = _stack_pair(_span_rows(q_ref, s * halo, r, blk, d) * scale, left)
                kcat = _key_rows(kp, kc, kn, s, r, subs, halo, blk, d).astype(BF16)
                scores.append(_dot_nt(qs, kcat) + bias2 + _item_penalty(t, nct, s, subs, blk))
                vcats.append(_key_rows(vp, vc, vn, s, r, subs, halo, blk, d).astype(BF16))
            ms = [jnp.max(s, axis=-1, keepdims=True) for s in scores]
            if has_sink:
                sk = s_ref[...]
                ms = [jnp.maximum(m, sk) for m in ms]
            ps = [jnp.exp(s - m) for s, m in zip(scores, ms)]
            dens = [jnp.sum(p, axis=-1, keepdims=True) for p in ps]
            if has_sink:
                dens = [den + jnp.exp(sk - m) for den, m in zip(dens, ms)]
            pns = [(p * (1.0 / den)).astype(BF16) for p, den in zip(ps, dens)]
            lses = [m + jnp.log(den) for m, den in zip(ms, dens)]
            for j in range(items):
                s, r = _item_of(j, r0, per_group)
                o2 = jnp.dot(pns[j], vcats[j], preferred_element_type=F32)
                _set_span_rows(o_ref, s * halo, r, blk, d, jnp.where(left, o2[:blk], o2[blk:]))
                _set_span_rows(l_ref, s * halo, r, blk, d, jnp.where(left, lses[j][:blk], lses[j][blk:]))

        _for_groups(group, groups, per_group)

    cur, prev, nxt = _attn_specs(seq, blk, d, lambda t: t)
    v_cur, v_prev, v_nxt = _attn_specs(seq, blk, d, lambda t: t, v_col)
    in_specs = [cur, prev, cur, nxt, v_prev, v_cur, v_nxt, pl.BlockSpec((2 * blk, 3 * blk), lambda hp, t: (hp, 0))]
    args = [q, k, k, k, v, v, v, bias]
    if has_sink:
        in_specs.append(pl.BlockSpec((2 * blk, 1), lambda hp, t: (hp, 0)))
        args.append(sink)
    return pl.pallas_call(
        body, out_shape=[jax.ShapeDtypeStruct((seq, WIDTH), F32)] * 2, grid=(N_PAIRS, nct),
        in_specs=in_specs, out_specs=[cur, cur], name=name,
        compiler_params=_params("parallel", "parallel"))(*args)


def _attn_bwd(q, k, v, do, lse, delta, bias, sink, blk, d, name, v_col=0):
    seq = q.shape[0]
    subs, per_group, halo = _attn_geometry(blk, d)
    items, chunk, groups = subs * per_group, subs * halo, d // per_group
    nct = seq // chunk
    has_sink = sink is not None
    n_in = 12 if has_sink else 11
    scale = HEAD_DIM ** -0.5

    def body(*refs):
        q_ref, kp, kc, kn, vp, vc, vn, do_ref, l_ref, d_ref, b_ref = refs[:11]
        s_ref = refs[11] if has_sink else None
        dq_ref, dk_ref, dv_ref, db_ref = refs[n_in:n_in + 4]
        ds_ref = refs[n_in + 4] if has_sink else None
        wk, wv = refs[-2], refs[-1]
        t = pl.program_id(1)

        @pl.when(t == 0)
        def _():
            wk[...] = jnp.zeros_like(wk)
            wv[...] = jnp.zeros_like(wv)
            db_ref[...] = jnp.zeros_like(db_ref)
            if has_sink:
                ds_ref[...] = jnp.zeros_like(ds_ref)

        @pl.when(t > 0)
        def _():
            for w in (wk, wv):
                keep = w[chunk:2 * chunk + halo]
                w[0:chunk + halo] = keep
                w[chunk + halo:2 * chunk + halo] = jnp.zeros((chunk, LANES), F32)

        @pl.when(t < nct)
        def _():
            lane = lax.broadcasted_iota(jnp.int32, (1, LANES), 1)
            left = lane < HEAD_DIM
            bias2 = b_ref[...]

            def group(r0):
                qss, doss, kcats, scores, dps, lcols, dcols = [], [], [], [], [], [], []
                for j in range(items):
                    s, r = _item_of(j, r0, per_group)
                    qs = _stack_pair(_span_rows(q_ref, s * halo, r, blk, d) * scale, left)
                    dos = _stack_pair(_span_rows(do_ref, s * halo, r, blk, d), left)
                    kcat = _key_rows(kp, kc, kn, s, r, subs, halo, blk, d).astype(BF16)
                    vcat = _key_rows(vp, vc, vn, s, r, subs, halo, blk, d).astype(BF16)
                    l2, d2 = _span_rows(l_ref, s * halo, r, blk, d), _span_rows(d_ref, s * halo, r, blk, d)
                    lcols.append(jnp.concatenate([jnp.max(jnp.where(left, l2, NEG_INF), axis=-1, keepdims=True),
                                                  jnp.max(jnp.where(left, NEG_INF, l2), axis=-1, keepdims=True)], axis=0))
                    dcols.append(jnp.concatenate([jnp.sum(jnp.where(lane == 0, d2, 0.0), axis=-1, keepdims=True),
                                                  jnp.sum(jnp.where(lane == HEAD_DIM, d2, 0.0), axis=-1, keepdims=True)],
                                                 axis=0))
                    scores.append(_dot_nt(qs, kcat) + bias2 + _item_penalty(t, nct, s, subs, blk))
                    dps.append(_dot_nt(dos, vcat))
                    qss.append(qs)
                    doss.append(dos)
                    kcats.append(kcat)
                ps = [jnp.exp(s - lc) for s, lc in zip(scores, lcols)]
                dss = [p * (dp - dc) for p, dp, dc in zip(ps, dps, dcols)]
                db_ref[...] += functools.reduce(lambda a, b: a + b, dss)
                if has_sink:
                    sk = s_ref[...]
                    ds_ref[...] -= functools.reduce(lambda a, b: a + b, [dc * jnp.exp(sk - lc) for dc, lc in zip(dcols, lcols)])
                dsbs = [ds.astype(BF16) for ds in dss]
                for j in range(items):
                    s, r = _item_of(j, r0, per_group)
                    dq2 = jnp.dot(dsbs[j], kcats[j], preferred_element_type=F32) * scale
                    _set_span_rows(dq_ref, s * halo, r, blk, d, jnp.where(left, dq2[:blk], dq2[blk:]))
                news = [(_dot_tn(dsbs[j], qss[j]), _dot_tn(ps[j].astype(BF16), doss[j])) for j in range(items)]
                for which, w in enumerate((wk, wv)):
                    for jr in range(per_group):
                        for sp in range(-1, subs + 1):
                            parts = [news[s * per_group + jr][which][(sp - s + 1) * blk:(sp - s + 2) * blk]
                                     for s in range(subs) if 0 <= sp - s + 1 < 3]
                            _set_span_rows(w, chunk + sp * halo, r0 + jr, blk, d,
                                           functools.reduce(lambda x, y: x + y, parts), add=True)

            _for_groups(group, groups, per_group)

        dk_ref[...] = wk[0:chunk]
        dv_ref[...] = wv[0:chunk]

    cur, prev, nxt = _attn_specs(seq, blk, d, lambda t: jnp.minimum(t, nct - 1))
    v_cur, v_prev, v_nxt = _attn_specs(seq, blk, d, lambda t: jnp.minimum(t, nct - 1), v_col)
    lag = pl.BlockSpec((chunk, LANES), lambda hp, t: (jnp.maximum(t - 1, 0), hp))
    band = pl.BlockSpec((2 * blk, 3 * blk), lambda hp, t: (hp, 0))
    col = pl.BlockSpec((2 * blk, 1), lambda hp, t: (hp, 0))
    in_specs = [cur, prev, cur, nxt, v_prev, v_cur, v_nxt, cur, cur, cur, band]
    args = [q, k, k, k, v, v, v, do, lse, delta, bias]
    out_shape = [jax.ShapeDtypeStruct((seq, WIDTH), F32)] * 3 + [jax.ShapeDtypeStruct((N_HEADS * blk, 3 * blk), F32)]
    out_specs = [cur, lag, lag, band]
    if has_sink:
        in_specs.append(col)
        args.append(sink)
        out_shape.append(jax.ShapeDtypeStruct((N_HEADS * blk, 1), F32))
        out_specs.append(col)
    window = pltpu.VMEM((2 * chunk + halo, LANES), F32)
    return pl.pallas_call(
        body, out_shape=out_shape, grid=(N_PAIRS, nct + 1), in_specs=in_specs, out_specs=out_specs,
        scratch_shapes=[window, window], name=name,
        compiler_params=_params("arbitrary", "arbitrary"))(*args)


def _combine_patterns(outs, lses):
    def combine(*tiles):
        os_, ls = tiles[:len(outs)], tiles[len(outs):]
        m = functools.reduce(jnp.maximum, ls)
        es = [jnp.exp(l - m) for l in ls]
        den = functools.reduce(lambda a, b: a + b, es)
        num = functools.reduce(lambda a, b: a + b, [e * o for e, o in zip(es, os_)])
        return num / den, m + jnp.log(den)

    return _ew(combine, [*outs, *lses], [F32, F32], "combine_a")


def _tile_gain(g, reps):
    return jnp.tile(g[None, :], (1, reps))


def _local_step(x, p, target, w_in_of, rest_of, small):
    rel_table = small["rel_table"]
    buckets_a = [_band_buckets(blk, d) for blk, d in DILATED]
    buckets_b = _band_buckets(BLK_B, 1)
    bias_a = [_bias_tiles(rel_table, bk, 0, "bias_a").reshape(N_HEADS * bk.shape[0], -1) for bk in buckets_a]
    bias_b = _bias_tiles(rel_table, buckets_b, N_HEADS, "bias_b").reshape(N_HEADS * BLK_B, -1)

    saved = []
    for l in range(DEPTH):
        g_mix, g_ffn, g_ple = (small[n][l][None, :] for n in ("norm_mix_g", "norm_ffn_g", "norm_ple_g"))
        gqa, gka, gqb = (_tile_gain(small[n][l], N_HEADS) for n in ("qnorm_a_g", "knorm_a_g", "qnorm_b_g"))
        gkb = _tile_gain(small["knorm_b_g"][l], N_KV_B)
        sink = jnp.repeat(small["sink_b"][l], BLK_B)[:, None]

        h = _rms_fwd(x, g_mix, "rms_mix") if l == 0 else h_next
        w_in = w_in_of(l, (h, bias_a, bias_b)) if l == 0 else w_in_next
        proj = _mm(h, w_in, "nt", F32, "mm_in")
        qa, ka, qb, kb, vb = _qknorm_fwd(proj, gqa, gka, gqb, gkb)
        outs, lses = [], []
        for (blk, d), bias in zip(DILATED, bias_a):
            o, ls = _attn_fwd(qa, ka, proj, bias, None, blk, d, f"attn_a{d}_fwd", v_col=OFF_VA)
            outs.append(o)
            lses.append(ls)
        ya, lse_a = _combine_patterns(outs, lses)
        yb, lse_b = _attn_fwd(qb, kb, vb, bias_b, sink, BLK_B, 1, "attn_b_fwd")
        w = dict(rest_of(l, yb), w_in=w_in)
        def gate(products, extra):
            (ca_, cb_), (ga_, gb_) = products, extra
            return _sigmoid(ga_) * ca_ + _sigmoid(gb_) * cb_, ca_, cb_

        merged, ca, cb = _mm_fused([(ya, w["w_branch_a"], "nn"), (yb, w["w_branch_b"], "nn")],
                                   [(proj, OFF_GA), (proj, OFF_GB)], gate, [BF16, BF16, BF16], "mm_branches_gate")
        x1, h2 = _mm_res_norm(merged, w["w_out"], "nn", x, g_ffn, "mm_out_norm")

        def swiglu(products, extra):
            a_, u_ = products
            return (a_ * _sigmoid(a_)) * u_, a_, u_

        hid, a, u = _mm_fused([(h2, w["w_ffn_gate"], "nt"), (h2, w["w_ffn_up"], "nt")], [], swiglu,
                              [BF16, BF16, BF16], "mm_ffn_gate_up")
        if l + 1 < DEPTH:
            w_in_next = w_in_of(l + 1, hid)
        x2, h3 = _mm_res_norm(hid, w["w_ffn_down"], "nn", x1, g_ple, "mm_ffn_down_norm")

        def ple(products, extra):
            z_, e_ = products
            return extra[0] + _sigmoid(z_) * e_, z_, e_

        next_gain = small["norm_mix_g"][l + 1][None, :] if l + 1 < DEPTH else None
        x3, z, e, *rest = _mm_fused([(h3, w["w_ple_gate"], "nn"), (p[l], w["w_ple_proj"], "nn")], [(x2, 0)], ple,
                                    [F32, BF16, BF16], "mm_ple", next_gain=next_gain)
        h_next = rest[0] if rest else None
        saved.append(dict(w=w, x0=x, h=h, proj=proj, qa=qa, ka=ka, qb=qb, kb=kb, vb=vb, ya=ya, lse_a=lse_a,
                          yb=yb, lse_b=lse_b, ca=ca, cb=cb, merged=merged, x1=x1, h2=h2, a=a, u=u, hid=hid,
                          x2=x2, h3=h3, z=z, e=e))
        x = x3

    dx, de, dz, loss_acc = _loss_grad(x, target, saved[-1]["z"], saved[-1]["e"])
    loss = loss_acc[0, 0]

    gbig = [{} for _ in range(DEPTH)]
    marks = [{} for _ in range(DEPTH)]
    gsmall = {n: [None] * DEPTH for n in SMALL if n != "rel_table"}
    dbias_a = [[] for _ in DILATED]
    dbias_b = []

    for l in reversed(range(DEPTH)):
        sv = saved[l]
        w = sv["w"]
        g_mix, g_ffn, g_ple = (small[n][l][None, :] for n in ("norm_mix_g", "norm_ffn_g", "norm_ple_g"))
        gqa, gka, gqb = (_tile_gain(small[n][l], N_HEADS) for n in ("qnorm_a_g", "knorm_a_g", "qnorm_b_g"))
        gkb = _tile_gain(small["knorm_b_g"][l], N_KV_B)
        sink = jnp.repeat(small["sink_b"][l], BLK_B)[:, None]

        if l < DEPTH - 1:
            de, dz = _ew(_ple_bwd, [dx, sv["z"], sv["e"]], [BF16, BF16], "ple_bwd")
        gbig[l]["w_ple_proj"] = _mm(p[l], de, "tn", BF16, "mm_d_ple_proj")
        gbig[l]["w_ple_gate"] = _mm(sv["h3"], dz, "tn", BF16, "mm_d_ple_gate")
        dx, dxb, gsmall["norm_ple_g"][l] = _mm_rms_bwd([(dz, w["w_ple_gate"], "nt")], sv["x2"], g_ple, dx,
                                                      "mm_dh3_rms_bwd")

        gbig[l]["w_ffn_down"] = _mm(sv["hid"], dxb, "tn", BF16, "mm_d_ffn_down")

        def swiglu_bwd(products, extra):
            dh_, a_, u_ = products[0], extra[0].astype(F32), extra[1].astype(F32)
            s = _sigmoid(a_)
            return dh_ * u_ * (s * (1.0 + a_ * (1.0 - s))), dh_ * (a_ * s)

        da, du = _mm_fused([(dxb, w["w_ffn_down"], "nt")], [(sv["a"], 0), (sv["u"], 0)], swiglu_bwd, [BF16, BF16],
                           "mm_dhid_swiglu_bwd")
        gbig[l]["w_ffn_gate"], gbig[l]["w_ffn_up"] = _mm_fused(
            [(da, sv["h2"], "tn"), (du, sv["h2"], "tn")], [], lambda products, extra: products, [BF16, BF16],
            "mm_d_ffn_gate_up")
        dx, dxb, gsmall["norm_ffn_g"][l] = _mm_rms_bwd([(da, w["w_ffn_gate"], "nn"), (du, w["w_ffn_up"], "nn")],
                                                      sv["x1"], g_ffn, dx, "mm_dh2_rms_bwd")

        gbig[l]["w_out"] = _mm(sv["merged"], dxb, "tn", BF16, "mm_d_out")

        def gate_bwd(products, extra):
            dm_, ca_, cb_ = products[0], extra[0].astype(F32), extra[1].astype(F32)
            sa, sb = _sigmoid(extra[2]), _sigmoid(extra[3])
            return dm_ * sa, dm_ * sb, dm_ * ca_ * (sa * (1.0 - sa)), dm_ * cb_ * (sb * (1.0 - sb))

        dca, dcb, dga, dgb = _mm_fused(
            [(dxb, w["w_out"], "nt")], [(sv["ca"], 0), (sv["cb"], 0), (sv["proj"], OFF_GA), (sv["proj"], OFF_GB)],
            gate_bwd, [BF16, BF16, BF16, BF16], "mm_dmerged_gate_bwd")
        gbig[l]["w_branch_a"], gbig[l]["w_branch_b"] = _mm_fused(
            [(sv["ya"], dca, "tn"), (sv["yb"], dcb, "tn")], [], lambda products, extra: products, [BF16, BF16],
            "mm_d_branches")

        def with_row_dots(products, extra):
            (dya_, dyb_), (ya_, yb_) = products, extra
            return dya_, _seg_sum(dya_ * ya_), dyb_, _seg_sum(dyb_ * yb_)

        dya, delta_a, dyb, delta_b = _mm_fused(
            [(dca, w["w_branch_a"], "nt"), (dcb, w["w_branch_b"], "nt")], [(sv["ya"], 0), (sv["yb"], 0)], with_row_dots,
            [F32, F32, F32, F32], "mm_dy_branches")

        dqa, dka, dva = [], [], []
        for (blk, d), bias, bk in zip(DILATED, bias_a, buckets_a):
            dq_, dk_, dv_, db_ = _attn_bwd(sv["qa"], sv["ka"], sv["proj"], dya, sv["lse_a"], delta_a, bias, None, blk, d,
                                           f"attn_a{d}_bwd", v_col=OFF_VA)
            dqa.append(dq_)
            dka.append(dk_)
            dva.append(dv_)
            dbias_a[len(dqa) - 1].append(db_)
        dqb, dkb, dvb, db_, dsink = _attn_bwd(sv["qb"], sv["kb"], sv["vb"], dyb, sv["lse_b"], delta_b, bias_b, sink,
                                              BLK_B, 1, "attn_b_bwd")
        dbias_b.append(db_)
        gsmall["sink_b"][l] = dsink.reshape(N_HEADS, BLK_B).sum(axis=1)

        dproj, pqa, pka, pqb, pkb = _qknorm_bwd(sv["proj"], gqa, gka, gqb, gkb, dqa, dka, dva, dqb, dkb, dvb, dga, dgb)
        marks[l]["attn_bwd_done"] = dproj
        gsmall["qnorm_a_g"][l] = pqa.reshape(N_HEADS, HEAD_DIM).sum(0)
        gsmall["knorm_a_g"][l] = pka.reshape(N_HEADS, HEAD_DIM).sum(0)
        gsmall["qnorm_b_g"][l] = pqb.reshape(N_HEADS, HEAD_DIM).sum(0)
        gsmall["knorm_b_g"][l] = pkb.reshape(N_KV_B, HEAD_DIM).sum(0)
        gbig[l]["w_in"] = _mm(dproj, sv["h"], "tn", BF16, "mm_d_in")
        dx, _, gsmall["norm_mix_g"][l] = _mm_rms_bwd([(dproj, w["w_in"], "nn")], sv["x0"], g_mix, dx, "mm_dh_rms_bwd")
        gsmall["norm_mix_g"][l] = gsmall["norm_mix_g"][l][0]
        gsmall["norm_ffn_g"][l] = gsmall["norm_ffn_g"][l][0]
        gsmall["norm_ple_g"][l] = gsmall["norm_ple_g"][l][0]

    gsmall = {n: jnp.stack(v) for n, v in gsmall.items()}
    dtable_a = sum(_table_grad(sum(dbs).reshape(N_HEADS, blk, 3 * blk), bk, "table_grad_a")
                   for dbs, (blk, _), bk in zip(dbias_a, DILATED, buckets_a))
    dtable_b = _table_grad(sum(dbias_b).reshape(N_HEADS, BLK_B, 3 * BLK_B), buckets_b, "table_grad_b")
    gsmall["rel_table"] = jnp.concatenate([dtable_a, dtable_b], axis=0).T
    return loss, dx, gbig, gsmall, marks


def _place():
    return lax.axis_index("x"), lax.axis_index("y"), lax.axis_index("c")


def _flip(v, bit):
    return 1 - v if bit else v


CHIP_RELATIONS = ((0, 1), (1, 0), (1, 1))
ANY = pl.BlockSpec(memory_space=pl.ANY)


def _allgather_body(w_refs, out_refs, send_sems, recv_sems):
    x, y, c = _place()
    chips = [(_flip(x, a), _flip(y, b)) for a, b in CHIP_RELATIONS]

    def make(g):
        w_ref, out_ref = w_refs[g], out_refs[g]
        half = w_ref.shape[0] // 2

        def part(px, py, pc):
            return out_ref.at[2 * px + py, pl.ds(pc * half, half), :]

        def copy(k, block, to, src=None):
            return pltpu.make_async_remote_copy(
                src_ref=part(*block) if src is None else src, dst_ref=part(*block),
                send_sem=send_sems.at[7 * g + k], recv_sem=recv_sems.at[7 * g + k], device_id=to,
                device_id_type=MESH_ID)

        own = pltpu.make_async_remote_copy(
            src_ref=w_ref, dst_ref=out_ref.at[2 * x + y], send_sem=send_sems.at[7 * g + 6],
            recv_sem=recv_sems.at[7 * g + 6], device_id=(x, y, 1 - c), device_id_type=MESH_ID)
        first = [copy(k, (x, y, c), (*chip, c), src=w_ref.at[pl.ds(c * half, half), :]) for k, chip in enumerate(chips)]
        passed = [copy(3 + k, (*chip, c), (x, y, 1 - c)) for k, chip in enumerate(chips)]
        arrive = [copy(k, (*chip, c), (x, y, c)) for k, chip in enumerate(chips)]
        arrive2 = [copy(3 + k, (*chip, 1 - c), (x, y, c)) for k, chip in enumerate(chips)]
        return own, first, passed, arrive, arrive2

    made = [make(g) for g in range(len(w_refs))]
    for own, first, _, _, _ in made:
        own.start()
        for cp in first:
            cp.start()
    for _, _, passed, arrive, _ in made:
        for k in range(3):
            arrive[k].wait_recv()
            passed[k].start()
    for own, first, passed, _, arrive2 in made:
        for k in range(3):
            arrive2[k].wait_recv()
        own.wait_recv()
        for cp in first + passed + [own]:
            cp.wait_send()


def _sibling(x, y, c):
    return [(x, y, 1 - c)]


def _same_core_of_other_chips(x, y, c):
    return [(_flip(x, a), _flip(y, b), c) for a, b in CHIP_RELATIONS]


def _exchange(body, ins, out_types, n_sems, name, sequencer=None):
    n = len(ins)
    sems = (pltpu.SemaphoreType.DMA((n_sems,)), pltpu.SemaphoreType.DMA((n_sems,)))
    if sequencer is None:
        in_place = out_types is None
        out_shape = [jax.ShapeDtypeStruct(a.shape, a.dtype) for a in ins] if in_place else out_types

        def tc_body(*refs):
            body(refs[:n], refs[n:n + len(out_shape)], refs[-2], refs[-1])

        return list(pl.pallas_call(
            tc_body, out_shape=out_shape, in_specs=[ANY] * n, out_specs=[ANY] * len(out_shape),
            input_output_aliases={g: g for g in range(n)} if in_place else {}, scratch_shapes=list(sems), name=name)(*ins))

    collective_id, peers = sequencer
    hbm = pltpu.MemorySpace.HBM
    in_refs = [jax.new_ref(a, memory_space=hbm) for a in ins]
    out_refs = in_refs if out_types is None else [jax.empty_ref(t, memory_space=hbm) for t in out_types]

    @pl.kernel(mesh=plsc.ScalarSubcoreMesh(axis_name="sequencer", num_cores=1), name=name, scratch_types=sems,
               compiler_params=pltpu.CompilerParams(collective_id=collective_id))
    def launch(send_sems, recv_sems):
        barrier = pltpu.get_barrier_semaphore()
        devices = peers(*_place())
        for device in devices:
            pl.semaphore_signal(barrier, inc=1, device_id=device, device_id_type=MESH_ID)
        pl.semaphore_wait(barrier, len(devices))
        body(in_refs, out_refs, send_sems, recv_sems)

    launch()
    return [r[...] for r in out_refs]


def _allgather(shards, name, sequencer=None):
    out_types = [jax.ShapeDtypeStruct((N_CHIPS,) + s.shape, s.dtype) for s in shards]
    if sequencer is not None:
        sequencer = (sequencer, lambda x, y, c: _sibling(x, y, c) + _same_core_of_other_chips(x, y, c))
    return _exchange(_allgather_body, shards, out_types, 7 * len(shards), name, sequencer)


def _half_tile(half):
    return max(t for t in range(16, 1025, 16) if half % t == 0)


def _run_copies(cps):
    for cp in cps:
        cp.start()
    for cp in cps:
        cp.wait_recv()
    for cp in cps:
        cp.wait_send()


def _sibling_halves(gsends, name, sequencer=None):
    def body(g_refs, out_refs, send_sems, recv_sems):
        x, y, c = _place()
        cps = []
        for g, (g_ref, out_ref) in enumerate(zip(g_refs, out_refs)):
            half = g_ref.shape[1] // 2
            cps.append(pltpu.make_async_remote_copy(
                src_ref=g_ref.at[:, pl.ds((1 - c) * half, half), :], dst_ref=out_ref,
                send_sem=send_sems.at[g], recv_sem=recv_sems.at[g], device_id=(x, y, 1 - c), device_id_type=MESH_ID))
        _run_copies(cps)

    out_types = [jax.ShapeDtypeStruct((s.shape[0], s.shape[1] // 2, s.shape[2]), s.dtype) for s in gsends]
    return _exchange(body, gsends, out_types, len(gsends), name, sequencer and (sequencer, _sibling))


def _chip_sums(gsend, sib, place):
    n, rows, cols = gsend.shape
    half = rows // 2
    tm = _half_tile(half)
    nblk = half // tm

    def body(s_ref, g_ref, sib_ref, o_ref):
        o_ref[0] = (g_ref[0].astype(F32) + sib_ref[0].astype(F32)).astype(o_ref.dtype)

    grid_spec = pltpu.PrefetchScalarGridSpec(
        num_scalar_prefetch=1, grid=(n, nblk),
        in_specs=[pl.BlockSpec((1, tm, cols), lambda k, i, s: (jnp.bitwise_xor(s[0], k), s[1] * nblk + i, 0)),
                  pl.BlockSpec((1, tm, cols), lambda k, i, s: (jnp.bitwise_xor(s[0], k), i, 0))],
        out_specs=pl.BlockSpec((1, tm, cols), lambda k, i, s: (k, i, 0)))
    return pl.pallas_call(
        body, out_shape=jax.ShapeDtypeStruct((n, half, cols), BF16), grid_spec=grid_spec,
        name="rs_chip_sums", compiler_params=_params("parallel", "parallel"))(place, gsend, sib)


def _exchange_chip_sums(tsends, name, sequencer=None):
    def body(t_refs, out_refs, send_sems, recv_sems):
        x, y, c = _place()
        cps = []
        for g, (t_ref, out_ref) in enumerate(zip(t_refs, out_refs)):
            for k, device in enumerate(_same_core_of_other_chips(x, y, c)):
                cps.append(pltpu.make_async_remote_copy(
                    src_ref=t_ref.at[k + 1], dst_ref=out_ref.at[k], send_sem=send_sems.at[3 * g + k],
                    recv_sem=recv_sems.at[3 * g + k], device_id=device, device_id_type=MESH_ID))
        _run_copies(cps)

    out_types = [jax.ShapeDtypeStruct((3,) + s.shape[1:], s.dtype) for s in tsends]
    return _exchange(body, tsends, out_types, 3 * len(tsends), name,
                     sequencer and (sequencer, _same_core_of_other_chips))


def _final_sum(tsend, recv, place):
    n, half, cols = tsend.shape
    tm = _half_tile(half)
    nblk = half // tm

    def body(s_ref, t_ref, r_ref, o_ref):
        o_ref[...] = ((t_ref[0].astype(F32) + r_ref[0].astype(F32)) + r_ref[1].astype(F32)) + r_ref[2].astype(F32)

    grid_spec = pltpu.PrefetchScalarGridSpec(
        num_scalar_prefetch=1, grid=(nblk,),
        in_specs=[pl.BlockSpec((1, tm, cols), lambda i, s: (0, i, 0)), pl.BlockSpec((n - 1, tm, cols), lambda i, s: (0, i, 0))],
        out_specs=pl.BlockSpec((tm, cols), lambda i, s: (s[1] * nblk + i, 0)))
    return pl.pallas_call(
        body, out_shape=jax.ShapeDtypeStruct((2 * half, cols), F32), grid_spec=grid_spec, name="rs_final_sum",
        compiler_params=_params("parallel"))(place, tsend, recv)


def _join_halves(gfulls, name, sequencer=None):
    def body(g_refs, out_refs, send_sems, recv_sems):
        x, y, c = _place()
        n = len(g_refs)

        def copy(g, pc):
            half = g_refs[g].shape[0] // 2
            return pltpu.make_async_remote_copy(
                src_ref=g_refs[g].at[pl.ds(pc * half, half), :], dst_ref=out_refs[g].at[pl.ds(pc * half, half), :],
                send_sem=send_sems.at[g], recv_sem=recv_sems.at[g], device_id=(x, y, 1 - c), device_id_type=MESH_ID)

        mine = [copy(g, c) for g in range(n)]
        for cp in mine:
            cp.start()
        for g in range(n):
            copy(g, 1 - c).wait_recv()
        for cp in mine:
            cp.wait_send()

    return _exchange(body, gfulls, None, len(gfulls), name, sequencer and (sequencer, _sibling))


def _allreduce_small(v):
    rows, cols = v.shape

    def body(v_ref, out_ref, buf, send_sems, recv_sems):
        x, y, c = _place()
        cps = []
        for k in range(1, 8):
            peer = (_flip(x, (k >> 2) & 1), _flip(y, (k >> 1) & 1), _flip(c, k & 1))
            cps.append(pltpu.make_async_remote_copy(
                src_ref=v_ref, dst_ref=buf.at[k - 1], send_sem=send_sems.at[k - 1], recv_sem=recv_sems.at[k - 1],
                device_id=peer, device_id_type=MESH_ID))
        for cp in cps:
            cp.start()
        for cp in cps:
            cp.wait_recv()
        for cp in cps:
            cp.wait_send()
        t0 = v_ref[...] + buf[0]
        t1 = buf[1] + buf[2]
        t2 = buf[3] + buf[4]
        t3 = buf[5] + buf[6]
        out_ref[...] = (t0 + t1) + (t2 + t3)

    vm = pl.BlockSpec(memory_space=pltpu.VMEM)
    return pl.pallas_call(
        body, out_shape=jax.ShapeDtypeStruct((rows, cols), F32), in_specs=[vm], out_specs=vm,
        scratch_shapes=[pltpu.VMEM((7, rows, cols), F32), pltpu.SemaphoreType.DMA((7,)), pltpu.SemaphoreType.DMA((7,))],
        name="allreduce_small")(v)


BIG_INFO = {n: (shape, ax) for n, shape, ax in BIG}
GROUPS = (("w_in",), ("w_ffn_gate", "w_ffn_up", "w_ffn_down", "w_out", "w_ple_gate"),
          ("w_branch_a", "w_branch_b", "w_ple_proj"))
GATHER_GROUPS = tuple((n,) for n in GROUPS[0] + GROUPS[1]) + GROUPS[2:]


def _shard_shape(name):
    (k, m), ax = BIG_INFO[name]
    return (k // N_CHIPS, m) if ax == 0 else (k, m // N_CHIPS)


def _group_rows(group):
    offs, off = {}, 0
    for n in group:
        offs[n] = off
        off += _shard_shape(n)[0]
    return offs, off


def _pack_groups(shards, layer, dtype):
    return [jnp.concatenate([shards[n][layer].astype(dtype) for n in group], axis=0) for group in GATHER_GROUPS]


def _unpack_full(gathered, groups):
    out = {}
    for group, arr in zip(groups, gathered):
        offs, _ = _group_rows(group)
        for n in group:
            rows, cols = _shard_shape(n)
            (k, m), ax = BIG_INFO[n]
            slab = arr[:, offs[n]:offs[n] + rows]
            out[n] = slab.reshape(k, m) if ax == 0 else jnp.transpose(slab, (1, 0, 2)).reshape(k, m)
    return out


def _pack_grads(gfull):
    out = []
    for group in GROUPS:
        parts = []
        for n in group:
            rows, cols = _shard_shape(n)
            ax = BIG_INFO[n][1]
            slab = (gfull[n].reshape(N_CHIPS, rows, cols) if ax == 0
                    else jnp.transpose(gfull[n].reshape(rows, N_CHIPS, cols), (1, 0, 2)))
            parts.append(slab)
        out.append(jnp.concatenate(parts, axis=1))
    return out


def _after(values, mark):
    values, _ = lax.optimization_barrier((values, mark))
    return values


def _reduce_scatter_begin(gsends, place, tag, ids):
    sibs = _sibling_halves(gsends, "rs_sibling_halves_" + tag, ids[0])
    tsends = [_chip_sums(g, s, place) for g, s in zip(gsends, sibs)]
    return tsends, _exchange_chip_sums(tsends, "rs_exchange_" + tag, ids[1])


def _reduce_scatter_finish(begun, place, tag, ids, hold):
    tsends, recvs = begun
    recvs = _after(recvs, hold)
    return _join_halves([_final_sum(t, r, place) for t, r in zip(tsends, recvs)], "rs_join_halves_" + tag, ids[2])


SMALL_SHAPES = {"rel_table": (NUM_BUCKETS, 2 * N_HEADS), "norm_mix_g": (DEPTH, D_MODEL), "qnorm_a_g": (DEPTH, HEAD_DIM),
                "knorm_a_g": (DEPTH, HEAD_DIM), "qnorm_b_g": (DEPTH, HEAD_DIM), "knorm_b_g": (DEPTH, HEAD_DIM),
                "sink_b": (DEPTH, N_HEADS), "norm_ffn_g": (DEPTH, D_MODEL), "norm_ple_g": (DEPTH, D_MODEL)}


def _pack_small(vals, last=None):
    flat = jnp.concatenate([vals[n].astype(F32).reshape(-1) for n in SMALL])
    tail = jnp.zeros((SMALL_ROWS * LANES - flat.shape[0],), F32)
    if last is not None:
        tail = tail.at[-1].set(last)
    return jnp.concatenate([flat, tail]).reshape(SMALL_ROWS, LANES)


def _unpack_small(packed):
    flat, out, off = packed.reshape(-1), {}, 0
    for n in SMALL:
        size = math.prod(SMALL_SHAPES[n])
        out[n] = flat[off:off + size].reshape(SMALL_SHAPES[n])
        off += size
    return out


def _adamw(w, gs, g_row, m, v, name):
    c1 = 1.0 - ADAM_B1 ** ADAM_STEP
    c2 = 1.0 - ADAM_B2 ** ADAM_STEP
    total, width = w.shape
    n_layers = len(gs)
    per = total // n_layers
    tm = max(t for t in range(8, 513, 8) if per % t == 0 and g_row % t == 0)
    nblk = per // tm

    def body(*refs):
        w_ref, g_refs = refs[0], refs[1:1 + n_layers]
        m_ref, v_ref, og, od, om, ov = refs[1 + n_layers:]
        layer = pl.program_id(0) // nblk
        g = g_refs[0][...]
        for l in range(1, n_layers):
            g = jnp.where(layer == l, g_refs[l][...], g)
        m_new = ADAM_B1 * m_ref[...] + (1.0 - ADAM_B1) * g
        v_new = ADAM_B2 * v_ref[...] + (1.0 - ADAM_B2) * (g * g)
        og[...] = g
        od[...] = -ADAM_LR * ((m_new / c1) / (jnp.sqrt(v_new / c2) + ADAM_EPS) + ADAM_WD * w_ref[...])
        om[...] = m_new
        ov[...] = v_new

    row = pl.BlockSpec((tm, width), lambda i: (i, 0))
    g_specs = [pl.BlockSpec((tm, width), lambda i, l=l: (g_row // tm + jnp.clip(i - l * nblk, 0, nblk - 1), 0))
               for l in range(n_layers)]
    return pl.pallas_call(
        body, out_shape=[jax.ShapeDtypeStruct((total, width), F32)] * 4, grid=(total // tm,),
        in_specs=[row] + g_specs + [row, row], out_specs=[row] * 4, name=name,
        compiler_params=_params("parallel"))(w, *gs, m, v)


def kernel(x, p, rel_table, norm_mix_g, w_in, qnorm_a_g, knorm_a_g, qnorm_b_g, knorm_b_g, sink_b, w_branch_a, w_branch_b, w_out, norm_ffn_g, w_ffn_gate, w_ffn_up, w_ffn_down, norm_ple_g, w_ple_gate, w_ple_proj, loss_target, m_rel_table, m_norm_mix_g, m_w_in, m_qnorm_a_g, m_knorm_a_g, m_qnorm_b_g, m_knorm_b_g, m_sink_b, m_w_branch_a, m_w_branch_b, m_w_out, m_norm_ffn_g, m_w_ffn_gate, m_w_ffn_up, m_w_ffn_down, m_norm_ple_g, m_w_ple_gate, m_w_ple_proj, v_rel_table, v_norm_mix_g, v_w_in, v_qnorm_a_g, v_knorm_a_g, v_qnorm_b_g, v_knorm_b_g, v_sink_b, v_w_branch_a, v_w_branch_b, v_w_out, v_norm_ffn_g, v_w_ffn_gate, v_w_ffn_up, v_w_ffn_down, v_norm_ple_g, v_w_ple_gate, v_w_ple_proj):
    given = dict(locals())

    def held(name, a):
        return jnp.swapaxes(a, 1, 2) if name in TRANSPOSED else a

    weights = {n: held(n, given[n]) for n in WEIGHTS}
    moments_m = {n: held(n, given["m_" + n]) for n in WEIGHTS}
    moments_v = {n: held(n, given["v_" + n]) for n in WEIGHTS}
    xi, yi, ci = _place()
    place = jnp.stack([2 * xi + yi, ci]).astype(jnp.int32)

    shards = [_pack_groups(weights, l, BF16) for l in range(DEPTH)]
    w_in_flight = [_allgather(shards[0][:1], "allgather_w_in_layer0", sequencer=9), None]
    rest_flight = [None, None]
    small = {n: weights[n] for n in SMALL}

    def w_in_of(l, mark):
        landed = _after(w_in_flight[l], (shards[1], mark) if l == 0 else mark)
        rest_flight[l] = _allgather(_after(shards[l][1:], landed), f"allgather_rest_layer{l}", sequencer=(1, 2)[l])
        return _unpack_full(landed, GATHER_GROUPS[:1])["w_in"]

    def rest_of(l, mark):
        if l + 1 < DEPTH:
            w_in_flight[l + 1] = _allgather(_after(shards[l + 1][:1], mark), f"allgather_w_in_layer{l + 1}", sequencer=16)
        return _unpack_full(_after(rest_flight[l], mark), GATHER_GROUPS[1:])

    loss, dx, gbig, gsmall, marks = _local_step(x[0], p[:, 0], loss_target[0], w_in_of, rest_of, small)

    gsends = [_pack_grads(gbig[l]) for l in range(DEPTH)]
    stages = {"rest_layer1": (gsends[1][1:], (3, 4, 5)), "w_in_layer1": (gsends[1][:1], (13, 14, 15)),
              "rest_layer0": (gsends[0][1:], (6, 7, 8)), "w_in_layer0": (gsends[0][:1], (10, 11, 12))}
    begun = {tag: _reduce_scatter_begin(g, place, tag, ids) for tag, (g, ids) in stages.items()}

    def finish(tag, hold):
        return _reduce_scatter_finish(begun[tag], place, tag, stages[tag][1], hold)

    red1 = finish("w_in_layer1", dx) + finish("rest_layer1", marks[0]["attn_bwd_done"])
    rest0 = finish("rest_layer0", gbig[0]["w_in"])

    grads, delta, new_m, new_v = {}, {}, {}, {}

    def update(group, reduced):
        offs, _ = _group_rows(group)
        for n in group:
            shape = weights[n].shape
            two_d = lambda a: a.reshape(shape[0] * shape[1], shape[2])
            outs = _adamw(two_d(weights[n]), reduced, offs[n], two_d(moments_m[n]), two_d(moments_v[n]), "adamw_" + n)
            grads[n], delta[n], new_m[n], new_v[n] = (held(n, o.reshape(shape)) for o in outs)

    for gi in (1, 2):
        update(GROUPS[gi], _after([rest0[gi - 1], red1[gi]], begun["w_in_layer0"][0]))
    small_grads = _allreduce_small(_pack_small(gsmall, last=loss))
    g_, d_, m_, v_ = _adamw(_pack_small(weights), [small_grads], 0, _pack_small(moments_m), _pack_small(moments_v),
                            "adamw_small")
    grads.update(_unpack_small(g_))
    delta.update(_unpack_small(d_))
    new_m.update(_unpack_small(m_))
    new_v.update(_unpack_small(v_))
    others_done = [dx, d_] + [delta[n] for gi in (1, 2) for n in GROUPS[gi]]
    update(GROUPS[0], [finish("w_in_layer0", others_done)[0], red1[0]])

    return (small_grads[-1, -1], dx[None], *[grads[n] for n in WEIGHTS], *[delta[n] for n in WEIGHTS],
            *[new_m[n] for n in WEIGHTS], *[new_v[n] for n in WEIGHTS])
```

```python
import functools
import math

import jax
import jax.numpy as jnp
from jax import lax
from jax.experimental import pallas as pl
from jax.experimental.pallas import tpu as pltpu
from jax.experimental.pallas import tpu_sc as plsc

F32 = jnp.float32
BF16 = jnp.bfloat16
MESH_ID = pl.DeviceIdType.MESH

D_MODEL = 1024
DEPTH = 2
HEAD_DIM = 64
N_HEADS = 8
WIDTH = N_HEADS * HEAD_DIM
N_PAIRS = 4
ITEMS = 16
MAX_CHUNK = 1024
N_KV_B = 2
PLE_DIM = 256
D_FF = 2816
D_IN = 4352
OFF_QA, OFF_KA, OFF_VA, OFF_QB, OFF_KB, OFF_VB, OFF_GA, OFF_GB = 0, 512, 1024, 1536, 2048, 2176, 2304, 3328
DILATED = ((64, 1), (64, 4), (64, 16))
BLK_B = 128
NUM_BUCKETS = 32
MAX_DISTANCE = 1024
RMS_EPS = 1e-6
NEG_INF = -1e30
LANES = 128
ROW_TILE = 256
VMEM_LIMIT = 60 * 1024 * 1024

ADAM_LR, ADAM_B1, ADAM_B2, ADAM_EPS, ADAM_WD, ADAM_STEP = 0.001, 0.9, 0.999, 1e-08, 0.01, 10

TRANSPOSED = ("w_in", "w_ffn_gate", "w_ffn_up")
BIG = (
    ("w_in", (D_IN, D_MODEL), 0),
    ("w_branch_a", (WIDTH, D_MODEL), 1),
    ("w_branch_b", (WIDTH, D_MODEL), 1),
    ("w_out", (D_MODEL, D_MODEL), 0),
    ("w_ffn_gate", (D_FF, D_MODEL), 0),
    ("w_ffn_up", (D_FF, D_MODEL), 0),
    ("w_ffn_down", (D_FF, D_MODEL), 0),
    ("w_ple_gate", (D_MODEL, D_MODEL), 0),
    ("w_ple_proj", (PLE_DIM, D_MODEL), 1),
)
SMALL = ("rel_table", "norm_mix_g", "qnorm_a_g", "knorm_a_g", "qnorm_b_g", "knorm_b_g", "sink_b",
         "norm_ffn_g", "norm_ple_g")
WEIGHTS = ("rel_table", "norm_mix_g", "w_in", "qnorm_a_g", "knorm_a_g", "qnorm_b_g", "knorm_b_g", "sink_b",
           "w_branch_a", "w_branch_b", "w_out", "norm_ffn_g", "w_ffn_gate", "w_ffn_up", "w_ffn_down",
           "norm_ple_g", "w_ple_gate", "w_ple_proj")
N_CHIPS = 4
SMALL_ROWS = 64


def _params(*sem):
    return pltpu.CompilerParams(dimension_semantics=sem, vmem_limit_bytes=VMEM_LIMIT)


MM_VMEM_BUDGET = 40 * 1024 * 1024
STEP_OVERHEAD_S = 0.4e-6
TILE_DMA_BYTES_PER_S = 1.5e12


def _mm_dims(a, b, mode):
    if mode == "nn":
        return a.shape[0], b.shape[1], a.shape[1]
    if mode == "nt":
        return a.shape[0], b.shape[0], a.shape[1]
    return a.shape[1], b.shape[1], a.shape[0]


def _mm_tiles(m, n, pairs, tile_bytes, col_offsets, full_rows=False):
    best = None
    widths = [n] if full_rows else [t for t in range(LANES, n + 1, LANES) if n % t == 0 and all(o % t == 0 for o in col_offsets)]
    for tm in (t for t in range(LANES, m + 1, LANES) if m % t == 0):
        for tn in widths:
            io = sum(tm * k * ab + tn * k * bb for k, ab, bb in pairs) + tm * tn * sum(tile_bytes)
            casts = sum((tm * k * 2 if ab == 4 else 0) + (tn * k * 2 if bb == 4 else 0) for k, ab, bb in pairs)
            if 2 * io + len(pairs) * tm * tn * 4 + casts > MM_VMEM_BUDGET:
                continue
            cost = (m // tm) * (n // tn) * STEP_OVERHEAD_S + io / TILE_DMA_BYTES_PER_S
            if best is None or (cost, -tm) < best[0]:
                best = ((cost, -tm), tm, tn)
    return best[1], best[2]


def _mm_fused(pairs, extras, epilogue, out_dtypes, name, next_gain=None):
    m, n, _ = _mm_dims(*pairs[0])
    assert all(_mm_dims(*p)[:2] == (m, n) for p in pairs)
    with_norm = next_gain is not None
    out_dtypes = list(out_dtypes) + ([BF16] if with_norm else [])
    tm, tn = _mm_tiles(
        m, n, [(_mm_dims(a, b, mode)[2], a.dtype.itemsize, b.dtype.itemsize) for a, b, mode in pairs],
        [e.dtype.itemsize for e, _ in extras] + [jnp.dtype(d).itemsize for d in out_dtypes], [off for _, off in extras],
        full_rows=with_norm)
    dims = {"nn": (((1,), (0,)), ((), ())), "nt": (((1,), (1,)), ((), ())), "tn": (((0,), (0,)), ((), ()))}
    in_specs, args = [], []
    for a, b, mode in pairs:
        k = _mm_dims(a, b, mode)[2]
        in_specs.append(pl.BlockSpec((k, tm), lambda i, j: (0, i)) if mode == "tn" else pl.BlockSpec((tm, k), lambda i, j: (i, 0)))
        in_specs.append(pl.BlockSpec((tn, k), lambda i, j: (j, 0)) if mode == "nt" else pl.BlockSpec((k, tn), lambda i, j: (0, j)))
        args += [a, b]
    for e, off in extras:
        in_specs.append(pl.BlockSpec((tm, tn), lambda i, j, o=off // tn: (i, o + j)))
        args.append(e)
    n_pairs, n_tiles = len(pairs), 2 * len(pairs) + len(extras)
    if with_norm:
        in_specs.append(pl.BlockSpec((1, n), lambda i, j: (0, 0)))
        args.append(next_gain)
    n_in = len(args)

    def body(*refs):
        products = [lax.dot_general(refs[2 * p][...].astype(BF16), refs[2 * p + 1][...].astype(BF16), dims[pairs[p][2]],
                                    preferred_element_type=F32) for p in range(n_pairs)]
        outs = list(epilogue(products, [r[...] for r in refs[2 * n_pairs:n_tiles]]))
        if with_norm:
            y = outs[0]
            outs.append((y * lax.rsqrt(jnp.mean(y * y, axis=-1, keepdims=True) + RMS_EPS)) * refs[n_tiles][...])
        for r, o in zip(refs[n_in:], outs):
            r[...] = o.astype(r.dtype)

    tile = pl.BlockSpec((tm, tn), lambda i, j: (i, j))
    return pl.pallas_call(
        body, out_shape=[jax.ShapeDtypeStruct((m, n), d) for d in out_dtypes], grid=(m // tm, n // tn),
        in_specs=in_specs, out_specs=[tile] * len(out_dtypes), name=name,
        compiler_params=_params("parallel", "parallel"))(*args)


def _mm(a, b, mode, out_dtype, name, res=None):
    if res is None:
        return _mm_fused([(a, b, mode)], [], lambda products, extra: products, [out_dtype], name)[0]
    return _mm_fused([(a, b, mode)], [(res, 0)], lambda products, extra: [products[0] + extra[0]], [out_dtype], name)[0]


def _mm_res_norm(a, b, mode, res, gain, name):
    return _mm_fused([(a, b, mode)], [(res, 0)], lambda products, extra: [products[0] + extra[0]], [F32], name,
                     next_gain=gain)


def _ew(fn, ins, out_dtypes, name):
    rows, width = ins[0].shape
    n_in = len(ins)

    def body(*refs):
        outs = fn(*[r[...] for r in refs[:n_in]])
        for r, o in zip(refs[n_in:], outs):
            r[...] = o.astype(r.dtype)

    row = pl.BlockSpec((ROW_TILE, width), lambda i: (i, 0))
    return pl.pallas_call(
        body, out_shape=[jax.ShapeDtypeStruct((rows, width), dt) for dt in out_dtypes], grid=(rows // ROW_TILE,),
        in_specs=[row] * n_in, out_specs=[row] * len(out_dtypes), name=name, compiler_params=_params("parallel"))(*ins)


def _sigmoid(x):
    return 1.0 / (1.0 + jnp.exp(-x))


def _seg_sum(v):
    outs = []
    for k in range(v.shape[1] // LANES):
        vp = v[:, k * LANES:(k + 1) * LANES]
        left = lax.broadcasted_iota(jnp.int32, vp.shape, 1) < HEAD_DIM
        sl = jnp.sum(jnp.where(left, vp, 0.0), axis=-1, keepdims=True)
        sr = jnp.sum(jnp.where(left, 0.0, vp), axis=-1, keepdims=True)
        outs.append(jnp.where(left, sl, sr))
    return outs[0] if len(outs) == 1 else jnp.concatenate(outs, axis=1)


def _seg_rstd(x):
    return lax.rsqrt(_seg_sum(x * x) * (1.0 / HEAD_DIM) + RMS_EPS)


def _rms_fwd(x, g, name):
    rows, d = x.shape
    tm = ROW_TILE

    def body(x_ref, g_ref, h_ref):
        xv = x_ref[...]
        r = lax.rsqrt(jnp.mean(xv * xv, axis=-1, keepdims=True) + RMS_EPS)
        h_ref[...] = ((xv * r) * g_ref[...]).astype(BF16)

    return pl.pallas_call(
        body, out_shape=jax.ShapeDtypeStruct((rows, d), BF16), grid=(rows // tm,),
        in_specs=[pl.BlockSpec((tm, d), lambda i: (i, 0)), pl.BlockSpec((1, d), lambda i: (0, 0))],
        out_specs=pl.BlockSpec((tm, d), lambda i: (i, 0)), name=name,
        compiler_params=_params("parallel"))(x, g)


def _mm_rms_bwd(pairs, x, g, dres, name):
    rows, d = x.shape
    assert all(_mm_dims(*p)[:2] == (rows, d) for p in pairs)
    kbytes = [(_mm_dims(a, b, mode)[2], a.dtype.itemsize, b.dtype.itemsize) for a, b, mode in pairs]
    tm = max(t for t in (512, 256, 128) if rows % t == 0 and
             2 * (sum(t * k * ab + d * k * bb for k, ab, bb in kbytes) + t * d * 14) + 2 * t * d * 4 <= MM_VMEM_BUDGET)
    dims = {"nn": (((1,), (0,)), ((), ())), "nt": (((1,), (1,)), ((), ()))}
    n_pairs = len(pairs)

    def body(*refs):
        x_ref, g_ref, dres_ref = refs[2 * n_pairs:2 * n_pairs + 3]
        dx_ref, dxb_ref, dg_ref = refs[2 * n_pairs + 3:]
        dhv = functools.reduce(lambda u, v: u + v, [
            lax.dot_general(refs[2 * p][...].astype(BF16), refs[2 * p + 1][...].astype(BF16), dims[pairs[p][2]],
                            preferred_element_type=F32) for p in range(n_pairs)])
        xv = x_ref[...]
        r = lax.rsqrt(jnp.mean(xv * xv, axis=-1, keepdims=True) + RMS_EPS)
        xh = xv * r
        dxh = dhv * g_ref[...]
        dxv = dres_ref[...] + r * (dxh - xh * jnp.mean(dxh * xh, axis=-1, keepdims=True))
        dx_ref[...] = dxv
        dxb_ref[...] = dxv.astype(BF16)
        part = jnp.sum(dhv * xh, axis=0, keepdims=True)

        @pl.when(pl.program_id(0) == 0)
        def _():
            dg_ref[...] = part

        @pl.when(pl.program_id(0) > 0)
        def _():
            dg_ref[...] += part

    row = pl.BlockSpec((tm, d), lambda i: (i, 0))
    vec = pl.BlockSpec((1, d), lambda i: (0, 0))
    in_specs, args = [], []
    for a, b, mode in pairs:
        in_specs += [pl.BlockSpec((tm, a.shape[1]), lambda i: (i, 0)), pl.BlockSpec(b.shape, lambda i: (0, 0))]
        args += [a, b]
    return pl.pallas_call(
        body, out_shape=[jax.ShapeDtypeStruct((rows, d), F32), jax.ShapeDtypeStruct((rows, d), BF16),
                         jax.ShapeDtypeStruct((1, d), F32)],
        grid=(rows // tm,), in_specs=in_specs + [row, vec, row], out_specs=[row, row, vec],
        name=name, compiler_params=_params("arbitrary"))(*args, x, g, dres)


def _ple_bwd(dx, z, e):
    s = _sigmoid(z.astype(F32))
    return dx * s, dx * e.astype(F32) * (s * (1.0 - s))


def _loss_grad(y, t, z, e):
    rows, d = y.shape
    tm = ROW_TILE

    def body(y_ref, t_ref, z_ref, e_ref, dy_ref, de_ref, dz_ref, l_ref):
        err = y_ref[...] - t_ref[...]
        dy = err * (1.0 / d)
        dy_ref[...] = dy
        de, dz = _ple_bwd(dy, z_ref[...], e_ref[...])
        de_ref[...] = de.astype(BF16)
        dz_ref[...] = dz.astype(BF16)
        part = jnp.zeros((1, LANES), F32) + jnp.sum(err * err) * (0.5 / d)

        @pl.when(pl.program_id(0) == 0)
        def _():
            l_ref[...] = part

        @pl.when(pl.program_id(0) > 0)
        def _():
            l_ref[...] += part

    row = pl.BlockSpec((tm, d), lambda i: (i, 0))
    return pl.pallas_call(
        body, out_shape=[jax.ShapeDtypeStruct((rows, d), F32), jax.ShapeDtypeStruct((rows, d), BF16),
                         jax.ShapeDtypeStruct((rows, d), BF16), jax.ShapeDtypeStruct((1, LANES), F32)],
        grid=(rows // tm,), in_specs=[row] * 4, out_specs=[row, row, row, pl.BlockSpec((1, LANES), lambda i: (0, 0))],
        name="loss_grad", compiler_params=_params("arbitrary"))(y, t, z, e)


def _swap_halves(v):
    return pltpu.roll(v, HEAD_DIM, axis=1)


def _expand_kv(kv):
    left = lax.broadcasted_iota(jnp.int32, kv.shape, 1) < HEAD_DIM
    sw = _swap_halves(kv)
    h0 = jnp.where(left, kv, sw)
    h1 = jnp.where(left, sw, kv)
    return jnp.concatenate([h0, h0, h1, h1], axis=1)


def _reduce_kv(dkv):
    left = lax.broadcasted_iota(jnp.int32, (dkv.shape[0], LANES), 1) < HEAD_DIM
    t = dkv[:, 0:LANES] + dkv[:, LANES:2 * LANES]
    u = dkv[:, 2 * LANES:3 * LANES] + dkv[:, 3 * LANES:4 * LANES]
    t = t + _swap_halves(t)
    u = u + _swap_halves(u)
    return jnp.where(left, t, u)


def _qknorm_fwd(proj, gqa, gka, gqb, gkb):
    rows = proj.shape[0]
    tm = ROW_TILE

    def body(qa_ref, ka_ref, qb_ref, kb_ref, vb_ref, gqa_ref, gka_ref, gqb_ref, gkb_ref, oqa, oka, oqb, okb, ovb):
        for src, g_ref, dst in ((qa_ref, gqa_ref, oqa), (ka_ref, gka_ref, oka), (qb_ref, gqb_ref, oqb)):
            xv = src[...]
            dst[...] = (xv * _seg_rstd(xv)) * g_ref[...]
        kv = kb_ref[...]
        okb[...] = _expand_kv((kv * _seg_rstd(kv)) * gkb_ref[...])
        ovb[...] = _expand_kv(vb_ref[...])

    def win(width, off):
        return pl.BlockSpec((tm, width), lambda i: (i, off // width))

    vec = lambda w: pl.BlockSpec((1, w), lambda i: (0, 0))
    out = pl.BlockSpec((tm, WIDTH), lambda i: (i, 0))
    return pl.pallas_call(
        body, out_shape=[jax.ShapeDtypeStruct((rows, WIDTH), F32)] * 5, grid=(rows // tm,),
        in_specs=[win(WIDTH, OFF_QA), win(WIDTH, OFF_KA), win(WIDTH, OFF_QB), win(LANES, OFF_KB), win(LANES, OFF_VB),
                  vec(WIDTH), vec(WIDTH), vec(WIDTH), vec(LANES)],
        out_specs=[out] * 5, name="qknorm_fwd", compiler_params=_params("parallel"))(
            proj, proj, proj, proj, proj, gqa, gka, gqb, gkb)


def _norm_bwd(xv, g, dy):
    r = _seg_rstd(xv)
    xh = xv * r
    dxh = dy * g
    dx = r * (dxh - xh * (_seg_sum(dxh * xh) * (1.0 / HEAD_DIM)))
    return dx, jnp.sum(dy * xh, axis=0, keepdims=True)


def _qknorm_bwd(proj, gqa, gka, gqb, gkb, dqa, dka, dva, dqb, dkb, dvb, dga, dgb):
    rows = proj.shape[0]
    tm = ROW_TILE
    n_a = len(dqa)

    def body(*refs):
        qa_ref, ka_ref, qb_ref, kb_ref, gqa_ref, gka_ref, gqb_ref, gkb_ref = refs[:8]
        pos = 8
        dqa_refs, dka_refs, dva_refs = refs[pos:pos + n_a], refs[pos + n_a:pos + 2 * n_a], refs[pos + 2 * n_a:pos + 3 * n_a]
        pos += 3 * n_a
        dqb_ref, dkb_ref, dvb_ref, dga_ref, dgb_ref = refs[pos:pos + 5]
        dproj_ref, ogqa, ogka, ogqb, ogkb = refs[pos + 5:]

        def total(rs):
            acc = rs[0][...]
            for r in rs[1:]:
                acc = acc + r[...]
            return acc

        dx_qa, p_qa = _norm_bwd(qa_ref[...], gqa_ref[...], total(dqa_refs))
        dx_ka, p_ka = _norm_bwd(ka_ref[...], gka_ref[...], total(dka_refs))
        dx_qb, p_qb = _norm_bwd(qb_ref[...], gqb_ref[...], dqb_ref[...])
        dx_kb, p_kb = _norm_bwd(kb_ref[...], gkb_ref[...], _reduce_kv(dkb_ref[...]))
        dproj_ref[:, OFF_QA:OFF_QA + WIDTH] = dx_qa.astype(BF16)
        dproj_ref[:, OFF_KA:OFF_KA + WIDTH] = dx_ka.astype(BF16)
        dproj_ref[:, OFF_VA:OFF_VA + WIDTH] = total(dva_refs).astype(BF16)
        dproj_ref[:, OFF_QB:OFF_QB + WIDTH] = dx_qb.astype(BF16)
        dproj_ref[:, OFF_KB:OFF_KB + LANES] = dx_kb.astype(BF16)
        dproj_ref[:, OFF_VB:OFF_VB + LANES] = _reduce_kv(dvb_ref[...]).astype(BF16)
        dproj_ref[:, OFF_GA:OFF_GB] = dga_ref[...]
        dproj_ref[:, OFF_GB:D_IN] = dgb_ref[...]
        first = pl.program_id(0) == 0
        for o_ref, part in ((ogqa, p_qa), (ogka, p_ka), (ogqb, p_qb), (ogkb, p_kb)):
            @pl.when(first)
            def _(o_ref=o_ref, part=part):
                o_ref[...] = part

            @pl.when(jnp.logical_not(first))
            def _(o_ref=o_ref, part=part):
                o_ref[...] += part

    def win(width, off):
        return pl.BlockSpec((tm, width), lambda i: (i, off // width))

    vec = lambda w: pl.BlockSpec((1, w), lambda i: (0, 0))
    row = lambda w: pl.BlockSpec((tm, w), lambda i: (i, 0))
    in_specs = [win(WIDTH, OFF_QA), win(WIDTH, OFF_KA), win(WIDTH, OFF_QB), win(LANES, OFF_KB),
                vec(WIDTH), vec(WIDTH), vec(WIDTH), vec(LANES)]
    in_specs += [row(WIDTH)] * (3 * n_a + 3) + [row(D_MODEL)] * 2
    return pl.pallas_call(
        body,
        out_shape=[jax.ShapeDtypeStruct((rows, D_IN), BF16), jax.ShapeDtypeStruct((1, WIDTH), F32),
                   jax.ShapeDtypeStruct((1, WIDTH), F32), jax.ShapeDtypeStruct((1, WIDTH), F32),
                   jax.ShapeDtypeStruct((1, LANES), F32)],
        grid=(rows // tm,), in_specs=in_specs,
        out_specs=[row(D_IN), vec(WIDTH), vec(WIDTH), vec(WIDTH), vec(LANES)],
        name="qknorm_bwd", compiler_params=_params("arbitrary"))(
            proj, proj, proj, proj, gqa, gka, gqb, gkb, *dqa, *dka, *dva, dqb, dkb, dvb, dga, dgb)


def _t5_bucket(rel):
    half_b = NUM_BUCKETS // 2
    max_exact = half_b // 2
    sign = jnp.where(rel > 0, half_b, 0)
    n = jnp.abs(rel)
    nf = jnp.maximum(n, 1).astype(F32)
    large = max_exact + (jnp.log(nf / max_exact) / math.log(MAX_DISTANCE / max_exact)
                         * (half_b - max_exact)).astype(jnp.int32)
    large = jnp.minimum(large, half_b - 1)
    return sign + jnp.where(n < max_exact, n, large)


def _band_buckets(blk, dilation):
    i = jnp.arange(blk, dtype=jnp.int32)[:, None]
    j = jnp.arange(3 * blk, dtype=jnp.int32)[None, :]
    rel = j - blk - i
    return jnp.where(jnp.abs(rel) <= blk, _t5_bucket(rel * dilation), -1)


def _bias_tiles(table, buckets, head_off, name):
    blk = buckets.shape[0]

    def body(tab_ref, bk_ref, o_ref):
        h = pl.program_id(0) + head_off
        bk = bk_ref[...]
        acc = jnp.full(bk.shape, NEG_INF, F32)
        for b in range(NUM_BUCKETS):
            acc = jnp.where(bk == b, tab_ref[b, h], acc)
        o_ref[0] = acc

    return pl.pallas_call(
        body, out_shape=jax.ShapeDtypeStruct((N_HEADS, blk, 3 * blk), F32), grid=(N_HEADS,),
        in_specs=[pl.BlockSpec(memory_space=pltpu.SMEM), pl.BlockSpec((blk, 3 * blk), lambda h: (0, 0))],
        out_specs=pl.BlockSpec((1, blk, 3 * blk), lambda h: (h, 0, 0)),
        name=name, compiler_params=_params("parallel"))(table, buckets)


def _table_grad(dbias, buckets, name):
    blk = buckets.shape[0]

    def body(db_ref, bk_ref, o_ref):
        bk = bk_ref[...]
        dbv = db_ref[0]
        lane = lax.broadcasted_iota(jnp.int32, (1, LANES), 1)
        acc = jnp.zeros((1, LANES), F32)
        for b in range(NUM_BUCKETS):
            acc = jnp.where(lane == b, jnp.sum(jnp.where(bk == b, dbv, 0.0)), acc)
        o_ref[0] = acc

    out = pl.pallas_call(
        body, out_shape=jax.ShapeDtypeStruct((N_HEADS, 1, LANES), F32), grid=(N_HEADS,),
        in_specs=[pl.BlockSpec((1, blk, 3 * blk), lambda h: (h, 0, 0)), pl.BlockSpec((blk, 3 * blk), lambda h: (0, 0))],
        out_specs=pl.BlockSpec((1, 1, LANES), lambda h: (h, 0, 0)),
        name=name, compiler_params=_params("parallel"))(dbias, buckets)
    return out[:, 0, :NUM_BUCKETS]


def _dot_nt(a, b):
    return lax.dot_general(a, b, (((1,), (1,)), ((), ())), preferred_element_type=F32)


def _dot_tn(a, b):
    return lax.dot_general(a, b, (((0,), (0,)), ((), ())), preferred_element_type=F32)


def _stack_pair(x2, left):
    return jnp.concatenate([jnp.where(left, x2, 0.0), jnp.where(left, 0.0, x2)], axis=0).astype(BF16)


def _attn_geometry(blk, d):
    halo = blk * d
    subs = max(1, min(ITEMS // d, MAX_CHUNK // halo))
    return subs, min(d, ITEMS // subs), halo


def _item_of(j, r0, per_group):
    return j // per_group, r0 + j % per_group


def _span_rows(ref, first, r, blk, d):
    if d == 1:
        return ref[first:first + blk, :]
    return ref[pl.ds(first + r, blk, stride=d), :]


def _set_span_rows(ref, first, r, blk, d, val, add=False):
    idx = slice(first, first + blk) if d == 1 else pl.ds(first + r, blk, stride=d)
    ref[idx, :] = ref[idx, :] + val if add else val


def _for_groups(group, groups, per_group):
    if groups == 1:
        group(0)
    else:
        def step(g, carry):
            group(g * per_group)
            return carry

        lax.fori_loop(0, groups, step, 0)


def _key_rows(p_ref, c_ref, n_ref, s, r, subs, halo, blk, d):
    parts = []
    for span in (s - 1, s, s + 1):
        if span < 0:
            parts.append(_span_rows(p_ref, 0, r, blk, d))
        elif span == subs:
            parts.append(_span_rows(n_ref, 0, r, blk, d))
        else:
            parts.append(_span_rows(c_ref, span * halo, r, blk, d))
    return jnp.concatenate(parts, axis=0)


def _item_penalty(t, nct, s, subs, blk):
    first_ok = True if s > 0 else t > 0
    last_ok = True if s < subs - 1 else t < nct - 1
    col = lax.broadcasted_iota(jnp.int32, (1, 3 * blk), 1)
    ok = jnp.logical_and(jnp.logical_or(col >= blk, first_ok), jnp.logical_or(col < 2 * blk, last_ok))
    return jnp.where(ok, 0.0, NEG_INF).astype(F32)


def _attn_specs(seq, blk, d, step_of, col=0):
    subs, _, halo = _attn_geometry(blk, d)
    last, first = seq // halo - 1, col // LANES
    cur = pl.BlockSpec((subs * halo, LANES), lambda hp, t: (step_of(t), first + hp))
    prev = pl.BlockSpec((halo, LANES), lambda hp, t: (jnp.clip(step_of(t) * subs - 1, 0, last), first + hp))
    nxt = pl.BlockSpec((halo, LANES), lambda hp, t: (jnp.minimum((step_of(t) + 1) * subs, last), first + hp))
    return cur, prev, nxt


def _attn_fwd(q, k, v, bias, sink, blk, d, name, v_col=0):
    seq = q.shape[0]
    subs, per_group, halo = _attn_geometry(blk, d)
    items, chunk, groups = subs * per_group, subs * halo, d // per_group
    nct = seq // chunk
    has_sink = sink is not None
    scale = HEAD_DIM ** -0.5

    def body(*refs):
        q_ref, kp, kc, kn, vp, vc, vn, b_ref = refs[:8]
        s_ref = refs[8] if has_sink else None
        o_ref, l_ref = refs[-2], refs[-1]
        t = pl.program_id(1)
        left = lax.broadcasted_iota(jnp.int32, (1, LANES), 1) < HEAD_DIM
        bias2 = b_ref[...]

        def group(r0):
            scores, vcats = [], []
            for j in range(items):
                s, r = _item_of(j, r0, per_group)
                qs = _stack_pair(_span_rows(q_ref, s * halo, r, blk, d) * scale, left)
                kcat = _key_rows(kp, kc, kn, s, r, subs, halo, blk, d).astype(BF16)
                scores.append(_dot_nt(qs, kcat) + bias2 + _item_penalty(t, nct, s, subs, blk))
                vcats.append(_key_rows(vp, vc, vn, s, r, subs, halo, blk, d).astype(BF16))
            ms = [jnp.max(s, axis=-1, keepdims=True) for s in scores]
            if has_sink:
                sk = s_ref[...]
                ms = [jnp.maximum(m, sk) for m in ms]
            ps = [jnp.exp(s - m) for s, m in zip(scores, ms)]
            dens = [jnp.sum(p, axis=-1, keepdims=True) for p in ps]
            if has_sink:
                dens = [den + jnp.exp(sk - m) for den, m in zip(dens, ms)]
            pns = [(p * (1.0 / den)).astype(BF16) for p, den in zip(ps, dens)]
            lses = [m + jnp.log(den) for m, den in zip(ms, dens)]
            for j in range(items):
                s, r = _item_of(j, r0, per_group)
                o2 = jnp.dot(pns[j], vcats[j], preferred_element_type=F32)
                _set_span_rows(o_ref, s * halo, r, blk, d, jnp.where(left, o2[:blk], o2[blk:]))
                _set_span_rows(l_ref, s * halo, r, blk, d, jnp.where(left, lses[j][:blk], lses[j][blk:]))

        _for_groups(group, groups, per_group)

    cur, prev, nxt = _attn_specs(seq, blk, d, lambda t: t)
    v_cur, v_prev, v_nxt = _attn_specs(seq, blk, d, lambda t: t, v_col)
    in_specs = [cur, prev, cur, nxt, v_prev, v_cur, v_nxt, pl.BlockSpec((2 * blk, 3 * blk), lambda hp, t: (hp, 0))]
    args = [q, k, k, k, v, v, v, bias]
    if has_sink:
        in_specs.append(pl.BlockSpec((2 * blk, 1), lambda hp, t: (hp, 0)))
        args.append(sink)
    return pl.pallas_call(
        body, out_shape=[jax.ShapeDtypeStruct((seq, WIDTH), F32)] * 2, grid=(N_PAIRS, nct),
        in_specs=in_specs, out_specs=[cur, cur], name=name,
        compiler_params=_params("parallel", "parallel"))(*args)


def _attn_bwd(q, k, v, do, lse, delta, bias, sink, blk, d, name, v_col=0):
    seq = q.shape[0]
    subs, per_group, halo = _attn_geometry(blk, d)
    items, chunk, groups = subs * per_group, subs * halo, d // per_group
    nct = seq // chunk
    has_sink = sink is not None
    n_in = 12 if has_sink else 11
    scale = HEAD_DIM ** -0.5

    def body(*refs):
        q_ref, kp, kc, kn, vp, vc, vn, do_ref, l_ref, d_ref, b_ref = refs[:11]
        s_ref = refs[11] if has_sink else None
        dq_ref, dk_ref, dv_ref, db_ref = refs[n_in:n_in + 4]
        ds_ref = refs[n_in + 4] if has_sink else None
        wk, wv = refs[-2], refs[-1]
        t = pl.program_id(1)

        @pl.when(t == 0)
        def _():
            wk[...] = jnp.zeros_like(wk)
            wv[...] = jnp.zeros_like(wv)
            db_ref[...] = jnp.zeros_like(db_ref)
            if has_sink:
                ds_ref[...] = jnp.zeros_like(ds_ref)

        @pl.when(t > 0)
        def _():
            for w in (wk, wv):
                keep = w[chunk:2 * chunk + halo]
                w[0:chunk + halo] = keep
                w[chunk + halo:2 * chunk + halo] = jnp.zeros((chunk, LANES), F32)

        @pl.when(t < nct)
        def _():
            lane = lax.broadcasted_iota(jnp.int32, (1, LANES), 1)
            left = lane < HEAD_DIM
            bias2 = b_ref[...]

            def group(r0):
                qss, doss, kcats, scores, dps, lcols, dcols = [], [], [], [], [], [], []
                for j in range(items):
                    s, r = _item_of(j, r0, per_group)
                    qs = _stack_pair(_span_rows(q_ref, s * halo, r, blk, d) * scale, left)
                    dos = _stack_pair(_span_rows(do_ref, s * halo, r, blk, d), left)
                    kcat = _key_rows(kp, kc, kn, s, r, subs, halo, blk, d).astype(BF16)
                    vcat = _key_rows(vp, vc, vn, s, r, subs, halo, blk, d).astype(BF16)
                    l2, d2 = _span_rows(l_ref, s * halo, r, blk, d), _span_rows(d_ref, s * halo, r, blk, d)
                    lcols.append(jnp.concatenate([jnp.max(jnp.where(left, l2, NEG_INF), axis=-1, keepdims=True),
                                                  jnp.max(jnp.where(left, NEG_INF, l2), axis=-1, keepdims=True)], axis=0))
                    dcols.append(jnp.concatenate([jnp.sum(jnp.where(lane == 0, d2, 0.0), axis=-1, keepdims=True),
                                                  jnp.sum(jnp.where(lane == HEAD_DIM, d2, 0.0), axis=-1, keepdims=True)],
                                                 axis=0))
                    scores.append(_dot_nt(qs, kcat) + bias2 + _item_penalty(t, nct, s, subs, blk))
                    dps.append(_dot_nt(dos, vcat))
                    qss.append(qs)
                    doss.append(dos)
                    kcats.append(kcat)
                ps = [jnp.exp(s - lc) for s, lc in zip(scores, lcols)]
                dss = [p * (dp - dc) for p, dp, dc in zip(ps, dps, dcols)]
                db_ref[...] += functools.reduce(lambda a, b: a + b, dss)
                if has_sink:
                    sk = s_ref[...]
                    ds_ref[...] -= functools.reduce(lambda a, b: a + b, [dc * jnp.exp(sk - lc) for dc, lc in zip(dcols, lcols)])
                dsbs = [ds.astype(BF16) for ds in dss]
                for j in range(items):
                    s, r = _item_of(j, r0, per_group)
                    dq2 = jnp.dot(dsbs[j], kcats[j], preferred_element_type=F32) * scale
                    _set_span_rows(dq_ref, s * halo, r, blk, d, jnp.where(left, dq2[:blk], dq2[blk:]))
                news = [(_dot_tn(dsbs[j], qss[j]), _dot_tn(ps[j].astype(BF16), doss[j])) for j in range(items)]
                for which, w in enumerate((wk, wv)):
                    for jr in range(per_group):
                        for sp in range(-1, subs + 1):
                            parts = [news[s * per_group + jr][which][(sp - s + 1) * blk:(sp - s + 2) * blk]
                                     for s in range(subs) if 0 <= sp - s + 1 < 3]
                            _set_span_rows(w, chunk + sp * halo, r0 + jr, blk, d,
                                           functools.reduce(lambda x, y: x + y, parts), add=True)

            _for_groups(group, groups, per_group)

        dk_ref[...] = wk[0:chunk]
        dv_ref[...] = wv[0:chunk]

    cur, prev, nxt = _attn_specs(seq, blk, d, lambda t: jnp.minimum(t, nct - 1))
    v_cur, v_prev, v_nxt = _attn_specs(seq, blk, d, lambda t: jnp.minimum(t, nct - 1), v_col)
    lag = pl.BlockSpec((chunk, LANES), lambda hp, t: (jnp.maximum(t - 1, 0), hp))
    band = pl.BlockSpec((2 * blk, 3 * blk), lambda hp, t: (hp, 0))
    col = pl.BlockSpec((2 * blk, 1), lambda hp, t: (hp, 0))
    in_specs = [cur, prev, cur, nxt, v_prev, v_cur, v_nxt, cur, cur, cur, band]
    args = [q, k, k, k, v, v, v, do, lse, delta, bias]
    out_shape = [jax.ShapeDtypeStruct((seq, WIDTH), F32)] * 3 + [jax.ShapeDtypeStruct((N_HEADS * blk, 3 * blk), F32)]
    out_specs = [cur, lag, lag, band]
    if has_sink:
        in_specs.append(col)
        args.append(sink)
        out_shape.append(jax.ShapeDtypeStruct((N_HEADS * blk, 1), F32))
        out_specs.append(col)
    window = pltpu.VMEM((2 * chunk + halo, LANES), F32)
    return pl.pallas_call(
        body, out_shape=out_shape, grid=(N_PAIRS, nct + 1), in_specs=in_specs, out_specs=out_specs,
        scratch_shapes=[window, window], name=name,
        compiler_params=_params("arbitrary", "arbitrary"))(*args)


def _combine_patterns(outs, lses):
    def combine(*tiles):
        os_, ls = tiles[:len(outs)], tiles[len(outs):]
        m = functools.reduce(jnp.maximum, ls)
        es = [jnp.exp(l - m) for l in ls]
        den = functools.reduce(lambda a, b: a + b, es)
        num = functools.reduce(lambda a, b: a + b, [e * o for e, o in zip(es, os_)])
        return num / den, m + jnp.log(den)

    return _ew(combine, [*outs, *lses], [F32, F32], "combine_a")


def _tile_gain(g, reps):
    return jnp.tile(g[None, :], (1, reps))


def _local_step(x, p, target, w_in_of, rest_of, small):
    rel_table = small["rel_table"]
    buckets_a = [_band_buckets(blk, d) for blk, d in DILATED]
    buckets_b = _band_buckets(BLK_B, 1)
    bias_a = [_bias_tiles(rel_table, bk, 0, "bias_a").reshape(N_HEADS * bk.shape[0], -1) for bk in buckets_a]
    bias_b = _bias_tiles(rel_table, buckets_b, N_HEADS, "bias_b").reshape(N_HEADS * BLK_B, -1)

    saved = []
    for l in range(DEPTH):
        g_mix, g_ffn, g_ple = (small[n][l][None, :] for n in ("norm_mix_g", "norm_ffn_g", "norm_ple_g"))
        gqa, gka, gqb = (_tile_gain(small[n][l], N_HEADS) for n in ("qnorm_a_g", "knorm_a_g", "qnorm_b_g"))
        gkb = _tile_gain(small["knorm_b_g"][l], N_KV_B)
        sink = jnp.repeat(small["sink_b"][l], BLK_B)[:, None]

        h = _rms_fwd(x, g_mix, "rms_mix") if l == 0 else h_next
        w_in = w_in_of(l, (h, bias_a, bias_b)) if l == 0 else w_in_next
        proj = _mm(h, w_in, "nt", F32, "mm_in")
        qa, ka, qb, kb, vb = _qknorm_fwd(proj, gqa, gka, gqb, gkb)
        outs, lses = [], []
        for (blk, d), bias in zip(DILATED, bias_a):
            o, ls = _attn_fwd(qa, ka, proj, bias, None, blk, d, f"attn_a{d}_fwd", v_col=OFF_VA)
            outs.append(o)
            lses.append(ls)
        ya, lse_a = _combine_patterns(outs, lses)
        yb, lse_b = _attn_fwd(qb, kb, vb, bias_b, sink, BLK_B, 1, "attn_b_fwd")
        w = dict(rest_of(l, yb), w_in=w_in)
        def gate(products, extra):
            (ca_, cb_), (ga_, gb_) = products, extra
            return _sigmoid(ga_) * ca_ + _sigmoid(gb_) * cb_, ca_, cb_

        merged, ca, cb = _mm_fused([(ya, w["w_branch_a"], "nn"), (yb, w["w_branch_b"], "nn")],
                                   [(proj, OFF_GA), (proj, OFF_GB)], gate, [BF16, BF16, BF16], "mm_branches_gate")
        x1, h2 = _mm_res_norm(merged, w["w_out"], "nn", x, g_ffn, "mm_out_norm")

        def swiglu(products, extra):
            a_, u_ = products
            return (a_ * _sigmoid(a_)) * u_, a_, u_

        hid, a, u = _mm_fused([(h2, w["w_ffn_gate"], "nt"), (h2, w["w_ffn_up"], "nt")], [], swiglu,
                              [BF16, BF16, BF16], "mm_ffn_gate_up")
        if l + 1 < DEPTH:
            w_in_next = w_in_of(l + 1, hid)
        x2, h3 = _mm_res_norm(hid, w["w_ffn_down"], "nn", x1, g_ple, "mm_ffn_down_norm")

        def ple(products, extra):
            z_, e_ = products
            return extra[0] + _sigmoid(z_) * e_, z_, e_

        next_gain = small["norm_mix_g"][l + 1][None, :] if l + 1 < DEPTH else None
        x3, z, e, *rest = _mm_fused([(h3, w["w_ple_gate"], "nn"), (p[l], w["w_ple_proj"], "nn")], [(x2, 0)], ple,
                                    [F32, BF16, BF16], "mm_ple", next_gain=next_gain)
        h_next = rest[0] if rest else None
        saved.append(dict(w=w, x0=x, h=h, proj=proj, qa=qa, ka=ka, qb=qb, kb=kb, vb=vb, ya=ya, lse_a=lse_a,
                          yb=yb, lse_b=lse_b, ca=ca, cb=cb, merged=merged, x1=x1, h2=h2, a=a, u=u, hid=hid,
                          x2=x2, h3=h3, z=z, e=e))
        x = x3

    dx, de, dz, loss_acc = _loss_grad(x, target, saved[-1]["z"], saved[-1]["e"])
    loss = loss_acc[0, 0]

    gbig = [{} for _ in range(DEPTH)]
    marks = [{} for _ in range(DEPTH)]
    gsmall = {n: [None] * DEPTH for n in SMALL if n != "rel_table"}
    dbias_a = [[] for _ in DILATED]
    dbias_b = []

    for l in reversed(range(DEPTH)):
        sv = saved[l]
        w = sv["w"]
        g_mix, g_ffn, g_ple = (small[n][l][None, :] for n in ("norm_mix_g", "norm_ffn_g", "norm_ple_g"))
        gqa, gka, gqb = (_tile_gain(small[n][l], N_HEADS) for n in ("qnorm_a_g", "knorm_a_g", "qnorm_b_g"))
        gkb = _tile_gain(small["knorm_b_g"][l], N_KV_B)
        sink = jnp.repeat(small["sink_b"][l], BLK_B)[:, None]

        if l < DEPTH - 1:
            de, dz = _ew(_ple_bwd, [dx, sv["z"], sv["e"]], [BF16, BF16], "ple_bwd")
        gbig[l]["w_ple_proj"] = _mm(p[l], de, "tn", BF16, "mm_d_ple_proj")
        gbig[l]["w_ple_gate"] = _mm(sv["h3"], dz, "tn", BF16, "mm_d_ple_gate")
        dx, dxb, gsmall["norm_ple_g"][l] = _mm_rms_bwd([(dz, w["w_ple_gate"], "nt")], sv["x2"], g_ple, dx,
                                                      "mm_dh3_rms_bwd")

        gbig[l]["w_ffn_down"] = _mm(sv["hid"], dxb, "tn", BF16, "mm_d_ffn_down")

        def swiglu_bwd(products, extra):
            dh_, a_, u_ = products[0], extra[0].astype(F32), extra[1].astype(F32)
            s = _sigmoid(a_)
            return dh_ * u_ * (s * (1.0 + a_ * (1.0 - s))), dh_ * (a_ * s)

        da, du = _mm_fused([(dxb, w["w_ffn_down"], "nt")], [(sv["a"], 0), (sv["u"], 0)], swiglu_bwd, [BF16, BF16],
                           "mm_dhid_swiglu_bwd")
        gbig[l]["w_ffn_gate"], gbig[l]["w_ffn_up"] = _mm_fused(
            [(da, sv["h2"], "tn"), (du, sv["h2"], "tn")], [], lambda products, extra: products, [BF16, BF16],
            "mm_d_ffn_gate_up")
        dx, dxb, gsmall["norm_ffn_g"][l] = _mm_rms_bwd([(da, w["w_ffn_gate"], "nn"), (du, w["w_ffn_up"], "nn")],
                                                      sv["x1"], g_ffn, dx, "mm_dh2_rms_bwd")

        gbig[l]["w_out"] = _mm(sv["merged"], dxb, "tn", BF16, "mm_d_out")

        def gate_bwd(products, extra):
            dm_, ca_, cb_ = products[0], extra[0].astype(F32), extra[1].astype(F32)
            sa, sb = _sigmoid(extra[2]), _sigmoid(extra[3])
            return dm_ * sa, dm_ * sb, dm_ * ca_ * (sa * (1.0 - sa)), dm_ * cb_ * (sb * (1.0 - sb))

        dca, dcb, dga, dgb = _mm_fused(
            [(dxb, w["w_out"], "nt")], [(sv["ca"], 0), (sv["cb"], 0), (sv["proj"], OFF_GA), (sv["proj"], OFF_GB)],
            gate_bwd, [BF16, BF16, BF16, BF16], "mm_dmerged_gate_bwd")
        gbig[l]["w_branch_a"], gbig[l]["w_branch_b"] = _mm_fused(
            [(sv["ya"], dca, "tn"), (sv["yb"], dcb, "tn")], [], lambda products, extra: products, [BF16, BF16],
            "mm_d_branches")

        def with_row_dots(products, extra):
            (dya_, dyb_), (ya_, yb_) = products, extra
            return dya_, _seg_sum(dya_ * ya_), dyb_, _seg_sum(dyb_ * yb_)

        dya, delta_a, dyb, delta_b = _mm_fused(
            [(dca, w["w_branch_a"], "nt"), (dcb, w["w_branch_b"], "nt")], [(sv["ya"], 0), (sv["yb"], 0)], with_row_dots,
            [F32, F32, F32, F32], "mm_dy_branches")

        dqa, dka, dva = [], [], []
        for (blk, d), bias, bk in zip(DILATED, bias_a, buckets_a):
            dq_, dk_, dv_, db_ = _attn_bwd(sv["qa"], sv["ka"], sv["proj"], dya, sv["lse_a"], delta_a, bias, None, blk, d,
                                           f"attn_a{d}_bwd", v_col=OFF_VA)
            dqa.append(dq_)
            dka.append(dk_)
            dva.append(dv_)
            dbias_a[len(dqa) - 1].append(db_)
        dqb, dkb, dvb, db_, dsink = _attn_bwd(sv["qb"], sv["kb"], sv["vb"], dyb, sv["lse_b"], delta_b, bias_b, sink,
                                              BLK_B, 1, "attn_b_bwd")
        dbias_b.append(db_)
        gsmall["sink_b"][l] = dsink.reshape(N_HEADS, BLK_B).sum(axis=1)

        dproj, pqa, pka, pqb, pkb = _qknorm_bwd(sv["proj"], gqa, gka, gqb, gkb, dqa, dka, dva, dqb, dkb, dvb, dga, dgb)
        marks[l]["attn_bwd_done"] = dproj
        gsmall["qnorm_a_g"][l] = pqa.reshape(N_HEADS, HEAD_DIM).sum(0)
        gsmall["knorm_a_g"][l] = pka.reshape(N_HEADS, HEAD_DIM).sum(0)
        gsmall["qnorm_b_g"][l] = pqb.reshape(N_HEADS, HEAD_DIM).sum(0)
        gsmall["knorm_b_g"][l] = pkb.reshape(N_KV_B, HEAD_DIM).sum(0)
        gbig[l]["w_in"] = _mm(dproj, sv["h"], "tn", BF16, "mm_d_in")
        dx, _, gsmall["norm_mix_g"][l] = _mm_rms_bwd([(dproj, w["w_in"], "nn")], sv["x0"], g_mix, dx, "mm_dh_rms_bwd")
        gsmall["norm_mix_g"][l] = gsmall["norm_mix_g"][l][0]
        gsmall["norm_ffn_g"][l] = gsmall["norm_ffn_g"][l][0]
        gsmall["norm_ple_g"][l] = gsmall["norm_ple_g"][l][0]

    gsmall = {n: jnp.stack(v) for n, v in gsmall.items()}
    dtable_a = sum(_table_grad(sum(dbs).reshape(N_HEADS, blk, 3 * blk), bk, "table_grad_a")
                   for dbs, (blk, _), bk in zip(dbias_a, DILATED, buckets_a))
    dtable_b = _table_grad(sum(dbias_b).reshape(N_HEADS, BLK_B, 3 * BLK_B), buckets_b, "table_grad_b")
    gsmall["rel_table"] = jnp.concatenate([dtable_a, dtable_b], axis=0).T
    return loss, dx, gbig, gsmall, marks


def _place():
    return lax.axis_index("x"), lax.axis_index("y"), lax.axis_index("c")


def _flip(v, bit):
    return 1 - v if bit else v


CHIP_RELATIONS = ((0, 1), (1, 0), (1, 1))
ANY = pl.BlockSpec(memory_space=pl.ANY)


def _allgather_body(w_refs, out_refs, send_sems, recv_sems):
    x, y, c = _place()
    chips = [(_flip(x, a), _flip(y, b)) for a, b in CHIP_RELATIONS]

    def make(g):
        w_ref, out_ref = w_refs[g], out_refs[g]
        half = w_ref.shape[0] // 2

        def part(px, py, pc):
            return out_ref.at[2 * px + py, pl.ds(pc * half, half), :]

        def copy(k, block, to, src=None):
            return pltpu.make_async_remote_copy(
                src_ref=part(*block) if src is None else src, dst_ref=part(*block),
                send_sem=send_sems.at[7 * g + k], recv_sem=recv_sems.at[7 * g + k], device_id=to,
                device_id_type=MESH_ID)

        own = pltpu.make_async_remote_copy(
            src_ref=w_ref, dst_ref=out_ref.at[2 * x + y], send_sem=send_sems.at[7 * g + 6],
            recv_sem=recv_sems.at[7 * g + 6], device_id=(x, y, 1 - c), device_id_type=MESH_ID)
        first = [copy(k, (x, y, c), (*chip, c), src=w_ref.at[pl.ds(c * half, half), :]) for k, chip in enumerate(chips)]
        passed = [copy(3 + k, (*chip, c), (x, y, 1 - c)) for k, chip in enumerate(chips)]
        arrive = [copy(k, (*chip, c), (x, y, c)) for k, chip in enumerate(chips)]
        arrive2 = [copy(3 + k, (*chip, 1 - c), (x, y, c)) for k, chip in enumerate(chips)]
        return own, first, passed, arrive, arrive2

    made = [make(g) for g in range(len(w_refs))]
    for own, first, _, _, _ in made:
        own.start()
        for cp in first:
            cp.start()
    for _, _, passed, arrive, _ in made:
        for k in range(3):
            arrive[k].wait_recv()
            passed[k].start()
    for own, first, passed, _, arrive2 in made:
        for k in range(3):
            arrive2[k].wait_recv()
        own.wait_recv()
        for cp in first + passed + [own]:
            cp.wait_send()


def _sibling(x, y, c):
    return [(x, y, 1 - c)]


def _same_core_of_other_chips(x, y, c):
    return [(_flip(x, a), _flip(y, b), c) for a, b in CHIP_RELATIONS]


def _exchange(body, ins, out_types, n_sems, name, sequencer=None):
    n = len(ins)
    sems = (pltpu.SemaphoreType.DMA((n_sems,)), pltpu.SemaphoreType.DMA((n_sems,)))
    if sequencer is None:
        in_place = out_types is None
        out_shape = [jax.ShapeDtypeStruct(a.shape, a.dtype) for a in ins] if in_place else out_types

        def tc_body(*refs):
            body(refs[:n], refs[n:n + len(out_shape)], refs[-2], refs[-1])

        return list(pl.pallas_call(
            tc_body, out_shape=out_shape, in_specs=[ANY] * n, out_specs=[ANY] * len(out_shape),
            input_output_aliases={g: g for g in range(n)} if in_place else {}, scratch_shapes=list(sems), name=name)(*ins))

    collective_id, peers = sequencer
    hbm = pltpu.MemorySpace.HBM
    in_refs = [jax.new_ref(a, memory_space=hbm) for a in ins]
    out_refs = in_refs if out_types is None else [jax.empty_ref(t, memory_space=hbm) for t in out_types]

    @pl.kernel(mesh=plsc.ScalarSubcoreMesh(axis_name="sequencer", num_cores=1), name=name, scratch_types=sems,
               compiler_params=pltpu.CompilerParams(collective_id=collective_id))
    def launch(send_sems, recv_sems):
        barrier = pltpu.get_barrier_semaphore()
        devices = peers(*_place())
        for device in devices:
            pl.semaphore_signal(barrier, inc=1, device_id=device, device_id_type=MESH_ID)
        pl.semaphore_wait(barrier, len(devices))
        body(in_refs, out_refs, send_sems, recv_sems)

    launch()
    return [r[...] for r in out_refs]


def _allgather(shards, name, sequencer=None):
    out_types = [jax.ShapeDtypeStruct((N_CHIPS,) + s.shape, s.dtype) for s in shards]
    if sequencer is not None:
        sequencer = (sequencer, lambda x, y, c: _sibling(x, y, c) + _same_core_of_other_chips(x, y, c))
    return _exchange(_allgather_body, shards, out_types, 7 * len(shards), name, sequencer)


def _half_tile(half):
    return max(t for t in range(16, 1025, 16) if half % t == 0)


def _run_copies(cps):
    for cp in cps:
        cp.start()
    for cp in cps:
        cp.wait_recv()
    for cp in cps:
        cp.wait_send()


def _sibling_halves(gsends, name, sequencer=None):
    def body(g_refs, out_refs, send_sems, recv_sems):
        x, y, c = _place()
        cps = []
        for g, (g_ref, out_ref) in enumerate(zip(g_refs, out_refs)):
            half = g_ref.shape[1] // 2
            cps.append(pltpu.make_async_remote_copy(
                src_ref=g_ref.at[:, pl.ds((1 - c) * half, half), :], dst_ref=out_ref,
                send_sem=send_sems.at[g], recv_sem=recv_sems.at[g], device_id=(x, y, 1 - c), device_id_type=MESH_ID))
        _run_copies(cps)

    out_types = [jax.ShapeDtypeStruct((s.shape[0], s.shape[1] // 2, s.shape[2]), s.dtype) for s in gsends]
    return _exchange(body, gsends, out_types, len(gsends), name, sequencer and (sequencer, _sibling))


def _chip_sums(gsend, sib, place):
    n, rows, cols = gsend.shape
    half = rows // 2
    tm = _half_tile(half)
    nblk = half // tm

    def body(s_ref, g_ref, sib_ref, o_ref):
        o_ref[0] = (g_ref[0].astype(F32) + sib_ref[0].astype(F32)).astype(o_ref.dtype)

    grid_spec = pltpu.PrefetchScalarGridSpec(
        num_scalar_prefetch=1, grid=(n, nblk),
        in_specs=[pl.BlockSpec((1, tm, cols), lambda k, i, s: (jnp.bitwise_xor(s[0], k), s[1] * nblk + i, 0)),
                  pl.BlockSpec((1, tm, cols), lambda k, i, s: (jnp.bitwise_xor(s[0], k), i, 0))],
        out_specs=pl.BlockSpec((1, tm, cols), lambda k, i, s: (k, i, 0)))
    return pl.pallas_call(
        body, out_shape=jax.ShapeDtypeStruct((n, half, cols), BF16), grid_spec=grid_spec,
        name="rs_chip_sums", compiler_params=_params("parallel", "parallel"))(place, gsend, sib)


def _exchange_chip_sums(tsends, name, sequencer=None):
    def body(t_refs, out_refs, send_sems, recv_sems):
        x, y, c = _place()
        cps = []
        for g, (t_ref, out_ref) in enumerate(zip(t_refs, out_refs)):
            for k, device in enumerate(_same_core_of_other_chips(x, y, c)):
                cps.append(pltpu.make_async_remote_copy(
                    src_ref=t_ref.at[k + 1], dst_ref=out_ref.at[k], send_sem=send_sems.at[3 * g + k],
                    recv_sem=recv_sems.at[3 * g + k], device_id=device, device_id_type=MESH_ID))
        _run_copies(cps)

    out_types = [jax.ShapeDtypeStruct((3,) + s.shape[1:], s.dtype) for s in tsends]
    return _exchange(body, tsends, out_types, 3 * len(tsends), name,
                     sequencer and (sequencer, _same_core_of_other_chips))


def _final_sum(tsend, recv, place):
    n, half, cols = tsend.shape
    tm = _half_tile(half)
    nblk = half // tm

    def body(s_ref, t_ref, r_ref, o_ref):
        o_ref[...] = ((t_ref[0].astype(F32) + r_ref[0].astype(F32)) + r_ref[1].astype(F32)) + r_ref[2].astype(F32)

    grid_spec = pltpu.PrefetchScalarGridSpec(
        num_scalar_prefetch=1, grid=(nblk,),
        in_specs=[pl.BlockSpec((1, tm, cols), lambda i, s: (0, i, 0)), pl.BlockSpec((n - 1, tm, cols), lambda i, s: (0, i, 0))],
        out_specs=pl.BlockSpec((tm, cols), lambda i, s: (s[1] * nblk + i, 0)))
    return pl.pallas_call(
        body, out_shape=jax.ShapeDtypeStruct((2 * half, cols), F32), grid_spec=grid_spec, name="rs_final_sum",
        compiler_params=_params("parallel"))(place, tsend, recv)


def _join_halves(gfulls, name, sequencer=None):
    def body(g_refs, out_refs, send_sems, recv_sems):
        x, y, c = _place()
        n = len(g_refs)

        def copy(g, pc):
            half = g_refs[g].shape[0] // 2
            return pltpu.make_async_remote_copy(
                src_ref=g_refs[g].at[pl.ds(pc * half, half), :], dst_ref=out_refs[g].at[pl.ds(pc * half, half), :],
                send_sem=send_sems.at[g], recv_sem=recv_sems.at[g], device_id=(x, y, 1 - c), device_id_type=MESH_ID)

        mine = [copy(g, c) for g in range(n)]
        for cp in mine:
            cp.start()
        for g in range(n):
            copy(g, 1 - c).wait_recv()
        for cp in mine:
            cp.wait_send()

    return _exchange(body, gfulls, None, len(gfulls), name, sequencer and (sequencer, _sibling))


def _allreduce_small(v):
    rows, cols = v.shape

    def body(v_ref, out_ref, buf, send_sems, recv_sems):
        x, y, c = _place()
        cps = []
        for k in range(1, 8):
            peer = (_flip(x, (k >> 2) & 1), _flip(y, (k >> 1) & 1), _flip(c, k & 1))
            cps.append(pltpu.make_async_remote_copy(
                src_ref=v_ref, dst_ref=buf.at[k - 1], send_sem=send_sems.at[k - 1], recv_sem=recv_sems.at[k - 1],
                device_id=peer, device_id_type=MESH_ID))
        for cp in cps:
            cp.start()
        for cp in cps:
            cp.wait_recv()
        for cp in cps:
            cp.wait_send()
        t0 = v_ref[...] + buf[0]
        t1 = buf[1] + buf[2]
        t2 = buf[3] + buf[4]
        t3 = buf[5] + buf[6]
        out_ref[...] = (t0 + t1) + (t2 + t3)

    vm = pl.BlockSpec(memory_space=pltpu.VMEM)
    return pl.pallas_call(
        body, out_shape=jax.ShapeDtypeStruct((rows, cols), F32), in_specs=[vm], out_specs=vm,
        scratch_shapes=[pltpu.VMEM((7, rows, cols), F32), pltpu.SemaphoreType.DMA((7,)), pltpu.SemaphoreType.DMA((7,))],
        name="allreduce_small")(v)


BIG_INFO = {n: (shape, ax) for n, shape, ax in BIG}
GROUPS = (("w_in",), ("w_ffn_gate", "w_ffn_up", "w_ffn_down", "w_out", "w_ple_gate"),
          ("w_branch_a", "w_branch_b", "w_ple_proj"))
GATHER_GROUPS = tuple((n,) for n in GROUPS[0] + GROUPS[1]) + GROUPS[2:]


def _shard_shape(name):
    (k, m), ax = BIG_INFO[name]
    return (k // N_CHIPS, m) if ax == 0 else (k, m // N_CHIPS)


def _group_rows(group):
    offs, off = {}, 0
    for n in group:
        offs[n] = off
        off += _shard_shape(n)[0]
    return offs, off


def _pack_groups(shards, layer, dtype):
    return [jnp.concatenate([shards[n][layer].astype(dtype) for n in group], axis=0) for group in GATHER_GROUPS]


def _unpack_full(gathered, groups):
    out = {}
    for group, arr in zip(groups, gathered):
        offs, _ = _group_rows(group)
        for n in group:
            rows, cols = _shard_shape(n)
            (k, m), ax = BIG_INFO[n]
            slab = arr[:, offs[n]:offs[n] + rows]
            out[n] = slab.reshape(k, m) if ax == 0 else jnp.transpose(slab, (1, 0, 2)).reshape(k, m)
    return out


def _pack_grads(gfull):
    out = []
    for group in GROUPS:
        parts = []
        for n in group:
            rows, cols = _shard_shape(n)
            ax = BIG_INFO[n][1]
            slab = (gfull[n].reshape(N_CHIPS, rows, cols) if ax == 0
                    else jnp.transpose(gfull[n].reshape(rows, N_CHIPS, cols), (1, 0, 2)))
            parts.append(slab)
        out.append(jnp.concatenate(parts, axis=1))
    return out


def _after(values, mark):
    values, _ = lax.optimization_barrier((values, mark))
    return values


def _reduce_scatter_begin(gsends, place, tag, ids):
    sibs = _sibling_halves(gsends, "rs_sibling_halves_" + tag, ids[0])
    tsends = [_chip_sums(g, s, place) for g, s in zip(gsends, sibs)]
    return tsends, _exchange_chip_sums(tsends, "rs_exchange_" + tag, ids[1])


def _reduce_scatter_finish(begun, place, tag, ids, hold):
    tsends, recvs = begun
    recvs = _after(recvs, hold)
    return _join_halves([_final_sum(t, r, place) for t, r in zip(tsends, recvs)], "rs_join_halves_" + tag, ids[2])


SMALL_SHAPES = {"rel_table": (NUM_BUCKETS, 2 * N_HEADS), "norm_mix_g": (DEPTH, D_MODEL), "qnorm_a_g": (DEPTH, HEAD_DIM),
                "knorm_a_g": (DEPTH, HEAD_DIM), "qnorm_b_g": (DEPTH, HEAD_DIM), "knorm_b_g": (DEPTH, HEAD_DIM),
                "sink_b": (DEPTH, N_HEADS), "norm_ffn_g": (DEPTH, D_MODEL), "norm_ple_g": (DEPTH, D_MODEL)}


def _pack_small(vals, last=None):
    flat = jnp.concatenate([vals[n].astype(F32).reshape(-1) for n in SMALL])
    tail = jnp.zeros((SMALL_ROWS * LANES - flat.shape[0],), F32)
    if last is not None:
        tail = tail.at[-1].set(last)
    return jnp.concatenate([flat, tail]).reshape(SMALL_ROWS, LANES)


def _unpack_small(packed):
    flat, out, off = packed.reshape(-1), {}, 0
    for n in SMALL:
        size = math.prod(SMALL_SHAPES[n])
        out[n] = flat[off:off + size].reshape(SMALL_SHAPES[n])
        off += size
    return out


def _adamw(w, gs, g_row, m, v, name):
    c1 = 1.0 - ADAM_B1 ** ADAM_STEP
    c2 = 1.0 - ADAM_B2 ** ADAM_STEP
    total, width = w.shape
    n_layers = len(gs)
    per = total // n_layers
    tm = max(t for t in range(8, 513, 8) if per % t == 0 and g_row % t == 0)
    nblk = per // tm

    def body(*refs):
        w_ref, g_refs = refs[0], refs[1:1 + n_layers]
        m_ref, v_ref, og, od, om, ov = refs[1 + n_layers:]
        layer = pl.program_id(0) // nblk
        g = g_refs[0][...]
        for l in range(1, n_layers):
            g = jnp.where(layer == l, g_refs[l][...], g)
        m_new = ADAM_B1 * m_ref[...] + (1.0 - ADAM_B1) * g
        v_new = ADAM_B2 * v_ref[...] + (1.0 - ADAM_B2) * (g * g)
        og[...] = g
        od[...] = -ADAM_LR * ((m_new / c1) / (jnp.sqrt(v_new / c2) + ADAM_EPS) + ADAM_WD * w_ref[...])
        om[...] = m_new
        ov[...] = v_new

    row = pl.BlockSpec((tm, width), lambda i: (i, 0))
    g_specs = [pl.BlockSpec((tm, width), lambda i, l=l: (g_row // tm + jnp.clip(i - l * nblk, 0, nblk - 1), 0))
               for l in range(n_layers)]
    return pl.pallas_call(
        body, out_shape=[jax.ShapeDtypeStruct((total, width), F32)] * 4, grid=(total // tm,),
        in_specs=[row] + g_specs + [row, row], out_specs=[row] * 4, name=name,
        compiler_params=_params("parallel"))(w, *gs, m, v)


def kernel(x, p, rel_table, norm_mix_g, w_in, qnorm_a_g, knorm_a_g, qnorm_b_g, knorm_b_g, sink_b, w_branch_a, w_branch_b, w_out, norm_ffn_g, w_ffn_gate, w_ffn_up, w_ffn_down, norm_ple_g, w_ple_gate, w_ple_proj, loss_target, m_rel_table, m_norm_mix_g, m_w_in, m_qnorm_a_g, m_knorm_a_g, m_qnorm_b_g, m_knorm_b_g, m_sink_b, m_w_branch_a, m_w_branch_b, m_w_out, m_norm_ffn_g, m_w_ffn_gate, m_w_ffn_up, m_w_ffn_down, m_norm_ple_g, m_w_ple_gate, m_w_ple_proj, v_rel_table, v_norm_mix_g, v_w_in, v_qnorm_a_g, v_knorm_a_g, v_qnorm_b_g, v_knorm_b_g, v_sink_b, v_w_branch_a, v_w_branch_b, v_w_out, v_norm_ffn_g, v_w_ffn_gate, v_w_ffn_up, v_w_ffn_down, v_norm_ple_g, v_w_ple_gate, v_w_ple_proj):
    given = dict(locals())

    def held(name, a):
        return jnp.swapaxes(a, 1, 2) if name in TRANSPOSED else a

    weights = {n: held(n, given[n]) for n in WEIGHTS}
    moments_m = {n: held(n, given["m_" + n]) for n in WEIGHTS}
    moments_v = {n: held(n, given["v_" + n]) for n in WEIGHTS}
    xi, yi, ci = _place()
    place = jnp.stack([2 * xi + yi, ci]).astype(jnp.int32)

    shards = [_pack_groups(weights, l, BF16) for l in range(DEPTH)]
    w_in_flight = [_allgather(shards[0][:1], "allgather_w_in_layer0", sequencer=9), None]
    rest_flight = [None, None]
    small = {n: weights[n] for n in SMALL}

    def w_in_of(l, mark):
        landed = _after(w_in_flight[l], (shards[1], mark) if l == 0 else mark)
        rest_flight[l] = _allgather(_after(shards[l][1:], landed), f"allgather_rest_layer{l}", sequencer=(1, 2)[l])
        return _unpack_full(landed, GATHER_GROUPS[:1])["w_in"]

    def rest_of(l, mark):
        if l + 1 < DEPTH:
            w_in_flight[l + 1] = _allgather(_after(shards[l + 1][:1], mark), f"allgather_w_in_layer{l + 1}", sequencer=16)
        return _unpack_full(_after(rest_flight[l], mark), GATHER_GROUPS[1:])

    loss, dx, gbig, gsmall, marks = _local_step(x[0], p[:, 0], loss_target[0], w_in_of, rest_of, small)

    gsends = [_pack_grads(gbig[l]) for l in range(DEPTH)]
    stages = {"rest_layer1": (gsends[1][1:], (3, 4, 5)), "w_in_layer1": (gsends[1][:1], (13, 14, 15)),
              "rest_layer0": (gsends[0][1:], (6, 7, 8)), "w_in_layer0": (gsends[0][:1], (10, 11, 12))}
    begun = {tag: _reduce_scatter_begin(g, place, tag, ids) for tag, (g, ids) in stages.items()}

    def finish(tag, hold):
        return _reduce_scatter_finish(begun[tag], place, tag, stages[tag][1], hold)

    red1 = finish("w_in_layer1", dx) + finish("rest_layer1", marks[0]["attn_bwd_done"])
    rest0 = finish("rest_layer0", gbig[0]["w_in"])

    grads, delta, new_m, new_v = {}, {}, {}, {}

    def update(group, reduced):
        offs, _ = _group_rows(group)
        for n in group:
            shape = weights[n].shape
            two_d = lambda a: a.reshape(shape[0] * shape[1], shape[2])
            outs = _adamw(two_d(weights[n]), reduced, offs[n], two_d(moments_m[n]), two_d(moments_v[n]), "adamw_" + n)
            grads[n], delta[n], new_m[n], new_v[n] = (held(n, o.reshape(shape)) for o in outs)

    for gi in (1, 2):
        update(GROUPS[gi], _after([rest0[gi - 1], red1[gi]], begun["w_in_layer0"][0]))
    small_grads = _allreduce_small(_pack_small(gsmall, last=loss))
    g_, d_, m_, v_ = _adamw(_pack_small(weights), [small_grads], 0, _pack_small(moments_m), _pack_small(moments_v),
                            "adamw_small")
    grads.update(_unpack_small(g_))
    delta.update(_unpack_small(d_))
    new_m.update(_unpack_small(m_))
    new_v.update(_unpack_small(v_))
    others_done = [dx, d_] + [delta[n] for gi in (1, 2) for n in GROUPS[gi]]
    update(GROUPS[0], [finish("w_in_layer0", others_done)[0], red1[0]])

    return (small_grads[-1, -1], dx[None], *[grads[n] for n in WEIGHTS], *[delta[n] for n in WEIGHTS],
            *[new_m[n] for n in WEIGHTS], *[new_v[n] for n in WEIGHTS])
```

```python
import functools
import math

import jax
import jax.numpy as jnp
from jax import lax
from jax.experimental import pallas as pl
from jax.experimental.pallas import tpu as pltpu
from jax.experimental.pallas import tpu_sc as plsc

F32 = jnp.float32
BF16 = jnp.bfloat16
MESH_ID = pl.DeviceIdType.MESH

D_MODEL = 1024
DEPTH = 2
HEAD_DIM = 64
N_HEADS = 8
WIDTH = N_HEADS * HEAD_DIM
N_PAIRS = 4
ITEMS = 16
MAX_CHUNK = 1024
N_KV_B = 2
PLE_DIM = 256
D_FF = 2816
D_IN = 4352
OFF_QA, OFF_KA, OFF_VA, OFF_QB, OFF_KB, OFF_VB, OFF_GA, OFF_GB = 0, 512, 1024, 1536, 2048, 2176, 2304, 3328
DILATED = ((64, 1), (64, 4), (64, 16))
BLK_B = 128
NUM_BUCKETS = 32
MAX_DISTANCE = 1024
RMS_EPS = 1e-6
NEG_INF = -1e30
LANES = 128
ROW_TILE = 256
VMEM_LIMIT = 40 * 1024 * 1024

ADAM_LR, ADAM_B1, ADAM_B2, ADAM_EPS, ADAM_WD, ADAM_STEP = 0.001, 0.9, 0.999, 1e-08, 0.01, 10

TRANSPOSED = ("w_in", "w_ffn_gate", "w_ffn_up")
BIG = (
    ("w_in", (D_IN, D_MODEL), 0),
    ("w_branch_a", (WIDTH, D_MODEL), 1),
    ("w_branch_b", (WIDTH, D_MODEL), 1),
    ("w_out", (D_MODEL, D_MODEL), 0),
    ("w_ffn_gate", (D_FF, D_MODEL), 0),
    ("w_ffn_up", (D_FF, D_MODEL), 0),
    ("w_ffn_down", (D_FF, D_MODEL), 0),
    ("w_ple_gate", (D_MODEL, D_MODEL), 0),
    ("w_ple_proj", (PLE_DIM, D_MODEL), 1),
)
SMALL = ("rel_table", "norm_mix_g", "qnorm_a_g", "knorm_a_g", "qnorm_b_g", "knorm_b_g", "sink_b",
         "norm_ffn_g", "norm_ple_g")
WEIGHTS = ("rel_table", "norm_mix_g", "w_in", "qnorm_a_g", "knorm_a_g", "qnorm_b_g", "knorm_b_g", "sink_b",
           "w_branch_a", "w_branch_b", "w_out", "norm_ffn_g", "w_ffn_gate", "w_ffn_up", "w_ffn_down",
           "norm_ple_g", "w_ple_gate", "w_ple_proj")
N_CHIPS = 4
SMALL_ROWS = 64


def _params(*sem):
    return pltpu.CompilerParams(dimension_semantics=sem, vmem_limit_bytes=VMEM_LIMIT)


MM_VMEM_BUDGET = 40 * 1024 * 1024
STEP_OVERHEAD_S = 0.4e-6
TILE_DMA_BYTES_PER_S = 1.5e12


def _mm_dims(a, b, mode):
    if mode == "nn":
        return a.shape[0], b.shape[1], a.shape[1]
    if mode == "nt":
        return a.shape[0], b.shape[0], a.shape[1]
    return a.shape[1], b.shape[1], a.shape[0]


def _mm_tiles(m, n, pairs, tile_bytes, col_offsets, full_rows=False):
    best = None
    widths = [n] if full_rows else [t for t in range(LANES, n + 1, LANES) if n % t == 0 and all(o % t == 0 for o in col_offsets)]
    for tm in (t for t in range(LANES, m + 1, LANES) if m % t == 0):
        for tn in widths:
            io = sum(tm * k * ab + tn * k * bb for k, ab, bb in pairs) + tm * tn * sum(tile_bytes)
            casts = sum((tm * k * 2 if ab == 4 else 0) + (tn * k * 2 if bb == 4 else 0) for k, ab, bb in pairs)
            if 2 * io + len(pairs) * tm * tn * 4 + casts > MM_VMEM_BUDGET:
                continue
            cost = (m // tm) * (n // tn) * STEP_OVERHEAD_S + io / TILE_DMA_BYTES_PER_S
            if best is None or (cost, -tm) < best[0]:
                best = ((cost, -tm), tm, tn)
    return best[1], best[2]


def _mm_fused(pairs, extras, epilogue, out_dtypes, name, next_gain=None):
    m, n, _ = _mm_dims(*pairs[0])
    assert all(_mm_dims(*p)[:2] == (m, n) for p in pairs)
    with_norm = next_gain is not None
    out_dtypes = list(out_dtypes) + ([BF16] if with_norm else [])
    tm, tn = _mm_tiles(
        m, n, [(_mm_dims(a, b, mode)[2], a.dtype.itemsize, b.dtype.itemsize) for a, b, mode in pairs],
        [e.dtype.itemsize for e, _ in extras] + [jnp.dtype(d).itemsize for d in out_dtypes], [off for _, off in extras],
        full_rows=with_norm)
    dims = {"nn": (((1,), (0,)), ((), ())), "nt": (((1,), (1,)), ((), ())), "tn": (((0,), (0,)), ((), ()))}
    in_specs, args = [], []
    for a, b, mode in pairs:
        k = _mm_dims(a, b, mode)[2]
        in_specs.append(pl.BlockSpec((k, tm), lambda i, j: (0, i)) if mode == "tn" else pl.BlockSpec((tm, k), lambda i, j: (i, 0)))
        in_specs.append(pl.BlockSpec((tn, k), lambda i, j: (j, 0)) if mode == "nt" else pl.BlockSpec((k, tn), lambda i, j: (0, j)))
        args += [a, b]
    for e, off in extras:
        in_specs.append(pl.BlockSpec((tm, tn), lambda i, j, o=off // tn: (i, o + j)))
        args.append(e)
    n_pairs, n_tiles = len(pairs), 2 * len(pairs) + len(extras)
    if with_norm:
        in_specs.append(pl.BlockSpec((1, n), lambda i, j: (0, 0)))
        args.append(next_gain)
    n_in = len(args)

    def body(*refs):
        products = [lax.dot_general(refs[2 * p][...].astype(BF16), refs[2 * p + 1][...].astype(BF16), dims[pairs[p][2]],
                                    preferred_element_type=F32) for p in range(n_pairs)]
        outs = list(epilogue(products, [r[...] for r in refs[2 * n_pairs:n_tiles]]))
        if with_norm:
            y = outs[0]
            outs.append((y * lax.rsqrt(jnp.mean(y * y, axis=-1, keepdims=True) + RMS_EPS)) * refs[n_tiles][...])
        for r, o in zip(refs[n_in:], outs):
            r[...] = o.astype(r.dtype)

    tile = pl.BlockSpec((tm, tn), lambda i, j: (i, j))
    return pl.pallas_call(
        body, out_shape=[jax.ShapeDtypeStruct((m, n), d) for d in out_dtypes], grid=(m // tm, n // tn),
        in_specs=in_specs, out_specs=[tile] * len(out_dtypes), name=name,
        compiler_params=_params("parallel", "parallel"))(*args)


def _mm(a, b, mode, out_dtype, name, res=None):
    if res is None:
        return _mm_fused([(a, b, mode)], [], lambda products, extra: products, [out_dtype], name)[0]
    return _mm_fused([(a, b, mode)], [(res, 0)], lambda products, extra: [products[0] + extra[0]], [out_dtype], name)[0]


def _mm_res_norm(a, b, mode, res, gain, name):
    return _mm_fused([(a, b, mode)], [(res, 0)], lambda products, extra: [products[0] + extra[0]], [F32], name,
                     next_gain=gain)


def _ew(fn, ins, out_dtypes, name):
    rows, width = ins[0].shape
    n_in = len(ins)

    def body(*refs):
        outs = fn(*[r[...] for r in refs[:n_in]])
        for r, o in zip(refs[n_in:], outs):
            r[...] = o.astype(r.dtype)

    row = pl.BlockSpec((ROW_TILE, width), lambda i: (i, 0))
    return pl.pallas_call(
        body, out_shape=[jax.ShapeDtypeStruct((rows, width), dt) for dt in out_dtypes], grid=(rows // ROW_TILE,),
        in_specs=[row] * n_in, out_specs=[row] * len(out_dtypes), name=name, compiler_params=_params("parallel"))(*ins)


def _sigmoid(x):
    return 1.0 / (1.0 + jnp.exp(-x))


def _seg_sum(v):
    outs = []
    for k in range(v.shape[1] // LANES):
        vp = v[:, k * LANES:(k + 1) * LANES]
        left = lax.broadcasted_iota(jnp.int32, vp.shape, 1) < HEAD_DIM
        sl = jnp.sum(jnp.where(left, vp, 0.0), axis=-1, keepdims=True)
        sr = jnp.sum(jnp.where(left, 0.0, vp), axis=-1, keepdims=True)
        outs.append(jnp.where(left, sl, sr))
    return outs[0] if len(outs) == 1 else jnp.concatenate(outs, axis=1)


def _seg_rstd(x):
    return lax.rsqrt(_seg_sum(x * x) * (1.0 / HEAD_DIM) + RMS_EPS)


def _rms_fwd(x, g, name):
    rows, d = x.shape
    tm = ROW_TILE

    def body(x_ref, g_ref, h_ref):
        xv = x_ref[...]
        r = lax.rsqrt(jnp.mean(xv * xv, axis=-1, keepdims=True) + RMS_EPS)
        h_ref[...] = ((xv * r) * g_ref[...]).astype(BF16)

    return pl.pallas_call(
        body, out_shape=jax.ShapeDtypeStruct((rows, d), BF16), grid=(rows // tm,),
        in_specs=[pl.BlockSpec((tm, d), lambda i: (i, 0)), pl.BlockSpec((1, d), lambda i: (0, 0))],
        out_specs=pl.BlockSpec((tm, d), lambda i: (i, 0)), name=name,
        compiler_params=_params("parallel"))(x, g)


def _mm_rms_bwd(pairs, x, g, dres, name):
    rows, d = x.shape
    assert all(_mm_dims(*p)[:2] == (rows, d) for p in pairs)
    kbytes = [(_mm_dims(a, b, mode)[2], a.dtype.itemsize, b.dtype.itemsize) for a, b, mode in pairs]
    tm = max(t for t in (512, 256, 128) if rows % t == 0 and
             2 * (sum(t * k * ab + d * k * bb for k, ab, bb in kbytes) + t * d * 14) + 2 * t * d * 4 <= MM_VMEM_BUDGET)
    dims = {"nn": (((1,), (0,)), ((), ())), "nt": (((1,), (1,)), ((), ()))}
    n_pairs = len(pairs)

    def body(*refs):
        x_ref, g_ref, dres_ref = refs[2 * n_pairs:2 * n_pairs + 3]
        dx_ref, dxb_ref, dg_ref = refs[2 * n_pairs + 3:]
        dhv = functools.reduce(lambda u, v: u + v, [
            lax.dot_general(refs[2 * p][...].astype(BF16), refs[2 * p + 1][...].astype(BF16), dims[pairs[p][2]],
                            preferred_element_type=F32) for p in range(n_pairs)])
        xv = x_ref[...]
        r = lax.rsqrt(jnp.mean(xv * xv, axis=-1, keepdims=True) + RMS_EPS)
        xh = xv * r
        dxh = dhv * g_ref[...]
        dxv = dres_ref[...] + r * (dxh - xh * jnp.mean(dxh * xh, axis=-1, keepdims=True))
        dx_ref[...] = dxv
        dxb_ref[...] = dxv.astype(BF16)
        part = jnp.sum(dhv * xh, axis=0, keepdims=True)

        @pl.when(pl.program_id(0) == 0)
        def _():
            dg_ref[...] = part

        @pl.when(pl.program_id(0) > 0)
        def _():
            dg_ref[...] += part

    row = pl.BlockSpec((tm, d), lambda i: (i, 0))
    vec = pl.BlockSpec((1, d), lambda i: (0, 0))
    in_specs, args = [], []
    for a, b, mode in pairs:
        in_specs += [pl.BlockSpec((tm, a.shape[1]), lambda i: (i, 0)), pl.BlockSpec(b.shape, lambda i: (0, 0))]
        args += [a, b]
    return pl.pallas_call(
        body, out_shape=[jax.ShapeDtypeStruct((rows, d), F32), jax.ShapeDtypeStruct((rows, d), BF16),
                         jax.ShapeDtypeStruct((1, d), F32)],
        grid=(rows // tm,), in_specs=in_specs + [row, vec, row], out_specs=[row, row, vec],
        name=name, compiler_params=_params("arbitrary"))(*args, x, g, dres)


def _ple_bwd(dx, z, e):
    s = _sigmoid(z.astype(F32))
    return dx * s, dx * e.astype(F32) * (s * (1.0 - s))


def _loss_grad(y, t, z, e):
    rows, d = y.shape
    tm = ROW_TILE

    def body(y_ref, t_ref, z_ref, e_ref, dy_ref, de_ref, dz_ref, l_ref):
        err = y_ref[...] - t_ref[...]
        dy = err * (1.0 / d)
        dy_ref[...] = dy
        de, dz = _ple_bwd(dy, z_ref[...], e_ref[...])
        de_ref[...] = de.astype(BF16)
        dz_ref[...] = dz.astype(BF16)
        part = jnp.zeros((1, LANES), F32) + jnp.sum(err * err) * (0.5 / d)

        @pl.when(pl.program_id(0) == 0)
        def _():
            l_ref[...] = part

        @pl.when(pl.program_id(0) > 0)
        def _():
            l_ref[...] += part

    row = pl.BlockSpec((tm, d), lambda i: (i, 0))
    return pl.pallas_call(
        body, out_shape=[jax.ShapeDtypeStruct((rows, d), F32), jax.ShapeDtypeStruct((rows, d), BF16),
                         jax.ShapeDtypeStruct((rows, d), BF16), jax.ShapeDtypeStruct((1, LANES), F32)],
        grid=(rows // tm,), in_specs=[row] * 4, out_specs=[row, row, row, pl.BlockSpec((1, LANES), lambda i: (0, 0))],
        name="loss_grad", compiler_params=_params("arbitrary"))(y, t, z, e)


def _swap_halves(v):
    return pltpu.roll(v, HEAD_DIM, axis=1)


def _expand_kv(kv):
    left = lax.broadcasted_iota(jnp.int32, kv.shape, 1) < HEAD_DIM
    sw = _swap_halves(kv)
    h0 = jnp.where(left, kv, sw)
    h1 = jnp.where(left, sw, kv)
    return jnp.concatenate([h0, h0, h1, h1], axis=1)


def _reduce_kv(dkv):
    left = lax.broadcasted_iota(jnp.int32, (dkv.shape[0], LANES), 1) < HEAD_DIM
    t = dkv[:, 0:LANES] + dkv[:, LANES:2 * LANES]
    u = dkv[:, 2 * LANES:3 * LANES] + dkv[:, 3 * LANES:4 * LANES]
    t = t + _swap_halves(t)
    u = u + _swap_halves(u)
    return jnp.where(left, t, u)


def _qknorm_fwd(proj, gqa, gka, gqb, gkb):
    rows = proj.shape[0]
    tm = ROW_TILE

    def body(qa_ref, ka_ref, qb_ref, kb_ref, vb_ref, gqa_ref, gka_ref, gqb_ref, gkb_ref, oqa, oka, oqb, okb, ovb):
        for src, g_ref, dst in ((qa_ref, gqa_ref, oqa), (ka_ref, gka_ref, oka), (qb_ref, gqb_ref, oqb)):
            xv = src[...]
            dst[...] = (xv * _seg_rstd(xv)) * g_ref[...]
        kv = kb_ref[...]
        okb[...] = _expand_kv((kv * _seg_rstd(kv)) * gkb_ref[...])
        ovb[...] = _expand_kv(vb_ref[...])

    def win(width, off):
        return pl.BlockSpec((tm, width), lambda i: (i, off // width))

    vec = lambda w: pl.BlockSpec((1, w), lambda i: (0, 0))
    out = pl.BlockSpec((tm, WIDTH), lambda i: (i, 0))
    return pl.pallas_call(
        body, out_shape=[jax.ShapeDtypeStruct((rows, WIDTH), F32)] * 5, grid=(rows // tm,),
        in_specs=[win(WIDTH, OFF_QA), win(WIDTH, OFF_KA), win(WIDTH, OFF_QB), win(LANES, OFF_KB), win(LANES, OFF_VB),
                  vec(WIDTH), vec(WIDTH), vec(WIDTH), vec(LANES)],
        out_specs=[out] * 5, name="qknorm_fwd", compiler_params=_params("parallel"))(
            proj, proj, proj, proj, proj, gqa, gka, gqb, gkb)


def _norm_bwd(xv, g, dy):
    r = _seg_rstd(xv)
    xh = xv * r
    dxh = dy * g
    dx = r * (dxh - xh * (_seg_sum(dxh * xh) * (1.0 / HEAD_DIM)))
    return dx, jnp.sum(dy * xh, axis=0, keepdims=True)


def _qknorm_bwd(proj, gqa, gka, gqb, gkb, dqa, dka, dva, dqb, dkb, dvb, dga, dgb):
    rows = proj.shape[0]
    tm = ROW_TILE
    n_a = len(dqa)

    def body(*refs):
        qa_ref, ka_ref, qb_ref, kb_ref, gqa_ref, gka_ref, gqb_ref, gkb_ref = refs[:8]
        pos = 8
        dqa_refs, dka_refs, dva_refs = refs[pos:pos + n_a], refs[pos + n_a:pos + 2 * n_a], refs[pos + 2 * n_a:pos + 3 * n_a]
        pos += 3 * n_a
        dqb_ref, dkb_ref, dvb_ref, dga_ref, dgb_ref = refs[pos:pos + 5]
        dproj_ref, ogqa, ogka, ogqb, ogkb = refs[pos + 5:]

        def total(rs):
            acc = rs[0][...]
            for r in rs[1:]:
                acc = acc + r[...]
            return acc

        dx_qa, p_qa = _norm_bwd(qa_ref[...], gqa_ref[...], total(dqa_refs))
        dx_ka, p_ka = _norm_bwd(ka_ref[...], gka_ref[...], total(dka_refs))
        dx_qb, p_qb = _norm_bwd(qb_ref[...], gqb_ref[...], dqb_ref[...])
        dx_kb, p_kb = _norm_bwd(kb_ref[...], gkb_ref[...], _reduce_kv(dkb_ref[...]))
        dproj_ref[:, OFF_QA:OFF_QA + WIDTH] = dx_qa.astype(BF16)
        dproj_ref[:, OFF_KA:OFF_KA + WIDTH] = dx_ka.astype(BF16)
        dproj_ref[:, OFF_VA:OFF_VA + WIDTH] = total(dva_refs).astype(BF16)
        dproj_ref[:, OFF_QB:OFF_QB + WIDTH] = dx_qb.astype(BF16)
        dproj_ref[:, OFF_KB:OFF_KB + LANES] = dx_kb.astype(BF16)
        dproj_ref[:, OFF_VB:OFF_VB + LANES] = _reduce_kv(dvb_ref[...]).astype(BF16)
        dproj_ref[:, OFF_GA:OFF_GB] = dga_ref[...]
        dproj_ref[:, OFF_GB:D_IN] = dgb_ref[...]
        first = pl.program_id(0) == 0
        for o_ref, part in ((ogqa, p_qa), (ogka, p_ka), (ogqb, p_qb), (ogkb, p_kb)):
            @pl.when(first)
            def _(o_ref=o_ref, part=part):
                o_ref[...] = part

            @pl.when(jnp.logical_not(first))
            def _(o_ref=o_ref, part=part):
                o_ref[...] += part

    def win(width, off):
        return pl.BlockSpec((tm, width), lambda i: (i, off // width))

    vec = lambda w: pl.BlockSpec((1, w), lambda i: (0, 0))
    row = lambda w: pl.BlockSpec((tm, w), lambda i: (i, 0))
    in_specs = [win(WIDTH, OFF_QA), win(WIDTH, OFF_KA), win(WIDTH, OFF_QB), win(LANES, OFF_KB),
                vec(WIDTH), vec(WIDTH), vec(WIDTH), vec(LANES)]
    in_specs += [row(WIDTH)] * (3 * n_a + 3) + [row(D_MODEL)] * 2
    return pl.pallas_call(
        body,
        out_shape=[jax.ShapeDtypeStruct((rows, D_IN), BF16), jax.ShapeDtypeStruct((1, WIDTH), F32),
                   jax.ShapeDtypeStruct((1, WIDTH), F32), jax.ShapeDtypeStruct((1, WIDTH), F32),
                   jax.ShapeDtypeStruct((1, LANES), F32)],
        grid=(rows // tm,), in_specs=in_specs,
        out_specs=[row(D_IN), vec(WIDTH), vec(WIDTH), vec(WIDTH), vec(LANES)],
        name="qknorm_bwd", compiler_params=_params("arbitrary"))(
            proj, proj, proj, proj, gqa, gka, gqb, gkb, *dqa, *dka, *dva, dqb, dkb, dvb, dga, dgb)


def _t5_bucket(rel):
    half_b = NUM_BUCKETS // 2
    max_exact = half_b // 2
    sign = jnp.where(rel > 0, half_b, 0)
    n = jnp.abs(rel)
    nf = jnp.maximum(n, 1).astype(F32)
    large = max_exact + (jnp.log(nf / max_exact) / math.log(MAX_DISTANCE / max_exact)
                         * (half_b - max_exact)).astype(jnp.int32)
    large = jnp.minimum(large, half_b - 1)
    return sign + jnp.where(n < max_exact, n, large)


def _band_buckets(blk, dilation):
    i = jnp.arange(blk, dtype=jnp.int32)[:, None]
    j = jnp.arange(3 * blk, dtype=jnp.int32)[None, :]
    rel = j - blk - i
    return jnp.where(jnp.abs(rel) <= blk, _t5_bucket(rel * dilation), -1)


def _bias_tiles(table, buckets, head_off, name):
    blk = buckets.shape[0]

    def body(tab_ref, bk_ref, o_ref):
        h = pl.program_id(0) + head_off
        bk = bk_ref[...]
        acc = jnp.full(bk.shape, NEG_INF, F32)
        for b in range(NUM_BUCKETS):
            acc = jnp.where(bk == b, tab_ref[b, h], acc)
        o_ref[0] = acc

    return pl.pallas_call(
        body, out_shape=jax.ShapeDtypeStruct((N_HEADS, blk, 3 * blk), F32), grid=(N_HEADS,),
        in_specs=[pl.BlockSpec(memory_space=pltpu.SMEM), pl.BlockSpec((blk, 3 * blk), lambda h: (0, 0))],
        out_specs=pl.BlockSpec((1, blk, 3 * blk), lambda h: (h, 0, 0)),
        name=name, compiler_params=_params("parallel"))(table, buckets)


def _table_grad(dbias, buckets, name):
    blk = buckets.shape[0]

    def body(db_ref, bk_ref, o_ref):
        bk = bk_ref[...]
        dbv = db_ref[0]
        lane = lax.broadcasted_iota(jnp.int32, (1, LANES), 1)
        acc = jnp.zeros((1, LANES), F32)
        for b in range(NUM_BUCKETS):
            acc = jnp.where(lane == b, jnp.sum(jnp.where(bk == b, dbv, 0.0)), acc)
        o_ref[0] = acc

    out = pl.pallas_call(
        body, out_shape=jax.ShapeDtypeStruct((N_HEADS, 1, LANES), F32), grid=(N_HEADS,),
        in_specs=[pl.BlockSpec((1, blk, 3 * blk), lambda h: (h, 0, 0)), pl.BlockSpec((blk, 3 * blk), lambda h: (0, 0))],
        out_specs=pl.BlockSpec((1, 1, LANES), lambda h: (h, 0, 0)),
        name=name, compiler_params=_params("parallel"))(dbias, buckets)
    return out[:, 0, :NUM_BUCKETS]


def _dot_nt(a, b):
    return lax.dot_general(a, b, (((1,), (1,)), ((), ())), preferred_element_type=F32)


def _dot_tn(a, b):
    return lax.dot_general(a, b, (((0,), (0,)), ((), ())), preferred_element_type=F32)


def _stack_pair(x2, left):
    return jnp.concatenate([jnp.where(left, x2, 0.0), jnp.where(left, 0.0, x2)], axis=0).astype(BF16)


def _attn_geometry(blk, d):
    halo = blk * d
    subs = max(1, min(ITEMS // d, MAX_CHUNK // halo))
    return subs, min(d, ITEMS // subs), halo


def _item_of(j, r0, per_group):
    return j // per_group, r0 + j % per_group


def _span_rows(ref, first, r, blk, d):
    if d == 1:
        return ref[first:first + blk, :]
    return ref[pl.ds(first + r, blk, stride=d), :]


def _set_span_rows(ref, first, r, blk, d, val, add=False):
    idx = slice(first, first + blk) if d == 1 else pl.ds(first + r, blk, stride=d)
    ref[idx, :] = ref[idx, :] + val if add else val


def _for_groups(group, groups, per_group):
    if groups == 1:
        group(0)
    else:
        def step(g, carry):
            group(g * per_group)
            return carry

        lax.fori_loop(0, groups, step, 0)


def _key_rows(p_ref, c_ref, n_ref, s, r, subs, halo, blk, d):
    parts = []
    for span in (s - 1, s, s + 1):
        if span < 0:
            parts.append(_span_rows(p_ref, 0, r, blk, d))
        elif span == subs:
            parts.append(_span_rows(n_ref, 0, r, blk, d))
        else:
            parts.append(_span_rows(c_ref, span * halo, r, blk, d))
    return jnp.concatenate(parts, axis=0)


def _item_penalty(t, nct, s, subs, blk):
    first_ok = True if s > 0 else t > 0
    last_ok = True if s < subs - 1 else t < nct - 1
    col = lax.broadcasted_iota(jnp.int32, (1, 3 * blk), 1)
    ok = jnp.logical_and(jnp.logical_or(col >= blk, first_ok), jnp.logical_or(col < 2 * blk, last_ok))
    return jnp.where(ok, 0.0, NEG_INF).astype(F32)


def _attn_specs(seq, blk, d, step_of, col=0):
    subs, _, halo = _attn_geometry(blk, d)
    last, first = seq // halo - 1, col // LANES
    cur = pl.BlockSpec((subs * halo, LANES), lambda hp, t: (step_of(t), first + hp))
    prev = pl.BlockSpec((halo, LANES), lambda hp, t: (jnp.clip(step_of(t) * subs - 1, 0, last), first + hp))
    nxt = pl.BlockSpec((halo, LANES), lambda hp, t: (jnp.minimum((step_of(t) + 1) * subs, last), first + hp))
    return cur, prev, nxt


def _attn_fwd(q, k, v, bias, sink, blk, d, name, v_col=0):
    seq = q.shape[0]
    subs, per_group, halo = _attn_geometry(blk, d)
    items, chunk, groups = subs * per_group, subs * halo, d // per_group
    nct = seq // chunk
    has_sink = sink is not None
    scale = HEAD_DIM ** -0.5

    def body(*refs):
        q_ref, kp, kc, kn, vp, vc, vn, b_ref = refs[:8]
        s_ref = refs[8] if has_sink else None
        o_ref, l_ref = refs[-2], refs[-1]
        t = pl.program_id(1)
        left = lax.broadcasted_iota(jnp.int32, (1, LANES), 1) < HEAD_DIM
        bias2 = b_ref[...]

        def group(r0):
            scores, vcats = [], []
            for j in range(items):
                s, r = _item_of(j, r0, per_group)
                qs = _stack_pair(_span_rows(q_ref, s * halo, r, blk, d) * scale, left)
                kcat = _key_rows(kp, kc, kn, s, r, subs, halo, blk, d).astype(BF16)
                scores.append(_dot_nt(qs, kcat) + bias2 + _item_penalty(t, nct, s, subs, blk))
                vcats.append(_key_rows(vp, vc, vn, s, r, subs, halo, blk, d).astype(BF16))
            ms = [jnp.max(s, axis=-1, keepdims=True) for s in scores]
            if has_sink:
                sk = s_ref[...]
                ms = [jnp.maximum(m, sk) for m in ms]
            ps = [jnp.exp(s - m) for s, m in zip(scores, ms)]
            dens = [jnp.sum(p, axis=-1, keepdims=True) for p in ps]
            if has_sink:
                dens = [den + jnp.exp(sk - m) for den, m in zip(dens, ms)]
            pns = [(p * (1.0 / den)).astype(BF16) for p, den in zip(ps, dens)]
            lses = [m + jnp.log(den) for m, den in zip(ms, dens)]
            for j in range(items):
                s, r = _item_of(j, r0, per_group)
                o2 = jnp.dot(pns[j], vcats[j], preferred_element_type=F32)
                _set_span_rows(o_ref, s * halo, r, blk, d, jnp.where(left, o2[:blk], o2[blk:]))
                _set_span_rows(l_ref, s * halo, r, blk, d, jnp.where(left, lses[j][:blk], lses[j][blk:]))

        _for_groups(group, groups, per_group)

    cur, prev, nxt = _attn_specs(seq, blk, d, lambda t: t)
    v_cur, v_prev, v_nxt = _attn_specs(seq, blk, d, lambda t: t, v_col)
    in_specs = [cur, prev, cur, nxt, v_prev, v_cur, v_nxt, pl.BlockSpec((2 * blk, 3 * blk), lambda hp, t: (hp, 0))]
    args = [q, k, k, k, v, v, v, bias]
    if has_sink:
        in_specs.append(pl.BlockSpec((2 * blk, 1), lambda hp, t: (hp, 0)))
        args.append(sink)
    return pl.pallas_call(
        body, out_shape=[jax.ShapeDtypeStruct((seq, WIDTH), F32)] * 2, grid=(N_PAIRS, nct),
        in_specs=in_specs, out_specs=[cur, cur], name=name,
        compiler_params=_params("parallel", "parallel"))(*args)


def _attn_bwd(q, k, v, do, lse, delta, bias, sink, blk, d, name, v_col=0):
    seq = q.shape[0]
    subs, per_group, halo = _attn_geometry(blk, d)
    items, chunk, groups = subs * per_group, subs * halo, d // per_group
    nct = seq // chunk
    has_sink = sink is not None
    n_in = 12 if has_sink else 11
    scale = HEAD_DIM ** -0.5

    def body(*refs):
        q_ref, kp, kc, kn, vp, vc, vn, do_ref, l_ref, d_ref, b_ref = refs[:11]
        s_ref = refs[11] if has_sink else None
        dq_ref, dk_ref, dv_ref, db_ref = refs[n_in:n_in + 4]
        ds_ref = refs[n_in + 4] if has_sink else None
        wk, wv = refs[-2], refs[-1]
        t = pl.program_id(1)

        @pl.when(t == 0)
        def _():
            wk[...] = jnp.zeros_like(wk)
            wv[...] = jnp.zeros_like(wv)
            db_ref[...] = jnp.zeros_like(db_ref)
            if has_sink:
                ds_ref[...] = jnp.zeros_like(ds_ref)

        @pl.when(t > 0)
        def _():
            for w in (wk, wv):
                keep = w[chunk:2 * chunk + halo]
                w[0:chunk + halo] = keep
                w[chunk + halo:2 * chunk + halo] = jnp.zeros((chunk, LANES), F32)

        @pl.when(t < nct)
        def _():
            lane = lax.broadcasted_iota(jnp.int32, (1, LANES), 1)
            left = lane < HEAD_DIM
            bias2 = b_ref[...]

            def group(r0):
                qss, doss, kcats, scores, dps, lcols, dcols = [], [], [], [], [], [], []
                for j in range(items):
                    s, r = _item_of(j, r0, per_group)
                    qs = _stack_pair(_span_rows(q_ref, s * halo, r, blk, d) * scale, left)
                    dos = _stack_pair(_span_rows(do_ref, s * halo, r, blk, d), left)
                    kcat = _key_rows(kp, kc, kn, s, r, subs, halo, blk, d).astype(BF16)
                    vcat = _key_rows(vp, vc, vn, s, r, subs, halo, blk, d).astype(BF16)
                    l2, d2 = _span_rows(l_ref, s * halo, r, blk, d), _span_rows(d_ref, s * halo, r, blk, d)
                    lcols.append(jnp.concatenate([jnp.max(jnp.where(left, l2, NEG_INF), axis=-1, keepdims=True),
                                                  jnp.max(jnp.where(left, NEG_INF, l2), axis=-1, keepdims=True)], axis=0))
                    dcols.append(jnp.concatenate([jnp.sum(jnp.where(lane == 0, d2, 0.0), axis=-1, keepdims=True),
                                                  jnp.sum(jnp.where(lane == HEAD_DIM, d2, 0.0), axis=-1, keepdims=True)],
                                                 axis=0))
                    scores.append(_dot_nt(qs, kcat) + bias2 + _item_penalty(t, nct, s, subs, blk))
                    dps.append(_dot_nt(dos, vcat))
                    qss.append(qs)
                    doss.append(dos)
                    kcats.append(kcat)
                ps = [jnp.exp(s - lc) for s, lc in zip(scores, lcols)]
                dss = [p * (dp - dc) for p, dp, dc in zip(ps, dps, dcols)]
                db_ref[...] += functools.reduce(lambda a, b: a + b, dss)
                if has_sink:
                    sk = s_ref[...]
                    ds_ref[...] -= functools.reduce(lambda a, b: a + b, [dc * jnp.exp(sk - lc) for dc, lc in zip(dcols, lcols)])
                dsbs = [ds.astype(BF16) for ds in dss]
                for j in range(items):
                    s, r = _item_of(j, r0, per_group)
                    dq2 = jnp.dot(dsbs[j], kcats[j], preferred_element_type=F32) * scale
                    _set_span_rows(dq_ref, s * halo, r, blk, d, jnp.where(left, dq2[:blk], dq2[blk:]))
                news = [(_dot_tn(dsbs[j], qss[j]), _dot_tn(ps[j].astype(BF16), doss[j])) for j in range(items)]
                for which, w in enumerate((wk, wv)):
                    for jr in range(per_group):
                        for sp in range(-1, subs + 1):
                            parts = [news[s * per_group + jr][which][(sp - s + 1) * blk:(sp - s + 2) * blk]
                                     for s in range(subs) if 0 <= sp - s + 1 < 3]
                            _set_span_rows(w, chunk + sp * halo, r0 + jr, blk, d,
                                           functools.reduce(lambda x, y: x + y, parts), add=True)

            _for_groups(group, groups, per_group)

        dk_ref[...] = wk[0:chunk]
        dv_ref[...] = wv[0:chunk]

    cur, prev, nxt = _attn_specs(seq, blk, d, lambda t: jnp.minimum(t, nct - 1))
    v_cur, v_prev, v_nxt = _attn_specs(seq, blk, d, lambda t: jnp.minimum(t, nct - 1), v_col)
    lag = pl.BlockSpec((chunk, LANES), lambda hp, t: (jnp.maximum(t - 1, 0), hp))
    band = pl.BlockSpec((2 * blk, 3 * blk), lambda hp, t: (hp, 0))
    col = pl.BlockSpec((2 * blk, 1), lambda hp, t: (hp, 0))
    in_specs = [cur, prev, cur, nxt, v_prev, v_cur, v_nxt, cur, cur, cur, band]
    args = [q, k, k, k, v, v, v, do, lse, delta, bias]
    out_shape = [jax.ShapeDtypeStruct((seq, WIDTH), F32)] * 3 + [jax.ShapeDtypeStruct((N_HEADS * blk, 3 * blk), F32)]
    out_specs = [cur, lag, lag, band]
    if has_sink:
        in_specs.append(col)
        args.append(sink)
        out_shape.append(jax.ShapeDtypeStruct((N_HEADS * blk, 1), F32))
        out_specs.append(col)
    window = pltpu.VMEM((2 * chunk + halo, LANES), F32)
    return pl.pallas_call(
        body, out_shape=out_shape, grid=(N_PAIRS, nct + 1), in_specs=in_specs, out_specs=out_specs,
        scratch_shapes=[window, window], name=name,
        compiler_params=_params("arbitrary", "arbitrary"))(*args)


def _combine_patterns(outs, lses):
    def combine(*tiles):
        os_, ls = tiles[:len(outs)], tiles[len(outs):]
        m = functools.reduce(jnp.maximum, ls)
        es = [jnp.exp(l - m) for l in ls]
        den = functools.reduce(lambda a, b: a + b, es)
        num = functools.reduce(lambda a, b: a + b, [e * o for e, o in zip(es, os_)])
        return num / den, m + jnp.log(den)

    return _ew(combine, [*outs, *lses], [F32, F32], "combine_a")


def _tile_gain(g, reps):
    return jnp.tile(g[None, :], (1, reps))


def _local_step(x, p, target, w_in_of, rest_of, small):
    rel_table = small["rel_table"]
    buckets_a = [_band_buckets(blk, d) for blk, d in DILATED]
    buckets_b = _band_buckets(BLK_B, 1)
    bias_a = [_bias_tiles(rel_table, bk, 0, "bias_a").reshape(N_HEADS * bk.shape[0], -1) for bk in buckets_a]
    bias_b = _bias_tiles(rel_table, buckets_b, N_HEADS, "bias_b").reshape(N_HEADS * BLK_B, -1)

    saved = []
    for l in range(DEPTH):
        g_mix, g_ffn, g_ple = (small[n][l][None, :] for n in ("norm_mix_g", "norm_ffn_g", "norm_ple_g"))
        gqa, gka, gqb = (_tile_gain(small[n][l], N_HEADS) for n in ("qnorm_a_g", "knorm_a_g", "qnorm_b_g"))
        gkb = _tile_gain(small["knorm_b_g"][l], N_KV_B)
        sink = jnp.repeat(small["sink_b"][l], BLK_B)[:, None]

        h = _rms_fwd(x, g_mix, "rms_mix") if l == 0 else h_next
        w_in = w_in_of(l, (h, bias_a, bias_b)) if l == 0 else w_in_next
        proj = _mm(h, w_in, "nt", F32, "mm_in")
        qa, ka, qb, kb, vb = _qknorm_fwd(proj, gqa, gka, gqb, gkb)
        outs, lses = [], []
        for (blk, d), bias in zip(DILATED, bias_a):
            o, ls = _attn_fwd(qa, ka, proj, bias, None, blk, d, f"attn_a{d}_fwd", v_col=OFF_VA)
            outs.append(o)
            lses.append(ls)
        ya, lse_a = _combine_patterns(outs, lses)
        yb, lse_b = _attn_fwd(qb, kb, vb, bias_b, sink, BLK_B, 1, "attn_b_fwd")
        w = dict(rest_of(l, yb), w_in=w_in)
        def gate(products, extra):
            (ca_, cb_), (ga_, gb_) = products, extra
            return _sigmoid(ga_) * ca_ + _sigmoid(gb_) * cb_, ca_, cb_

        merged, ca, cb = _mm_fused([(ya, w["w_branch_a"], "nn"), (yb, w["w_branch_b"], "nn")],
                                   [(proj, OFF_GA), (proj, OFF_GB)], gate, [BF16, BF16, BF16], "mm_branches_gate")
        x1, h2 = _mm_res_norm(merged, w["w_out"], "nn", x, g_ffn, "mm_out_norm")

        def swiglu(products, extra):
            a_, u_ = products
            return (a_ * _sigmoid(a_)) * u_, a_, u_

        hid, a, u = _mm_fused([(h2, w["w_ffn_gate"], "nt"), (h2, w["w_ffn_up"], "nt")], [], swiglu,
                              [BF16, BF16, BF16], "mm_ffn_gate_up")
        if l + 1 < DEPTH:
            w_in_next = w_in_of(l + 1, hid)
        x2, h3 = _mm_res_norm(hid, w["w_ffn_down"], "nn", x1, g_ple, "mm_ffn_down_norm")

        def ple(products, extra):
            z_, e_ = products
            return extra[0] + _sigmoid(z_) * e_, z_, e_

        next_gain = small["norm_mix_g"][l + 1][None, :] if l + 1 < DEPTH else None
        x3, z, e, *rest = _mm_fused([(h3, w["w_ple_gate"], "nn"), (p[l], w["w_ple_proj"], "nn")], [(x2, 0)], ple,
                                    [F32, BF16, BF16], "mm_ple", next_gain=next_gain)
        h_next = rest[0] if rest else None
        saved.append(dict(w=w, x0=x, h=h, proj=proj, qa=qa, ka=ka, qb=qb, kb=kb, vb=vb, ya=ya, lse_a=lse_a,
                          yb=yb, lse_b=lse_b, ca=ca, cb=cb, merged=merged, x1=x1, h2=h2, a=a, u=u, hid=hid,
                          x2=x2, h3=h3, z=z, e=e))
        x = x3

    dx, de, dz, loss_acc = _loss_grad(x, target, saved[-1]["z"], saved[-1]["e"])
    loss = loss_acc[0, 0]

    gbig = [{} for _ in range(DEPTH)]
    marks = [{} for _ in range(DEPTH)]
    gsmall = {n: [None] * DEPTH for n in SMALL if n != "rel_table"}
    dbias_a = [[] for _ in DILATED]
    dbias_b = []

    for l in reversed(range(DEPTH)):
        sv = saved[l]
        w = sv["w"]
        g_mix, g_ffn, g_ple = (small[n][l][None, :] for n in ("norm_mix_g", "norm_ffn_g", "norm_ple_g"))
        gqa, gka, gqb = (_tile_gain(small[n][l], N_HEADS) for n in ("qnorm_a_g", "knorm_a_g", "qnorm_b_g"))
        gkb = _tile_gain(small["knorm_b_g"][l], N_KV_B)
        sink = jnp.repeat(small["sink_b"][l], BLK_B)[:, None]

        if l < DEPTH - 1:
            de, dz = _ew(_ple_bwd, [dx, sv["z"], sv["e"]], [BF16, BF16], "ple_bwd")
        gbig[l]["w_ple_proj"] = _mm(p[l], de, "tn", BF16, "mm_d_ple_proj")
        gbig[l]["w_ple_gate"] = _mm(sv["h3"], dz, "tn", BF16, "mm_d_ple_gate")
        dx, dxb, gsmall["norm_ple_g"][l] = _mm_rms_bwd([(dz, w["w_ple_gate"], "nt")], sv["x2"], g_ple, dx,
                                                      "mm_dh3_rms_bwd")

        gbig[l]["w_ffn_down"] = _mm(sv["hid"], dxb, "tn", BF16, "mm_d_ffn_down")

        def swiglu_bwd(products, extra):
            dh_, a_, u_ = products[0], extra[0].astype(F32), extra[1].astype(F32)
            s = _sigmoid(a_)
            return dh_ * u_ * (s * (1.0 + a_ * (1.0 - s))), dh_ * (a_ * s)

        da, du = _mm_fused([(dxb, w["w_ffn_down"], "nt")], [(sv["a"], 0), (sv["u"], 0)], swiglu_bwd, [BF16, BF16],
                           "mm_dhid_swiglu_bwd")
        gbig[l]["w_ffn_gate"], gbig[l]["w_ffn_up"] = _mm_fused(
            [(da, sv["h2"], "tn"), (du, sv["h2"], "tn")], [], lambda products, extra: products, [BF16, BF16],
            "mm_d_ffn_gate_up")
        dx, dxb, gsmall["norm_ffn_g"][l] = _mm_rms_bwd([(da, w["w_ffn_gate"], "nn"), (du, w["w_ffn_up"], "nn")],
                                                      sv["x1"], g_ffn, dx, "mm_dh2_rms_bwd")

        gbig[l]["w_out"] = _mm(sv["merged"], dxb, "tn", BF16, "mm_d_out")

        def gate_bwd(products, extra):
            dm_, ca_, cb_ = products[0], extra[0].astype(F32), extra[1].astype(F32)
            sa, sb = _sigmoid(extra[2]), _sigmoid(extra[3])
            return dm_ * sa, dm_ * sb, dm_ * ca_ * (sa * (1.0 - sa)), dm_ * cb_ * (sb * (1.0 - sb))

        dca, dcb, dga, dgb = _mm_fused(
            [(dxb, w["w_out"], "nt")], [(sv["ca"], 0), (sv["cb"], 0), (sv["proj"], OFF_GA), (sv["proj"], OFF_GB)],
            gate_bwd, [BF16, BF16, BF16, BF16], "mm_dmerged_gate_bwd")
        gbig[l]["w_branch_a"], gbig[l]["w_branch_b"] = _mm_fused(
            [(sv["ya"], dca, "tn"), (sv["yb"], dcb, "tn")], [], lambda products, extra: products, [BF16, BF16],
            "mm_d_branches")

        def with_row_dots(products, extra):
            (dya_, dyb_), (ya_, yb_) = products, extra
            return dya_, _seg_sum(dya_ * ya_), dyb_, _seg_sum(dyb_ * yb_)

        dya, delta_a, dyb, delta_b = _mm_fused(
            [(dca, w["w_branch_a"], "nt"), (dcb, w["w_branch_b"], "nt")], [(sv["ya"], 0), (sv["yb"], 0)], with_row_dots,
            [F32, F32, F32, F32], "mm_dy_branches")

        dqa, dka, dva = [], [], []
        for (blk, d), bias, bk in zip(DILATED, bias_a, buckets_a):
            dq_, dk_, dv_, db_ = _attn_bwd(sv["qa"], sv["ka"], sv["proj"], dya, sv["lse_a"], delta_a, bias, None, blk, d,
                                           f"attn_a{d}_bwd", v_col=OFF_VA)
            dqa.append(dq_)
            dka.append(dk_)
            dva.append(dv_)
            dbias_a[len(dqa) - 1].append(db_)
        dqb, dkb, dvb, db_, dsink = _attn_bwd(sv["qb"], sv["kb"], sv["vb"], dyb, sv["lse_b"], delta_b, bias_b, sink,
                                              BLK_B, 1, "attn_b_bwd")
        dbias_b.append(db_)
        gsmall["sink_b"][l] = dsink.reshape(N_HEADS, BLK_B).sum(axis=1)

        dproj, pqa, pka, pqb, pkb = _qknorm_bwd(sv["proj"], gqa, gka, gqb, gkb, dqa, dka, dva, dqb, dkb, dvb, dga, dgb)
        marks[l]["attn_bwd_done"] = dproj
        gsmall["qnorm_a_g"][l] = pqa.reshape(N_HEADS, HEAD_DIM).sum(0)
        gsmall["knorm_a_g"][l] = pka.reshape(N_HEADS, HEAD_DIM).sum(0)
        gsmall["qnorm_b_g"][l] = pqb.reshape(N_HEADS, HEAD_DIM).sum(0)
        gsmall["knorm_b_g"][l] = pkb.reshape(N_KV_B, HEAD_DIM).sum(0)
        gbig[l]["w_in"] = _mm(dproj, sv["h"], "tn", BF16, "mm_d_in")
        dx, _, gsmall["norm_mix_g"][l] = _mm_rms_bwd([(dproj, w["w_in"], "nn")], sv["x0"], g_mix, dx, "mm_dh_rms_bwd")
        gsmall["norm_mix_g"][l] = gsmall["norm_mix_g"][l][0]
        gsmall["norm_ffn_g"][l] = gsmall["norm_ffn_g"][l][0]
        gsmall["norm_ple_g"][l] = gsmall["norm_ple_g"][l][0]

    gsmall = {n: jnp.stack(v) for n, v in gsmall.items()}
    dtable_a = sum(_table_grad(sum(dbs).reshape(N_HEADS, blk, 3 * blk), bk, "table_grad_a")
                   for dbs, (blk, _), bk in zip(dbias_a, DILATED, buckets_a))
    dtable_b = _table_grad(sum(dbias_b).reshape(N_HEADS, BLK_B, 3 * BLK_B), buckets_b, "table_grad_b")
    gsmall["rel_table"] = jnp.concatenate([dtable_a, dtable_b], axis=0).T
    return loss, dx, gbig, gsmall, marks


def _place():
    return lax.axis_index("x"), lax.axis_index("y"), lax.axis_index("c")


def _flip(v, bit):
    return 1 - v if bit else v


CHIP_RELATIONS = ((0, 1), (1, 0), (1, 1))
ANY = pl.BlockSpec(memory_space=pl.ANY)


def _allgather_body(w_refs, out_refs, send_sems, recv_sems):
    x, y, c = _place()
    chips = [(_flip(x, a), _flip(y, b)) for a, b in CHIP_RELATIONS]

    def make(g):
        w_ref, out_ref = w_refs[g], out_refs[g]
        half = w_ref.shape[0] // 2

        def part(px, py, pc):
            return out_ref.at[2 * px + py, pl.ds(pc * half, half), :]

        def copy(k, block, to, src=None):
            return pltpu.make_async_remote_copy(
                src_ref=part(*block) if src is None else src, dst_ref=part(*block),
                send_sem=send_sems.at[7 * g + k], recv_sem=recv_sems.at[7 * g + k], device_id=to,
                device_id_type=MESH_ID)

        own = pltpu.make_async_remote_copy(
            src_ref=w_ref, dst_ref=out_ref.at[2 * x + y], send_sem=send_sems.at[7 * g + 6],
            recv_sem=recv_sems.at[7 * g + 6], device_id=(x, y, 1 - c), device_id_type=MESH_ID)
        first = [copy(k, (x, y, c), (*chip, c), src=w_ref.at[pl.ds(c * half, half), :]) for k, chip in enumerate(chips)]
        passed = [copy(3 + k, (*chip, c), (x, y, 1 - c)) for k, chip in enumerate(chips)]
        arrive = [copy(k, (*chip, c), (x, y, c)) for k, chip in enumerate(chips)]
        arrive2 = [copy(3 + k, (*chip, 1 - c), (x, y, c)) for k, chip in enumerate(chips)]
        return own, first, passed, arrive, arrive2

    made = [make(g) for g in range(len(w_refs))]
    for own, first, _, _, _ in made:
        own.start()
        for cp in first:
            cp.start()
    for _, _, passed, arrive, _ in made:
        for k in range(3):
            arrive[k].wait_recv()
            passed[k].start()
    for own, first, passed, _, arrive2 in made:
        for k in range(3):
            arrive2[k].wait_recv()
        own.wait_recv()
        for cp in first + passed + [own]:
            cp.wait_send()


def _sibling(x, y, c):
    return [(x, y, 1 - c)]


def _same_core_of_other_chips(x, y, c):
    return [(_flip(x, a), _flip(y, b), c) for a, b in CHIP_RELATIONS]


def _exchange(body, ins, out_types, n_sems, name, sequencer=None):
    n = len(ins)
    sems = (pltpu.SemaphoreType.DMA((n_sems,)), pltpu.SemaphoreType.DMA((n_sems,)))
    if sequencer is None:
        in_place = out_types is None
        out_shape = [jax.ShapeDtypeStruct(a.shape, a.dtype) for a in ins] if in_place else out_types

        def tc_body(*refs):
            body(refs[:n], refs[n:n + len(out_shape)], refs[-2], refs[-1])

        return list(pl.pallas_call(
            tc_body, out_shape=out_shape, in_specs=[ANY] * n, out_specs=[ANY] * len(out_shape),
            input_output_aliases={g: g for g in range(n)} if in_place else {}, scratch_shapes=list(sems), name=name)(*ins))

    collective_id, peers = sequencer
    hbm = pltpu.MemorySpace.HBM
    in_refs = [jax.new_ref(a, memory_space=hbm) for a in ins]
    out_refs = in_refs if out_types is None else [jax.empty_ref(t, memory_space=hbm) for t in out_types]

    @pl.kernel(mesh=plsc.ScalarSubcoreMesh(axis_name="sequencer", num_cores=1), name=name, scratch_types=sems,
               compiler_params=pltpu.CompilerParams(collective_id=collective_id))
    def launch(send_sems, recv_sems):
        barrier = pltpu.get_barrier_semaphore()
        devices = peers(*_place())
        for device in devices:
            pl.semaphore_signal(barrier, inc=1, device_id=device, device_id_type=MESH_ID)
        pl.semaphore_wait(barrier, len(devices))
        body(in_refs, out_refs, send_sems, recv_sems)

    launch()
    return [r[...] for r in out_refs]


def _allgather(shards, name, sequencer=None):
    out_types = [jax.ShapeDtypeStruct((N_CHIPS,) + s.shape, s.dtype) for s in shards]
    if sequencer is not None:
        sequencer = (sequencer, lambda x, y, c: _sibling(x, y, c) + _same_core_of_other_chips(x, y, c))
    return _exchange(_allgather_body, shards, out_types, 7 * len(shards), name, sequencer)


def _half_tile(half):
    return max(t for t in range(16, 1025, 16) if half % t == 0)


def _run_copies(cps):
    for cp in cps:
        cp.start()
    for cp in cps:
        cp.wait_recv()
    for cp in cps:
        cp.wait_send()


def _sibling_halves(gsends, name, sequencer=None):
    def body(g_refs, out_refs, send_sems, recv_sems):
        x, y, c = _place()
        cps = []
        for g, (g_ref, out_ref) in enumerate(zip(g_refs, out_refs)):
            half = g_ref.shape[1] // 2
            cps.append(pltpu.make_async_remote_copy(
                src_ref=g_ref.at[:, pl.ds((1 - c) * half, half), :], dst_ref=out_ref,
                send_sem=send_sems.at[g], recv_sem=recv_sems.at[g], device_id=(x, y, 1 - c), device_id_type=MESH_ID))
        _run_copies(cps)

    out_types = [jax.ShapeDtypeStruct((s.shape[0], s.shape[1] // 2, s.shape[2]), s.dtype) for s in gsends]
    return _exchange(body, gsends, out_types, len(gsends), name, sequencer and (sequencer, _sibling))


def _chip_sums(gsend, sib, place):
    n, rows, cols = gsend.shape
    half = rows // 2
    tm = _half_tile(half)
    nblk = half // tm

    def body(s_ref, g_ref, sib_ref, o_ref):
        o_ref[0] = (g_ref[0].astype(F32) + sib_ref[0].astype(F32)).astype(o_ref.dtype)

    grid_spec = pltpu.PrefetchScalarGridSpec(
        num_scalar_prefetch=1, grid=(n, nblk),
        in_specs=[pl.BlockSpec((1, tm, cols), lambda k, i, s: (jnp.bitwise_xor(s[0], k), s[1] * nblk + i, 0)),
                  pl.BlockSpec((1, tm, cols), lambda k, i, s: (jnp.bitwise_xor(s[0], k), i, 0))],
        out_specs=pl.BlockSpec((1, tm, cols), lambda k, i, s: (k, i, 0)))
    return pl.pallas_call(
        body, out_shape=jax.ShapeDtypeStruct((n, half, cols), BF16), grid_spec=grid_spec,
        name="rs_chip_sums", compiler_params=_params("parallel", "parallel"))(place, gsend, sib)


def _exchange_chip_sums(tsends, name, sequencer=None):
    def body(t_refs, out_refs, send_sems, recv_sems):
        x, y, c = _place()
        cps = []
        for g, (t_ref, out_ref) in enumerate(zip(t_refs, out_refs)):
            for k, device in enumerate(_same_core_of_other_chips(x, y, c)):
                cps.append(pltpu.make_async_remote_copy(
                    src_ref=t_ref.at[k + 1], dst_ref=out_ref.at[k], send_sem=send_sems.at[3 * g + k],
                    recv_sem=recv_sems.at[3 * g + k], device_id=device, device_id_type=MESH_ID))
        _run_copies(cps)

    out_types = [jax.ShapeDtypeStruct((3,) + s.shape[1:], s.dtype) for s in tsends]
    return _exchange(body, tsends, out_types, 3 * len(tsends), name,
                     sequencer and (sequencer, _same_core_of_other_chips))


def _final_sum(tsend, recv, place):
    n, half, cols = tsend.shape
    tm = _half_tile(half)
    nblk = half // tm

    def body(s_ref, t_ref, r_ref, o_ref):
        o_ref[...] = ((t_ref[0].astype(F32) + r_ref[0].astype(F32)) + r_ref[1].astype(F32)) + r_ref[2].astype(F32)

    grid_spec = pltpu.PrefetchScalarGridSpec(
        num_scalar_prefetch=1, grid=(nblk,),
        in_specs=[pl.BlockSpec((1, tm, cols), lambda i, s: (0, i, 0)), pl.BlockSpec((n - 1, tm, cols), lambda i, s: (0, i, 0))],
        out_specs=pl.BlockSpec((tm, cols), lambda i, s: (s[1] * nblk + i, 0)))
    return pl.pallas_call(
        body, out_shape=jax.ShapeDtypeStruct((2 * half, cols), F32), grid_spec=grid_spec, name="rs_final_sum",
        compiler_params=_params("parallel"))(place, tsend, recv)


def _join_halves(gfulls, name, sequencer=None):
    def body(g_refs, out_refs, send_sems, recv_sems):
        x, y, c = _place()
        n = len(g_refs)

        def copy(g, pc):
            half = g_refs[g].shape[0] // 2
            return pltpu.make_async_remote_copy(
                src_ref=g_refs[g].at[pl.ds(pc * half, half), :], dst_ref=out_refs[g].at[pl.ds(pc * half, half), :],
                send_sem=send_sems.at[g], recv_sem=recv_sems.at[g], device_id=(x, y, 1 - c), device_id_type=MESH_ID)

        mine = [copy(g, c) for g in range(n)]
        for cp in mine:
            cp.start()
        for g in range(n):
            copy(g, 1 - c).wait_recv()
        for cp in mine:
            cp.wait_send()

    return _exchange(body, gfulls, None, len(gfulls), name, sequencer and (sequencer, _sibling))


def _allreduce_small(v):
    rows, cols = v.shape

    def body(v_ref, out_ref, buf, send_sems, recv_sems):
        x, y, c = _place()
        cps = []
        for k in range(1, 8):
            peer = (_flip(x, (k >> 2) & 1), _flip(y, (k >> 1) & 1), _flip(c, k & 1))
            cps.append(pltpu.make_async_remote_copy(
                src_ref=v_ref, dst_ref=buf.at[k - 1], send_sem=send_sems.at[k - 1], recv_sem=recv_sems.at[k - 1],
                device_id=peer, device_id_type=MESH_ID))
        for cp in cps:
            cp.start()
        for cp in cps:
            cp.wait_recv()
        for cp in cps:
            cp.wait_send()
        t0 = v_ref[...] + buf[0]
        t1 = buf[1] + buf[2]
        t2 = buf[3] + buf[4]
        t3 = buf[5] + buf[6]
        out_ref[...] = (t0 + t1) + (t2 + t3)

    vm = pl.BlockSpec(memory_space=pltpu.VMEM)
    return pl.pallas_call(
        body, out_shape=jax.ShapeDtypeStruct((rows, cols), F32), in_specs=[vm], out_specs=vm,
        scratch_shapes=[pltpu.VMEM((7, rows, cols), F32), pltpu.SemaphoreType.DMA((7,)), pltpu.SemaphoreType.DMA((7,))],
        name="allreduce_small")(v)


BIG_INFO = {n: (shape, ax) for n, shape, ax in BIG}
GROUPS = (("w_in",), ("w_ffn_gate", "w_ffn_up", "w_ffn_down", "w_out", "w_ple_gate"),
          ("w_branch_a", "w_branch_b", "w_ple_proj"))
GATHER_GROUPS = tuple((n,) for n in GROUPS[0] + GROUPS[1]) + GROUPS[2:]


def _shard_shape(name):
    (k, m), ax = BIG_INFO[name]
    return (k // N_CHIPS, m) if ax == 0 else (k, m // N_CHIPS)


def _group_rows(group):
    offs, off = {}, 0
    for n in group:
        offs[n] = off
        off += _shard_shape(n)[0]
    return offs, off


def _pack_groups(shards, layer, dtype):
    return [jnp.concatenate([shards[n][layer].astype(dtype) for n in group], axis=0) for group in GATHER_GROUPS]


def _unpack_full(gathered, groups):
    out = {}
    for group, arr in zip(groups, gathered):
        offs, _ = _group_rows(group)
        for n in group:
            rows, cols = _shard_shape(n)
            (k, m), ax = BIG_INFO[n]
            slab = arr[:, offs[n]:offs[n] + rows]
            out[n] = slab.reshape(k, m) if ax == 0 else jnp.transpose(slab, (1, 0, 2)).reshape(k, m)
    return out


def _pack_grads(gfull):
    out = []
    for group in GROUPS:
        parts = []
        for n in group:
            rows, cols = _shard_shape(n)
            ax = BIG_INFO[n][1]
            slab = (gfull[n].reshape(N_CHIPS, rows, cols) if ax == 0
                    else jnp.transpose(gfull[n].reshape(rows, N_CHIPS, cols), (1, 0, 2)))
            parts.append(slab)
        out.append(jnp.concatenate(parts, axis=1))
    return out


def _after(values, mark):
    values, _ = lax.optimization_barrier((values, mark))
    return values


def _reduce_scatter_begin(gsends, place, tag, ids):
    sibs = _sibling_halves(gsends, "rs_sibling_halves_" + tag, ids[0])
    tsends = [_chip_sums(g, s, place) for g, s in zip(gsends, sibs)]
    return tsends, _exchange_chip_sums(tsends, "rs_exchange_" + tag, ids[1])


def _reduce_scatter_finish(begun, place, tag, ids, hold):
    tsends, recvs = begun
    recvs = _after(recvs, hold)
    return _join_halves([_final_sum(t, r, place) for t, r in zip(tsends, recvs)], "rs_join_halves_" + tag, ids[2])


SMALL_SHAPES = {"rel_table": (NUM_BUCKETS, 2 * N_HEADS), "norm_mix_g": (DEPTH, D_MODEL), "qnorm_a_g": (DEPTH, HEAD_DIM),
                "knorm_a_g": (DEPTH, HEAD_DIM), "qnorm_b_g": (DEPTH, HEAD_DIM), "knorm_b_g": (DEPTH, HEAD_DIM),
                "sink_b": (DEPTH, N_HEADS), "norm_ffn_g": (DEPTH, D_MODEL), "norm_ple_g": (DEPTH, D_MODEL)}


def _pack_small(vals, last=None):
    flat = jnp.concatenate([vals[n].astype(F32).reshape(-1) for n in SMALL])
    tail = jnp.zeros((SMALL_ROWS * LANES - flat.shape[0],), F32)
    if last is not None:
        tail = tail.at[-1].set(last)
    return jnp.concatenate([flat, tail]).reshape(SMALL_ROWS, LANES)


def _unpack_small(packed):
    flat, out, off = packed.reshape(-1), {}, 0
    for n in SMALL:
        size = math.prod(SMALL_SHAPES[n])
        out[n] = flat[off:off + size].reshape(SMALL_SHAPES[n])
        off += size
    return out


def _adamw(w, gs, g_row, m, v, name):
    c1 = 1.0 - ADAM_B1 ** ADAM_STEP
    c2 = 1.0 - ADAM_B2 ** ADAM_STEP
    total, width = w.shape
    n_layers = len(gs)
    per = total // n_layers
    tm = max(t for t in range(8, 513, 8) if per % t == 0 and g_row % t == 0)
    nblk = per // tm

    def body(*refs):
        w_ref, g_refs = refs[0], refs[1:1 + n_layers]
        m_ref, v_ref, og, od, om, ov = refs[1 + n_layers:]
        layer = pl.program_id(0) // nblk
        g = g_refs[0][...]
        for l in range(1, n_layers):
            g = jnp.where(layer == l, g_refs[l][...], g)
        m_new = ADAM_B1 * m_ref[...] + (1.0 - ADAM_B1) * g
        v_new = ADAM_B2 * v_ref[...] + (1.0 - ADAM_B2) * (g * g)
        og[...] = g
        od[...] = -ADAM_LR * ((m_new / c1) / (jnp.sqrt(v_new / c2) + ADAM_EPS) + ADAM_WD * w_ref[...])
        om[...] = m_new
        ov[...] = v_new

    row = pl.BlockSpec((tm, width), lambda i: (i, 0))
    g_specs = [pl.BlockSpec((tm, width), lambda i, l=l: (g_row // tm + jnp.clip(i - l * nblk, 0, nblk - 1), 0))
               for l in range(n_layers)]
    return pl.pallas_call(
        body, out_shape=[jax.ShapeDtypeStruct((total, width), F32)] * 4, grid=(total // tm,),
        in_specs=[row] + g_specs + [row, row], out_specs=[row] * 4, name=name,
        compiler_params=_params("parallel"))(w, *gs, m, v)


def kernel(x, p, rel_table, norm_mix_g, w_in, qnorm_a_g, knorm_a_g, qnorm_b_g, knorm_b_g, sink_b, w_branch_a, w_branch_b, w_out, norm_ffn_g, w_ffn_gate, w_ffn_up, w_ffn_down, norm_ple_g, w_ple_gate, w_ple_proj, loss_target, m_rel_table, m_norm_mix_g, m_w_in, m_qnorm_a_g, m_knorm_a_g, m_qnorm_b_g, m_knorm_b_g, m_sink_b, m_w_branch_a, m_w_branch_b, m_w_out, m_norm_ffn_g, m_w_ffn_gate, m_w_ffn_up, m_w_ffn_down, m_norm_ple_g, m_w_ple_gate, m_w_ple_proj, v_rel_table, v_norm_mix_g, v_w_in, v_qnorm_a_g, v_knorm_a_g, v_qnorm_b_g, v_knorm_b_g, v_sink_b, v_w_branch_a, v_w_branch_b, v_w_out, v_norm_ffn_g, v_w_ffn_gate, v_w_ffn_up, v_w_ffn_down, v_norm_ple_g, v_w_ple_gate, v_w_ple_proj):
    given = dict(locals())

    def held(name, a):
        return jnp.swapaxes(a, 1, 2) if name in TRANSPOSED else a

    weights = {n: held(n, given[n]) for n in WEIGHTS}
    moments_m = {n: held(n, given["m_" + n]) for n in WEIGHTS}
    moments_v = {n: held(n, given["v_" + n]) for n in WEIGHTS}
    xi, yi, ci = _place()
    place = jnp.stack([2 * xi + yi, ci]).astype(jnp.int32)

    shards = [_pack_groups(weights, l, BF16) for l in range(DEPTH)]
    w_in_flight = [_allgather(shards[0][:1], "allgather_w_in_layer0", sequencer=9), None]
    rest_flight = [None, None]
    small = {n: weights[n] for n in SMALL}

    def w_in_of(l, mark):
        landed = _after(w_in_flight[l], (shards[1], mark) if l == 0 else mark)
        rest_flight[l] = _allgather(_after(shards[l][1:], landed), f"allgather_rest_layer{l}", sequencer=(1, 2)[l])
        return _unpack_full(landed, GATHER_GROUPS[:1])["w_in"]

    def rest_of(l, mark):
        if l + 1 < DEPTH:
            w_in_flight[l + 1] = _allgather(_after(shards[l + 1][:1], mark), f"allgather_w_in_layer{l + 1}", sequencer=16)
        return _unpack_full(_after(rest_flight[l], mark), GATHER_GROUPS[1:])

    loss, dx, gbig, gsmall, marks = _local_step(x[0], p[:, 0], loss_target[0], w_in_of, rest_of, small)

    gsends = [_pack_grads(gbig[l]) for l in range(DEPTH)]
    stages = {"rest_layer1": (gsends[1][1:], (3, 4, 5)), "w_in_layer1": (gsends[1][:1], (13, 14, 15)),
              "rest_layer0": (gsends[0][1:], (6, 7, 8)), "w_in_layer0": (gsends[0][:1], (10, 11, 12))}
    begun = {tag: _reduce_scatter_begin(g, place, tag, ids) for tag, (g, ids) in stages.items()}

    def finish(tag, hold):
        return _reduce_scatter_finish(begun[tag], place, tag, stages[tag][1], hold)

    red1 = finish("w_in_layer1", dx) + finish("rest_layer1", marks[0]["attn_bwd_done"])
    rest0 = finish("rest_layer0", gbig[0]["w_in"])

    grads, delta, new_m, new_v = {}, {}, {}, {}

    def update(group, reduced):
        offs, _ = _group_rows(group)
        for n in group:
            shape = weights[n].shape
            two_d = lambda a: a.reshape(shape[0] * shape[1], shape[2])
            outs = _adamw(two_d(weights[n]), reduced, offs[n], two_d(moments_m[n]), two_d(moments_v[n]), "adamw_" + n)
            grads[n], delta[n], new_m[n], new_v[n] = (held(n, o.reshape(shape)) for o in outs)

    for gi in (1, 2):
        update(GROUPS[gi], _after([rest0[gi - 1], red1[gi]], begun["w_in_layer0"][0]))
    small_grads = _allreduce_small(_pack_small(gsmall, last=loss))
    g_, d_, m_, v_ = _adamw(_pack_small(weights), [small_grads], 0, _pack_small(moments_m), _pack_small(moments_v),
                            "adamw_small")
    grads.update(_unpack_small(g_))
    delta.update(_unpack_small(d_))
    new_m.update(_unpack_small(m_))
    new_v.update(_unpack_small(v_))
    others_done = [dx, d_] + [delta[n] for gi in (1, 2) for n in GROUPS[gi]]
    update(GROUPS[0], [finish("w_in_layer0", others_done)[0], red1[0]])

    return (small_grads[-1, -1], dx[None], *[grads[n] for n in WEIGHTS], *[delta[n] for n in WEIGHTS],
            *[new_m[n] for n in WEIGHTS], *[new_v[n] for n in WEIGHTS])
```
